```python
import math
import jax, jax.numpy as jnp
from jax import lax
import numpy as np

D_MODEL = 1024
BATCH = 8
SEQ = 4096
DEPTH = 1

HEAD_DIM = 64
N_Q_HEADS = 16
N_KV_HEADS = 4
GROUP = N_Q_HEADS // N_KV_HEADS
ATTN_WIDTH = N_Q_HEADS * HEAD_DIM
KV_WIDTH = N_KV_HEADS * HEAD_DIM
WINDOW = 128
BLOCK = 128
ROT_DIM = HEAD_DIM // 4
ROPE_THETA = 500000.0
NEG_INF = -1e30
LRU_WIDTH = 1024
LRU_BLOCKS = 16
LRU_BLOCK_W = LRU_WIDTH // LRU_BLOCKS
CONV_W = 4
LRU_C = 8.0
MIX_WIDTH = ATTN_WIDTH + LRU_WIDTH
IN_DIM = ATTN_WIDTH + 2 * KV_WIDTH + ATTN_WIDTH + 2 * LRU_WIDTH
SPLITS = tuple(np.cumsum([ATTN_WIDTH, KV_WIDTH, KV_WIDTH, ATTN_WIDTH, LRU_WIDTH]).tolist())
EPS = 1e-6

kernel_name = "hymba_swa_sink_rglru_hybrid"


def rmsnorm(x, g):
    xf = x.astype(jnp.float32)
    y = xf * lax.rsqrt(jnp.mean(xf * xf, axis=-1, keepdims=True) + EPS)
    return (y * g.astype(jnp.float32)).astype(x.dtype)


def partial_rope(t, cos, sin):
    half = ROT_DIM // 2
    t1 = t[..., :half].astype(jnp.float32)
    t2 = t[..., half:ROT_DIM].astype(jnp.float32)
    rot = jnp.concatenate([t1 * cos - t2 * sin, t2 * cos + t1 * sin], axis=-1)
    return jnp.concatenate([rot.astype(t.dtype), t[..., ROT_DIM:]], axis=-1)


def swa_sink_attention(q, k, v, sinks):
    b, s = q.shape[0], q.shape[1]
    nb = s // BLOCK
    qb = q.reshape(b, nb, BLOCK, N_KV_HEADS, GROUP, HEAD_DIM)

    def with_prev(t):
        tb = t.reshape(b, nb, BLOCK, N_KV_HEADS, HEAD_DIM)
        prev = jnp.pad(tb, ((0, 0), (1, 0), (0, 0), (0, 0), (0, 0)))[:, :-1]
        return jnp.concatenate([prev, tb], axis=2)

    kc, vc = with_prev(k), with_prev(v)
    scores = jnp.einsum('bnqhgd,bnkhd->bnhgqk', qb, kc,
                        preferred_element_type=jnp.float32) * (HEAD_DIM ** -0.5)
    qi = jnp.arange(BLOCK)[:, None]
    kj = jnp.arange(2 * BLOCK)[None, :]
    diff = qi + BLOCK - kj
    band = (diff >= 0) & (diff < WINDOW)
    blk = jnp.arange(nb)[:, None, None]
    valid = band[None] & ((blk > 0) | (kj >= BLOCK)[None])
    scores = jnp.where(valid[None, :, None, None], scores, NEG_INF)
    sink = sinks.astype(jnp.float32).reshape(N_KV_HEADS, GROUP)[None, None, :, :, None, None]
    m = jnp.maximum(jnp.max(scores, axis=-1, keepdims=True), sink)
    p = jnp.exp(scores - m)
    denom = jnp.sum(p, axis=-1, keepdims=True) + jnp.exp(sink - m)
    o = jnp.einsum('bnhgqk,bnkhd->bnqhgd', p / denom, vc.astype(jnp.float32))
    return o.reshape(b, s, ATTN_WIDTH).astype(q.dtype)


def causal_depthwise_conv(x, w, bias):
    s = x.shape[1]
    xp = jnp.pad(x, ((0, 0), (CONV_W - 1, 0), (0, 0)))
    y = bias[None, None, :]
    for tap in range(CONV_W):
        y = y + xp[:, tap:tap + s] * w[tap][None, None, :]
    return y


def rg_lru(x, w_r, b_r, w_i, b_i, lam):
    b, s, _ = x.shape
    xf = x.astype(jnp.float32)
    xb = xf.reshape(b, s, LRU_BLOCKS, LRU_BLOCK_W)
    r = jax.nn.sigmoid(jnp.einsum('bsnc,ncd->bsnd', xb, w_r.astype(jnp.float32)).reshape(b, s, LRU_WIDTH)
                       + b_r.astype(jnp.float32))
    i = jax.nn.sigmoid(jnp.einsum('bsnc,ncd->bsnd', xb, w_i.astype(jnp.float32)).reshape(b, s, LRU_WIDTH)
                       + b_i.astype(jnp.float32))
    log_a = -LRU_C * r * jax.nn.softplus(-lam.astype(jnp.float32))
    a = jnp.exp(log_a)
    u = jnp.sqrt(-jnp.expm1(2.0 * log_a)) * (i * xf)

    def combine(left, right):
        a1, b1 = left
        a2, b2 = right
        return a1 * a2, a2 * b1 + b2

    _, h = lax.associative_scan(combine, (a, u), axis=1)
    return h.astype(x.dtype)


def _fwd_setup_inputs(seed: int = 0) -> dict:
    key = jax.random.key(seed)
    ks = jax.random.split(key, 16)
    f32 = jnp.float32
    x = jax.random.normal(ks[0], (BATCH, SEQ, D_MODEL), f32)
    ln_gain = 1.0 + 0.02 * jax.random.normal(ks[1], (DEPTH, D_MODEL), f32)
    w_in = jax.random.normal(ks[2], (DEPTH, D_MODEL, IN_DIM), f32) * D_MODEL ** -0.5
    sinks = 0.5 * jax.random.normal(ks[3], (DEPTH, N_Q_HEADS), f32)
    conv_w = jax.random.normal(ks[4], (DEPTH, CONV_W, LRU_WIDTH), f32) * CONV_W ** -0.5
    conv_b = 0.02 * jax.random.normal(ks[5], (DEPTH, LRU_WIDTH), f32)
    w_rgate = jax.random.normal(ks[6], (DEPTH, LRU_BLOCKS, LRU_BLOCK_W, LRU_BLOCK_W), f32) * LRU_BLOCK_W ** -0.5
    b_rgate = 0.02 * jax.random.normal(ks[7], (DEPTH, LRU_WIDTH), f32)
    w_igate = jax.random.normal(ks[8], (DEPTH, LRU_BLOCKS, LRU_BLOCK_W, LRU_BLOCK_W), f32) * LRU_BLOCK_W ** -0.5
    b_igate = 0.02 * jax.random.normal(ks[9], (DEPTH, LRU_WIDTH), f32)
    a0 = jax.random.uniform(ks[10], (DEPTH, LRU_WIDTH), f32, 0.9, 0.999)
    sig = a0 ** (1.0 / LRU_C)
    lru_lambda = jnp.log(sig) - jnp.log1p(-sig)
    attn_out_gain = 1.0 + 0.02 * jax.random.normal(ks[11], (DEPTH, ATTN_WIDTH), f32)
    lru_out_gain = 1.0 + 0.02 * jax.random.normal(ks[12], (DEPTH, LRU_WIDTH), f32)
    w_out = jax.random.normal(ks[13], (DEPTH, MIX_WIDTH, D_MODEL), f32) * MIX_WIDTH ** -0.5
    final_gain = 1.0 + 0.02 * jax.random.normal(ks[14], (D_MODEL,), f32)
    return {"x": x, "ln_gain": ln_gain, "w_in": w_in, "sinks": sinks,
            "conv_w": conv_w, "conv_b": conv_b, "w_rgate": w_rgate, "b_rgate": b_rgate,
            "w_igate": w_igate, "b_igate": b_igate, "lru_lambda": lru_lambda,
            "attn_out_gain": attn_out_gain, "lru_out_gain": lru_out_gain,
            "w_out": w_out, "final_gain": final_gain}


def _fwd_reference(x, ln_gain, w_in, sinks, conv_w, conv_b, w_rgate, b_rgate, w_igate, b_igate,
              lru_lambda, attn_out_gain, lru_out_gain, w_out, final_gain):
    b, s, _ = x.shape
    pos = jnp.arange(s, dtype=jnp.float32)
    inv_freq = ROPE_THETA ** (-jnp.arange(0, ROT_DIM, 2, dtype=jnp.float32) / ROT_DIM)
    ang = pos[:, None] * inv_freq[None, :]
    cos = jnp.cos(ang)[None, :, None, :]
    sin = jnp.sin(ang)[None, :, None, :]
    for l in range(DEPTH):
        h = rmsnorm(x, ln_gain[l])
        z = jnp.einsum('bsd,de->bse', h, w_in[l])
        q, k, v, g_attn, x_lru, g_lru = jnp.split(z, SPLITS, axis=-1)
        q = partial_rope(q.reshape(b, s, N_Q_HEADS, HEAD_DIM), cos, sin)
        k = partial_rope(k.reshape(b, s, N_KV_HEADS, HEAD_DIM), cos, sin)
        v = v.reshape(b, s, N_KV_HEADS, HEAD_DIM)
        y_attn = swa_sink_attention(q, k, v, sinks[l])
        y_attn = rmsnorm(y_attn * jax.nn.silu(g_attn), attn_out_gain[l])
        u = causal_depthwise_conv(x_lru, conv_w[l], conv_b[l])
        y_lru = rg_lru(u, w_rgate[l], b_rgate[l], w_igate[l], b_igate[l], lru_lambda[l])
        y_lru = rmsnorm(y_lru * jax.nn.silu(g_lru), lru_out_gain[l])
        y = jnp.einsum('bse,ed->bsd', jnp.concatenate([y_attn, y_lru], axis=-1), w_out[l])
        x = x + y
    return rmsnorm(x, final_gain)


import jax as _jax
import jax.numpy as _jnp

TWIN_FORMAT = 'train_step'
FWD_PARAMS = ['x', 'ln_gain', 'w_in', 'sinks', 'conv_w', 'conv_b', 'w_rgate', 'b_rgate', 'w_igate', 'b_igate', 'lru_lambda', 'attn_out_gain', 'lru_out_gain', 'w_out', 'final_gain']
TWIN_WEIGHTS = ['ln_gain', 'w_in', 'sinks', 'conv_w', 'conv_b', 'w_rgate', 'b_rgate', 'w_igate', 'b_igate', 'lru_lambda', 'attn_out_gain', 'lru_out_gain', 'w_out', 'final_gain']
TWIN_DIFF_INPUT = 'x'
TWIN_INPUTS = ['x', 'ln_gain', 'w_in', 'sinks', 'conv_w', 'conv_b', 'w_rgate', 'b_rgate', 'w_igate', 'b_igate', 'lru_lambda', 'attn_out_gain', 'lru_out_gain', 'w_out', 'final_gain', 'loss_target', 'm_ln_gain', 'm_w_in', 'm_sinks', 'm_conv_w', 'm_conv_b', 'm_w_rgate', 'm_b_rgate', 'm_w_igate', 'm_b_igate', 'm_lru_lambda', 'm_attn_out_gain', 'm_lru_out_gain', 'm_w_out', 'm_final_gain', 'v_ln_gain', 'v_w_in', 'v_sinks', 'v_conv_w', 'v_conv_b', 'v_w_rgate', 'v_b_rgate', 'v_w_igate', 'v_b_igate', 'v_lru_lambda', 'v_attn_out_gain', 'v_lru_out_gain', 'v_w_out', 'v_final_gain']
TWIN_OUTPUTS = ['loss', 'grad_x', 'grad_ln_gain', 'grad_w_in', 'grad_sinks', 'grad_conv_w', 'grad_conv_b', 'grad_w_rgate', 'grad_b_rgate', 'grad_w_igate', 'grad_b_igate', 'grad_lru_lambda', 'grad_attn_out_gain', 'grad_lru_out_gain', 'grad_w_out', 'grad_final_gain', 'delta_ln_gain', 'delta_w_in', 'delta_sinks', 'delta_conv_w', 'delta_conv_b', 'delta_w_rgate', 'delta_b_rgate', 'delta_w_igate', 'delta_b_igate', 'delta_lru_lambda', 'delta_attn_out_gain', 'delta_lru_out_gain', 'delta_w_out', 'delta_final_gain', 'new_m_ln_gain', 'new_m_w_in', 'new_m_sinks', 'new_m_conv_w', 'new_m_conv_b', 'new_m_w_rgate', 'new_m_b_rgate', 'new_m_w_igate', 'new_m_b_igate', 'new_m_lru_lambda', 'new_m_attn_out_gain', 'new_m_lru_out_gain', 'new_m_w_out', 'new_m_final_gain', 'new_v_ln_gain', 'new_v_w_in', 'new_v_sinks', 'new_v_conv_w', 'new_v_conv_b', 'new_v_w_rgate', 'new_v_b_rgate', 'new_v_w_igate', 'new_v_b_igate', 'new_v_lru_lambda', 'new_v_attn_out_gain', 'new_v_lru_out_gain', 'new_v_w_out', 'new_v_final_gain']
TWIN_LEAF_KINDS = {'loss': 'loss', 'grad_x': 'grad_x', 'grad_ln_gain': 'grad_w', 'grad_w_in': 'grad_w', 'grad_sinks': 'grad_w', 'grad_conv_w': 'grad_w', 'grad_conv_b': 'grad_w', 'grad_w_rgate': 'grad_w', 'grad_b_rgate': 'grad_w', 'grad_w_igate': 'grad_w', 'grad_b_igate': 'grad_w', 'grad_lru_lambda': 'grad_w', 'grad_attn_out_gain': 'grad_w', 'grad_lru_out_gain': 'grad_w', 'grad_w_out': 'grad_w', 'grad_final_gain': 'grad_w', 'delta_ln_gain': 'delta_w', 'delta_w_in': 'delta_w', 'delta_sinks': 'delta_w', 'delta_conv_w': 'delta_w', 'delta_conv_b': 'delta_w', 'delta_w_rgate': 'delta_w', 'delta_b_rgate': 'delta_w', 'delta_w_igate': 'delta_w', 'delta_b_igate': 'delta_w', 'delta_lru_lambda': 'delta_w', 'delta_attn_out_gain': 'delta_w', 'delta_lru_out_gain': 'delta_w', 'delta_w_out': 'delta_w', 'delta_final_gain': 'delta_w', 'new_m_ln_gain': 'new_m', 'new_m_w_in': 'new_m', 'new_m_sinks': 'new_m', 'new_m_conv_w': 'new_m', 'new_m_conv_b': 'new_m', 'new_m_w_rgate': 'new_m', 'new_m_b_rgate': 'new_m', 'new_m_w_igate': 'new_m', 'new_m_b_igate': 'new_m', 'new_m_lru_lambda': 'new_m', 'new_m_attn_out_gain': 'new_m', 'new_m_lru_out_gain': 'new_m', 'new_m_w_out': 'new_m', 'new_m_final_gain': 'new_m', 'new_v_ln_gain': 'new_v', 'new_v_w_in': 'new_v', 'new_v_sinks': 'new_v', 'new_v_conv_w': 'new_v', 'new_v_conv_b': 'new_v', 'new_v_w_rgate': 'new_v', 'new_v_b_rgate': 'new_v', 'new_v_w_igate': 'new_v', 'new_v_b_igate': 'new_v', 'new_v_lru_lambda': 'new_v', 'new_v_attn_out_gain': 'new_v', 'new_v_lru_out_gain': 'new_v', 'new_v_w_out': 'new_v', 'new_v_final_gain': 'new_v'}


def _forward(args):
    return _fwd_reference(*[args[k] for k in FWD_PARAMS])


def _output_shape():
    out = _jax.eval_shape(lambda: _forward(_fwd_setup_inputs(0)))
    return out.shape, out.dtype

N_MICROBATCH = 1
ADAM_LR = 0.001
ADAM_B1 = 0.9
ADAM_B2 = 0.999
ADAM_EPS = 1e-08
ADAM_WD = 0.01
ADAM_STEP = 10
PER_EXAMPLE_BATCH_AXIS = {'x': 0, 'loss_target': 0}
SHARED_INPUTS = []
_WEIGHT_DTYPES = {'ln_gain': _jnp.float32, 'w_in': _jnp.float32, 'sinks': _jnp.float32, 'conv_w': _jnp.float32, 'conv_b': _jnp.float32, 'w_rgate': _jnp.float32, 'b_rgate': _jnp.float32, 'w_igate': _jnp.float32, 'b_igate': _jnp.float32, 'lru_lambda': _jnp.float32, 'attn_out_gain': _jnp.float32, 'lru_out_gain': _jnp.float32, 'w_out': _jnp.float32, 'final_gain': _jnp.float32}
MOMENT_SCALE = {'ln_gain': 2.201807e-01, 'w_in': 1.016200e-01, 'sinks': 3.316925e-02, 'conv_w': 9.259153e-02, 'conv_b': 1.025332e+00, 'w_rgate': 3.549930e-02, 'b_rgate': 2.630999e-02, 'w_igate': 6.646515e-02, 'b_igate': 3.180032e-02, 'lru_lambda': 4.576567e-02, 'attn_out_gain': 8.916039e-02, 'lru_out_gain': 9.046281e-02, 'w_out': 1.254936e-01, 'final_gain': 3.200820e+01}


def _to_microbatches(a, axis):
    t = _jnp.moveaxis(a, axis, 0)
    t = t.reshape((N_MICROBATCH, t.shape[0] // N_MICROBATCH) + t.shape[1:])
    return _jnp.moveaxis(t, 1, axis + 1)


def setup_inputs(seed: int = 0) -> dict:
    inp = _fwd_setup_inputs(seed)
    key = _jax.random.fold_in(_jax.random.key(seed), 7919)
    shape, _ = _output_shape()
    out = dict(inp)
    out["loss_target"] = _jax.random.normal(_jax.random.fold_in(key, 0), shape, _jnp.float32)
    for i, name in enumerate(TWIN_WEIGHTS):
        w = inp[name].astype(_jnp.float32)
        if MOMENT_SCALE is None:
            s = _jnp.sqrt(_jnp.mean(_jnp.square(w)) + 1e-30)
        else:
            s = MOMENT_SCALE[name]
        km, kv = _jax.random.split(_jax.random.fold_in(key, i + 1))
        out[name] = w
        out["m_" + name] = s * _jax.random.normal(km, w.shape, _jnp.float32)
        out["v_" + name] = (s * s) * _jax.random.uniform(kv, w.shape, _jnp.float32, 0.5, 1.5)
    if N_MICROBATCH > 1:
        for name, axis in PER_EXAMPLE_BATCH_AXIS.items():
            out[name] = _to_microbatches(out[name], axis)
    return {'x': out['x'], 'ln_gain': out['ln_gain'], 'w_in': out['w_in'], 'sinks': out['sinks'], 'conv_w': out['conv_w'], 'conv_b': out['conv_b'], 'w_rgate': out['w_rgate'], 'b_rgate': out['b_rgate'], 'w_igate': out['w_igate'], 'b_igate': out['b_igate'], 'lru_lambda': out['lru_lambda'], 'attn_out_gain': out['attn_out_gain'], 'lru_out_gain': out['lru_out_gain'], 'w_out': out['w_out'], 'final_gain': out['final_gain'], 'loss_target': out['loss_target'], 'm_ln_gain': out['m_ln_gain'], 'm_w_in': out['m_w_in'], 'm_sinks': out['m_sinks'], 'm_conv_w': out['m_conv_w'], 'm_conv_b': out['m_conv_b'], 'm_w_rgate': out['m_w_rgate'], 'm_b_rgate': out['m_b_rgate'], 'm_w_igate': out['m_w_igate'], 'm_b_igate': out['m_b_igate'], 'm_lru_lambda': out['m_lru_lambda'], 'm_attn_out_gain': out['m_attn_out_gain'], 'm_lru_out_gain': out['m_lru_out_gain'], 'm_w_out': out['m_w_out'], 'm_final_gain': out['m_final_gain'], 'v_ln_gain': out['v_ln_gain'], 'v_w_in': out['v_w_in'], 'v_sinks': out['v_sinks'], 'v_conv_w': out['v_conv_w'], 'v_conv_b': out['v_conv_b'], 'v_w_rgate': out['v_w_rgate'], 'v_b_rgate': out['v_b_rgate'], 'v_w_igate': out['v_w_igate'], 'v_b_igate': out['v_b_igate'], 'v_lru_lambda': out['v_lru_lambda'], 'v_attn_out_gain': out['v_attn_out_gain'], 'v_lru_out_gain': out['v_lru_out_gain'], 'v_w_out': out['v_w_out'], 'v_final_gain': out['v_final_gain']}


def _loss(weights, diff, rest, loss_target):
    with _jax.named_scope("forward"):
        args = {**rest, TWIN_DIFF_INPUT: diff, **{k: w.astype(_WEIGHT_DTYPES[k]) for k, w in weights.items()}}
        y = _forward(args)
    with _jax.named_scope("loss_head"):
        err = _jnp.square(y.astype(_jnp.float32) - loss_target)
        return 0.5 * _jnp.sum(_jnp.mean(err, axis=-1)) if err.ndim else 0.5 * err


def _adamw(w, g, m, v):
    m = ADAM_B1 * m + (1.0 - ADAM_B1) * g
    v = ADAM_B2 * v + (1.0 - ADAM_B2) * _jnp.square(g)
    m_hat = m / (1.0 - ADAM_B1 ** ADAM_STEP)
    v_hat = v / (1.0 - ADAM_B2 ** ADAM_STEP)
    delta = -ADAM_LR * (m_hat / (_jnp.sqrt(v_hat) + ADAM_EPS) + ADAM_WD * w)
    return delta, m, v


def reference(x, ln_gain, w_in, sinks, conv_w, conv_b, w_rgate, b_rgate, w_igate, b_igate, lru_lambda, attn_out_gain, lru_out_gain, w_out, final_gain, loss_target, m_ln_gain, m_w_in, m_sinks, m_conv_w, m_conv_b, m_w_rgate, m_b_rgate, m_w_igate, m_b_igate, m_lru_lambda, m_attn_out_gain, m_lru_out_gain, m_w_out, m_final_gain, v_ln_gain, v_w_in, v_sinks, v_conv_w, v_conv_b, v_w_rgate, v_b_rgate, v_w_igate, v_b_igate, v_lru_lambda, v_attn_out_gain, v_lru_out_gain, v_w_out, v_final_gain):
    given = dict(x=x, ln_gain=ln_gain, w_in=w_in, sinks=sinks, conv_w=conv_w, conv_b=conv_b, w_rgate=w_rgate, b_rgate=b_rgate, w_igate=w_igate, b_igate=b_igate, lru_lambda=lru_lambda, attn_out_gain=attn_out_gain, lru_out_gain=lru_out_gain, w_out=w_out, final_gain=final_gain, loss_target=loss_target, m_ln_gain=m_ln_gain, m_w_in=m_w_in, m_sinks=m_sinks, m_conv_w=m_conv_w, m_conv_b=m_conv_b, m_w_rgate=m_w_rgate, m_b_rgate=m_b_rgate, m_w_igate=m_w_igate, m_b_igate=m_b_igate, m_lru_lambda=m_lru_lambda, m_attn_out_gain=m_attn_out_gain, m_lru_out_gain=m_lru_out_gain, m_w_out=m_w_out, m_final_gain=m_final_gain, v_ln_gain=v_ln_gain, v_w_in=v_w_in, v_sinks=v_sinks, v_conv_w=v_conv_w, v_conv_b=v_conv_b, v_w_rgate=v_w_rgate, v_b_rgate=v_b_rgate, v_w_igate=v_w_igate, v_b_igate=v_b_igate, v_lru_lambda=v_lru_lambda, v_attn_out_gain=v_attn_out_gain, v_lru_out_gain=v_lru_out_gain, v_w_out=v_w_out, v_final_gain=v_final_gain)
    weights = {n: given[n] for n in TWIN_WEIGHTS}
    shared = {n: given[n] for n in SHARED_INPUTS}
    per_example = {n: given[n] for n in ['x']}
    grad_fn = _jax.value_and_grad(_loss, argnums=(0, 1))

    def one_microbatch(ex, loss_target):
        ex = dict(ex)
        diff = ex.pop(TWIN_DIFF_INPUT)
        return grad_fn(weights, diff, {**shared, **ex}, loss_target)

    if N_MICROBATCH == 1:
        loss, (grad_w, grad_x) = one_microbatch(per_example, given["loss_target"])
    else:
        def body(carry, xs):
            loss_sum, grad_sum = carry
            l_k, (gw_k, gx_k) = one_microbatch(xs[0], xs[1])
            with _jax.named_scope("update"):
                return (loss_sum + l_k, _jax.tree.map(_jnp.add, grad_sum, gw_k)), gx_k

        init = (_jnp.zeros((), _jnp.float32), _jax.tree.map(_jnp.zeros_like, weights))
        (loss, grad_w), grad_x = _jax.lax.scan(body, init, (per_example, given["loss_target"]))
    with _jax.named_scope("update"):
        delta_w, new_m, new_v = {}, {}, {}
        for n in TWIN_WEIGHTS:
            delta_w[n], new_m[n], new_v[n] = _adamw(weights[n], grad_w[n], given["m_" + n], given["v_" + n])
    return (loss, grad_x, *[grad_w[n] for n in TWIN_WEIGHTS], *[delta_w[n] for n in TWIN_WEIGHTS],
            *[new_m[n] for n in TWIN_WEIGHTS], *[new_v[n] for n in TWIN_WEIGHTS])
```

```python
import jax
import jax.numpy as jnp
from jax import lax
from jax.experimental import pallas as pl
from jax.experimental.pallas import tpu as pltpu

f32 = jnp.float32
bf16 = jnp.bfloat16

D = 1024
HD = 64
NQ = 16
NKV = 4
GROUP = NQ // NKV
KVW = NKV * HD
BLK = 128
ROT = 16
THETA = 500000.0
NEG = -1e30
LW = 1024
NGRP = 4
CONVW = 4
LRU_C = 8.0
NIN = 4608
EPS = 1e-6
NDEV = 8
WT_ROWS = NIN // NDEV
WO_ROWS = 2 * D // NDEV
SMALL_ROWS = 192
SMALL_PER = SMALL_ROWS // NDEV

ADAM_LR = 0.001
ADAM_B1 = 0.9
ADAM_B2 = 0.999
ADAM_EPS = 1e-08
ADAM_WD = 0.01
ADAM_STEP = 10

NT = (((1,), (1,)), ((), ()))
TN = (((0,), (0,)), ((), ()))
MESH = pl.DeviceIdType.MESH
MIB = 1024 * 1024


def _dot(a, b):
    return jnp.dot(a, b, preferred_element_type=f32)


def _dot_nt(a, b):
    return lax.dot_general(a, b, NT, preferred_element_type=f32)


def _dot_tn(a, b):
    return lax.dot_general(a, b, TN, preferred_element_type=f32)


def _params(sem, vmem_mib):
    return pltpu.CompilerParams(dimension_semantics=sem, vmem_limit_bytes=vmem_mib * MIB)


def _sigmoid(x):
    return 1.0 / (1.0 + jnp.exp(-x))


def _softplus(x):
    return jnp.maximum(x, 0.0) + jnp.log(1.0 + jnp.exp(-jnp.abs(x)))


def _neg_expm1(x):
    series = -x * (1.0 + x * (0.5 + x * (1.0 / 6.0 + x * (1.0 / 24.0 + x * (1.0 / 120.0)))))
    return jnp.where(x > -0.1, series, 1.0 - jnp.exp(x))


def _rope_tables(s):
    pos = jnp.arange(s, dtype=f32)
    inv_freq = THETA ** (-jnp.arange(0, ROT, 2, dtype=f32) / ROT)
    ang = pos[:, None] * inv_freq[None, :]
    cos, sin = jnp.cos(ang), jnp.sin(ang)
    ones = jnp.ones((s, HD - ROT), f32)
    zeros8 = jnp.zeros((s, ROT // 2), f32)
    zeros48 = jnp.zeros((s, HD - ROT), f32)
    c = jnp.concatenate([cos, cos, ones], axis=1)
    sa = jnp.concatenate([zeros8, sin, zeros48], axis=1)
    sb = jnp.concatenate([-sin, zeros8, zeros48], axis=1)
    return tuple(jnp.concatenate([t, t], axis=1) for t in (c, sa, sb))


def _rope(t, c, sa, sb):
    return t * c + pltpu.roll(t, 8, 1) * sa + pltpu.roll(t, 120, 1) * sb


def _unrope(dr, c, sa, sb):
    return dr * c + pltpu.roll(dr * sa, 120, 1) + pltpu.roll(dr * sb, 8, 1)


def _fwd_in(x, ln_gain, wt, tabs, tm):
    s = x.shape[0]
    nc = 512

    def body(x_ref, g_ref, wt_ref, c_ref, sa_ref, sb_ref, h_ref, q_ref, k_ref, v_ref, ga_ref, xl_ref, gl_ref):
        xx = x_ref[...]
        rstd = lax.rsqrt(jnp.mean(xx * xx, axis=-1, keepdims=True) + EPS)
        h = (xx * rstd * g_ref[...]).astype(bf16)
        h_ref[...] = h
        c, sa, sb = c_ref[...], sa_ref[...], sb_ref[...]

        def z_chunk(ci):
            return _dot_nt(h, wt_ref[ci * nc:(ci + 1) * nc, :])

        for ci in range(2):
            z = z_chunk(ci)
            for j in range(nc // 128):
                r = _rope(z[:, 128 * j:128 * j + 128], c, sa, sb) * (HD ** -0.5)
                q_ref[:, ci * nc + 128 * j:ci * nc + 128 * j + 128] = r.astype(bf16)
        z = z_chunk(2)
        for j in range(2):
            k_ref[:, 128 * j:128 * j + 128] = _rope(z[:, 128 * j:128 * j + 128], c, sa, sb).astype(bf16)
        v_ref[...] = z[:, 256:512].astype(bf16)
        for i, ref in enumerate((ga_ref, xl_ref, gl_ref)):
            for j in range(2):
                ref[:, j * nc:(j + 1) * nc] = z_chunk(3 + 2 * i + j)

    row = lambda w: pl.BlockSpec((tm, w), lambda i: (i, 0))
    full = lambda a: pl.BlockSpec(a.shape, lambda i: (0, 0))
    return pl.pallas_call(
        body, name="fwd_in", grid=(s // tm,),
        in_specs=[row(D), full(ln_gain), full(wt), row(128), row(128), row(128)],
        out_specs=[row(D), row(D), row(KVW), row(KVW), row(D), row(D), row(D)],
        out_shape=[jax.ShapeDtypeStruct((s, D), bf16), jax.ShapeDtypeStruct((s, D), bf16),
                   jax.ShapeDtypeStruct((s, KVW), bf16), jax.ShapeDtypeStruct((s, KVW), bf16),
                   jax.ShapeDtypeStruct((s, D), f32), jax.ShapeDtypeStruct((s, D), f32),
                   jax.ShapeDtypeStruct((s, D), f32)],
        compiler_params=_params(("arbitrary",), 48),
    )(x, ln_gain, wt, *tabs)


def _band_mask(n):
    qi = lax.broadcasted_iota(jnp.int32, (BLK, 2 * BLK), 0)
    kj = lax.broadcasted_iota(jnp.int32, (BLK, 2 * BLK), 1)
    diff = qi + BLK - kj
    return (diff >= 0) & (diff < BLK) & ((n > 0) | (kj >= BLK))


def _attn_probs(qa, kh, valid, sink):
    sc = jnp.where(valid, _dot_nt(qa, kh), NEG)
    m = jnp.maximum(jnp.max(sc, axis=1, keepdims=True), sink)
    p = jnp.exp(sc - m)
    es = jnp.exp(sink - m)
    inv = 1.0 / (jnp.sum(p, axis=1, keepdims=True) + es)
    return p * inv, es * inv


def _kv_specs():
    prev = pl.BlockSpec((BLK, KVW), lambda n: (jnp.maximum(n - 1, 0), 0))
    cur = pl.BlockSpec((BLK, KVW), lambda n: (n, 0))
    return [prev, cur, prev, cur]


def _attn_fwd(q, k, v, sinks):
    s = q.shape[0]

    def body(sink_ref, q_ref, kp_ref, kc_ref, vp_ref, vc_ref, o_ref):
        valid = _band_mask(pl.program_id(0))
        for h in range(NKV):
            hs = slice(HD * h, HD * h + HD)
            kh = jnp.concatenate([kp_ref[:, hs], kc_ref[:, hs]], axis=0)
            vh = jnp.concatenate([vp_ref[:, hs], vc_ref[:, hs]], axis=0)
            for g in range(GROUP):
                a = GROUP * h + g
                pn, _ = _attn_probs(q_ref[:, HD * a:HD * a + HD], kh, valid, sink_ref[a])
                o_ref[:, HD * a:HD * a + HD] = _dot(pn.astype(bf16), vh)

    return pl.pallas_call(
        body, name="attn_fwd", grid=(s // BLK,),
        in_specs=[pl.BlockSpec(memory_space=pltpu.SMEM), pl.BlockSpec((BLK, D), lambda n: (n, 0))] + _kv_specs(),
        out_specs=pl.BlockSpec((BLK, D), lambda n: (n, 0)),
        out_shape=jax.ShapeDtypeStruct((s, D), f32),
        compiler_params=_params(("arbitrary",), 32),
    )(sinks, q, k, k, v, v)


def _attn_bwd(q, k, v, do, sinks):
    s = q.shape[0]

    def body(sink_ref, q_ref, do_ref, kp_ref, kc_ref, vp_ref, vc_ref, dq_ref, dk_ref, dv_ref, ds_ref):
        n = pl.program_id(0)

        @pl.when(n == 0)
        def _():
            dk_ref[...] = jnp.zeros_like(dk_ref)
            dv_ref[...] = jnp.zeros_like(dv_ref)
            ds_ref[...] = jnp.zeros_like(ds_ref)

        valid = _band_mask(n)
        lane = lax.broadcasted_iota(jnp.int32, (8, 128), 1)
        prev_rows = pl.ds(pl.multiple_of(jnp.maximum(n - 1, 0) * BLK, BLK), BLK)
        cur_rows = pl.ds(pl.multiple_of(n * BLK, BLK), BLK)
        for h in range(NKV):
            hs = slice(HD * h, HD * h + HD)
            kh = jnp.concatenate([kp_ref[:, hs], kc_ref[:, hs]], axis=0)
            vh = jnp.concatenate([vp_ref[:, hs], vc_ref[:, hs]], axis=0)
            dkh = jnp.zeros((2 * BLK, HD), f32)
            dvh = jnp.zeros((2 * BLK, HD), f32)
            for g in range(GROUP):
                a = GROUP * h + g
                qa = q_ref[:, HD * a:HD * a + HD]
                doa = do_ref[:, HD * a:HD * a + HD]
                pn, ps = _attn_probs(qa, kh, valid, sink_ref[a])
                dp = _dot_nt(doa, vh)
                dsum = jnp.sum(pn * dp, axis=1, keepdims=True)
                dsc = (pn * (dp - dsum)).astype(bf16)
                dq_ref[:, HD * a:HD * a + HD] = _dot(dsc, kh)
                dkh = dkh + _dot_tn(dsc, qa)
                dvh = dvh + _dot_tn(pn.astype(bf16), doa)
                ds_ref[...] += jnp.where(lane == a, -jnp.sum(ps * dsum), 0.0)
            dk_ref[prev_rows, hs] += dkh[0:BLK]
            dk_ref[cur_rows, hs] += dkh[BLK:2 * BLK]
            dv_ref[prev_rows, hs] += dvh[0:BLK]
            dv_ref[cur_rows, hs] += dvh[BLK:2 * BLK]

    blk = pl.BlockSpec((BLK, D), lambda n: (n, 0))
    whole = lambda r, w: pl.BlockSpec((r, w), lambda n: (0, 0))
    return pl.pallas_call(
        body, name="attn_bwd", grid=(s // BLK,),
        in_specs=[pl.BlockSpec(memory_space=pltpu.SMEM), blk, blk] + _kv_specs(),
        out_specs=[blk, whole(s, KVW), whole(s, KVW), whole(8, 128)],
        out_shape=[jax.ShapeDtypeStruct((s, D), f32), jax.ShapeDtypeStruct((s, KVW), f32),
                   jax.ShapeDtypeStruct((s, KVW), f32), jax.ShapeDtypeStruct((8, 128), f32)],
        compiler_params=_params(("arbitrary",), 48),
    )(sinks, q, do, k, k, v, v)


def _block_diag(w):
    w4 = w.reshape(NGRP, 4, HD, HD)
    eye = jnp.eye(4, dtype=w.dtype)
    return jnp.einsum('gjcd,jk->gjckd', w4, eye).reshape(NGRP, 256, 256).astype(bf16)


def _gates(u, wr_ref, wi_ref, br, bi, sp):
    ub = u.astype(bf16)
    pr = jnp.concatenate([_dot(ub[:, 256 * g:256 * g + 256], wr_ref[g]) for g in range(NGRP)], axis=1)
    pi = jnp.concatenate([_dot(ub[:, 256 * g:256 * g + 256], wi_ref[g]) for g in range(NGRP)], axis=1)
    r = _sigmoid(pr + br)
    i = _sigmoid(pi + bi)
    la = -LRU_C * r * sp
    a = jnp.exp(la)
    mult = jnp.sqrt(_neg_expm1(2.0 * la))
    return ub, r, i, a, mult


def _lru_fwd(xl, conv_w, conv_b, wr, wi, br, bi, lam, tm):
    s = xl.shape[0]

    def body(xp_ref, x_ref, cw_ref, cb_ref, wr_ref, wi_ref, br_ref, bi_ref, lam_ref, u_ref, h_ref,
             pad, a_scr, b_scr, hcar):
        t0 = pl.program_id(0)

        @pl.when(t0 == 0)
        def _():
            hcar[...] = jnp.zeros_like(hcar)

        pad[0:8, :] = jnp.where(t0 > 0, xp_ref[...], 0.0)
        pad[8:tm + 8, :] = x_ref[...]
        u = cb_ref[...] + sum(cw_ref[k:k + 1, :] * pad[5 + k:5 + k + tm, :] for k in range(CONVW))
        u_ref[...] = u
        sp = _softplus(-lam_ref[...])
        _, _, i, a, mult = _gates(u, wr_ref, wi_ref, br_ref[...], bi_ref[...], sp)
        a_scr[...] = a
        b_scr[...] = mult * (i * u)

        def step(t, hc):
            hn = a_scr[pl.ds(t, 1), :] * hc + b_scr[pl.ds(t, 1), :]
            h_ref[pl.ds(t, 1), :] = hn
            return hn

        hcar[...] = lax.fori_loop(0, tm, step, hcar[...], unroll=8)

    row = pl.BlockSpec((tm, LW), lambda i: (i, 0))
    prev8 = pl.BlockSpec((8, LW), lambda i: (jnp.maximum(i * (tm // 8) - 1, 0), 0))
    full = lambda a: pl.BlockSpec(a.shape, lambda i: (0,) * a.ndim)
    return pl.pallas_call(
        body, name="lru_fwd", grid=(s // tm,),
        in_specs=[prev8, row, full(conv_w), full(conv_b), full(wr), full(wi), full(br), full(bi), full(lam)],
        out_specs=[row, row],
        out_shape=[jax.ShapeDtypeStruct((s, LW), f32), jax.ShapeDtypeStruct((s, LW), f32)],
        scratch_shapes=[pltpu.VMEM((tm + 8, LW), f32), pltpu.VMEM((tm, LW), f32), pltpu.VMEM((tm, LW), f32),
                        pltpu.VMEM((1, LW), f32)],
        compiler_params=_params(("arbitrary",), 48),
    )(xl, xl, conv_w, conv_b, wr, wi, br, bi, lam)


def _lru_bwd(u, hl, dhl, xl, conv_w, wr, wi, br, bi, lam, tm):
    s = u.shape[0]
    nt = s // tm

    def body(u_ref, h_ref, hp_ref, dh_ref, x_ref, xp_ref, cw_ref, wr_ref, wi_ref, br_ref, bi_ref, lam_ref,
             dxl_ref, dwr_ref, dwi_ref, dbr_ref, dbi_ref, dlam_ref, dcb_ref, dcw_ref,
             pad, a_scr, l_scr, lcar, dunext):
        t0 = pl.program_id(0)
        tile = nt - 1 - t0

        @pl.when(t0 == 0)
        def _():
            lcar[...] = jnp.zeros_like(lcar)
            dunext[...] = jnp.zeros_like(dunext)
            for ref in (dwr_ref, dwi_ref, dbr_ref, dbi_ref, dlam_ref, dcb_ref, dcw_ref):
                ref[...] = jnp.zeros_like(ref)

        u = u_ref[...]
        lam = lam_ref[...]
        sp = _softplus(-lam)
        ub, r, i, a, mult = _gates(u, wr_ref, wi_ref, br_ref[...], bi_ref[...], sp)
        a_scr[...] = a

        def step(k, c):
            t = tm - 1 - k
            lt = dh_ref[pl.ds(t, 1), :] + c
            l_scr[pl.ds(t, 1), :] = lt
            return a_scr[pl.ds(t, 1), :] * lt

        lcar[...] = lax.fori_loop(0, tm, step, lcar[...], unroll=8)
        lt = l_scr[...]

        pad[0:8, :] = jnp.where(tile > 0, hp_ref[...], 0.0)
        pad[8:tm + 8, :] = h_ref[...]
        hprev = pad[7:tm + 7, :]
        da = lt * hprev
        dmult = lt * (i * u)
        di = lt * mult * u
        du = lt * mult * i
        dla = da * a - dmult * (a * a) / mult
        dr = dla * (-LRU_C * sp)
        dlam_ref[...] += jnp.sum(dla * (-LRU_C * r), axis=0, keepdims=True)
        dpr = dr * r * (1.0 - r)
        dpi = di * i * (1.0 - i)
        dbr_ref[...] += jnp.sum(dpr, axis=0, keepdims=True)
        dbi_ref[...] += jnp.sum(dpi, axis=0, keepdims=True)
        dprb, dpib = dpr.astype(bf16), dpi.astype(bf16)
        dug = []
        for g in range(NGRP):
            gs = slice(256 * g, 256 * g + 256)
            dwr_ref[g] += _dot_tn(ub[:, gs], dprb[:, gs])
            dwi_ref[g] += _dot_tn(ub[:, gs], dpib[:, gs])
            dug.append(_dot_nt(dprb[:, gs], wr_ref[g]) + _dot_nt(dpib[:, gs], wi_ref[g]))
        du = du + jnp.concatenate(dug, axis=1)

        dcb_ref[...] += jnp.sum(du, axis=0, keepdims=True)
        pad[0:8, :] = jnp.where(tile > 0, xp_ref[...], 0.0)
        pad[8:tm + 8, :] = x_ref[...]
        for k in range(CONVW):
            dcw_ref[k:k + 1, :] += jnp.sum(du * pad[5 + k:5 + k + tm, :], axis=0, keepdims=True)
        pad[0:tm, :] = du
        pad[tm:tm + 8, :] = dunext[...]
        dxl = sum(cw_ref[k:k + 1, :] * pad[3 - k:3 - k + tm, :] for k in range(CONVW))
        dxl_ref[...] = dxl.astype(bf16)
        dunext[...] = du[0:8, :]

        @pl.when(t0 == nt - 1)
        def _():
            dlam_ref[...] = dlam_ref[...] * (-_sigmoid(-lam))

    rev = lambda i: (nt - 1 - i, 0)
    row = pl.BlockSpec((tm, LW), rev)
    prev8 = pl.BlockSpec((8, LW), lambda i: (jnp.maximum((nt - 1 - i) * (tm // 8) - 1, 0), 0))
    full = lambda a: pl.BlockSpec(a.shape, lambda i: (0,) * a.ndim)
    vec = pl.BlockSpec((1, LW), lambda i: (0, 0))
    bd = pl.BlockSpec((NGRP, 256, 256), lambda i: (0, 0, 0))
    return pl.pallas_call(
        body, name="lru_bwd", grid=(nt,),
        in_specs=[row, row, prev8, row, row, prev8, full(conv_w), full(wr), full(wi), full(br), full(bi), full(lam)],
        out_specs=[row, bd, bd, vec, vec, vec, vec, pl.BlockSpec((CONVW, LW), lambda i: (0, 0))],
        out_shape=[jax.ShapeDtypeStruct((s, LW), bf16),
                   jax.ShapeDtypeStruct((NGRP, 256, 256), f32), jax.ShapeDtypeStruct((NGRP, 256, 256), f32),
                   jax.ShapeDtypeStruct((1, LW), f32), jax.ShapeDtypeStruct((1, LW), f32),
                   jax.ShapeDtypeStruct((1, LW), f32), jax.ShapeDtypeStruct((1, LW), f32),
                   jax.ShapeDtypeStruct((CONVW, LW), f32)],
        scratch_shapes=[pltpu.VMEM((tm + 8, LW), f32), pltpu.VMEM((tm, LW), f32), pltpu.VMEM((tm, LW), f32),
                        pltpu.VMEM((1, LW), f32), pltpu.VMEM((8, LW), f32)],
        compiler_params=_params(("arbitrary",), 56),
    )(u, hl, hl, dhl, xl, xl, conv_w, wr, wi, br, bi, lam)


def _gated_norm(t, gate, gain):
    sg = _sigmoid(gate)
    silu = gate * sg
    p = t * silu
    rstd = lax.rsqrt(jnp.mean(p * p, axis=-1, keepdims=True) + EPS)
    ph = p * rstd
    return sg, silu, rstd, ph, ph * gain


def _gated_norm_bwd(dy, t, gate, gain, sg, silu, rstd, ph):
    w = dy * gain
    dp = rstd * (w - ph * jnp.mean(w * ph, axis=-1, keepdims=True))
    dgate = dp * t * (sg * (1.0 + gate * (1.0 - sg)))
    return jnp.sum(dy * ph, axis=0, keepdims=True), dp * silu, dgate


def _out_fwd_bwd(x, tgt, o, ga, hl, gl, again, lgain, fgain, wo, tm):
    s = x.shape[0]
    nt = s // tm

    def body(x_ref, t_ref, o_ref, ga_ref, hl_ref, gl_ref, ag_ref, lg_ref, fg_ref, wo_ref,
             dx2_ref, do_ref, dga_ref, dhl_ref, dgl_ref, dwo_ref, gfg_ref, gag_ref, glg_ref, loss_ref, acc):
        i = pl.program_id(0)

        @pl.when(i == 0)
        def _():
            acc[...] = jnp.zeros_like(acc)
            for ref in (gfg_ref, gag_ref, glg_ref, loss_ref):
                ref[...] = jnp.zeros_like(ref)

        oo, gga, hh, ggl = o_ref[...], ga_ref[...], hl_ref[...], gl_ref[...]
        ag, lg, fg = ag_ref[...], lg_ref[...], fg_ref[...]
        sga, silua, ra, pah, ya = _gated_norm(oo, gga, ag)
        sgl, silul, rl, plh, yl = _gated_norm(hh, ggl, lg)
        yab, ylb = ya.astype(bf16), yl.astype(bf16)
        y = _dot(yab, wo_ref[0:D, :]) + _dot(ylb, wo_ref[D:2 * D, :])
        x2 = x_ref[...] + y
        r2 = lax.rsqrt(jnp.mean(x2 * x2, axis=-1, keepdims=True) + EPS)
        x2h = x2 * r2
        err = x2h * fg - t_ref[...]
        loss_ref[...] += 0.5 * jnp.sum(jnp.sum(err * err, axis=-1, keepdims=True) * (1.0 / D))
        dout = err * (1.0 / D)
        gfg_ref[...] += jnp.sum(dout * x2h, axis=0, keepdims=True)
        w = dout * fg
        dx2 = r2 * (w - x2h * jnp.mean(w * x2h, axis=-1, keepdims=True))
        dx2_ref[...] = dx2
        dyb = dx2.astype(bf16)
        acc[0:D, :] += _dot_tn(yab, dyb)
        acc[D:2 * D, :] += _dot_tn(ylb, dyb)
        dya = _dot_nt(dyb, wo_ref[0:D, :])
        dyl = _dot_nt(dyb, wo_ref[D:2 * D, :])
        gag, do, dga = _gated_norm_bwd(dya, oo, gga, ag, sga, silua, ra, pah)
        glg, dhl, dgl = _gated_norm_bwd(dyl, hh, ggl, lg, sgl, silul, rl, plh)
        gag_ref[...] += gag
        glg_ref[...] += glg
        do_ref[...] = do.astype(bf16)
        dga_ref[...] = dga.astype(bf16)
        dhl_ref[...] = dhl
        dgl_ref[...] = dgl.astype(bf16)

        @pl.when(i == nt - 1)
        def _():
            dwo_ref[...] = acc[...].astype(bf16)

    row = pl.BlockSpec((tm, D), lambda i: (i, 0))
    vec = pl.BlockSpec((1, D), lambda i: (0, 0))
    mat = pl.BlockSpec((2 * D, D), lambda i: (0, 0))
    return pl.pallas_call(
        body, name="out_fwd_bwd", grid=(nt,),
        in_specs=[row] * 6 + [vec] * 3 + [mat],
        out_specs=[row] * 5 + [mat, vec, vec, vec, pl.BlockSpec((1, 128), lambda i: (0, 0))],
        out_shape=[jax.ShapeDtypeStruct((s, D), f32), jax.ShapeDtypeStruct((s, D), bf16),
                   jax.ShapeDtypeStruct((s, D), bf16), jax.ShapeDtypeStruct((s, D), f32),
                   jax.ShapeDtypeStruct((s, D), bf16), jax.ShapeDtypeStruct((2 * D, D), bf16),
                   jax.ShapeDtypeStruct((1, D), f32), jax.ShapeDtypeStruct((1, D), f32),
                   jax.ShapeDtypeStruct((1, D), f32), jax.ShapeDtypeStruct((1, 128), f32)],
        scratch_shapes=[pltpu.VMEM((2 * D, D), f32)],
        compiler_params=_params(("arbitrary",), 56),
    )(x, tgt, o, ga, hl, gl, again, lgain, fgain, wo)


def _bwd_in(x, dx2, dq, dk, dv, dga, dxl, dgl, ln_gain, wt, tabs, tm):
    s = x.shape[0]

    def body(x_ref, dx2_ref, dq_ref, dk_ref, dv_ref, dga_ref, dxl_ref, dgl_ref, g_ref, wt_ref,
             c_ref, sa_ref, sb_ref, gx_ref, gln_ref, dz_ref, dz_scr):
        @pl.when(pl.program_id(0) == 0)
        def _():
            gln_ref[...] = jnp.zeros_like(gln_ref)

        c, sa, sb = c_ref[...], sa_ref[...], sb_ref[...]
        for j in range(D // 128):
            js = slice(128 * j, 128 * j + 128)
            dz_scr[:, js] = (_unrope(dq_ref[:, js], c, sa, sb) * (HD ** -0.5)).astype(bf16)
        for j in range(KVW // 128):
            js = slice(128 * j, 128 * j + 128)
            dz_scr[:, D + 128 * j:D + 128 * j + 128] = _unrope(dk_ref[:, js], c, sa, sb).astype(bf16)
        dz_scr[:, D + KVW:D + 2 * KVW] = dv_ref[...].astype(bf16)
        dz_scr[:, 1536:2560] = dga_ref[...]
        dz_scr[:, 2560:3584] = dxl_ref[...]
        dz_scr[:, 3584:4608] = dgl_ref[...]
        for p in range(NDEV):
            dz_ref[p] = dz_scr[:, WT_ROWS * p:WT_ROWS * (p + 1)]
        dh = _dot(dz_scr[:, 0:512], wt_ref[0:512, :])
        for ci in range(1, NIN // 512):
            dh = dh + _dot(dz_scr[:, 512 * ci:512 * ci + 512], wt_ref[512 * ci:512 * ci + 512, :])
        xx = x_ref[...]
        rstd = lax.rsqrt(jnp.mean(xx * xx, axis=-1, keepdims=True) + EPS)
        xh = xx * rstd
        gln_ref[...] += jnp.sum(dh * xh, axis=0, keepdims=True)
        w = dh * g_ref[...]
        gx_ref[...] = dx2_ref[...] + rstd * (w - xh * jnp.mean(w * xh, axis=-1, keepdims=True))

    row = lambda w: pl.BlockSpec((tm, w), lambda i: (i, 0))
    full = lambda a: pl.BlockSpec(a.shape, lambda i: (0, 0))
    return pl.pallas_call(
        body, name="bwd_in", grid=(s // tm,),
        in_specs=[row(D), row(D), row(D), row(KVW), row(KVW), row(D), row(D), row(D), full(ln_gain), full(wt),
                  row(128), row(128), row(128)],
        out_specs=[row(D), pl.BlockSpec((1, D), lambda i: (0, 0)),
                   pl.BlockSpec((NDEV, tm, WT_ROWS), lambda i: (0, i, 0))],
        out_shape=[jax.ShapeDtypeStruct((s, D), f32), jax.ShapeDtypeStruct((1, D), f32),
                   jax.ShapeDtypeStruct((NDEV, s, WT_ROWS), bf16)],
        scratch_shapes=[pltpu.VMEM((tm, NIN), bf16)],
        compiler_params=_params(("arbitrary",), 56),
    )(x, dx2, dq, dk, dv, dga, dxl, dgl, ln_gain, wt, *tabs)


def _dwt(dzs, h, tm):
    s = h.shape[0]
    nk = s // tm

    def body(dz_ref, h_ref, o_ref, acc):
        k = pl.program_id(1)

        @pl.when(k == 0)
        def _():
            acc[...] = jnp.zeros_like(acc)

        acc[...] += _dot_tn(dz_ref[...], h_ref[...])

        @pl.when(k == nk - 1)
        def _():
            o_ref[...] = acc[...].astype(bf16)

    return pl.pallas_call(
        body, name="dwt", grid=(NDEV, nk),
        in_specs=[pl.BlockSpec((None, tm, WT_ROWS), lambda p, k: (p, k, 0)), pl.BlockSpec((tm, D), lambda p, k: (k, 0))],
        out_specs=pl.BlockSpec((WT_ROWS, D), lambda p, k: (p, 0)),
        out_shape=jax.ShapeDtypeStruct((NIN, D), bf16),
        scratch_shapes=[pltpu.VMEM((WT_ROWS, D), f32)],
        compiler_params=_params(("arbitrary", "arbitrary"), 32),
    )(dzs, h)


def _diag_blocks(bd):
    return jnp.stack([bd[n // 4, HD * (n % 4):HD * (n % 4) + HD, HD * (n % 4):HD * (n % 4) + HD] for n in range(NQ)])


def _local_step(x, tgt, wt, wo, conv_w, p):
    s = x.shape[0]
    tm = min(256, s)
    tabs = _rope_tables(s)
    wr, wi = _block_diag(p["w_rgate"]), _block_diag(p["w_igate"])
    sinks = p["sinks"].reshape(NQ)
    h, q, k, v, ga, xl, gl = _fwd_in(x, p["ln_gain"], wt, tabs, tm)
    o = _attn_fwd(q, k, v, sinks)
    u, hl = _lru_fwd(xl, conv_w, p["conv_b"], wr, wi, p["b_rgate"], p["b_igate"], p["lru_lambda"], tm)
    dx2, do, dga, dhl, dgl, dwo, g_fg, g_ag, g_lg, loss = _out_fwd_bwd(
        x, tgt, o, ga, hl, gl, p["attn_out_gain"], p["lru_out_gain"], p["final_gain"], wo, tm)
    dq, dk, dv, dsink = _attn_bwd(q, k, v, do, sinks)
    dxl, dwr, dwi, dbr, dbi, dlam, dcb, dcw = _lru_bwd(
        u, hl, dhl, xl, conv_w, wr, wi, p["b_rgate"], p["b_igate"], p["lru_lambda"], tm)
    gx, g_ln, dzs = _bwd_in(x, dx2, dq, dk, dv, dga, dxl, dgl, p["ln_gain"], wt, tabs, tm)
    dwt = _dwt(dzs, h, min(512, s))
    small = dict(ln_gain=g_ln, sinks=dsink[0:1, 0:NQ], conv_w=dcw, conv_b=dcb, w_rgate=_diag_blocks(dwr),
                 b_rgate=dbr, w_igate=_diag_blocks(dwi), b_igate=dbi, lru_lambda=dlam, attn_out_gain=g_ag,
                 lru_out_gain=g_lg, final_gain=g_fg)
    return loss, gx, dwt, dwo, small


def _place():
    return lax.axis_index("x"), lax.axis_index("y"), lax.axis_index("c")


def _all_gather(srcs, out_dtypes, name):
    n = len(srcs)
    cast = [a.dtype != dt for a, dt in zip(srcs, out_dtypes)]

    def body(*refs):
        src_refs, out_refs = refs[:n], refs[n:2 * n]
        stage_refs = list(refs[2 * n:2 * n + sum(cast)])
        send_sems, recv_sems, local_sems = refs[-3:]
        x, y, c = _place()
        me, sibling = (x, y, c), (x, y, 1 - c)
        chips = [(1 - x, y), (x, 1 - y), (1 - x, 1 - y)]
        mine_refs = []
        for a in range(n):
            if cast[a]:
                st = stage_refs.pop(0)
                st[...] = src_refs[a][...].astype(out_dtypes[a])
                mine_refs.append(st)
            else:
                mine_refs.append(src_refs[a])

        def rows(a, dev):
            m = srcs[a].shape[0]
            return out_refs[a].at[pl.ds((4 * dev[0] + 2 * dev[1] + dev[2]) * m, m), :]

        def copy(a, k, block, to, own=False):
            return pltpu.make_async_remote_copy(
                src_ref=mine_refs[a] if own else rows(a, block), dst_ref=rows(a, block),
                send_sem=send_sems.at[a, k], recv_sem=recv_sems.at[a, k], device_id=to, device_id_type=MESH)

        local = [pltpu.make_async_copy(mine_refs[a], rows(a, me), local_sems.at[a]) for a in range(n)]
        for cp in local:
            cp.start()
        first = []
        for a in range(n):
            first.append(copy(a, 0, me, sibling, own=True))
            first += [copy(a, 1 + j, me, (*chip, c), own=True) for j, chip in enumerate(chips)]
        for cp in first:
            cp.start()
        passed = []
        for j, chip in enumerate(chips):
            for a in range(n):
                copy(a, 1 + j, (*chip, c), me).wait_recv()
                fwd = copy(a, 4 + j, (*chip, c), sibling)
                fwd.start()
                passed.append(fwd)
        for a in range(n):
            copy(a, 0, sibling, me).wait_recv()
            for j, chip in enumerate(chips):
                copy(a, 4 + j, (*chip, 1 - c), me).wait_recv()
        for cp in first + passed:
            cp.wait_send()
        for cp in local:
            cp.wait()

    vmem = pl.BlockSpec(memory_space=pltpu.VMEM)
    hbm = pl.BlockSpec(memory_space=pl.ANY)
    return pl.pallas_call(
        body, name=name,
        in_specs=[vmem] * n, out_specs=[hbm] * n,
        out_shape=[jax.ShapeDtypeStruct((NDEV * a.shape[0], a.shape[1]), dt) for a, dt in zip(srcs, out_dtypes)],
        scratch_shapes=[pltpu.VMEM(a.shape, dt) for a, dt, cst in zip(srcs, out_dtypes, cast) if cst]
        + [pltpu.SemaphoreType.DMA((n, 7)), pltpu.SemaphoreType.DMA((n, 7)), pltpu.SemaphoreType.DMA((n,))],
        compiler_params=pltpu.CompilerParams(vmem_limit_bytes=32 * MIB),
    )(*srcs)


def _scatter_grads(parts):
    n = len(parts)

    def body(*refs):
        src_refs, land_refs = refs[:n], refs[n:2 * n]
        send_sems, recv_sems, local_sems = refs[-3:]
        x, y, c = _place()
        my = 4 * x + 2 * y + c

        def piece(a, dev):
            m = parts[a].shape[0] // NDEV
            return src_refs[a].at[pl.ds(dev * m, m), :]

        local = [pltpu.make_async_copy(piece(a, my), land_refs[a].at[my], local_sems.at[a]) for a in range(n)]
        for cp in local:
            cp.start()
        copies = []
        for k in range(1, NDEV):
            px, py, pc = x ^ (k >> 2), y ^ ((k >> 1) & 1), c ^ (k & 1)
            for a in range(n):
                copies.append(pltpu.make_async_remote_copy(
                    src_ref=piece(a, 4 * px + 2 * py + pc), dst_ref=land_refs[a].at[my],
                    send_sem=send_sems.at[a, k - 1], recv_sem=recv_sems.at[a, k - 1],
                    device_id=(px, py, pc), device_id_type=MESH))
        for cp in copies:
            cp.start()
        for cp in copies:
            cp.wait_send()
        for k in range(1, NDEV):
            px, py, pc = x ^ (k >> 2), y ^ ((k >> 1) & 1), c ^ (k & 1)
            for a in range(n):
                pltpu.make_async_remote_copy(
                    src_ref=piece(a, my), dst_ref=land_refs[a].at[4 * px + 2 * py + pc],
                    send_sem=send_sems.at[a, k - 1], recv_sem=recv_sems.at[a, k - 1],
                    device_id=(px, py, pc), device_id_type=MESH).wait_recv()
        for cp in local:
            cp.wait()

    hbm = pl.BlockSpec(memory_space=pl.ANY)
    return pl.pallas_call(
        body, name="scatter_grads",
        in_specs=[hbm] * n, out_specs=[hbm] * n,
        out_shape=[jax.ShapeDtypeStruct((NDEV, a.shape[0] // NDEV, a.shape[1]), a.dtype) for a in parts],
        scratch_shapes=[pltpu.SemaphoreType.DMA((n, 7)), pltpu.SemaphoreType.DMA((n, 7)),
                        pltpu.SemaphoreType.DMA((n,))],
    )(*parts)


def _sum_slots(land, tr, name):
    _, rows, cols = land.shape

    def body(l_ref, o_ref):
        acc = l_ref[0].astype(f32)
        for d in range(1, NDEV):
            acc = acc + l_ref[d].astype(f32)
        o_ref[...] = acc

    return pl.pallas_call(
        body, name=name, grid=(rows // tr,),
        in_specs=[pl.BlockSpec((NDEV, tr, cols), lambda i: (0, i, 0))],
        out_specs=pl.BlockSpec((tr, cols), lambda i: (i, 0)),
        out_shape=jax.ShapeDtypeStruct((rows, cols), f32),
        compiler_params=_params(("arbitrary",), 32),
    )(land)


def _adam_math(w, g, m, v):
    m2 = ADAM_B1 * m + (1.0 - ADAM_B1) * g
    v2 = ADAM_B2 * v + (1.0 - ADAM_B2) * (g * g)
    m_hat = m2 / (1.0 - ADAM_B1 ** ADAM_STEP)
    v_hat = v2 / (1.0 - ADAM_B2 ** ADAM_STEP)
    delta = -ADAM_LR * (m_hat / (jnp.sqrt(v_hat) + ADAM_EPS) + ADAM_WD * w)
    return delta, m2, v2


def _adamw(w, g, m, v, tr, name):
    rows, cols = w.shape

    def body(w_ref, g_ref, m_ref, v_ref, d_ref, m2_ref, v2_ref):
        d_ref[...], m2_ref[...], v2_ref[...] = _adam_math(w_ref[...], g_ref[...], m_ref[...], v_ref[...])

    blk = pl.BlockSpec((tr, cols), lambda i: (i, 0))
    return pl.pallas_call(
        body, name=name, grid=(rows // tr,),
        in_specs=[blk] * 4, out_specs=[blk] * 3,
        out_shape=[jax.ShapeDtypeStruct((rows, cols), f32)] * 3,
        compiler_params=_params(("arbitrary",), 32),
    )(w, g, m, v)


VEC_NAMES = ("ln_gain", "conv_b", "b_rgate", "b_igate", "lru_lambda", "attn_out_gain", "lru_out_gain", "final_gain")
ROW_RGATE, ROW_IGATE, ROW_VEC, ROW_SINKS = 0, 64, 128, 136
LOSS_LANE = NQ


def _adamw_small(g_rep, g_conv, w, m, v):
    names = list(VEC_NAMES) + ["sinks", "conv_w", "w_rgate", "w_igate"]
    ins = [g_rep, g_conv] + [d[k] for k in names for d in (w, m, v)]

    def body(*refs):
        g_ref, gc_ref = refs[0], refs[1]
        in_refs = refs[2:2 + 3 * len(names)]
        out_refs = refs[2 + 3 * len(names):]

        def update(j, g, at=None):
            w_ref, m_ref, v_ref = in_refs[3 * j:3 * j + 3]
            outs = out_refs[4 * j:4 * j + 4]
            pick = (lambda r: r[...]) if at is None else (lambda r: r[at])
            res = (g,) + _adam_math(pick(w_ref), g, pick(m_ref), pick(v_ref))
            for o_ref, val in zip(outs, res):
                if at is None:
                    o_ref[...] = val
                else:
                    o_ref[at] = val

        for j in range(len(VEC_NAMES)):
            update(j, g_ref[ROW_VEC + j:ROW_VEC + j + 1, :])
        update(len(VEC_NAMES), g_ref[ROW_SINKS:ROW_SINKS + 1, 0:NQ])
        update(len(VEC_NAMES) + 1, gc_ref[...], at=0)
        for gi, row0 in ((len(VEC_NAMES) + 2, ROW_RGATE), (len(VEC_NAMES) + 3, ROW_IGATE)):
            for nb in range(NQ):
                update(gi, g_ref[row0:row0 + HD, HD * nb:HD * nb + HD], at=(0, nb))

    vmem = pl.BlockSpec(memory_space=pltpu.VMEM)
    out_shape = [jax.ShapeDtypeStruct(w[k].shape, f32) for k in names for _ in range(4)]
    outs = pl.pallas_call(
        body, name="adamw_small",
        in_specs=[vmem] * len(ins), out_specs=[vmem] * len(out_shape), out_shape=out_shape,
        compiler_params=pltpu.CompilerParams(vmem_limit_bytes=32 * MIB),
    )(*ins)
    return {k: tuple(outs[4 * j:4 * j + 4]) for j, k in enumerate(names)}


def _pack_small(small, loss):
    gate = lambda g: g.transpose(1, 0, 2).reshape(HD, NQ * HD)
    row_s = jnp.concatenate([small["sinks"], loss[:, LOSS_LANE:128], jnp.zeros((1, D - 128), f32)], axis=1)
    rep = jnp.concatenate([gate(small["w_rgate"]), gate(small["w_igate"])] + [small[k] for k in VEC_NAMES]
                          + [row_s, jnp.zeros((SMALL_ROWS - ROW_SINKS - 1, D), f32)], axis=0)
    conv = small["conv_w"].reshape(CONVW, NDEV, 128).transpose(1, 0, 2)
    conv = jnp.pad(conv, ((0, 0), (0, 8 - CONVW), (0, D - 128)))
    return jnp.concatenate([rep.reshape(NDEV, SMALL_PER, D), conv], axis=1).reshape(NDEV * (SMALL_PER + 8), D)


def kernel(x, ln_gain, w_in, sinks, conv_w, conv_b, w_rgate, b_rgate, w_igate, b_igate, lru_lambda, attn_out_gain, lru_out_gain, w_out, final_gain, loss_target, m_ln_gain, m_w_in, m_sinks, m_conv_w, m_conv_b, m_w_rgate, m_b_rgate, m_w_igate, m_b_igate, m_lru_lambda, m_attn_out_gain, m_lru_out_gain, m_w_out, m_final_gain, v_ln_gain, v_w_in, v_sinks, v_conv_w, v_conv_b, v_w_rgate, v_b_rgate, v_w_igate, v_b_igate, v_lru_lambda, v_attn_out_gain, v_lru_out_gain, v_w_out, v_final_gain):
    w = dict(ln_gain=ln_gain, sinks=sinks, conv_w=conv_w, conv_b=conv_b, w_rgate=w_rgate, b_rgate=b_rgate,
             w_igate=w_igate, b_igate=b_igate, lru_lambda=lru_lambda, attn_out_gain=attn_out_gain,
             lru_out_gain=lru_out_gain, final_gain=final_gain.reshape(1, D))
    m = dict(ln_gain=m_ln_gain, sinks=m_sinks, conv_w=m_conv_w, conv_b=m_conv_b, w_rgate=m_w_rgate,
             b_rgate=m_b_rgate, w_igate=m_w_igate, b_igate=m_b_igate, lru_lambda=m_lru_lambda,
             attn_out_gain=m_attn_out_gain, lru_out_gain=m_lru_out_gain, final_gain=m_final_gain.reshape(1, D))
    v = dict(ln_gain=v_ln_gain, sinks=v_sinks, conv_w=v_conv_w, conv_b=v_conv_b, w_rgate=v_w_rgate,
             b_rgate=v_b_rgate, w_igate=v_w_igate, b_igate=v_b_igate, lru_lambda=v_lru_lambda,
             attn_out_gain=v_attn_out_gain, lru_out_gain=v_lru_out_gain, final_gain=v_final_gain.reshape(1, D))

    conv_blk = jnp.pad(conv_w[0], ((0, 8 - CONVW), (0, 0)))
    wt, wo, cw_all = _all_gather([w_in[0].T, w_out[0], conv_blk], [bf16, bf16, f32], "gather_weights")
    conv_full = cw_all.reshape(NDEV, 8, 128)[:, 0:CONVW].transpose(1, 0, 2).reshape(CONVW, LW)

    p = {k: (w[k][0] if k in ("w_rgate", "w_igate") else w[k]) for k in w if k != "conv_w"}
    loss, gx, dwt, dwo, small = _local_step(x[0], loss_target[0], wt, wo, conv_full, p)

    land_wt, land_wo, land_sm = _scatter_grads([dwt, dwo, _pack_small(small, loss)])
    g_wt = _sum_slots(land_wt, 192, "sum_wt")
    g_wo = _sum_slots(land_wo, 256, "sum_wo")
    g_sm = _sum_slots(land_sm, SMALL_PER + 8, "sum_small")
    (g_rep,) = _all_gather([g_sm[0:SMALL_PER]], [f32], "gather_small")
    g_conv = g_sm[SMALL_PER:SMALL_PER + CONVW, 0:128]

    g_win = g_wt.T
    d_win, m_win, v_win = _adamw(w_in[0], g_win, m_w_in[0], v_w_in[0], 256, "adamw_w_in")
    d_wo, m_wo, v_wo = _adamw(w_out[0], g_wo, m_w_out[0], v_w_out[0], 256, "adamw_w_out")
    res = _adamw_small(g_rep, g_conv, w, m, v)
    res["w_in"] = tuple(t[None] for t in (g_win, d_win, m_win, v_win))
    res["w_out"] = tuple(t[None] for t in (g_wo, d_wo, m_wo, v_wo))
    res["final_gain"] = tuple(t.reshape(D) for t in res["final_gain"])

    order = ("ln_gain", "w_in", "sinks", "conv_w", "conv_b", "w_rgate", "b_rgate", "w_igate", "b_igate",
             "lru_lambda", "attn_out_gain", "lru_out_gain", "w_out", "final_gain")
    total_loss = g_rep[ROW_SINKS, LOSS_LANE]
    return (total_loss, gx[None]) + tuple(res[k][i] for i in range(4) for k in order)
```

```python
import jax
import jax.numpy as jnp
from jax import lax
from jax.experimental import pallas as pl
from jax.experimental.pallas import tpu as pltpu

f32 = jnp.float32
bf16 = jnp.bfloat16

D = 1024
HD = 64
NQ = 16
NKV = 4
GROUP = NQ // NKV
KVW = NKV * HD
BLK = 128
ROT = 16
THETA = 500000.0
NEG = -1e30
LW = 1024
NGRP = 4
CONVW = 4
LRU_C = 8.0
NIN = 4608
EPS = 1e-6
NDEV = 8
WT_ROWS = NIN // NDEV
WO_ROWS = 2 * D // NDEV
SMALL_ROWS = 192
SMALL_PER = SMALL_ROWS // NDEV

ADAM_LR = 0.001
ADAM_B1 = 0.9
ADAM_B2 = 0.999
ADAM_EPS = 1e-08
ADAM_WD = 0.01
ADAM_STEP = 10

NT = (((1,), (1,)), ((), ()))
TN = (((0,), (0,)), ((), ()))
MESH = pl.DeviceIdType.MESH
MIB = 1024 * 1024


def _dot(a, b):
    return jnp.dot(a, b, preferred_element_type=f32)


def _dot_nt(a, b):
    return lax.dot_general(a, b, NT, preferred_element_type=f32)


def _dot_tn(a, b):
    return lax.dot_general(a, b, TN, preferred_element_type=f32)


def _params(sem, vmem_mib):
    return pltpu.CompilerParams(dimension_semantics=sem, vmem_limit_bytes=vmem_mib * MIB)


def _sigmoid(x):
    return 1.0 / (1.0 + jnp.exp(-x))


def _softplus(x):
    return jnp.maximum(x, 0.0) + jnp.log(1.0 + jnp.exp(-jnp.abs(x)))


def _neg_expm1(x):
    series = -x * (1.0 + x * (0.5 + x * (1.0 / 6.0 + x * (1.0 / 24.0 + x * (1.0 / 120.0)))))
    return jnp.where(x > -0.1, series, 1.0 - jnp.exp(x))


def _rope_tables(s):
    pos = jnp.arange(s, dtype=f32)
    inv_freq = THETA ** (-jnp.arange(0, ROT, 2, dtype=f32) / ROT)
    d = jnp.arange(128) % HD
    ang = pos[:, None] * inv_freq[d % (ROT // 2)][None, :]
    cos, sin = jnp.cos(ang), jnp.sin(ang)
    c = jnp.where(d < ROT, cos, 1.0)
    sa = jnp.where((d >= ROT // 2) & (d < ROT), sin, 0.0)
    sb = jnp.where(d < ROT // 2, -sin, 0.0)
    return c, sa, sb


def _rope(t, c, sa, sb):
    return t * c + pltpu.roll(t, 8, 1) * sa + pltpu.roll(t, 120, 1) * sb


def _unrope(dr, c, sa, sb):
    return dr * c + pltpu.roll(dr * sa, 120, 1) + pltpu.roll(dr * sb, 8, 1)


def _fwd_in(x, ln_gain, wt, tabs, tm):
    s = x.shape[0]
    nc = 512

    def body(x_ref, g_ref, wt_ref, c_ref, sa_ref, sb_ref, h_ref, q_ref, k_ref, v_ref, ga_ref, xl_ref, gl_ref):
        xx = x_ref[...]
        rstd = lax.rsqrt(jnp.mean(xx * xx, axis=-1, keepdims=True) + EPS)
        h = (xx * rstd * g_ref[...]).astype(bf16)
        h_ref[...] = h
        c, sa, sb = c_ref[...], sa_ref[...], sb_ref[...]

        def z_chunk(ci):
            return _dot_nt(h, wt_ref[ci * nc:(ci + 1) * nc, :])

        for ci in range(2):
            z = z_chunk(ci)
            for j in range(nc // 128):
                r = _rope(z[:, 128 * j:128 * j + 128], c, sa, sb) * (HD ** -0.5)
                q_ref[:, ci * nc + 128 * j:ci * nc + 128 * j + 128] = r.astype(bf16)
        z = z_chunk(2)
        for j in range(2):
            k_ref[:, 128 * j:128 * j + 128] = _rope(z[:, 128 * j:128 * j + 128], c, sa, sb).astype(bf16)
        v_ref[...] = z[:, 256:512].astype(bf16)
        for i, ref in enumerate((ga_ref, xl_ref, gl_ref)):
            for j in range(2):
                ref[:, j * nc:(j + 1) * nc] = z_chunk(3 + 2 * i + j)

    row = lambda w: pl.BlockSpec((tm, w), lambda i: (i, 0))
    full = lambda a: pl.BlockSpec(a.shape, lambda i: (0, 0))
    return pl.pallas_call(
        body, name="fwd_in", grid=(s // tm,),
        in_specs=[row(D), full(ln_gain), full(wt), row(128), row(128), row(128)],
        out_specs=[row(D), row(D), row(KVW), row(KVW), row(D), row(D), row(D)],
        out_shape=[jax.ShapeDtypeStruct((s, D), bf16), jax.ShapeDtypeStruct((s, D), bf16),
                   jax.ShapeDtypeStruct((s, KVW), bf16), jax.ShapeDtypeStruct((s, KVW), bf16),
                   jax.ShapeDtypeStruct((s, D), f32), jax.ShapeDtypeStruct((s, D), f32),
                   jax.ShapeDtypeStruct((s, D), f32)],
        compiler_params=_params(("arbitrary",), 48),
    )(x, ln_gain, wt, *tabs)


HSUB = 4
SUBW = HSUB * BLK


def _sub_probs(kh, qg, n, sink_row):
    jj = lax.broadcasted_iota(jnp.int32, (BLK, SUBW), 0)
    ii = lax.broadcasted_iota(jnp.int32, (BLK, SUBW), 1) % BLK
    from_prev = jj > ii
    s2 = _dot_nt(kh, qg)
    sc = jnp.where(from_prev, s2[0:BLK] + jnp.where(n > 0, 0.0, NEG), s2[BLK:2 * BLK])
    m = jnp.maximum(jnp.max(sc, axis=0, keepdims=True), sink_row)
    p = jnp.exp(sc - m)
    es = jnp.exp(sink_row - m)
    inv = 1.0 / (jnp.sum(p, axis=0, keepdims=True) + es)
    return from_prev, p * inv, es * inv


def _split(t, from_prev):
    t = t.astype(bf16)
    zero = jnp.zeros_like(t)
    return jnp.concatenate([jnp.where(from_prev, t, zero), jnp.where(from_prev, zero, t)], axis=0)


def _stack_heads(ref, first):
    return jnp.concatenate([ref[:, HD * (first + g):HD * (first + g) + HD] for g in range(HSUB)], axis=0)


def _sink_rows(sinks):
    return jnp.repeat(sinks.reshape(NKV, GROUP), BLK, axis=1)


def _kv_specs():
    prev = pl.BlockSpec((BLK, KVW), lambda n: (jnp.maximum(n - 1, 0), 0))
    cur = pl.BlockSpec((BLK, KVW), lambda n: (n, 0))
    return [prev, cur, prev, cur]


def _attn_fwd(q, k, v, sinks):
    s = q.shape[0]

    def body(sink_ref, q_ref, kp_ref, kc_ref, vp_ref, vc_ref, o_ref):
        n = pl.program_id(0)
        for h in range(NKV):
            hs = slice(HD * h, HD * h + HD)
            kh = jnp.concatenate([kp_ref[:, hs], kc_ref[:, hs]], axis=0)
            vh = jnp.concatenate([vp_ref[:, hs], vc_ref[:, hs]], axis=0)
            for t in range(GROUP // HSUB):
                first = GROUP * h + HSUB * t
                from_prev, pn, _ = _sub_probs(kh, _stack_heads(q_ref, first), n,
                                              sink_ref[h:h + 1, SUBW * t:SUBW * t + SUBW])
                og = _dot_tn(_split(pn, from_prev), vh)
                for g in range(HSUB):
                    o_ref[:, HD * (first + g):HD * (first + g) + HD] = og[BLK * g:BLK * g + BLK]

    return pl.pallas_call(
        body, name="attn_fwd", grid=(s // BLK,),
        in_specs=[pl.BlockSpec((NKV, GROUP * BLK), lambda n: (0, 0)), pl.BlockSpec((BLK, D), lambda n: (n, 0))]
        + _kv_specs(),
        out_specs=pl.BlockSpec((BLK, D), lambda n: (n, 0)),
        out_shape=jax.ShapeDtypeStruct((s, D), f32),
        compiler_params=_params(("arbitrary",), 32),
    )(_sink_rows(sinks), q, k, k, v, v)


def _attn_bwd(q, k, v, do, sinks):
    s = q.shape[0]

    def body(sink_ref, q_ref, do_ref, kp_ref, kc_ref, vp_ref, vc_ref, dq_ref, dk_ref, dv_ref, ds_ref):
        n = pl.program_id(0)

        @pl.when(n == 0)
        def _():
            dk_ref[...] = jnp.zeros_like(dk_ref)
            dv_ref[...] = jnp.zeros_like(dv_ref)
            ds_ref[...] = jnp.zeros_like(ds_ref)

        prev_rows = pl.ds(pl.multiple_of(jnp.maximum(n - 1, 0) * BLK, BLK), BLK)
        cur_rows = pl.ds(pl.multiple_of(n * BLK, BLK), BLK)
        for h in range(NKV):
            hs = slice(HD * h, HD * h + HD)
            kh = jnp.concatenate([kp_ref[:, hs], kc_ref[:, hs]], axis=0)
            vh = jnp.concatenate([vp_ref[:, hs], vc_ref[:, hs]], axis=0)
            dkh = jnp.zeros((2 * BLK, HD), f32)
            dvh = jnp.zeros((2 * BLK, HD), f32)
            for t in range(GROUP // HSUB):
                first = GROUP * h + HSUB * t
                lanes = slice(SUBW * t, SUBW * t + SUBW)
                qg, dog = _stack_heads(q_ref, first), _stack_heads(do_ref, first)
                from_prev, pn, ps = _sub_probs(kh, qg, n, sink_ref[h:h + 1, lanes])
                dp2 = _dot_nt(vh, dog)
                dp = jnp.where(from_prev, dp2[0:BLK], dp2[BLK:2 * BLK])
                dsum = jnp.sum(pn * dp, axis=0, keepdims=True)
                ds_ref[h:h + 1, lanes] += -ps * dsum
                ds2 = _split(pn * (dp - dsum), from_prev)
                dqg = _dot_tn(ds2, kh)
                for g in range(HSUB):
                    dq_ref[:, HD * (first + g):HD * (first + g) + HD] = dqg[BLK * g:BLK * g + BLK]
                dkh = dkh + _dot(ds2, qg)
                dvh = dvh + _dot(_split(pn, from_prev), dog)
            dk_ref[prev_rows, hs] += dkh[0:BLK]
            dk_ref[cur_rows, hs] += dkh[BLK:2 * BLK]
            dv_ref[prev_rows, hs] += dvh[0:BLK]
            dv_ref[cur_rows, hs] += dvh[BLK:2 * BLK]

    blk = pl.BlockSpec((BLK, D), lambda n: (n, 0))
    whole = lambda r, w: pl.BlockSpec((r, w), lambda n: (0, 0))
    return pl.pallas_call(
        body, name="attn_bwd", grid=(s // BLK,),
        in_specs=[whole(NKV, GROUP * BLK), blk, blk] + _kv_specs(),
        out_specs=[blk, whole(s, KVW), whole(s, KVW), whole(NKV, GROUP * BLK)],
        out_shape=[jax.ShapeDtypeStruct((s, D), f32), jax.ShapeDtypeStruct((s, KVW), f32),
                   jax.ShapeDtypeStruct((s, KVW), f32), jax.ShapeDtypeStruct((NKV, GROUP * BLK), f32)],
        compiler_params=_params(("arbitrary",), 48),
    )(_sink_rows(sinks), q, do, k, k, v, v)


def _block_diag(w):
    w4 = w.reshape(NGRP, 4, HD, HD)
    eye = jnp.eye(4, dtype=w.dtype)
    return jnp.einsum('gjcd,jk->gjckd', w4, eye).reshape(NGRP, 256, 256).astype(bf16)


def _gates(u, wr_ref, wi_ref, br, bi, sp):
    ub = u.astype(bf16)
    pr = jnp.concatenate([_dot(ub[:, 256 * g:256 * g + 256], wr_ref[g]) for g in range(NGRP)], axis=1)
    pi = jnp.concatenate([_dot(ub[:, 256 * g:256 * g + 256], wi_ref[g]) for g in range(NGRP)], axis=1)
    r = _sigmoid(pr + br)
    i = _sigmoid(pi + bi)
    la = -LRU_C * r * sp
    a = jnp.exp(la)
    mult = jnp.sqrt(_neg_expm1(2.0 * la))
    return ub, r, i, a, mult


def _lru_fwd(xl, conv_w, conv_b, wr, wi, br, bi, lam, tm):
    s = xl.shape[0]

    def body(xp_ref, x_ref, cw_ref, cb_ref, wr_ref, wi_ref, br_ref, bi_ref, lam_ref, u_ref, h_ref,
             pad, a_scr, b_scr, hcar):
        t0 = pl.program_id(0)

        @pl.when(t0 == 0)
        def _():
            hcar[...] = jnp.zeros_like(hcar)

        pad[0:8, :] = jnp.where(t0 > 0, xp_ref[...], 0.0)
        pad[8:tm + 8, :] = x_ref[...]
        u = cb_ref[...] + sum(cw_ref[k:k + 1, :] * pad[5 + k:5 + k + tm, :] for k in range(CONVW))
        u_ref[...] = u
        sp = _softplus(-lam_ref[...])
        _, _, i, a, mult = _gates(u, wr_ref, wi_ref, br_ref[...], bi_ref[...], sp)
        a_scr[...] = a
        b_scr[...] = mult * (i * u)

        def step(t, hc):
            hn = a_scr[pl.ds(t, 1), :] * hc + b_scr[pl.ds(t, 1), :]
            h_ref[pl.ds(t, 1), :] = hn
            return hn

        hcar[...] = lax.fori_loop(0, tm, step, hcar[...], unroll=8)

    row = pl.BlockSpec((tm, LW), lambda i: (i, 0))
    prev8 = pl.BlockSpec((8, LW), lambda i: (jnp.maximum(i * (tm // 8) - 1, 0), 0))
    full = lambda a: pl.BlockSpec(a.shape, lambda i: (0,) * a.ndim)
    return pl.pallas_call(
        body, name="lru_fwd", grid=(s // tm,),
        in_specs=[prev8, row, full(conv_w), full(conv_b), full(wr), full(wi), full(br), full(bi), full(lam)],
        out_specs=[row, row],
        out_shape=[jax.ShapeDtypeStruct((s, LW), f32), jax.ShapeDtypeStruct((s, LW), f32)],
        scratch_shapes=[pltpu.VMEM((tm + 8, LW), f32), pltpu.VMEM((tm, LW), f32), pltpu.VMEM((tm, LW), f32),
                        pltpu.VMEM((1, LW), f32)],
        compiler_params=_params(("arbitrary",), 48),
    )(xl, xl, conv_w, conv_b, wr, wi, br, bi, lam)


def _lru_bwd(u, hl, dhl, xl, conv_w, wr, wi, br, bi, lam, tm):
    s = u.shape[0]
    nt = s // tm

    def body(u_ref, h_ref, hp_ref, dh_ref, x_ref, xp_ref, cw_ref, wr_ref, wi_ref, br_ref, bi_ref, lam_ref,
             dxl_ref, dwr_ref, dwi_ref, dbr_ref, dbi_ref, dlam_ref, dcb_ref, dcw_ref,
             pad, a_scr, l_scr, lcar, dunext):
        t0 = pl.program_id(0)
        tile = nt - 1 - t0

        @pl.when(t0 == 0)
        def _():
            lcar[...] = jnp.zeros_like(lcar)
            dunext[...] = jnp.zeros_like(dunext)
            for ref in (dwr_ref, dwi_ref, dbr_ref, dbi_ref, dlam_ref, dcb_ref, dcw_ref):
                ref[...] = jnp.zeros_like(ref)

        u = u_ref[...]
        lam = lam_ref[...]
        sp = _softplus(-lam)
        ub, r, i, a, mult = _gates(u, wr_ref, wi_ref, br_ref[...], bi_ref[...], sp)
        a_scr[...] = a

        def step(k, c):
            t = tm - 1 - k
            lt = dh_ref[pl.ds(t, 1), :] + c
            l_scr[pl.ds(t, 1), :] = lt
            return a_scr[pl.ds(t, 1), :] * lt

        lcar[...] = lax.fori_loop(0, tm, step, lcar[...], unroll=8)
        lt = l_scr[...]

        pad[0:8, :] = jnp.where(tile > 0, hp_ref[...], 0.0)
        pad[8:tm + 8, :] = h_ref[...]
        hprev = pad[7:tm + 7, :]
        da = lt * hprev
        dmult = lt * (i * u)
        di = lt * mult * u
        du = lt * mult * i
        dla = da * a - dmult * (a * a) / mult
        dr = dla * (-LRU_C * sp)
        dlam_ref[...] += jnp.sum(dla * (-LRU_C * r), axis=0, keepdims=True)
        dpr = dr * r * (1.0 - r)
        dpi = di * i * (1.0 - i)
        dbr_ref[...] += jnp.sum(dpr, axis=0, keepdims=True)
        dbi_ref[...] += jnp.sum(dpi, axis=0, keepdims=True)
        dprb, dpib = dpr.astype(bf16), dpi.astype(bf16)
        dug = []
        for g in range(NGRP):
            gs = slice(256 * g, 256 * g + 256)
            dwr_ref[g] += _dot_tn(ub[:, gs], dprb[:, gs])
            dwi_ref[g] += _dot_tn(ub[:, gs], dpib[:, gs])
            dug.append(_dot_nt(dprb[:, gs], wr_ref[g]) + _dot_nt(dpib[:, gs], wi_ref[g]))
        du = du + jnp.concatenate(dug, axis=1)

        dcb_ref[...] += jnp.sum(du, axis=0, keepdims=True)
        pad[0:8, :] = jnp.where(tile > 0, xp_ref[...], 0.0)
        pad[8:tm + 8, :] = x_ref[...]
        for k in range(CONVW):
            dcw_ref[k:k + 1, :] += jnp.sum(du * pad[5 + k:5 + k + tm, :], axis=0, keepdims=True)
        pad[0:tm, :] = du
        pad[tm:tm + 8, :] = dunext[...]
        dxl = sum(cw_ref[k:k + 1, :] * pad[3 - k:3 - k + tm, :] for k in range(CONVW))
        dxl_ref[...] = dxl.astype(bf16)
        dunext[...] = du[0:8, :]

        @pl.when(t0 == nt - 1)
        def _():
            dlam_ref[...] = dlam_ref[...] * (-_sigmoid(-lam))

    rev = lambda i: (nt - 1 - i, 0)
    row = pl.BlockSpec((tm, LW), rev)
    prev8 = pl.BlockSpec((8, LW), lambda i: (jnp.maximum((nt - 1 - i) * (tm // 8) - 1, 0), 0))
    full = lambda a: pl.BlockSpec(a.shape, lambda i: (0,) * a.ndim)
    vec = pl.BlockSpec((1, LW), lambda i: (0, 0))
    bd = pl.BlockSpec((NGRP, 256, 256), lambda i: (0, 0, 0))
    return pl.pallas_call(
        body, name="lru_bwd", grid=(nt,),
        in_specs=[row, row, prev8, row, row, prev8, full(conv_w), full(wr), full(wi), full(br), full(bi), full(lam)],
        out_specs=[row, bd, bd, vec, vec, vec, vec, pl.BlockSpec((CONVW, LW), lambda i: (0, 0))],
        out_shape=[jax.ShapeDtypeStruct((s, LW), bf16),
                   jax.ShapeDtypeStruct((NGRP, 256, 256), f32), jax.ShapeDtypeStruct((NGRP, 256, 256), f32),
                   jax.ShapeDtypeStruct((1, LW), f32), jax.ShapeDtypeStruct((1, LW), f32),
                   jax.ShapeDtypeStruct((1, LW), f32), jax.ShapeDtypeStruct((1, LW), f32),
                   jax.ShapeDtypeStruct((CONVW, LW), f32)],
        scratch_shapes=[pltpu.VMEM((tm + 8, LW), f32), pltpu.VMEM((tm, LW), f32), pltpu.VMEM((tm, LW), f32),
                        pltpu.VMEM((1, LW), f32), pltpu.VMEM((8, LW), f32)],
        compiler_params=_params(("arbitrary",), 56),
    )(u, hl, hl, dhl, xl, xl, conv_w, wr, wi, br, bi, lam)


def _gated_norm(t, gate, gain):
    sg = _sigmoid(gate)
    silu = gate * sg
    p = t * silu
    rstd = lax.rsqrt(jnp.mean(p * p, axis=-1, keepdims=True) + EPS)
    ph = p * rstd
    return sg, silu, rstd, ph, ph * gain


def _gated_norm_bwd(dy, t, gate, gain, sg, silu, rstd, ph):
    w = dy * gain
    dp = rstd * (w - ph * jnp.mean(w * ph, axis=-1, keepdims=True))
    dgate = dp * t * (sg * (1.0 + gate * (1.0 - sg)))
    return jnp.sum(dy * ph, axis=0, keepdims=True), dp * silu, dgate


def _out_fwd_bwd(x, tgt, o, ga, hl, gl, again, lgain, fgain, wo, tm):
    s = x.shape[0]
    nt = s // tm

    def body(x_ref, t_ref, o_ref, ga_ref, hl_ref, gl_ref, ag_ref, lg_ref, fg_ref, wo_ref,
             dx2_ref, do_ref, dga_ref, dhl_ref, dgl_ref, dwo_ref, gfg_ref, gag_ref, glg_ref, loss_ref, acc):
        i = pl.program_id(0)

        @pl.when(i == 0)
        def _():
            acc[...] = jnp.zeros_like(acc)
            for ref in (gfg_ref, gag_ref, glg_ref, loss_ref):
                ref[...] = jnp.zeros_like(ref)

        oo, gga, hh, ggl = o_ref[...], ga_ref[...], hl_ref[...], gl_ref[...]
        ag, lg, fg = ag_ref[...], lg_ref[...], fg_ref[...]
        sga, silua, ra, pah, ya = _gated_norm(oo, gga, ag)
        sgl, silul, rl, plh, yl = _gated_norm(hh, ggl, lg)
        yab, ylb = ya.astype(bf16), yl.astype(bf16)
        y = _dot(yab, wo_ref[0:D, :]) + _dot(ylb, wo_ref[D:2 * D, :])
        x2 = x_ref[...] + y
        r2 = lax.rsqrt(jnp.mean(x2 * x2, axis=-1, keepdims=True) + EPS)
        x2h = x2 * r2
        err = x2h * fg - t_ref[...]
        loss_ref[...] += 0.5 * jnp.sum(jnp.sum(err * err, axis=-1, keepdims=True) * (1.0 / D))
        dout = err * (1.0 / D)
        gfg_ref[...] += jnp.sum(dout * x2h, axis=0, keepdims=True)
        w = dout * fg
        dx2 = r2 * (w - x2h * jnp.mean(w * x2h, axis=-1, keepdims=True))
        dx2_ref[...] = dx2
        dyb = dx2.astype(bf16)
        acc[0:D, :] += _dot_tn(yab, dyb)
        acc[D:2 * D, :] += _dot_tn(ylb, dyb)
        dya = _dot_nt(dyb, wo_ref[0:D, :])
        dyl = _dot_nt(dyb, wo_ref[D:2 * D, :])
        gag, do, dga = _gated_norm_bwd(dya, oo, gga, ag, sga, silua, ra, pah)
        glg, dhl, dgl = _gated_norm_bwd(dyl, hh, ggl, lg, sgl, silul, rl, plh)
        gag_ref[...] += gag
        glg_ref[...] += glg
        do_ref[...] = do.astype(bf16)
        dga_ref[...] = dga.astype(bf16)
        dhl_ref[...] = dhl
        dgl_ref[...] = dgl.astype(bf16)

        @pl.when(i == nt - 1)
        def _():
            dwo_ref[...] = acc[...].astype(bf16)

    row = pl.BlockSpec((tm, D), lambda i: (i, 0))
    vec = pl.BlockSpec((1, D), lambda i: (0, 0))
    mat = pl.BlockSpec((2 * D, D), lambda i: (0, 0))
    return pl.pallas_call(
        body, name="out_fwd_bwd", grid=(nt,),
        in_specs=[row] * 6 + [vec] * 3 + [mat],
        out_specs=[row] * 5 + [mat, vec, vec, vec, pl.BlockSpec((1, 128), lambda i: (0, 0))],
        out_shape=[jax.ShapeDtypeStruct((s, D), f32), jax.ShapeDtypeStruct((s, D), bf16),
                   jax.ShapeDtypeStruct((s, D), bf16), jax.ShapeDtypeStruct((s, D), f32),
                   jax.ShapeDtypeStruct((s, D), bf16), jax.ShapeDtypeStruct((2 * D, D), bf16),
                   jax.ShapeDtypeStruct((1, D), f32), jax.ShapeDtypeStruct((1, D), f32),
                   jax.ShapeDtypeStruct((1, D), f32), jax.ShapeDtypeStruct((1, 128), f32)],
        scratch_shapes=[pltpu.VMEM((2 * D, D), f32)],
        compiler_params=_params(("arbitrary",), 56),
    )(x, tgt, o, ga, hl, gl, again, lgain, fgain, wo)


def _bwd_in(x, dx2, dq, dk, dv, dga, dxl, dgl, ln_gain, wt, tabs, tm):
    s = x.shape[0]

    def body(x_ref, dx2_ref, dq_ref, dk_ref, dv_ref, dga_ref, dxl_ref, dgl_ref, g_ref, wt_ref,
             c_ref, sa_ref, sb_ref, gx_ref, gln_ref, dz_ref, dz_scr):
        @pl.when(pl.program_id(0) == 0)
        def _():
            gln_ref[...] = jnp.zeros_like(gln_ref)

        c, sa, sb = c_ref[...], sa_ref[...], sb_ref[...]
        for j in range(D // 128):
            js = slice(128 * j, 128 * j + 128)
            dz_scr[:, js] = (_unrope(dq_ref[:, js], c, sa, sb) * (HD ** -0.5)).astype(bf16)
        for j in range(KVW // 128):
            js = slice(128 * j, 128 * j + 128)
            dz_scr[:, D + 128 * j:D + 128 * j + 128] = _unrope(dk_ref[:, js], c, sa, sb).astype(bf16)
        dz_scr[:, D + KVW:D + 2 * KVW] = dv_ref[...].astype(bf16)
        dz_scr[:, 1536:2560] = dga_ref[...]
        dz_scr[:, 2560:3584] = dxl_ref[...]
        dz_scr[:, 3584:4608] = dgl_ref[...]
        for p in range(NDEV):
            dz_ref[p] = dz_scr[:, WT_ROWS * p:WT_ROWS * (p + 1)]
        dh = _dot(dz_scr[:, 0:512], wt_ref[0:512, :])
        for ci in range(1, NIN // 512):
            dh = dh + _dot(dz_scr[:, 512 * ci:512 * ci + 512], wt_ref[512 * ci:512 * ci + 512, :])
        xx = x_ref[...]
        rstd = lax.rsqrt(jnp.mean(xx * xx, axis=-1, keepdims=True) + EPS)
        xh = xx * rstd
        gln_ref[...] += jnp.sum(dh * xh, axis=0, keepdims=True)
        w = dh * g_ref[...]
        gx_ref[...] = dx2_ref[...] + rstd * (w - xh * jnp.mean(w * xh, axis=-1, keepdims=True))

    row = lambda w: pl.BlockSpec((tm, w), lambda i: (i, 0))
    full = lambda a: pl.BlockSpec(a.shape, lambda i: (0, 0))
    return pl.pallas_call(
        body, name="bwd_in", grid=(s // tm,),
        in_specs=[row(D), row(D), row(D), row(KVW), row(KVW), row(D), row(D), row(D), full(ln_gain), full(wt),
                  row(128), row(128), row(128)],
        out_specs=[row(D), pl.BlockSpec((1, D), lambda i: (0, 0)),
                   pl.BlockSpec((NDEV, tm, WT_ROWS), lambda i: (0, i, 0))],
        out_shape=[jax.ShapeDtypeStruct((s, D), f32), jax.ShapeDtypeStruct((1, D), f32),
                   jax.ShapeDtypeStruct((NDEV, s, WT_ROWS), bf16)],
        scratch_shapes=[pltpu.VMEM((tm, NIN), bf16)],
        compiler_params=_params(("arbitrary",), 56),
    )(x, dx2, dq, dk, dv, dga, dxl, dgl, ln_gain, wt, *tabs)


def _dwt(dzs, h, tm):
    s = h.shape[0]
    nk = s // tm

    def body(dz_ref, h_ref, o_ref, acc):
        k = pl.program_id(1)

        @pl.when(k == 0)
        def _():
            acc[...] = jnp.zeros_like(acc)

        acc[...] += _dot_tn(dz_ref[...], h_ref[...])

        @pl.when(k == nk - 1)
        def _():
            o_ref[...] = acc[...].astype(bf16)

    return pl.pallas_call(
        body, name="dwt", grid=(NDEV, nk),
        in_specs=[pl.BlockSpec((None, tm, WT_ROWS), lambda p, k: (p, k, 0)), pl.BlockSpec((tm, D), lambda p, k: (k, 0))],
        out_specs=pl.BlockSpec((WT_ROWS, D), lambda p, k: (p, 0)),
        out_shape=jax.ShapeDtypeStruct((NIN, D), bf16),
        scratch_shapes=[pltpu.VMEM((WT_ROWS, D), f32)],
        compiler_params=_params(("arbitrary", "arbitrary"), 32),
    )(dzs, h)


def _diag_blocks(bd):
    eye = jnp.eye(4, dtype=bd.dtype)
    return jnp.einsum('gjckd,jk->gjcd', bd.reshape(NGRP, 4, HD, 4, HD), eye).reshape(NQ, HD, HD)


def _local_step(x, tgt, wt, wo, conv_w, p):
    s = x.shape[0]
    tm = min(256, s)
    tabs = _rope_tables(s)
    wr, wi = _block_diag(p["w_rgate"]), _block_diag(p["w_igate"])
    sinks = p["sinks"].reshape(NQ)
    h, q, k, v, ga, xl, gl = _fwd_in(x, p["ln_gain"], wt, tabs, tm)
    o = _attn_fwd(q, k, v, sinks)
    u, hl = _lru_fwd(xl, conv_w, p["conv_b"], wr, wi, p["b_rgate"], p["b_igate"], p["lru_lambda"], tm)
    dx2, do, dga, dhl, dgl, dwo, g_fg, g_ag, g_lg, loss = _out_fwd_bwd(
        x, tgt, o, ga, hl, gl, p["attn_out_gain"], p["lru_out_gain"], p["final_gain"], wo, tm)
    dq, dk, dv, dsink = _attn_bwd(q, k, v, do, sinks)
    dxl, dwr, dwi, dbr, dbi, dlam, dcb, dcw = _lru_bwd(
        u, hl, dhl, xl, conv_w, wr, wi, p["b_rgate"], p["b_igate"], p["lru_lambda"], tm)
    gx, g_ln, dzs = _bwd_in(x, dx2, dq, dk, dv, dga, dxl, dgl, p["ln_gain"], wt, tabs, tm)
    dwt = _dwt(dzs, h, min(512, s))
    small = dict(ln_gain=g_ln, sinks=dsink.reshape(NQ, BLK).sum(axis=1)[None], conv_w=dcw, conv_b=dcb, w_rgate=_diag_blocks(dwr),
                 b_rgate=dbr, w_igate=_diag_blocks(dwi), b_igate=dbi, lru_lambda=dlam, attn_out_gain=g_ag,
                 lru_out_gain=g_lg, final_gain=g_fg)
    return loss, gx, dwt, dwo, small


def _place():
    return lax.axis_index("x"), lax.axis_index("y"), lax.axis_index("c")


def _all_gather(srcs, out_dtypes, name):
    n = len(srcs)
    cast = [a.dtype != dt for a, dt in zip(srcs, out_dtypes)]

    def body(*refs):
        src_refs, out_refs = refs[:n], refs[n:2 * n]
        stage_refs = list(refs[2 * n:2 * n + sum(cast)])
        send_sems, recv_sems, local_sems = refs[-3:]
        x, y, c = _place()
        me, sibling = (x, y, c), (x, y, 1 - c)
        chips = [(1 - x, y), (x, 1 - y), (1 - x, 1 - y)]
        mine_refs = []
        for a in range(n):
            if cast[a]:
                st = stage_refs.pop(0)
                st[...] = src_refs[a][...].astype(out_dtypes[a])
                mine_refs.append(st)
            else:
                mine_refs.append(src_refs[a])

        def rows(a, dev):
            m = srcs[a].shape[0]
            return out_refs[a].at[pl.ds((4 * dev[0] + 2 * dev[1] + dev[2]) * m, m), :]

        def copy(a, k, block, to, own=False):
            return pltpu.make_async_remote_copy(
                src_ref=mine_refs[a] if own else rows(a, block), dst_ref=rows(a, block),
                send_sem=send_sems.at[a, k], recv_sem=recv_sems.at[a, k], device_id=to, device_id_type=MESH)

        local = [pltpu.make_async_copy(mine_refs[a], rows(a, me), local_sems.at[a]) for a in range(n)]
        for cp in local:
            cp.start()
        first = []
        for a in range(n):
            first.append(copy(a, 0, me, sibling, own=True))
            first += [copy(a, 1 + j, me, (*chip, c), own=True) for j, chip in enumerate(chips)]
        for cp in first:
            cp.start()
        passed = []
        for j, chip in enumerate(chips):
            for a in range(n):
                copy(a, 1 + j, (*chip, c), me).wait_recv()
                fwd = copy(a, 4 + j, (*chip, c), sibling)
                fwd.start()
                passed.append(fwd)
        for a in range(n):
            copy(a, 0, sibling, me).wait_recv()
            for j, chip in enumerate(chips):
                copy(a, 4 + j, (*chip, 1 - c), me).wait_recv()
        for cp in first + passed:
            cp.wait_send()
        for cp in local:
            cp.wait()

    vmem = pl.BlockSpec(memory_space=pltpu.VMEM)
    hbm = pl.BlockSpec(memory_space=pl.ANY)
    return pl.pallas_call(
        body, name=name,
        in_specs=[vmem] * n, out_specs=[hbm] * n,
        out_shape=[jax.ShapeDtypeStruct((NDEV * a.shape[0], a.shape[1]), dt) for a, dt in zip(srcs, out_dtypes)],
        scratch_shapes=[pltpu.VMEM(a.shape, dt) for a, dt, cst in zip(srcs, out_dtypes, cast) if cst]
        + [pltpu.SemaphoreType.DMA((n, 7)), pltpu.SemaphoreType.DMA((n, 7)), pltpu.SemaphoreType.DMA((n,))],
        compiler_params=pltpu.CompilerParams(vmem_limit_bytes=32 * MIB),
    )(*srcs)


def _scatter_grads(parts):
    n = len(parts)

    def body(*refs):
        src_refs, land_refs = refs[:n], refs[n:2 * n]
        send_sems, recv_sems, local_sems = refs[-3:]
        x, y, c = _place()
        my = 4 * x + 2 * y + c

        def piece(a, dev):
            m = parts[a].shape[0] // NDEV
            return src_refs[a].at[pl.ds(dev * m, m), :]

        local = [pltpu.make_async_copy(piece(a, my), land_refs[a].at[my], local_sems.at[a]) for a in range(n)]
        for cp in local:
            cp.start()
        copies = []
        for k in range(1, NDEV):
            px, py, pc = x ^ (k >> 2), y ^ ((k >> 1) & 1), c ^ (k & 1)
            for a in range(n):
                copies.append(pltpu.make_async_remote_copy(
                    src_ref=piece(a, 4 * px + 2 * py + pc), dst_ref=land_refs[a].at[my],
                    send_sem=send_sems.at[a, k - 1], recv_sem=recv_sems.at[a, k - 1],
                    device_id=(px, py, pc), device_id_type=MESH))
        for cp in copies:
            cp.start()
        for cp in copies:
            cp.wait_send()
        for k in range(1, NDEV):
            px, py, pc = x ^ (k >> 2), y ^ ((k >> 1) & 1), c ^ (k & 1)
            for a in range(n):
                pltpu.make_async_remote_copy(
                    src_ref=piece(a, my), dst_ref=land_refs[a].at[4 * px + 2 * py + pc],
                    send_sem=send_sems.at[a, k - 1], recv_sem=recv_sems.at[a, k - 1],
                    device_id=(px, py, pc), device_id_type=MESH).wait_recv()
        for cp in local:
            cp.wait()

    hbm = pl.BlockSpec(memory_space=pl.ANY)
    return pl.pallas_call(
        body, name="scatter_grads",
        in_specs=[hbm] * n, out_specs=[hbm] * n,
        out_shape=[jax.ShapeDtypeStruct((NDEV, a.shape[0] // NDEV, a.shape[1]), a.dtype) for a in parts],
        scratch_shapes=[pltpu.SemaphoreType.DMA((n, 7)), pltpu.SemaphoreType.DMA((n, 7)),
                        pltpu.SemaphoreType.DMA((n,))],
    )(*parts)


def _sum_slots(land, tr, name):
    _, rows, cols = land.shape

    def body(l_ref, o_ref):
        acc = l_ref[0].astype(f32)
        for d in range(1, NDEV):
            acc = acc + l_ref[d].astype(f32)
        o_ref[...] = acc

    return pl.pallas_call(
        body, name=name, grid=(rows // tr,),
        in_specs=[pl.BlockSpec((NDEV, tr, cols), lambda i: (0, i, 0))],
        out_specs=pl.BlockSpec((tr, cols), lambda i: (i, 0)),
        out_shape=jax.ShapeDtypeStruct((rows, cols), f32),
        compiler_params=_params(("arbitrary",), 32),
    )(land)


def _adam_math(w, g, m, v):
    m2 = ADAM_B1 * m + (1.0 - ADAM_B1) * g
    v2 = ADAM_B2 * v + (1.0 - ADAM_B2) * (g * g)
    m_hat = m2 / (1.0 - ADAM_B1 ** ADAM_STEP)
    v_hat = v2 / (1.0 - ADAM_B2 ** ADAM_STEP)
    delta = -ADAM_LR * (m_hat / (jnp.sqrt(v_hat) + ADAM_EPS) + ADAM_WD * w)
    return delta, m2, v2


def _adamw(w, g, m, v, tr, name):
    rows, cols = w.shape

    def body(w_ref, g_ref, m_ref, v_ref, d_ref, m2_ref, v2_ref):
        d_ref[...], m2_ref[...], v2_ref[...] = _adam_math(w_ref[...], g_ref[...], m_ref[...], v_ref[...])

    blk = pl.BlockSpec((tr, cols), lambda i: (i, 0))
    return pl.pallas_call(
        body, name=name, grid=(rows // tr,),
        in_specs=[blk] * 4, out_specs=[blk] * 3,
        out_shape=[jax.ShapeDtypeStruct((rows, cols), f32)] * 3,
        compiler_params=_params(("arbitrary",), 32),
    )(w, g, m, v)


VEC_NAMES = ("ln_gain", "conv_b", "b_rgate", "b_igate", "lru_lambda", "attn_out_gain", "lru_out_gain", "final_gain")
ROW_RGATE, ROW_IGATE, ROW_VEC, ROW_SINKS = 0, 64, 128, 136
LOSS_LANE = NQ


def _adamw_small(g_rep, g_conv, w, m, v):
    names = list(VEC_NAMES) + ["sinks", "conv_w", "w_rgate", "w_igate"]
    ins = [g_rep, g_conv] + [d[k] for k in names for d in (w, m, v)]

    def body(*refs):
        g_ref, gc_ref = refs[0], refs[1]
        in_refs = refs[2:2 + 3 * len(names)]
        out_refs = refs[2 + 3 * len(names):]

        def update(j, g, at=None):
            w_ref, m_ref, v_ref = in_refs[3 * j:3 * j + 3]
            outs = out_refs[4 * j:4 * j + 4]
            pick = (lambda r: r[...]) if at is None else (lambda r: r[at])
            res = (g,) + _adam_math(pick(w_ref), g, pick(m_ref), pick(v_ref))
            for o_ref, val in zip(outs, res):
                if at is None:
                    o_ref[...] = val
                else:
                    o_ref[at] = val

        for j in range(len(VEC_NAMES)):
            update(j, g_ref[ROW_VEC + j:ROW_VEC + j + 1, :])
        update(len(VEC_NAMES), g_ref[ROW_SINKS:ROW_SINKS + 1, 0:NQ])
        update(len(VEC_NAMES) + 1, gc_ref[...], at=0)
        for gi, row0 in ((len(VEC_NAMES) + 2, ROW_RGATE), (len(VEC_NAMES) + 3, ROW_IGATE)):
            for nb in range(NQ):
                update(gi, g_ref[row0:row0 + HD, HD * nb:HD * nb + HD], at=(0, nb))

    vmem = pl.BlockSpec(memory_space=pltpu.VMEM)
    out_shape = [jax.ShapeDtypeStruct(w[k].shape, f32) for k in names for _ in range(4)]
    outs = pl.pallas_call(
        body, name="adamw_small",
        in_specs=[vmem] * len(ins), out_specs=[vmem] * len(out_shape), out_shape=out_shape,
        compiler_params=pltpu.CompilerParams(vmem_limit_bytes=32 * MIB),
    )(*ins)
    return {k: tuple(outs[4 * j:4 * j + 4]) for j, k in enumerate(names)}


def _pack_small(small, loss):
    gate = lambda g: g.transpose(1, 0, 2).reshape(HD, NQ * HD)
    row_s = jnp.concatenate([small["sinks"], loss[:, LOSS_LANE:128], jnp.zeros((1, D - 128), f32)], axis=1)
    rep = jnp.concatenate([gate(small["w_rgate"]), gate(small["w_igate"])] + [small[k] for k in VEC_NAMES]
                          + [row_s, jnp.zeros((SMALL_ROWS - ROW_SINKS - 1, D), f32)], axis=0)
    conv = small["conv_w"].reshape(CONVW, NDEV, 128).transpose(1, 0, 2)
    conv = jnp.pad(conv, ((0, 0), (0, 8 - CONVW), (0, D - 128)))
    return jnp.concatenate([rep.reshape(NDEV, SMALL_PER, D), conv], axis=1).reshape(NDEV * (SMALL_PER + 8), D)


def kernel(x, ln_gain, w_in, sinks, conv_w, conv_b, w_rgate, b_rgate, w_igate, b_igate, lru_lambda, attn_out_gain, lru_out_gain, w_out, final_gain, loss_target, m_ln_gain, m_w_in, m_sinks, m_conv_w, m_conv_b, m_w_rgate, m_b_rgate, m_w_igate, m_b_igate, m_lru_lambda, m_attn_out_gain, m_lru_out_gain, m_w_out, m_final_gain, v_ln_gain, v_w_in, v_sinks, v_conv_w, v_conv_b, v_w_rgate, v_b_rgate, v_w_igate, v_b_igate, v_lru_lambda, v_attn_out_gain, v_lru_out_gain, v_w_out, v_final_gain):
    w = dict(ln_gain=ln_gain, sinks=sinks, conv_w=conv_w, conv_b=conv_b, w_rgate=w_rgate, b_rgate=b_rgate,
             w_igate=w_igate, b_igate=b_igate, lru_lambda=lru_lambda, attn_out_gain=attn_out_gain,
             lru_out_gain=lru_out_gain, final_gain=final_gain.reshape(1, D))
    m = dict(ln_gain=m_ln_gain, sinks=m_sinks, conv_w=m_conv_w, conv_b=m_conv_b, w_rgate=m_w_rgate,
             b_rgate=m_b_rgate, w_igate=m_w_igate, b_igate=m_b_igate, lru_lambda=m_lru_lambda,
             attn_out_gain=m_attn_out_gain, lru_out_gain=m_lru_out_gain, final_gain=m_final_gain.reshape(1, D))
    v = dict(ln_gain=v_ln_gain, sinks=v_sinks, conv_w=v_conv_w, conv_b=v_conv_b, w_rgate=v_w_rgate,
             b_rgate=v_b_rgate, w_igate=v_w_igate, b_igate=v_b_igate, lru_lambda=v_lru_lambda,
             attn_out_gain=v_attn_out_gain, lru_out_gain=v_lru_out_gain, final_gain=v_final_gain.reshape(1, D))

    conv_blk = jnp.pad(conv_w[0], ((0, 8 - CONVW), (0, 0)))
    wt, wo, cw_all = _all_gather([w_in[0].T, w_out[0], conv_blk], [bf16, bf16, f32], "gather_weights")
    conv_full = cw_all.reshape(NDEV, 8, 128)[:, 0:CONVW].transpose(1, 0, 2).reshape(CONVW, LW)

    p = {k: (w[k][0] if k in ("w_rgate", "w_igate") else w[k]) for k in w if k != "conv_w"}
    loss, gx, dwt, dwo, small = _local_step(x[0], loss_target[0], wt, wo, conv_full, p)

    land_wt, land_wo, land_sm = _scatter_grads([dwt, dwo, _pack_small(small, loss)])
    g_wt = _sum_slots(land_wt, 192, "sum_wt")
    g_wo = _sum_slots(land_wo, 256, "sum_wo")
    g_sm = _sum_slots(land_sm, SMALL_PER + 8, "sum_small")
    (g_rep,) = _all_gather([g_sm[0:SMALL_PER]], [f32], "gather_small")
    g_conv = g_sm[SMALL_PER:SMALL_PER + CONVW, 0:128]

    g_win = g_wt.T
    d_win, m_win, v_win = _adamw(w_in[0], g_win, m_w_in[0], v_w_in[0], 256, "adamw_w_in")
    d_wo, m_wo, v_wo = _adamw(w_out[0], g_wo, m_w_out[0], v_w_out[0], 256, "adamw_w_out")
    res = _adamw_small(g_rep, g_conv, w, m, v)
    res["w_in"] = tuple(t[None] for t in (g_win, d_win, m_win, v_win))
    res["w_out"] = tuple(t[None] for t in (g_wo, d_wo, m_wo, v_wo))
    res["final_gain"] = tuple(t.reshape(D) for t in res["final_gain"])

    order = ("ln_gain", "w_in", "sinks", "conv_w", "conv_b", "w_rgate", "b_rgate", "w_igate", "b_igate",
             "lru_lambda", "attn_out_gain", "lru_out_gain", "w_out", "final_gain")
    total_loss = g_rep[ROW_SINKS, LOSS_LANE]
    return (total_loss, gx[None]) + tuple(res[k][i] for i in range(4) for k in order)
```

```python
import jax
import jax.numpy as jnp
from jax import lax
from jax.experimental import pallas as pl
from jax.experimental.pallas import tpu as pltpu

f32 = jnp.float32
bf16 = jnp.bfloat16

D = 1024
HD = 64
NQ = 16
NKV = 4
GROUP = NQ // NKV
KVW = NKV * HD
BLK = 128
ROT = 16
THETA = 500000.0
NEG = -1e30
LW = 1024
NGRP = 4
CONVW = 4
LRU_C = 8.0
NIN = 4608
EPS = 1e-6
NDEV = 8
WT_ROWS = NIN // NDEV
WO_ROWS = 2 * D // NDEV
SMALL_ROWS = 192
SMALL_PER = SMALL_ROWS // NDEV

ADAM_LR = 0.001
ADAM_B1 = 0.9
ADAM_B2 = 0.999
ADAM_EPS = 1e-08
ADAM_WD = 0.01
ADAM_STEP = 10

NT = (((1,), (1,)), ((), ()))
TN = (((0,), (0,)), ((), ()))
MESH = pl.DeviceIdType.MESH
MIB = 1024 * 1024


def _dot(a, b):
    return jnp.dot(a, b, preferred_element_type=f32)


def _dot_nt(a, b):
    return lax.dot_general(a, b, NT, preferred_element_type=f32)


def _dot_tn(a, b):
    return lax.dot_general(a, b, TN, preferred_element_type=f32)


def _params(sem, vmem_mib):
    return pltpu.CompilerParams(dimension_semantics=sem, vmem_limit_bytes=vmem_mib * MIB)


def _sigmoid(x):
    return 1.0 / (1.0 + jnp.exp(-x))


def _softplus(x):
    return jnp.maximum(x, 0.0) + jnp.log(1.0 + jnp.exp(-jnp.abs(x)))


def _neg_expm1(x):
    series = -x * (1.0 + x * (0.5 + x * (1.0 / 6.0 + x * (1.0 / 24.0 + x * (1.0 / 120.0)))))
    return jnp.where(x > -0.1, series, 1.0 - jnp.exp(x))


def _rope_tables(s):
    pos = jnp.arange(s, dtype=f32)
    inv_freq = THETA ** (-jnp.arange(0, ROT, 2, dtype=f32) / ROT)
    d = jnp.arange(128) % HD
    ang = pos[:, None] * inv_freq[d % (ROT // 2)][None, :]
    cos, sin = jnp.cos(ang), jnp.sin(ang)
    c = jnp.where(d < ROT, cos, 1.0)
    sa = jnp.where((d >= ROT // 2) & (d < ROT), sin, 0.0)
    sb = jnp.where(d < ROT // 2, -sin, 0.0)
    return c, sa, sb


def _rope(t, c, sa, sb):
    return t * c + pltpu.roll(t, 8, 1) * sa + pltpu.roll(t, 120, 1) * sb


def _unrope(dr, c, sa, sb):
    return dr * c + pltpu.roll(dr * sa, 120, 1) + pltpu.roll(dr * sb, 8, 1)


def _place():
    return lax.axis_index("x"), lax.axis_index("y"), lax.axis_index("c")


def _gather_ops(mine_refs, out_refs, send_sems, recv_sems, local_sems):
    n = len(mine_refs)
    x, y, c = _place()
    me, sibling = (x, y, c), (x, y, 1 - c)
    chips = [(1 - x, y), (x, 1 - y), (1 - x, 1 - y)]

    def rows(a, dev):
        m = mine_refs[a].shape[0]
        return out_refs[a].at[pl.ds((4 * dev[0] + 2 * dev[1] + dev[2]) * m, m), :]

    def copy(a, k, block, to, own=False):
        return pltpu.make_async_remote_copy(
            src_ref=mine_refs[a] if own else rows(a, block), dst_ref=rows(a, block),
            send_sem=send_sems.at[a, k], recv_sem=recv_sems.at[a, k], device_id=to, device_id_type=MESH)

    def local(a):
        return pltpu.make_async_copy(mine_refs[a], rows(a, me), local_sems.at[a])

    def first(a):
        return [copy(a, 0, me, sibling, own=True)] + [copy(a, 1 + j, me, (*chip, c), own=True)
                                                      for j, chip in enumerate(chips)]

    def start():
        for a in range(n):
            local(a).start()
            for cp in first(a):
                cp.start()

    def finish():
        for j, chip in enumerate(chips):
            for a in range(n):
                copy(a, 1 + j, (*chip, c), me).wait_recv()
                copy(a, 4 + j, (*chip, c), sibling).start()
        for a in range(n):
            copy(a, 0, sibling, me).wait_recv()
            for j, chip in enumerate(chips):
                copy(a, 4 + j, (*chip, 1 - c), me).wait_recv()
        for a in range(n):
            for cp in first(a) + [copy(a, 4 + j, (*chip, c), sibling) for j, chip in enumerate(chips)]:
                cp.wait_send()
            local(a).wait()

    return start, finish


def _scatter_ops(src_refs, land_refs, send_sems, recv_sems, local_sems):
    n = len(src_refs)
    x, y, c = _place()
    my = 4 * x + 2 * y + c

    def peer(k):
        return x ^ (k >> 2), y ^ ((k >> 1) & 1), c ^ (k & 1)

    def piece(a, dev):
        m = src_refs[a].shape[0] // NDEV
        return src_refs[a].at[pl.ds(dev * m, m), :]

    def local(a):
        return pltpu.make_async_copy(piece(a, my), land_refs[a].at[my], local_sems.at[a])

    def send(a, k):
        px, py, pc = peer(k)
        return pltpu.make_async_remote_copy(
            src_ref=piece(a, 4 * px + 2 * py + pc), dst_ref=land_refs[a].at[my],
            send_sem=send_sems.at[a, k - 1], recv_sem=recv_sems.at[a, k - 1],
            device_id=(px, py, pc), device_id_type=MESH)

    def arrival(a, k):
        px, py, pc = peer(k)
        return pltpu.make_async_remote_copy(
            src_ref=piece(a, my), dst_ref=land_refs[a].at[4 * px + 2 * py + pc],
            send_sem=send_sems.at[a, k - 1], recv_sem=recv_sems.at[a, k - 1],
            device_id=(px, py, pc), device_id_type=MESH)

    def start():
        for a in range(n):
            local(a).start()
        for k in range(1, NDEV):
            for a in range(n):
                send(a, k).start()

    def finish():
        for k in range(1, NDEV):
            for a in range(n):
                send(a, k).wait_send()
        for k in range(1, NDEV):
            for a in range(n):
                arrival(a, k).wait_recv()
        for a in range(n):
            local(a).wait()

    return start, finish


def _comm_sems(n):
    return [pltpu.SemaphoreType.DMA((n, 7)), pltpu.SemaphoreType.DMA((n, 7)), pltpu.SemaphoreType.DMA((n,))]


HBM = pl.BlockSpec(memory_space=pl.ANY)


def _fwd_in(x, ln_gain, wt, tabs, wo_shard, tm):
    s = x.shape[0]
    nt = s // tm
    nc = 512

    def body(x_ref, g_ref, wt_ref, c_ref, sa_ref, sb_ref, wo_ref, h_ref, q_ref, k_ref, v_ref, ga_ref, xl_ref, gl_ref,
             wo_all, wo_stage, send_sems, recv_sems, local_sems):
        i = pl.program_id(0)
        start, finish = _gather_ops([wo_stage], [wo_all], send_sems, recv_sems, local_sems)

        @pl.when(i == 0)
        def _():
            wo_stage[...] = wo_ref[...].astype(bf16)
            start()

        xx = x_ref[...]
        rstd = lax.rsqrt(jnp.mean(xx * xx, axis=-1, keepdims=True) + EPS)
        h = (xx * rstd * g_ref[...]).astype(bf16)
        h_ref[...] = h
        c, sa, sb = c_ref[...], sa_ref[...], sb_ref[...]

        def z_chunk(ci):
            return _dot_nt(h, wt_ref[ci * nc:(ci + 1) * nc, :])

        for ci in range(2):
            z = z_chunk(ci)
            for j in range(nc // 128):
                r = _rope(z[:, 128 * j:128 * j + 128], c, sa, sb) * (HD ** -0.5)
                q_ref[:, ci * nc + 128 * j:ci * nc + 128 * j + 128] = r.astype(bf16)
        z = z_chunk(2)
        for j in range(2):
            k_ref[:, 128 * j:128 * j + 128] = _rope(z[:, 128 * j:128 * j + 128], c, sa, sb).astype(bf16)
        v_ref[...] = z[:, 256:512].astype(bf16)
        for sec, ref in enumerate((ga_ref, xl_ref, gl_ref)):
            for j in range(2):
                ref[:, j * nc:(j + 1) * nc] = z_chunk(3 + 2 * sec + j)

        @pl.when(i == nt - 1)
        def _():
            finish()

    row = lambda w: pl.BlockSpec((tm, w), lambda i: (i, 0))
    full = lambda a: pl.BlockSpec(a.shape, lambda i: (0, 0))
    return pl.pallas_call(
        body, name="fwd_in", grid=(nt,),
        in_specs=[row(D), full(ln_gain), full(wt), row(128), row(128), row(128), full(wo_shard)],
        out_specs=[row(D), row(D), row(KVW), row(KVW), row(D), row(D), row(D), HBM],
        out_shape=[jax.ShapeDtypeStruct((s, D), bf16), jax.ShapeDtypeStruct((s, D), bf16),
                   jax.ShapeDtypeStruct((s, KVW), bf16), jax.ShapeDtypeStruct((s, KVW), bf16),
                   jax.ShapeDtypeStruct((s, D), f32), jax.ShapeDtypeStruct((s, D), f32),
                   jax.ShapeDtypeStruct((s, D), f32), jax.ShapeDtypeStruct((2 * D, D), bf16)],
        scratch_shapes=[pltpu.VMEM((WO_ROWS, D), bf16)] + _comm_sems(1),
        compiler_params=_params(("arbitrary",), 48),
    )(x, ln_gain, wt, *tabs, wo_shard)


HSUB = 4
SUBW = HSUB * BLK


def _sub_probs(kh, qg, n, sink_row):
    jj = lax.broadcasted_iota(jnp.int32, (BLK, SUBW), 0)
    ii = lax.broadcasted_iota(jnp.int32, (BLK, SUBW), 1) % BLK
    from_prev = jj > ii
    s2 = _dot_nt(kh, qg)
    sc = jnp.where(from_prev, s2[0:BLK] + jnp.where(n > 0, 0.0, NEG), s2[BLK:2 * BLK])
    m = jnp.maximum(jnp.max(sc, axis=0, keepdims=True), sink_row)
    p = jnp.exp(sc - m)
    es = jnp.exp(sink_row - m)
    inv = 1.0 / (jnp.sum(p, axis=0, keepdims=True) + es)
    return from_prev, p * inv, es * inv


def _split(t, from_prev):
    t = t.astype(bf16)
    zero = jnp.zeros_like(t)
    return jnp.concatenate([jnp.where(from_prev, t, zero), jnp.where(from_prev, zero, t)], axis=0)


def _stack_heads(ref, first):
    return jnp.concatenate([ref[:, HD * (first + g):HD * (first + g) + HD] for g in range(HSUB)], axis=0)


def _sink_rows(sinks):
    return jnp.repeat(sinks.reshape(NKV, GROUP), BLK, axis=1)


def _kv_specs():
    prev = pl.BlockSpec((BLK, KVW), lambda n: (jnp.maximum(n - 1, 0), 0))
    cur = pl.BlockSpec((BLK, KVW), lambda n: (n, 0))
    return [prev, cur, prev, cur]


def _attn_fwd(q, k, v, sinks):
    s = q.shape[0]

    def body(sink_ref, q_ref, kp_ref, kc_ref, vp_ref, vc_ref, o_ref):
        n = pl.program_id(0)
        for h in range(NKV):
            hs = slice(HD * h, HD * h + HD)
            kh = jnp.concatenate([kp_ref[:, hs], kc_ref[:, hs]], axis=0)
            vh = jnp.concatenate([vp_ref[:, hs], vc_ref[:, hs]], axis=0)
            for t in range(GROUP // HSUB):
                first = GROUP * h + HSUB * t
                from_prev, pn, _ = _sub_probs(kh, _stack_heads(q_ref, first), n,
                                              sink_ref[h:h + 1, SUBW * t:SUBW * t + SUBW])
                og = _dot_tn(_split(pn, from_prev), vh)
                for g in range(HSUB):
                    o_ref[:, HD * (first + g):HD * (first + g) + HD] = og[BLK * g:BLK * g + BLK]

    return pl.pallas_call(
        body, name="attn_fwd", grid=(s // BLK,),
        in_specs=[pl.BlockSpec((NKV, GROUP * BLK), lambda n: (0, 0)), pl.BlockSpec((BLK, D), lambda n: (n, 0))]
        + _kv_specs(),
        out_specs=pl.BlockSpec((BLK, D), lambda n: (n, 0)),
        out_shape=jax.ShapeDtypeStruct((s, D), f32),
        compiler_params=_params(("arbitrary",), 32),
    )(_sink_rows(sinks), q, k, k, v, v)


def _attn_bwd(q, k, v, do, sinks, dwo):
    s = q.shape[0]
    nb = s // BLK

    def body(sink_ref, q_ref, do_ref, kp_ref, kc_ref, vp_ref, vc_ref, dwo_ref, dq_ref, dk_ref, dv_ref, ds_ref,
             land_ref, send_sems, recv_sems, local_sems):
        n = pl.program_id(0)
        start, finish = _scatter_ops([dwo_ref], [land_ref], send_sems, recv_sems, local_sems)

        @pl.when(n == 0)
        def _():
            start()
            dk_ref[...] = jnp.zeros_like(dk_ref)
            dv_ref[...] = jnp.zeros_like(dv_ref)
            ds_ref[...] = jnp.zeros_like(ds_ref)

        prev_rows = pl.ds(pl.multiple_of(jnp.maximum(n - 1, 0) * BLK, BLK), BLK)
        cur_rows = pl.ds(pl.multiple_of(n * BLK, BLK), BLK)
        for h in range(NKV):
            hs = slice(HD * h, HD * h + HD)
            kh = jnp.concatenate([kp_ref[:, hs], kc_ref[:, hs]], axis=0)
            vh = jnp.concatenate([vp_ref[:, hs], vc_ref[:, hs]], axis=0)
            dkh = jnp.zeros((2 * BLK, HD), f32)
            dvh = jnp.zeros((2 * BLK, HD), f32)
            for t in range(GROUP // HSUB):
                first = GROUP * h + HSUB * t
                lanes = slice(SUBW * t, SUBW * t + SUBW)
                qg, dog = _stack_heads(q_ref, first), _stack_heads(do_ref, first)
                from_prev, pn, ps = _sub_probs(kh, qg, n, sink_ref[h:h + 1, lanes])
                dp2 = _dot_nt(vh, dog)
                dp = jnp.where(from_prev, dp2[0:BLK], dp2[BLK:2 * BLK])
                dsum = jnp.sum(pn * dp, axis=0, keepdims=True)
                ds_ref[h:h + 1, lanes] += -ps * dsum
                ds2 = _split(pn * (dp - dsum), from_prev)
                dqg = _dot_tn(ds2, kh)
                for g in range(HSUB):
                    dq_ref[:, HD * (first + g):HD * (first + g) + HD] = dqg[BLK * g:BLK * g + BLK]
                dkh = dkh + _dot(ds2, qg)
                dvh = dvh + _dot(_split(pn, from_prev), dog)
            dk_ref[prev_rows, hs] += dkh[0:BLK]
            dk_ref[cur_rows, hs] += dkh[BLK:2 * BLK]
            dv_ref[prev_rows, hs] += dvh[0:BLK]
            dv_ref[cur_rows, hs] += dvh[BLK:2 * BLK]

        @pl.when(n == nb - 1)
        def _():
            finish()

    blk = pl.BlockSpec((BLK, D), lambda n: (n, 0))
    whole = lambda r, w: pl.BlockSpec((r, w), lambda n: (0, 0))
    return pl.pallas_call(
        body, name="attn_bwd", grid=(nb,),
        in_specs=[whole(NKV, GROUP * BLK), blk, blk] + _kv_specs() + [HBM],
        out_specs=[blk, whole(s, KVW), whole(s, KVW), whole(NKV, GROUP * BLK), HBM],
        out_shape=[jax.ShapeDtypeStruct((s, D), f32), jax.ShapeDtypeStruct((s, KVW), f32),
                   jax.ShapeDtypeStruct((s, KVW), f32), jax.ShapeDtypeStruct((NKV, GROUP * BLK), f32),
                   jax.ShapeDtypeStruct((NDEV, WO_ROWS, D), bf16)],
        scratch_shapes=_comm_sems(1),
        compiler_params=_params(("arbitrary",), 48),
    )(_sink_rows(sinks), q, do, k, k, v, v, dwo)


def _block_diag(w):
    w4 = w.reshape(NGRP, 4, HD, HD)
    eye = jnp.eye(4, dtype=w.dtype)
    return jnp.einsum('gjcd,jk->gjckd', w4, eye).reshape(NGRP, 256, 256).astype(bf16)


def _gates(u, wr_ref, wi_ref, br, bi, sp):
    ub = u.astype(bf16)
    pr = jnp.concatenate([_dot(ub[:, 256 * g:256 * g + 256], wr_ref[g]) for g in range(NGRP)], axis=1)
    pi = jnp.concatenate([_dot(ub[:, 256 * g:256 * g + 256], wi_ref[g]) for g in range(NGRP)], axis=1)
    r = _sigmoid(pr + br)
    i = _sigmoid(pi + bi)
    la = -LRU_C * r * sp
    a = jnp.exp(la)
    mult = jnp.sqrt(_neg_expm1(2.0 * la))
    return ub, r, i, a, mult


def _lru_fwd(xl, conv_w, conv_b, wr, wi, br, bi, lam, tm):
    s = xl.shape[0]

    def body(xp_ref, x_ref, cw_ref, cb_ref, wr_ref, wi_ref, br_ref, bi_ref, lam_ref, u_ref, h_ref,
             pad, a_scr, b_scr, hcar):
        t0 = pl.program_id(0)

        @pl.when(t0 == 0)
        def _():
            hcar[...] = jnp.zeros_like(hcar)

        pad[0:8, :] = jnp.where(t0 > 0, xp_ref[...], 0.0)
        pad[8:tm + 8, :] = x_ref[...]
        u = cb_ref[...] + sum(cw_ref[k:k + 1, :] * pad[5 + k:5 + k + tm, :] for k in range(CONVW))
        u_ref[...] = u
        sp = _softplus(-lam_ref[...])
        _, _, i, a, mult = _gates(u, wr_ref, wi_ref, br_ref[...], bi_ref[...], sp)
        a_scr[...] = a
        b_scr[...] = mult * (i * u)

        def step(t, hc):
            hn = a_scr[pl.ds(t, 1), :] * hc + b_scr[pl.ds(t, 1), :]
            h_ref[pl.ds(t, 1), :] = hn
            return hn

        hcar[...] = lax.fori_loop(0, tm, step, hcar[...], unroll=8)

    row = pl.BlockSpec((tm, LW), lambda i: (i, 0))
    prev8 = pl.BlockSpec((8, LW), lambda i: (jnp.maximum(i * (tm // 8) - 1, 0), 0))
    full = lambda a: pl.BlockSpec(a.shape, lambda i: (0,) * a.ndim)
    return pl.pallas_call(
        body, name="lru_fwd", grid=(s // tm,),
        in_specs=[prev8, row, full(conv_w), full(conv_b), full(wr), full(wi), full(br), full(bi), full(lam)],
        out_specs=[row, row],
        out_shape=[jax.ShapeDtypeStruct((s, LW), f32), jax.ShapeDtypeStruct((s, LW), f32)],
        scratch_shapes=[pltpu.VMEM((tm + 8, LW), f32), pltpu.VMEM((tm, LW), f32), pltpu.VMEM((tm, LW), f32),
                        pltpu.VMEM((1, LW), f32)],
        compiler_params=_params(("arbitrary",), 48),
    )(xl, xl, conv_w, conv_b, wr, wi, br, bi, lam)


def _lru_bwd(u, hl, dhl, xl, conv_w, wr, wi, br, bi, lam, tm):
    s = u.shape[0]
    nt = s // tm

    def body(u_ref, h_ref, hp_ref, dh_ref, x_ref, xp_ref, cw_ref, wr_ref, wi_ref, br_ref, bi_ref, lam_ref,
             dxl_ref, dwr_ref, dwi_ref, dbr_ref, dbi_ref, dlam_ref, dcb_ref, dcw_ref,
             pad, a_scr, l_scr, lcar, dunext):
        t0 = pl.program_id(0)
        tile = nt - 1 - t0

        @pl.when(t0 == 0)
        def _():
            lcar[...] = jnp.zeros_like(lcar)
            dunext[...] = jnp.zeros_like(dunext)
            for ref in (dwr_ref, dwi_ref, dbr_ref, dbi_ref, dlam_ref, dcb_ref, dcw_ref):
                ref[...] = jnp.zeros_like(ref)

        u = u_ref[...]
        lam = lam_ref[...]
        sp = _softplus(-lam)
        ub, r, i, a, mult = _gates(u, wr_ref, wi_ref, br_ref[...], bi_ref[...], sp)
        a_scr[...] = a

        def step(k, c):
            t = tm - 1 - k
            lt = dh_ref[pl.ds(t, 1), :] + c
            l_scr[pl.ds(t, 1), :] = lt
            return a_scr[pl.ds(t, 1), :] * lt

        lcar[...] = lax.fori_loop(0, tm, step, lcar[...], unroll=8)
        lt = l_scr[...]

        pad[0:8, :] = jnp.where(tile > 0, hp_ref[...], 0.0)
        pad[8:tm + 8, :] = h_ref[...]
        hprev = pad[7:tm + 7, :]
        da = lt * hprev
        dmult = lt * (i * u)
        di = lt * mult * u
        du = lt * mult * i
        dla = da * a - dmult * (a * a) / mult
        dr = dla * (-LRU_C * sp)
        dlam_ref[...] += jnp.sum(dla * (-LRU_C * r), axis=0, keepdims=True)
        dpr = dr * r * (1.0 - r)
        dpi = di * i * (1.0 - i)
        dbr_ref[...] += jnp.sum(dpr, axis=0, keepdims=True)
        dbi_ref[...] += jnp.sum(dpi, axis=0, keepdims=True)
        dprb, dpib = dpr.astype(bf16), dpi.astype(bf16)
        dug = []
        for g in range(NGRP):
            gs = slice(256 * g, 256 * g + 256)
            dwr_ref[g] += _dot_tn(ub[:, gs], dprb[:, gs])
            dwi_ref[g] += _dot_tn(ub[:, gs], dpib[:, gs])
            dug.append(_dot_nt(dprb[:, gs], wr_ref[g]) + _dot_nt(dpib[:, gs], wi_ref[g]))
        du = du + jnp.concatenate(dug, axis=1)

        dcb_ref[...] += jnp.sum(du, axis=0, keepdims=True)
        pad[0:8, :] = jnp.where(tile > 0, xp_ref[...], 0.0)
        pad[8:tm + 8, :] = x_ref[...]
        for k in range(CONVW):
            dcw_ref[k:k + 1, :] += jnp.sum(du * pad[5 + k:5 + k + tm, :], axis=0, keepdims=True)
        pad[0:tm, :] = du
        pad[tm:tm + 8, :] = dunext[...]
        dxl = sum(cw_ref[k:k + 1, :] * pad[3 - k:3 - k + tm, :] for k in range(CONVW))
        dxl_ref[...] = dxl.astype(bf16)
        dunext[...] = du[0:8, :]

        @pl.when(t0 == nt - 1)
        def _():
            dlam_ref[...] = dlam_ref[...] * (-_sigmoid(-lam))

    rev = lambda i: (nt - 1 - i, 0)
    row = pl.BlockSpec((tm, LW), rev)
    prev8 = pl.BlockSpec((8, LW), lambda i: (jnp.maximum((nt - 1 - i) * (tm // 8) - 1, 0), 0))
    full = lambda a: pl.BlockSpec(a.shape, lambda i: (0,) * a.ndim)
    vec = pl.BlockSpec((1, LW), lambda i: (0, 0))
    bd = pl.BlockSpec((NGRP, 256, 256), lambda i: (0, 0, 0))
    return pl.pallas_call(
        body, name="lru_bwd", grid=(nt,),
        in_specs=[row, row, prev8, row, row, prev8, full(conv_w), full(wr), full(wi), full(br), full(bi), full(lam)],
        out_specs=[row, bd, bd, vec, vec, vec, vec, pl.BlockSpec((CONVW, LW), lambda i: (0, 0))],
        out_shape=[jax.ShapeDtypeStruct((s, LW), bf16),
                   jax.ShapeDtypeStruct((NGRP, 256, 256), f32), jax.ShapeDtypeStruct((NGRP, 256, 256), f32),
                   jax.ShapeDtypeStruct((1, LW), f32), jax.ShapeDtypeStruct((1, LW), f32),
                   jax.ShapeDtypeStruct((1, LW), f32), jax.ShapeDtypeStruct((1, LW), f32),
                   jax.ShapeDtypeStruct((CONVW, LW), f32)],
        scratch_shapes=[pltpu.VMEM((tm + 8, LW), f32), pltpu.VMEM((tm, LW), f32), pltpu.VMEM((tm, LW), f32),
                        pltpu.VMEM((1, LW), f32), pltpu.VMEM((8, LW), f32)],
        compiler_params=_params(("arbitrary",), 56),
    )(u, hl, hl, dhl, xl, xl, conv_w, wr, wi, br, bi, lam)


def _gated_norm(t, gate, gain):
    sg = _sigmoid(gate)
    silu = gate * sg
    p = t * silu
    rstd = lax.rsqrt(jnp.mean(p * p, axis=-1, keepdims=True) + EPS)
    ph = p * rstd
    return sg, silu, rstd, ph, ph * gain


def _gated_norm_bwd(dy, t, gate, gain, sg, silu, rstd, ph):
    w = dy * gain
    dp = rstd * (w - ph * jnp.mean(w * ph, axis=-1, keepdims=True))
    dgate = dp * t * (sg * (1.0 + gate * (1.0 - sg)))
    return jnp.sum(dy * ph, axis=0, keepdims=True), dp * silu, dgate


def _out_fwd_bwd(x, tgt, o, ga, hl, gl, again, lgain, fgain, wo, tm):
    s = x.shape[0]
    nt = s // tm

    def body(x_ref, t_ref, o_ref, ga_ref, hl_ref, gl_ref, ag_ref, lg_ref, fg_ref, wo_ref,
             dx2_ref, do_ref, dga_ref, dhl_ref, dgl_ref, dwo_ref, gfg_ref, gag_ref, glg_ref, loss_ref, acc):
        i = pl.program_id(0)

        @pl.when(i == 0)
        def _():
            acc[...] = jnp.zeros_like(acc)
            for ref in (gfg_ref, gag_ref, glg_ref, loss_ref):
                ref[...] = jnp.zeros_like(ref)

        oo, gga, hh, ggl = o_ref[...], ga_ref[...], hl_ref[...], gl_ref[...]
        ag, lg, fg = ag_ref[...], lg_ref[...], fg_ref[...]
        sga, silua, ra, pah, ya = _gated_norm(oo, gga, ag)
        sgl, silul, rl, plh, yl = _gated_norm(hh, ggl, lg)
        yab, ylb = ya.astype(bf16), yl.astype(bf16)
        y = _dot(yab, wo_ref[0:D, :]) + _dot(ylb, wo_ref[D:2 * D, :])
        x2 = x_ref[...] + y
        r2 = lax.rsqrt(jnp.mean(x2 * x2, axis=-1, keepdims=True) + EPS)
        x2h = x2 * r2
        err = x2h * fg - t_ref[...]
        loss_ref[...] += 0.5 * jnp.sum(jnp.sum(err * err, axis=-1, keepdims=True) * (1.0 / D))
        dout = err * (1.0 / D)
        gfg_ref[...] += jnp.sum(dout * x2h, axis=0, keepdims=True)
        w = dout * fg
        dx2 = r2 * (w - x2h * jnp.mean(w * x2h, axis=-1, keepdims=True))
        dx2_ref[...] = dx2
        dyb = dx2.astype(bf16)
        acc[0:D, :] += _dot_tn(yab, dyb)
        acc[D:2 * D, :] += _dot_tn(ylb, dyb)
        dya = _dot_nt(dyb, wo_ref[0:D, :])
        dyl = _dot_nt(dyb, wo_ref[D:2 * D, :])
        gag, do, dga = _gated_norm_bwd(dya, oo, gga, ag, sga, silua, ra, pah)
        glg, dhl, dgl = _gated_norm_bwd(dyl, hh, ggl, lg, sgl, silul, rl, plh)
        gag_ref[...] += gag
        glg_ref[...] += glg
        do_ref[...] = do.astype(bf16)
        dga_ref[...] = dga.astype(bf16)
        dhl_ref[...] = dhl
        dgl_ref[...] = dgl.astype(bf16)

        @pl.when(i == nt - 1)
        def _():
            dwo_ref[...] = acc[...].astype(bf16)

    row = pl.BlockSpec((tm, D), lambda i: (i, 0))
    vec = pl.BlockSpec((1, D), lambda i: (0, 0))
    mat = pl.BlockSpec((2 * D, D), lambda i: (0, 0))
    return pl.pallas_call(
        body, name="out_fwd_bwd", grid=(nt,),
        in_specs=[row] * 6 + [vec] * 3 + [mat],
        out_specs=[row] * 5 + [mat, vec, vec, vec, pl.BlockSpec((1, 128), lambda i: (0, 0))],
        out_shape=[jax.ShapeDtypeStruct((s, D), f32), jax.ShapeDtypeStruct((s, D), bf16),
                   jax.ShapeDtypeStruct((s, D), bf16), jax.ShapeDtypeStruct((s, D), f32),
                   jax.ShapeDtypeStruct((s, D), bf16), jax.ShapeDtypeStruct((2 * D, D), bf16),
                   jax.ShapeDtypeStruct((1, D), f32), jax.ShapeDtypeStruct((1, D), f32),
                   jax.ShapeDtypeStruct((1, D), f32), jax.ShapeDtypeStruct((1, 128), f32)],
        scratch_shapes=[pltpu.VMEM((2 * D, D), f32)],
        compiler_params=_params(("arbitrary",), 56),
    )(x, tgt, o, ga, hl, gl, again, lgain, fgain, wo)


def _bwd_in(x, dx2, dq, dk, dv, dga, dxl, dgl, ln_gain, wt, tabs, tm):
    s = x.shape[0]

    def body(x_ref, dx2_ref, dq_ref, dk_ref, dv_ref, dga_ref, dxl_ref, dgl_ref, g_ref, wt_ref,
             c_ref, sa_ref, sb_ref, gx_ref, gln_ref, dz_ref, dz_scr):
        @pl.when(pl.program_id(0) == 0)
        def _():
            gln_ref[...] = jnp.zeros_like(gln_ref)

        c, sa, sb = c_ref[...], sa_ref[...], sb_ref[...]
        for j in range(D // 128):
            js = slice(128 * j, 128 * j + 128)
            dz_scr[:, js] = (_unrope(dq_ref[:, js], c, sa, sb) * (HD ** -0.5)).astype(bf16)
        for j in range(KVW // 128):
            js = slice(128 * j, 128 * j + 128)
            dz_scr[:, D + 128 * j:D + 128 * j + 128] = _unrope(dk_ref[:, js], c, sa, sb).astype(bf16)
        dz_scr[:, D + KVW:D + 2 * KVW] = dv_ref[...].astype(bf16)
        dz_scr[:, 1536:2560] = dga_ref[...]
        dz_scr[:, 2560:3584] = dxl_ref[...]
        dz_scr[:, 3584:4608] = dgl_ref[...]
        for p in range(NDEV):
            dz_ref[p] = dz_scr[:, WT_ROWS * p:WT_ROWS * (p + 1)]
        dh = _dot(dz_scr[:, 0:512], wt_ref[0:512, :])
        for ci in range(1, NIN // 512):
            dh = dh + _dot(dz_scr[:, 512 * ci:512 * ci + 512], wt_ref[512 * ci:512 * ci + 512, :])
        xx = x_ref[...]
        rstd = lax.rsqrt(jnp.mean(xx * xx, axis=-1, keepdims=True) + EPS)
        xh = xx * rstd
        gln_ref[...] += jnp.sum(dh * xh, axis=0, keepdims=True)
        w = dh * g_ref[...]
        gx_ref[...] = dx2_ref[...] + rstd * (w - xh * jnp.mean(w * xh, axis=-1, keepdims=True))

    row = lambda w: pl.BlockSpec((tm, w), lambda i: (i, 0))
    full = lambda a: pl.BlockSpec(a.shape, lambda i: (0, 0))
    return pl.pallas_call(
        body, name="bwd_in", grid=(s // tm,),
        in_specs=[row(D), row(D), row(D), row(KVW), row(KVW), row(D), row(D), row(D), full(ln_gain), full(wt),
                  row(128), row(128), row(128)],
        out_specs=[row(D), pl.BlockSpec((1, D), lambda i: (0, 0)),
                   pl.BlockSpec((NDEV, tm, WT_ROWS), lambda i: (0, i, 0))],
        out_shape=[jax.ShapeDtypeStruct((s, D), f32), jax.ShapeDtypeStruct((1, D), f32),
                   jax.ShapeDtypeStruct((NDEV, s, WT_ROWS), bf16)],
        scratch_shapes=[pltpu.VMEM((tm, NIN), bf16)],
        compiler_params=_params(("arbitrary",), 56),
    )(x, dx2, dq, dk, dv, dga, dxl, dgl, ln_gain, wt, *tabs)


def _dwt_scatter(dzs, h, small, tm):
    s = h.shape[0]
    nk = s // tm
    srows = small.shape[0] // NDEV

    def body(order_ref, dz_ref, h_ref, sm_ref, lwt_ref, lsm_ref, acc, stage, send_sems, recv_sems, local_sem,
             sm_send, sm_recv, sm_local):
        j, k = pl.program_id(0), pl.program_id(1)
        x, y, c = _place()
        my = 4 * x + 2 * y + c
        sm_start, sm_finish = _scatter_ops([sm_ref], [lsm_ref], sm_send, sm_recv, sm_local)

        def owner(jj):
            p = (my + 1 + jj) % NDEV
            return p >> 2, (p >> 1) & 1, p & 1

        def send(jj):
            return pltpu.make_async_remote_copy(
                src_ref=stage.at[jj % 2], dst_ref=lwt_ref.at[my], send_sem=send_sems.at[jj],
                recv_sem=recv_sems.at[jj], device_id=owner(jj), device_id_type=MESH)

        def arrival(jj):
            return pltpu.make_async_remote_copy(
                src_ref=stage.at[0], dst_ref=lwt_ref.at[(my + 2 * NDEV - 1 - jj) % NDEV], send_sem=send_sems.at[jj],
                recv_sem=recv_sems.at[jj], device_id=owner(jj), device_id_type=MESH)

        def keep():
            return pltpu.make_async_copy(stage.at[(NDEV - 1) % 2], lwt_ref.at[my], local_sem)

        @pl.when((j == 0) & (k == 0))
        def _():
            sm_start()

        @pl.when(k == 0)
        def _():
            acc[...] = jnp.zeros_like(acc)

        acc[...] += _dot_tn(dz_ref[...], h_ref[...])

        @pl.when(k == nk - 1)
        def _():
            @pl.when(j >= 2)
            def _():
                send(j - 2).wait_send()

            stage[j % 2] = acc[...].astype(bf16)

            @pl.when(j < NDEV - 1)
            def _():
                send(j).start()

            @pl.when(j == NDEV - 1)
            def _():
                keep().start()
                send(NDEV - 2).wait_send()
                for jj in range(NDEV - 1):
                    arrival(jj).wait_recv()
                keep().wait()
                sm_finish()

    return pl.pallas_call(
        body, name="dwt_scatter",
        grid_spec=pltpu.PrefetchScalarGridSpec(
            num_scalar_prefetch=1, grid=(NDEV, nk),
            in_specs=[pl.BlockSpec((None, tm, WT_ROWS), lambda j, k, order: (order[j], k, 0)),
                      pl.BlockSpec((tm, D), lambda j, k, order: (k, 0)), HBM],
            out_specs=[HBM, HBM],
            scratch_shapes=[pltpu.VMEM((WT_ROWS, D), f32), pltpu.VMEM((2, WT_ROWS, D), bf16),
                            pltpu.SemaphoreType.DMA((NDEV - 1,)), pltpu.SemaphoreType.DMA((NDEV - 1,)),
                            pltpu.SemaphoreType.DMA(())] + _comm_sems(1)),
        out_shape=[jax.ShapeDtypeStruct((NDEV, WT_ROWS, D), bf16), jax.ShapeDtypeStruct((NDEV, srows, D), f32)],
        compiler_params=_params(("arbitrary", "arbitrary"), 32),
    )((4 * lax.axis_index("x") + 2 * lax.axis_index("y") + lax.axis_index("c") + 1 + jnp.arange(NDEV)) % NDEV,
      dzs, h, small)


def _diag_blocks(bd):
    eye = jnp.eye(4, dtype=bd.dtype)
    return jnp.einsum('gjckd,jk->gjcd', bd.reshape(NGRP, 4, HD, 4, HD), eye).reshape(NQ, HD, HD)


def _sequence_step(x, tgt, wt, wo_shard, conv_w, p):
    s = x.shape[0]
    tm = min(256, s)
    tabs = _rope_tables(s)
    wr, wi = _block_diag(p["w_rgate"]), _block_diag(p["w_igate"])
    sinks = p["sinks"].reshape(NQ)
    h, q, k, v, ga, xl, gl, wo = _fwd_in(x, p["ln_gain"], wt, tabs, wo_shard, tm)
    o = _attn_fwd(q, k, v, sinks)
    u, hl = _lru_fwd(xl, conv_w, p["conv_b"], wr, wi, p["b_rgate"], p["b_igate"], p["lru_lambda"], tm)
    dx2, do, dga, dhl, dgl, dwo, g_fg, g_ag, g_lg, loss = _out_fwd_bwd(
        x, tgt, o, ga, hl, gl, p["attn_out_gain"], p["lru_out_gain"], p["final_gain"], wo, tm)
    dq, dk, dv, dsink, land_wo = _attn_bwd(q, k, v, do, sinks, dwo)
    dxl, dwr, dwi, dbr, dbi, dlam, dcb, dcw = _lru_bwd(
        u, hl, dhl, xl, conv_w, wr, wi, p["b_rgate"], p["b_igate"], p["lru_lambda"], tm)
    gx, g_ln, dzs = _bwd_in(x, dx2, dq, dk, dv, dga, dxl, dgl, p["ln_gain"], wt, tabs, tm)
    small = dict(ln_gain=g_ln, sinks=dsink.reshape(NQ, BLK).sum(axis=1)[None], conv_w=dcw, conv_b=dcb,
                 w_rgate=_diag_blocks(dwr), b_rgate=dbr, w_igate=_diag_blocks(dwi), b_igate=dbi, lru_lambda=dlam,
                 attn_out_gain=g_ag, lru_out_gain=g_lg, final_gain=g_fg)
    land_wt, land_sm = _dwt_scatter(dzs, h, _pack_small(small, loss), min(512, s))
    return gx, land_wt, land_wo, land_sm


def _all_gather(srcs, out_dtypes, name):
    n = len(srcs)
    cast = [a.dtype != dt for a, dt in zip(srcs, out_dtypes)]

    def body(*refs):
        src_refs, out_refs = refs[:n], refs[n:2 * n]
        stage_refs = list(refs[2 * n:2 * n + sum(cast)])
        mine_refs = []
        for a in range(n):
            if cast[a]:
                st = stage_refs.pop(0)
                st[...] = src_refs[a][...].astype(out_dtypes[a])
                mine_refs.append(st)
            else:
                mine_refs.append(src_refs[a])
        start, finish = _gather_ops(mine_refs, out_refs, *refs[-3:])
        start()
        finish()

    vmem = pl.BlockSpec(memory_space=pltpu.VMEM)
    return pl.pallas_call(
        body, name=name,
        in_specs=[vmem] * n, out_specs=[HBM] * n,
        out_shape=[jax.ShapeDtypeStruct((NDEV * a.shape[0], a.shape[1]), dt) for a, dt in zip(srcs, out_dtypes)],
        scratch_shapes=[pltpu.VMEM(a.shape, dt) for a, dt, cst in zip(srcs, out_dtypes, cast) if cst] + _comm_sems(n),
        compiler_params=pltpu.CompilerParams(vmem_limit_bytes=32 * MIB),
    )(*srcs)


def _sum_slots(land, tr, name):
    _, rows, cols = land.shape

    def body(l_ref, o_ref):
        acc = l_ref[0].astype(f32)
        for d in range(1, NDEV):
            acc = acc + l_ref[d].astype(f32)
        o_ref[...] = acc

    return pl.pallas_call(
        body, name=name, grid=(rows // tr,),
        in_specs=[pl.BlockSpec((NDEV, tr, cols), lambda i: (0, i, 0))],
        out_specs=pl.BlockSpec((tr, cols), lambda i: (i, 0)),
        out_shape=jax.ShapeDtypeStruct((rows, cols), f32),
        compiler_params=_params(("arbitrary",), 32),
    )(land)


def _adam_math(w, g, m, v):
    m2 = ADAM_B1 * m + (1.0 - ADAM_B1) * g
    v2 = ADAM_B2 * v + (1.0 - ADAM_B2) * (g * g)
    m_hat = m2 / (1.0 - ADAM_B1 ** ADAM_STEP)
    v_hat = v2 / (1.0 - ADAM_B2 ** ADAM_STEP)
    delta = -ADAM_LR * (m_hat / (jnp.sqrt(v_hat) + ADAM_EPS) + ADAM_WD * w)
    return delta, m2, v2


def _adamw(w, g, m, v, tr, name):
    rows, cols = w.shape

    def body(w_ref, g_ref, m_ref, v_ref, d_ref, m2_ref, v2_ref):
        d_ref[...], m2_ref[...], v2_ref[...] = _adam_math(w_ref[...], g_ref[...], m_ref[...], v_ref[...])

    blk = pl.BlockSpec((tr, cols), lambda i: (i, 0))
    return pl.pallas_call(
        body, name=name, grid=(rows // tr,),
        in_specs=[blk] * 4, out_specs=[blk] * 3,
        out_shape=[jax.ShapeDtypeStruct((rows, cols), f32)] * 3,
        compiler_params=_params(("arbitrary",), 32),
    )(w, g, m, v)


VEC_NAMES = ("ln_gain", "conv_b", "b_rgate", "b_igate", "lru_lambda", "attn_out_gain", "lru_out_gain", "final_gain")
ROW_RGATE, ROW_IGATE, ROW_VEC, ROW_SINKS = 0, 64, 128, 136
LOSS_LANE = NQ


def _adamw_small(g_rep, g_conv, w, m, v):
    names = list(VEC_NAMES) + ["sinks", "conv_w", "w_rgate", "w_igate"]
    ins = [g_rep, g_conv] + [d[k] for k in names for d in (w, m, v)]

    def body(*refs):
        g_ref, gc_ref = refs[0], refs[1]
        in_refs = refs[2:2 + 3 * len(names)]
        out_refs = refs[2 + 3 * len(names):]

        def update(j, g, at=None):
            w_ref, m_ref, v_ref = in_refs[3 * j:3 * j + 3]
            outs = out_refs[4 * j:4 * j + 4]
            pick = (lambda r: r[...]) if at is None else (lambda r: r[at])
            res = (g,) + _adam_math(pick(w_ref), g, pick(m_ref), pick(v_ref))
            for o_ref, val in zip(outs, res):
                if at is None:
                    o_ref[...] = val
                else:
                    o_ref[at] = val

        for j in range(len(VEC_NAMES)):
            update(j, g_ref[ROW_VEC + j:ROW_VEC + j + 1, :])
        update(len(VEC_NAMES), g_ref[ROW_SINKS:ROW_SINKS + 1, 0:NQ])
        update(len(VEC_NAMES) + 1, gc_ref[...], at=0)
        for gi, row0 in ((len(VEC_NAMES) + 2, ROW_RGATE), (len(VEC_NAMES) + 3, ROW_IGATE)):
            for nb in range(NQ):
                update(gi, g_ref[row0:row0 + HD, HD * nb:HD * nb + HD], at=(0, nb))

    vmem = pl.BlockSpec(memory_space=pltpu.VMEM)
    out_shape = [jax.ShapeDtypeStruct(w[k].shape, f32) for k in names for _ in range(4)]
    outs = pl.pallas_call(
        body, name="adamw_small",
        in_specs=[vmem] * len(ins), out_specs=[vmem] * len(out_shape), out_shape=out_shape,
        compiler_params=pltpu.CompilerParams(vmem_limit_bytes=32 * MIB),
    )(*ins)
    return {k: tuple(outs[4 * j:4 * j + 4]) for j, k in enumerate(names)}


def _pack_small(small, loss):
    gate = lambda g: g.transpose(1, 0, 2).reshape(HD, NQ * HD)
    row_s = jnp.concatenate([small["sinks"], loss[:, LOSS_LANE:128], jnp.zeros((1, D - 128), f32)], axis=1)
    rep = jnp.concatenate([gate(small["w_rgate"]), gate(small["w_igate"])] + [small[k] for k in VEC_NAMES]
                          + [row_s, jnp.zeros((SMALL_ROWS - ROW_SINKS - 1, D), f32)], axis=0)
    conv = small["conv_w"].reshape(CONVW, NDEV, 128).transpose(1, 0, 2)
    conv = jnp.pad(conv, ((0, 0), (0, 8 - CONVW), (0, D - 128)))
    return jnp.concatenate([rep.reshape(NDEV, SMALL_PER, D), conv], axis=1).reshape(NDEV * (SMALL_PER + 8), D)


def kernel(x, ln_gain, w_in, sinks, conv_w, conv_b, w_rgate, b_rgate, w_igate, b_igate, lru_lambda, attn_out_gain, lru_out_gain, w_out, final_gain, loss_target, m_ln_gain, m_w_in, m_sinks, m_conv_w, m_conv_b, m_w_rgate, m_b_rgate, m_w_igate, m_b_igate, m_lru_lambda, m_attn_out_gain, m_lru_out_gain, m_w_out, m_final_gain, v_ln_gain, v_w_in, v_sinks, v_conv_w, v_conv_b, v_w_rgate, v_b_rgate, v_w_igate, v_b_igate, v_lru_lambda, v_attn_out_gain, v_lru_out_gain, v_w_out, v_final_gain):
    w = dict(ln_gain=ln_gain, sinks=sinks, conv_w=conv_w, conv_b=conv_b, w_rgate=w_rgate, b_rgate=b_rgate,
             w_igate=w_igate, b_igate=b_igate, lru_lambda=lru_lambda, attn_out_gain=attn_out_gain,
             lru_out_gain=lru_out_gain, final_gain=final_gain.reshape(1, D))
    m = dict(ln_gain=m_ln_gain, sinks=m_sinks, conv_w=m_conv_w, conv_b=m_conv_b, w_rgate=m_w_rgate,
             b_rgate=m_b_rgate, w_igate=m_w_igate, b_igate=m_b_igate, lru_lambda=m_lru_lambda,
             attn_out_gain=m_attn_out_gain, lru_out_gain=m_lru_out_gain, final_gain=m_final_gain.reshape(1, D))
    v = dict(ln_gain=v_ln_gain, sinks=v_sinks, conv_w=v_conv_w, conv_b=v_conv_b, w_rgate=v_w_rgate,
             b_rgate=v_b_rgate, w_igate=v_w_igate, b_igate=v_b_igate, lru_lambda=v_lru_lambda,
             attn_out_gain=v_attn_out_gain, lru_out_gain=v_lru_out_gain, final_gain=v_final_gain.reshape(1, D))

    conv_blk = jnp.pad(conv_w[0], ((0, 8 - CONVW), (0, 0)))
    wt, cw_all = _all_gather([w_in[0].T, conv_blk], [bf16, f32], "gather_weights")
    conv_full = cw_all.reshape(NDEV, 8, 128)[:, 0:CONVW].transpose(1, 0, 2).reshape(CONVW, LW)

    p = {k: (w[k][0] if k in ("w_rgate", "w_igate") else w[k]) for k in w if k != "conv_w"}
    gx, land_wt, land_wo, land_sm = _sequence_step(x[0], loss_target[0], wt, w_out[0], conv_full, p)

    g_wt = _sum_slots(land_wt, 192, "sum_wt")
    g_wo = _sum_slots(land_wo, 256, "sum_wo")
    g_sm = _sum_slots(land_sm, SMALL_PER + 8, "sum_small")
    (g_rep,) = _all_gather([g_sm[0:SMALL_PER]], [f32], "gather_small")
    g_conv = g_sm[SMALL_PER:SMALL_PER + CONVW, 0:128]

    g_win = g_wt.T
    d_win, m_win, v_win = _adamw(w_in[0], g_win, m_w_in[0], v_w_in[0], 256, "adamw_w_in")
    d_wo, m_wo, v_wo = _adamw(w_out[0], g_wo, m_w_out[0], v_w_out[0], 256, "adamw_w_out")
    res = _adamw_small(g_rep, g_conv, w, m, v)
    res["w_in"] = tuple(t[None] for t in (g_win, d_win, m_win, v_win))
    res["w_out"] = tuple(t[None] for t in (g_wo, d_wo, m_wo, v_wo))
    res["final_gain"] = tuple(t.reshape(D) for t in res["final_gain"])

    order = ("ln_gain", "w_in", "sinks", "conv_w", "conv_b", "w_rgate", "b_rgate", "w_igate", "b_igate",
             "lru_lambda", "attn_out_gain", "lru_out_gain", "w_out", "final_gain")
    total_loss = g_rep[ROW_SINKS, LOSS_LANE]
    return (total_loss, gx[None]) + tuple(res[k][i] for i in range(4) for k in order)
```

```python
import jax
import jax.numpy as jnp
from jax import lax
from jax.experimental import pallas as pl
from jax.experimental.pallas import tpu as pltpu

f32 = jnp.float32
bf16 = jnp.bfloat16

D = 1024
HD = 64
NQ = 16
NKV = 4
GROUP = NQ // NKV
KVW = NKV * HD
BLK = 128
ROT = 16
THETA = 500000.0
NEG = -1e30
LW = 1024
NGRP = 4
CONVW = 4
LRU_C = 8.0
NIN = 4608
EPS = 1e-6
NDEV = 8
WT_ROWS = NIN // NDEV
WO_ROWS = 2 * D // NDEV
SMALL_ROWS = 192
SMALL_PER = SMALL_ROWS // NDEV

ADAM_LR = 0.001
ADAM_B1 = 0.9
ADAM_B2 = 0.999
ADAM_EPS = 1e-08
ADAM_WD = 0.01
ADAM_STEP = 10

NT = (((1,), (1,)), ((), ()))
TN = (((0,), (0,)), ((), ()))
MESH = pl.DeviceIdType.MESH
MIB = 1024 * 1024


def _dot(a, b):
    return jnp.dot(a, b, preferred_element_type=f32)


def _dot_nt(a, b):
    return lax.dot_general(a, b, NT, preferred_element_type=f32)


def _dot_tn(a, b):
    return lax.dot_general(a, b, TN, preferred_element_type=f32)


def _params(sem, vmem_mib):
    return pltpu.CompilerParams(dimension_semantics=sem, vmem_limit_bytes=vmem_mib * MIB)


def _sigmoid(x):
    return 0.5 * jnp.tanh(0.5 * x) + 0.5


def _softplus(x):
    return jnp.maximum(x, 0.0) + jnp.log(1.0 + jnp.exp(-jnp.abs(x)))


def _rope_tables(s):
    pos = jnp.arange(s, dtype=f32)
    inv_freq = THETA ** (-jnp.arange(0, ROT, 2, dtype=f32) / ROT)
    ang = pos[:, None] * inv_freq[None, :]
    cs = jnp.concatenate([jnp.cos(ang) - 1.0, jnp.sin(ang)], axis=1)
    d = jnp.arange(128) % HD
    j = jnp.arange(ROT)[:, None]
    pick_c = ((d < ROT) & (j == d % (ROT // 2))).astype(f32)
    pick_sa = ((d >= ROT // 2) & (d < ROT) & (j == d)).astype(f32)
    pick_sb = -((d < ROT // 2) & (j == d + ROT // 2)).astype(f32)
    spread = lambda pick: jnp.dot(cs, pick, precision=lax.Precision.HIGHEST)
    return 1.0 + spread(pick_c), spread(pick_sa), spread(pick_sb)


def _rope(t, c, sa, sb):
    return t * c + pltpu.roll(t, 8, 1) * sa + pltpu.roll(t, 120, 1) * sb


def _unrope(dr, c, sa, sb):
    return dr * c + pltpu.roll(dr * sa, 120, 1) + pltpu.roll(dr * sb, 8, 1)


def _place():
    return lax.axis_index("x"), lax.axis_index("y"), lax.axis_index("c")


def _gather_ops(mine_refs, out_refs, send_sems, recv_sems, local_sems):
    n = len(mine_refs)
    x, y, c = _place()
    me, sibling = (x, y, c), (x, y, 1 - c)
    chips = [(1 - x, y), (x, 1 - y), (1 - x, 1 - y)]

    def rows(a, dev):
        m = mine_refs[a].shape[0]
        return out_refs[a].at[pl.ds((4 * dev[0] + 2 * dev[1] + dev[2]) * m, m), :]

    def copy(a, k, block, to, own=False):
        return pltpu.make_async_remote_copy(
            src_ref=mine_refs[a] if own else rows(a, block), dst_ref=rows(a, block),
            send_sem=send_sems.at[a, k], recv_sem=recv_sems.at[a, k], device_id=to, device_id_type=MESH)

    def local(a):
        return pltpu.make_async_copy(mine_refs[a], rows(a, me), local_sems.at[a])

    def first(a):
        return [copy(a, 0, me, sibling, own=True)] + [copy(a, 1 + j, me, (*chip, c), own=True)
                                                      for j, chip in enumerate(chips)]

    def start():
        for a in range(n):
            local(a).start()
            for cp in first(a):
                cp.start()

    def finish():
        for j, chip in enumerate(chips):
            for a in range(n):
                copy(a, 1 + j, (*chip, c), me).wait_recv()
                copy(a, 4 + j, (*chip, c), sibling).start()
        for a in range(n):
            copy(a, 0, sibling, me).wait_recv()
            for j, chip in enumerate(chips):
                copy(a, 4 + j, (*chip, 1 - c), me).wait_recv()
        for a in range(n):
            for cp in first(a) + [copy(a, 4 + j, (*chip, c), sibling) for j, chip in enumerate(chips)]:
                cp.wait_send()
            local(a).wait()

    return start, finish


def _scatter_ops(src_refs, land_refs, send_sems, recv_sems, local_sems):
    n = len(src_refs)
    x, y, c = _place()
    my = 4 * x + 2 * y + c

    def peer(k):
        return x ^ (k >> 2), y ^ ((k >> 1) & 1), c ^ (k & 1)

    def piece(a, dev):
        m = src_refs[a].shape[0] // NDEV
        return src_refs[a].at[pl.ds(dev * m, m), :]

    def local(a):
        return pltpu.make_async_copy(piece(a, my), land_refs[a].at[my], local_sems.at[a])

    def send(a, k):
        px, py, pc = peer(k)
        return pltpu.make_async_remote_copy(
            src_ref=piece(a, 4 * px + 2 * py + pc), dst_ref=land_refs[a].at[my],
            send_sem=send_sems.at[a, k - 1], recv_sem=recv_sems.at[a, k - 1],
            device_id=(px, py, pc), device_id_type=MESH)

    def arrival(a, k):
        px, py, pc = peer(k)
        return pltpu.make_async_remote_copy(
            src_ref=piece(a, my), dst_ref=land_refs[a].at[4 * px + 2 * py + pc],
            send_sem=send_sems.at[a, k - 1], recv_sem=recv_sems.at[a, k - 1],
            device_id=(px, py, pc), device_id_type=MESH)

    def start():
        for a in range(n):
            local(a).start()
        for k in range(1, NDEV):
            for a in range(n):
                send(a, k).start()

    def finish():
        for k in range(1, NDEV):
            for a in range(n):
                send(a, k).wait_send()
        for k in range(1, NDEV):
            for a in range(n):
                arrival(a, k).wait_recv()
        for a in range(n):
            local(a).wait()

    return start, finish


def _comm_sems(n):
    return [pltpu.SemaphoreType.DMA((n, 7)), pltpu.SemaphoreType.DMA((n, 7)), pltpu.SemaphoreType.DMA((n,))]


HBM = pl.BlockSpec(memory_space=pl.ANY)


def _fwd_in(x, ln_gain, wt, tabs, wo_shard, tm):
    s = x.shape[0]
    nt = s // tm
    nc = 512

    def body(x_ref, g_ref, wt_ref, c_ref, sa_ref, sb_ref, wo_ref, h_ref, q_ref, k_ref, v_ref, ga_ref, xl_ref, gl_ref,
             wo_all, wo_stage, send_sems, recv_sems, local_sems):
        i = pl.program_id(0)
        start, finish = _gather_ops([wo_stage], [wo_all], send_sems, recv_sems, local_sems)

        @pl.when(i == 0)
        def _():
            wo_stage[...] = wo_ref[...].astype(bf16)
            start()

        xx = x_ref[...]
        rstd = lax.rsqrt(jnp.mean(xx * xx, axis=-1, keepdims=True) + EPS)
        h = (xx * rstd * g_ref[...]).astype(bf16)
        h_ref[...] = h
        c, sa, sb = c_ref[...], sa_ref[...], sb_ref[...]

        def z_chunk(ci):
            return _dot_nt(h, wt_ref[ci * nc:(ci + 1) * nc, :])

        for ci in range(2):
            z = z_chunk(ci)
            for j in range(nc // 128):
                r = _rope(z[:, 128 * j:128 * j + 128], c, sa, sb) * (HD ** -0.5)
                q_ref[:, ci * nc + 128 * j:ci * nc + 128 * j + 128] = r.astype(bf16)
        z = z_chunk(2)
        for j in range(2):
            k_ref[:, 128 * j:128 * j + 128] = _rope(z[:, 128 * j:128 * j + 128], c, sa, sb).astype(bf16)
        v_ref[...] = z[:, 256:512].astype(bf16)
        for sec, ref in enumerate((ga_ref, xl_ref, gl_ref)):
            for j in range(2):
                ref[:, j * nc:(j + 1) * nc] = z_chunk(3 + 2 * sec + j)

        @pl.when(i == nt - 1)
        def _():
            finish()

    row = lambda w: pl.BlockSpec((tm, w), lambda i: (i, 0))
    full = lambda a: pl.BlockSpec(a.shape, lambda i: (0, 0))
    return pl.pallas_call(
        body, name="fwd_in", grid=(nt,),
        in_specs=[row(D), full(ln_gain), full(wt), row(128), row(128), row(128), full(wo_shard)],
        out_specs=[row(D), row(D), row(KVW), row(KVW), row(D), row(D), row(D), HBM],
        out_shape=[jax.ShapeDtypeStruct((s, D), bf16), jax.ShapeDtypeStruct((s, D), bf16),
                   jax.ShapeDtypeStruct((s, KVW), bf16), jax.ShapeDtypeStruct((s, KVW), bf16),
                   jax.ShapeDtypeStruct((s, D), f32), jax.ShapeDtypeStruct((s, D), f32),
                   jax.ShapeDtypeStruct((s, D), f32), jax.ShapeDtypeStruct((2 * D, D), bf16)],
        scratch_shapes=[pltpu.VMEM((WO_ROWS, D), bf16)] + _comm_sems(1),
        compiler_params=_params(("arbitrary",), 48),
    )(x, ln_gain, wt, *tabs, wo_shard)


HSUB = 4
SUBW = HSUB * BLK


def _sub_probs(kh, qg, n, sink_row):
    jj = lax.broadcasted_iota(jnp.int32, (BLK, SUBW), 0)
    ii = lax.broadcasted_iota(jnp.int32, (BLK, SUBW), 1) % BLK
    from_prev = jj > ii
    s2 = _dot_nt(kh, qg)
    sc = jnp.where(from_prev, s2[0:BLK] + jnp.where(n > 0, 0.0, NEG), s2[BLK:2 * BLK])
    m = jnp.maximum(jnp.max(sc, axis=0, keepdims=True), sink_row)
    p = jnp.exp(sc - m)
    es = jnp.exp(sink_row - m)
    inv = 1.0 / (jnp.sum(p, axis=0, keepdims=True) + es)
    return from_prev, p * inv, es * inv


def _split(t, from_prev):
    t = t.astype(bf16)
    zero = jnp.zeros_like(t)
    return jnp.concatenate([jnp.where(from_prev, t, zero), jnp.where(from_prev, zero, t)], axis=0)


def _stack_heads(ref, first):
    return jnp.concatenate([ref[:, HD * (first + g):HD * (first + g) + HD] for g in range(HSUB)], axis=0)


def _sink_rows(sinks):
    return jnp.repeat(sinks.reshape(NKV, GROUP), BLK, axis=1)


def _kv_specs():
    prev = pl.BlockSpec((BLK, KVW), lambda n: (jnp.maximum(n - 1, 0), 0))
    cur = pl.BlockSpec((BLK, KVW), lambda n: (n, 0))
    return [prev, cur, prev, cur]


def _attn_fwd(q, k, v, sinks):
    s = q.shape[0]

    def body(sink_ref, q_ref, kp_ref, kc_ref, vp_ref, vc_ref, o_ref):
        n = pl.program_id(0)
        for h in range(NKV):
            hs = slice(HD * h, HD * h + HD)
            kh = jnp.concatenate([kp_ref[:, hs], kc_ref[:, hs]], axis=0)
            vh = jnp.concatenate([vp_ref[:, hs], vc_ref[:, hs]], axis=0)
            for t in range(GROUP // HSUB):
                first = GROUP * h + HSUB * t
                from_prev, pn, _ = _sub_probs(kh, _stack_heads(q_ref, first), n,
                                              sink_ref[h:h + 1, SUBW * t:SUBW * t + SUBW])
                og = _dot_tn(_split(pn, from_prev), vh)
                for g in range(HSUB):
                    o_ref[:, HD * (first + g):HD * (first + g) + HD] = og[BLK * g:BLK * g + BLK]

    return pl.pallas_call(
        body, name="attn_fwd", grid=(s // BLK,),
        in_specs=[pl.BlockSpec((NKV, GROUP * BLK), lambda n: (0, 0)), pl.BlockSpec((BLK, D), lambda n: (n, 0))]
        + _kv_specs(),
        out_specs=pl.BlockSpec((BLK, D), lambda n: (n, 0)),
        out_shape=jax.ShapeDtypeStruct((s, D), f32),
        compiler_params=_params(("arbitrary",), 32),
    )(_sink_rows(sinks), q, k, k, v, v)


def _attn_bwd(q, k, v, do, sinks, dwo):
    s = q.shape[0]
    nb = s // BLK

    def body(sink_ref, q_ref, do_ref, kp_ref, kc_ref, vp_ref, vc_ref, dwo_ref, dq_ref, dk_ref, dv_ref, ds_ref,
             land_ref, send_sems, recv_sems, local_sems):
        n = pl.program_id(0)
        start, finish = _scatter_ops([dwo_ref], [land_ref], send_sems, recv_sems, local_sems)

        @pl.when(n == 0)
        def _():
            start()
            dk_ref[...] = jnp.zeros_like(dk_ref)
            dv_ref[...] = jnp.zeros_like(dv_ref)
            ds_ref[...] = jnp.zeros_like(ds_ref)

        prev_rows = pl.ds(pl.multiple_of(jnp.maximum(n - 1, 0) * BLK, BLK), BLK)
        cur_rows = pl.ds(pl.multiple_of(n * BLK, BLK), BLK)
        for h in range(NKV):
            hs = slice(HD * h, HD * h + HD)
            kh = jnp.concatenate([kp_ref[:, hs], kc_ref[:, hs]], axis=0)
            vh = jnp.concatenate([vp_ref[:, hs], vc_ref[:, hs]], axis=0)
            dkh = jnp.zeros((2 * BLK, HD), f32)
            dvh = jnp.zeros((2 * BLK, HD), f32)
            for t in range(GROUP // HSUB):
                first = GROUP * h + HSUB * t
                lanes = slice(SUBW * t, SUBW * t + SUBW)
                qg, dog = _stack_heads(q_ref, first), _stack_heads(do_ref, first)
                from_prev, pn, ps = _sub_probs(kh, qg, n, sink_ref[h:h + 1, lanes])
                dp2 = _dot_nt(vh, dog)
                dp = jnp.where(from_prev, dp2[0:BLK], dp2[BLK:2 * BLK])
                dsum = jnp.sum(pn * dp, axis=0, keepdims=True)
                ds_ref[h:h + 1, lanes] += -ps * dsum
                ds2 = _split(pn * (dp - dsum), from_prev)
                dqg = _dot_tn(ds2, kh)
                for g in range(HSUB):
                    dq_ref[:, HD * (first + g):HD * (first + g) + HD] = dqg[BLK * g:BLK * g + BLK]
                dkh = dkh + _dot(ds2, qg)
                dvh = dvh + _dot(_split(pn, from_prev), dog)
            dk_ref[prev_rows, hs] += dkh[0:BLK]
            dk_ref[cur_rows, hs] += dkh[BLK:2 * BLK]
            dv_ref[prev_rows, hs] += dvh[0:BLK]
            dv_ref[cur_rows, hs] += dvh[BLK:2 * BLK]

        @pl.when(n == nb - 1)
        def _():
            finish()

    blk = pl.BlockSpec((BLK, D), lambda n: (n, 0))
    whole = lambda r, w: pl.BlockSpec((r, w), lambda n: (0, 0))
    return pl.pallas_call(
        body, name="attn_bwd", grid=(nb,),
        in_specs=[whole(NKV, GROUP * BLK), blk, blk] + _kv_specs() + [HBM],
        out_specs=[blk, whole(s, KVW), whole(s, KVW), whole(NKV, GROUP * BLK), HBM],
        out_shape=[jax.ShapeDtypeStruct((s, D), f32), jax.ShapeDtypeStruct((s, KVW), f32),
                   jax.ShapeDtypeStruct((s, KVW), f32), jax.ShapeDtypeStruct((NKV, GROUP * BLK), f32),
                   jax.ShapeDtypeStruct((NDEV, WO_ROWS, D), bf16)],
        scratch_shapes=_comm_sems(1),
        compiler_params=_params(("arbitrary",), 48),
    )(_sink_rows(sinks), q, do, k, k, v, v, dwo)


def _block_diag(w):
    w4 = w.reshape(NGRP, 4, HD, HD)
    eye = jnp.eye(4, dtype=w.dtype)
    return jnp.einsum('gjcd,jk->gjckd', w4, eye).reshape(NGRP, 256, 256).astype(bf16)


def _gates(u, wr_ref, wi_ref, br, bi, sp):
    ub = u.astype(bf16)
    pr = jnp.concatenate([_dot(ub[:, 256 * g:256 * g + 256], wr_ref[g]) for g in range(NGRP)], axis=1)
    pi = jnp.concatenate([_dot(ub[:, 256 * g:256 * g + 256], wi_ref[g]) for g in range(NGRP)], axis=1)
    r = _sigmoid(pr + br)
    i = _sigmoid(pi + bi)
    la = -LRU_C * r * sp
    a = jnp.exp(la)
    x2 = 2.0 * la
    y = jnp.where(x2 > -0.02, -x2 * (1.0 + x2 * (0.5 + x2 * (1.0 / 6.0))), 1.0 - a * a)
    inv_mult = lax.rsqrt(jnp.maximum(y, 1e-30))
    return ub, r, i, a, y * inv_mult, inv_mult


def _later(x, before, k):
    if k == 0:
        return x
    row = lax.broadcasted_iota(jnp.int32, before.shape, 0)
    rolled = pltpu.roll(x, k, 0)
    first = jnp.where(row < k, pltpu.roll(before, k, 0), rolled[0:8])
    return jnp.concatenate([first, rolled[8:]], axis=0)


def _earlier(x, after, k):
    if k == 0:
        return x
    n = x.shape[0]
    row = lax.broadcasted_iota(jnp.int32, after.shape, 0)
    rolled = pltpu.roll(x, n - k, 0)
    last = jnp.where(row >= 8 - k, pltpu.roll(after, 8 - k, 0), rolled[n - 8:n])
    return jnp.concatenate([rolled[0:n - 8], last], axis=0)


def _lru_fwd(xl, conv_w, conv_b, wr, wi, br, bi, lam, tm):
    s = xl.shape[0]

    def body(xp_ref, x_ref, cw_ref, cb_ref, wr_ref, wi_ref, br_ref, bi_ref, lam_ref, u_ref, h_ref,
             a_scr, b_scr, hcar):
        t0 = pl.program_id(0)

        @pl.when(t0 == 0)
        def _():
            hcar[...] = jnp.zeros_like(hcar)

        x = x_ref[...]
        before = jnp.where(t0 > 0, xp_ref[...], 0.0)
        u = cb_ref[...] + sum(cw_ref[k:k + 1, :] * _later(x, before, CONVW - 1 - k) for k in range(CONVW))
        u_ref[...] = u
        sp = _softplus(-lam_ref[...])
        _, _, i, a, mult, _ = _gates(u, wr_ref, wi_ref, br_ref[...], bi_ref[...], sp)
        a_scr[...] = a
        b_scr[...] = mult * (i * u)

        def step(t, hc):
            hn = a_scr[pl.ds(t, 1), :] * hc + b_scr[pl.ds(t, 1), :]
            h_ref[pl.ds(t, 1), :] = hn
            return hn

        hcar[...] = lax.fori_loop(0, tm, step, hcar[...], unroll=8)

    row = pl.BlockSpec((tm, LW), lambda i: (i, 0))
    prev8 = pl.BlockSpec((8, LW), lambda i: (jnp.maximum(i * (tm // 8) - 1, 0), 0))
    full = lambda a: pl.BlockSpec(a.shape, lambda i: (0,) * a.ndim)
    return pl.pallas_call(
        body, name="lru_fwd", grid=(s // tm,),
        in_specs=[prev8, row, full(conv_w), full(conv_b), full(wr), full(wi), full(br), full(bi), full(lam)],
        out_specs=[row, row],
        out_shape=[jax.ShapeDtypeStruct((s, LW), f32), jax.ShapeDtypeStruct((s, LW), f32)],
        scratch_shapes=[pltpu.VMEM((tm, LW), f32), pltpu.VMEM((tm, LW), f32), pltpu.VMEM((1, LW), f32)],
        compiler_params=_params(("arbitrary",), 48),
    )(xl, xl, conv_w, conv_b, wr, wi, br, bi, lam)


def _lru_bwd(u, hl, dhl, xl, conv_w, wr, wi, br, bi, lam, tm):
    s = u.shape[0]
    nt = s // tm

    def body(u_ref, h_ref, hp_ref, dh_ref, x_ref, xp_ref, cw_ref, wr_ref, wi_ref, br_ref, bi_ref, lam_ref,
             dxl_ref, dwr_ref, dwi_ref, dbr_ref, dbi_ref, dlam_ref, dcb_ref, dcw_ref,
             a_scr, l_scr, lcar, dunext):
        t0 = pl.program_id(0)
        tile = nt - 1 - t0

        @pl.when(t0 == 0)
        def _():
            lcar[...] = jnp.zeros_like(lcar)
            dunext[...] = jnp.zeros_like(dunext)
            for ref in (dwr_ref, dwi_ref, dbr_ref, dbi_ref, dlam_ref, dcb_ref, dcw_ref):
                ref[...] = jnp.zeros_like(ref)

        u = u_ref[...]
        lam = lam_ref[...]
        sp = _softplus(-lam)
        ub, r, i, a, mult, inv_mult = _gates(u, wr_ref, wi_ref, br_ref[...], bi_ref[...], sp)
        a_scr[...] = a

        def step(k, c):
            t = tm - 1 - k
            lt = dh_ref[pl.ds(t, 1), :] + c
            l_scr[pl.ds(t, 1), :] = lt
            return a_scr[pl.ds(t, 1), :] * lt

        lcar[...] = lax.fori_loop(0, tm, step, lcar[...], unroll=8)
        lt = l_scr[...]

        hprev = _later(h_ref[...], jnp.where(tile > 0, hp_ref[...], 0.0), 1)
        da = lt * hprev
        dmult = lt * (i * u)
        di = lt * mult * u
        du = lt * mult * i
        dla = da * a - dmult * (a * a) * inv_mult
        dr = dla * (-LRU_C * sp)
        dlam_ref[...] += jnp.sum(dla * (-LRU_C * r), axis=0, keepdims=True)
        dpr = dr * r * (1.0 - r)
        dpi = di * i * (1.0 - i)
        dbr_ref[...] += jnp.sum(dpr, axis=0, keepdims=True)
        dbi_ref[...] += jnp.sum(dpi, axis=0, keepdims=True)
        dprb, dpib = dpr.astype(bf16), dpi.astype(bf16)
        dug = []
        for g in range(NGRP):
            gs = slice(256 * g, 256 * g + 256)
            dwr_ref[g] += _dot_tn(ub[:, gs], dprb[:, gs])
            dwi_ref[g] += _dot_tn(ub[:, gs], dpib[:, gs])
            dug.append(_dot_nt(dprb[:, gs], wr_ref[g]) + _dot_nt(dpib[:, gs], wi_ref[g]))
        du = du + jnp.concatenate(dug, axis=1)

        dcb_ref[...] += jnp.sum(du, axis=0, keepdims=True)
        x = x_ref[...]
        before = jnp.where(tile > 0, xp_ref[...], 0.0)
        for k in range(CONVW):
            dcw_ref[k:k + 1, :] += jnp.sum(du * _later(x, before, CONVW - 1 - k), axis=0, keepdims=True)
        after = dunext[...]
        dxl = sum(cw_ref[k:k + 1, :] * _earlier(du, after, CONVW - 1 - k) for k in range(CONVW))
        dxl_ref[...] = dxl.astype(bf16)
        dunext[...] = du[0:8, :]

        @pl.when(t0 == nt - 1)
        def _():
            dlam_ref[...] = dlam_ref[...] * (-_sigmoid(-lam))

    rev = lambda i: (nt - 1 - i, 0)
    row = pl.BlockSpec((tm, LW), rev)
    prev8 = pl.BlockSpec((8, LW), lambda i: (jnp.maximum((nt - 1 - i) * (tm // 8) - 1, 0), 0))
    full = lambda a: pl.BlockSpec(a.shape, lambda i: (0,) * a.ndim)
    vec = pl.BlockSpec((1, LW), lambda i: (0, 0))
    bd = pl.BlockSpec((NGRP, 256, 256), lambda i: (0, 0, 0))
    return pl.pallas_call(
        body, name="lru_bwd", grid=(nt,),
        in_specs=[row, row, prev8, row, row, prev8, full(conv_w), full(wr), full(wi), full(br), full(bi), full(lam)],
        out_specs=[row, bd, bd, vec, vec, vec, vec, pl.BlockSpec((CONVW, LW), lambda i: (0, 0))],
        out_shape=[jax.ShapeDtypeStruct((s, LW), bf16),
                   jax.ShapeDtypeStruct((NGRP, 256, 256), f32), jax.ShapeDtypeStruct((NGRP, 256, 256), f32),
                   jax.ShapeDtypeStruct((1, LW), f32), jax.ShapeDtypeStruct((1, LW), f32),
                   jax.ShapeDtypeStruct((1, LW), f32), jax.ShapeDtypeStruct((1, LW), f32),
                   jax.ShapeDtypeStruct((CONVW, LW), f32)],
        scratch_shapes=[pltpu.VMEM((tm, LW), f32), pltpu.VMEM((tm, LW), f32),
                        pltpu.VMEM((1, LW), f32), pltpu.VMEM((8, LW), f32)],
        compiler_params=_params(("arbitrary",), 56),
    )(u, hl, hl, dhl, xl, xl, conv_w, wr, wi, br, bi, lam)


def _gated_norm(t, gate, gain):
    sg = _sigmoid(gate)
    silu = gate * sg
    p = t * silu
    rstd = lax.rsqrt(jnp.mean(p * p, axis=-1, keepdims=True) + EPS)
    ph = p * rstd
    return sg, silu, rstd, ph, ph * gain


def _gated_norm_bwd(dy, t, gate, gain, sg, silu, rstd, ph):
    w = dy * gain
    dp = rstd * (w - ph * jnp.mean(w * ph, axis=-1, keepdims=True))
    dgate = dp * t * (sg * (1.0 + gate * (1.0 - sg)))
    return jnp.sum(dy * ph, axis=0, keepdims=True), dp * silu, dgate


def _out_fwd_bwd(x, tgt, o, ga, hl, gl, again, lgain, fgain, wo, tm):
    s = x.shape[0]
    nt = s // tm

    def body(x_ref, t_ref, o_ref, ga_ref, hl_ref, gl_ref, ag_ref, lg_ref, fg_ref, wo_ref,
             dx2_ref, do_ref, dga_ref, dhl_ref, dgl_ref, dwo_ref, gfg_ref, gag_ref, glg_ref, loss_ref, acc):
        i = pl.program_id(0)

        @pl.when(i == 0)
        def _():
            acc[...] = jnp.zeros_like(acc)
            for ref in (gfg_ref, gag_ref, glg_ref, loss_ref):
                ref[...] = jnp.zeros_like(ref)

        oo, gga, hh, ggl = o_ref[...], ga_ref[...], hl_ref[...], gl_ref[...]
        ag, lg, fg = ag_ref[...], lg_ref[...], fg_ref[...]
        sga, silua, ra, pah, ya = _gated_norm(oo, gga, ag)
        sgl, silul, rl, plh, yl = _gated_norm(hh, ggl, lg)
        yab, ylb = ya.astype(bf16), yl.astype(bf16)
        y = _dot(yab, wo_ref[0:D, :]) + _dot(ylb, wo_ref[D:2 * D, :])
        x2 = x_ref[...] + y
        r2 = lax.rsqrt(jnp.mean(x2 * x2, axis=-1, keepdims=True) + EPS)
        x2h = x2 * r2
        err = x2h * fg - t_ref[...]
        loss_ref[...] += 0.5 * jnp.sum(jnp.sum(err * err, axis=-1, keepdims=True) * (1.0 / D))
        dout = err * (1.0 / D)
        gfg_ref[...] += jnp.sum(dout * x2h, axis=0, keepdims=True)
        w = dout * fg
        dx2 = r2 * (w - x2h * jnp.mean(w * x2h, axis=-1, keepdims=True))
        dx2_ref[...] = dx2
        dyb = dx2.astype(bf16)
        acc[0:D, :] += _dot_tn(yab, dyb)
        acc[D:2 * D, :] += _dot_tn(ylb, dyb)
        dya = _dot_nt(dyb, wo_ref[0:D, :])
        dyl = _dot_nt(dyb, wo_ref[D:2 * D, :])
        gag, do, dga = _gated_norm_bwd(dya, oo, gga, ag, sga, silua, ra, pah)
        glg, dhl, dgl = _gated_norm_bwd(dyl, hh, ggl, lg, sgl, silul, rl, plh)
        gag_ref[...] += gag
        glg_ref[...] += glg
        do_ref[...] = do.astype(bf16)
        dga_ref[...] = dga.astype(bf16)
        dhl_ref[...] = dhl
        dgl_ref[...] = dgl.astype(bf16)

        @pl.when(i == nt - 1)
        def _():
            dwo_ref[...] = acc[...].astype(bf16)

    row = pl.BlockSpec((tm, D), lambda i: (i, 0))
    vec = pl.BlockSpec((1, D), lambda i: (0, 0))
    mat = pl.BlockSpec((2 * D, D), lambda i: (0, 0))
    return pl.pallas_call(
        body, name="out_fwd_bwd", grid=(nt,),
        in_specs=[row] * 6 + [vec] * 3 + [mat],
        out_specs=[row] * 5 + [mat, vec, vec, vec, pl.BlockSpec((1, 128), lambda i: (0, 0))],
        out_shape=[jax.ShapeDtypeStruct((s, D), f32), jax.ShapeDtypeStruct((s, D), bf16),
                   jax.ShapeDtypeStruct((s, D), bf16), jax.ShapeDtypeStruct((s, D), f32),
                   jax.ShapeDtypeStruct((s, D), bf16), jax.ShapeDtypeStruct((2 * D, D), bf16),
                   jax.ShapeDtypeStruct((1, D), f32), jax.ShapeDtypeStruct((1, D), f32),
                   jax.ShapeDtypeStruct((1, D), f32), jax.ShapeDtypeStruct((1, 128), f32)],
        scratch_shapes=[pltpu.VMEM((2 * D, D), f32)],
        compiler_params=_params(("arbitrary",), 56),
    )(x, tgt, o, ga, hl, gl, again, lgain, fgain, wo)


def _bwd_in(x, dx2, dq, dk, dv, dga, dxl, dgl, ln_gain, wt, tabs, tm):
    s = x.shape[0]

    def body(x_ref, dx2_ref, dq_ref, dk_ref, dv_ref, dga_ref, dxl_ref, dgl_ref, g_ref, wt_ref,
             c_ref, sa_ref, sb_ref, gx_ref, gln_ref, dz_ref, dz_scr):
        @pl.when(pl.program_id(0) == 0)
        def _():
            gln_ref[...] = jnp.zeros_like(gln_ref)

        c, sa, sb = c_ref[...], sa_ref[...], sb_ref[...]
        for j in range(D // 128):
            js = slice(128 * j, 128 * j + 128)
            dz_scr[:, js] = (_unrope(dq_ref[:, js], c, sa, sb) * (HD ** -0.5)).astype(bf16)
        for j in range(KVW // 128):
            js = slice(128 * j, 128 * j + 128)
            dz_scr[:, D + 128 * j:D + 128 * j + 128] = _unrope(dk_ref[:, js], c, sa, sb).astype(bf16)
        dz_scr[:, D + KVW:D + 2 * KVW] = dv_ref[...].astype(bf16)
        dz_scr[:, 1536:2560] = dga_ref[...]
        dz_scr[:, 2560:3584] = dxl_ref[...]
        dz_scr[:, 3584:4608] = dgl_ref[...]
        for p in range(NDEV):
            dz_ref[p] = dz_scr[:, WT_ROWS * p:WT_ROWS * (p + 1)]
        dh = _dot(dz_scr[:, 0:512], wt_ref[0:512, :])
        for ci in range(1, NIN // 512):
            dh = dh + _dot(dz_scr[:, 512 * ci:512 * ci + 512], wt_ref[512 * ci:512 * ci + 512, :])
        xx = x_ref[...]
        rstd = lax.rsqrt(jnp.mean(xx * xx, axis=-1, keepdims=True) + EPS)
        xh = xx * rstd
        gln_ref[...] += jnp.sum(dh * xh, axis=0, keepdims=True)
        w = dh * g_ref[...]
        gx_ref[...] = dx2_ref[...] + rstd * (w - xh * jnp.mean(w * xh, axis=-1, keepdims=True))

    row = lambda w: pl.BlockSpec((tm, w), lambda i: (i, 0))
    full = lambda a: pl.BlockSpec(a.shape, lambda i: (0, 0))
    return pl.pallas_call(
        body, name="bwd_in", grid=(s // tm,),
        in_specs=[row(D), row(D), row(D), row(KVW), row(KVW), row(D), row(D), row(D), full(ln_gain), full(wt),
                  row(128), row(128), row(128)],
        out_specs=[row(D), pl.BlockSpec((1, D), lambda i: (0, 0)),
                   pl.BlockSpec((NDEV, tm, WT_ROWS), lambda i: (0, i, 0))],
        out_shape=[jax.ShapeDtypeStruct((s, D), f32), jax.ShapeDtypeStruct((1, D), f32),
                   jax.ShapeDtypeStruct((NDEV, s, WT_ROWS), bf16)],
        scratch_shapes=[pltpu.VMEM((tm, NIN), bf16)],
        compiler_params=_params(("arbitrary",), 56),
    )(x, dx2, dq, dk, dv, dga, dxl, dgl, ln_gain, wt, *tabs)


def _dwt_scatter(dzs, h, small, tm):
    s = h.shape[0]
    nk = s // tm
    srows = small.shape[0] // NDEV

    def body(order_ref, dz_ref, h_ref, sm_ref, lwt_ref, lsm_ref, acc, stage, send_sems, recv_sems, local_sem,
             sm_send, sm_recv, sm_local):
        j, k = pl.program_id(0), pl.program_id(1)
        x, y, c = _place()
        my = 4 * x + 2 * y + c
        sm_start, sm_finish = _scatter_ops([sm_ref], [lsm_ref], sm_send, sm_recv, sm_local)

        def owner(jj):
            p = (my + 1 + jj) % NDEV
            return p >> 2, (p >> 1) & 1, p & 1

        def send(jj):
            return pltpu.make_async_remote_copy(
                src_ref=stage.at[jj % 2], dst_ref=lwt_ref.at[my], send_sem=send_sems.at[jj],
                recv_sem=recv_sems.at[jj], device_id=owner(jj), device_id_type=MESH)

        def arrival(jj):
            return pltpu.make_async_remote_copy(
                src_ref=stage.at[0], dst_ref=lwt_ref.at[(my + 2 * NDEV - 1 - jj) % NDEV], send_sem=send_sems.at[jj],
                recv_sem=recv_sems.at[jj], device_id=owner(jj), device_id_type=MESH)

        def keep():
            return pltpu.make_async_copy(stage.at[(NDEV - 1) % 2], lwt_ref.at[my], local_sem)

        @pl.when((j == 0) & (k == 0))
        def _():
            sm_start()

        @pl.when(k == 0)
        def _():
            acc[...] = jnp.zeros_like(acc)

        acc[...] += _dot_tn(dz_ref[...], h_ref[...])

        @pl.when(k == nk - 1)
        def _():
            @pl.when(j >= 2)
            def _():
                send(j - 2).wait_send()

            stage[j % 2] = acc[...].astype(bf16)

            @pl.when(j < NDEV - 1)
            def _():
                send(j).start()

            @pl.when(j == NDEV - 1)
            def _():
                keep().start()
                send(NDEV - 2).wait_send()
                for jj in range(NDEV - 1):
                    arrival(jj).wait_recv()
                keep().wait()
                sm_finish()

    return pl.pallas_call(
        body, name="dwt_scatter",
        grid_spec=pltpu.PrefetchScalarGridSpec(
            num_scalar_prefetch=1, grid=(NDEV, nk),
            in_specs=[pl.BlockSpec((None, tm, WT_ROWS), lambda j, k, order: (order[j], k, 0)),
                      pl.BlockSpec((tm, D), lambda j, k, order: (k, 0)), HBM],
            out_specs=[HBM, HBM],
            scratch_shapes=[pltpu.VMEM((WT_ROWS, D), f32), pltpu.VMEM((2, WT_ROWS, D), bf16),
                            pltpu.SemaphoreType.DMA((NDEV - 1,)), pltpu.SemaphoreType.DMA((NDEV - 1,)),
                            pltpu.SemaphoreType.DMA(())] + _comm_sems(1)),
        out_shape=[jax.ShapeDtypeStruct((NDEV, WT_ROWS, D), bf16), jax.ShapeDtypeStruct((NDEV, srows, D), f32)],
        compiler_params=_params(("arbitrary", "arbitrary"), 32),
    )((4 * lax.axis_index("x") + 2 * lax.axis_index("y") + lax.axis_index("c") + 1 + jnp.arange(NDEV)) % NDEV,
      dzs, h, small)


def _diag_blocks(bd):
    eye = jnp.eye(4, dtype=bd.dtype)
    return jnp.einsum('gjckd,jk->gjcd', bd.reshape(NGRP, 4, HD, 4, HD), eye).reshape(NQ, HD, HD)


def _sequence_step(x, tgt, wt, wo_shard, conv_w, p):
    s = x.shape[0]
    tm = min(256, s)
    tabs = _rope_tables(s)
    wr, wi = _block_diag(p["w_rgate"]), _block_diag(p["w_igate"])
    sinks = p["sinks"].reshape(NQ)
    h, q, k, v, ga, xl, gl, wo = _fwd_in(x, p["ln_gain"], wt, tabs, wo_shard, tm)
    o = _attn_fwd(q, k, v, sinks)
    u, hl = _lru_fwd(xl, conv_w, p["conv_b"], wr, wi, p["b_rgate"], p["b_igate"], p["lru_lambda"], tm)
    dx2, do, dga, dhl, dgl, dwo, g_fg, g_ag, g_lg, loss = _out_fwd_bwd(
        x, tgt, o, ga, hl, gl, p["attn_out_gain"], p["lru_out_gain"], p["final_gain"], wo, tm)
    dq, dk, dv, dsink, land_wo = _attn_bwd(q, k, v, do, sinks, dwo)
    dxl, dwr, dwi, dbr, dbi, dlam, dcb, dcw = _lru_bwd(
        u, hl, dhl, xl, conv_w, wr, wi, p["b_rgate"], p["b_igate"], p["lru_lambda"], tm)
    gx, g_ln, dzs = _bwd_in(x, dx2, dq, dk, dv, dga, dxl, dgl, p["ln_gain"], wt, tabs, tm)
    small = dict(ln_gain=g_ln, sinks=dsink.reshape(NQ, BLK).sum(axis=1)[None], conv_w=dcw, conv_b=dcb,
                 w_rgate=_diag_blocks(dwr), b_rgate=dbr, w_igate=_diag_blocks(dwi), b_igate=dbi, lru_lambda=dlam,
                 attn_out_gain=g_ag, lru_out_gain=g_lg, final_gain=g_fg)
    land_wt, land_sm = _dwt_scatter(dzs, h, _pack_small(small, loss), min(512, s))
    return gx, land_wt, land_wo, land_sm


def _all_gather(srcs, out_dtypes, name):
    n = len(srcs)
    cast = [a.dtype != dt for a, dt in zip(srcs, out_dtypes)]

    def body(*refs):
        src_refs, out_refs = refs[:n], refs[n:2 * n]
        stage_refs = list(refs[2 * n:2 * n + sum(cast)])
        mine_refs = []
        for a in range(n):
            if cast[a]:
                st = stage_refs.pop(0)
                st[...] = src_refs[a][...].astype(out_dtypes[a])
                mine_refs.append(st)
            else:
                mine_refs.append(src_refs[a])
        start, finish = _gather_ops(mine_refs, out_refs, *refs[-3:])
        start()
        finish()

    vmem = pl.BlockSpec(memory_space=pltpu.VMEM)
    return pl.pallas_call(
        body, name=name,
        in_specs=[vmem] * n, out_specs=[HBM] * n,
        out_shape=[jax.ShapeDtypeStruct((NDEV * a.shape[0], a.shape[1]), dt) for a, dt in zip(srcs, out_dtypes)],
        scratch_shapes=[pltpu.VMEM(a.shape, dt) for a, dt, cst in zip(srcs, out_dtypes, cast) if cst] + _comm_sems(n),
        compiler_params=pltpu.CompilerParams(vmem_limit_bytes=32 * MIB),
    )(*srcs)


def _sum_slots(land, tr, name):
    _, rows, cols = land.shape

    def body(l_ref, o_ref):
        acc = l_ref[0].astype(f32)
        for d in range(1, NDEV):
            acc = acc + l_ref[d].astype(f32)
        o_ref[...] = acc

    return pl.pallas_call(
        body, name=name, grid=(rows // tr,),
        in_specs=[pl.BlockSpec((NDEV, tr, cols), lambda i: (0, i, 0))],
        out_specs=pl.BlockSpec((tr, cols), lambda i: (i, 0)),
        out_shape=jax.ShapeDtypeStruct((rows, cols), f32),
        compiler_params=_params(("arbitrary",), 32),
    )(land)


def _adam_math(w, g, m, v):
    m2 = ADAM_B1 * m + (1.0 - ADAM_B1) * g
    v2 = ADAM_B2 * v + (1.0 - ADAM_B2) * (g * g)
    m_hat = m2 / (1.0 - ADAM_B1 ** ADAM_STEP)
    v_hat = v2 / (1.0 - ADAM_B2 ** ADAM_STEP)
    delta = -ADAM_LR * (m_hat / (jnp.sqrt(v_hat) + ADAM_EPS) + ADAM_WD * w)
    return delta, m2, v2


def _adamw(w, g, m, v, tr, name):
    rows, cols = w.shape

    def body(w_ref, g_ref, m_ref, v_ref, d_ref, m2_ref, v2_ref):
        d_ref[...], m2_ref[...], v2_ref[...] = _adam_math(w_ref[...], g_ref[...], m_ref[...], v_ref[...])

    blk = pl.BlockSpec((tr, cols), lambda i: (i, 0))
    return pl.pallas_call(
        body, name=name, grid=(rows // tr,),
        in_specs=[blk] * 4, out_specs=[blk] * 3,
        out_shape=[jax.ShapeDtypeStruct((rows, cols), f32)] * 3,
        compiler_params=_params(("arbitrary",), 32),
    )(w, g, m, v)


VEC_NAMES = ("ln_gain", "conv_b", "b_rgate", "b_igate", "lru_lambda", "attn_out_gain", "lru_out_gain", "final_gain")
ROW_RGATE, ROW_IGATE, ROW_VEC, ROW_SINKS = 0, 64, 128, 136
LOSS_LANE = NQ


def _adamw_small(g_rep, g_conv, w, m, v):
    names = list(VEC_NAMES) + ["sinks", "conv_w", "w_rgate", "w_igate"]
    ins = [g_rep, g_conv] + [d[k] for k in names for d in (w, m, v)]

    def body(*refs):
        g_ref, gc_ref = refs[0], refs[1]
        in_refs = refs[2:2 + 3 * len(names)]
        out_refs = refs[2 + 3 * len(names):]

        def update(j, g, at=None):
            w_ref, m_ref, v_ref = in_refs[3 * j:3 * j + 3]
            outs = out_refs[4 * j:4 * j + 4]
            pick = (lambda r: r[...]) if at is None else (lambda r: r[at])
            res = (g,) + _adam_math(pick(w_ref), g, pick(m_ref), pick(v_ref))
            for o_ref, val in zip(outs, res):
                if at is None:
                    o_ref[...] = val
                else:
                    o_ref[at] = val

        for j in range(len(VEC_NAMES)):
            update(j, g_ref[ROW_VEC + j:ROW_VEC + j + 1, :])
        update(len(VEC_NAMES), g_ref[ROW_SINKS:ROW_SINKS + 1, 0:NQ])
        update(len(VEC_NAMES) + 1, gc_ref[...], at=0)
        for gi, row0 in ((len(VEC_NAMES) + 2, ROW_RGATE), (len(VEC_NAMES) + 3, ROW_IGATE)):
            for nb in range(NQ):
                update(gi, g_ref[row0:row0 + HD, HD * nb:HD * nb + HD], at=(0, nb))

    vmem = pl.BlockSpec(memory_space=pltpu.VMEM)
    out_shape = [jax.ShapeDtypeStruct(w[k].shape, f32) for k in names for _ in range(4)]
    outs = pl.pallas_call(
        body, name="adamw_small",
        in_specs=[vmem] * len(ins), out_specs=[vmem] * len(out_shape), out_shape=out_shape,
        compiler_params=pltpu.CompilerParams(vmem_limit_bytes=32 * MIB),
    )(*ins)
    return {k: tuple(outs[4 * j:4 * j + 4]) for j, k in enumerate(names)}


def _pack_small(small, loss):
    gate = lambda g: g.transpose(1, 0, 2).reshape(HD, NQ * HD)
    row_s = jnp.concatenate([small["sinks"], loss[:, LOSS_LANE:128], jnp.zeros((1, D - 128), f32)], axis=1)
    rep = jnp.concatenate([gate(small["w_rgate"]), gate(small["w_igate"])] + [small[k] for k in VEC_NAMES]
                          + [row_s, jnp.zeros((SMALL_ROWS - ROW_SINKS - 1, D), f32)], axis=0)
    conv = small["conv_w"].reshape(CONVW, NDEV, 128).transpose(1, 0, 2)
    conv = jnp.pad(conv, ((0, 0), (0, 8 - CONVW), (0, D - 128)))
    return jnp.concatenate([rep.reshape(NDEV, SMALL_PER, D), conv], axis=1).reshape(NDEV * (SMALL_PER + 8), D)


def kernel(x, ln_gain, w_in, sinks, conv_w, conv_b, w_rgate, b_rgate, w_igate, b_igate, lru_lambda, attn_out_gain, lru_out_gain, w_out, final_gain, loss_target, m_ln_gain, m_w_in, m_sinks, m_conv_w, m_conv_b, m_w_rgate, m_b_rgate, m_w_igate, m_b_igate, m_lru_lambda, m_attn_out_gain, m_lru_out_gain, m_w_out, m_final_gain, v_ln_gain, v_w_in, v_sinks, v_conv_w, v_conv_b, v_w_rgate, v_b_rgate, v_w_igate, v_b_igate, v_lru_lambda, v_attn_out_gain, v_lru_out_gain, v_w_out, v_final_gain):
    w = dict(ln_gain=ln_gain, sinks=sinks, conv_w=conv_w, conv_b=conv_b, w_rgate=w_rgate, b_rgate=b_rgate,
             w_igate=w_igate, b_igate=b_igate, lru_lambda=lru_lambda, attn_out_gain=attn_out_gain,
             lru_out_gain=lru_out_gain, final_gain=final_gain.reshape(1, D))
    m = dict(ln_gain=m_ln_gain, sinks=m_sinks, conv_w=m_conv_w, conv_b=m_conv_b, w_rgate=m_w_rgate,
             b_rgate=m_b_rgate, w_igate=m_w_igate, b_igate=m_b_igate, lru_lambda=m_lru_lambda,
             attn_out_gain=m_attn_out_gain, lru_out_gain=m_lru_out_gain, final_gain=m_final_gain.reshape(1, D))
    v = dict(ln_gain=v_ln_gain, sinks=v_sinks, conv_w=v_conv_w, conv_b=v_conv_b, w_rgate=v_w_rgate,
             b_rgate=v_b_rgate, w_igate=v_w_igate, b_igate=v_b_igate, lru_lambda=v_lru_lambda,
             attn_out_gain=v_attn_out_gain, lru_out_gain=v_lru_out_gain, final_gain=v_final_gain.reshape(1, D))

    conv_blk = jnp.pad(conv_w[0], ((0, 8 - CONVW), (0, 0)))
    wt, cw_all = _all_gather([w_in[0].T, conv_blk], [bf16, f32], "gather_weights")
    conv_full = cw_all.reshape(NDEV, 8, 128)[:, 0:CONVW].transpose(1, 0, 2).reshape(CONVW, LW)

    p = {k: (w[k][0] if k in ("w_rgate", "w_igate") else w[k]) for k in w if k != "conv_w"}
    gx, land_wt, land_wo, land_sm = _sequence_step(x[0], loss_target[0], wt, w_out[0], conv_full, p)

    g_wt = _sum_slots(land_wt, 192, "sum_wt")
    g_wo = _sum_slots(land_wo, 256, "sum_wo")
    g_sm = _sum_slots(land_sm, SMALL_PER + 8, "sum_small")
    (g_rep,) = _all_gather([g_sm[0:SMALL_PER]], [f32], "gather_small")
    g_conv = g_sm[SMALL_PER:SMALL_PER + CONVW, 0:128]

    d_win, m_win, v_win = _adamw(w_in[0].T, g_wt, m_w_in[0].T, v_w_in[0].T, 192, "adamw_w_in")
    g_win, d_win, m_win, v_win = (t.T for t in (g_wt, d_win, m_win, v_win))
    d_wo, m_wo, v_wo = _adamw(w_out[0], g_wo, m_w_out[0], v_w_out[0], 256, "adamw_w_out")
    res = _adamw_small(g_rep, g_conv, w, m, v)
    res["w_in"] = tuple(t[None] for t in (g_win, d_win, m_win, v_win))
    res["w_out"] = tuple(t[None] for t in (g_wo, d_wo, m_wo, v_wo))
    res["final_gain"] = tuple(t.reshape(D) for t in res["final_gain"])

    order = ("ln_gain", "w_in", "sinks", "conv_w", "conv_b", "w_rgate", "b_rgate", "w_igate", "b_igate",
             "lru_lambda", "attn_out_gain", "lru_out_gain", "w_out", "final_gain")
    total_loss = g_rep[ROW_SINKS, LOSS_LANE]
    return (total_loss, gx[None]) + tuple(res[k][i] for i in range(4) for k in order)
```

```python
import jax
import jax.numpy as jnp
from jax import lax
from jax.experimental import pallas as pl
from jax.experimental.pallas import tpu as pltpu

f32 = jnp.float32
bf16 = jnp.bfloat16

D = 1024
HD = 64
NQ = 16
NKV = 4
GROUP = NQ // NKV
KVW = NKV * HD
BLK = 128
ROT = 16
THETA = 500000.0
NEG = -1e30
LW = 1024
NGRP = 4
CONVW = 4
LRU_C = 8.0
NIN = 4608
EPS = 1e-6
NDEV = 8
WT_ROWS = NIN // NDEV
WO_ROWS = 2 * D // NDEV
SMALL_ROWS = 192
SMALL_PER = SMALL_ROWS // NDEV

ADAM_LR = 0.001
ADAM_B1 = 0.9
ADAM_B2 = 0.999
ADAM_EPS = 1e-08
ADAM_WD = 0.01
ADAM_STEP = 10

NT = (((1,), (1,)), ((), ()))
TN = (((0,), (0,)), ((), ()))
MESH = pl.DeviceIdType.MESH
MIB = 1024 * 1024


def _dot(a, b):
    return jnp.dot(a, b, preferred_element_type=f32)


def _dot_nt(a, b):
    return lax.dot_general(a, b, NT, preferred_element_type=f32)


def _dot_tn(a, b):
    return lax.dot_general(a, b, TN, preferred_element_type=f32)


def _params(sem, vmem_mib):
    return pltpu.CompilerParams(dimension_semantics=sem, vmem_limit_bytes=vmem_mib * MIB)


def _sigmoid(x):
    return 0.5 * jnp.tanh(0.5 * x) + 0.5


def _softplus(x):
    return jnp.maximum(x, 0.0) + jnp.log(1.0 + jnp.exp(-jnp.abs(x)))


def _rope_tables(s):
    pos = jnp.arange(s, dtype=f32)
    inv_freq = THETA ** (-jnp.arange(0, ROT, 2, dtype=f32) / ROT)
    ang = pos[:, None] * inv_freq[None, :]
    cs = jnp.concatenate([jnp.cos(ang) - 1.0, jnp.sin(ang)], axis=1)
    d = jnp.arange(128) % HD
    j = jnp.arange(ROT)[:, None]
    pick_c = ((d < ROT) & (j == d % (ROT // 2))).astype(f32)
    pick_sa = ((d >= ROT // 2) & (d < ROT) & (j == d)).astype(f32)
    pick_sb = -((d < ROT // 2) & (j == d + ROT // 2)).astype(f32)
    spread = lambda pick: jnp.dot(cs, pick, precision=lax.Precision.HIGHEST)
    return 1.0 + spread(pick_c), spread(pick_sa), spread(pick_sb)


def _rope(t, c, sa, sb):
    return t * c + pltpu.roll(t, 8, 1) * sa + pltpu.roll(t, 120, 1) * sb


def _unrope(dr, c, sa, sb):
    return dr * c + pltpu.roll(dr * sa, 120, 1) + pltpu.roll(dr * sb, 8, 1)


def _place():
    return lax.axis_index("x"), lax.axis_index("y"), lax.axis_index("c")


def _gather_ops(mine_refs, out_refs, send_sems, recv_sems, local_sems):
    n = len(mine_refs)
    x, y, c = _place()
    me, sibling = (x, y, c), (x, y, 1 - c)
    chips = [(1 - x, y), (x, 1 - y), (1 - x, 1 - y)]

    def rows(a, dev):
        m = mine_refs[a].shape[0]
        return out_refs[a].at[pl.ds((4 * dev[0] + 2 * dev[1] + dev[2]) * m, m), :]

    def copy(a, k, block, to, own=False):
        return pltpu.make_async_remote_copy(
            src_ref=mine_refs[a] if own else rows(a, block), dst_ref=rows(a, block),
            send_sem=send_sems.at[a, k], recv_sem=recv_sems.at[a, k], device_id=to, device_id_type=MESH)

    def local(a):
        return pltpu.make_async_copy(mine_refs[a], rows(a, me), local_sems.at[a])

    def first(a):
        return [copy(a, 0, me, sibling, own=True)] + [copy(a, 1 + j, me, (*chip, c), own=True)
                                                      for j, chip in enumerate(chips)]

    def start():
        for a in range(n):
            local(a).start()
            for cp in first(a):
                cp.start()

    def finish():
        for j, chip in enumerate(chips):
            for a in range(n):
                copy(a, 1 + j, (*chip, c), me).wait_recv()
                copy(a, 4 + j, (*chip, c), sibling).start()
        for a in range(n):
            copy(a, 0, sibling, me).wait_recv()
            for j, chip in enumerate(chips):
                copy(a, 4 + j, (*chip, 1 - c), me).wait_recv()
        for a in range(n):
            for cp in first(a) + [copy(a, 4 + j, (*chip, c), sibling) for j, chip in enumerate(chips)]:
                cp.wait_send()
            local(a).wait()

    return start, finish


def _scatter_ops(src_refs, land_refs, send_sems, recv_sems, local_sems):
    n = len(src_refs)
    x, y, c = _place()
    my = 4 * x + 2 * y + c

    def peer(k):
        return x ^ (k >> 2), y ^ ((k >> 1) & 1), c ^ (k & 1)

    def piece(a, dev):
        m = src_refs[a].shape[0] // NDEV
        return src_refs[a].at[pl.ds(dev * m, m), :]

    def local(a):
        return pltpu.make_async_copy(piece(a, my), land_refs[a].at[my], local_sems.at[a])

    def send(a, k):
        px, py, pc = peer(k)
        return pltpu.make_async_remote_copy(
            src_ref=piece(a, 4 * px + 2 * py + pc), dst_ref=land_refs[a].at[my],
            send_sem=send_sems.at[a, k - 1], recv_sem=recv_sems.at[a, k - 1],
            device_id=(px, py, pc), device_id_type=MESH)

    def arrival(a, k):
        px, py, pc = peer(k)
        return pltpu.make_async_remote_copy(
            src_ref=piece(a, my), dst_ref=land_refs[a].at[4 * px + 2 * py + pc],
            send_sem=send_sems.at[a, k - 1], recv_sem=recv_sems.at[a, k - 1],
            device_id=(px, py, pc), device_id_type=MESH)

    def start():
        for a in range(n):
            local(a).start()
        for k in range(1, NDEV):
            for a in range(n):
                send(a, k).start()

    def finish():
        for k in range(1, NDEV):
            for a in range(n):
                send(a, k).wait_send()
        for k in range(1, NDEV):
            for a in range(n):
                arrival(a, k).wait_recv()
        for a in range(n):
            local(a).wait()

    return start, finish


def _comm_sems(n):
    return [pltpu.SemaphoreType.DMA((n, 7)), pltpu.SemaphoreType.DMA((n, 7)), pltpu.SemaphoreType.DMA((n,))]


HBM = pl.BlockSpec(memory_space=pl.ANY)


def _fwd_in(x, ln_gain, wt, tabs, wo_shard, tm):
    s = x.shape[0]
    nt = s // tm
    nc = 512

    def body(x_ref, g_ref, wt_ref, c_ref, sa_ref, sb_ref, wo_ref, h_ref, q_ref, k_ref, v_ref, ga_ref, xl_ref, gl_ref,
             wo_all, wo_stage, send_sems, recv_sems, local_sems):
        i = pl.program_id(0)
        start, finish = _gather_ops([wo_stage], [wo_all], send_sems, recv_sems, local_sems)

        @pl.when(i == 0)
        def _():
            wo_stage[...] = wo_ref[...].astype(bf16)
            start()

        xx = x_ref[...]
        rstd = lax.rsqrt(jnp.mean(xx * xx, axis=-1, keepdims=True) + EPS)
        h = (xx * rstd * g_ref[...]).astype(bf16)
        h_ref[...] = h
        c, sa, sb = c_ref[...], sa_ref[...], sb_ref[...]

        def z_chunk(ci):
            return _dot_nt(h, wt_ref[ci * nc:(ci + 1) * nc, :])

        for ci in range(2):
            z = z_chunk(ci)
            for j in range(nc // 128):
                r = _rope(z[:, 128 * j:128 * j + 128], c, sa, sb) * (HD ** -0.5)
                q_ref[:, ci * nc + 128 * j:ci * nc + 128 * j + 128] = r.astype(bf16)
        z = z_chunk(2)
        for j in range(2):
            k_ref[:, 128 * j:128 * j + 128] = _rope(z[:, 128 * j:128 * j + 128], c, sa, sb).astype(bf16)
        v_ref[...] = z[:, 256:512].astype(bf16)
        for sec, ref in enumerate((ga_ref, xl_ref, gl_ref)):
            for j in range(2):
                ref[:, j * nc:(j + 1) * nc] = z_chunk(3 + 2 * sec + j)

        @pl.when(i == nt - 1)
        def _():
            finish()

    row = lambda w: pl.BlockSpec((tm, w), lambda i: (i, 0))
    full = lambda a: pl.BlockSpec(a.shape, lambda i: (0, 0))
    return pl.pallas_call(
        body, name="fwd_in", grid=(nt,),
        in_specs=[row(D), full(ln_gain), full(wt), row(128), row(128), row(128), full(wo_shard)],
        out_specs=[row(D), row(D), row(KVW), row(KVW), row(D), row(D), row(D), HBM],
        out_shape=[jax.ShapeDtypeStruct((s, D), bf16), jax.ShapeDtypeStruct((s, D), bf16),
                   jax.ShapeDtypeStruct((s, KVW), bf16), jax.ShapeDtypeStruct((s, KVW), bf16),
                   jax.ShapeDtypeStruct((s, D), f32), jax.ShapeDtypeStruct((s, D), f32),
                   jax.ShapeDtypeStruct((s, D), f32), jax.ShapeDtypeStruct((2 * D, D), bf16)],
        scratch_shapes=[pltpu.VMEM((WO_ROWS, D), bf16)] + _comm_sems(1),
        compiler_params=_params(("arbitrary",), 48),
    )(x, ln_gain, wt, *tabs, wo_shard)


HSUB = 4
SUBW = HSUB * BLK


def _sub_probs(kh, qg, n, sink_row):
    jj = lax.broadcasted_iota(jnp.int32, (BLK, SUBW), 0)
    ii = lax.broadcasted_iota(jnp.int32, (BLK, SUBW), 1) % BLK
    from_prev = jj > ii
    s2 = _dot_nt(kh, qg)
    sc = jnp.where(from_prev, s2[0:BLK] + jnp.where(n > 0, 0.0, NEG), s2[BLK:2 * BLK])
    m = jnp.maximum(jnp.max(sc, axis=0, keepdims=True), sink_row)
    p = jnp.exp(sc - m)
    es = jnp.exp(sink_row - m)
    inv = 1.0 / (jnp.sum(p, axis=0, keepdims=True) + es)
    return from_prev, p * inv, es * inv


def _split(t, from_prev):
    t = t.astype(bf16)
    zero = jnp.zeros_like(t)
    return jnp.concatenate([jnp.where(from_prev, t, zero), jnp.where(from_prev, zero, t)], axis=0)


def _stack_heads(ref, first):
    return jnp.concatenate([ref[:, HD * (first + g):HD * (first + g) + HD] for g in range(HSUB)], axis=0)


def _sink_rows(sinks):
    return jnp.repeat(sinks.reshape(NKV, GROUP), BLK, axis=1)


def _kv_specs():
    prev = pl.BlockSpec((BLK, KVW), lambda n: (jnp.maximum(n - 1, 0), 0))
    cur = pl.BlockSpec((BLK, KVW), lambda n: (n, 0))
    return [prev, cur, prev, cur]


def _attn_fwd(q, k, v, sinks):
    s = q.shape[0]

    def body(sink_ref, q_ref, kp_ref, kc_ref, vp_ref, vc_ref, o_ref):
        n = pl.program_id(0)
        for h in range(NKV):
            hs = slice(HD * h, HD * h + HD)
            kh = jnp.concatenate([kp_ref[:, hs], kc_ref[:, hs]], axis=0)
            vh = jnp.concatenate([vp_ref[:, hs], vc_ref[:, hs]], axis=0)
            for t in range(GROUP // HSUB):
                first = GROUP * h + HSUB * t
                from_prev, pn, _ = _sub_probs(kh, _stack_heads(q_ref, first), n,
                                              sink_ref[h:h + 1, SUBW * t:SUBW * t + SUBW])
                og = _dot_tn(_split(pn, from_prev), vh)
                for g in range(HSUB):
                    o_ref[:, HD * (first + g):HD * (first + g) + HD] = og[BLK * g:BLK * g + BLK]

    return pl.pallas_call(
        body, name="attn_fwd", grid=(s // BLK,),
        in_specs=[pl.BlockSpec((NKV, GROUP * BLK), lambda n: (0, 0)), pl.BlockSpec((BLK, D), lambda n: (n, 0))]
        + _kv_specs(),
        out_specs=pl.BlockSpec((BLK, D), lambda n: (n, 0)),
        out_shape=jax.ShapeDtypeStruct((s, D), f32),
        compiler_params=_params(("arbitrary",), 32),
    )(_sink_rows(sinks), q, k, k, v, v)


def _attn_bwd(q, k, v, do, sinks, dwo):
    s = q.shape[0]
    nb = s // BLK

    def body(sink_ref, q_ref, do_ref, kp_ref, kc_ref, vp_ref, vc_ref, dwo_ref, dq_ref, dk_ref, dv_ref, ds_ref,
             land_ref, send_sems, recv_sems, local_sems):
        n = pl.program_id(0)
        start, finish = _scatter_ops([dwo_ref], [land_ref], send_sems, recv_sems, local_sems)

        @pl.when(n == 0)
        def _():
            start()
            dk_ref[...] = jnp.zeros_like(dk_ref)
            dv_ref[...] = jnp.zeros_like(dv_ref)
            ds_ref[...] = jnp.zeros_like(ds_ref)

        prev_rows = pl.ds(pl.multiple_of(jnp.maximum(n - 1, 0) * BLK, BLK), BLK)
        cur_rows = pl.ds(pl.multiple_of(n * BLK, BLK), BLK)
        for h in range(NKV):
            hs = slice(HD * h, HD * h + HD)
            kh = jnp.concatenate([kp_ref[:, hs], kc_ref[:, hs]], axis=0)
            vh = jnp.concatenate([vp_ref[:, hs], vc_ref[:, hs]], axis=0)
            dkh = jnp.zeros((2 * BLK, HD), f32)
            dvh = jnp.zeros((2 * BLK, HD), f32)
            for t in range(GROUP // HSUB):
                first = GROUP * h + HSUB * t
                lanes = slice(SUBW * t, SUBW * t + SUBW)
                qg, dog = _stack_heads(q_ref, first), _stack_heads(do_ref, first)
                from_prev, pn, ps = _sub_probs(kh, qg, n, sink_ref[h:h + 1, lanes])
                dp2 = _dot_nt(vh, dog)
                dp = jnp.where(from_prev, dp2[0:BLK], dp2[BLK:2 * BLK])
                dsum = jnp.sum(pn * dp, axis=0, keepdims=True)
                ds_ref[h:h + 1, lanes] += -ps * dsum
                ds2 = _split(pn * (dp - dsum), from_prev)
                dqg = _dot_tn(ds2, kh)
                for g in range(HSUB):
                    dq_ref[:, HD * (first + g):HD * (first + g) + HD] = dqg[BLK * g:BLK * g + BLK]
                dkh = dkh + _dot(ds2, qg)
                dvh = dvh + _dot(_split(pn, from_prev), dog)
            dk_ref[prev_rows, hs] += dkh[0:BLK]
            dk_ref[cur_rows, hs] += dkh[BLK:2 * BLK]
            dv_ref[prev_rows, hs] += dvh[0:BLK]
            dv_ref[cur_rows, hs] += dvh[BLK:2 * BLK]

        @pl.when(n == nb - 1)
        def _():
            finish()

    blk = pl.BlockSpec((BLK, D), lambda n: (n, 0))
    whole = lambda r, w: pl.BlockSpec((r, w), lambda n: (0, 0))
    return pl.pallas_call(
        body, name="attn_bwd", grid=(nb,),
        in_specs=[whole(NKV, GROUP * BLK), blk, blk] + _kv_specs() + [HBM],
        out_specs=[blk, whole(s, KVW), whole(s, KVW), whole(NKV, GROUP * BLK), HBM],
        out_shape=[jax.ShapeDtypeStruct((s, D), f32), jax.ShapeDtypeStruct((s, KVW), f32),
                   jax.ShapeDtypeStruct((s, KVW), f32), jax.ShapeDtypeStruct((NKV, GROUP * BLK), f32),
                   jax.ShapeDtypeStruct((NDEV, WO_ROWS, D), bf16)],
        scratch_shapes=_comm_sems(1),
        compiler_params=_params(("arbitrary",), 48),
    )(_sink_rows(sinks), q, do, k, k, v, v, dwo)


def _block_diag(w):
    w4 = w.reshape(NGRP, 4, HD, HD)
    eye = jnp.eye(4, dtype=w.dtype)
    return jnp.einsum('gjcd,jk->gjckd', w4, eye).reshape(NGRP, 256, 256).astype(bf16)


def _gates(u, wr_ref, wi_ref, br, bi, sp):
    ub = u.astype(bf16)
    pr = jnp.concatenate([_dot(ub[:, 256 * g:256 * g + 256], wr_ref[g]) for g in range(NGRP)], axis=1)
    pi = jnp.concatenate([_dot(ub[:, 256 * g:256 * g + 256], wi_ref[g]) for g in range(NGRP)], axis=1)
    r = _sigmoid(pr + br)
    i = _sigmoid(pi + bi)
    la = -LRU_C * r * sp
    a = jnp.exp(la)
    x2 = 2.0 * la
    y = jnp.where(x2 > -0.02, -x2 * (1.0 + x2 * (0.5 + x2 * (1.0 / 6.0))), 1.0 - a * a)
    inv_mult = lax.rsqrt(jnp.maximum(y, 1e-30))
    return ub, r, i, a, y * inv_mult, inv_mult


def _later(x, before, k):
    if k == 0:
        return x
    row = lax.broadcasted_iota(jnp.int32, before.shape, 0)
    rolled = pltpu.roll(x, k, 0)
    first = jnp.where(row < k, pltpu.roll(before, k, 0), rolled[0:8])
    return jnp.concatenate([first, rolled[8:]], axis=0)


def _earlier(x, after, k):
    if k == 0:
        return x
    n = x.shape[0]
    row = lax.broadcasted_iota(jnp.int32, after.shape, 0)
    rolled = pltpu.roll(x, n - k, 0)
    last = jnp.where(row >= 8 - k, pltpu.roll(after, 8 - k, 0), rolled[n - 8:n])
    return jnp.concatenate([rolled[0:n - 8], last], axis=0)


def _lru_fwd(xl, conv_w, conv_b, wr, wi, br, bi, lam, tm):
    s = xl.shape[0]

    def body(xp_ref, x_ref, cw_ref, cb_ref, wr_ref, wi_ref, br_ref, bi_ref, lam_ref, u_ref, h_ref,
             a_scr, b_scr, hcar):
        t0 = pl.program_id(0)

        @pl.when(t0 == 0)
        def _():
            hcar[...] = jnp.zeros_like(hcar)

        x = x_ref[...]
        before = jnp.where(t0 > 0, xp_ref[...], 0.0)
        u = cb_ref[...] + sum(cw_ref[k:k + 1, :] * _later(x, before, CONVW - 1 - k) for k in range(CONVW))
        u_ref[...] = u
        sp = _softplus(-lam_ref[...])
        _, _, i, a, mult, _ = _gates(u, wr_ref, wi_ref, br_ref[...], bi_ref[...], sp)
        a_scr[...] = a
        b_scr[...] = mult * (i * u)

        def step(t, hc):
            hn = a_scr[pl.ds(t, 1), :] * hc + b_scr[pl.ds(t, 1), :]
            h_ref[pl.ds(t, 1), :] = hn
            return hn

        hcar[...] = lax.fori_loop(0, tm, step, hcar[...], unroll=8)

    row = pl.BlockSpec((tm, LW), lambda i: (i, 0))
    prev8 = pl.BlockSpec((8, LW), lambda i: (jnp.maximum(i * (tm // 8) - 1, 0), 0))
    full = lambda a: pl.BlockSpec(a.shape, lambda i: (0,) * a.ndim)
    return pl.pallas_call(
        body, name="lru_fwd", grid=(s // tm,),
        in_specs=[prev8, row, full(conv_w), full(conv_b), full(wr), full(wi), full(br), full(bi), full(lam)],
        out_specs=[row, row],
        out_shape=[jax.ShapeDtypeStruct((s, LW), f32), jax.ShapeDtypeStruct((s, LW), f32)],
        scratch_shapes=[pltpu.VMEM((tm, LW), f32), pltpu.VMEM((tm, LW), f32), pltpu.VMEM((1, LW), f32)],
        compiler_params=_params(("arbitrary",), 48),
    )(xl, xl, conv_w, conv_b, wr, wi, br, bi, lam)


def _lru_bwd(u, hl, dhl, xl, conv_w, wr, wi, br, bi, lam, tm):
    s = u.shape[0]
    nt = s // tm

    def body(u_ref, h_ref, hp_ref, dh_ref, x_ref, xp_ref, cw_ref, wr_ref, wi_ref, br_ref, bi_ref, lam_ref,
             dxl_ref, dwr_ref, dwi_ref, dbr_ref, dbi_ref, dlam_ref, dcb_ref, dcw_ref,
             a_scr, l_scr, lcar, dunext):
        t0 = pl.program_id(0)
        tile = nt - 1 - t0

        @pl.when(t0 == 0)
        def _():
            lcar[...] = jnp.zeros_like(lcar)
            dunext[...] = jnp.zeros_like(dunext)
            for ref in (dwr_ref, dwi_ref, dbr_ref, dbi_ref, dlam_ref, dcb_ref, dcw_ref):
                ref[...] = jnp.zeros_like(ref)

        u = u_ref[...]
        lam = lam_ref[...]
        sp = _softplus(-lam)
        ub, r, i, a, mult, inv_mult = _gates(u, wr_ref, wi_ref, br_ref[...], bi_ref[...], sp)
        a_scr[...] = a

        def step(k, c):
            t = tm - 1 - k
            lt = dh_ref[pl.ds(t, 1), :] + c
            l_scr[pl.ds(t, 1), :] = lt
            return a_scr[pl.ds(t, 1), :] * lt

        lcar[...] = lax.fori_loop(0, tm, step, lcar[...], unroll=8)
        lt = l_scr[...]

        hprev = _later(h_ref[...], jnp.where(tile > 0, hp_ref[...], 0.0), 1)
        da = lt * hprev
        dmult = lt * (i * u)
        di = lt * mult * u
        du = lt * mult * i
        dla = da * a - dmult * (a * a) * inv_mult
        dr = dla * (-LRU_C * sp)
        dlam_ref[...] += jnp.sum(dla * (-LRU_C * r), axis=0, keepdims=True)
        dpr = dr * r * (1.0 - r)
        dpi = di * i * (1.0 - i)
        dbr_ref[...] += jnp.sum(dpr, axis=0, keepdims=True)
        dbi_ref[...] += jnp.sum(dpi, axis=0, keepdims=True)
        dprb, dpib = dpr.astype(bf16), dpi.astype(bf16)
        dug = []
        for g in range(NGRP):
            gs = slice(256 * g, 256 * g + 256)
            dwr_ref[g] += _dot_tn(ub[:, gs], dprb[:, gs])
            dwi_ref[g] += _dot_tn(ub[:, gs], dpib[:, gs])
            dug.append(_dot_nt(dprb[:, gs], wr_ref[g]) + _dot_nt(dpib[:, gs], wi_ref[g]))
        du = du + jnp.concatenate(dug, axis=1)

        dcb_ref[...] += jnp.sum(du, axis=0, keepdims=True)
        x = x_ref[...]
        before = jnp.where(tile > 0, xp_ref[...], 0.0)
        for k in range(CONVW):
            dcw_ref[k:k + 1, :] += jnp.sum(du * _later(x, before, CONVW - 1 - k), axis=0, keepdims=True)
        after = dunext[...]
        dxl = sum(cw_ref[k:k + 1, :] * _earlier(du, after, CONVW - 1 - k) for k in range(CONVW))
        dxl_ref[...] = dxl.astype(bf16)
        dunext[...] = du[0:8, :]

        @pl.when(t0 == nt - 1)
        def _():
            dlam_ref[...] = dlam_ref[...] * (-_sigmoid(-lam))

    rev = lambda i: (nt - 1 - i, 0)
    row = pl.BlockSpec((tm, LW), rev)
    prev8 = pl.BlockSpec((8, LW), lambda i: (jnp.maximum((nt - 1 - i) * (tm // 8) - 1, 0), 0))
    full = lambda a: pl.BlockSpec(a.shape, lambda i: (0,) * a.ndim)
    vec = pl.BlockSpec((1, LW), lambda i: (0, 0))
    bd = pl.BlockSpec((NGRP, 256, 256), lambda i: (0, 0, 0))
    return pl.pallas_call(
        body, name="lru_bwd", grid=(nt,),
        in_specs=[row, row, prev8, row, row, prev8, full(conv_w), full(wr), full(wi), full(br), full(bi), full(lam)],
        out_specs=[row, bd, bd, vec, vec, vec, vec, pl.BlockSpec((CONVW, LW), lambda i: (0, 0))],
        out_shape=[jax.ShapeDtypeStruct((s, LW), bf16),
                   jax.ShapeDtypeStruct((NGRP, 256, 256), f32), jax.ShapeDtypeStruct((NGRP, 256, 256), f32),
                   jax.ShapeDtypeStruct((1, LW), f32), jax.ShapeDtypeStruct((1, LW), f32),
                   jax.ShapeDtypeStruct((1, LW), f32), jax.ShapeDtypeStruct((1, LW), f32),
                   jax.ShapeDtypeStruct((CONVW, LW), f32)],
        scratch_shapes=[pltpu.VMEM((tm, LW), f32), pltpu.VMEM((tm, LW), f32),
                        pltpu.VMEM((1, LW), f32), pltpu.VMEM((8, LW), f32)],
        compiler_params=_params(("arbitrary",), 56),
    )(u, hl, hl, dhl, xl, xl, conv_w, wr, wi, br, bi, lam)


def _gated_norm(t, gate, gain):
    sg = _sigmoid(gate)
    silu = gate * sg
    p = t * silu
    rstd = lax.rsqrt(jnp.mean(p * p, axis=-1, keepdims=True) + EPS)
    ph = p * rstd
    return sg, silu, rstd, ph, ph * gain


def _gated_norm_bwd(dy, t, gate, gain, sg, silu, rstd, ph):
    w = dy * gain
    dp = rstd * (w - ph * jnp.mean(w * ph, axis=-1, keepdims=True))
    dgate = dp * t * (sg * (1.0 + gate * (1.0 - sg)))
    return jnp.sum(dy * ph, axis=0, keepdims=True), dp * silu, dgate


def _out_fwd_bwd(x, tgt, o, ga, hl, gl, again, lgain, fgain, wo, tm):
    s = x.shape[0]
    nt = s // tm

    def body(x_ref, t_ref, o_ref, ga_ref, hl_ref, gl_ref, ag_ref, lg_ref, fg_ref, wo_ref,
             dx2_ref, do_ref, dga_ref, dhl_ref, dgl_ref, dwo_ref, gfg_ref, gag_ref, glg_ref, loss_ref, acc):
        i = pl.program_id(0)

        @pl.when(i == 0)
        def _():
            acc[...] = jnp.zeros_like(acc)
            for ref in (gfg_ref, gag_ref, glg_ref, loss_ref):
                ref[...] = jnp.zeros_like(ref)

        oo, gga, hh, ggl = o_ref[...], ga_ref[...], hl_ref[...], gl_ref[...]
        ag, lg, fg = ag_ref[...], lg_ref[...], fg_ref[...]
        sga, silua, ra, pah, ya = _gated_norm(oo, gga, ag)
        sgl, silul, rl, plh, yl = _gated_norm(hh, ggl, lg)
        yab, ylb = ya.astype(bf16), yl.astype(bf16)
        y = _dot(yab, wo_ref[0:D, :]) + _dot(ylb, wo_ref[D:2 * D, :])
        x2 = x_ref[...] + y
        r2 = lax.rsqrt(jnp.mean(x2 * x2, axis=-1, keepdims=True) + EPS)
        x2h = x2 * r2
        err = x2h * fg - t_ref[...]
        loss_ref[...] += 0.5 * jnp.sum(jnp.sum(err * err, axis=-1, keepdims=True) * (1.0 / D))
        dout = err * (1.0 / D)
        gfg_ref[...] += jnp.sum(dout * x2h, axis=0, keepdims=True)
        w = dout * fg
        dx2 = r2 * (w - x2h * jnp.mean(w * x2h, axis=-1, keepdims=True))
        dx2_ref[...] = dx2
        dyb = dx2.astype(bf16)
        acc[0:D, :] += _dot_tn(yab, dyb)
        acc[D:2 * D, :] += _dot_tn(ylb, dyb)
        dya = _dot_nt(dyb, wo_ref[0:D, :])
        dyl = _dot_nt(dyb, wo_ref[D:2 * D, :])
        gag, do, dga = _gated_norm_bwd(dya, oo, gga, ag, sga, silua, ra, pah)
        glg, dhl, dgl = _gated_norm_bwd(dyl, hh, ggl, lg, sgl, silul, rl, plh)
        gag_ref[...] += gag
        glg_ref[...] += glg
        do_ref[...] = do.astype(bf16)
        dga_ref[...] = dga.astype(bf16)
        dhl_ref[...] = dhl
        dgl_ref[...] = dgl.astype(bf16)

        @pl.when(i == nt - 1)
        def _():
            dwo_ref[...] = acc[...].astype(bf16)

    row = pl.BlockSpec((tm, D), lambda i: (i, 0))
    vec = pl.BlockSpec((1, D), lambda i: (0, 0))
    mat = pl.BlockSpec((2 * D, D), lambda i: (0, 0))
    return pl.pallas_call(
        body, name="out_fwd_bwd", grid=(nt,),
        in_specs=[row] * 6 + [vec] * 3 + [mat],
        out_specs=[row] * 5 + [mat, vec, vec, vec, pl.BlockSpec((1, 128), lambda i: (0, 0))],
        out_shape=[jax.ShapeDtypeStruct((s, D), f32), jax.ShapeDtypeStruct((s, D), bf16),
                   jax.ShapeDtypeStruct((s, D), bf16), jax.ShapeDtypeStruct((s, D), f32),
                   jax.ShapeDtypeStruct((s, D), bf16), jax.ShapeDtypeStruct((2 * D, D), bf16),
                   jax.ShapeDtypeStruct((1, D), f32), jax.ShapeDtypeStruct((1, D), f32),
                   jax.ShapeDtypeStruct((1, D), f32), jax.ShapeDtypeStruct((1, 128), f32)],
        scratch_shapes=[pltpu.VMEM((2 * D, D), f32)],
        compiler_params=_params(("arbitrary",), 56),
    )(x, tgt, o, ga, hl, gl, again, lgain, fgain, wo)


def _bwd_in(x, dx2, dq, dk, dv, dga, dxl, dgl, ln_gain, wt, tabs, tm):
    s = x.shape[0]

    def body(x_ref, dx2_ref, dq_ref, dk_ref, dv_ref, dga_ref, dxl_ref, dgl_ref, g_ref, wt_ref,
             c_ref, sa_ref, sb_ref, gx_ref, gln_ref, dzt_ref, dz_scr):
        @pl.when(pl.program_id(0) == 0)
        def _():
            gln_ref[...] = jnp.zeros_like(gln_ref)

        c, sa, sb = c_ref[...], sa_ref[...], sb_ref[...]
        for j in range(D // 128):
            js = slice(128 * j, 128 * j + 128)
            dz_scr[:, js] = (_unrope(dq_ref[:, js], c, sa, sb) * (HD ** -0.5)).astype(bf16)
        for j in range(KVW // 128):
            js = slice(128 * j, 128 * j + 128)
            dz_scr[:, D + 128 * j:D + 128 * j + 128] = _unrope(dk_ref[:, js], c, sa, sb).astype(bf16)
        dz_scr[:, D + KVW:D + 2 * KVW] = dv_ref[...].astype(bf16)
        dz_scr[:, 1536:2560] = dga_ref[...]
        dz_scr[:, 2560:3584] = dxl_ref[...]
        dz_scr[:, 3584:4608] = dgl_ref[...]
        for j in range(NIN // 128):
            dzt_ref[128 * j:128 * j + 128, :] = dz_scr[:, 128 * j:128 * j + 128].T
        dh = _dot(dz_scr[:, 0:512], wt_ref[0:512, :])
        for ci in range(1, NIN // 512):
            dh = dh + _dot(dz_scr[:, 512 * ci:512 * ci + 512], wt_ref[512 * ci:512 * ci + 512, :])
        xx = x_ref[...]
        rstd = lax.rsqrt(jnp.mean(xx * xx, axis=-1, keepdims=True) + EPS)
        xh = xx * rstd
        gln_ref[...] += jnp.sum(dh * xh, axis=0, keepdims=True)
        w = dh * g_ref[...]
        gx_ref[...] = dx2_ref[...] + rstd * (w - xh * jnp.mean(w * xh, axis=-1, keepdims=True))

    row = lambda w: pl.BlockSpec((tm, w), lambda i: (i, 0))
    full = lambda a: pl.BlockSpec(a.shape, lambda i: (0, 0))
    return pl.pallas_call(
        body, name="bwd_in", grid=(s // tm,),
        in_specs=[row(D), row(D), row(D), row(KVW), row(KVW), row(D), row(D), row(D), full(ln_gain), full(wt),
                  row(128), row(128), row(128)],
        out_specs=[row(D), pl.BlockSpec((1, D), lambda i: (0, 0)), pl.BlockSpec((NIN, tm), lambda i: (0, i))],
        out_shape=[jax.ShapeDtypeStruct((s, D), f32), jax.ShapeDtypeStruct((1, D), f32),
                   jax.ShapeDtypeStruct((NIN, s), bf16)],
        scratch_shapes=[pltpu.VMEM((tm, NIN), bf16)],
        compiler_params=_params(("arbitrary",), 56),
    )(x, dx2, dq, dk, dv, dga, dxl, dgl, ln_gain, wt, *tabs)


WT_TERMS = 5


def _dwt_scatter(dzt, h, small, tm):
    s = h.shape[0]
    nk = s // tm
    srows = small.shape[0] // NDEV
    last = NDEV - 1

    def body(order_ref, dz_ref, h_ref, sm_ref, lwt_ref, lsm_ref, acc, stage, given, send_sems, recv_sems, local_sem,
             sm_send, sm_recv, sm_local):
        j, k = pl.program_id(0), pl.program_id(1)
        x, y, c = _place()
        sibling = (x, y, 1 - c)
        chips = [(1 - x, 1 - y), (1 - x, y), (x, 1 - y)]
        sm_start, sm_finish = _scatter_ops([sm_ref], [lsm_ref], sm_send, sm_recv, sm_local)

        def send(step):
            if step == last - 1:
                dst, to = lwt_ref.at[1], sibling
            elif step % 2 == 0:
                dst, to = given.at[step // 2], sibling
            else:
                dst, to = lwt_ref.at[2 + step // 2], (*chips[step // 2], c)
            return pltpu.make_async_remote_copy(
                src_ref=stage.at[step % 2], dst_ref=dst, send_sem=send_sems.at[step], recv_sem=recv_sems.at[step],
                device_id=to, device_id_type=MESH)

        def keep():
            return pltpu.make_async_copy(stage.at[last % 2], lwt_ref.at[0], local_sem)

        @pl.when((j == 0) & (k == 0))
        def _():
            sm_start()

        @pl.when(k == 0)
        def _():
            acc[...] = jnp.zeros_like(acc)

        acc[...] += _dot(dz_ref[...], h_ref[...])

        for step in range(NDEV):
            @pl.when((k == nk - 1) & (j == step))
            def _(step=step):
                if step >= 2:
                    send(step - 2).wait_send()
                if step % 2 == 1 and step < last:
                    send(step - 1).wait_recv()
                    stage[step % 2] = (acc[...] + given[step // 2].astype(f32)).astype(bf16)
                else:
                    stage[step % 2] = acc[...].astype(bf16)
                if step < last:
                    send(step).start()
                else:
                    keep().start()
                    send(last - 1).wait_send()
                    for peer_step in (1, 3, 5, last - 1):
                        send(peer_step).wait_recv()
                    keep().wait()
                    sm_finish()

    x, y, c = _place()
    dest = lambda cx, cy, cc: 4 * cx + 2 * cy + cc
    order = jnp.stack([dest(1 - x, 1 - y, 1 - c), dest(1 - x, 1 - y, c), dest(1 - x, y, 1 - c), dest(1 - x, y, c),
                       dest(x, 1 - y, 1 - c), dest(x, 1 - y, c), dest(x, y, 1 - c), dest(x, y, c)])
    return pl.pallas_call(
        body, name="dwt_scatter",
        grid_spec=pltpu.PrefetchScalarGridSpec(
            num_scalar_prefetch=1, grid=(NDEV, nk),
            in_specs=[pl.BlockSpec((WT_ROWS, tm), lambda j, k, order: (order[j], k)),
                      pl.BlockSpec((tm, D), lambda j, k, order: (k, 0)), HBM],
            out_specs=[HBM, HBM],
            scratch_shapes=[pltpu.VMEM((WT_ROWS, D), f32), pltpu.VMEM((2, WT_ROWS, D), bf16),
                            pltpu.VMEM((3, WT_ROWS, D), bf16),
                            pltpu.SemaphoreType.DMA((last,)), pltpu.SemaphoreType.DMA((last,)),
                            pltpu.SemaphoreType.DMA(())] + _comm_sems(1)),
        out_shape=[jax.ShapeDtypeStruct((WT_TERMS, WT_ROWS, D), bf16), jax.ShapeDtypeStruct((NDEV, srows, D), f32)],
        compiler_params=_params(("arbitrary", "arbitrary"), 32),
    )(order, dzt, h, small)


def _diag_blocks(bd):
    eye = jnp.eye(4, dtype=bd.dtype)
    return jnp.einsum('gjckd,jk->gjcd', bd.reshape(NGRP, 4, HD, 4, HD), eye).reshape(NQ, HD, HD)


def _sequence_step(x, tgt, wt, wo_shard, conv_w, p):
    s = x.shape[0]
    tm = min(256, s)
    tabs = _rope_tables(s)
    wr, wi = _block_diag(p["w_rgate"]), _block_diag(p["w_igate"])
    sinks = p["sinks"].reshape(NQ)
    h, q, k, v, ga, xl, gl, wo = _fwd_in(x, p["ln_gain"], wt, tabs, wo_shard, tm)
    o = _attn_fwd(q, k, v, sinks)
    u, hl = _lru_fwd(xl, conv_w, p["conv_b"], wr, wi, p["b_rgate"], p["b_igate"], p["lru_lambda"], tm)
    dx2, do, dga, dhl, dgl, dwo, g_fg, g_ag, g_lg, loss = _out_fwd_bwd(
        x, tgt, o, ga, hl, gl, p["attn_out_gain"], p["lru_out_gain"], p["final_gain"], wo, tm)
    dq, dk, dv, dsink, land_wo = _attn_bwd(q, k, v, do, sinks, dwo)
    dxl, dwr, dwi, dbr, dbi, dlam, dcb, dcw = _lru_bwd(
        u, hl, dhl, xl, conv_w, wr, wi, p["b_rgate"], p["b_igate"], p["lru_lambda"], tm)
    gx, g_ln, dzt = _bwd_in(x, dx2, dq, dk, dv, dga, dxl, dgl, p["ln_gain"], wt, tabs, tm)
    small = dict(ln_gain=g_ln, sinks=dsink.reshape(NQ, BLK).sum(axis=1)[None], conv_w=dcw, conv_b=dcb,
                 w_rgate=_diag_blocks(dwr), b_rgate=dbr, w_igate=_diag_blocks(dwi), b_igate=dbi, lru_lambda=dlam,
                 attn_out_gain=g_ag, lru_out_gain=g_lg, final_gain=g_fg)
    land_wt, land_sm = _dwt_scatter(dzt, h, _pack_small(small, loss), min(512, s))
    return gx, land_wt, land_wo, land_sm


def _all_gather(srcs, out_dtypes, name):
    n = len(srcs)
    cast = [a.dtype != dt for a, dt in zip(srcs, out_dtypes)]

    def body(*refs):
        src_refs, out_refs = refs[:n], refs[n:2 * n]
        stage_refs = list(refs[2 * n:2 * n + sum(cast)])
        mine_refs = []
        for a in range(n):
            if cast[a]:
                st = stage_refs.pop(0)
                st[...] = src_refs[a][...].astype(out_dtypes[a])
                mine_refs.append(st)
            else:
                mine_refs.append(src_refs[a])
        start, finish = _gather_ops(mine_refs, out_refs, *refs[-3:])
        start()
        finish()

    vmem = pl.BlockSpec(memory_space=pltpu.VMEM)
    return pl.pallas_call(
        body, name=name,
        in_specs=[vmem] * n, out_specs=[HBM] * n,
        out_shape=[jax.ShapeDtypeStruct((NDEV * a.shape[0], a.shape[1]), dt) for a, dt in zip(srcs, out_dtypes)],
        scratch_shapes=[pltpu.VMEM(a.shape, dt) for a, dt, cst in zip(srcs, out_dtypes, cast) if cst] + _comm_sems(n),
        compiler_params=pltpu.CompilerParams(vmem_limit_bytes=32 * MIB),
    )(*srcs)


def _sum_slots(land, tr, name):
    terms, rows, cols = land.shape

    def body(l_ref, o_ref):
        acc = l_ref[0].astype(f32)
        for d in range(1, terms):
            acc = acc + l_ref[d].astype(f32)
        o_ref[...] = acc

    return pl.pallas_call(
        body, name=name, grid=(rows // tr,),
        in_specs=[pl.BlockSpec((terms, tr, cols), lambda i: (0, i, 0))],
        out_specs=pl.BlockSpec((tr, cols), lambda i: (i, 0)),
        out_shape=jax.ShapeDtypeStruct((rows, cols), f32),
        compiler_params=_params(("arbitrary",), 32),
    )(land)


def _adam_math(w, g, m, v):
    m2 = ADAM_B1 * m + (1.0 - ADAM_B1) * g
    v2 = ADAM_B2 * v + (1.0 - ADAM_B2) * (g * g)
    m_hat = m2 / (1.0 - ADAM_B1 ** ADAM_STEP)
    v_hat = v2 / (1.0 - ADAM_B2 ** ADAM_STEP)
    delta = -ADAM_LR * (m_hat / (jnp.sqrt(v_hat) + ADAM_EPS) + ADAM_WD * w)
    return delta, m2, v2


def _adamw(w, g, m, v, tr, name):
    rows, cols = w.shape

    def body(w_ref, g_ref, m_ref, v_ref, d_ref, m2_ref, v2_ref):
        d_ref[...], m2_ref[...], v2_ref[...] = _adam_math(w_ref[...], g_ref[...], m_ref[...], v_ref[...])

    blk = pl.BlockSpec((tr, cols), lambda i: (i, 0))
    return pl.pallas_call(
        body, name=name, grid=(rows // tr,),
        in_specs=[blk] * 4, out_specs=[blk] * 3,
        out_shape=[jax.ShapeDtypeStruct((rows, cols), f32)] * 3,
        compiler_params=_params(("arbitrary",), 32),
    )(w, g, m, v)


VEC_NAMES = ("ln_gain", "conv_b", "b_rgate", "b_igate", "lru_lambda", "attn_out_gain", "lru_out_gain", "final_gain")
ROW_RGATE, ROW_IGATE, ROW_VEC, ROW_SINKS = 0, 64, 128, 136
LOSS_LANE = NQ


def _adamw_small(g_rep, g_conv, w, m, v):
    names = list(VEC_NAMES) + ["sinks", "conv_w", "w_rgate", "w_igate"]
    ins = [g_rep, g_conv] + [d[k] for k in names for d in (w, m, v)]

    def body(*refs):
        g_ref, gc_ref = refs[0], refs[1]
        in_refs = refs[2:2 + 3 * len(names)]
        out_refs = refs[2 + 3 * len(names):]

        def update(j, g, at=None):
            w_ref, m_ref, v_ref = in_refs[3 * j:3 * j + 3]
            outs = out_refs[4 * j:4 * j + 4]
            pick = (lambda r: r[...]) if at is None else (lambda r: r[at])
            res = (g,) + _adam_math(pick(w_ref), g, pick(m_ref), pick(v_ref))
            for o_ref, val in zip(outs, res):
                if at is None:
                    o_ref[...] = val
                else:
                    o_ref[at] = val

        for j in range(len(VEC_NAMES)):
            update(j, g_ref[ROW_VEC + j:ROW_VEC + j + 1, :])
        update(len(VEC_NAMES), g_ref[ROW_SINKS:ROW_SINKS + 1, 0:NQ])
        update(len(VEC_NAMES) + 1, gc_ref[...], at=0)
        for gi, row0 in ((len(VEC_NAMES) + 2, ROW_RGATE), (len(VEC_NAMES) + 3, ROW_IGATE)):
            for nb in range(NQ):
                update(gi, g_ref[row0:row0 + HD, HD * nb:HD * nb + HD], at=(0, nb))

    vmem = pl.BlockSpec(memory_space=pltpu.VMEM)
    out_shape = [jax.ShapeDtypeStruct(w[k].shape, f32) for k in names for _ in range(4)]
    outs = pl.pallas_call(
        body, name="adamw_small",
        in_specs=[vmem] * len(ins), out_specs=[vmem] * len(out_shape), out_shape=out_shape,
        compiler_params=pltpu.CompilerParams(vmem_limit_bytes=32 * MIB),
    )(*ins)
    return {k: tuple(outs[4 * j:4 * j + 4]) for j, k in enumerate(names)}


def _pack_small(small, loss):
    gate = lambda g: g.transpose(1, 0, 2).reshape(HD, NQ * HD)
    row_s = jnp.concatenate([small["sinks"], loss[:, LOSS_LANE:128], jnp.zeros((1, D - 128), f32)], axis=1)
    rep = jnp.concatenate([gate(small["w_rgate"]), gate(small["w_igate"])] + [small[k] for k in VEC_NAMES]
                          + [row_s, jnp.zeros((SMALL_ROWS - ROW_SINKS - 1, D), f32)], axis=0)
    conv = small["conv_w"].reshape(CONVW, NDEV, 128).transpose(1, 0, 2)
    conv = jnp.pad(conv, ((0, 0), (0, 8 - CONVW), (0, D - 128)))
    return jnp.concatenate([rep.reshape(NDEV, SMALL_PER, D), conv], axis=1).reshape(NDEV * (SMALL_PER + 8), D)


def kernel(x, ln_gain, w_in, sinks, conv_w, conv_b, w_rgate, b_rgate, w_igate, b_igate, lru_lambda, attn_out_gain, lru_out_gain, w_out, final_gain, loss_target, m_ln_gain, m_w_in, m_sinks, m_conv_w, m_conv_b, m_w_rgate, m_b_rgate, m_w_igate, m_b_igate, m_lru_lambda, m_attn_out_gain, m_lru_out_gain, m_w_out, m_final_gain, v_ln_gain, v_w_in, v_sinks, v_conv_w, v_conv_b, v_w_rgate, v_b_rgate, v_w_igate, v_b_igate, v_lru_lambda, v_attn_out_gain, v_lru_out_gain, v_w_out, v_final_gain):
    w = dict(ln_gain=ln_gain, sinks=sinks, conv_w=conv_w, conv_b=conv_b, w_rgate=w_rgate, b_rgate=b_rgate,
             w_igate=w_igate, b_igate=b_igate, lru_lambda=lru_lambda, attn_out_gain=attn_out_gain,
             lru_out_gain=lru_out_gain, final_gain=final_gain.reshape(1, D))
    m = dict(ln_gain=m_ln_gain, sinks=m_sinks, conv_w=m_conv_w, conv_b=m_conv_b, w_rgate=m_w_rgate,
             b_rgate=m_b_rgate, w_igate=m_w_igate, b_igate=m_b_igate, lru_lambda=m_lru_lambda,
             attn_out_gain=m_attn_out_gain, lru_out_gain=m_lru_out_gain, final_gain=m_final_gain.reshape(1, D))
    v = dict(ln_gain=v_ln_gain, sinks=v_sinks, conv_w=v_conv_w, conv_b=v_conv_b, w_rgate=v_w_rgate,
             b_rgate=v_b_rgate, w_igate=v_w_igate, b_igate=v_b_igate, lru_lambda=v_lru_lambda,
             attn_out_gain=v_attn_out_gain, lru_out_gain=v_lru_out_gain, final_gain=v_final_gain.reshape(1, D))

    conv_blk = jnp.pad(conv_w[0], ((0, 8 - CONVW), (0, 0)))
    wt, cw_all = _all_gather([w_in[0].T, conv_blk], [bf16, f32], "gather_weights")
    conv_full = cw_all.reshape(NDEV, 8, 128)[:, 0:CONVW].transpose(1, 0, 2).reshape(CONVW, LW)

    p = {k: (w[k][0] if k in ("w_rgate", "w_igate") else w[k]) for k in w if k != "conv_w"}
    gx, land_wt, land_wo, land_sm = _sequence_step(x[0], loss_target[0], wt, w_out[0], conv_full, p)

    g_wt = _sum_slots(land_wt, 192, "sum_wt")
    g_wo = _sum_slots(land_wo, 256, "sum_wo")
    g_sm = _sum_slots(land_sm, SMALL_PER + 8, "sum_small")
    (g_rep,) = _all_gather([g_sm[0:SMALL_PER]], [f32], "gather_small")
    g_conv = g_sm[SMALL_PER:SMALL_PER + CONVW, 0:128]

    d_win, m_win, v_win = _adamw(w_in[0].T, g_wt, m_w_in[0].T, v_w_in[0].T, 192, "adamw_w_in")
    g_win, d_win, m_win, v_win = (t.T for t in (g_wt, d_win, m_win, v_win))
    d_wo, m_wo, v_wo = _adamw(w_out[0], g_wo, m_w_out[0], v_w_out[0], 256, "adamw_w_out")
    res = _adamw_small(g_rep, g_conv, w, m, v)
    res["w_in"] = tuple(t[None] for t in (g_win, d_win, m_win, v_win))
    res["w_out"] = tuple(t[None] for t in (g_wo, d_wo, m_wo, v_wo))
    res["final_gain"] = tuple(t.reshape(D) for t in res["final_gain"])

    order = ("ln_gain", "w_in", "sinks", "conv_w", "conv_b", "w_rgate", "b_rgate", "w_igate", "b_igate",
             "lru_lambda", "attn_out_gain", "lru_out_gain", "w_out", "final_gain")
    total_loss = g_rep[ROW_SINKS, LOSS_LANE]
    return (total_loss, gx[None]) + tuple(res[k][i] for i in range(4) for k in order)
```

```python
import jax
import jax.numpy as jnp
from jax import lax
from jax.experimental import pallas as pl
from jax.experimental.pallas import tpu as pltpu

f32 = jnp.float32
bf16 = jnp.bfloat16

D = 1024
HD = 64
NQ = 16
NKV = 4
GROUP = NQ // NKV
KVW = NKV * HD
BLK = 128
ROT = 16
THETA = 500000.0
NEG = -1e30
LW = 1024
NGRP = 4
CONVW = 4
LRU_C = 8.0
NIN = 4608
EPS = 1e-6
NDEV = 8
WT_ROWS = NIN // NDEV
WO_ROWS = 2 * D // NDEV
SMALL_ROWS = 192
SMALL_PER = SMALL_ROWS // NDEV

ADAM_LR = 0.001
ADAM_B1 = 0.9
ADAM_B2 = 0.999
ADAM_EPS = 1e-08
ADAM_WD = 0.01
ADAM_STEP = 10

NT = (((1,), (1,)), ((), ()))
TN = (((0,), (0,)), ((), ()))
MESH = pl.DeviceIdType.MESH
MIB = 1024 * 1024


def _dot(a, b):
    return jnp.dot(a, b, preferred_element_type=f32)


def _dot_nt(a, b):
    return lax.dot_general(a, b, NT, preferred_element_type=f32)


def _dot_tn(a, b):
    return lax.dot_general(a, b, TN, preferred_element_type=f32)


def _params(sem, vmem_mib):
    return pltpu.CompilerParams(dimension_semantics=sem, vmem_limit_bytes=vmem_mib * MIB)


def _sigmoid(x):
    return 0.5 * jnp.tanh(0.5 * x) + 0.5


def _softplus(x):
    return jnp.maximum(x, 0.0) + jnp.log(1.0 + jnp.exp(-jnp.abs(x)))


def _rope_tables(s):
    pos = jnp.arange(s, dtype=f32)
    inv_freq = THETA ** (-jnp.arange(0, ROT, 2, dtype=f32) / ROT)
    ang = pos[:, None] * inv_freq[None, :]
    cs = jnp.concatenate([jnp.cos(ang) - 1.0, jnp.sin(ang)], axis=1)
    d = jnp.arange(128) % HD
    j = jnp.arange(ROT)[:, None]
    pick_c = ((d < ROT) & (j == d % (ROT // 2))).astype(f32)
    pick_sa = ((d >= ROT // 2) & (d < ROT) & (j == d)).astype(f32)
    pick_sb = -((d < ROT // 2) & (j == d + ROT // 2)).astype(f32)
    spread = lambda pick: jnp.dot(cs, pick, precision=lax.Precision.HIGHEST)
    return 1.0 + spread(pick_c), spread(pick_sa), spread(pick_sb)


def _rope(t, c, sa, sb):
    return t * c + pltpu.roll(t, 8, 1) * sa + pltpu.roll(t, 120, 1) * sb


def _unrope(dr, c, sa, sb):
    return dr * c + pltpu.roll(dr * sa, 120, 1) + pltpu.roll(dr * sb, 8, 1)


def _place():
    return lax.axis_index("x"), lax.axis_index("y"), lax.axis_index("c")


def _gather_ops(mine_refs, out_refs, send_sems, recv_sems, local_sems):
    n = len(mine_refs)
    x, y, c = _place()
    me, sibling = (x, y, c), (x, y, 1 - c)
    chips = [(1 - x, y), (x, 1 - y), (1 - x, 1 - y)]

    def rows(a, dev):
        m = mine_refs[a].shape[0]
        return out_refs[a].at[pl.ds((4 * dev[0] + 2 * dev[1] + dev[2]) * m, m), :]

    def copy(a, k, block, to, own=False):
        return pltpu.make_async_remote_copy(
            src_ref=mine_refs[a] if own else rows(a, block), dst_ref=rows(a, block),
            send_sem=send_sems.at[a, k], recv_sem=recv_sems.at[a, k], device_id=to, device_id_type=MESH)

    def local(a):
        return pltpu.make_async_copy(mine_refs[a], rows(a, me), local_sems.at[a])

    def first(a):
        return [copy(a, 0, me, sibling, own=True)] + [copy(a, 1 + j, me, (*chip, c), own=True)
                                                      for j, chip in enumerate(chips)]

    def start():
        for a in range(n):
            local(a).start()
            for cp in first(a):
                cp.start()

    def finish():
        for j, chip in enumerate(chips):
            for a in range(n):
                copy(a, 1 + j, (*chip, c), me).wait_recv()
                copy(a, 4 + j, (*chip, c), sibling).start()
        for a in range(n):
            copy(a, 0, sibling, me).wait_recv()
            for j, chip in enumerate(chips):
                copy(a, 4 + j, (*chip, 1 - c), me).wait_recv()
        for a in range(n):
            for cp in first(a) + [copy(a, 4 + j, (*chip, c), sibling) for j, chip in enumerate(chips)]:
                cp.wait_send()
            local(a).wait()

    return start, finish


def _scatter_ops(src_refs, land_refs, send_sems, recv_sems, local_sems):
    n = len(src_refs)
    x, y, c = _place()
    my = 4 * x + 2 * y + c

    def peer(k):
        return x ^ (k >> 2), y ^ ((k >> 1) & 1), c ^ (k & 1)

    def piece(a, dev):
        m = src_refs[a].shape[0] // NDEV
        return src_refs[a].at[pl.ds(dev * m, m), :]

    def local(a):
        return pltpu.make_async_copy(piece(a, my), land_refs[a].at[my], local_sems.at[a])

    def send(a, k):
        px, py, pc = peer(k)
        return pltpu.make_async_remote_copy(
            src_ref=piece(a, 4 * px + 2 * py + pc), dst_ref=land_refs[a].at[my],
            send_sem=send_sems.at[a, k - 1], recv_sem=recv_sems.at[a, k - 1],
            device_id=(px, py, pc), device_id_type=MESH)

    def arrival(a, k):
        px, py, pc = peer(k)
        return pltpu.make_async_remote_copy(
            src_ref=piece(a, my), dst_ref=land_refs[a].at[4 * px + 2 * py + pc],
            send_sem=send_sems.at[a, k - 1], recv_sem=recv_sems.at[a, k - 1],
            device_id=(px, py, pc), device_id_type=MESH)

    def start():
        for a in range(n):
            local(a).start()
        for k in range(1, NDEV):
            for a in range(n):
                send(a, k).start()

    def finish():
        for k in range(1, NDEV):
            for a in range(n):
                send(a, k).wait_send()
        for k in range(1, NDEV):
            for a in range(n):
                arrival(a, k).wait_recv()
        for a in range(n):
            local(a).wait()

    return start, finish


def _in_hbm(*arrays):
    return tuple(pltpu.with_memory_space_constraint(a, pltpu.HBM) for a in arrays)


def _comm_sems(n):
    return [pltpu.SemaphoreType.DMA((n, 7)), pltpu.SemaphoreType.DMA((n, 7)), pltpu.SemaphoreType.DMA((n,))]


HBM = pl.BlockSpec(memory_space=pltpu.HBM)


def _fwd_in(x, ln_gain, wt, tabs, wo_shard, tm):
    s = x.shape[0]
    nt = s // tm
    nc = 512

    def body(x_ref, g_ref, wt_ref, c_ref, sa_ref, sb_ref, wo_ref, h_ref, q_ref, k_ref, v_ref, ga_ref, xl_ref, gl_ref,
             wo_all, wo_stage, send_sems, recv_sems, local_sems):
        i = pl.program_id(0)
        start, finish = _gather_ops([wo_stage], [wo_all], send_sems, recv_sems, local_sems)

        @pl.when(i == 0)
        def _():
            wo_stage[...] = wo_ref[...].astype(bf16)
            start()

        xx = x_ref[...]
        rstd = lax.rsqrt(jnp.mean(xx * xx, axis=-1, keepdims=True) + EPS)
        h = (xx * rstd * g_ref[...]).astype(bf16)
        h_ref[...] = h
        c, sa, sb = c_ref[...], sa_ref[...], sb_ref[...]

        def z_chunk(ci):
            return _dot_nt(h, wt_ref[ci * nc:(ci + 1) * nc, :])

        for ci in range(2):
            z = z_chunk(ci)
            for j in range(nc // 128):
                r = _rope(z[:, 128 * j:128 * j + 128], c, sa, sb) * (HD ** -0.5)
                q_ref[:, ci * nc + 128 * j:ci * nc + 128 * j + 128] = r.astype(bf16)
        z = z_chunk(2)
        for j in range(2):
            k_ref[:, 128 * j:128 * j + 128] = _rope(z[:, 128 * j:128 * j + 128], c, sa, sb).astype(bf16)
        v_ref[...] = z[:, 256:512].astype(bf16)
        for sec, ref in enumerate((ga_ref, xl_ref, gl_ref)):
            for j in range(2):
                ref[:, j * nc:(j + 1) * nc] = z_chunk(3 + 2 * sec + j)

        @pl.when(i == nt - 1)
        def _():
            finish()

    row = lambda w: pl.BlockSpec((tm, w), lambda i: (i, 0))
    full = lambda a: pl.BlockSpec(a.shape, lambda i: (0, 0))
    return pl.pallas_call(
        body, name="fwd_in", grid=(nt,),
        in_specs=[row(D), full(ln_gain), full(wt), row(128), row(128), row(128), full(wo_shard)],
        out_specs=[row(D), row(D), row(KVW), row(KVW), row(D), row(D), row(D), HBM],
        out_shape=[pltpu.HBM((s,D), bf16), pltpu.HBM((s,D), bf16),
                   pltpu.HBM((s,KVW), bf16), pltpu.HBM((s,KVW), bf16),
                   pltpu.HBM((s,D), f32), pltpu.HBM((s,D), f32),
                   pltpu.HBM((s,D), f32), pltpu.HBM((2 * D, D), bf16)],
        scratch_shapes=[pltpu.VMEM((WO_ROWS, D), bf16)] + _comm_sems(1),
        compiler_params=_params(("arbitrary",), 48),
    )(*_in_hbm(x), ln_gain, *_in_hbm(wt), *tabs, wo_shard)


HSUB = 4
SUBW = HSUB * BLK


def _sub_probs(kh, qg, n, sink_row):
    jj = lax.broadcasted_iota(jnp.int32, (BLK, SUBW), 0)
    ii = lax.broadcasted_iota(jnp.int32, (BLK, SUBW), 1) % BLK
    from_prev = jj > ii
    s2 = _dot_nt(kh, qg)
    sc = jnp.where(from_prev, s2[0:BLK] + jnp.where(n > 0, 0.0, NEG), s2[BLK:2 * BLK])
    m = jnp.maximum(jnp.max(sc, axis=0, keepdims=True), sink_row)
    p = jnp.exp(sc - m)
    es = jnp.exp(sink_row - m)
    inv = 1.0 / (jnp.sum(p, axis=0, keepdims=True) + es)
    return from_prev, p * inv, es * inv


def _split(t, from_prev):
    t = t.astype(bf16)
    zero = jnp.zeros_like(t)
    return jnp.concatenate([jnp.where(from_prev, t, zero), jnp.where(from_prev, zero, t)], axis=0)


def _stack_heads(ref, first):
    return jnp.concatenate([ref[:, HD * (first + g):HD * (first + g) + HD] for g in range(HSUB)], axis=0)


def _sink_rows(sinks):
    return jnp.repeat(sinks.reshape(NKV, GROUP), BLK, axis=1)


def _kv_specs():
    prev = pl.BlockSpec((BLK, KVW), lambda n: (jnp.maximum(n - 1, 0), 0))
    cur = pl.BlockSpec((BLK, KVW), lambda n: (n, 0))
    return [prev, cur, prev, cur]


def _attn_fwd(q, k, v, sinks):
    s = q.shape[0]

    def body(sink_ref, q_ref, kp_ref, kc_ref, vp_ref, vc_ref, o_ref):
        n = pl.program_id(0)
        for h in range(NKV):
            hs = slice(HD * h, HD * h + HD)
            kh = jnp.concatenate([kp_ref[:, hs], kc_ref[:, hs]], axis=0)
            vh = jnp.concatenate([vp_ref[:, hs], vc_ref[:, hs]], axis=0)
            for t in range(GROUP // HSUB):
                first = GROUP * h + HSUB * t
                from_prev, pn, _ = _sub_probs(kh, _stack_heads(q_ref, first), n,
                                              sink_ref[h:h + 1, SUBW * t:SUBW * t + SUBW])
                og = _dot_tn(_split(pn, from_prev), vh)
                for g in range(HSUB):
                    o_ref[:, HD * (first + g):HD * (first + g) + HD] = og[BLK * g:BLK * g + BLK]

    return pl.pallas_call(
        body, name="attn_fwd", grid=(s // BLK,),
        in_specs=[pl.BlockSpec((NKV, GROUP * BLK), lambda n: (0, 0)), pl.BlockSpec((BLK, D), lambda n: (n, 0))]
        + _kv_specs(),
        out_specs=pl.BlockSpec((BLK, D), lambda n: (n, 0)),
        out_shape=pltpu.HBM((s,D), f32),
        compiler_params=_params(("arbitrary",), 32),
    )(_sink_rows(sinks), *_in_hbm(q, k, k, v, v))


def _attn_bwd(q, k, v, do, sinks, dwo):
    s = q.shape[0]
    nb = s // BLK

    def body(sink_ref, q_ref, do_ref, kp_ref, kc_ref, vp_ref, vc_ref, dwo_ref, dq_ref, dk_ref, dv_ref, ds_ref,
             land_ref, send_sems, recv_sems, local_sems):
        n = pl.program_id(0)
        start, finish = _scatter_ops([dwo_ref], [land_ref], send_sems, recv_sems, local_sems)

        @pl.when(n == 0)
        def _():
            start()
            dk_ref[...] = jnp.zeros_like(dk_ref)
            dv_ref[...] = jnp.zeros_like(dv_ref)
            ds_ref[...] = jnp.zeros_like(ds_ref)

        prev_rows = pl.ds(pl.multiple_of(jnp.maximum(n - 1, 0) * BLK, BLK), BLK)
        cur_rows = pl.ds(pl.multiple_of(n * BLK, BLK), BLK)
        for h in range(NKV):
            hs = slice(HD * h, HD * h + HD)
            kh = jnp.concatenate([kp_ref[:, hs], kc_ref[:, hs]], axis=0)
            vh = jnp.concatenate([vp_ref[:, hs], vc_ref[:, hs]], axis=0)
            dkh = jnp.zeros((2 * BLK, HD), f32)
            dvh = jnp.zeros((2 * BLK, HD), f32)
            for t in range(GROUP // HSUB):
                first = GROUP * h + HSUB * t
                lanes = slice(SUBW * t, SUBW * t + SUBW)
                qg, dog = _stack_heads(q_ref, first), _stack_heads(do_ref, first)
                from_prev, pn, ps = _sub_probs(kh, qg, n, sink_ref[h:h + 1, lanes])
                dp2 = _dot_nt(vh, dog)
                dp = jnp.where(from_prev, dp2[0:BLK], dp2[BLK:2 * BLK])
                dsum = jnp.sum(pn * dp, axis=0, keepdims=True)
                ds_ref[h:h + 1, lanes] += -ps * dsum
                ds2 = _split(pn * (dp - dsum), from_prev)
                dqg = _dot_tn(ds2, kh)
                for g in range(HSUB):
                    dq_ref[:, HD * (first + g):HD * (first + g) + HD] = dqg[BLK * g:BLK * g + BLK]
                dkh = dkh + _dot(ds2, qg)
                dvh = dvh + _dot(_split(pn, from_prev), dog)
            dk_ref[prev_rows, hs] += dkh[0:BLK]
            dk_ref[cur_rows, hs] += dkh[BLK:2 * BLK]
            dv_ref[prev_rows, hs] += dvh[0:BLK]
            dv_ref[cur_rows, hs] += dvh[BLK:2 * BLK]

        @pl.when(n == nb - 1)
        def _():
            finish()

    blk = pl.BlockSpec((BLK, D), lambda n: (n, 0))
    whole = lambda r, w: pl.BlockSpec((r, w), lambda n: (0, 0))
    return pl.pallas_call(
        body, name="attn_bwd", grid=(nb,),
        in_specs=[whole(NKV, GROUP * BLK), blk, blk] + _kv_specs() + [HBM],
        out_specs=[blk, whole(s, KVW), whole(s, KVW), whole(NKV, GROUP * BLK), HBM],
        out_shape=[pltpu.HBM((s,D), f32), pltpu.HBM((s,KVW), f32),
                   pltpu.HBM((s,KVW), f32), jax.ShapeDtypeStruct((NKV, GROUP * BLK), f32),
                   pltpu.HBM((NDEV, WO_ROWS, D), bf16)],
        scratch_shapes=_comm_sems(1),
        compiler_params=_params(("arbitrary",), 48),
    )(_sink_rows(sinks), *_in_hbm(q, do, k, k, v, v, dwo))


def _block_diag(w):
    w4 = w.reshape(NGRP, 4, HD, HD)
    eye = jnp.eye(4, dtype=w.dtype)
    return jnp.einsum('gjcd,jk->gjckd', w4, eye).reshape(NGRP, 256, 256).astype(bf16)


def _gates(u, wr_ref, wi_ref, br, bi, sp):
    ub = u.astype(bf16)
    pr = jnp.concatenate([_dot(ub[:, 256 * g:256 * g + 256], wr_ref[g]) for g in range(NGRP)], axis=1)
    pi = jnp.concatenate([_dot(ub[:, 256 * g:256 * g + 256], wi_ref[g]) for g in range(NGRP)], axis=1)
    r = _sigmoid(pr + br)
    i = _sigmoid(pi + bi)
    la = -LRU_C * r * sp
    a = jnp.exp(la)
    x2 = 2.0 * la
    y = jnp.where(x2 > -0.02, -x2 * (1.0 + x2 * (0.5 + x2 * (1.0 / 6.0))), 1.0 - a * a)
    inv_mult = lax.rsqrt(jnp.maximum(y, 1e-30))
    return ub, r, i, a, y * inv_mult, inv_mult


def _later(x, before, k):
    if k == 0:
        return x
    row = lax.broadcasted_iota(jnp.int32, before.shape, 0)
    rolled = pltpu.roll(x, k, 0)
    first = jnp.where(row < k, pltpu.roll(before, k, 0), rolled[0:8])
    return jnp.concatenate([first, rolled[8:]], axis=0)


def _earlier(x, after, k):
    if k == 0:
        return x
    n = x.shape[0]
    row = lax.broadcasted_iota(jnp.int32, after.shape, 0)
    rolled = pltpu.roll(x, n - k, 0)
    last = jnp.where(row >= 8 - k, pltpu.roll(after, 8 - k, 0), rolled[n - 8:n])
    return jnp.concatenate([rolled[0:n - 8], last], axis=0)


def _lru_fwd(xl, conv_w, conv_b, wr, wi, br, bi, lam, tm):
    s = xl.shape[0]

    def body(xp_ref, x_ref, cw_ref, cb_ref, wr_ref, wi_ref, br_ref, bi_ref, lam_ref, u_ref, h_ref,
             a_scr, b_scr, hcar):
        t0 = pl.program_id(0)

        @pl.when(t0 == 0)
        def _():
            hcar[...] = jnp.zeros_like(hcar)

        x = x_ref[...]
        before = jnp.where(t0 > 0, xp_ref[...], 0.0)
        u = cb_ref[...] + sum(cw_ref[k:k + 1, :] * _later(x, before, CONVW - 1 - k) for k in range(CONVW))
        u_ref[...] = u
        sp = _softplus(-lam_ref[...])
        _, _, i, a, mult, _ = _gates(u, wr_ref, wi_ref, br_ref[...], bi_ref[...], sp)
        a_scr[...] = a
        b_scr[...] = mult * (i * u)

        def step(t, hc):
            hn = a_scr[pl.ds(t, 1), :] * hc + b_scr[pl.ds(t, 1), :]
            h_ref[pl.ds(t, 1), :] = hn
            return hn

        hcar[...] = lax.fori_loop(0, tm, step, hcar[...], unroll=8)

    row = pl.BlockSpec((tm, LW), lambda i: (i, 0))
    prev8 = pl.BlockSpec((8, LW), lambda i: (jnp.maximum(i * (tm // 8) - 1, 0), 0))
    full = lambda a: pl.BlockSpec(a.shape, lambda i: (0,) * a.ndim)
    return pl.pallas_call(
        body, name="lru_fwd", grid=(s // tm,),
        in_specs=[prev8, row, full(conv_w), full(conv_b), full(wr), full(wi), full(br), full(bi), full(lam)],
        out_specs=[row, row],
        out_shape=[pltpu.HBM((s,LW), f32), pltpu.HBM((s,LW), f32)],
        scratch_shapes=[pltpu.VMEM((tm, LW), f32), pltpu.VMEM((tm, LW), f32), pltpu.VMEM((1, LW), f32)],
        compiler_params=_params(("arbitrary",), 48),
    )(*_in_hbm(xl, xl), conv_w, conv_b, wr, wi, br, bi, lam)


def _lru_bwd(u, hl, dhl, xl, conv_w, wr, wi, br, bi, lam, tm):
    s = u.shape[0]
    nt = s // tm

    def body(u_ref, h_ref, hp_ref, dh_ref, x_ref, xp_ref, cw_ref, wr_ref, wi_ref, br_ref, bi_ref, lam_ref,
             dxl_ref, dwr_ref, dwi_ref, dbr_ref, dbi_ref, dlam_ref, dcb_ref, dcw_ref,
             a_scr, l_scr, lcar, dunext):
        t0 = pl.program_id(0)
        tile = nt - 1 - t0

        @pl.when(t0 == 0)
        def _():
            lcar[...] = jnp.zeros_like(lcar)
            dunext[...] = jnp.zeros_like(dunext)
            for ref in (dwr_ref, dwi_ref, dbr_ref, dbi_ref, dlam_ref, dcb_ref, dcw_ref):
                ref[...] = jnp.zeros_like(ref)

        u = u_ref[...]
        lam = lam_ref[...]
        sp = _softplus(-lam)
        ub, r, i, a, mult, inv_mult = _gates(u, wr_ref, wi_ref, br_ref[...], bi_ref[...], sp)
        a_scr[...] = a

        def step(k, c):
            t = tm - 1 - k
            lt = dh_ref[pl.ds(t, 1), :] + c
            l_scr[pl.ds(t, 1), :] = lt
            return a_scr[pl.ds(t, 1), :] * lt

        lcar[...] = lax.fori_loop(0, tm, step, lcar[...], unroll=8)
        lt = l_scr[...]

        hprev = _later(h_ref[...], jnp.where(tile > 0, hp_ref[...], 0.0), 1)
        da = lt * hprev
        dmult = lt * (i * u)
        di = lt * mult * u
        du = lt * mult * i
        dla = da * a - dmult * (a * a) * inv_mult
        dr = dla * (-LRU_C * sp)
        dlam_ref[...] += jnp.sum(dla * (-LRU_C * r), axis=0, keepdims=True)
        dpr = dr * r * (1.0 - r)
        dpi = di * i * (1.0 - i)
        dbr_ref[...] += jnp.sum(dpr, axis=0, keepdims=True)
        dbi_ref[...] += jnp.sum(dpi, axis=0, keepdims=True)
        dprb, dpib = dpr.astype(bf16), dpi.astype(bf16)
        dug = []
        for g in range(NGRP):
            gs = slice(256 * g, 256 * g + 256)
            dwr_ref[g] += _dot_tn(ub[:, gs], dprb[:, gs])
            dwi_ref[g] += _dot_tn(ub[:, gs], dpib[:, gs])
            dug.append(_dot_nt(dprb[:, gs], wr_ref[g]) + _dot_nt(dpib[:, gs], wi_ref[g]))
        du = du + jnp.concatenate(dug, axis=1)

        dcb_ref[...] += jnp.sum(du, axis=0, keepdims=True)
        x = x_ref[...]
        before = jnp.where(tile > 0, xp_ref[...], 0.0)
        for k in range(CONVW):
            dcw_ref[k:k + 1, :] += jnp.sum(du * _later(x, before, CONVW - 1 - k), axis=0, keepdims=True)
        after = dunext[...]
        dxl = sum(cw_ref[k:k + 1, :] * _earlier(du, after, CONVW - 1 - k) for k in range(CONVW))
        dxl_ref[...] = dxl.astype(bf16)
        dunext[...] = du[0:8, :]

        @pl.when(t0 == nt - 1)
        def _():
            dlam_ref[...] = dlam_ref[...] * (-_sigmoid(-lam))

    rev = lambda i: (nt - 1 - i, 0)
    row = pl.BlockSpec((tm, LW), rev)
    prev8 = pl.BlockSpec((8, LW), lambda i: (jnp.maximum((nt - 1 - i) * (tm // 8) - 1, 0), 0))
    full = lambda a: pl.BlockSpec(a.shape, lambda i: (0,) * a.ndim)
    vec = pl.BlockSpec((1, LW), lambda i: (0, 0))
    bd = pl.BlockSpec((NGRP, 256, 256), lambda i: (0, 0, 0))
    return pl.pallas_call(
        body, name="lru_bwd", grid=(nt,),
        in_specs=[row, row, prev8, row, row, prev8, full(conv_w), full(wr), full(wi), full(br), full(bi), full(lam)],
        out_specs=[row, bd, bd, vec, vec, vec, vec, pl.BlockSpec((CONVW, LW), lambda i: (0, 0))],
        out_shape=[pltpu.HBM((s,LW), bf16),
                   jax.ShapeDtypeStruct((NGRP, 256, 256), f32), jax.ShapeDtypeStruct((NGRP, 256, 256), f32),
                   jax.ShapeDtypeStruct((1, LW), f32), jax.ShapeDtypeStruct((1, LW), f32),
                   jax.ShapeDtypeStruct((1, LW), f32), jax.ShapeDtypeStruct((1, LW), f32),
                   jax.ShapeDtypeStruct((CONVW, LW), f32)],
        scratch_shapes=[pltpu.VMEM((tm, LW), f32), pltpu.VMEM((tm, LW), f32),
                        pltpu.VMEM((1, LW), f32), pltpu.VMEM((8, LW), f32)],
        compiler_params=_params(("arbitrary",), 56),
    )(*_in_hbm(u, hl, hl, dhl, xl, xl), conv_w, wr, wi, br, bi, lam)


def _gated_norm(t, gate, gain):
    sg = _sigmoid(gate)
    silu = gate * sg
    p = t * silu
    rstd = lax.rsqrt(jnp.mean(p * p, axis=-1, keepdims=True) + EPS)
    ph = p * rstd
    return sg, silu, rstd, ph, ph * gain


def _gated_norm_bwd(dy, t, gate, gain, sg, silu, rstd, ph):
    w = dy * gain
    dp = rstd * (w - ph * jnp.mean(w * ph, axis=-1, keepdims=True))
    dgate = dp * t * (sg * (1.0 + gate * (1.0 - sg)))
    return jnp.sum(dy * ph, axis=0, keepdims=True), dp * silu, dgate


def _out_fwd_bwd(x, tgt, o, ga, hl, gl, again, lgain, fgain, wo, tm):
    s = x.shape[0]
    nt = s // tm

    def body(x_ref, t_ref, o_ref, ga_ref, hl_ref, gl_ref, ag_ref, lg_ref, fg_ref, wo_ref,
             dx2_ref, do_ref, dga_ref, dhl_ref, dgl_ref, dwo_ref, gfg_ref, gag_ref, glg_ref, loss_ref, acc):
        i = pl.program_id(0)

        @pl.when(i == 0)
        def _():
            acc[...] = jnp.zeros_like(acc)
            for ref in (gfg_ref, gag_ref, glg_ref, loss_ref):
                ref[...] = jnp.zeros_like(ref)

        oo, gga, hh, ggl = o_ref[...], ga_ref[...], hl_ref[...], gl_ref[...]
        ag, lg, fg = ag_ref[...], lg_ref[...], fg_ref[...]
        sga, silua, ra, pah, ya = _gated_norm(oo, gga, ag)
        sgl, silul, rl, plh, yl = _gated_norm(hh, ggl, lg)
        yab, ylb = ya.astype(bf16), yl.astype(bf16)
        y = _dot(yab, wo_ref[0:D, :]) + _dot(ylb, wo_ref[D:2 * D, :])
        x2 = x_ref[...] + y
        r2 = lax.rsqrt(jnp.mean(x2 * x2, axis=-1, keepdims=True) + EPS)
        x2h = x2 * r2
        err = x2h * fg - t_ref[...]
        loss_ref[...] += 0.5 * jnp.sum(jnp.sum(err * err, axis=-1, keepdims=True) * (1.0 / D))
        dout = err * (1.0 / D)
        gfg_ref[...] += jnp.sum(dout * x2h, axis=0, keepdims=True)
        w = dout * fg
        dx2 = r2 * (w - x2h * jnp.mean(w * x2h, axis=-1, keepdims=True))
        dx2_ref[...] = dx2
        dyb = dx2.astype(bf16)
        acc[0:D, :] += _dot_tn(yab, dyb)
        acc[D:2 * D, :] += _dot_tn(ylb, dyb)
        dya = _dot_nt(dyb, wo_ref[0:D, :])
        dyl = _dot_nt(dyb, wo_ref[D:2 * D, :])
        gag, do, dga = _gated_norm_bwd(dya, oo, gga, ag, sga, silua, ra, pah)
        glg, dhl, dgl = _gated_norm_bwd(dyl, hh, ggl, lg, sgl, silul, rl, plh)
        gag_ref[...] += gag
        glg_ref[...] += glg
        do_ref[...] = do.astype(bf16)
        dga_ref[...] = dga.astype(bf16)
        dhl_ref[...] = dhl
        dgl_ref[...] = dgl.astype(bf16)

        @pl.when(i == nt - 1)
        def _():
            dwo_ref[...] = acc[...].astype(bf16)

    row = pl.BlockSpec((tm, D), lambda i: (i, 0))
    vec = pl.BlockSpec((1, D), lambda i: (0, 0))
    mat = pl.BlockSpec((2 * D, D), lambda i: (0, 0))
    return pl.pallas_call(
        body, name="out_fwd_bwd", grid=(nt,),
        in_specs=[row] * 6 + [vec] * 3 + [mat],
        out_specs=[row] * 5 + [mat, vec, vec, vec, pl.BlockSpec((1, 128), lambda i: (0, 0))],
        out_shape=[pltpu.HBM((s,D), f32), pltpu.HBM((s,D), bf16),
                   pltpu.HBM((s,D), bf16), pltpu.HBM((s,D), f32),
                   pltpu.HBM((s,D), bf16), pltpu.HBM((2 * D, D), bf16),
                   jax.ShapeDtypeStruct((1, D), f32), jax.ShapeDtypeStruct((1, D), f32),
                   jax.ShapeDtypeStruct((1, D), f32), jax.ShapeDtypeStruct((1, 128), f32)],
        scratch_shapes=[pltpu.VMEM((2 * D, D), f32)],
        compiler_params=_params(("arbitrary",), 56),
    )(*_in_hbm(x, tgt, o, ga, hl, gl), again, lgain, fgain, *_in_hbm(wo))


def _bwd_in(x, dx2, dq, dk, dv, dga, dxl, dgl, ln_gain, wt, tabs, tm):
    s = x.shape[0]

    def body(x_ref, dx2_ref, dq_ref, dk_ref, dv_ref, dga_ref, dxl_ref, dgl_ref, g_ref, wt_ref,
             c_ref, sa_ref, sb_ref, gx_ref, gln_ref, dzt_ref, dz_scr):
        @pl.when(pl.program_id(0) == 0)
        def _():
            gln_ref[...] = jnp.zeros_like(gln_ref)

        c, sa, sb = c_ref[...], sa_ref[...], sb_ref[...]
        for j in range(D // 128):
            js = slice(128 * j, 128 * j + 128)
            dz_scr[:, js] = (_unrope(dq_ref[:, js], c, sa, sb) * (HD ** -0.5)).astype(bf16)
        for j in range(KVW // 128):
            js = slice(128 * j, 128 * j + 128)
            dz_scr[:, D + 128 * j:D + 128 * j + 128] = _unrope(dk_ref[:, js], c, sa, sb).astype(bf16)
        dz_scr[:, D + KVW:D + 2 * KVW] = dv_ref[...].astype(bf16)
        dz_scr[:, 1536:2560] = dga_ref[...]
        dz_scr[:, 2560:3584] = dxl_ref[...]
        dz_scr[:, 3584:4608] = dgl_ref[...]
        for j in range(NIN // 128):
            dzt_ref[128 * j:128 * j + 128, :] = dz_scr[:, 128 * j:128 * j + 128].T
        dh = _dot(dz_scr[:, 0:512], wt_ref[0:512, :])
        for ci in range(1, NIN // 512):
            dh = dh + _dot(dz_scr[:, 512 * ci:512 * ci + 512], wt_ref[512 * ci:512 * ci + 512, :])
        xx = x_ref[...]
        rstd = lax.rsqrt(jnp.mean(xx * xx, axis=-1, keepdims=True) + EPS)
        xh = xx * rstd
        gln_ref[...] += jnp.sum(dh * xh, axis=0, keepdims=True)
        w = dh * g_ref[...]
        gx_ref[...] = dx2_ref[...] + rstd * (w - xh * jnp.mean(w * xh, axis=-1, keepdims=True))

    row = lambda w: pl.BlockSpec((tm, w), lambda i: (i, 0))
    full = lambda a: pl.BlockSpec(a.shape, lambda i: (0, 0))
    return pl.pallas_call(
        body, name="bwd_in", grid=(s // tm,),
        in_specs=[row(D), row(D), row(D), row(KVW), row(KVW), row(D), row(D), row(D), full(ln_gain), full(wt),
                  row(128), row(128), row(128)],
        out_specs=[row(D), pl.BlockSpec((1, D), lambda i: (0, 0)), pl.BlockSpec((NIN, tm), lambda i: (0, i))],
        out_shape=[pltpu.HBM((s,D), f32), jax.ShapeDtypeStruct((1, D), f32),
                   pltpu.HBM((NIN, s), bf16)],
        scratch_shapes=[pltpu.VMEM((tm, NIN), bf16)],
        compiler_params=_params(("arbitrary",), 56),
    )(*_in_hbm(x, dx2, dq, dk, dv, dga, dxl, dgl), ln_gain, *_in_hbm(wt), *tabs)


WT_TERMS = 5


def _dwt_scatter(dzt, h, small, tm):
    s = h.shape[0]
    nk = s // tm
    srows = small.shape[0] // NDEV
    last = NDEV - 1

    def body(order_ref, dz_ref, h_ref, sm_ref, lwt_ref, lsm_ref, acc, stage, given, send_sems, recv_sems, local_sem,
             sm_send, sm_recv, sm_local):
        j, k = pl.program_id(0), pl.program_id(1)
        x, y, c = _place()
        sibling = (x, y, 1 - c)
        chips = [(1 - x, 1 - y), (1 - x, y), (x, 1 - y)]
        sm_start, sm_finish = _scatter_ops([sm_ref], [lsm_ref], sm_send, sm_recv, sm_local)

        def send(step):
            if step == last - 1:
                dst, to = lwt_ref.at[1], sibling
            elif step % 2 == 0:
                dst, to = given.at[step // 2], sibling
            else:
                dst, to = lwt_ref.at[2 + step // 2], (*chips[step // 2], c)
            return pltpu.make_async_remote_copy(
                src_ref=stage.at[step % 2], dst_ref=dst, send_sem=send_sems.at[step], recv_sem=recv_sems.at[step],
                device_id=to, device_id_type=MESH)

        def keep():
            return pltpu.make_async_copy(stage.at[last % 2], lwt_ref.at[0], local_sem)

        @pl.when((j == 0) & (k == 0))
        def _():
            sm_start()

        @pl.when(k == 0)
        def _():
            acc[...] = jnp.zeros_like(acc)

        acc[...] += _dot(dz_ref[...], h_ref[...])

        for step in range(NDEV):
            @pl.when((k == nk - 1) & (j == step))
            def _(step=step):
                if step >= 2:
                    send(step - 2).wait_send()
                if step % 2 == 1 and step < last:
                    send(step - 1).wait_recv()
                    stage[step % 2] = (acc[...] + given[step // 2].astype(f32)).astype(bf16)
                else:
                    stage[step % 2] = acc[...].astype(bf16)
                if step < last:
                    send(step).start()
                else:
                    keep().start()
                    send(last - 1).wait_send()
                    for peer_step in (1, 3, 5, last - 1):
                        send(peer_step).wait_recv()
                    keep().wait()
                    sm_finish()

    x, y, c = _place()
    dest = lambda cx, cy, cc: 4 * cx + 2 * cy + cc
    order = jnp.stack([dest(1 - x, 1 - y, 1 - c), dest(1 - x, 1 - y, c), dest(1 - x, y, 1 - c), dest(1 - x, y, c),
                       dest(x, 1 - y, 1 - c), dest(x, 1 - y, c), dest(x, y, 1 - c), dest(x, y, c)])
    return pl.pallas_call(
        body, name="dwt_scatter",
        grid_spec=pltpu.PrefetchScalarGridSpec(
            num_scalar_prefetch=1, grid=(NDEV, nk),
            in_specs=[pl.BlockSpec((WT_ROWS, tm), lambda j, k, order: (order[j], k)),
                      pl.BlockSpec((tm, D), lambda j, k, order: (k, 0)), HBM],
            out_specs=[HBM, HBM],
            scratch_shapes=[pltpu.VMEM((WT_ROWS, D), f32), pltpu.VMEM((2, WT_ROWS, D), bf16),
                            pltpu.VMEM((3, WT_ROWS, D), bf16),
                            pltpu.SemaphoreType.DMA((last,)), pltpu.SemaphoreType.DMA((last,)),
                            pltpu.SemaphoreType.DMA(())] + _comm_sems(1)),
        out_shape=[pltpu.HBM((WT_TERMS, WT_ROWS, D), bf16), pltpu.HBM((NDEV, srows, D), f32)],
        compiler_params=_params(("arbitrary", "arbitrary"), 32),
    )(order, *_in_hbm(dzt, h, small))


def _diag_blocks(bd):
    eye = jnp.eye(4, dtype=bd.dtype)
    return jnp.einsum('gjckd,jk->gjcd', bd.reshape(NGRP, 4, HD, 4, HD), eye).reshape(NQ, HD, HD)


def _sequence_step(x, tgt, wt, wo_shard, conv_w, p):
    s = x.shape[0]
    tm = min(256, s)
    tabs = _rope_tables(s)
    wr, wi = _block_diag(p["w_rgate"]), _block_diag(p["w_igate"])
    sinks = p["sinks"].reshape(NQ)
    h, q, k, v, ga, xl, gl, wo = _fwd_in(x, p["ln_gain"], wt, tabs, wo_shard, tm)
    o = _attn_fwd(q, k, v, sinks)
    u, hl = _lru_fwd(xl, conv_w, p["conv_b"], wr, wi, p["b_rgate"], p["b_igate"], p["lru_lambda"], tm)
    dx2, do, dga, dhl, dgl, dwo, g_fg, g_ag, g_lg, loss = _out_fwd_bwd(
        x, tgt, o, ga, hl, gl, p["attn_out_gain"], p["lru_out_gain"], p["final_gain"], wo, tm)
    dq, dk, dv, dsink, land_wo = _attn_bwd(q, k, v, do, sinks, dwo)
    dxl, dwr, dwi, dbr, dbi, dlam, dcb, dcw = _lru_bwd(
        u, hl, dhl, xl, conv_w, wr, wi, p["b_rgate"], p["b_igate"], p["lru_lambda"], tm)
    gx, g_ln, dzt = _bwd_in(x, dx2, dq, dk, dv, dga, dxl, dgl, p["ln_gain"], wt, tabs, tm)
    small = dict(ln_gain=g_ln, sinks=dsink.reshape(NQ, BLK).sum(axis=1)[None], conv_w=dcw, conv_b=dcb,
                 w_rgate=_diag_blocks(dwr), b_rgate=dbr, w_igate=_diag_blocks(dwi), b_igate=dbi, lru_lambda=dlam,
                 attn_out_gain=g_ag, lru_out_gain=g_lg, final_gain=g_fg)
    land_wt, land_sm = _dwt_scatter(dzt, h, _pack_small(small, loss), min(512, s))
    return gx, land_wt, land_wo, land_sm


def _all_gather(srcs, out_dtypes, name):
    n = len(srcs)
    cast = [a.dtype != dt for a, dt in zip(srcs, out_dtypes)]

    def body(*refs):
        src_refs, out_refs = refs[:n], refs[n:2 * n]
        stage_refs = list(refs[2 * n:2 * n + sum(cast)])
        mine_refs = []
        for a in range(n):
            if cast[a]:
                st = stage_refs.pop(0)
                st[...] = src_refs[a][...].astype(out_dtypes[a])
                mine_refs.append(st)
            else:
                mine_refs.append(src_refs[a])
        start, finish = _gather_ops(mine_refs, out_refs, *refs[-3:])
        start()
        finish()

    vmem = pl.BlockSpec(memory_space=pltpu.VMEM)
    return pl.pallas_call(
        body, name=name,
        in_specs=[vmem] * n, out_specs=[HBM] * n,
        out_shape=[pltpu.HBM((NDEV * a.shape[0], a.shape[1]), dt) for a, dt in zip(srcs, out_dtypes)],
        scratch_shapes=[pltpu.VMEM(a.shape, dt) for a, dt, cst in zip(srcs, out_dtypes, cast) if cst] + _comm_sems(n),
        compiler_params=pltpu.CompilerParams(vmem_limit_bytes=32 * MIB),
    )(*srcs)


def _sum_slots(land, tr, name):
    terms, rows, cols = land.shape

    def body(l_ref, o_ref):
        acc = l_ref[0].astype(f32)
        for d in range(1, terms):
            acc = acc + l_ref[d].astype(f32)
        o_ref[...] = acc

    return pl.pallas_call(
        body, name=name, grid=(rows // tr,),
        in_specs=[pl.BlockSpec((terms, tr, cols), lambda i: (0, i, 0))],
        out_specs=pl.BlockSpec((tr, cols), lambda i: (i, 0)),
        out_shape=jax.ShapeDtypeStruct((rows, cols), f32),
        compiler_params=_params(("arbitrary",), 32),
    )(*_in_hbm(land))


def _adam_math(w, g, m, v):
    m2 = ADAM_B1 * m + (1.0 - ADAM_B1) * g
    v2 = ADAM_B2 * v + (1.0 - ADAM_B2) * (g * g)
    m_hat = m2 / (1.0 - ADAM_B1 ** ADAM_STEP)
    v_hat = v2 / (1.0 - ADAM_B2 ** ADAM_STEP)
    delta = -ADAM_LR * (m_hat / (jnp.sqrt(v_hat) + ADAM_EPS) + ADAM_WD * w)
    return delta, m2, v2


def _adamw(w, g, m, v, tr, name):
    rows, cols = w.shape

    def body(w_ref, g_ref, m_ref, v_ref, d_ref, m2_ref, v2_ref):
        d_ref[...], m2_ref[...], v2_ref[...] = _adam_math(w_ref[...], g_ref[...], m_ref[...], v_ref[...])

    blk = pl.BlockSpec((tr, cols), lambda i: (i, 0))
    return pl.pallas_call(
        body, name=name, grid=(rows // tr,),
        in_specs=[blk] * 4, out_specs=[blk] * 3,
        out_shape=[jax.ShapeDtypeStruct((rows, cols), f32)] * 3,
        compiler_params=_params(("arbitrary",), 32),
    )(*_in_hbm(w, g, m, v))


VEC_NAMES = ("ln_gain", "conv_b", "b_rgate", "b_igate", "lru_lambda", "attn_out_gain", "lru_out_gain", "final_gain")
ROW_RGATE, ROW_IGATE, ROW_VEC, ROW_SINKS = 0, 64, 128, 136
LOSS_LANE = NQ


def _adamw_small(g_rep, g_conv, w, m, v):
    names = list(VEC_NAMES) + ["sinks", "conv_w", "w_rgate", "w_igate"]
    ins = [g_rep, g_conv] + [d[k] for k in names for d in (w, m, v)]

    def body(*refs):
        g_ref, gc_ref = refs[0], refs[1]
        in_refs = refs[2:2 + 3 * len(names)]
        out_refs = refs[2 + 3 * len(names):]

        def update(j, g, at=None):
            w_ref, m_ref, v_ref = in_refs[3 * j:3 * j + 3]
            outs = out_refs[4 * j:4 * j + 4]
            pick = (lambda r: r[...]) if at is None else (lambda r: r[at])
            res = (g,) + _adam_math(pick(w_ref), g, pick(m_ref), pick(v_ref))
            for o_ref, val in zip(outs, res):
                if at is None:
                    o_ref[...] = val
                else:
                    o_ref[at] = val

        for j in range(len(VEC_NAMES)):
            update(j, g_ref[ROW_VEC + j:ROW_VEC + j + 1, :])
        update(len(VEC_NAMES), g_ref[ROW_SINKS:ROW_SINKS + 1, 0:NQ])
        update(len(VEC_NAMES) + 1, gc_ref[...], at=0)
        for gi, row0 in ((len(VEC_NAMES) + 2, ROW_RGATE), (len(VEC_NAMES) + 3, ROW_IGATE)):
            for nb in range(NQ):
                update(gi, g_ref[row0:row0 + HD, HD * nb:HD * nb + HD], at=(0, nb))

    vmem = pl.BlockSpec(memory_space=pltpu.VMEM)
    out_shape = [jax.ShapeDtypeStruct(w[k].shape, f32) for k in names for _ in range(4)]
    outs = pl.pallas_call(
        body, name="adamw_small",
        in_specs=[vmem] * len(ins), out_specs=[vmem] * len(out_shape), out_shape=out_shape,
        compiler_params=pltpu.CompilerParams(vmem_limit_bytes=32 * MIB),
    )(*ins)
    return {k: tuple(outs[4 * j:4 * j + 4]) for j, k in enumerate(names)}


def _pack_small(small, loss):
    gate = lambda g: g.transpose(1, 0, 2).reshape(HD, NQ * HD)
    row_s = jnp.concatenate([small["sinks"], loss[:, LOSS_LANE:128], jnp.zeros((1, D - 128), f32)], axis=1)
    rep = jnp.concatenate([gate(small["w_rgate"]), gate(small["w_igate"])] + [small[k] for k in VEC_NAMES]
                          + [row_s, jnp.zeros((SMALL_ROWS - ROW_SINKS - 1, D), f32)], axis=0)
    conv = small["conv_w"].reshape(CONVW, NDEV, 128).transpose(1, 0, 2)
    conv = jnp.pad(conv, ((0, 0), (0, 8 - CONVW), (0, D - 128)))
    return jnp.concatenate([rep.reshape(NDEV, SMALL_PER, D), conv], axis=1).reshape(NDEV * (SMALL_PER + 8), D)


def kernel(x, ln_gain, w_in, sinks, conv_w, conv_b, w_rgate, b_rgate, w_igate, b_igate, lru_lambda, attn_out_gain, lru_out_gain, w_out, final_gain, loss_target, m_ln_gain, m_w_in, m_sinks, m_conv_w, m_conv_b, m_w_rgate, m_b_rgate, m_w_igate, m_b_igate, m_lru_lambda, m_attn_out_gain, m_lru_out_gain, m_w_out, m_final_gain, v_ln_gain, v_w_in, v_sinks, v_conv_w, v_conv_b, v_w_rgate, v_b_rgate, v_w_igate, v_b_igate, v_lru_lambda, v_attn_out_gain, v_lru_out_gain, v_w_out, v_final_gain):
    w = dict(ln_gain=ln_gain, sinks=sinks, conv_w=conv_w, conv_b=conv_b, w_rgate=w_rgate, b_rgate=b_rgate,
             w_igate=w_igate, b_igate=b_igate, lru_lambda=lru_lambda, attn_out_gain=attn_out_gain,
             lru_out_gain=lru_out_gain, final_gain=final_gain.reshape(1, D))
    m = dict(ln_gain=m_ln_gain, sinks=m_sinks, conv_w=m_conv_w, conv_b=m_conv_b, w_rgate=m_w_rgate,
             b_rgate=m_b_rgate, w_igate=m_w_igate, b_igate=m_b_igate, lru_lambda=m_lru_lambda,
             attn_out_gain=m_attn_out_gain, lru_out_gain=m_lru_out_gain, final_gain=m_final_gain.reshape(1, D))
    v = dict(ln_gain=v_ln_gain, sinks=v_sinks, conv_w=v_conv_w, conv_b=v_conv_b, w_rgate=v_w_rgate,
             b_rgate=v_b_rgate, w_igate=v_w_igate, b_igate=v_b_igate, lru_lambda=v_lru_lambda,
             attn_out_gain=v_attn_out_gain, lru_out_gain=v_lru_out_gain, final_gain=v_final_gain.reshape(1, D))

    conv_blk = jnp.pad(conv_w[0], ((0, 8 - CONVW), (0, 0)))
    wt, cw_all = _all_gather([w_in[0].T, conv_blk], [bf16, f32], "gather_weights")
    conv_full = cw_all.reshape(NDEV, 8, 128)[:, 0:CONVW].transpose(1, 0, 2).reshape(CONVW, LW)

    p = {k: (w[k][0] if k in ("w_rgate", "w_igate") else w[k]) for k in w if k != "conv_w"}
    gx, land_wt, land_wo, land_sm = _sequence_step(x[0], loss_target[0], wt, w_out[0], conv_full, p)

    g_wt = _sum_slots(land_wt, 192, "sum_wt")
    g_wo = _sum_slots(land_wo, 256, "sum_wo")
    g_sm = _sum_slots(land_sm, SMALL_PER + 8, "sum_small")
    (g_rep,) = _all_gather([g_sm[0:SMALL_PER]], [f32], "gather_small")
    g_conv = g_sm[SMALL_PER:SMALL_PER + CONVW, 0:128]

    d_win, m_win, v_win = _adamw(w_in[0].T, g_wt, m_w_in[0].T, v_w_in[0].T, 192, "adamw_w_in")
    g_win, d_win, m_win, v_win = (t.T for t in (g_wt, d_win, m_win, v_win))
    d_wo, m_wo, v_wo = _adamw(w_out[0], g_wo, m_w_out[0], v_w_out[0], 256, "adamw_w_out")
    res = _adamw_small(g_rep, g_conv, w, m, v)
    res["w_in"] = tuple(t[None] for t in (g_win, d_win, m_win, v_win))
    res["w_out"] = tuple(t[None] for t in (g_wo, d_wo, m_wo, v_wo))
    res["final_gain"] = tuple(t.reshape(D) for t in res["final_gain"])

    order = ("ln_gain", "w_in", "sinks", "conv_w", "conv_b", "w_rgate", "b_rgate", "w_igate", "b_igate",
             "lru_lambda", "attn_out_gain", "lru_out_gain", "w_out", "final_gain")
    total_loss = g_rep[ROW_SINKS, LOSS_LANE]
    return (total_loss, gx[None]) + tuple(res[k][i] for i in range(4) for k in order)
```

```python
import jax
import jax.numpy as jnp
from jax import lax
from jax.experimental import pallas as pl
from jax.experimental.pallas import tpu as pltpu

f32 = jnp.float32
bf16 = jnp.bfloat16

D = 1024
HD = 64
NQ = 16
NKV = 4
GROUP = NQ // NKV
KVW = NKV * HD
BLK = 128
ROT = 16
THETA = 500000.0
NEG = -1e30
LW = 1024
NGRP = 4
CONVW = 4
LRU_C = 8.0
NIN = 4608
EPS = 1e-6
NDEV = 8
WT_ROWS = NIN // NDEV
WO_ROWS = 2 * D // NDEV
SMALL_ROWS = 192
SMALL_PER = SMALL_ROWS // NDEV

ADAM_LR = 0.001
ADAM_B1 = 0.9
ADAM_B2 = 0.999
ADAM_EPS = 1e-08
ADAM_WD = 0.01
ADAM_STEP = 10

NT = (((1,), (1,)), ((), ()))
TN = (((0,), (0,)), ((), ()))
MESH = pl.DeviceIdType.MESH
MIB = 1024 * 1024


def _dot(a, b):
    return jnp.dot(a, b, preferred_element_type=f32)


def _dot_nt(a, b):
    return lax.dot_general(a, b, NT, preferred_element_type=f32)


def _dot_tn(a, b):
    return lax.dot_general(a, b, TN, preferred_element_type=f32)


def _params(sem, vmem_mib):
    return pltpu.CompilerParams(dimension_semantics=sem, vmem_limit_bytes=vmem_mib * MIB)


def _sigmoid(x):
    return 0.5 * jnp.tanh(0.5 * x) + 0.5


def _softplus(x):
    return jnp.maximum(x, 0.0) + jnp.log(1.0 + jnp.exp(-jnp.abs(x)))


def _rope_tables(s):
    pos = jnp.arange(s, dtype=f32)
    inv_freq = THETA ** (-jnp.arange(0, ROT, 2, dtype=f32) / ROT)
    ang = pos[:, None] * inv_freq[None, :]
    cs = jnp.concatenate([jnp.cos(ang) - 1.0, jnp.sin(ang)], axis=1)
    d = jnp.arange(128) % HD
    j = jnp.arange(ROT)[:, None]
    pick_c = ((d < ROT) & (j == d % (ROT // 2))).astype(f32)
    pick_sa = ((d >= ROT // 2) & (d < ROT) & (j == d)).astype(f32)
    pick_sb = -((d < ROT // 2) & (j == d + ROT // 2)).astype(f32)
    spread = lambda pick: jnp.dot(cs, pick, precision=lax.Precision.HIGHEST)
    return 1.0 + spread(pick_c), spread(pick_sa), spread(pick_sb)


def _rope(t, c, sa, sb):
    return t * c + pltpu.roll(t, 8, 1) * sa + pltpu.roll(t, 120, 1) * sb


def _unrope(dr, c, sa, sb):
    return dr * c + pltpu.roll(dr * sa, 120, 1) + pltpu.roll(dr * sb, 8, 1)


def _place():
    return lax.axis_index("x"), lax.axis_index("y"), lax.axis_index("c")


def _gather_ops(mine_refs, out_refs, send_sems, recv_sems, local_sems):
    n = len(mine_refs)
    x, y, c = _place()
    me, sibling = (x, y, c), (x, y, 1 - c)
    chips = [(1 - x, y), (x, 1 - y), (1 - x, 1 - y)]

    def rows(a, dev):
        m = mine_refs[a].shape[0]
        return out_refs[a].at[pl.ds((4 * dev[0] + 2 * dev[1] + dev[2]) * m, m), :]

    def copy(a, k, block, to, own=False):
        return pltpu.make_async_remote_copy(
            src_ref=mine_refs[a] if own else rows(a, block), dst_ref=rows(a, block),
            send_sem=send_sems.at[a, k], recv_sem=recv_sems.at[a, k], device_id=to, device_id_type=MESH)

    def local(a):
        return pltpu.make_async_copy(mine_refs[a], rows(a, me), local_sems.at[a])

    def first(a):
        return [copy(a, 0, me, sibling, own=True)] + [copy(a, 1 + j, me, (*chip, c), own=True)
                                                      for j, chip in enumerate(chips)]

    def start():
        for a in range(n):
            local(a).start()
            for cp in first(a):
                cp.start()

    def finish():
        for j, chip in enumerate(chips):
            for a in range(n):
                copy(a, 1 + j, (*chip, c), me).wait_recv()
                copy(a, 4 + j, (*chip, c), sibling).start()
        for a in range(n):
            copy(a, 0, sibling, me).wait_recv()
            for j, chip in enumerate(chips):
                copy(a, 4 + j, (*chip, 1 - c), me).wait_recv()
        for a in range(n):
            for cp in first(a) + [copy(a, 4 + j, (*chip, c), sibling) for j, chip in enumerate(chips)]:
                cp.wait_send()
            local(a).wait()

    return start, finish


def _scatter_ops(src_refs, land_refs, send_sems, recv_sems, local_sems):
    n = len(src_refs)
    x, y, c = _place()
    my = 4 * x + 2 * y + c

    def peer(k):
        return x ^ (k >> 2), y ^ ((k >> 1) & 1), c ^ (k & 1)

    def piece(a, dev):
        m = src_refs[a].shape[0] // NDEV
        return src_refs[a].at[pl.ds(dev * m, m), :]

    def local(a):
        return pltpu.make_async_copy(piece(a, my), land_refs[a].at[my], local_sems.at[a])

    def send(a, k):
        px, py, pc = peer(k)
        return pltpu.make_async_remote_copy(
            src_ref=piece(a, 4 * px + 2 * py + pc), dst_ref=land_refs[a].at[my],
            send_sem=send_sems.at[a, k - 1], recv_sem=recv_sems.at[a, k - 1],
            device_id=(px, py, pc), device_id_type=MESH)

    def arrival(a, k):
        px, py, pc = peer(k)
        return pltpu.make_async_remote_copy(
            src_ref=piece(a, my), dst_ref=land_refs[a].at[4 * px + 2 * py + pc],
            send_sem=send_sems.at[a, k - 1], recv_sem=recv_sems.at[a, k - 1],
            device_id=(px, py, pc), device_id_type=MESH)

    def start():
        for a in range(n):
            local(a).start()
        for k in range(1, NDEV):
            for a in range(n):
                send(a, k).start()

    def finish():
        for k in range(1, NDEV):
            for a in range(n):
                send(a, k).wait_send()
        for k in range(1, NDEV):
            for a in range(n):
                arrival(a, k).wait_recv()
        for a in range(n):
            local(a).wait()

    return start, finish


def _in_hbm(*arrays):
    return tuple(pltpu.with_memory_space_constraint(a, pltpu.HBM) for a in arrays)


def _comm_sems(n):
    return [pltpu.SemaphoreType.DMA((n, 7)), pltpu.SemaphoreType.DMA((n, 7)), pltpu.SemaphoreType.DMA((n,))]


HBM = pl.BlockSpec(memory_space=pltpu.HBM)


def _fwd_in(x, ln_gain, wt, tabs, wo_shard, tm):
    s = x.shape[0]
    nt = s // tm
    nc = 512

    def body(x_ref, g_ref, wt_ref, c_ref, sa_ref, sb_ref, wo_ref, h_ref, q_ref, k_ref, v_ref, ga_ref, xl_ref, gl_ref,
             wo_all, wo_stage, send_sems, recv_sems, local_sems):
        i = pl.program_id(0)
        start, finish = _gather_ops([wo_stage], [wo_all], send_sems, recv_sems, local_sems)

        @pl.when(i == 0)
        def _():
            wo_stage[...] = wo_ref[...].astype(bf16)
            start()

        xx = x_ref[...]
        rstd = lax.rsqrt(jnp.mean(xx * xx, axis=-1, keepdims=True) + EPS)
        h = (xx * rstd * g_ref[...]).astype(bf16)
        h_ref[...] = h
        c, sa, sb = c_ref[...], sa_ref[...], sb_ref[...]

        def z_chunk(ci):
            return _dot_nt(h, wt_ref[ci * nc:(ci + 1) * nc, :])

        for ci in range(2):
            z = z_chunk(ci)
            for j in range(nc // 128):
                r = _rope(z[:, 128 * j:128 * j + 128], c, sa, sb) * (HD ** -0.5)
                q_ref[:, ci * nc + 128 * j:ci * nc + 128 * j + 128] = r.astype(bf16)
        z = z_chunk(2)
        for j in range(2):
            k_ref[:, 128 * j:128 * j + 128] = _rope(z[:, 128 * j:128 * j + 128], c, sa, sb).astype(bf16)
        v_ref[...] = z[:, 256:512].astype(bf16)
        for sec, ref in enumerate((ga_ref, xl_ref, gl_ref)):
            for j in range(2):
                ref[:, j * nc:(j + 1) * nc] = z_chunk(3 + 2 * sec + j)

        @pl.when(i == nt - 1)
        def _():
            finish()

    row = lambda w: pl.BlockSpec((tm, w), lambda i: (i, 0))
    full = lambda a: pl.BlockSpec(a.shape, lambda i: (0, 0))
    return pl.pallas_call(
        body, name="fwd_in", grid=(nt,),
        in_specs=[row(D), full(ln_gain), full(wt), row(128), row(128), row(128), full(wo_shard)],
        out_specs=[row(D), row(D), row(KVW), row(KVW), row(D), row(D), row(D), HBM],
        out_shape=[pltpu.HBM((s,D), bf16), pltpu.HBM((s,D), bf16),
                   pltpu.HBM((s,KVW), bf16), pltpu.HBM((s,KVW), bf16),
                   pltpu.HBM((s,D), f32), pltpu.HBM((s,D), f32),
                   pltpu.HBM((s,D), f32), pltpu.HBM((2 * D, D), bf16)],
        scratch_shapes=[pltpu.VMEM((WO_ROWS, D), bf16)] + _comm_sems(1),
        compiler_params=_params(("arbitrary",), 48),
    )(*_in_hbm(x), ln_gain, *_in_hbm(wt), *tabs, wo_shard)


HSUB = 4
SUBW = HSUB * BLK


def _sub_probs(kh, qg, n, sink_row):
    jj = lax.broadcasted_iota(jnp.int32, (BLK, SUBW), 0)
    ii = lax.broadcasted_iota(jnp.int32, (BLK, SUBW), 1) % BLK
    from_prev = jj > ii
    s2 = _dot_nt(kh, qg)
    sc = jnp.where(from_prev, s2[0:BLK] + jnp.where(n > 0, 0.0, NEG), s2[BLK:2 * BLK])
    m = jnp.maximum(jnp.max(sc, axis=0, keepdims=True), sink_row)
    p = jnp.exp(sc - m)
    es = jnp.exp(sink_row - m)
    inv = 1.0 / (jnp.sum(p, axis=0, keepdims=True) + es)
    return from_prev, p * inv, es * inv


def _split(t, from_prev):
    t = t.astype(bf16)
    zero = jnp.zeros_like(t)
    return jnp.concatenate([jnp.where(from_prev, t, zero), jnp.where(from_prev, zero, t)], axis=0)


def _stack_heads(ref, first):
    return jnp.concatenate([ref[:, HD * (first + g):HD * (first + g) + HD] for g in range(HSUB)], axis=0)


def _sink_rows(sinks):
    return jnp.repeat(sinks.reshape(NKV, GROUP), BLK, axis=1)


def _kv_specs():
    prev = pl.BlockSpec((BLK, KVW), lambda n: (jnp.maximum(n - 1, 0), 0))
    cur = pl.BlockSpec((BLK, KVW), lambda n: (n, 0))
    return [prev, cur, prev, cur]


def _attn_fwd(q, k, v, sinks):
    s = q.shape[0]

    def body(sink_ref, q_ref, kp_ref, kc_ref, vp_ref, vc_ref, o_ref):
        n = pl.program_id(0)
        for h in range(NKV):
            hs = slice(HD * h, HD * h + HD)
            kh = jnp.concatenate([kp_ref[:, hs], kc_ref[:, hs]], axis=0)
            vh = jnp.concatenate([vp_ref[:, hs], vc_ref[:, hs]], axis=0)
            for t in range(GROUP // HSUB):
                first = GROUP * h + HSUB * t
                from_prev, pn, _ = _sub_probs(kh, _stack_heads(q_ref, first), n,
                                              sink_ref[h:h + 1, SUBW * t:SUBW * t + SUBW])
                og = _dot_tn(_split(pn, from_prev), vh)
                for g in range(HSUB):
                    o_ref[:, HD * (first + g):HD * (first + g) + HD] = og[BLK * g:BLK * g + BLK]

    return pl.pallas_call(
        body, name="attn_fwd", grid=(s // BLK,),
        in_specs=[pl.BlockSpec((NKV, GROUP * BLK), lambda n: (0, 0)), pl.BlockSpec((BLK, D), lambda n: (n, 0))]
        + _kv_specs(),
        out_specs=pl.BlockSpec((BLK, D), lambda n: (n, 0)),
        out_shape=pltpu.HBM((s,D), f32),
        compiler_params=_params(("arbitrary",), 32),
    )(_sink_rows(sinks), *_in_hbm(q, k, k, v, v))


def _attn_bwd(q, k, v, do, sinks, dwo):
    s = q.shape[0]
    nb = s // BLK

    def body(sink_ref, q_ref, do_ref, kp_ref, kc_ref, vp_ref, vc_ref, dwo_ref, dq_ref, dk_ref, dv_ref, ds_ref,
             land_ref, send_sems, recv_sems, local_sems):
        n = pl.program_id(0)
        start, finish = _scatter_ops([dwo_ref], [land_ref], send_sems, recv_sems, local_sems)

        @pl.when(n == 0)
        def _():
            start()
            dk_ref[...] = jnp.zeros_like(dk_ref)
            dv_ref[...] = jnp.zeros_like(dv_ref)
            ds_ref[...] = jnp.zeros_like(ds_ref)

        prev_rows = pl.ds(pl.multiple_of(jnp.maximum(n - 1, 0) * BLK, BLK), BLK)
        cur_rows = pl.ds(pl.multiple_of(n * BLK, BLK), BLK)
        for h in range(NKV):
            hs = slice(HD * h, HD * h + HD)
            kh = jnp.concatenate([kp_ref[:, hs], kc_ref[:, hs]], axis=0)
            vh = jnp.concatenate([vp_ref[:, hs], vc_ref[:, hs]], axis=0)
            dkh = jnp.zeros((2 * BLK, HD), f32)
            dvh = jnp.zeros((2 * BLK, HD), f32)
            for t in range(GROUP // HSUB):
                first = GROUP * h + HSUB * t
                lanes = slice(SUBW * t, SUBW * t + SUBW)
                qg, dog = _stack_heads(q_ref, first), _stack_heads(do_ref, first)
                from_prev, pn, ps = _sub_probs(kh, qg, n, sink_ref[h:h + 1, lanes])
                dp2 = _dot_nt(vh, dog)
                dp = jnp.where(from_prev, dp2[0:BLK], dp2[BLK:2 * BLK])
                dsum = jnp.sum(pn * dp, axis=0, keepdims=True)
                ds_ref[h:h + 1, lanes] += -ps * dsum
                ds2 = _split(pn * (dp - dsum), from_prev)
                dqg = _dot_tn(ds2, kh)
                for g in range(HSUB):
                    dq_ref[:, HD * (first + g):HD * (first + g) + HD] = dqg[BLK * g:BLK * g + BLK]
                dkh = dkh + _dot(ds2, qg)
                dvh = dvh + _dot(_split(pn, from_prev), dog)
            dk_ref[prev_rows, hs] += dkh[0:BLK]
            dk_ref[cur_rows, hs] += dkh[BLK:2 * BLK]
            dv_ref[prev_rows, hs] += dvh[0:BLK]
            dv_ref[cur_rows, hs] += dvh[BLK:2 * BLK]

        @pl.when(n == nb - 1)
        def _():
            finish()

    blk = pl.BlockSpec((BLK, D), lambda n: (n, 0))
    whole = lambda r, w: pl.BlockSpec((r, w), lambda n: (0, 0))
    return pl.pallas_call(
        body, name="attn_bwd", grid=(nb,),
        in_specs=[whole(NKV, GROUP * BLK), blk, blk] + _kv_specs() + [HBM],
        out_specs=[blk, whole(s, KVW), whole(s, KVW), whole(NKV, GROUP * BLK), HBM],
        out_shape=[pltpu.HBM((s,D), f32), pltpu.HBM((s,KVW), f32),
                   pltpu.HBM((s,KVW), f32), jax.ShapeDtypeStruct((NKV, GROUP * BLK), f32),
                   pltpu.HBM((NDEV, WO_ROWS, D), bf16)],
        scratch_shapes=_comm_sems(1),
        compiler_params=_params(("arbitrary",), 48),
    )(_sink_rows(sinks), *_in_hbm(q, do, k, k, v, v, dwo))


def _block_diag(w):
    w4 = w.reshape(NGRP, 4, HD, HD)
    eye = jnp.eye(4, dtype=w.dtype)
    return jnp.einsum('gjcd,jk->gjckd', w4, eye).reshape(NGRP, 256, 256).astype(bf16)


def _gate_terms(pr, pi, br, bi, sp):
    r = _sigmoid(pr + br)
    i = _sigmoid(pi + bi)
    la = -LRU_C * r * sp
    a = jnp.exp(la)
    x2 = 2.0 * la
    y = jnp.where(x2 > -0.02, -x2 * (1.0 + x2 * (0.5 + x2 * (1.0 / 6.0))), 1.0 - a * a)
    inv_mult = lax.rsqrt(jnp.maximum(y, 1e-30))
    return r, i, a, y * inv_mult, inv_mult


def _gates(u, wr_ref, wi_ref, br, bi, sp):
    ub = u.astype(bf16)
    pr = jnp.concatenate([_dot(ub[:, 256 * g:256 * g + 256], wr_ref[g]) for g in range(NGRP)], axis=1)
    pi = jnp.concatenate([_dot(ub[:, 256 * g:256 * g + 256], wi_ref[g]) for g in range(NGRP)], axis=1)
    return (ub,) + _gate_terms(pr, pi, br, bi, sp)


def _later(x, before, k):
    if k == 0:
        return x
    row = lax.broadcasted_iota(jnp.int32, before.shape, 0)
    rolled = pltpu.roll(x, k, 0)
    first = jnp.where(row < k, pltpu.roll(before, k, 0), rolled[0:8])
    return jnp.concatenate([first, rolled[8:]], axis=0)


def _earlier(x, after, k):
    if k == 0:
        return x
    n = x.shape[0]
    row = lax.broadcasted_iota(jnp.int32, after.shape, 0)
    rolled = pltpu.roll(x, n - k, 0)
    last = jnp.where(row >= 8 - k, pltpu.roll(after, 8 - k, 0), rolled[n - 8:n])
    return jnp.concatenate([rolled[0:n - 8], last], axis=0)


def _lru_fwd(xl, conv_w, conv_b, wr, wi, br, bi, lam, tm):
    s = xl.shape[0]

    def body(xp_ref, x_ref, cw_ref, cb_ref, wr_ref, wi_ref, br_ref, bi_ref, lam_ref, u_ref, h_ref,
             a_scr, b_scr, hcar):
        t0 = pl.program_id(0)

        @pl.when(t0 == 0)
        def _():
            hcar[...] = jnp.zeros_like(hcar)

        x = x_ref[...]
        before = jnp.where(t0 > 0, xp_ref[...], 0.0)
        u = cb_ref[...] + sum(cw_ref[k:k + 1, :] * _later(x, before, CONVW - 1 - k) for k in range(CONVW))
        u_ref[...] = u
        sp = _softplus(-lam_ref[...])
        _, _, i, a, mult, _ = _gates(u, wr_ref, wi_ref, br_ref[...], bi_ref[...], sp)
        a_scr[...] = a
        b_scr[...] = mult * (i * u)

        def step(t, hc):
            hn = a_scr[pl.ds(t, 1), :] * hc + b_scr[pl.ds(t, 1), :]
            h_ref[pl.ds(t, 1), :] = hn
            return hn

        hcar[...] = lax.fori_loop(0, tm, step, hcar[...], unroll=8)

    row = pl.BlockSpec((tm, LW), lambda i: (i, 0))
    prev8 = pl.BlockSpec((8, LW), lambda i: (jnp.maximum(i * (tm // 8) - 1, 0), 0))
    full = lambda a: pl.BlockSpec(a.shape, lambda i: (0,) * a.ndim)
    return pl.pallas_call(
        body, name="lru_fwd", grid=(s // tm,),
        in_specs=[prev8, row, full(conv_w), full(conv_b), full(wr), full(wi), full(br), full(bi), full(lam)],
        out_specs=[row, row],
        out_shape=[pltpu.HBM((s,LW), f32), pltpu.HBM((s,LW), f32)],
        scratch_shapes=[pltpu.VMEM((tm, LW), f32), pltpu.VMEM((tm, LW), f32), pltpu.VMEM((1, LW), f32)],
        compiler_params=_params(("arbitrary",), 48),
    )(*_in_hbm(xl, xl), conv_w, conv_b, wr, wi, br, bi, lam)


def _fwd_in_lru(x, ln_gain, wt, tabs, wo_shard, conv_w, conv_b, wr, wi, br, bi, lam, tm):
    s = x.shape[0]
    nt = s // tm
    nc = 512

    def body(x_ref, g_ref, wt_ref, c_ref, sa_ref, sb_ref, wo_ref, cw_ref, cb_ref, wr_ref, wi_ref, br_ref, bi_ref,
             lam_ref, h_ref, q_ref, k_ref, v_ref, ga_ref, xl_ref, gl_ref, u_ref, hl_ref, wo_all,
             wo_stage, xl_scr, halo, ub_scr, pr_scr, pi_scr, a_scr, b_scr, hcar, send_sems, recv_sems, local_sems):
        i = pl.program_id(0)
        start, finish = _gather_ops([wo_stage], [wo_all], send_sems, recv_sems, local_sems)

        @pl.when(i == 0)
        def _():
            wo_stage[...] = wo_ref[...].astype(bf16)
            start()
            xl_scr[1] = jnp.zeros((tm, LW), f32)
            halo[...] = jnp.zeros_like(halo)
            hcar[...] = jnp.zeros_like(hcar)

        rows_per = tm // 8
        xp_ref = xl_scr.at[(i + 1) % 2]
        sp = _softplus(-lam_ref[...])
        br, bi = br_ref[...], bi_ref[...]

        def lru_conv():
            xp = xp_ref[...]
            u = cb_ref[...] + sum(cw_ref[k:k + 1, :] * _later(xp, halo[...], CONVW - 1 - k) for k in range(CONVW))
            halo[...] = xp[tm - 8:tm, :]
            u_ref[...] = u
            ub_scr[...] = u.astype(bf16)

        def lru_gate_matmuls():
            for g in range(NGRP):
                gs = slice(256 * g, 256 * g + 256)
                pr_scr[:, gs] = _dot(ub_scr[:, gs], wr_ref[g])
                pi_scr[:, gs] = _dot(ub_scr[:, gs], wi_ref[g])

        def lru_terms(piece):
            rows = slice(rows_per * piece, rows_per * piece + rows_per)
            _, ig, a, mult, _ = _gate_terms(pr_scr[rows, :], pi_scr[rows, :], br, bi, sp)
            a_scr[rows, :] = a
            b_scr[rows, :] = mult * (ig * u_ref[rows, :])

        def lru_scan(piece, hc):
            for t in range(rows_per * piece, rows_per * piece + rows_per):
                hc = a_scr[t:t + 1, :] * hc + b_scr[t:t + 1, :]
                hl_ref[t:t + 1, :] = hc
            return hc

        def lru_piece(ci, hc):
            if ci == 0:
                lru_conv()
            elif ci == 1:
                lru_gate_matmuls()
            elif ci == 2:
                lru_terms(0)
                lru_terms(1)
            else:
                hc = lru_scan(ci - 3, hc)
                lru_terms(ci - 1)
            return hc

        xx = x_ref[...]
        rstd = lax.rsqrt(jnp.mean(xx * xx, axis=-1, keepdims=True) + EPS)
        h_ref[...] = (xx * rstd * g_ref[...]).astype(bf16)
        c, sa, sb = c_ref[...], sa_ref[...], sb_ref[...]
        hc = jnp.where(i >= 2, hcar[...], 0.0)

        def z_chunk(ci):
            return _dot_nt(h_ref[...], wt_ref[ci * nc:(ci + 1) * nc, :])

        for ci in range(2):
            z = z_chunk(ci)
            hc = lru_piece(ci, hc)
            for j in range(nc // 128):
                r = _rope(z[:, 128 * j:128 * j + 128], c, sa, sb) * (HD ** -0.5)
                q_ref[:, ci * nc + 128 * j:ci * nc + 128 * j + 128] = r.astype(bf16)
        z = z_chunk(2)
        hc = lru_piece(2, hc)
        for j in range(2):
            k_ref[:, 128 * j:128 * j + 128] = _rope(z[:, 128 * j:128 * j + 128], c, sa, sb).astype(bf16)
        v_ref[...] = z[:, 256:512].astype(bf16)
        for sec, ref in enumerate((ga_ref, xl_ref, gl_ref)):
            for j in range(2):
                z = z_chunk(3 + 2 * sec + j)
                hc = lru_piece(3 + 2 * sec + j, hc)
                ref[:, j * nc:(j + 1) * nc] = z
                if sec == 1:
                    xl_scr[i % 2, :, j * nc:(j + 1) * nc] = z
        hcar[...] = lru_scan(7, lru_scan(6, hc))

        @pl.when(i == nt)
        def _():
            finish()

    cur = lambda w: pl.BlockSpec((tm, w), lambda i: (jnp.minimum(i, nt - 1), 0))
    prev = pl.BlockSpec((tm, LW), lambda i: (jnp.maximum(i - 1, 0), 0))
    full = lambda a: pl.BlockSpec(a.shape, lambda i: (0,) * a.ndim)
    big = lambda w, dt: pltpu.HBM((s, w), dt)
    return pl.pallas_call(
        body, name="fwd_in_lru", grid=(nt + 1,),
        in_specs=[cur(D), full(ln_gain), full(wt), cur(128), cur(128), cur(128), full(wo_shard), full(conv_w),
                  full(conv_b), full(wr), full(wi), full(br), full(bi), full(lam)],
        out_specs=[cur(D), cur(D), cur(KVW), cur(KVW), cur(D), cur(D), cur(D), prev, prev, HBM],
        out_shape=[big(D, bf16), big(D, bf16), big(KVW, bf16), big(KVW, bf16), big(D, f32), big(D, f32), big(D, f32),
                   big(LW, f32), big(LW, f32), pltpu.HBM((2 * D, D), bf16)],
        scratch_shapes=[pltpu.VMEM((WO_ROWS, D), bf16), pltpu.VMEM((2, tm, LW), f32), pltpu.VMEM((8, LW), f32),
                        pltpu.VMEM((tm, LW), bf16)] + [pltpu.VMEM((tm, LW), f32)] * 4 + [pltpu.VMEM((1, LW), f32)]
        + _comm_sems(1),
        compiler_params=_params(("arbitrary",), 56),
    )(*_in_hbm(x), ln_gain, *_in_hbm(wt), *tabs, wo_shard, conv_w, conv_b, wr, wi, br, bi, lam)


def _fwd_pipeline(x, ln_gain, wt, tabs, wo_shard, conv_w, conv_b, wr, wi, br, bi, lam, tm):
    s = x.shape[0]
    nt = s // tm
    nc = 512
    pieces = 8
    rows_per = tm // pieces

    def body(x0_ref, xn_ref, g_ref, wt_ref, c_ref, sa_ref, sb_ref, wo_ref, cw_ref, cb_ref, wr_ref, wi_ref, br_ref,
             bi_ref, lam_ref, h_ref, q_ref, k_ref, v_ref, ga_ref, xl_ref, gl_ref, u_ref, hl_ref, wo_all,
             wo_stage, hb, xl_scr, halo, u_scr, ub_scr, pr_scr, pi_scr, a_scr, b_scr, hcar,
             send_sems, recv_sems, local_sems):
        i = pl.program_id(0)
        start, finish = _gather_ops([wo_stage], [wo_all], send_sems, recv_sems, local_sems)
        gain = g_ref[...]

        def normed(xx):
            rstd = lax.rsqrt(jnp.mean(xx * xx, axis=-1, keepdims=True) + EPS)
            return (xx * rstd * gain).astype(bf16)

        @pl.when(i == 0)
        def _():
            wo_stage[...] = wo_ref[...].astype(bf16)
            start()
            hb[0] = normed(x0_ref[...])
            xl_scr[1] = jnp.zeros((tm, LW), f32)
            u_scr[0] = jnp.zeros((tm, LW), f32)
            ub_scr[0] = jnp.zeros((tm, LW), bf16)
            halo[...] = jnp.zeros_like(halo)
            hcar[...] = jnp.zeros_like(hcar)

        cur, nxt = i % 2, (i + 1) % 2

        sp = _softplus(-lam_ref[...])
        br, bi = br_ref[...], bi_ref[...]
        c, sa, sb = c_ref[...], sa_ref[...], sb_ref[...]
        piece_rows = lambda p: slice(rows_per * p, rows_per * p + rows_per)

        def project(ci):
            z = _dot_nt(hb[cur], wt_ref[ci * nc:(ci + 1) * nc, :])
            if ci < 2:
                for j in range(nc // 128):
                    r = _rope(z[:, 128 * j:128 * j + 128], c, sa, sb) * (HD ** -0.5)
                    q_ref[:, ci * nc + 128 * j:ci * nc + 128 * j + 128] = r.astype(bf16)
            elif ci == 2:
                for j in range(2):
                    k_ref[:, 128 * j:128 * j + 128] = _rope(z[:, 128 * j:128 * j + 128], c, sa, sb).astype(bf16)
                v_ref[...] = z[:, 256:512].astype(bf16)
            else:
                sec, j = divmod(ci - 3, 2)
                (ga_ref, xl_ref, gl_ref)[sec][:, j * nc:(j + 1) * nc] = z
                if sec == 1:
                    xl_scr[cur, :, j * nc:(j + 1) * nc] = z

        def gate_matmuls():
            for g in range(NGRP):
                gs = slice(256 * g, 256 * g + 256)
                pr_scr[:, gs] = _dot(ub_scr[cur, :, gs], wr_ref[g])
                pi_scr[:, gs] = _dot(ub_scr[cur, :, gs], wi_ref[g])

        def gate_terms(p):
            rows = piece_rows(p)
            _, ig, a, mult, _ = _gate_terms(pr_scr[rows, :], pi_scr[rows, :], br, bi, sp)
            a_scr[rows, :] = a
            b_scr[rows, :] = mult * (ig * u_scr[cur, rows, :])

        def scan(p, hc):
            for t in range(rows_per * p, rows_per * p + rows_per):
                hc = a_scr[t:t + 1, :] * hc + b_scr[t:t + 1, :]
                hl_ref[t:t + 1, :] = hc
            return hc

        def conv(p):
            rows = piece_rows(p)
            xp = xl_scr[nxt, rows, :]
            before = halo[...] if p == 0 else xl_scr[nxt, rows_per * p - 8:rows_per * p, :]
            u = cb_ref[...] + sum(cw_ref[k:k + 1, :] * _later(xp, before, CONVW - 1 - k) for k in range(CONVW))
            u_scr[nxt, rows, :] = u
            ub_scr[nxt, rows, :] = u.astype(bf16)

        def norm(p):
            hb[nxt, piece_rows(p), :] = normed(xn_ref[piece_rows(p), :])

        h_ref[...] = hb[cur]
        gate_matmuls()
        hc = jnp.where(i >= 3, hcar[...], 0.0)
        for ci in range(NIN // nc):
            project(ci)
            if ci < pieces:
                conv(ci)
                norm(ci)
            if ci >= 1:
                gate_terms(ci - 1)
            if ci >= 2:
                hc = scan(ci - 2, hc)
        hcar[...] = scan(pieces - 1, hc)
        halo[...] = xl_scr[nxt, tm - 8:tm, :]

        @pl.when(i <= nt)
        def _():
            u_ref[...] = u_scr[nxt]

        @pl.when(i == nt + 1)
        def _():
            finish()

    at = lambda w, off: pl.BlockSpec((tm, w), lambda i: (jnp.clip(i + off, 0, nt - 1), 0))
    full = lambda a: pl.BlockSpec(a.shape, lambda i: (0,) * a.ndim)
    big = lambda w, dt: pltpu.HBM((s, w), dt)
    ring = lambda dt: pltpu.VMEM((2, tm, LW), dt)
    tile = pltpu.VMEM((tm, LW), f32)
    return pl.pallas_call(
        body, name="fwd_pipeline", grid=(nt + 2,),
        in_specs=[pl.BlockSpec((tm, D), lambda i: (0, 0)), at(D, 1), full(ln_gain), full(wt), at(128, 0), at(128, 0),
                  at(128, 0), full(wo_shard), full(conv_w), full(conv_b), full(wr), full(wi), full(br), full(bi),
                  full(lam)],
        out_specs=[at(D, 0), at(D, 0), at(KVW, 0), at(KVW, 0), at(D, 0), at(D, 0), at(D, 0), at(LW, -1), at(LW, -2),
                   HBM],
        out_shape=[big(D, bf16), big(D, bf16), big(KVW, bf16), big(KVW, bf16), big(D, f32), big(D, f32), big(D, f32),
                   big(LW, f32), big(LW, f32), pltpu.HBM((2 * D, D), bf16)],
        scratch_shapes=[pltpu.VMEM((WO_ROWS, D), bf16), ring(bf16), ring(f32), pltpu.VMEM((8, LW), f32), ring(f32),
                        ring(bf16), tile, tile, tile, tile, pltpu.VMEM((1, LW), f32)] + _comm_sems(1),
        compiler_params=_params(("arbitrary",), 56),
    )(*_in_hbm(x, x), ln_gain, *_in_hbm(wt), *tabs, wo_shard, conv_w, conv_b, wr, wi, br, bi, lam)


def _lru_bwd(u, hl, dhl, xl, conv_w, wr, wi, br, bi, lam, tm):
    s = u.shape[0]
    nt = s // tm

    def body(u_ref, h_ref, hp_ref, dh_ref, x_ref, xp_ref, cw_ref, wr_ref, wi_ref, br_ref, bi_ref, lam_ref,
             dxl_ref, dwr_ref, dwi_ref, dbr_ref, dbi_ref, dlam_ref, dcb_ref, dcw_ref,
             a_scr, l_scr, lcar, dunext):
        t0 = pl.program_id(0)
        tile = nt - 1 - t0

        @pl.when(t0 == 0)
        def _():
            lcar[...] = jnp.zeros_like(lcar)
            dunext[...] = jnp.zeros_like(dunext)
            for ref in (dwr_ref, dwi_ref, dbr_ref, dbi_ref, dlam_ref, dcb_ref, dcw_ref):
                ref[...] = jnp.zeros_like(ref)

        u = u_ref[...]
        lam = lam_ref[...]
        sp = _softplus(-lam)
        ub, r, i, a, mult, inv_mult = _gates(u, wr_ref, wi_ref, br_ref[...], bi_ref[...], sp)
        a_scr[...] = a

        def step(k, c):
            t = tm - 1 - k
            lt = dh_ref[pl.ds(t, 1), :] + c
            l_scr[pl.ds(t, 1), :] = lt
            return a_scr[pl.ds(t, 1), :] * lt

        lcar[...] = lax.fori_loop(0, tm, step, lcar[...], unroll=8)
        lt = l_scr[...]

        hprev = _later(h_ref[...], jnp.where(tile > 0, hp_ref[...], 0.0), 1)
        da = lt * hprev
        dmult = lt * (i * u)
        di = lt * mult * u
        du = lt * mult * i
        dla = da * a - dmult * (a * a) * inv_mult
        dr = dla * (-LRU_C * sp)
        dlam_ref[...] += jnp.sum(dla * (-LRU_C * r), axis=0, keepdims=True)
        dpr = dr * r * (1.0 - r)
        dpi = di * i * (1.0 - i)
        dbr_ref[...] += jnp.sum(dpr, axis=0, keepdims=True)
        dbi_ref[...] += jnp.sum(dpi, axis=0, keepdims=True)
        dprb, dpib = dpr.astype(bf16), dpi.astype(bf16)
        dug = []
        for g in range(NGRP):
            gs = slice(256 * g, 256 * g + 256)
            dwr_ref[g] += _dot_tn(ub[:, gs], dprb[:, gs])
            dwi_ref[g] += _dot_tn(ub[:, gs], dpib[:, gs])
            dug.append(_dot_nt(dprb[:, gs], wr_ref[g]) + _dot_nt(dpib[:, gs], wi_ref[g]))
        du = du + jnp.concatenate(dug, axis=1)

        dcb_ref[...] += jnp.sum(du, axis=0, keepdims=True)
        x = x_ref[...]
        before = jnp.where(tile > 0, xp_ref[...], 0.0)
        for k in range(CONVW):
            dcw_ref[k:k + 1, :] += jnp.sum(du * _later(x, before, CONVW - 1 - k), axis=0, keepdims=True)
        after = dunext[...]
        dxl = sum(cw_ref[k:k + 1, :] * _earlier(du, after, CONVW - 1 - k) for k in range(CONVW))
        dxl_ref[...] = dxl.astype(bf16)
        dunext[...] = du[0:8, :]

        @pl.when(t0 == nt - 1)
        def _():
            dlam_ref[...] = dlam_ref[...] * (-_sigmoid(-lam))

    rev = lambda i: (nt - 1 - i, 0)
    row = pl.BlockSpec((tm, LW), rev)
    prev8 = pl.BlockSpec((8, LW), lambda i: (jnp.maximum((nt - 1 - i) * (tm // 8) - 1, 0), 0))
    full = lambda a: pl.BlockSpec(a.shape, lambda i: (0,) * a.ndim)
    vec = pl.BlockSpec((1, LW), lambda i: (0, 0))
    bd = pl.BlockSpec((NGRP, 256, 256), lambda i: (0, 0, 0))
    return pl.pallas_call(
        body, name="lru_bwd", grid=(nt,),
        in_specs=[row, row, prev8, row, row, prev8, full(conv_w), full(wr), full(wi), full(br), full(bi), full(lam)],
        out_specs=[row, bd, bd, vec, vec, vec, vec, pl.BlockSpec((CONVW, LW), lambda i: (0, 0))],
        out_shape=[pltpu.HBM((s,LW), bf16),
                   jax.ShapeDtypeStruct((NGRP, 256, 256), f32), jax.ShapeDtypeStruct((NGRP, 256, 256), f32),
                   jax.ShapeDtypeStruct((1, LW), f32), jax.ShapeDtypeStruct((1, LW), f32),
                   jax.ShapeDtypeStruct((1, LW), f32), jax.ShapeDtypeStruct((1, LW), f32),
                   jax.ShapeDtypeStruct((CONVW, LW), f32)],
        scratch_shapes=[pltpu.VMEM((tm, LW), f32), pltpu.VMEM((tm, LW), f32),
                        pltpu.VMEM((1, LW), f32), pltpu.VMEM((8, LW), f32)],
        compiler_params=_params(("arbitrary",), 56),
    )(*_in_hbm(u, hl, hl, dhl, xl, xl), conv_w, wr, wi, br, bi, lam)


def _gated_norm(t, gate, gain):
    sg = _sigmoid(gate)
    silu = gate * sg
    p = t * silu
    rstd = lax.rsqrt(jnp.mean(p * p, axis=-1, keepdims=True) + EPS)
    ph = p * rstd
    return sg, silu, rstd, ph, ph * gain


def _gated_norm_bwd(dy, t, gate, gain, sg, silu, rstd, ph):
    w = dy * gain
    dp = rstd * (w - ph * jnp.mean(w * ph, axis=-1, keepdims=True))
    dgate = dp * t * (sg * (1.0 + gate * (1.0 - sg)))
    return jnp.sum(dy * ph, axis=0, keepdims=True), dp * silu, dgate


def _out_fwd_bwd(x, tgt, o, ga, hl, gl, again, lgain, fgain, wo, tm):
    s = x.shape[0]
    nt = s // tm

    def body(x_ref, t_ref, o_ref, ga_ref, hl_ref, gl_ref, ag_ref, lg_ref, fg_ref, wo_ref,
             dx2_ref, do_ref, dga_ref, dhl_ref, dgl_ref, dwo_ref, gfg_ref, gag_ref, glg_ref, loss_ref, acc):
        i = pl.program_id(0)

        @pl.when(i == 0)
        def _():
            acc[...] = jnp.zeros_like(acc)
            for ref in (gfg_ref, gag_ref, glg_ref, loss_ref):
                ref[...] = jnp.zeros_like(ref)

        oo, gga, hh, ggl = o_ref[...], ga_ref[...], hl_ref[...], gl_ref[...]
        ag, lg, fg = ag_ref[...], lg_ref[...], fg_ref[...]
        sga, silua, ra, pah, ya = _gated_norm(oo, gga, ag)
        sgl, silul, rl, plh, yl = _gated_norm(hh, ggl, lg)
        yab, ylb = ya.astype(bf16), yl.astype(bf16)
        y = _dot(yab, wo_ref[0:D, :]) + _dot(ylb, wo_ref[D:2 * D, :])
        x2 = x_ref[...] + y
        r2 = lax.rsqrt(jnp.mean(x2 * x2, axis=-1, keepdims=True) + EPS)
        x2h = x2 * r2
        err = x2h * fg - t_ref[...]
        loss_ref[...] += 0.5 * jnp.sum(jnp.sum(err * err, axis=-1, keepdims=True) * (1.0 / D))
        dout = err * (1.0 / D)
        gfg_ref[...] += jnp.sum(dout * x2h, axis=0, keepdims=True)
        w = dout * fg
        dx2 = r2 * (w - x2h * jnp.mean(w * x2h, axis=-1, keepdims=True))
        dx2_ref[...] = dx2
        dyb = dx2.astype(bf16)
        acc[0:D, :] += _dot_tn(yab, dyb)
        acc[D:2 * D, :] += _dot_tn(ylb, dyb)
        dya = _dot_nt(dyb, wo_ref[0:D, :])
        dyl = _dot_nt(dyb, wo_ref[D:2 * D, :])
        gag, do, dga = _gated_norm_bwd(dya, oo, gga, ag, sga, silua, ra, pah)
        glg, dhl, dgl = _gated_norm_bwd(dyl, hh, ggl, lg, sgl, silul, rl, plh)
        gag_ref[...] += gag
        glg_ref[...] += glg
        do_ref[...] = do.astype(bf16)
        dga_ref[...] = dga.astype(bf16)
        dhl_ref[...] = dhl
        dgl_ref[...] = dgl.astype(bf16)

        @pl.when(i == nt - 1)
        def _():
            dwo_ref[...] = acc[...].astype(bf16)

    row = pl.BlockSpec((tm, D), lambda i: (i, 0))
    vec = pl.BlockSpec((1, D), lambda i: (0, 0))
    mat = pl.BlockSpec((2 * D, D), lambda i: (0, 0))
    return pl.pallas_call(
        body, name="out_fwd_bwd", grid=(nt,),
        in_specs=[row] * 6 + [vec] * 3 + [mat],
        out_specs=[row] * 5 + [mat, vec, vec, vec, pl.BlockSpec((1, 128), lambda i: (0, 0))],
        out_shape=[pltpu.HBM((s,D), f32), pltpu.HBM((s,D), bf16),
                   pltpu.HBM((s,D), bf16), pltpu.HBM((s,D), f32),
                   pltpu.HBM((s,D), bf16), pltpu.HBM((2 * D, D), bf16),
                   jax.ShapeDtypeStruct((1, D), f32), jax.ShapeDtypeStruct((1, D), f32),
                   jax.ShapeDtypeStruct((1, D), f32), jax.ShapeDtypeStruct((1, 128), f32)],
        scratch_shapes=[pltpu.VMEM((2 * D, D), f32)],
        compiler_params=_params(("arbitrary",), 56),
    )(*_in_hbm(x, tgt, o, ga, hl, gl), again, lgain, fgain, *_in_hbm(wo))


def _bwd_in(x, dx2, dq, dk, dv, dga, dxl, dgl, ln_gain, wt, tabs, tm):
    s = x.shape[0]

    def body(x_ref, dx2_ref, dq_ref, dk_ref, dv_ref, dga_ref, dxl_ref, dgl_ref, g_ref, wt_ref,
             c_ref, sa_ref, sb_ref, gx_ref, gln_ref, dzt_ref, dz_scr):
        @pl.when(pl.program_id(0) == 0)
        def _():
            gln_ref[...] = jnp.zeros_like(gln_ref)

        c, sa, sb = c_ref[...], sa_ref[...], sb_ref[...]
        for j in range(D // 128):
            js = slice(128 * j, 128 * j + 128)
            dz_scr[:, js] = (_unrope(dq_ref[:, js], c, sa, sb) * (HD ** -0.5)).astype(bf16)
        for j in range(KVW // 128):
            js = slice(128 * j, 128 * j + 128)
            dz_scr[:, D + 128 * j:D + 128 * j + 128] = _unrope(dk_ref[:, js], c, sa, sb).astype(bf16)
        dz_scr[:, D + KVW:D + 2 * KVW] = dv_ref[...].astype(bf16)
        dz_scr[:, 1536:2560] = dga_ref[...]
        dz_scr[:, 2560:3584] = dxl_ref[...]
        dz_scr[:, 3584:4608] = dgl_ref[...]
        for j in range(NIN // 128):
            dzt_ref[128 * j:128 * j + 128, :] = dz_scr[:, 128 * j:128 * j + 128].T
        dh = _dot(dz_scr[:, 0:512], wt_ref[0:512, :])
        for ci in range(1, NIN // 512):
            dh = dh + _dot(dz_scr[:, 512 * ci:512 * ci + 512], wt_ref[512 * ci:512 * ci + 512, :])
        xx = x_ref[...]
        rstd = lax.rsqrt(jnp.mean(xx * xx, axis=-1, keepdims=True) + EPS)
        xh = xx * rstd
        gln_ref[...] += jnp.sum(dh * xh, axis=0, keepdims=True)
        w = dh * g_ref[...]
        gx_ref[...] = dx2_ref[...] + rstd * (w - xh * jnp.mean(w * xh, axis=-1, keepdims=True))

    row = lambda w: pl.BlockSpec((tm, w), lambda i: (i, 0))
    full = lambda a: pl.BlockSpec(a.shape, lambda i: (0, 0))
    return pl.pallas_call(
        body, name="bwd_in", grid=(s // tm,),
        in_specs=[row(D), row(D), row(D), row(KVW), row(KVW), row(D), row(D), row(D), full(ln_gain), full(wt),
                  row(128), row(128), row(128)],
        out_specs=[row(D), pl.BlockSpec((1, D), lambda i: (0, 0)), pl.BlockSpec((NIN, tm), lambda i: (0, i))],
        out_shape=[pltpu.HBM((s,D), f32), jax.ShapeDtypeStruct((1, D), f32),
                   pltpu.HBM((NIN, s), bf16)],
        scratch_shapes=[pltpu.VMEM((tm, NIN), bf16)],
        compiler_params=_params(("arbitrary",), 56),
    )(*_in_hbm(x, dx2, dq, dk, dv, dga, dxl, dgl), ln_gain, *_in_hbm(wt), *tabs)


WT_TERMS = 5


def _dwt_scatter(dzt, h, small, tm):
    s = h.shape[0]
    nk = s // tm
    srows = small.shape[0] // NDEV
    last = NDEV - 1

    def body(order_ref, dz_ref, h_ref, sm_ref, lwt_ref, lsm_ref, acc, stage, given, send_sems, recv_sems, local_sem,
             sm_send, sm_recv, sm_local):
        j, k = pl.program_id(0), pl.program_id(1)
        x, y, c = _place()
        sibling = (x, y, 1 - c)
        chips = [(1 - x, 1 - y), (1 - x, y), (x, 1 - y)]
        sm_start, sm_finish = _scatter_ops([sm_ref], [lsm_ref], sm_send, sm_recv, sm_local)

        def send(step):
            if step == last - 1:
                dst, to = lwt_ref.at[1], sibling
            elif step % 2 == 0:
                dst, to = given.at[step // 2], sibling
            else:
                dst, to = lwt_ref.at[2 + step // 2], (*chips[step // 2], c)
            return pltpu.make_async_remote_copy(
                src_ref=stage.at[step % 2], dst_ref=dst, send_sem=send_sems.at[step], recv_sem=recv_sems.at[step],
                device_id=to, device_id_type=MESH)

        def keep():
            return pltpu.make_async_copy(stage.at[last % 2], lwt_ref.at[0], local_sem)

        @pl.when((j == 0) & (k == 0))
        def _():
            sm_start()

        @pl.when(k == 0)
        def _():
            acc[...] = jnp.zeros_like(acc)

        acc[...] += _dot(dz_ref[...], h_ref[...])

        for step in range(NDEV):
            @pl.when((k == nk - 1) & (j == step))
            def _(step=step):
                if step >= 2:
                    send(step - 2).wait_send()
                if step % 2 == 1 and step < last:
                    send(step - 1).wait_recv()
                    stage[step % 2] = (acc[...] + given[step // 2].astype(f32)).astype(bf16)
                else:
                    stage[step % 2] = acc[...].astype(bf16)
                if step < last:
                    send(step).start()
                else:
                    keep().start()
                    send(last - 1).wait_send()
                    for peer_step in (1, 3, 5, last - 1):
                        send(peer_step).wait_recv()
                    keep().wait()
                    sm_finish()

    x, y, c = _place()
    dest = lambda cx, cy, cc: 4 * cx + 2 * cy + cc
    order = jnp.stack([dest(1 - x, 1 - y, 1 - c), dest(1 - x, 1 - y, c), dest(1 - x, y, 1 - c), dest(1 - x, y, c),
                       dest(x, 1 - y, 1 - c), dest(x, 1 - y, c), dest(x, y, 1 - c), dest(x, y, c)])
    return pl.pallas_call(
        body, name="dwt_scatter",
        grid_spec=pltpu.PrefetchScalarGridSpec(
            num_scalar_prefetch=1, grid=(NDEV, nk),
            in_specs=[pl.BlockSpec((WT_ROWS, tm), lambda j, k, order: (order[j], k)),
                      pl.BlockSpec((tm, D), lambda j, k, order: (k, 0)), HBM],
            out_specs=[HBM, HBM],
            scratch_shapes=[pltpu.VMEM((WT_ROWS, D), f32), pltpu.VMEM((2, WT_ROWS, D), bf16),
                            pltpu.VMEM((3, WT_ROWS, D), bf16),
                            pltpu.SemaphoreType.DMA((last,)), pltpu.SemaphoreType.DMA((last,)),
                            pltpu.SemaphoreType.DMA(())] + _comm_sems(1)),
        out_shape=[pltpu.HBM((WT_TERMS, WT_ROWS, D), bf16), pltpu.HBM((NDEV, srows, D), f32)],
        compiler_params=_params(("arbitrary", "arbitrary"), 32),
    )(order, *_in_hbm(dzt, h, small))


def _diag_blocks(bd):
    eye = jnp.eye(4, dtype=bd.dtype)
    return jnp.einsum('gjckd,jk->gjcd', bd.reshape(NGRP, 4, HD, 4, HD), eye).reshape(NQ, HD, HD)


def _sequence_step(x, tgt, wt, wo_shard, conv_w, p):
    s = x.shape[0]
    tm = min(256, s)
    tabs = _rope_tables(s)
    wr, wi = _block_diag(p["w_rgate"]), _block_diag(p["w_igate"])
    sinks = p["sinks"].reshape(NQ)
    h, q, k, v, ga, xl, gl, u, hl, wo = _fwd_pipeline(x, p["ln_gain"], wt, tabs, wo_shard, conv_w, p["conv_b"], wr, wi,
                                                      p["b_rgate"], p["b_igate"], p["lru_lambda"], tm)
    o = _attn_fwd(q, k, v, sinks)
    dx2, do, dga, dhl, dgl, dwo, g_fg, g_ag, g_lg, loss = _out_fwd_bwd(
        x, tgt, o, ga, hl, gl, p["attn_out_gain"], p["lru_out_gain"], p["final_gain"], wo, tm)
    dq, dk, dv, dsink, land_wo = _attn_bwd(q, k, v, do, sinks, dwo)
    dxl, dwr, dwi, dbr, dbi, dlam, dcb, dcw = _lru_bwd(
        u, hl, dhl, xl, conv_w, wr, wi, p["b_rgate"], p["b_igate"], p["lru_lambda"], tm)
    gx, g_ln, dzt = _bwd_in(x, dx2, dq, dk, dv, dga, dxl, dgl, p["ln_gain"], wt, tabs, tm)
    small = dict(ln_gain=g_ln, sinks=dsink.reshape(NQ, BLK).sum(axis=1)[None], conv_w=dcw, conv_b=dcb,
                 w_rgate=_diag_blocks(dwr), b_rgate=dbr, w_igate=_diag_blocks(dwi), b_igate=dbi, lru_lambda=dlam,
                 attn_out_gain=g_ag, lru_out_gain=g_lg, final_gain=g_fg)
    land_wt, land_sm = _dwt_scatter(dzt, h, _pack_small(small, loss), min(512, s))
    return gx, land_wt, land_wo, land_sm


def _all_gather(srcs, out_dtypes, name):
    n = len(srcs)
    cast = [a.dtype != dt for a, dt in zip(srcs, out_dtypes)]

    def body(*refs):
        src_refs, out_refs = refs[:n], refs[n:2 * n]
        stage_refs = list(refs[2 * n:2 * n + sum(cast)])
        mine_refs = []
        for a in range(n):
            if cast[a]:
                st = stage_refs.pop(0)
                st[...] = src_refs[a][...].astype(out_dtypes[a])
                mine_refs.append(st)
            else:
                mine_refs.append(src_refs[a])
        start, finish = _gather_ops(mine_refs, out_refs, *refs[-3:])
        start()
        finish()

    vmem = pl.BlockSpec(memory_space=pltpu.VMEM)
    return pl.pallas_call(
        body, name=name,
        in_specs=[vmem] * n, out_specs=[HBM] * n,
        out_shape=[pltpu.HBM((NDEV * a.shape[0], a.shape[1]), dt) for a, dt in zip(srcs, out_dtypes)],
        scratch_shapes=[pltpu.VMEM(a.shape, dt) for a, dt, cst in zip(srcs, out_dtypes, cast) if cst] + _comm_sems(n),
        compiler_params=pltpu.CompilerParams(vmem_limit_bytes=32 * MIB),
    )(*srcs)


def _sum_slots(land, tr, name):
    terms, rows, cols = land.shape

    def body(l_ref, o_ref):
        acc = l_ref[0].astype(f32)
        for d in range(1, terms):
            acc = acc + l_ref[d].astype(f32)
        o_ref[...] = acc

    return pl.pallas_call(
        body, name=name, grid=(rows // tr,),
        in_specs=[pl.BlockSpec((terms, tr, cols), lambda i: (0, i, 0))],
        out_specs=pl.BlockSpec((tr, cols), lambda i: (i, 0)),
        out_shape=jax.ShapeDtypeStruct((rows, cols), f32),
        compiler_params=_params(("arbitrary",), 32),
    )(*_in_hbm(land))


def _adam_math(w, g, m, v):
    m2 = ADAM_B1 * m + (1.0 - ADAM_B1) * g
    v2 = ADAM_B2 * v + (1.0 - ADAM_B2) * (g * g)
    m_hat = m2 / (1.0 - ADAM_B1 ** ADAM_STEP)
    v_hat = v2 / (1.0 - ADAM_B2 ** ADAM_STEP)
    delta = -ADAM_LR * (m_hat / (jnp.sqrt(v_hat) + ADAM_EPS) + ADAM_WD * w)
    return delta, m2, v2


def _adamw(w, g, m, v, tr, name):
    rows, cols = w.shape

    def body(w_ref, g_ref, m_ref, v_ref, d_ref, m2_ref, v2_ref):
        d_ref[...], m2_ref[...], v2_ref[...] = _adam_math(w_ref[...], g_ref[...], m_ref[...], v_ref[...])

    blk = pl.BlockSpec((tr, cols), lambda i: (i, 0))
    return pl.pallas_call(
        body, name=name, grid=(rows // tr,),
        in_specs=[blk] * 4, out_specs=[blk] * 3,
        out_shape=[jax.ShapeDtypeStruct((rows, cols), f32)] * 3,
        compiler_params=_params(("arbitrary",), 32),
    )(*_in_hbm(w, g, m, v))


VEC_NAMES = ("ln_gain", "conv_b", "b_rgate", "b_igate", "lru_lambda", "attn_out_gain", "lru_out_gain", "final_gain")
ROW_RGATE, ROW_IGATE, ROW_VEC, ROW_SINKS = 0, 64, 128, 136
LOSS_LANE = NQ


def _adamw_small(g_rep, g_conv, w, m, v):
    names = list(VEC_NAMES) + ["sinks", "conv_w", "w_rgate", "w_igate"]
    ins = [g_rep, g_conv] + [d[k] for k in names for d in (w, m, v)]

    def body(*refs):
        g_ref, gc_ref = refs[0], refs[1]
        in_refs = refs[2:2 + 3 * len(names)]
        out_refs = refs[2 + 3 * len(names):]

        def update(j, g, at=None):
            w_ref, m_ref, v_ref = in_refs[3 * j:3 * j + 3]
            outs = out_refs[4 * j:4 * j + 4]
            pick = (lambda r: r[...]) if at is None else (lambda r: r[at])
            res = (g,) + _adam_math(pick(w_ref), g, pick(m_ref), pick(v_ref))
            for o_ref, val in zip(outs, res):
                if at is None:
                    o_ref[...] = val
                else:
                    o_ref[at] = val

        for j in range(len(VEC_NAMES)):
            update(j, g_ref[ROW_VEC + j:ROW_VEC + j + 1, :])
        update(len(VEC_NAMES), g_ref[ROW_SINKS:ROW_SINKS + 1, 0:NQ])
        update(len(VEC_NAMES) + 1, gc_ref[...], at=0)
        for gi, row0 in ((len(VEC_NAMES) + 2, ROW_RGATE), (len(VEC_NAMES) + 3, ROW_IGATE)):
            for nb in range(NQ):
                update(gi, g_ref[row0:row0 + HD, HD * nb:HD * nb + HD], at=(0, nb))

    vmem = pl.BlockSpec(memory_space=pltpu.VMEM)
    out_shape = [jax.ShapeDtypeStruct(w[k].shape, f32) for k in names for _ in range(4)]
    outs = pl.pallas_call(
        body, name="adamw_small",
        in_specs=[vmem] * len(ins), out_specs=[vmem] * len(out_shape), out_shape=out_shape,
        compiler_params=pltpu.CompilerParams(vmem_limit_bytes=32 * MIB),
    )(*ins)
    return {k: tuple(outs[4 * j:4 * j + 4]) for j, k in enumerate(names)}


def _pack_small(small, loss):
    gate = lambda g: g.transpose(1, 0, 2).reshape(HD, NQ * HD)
    row_s = jnp.concatenate([small["sinks"], loss[:, LOSS_LANE:128], jnp.zeros((1, D - 128), f32)], axis=1)
    rep = jnp.concatenate([gate(small["w_rgate"]), gate(small["w_igate"])] + [small[k] for k in VEC_NAMES]
                          + [row_s, jnp.zeros((SMALL_ROWS - ROW_SINKS - 1, D), f32)], axis=0)
    conv = small["conv_w"].reshape(CONVW, NDEV, 128).transpose(1, 0, 2)
    conv = jnp.pad(conv, ((0, 0), (0, 8 - CONVW), (0, D - 128)))
    return jnp.concatenate([rep.reshape(NDEV, SMALL_PER, D), conv], axis=1).reshape(NDEV * (SMALL_PER + 8), D)


def kernel(x, ln_gain, w_in, sinks, conv_w, conv_b, w_rgate, b_rgate, w_igate, b_igate, lru_lambda, attn_out_gain, lru_out_gain, w_out, final_gain, loss_target, m_ln_gain, m_w_in, m_sinks, m_conv_w, m_conv_b, m_w_rgate, m_b_rgate, m_w_igate, m_b_igate, m_lru_lambda, m_attn_out_gain, m_lru_out_gain, m_w_out, m_final_gain, v_ln_gain, v_w_in, v_sinks, v_conv_w, v_conv_b, v_w_rgate, v_b_rgate, v_w_igate, v_b_igate, v_lru_lambda, v_attn_out_gain, v_lru_out_gain, v_w_out, v_final_gain):
    w = dict(ln_gain=ln_gain, sinks=sinks, conv_w=conv_w, conv_b=conv_b, w_rgate=w_rgate, b_rgate=b_rgate,
             w_igate=w_igate, b_igate=b_igate, lru_lambda=lru_lambda, attn_out_gain=attn_out_gain,
             lru_out_gain=lru_out_gain, final_gain=final_gain.reshape(1, D))
    m = dict(ln_gain=m_ln_gain, sinks=m_sinks, conv_w=m_conv_w, conv_b=m_conv_b, w_rgate=m_w_rgate,
             b_rgate=m_b_rgate, w_igate=m_w_igate, b_igate=m_b_igate, lru_lambda=m_lru_lambda,
             attn_out_gain=m_attn_out_gain, lru_out_gain=m_lru_out_gain, final_gain=m_final_gain.reshape(1, D))
    v = dict(ln_gain=v_ln_gain, sinks=v_sinks, conv_w=v_conv_w, conv_b=v_conv_b, w_rgate=v_w_rgate,
             b_rgate=v_b_rgate, w_igate=v_w_igate, b_igate=v_b_igate, lru_lambda=v_lru_lambda,
             attn_out_gain=v_attn_out_gain, lru_out_gain=v_lru_out_gain, final_gain=v_final_gain.reshape(1, D))

    conv_blk = jnp.pad(conv_w[0], ((0, 8 - CONVW), (0, 0)))
    wt, cw_all = _all_gather([w_in[0].T, conv_blk], [bf16, f32], "gather_weights")
    conv_full = cw_all.reshape(NDEV, 8, 128)[:, 0:CONVW].transpose(1, 0, 2).reshape(CONVW, LW)

    p = {k: (w[k][0] if k in ("w_rgate", "w_igate") else w[k]) for k in w if k != "conv_w"}
    gx, land_wt, land_wo, land_sm = _sequence_step(x[0], loss_target[0], wt, w_out[0], conv_full, p)

    g_wt = _sum_slots(land_wt, 192, "sum_wt")
    g_wo = _sum_slots(land_wo, 256, "sum_wo")
    g_sm = _sum_slots(land_sm, SMALL_PER + 8, "sum_small")
    (g_rep,) = _all_gather([g_sm[0:SMALL_PER]], [f32], "gather_small")
    g_conv = g_sm[SMALL_PER:SMALL_PER + CONVW, 0:128]

    d_win, m_win, v_win = _adamw(w_in[0].T, g_wt, m_w_in[0].T, v_w_in[0].T, 192, "adamw_w_in")
    g_win, d_win, m_win, v_win = (t.T for t in (g_wt, d_win, m_win, v_win))
    d_wo, m_wo, v_wo = _adamw(w_out[0], g_wo, m_w_out[0], v_w_out[0], 256, "adamw_w_out")
    res = _adamw_small(g_rep, g_conv, w, m, v)
    res["w_in"] = tuple(t[None] for t in (g_win, d_win, m_win, v_win))
    res["w_out"] = tuple(t[None] for t in (g_wo, d_wo, m_wo, v_wo))
    res["final_gain"] = tuple(t.reshape(D) for t in res["final_gain"])

    order = ("ln_gain", "w_in", "sinks", "conv_w", "conv_b", "w_rgate", "b_rgate", "w_igate", "b_igate",
             "lru_lambda", "attn_out_gain", "lru_out_gain", "w_out", "final_gain")
    total_loss = g_rep[ROW_SINKS, LOSS_LANE]
    return (total_loss, gx[None]) + tuple(res[k][i] for i in range(4) for k in order)
```

```python
import jax
import jax.numpy as jnp
from jax import lax
from jax.experimental import pallas as pl
from jax.experimental.pallas import tpu as pltpu

f32 = jnp.float32
bf16 = jnp.bfloat16

D = 1024
HD = 64
NQ = 16
NKV = 4
GROUP = NQ // NKV
KVW = NKV * HD
BLK = 128
ROT = 16
THETA = 500000.0
NEG = -1e30
LW = 1024
NGRP = 4
CONVW = 4
LRU_C = 8.0
NIN = 4608
EPS = 1e-6
NDEV = 8
WT_ROWS = NIN // NDEV
WO_ROWS = 2 * D // NDEV
SMALL_ROWS = 192
SMALL_PER = SMALL_ROWS // NDEV

ADAM_LR = 0.001
ADAM_B1 = 0.9
ADAM_B2 = 0.999
ADAM_EPS = 1e-08
ADAM_WD = 0.01
ADAM_STEP = 10

NT = (((1,), (1,)), ((), ()))
TN = (((0,), (0,)), ((), ()))
MESH = pl.DeviceIdType.MESH
MIB = 1024 * 1024


def _dot(a, b):
    return jnp.dot(a, b, preferred_element_type=f32)


def _dot_nt(a, b):
    return lax.dot_general(a, b, NT, preferred_element_type=f32)


def _dot_tn(a, b):
    return lax.dot_general(a, b, TN, preferred_element_type=f32)


def _params(sem, vmem_mib):
    return pltpu.CompilerParams(dimension_semantics=sem, vmem_limit_bytes=vmem_mib * MIB)


def _sigmoid(x):
    return 0.5 * jnp.tanh(0.5 * x) + 0.5


def _softplus(x):
    return jnp.maximum(x, 0.0) + jnp.log(1.0 + jnp.exp(-jnp.abs(x)))


def _rope_tables(s):
    pos = jnp.arange(s, dtype=f32)
    inv_freq = THETA ** (-jnp.arange(0, ROT, 2, dtype=f32) / ROT)
    ang = pos[:, None] * inv_freq[None, :]
    cs = jnp.concatenate([jnp.cos(ang) - 1.0, jnp.sin(ang)], axis=1)
    d = jnp.arange(128) % HD
    j = jnp.arange(ROT)[:, None]
    pick_c = ((d < ROT) & (j == d % (ROT // 2))).astype(f32)
    pick_sa = ((d >= ROT // 2) & (d < ROT) & (j == d)).astype(f32)
    pick_sb = -((d < ROT // 2) & (j == d + ROT // 2)).astype(f32)
    spread = lambda pick: jnp.dot(cs, pick, precision=lax.Precision.HIGHEST)
    return 1.0 + spread(pick_c), spread(pick_sa), spread(pick_sb)


def _rope(t, c, sa, sb):
    return t * c + pltpu.roll(t, 8, 1) * sa + pltpu.roll(t, 120, 1) * sb


def _unrope(dr, c, sa, sb):
    return dr * c + pltpu.roll(dr * sa, 120, 1) + pltpu.roll(dr * sb, 8, 1)


def _place():
    return lax.axis_index("x"), lax.axis_index("y"), lax.axis_index("c")


def _gather_ops(mine_refs, out_refs, send_sems, recv_sems, local_sems):
    n = len(mine_refs)
    x, y, c = _place()
    me, sibling = (x, y, c), (x, y, 1 - c)
    chips = [(1 - x, y), (x, 1 - y), (1 - x, 1 - y)]

    def rows(a, dev):
        m = mine_refs[a].shape[0]
        return out_refs[a].at[pl.ds((4 * dev[0] + 2 * dev[1] + dev[2]) * m, m), :]

    def copy(a, k, block, to, own=False):
        return pltpu.make_async_remote_copy(
            src_ref=mine_refs[a] if own else rows(a, block), dst_ref=rows(a, block),
            send_sem=send_sems.at[a, k], recv_sem=recv_sems.at[a, k], device_id=to, device_id_type=MESH)

    def local(a):
        return pltpu.make_async_copy(mine_refs[a], rows(a, me), local_sems.at[a])

    def first(a):
        return [copy(a, 0, me, sibling, own=True)] + [copy(a, 1 + j, me, (*chip, c), own=True)
                                                      for j, chip in enumerate(chips)]

    def start():
        for a in range(n):
            local(a).start()
            for cp in first(a):
                cp.start()

    def finish():
        for j, chip in enumerate(chips):
            for a in range(n):
                copy(a, 1 + j, (*chip, c), me).wait_recv()
                copy(a, 4 + j, (*chip, c), sibling).start()
        for a in range(n):
            copy(a, 0, sibling, me).wait_recv()
            for j, chip in enumerate(chips):
                copy(a, 4 + j, (*chip, 1 - c), me).wait_recv()
        for a in range(n):
            for cp in first(a) + [copy(a, 4 + j, (*chip, c), sibling) for j, chip in enumerate(chips)]:
                cp.wait_send()
            local(a).wait()

    return start, finish


def _scatter_ops(src_refs, land_refs, send_sems, recv_sems, local_sems):
    n = len(src_refs)
    x, y, c = _place()
    my = 4 * x + 2 * y + c

    def peer(k):
        return x ^ (k >> 2), y ^ ((k >> 1) & 1), c ^ (k & 1)

    def piece(a, dev):
        m = src_refs[a].shape[0] // NDEV
        return src_refs[a].at[pl.ds(dev * m, m), :]

    def local(a):
        return pltpu.make_async_copy(piece(a, my), land_refs[a].at[my], local_sems.at[a])

    def send(a, k):
        px, py, pc = peer(k)
        return pltpu.make_async_remote_copy(
            src_ref=piece(a, 4 * px + 2 * py + pc), dst_ref=land_refs[a].at[my],
            send_sem=send_sems.at[a, k - 1], recv_sem=recv_sems.at[a, k - 1],
            device_id=(px, py, pc), device_id_type=MESH)

    def arrival(a, k):
        px, py, pc = peer(k)
        return pltpu.make_async_remote_copy(
            src_ref=piece(a, my), dst_ref=land_refs[a].at[4 * px + 2 * py + pc],
            send_sem=send_sems.at[a, k - 1], recv_sem=recv_sems.at[a, k - 1],
            device_id=(px, py, pc), device_id_type=MESH)

    def start():
        for a in range(n):
            local(a).start()
        for k in range(1, NDEV):
            for a in range(n):
                send(a, k).start()

    def finish():
        for k in range(1, NDEV):
            for a in range(n):
                send(a, k).wait_send()
        for k in range(1, NDEV):
            for a in range(n):
                arrival(a, k).wait_recv()
        for a in range(n):
            local(a).wait()

    return start, finish


def _in_hbm(*arrays):
    return tuple(pltpu.with_memory_space_constraint(a, pltpu.HBM) for a in arrays)


def _comm_sems(n):
    return [pltpu.SemaphoreType.DMA((n, 7)), pltpu.SemaphoreType.DMA((n, 7)), pltpu.SemaphoreType.DMA((n,))]


HBM = pl.BlockSpec(memory_space=pltpu.HBM)


def _fwd_in(x, ln_gain, wt, tabs, wo_shard, tm):
    s = x.shape[0]
    nt = s // tm
    nc = 512

    def body(x_ref, g_ref, wt_ref, c_ref, sa_ref, sb_ref, wo_ref, h_ref, q_ref, k_ref, v_ref, ga_ref, xl_ref, gl_ref,
             wo_all, wo_stage, send_sems, recv_sems, local_sems):
        i = pl.program_id(0)
        start, finish = _gather_ops([wo_stage], [wo_all], send_sems, recv_sems, local_sems)

        @pl.when(i == 0)
        def _():
            wo_stage[...] = wo_ref[...].astype(bf16)
            start()

        xx = x_ref[...]
        rstd = lax.rsqrt(jnp.mean(xx * xx, axis=-1, keepdims=True) + EPS)
        h = (xx * rstd * g_ref[...]).astype(bf16)
        h_ref[...] = h
        c, sa, sb = c_ref[...], sa_ref[...], sb_ref[...]

        def z_chunk(ci):
            return _dot_nt(h, wt_ref[ci * nc:(ci + 1) * nc, :])

        for ci in range(2):
            z = z_chunk(ci)
            for j in range(nc // 128):
                r = _rope(z[:, 128 * j:128 * j + 128], c, sa, sb) * (HD ** -0.5)
                q_ref[:, ci * nc + 128 * j:ci * nc + 128 * j + 128] = r.astype(bf16)
        z = z_chunk(2)
        for j in range(2):
            k_ref[:, 128 * j:128 * j + 128] = _rope(z[:, 128 * j:128 * j + 128], c, sa, sb).astype(bf16)
        v_ref[...] = z[:, 256:512].astype(bf16)
        for sec, ref in enumerate((ga_ref, xl_ref, gl_ref)):
            for j in range(2):
                ref[:, j * nc:(j + 1) * nc] = z_chunk(3 + 2 * sec + j)

        @pl.when(i == nt - 1)
        def _():
            finish()

    row = lambda w: pl.BlockSpec((tm, w), lambda i: (i, 0))
    full = lambda a: pl.BlockSpec(a.shape, lambda i: (0, 0))
    return pl.pallas_call(
        body, name="fwd_in", grid=(nt,),
        in_specs=[row(D), full(ln_gain), full(wt), row(128), row(128), row(128), full(wo_shard)],
        out_specs=[row(D), row(D), row(KVW), row(KVW), row(D), row(D), row(D), HBM],
        out_shape=[pltpu.HBM((s,D), bf16), pltpu.HBM((s,D), bf16),
                   pltpu.HBM((s,KVW), bf16), pltpu.HBM((s,KVW), bf16),
                   pltpu.HBM((s,D), f32), pltpu.HBM((s,D), f32),
                   pltpu.HBM((s,D), f32), pltpu.HBM((2 * D, D), bf16)],
        scratch_shapes=[pltpu.VMEM((WO_ROWS, D), bf16)] + _comm_sems(1),
        compiler_params=_params(("arbitrary",), 48),
    )(*_in_hbm(x), ln_gain, *_in_hbm(wt), *tabs, wo_shard)


HSUB = 4
SUBW = HSUB * BLK


def _sub_probs(kh, qg, n, sink_row):
    jj = lax.broadcasted_iota(jnp.int32, (BLK, SUBW), 0)
    ii = lax.broadcasted_iota(jnp.int32, (BLK, SUBW), 1) % BLK
    from_prev = jj > ii
    s2 = _dot_nt(kh, qg)
    sc = jnp.where(from_prev, s2[0:BLK] + jnp.where(n > 0, 0.0, NEG), s2[BLK:2 * BLK])
    m = jnp.maximum(jnp.max(sc, axis=0, keepdims=True), sink_row)
    p = jnp.exp(sc - m)
    es = jnp.exp(sink_row - m)
    inv = 1.0 / (jnp.sum(p, axis=0, keepdims=True) + es)
    return from_prev, p * inv, es * inv


def _split(t, from_prev):
    t = t.astype(bf16)
    zero = jnp.zeros_like(t)
    return jnp.concatenate([jnp.where(from_prev, t, zero), jnp.where(from_prev, zero, t)], axis=0)


def _stack_heads(ref, first):
    return jnp.concatenate([ref[:, HD * (first + g):HD * (first + g) + HD] for g in range(HSUB)], axis=0)


def _sink_rows(sinks):
    return jnp.repeat(sinks.reshape(NKV, GROUP), BLK, axis=1)


def _kv_specs():
    prev = pl.BlockSpec((BLK, KVW), lambda n: (jnp.maximum(n - 1, 0), 0))
    cur = pl.BlockSpec((BLK, KVW), lambda n: (n, 0))
    return [prev, cur, prev, cur]


def _attn_fwd(q, k, v, sinks):
    s = q.shape[0]

    def body(sink_ref, q_ref, kp_ref, kc_ref, vp_ref, vc_ref, o_ref):
        n = pl.program_id(0)
        for h in range(NKV):
            hs = slice(HD * h, HD * h + HD)
            kh = jnp.concatenate([kp_ref[:, hs], kc_ref[:, hs]], axis=0)
            vh = jnp.concatenate([vp_ref[:, hs], vc_ref[:, hs]], axis=0)
            for t in range(GROUP // HSUB):
                first = GROUP * h + HSUB * t
                from_prev, pn, _ = _sub_probs(kh, _stack_heads(q_ref, first), n,
                                              sink_ref[h:h + 1, SUBW * t:SUBW * t + SUBW])
                og = _dot_tn(_split(pn, from_prev), vh)
                for g in range(HSUB):
                    o_ref[:, HD * (first + g):HD * (first + g) + HD] = og[BLK * g:BLK * g + BLK]

    return pl.pallas_call(
        body, name="attn_fwd", grid=(s // BLK,),
        in_specs=[pl.BlockSpec((NKV, GROUP * BLK), lambda n: (0, 0)), pl.BlockSpec((BLK, D), lambda n: (n, 0))]
        + _kv_specs(),
        out_specs=pl.BlockSpec((BLK, D), lambda n: (n, 0)),
        out_shape=pltpu.HBM((s,D), f32),
        compiler_params=_params(("arbitrary",), 32),
    )(_sink_rows(sinks), *_in_hbm(q, k, k, v, v))


def _attn_bwd(q, k, v, do, sinks, dwo):
    s = q.shape[0]
    nb = s // BLK

    def body(sink_ref, q_ref, do_ref, kp_ref, kc_ref, vp_ref, vc_ref, dwo_ref, dq_ref, dk_ref, dv_ref, ds_ref,
             land_ref, send_sems, recv_sems, local_sems):
        n = pl.program_id(0)
        start, finish = _scatter_ops([dwo_ref], [land_ref], send_sems, recv_sems, local_sems)

        @pl.when(n == 0)
        def _():
            start()
            dk_ref[...] = jnp.zeros_like(dk_ref)
            dv_ref[...] = jnp.zeros_like(dv_ref)
            ds_ref[...] = jnp.zeros_like(ds_ref)

        prev_rows = pl.ds(pl.multiple_of(jnp.maximum(n - 1, 0) * BLK, BLK), BLK)
        cur_rows = pl.ds(pl.multiple_of(n * BLK, BLK), BLK)
        for h in range(NKV):
            hs = slice(HD * h, HD * h + HD)
            kh = jnp.concatenate([kp_ref[:, hs], kc_ref[:, hs]], axis=0)
            vh = jnp.concatenate([vp_ref[:, hs], vc_ref[:, hs]], axis=0)
            dkh = jnp.zeros((2 * BLK, HD), f32)
            dvh = jnp.zeros((2 * BLK, HD), f32)
            for t in range(GROUP // HSUB):
                first = GROUP * h + HSUB * t
                lanes = slice(SUBW * t, SUBW * t + SUBW)
                qg, dog = _stack_heads(q_ref, first), _stack_heads(do_ref, first)
                from_prev, pn, ps = _sub_probs(kh, qg, n, sink_ref[h:h + 1, lanes])
                dp2 = _dot_nt(vh, dog)
                dp = jnp.where(from_prev, dp2[0:BLK], dp2[BLK:2 * BLK])
                dsum = jnp.sum(pn * dp, axis=0, keepdims=True)
                ds_ref[h:h + 1, lanes] += -ps * dsum
                ds2 = _split(pn * (dp - dsum), from_prev)
                dqg = _dot_tn(ds2, kh)
                for g in range(HSUB):
                    dq_ref[:, HD * (first + g):HD * (first + g) + HD] = dqg[BLK * g:BLK * g + BLK]
                dkh = dkh + _dot(ds2, qg)
                dvh = dvh + _dot(_split(pn, from_prev), dog)
            dk_ref[prev_rows, hs] += dkh[0:BLK]
            dk_ref[cur_rows, hs] += dkh[BLK:2 * BLK]
            dv_ref[prev_rows, hs] += dvh[0:BLK]
            dv_ref[cur_rows, hs] += dvh[BLK:2 * BLK]

        @pl.when(n == nb - 1)
        def _():
            finish()

    blk = pl.BlockSpec((BLK, D), lambda n: (n, 0))
    whole = lambda r, w: pl.BlockSpec((r, w), lambda n: (0, 0))
    return pl.pallas_call(
        body, name="attn_bwd", grid=(nb,),
        in_specs=[whole(NKV, GROUP * BLK), blk, blk] + _kv_specs() + [HBM],
        out_specs=[blk, whole(s, KVW), whole(s, KVW), whole(NKV, GROUP * BLK), HBM],
        out_shape=[pltpu.HBM((s,D), f32), pltpu.HBM((s,KVW), f32),
                   pltpu.HBM((s,KVW), f32), jax.ShapeDtypeStruct((NKV, GROUP * BLK), f32),
                   pltpu.HBM((NDEV, WO_ROWS, D), bf16)],
        scratch_shapes=_comm_sems(1),
        compiler_params=_params(("arbitrary",), 48),
    )(_sink_rows(sinks), *_in_hbm(q, do, k, k, v, v, dwo))


def _block_diag(w):
    w4 = w.reshape(NGRP, 4, HD, HD)
    eye = jnp.eye(4, dtype=w.dtype)
    return jnp.einsum('gjcd,jk->gjckd', w4, eye).reshape(NGRP, 256, 256).astype(bf16)


def _gate_terms(pr, pi, br, bi, sp):
    r = _sigmoid(pr + br)
    i = _sigmoid(pi + bi)
    la = -LRU_C * r * sp
    a = jnp.exp(la)
    x2 = 2.0 * la
    y = jnp.where(x2 > -0.02, -x2 * (1.0 + x2 * (0.5 + x2 * (1.0 / 6.0))), 1.0 - a * a)
    inv_mult = lax.rsqrt(jnp.maximum(y, 1e-30))
    return r, i, a, y * inv_mult, inv_mult


def _gates(u, wr_ref, wi_ref, br, bi, sp):
    ub = u.astype(bf16)
    pr = jnp.concatenate([_dot(ub[:, 256 * g:256 * g + 256], wr_ref[g]) for g in range(NGRP)], axis=1)
    pi = jnp.concatenate([_dot(ub[:, 256 * g:256 * g + 256], wi_ref[g]) for g in range(NGRP)], axis=1)
    return (ub,) + _gate_terms(pr, pi, br, bi, sp)


def _later(x, before, k):
    if k == 0:
        return x
    row = lax.broadcasted_iota(jnp.int32, before.shape, 0)
    rolled = pltpu.roll(x, k, 0)
    first = jnp.where(row < k, pltpu.roll(before, k, 0), rolled[0:8])
    return jnp.concatenate([first, rolled[8:]], axis=0)


def _earlier(x, after, k):
    if k == 0:
        return x
    n = x.shape[0]
    row = lax.broadcasted_iota(jnp.int32, after.shape, 0)
    rolled = pltpu.roll(x, n - k, 0)
    last = jnp.where(row >= 8 - k, pltpu.roll(after, 8 - k, 0), rolled[n - 8:n])
    return jnp.concatenate([rolled[0:n - 8], last], axis=0)


def _lru_fwd(xl, conv_w, conv_b, wr, wi, br, bi, lam, tm):
    s = xl.shape[0]

    def body(xp_ref, x_ref, cw_ref, cb_ref, wr_ref, wi_ref, br_ref, bi_ref, lam_ref, u_ref, h_ref,
             a_scr, b_scr, hcar):
        t0 = pl.program_id(0)

        @pl.when(t0 == 0)
        def _():
            hcar[...] = jnp.zeros_like(hcar)

        x = x_ref[...]
        before = jnp.where(t0 > 0, xp_ref[...], 0.0)
        u = cb_ref[...] + sum(cw_ref[k:k + 1, :] * _later(x, before, CONVW - 1 - k) for k in range(CONVW))
        u_ref[...] = u
        sp = _softplus(-lam_ref[...])
        _, _, i, a, mult, _ = _gates(u, wr_ref, wi_ref, br_ref[...], bi_ref[...], sp)
        a_scr[...] = a
        b_scr[...] = mult * (i * u)

        def step(t, hc):
            hn = a_scr[pl.ds(t, 1), :] * hc + b_scr[pl.ds(t, 1), :]
            h_ref[pl.ds(t, 1), :] = hn
            return hn

        hcar[...] = lax.fori_loop(0, tm, step, hcar[...], unroll=8)

    row = pl.BlockSpec((tm, LW), lambda i: (i, 0))
    prev8 = pl.BlockSpec((8, LW), lambda i: (jnp.maximum(i * (tm // 8) - 1, 0), 0))
    full = lambda a: pl.BlockSpec(a.shape, lambda i: (0,) * a.ndim)
    return pl.pallas_call(
        body, name="lru_fwd", grid=(s // tm,),
        in_specs=[prev8, row, full(conv_w), full(conv_b), full(wr), full(wi), full(br), full(bi), full(lam)],
        out_specs=[row, row],
        out_shape=[pltpu.HBM((s,LW), f32), pltpu.HBM((s,LW), f32)],
        scratch_shapes=[pltpu.VMEM((tm, LW), f32), pltpu.VMEM((tm, LW), f32), pltpu.VMEM((1, LW), f32)],
        compiler_params=_params(("arbitrary",), 48),
    )(*_in_hbm(xl, xl), conv_w, conv_b, wr, wi, br, bi, lam)


def _fwd_in_lru(x, ln_gain, wt, tabs, wo_shard, conv_w, conv_b, wr, wi, br, bi, lam, tm):
    s = x.shape[0]
    nt = s // tm
    nc = 512

    def body(x_ref, g_ref, wt_ref, c_ref, sa_ref, sb_ref, wo_ref, cw_ref, cb_ref, wr_ref, wi_ref, br_ref, bi_ref,
             lam_ref, h_ref, q_ref, k_ref, v_ref, ga_ref, xl_ref, gl_ref, u_ref, hl_ref, wo_all,
             wo_stage, xl_scr, halo, ub_scr, pr_scr, pi_scr, a_scr, b_scr, hcar, send_sems, recv_sems, local_sems):
        i = pl.program_id(0)
        start, finish = _gather_ops([wo_stage], [wo_all], send_sems, recv_sems, local_sems)

        @pl.when(i == 0)
        def _():
            wo_stage[...] = wo_ref[...].astype(bf16)
            start()
            xl_scr[1] = jnp.zeros((tm, LW), f32)
            halo[...] = jnp.zeros_like(halo)
            hcar[...] = jnp.zeros_like(hcar)

        rows_per = tm // 8
        xp_ref = xl_scr.at[(i + 1) % 2]
        sp = _softplus(-lam_ref[...])
        br, bi = br_ref[...], bi_ref[...]

        def lru_conv():
            xp = xp_ref[...]
            u = cb_ref[...] + sum(cw_ref[k:k + 1, :] * _later(xp, halo[...], CONVW - 1 - k) for k in range(CONVW))
            halo[...] = xp[tm - 8:tm, :]
            u_ref[...] = u
            ub_scr[...] = u.astype(bf16)

        def lru_gate_matmuls():
            for g in range(NGRP):
                gs = slice(256 * g, 256 * g + 256)
                pr_scr[:, gs] = _dot(ub_scr[:, gs], wr_ref[g])
                pi_scr[:, gs] = _dot(ub_scr[:, gs], wi_ref[g])

        def lru_terms(piece):
            rows = slice(rows_per * piece, rows_per * piece + rows_per)
            _, ig, a, mult, _ = _gate_terms(pr_scr[rows, :], pi_scr[rows, :], br, bi, sp)
            a_scr[rows, :] = a
            b_scr[rows, :] = mult * (ig * u_ref[rows, :])

        def lru_scan(piece, hc):
            for t in range(rows_per * piece, rows_per * piece + rows_per):
                hc = a_scr[t:t + 1, :] * hc + b_scr[t:t + 1, :]
                hl_ref[t:t + 1, :] = hc
            return hc

        def lru_piece(ci, hc):
            if ci == 0:
                lru_conv()
            elif ci == 1:
                lru_gate_matmuls()
            elif ci == 2:
                lru_terms(0)
                lru_terms(1)
            else:
                hc = lru_scan(ci - 3, hc)
                lru_terms(ci - 1)
            return hc

        xx = x_ref[...]
        rstd = lax.rsqrt(jnp.mean(xx * xx, axis=-1, keepdims=True) + EPS)
        h_ref[...] = (xx * rstd * g_ref[...]).astype(bf16)
        c, sa, sb = c_ref[...], sa_ref[...], sb_ref[...]
        hc = jnp.where(i >= 2, hcar[...], 0.0)

        def z_chunk(ci):
            return _dot_nt(h_ref[...], wt_ref[ci * nc:(ci + 1) * nc, :])

        for ci in range(2):
            z = z_chunk(ci)
            hc = lru_piece(ci, hc)
            for j in range(nc // 128):
                r = _rope(z[:, 128 * j:128 * j + 128], c, sa, sb) * (HD ** -0.5)
                q_ref[:, ci * nc + 128 * j:ci * nc + 128 * j + 128] = r.astype(bf16)
        z = z_chunk(2)
        hc = lru_piece(2, hc)
        for j in range(2):
            k_ref[:, 128 * j:128 * j + 128] = _rope(z[:, 128 * j:128 * j + 128], c, sa, sb).astype(bf16)
        v_ref[...] = z[:, 256:512].astype(bf16)
        for sec, ref in enumerate((ga_ref, xl_ref, gl_ref)):
            for j in range(2):
                z = z_chunk(3 + 2 * sec + j)
                hc = lru_piece(3 + 2 * sec + j, hc)
                ref[:, j * nc:(j + 1) * nc] = z
                if sec == 1:
                    xl_scr[i % 2, :, j * nc:(j + 1) * nc] = z
        hcar[...] = lru_scan(7, lru_scan(6, hc))

        @pl.when(i == nt)
        def _():
            finish()

    cur = lambda w: pl.BlockSpec((tm, w), lambda i: (jnp.minimum(i, nt - 1), 0))
    prev = pl.BlockSpec((tm, LW), lambda i: (jnp.maximum(i - 1, 0), 0))
    full = lambda a: pl.BlockSpec(a.shape, lambda i: (0,) * a.ndim)
    big = lambda w, dt: pltpu.HBM((s, w), dt)
    return pl.pallas_call(
        body, name="fwd_in_lru", grid=(nt + 1,),
        in_specs=[cur(D), full(ln_gain), full(wt), cur(128), cur(128), cur(128), full(wo_shard), full(conv_w),
                  full(conv_b), full(wr), full(wi), full(br), full(bi), full(lam)],
        out_specs=[cur(D), cur(D), cur(KVW), cur(KVW), cur(D), cur(D), cur(D), prev, prev, HBM],
        out_shape=[big(D, bf16), big(D, bf16), big(KVW, bf16), big(KVW, bf16), big(D, f32), big(D, f32), big(D, f32),
                   big(LW, f32), big(LW, f32), pltpu.HBM((2 * D, D), bf16)],
        scratch_shapes=[pltpu.VMEM((WO_ROWS, D), bf16), pltpu.VMEM((2, tm, LW), f32), pltpu.VMEM((8, LW), f32),
                        pltpu.VMEM((tm, LW), bf16)] + [pltpu.VMEM((tm, LW), f32)] * 4 + [pltpu.VMEM((1, LW), f32)]
        + _comm_sems(1),
        compiler_params=_params(("arbitrary",), 56),
    )(*_in_hbm(x), ln_gain, *_in_hbm(wt), *tabs, wo_shard, conv_w, conv_b, wr, wi, br, bi, lam)


def _fwd_pipeline(x, ln_gain, wt, tabs, wo_shard, conv_w, conv_b, wr, wi, br, bi, lam, tm):
    s = x.shape[0]
    nt = s // tm
    nc = 512
    pieces = 8
    rows_per = tm // pieces

    def body(x0_ref, xn_ref, g_ref, wt_ref, c_ref, sa_ref, sb_ref, wo_ref, cw_ref, cb_ref, wr_ref, wi_ref, br_ref,
             bi_ref, lam_ref, h_ref, q_ref, k_ref, v_ref, ga_ref, xl_ref, gl_ref, u_ref, hl_ref, wo_all,
             wo_stage, hb, xl_scr, halo, u_scr, ub_scr, pr_scr, pi_scr, a_scr, b_scr, hcar,
             send_sems, recv_sems, local_sems):
        i = pl.program_id(0)
        start, finish = _gather_ops([wo_stage], [wo_all], send_sems, recv_sems, local_sems)
        gain = g_ref[...]

        def normed(xx):
            rstd = lax.rsqrt(jnp.mean(xx * xx, axis=-1, keepdims=True) + EPS)
            return (xx * rstd * gain).astype(bf16)

        @pl.when(i == 0)
        def _():
            wo_stage[...] = wo_ref[...].astype(bf16)
            start()
            hb[0] = normed(x0_ref[...])
            xl_scr[1] = jnp.zeros((tm, LW), f32)
            u_scr[0] = jnp.zeros((tm, LW), f32)
            ub_scr[0] = jnp.zeros((tm, LW), bf16)
            halo[...] = jnp.zeros_like(halo)
            hcar[...] = jnp.zeros_like(hcar)

        cur, nxt = 0, 1

        sp = _softplus(-lam_ref[...])
        br, bi = br_ref[...], bi_ref[...]
        c, sa, sb = c_ref[...], sa_ref[...], sb_ref[...]
        piece_rows = lambda p: slice(rows_per * p, rows_per * p + rows_per)

        def project(ci):
            z = _dot_nt(hb[cur], wt_ref[ci * nc:(ci + 1) * nc, :])
            if ci < 2:
                for j in range(nc // 128):
                    r = _rope(z[:, 128 * j:128 * j + 128], c, sa, sb) * (HD ** -0.5)
                    q_ref[:, ci * nc + 128 * j:ci * nc + 128 * j + 128] = r.astype(bf16)
            elif ci == 2:
                for j in range(2):
                    k_ref[:, 128 * j:128 * j + 128] = _rope(z[:, 128 * j:128 * j + 128], c, sa, sb).astype(bf16)
                v_ref[...] = z[:, 256:512].astype(bf16)
            else:
                sec, j = divmod(ci - 3, 2)
                (ga_ref, xl_ref, gl_ref)[sec][:, j * nc:(j + 1) * nc] = z
                if sec == 1:
                    xl_scr[cur, :, j * nc:(j + 1) * nc] = z

        def gate_matmuls():
            for g in range(NGRP):
                gs = slice(256 * g, 256 * g + 256)
                pr_scr[:, gs] = _dot(ub_scr[cur, :, gs], wr_ref[g])
                pi_scr[:, gs] = _dot(ub_scr[cur, :, gs], wi_ref[g])

        def gate_terms(p):
            rows = piece_rows(p)
            _, ig, a, mult, _ = _gate_terms(pr_scr[rows, :], pi_scr[rows, :], br, bi, sp)
            a_scr[rows, :] = a
            b_scr[rows, :] = mult * (ig * u_scr[cur, rows, :])

        def scan(p, hc):
            for t in range(rows_per * p, rows_per * p + rows_per):
                hc = a_scr[t:t + 1, :] * hc + b_scr[t:t + 1, :]
                hl_ref[t:t + 1, :] = hc
            return hc

        def conv(p):
            rows = piece_rows(p)
            xp = xl_scr[nxt, rows, :]
            before = halo[...] if p == 0 else xl_scr[nxt, rows_per * p - 8:rows_per * p, :]
            u = cb_ref[...] + sum(cw_ref[k:k + 1, :] * _later(xp, before, CONVW - 1 - k) for k in range(CONVW))
            u_scr[nxt, rows, :] = u
            ub_scr[nxt, rows, :] = u.astype(bf16)

        def norm(p):
            hb[nxt, piece_rows(p), :] = normed(xn_ref[piece_rows(p), :])

        def run():
            h_ref[...] = hb[cur]
            gate_matmuls()
            hc = jnp.where(i >= 3, hcar[...], 0.0)
            for ci in range(NIN // nc):
                project(ci)
                if ci < pieces:
                    conv(ci)
                    norm(ci)
                if ci >= 1:
                    gate_terms(ci - 1)
                if ci >= 2:
                    hc = scan(ci - 2, hc)
            hcar[...] = scan(pieces - 1, hc)
            halo[...] = xl_scr[nxt, tm - 8:tm, :]

            @pl.when(i <= nt)
            def _():
                u_ref[...] = u_scr[nxt]

        for parity in range(2):
            cur, nxt = parity, 1 - parity
            pl.when(i % 2 == parity)(run)

        @pl.when(i == nt + 1)
        def _():
            finish()

    at = lambda w, off: pl.BlockSpec((tm, w), lambda i: (jnp.clip(i + off, 0, nt - 1), 0))
    full = lambda a: pl.BlockSpec(a.shape, lambda i: (0,) * a.ndim)
    big = lambda w, dt: pltpu.HBM((s, w), dt)
    ring = lambda dt: pltpu.VMEM((2, tm, LW), dt)
    tile = pltpu.VMEM((tm, LW), f32)
    return pl.pallas_call(
        body, name="fwd_pipeline", grid=(nt + 2,),
        in_specs=[pl.BlockSpec((tm, D), lambda i: (0, 0)), at(D, 1), full(ln_gain), full(wt), at(128, 0), at(128, 0),
                  at(128, 0), full(wo_shard), full(conv_w), full(conv_b), full(wr), full(wi), full(br), full(bi),
                  full(lam)],
        out_specs=[at(D, 0), at(D, 0), at(KVW, 0), at(KVW, 0), at(D, 0), at(D, 0), at(D, 0), at(LW, -1), at(LW, -2),
                   HBM],
        out_shape=[big(D, bf16), big(D, bf16), big(KVW, bf16), big(KVW, bf16), big(D, f32), big(D, f32), big(D, f32),
                   big(LW, f32), big(LW, f32), pltpu.HBM((2 * D, D), bf16)],
        scratch_shapes=[pltpu.VMEM((WO_ROWS, D), bf16), ring(bf16), ring(f32), pltpu.VMEM((8, LW), f32), ring(f32),
                        ring(bf16), tile, tile, tile, tile, pltpu.VMEM((1, LW), f32)] + _comm_sems(1),
        compiler_params=_params(("arbitrary",), 56),
    )(*_in_hbm(x, x), ln_gain, *_in_hbm(wt), *tabs, wo_shard, conv_w, conv_b, wr, wi, br, bi, lam)


def _fwd_fused(x, ln_gain, wt, tabs, wo_shard, conv_w, conv_b, wr, wi, br, bi, lam, tm):
    s = x.shape[0]
    nt = s // tm
    nc = 512
    pieces = 8
    rows_per = tm // pieces
    later_chunks = (0, 1, 2, 3, 4, 7, 8)

    def body(x0_ref, xn_ref, g_ref, wt_ref, c_ref, sa_ref, sb_ref, wo_ref, cw_ref, cb_ref, wr_ref, wi_ref, br_ref,
             bi_ref, lam_ref, h_ref, q_ref, k_ref, v_ref, ga_ref, xl_ref, gl_ref, u_ref, hl_ref, wo_all,
             wo_stage, hb, halo, ub_scr, pr_scr, pi_scr, a_scr, b_scr, hcar, send_sems, recv_sems, local_sems):
        i = pl.program_id(0)
        start, finish = _gather_ops([wo_stage], [wo_all], send_sems, recv_sems, local_sems)
        gain = g_ref[...]

        def normed(xx):
            rstd = lax.rsqrt(jnp.mean(xx * xx, axis=-1, keepdims=True) + EPS)
            return (xx * rstd * gain).astype(bf16)

        @pl.when(i == 0)
        def _():
            wo_stage[...] = wo_ref[...].astype(bf16)
            start()
            hb[0] = normed(x0_ref[...])
            halo[...] = jnp.zeros_like(halo)
            hcar[...] = jnp.zeros_like(hcar)

        cur, nxt = i % 2, (i + 1) % 2
        sp = _softplus(-lam_ref[...])
        br, bi = br_ref[...], bi_ref[...]
        c, sa, sb = c_ref[...], sa_ref[...], sb_ref[...]
        piece_rows = lambda p: slice(rows_per * p, rows_per * p + rows_per)

        def project(ci):
            z = _dot_nt(hb[cur], wt_ref[ci * nc:(ci + 1) * nc, :])
            if ci < 2:
                for j in range(nc // 128):
                    r = _rope(z[:, 128 * j:128 * j + 128], c, sa, sb) * (HD ** -0.5)
                    q_ref[:, ci * nc + 128 * j:ci * nc + 128 * j + 128] = r.astype(bf16)
            elif ci == 2:
                for j in range(2):
                    k_ref[:, 128 * j:128 * j + 128] = _rope(z[:, 128 * j:128 * j + 128], c, sa, sb).astype(bf16)
                v_ref[...] = z[:, 256:512].astype(bf16)
            else:
                sec, j = divmod(ci - 3, 2)
                (ga_ref, xl_ref, gl_ref)[sec][:, j * nc:(j + 1) * nc] = z

        def gate_terms(p):
            rows = piece_rows(p)
            _, ig, a, mult, _ = _gate_terms(pr_scr[rows, :], pi_scr[rows, :], br, bi, sp)
            a_scr[rows, :] = a
            b_scr[rows, :] = mult * (ig * u_ref[rows, :])

        def scan(p, hc):
            for t in range(rows_per * p, rows_per * p + rows_per):
                hc = a_scr[t:t + 1, :] * hc + b_scr[t:t + 1, :]
                hl_ref[t:t + 1, :] = hc
            return hc

        def norm_next(p):
            hb[nxt, piece_rows(p), :] = normed(xn_ref[piece_rows(p), :])

        h_ref[...] = hb[cur]
        project(5)
        project(6)
        xl = xl_ref[...]
        u = cb_ref[...] + sum(cw_ref[k:k + 1, :] * _later(xl, halo[...], CONVW - 1 - k) for k in range(CONVW))
        halo[...] = xl[tm - 8:tm, :]
        u_ref[...] = u
        ub_scr[...] = u.astype(bf16)
        for g in range(NGRP):
            gs = slice(256 * g, 256 * g + 256)
            pr_scr[:, gs] = _dot(ub_scr[:, gs], wr_ref[g])
            pi_scr[:, gs] = _dot(ub_scr[:, gs], wi_ref[g])
        hc = hcar[...]
        gate_terms(0)
        for slot, ci in enumerate(later_chunks):
            project(ci)
            norm_next(slot)
            gate_terms(slot + 1)
            hc = scan(slot, hc)
        norm_next(pieces - 1)
        hcar[...] = scan(pieces - 1, hc)

        @pl.when(i == nt - 1)
        def _():
            finish()

    row = lambda w: pl.BlockSpec((tm, w), lambda i: (i, 0))
    full = lambda a: pl.BlockSpec(a.shape, lambda i: (0,) * a.ndim)
    big = lambda w, dt: pltpu.HBM((s, w), dt)
    tile = pltpu.VMEM((tm, LW), f32)
    return pl.pallas_call(
        body, name="fwd_fused", grid=(nt,),
        in_specs=[pl.BlockSpec((tm, D), lambda i: (0, 0)), pl.BlockSpec((tm, D), lambda i: (jnp.minimum(i + 1, nt - 1), 0)),
                  full(ln_gain), full(wt), row(128), row(128), row(128), full(wo_shard), full(conv_w), full(conv_b),
                  full(wr), full(wi), full(br), full(bi), full(lam)],
        out_specs=[row(D), row(D), row(KVW), row(KVW), row(D), row(D), row(D), row(LW), row(LW), HBM],
        out_shape=[big(D, bf16), big(D, bf16), big(KVW, bf16), big(KVW, bf16), big(D, f32), big(D, f32), big(D, f32),
                   big(LW, f32), big(LW, f32), pltpu.HBM((2 * D, D), bf16)],
        scratch_shapes=[pltpu.VMEM((WO_ROWS, D), bf16), pltpu.VMEM((2, tm, D), bf16), pltpu.VMEM((8, LW), f32),
                        pltpu.VMEM((tm, LW), bf16), tile, tile, tile, tile, pltpu.VMEM((1, LW), f32)] + _comm_sems(1),
        compiler_params=_params(("arbitrary",), 56),
    )(*_in_hbm(x, x), ln_gain, *_in_hbm(wt), *tabs, wo_shard, conv_w, conv_b, wr, wi, br, bi, lam)


def _lru_bwd(u, hl, dhl, xl, conv_w, wr, wi, br, bi, lam, tm):
    s = u.shape[0]
    nt = s // tm

    def body(u_ref, h_ref, hp_ref, dh_ref, x_ref, xp_ref, cw_ref, wr_ref, wi_ref, br_ref, bi_ref, lam_ref,
             dxl_ref, dwr_ref, dwi_ref, dbr_ref, dbi_ref, dlam_ref, dcb_ref, dcw_ref,
             a_scr, l_scr, lcar, dunext):
        t0 = pl.program_id(0)
        tile = nt - 1 - t0

        @pl.when(t0 == 0)
        def _():
            lcar[...] = jnp.zeros_like(lcar)
            dunext[...] = jnp.zeros_like(dunext)
            for ref in (dwr_ref, dwi_ref, dbr_ref, dbi_ref, dlam_ref, dcb_ref, dcw_ref):
                ref[...] = jnp.zeros_like(ref)

        u = u_ref[...]
        lam = lam_ref[...]
        sp = _softplus(-lam)
        ub, r, i, a, mult, inv_mult = _gates(u, wr_ref, wi_ref, br_ref[...], bi_ref[...], sp)
        a_scr[...] = a

        def step(k, c):
            t = tm - 1 - k
            lt = dh_ref[pl.ds(t, 1), :] + c
            l_scr[pl.ds(t, 1), :] = lt
            return a_scr[pl.ds(t, 1), :] * lt

        lcar[...] = lax.fori_loop(0, tm, step, lcar[...], unroll=8)
        lt = l_scr[...]

        hprev = _later(h_ref[...], jnp.where(tile > 0, hp_ref[...], 0.0), 1)
        da = lt * hprev
        dmult = lt * (i * u)
        di = lt * mult * u
        du = lt * mult * i
        dla = da * a - dmult * (a * a) * inv_mult
        dr = dla * (-LRU_C * sp)
        dlam_ref[...] += jnp.sum(dla * (-LRU_C * r), axis=0, keepdims=True)
        dpr = dr * r * (1.0 - r)
        dpi = di * i * (1.0 - i)
        dbr_ref[...] += jnp.sum(dpr, axis=0, keepdims=True)
        dbi_ref[...] += jnp.sum(dpi, axis=0, keepdims=True)
        dprb, dpib = dpr.astype(bf16), dpi.astype(bf16)
        dug = []
        for g in range(NGRP):
            gs = slice(256 * g, 256 * g + 256)
            dwr_ref[g] += _dot_tn(ub[:, gs], dprb[:, gs])
            dwi_ref[g] += _dot_tn(ub[:, gs], dpib[:, gs])
            dug.append(_dot_nt(dprb[:, gs], wr_ref[g]) + _dot_nt(dpib[:, gs], wi_ref[g]))
        du = du + jnp.concatenate(dug, axis=1)

        dcb_ref[...] += jnp.sum(du, axis=0, keepdims=True)
        x = x_ref[...]
        before = jnp.where(tile > 0, xp_ref[...], 0.0)
        for k in range(CONVW):
            dcw_ref[k:k + 1, :] += jnp.sum(du * _later(x, before, CONVW - 1 - k), axis=0, keepdims=True)
        after = dunext[...]
        dxl = sum(cw_ref[k:k + 1, :] * _earlier(du, after, CONVW - 1 - k) for k in range(CONVW))
        dxl_ref[...] = dxl.astype(bf16)
        dunext[...] = du[0:8, :]

        @pl.when(t0 == nt - 1)
        def _():
            dlam_ref[...] = dlam_ref[...] * (-_sigmoid(-lam))

    rev = lambda i: (nt - 1 - i, 0)
    row = pl.BlockSpec((tm, LW), rev)
    prev8 = pl.BlockSpec((8, LW), lambda i: (jnp.maximum((nt - 1 - i) * (tm // 8) - 1, 0), 0))
    full = lambda a: pl.BlockSpec(a.shape, lambda i: (0,) * a.ndim)
    vec = pl.BlockSpec((1, LW), lambda i: (0, 0))
    bd = pl.BlockSpec((NGRP, 256, 256), lambda i: (0, 0, 0))
    return pl.pallas_call(
        body, name="lru_bwd", grid=(nt,),
        in_specs=[row, row, prev8, row, row, prev8, full(conv_w), full(wr), full(wi), full(br), full(bi), full(lam)],
        out_specs=[row, bd, bd, vec, vec, vec, vec, pl.BlockSpec((CONVW, LW), lambda i: (0, 0))],
        out_shape=[pltpu.HBM((s,LW), bf16),
                   jax.ShapeDtypeStruct((NGRP, 256, 256), f32), jax.ShapeDtypeStruct((NGRP, 256, 256), f32),
                   jax.ShapeDtypeStruct((1, LW), f32), jax.ShapeDtypeStruct((1, LW), f32),
                   jax.ShapeDtypeStruct((1, LW), f32), jax.ShapeDtypeStruct((1, LW), f32),
                   jax.ShapeDtypeStruct((CONVW, LW), f32)],
        scratch_shapes=[pltpu.VMEM((tm, LW), f32), pltpu.VMEM((tm, LW), f32),
                        pltpu.VMEM((1, LW), f32), pltpu.VMEM((8, LW), f32)],
        compiler_params=_params(("arbitrary",), 56),
    )(*_in_hbm(u, hl, hl, dhl, xl, xl), conv_w, wr, wi, br, bi, lam)


def _gated_norm(t, gate, gain):
    sg = _sigmoid(gate)
    silu = gate * sg
    p = t * silu
    rstd = lax.rsqrt(jnp.mean(p * p, axis=-1, keepdims=True) + EPS)
    ph = p * rstd
    return sg, silu, rstd, ph, ph * gain


def _gated_norm_bwd(dy, t, gate, gain, sg, silu, rstd, ph):
    w = dy * gain
    dp = rstd * (w - ph * jnp.mean(w * ph, axis=-1, keepdims=True))
    dgate = dp * t * (sg * (1.0 + gate * (1.0 - sg)))
    return jnp.sum(dy * ph, axis=0, keepdims=True), dp * silu, dgate


def _out_fwd_bwd(x, tgt, o, ga, hl, gl, again, lgain, fgain, wo, tm):
    s = x.shape[0]
    nt = s // tm

    def body(x_ref, t_ref, o_ref, ga_ref, hl_ref, gl_ref, ag_ref, lg_ref, fg_ref, wo_ref,
             dx2_ref, do_ref, dga_ref, dhl_ref, dgl_ref, dwo_ref, gfg_ref, gag_ref, glg_ref, loss_ref, acc):
        i = pl.program_id(0)

        @pl.when(i == 0)
        def _():
            acc[...] = jnp.zeros_like(acc)
            for ref in (gfg_ref, gag_ref, glg_ref, loss_ref):
                ref[...] = jnp.zeros_like(ref)

        oo, gga, hh, ggl = o_ref[...], ga_ref[...], hl_ref[...], gl_ref[...]
        ag, lg, fg = ag_ref[...], lg_ref[...], fg_ref[...]
        sga, silua, ra, pah, ya = _gated_norm(oo, gga, ag)
        sgl, silul, rl, plh, yl = _gated_norm(hh, ggl, lg)
        yab, ylb = ya.astype(bf16), yl.astype(bf16)
        y = _dot(yab, wo_ref[0:D, :]) + _dot(ylb, wo_ref[D:2 * D, :])
        x2 = x_ref[...] + y
        r2 = lax.rsqrt(jnp.mean(x2 * x2, axis=-1, keepdims=True) + EPS)
        x2h = x2 * r2
        err = x2h * fg - t_ref[...]
        loss_ref[...] += 0.5 * jnp.sum(jnp.sum(err * err, axis=-1, keepdims=True) * (1.0 / D))
        dout = err * (1.0 / D)
        gfg_ref[...] += jnp.sum(dout * x2h, axis=0, keepdims=True)
        w = dout * fg
        dx2 = r2 * (w - x2h * jnp.mean(w * x2h, axis=-1, keepdims=True))
        dx2_ref[...] = dx2
        dyb = dx2.astype(bf16)
        acc[0:D, :] += _dot_tn(yab, dyb)
        acc[D:2 * D, :] += _dot_tn(ylb, dyb)
        dya = _dot_nt(dyb, wo_ref[0:D, :])
        dyl = _dot_nt(dyb, wo_ref[D:2 * D, :])
        gag, do, dga = _gated_norm_bwd(dya, oo, gga, ag, sga, silua, ra, pah)
        glg, dhl, dgl = _gated_norm_bwd(dyl, hh, ggl, lg, sgl, silul, rl, plh)
        gag_ref[...] += gag
        glg_ref[...] += glg
        do_ref[...] = do.astype(bf16)
        dga_ref[...] = dga.astype(bf16)
        dhl_ref[...] = dhl
        dgl_ref[...] = dgl.astype(bf16)

        @pl.when(i == nt - 1)
        def _():
            dwo_ref[...] = acc[...].astype(bf16)

    row = pl.BlockSpec((tm, D), lambda i: (i, 0))
    vec = pl.BlockSpec((1, D), lambda i: (0, 0))
    mat = pl.BlockSpec((2 * D, D), lambda i: (0, 0))
    return pl.pallas_call(
        body, name="out_fwd_bwd", grid=(nt,),
        in_specs=[row] * 6 + [vec] * 3 + [mat],
        out_specs=[row] * 5 + [mat, vec, vec, vec, pl.BlockSpec((1, 128), lambda i: (0, 0))],
        out_shape=[pltpu.HBM((s,D), f32), pltpu.HBM((s,D), bf16),
                   pltpu.HBM((s,D), bf16), pltpu.HBM((s,D), f32),
                   pltpu.HBM((s,D), bf16), pltpu.HBM((2 * D, D), bf16),
                   jax.ShapeDtypeStruct((1, D), f32), jax.ShapeDtypeStruct((1, D), f32),
                   jax.ShapeDtypeStruct((1, D), f32), jax.ShapeDtypeStruct((1, 128), f32)],
        scratch_shapes=[pltpu.VMEM((2 * D, D), f32)],
        compiler_params=_params(("arbitrary",), 56),
    )(*_in_hbm(x, tgt, o, ga, hl, gl), again, lgain, fgain, *_in_hbm(wo))


def _bwd_in(x, dx2, dq, dk, dv, dga, dxl, dgl, ln_gain, wt, tabs, tm):
    s = x.shape[0]

    def body(x_ref, dx2_ref, dq_ref, dk_ref, dv_ref, dga_ref, dxl_ref, dgl_ref, g_ref, wt_ref,
             c_ref, sa_ref, sb_ref, gx_ref, gln_ref, dzt_ref, dz_scr):
        @pl.when(pl.program_id(0) == 0)
        def _():
            gln_ref[...] = jnp.zeros_like(gln_ref)

        c, sa, sb = c_ref[...], sa_ref[...], sb_ref[...]
        for j in range(D // 128):
            js = slice(128 * j, 128 * j + 128)
            dz_scr[:, js] = (_unrope(dq_ref[:, js], c, sa, sb) * (HD ** -0.5)).astype(bf16)
        for j in range(KVW // 128):
            js = slice(128 * j, 128 * j + 128)
            dz_scr[:, D + 128 * j:D + 128 * j + 128] = _unrope(dk_ref[:, js], c, sa, sb).astype(bf16)
        dz_scr[:, D + KVW:D + 2 * KVW] = dv_ref[...].astype(bf16)
        dz_scr[:, 1536:2560] = dga_ref[...]
        dz_scr[:, 2560:3584] = dxl_ref[...]
        dz_scr[:, 3584:4608] = dgl_ref[...]
        for j in range(NIN // 128):
            dzt_ref[128 * j:128 * j + 128, :] = dz_scr[:, 128 * j:128 * j + 128].T
        dh = _dot(dz_scr[:, 0:512], wt_ref[0:512, :])
        for ci in range(1, NIN // 512):
            dh = dh + _dot(dz_scr[:, 512 * ci:512 * ci + 512], wt_ref[512 * ci:512 * ci + 512, :])
        xx = x_ref[...]
        rstd = lax.rsqrt(jnp.mean(xx * xx, axis=-1, keepdims=True) + EPS)
        xh = xx * rstd
        gln_ref[...] += jnp.sum(dh * xh, axis=0, keepdims=True)
        w = dh * g_ref[...]
        gx_ref[...] = dx2_ref[...] + rstd * (w - xh * jnp.mean(w * xh, axis=-1, keepdims=True))

    row = lambda w: pl.BlockSpec((tm, w), lambda i: (i, 0))
    full = lambda a: pl.BlockSpec(a.shape, lambda i: (0, 0))
    return pl.pallas_call(
        body, name="bwd_in", grid=(s // tm,),
        in_specs=[row(D), row(D), row(D), row(KVW), row(KVW), row(D), row(D), row(D), full(ln_gain), full(wt),
                  row(128), row(128), row(128)],
        out_specs=[row(D), pl.BlockSpec((1, D), lambda i: (0, 0)), pl.BlockSpec((NIN, tm), lambda i: (0, i))],
        out_shape=[pltpu.HBM((s,D), f32), jax.ShapeDtypeStruct((1, D), f32),
                   pltpu.HBM((NIN, s), bf16)],
        scratch_shapes=[pltpu.VMEM((tm, NIN), bf16)],
        compiler_params=_params(("arbitrary",), 56),
    )(*_in_hbm(x, dx2, dq, dk, dv, dga, dxl, dgl), ln_gain, *_in_hbm(wt), *tabs)


WT_TERMS = 5


def _dwt_scatter(dzt, h, small, tm):
    s = h.shape[0]
    nk = s // tm
    srows = small.shape[0] // NDEV
    last = NDEV - 1

    def body(order_ref, dz_ref, h_ref, sm_ref, lwt_ref, lsm_ref, acc, stage, given, send_sems, recv_sems, local_sem,
             sm_send, sm_recv, sm_local):
        j, k = pl.program_id(0), pl.program_id(1)
        x, y, c = _place()
        sibling = (x, y, 1 - c)
        chips = [(1 - x, 1 - y), (1 - x, y), (x, 1 - y)]
        sm_start, sm_finish = _scatter_ops([sm_ref], [lsm_ref], sm_send, sm_recv, sm_local)

        def send(step):
            if step == last - 1:
                dst, to = lwt_ref.at[1], sibling
            elif step % 2 == 0:
                dst, to = given.at[step // 2], sibling
            else:
                dst, to = lwt_ref.at[2 + step // 2], (*chips[step // 2], c)
            return pltpu.make_async_remote_copy(
                src_ref=stage.at[step % 2], dst_ref=dst, send_sem=send_sems.at[step], recv_sem=recv_sems.at[step],
                device_id=to, device_id_type=MESH)

        def keep():
            return pltpu.make_async_copy(stage.at[last % 2], lwt_ref.at[0], local_sem)

        @pl.when((j == 0) & (k == 0))
        def _():
            sm_start()

        @pl.when(k == 0)
        def _():
            acc[...] = jnp.zeros_like(acc)

        acc[...] += _dot(dz_ref[...], h_ref[...])

        for step in range(NDEV):
            @pl.when((k == nk - 1) & (j == step))
            def _(step=step):
                if step >= 2:
                    send(step - 2).wait_send()
                if step % 2 == 1 and step < last:
                    send(step - 1).wait_recv()
                    stage[step % 2] = (acc[...] + given[step // 2].astype(f32)).astype(bf16)
                else:
                    stage[step % 2] = acc[...].astype(bf16)
                if step < last:
                    send(step).start()
                else:
                    keep().start()
                    send(last - 1).wait_send()
                    for peer_step in (1, 3, 5, last - 1):
                        send(peer_step).wait_recv()
                    keep().wait()
                    sm_finish()

    x, y, c = _place()
    dest = lambda cx, cy, cc: 4 * cx + 2 * cy + cc
    order = jnp.stack([dest(1 - x, 1 - y, 1 - c), dest(1 - x, 1 - y, c), dest(1 - x, y, 1 - c), dest(1 - x, y, c),
                       dest(x, 1 - y, 1 - c), dest(x, 1 - y, c), dest(x, y, 1 - c), dest(x, y, c)])
    return pl.pallas_call(
        body, name="dwt_scatter",
        grid_spec=pltpu.PrefetchScalarGridSpec(
            num_scalar_prefetch=1, grid=(NDEV, nk),
            in_specs=[pl.BlockSpec((WT_ROWS, tm), lambda j, k, order: (order[j], k)),
                      pl.BlockSpec((tm, D), lambda j, k, order: (k, 0)), HBM],
            out_specs=[HBM, HBM],
            scratch_shapes=[pltpu.VMEM((WT_ROWS, D), f32), pltpu.VMEM((2, WT_ROWS, D), bf16),
                            pltpu.VMEM((3, WT_ROWS, D), bf16),
                            pltpu.SemaphoreType.DMA((last,)), pltpu.SemaphoreType.DMA((last,)),
                            pltpu.SemaphoreType.DMA(())] + _comm_sems(1)),
        out_shape=[pltpu.HBM((WT_TERMS, WT_ROWS, D), bf16), pltpu.HBM((NDEV, srows, D), f32)],
        compiler_params=_params(("arbitrary", "arbitrary"), 32),
    )(order, *_in_hbm(dzt, h, small))


def _diag_blocks(bd):
    eye = jnp.eye(4, dtype=bd.dtype)
    return jnp.einsum('gjckd,jk->gjcd', bd.reshape(NGRP, 4, HD, 4, HD), eye).reshape(NQ, HD, HD)


def _sequence_step(x, tgt, wt, wo_shard, conv_w, p):
    s = x.shape[0]
    tm = min(256, s)
    tabs = _rope_tables(s)
    wr, wi = _block_diag(p["w_rgate"]), _block_diag(p["w_igate"])
    sinks = p["sinks"].reshape(NQ)
    h, q, k, v, ga, xl, gl, u, hl, wo = _fwd_fused(x, p["ln_gain"], wt, tabs, wo_shard, conv_w, p["conv_b"], wr, wi,
                                                   p["b_rgate"], p["b_igate"], p["lru_lambda"], tm)
    o = _attn_fwd(q, k, v, sinks)
    dx2, do, dga, dhl, dgl, dwo, g_fg, g_ag, g_lg, loss = _out_fwd_bwd(
        x, tgt, o, ga, hl, gl, p["attn_out_gain"], p["lru_out_gain"], p["final_gain"], wo, tm)
    dq, dk, dv, dsink, land_wo = _attn_bwd(q, k, v, do, sinks, dwo)
    dxl, dwr, dwi, dbr, dbi, dlam, dcb, dcw = _lru_bwd(
        u, hl, dhl, xl, conv_w, wr, wi, p["b_rgate"], p["b_igate"], p["lru_lambda"], tm)
    gx, g_ln, dzt = _bwd_in(x, dx2, dq, dk, dv, dga, dxl, dgl, p["ln_gain"], wt, tabs, tm)
    small = dict(ln_gain=g_ln, sinks=dsink.reshape(NQ, BLK).sum(axis=1)[None], conv_w=dcw, conv_b=dcb,
                 w_rgate=_diag_blocks(dwr), b_rgate=dbr, w_igate=_diag_blocks(dwi), b_igate=dbi, lru_lambda=dlam,
                 attn_out_gain=g_ag, lru_out_gain=g_lg, final_gain=g_fg)
    land_wt, land_sm = _dwt_scatter(dzt, h, _pack_small(small, loss), min(512, s))
    return gx, land_wt, land_wo, land_sm


def _all_gather(srcs, out_dtypes, name):
    n = len(srcs)
    cast = [a.dtype != dt for a, dt in zip(srcs, out_dtypes)]

    def body(*refs):
        src_refs, out_refs = refs[:n], refs[n:2 * n]
        stage_refs = list(refs[2 * n:2 * n + sum(cast)])
        mine_refs = []
        for a in range(n):
            if cast[a]:
                st = stage_refs.pop(0)
                st[...] = src_refs[a][...].astype(out_dtypes[a])
                mine_refs.append(st)
            else:
                mine_refs.append(src_refs[a])
        start, finish = _gather_ops(mine_refs, out_refs, *refs[-3:])
        start()
        finish()

    vmem = pl.BlockSpec(memory_space=pltpu.VMEM)
    return pl.pallas_call(
        body, name=name,
        in_specs=[vmem] * n, out_specs=[HBM] * n,
        out_shape=[pltpu.HBM((NDEV * a.shape[0], a.shape[1]), dt) for a, dt in zip(srcs, out_dtypes)],
        scratch_shapes=[pltpu.VMEM(a.shape, dt) for a, dt, cst in zip(srcs, out_dtypes, cast) if cst] + _comm_sems(n),
        compiler_params=pltpu.CompilerParams(vmem_limit_bytes=32 * MIB),
    )(*srcs)


def _sum_slots(land, tr, name):
    terms, rows, cols = land.shape

    def body(l_ref, o_ref):
        acc = l_ref[0].astype(f32)
        for d in range(1, terms):
            acc = acc + l_ref[d].astype(f32)
        o_ref[...] = acc

    return pl.pallas_call(
        body, name=name, grid=(rows // tr,),
        in_specs=[pl.BlockSpec((terms, tr, cols), lambda i: (0, i, 0))],
        out_specs=pl.BlockSpec((tr, cols), lambda i: (i, 0)),
        out_shape=jax.ShapeDtypeStruct((rows, cols), f32),
        compiler_params=_params(("arbitrary",), 32),
    )(*_in_hbm(land))


def _adam_math(w, g, m, v):
    m2 = ADAM_B1 * m + (1.0 - ADAM_B1) * g
    v2 = ADAM_B2 * v + (1.0 - ADAM_B2) * (g * g)
    m_hat = m2 / (1.0 - ADAM_B1 ** ADAM_STEP)
    v_hat = v2 / (1.0 - ADAM_B2 ** ADAM_STEP)
    delta = -ADAM_LR * (m_hat / (jnp.sqrt(v_hat) + ADAM_EPS) + ADAM_WD * w)
    return delta, m2, v2


def _adamw(w, g, m, v, tr, name):
    rows, cols = w.shape

    def body(w_ref, g_ref, m_ref, v_ref, d_ref, m2_ref, v2_ref):
        d_ref[...], m2_ref[...], v2_ref[...] = _adam_math(w_ref[...], g_ref[...], m_ref[...], v_ref[...])

    blk = pl.BlockSpec((tr, cols), lambda i: (i, 0))
    return pl.pallas_call(
        body, name=name, grid=(rows // tr,),
        in_specs=[blk] * 4, out_specs=[blk] * 3,
        out_shape=[jax.ShapeDtypeStruct((rows, cols), f32)] * 3,
        compiler_params=_params(("arbitrary",), 32),
    )(*_in_hbm(w, g, m, v))


VEC_NAMES = ("ln_gain", "conv_b", "b_rgate", "b_igate", "lru_lambda", "attn_out_gain", "lru_out_gain", "final_gain")
ROW_RGATE, ROW_IGATE, ROW_VEC, ROW_SINKS = 0, 64, 128, 136
LOSS_LANE = NQ


def _adamw_small(g_rep, g_conv, w, m, v):
    names = list(VEC_NAMES) + ["sinks", "conv_w", "w_rgate", "w_igate"]
    ins = [g_rep, g_conv] + [d[k] for k in names for d in (w, m, v)]

    def body(*refs):
        g_ref, gc_ref = refs[0], refs[1]
        in_refs = refs[2:2 + 3 * len(names)]
        out_refs = refs[2 + 3 * len(names):]

        def update(j, g, at=None):
            w_ref, m_ref, v_ref = in_refs[3 * j:3 * j + 3]
            outs = out_refs[4 * j:4 * j + 4]
            pick = (lambda r: r[...]) if at is None else (lambda r: r[at])
            res = (g,) + _adam_math(pick(w_ref), g, pick(m_ref), pick(v_ref))
            for o_ref, val in zip(outs, res):
                if at is None:
                    o_ref[...] = val
                else:
                    o_ref[at] = val

        for j in range(len(VEC_NAMES)):
            update(j, g_ref[ROW_VEC + j:ROW_VEC + j + 1, :])
        update(len(VEC_NAMES), g_ref[ROW_SINKS:ROW_SINKS + 1, 0:NQ])
        update(len(VEC_NAMES) + 1, gc_ref[...], at=0)
        for gi, row0 in ((len(VEC_NAMES) + 2, ROW_RGATE), (len(VEC_NAMES) + 3, ROW_IGATE)):
            for nb in range(NQ):
                update(gi, g_ref[row0:row0 + HD, HD * nb:HD * nb + HD], at=(0, nb))

    vmem = pl.BlockSpec(memory_space=pltpu.VMEM)
    out_shape = [jax.ShapeDtypeStruct(w[k].shape, f32) for k in names for _ in range(4)]
    outs = pl.pallas_call(
        body, name="adamw_small",
        in_specs=[vmem] * len(ins), out_specs=[vmem] * len(out_shape), out_shape=out_shape,
        compiler_params=pltpu.CompilerParams(vmem_limit_bytes=32 * MIB),
    )(*ins)
    return {k: tuple(outs[4 * j:4 * j + 4]) for j, k in enumerate(names)}


def _pack_small(small, loss):
    gate = lambda g: g.transpose(1, 0, 2).reshape(HD, NQ * HD)
    row_s = jnp.concatenate([small["sinks"], loss[:, LOSS_LANE:128], jnp.zeros((1, D - 128), f32)], axis=1)
    rep = jnp.concatenate([gate(small["w_rgate"]), gate(small["w_igate"])] + [small[k] for k in VEC_NAMES]
                          + [row_s, jnp.zeros((SMALL_ROWS - ROW_SINKS - 1, D), f32)], axis=0)
    conv = small["conv_w"].reshape(CONVW, NDEV, 128).transpose(1, 0, 2)
    conv = jnp.pad(conv, ((0, 0), (0, 8 - CONVW), (0, D - 128)))
    return jnp.concatenate([rep.reshape(NDEV, SMALL_PER, D), conv], axis=1).reshape(NDEV * (SMALL_PER + 8), D)


def kernel(x, ln_gain, w_in, sinks, conv_w, conv_b, w_rgate, b_rgate, w_igate, b_igate, lru_lambda, attn_out_gain, lru_out_gain, w_out, final_gain, loss_target, m_ln_gain, m_w_in, m_sinks, m_conv_w, m_conv_b, m_w_rgate, m_b_rgate, m_w_igate, m_b_igate, m_lru_lambda, m_attn_out_gain, m_lru_out_gain, m_w_out, m_final_gain, v_ln_gain, v_w_in, v_sinks, v_conv_w, v_conv_b, v_w_rgate, v_b_rgate, v_w_igate, v_b_igate, v_lru_lambda, v_attn_out_gain, v_lru_out_gain, v_w_out, v_final_gain):
    w = dict(ln_gain=ln_gain, sinks=sinks, conv_w=conv_w, conv_b=conv_b, w_rgate=w_rgate, b_rgate=b_rgate,
             w_igate=w_igate, b_igate=b_igate, lru_lambda=lru_lambda, attn_out_gain=attn_out_gain,
             lru_out_gain=lru_out_gain, final_gain=final_gain.reshape(1, D))
    m = dict(ln_gain=m_ln_gain, sinks=m_sinks, conv_w=m_conv_w, conv_b=m_conv_b, w_rgate=m_w_rgate,
             b_rgate=m_b_rgate, w_igate=m_w_igate, b_igate=m_b_igate, lru_lambda=m_lru_lambda,
             attn_out_gain=m_attn_out_gain, lru_out_gain=m_lru_out_gain, final_gain=m_final_gain.reshape(1, D))
    v = dict(ln_gain=v_ln_gain, sinks=v_sinks, conv_w=v_conv_w, conv_b=v_conv_b, w_rgate=v_w_rgate,
             b_rgate=v_b_rgate, w_igate=v_w_igate, b_igate=v_b_igate, lru_lambda=v_lru_lambda,
             attn_out_gain=v_attn_out_gain, lru_out_gain=v_lru_out_gain, final_gain=v_final_gain.reshape(1, D))

    conv_blk = jnp.pad(conv_w[0], ((0, 8 - CONVW), (0, 0)))
    wt, cw_all = _all_gather([w_in[0].T, conv_blk], [bf16, f32], "gather_weights")
    conv_full = cw_all.reshape(NDEV, 8, 128)[:, 0:CONVW].transpose(1, 0, 2).reshape(CONVW, LW)

    p = {k: (w[k][0] if k in ("w_rgate", "w_igate") else w[k]) for k in w if k != "conv_w"}
    gx, land_wt, land_wo, land_sm = _sequence_step(x[0], loss_target[0], wt, w_out[0], conv_full, p)

    g_wt = _sum_slots(land_wt, 192, "sum_wt")
    g_wo = _sum_slots(land_wo, 256, "sum_wo")
    g_sm = _sum_slots(land_sm, SMALL_PER + 8, "sum_small")
    (g_rep,) = _all_gather([g_sm[0:SMALL_PER]], [f32], "gather_small")
    g_conv = g_sm[SMALL_PER:SMALL_PER + CONVW, 0:128]

    d_win, m_win, v_win = _adamw(w_in[0].T, g_wt, m_w_in[0].T, v_w_in[0].T, 192, "adamw_w_in")
    g_win, d_win, m_win, v_win = (t.T for t in (g_wt, d_win, m_win, v_win))
    d_wo, m_wo, v_wo = _adamw(w_out[0], g_wo, m_w_out[0], v_w_out[0], 256, "adamw_w_out")
    res = _adamw_small(g_rep, g_conv, w, m, v)
    res["w_in"] = tuple(t[None] for t in (g_win, d_win, m_win, v_win))
    res["w_out"] = tuple(t[None] for t in (g_wo, d_wo, m_wo, v_wo))
    res["final_gain"] = tuple(t.reshape(D) for t in res["final_gain"])

    order = ("ln_gain", "w_in", "sinks", "conv_w", "conv_b", "w_rgate", "b_rgate", "w_igate", "b_igate",
             "lru_lambda", "attn_out_gain", "lru_out_gain", "w_out", "final_gain")
    total_loss = g_rep[ROW_SINKS, LOSS_LANE]
    return (total_loss, gx[None]) + tuple(res[k][i] for i in range(4) for k in order)
```

```python
import jax
import jax.numpy as jnp
from jax import lax
from jax.experimental import pallas as pl
from jax.experimental.pallas import tpu as pltpu

f32 = jnp.float32
bf16 = jnp.bfloat16

D = 1024
HD = 64
NQ = 16
NKV = 4
GROUP = NQ // NKV
KVW = NKV * HD
BLK = 128
ROT = 16
THETA = 500000.0
NEG = -1e30
LW = 1024
NGRP = 4
CONVW = 4
LRU_C = 8.0
NIN = 4608
EPS = 1e-6
NDEV = 8
WT_ROWS = NIN // NDEV
WO_ROWS = 2 * D // NDEV
SMALL_ROWS = 192
SMALL_PER = SMALL_ROWS // NDEV

ADAM_LR = 0.001
ADAM_B1 = 0.9
ADAM_B2 = 0.999
ADAM_EPS = 1e-08
ADAM_WD = 0.01
ADAM_STEP = 10

NT = (((1,), (1,)), ((), ()))
TN = (((0,), (0,)), ((), ()))
MESH = pl.DeviceIdType.MESH
MIB = 1024 * 1024


def _dot(a, b):
    return jnp.dot(a, b, preferred_element_type=f32)


def _dot_nt(a, b):
    return lax.dot_general(a, b, NT, preferred_element_type=f32)


def _dot_tn(a, b):
    return lax.dot_general(a, b, TN, preferred_element_type=f32)


def _params(sem, vmem_mib):
    return pltpu.CompilerParams(dimension_semantics=sem, vmem_limit_bytes=vmem_mib * MIB)


def _sigmoid(x):
    return 0.5 * jnp.tanh(0.5 * x) + 0.5


def _softplus(x):
    return jnp.maximum(x, 0.0) + jnp.log(1.0 + jnp.exp(-jnp.abs(x)))


def _rope_tables(s):
    pos = jnp.arange(s, dtype=f32)
    inv_freq = THETA ** (-jnp.arange(0, ROT, 2, dtype=f32) / ROT)
    ang = pos[:, None] * inv_freq[None, :]
    cs = jnp.concatenate([jnp.cos(ang) - 1.0, jnp.sin(ang)], axis=1)
    d = jnp.arange(128) % HD
    j = jnp.arange(ROT)[:, None]
    pick_c = ((d < ROT) & (j == d % (ROT // 2))).astype(f32)
    pick_sa = ((d >= ROT // 2) & (d < ROT) & (j == d)).astype(f32)
    pick_sb = -((d < ROT // 2) & (j == d + ROT // 2)).astype(f32)
    spread = lambda pick: jnp.dot(cs, pick, precision=lax.Precision.HIGHEST)
    spread_t = lambda pick: jnp.dot(pick.T, cs.T, precision=lax.Precision.HIGHEST)
    tabs = 1.0 + spread(pick_c), spread(pick_sa), spread(pick_sb)
    tabs_t = 1.0 + spread_t(pick_c), spread_t(pick_sa), spread_t(pick_sb)
    return tabs, tabs_t


def _rope(t, c, sa, sb):
    return t * c + pltpu.roll(t, 8, 1) * sa + pltpu.roll(t, 120, 1) * sb


def _unrope_t(dr, c, sa, sb):
    return dr * c + pltpu.roll(dr * sa, 120, 0) + pltpu.roll(dr * sb, 8, 0)


def _place():
    return lax.axis_index("x"), lax.axis_index("y"), lax.axis_index("c")


def _gather_ops(mine_refs, out_refs, send_sems, recv_sems, local_sems):
    n = len(mine_refs)
    x, y, c = _place()
    me, sibling = (x, y, c), (x, y, 1 - c)
    chips = [(1 - x, y), (x, 1 - y), (1 - x, 1 - y)]

    def rows(a, dev):
        m = mine_refs[a].shape[0]
        return out_refs[a].at[pl.ds((4 * dev[0] + 2 * dev[1] + dev[2]) * m, m), :]

    def copy(a, k, block, to, own=False):
        return pltpu.make_async_remote_copy(
            src_ref=mine_refs[a] if own else rows(a, block), dst_ref=rows(a, block),
            send_sem=send_sems.at[a, k], recv_sem=recv_sems.at[a, k], device_id=to, device_id_type=MESH)

    def local(a):
        return pltpu.make_async_copy(mine_refs[a], rows(a, me), local_sems.at[a])

    def first(a):
        return [copy(a, 0, me, sibling, own=True)] + [copy(a, 1 + j, me, (*chip, c), own=True)
                                                      for j, chip in enumerate(chips)]

    def start():
        for a in range(n):
            local(a).start()
            for cp in first(a):
                cp.start()

    def finish():
        for j, chip in enumerate(chips):
            for a in range(n):
                copy(a, 1 + j, (*chip, c), me).wait_recv()
                copy(a, 4 + j, (*chip, c), sibling).start()
        for a in range(n):
            copy(a, 0, sibling, me).wait_recv()
            for j, chip in enumerate(chips):
                copy(a, 4 + j, (*chip, 1 - c), me).wait_recv()
        for a in range(n):
            for cp in first(a) + [copy(a, 4 + j, (*chip, c), sibling) for j, chip in enumerate(chips)]:
                cp.wait_send()
            local(a).wait()

    return start, finish


def _scatter_ops(src_refs, land_refs, send_sems, recv_sems, local_sems):
    n = len(src_refs)
    x, y, c = _place()
    my = 4 * x + 2 * y + c

    def peer(k):
        return x ^ (k >> 2), y ^ ((k >> 1) & 1), c ^ (k & 1)

    def piece(a, dev):
        m = src_refs[a].shape[0] // NDEV
        return src_refs[a].at[pl.ds(dev * m, m), :]

    def local(a):
        return pltpu.make_async_copy(piece(a, my), land_refs[a].at[my], local_sems.at[a])

    def send(a, k):
        px, py, pc = peer(k)
        return pltpu.make_async_remote_copy(
            src_ref=piece(a, 4 * px + 2 * py + pc), dst_ref=land_refs[a].at[my],
            send_sem=send_sems.at[a, k - 1], recv_sem=recv_sems.at[a, k - 1],
            device_id=(px, py, pc), device_id_type=MESH)

    def arrival(a, k):
        px, py, pc = peer(k)
        return pltpu.make_async_remote_copy(
            src_ref=piece(a, my), dst_ref=land_refs[a].at[4 * px + 2 * py + pc],
            send_sem=send_sems.at[a, k - 1], recv_sem=recv_sems.at[a, k - 1],
            device_id=(px, py, pc), device_id_type=MESH)

    def start():
        for a in range(n):
            local(a).start()
        for k in range(1, NDEV):
            for a in range(n):
                send(a, k).start()

    def finish():
        for k in range(1, NDEV):
            for a in range(n):
                send(a, k).wait_send()
        for k in range(1, NDEV):
            for a in range(n):
                arrival(a, k).wait_recv()
        for a in range(n):
            local(a).wait()

    return start, finish


def _in_hbm(*arrays):
    return tuple(pltpu.with_memory_space_constraint(a, pltpu.HBM) for a in arrays)


def _comm_sems(n):
    return [pltpu.SemaphoreType.DMA((n, 7)), pltpu.SemaphoreType.DMA((n, 7)), pltpu.SemaphoreType.DMA((n,))]


HBM = pl.BlockSpec(memory_space=pltpu.HBM)


def _fwd_in(x, ln_gain, wt, tabs, wo_shard, tm):
    s = x.shape[0]
    nt = s // tm
    nc = 512

    def body(x_ref, g_ref, wt_ref, c_ref, sa_ref, sb_ref, wo_ref, h_ref, q_ref, k_ref, v_ref, ga_ref, xl_ref, gl_ref,
             wo_all, wo_stage, send_sems, recv_sems, local_sems):
        i = pl.program_id(0)
        start, finish = _gather_ops([wo_stage], [wo_all], send_sems, recv_sems, local_sems)

        @pl.when(i == 0)
        def _():
            wo_stage[...] = wo_ref[...].astype(bf16)
            start()

        xx = x_ref[...]
        rstd = lax.rsqrt(jnp.mean(xx * xx, axis=-1, keepdims=True) + EPS)
        h = (xx * rstd * g_ref[...]).astype(bf16)
        h_ref[...] = h
        c, sa, sb = c_ref[...], sa_ref[...], sb_ref[...]

        def z_chunk(ci):
            return _dot_nt(h, wt_ref[ci * nc:(ci + 1) * nc, :])

        for ci in range(2):
            z = z_chunk(ci)
            for j in range(nc // 128):
                r = _rope(z[:, 128 * j:128 * j + 128], c, sa, sb) * (HD ** -0.5)
                q_ref[:, ci * nc + 128 * j:ci * nc + 128 * j + 128] = r.astype(bf16)
        z = z_chunk(2)
        for j in range(2):
            k_ref[:, 128 * j:128 * j + 128] = _rope(z[:, 128 * j:128 * j + 128], c, sa, sb).astype(bf16)
        v_ref[...] = z[:, 256:512].astype(bf16)
        for sec, ref in enumerate((ga_ref, xl_ref, gl_ref)):
            for j in range(2):
                ref[:, j * nc:(j + 1) * nc] = z_chunk(3 + 2 * sec + j)

        @pl.when(i == nt - 1)
        def _():
            finish()

    row = lambda w: pl.BlockSpec((tm, w), lambda i: (i, 0))
    full = lambda a: pl.BlockSpec(a.shape, lambda i: (0, 0))
    return pl.pallas_call(
        body, name="fwd_in", grid=(nt,),
        in_specs=[row(D), full(ln_gain), full(wt), row(128), row(128), row(128), full(wo_shard)],
        out_specs=[row(D), row(D), row(KVW), row(KVW), row(D), row(D), row(D), HBM],
        out_shape=[pltpu.HBM((s,D), bf16), pltpu.HBM((s,D), bf16),
                   pltpu.HBM((s,KVW), bf16), pltpu.HBM((s,KVW), bf16),
                   pltpu.HBM((s,D), f32), pltpu.HBM((s,D), f32),
                   pltpu.HBM((s,D), f32), pltpu.HBM((2 * D, D), bf16)],
        scratch_shapes=[pltpu.VMEM((WO_ROWS, D), bf16)] + _comm_sems(1),
        compiler_params=_params(("arbitrary",), 48),
    )(*_in_hbm(x), ln_gain, *_in_hbm(wt), *tabs, wo_shard)


HSUB = 4
SUBW = HSUB * BLK


def _sub_probs(kh, qg, n, sink_row):
    jj = lax.broadcasted_iota(jnp.int32, (BLK, SUBW), 0)
    ii = lax.broadcasted_iota(jnp.int32, (BLK, SUBW), 1) % BLK
    from_prev = jj > ii
    s2 = _dot_nt(kh, qg)
    sc = jnp.where(from_prev, s2[0:BLK] + jnp.where(n > 0, 0.0, NEG), s2[BLK:2 * BLK])
    m = jnp.maximum(jnp.max(sc, axis=0, keepdims=True), sink_row)
    p = jnp.exp(sc - m)
    es = jnp.exp(sink_row - m)
    inv = 1.0 / (jnp.sum(p, axis=0, keepdims=True) + es)
    return from_prev, p * inv, es * inv


def _split(t, from_prev):
    t = t.astype(bf16)
    zero = jnp.zeros_like(t)
    return jnp.concatenate([jnp.where(from_prev, t, zero), jnp.where(from_prev, zero, t)], axis=0)


def _stack_heads(ref, first):
    return jnp.concatenate([ref[:, HD * (first + g):HD * (first + g) + HD] for g in range(HSUB)], axis=0)


def _sink_rows(sinks):
    return jnp.repeat(sinks.reshape(NKV, GROUP), BLK, axis=1)


def _kv_specs():
    prev = pl.BlockSpec((BLK, KVW), lambda n: (jnp.maximum(n - 1, 0), 0))
    cur = pl.BlockSpec((BLK, KVW), lambda n: (n, 0))
    return [prev, cur, prev, cur]


def _attn_fwd(q, k, v, sinks):
    s = q.shape[0]

    def body(sink_ref, q_ref, kp_ref, kc_ref, vp_ref, vc_ref, o_ref):
        n = pl.program_id(0)
        for h in range(NKV):
            hs = slice(HD * h, HD * h + HD)
            kh = jnp.concatenate([kp_ref[:, hs], kc_ref[:, hs]], axis=0)
            vh = jnp.concatenate([vp_ref[:, hs], vc_ref[:, hs]], axis=0)
            for t in range(GROUP // HSUB):
                first = GROUP * h + HSUB * t
                from_prev, pn, _ = _sub_probs(kh, _stack_heads(q_ref, first), n,
                                              sink_ref[h:h + 1, SUBW * t:SUBW * t + SUBW])
                og = _dot_tn(_split(pn, from_prev), vh)
                for g in range(HSUB):
                    o_ref[:, HD * (first + g):HD * (first + g) + HD] = og[BLK * g:BLK * g + BLK]

    return pl.pallas_call(
        body, name="attn_fwd", grid=(s // BLK,),
        in_specs=[pl.BlockSpec((NKV, GROUP * BLK), lambda n: (0, 0)), pl.BlockSpec((BLK, D), lambda n: (n, 0))]
        + _kv_specs(),
        out_specs=pl.BlockSpec((BLK, D), lambda n: (n, 0)),
        out_shape=pltpu.HBM((s,D), f32),
        compiler_params=_params(("arbitrary",), 32),
    )(_sink_rows(sinks), *_in_hbm(q, k, k, v, v))


def _attn_bwd(q, k, v, do, sinks, dwo):
    s = q.shape[0]
    nb = s // BLK

    def body(sink_ref, q_ref, do_ref, kp_ref, kc_ref, vp_ref, vc_ref, dwo_ref, dq_ref, dk_ref, dv_ref, ds_ref,
             land_ref, send_sems, recv_sems, local_sems):
        n = pl.program_id(0)
        start, finish = _scatter_ops([dwo_ref], [land_ref], send_sems, recv_sems, local_sems)

        @pl.when(n == 0)
        def _():
            start()
            dk_ref[...] = jnp.zeros_like(dk_ref)
            dv_ref[...] = jnp.zeros_like(dv_ref)
            ds_ref[...] = jnp.zeros_like(ds_ref)

        prev_rows = pl.ds(pl.multiple_of(jnp.maximum(n - 1, 0) * BLK, BLK), BLK)
        cur_rows = pl.ds(pl.multiple_of(n * BLK, BLK), BLK)
        for h in range(NKV):
            hs = slice(HD * h, HD * h + HD)
            kh = jnp.concatenate([kp_ref[:, hs], kc_ref[:, hs]], axis=0)
            vh = jnp.concatenate([vp_ref[:, hs], vc_ref[:, hs]], axis=0)
            dkh = jnp.zeros((2 * BLK, HD), f32)
            dvh = jnp.zeros((2 * BLK, HD), f32)
            for t in range(GROUP // HSUB):
                first = GROUP * h + HSUB * t
                lanes = slice(SUBW * t, SUBW * t + SUBW)
                qg, dog = _stack_heads(q_ref, first), _stack_heads(do_ref, first)
                from_prev, pn, ps = _sub_probs(kh, qg, n, sink_ref[h:h + 1, lanes])
                dp2 = _dot_nt(vh, dog)
                dp = jnp.where(from_prev, dp2[0:BLK], dp2[BLK:2 * BLK])
                dsum = jnp.sum(pn * dp, axis=0, keepdims=True)
                ds_ref[h:h + 1, lanes] += -ps * dsum
                ds2 = _split(pn * (dp - dsum), from_prev)
                dqg = _dot_tn(ds2, kh)
                for g in range(HSUB):
                    dq_ref[:, HD * (first + g):HD * (first + g) + HD] = dqg[BLK * g:BLK * g + BLK]
                dkh = dkh + _dot(ds2, qg)
                dvh = dvh + _dot(_split(pn, from_prev), dog)
            dk_ref[prev_rows, hs] += dkh[0:BLK]
            dk_ref[cur_rows, hs] += dkh[BLK:2 * BLK]
            dv_ref[prev_rows, hs] += dvh[0:BLK]
            dv_ref[cur_rows, hs] += dvh[BLK:2 * BLK]

        @pl.when(n == nb - 1)
        def _():
            finish()

    blk = pl.BlockSpec((BLK, D), lambda n: (n, 0))
    whole = lambda r, w: pl.BlockSpec((r, w), lambda n: (0, 0))
    return pl.pallas_call(
        body, name="attn_bwd", grid=(nb,),
        in_specs=[whole(NKV, GROUP * BLK), blk, blk] + _kv_specs() + [HBM],
        out_specs=[blk, whole(s, KVW), whole(s, KVW), whole(NKV, GROUP * BLK), HBM],
        out_shape=[pltpu.HBM((s,D), f32), pltpu.HBM((s,KVW), f32),
                   pltpu.HBM((s,KVW), f32), jax.ShapeDtypeStruct((NKV, GROUP * BLK), f32),
                   pltpu.HBM((NDEV, WO_ROWS, D), bf16)],
        scratch_shapes=_comm_sems(1),
        compiler_params=_params(("arbitrary",), 48),
    )(_sink_rows(sinks), *_in_hbm(q, do, k, k, v, v, dwo))


def _band_softmax(s2_ref, ls, prev_offset, sink_row):
    jj = lax.broadcasted_iota(jnp.int32, (BLK, BLK), 0)
    ii = lax.broadcasted_iota(jnp.int32, (BLK, BLK), 1)
    from_prev = jj > ii
    sc = jnp.where(from_prev, s2_ref[0:BLK, ls] + prev_offset, s2_ref[BLK:2 * BLK, ls])
    m = jnp.maximum(jnp.max(sc, axis=0, keepdims=True), sink_row)
    p = jnp.exp(sc - m)
    es = jnp.exp(sink_row - m)
    inv = 1.0 / (jnp.sum(p, axis=0, keepdims=True) + es)
    return from_prev, p * inv, es * inv


def _put_split(dst_ref, ls, t, from_prev):
    t = t.astype(bf16)
    zero = jnp.zeros_like(t)
    dst_ref[0:BLK, ls] = jnp.where(from_prev, t, zero)
    dst_ref[BLK:2 * BLK, ls] = jnp.where(from_prev, zero, t)


def _heads_side_by_side(ref, h):
    return jnp.concatenate([ref[HD * (GROUP * h + g):HD * (GROUP * h + g) + HD, :] for g in range(GROUP)], axis=1)


def _kv_specs_t():
    prev = pl.BlockSpec((KVW, BLK), lambda n: (0, jnp.maximum(n - 1, 0)))
    cur = pl.BlockSpec((KVW, BLK), lambda n: (0, n))
    return [prev, cur, prev, cur]


def _attn_fwd_t(qt, kt, vt, sinks):
    s = qt.shape[1]

    def body(sink_ref, q_ref, kp_ref, kc_ref, vp_ref, vc_ref, o_ref, s2_scr, pn2_scr):
        n = pl.program_id(0)
        off = jnp.where(n > 0, 0.0, NEG)

        def scores(h):
            hs = slice(HD * h, HD * h + HD)
            kh = jnp.concatenate([kp_ref[hs, :], kc_ref[hs, :]], axis=1)
            s2_scr[h % 2] = _dot_tn(kh, _heads_side_by_side(q_ref, h))

        def probs(h):
            for g in range(GROUP):
                ls = slice(BLK * g, BLK * g + BLK)
                from_prev, pn, _ = _band_softmax(s2_scr.at[h % 2], ls, off, sink_ref[h:h + 1, ls])
                _put_split(pn2_scr.at[h % 2], ls, pn, from_prev)

        def outputs(h):
            hs = slice(HD * h, HD * h + HD)
            vh = jnp.concatenate([vp_ref[hs, :], vc_ref[hs, :]], axis=1)
            og = _dot(vh, pn2_scr[h % 2])
            for g in range(GROUP):
                a = GROUP * h + g
                o_ref[HD * a:HD * a + HD, :] = og[:, BLK * g:BLK * g + BLK]

        scores(0)
        for h in range(NKV):
            if h + 1 < NKV:
                scores(h + 1)
            probs(h)
            outputs(h)

    return pl.pallas_call(
        body, name="attn_fwd", grid=(s // BLK,),
        in_specs=[pl.BlockSpec((NKV, GROUP * BLK), lambda n: (0, 0)), pl.BlockSpec((D, BLK), lambda n: (0, n))]
        + _kv_specs_t(),
        out_specs=pl.BlockSpec((D, BLK), lambda n: (0, n)),
        out_shape=pltpu.HBM((D, s), f32),
        scratch_shapes=[pltpu.VMEM((2, 2 * BLK, GROUP * BLK), f32), pltpu.VMEM((2, 2 * BLK, GROUP * BLK), bf16)],
        compiler_params=_params(("arbitrary",), 32),
    )(_sink_rows(sinks), *_in_hbm(qt, kt, kt, vt, vt))


def _attn_bwd_t(qt, kt, vt, dot, sinks, dwo):
    s = qt.shape[1]
    nb = s // BLK

    def body(sink_ref, q_ref, do_ref, kp_ref, kc_ref, vp_ref, vc_ref, dwo_ref, dq_ref, dk_ref, dv_ref, ds_ref,
             land_ref, dk_hold, dv_hold, s2_scr, dp2_scr, pn2_scr, ds2_scr, send_sems, recv_sems, local_sems):
        n = pl.program_id(0)
        start, finish = _scatter_ops([dwo_ref], [land_ref], send_sems, recv_sems, local_sems)

        @pl.when(n == 0)
        def _():
            start()
            dk_hold[...] = jnp.zeros_like(dk_hold)
            dv_hold[...] = jnp.zeros_like(dv_hold)
            ds_ref[...] = jnp.zeros_like(ds_ref)

        @pl.when(n < nb)
        def _():
            off = jnp.where(n > 0, 0.0, NEG)

            def scores(h):
                hs = slice(HD * h, HD * h + HD)
                kh = jnp.concatenate([kp_ref[hs, :], kc_ref[hs, :]], axis=1)
                vh = jnp.concatenate([vp_ref[hs, :], vc_ref[hs, :]], axis=1)
                s2_scr[h % 2] = _dot_tn(kh, _heads_side_by_side(q_ref, h))
                dp2_scr[h % 2] = _dot_tn(vh, _heads_side_by_side(do_ref, h))

            def softmax_bwd(h):
                for g in range(GROUP):
                    ls = slice(BLK * g, BLK * g + BLK)
                    from_prev, pn, ps = _band_softmax(s2_scr.at[h % 2], ls, off, sink_ref[h:h + 1, ls])
                    dp = jnp.where(from_prev, dp2_scr[h % 2, 0:BLK, ls], dp2_scr[h % 2, BLK:2 * BLK, ls])
                    dsum = jnp.sum(pn * dp, axis=0, keepdims=True)
                    ds_ref[h:h + 1, ls] += -ps * dsum
                    _put_split(pn2_scr.at[h % 2], ls, pn, from_prev)
                    _put_split(ds2_scr.at[h % 2], ls, pn * (dp - dsum), from_prev)

            def grads(h):
                hs = slice(HD * h, HD * h + HD)
                kh = jnp.concatenate([kp_ref[hs, :], kc_ref[hs, :]], axis=1)
                dqg = _dot(kh, ds2_scr[h % 2])
                for g in range(GROUP):
                    a = GROUP * h + g
                    dq_ref[HD * a:HD * a + HD, :] = dqg[:, BLK * g:BLK * g + BLK]
                dkh = _dot_nt(_heads_side_by_side(q_ref, h), ds2_scr[h % 2])
                dvh = _dot_nt(_heads_side_by_side(do_ref, h), pn2_scr[h % 2])
                dk_ref[hs, :] = dk_hold[hs, :] + dkh[:, 0:BLK]
                dv_ref[hs, :] = dv_hold[hs, :] + dvh[:, 0:BLK]
                dk_hold[hs, :] = dkh[:, BLK:2 * BLK]
                dv_hold[hs, :] = dvh[:, BLK:2 * BLK]

            scores(0)
            for h in range(NKV):
                if h + 1 < NKV:
                    scores(h + 1)
                softmax_bwd(h)
                grads(h)

        @pl.when(n == nb)
        def _():
            dk_ref[...] = dk_hold[...]
            dv_ref[...] = dv_hold[...]
            finish()

    blk = pl.BlockSpec((D, BLK), lambda n: (0, jnp.minimum(n, nb - 1)))
    late = pl.BlockSpec((KVW, BLK), lambda n: (0, jnp.maximum(n - 1, 0)))
    whole = pl.BlockSpec((NKV, GROUP * BLK), lambda n: (0, 0))
    kv = [pl.BlockSpec((KVW, BLK), lambda n: (0, jnp.clip(n - 1, 0, nb - 1))),
          pl.BlockSpec((KVW, BLK), lambda n: (0, jnp.minimum(n, nb - 1)))]
    return pl.pallas_call(
        body, name="attn_bwd", grid=(nb + 1,),
        in_specs=[whole, blk, blk] + kv + kv + [HBM],
        out_specs=[blk, late, late, whole, HBM],
        out_shape=[pltpu.HBM((D, s), f32), pltpu.HBM((KVW, s), f32), pltpu.HBM((KVW, s), f32),
                   jax.ShapeDtypeStruct((NKV, GROUP * BLK), f32), pltpu.HBM((NDEV, WO_ROWS, D), bf16)],
        scratch_shapes=[pltpu.VMEM((KVW, BLK), f32), pltpu.VMEM((KVW, BLK), f32)]
        + [pltpu.VMEM((2, 2 * BLK, GROUP * BLK), f32)] * 2 + [pltpu.VMEM((2, 2 * BLK, GROUP * BLK), bf16)] * 2
        + _comm_sems(1),
        compiler_params=_params(("arbitrary",), 48),
    )(_sink_rows(sinks), *_in_hbm(qt, dot, kt, kt, vt, vt, dwo))


def _block_diag(w):
    w4 = w.reshape(NGRP, 4, HD, HD)
    eye = jnp.eye(4, dtype=w.dtype)
    return jnp.einsum('gjcd,jk->gjckd', w4, eye).reshape(NGRP, 256, 256).astype(bf16)


def _gate_terms(pr, pi, br, bi, sp):
    r = _sigmoid(pr + br)
    i = _sigmoid(pi + bi)
    la = -LRU_C * r * sp
    a = jnp.exp(la)
    x2 = 2.0 * la
    y = jnp.where(x2 > -0.02, -x2 * (1.0 + x2 * (0.5 + x2 * (1.0 / 6.0))), 1.0 - a * a)
    inv_mult = lax.rsqrt(jnp.maximum(y, 1e-30))
    return r, i, a, y * inv_mult, inv_mult


def _gates(u, wr_ref, wi_ref, br, bi, sp):
    ub = u.astype(bf16)
    pr = jnp.concatenate([_dot(ub[:, 256 * g:256 * g + 256], wr_ref[g]) for g in range(NGRP)], axis=1)
    pi = jnp.concatenate([_dot(ub[:, 256 * g:256 * g + 256], wi_ref[g]) for g in range(NGRP)], axis=1)
    return (ub,) + _gate_terms(pr, pi, br, bi, sp)


def _later(x, before, k):
    if k == 0:
        return x
    row = lax.broadcasted_iota(jnp.int32, before.shape, 0)
    rolled = pltpu.roll(x, k, 0)
    first = jnp.where(row < k, pltpu.roll(before, k, 0), rolled[0:8])
    return jnp.concatenate([first, rolled[8:]], axis=0)


def _earlier(x, after, k):
    if k == 0:
        return x
    n = x.shape[0]
    row = lax.broadcasted_iota(jnp.int32, after.shape, 0)
    rolled = pltpu.roll(x, n - k, 0)
    last = jnp.where(row >= 8 - k, pltpu.roll(after, 8 - k, 0), rolled[n - 8:n])
    return jnp.concatenate([rolled[0:n - 8], last], axis=0)


def _lru_fwd(xl, conv_w, conv_b, wr, wi, br, bi, lam, tm):
    s = xl.shape[0]

    def body(xp_ref, x_ref, cw_ref, cb_ref, wr_ref, wi_ref, br_ref, bi_ref, lam_ref, u_ref, h_ref,
             a_scr, b_scr, hcar):
        t0 = pl.program_id(0)

        @pl.when(t0 == 0)
        def _():
            hcar[...] = jnp.zeros_like(hcar)

        x = x_ref[...]
        before = jnp.where(t0 > 0, xp_ref[...], 0.0)
        u = cb_ref[...] + sum(cw_ref[k:k + 1, :] * _later(x, before, CONVW - 1 - k) for k in range(CONVW))
        u_ref[...] = u
        sp = _softplus(-lam_ref[...])
        _, _, i, a, mult, _ = _gates(u, wr_ref, wi_ref, br_ref[...], bi_ref[...], sp)
        a_scr[...] = a
        b_scr[...] = mult * (i * u)

        def step(t, hc):
            hn = a_scr[pl.ds(t, 1), :] * hc + b_scr[pl.ds(t, 1), :]
            h_ref[pl.ds(t, 1), :] = hn
            return hn

        hcar[...] = lax.fori_loop(0, tm, step, hcar[...], unroll=8)

    row = pl.BlockSpec((tm, LW), lambda i: (i, 0))
    prev8 = pl.BlockSpec((8, LW), lambda i: (jnp.maximum(i * (tm // 8) - 1, 0), 0))
    full = lambda a: pl.BlockSpec(a.shape, lambda i: (0,) * a.ndim)
    return pl.pallas_call(
        body, name="lru_fwd", grid=(s // tm,),
        in_specs=[prev8, row, full(conv_w), full(conv_b), full(wr), full(wi), full(br), full(bi), full(lam)],
        out_specs=[row, row],
        out_shape=[pltpu.HBM((s,LW), f32), pltpu.HBM((s,LW), f32)],
        scratch_shapes=[pltpu.VMEM((tm, LW), f32), pltpu.VMEM((tm, LW), f32), pltpu.VMEM((1, LW), f32)],
        compiler_params=_params(("arbitrary",), 48),
    )(*_in_hbm(xl, xl), conv_w, conv_b, wr, wi, br, bi, lam)


def _fwd_in_lru(x, ln_gain, wt, tabs, wo_shard, conv_w, conv_b, wr, wi, br, bi, lam, tm):
    s = x.shape[0]
    nt = s // tm
    nc = 512

    def body(x_ref, g_ref, wt_ref, c_ref, sa_ref, sb_ref, wo_ref, cw_ref, cb_ref, wr_ref, wi_ref, br_ref, bi_ref,
             lam_ref, h_ref, q_ref, k_ref, v_ref, ga_ref, xl_ref, gl_ref, u_ref, hl_ref, wo_all,
             wo_stage, xl_scr, halo, ub_scr, pr_scr, pi_scr, a_scr, b_scr, hcar, send_sems, recv_sems, local_sems):
        i = pl.program_id(0)
        start, finish = _gather_ops([wo_stage], [wo_all], send_sems, recv_sems, local_sems)

        @pl.when(i == 0)
        def _():
            wo_stage[...] = wo_ref[...].astype(bf16)
            start()
            xl_scr[1] = jnp.zeros((tm, LW), f32)
            halo[...] = jnp.zeros_like(halo)
            hcar[...] = jnp.zeros_like(hcar)

        rows_per = tm // 8
        xp_ref = xl_scr.at[(i + 1) % 2]
        sp = _softplus(-lam_ref[...])
        br, bi = br_ref[...], bi_ref[...]

        def lru_conv():
            xp = xp_ref[...]
            u = cb_ref[...] + sum(cw_ref[k:k + 1, :] * _later(xp, halo[...], CONVW - 1 - k) for k in range(CONVW))
            halo[...] = xp[tm - 8:tm, :]
            u_ref[...] = u
            ub_scr[...] = u.astype(bf16)

        def lru_gate_matmuls():
            for g in range(NGRP):
                gs = slice(256 * g, 256 * g + 256)
                pr_scr[:, gs] = _dot(ub_scr[:, gs], wr_ref[g])
                pi_scr[:, gs] = _dot(ub_scr[:, gs], wi_ref[g])

        def lru_terms(piece):
            rows = slice(rows_per * piece, rows_per * piece + rows_per)
            _, ig, a, mult, _ = _gate_terms(pr_scr[rows, :], pi_scr[rows, :], br, bi, sp)
            a_scr[rows, :] = a
            b_scr[rows, :] = mult * (ig * u_ref[rows, :])

        def lru_scan(piece, hc):
            for t in range(rows_per * piece, rows_per * piece + rows_per):
                hc = a_scr[t:t + 1, :] * hc + b_scr[t:t + 1, :]
                hl_ref[t:t + 1, :] = hc
            return hc

        def lru_piece(ci, hc):
            if ci == 0:
                lru_conv()
            elif ci == 1:
                lru_gate_matmuls()
            elif ci == 2:
                lru_terms(0)
                lru_terms(1)
            else:
                hc = lru_scan(ci - 3, hc)
                lru_terms(ci - 1)
            return hc

        xx = x_ref[...]
        rstd = lax.rsqrt(jnp.mean(xx * xx, axis=-1, keepdims=True) + EPS)
        h_ref[...] = (xx * rstd * g_ref[...]).astype(bf16)
        c, sa, sb = c_ref[...], sa_ref[...], sb_ref[...]
        hc = jnp.where(i >= 2, hcar[...], 0.0)

        def z_chunk(ci):
            return _dot_nt(h_ref[...], wt_ref[ci * nc:(ci + 1) * nc, :])

        for ci in range(2):
            z = z_chunk(ci)
            hc = lru_piece(ci, hc)
            for j in range(nc // 128):
                r = _rope(z[:, 128 * j:128 * j + 128], c, sa, sb) * (HD ** -0.5)
                q_ref[:, ci * nc + 128 * j:ci * nc + 128 * j + 128] = r.astype(bf16)
        z = z_chunk(2)
        hc = lru_piece(2, hc)
        for j in range(2):
            k_ref[:, 128 * j:128 * j + 128] = _rope(z[:, 128 * j:128 * j + 128], c, sa, sb).astype(bf16)
        v_ref[...] = z[:, 256:512].astype(bf16)
        for sec, ref in enumerate((ga_ref, xl_ref, gl_ref)):
            for j in range(2):
                z = z_chunk(3 + 2 * sec + j)
                hc = lru_piece(3 + 2 * sec + j, hc)
                ref[:, j * nc:(j + 1) * nc] = z
                if sec == 1:
                    xl_scr[i % 2, :, j * nc:(j + 1) * nc] = z
        hcar[...] = lru_scan(7, lru_scan(6, hc))

        @pl.when(i == nt)
        def _():
            finish()

    cur = lambda w: pl.BlockSpec((tm, w), lambda i: (jnp.minimum(i, nt - 1), 0))
    prev = pl.BlockSpec((tm, LW), lambda i: (jnp.maximum(i - 1, 0), 0))
    full = lambda a: pl.BlockSpec(a.shape, lambda i: (0,) * a.ndim)
    big = lambda w, dt: pltpu.HBM((s, w), dt)
    return pl.pallas_call(
        body, name="fwd_in_lru", grid=(nt + 1,),
        in_specs=[cur(D), full(ln_gain), full(wt), cur(128), cur(128), cur(128), full(wo_shard), full(conv_w),
                  full(conv_b), full(wr), full(wi), full(br), full(bi), full(lam)],
        out_specs=[cur(D), cur(D), cur(KVW), cur(KVW), cur(D), cur(D), cur(D), prev, prev, HBM],
        out_shape=[big(D, bf16), big(D, bf16), big(KVW, bf16), big(KVW, bf16), big(D, f32), big(D, f32), big(D, f32),
                   big(LW, f32), big(LW, f32), pltpu.HBM((2 * D, D), bf16)],
        scratch_shapes=[pltpu.VMEM((WO_ROWS, D), bf16), pltpu.VMEM((2, tm, LW), f32), pltpu.VMEM((8, LW), f32),
                        pltpu.VMEM((tm, LW), bf16)] + [pltpu.VMEM((tm, LW), f32)] * 4 + [pltpu.VMEM((1, LW), f32)]
        + _comm_sems(1),
        compiler_params=_params(("arbitrary",), 56),
    )(*_in_hbm(x), ln_gain, *_in_hbm(wt), *tabs, wo_shard, conv_w, conv_b, wr, wi, br, bi, lam)


def _fwd_pipeline(x, ln_gain, wt, tabs, wo_shard, conv_w, conv_b, wr, wi, br, bi, lam, tm):
    s = x.shape[0]
    nt = s // tm
    nc = 512
    pieces = 8
    rows_per = tm // pieces

    def body(x0_ref, xn_ref, g_ref, wt_ref, c_ref, sa_ref, sb_ref, wo_ref, cw_ref, cb_ref, wr_ref, wi_ref, br_ref,
             bi_ref, lam_ref, h_ref, q_ref, k_ref, v_ref, ga_ref, xl_ref, gl_ref, u_ref, hl_ref, wo_all,
             wo_stage, hb, xl_scr, halo, u_scr, ub_scr, pr_scr, pi_scr, a_scr, b_scr, hcar,
             send_sems, recv_sems, local_sems):
        i = pl.program_id(0)
        start, finish = _gather_ops([wo_stage], [wo_all], send_sems, recv_sems, local_sems)
        gain = g_ref[...]

        def normed(xx):
            rstd = lax.rsqrt(jnp.mean(xx * xx, axis=-1, keepdims=True) + EPS)
            return (xx * rstd * gain).astype(bf16)

        @pl.when(i == 0)
        def _():
            wo_stage[...] = wo_ref[...].astype(bf16)
            start()
            hb[0] = normed(x0_ref[...])
            xl_scr[1] = jnp.zeros((tm, LW), f32)
            u_scr[0] = jnp.zeros((tm, LW), f32)
            ub_scr[0] = jnp.zeros((tm, LW), bf16)
            halo[...] = jnp.zeros_like(halo)
            hcar[...] = jnp.zeros_like(hcar)

        cur, nxt = 0, 1

        sp = _softplus(-lam_ref[...])
        br, bi = br_ref[...], bi_ref[...]
        c, sa, sb = c_ref[...], sa_ref[...], sb_ref[...]
        piece_rows = lambda p: slice(rows_per * p, rows_per * p + rows_per)

        def project(ci):
            z = _dot_nt(hb[cur], wt_ref[ci * nc:(ci + 1) * nc, :])
            if ci < 2:
                for j in range(nc // 128):
                    r = _rope(z[:, 128 * j:128 * j + 128], c, sa, sb) * (HD ** -0.5)
                    q_ref[:, ci * nc + 128 * j:ci * nc + 128 * j + 128] = r.astype(bf16)
            elif ci == 2:
                for j in range(2):
                    k_ref[:, 128 * j:128 * j + 128] = _rope(z[:, 128 * j:128 * j + 128], c, sa, sb).astype(bf16)
                v_ref[...] = z[:, 256:512].astype(bf16)
            else:
                sec, j = divmod(ci - 3, 2)
                (ga_ref, xl_ref, gl_ref)[sec][:, j * nc:(j + 1) * nc] = z
                if sec == 1:
                    xl_scr[cur, :, j * nc:(j + 1) * nc] = z

        def gate_matmuls():
            for g in range(NGRP):
                gs = slice(256 * g, 256 * g + 256)
                pr_scr[:, gs] = _dot(ub_scr[cur, :, gs], wr_ref[g])
                pi_scr[:, gs] = _dot(ub_scr[cur, :, gs], wi_ref[g])

        def gate_terms(p):
            rows = piece_rows(p)
            _, ig, a, mult, _ = _gate_terms(pr_scr[rows, :], pi_scr[rows, :], br, bi, sp)
            a_scr[rows, :] = a
            b_scr[rows, :] = mult * (ig * u_scr[cur, rows, :])

        def scan(p, hc):
            for t in range(rows_per * p, rows_per * p + rows_per):
                hc = a_scr[t:t + 1, :] * hc + b_scr[t:t + 1, :]
                hl_ref[t:t + 1, :] = hc
            return hc

        def conv(p):
            rows = piece_rows(p)
            xp = xl_scr[nxt, rows, :]
            before = halo[...] if p == 0 else xl_scr[nxt, rows_per * p - 8:rows_per * p, :]
            u = cb_ref[...] + sum(cw_ref[k:k + 1, :] * _later(xp, before, CONVW - 1 - k) for k in range(CONVW))
            u_scr[nxt, rows, :] = u
            ub_scr[nxt, rows, :] = u.astype(bf16)

        def norm(p):
            hb[nxt, piece_rows(p), :] = normed(xn_ref[piece_rows(p), :])

        def run():
            h_ref[...] = hb[cur]
            gate_matmuls()
            hc = jnp.where(i >= 3, hcar[...], 0.0)
            for ci in range(NIN // nc):
                project(ci)
                if ci < pieces:
                    conv(ci)
                    norm(ci)
                if ci >= 1:
                    gate_terms(ci - 1)
                if ci >= 2:
                    hc = scan(ci - 2, hc)
            hcar[...] = scan(pieces - 1, hc)
            halo[...] = xl_scr[nxt, tm - 8:tm, :]

            @pl.when(i <= nt)
            def _():
                u_ref[...] = u_scr[nxt]

        for parity in range(2):
            cur, nxt = parity, 1 - parity
            pl.when(i % 2 == parity)(run)

        @pl.when(i == nt + 1)
        def _():
            finish()

    at = lambda w, off: pl.BlockSpec((tm, w), lambda i: (jnp.clip(i + off, 0, nt - 1), 0))
    full = lambda a: pl.BlockSpec(a.shape, lambda i: (0,) * a.ndim)
    big = lambda w, dt: pltpu.HBM((s, w), dt)
    ring = lambda dt: pltpu.VMEM((2, tm, LW), dt)
    tile = pltpu.VMEM((tm, LW), f32)
    return pl.pallas_call(
        body, name="fwd_pipeline", grid=(nt + 2,),
        in_specs=[pl.BlockSpec((tm, D), lambda i: (0, 0)), at(D, 1), full(ln_gain), full(wt), at(128, 0), at(128, 0),
                  at(128, 0), full(wo_shard), full(conv_w), full(conv_b), full(wr), full(wi), full(br), full(bi),
                  full(lam)],
        out_specs=[at(D, 0), at(D, 0), at(KVW, 0), at(KVW, 0), at(D, 0), at(D, 0), at(D, 0), at(LW, -1), at(LW, -2),
                   HBM],
        out_shape=[big(D, bf16), big(D, bf16), big(KVW, bf16), big(KVW, bf16), big(D, f32), big(D, f32), big(D, f32),
                   big(LW, f32), big(LW, f32), pltpu.HBM((2 * D, D), bf16)],
        scratch_shapes=[pltpu.VMEM((WO_ROWS, D), bf16), ring(bf16), ring(f32), pltpu.VMEM((8, LW), f32), ring(f32),
                        ring(bf16), tile, tile, tile, tile, pltpu.VMEM((1, LW), f32)] + _comm_sems(1),
        compiler_params=_params(("arbitrary",), 56),
    )(*_in_hbm(x, x), ln_gain, *_in_hbm(wt), *tabs, wo_shard, conv_w, conv_b, wr, wi, br, bi, lam)


def _fwd_fused(x, ln_gain, wt, tabs, wo_shard, conv_w, conv_b, wr, wi, br, bi, lam, tm):
    s = x.shape[0]
    nt = s // tm
    nc = 512
    pieces = 8
    rows_per = tm // pieces
    later_chunks = (0, 1, 2, 3, 4, 7, 8)

    def body(x0_ref, xn_ref, g_ref, wt_ref, c_ref, sa_ref, sb_ref, wo_ref, cw_ref, cb_ref, wr_ref, wi_ref, br_ref,
             bi_ref, lam_ref, h_ref, q_ref, k_ref, v_ref, ga_ref, xl_ref, gl_ref, u_ref, hl_ref, wo_all,
             wo_stage, hb, halo, ub_scr, pr_scr, pi_scr, a_scr, b_scr, hcar, send_sems, recv_sems, local_sems):
        i = pl.program_id(0)
        start, finish = _gather_ops([wo_stage], [wo_all], send_sems, recv_sems, local_sems)
        gain = g_ref[...]

        def normed(xx):
            rstd = lax.rsqrt(jnp.mean(xx * xx, axis=-1, keepdims=True) + EPS)
            return (xx * rstd * gain).astype(bf16)

        @pl.when(i == 0)
        def _():
            wo_stage[...] = wo_ref[...].astype(bf16)
            start()
            hb[0] = normed(x0_ref[...])
            halo[...] = jnp.zeros_like(halo)
            hcar[...] = jnp.zeros_like(hcar)

        cur, nxt = i % 2, (i + 1) % 2
        sp = _softplus(-lam_ref[...])
        br, bi = br_ref[...], bi_ref[...]
        c, sa, sb = c_ref[...], sa_ref[...], sb_ref[...]
        piece_rows = lambda p: slice(rows_per * p, rows_per * p + rows_per)

        def project(ci):
            z = _dot_nt(hb[cur], wt_ref[ci * nc:(ci + 1) * nc, :])
            if ci < 2:
                for j in range(nc // 128):
                    r = _rope(z[:, 128 * j:128 * j + 128], c, sa, sb) * (HD ** -0.5)
                    q_ref[ci * nc + 128 * j:ci * nc + 128 * j + 128, :] = r.astype(bf16).T
            elif ci == 2:
                for j in range(2):
                    js = slice(128 * j, 128 * j + 128)
                    k_ref[js, :] = _rope(z[:, js], c, sa, sb).astype(bf16).T
                    v_ref[js, :] = z[:, KVW + 128 * j:KVW + 128 * j + 128].astype(bf16).T
            else:
                sec, j = divmod(ci - 3, 2)
                (ga_ref, xl_ref, gl_ref)[sec][:, j * nc:(j + 1) * nc] = z

        def gate_terms(p):
            rows = piece_rows(p)
            _, ig, a, mult, _ = _gate_terms(pr_scr[rows, :], pi_scr[rows, :], br, bi, sp)
            a_scr[rows, :] = a
            b_scr[rows, :] = mult * (ig * u_ref[rows, :])

        def scan(p, hc):
            for t in range(rows_per * p, rows_per * p + rows_per):
                hc = a_scr[t:t + 1, :] * hc + b_scr[t:t + 1, :]
                hl_ref[t:t + 1, :] = hc
            return hc

        def norm_next(p):
            hb[nxt, piece_rows(p), :] = normed(xn_ref[piece_rows(p), :])

        h_ref[...] = hb[cur]
        project(5)
        project(6)
        xl = xl_ref[...]
        u = cb_ref[...] + sum(cw_ref[k:k + 1, :] * _later(xl, halo[...], CONVW - 1 - k) for k in range(CONVW))
        halo[...] = xl[tm - 8:tm, :]
        u_ref[...] = u
        ub_scr[...] = u.astype(bf16)
        for g in range(NGRP):
            gs = slice(256 * g, 256 * g + 256)
            pr_scr[:, gs] = _dot(ub_scr[:, gs], wr_ref[g])
            pi_scr[:, gs] = _dot(ub_scr[:, gs], wi_ref[g])
        hc = hcar[...]
        gate_terms(0)
        for slot, ci in enumerate(later_chunks):
            project(ci)
            norm_next(slot)
            gate_terms(slot + 1)
            hc = scan(slot, hc)
        norm_next(pieces - 1)
        hcar[...] = scan(pieces - 1, hc)

        @pl.when(i == nt - 1)
        def _():
            finish()

    row = lambda w: pl.BlockSpec((tm, w), lambda i: (i, 0))
    col = lambda w: pl.BlockSpec((w, tm), lambda i: (0, i))
    full = lambda a: pl.BlockSpec(a.shape, lambda i: (0,) * a.ndim)
    big = lambda w, dt: pltpu.HBM((s, w), dt)
    tile = pltpu.VMEM((tm, LW), f32)
    return pl.pallas_call(
        body, name="fwd_fused", grid=(nt,),
        in_specs=[pl.BlockSpec((tm, D), lambda i: (0, 0)), pl.BlockSpec((tm, D), lambda i: (jnp.minimum(i + 1, nt - 1), 0)),
                  full(ln_gain), full(wt), row(128), row(128), row(128), full(wo_shard), full(conv_w), full(conv_b),
                  full(wr), full(wi), full(br), full(bi), full(lam)],
        out_specs=[row(D), col(D), col(KVW), col(KVW), row(D), row(D), row(D), row(LW), row(LW), HBM],
        out_shape=[big(D, bf16), pltpu.HBM((D, s), bf16), pltpu.HBM((KVW, s), bf16), pltpu.HBM((KVW, s), bf16),
                   big(D, f32), big(D, f32), big(D, f32), big(LW, f32), big(LW, f32), pltpu.HBM((2 * D, D), bf16)],
        scratch_shapes=[pltpu.VMEM((WO_ROWS, D), bf16), pltpu.VMEM((2, tm, D), bf16), pltpu.VMEM((8, LW), f32),
                        pltpu.VMEM((tm, LW), bf16), tile, tile, tile, tile, pltpu.VMEM((1, LW), f32)] + _comm_sems(1),
        compiler_params=_params(("arbitrary",), 56),
    )(*_in_hbm(x, x), ln_gain, *_in_hbm(wt), *tabs, wo_shard, conv_w, conv_b, wr, wi, br, bi, lam)


def _lru_bwd(u, hl, dhl, xl, conv_w, wr, wi, br, bi, lam, tm):
    s = u.shape[0]
    nt = s // tm

    def body(u_ref, h_ref, hp_ref, dh_ref, x_ref, xp_ref, cw_ref, wr_ref, wi_ref, br_ref, bi_ref, lam_ref,
             dxl_ref, dwr_ref, dwi_ref, dbr_ref, dbi_ref, dlam_ref, dcb_ref, dcw_ref,
             a_scr, l_scr, lcar, dunext):
        t0 = pl.program_id(0)
        tile = nt - 1 - t0

        @pl.when(t0 == 0)
        def _():
            lcar[...] = jnp.zeros_like(lcar)
            dunext[...] = jnp.zeros_like(dunext)
            for ref in (dwr_ref, dwi_ref, dbr_ref, dbi_ref, dlam_ref, dcb_ref, dcw_ref):
                ref[...] = jnp.zeros_like(ref)

        u = u_ref[...]
        lam = lam_ref[...]
        sp = _softplus(-lam)
        ub, r, i, a, mult, inv_mult = _gates(u, wr_ref, wi_ref, br_ref[...], bi_ref[...], sp)
        a_scr[...] = a

        def step(k, c):
            t = tm - 1 - k
            lt = dh_ref[pl.ds(t, 1), :] + c
            l_scr[pl.ds(t, 1), :] = lt
            return a_scr[pl.ds(t, 1), :] * lt

        lcar[...] = lax.fori_loop(0, tm, step, lcar[...], unroll=8)
        lt = l_scr[...]

        hprev = _later(h_ref[...], jnp.where(tile > 0, hp_ref[...], 0.0), 1)
        da = lt * hprev
        dmult = lt * (i * u)
        di = lt * mult * u
        du = lt * mult * i
        dla = da * a - dmult * (a * a) * inv_mult
        dr = dla * (-LRU_C * sp)
        dlam_ref[...] += jnp.sum(dla * (-LRU_C * r), axis=0, keepdims=True)
        dpr = dr * r * (1.0 - r)
        dpi = di * i * (1.0 - i)
        dbr_ref[...] += jnp.sum(dpr, axis=0, keepdims=True)
        dbi_ref[...] += jnp.sum(dpi, axis=0, keepdims=True)
        dprb, dpib = dpr.astype(bf16), dpi.astype(bf16)
        dug = []
        for g in range(NGRP):
            gs = slice(256 * g, 256 * g + 256)
            dwr_ref[g] += _dot_tn(ub[:, gs], dprb[:, gs])
            dwi_ref[g] += _dot_tn(ub[:, gs], dpib[:, gs])
            dug.append(_dot_nt(dprb[:, gs], wr_ref[g]) + _dot_nt(dpib[:, gs], wi_ref[g]))
        du = du + jnp.concatenate(dug, axis=1)

        dcb_ref[...] += jnp.sum(du, axis=0, keepdims=True)
        x = x_ref[...]
        before = jnp.where(tile > 0, xp_ref[...], 0.0)
        for k in range(CONVW):
            dcw_ref[k:k + 1, :] += jnp.sum(du * _later(x, before, CONVW - 1 - k), axis=0, keepdims=True)
        after = dunext[...]
        dxl = sum(cw_ref[k:k + 1, :] * _earlier(du, after, CONVW - 1 - k) for k in range(CONVW))
        dxl_ref[...] = dxl.astype(bf16)
        dunext[...] = du[0:8, :]

        @pl.when(t0 == nt - 1)
        def _():
            dlam_ref[...] = dlam_ref[...] * (-_sigmoid(-lam))

    rev = lambda i: (nt - 1 - i, 0)
    row = pl.BlockSpec((tm, LW), rev)
    prev8 = pl.BlockSpec((8, LW), lambda i: (jnp.maximum((nt - 1 - i) * (tm // 8) - 1, 0), 0))
    full = lambda a: pl.BlockSpec(a.shape, lambda i: (0,) * a.ndim)
    vec = pl.BlockSpec((1, LW), lambda i: (0, 0))
    bd = pl.BlockSpec((NGRP, 256, 256), lambda i: (0, 0, 0))
    return pl.pallas_call(
        body, name="lru_bwd", grid=(nt,),
        in_specs=[row, row, prev8, row, row, prev8, full(conv_w), full(wr), full(wi), full(br), full(bi), full(lam)],
        out_specs=[row, bd, bd, vec, vec, vec, vec, pl.BlockSpec((CONVW, LW), lambda i: (0, 0))],
        out_shape=[pltpu.HBM((s,LW), bf16),
                   jax.ShapeDtypeStruct((NGRP, 256, 256), f32), jax.ShapeDtypeStruct((NGRP, 256, 256), f32),
                   jax.ShapeDtypeStruct((1, LW), f32), jax.ShapeDtypeStruct((1, LW), f32),
                   jax.ShapeDtypeStruct((1, LW), f32), jax.ShapeDtypeStruct((1, LW), f32),
                   jax.ShapeDtypeStruct((CONVW, LW), f32)],
        scratch_shapes=[pltpu.VMEM((tm, LW), f32), pltpu.VMEM((tm, LW), f32),
                        pltpu.VMEM((1, LW), f32), pltpu.VMEM((8, LW), f32)],
        compiler_params=_params(("arbitrary",), 56),
    )(*_in_hbm(u, hl, hl, dhl, xl, xl), conv_w, wr, wi, br, bi, lam)


def _gated_norm(t, gate, gain):
    sg = _sigmoid(gate)
    silu = gate * sg
    p = t * silu
    rstd = lax.rsqrt(jnp.mean(p * p, axis=-1, keepdims=True) + EPS)
    ph = p * rstd
    return sg, silu, rstd, ph, ph * gain


def _gated_norm_bwd(dy, t, gate, gain, sg, silu, rstd, ph):
    w = dy * gain
    dp = rstd * (w - ph * jnp.mean(w * ph, axis=-1, keepdims=True))
    dgate = dp * t * (sg * (1.0 + gate * (1.0 - sg)))
    return jnp.sum(dy * ph, axis=0, keepdims=True), dp * silu, dgate


def _out_fwd_bwd(x, tgt, o, ga, hl, gl, again, lgain, fgain, wo, tm):
    s = x.shape[0]
    nt = s // tm

    def body(x_ref, t_ref, o_ref, ga_ref, hl_ref, gl_ref, ag_ref, lg_ref, fg_ref, wo_ref,
             dx2_ref, do_ref, dga_ref, dhl_ref, dgl_ref, dwo_ref, gfg_ref, gag_ref, glg_ref, loss_ref, acc):
        i = pl.program_id(0)

        @pl.when(i == 0)
        def _():
            acc[...] = jnp.zeros_like(acc)
            for ref in (gfg_ref, gag_ref, glg_ref, loss_ref):
                ref[...] = jnp.zeros_like(ref)

        oo = jnp.concatenate([o_ref[128 * j:128 * j + 128, :].T for j in range(D // 128)], axis=1)
        gga, hh, ggl = ga_ref[...], hl_ref[...], gl_ref[...]
        ag, lg, fg = ag_ref[...], lg_ref[...], fg_ref[...]
        sga, silua, ra, pah, ya = _gated_norm(oo, gga, ag)
        sgl, silul, rl, plh, yl = _gated_norm(hh, ggl, lg)
        yab, ylb = ya.astype(bf16), yl.astype(bf16)
        y = _dot(yab, wo_ref[0:D, :]) + _dot(ylb, wo_ref[D:2 * D, :])
        x2 = x_ref[...] + y
        r2 = lax.rsqrt(jnp.mean(x2 * x2, axis=-1, keepdims=True) + EPS)
        x2h = x2 * r2
        err = x2h * fg - t_ref[...]
        loss_ref[...] += 0.5 * jnp.sum(jnp.sum(err * err, axis=-1, keepdims=True) * (1.0 / D))
        dout = err * (1.0 / D)
        gfg_ref[...] += jnp.sum(dout * x2h, axis=0, keepdims=True)
        w = dout * fg
        dx2 = r2 * (w - x2h * jnp.mean(w * x2h, axis=-1, keepdims=True))
        dx2_ref[...] = dx2
        dyb = dx2.astype(bf16)
        acc[0:D, :] += _dot_tn(yab, dyb)
        acc[D:2 * D, :] += _dot_tn(ylb, dyb)
        dya = _dot_nt(dyb, wo_ref[0:D, :])
        dyl = _dot_nt(dyb, wo_ref[D:2 * D, :])
        gag, do, dga = _gated_norm_bwd(dya, oo, gga, ag, sga, silua, ra, pah)
        glg, dhl, dgl = _gated_norm_bwd(dyl, hh, ggl, lg, sgl, silul, rl, plh)
        gag_ref[...] += gag
        glg_ref[...] += glg
        dob = do.astype(bf16)
        for j in range(D // 128):
            do_ref[128 * j:128 * j + 128, :] = dob[:, 128 * j:128 * j + 128].T
        dga_ref[...] = dga.astype(bf16)
        dhl_ref[...] = dhl
        dgl_ref[...] = dgl.astype(bf16)

        @pl.when(i == nt - 1)
        def _():
            dwo_ref[...] = acc[...].astype(bf16)

    row = pl.BlockSpec((tm, D), lambda i: (i, 0))
    col = pl.BlockSpec((D, tm), lambda i: (0, i))
    vec = pl.BlockSpec((1, D), lambda i: (0, 0))
    mat = pl.BlockSpec((2 * D, D), lambda i: (0, 0))
    return pl.pallas_call(
        body, name="out_fwd_bwd", grid=(nt,),
        in_specs=[row, row, col, row, row, row] + [vec] * 3 + [mat],
        out_specs=[row, col, row, row, row] + [mat, vec, vec, vec, pl.BlockSpec((1, 128), lambda i: (0, 0))],
        out_shape=[pltpu.HBM((s,D), f32), pltpu.HBM((D, s), bf16),
                   pltpu.HBM((s,D), bf16), pltpu.HBM((s,D), f32),
                   pltpu.HBM((s,D), bf16), pltpu.HBM((2 * D, D), bf16),
                   jax.ShapeDtypeStruct((1, D), f32), jax.ShapeDtypeStruct((1, D), f32),
                   jax.ShapeDtypeStruct((1, D), f32), jax.ShapeDtypeStruct((1, 128), f32)],
        scratch_shapes=[pltpu.VMEM((2 * D, D), f32)],
        compiler_params=_params(("arbitrary",), 56),
    )(*_in_hbm(x, tgt, o, ga, hl, gl), again, lgain, fgain, *_in_hbm(wo))


def _bwd_in(x, dx2, dq, dk, dv, dga, dxl, dgl, ln_gain, wt, tabs, tm):
    s = x.shape[0]

    def body(x_ref, dx2_ref, dq_ref, dk_ref, dv_ref, dga_ref, dxl_ref, dgl_ref, g_ref, wt_ref,
             c_ref, sa_ref, sb_ref, gx_ref, gln_ref, dzt_ref):
        @pl.when(pl.program_id(0) == 0)
        def _():
            gln_ref[...] = jnp.zeros_like(gln_ref)

        c, sa, sb = c_ref[...], sa_ref[...], sb_ref[...]
        for j in range(D // 128):
            js = slice(128 * j, 128 * j + 128)
            dzt_ref[js, :] = (_unrope_t(dq_ref[js, :], c, sa, sb) * (HD ** -0.5)).astype(bf16)
        for j in range(KVW // 128):
            js = slice(128 * j, 128 * j + 128)
            dzt_ref[D + 128 * j:D + 128 * j + 128, :] = _unrope_t(dk_ref[js, :], c, sa, sb).astype(bf16)
        dzt_ref[D + KVW:D + 2 * KVW, :] = dv_ref[...].astype(bf16)
        first = D + 2 * KVW
        dh = _dot_tn(dzt_ref[0:512, :], wt_ref[0:512, :])
        for ci in range(1, first // 512):
            dh = dh + _dot_tn(dzt_ref[512 * ci:512 * ci + 512, :], wt_ref[512 * ci:512 * ci + 512, :])
        for sec, ref in enumerate((dga_ref, dxl_ref, dgl_ref)):
            for j in range(D // 512):
                rows = slice(first + D * sec + 512 * j, first + D * sec + 512 * j + 512)
                dh = dh + _dot(ref[:, 512 * j:512 * j + 512], wt_ref[rows, :])
            for j in range(D // 128):
                dzt_ref[first + D * sec + 128 * j:first + D * sec + 128 * j + 128, :] = ref[:, 128 * j:128 * j + 128].T
        xx = x_ref[...]
        rstd = lax.rsqrt(jnp.mean(xx * xx, axis=-1, keepdims=True) + EPS)
        xh = xx * rstd
        gln_ref[...] += jnp.sum(dh * xh, axis=0, keepdims=True)
        w = dh * g_ref[...]
        gx_ref[...] = dx2_ref[...] + rstd * (w - xh * jnp.mean(w * xh, axis=-1, keepdims=True))

    row = lambda w: pl.BlockSpec((tm, w), lambda i: (i, 0))
    col = lambda w: pl.BlockSpec((w, tm), lambda i: (0, i))
    full = lambda a: pl.BlockSpec(a.shape, lambda i: (0, 0))
    return pl.pallas_call(
        body, name="bwd_in", grid=(s // tm,),
        in_specs=[row(D), row(D), col(D), col(KVW), col(KVW), row(D), row(D), row(D), full(ln_gain), full(wt),
                  col(128), col(128), col(128)],
        out_specs=[row(D), pl.BlockSpec((1, D), lambda i: (0, 0)), col(NIN)],
        out_shape=[pltpu.HBM((s,D), f32), jax.ShapeDtypeStruct((1, D), f32),
                   pltpu.HBM((NIN, s), bf16)],
        compiler_params=_params(("arbitrary",), 56),
    )(*_in_hbm(x, dx2, dq, dk, dv, dga, dxl, dgl), ln_gain, *_in_hbm(wt), *tabs)


WT_TERMS = 5


def _dwt_scatter(dzt, h, small, tm):
    s = h.shape[0]
    nk = s // tm
    srows = small.shape[0] // NDEV
    last = NDEV - 1

    def body(order_ref, dz_ref, h_ref, sm_ref, lwt_ref, lsm_ref, acc, stage, given, send_sems, recv_sems, local_sem,
             sm_send, sm_recv, sm_local):
        j, k = pl.program_id(0), pl.program_id(1)
        x, y, c = _place()
        sibling = (x, y, 1 - c)
        chips = [(1 - x, 1 - y), (1 - x, y), (x, 1 - y)]
        sm_start, sm_finish = _scatter_ops([sm_ref], [lsm_ref], sm_send, sm_recv, sm_local)

        def send(step):
            if step == last - 1:
                dst, to = lwt_ref.at[1], sibling
            elif step % 2 == 0:
                dst, to = given.at[step // 2], sibling
            else:
                dst, to = lwt_ref.at[2 + step // 2], (*chips[step // 2], c)
            return pltpu.make_async_remote_copy(
                src_ref=stage.at[step % 2], dst_ref=dst, send_sem=send_sems.at[step], recv_sem=recv_sems.at[step],
                device_id=to, device_id_type=MESH)

        def keep():
            return pltpu.make_async_copy(stage.at[last % 2], lwt_ref.at[0], local_sem)

        @pl.when((j == 0) & (k == 0))
        def _():
            sm_start()

        @pl.when(k == 0)
        def _():
            acc[...] = jnp.zeros_like(acc)

        acc[...] += _dot(dz_ref[...], h_ref[...])

        for step in range(NDEV):
            @pl.when((k == nk - 1) & (j == step))
            def _(step=step):
                if step >= 2:
                    send(step - 2).wait_send()
                if step % 2 == 1 and step < last:
                    send(step - 1).wait_recv()
                    stage[step % 2] = (acc[...] + given[step // 2].astype(f32)).astype(bf16)
                else:
                    stage[step % 2] = acc[...].astype(bf16)
                if step < last:
                    send(step).start()
                else:
                    keep().start()
                    send(last - 1).wait_send()
                    for peer_step in (1, 3, 5, last - 1):
                        send(peer_step).wait_recv()
                    keep().wait()
                    sm_finish()

    x, y, c = _place()
    dest = lambda cx, cy, cc: 4 * cx + 2 * cy + cc
    order = jnp.stack([dest(1 - x, 1 - y, 1 - c), dest(1 - x, 1 - y, c), dest(1 - x, y, 1 - c), dest(1 - x, y, c),
                       dest(x, 1 - y, 1 - c), dest(x, 1 - y, c), dest(x, y, 1 - c), dest(x, y, c)])
    return pl.pallas_call(
        body, name="dwt_scatter",
        grid_spec=pltpu.PrefetchScalarGridSpec(
            num_scalar_prefetch=1, grid=(NDEV, nk),
            in_specs=[pl.BlockSpec((WT_ROWS, tm), lambda j, k, order: (order[j], k)),
                      pl.BlockSpec((tm, D), lambda j, k, order: (k, 0)), HBM],
            out_specs=[HBM, HBM],
            scratch_shapes=[pltpu.VMEM((WT_ROWS, D), f32), pltpu.VMEM((2, WT_ROWS, D), bf16),
                            pltpu.VMEM((3, WT_ROWS, D), bf16),
                            pltpu.SemaphoreType.DMA((last,)), pltpu.SemaphoreType.DMA((last,)),
                            pltpu.SemaphoreType.DMA(())] + _comm_sems(1)),
        out_shape=[pltpu.HBM((WT_TERMS, WT_ROWS, D), bf16), pltpu.HBM((NDEV, srows, D), f32)],
        compiler_params=_params(("arbitrary", "arbitrary"), 32),
    )(order, *_in_hbm(dzt, h, small))


def _diag_blocks(bd):
    eye = jnp.eye(4, dtype=bd.dtype)
    return jnp.einsum('gjckd,jk->gjcd', bd.reshape(NGRP, 4, HD, 4, HD), eye).reshape(NQ, HD, HD)


def _sequence_step(x, tgt, wt, wo_shard, conv_w, p):
    s = x.shape[0]
    tm = min(256, s)
    tabs, tabs_t = _rope_tables(s)
    wr, wi = _block_diag(p["w_rgate"]), _block_diag(p["w_igate"])
    sinks = p["sinks"].reshape(NQ)
    h, qt, kt, vt, ga, xl, gl, u, hl, wo = _fwd_fused(x, p["ln_gain"], wt, tabs, wo_shard, conv_w, p["conv_b"], wr, wi,
                                                      p["b_rgate"], p["b_igate"], p["lru_lambda"], tm)
    ot = _attn_fwd_t(qt, kt, vt, sinks)
    dx2, dot, dga, dhl, dgl, dwo, g_fg, g_ag, g_lg, loss = _out_fwd_bwd(
        x, tgt, ot, ga, hl, gl, p["attn_out_gain"], p["lru_out_gain"], p["final_gain"], wo, tm)
    dqt, dkt, dvt, dsink, land_wo = _attn_bwd_t(qt, kt, vt, dot, sinks, dwo)
    dxl, dwr, dwi, dbr, dbi, dlam, dcb, dcw = _lru_bwd(
        u, hl, dhl, xl, conv_w, wr, wi, p["b_rgate"], p["b_igate"], p["lru_lambda"], tm)
    gx, g_ln, dzt = _bwd_in(x, dx2, dqt, dkt, dvt, dga, dxl, dgl, p["ln_gain"], wt, tabs_t, tm)
    small = dict(ln_gain=g_ln, sinks=dsink.reshape(NQ, BLK).sum(axis=1)[None], conv_w=dcw, conv_b=dcb,
                 w_rgate=_diag_blocks(dwr), b_rgate=dbr, w_igate=_diag_blocks(dwi), b_igate=dbi, lru_lambda=dlam,
                 attn_out_gain=g_ag, lru_out_gain=g_lg, final_gain=g_fg)
    land_wt, land_sm = _dwt_scatter(dzt, h, _pack_small(small, loss), min(512, s))
    return gx, land_wt, land_wo, land_sm


def _all_gather(srcs, out_dtypes, name):
    n = len(srcs)
    cast = [a.dtype != dt for a, dt in zip(srcs, out_dtypes)]

    def body(*refs):
        src_refs, out_refs = refs[:n], refs[n:2 * n]
        stage_refs = list(refs[2 * n:2 * n + sum(cast)])
        mine_refs = []
        for a in range(n):
            if cast[a]:
                st = stage_refs.pop(0)
                st[...] = src_refs[a][...].astype(out_dtypes[a])
                mine_refs.append(st)
            else:
                mine_refs.append(src_refs[a])
        start, finish = _gather_ops(mine_refs, out_refs, *refs[-3:])
        start()
        finish()

    vmem = pl.BlockSpec(memory_space=pltpu.VMEM)
    return pl.pallas_call(
        body, name=name,
        in_specs=[vmem] * n, out_specs=[HBM] * n,
        out_shape=[pltpu.HBM((NDEV * a.shape[0], a.shape[1]), dt) for a, dt in zip(srcs, out_dtypes)],
        scratch_shapes=[pltpu.VMEM(a.shape, dt) for a, dt, cst in zip(srcs, out_dtypes, cast) if cst] + _comm_sems(n),
        compiler_params=pltpu.CompilerParams(vmem_limit_bytes=32 * MIB),
    )(*srcs)


def _sum_slots(land, tr, name):
    terms, rows, cols = land.shape

    def body(l_ref, o_ref):
        acc = l_ref[0].astype(f32)
        for d in range(1, terms):
            acc = acc + l_ref[d].astype(f32)
        o_ref[...] = acc

    return pl.pallas_call(
        body, name=name, grid=(rows // tr,),
        in_specs=[pl.BlockSpec((terms, tr, cols), lambda i: (0, i, 0))],
        out_specs=pl.BlockSpec((tr, cols), lambda i: (i, 0)),
        out_shape=jax.ShapeDtypeStruct((rows, cols), f32),
        compiler_params=_params(("arbitrary",), 32),
    )(*_in_hbm(land))


def _adam_math(w, g, m, v):
    m2 = ADAM_B1 * m + (1.0 - ADAM_B1) * g
    v2 = ADAM_B2 * v + (1.0 - ADAM_B2) * (g * g)
    m_hat = m2 / (1.0 - ADAM_B1 ** ADAM_STEP)
    v_hat = v2 / (1.0 - ADAM_B2 ** ADAM_STEP)
    delta = -ADAM_LR * (m_hat / (jnp.sqrt(v_hat) + ADAM_EPS) + ADAM_WD * w)
    return delta, m2, v2


def _adamw(w, g, m, v, tr, name):
    rows, cols = w.shape

    def body(w_ref, g_ref, m_ref, v_ref, d_ref, m2_ref, v2_ref):
        d_ref[...], m2_ref[...], v2_ref[...] = _adam_math(w_ref[...], g_ref[...], m_ref[...], v_ref[...])

    blk = pl.BlockSpec((tr, cols), lambda i: (i, 0))
    return pl.pallas_call(
        body, name=name, grid=(rows // tr,),
        in_specs=[blk] * 4, out_specs=[blk] * 3,
        out_shape=[jax.ShapeDtypeStruct((rows, cols), f32)] * 3,
        compiler_params=_params(("arbitrary",), 32),
    )(*_in_hbm(w, g, m, v))


VEC_NAMES = ("ln_gain", "conv_b", "b_rgate", "b_igate", "lru_lambda", "attn_out_gain", "lru_out_gain", "final_gain")
ROW_RGATE, ROW_IGATE, ROW_VEC, ROW_SINKS = 0, 64, 128, 136
LOSS_LANE = NQ


def _adamw_small(g_rep, g_conv, w, m, v):
    names = list(VEC_NAMES) + ["sinks", "conv_w", "w_rgate", "w_igate"]
    ins = [g_rep, g_conv] + [d[k] for k in names for d in (w, m, v)]

    def body(*refs):
        g_ref, gc_ref = refs[0], refs[1]
        in_refs = refs[2:2 + 3 * len(names)]
        out_refs = refs[2 + 3 * len(names):]

        def update(j, g, at=None):
            w_ref, m_ref, v_ref = in_refs[3 * j:3 * j + 3]
            outs = out_refs[4 * j:4 * j + 4]
            pick = (lambda r: r[...]) if at is None else (lambda r: r[at])
            res = (g,) + _adam_math(pick(w_ref), g, pick(m_ref), pick(v_ref))
            for o_ref, val in zip(outs, res):
                if at is None:
                    o_ref[...] = val
                else:
                    o_ref[at] = val

        for j in range(len(VEC_NAMES)):
            update(j, g_ref[ROW_VEC + j:ROW_VEC + j + 1, :])
        update(len(VEC_NAMES), g_ref[ROW_SINKS:ROW_SINKS + 1, 0:NQ])
        update(len(VEC_NAMES) + 1, gc_ref[...], at=0)
        for gi, row0 in ((len(VEC_NAMES) + 2, ROW_RGATE), (len(VEC_NAMES) + 3, ROW_IGATE)):
            for nb in range(NQ):
                update(gi, g_ref[row0:row0 + HD, HD * nb:HD * nb + HD], at=(0, nb))

    vmem = pl.BlockSpec(memory_space=pltpu.VMEM)
    out_shape = [jax.ShapeDtypeStruct(w[k].shape, f32) for k in names for _ in range(4)]
    outs = pl.pallas_call(
        body, name="adamw_small",
        in_specs=[vmem] * len(ins), out_specs=[vmem] * len(out_shape), out_shape=out_shape,
        compiler_params=pltpu.CompilerParams(vmem_limit_bytes=32 * MIB),
    )(*ins)
    return {k: tuple(outs[4 * j:4 * j + 4]) for j, k in enumerate(names)}


def _pack_small(small, loss):
    gate = lambda g: g.transpose(1, 0, 2).reshape(HD, NQ * HD)
    row_s = jnp.concatenate([small["sinks"], loss[:, LOSS_LANE:128], jnp.zeros((1, D - 128), f32)], axis=1)
    rep = jnp.concatenate([gate(small["w_rgate"]), gate(small["w_igate"])] + [small[k] for k in VEC_NAMES]
                          + [row_s, jnp.zeros((SMALL_ROWS - ROW_SINKS - 1, D), f32)], axis=0)
    conv = small["conv_w"].reshape(CONVW, NDEV, 128).transpose(1, 0, 2)
    conv = jnp.pad(conv, ((0, 0), (0, 8 - CONVW), (0, D - 128)))
    return jnp.concatenate([rep.reshape(NDEV, SMALL_PER, D), conv], axis=1).reshape(NDEV * (SMALL_PER + 8), D)


def kernel(x, ln_gain, w_in, sinks, conv_w, conv_b, w_rgate, b_rgate, w_igate, b_igate, lru_lambda, attn_out_gain, lru_out_gain, w_out, final_gain, loss_target, m_ln_gain, m_w_in, m_sinks, m_conv_w, m_conv_b, m_w_rgate, m_b_rgate, m_w_igate, m_b_igate, m_lru_lambda, m_attn_out_gain, m_lru_out_gain, m_w_out, m_final_gain, v_ln_gain, v_w_in, v_sinks, v_conv_w, v_conv_b, v_w_rgate, v_b_rgate, v_w_igate, v_b_igate, v_lru_lambda, v_attn_out_gain, v_lru_out_gain, v_w_out, v_final_gain):
    w = dict(ln_gain=ln_gain, sinks=sinks, conv_w=conv_w, conv_b=conv_b, w_rgate=w_rgate, b_rgate=b_rgate,
             w_igate=w_igate, b_igate=b_igate, lru_lambda=lru_lambda, attn_out_gain=attn_out_gain,
             lru_out_gain=lru_out_gain, final_gain=final_gain.reshape(1, D))
    m = dict(ln_gain=m_ln_gain, sinks=m_sinks, conv_w=m_conv_w, conv_b=m_conv_b, w_rgate=m_w_rgate,
             b_rgate=m_b_rgate, w_igate=m_w_igate, b_igate=m_b_igate, lru_lambda=m_lru_lambda,
             attn_out_gain=m_attn_out_gain, lru_out_gain=m_lru_out_gain, final_gain=m_final_gain.reshape(1, D))
    v = dict(ln_gain=v_ln_gain, sinks=v_sinks, conv_w=v_conv_w, conv_b=v_conv_b, w_rgate=v_w_rgate,
             b_rgate=v_b_rgate, w_igate=v_w_igate, b_igate=v_b_igate, lru_lambda=v_lru_lambda,
             attn_out_gain=v_attn_out_gain, lru_out_gain=v_lru_out_gain, final_gain=v_final_gain.reshape(1, D))

    conv_blk = jnp.pad(conv_w[0], ((0, 8 - CONVW), (0, 0)))
    wt, cw_all = _all_gather([w_in[0].T, conv_blk], [bf16, f32], "gather_weights")
    conv_full = cw_all.reshape(NDEV, 8, 128)[:, 0:CONVW].transpose(1, 0, 2).reshape(CONVW, LW)

    p = {k: (w[k][0] if k in ("w_rgate", "w_igate") else w[k]) for k in w if k != "conv_w"}
    gx, land_wt, land_wo, land_sm = _sequence_step(x[0], loss_target[0], wt, w_out[0], conv_full, p)

    g_wt = _sum_slots(land_wt, 192, "sum_wt")
    g_wo = _sum_slots(land_wo, 256, "sum_wo")
    g_sm = _sum_slots(land_sm, SMALL_PER + 8, "sum_small")
    (g_rep,) = _all_gather([g_sm[0:SMALL_PER]], [f32], "gather_small")
    g_conv = g_sm[SMALL_PER:SMALL_PER + CONVW, 0:128]

    d_win, m_win, v_win = _adamw(w_in[0].T, g_wt, m_w_in[0].T, v_w_in[0].T, 192, "adamw_w_in")
    g_win, d_win, m_win, v_win = (t.T for t in (g_wt, d_win, m_win, v_win))
    d_wo, m_wo, v_wo = _adamw(w_out[0], g_wo, m_w_out[0], v_w_out[0], 256, "adamw_w_out")
    res = _adamw_small(g_rep, g_conv, w, m, v)
    res["w_in"] = tuple(t[None] for t in (g_win, d_win, m_win, v_win))
    res["w_out"] = tuple(t[None] for t in (g_wo, d_wo, m_wo, v_wo))
    res["final_gain"] = tuple(t.reshape(D) for t in res["final_gain"])

    order = ("ln_gain", "w_in", "sinks", "conv_w", "conv_b", "w_rgate", "b_rgate", "w_igate", "b_igate",
             "lru_lambda", "attn_out_gain", "lru_out_gain", "w_out", "final_gain")
    total_loss = g_rep[ROW_SINKS, LOSS_LANE]
    return (total_loss, gx[None]) + tuple(res[k][i] for i in range(4) for k in order)
```

```python
import jax
import jax.numpy as jnp
from jax import lax
from jax.experimental import pallas as pl
from jax.experimental.pallas import tpu as pltpu

f32 = jnp.float32
bf16 = jnp.bfloat16

D = 1024
HD = 64
NQ = 16
NKV = 4
GROUP = NQ // NKV
KVW = NKV * HD
BLK = 128
ROT = 16
THETA = 500000.0
NEG = -1e30
LW = 1024
NGRP = 4
CONVW = 4
LRU_C = 8.0
NIN = 4608
EPS = 1e-6
NDEV = 8
WT_ROWS = NIN // NDEV
WO_ROWS = 2 * D // NDEV
SMALL_ROWS = 192
SMALL_PER = SMALL_ROWS // NDEV

ADAM_LR = 0.001
ADAM_B1 = 0.9
ADAM_B2 = 0.999
ADAM_EPS = 1e-08
ADAM_WD = 0.01
ADAM_STEP = 10

NT = (((1,), (1,)), ((), ()))
TN = (((0,), (0,)), ((), ()))
MESH = pl.DeviceIdType.MESH
MIB = 1024 * 1024


def _dot(a, b):
    return jnp.dot(a, b, preferred_element_type=f32)


def _dot_nt(a, b):
    return lax.dot_general(a, b, NT, preferred_element_type=f32)


def _dot_tn(a, b):
    return lax.dot_general(a, b, TN, preferred_element_type=f32)


def _params(sem, vmem_mib):
    return pltpu.CompilerParams(dimension_semantics=sem, vmem_limit_bytes=vmem_mib * MIB)


def _sigmoid(x):
    return 0.5 * jnp.tanh(0.5 * x) + 0.5


def _softplus(x):
    return jnp.maximum(x, 0.0) + jnp.log(1.0 + jnp.exp(-jnp.abs(x)))


def _rope_tables(s):
    pos = jnp.arange(s, dtype=f32)
    inv_freq = THETA ** (-jnp.arange(0, ROT, 2, dtype=f32) / ROT)
    ang = pos[:, None] * inv_freq[None, :]
    cs = jnp.concatenate([jnp.cos(ang) - 1.0, jnp.sin(ang)], axis=1)
    d = jnp.arange(128) % HD
    j = jnp.arange(ROT)[:, None]
    pick_c = ((d < ROT) & (j == d % (ROT // 2))).astype(f32)
    pick_sa = ((d >= ROT // 2) & (d < ROT) & (j == d)).astype(f32)
    pick_sb = -((d < ROT // 2) & (j == d + ROT // 2)).astype(f32)
    spread = lambda pick: jnp.dot(cs, pick, precision=lax.Precision.HIGHEST)
    spread_t = lambda pick: jnp.dot(pick.T, cs.T, precision=lax.Precision.HIGHEST)
    tabs = 1.0 + spread(pick_c), spread(pick_sa), spread(pick_sb)
    tabs_t = 1.0 + spread_t(pick_c), spread_t(pick_sa), spread_t(pick_sb)
    return tabs, tabs_t


def _rope(t, c, sa, sb):
    return t * c + pltpu.roll(t, 8, 1) * sa + pltpu.roll(t, 120, 1) * sb


def _unrope_t(dr, c, sa, sb):
    return dr * c + pltpu.roll(dr * sa, 120, 0) + pltpu.roll(dr * sb, 8, 0)


def _place():
    return lax.axis_index("x"), lax.axis_index("y"), lax.axis_index("c")


def _gather_ops(mine_refs, out_refs, send_sems, recv_sems, local_sems):
    n = len(mine_refs)
    x, y, c = _place()
    me, sibling = (x, y, c), (x, y, 1 - c)
    chips = [(1 - x, y), (x, 1 - y), (1 - x, 1 - y)]

    def rows(a, dev):
        m = mine_refs[a].shape[0]
        return out_refs[a].at[pl.ds((4 * dev[0] + 2 * dev[1] + dev[2]) * m, m), :]

    def copy(a, k, block, to, own=False):
        return pltpu.make_async_remote_copy(
            src_ref=mine_refs[a] if own else rows(a, block), dst_ref=rows(a, block),
            send_sem=send_sems.at[a, k], recv_sem=recv_sems.at[a, k], device_id=to, device_id_type=MESH)

    def local(a):
        return pltpu.make_async_copy(mine_refs[a], rows(a, me), local_sems.at[a])

    def first(a):
        return [copy(a, 0, me, sibling, own=True)] + [copy(a, 1 + j, me, (*chip, c), own=True)
                                                      for j, chip in enumerate(chips)]

    def start():
        for a in range(n):
            local(a).start()
            for cp in first(a):
                cp.start()

    def finish():
        for j, chip in enumerate(chips):
            for a in range(n):
                copy(a, 1 + j, (*chip, c), me).wait_recv()
                copy(a, 4 + j, (*chip, c), sibling).start()
        for a in range(n):
            copy(a, 0, sibling, me).wait_recv()
            for j, chip in enumerate(chips):
                copy(a, 4 + j, (*chip, 1 - c), me).wait_recv()
        for a in range(n):
            for cp in first(a) + [copy(a, 4 + j, (*chip, c), sibling) for j, chip in enumerate(chips)]:
                cp.wait_send()
            local(a).wait()

    return start, finish


def _scatter_ops(src_refs, land_refs, send_sems, recv_sems, local_sems):
    n = len(src_refs)
    x, y, c = _place()
    my = 4 * x + 2 * y + c

    def peer(k):
        return x ^ (k >> 2), y ^ ((k >> 1) & 1), c ^ (k & 1)

    def piece(a, dev):
        m = src_refs[a].shape[0] // NDEV
        return src_refs[a].at[pl.ds(dev * m, m), :]

    def local(a):
        return pltpu.make_async_copy(piece(a, my), land_refs[a].at[my], local_sems.at[a])

    def send(a, k):
        px, py, pc = peer(k)
        return pltpu.make_async_remote_copy(
            src_ref=piece(a, 4 * px + 2 * py + pc), dst_ref=land_refs[a].at[my],
            send_sem=send_sems.at[a, k - 1], recv_sem=recv_sems.at[a, k - 1],
            device_id=(px, py, pc), device_id_type=MESH)

    def arrival(a, k):
        px, py, pc = peer(k)
        return pltpu.make_async_remote_copy(
            src_ref=piece(a, my), dst_ref=land_refs[a].at[4 * px + 2 * py + pc],
            send_sem=send_sems.at[a, k - 1], recv_sem=recv_sems.at[a, k - 1],
            device_id=(px, py, pc), device_id_type=MESH)

    def start():
        for a in range(n):
            local(a).start()
        for k in range(1, NDEV):
            for a in range(n):
                send(a, k).start()

    def finish():
        for k in range(1, NDEV):
            for a in range(n):
                send(a, k).wait_send()
        for k in range(1, NDEV):
            for a in range(n):
                arrival(a, k).wait_recv()
        for a in range(n):
            local(a).wait()

    return start, finish


def _in_hbm(*arrays):
    return tuple(pltpu.with_memory_space_constraint(a, pltpu.HBM) for a in arrays)


def _comm_sems(n):
    return [pltpu.SemaphoreType.DMA((n, 7)), pltpu.SemaphoreType.DMA((n, 7)), pltpu.SemaphoreType.DMA((n,))]


HBM = pl.BlockSpec(memory_space=pltpu.HBM)


def _fwd_in(x, ln_gain, wt, tabs, wo_shard, tm):
    s = x.shape[0]
    nt = s // tm
    nc = 512

    def body(x_ref, g_ref, wt_ref, c_ref, sa_ref, sb_ref, wo_ref, h_ref, q_ref, k_ref, v_ref, ga_ref, xl_ref, gl_ref,
             wo_all, wo_stage, send_sems, recv_sems, local_sems):
        i = pl.program_id(0)
        start, finish = _gather_ops([wo_stage], [wo_all], send_sems, recv_sems, local_sems)

        @pl.when(i == 0)
        def _():
            wo_stage[...] = wo_ref[...].astype(bf16)
            start()

        xx = x_ref[...]
        rstd = lax.rsqrt(jnp.mean(xx * xx, axis=-1, keepdims=True) + EPS)
        h = (xx * rstd * g_ref[...]).astype(bf16)
        h_ref[...] = h
        c, sa, sb = c_ref[...], sa_ref[...], sb_ref[...]

        def z_chunk(ci):
            return _dot_nt(h, wt_ref[ci * nc:(ci + 1) * nc, :])

        for ci in range(2):
            z = z_chunk(ci)
            for j in range(nc // 128):
                r = _rope(z[:, 128 * j:128 * j + 128], c, sa, sb) * (HD ** -0.5)
                q_ref[:, ci * nc + 128 * j:ci * nc + 128 * j + 128] = r.astype(bf16)
        z = z_chunk(2)
        for j in range(2):
            k_ref[:, 128 * j:128 * j + 128] = _rope(z[:, 128 * j:128 * j + 128], c, sa, sb).astype(bf16)
        v_ref[...] = z[:, 256:512].astype(bf16)
        for sec, ref in enumerate((ga_ref, xl_ref, gl_ref)):
            for j in range(2):
                ref[:, j * nc:(j + 1) * nc] = z_chunk(3 + 2 * sec + j)

        @pl.when(i == nt - 1)
        def _():
            finish()

    row = lambda w: pl.BlockSpec((tm, w), lambda i: (i, 0))
    full = lambda a: pl.BlockSpec(a.shape, lambda i: (0, 0))
    return pl.pallas_call(
        body, name="fwd_in", grid=(nt,),
        in_specs=[row(D), full(ln_gain), full(wt), row(128), row(128), row(128), full(wo_shard)],
        out_specs=[row(D), row(D), row(KVW), row(KVW), row(D), row(D), row(D), HBM],
        out_shape=[pltpu.HBM((s,D), bf16), pltpu.HBM((s,D), bf16),
                   pltpu.HBM((s,KVW), bf16), pltpu.HBM((s,KVW), bf16),
                   pltpu.HBM((s,D), f32), pltpu.HBM((s,D), f32),
                   pltpu.HBM((s,D), f32), pltpu.HBM((2 * D, D), bf16)],
        scratch_shapes=[pltpu.VMEM((WO_ROWS, D), bf16)] + _comm_sems(1),
        compiler_params=_params(("arbitrary",), 48),
    )(*_in_hbm(x), ln_gain, *_in_hbm(wt), *tabs, wo_shard)


HSUB = 4
SUBW = HSUB * BLK


def _sub_probs(kh, qg, n, sink_row):
    jj = lax.broadcasted_iota(jnp.int32, (BLK, SUBW), 0)
    ii = lax.broadcasted_iota(jnp.int32, (BLK, SUBW), 1) % BLK
    from_prev = jj > ii
    s2 = _dot_nt(kh, qg)
    sc = jnp.where(from_prev, s2[0:BLK] + jnp.where(n > 0, 0.0, NEG), s2[BLK:2 * BLK])
    m = jnp.maximum(jnp.max(sc, axis=0, keepdims=True), sink_row)
    p = jnp.exp(sc - m)
    es = jnp.exp(sink_row - m)
    inv = 1.0 / (jnp.sum(p, axis=0, keepdims=True) + es)
    return from_prev, p * inv, es * inv


def _split(t, from_prev):
    t = t.astype(bf16)
    zero = jnp.zeros_like(t)
    return jnp.concatenate([jnp.where(from_prev, t, zero), jnp.where(from_prev, zero, t)], axis=0)


def _stack_heads(ref, first):
    return jnp.concatenate([ref[:, HD * (first + g):HD * (first + g) + HD] for g in range(HSUB)], axis=0)


def _sink_rows(sinks):
    return jnp.repeat(sinks.reshape(NKV, GROUP), BLK, axis=1)


def _kv_specs():
    prev = pl.BlockSpec((BLK, KVW), lambda n: (jnp.maximum(n - 1, 0), 0))
    cur = pl.BlockSpec((BLK, KVW), lambda n: (n, 0))
    return [prev, cur, prev, cur]


def _attn_fwd(q, k, v, sinks):
    s = q.shape[0]

    def body(sink_ref, q_ref, kp_ref, kc_ref, vp_ref, vc_ref, o_ref):
        n = pl.program_id(0)
        for h in range(NKV):
            hs = slice(HD * h, HD * h + HD)
            kh = jnp.concatenate([kp_ref[:, hs], kc_ref[:, hs]], axis=0)
            vh = jnp.concatenate([vp_ref[:, hs], vc_ref[:, hs]], axis=0)
            for t in range(GROUP // HSUB):
                first = GROUP * h + HSUB * t
                from_prev, pn, _ = _sub_probs(kh, _stack_heads(q_ref, first), n,
                                              sink_ref[h:h + 1, SUBW * t:SUBW * t + SUBW])
                og = _dot_tn(_split(pn, from_prev), vh)
                for g in range(HSUB):
                    o_ref[:, HD * (first + g):HD * (first + g) + HD] = og[BLK * g:BLK * g + BLK]

    return pl.pallas_call(
        body, name="attn_fwd", grid=(s // BLK,),
        in_specs=[pl.BlockSpec((NKV, GROUP * BLK), lambda n: (0, 0)), pl.BlockSpec((BLK, D), lambda n: (n, 0))]
        + _kv_specs(),
        out_specs=pl.BlockSpec((BLK, D), lambda n: (n, 0)),
        out_shape=pltpu.HBM((s,D), f32),
        compiler_params=_params(("arbitrary",), 32),
    )(_sink_rows(sinks), *_in_hbm(q, k, k, v, v))


def _attn_bwd(q, k, v, do, sinks, dwo):
    s = q.shape[0]
    nb = s // BLK

    def body(sink_ref, q_ref, do_ref, kp_ref, kc_ref, vp_ref, vc_ref, dwo_ref, dq_ref, dk_ref, dv_ref, ds_ref,
             land_ref, send_sems, recv_sems, local_sems):
        n = pl.program_id(0)
        start, finish = _scatter_ops([dwo_ref], [land_ref], send_sems, recv_sems, local_sems)

        @pl.when(n == 0)
        def _():
            start()
            dk_ref[...] = jnp.zeros_like(dk_ref)
            dv_ref[...] = jnp.zeros_like(dv_ref)
            ds_ref[...] = jnp.zeros_like(ds_ref)

        prev_rows = pl.ds(pl.multiple_of(jnp.maximum(n - 1, 0) * BLK, BLK), BLK)
        cur_rows = pl.ds(pl.multiple_of(n * BLK, BLK), BLK)
        for h in range(NKV):
            hs = slice(HD * h, HD * h + HD)
            kh = jnp.concatenate([kp_ref[:, hs], kc_ref[:, hs]], axis=0)
            vh = jnp.concatenate([vp_ref[:, hs], vc_ref[:, hs]], axis=0)
            dkh = jnp.zeros((2 * BLK, HD), f32)
            dvh = jnp.zeros((2 * BLK, HD), f32)
            for t in range(GROUP // HSUB):
                first = GROUP * h + HSUB * t
                lanes = slice(SUBW * t, SUBW * t + SUBW)
                qg, dog = _stack_heads(q_ref, first), _stack_heads(do_ref, first)
                from_prev, pn, ps = _sub_probs(kh, qg, n, sink_ref[h:h + 1, lanes])
                dp2 = _dot_nt(vh, dog)
                dp = jnp.where(from_prev, dp2[0:BLK], dp2[BLK:2 * BLK])
                dsum = jnp.sum(pn * dp, axis=0, keepdims=True)
                ds_ref[h:h + 1, lanes] += -ps * dsum
                ds2 = _split(pn * (dp - dsum), from_prev)
                dqg = _dot_tn(ds2, kh)
                for g in range(HSUB):
                    dq_ref[:, HD * (first + g):HD * (first + g) + HD] = dqg[BLK * g:BLK * g + BLK]
                dkh = dkh + _dot(ds2, qg)
                dvh = dvh + _dot(_split(pn, from_prev), dog)
            dk_ref[prev_rows, hs] += dkh[0:BLK]
            dk_ref[cur_rows, hs] += dkh[BLK:2 * BLK]
            dv_ref[prev_rows, hs] += dvh[0:BLK]
            dv_ref[cur_rows, hs] += dvh[BLK:2 * BLK]

        @pl.when(n == nb - 1)
        def _():
            finish()

    blk = pl.BlockSpec((BLK, D), lambda n: (n, 0))
    whole = lambda r, w: pl.BlockSpec((r, w), lambda n: (0, 0))
    return pl.pallas_call(
        body, name="attn_bwd", grid=(nb,),
        in_specs=[whole(NKV, GROUP * BLK), blk, blk] + _kv_specs() + [HBM],
        out_specs=[blk, whole(s, KVW), whole(s, KVW), whole(NKV, GROUP * BLK), HBM],
        out_shape=[pltpu.HBM((s,D), f32), pltpu.HBM((s,KVW), f32),
                   pltpu.HBM((s,KVW), f32), jax.ShapeDtypeStruct((NKV, GROUP * BLK), f32),
                   pltpu.HBM((NDEV, WO_ROWS, D), bf16)],
        scratch_shapes=_comm_sems(1),
        compiler_params=_params(("arbitrary",), 48),
    )(_sink_rows(sinks), *_in_hbm(q, do, k, k, v, v, dwo))


def _band_softmax(s2_ref, ls, prev_offset, sink_row):
    jj = lax.broadcasted_iota(jnp.int32, (BLK, BLK), 0)
    ii = lax.broadcasted_iota(jnp.int32, (BLK, BLK), 1)
    from_prev = jj > ii
    sc = jnp.where(from_prev, s2_ref[0:BLK, ls] + prev_offset, s2_ref[BLK:2 * BLK, ls])
    m = jnp.maximum(jnp.max(sc, axis=0, keepdims=True), sink_row)
    p = jnp.exp(sc - m)
    es = jnp.exp(sink_row - m)
    inv = 1.0 / (jnp.sum(p, axis=0, keepdims=True) + es)
    return from_prev, p * inv, es * inv


def _put_split(dst_ref, ls, t, from_prev):
    t = t.astype(bf16)
    zero = jnp.zeros_like(t)
    dst_ref[0:BLK, ls] = jnp.where(from_prev, t, zero)
    dst_ref[BLK:2 * BLK, ls] = jnp.where(from_prev, zero, t)


def _heads_side_by_side(ref, h):
    return jnp.concatenate([ref[HD * (GROUP * h + g):HD * (GROUP * h + g) + HD, :] for g in range(GROUP)], axis=1)


def _kv_specs_t():
    prev = pl.BlockSpec((KVW, BLK), lambda n: (0, jnp.maximum(n - 1, 0)))
    cur = pl.BlockSpec((KVW, BLK), lambda n: (0, n))
    return [prev, cur, prev, cur]


def _attn_fwd_t(qt, kt, vt, sinks):
    s = qt.shape[1]

    def body(sink_ref, q_ref, kp_ref, kc_ref, vp_ref, vc_ref, o_ref, s2_scr, pn2_scr):
        n = pl.program_id(0)
        off = jnp.where(n > 0, 0.0, NEG)

        def scores(h):
            hs = slice(HD * h, HD * h + HD)
            kh = jnp.concatenate([kp_ref[hs, :], kc_ref[hs, :]], axis=1)
            s2_scr[h % 2] = _dot_tn(kh, _heads_side_by_side(q_ref, h))

        def probs(h):
            for g in range(GROUP):
                ls = slice(BLK * g, BLK * g + BLK)
                from_prev, pn, _ = _band_softmax(s2_scr.at[h % 2], ls, off, sink_ref[h:h + 1, ls])
                _put_split(pn2_scr.at[h % 2], ls, pn, from_prev)

        def outputs(h):
            hs = slice(HD * h, HD * h + HD)
            vh = jnp.concatenate([vp_ref[hs, :], vc_ref[hs, :]], axis=1)
            og = _dot(vh, pn2_scr[h % 2])
            for g in range(GROUP):
                a = GROUP * h + g
                o_ref[HD * a:HD * a + HD, :] = og[:, BLK * g:BLK * g + BLK]

        scores(0)
        for h in range(NKV):
            if h + 1 < NKV:
                scores(h + 1)
            probs(h)
            outputs(h)

    return pl.pallas_call(
        body, name="attn_fwd", grid=(s // BLK,),
        in_specs=[pl.BlockSpec((NKV, GROUP * BLK), lambda n: (0, 0)), pl.BlockSpec((D, BLK), lambda n: (0, n))]
        + _kv_specs_t(),
        out_specs=pl.BlockSpec((D, BLK), lambda n: (0, n)),
        out_shape=pltpu.HBM((D, s), f32),
        scratch_shapes=[pltpu.VMEM((2, 2 * BLK, GROUP * BLK), f32), pltpu.VMEM((2, 2 * BLK, GROUP * BLK), bf16)],
        compiler_params=_params(("arbitrary",), 32),
    )(_sink_rows(sinks), *_in_hbm(qt, kt, kt, vt, vt))


def _attn_bwd_t(qt, kt, vt, dot, sinks, dwo):
    s = qt.shape[1]
    nb = s // BLK

    def body(sink_ref, q_ref, do_ref, kp_ref, kc_ref, vp_ref, vc_ref, dwo_ref, dq_ref, dk_ref, dv_ref, ds_ref,
             land_ref, dk_hold, dv_hold, s2_scr, dp2_scr, pn2_scr, ds2_scr, send_sems, recv_sems, local_sems):
        n = pl.program_id(0)
        start, finish = _scatter_ops([dwo_ref], [land_ref], send_sems, recv_sems, local_sems)

        @pl.when(n == 0)
        def _():
            start()
            dk_hold[...] = jnp.zeros_like(dk_hold)
            dv_hold[...] = jnp.zeros_like(dv_hold)
            ds_ref[...] = jnp.zeros_like(ds_ref)

        @pl.when(n < nb)
        def _():
            off = jnp.where(n > 0, 0.0, NEG)

            def scores(h):
                hs = slice(HD * h, HD * h + HD)
                kh = jnp.concatenate([kp_ref[hs, :], kc_ref[hs, :]], axis=1)
                vh = jnp.concatenate([vp_ref[hs, :], vc_ref[hs, :]], axis=1)
                s2_scr[h % 2] = _dot_tn(kh, _heads_side_by_side(q_ref, h))
                dp2_scr[h % 2] = _dot_tn(vh, _heads_side_by_side(do_ref, h))

            def softmax_bwd(h):
                for g in range(GROUP):
                    ls = slice(BLK * g, BLK * g + BLK)
                    from_prev, pn, ps = _band_softmax(s2_scr.at[h % 2], ls, off, sink_ref[h:h + 1, ls])
                    dp = jnp.where(from_prev, dp2_scr[h % 2, 0:BLK, ls], dp2_scr[h % 2, BLK:2 * BLK, ls])
                    dsum = jnp.sum(pn * dp, axis=0, keepdims=True)
                    ds_ref[h:h + 1, ls] += -ps * dsum
                    _put_split(pn2_scr.at[h % 2], ls, pn, from_prev)
                    _put_split(ds2_scr.at[h % 2], ls, pn * (dp - dsum), from_prev)

            def grads(h):
                hs = slice(HD * h, HD * h + HD)
                kh = jnp.concatenate([kp_ref[hs, :], kc_ref[hs, :]], axis=1)
                dqg = _dot(kh, ds2_scr[h % 2])
                for g in range(GROUP):
                    a = GROUP * h + g
                    dq_ref[HD * a:HD * a + HD, :] = dqg[:, BLK * g:BLK * g + BLK]
                dkh = _dot_nt(_heads_side_by_side(q_ref, h), ds2_scr[h % 2])
                dvh = _dot_nt(_heads_side_by_side(do_ref, h), pn2_scr[h % 2])
                dk_ref[hs, :] = dk_hold[hs, :] + dkh[:, 0:BLK]
                dv_ref[hs, :] = dv_hold[hs, :] + dvh[:, 0:BLK]
                dk_hold[hs, :] = dkh[:, BLK:2 * BLK]
                dv_hold[hs, :] = dvh[:, BLK:2 * BLK]

            scores(0)
            for h in range(NKV):
                if h + 1 < NKV:
                    scores(h + 1)
                softmax_bwd(h)
                grads(h)

        @pl.when(n == nb)
        def _():
            dk_ref[...] = dk_hold[...]
            dv_ref[...] = dv_hold[...]
            finish()

    blk = pl.BlockSpec((D, BLK), lambda n: (0, jnp.minimum(n, nb - 1)))
    late = pl.BlockSpec((KVW, BLK), lambda n: (0, jnp.maximum(n - 1, 0)))
    whole = pl.BlockSpec((NKV, GROUP * BLK), lambda n: (0, 0))
    kv = [pl.BlockSpec((KVW, BLK), lambda n: (0, jnp.clip(n - 1, 0, nb - 1))),
          pl.BlockSpec((KVW, BLK), lambda n: (0, jnp.minimum(n, nb - 1)))]
    return pl.pallas_call(
        body, name="attn_bwd", grid=(nb + 1,),
        in_specs=[whole, blk, blk] + kv + kv + [HBM],
        out_specs=[blk, late, late, whole, HBM],
        out_shape=[pltpu.HBM((D, s), f32), pltpu.HBM((KVW, s), f32), pltpu.HBM((KVW, s), f32),
                   jax.ShapeDtypeStruct((NKV, GROUP * BLK), f32), pltpu.HBM((NDEV, WO_ROWS, D), bf16)],
        scratch_shapes=[pltpu.VMEM((KVW, BLK), f32), pltpu.VMEM((KVW, BLK), f32)]
        + [pltpu.VMEM((2, 2 * BLK, GROUP * BLK), f32)] * 2 + [pltpu.VMEM((2, 2 * BLK, GROUP * BLK), bf16)] * 2
        + _comm_sems(1),
        compiler_params=_params(("arbitrary",), 48),
    )(_sink_rows(sinks), *_in_hbm(qt, dot, kt, kt, vt, vt, dwo))


def _block_diag(w):
    w4 = w.reshape(NGRP, 4, HD, HD)
    eye = jnp.eye(4, dtype=w.dtype)
    return jnp.einsum('gjcd,jk->gjckd', w4, eye).reshape(NGRP, 256, 256).astype(bf16)


def _gate_terms(pr, pi, br, bi, sp):
    r = _sigmoid(pr + br)
    i = _sigmoid(pi + bi)
    la = -LRU_C * r * sp
    a = jnp.exp(la)
    x2 = 2.0 * la
    y = jnp.where(x2 > -0.02, -x2 * (1.0 + x2 * (0.5 + x2 * (1.0 / 6.0))), 1.0 - a * a)
    inv_mult = lax.rsqrt(jnp.maximum(y, 1e-30))
    return r, i, a, y * inv_mult, inv_mult


def _gates(u, wr_ref, wi_ref, br, bi, sp):
    ub = u.astype(bf16)
    pr = jnp.concatenate([_dot(ub[:, 256 * g:256 * g + 256], wr_ref[g]) for g in range(NGRP)], axis=1)
    pi = jnp.concatenate([_dot(ub[:, 256 * g:256 * g + 256], wi_ref[g]) for g in range(NGRP)], axis=1)
    return (ub,) + _gate_terms(pr, pi, br, bi, sp)


def _later(x, before, k):
    if k == 0:
        return x
    row = lax.broadcasted_iota(jnp.int32, before.shape, 0)
    rolled = pltpu.roll(x, k, 0)
    first = jnp.where(row < k, pltpu.roll(before, k, 0), rolled[0:8])
    return jnp.concatenate([first, rolled[8:]], axis=0)


def _earlier(x, after, k):
    if k == 0:
        return x
    n = x.shape[0]
    row = lax.broadcasted_iota(jnp.int32, after.shape, 0)
    rolled = pltpu.roll(x, n - k, 0)
    last = jnp.where(row >= 8 - k, pltpu.roll(after, 8 - k, 0), rolled[n - 8:n])
    return jnp.concatenate([rolled[0:n - 8], last], axis=0)


def _lru_fwd(xl, conv_w, conv_b, wr, wi, br, bi, lam, tm):
    s = xl.shape[0]

    def body(xp_ref, x_ref, cw_ref, cb_ref, wr_ref, wi_ref, br_ref, bi_ref, lam_ref, u_ref, h_ref,
             a_scr, b_scr, hcar):
        t0 = pl.program_id(0)

        @pl.when(t0 == 0)
        def _():
            hcar[...] = jnp.zeros_like(hcar)

        x = x_ref[...]
        before = jnp.where(t0 > 0, xp_ref[...], 0.0)
        u = cb_ref[...] + sum(cw_ref[k:k + 1, :] * _later(x, before, CONVW - 1 - k) for k in range(CONVW))
        u_ref[...] = u
        sp = _softplus(-lam_ref[...])
        _, _, i, a, mult, _ = _gates(u, wr_ref, wi_ref, br_ref[...], bi_ref[...], sp)
        a_scr[...] = a
        b_scr[...] = mult * (i * u)

        def step(t, hc):
            hn = a_scr[pl.ds(t, 1), :] * hc + b_scr[pl.ds(t, 1), :]
            h_ref[pl.ds(t, 1), :] = hn
            return hn

        hcar[...] = lax.fori_loop(0, tm, step, hcar[...], unroll=8)

    row = pl.BlockSpec((tm, LW), lambda i: (i, 0))
    prev8 = pl.BlockSpec((8, LW), lambda i: (jnp.maximum(i * (tm // 8) - 1, 0), 0))
    full = lambda a: pl.BlockSpec(a.shape, lambda i: (0,) * a.ndim)
    return pl.pallas_call(
        body, name="lru_fwd", grid=(s // tm,),
        in_specs=[prev8, row, full(conv_w), full(conv_b), full(wr), full(wi), full(br), full(bi), full(lam)],
        out_specs=[row, row],
        out_shape=[pltpu.HBM((s,LW), f32), pltpu.HBM((s,LW), f32)],
        scratch_shapes=[pltpu.VMEM((tm, LW), f32), pltpu.VMEM((tm, LW), f32), pltpu.VMEM((1, LW), f32)],
        compiler_params=_params(("arbitrary",), 48),
    )(*_in_hbm(xl, xl), conv_w, conv_b, wr, wi, br, bi, lam)


def _fwd_in_lru(x, ln_gain, wt, tabs, wo_shard, conv_w, conv_b, wr, wi, br, bi, lam, tm):
    s = x.shape[0]
    nt = s // tm
    nc = 512

    def body(x_ref, g_ref, wt_ref, c_ref, sa_ref, sb_ref, wo_ref, cw_ref, cb_ref, wr_ref, wi_ref, br_ref, bi_ref,
             lam_ref, h_ref, q_ref, k_ref, v_ref, ga_ref, xl_ref, gl_ref, u_ref, hl_ref, wo_all,
             wo_stage, xl_scr, halo, ub_scr, pr_scr, pi_scr, a_scr, b_scr, hcar, send_sems, recv_sems, local_sems):
        i = pl.program_id(0)
        start, finish = _gather_ops([wo_stage], [wo_all], send_sems, recv_sems, local_sems)

        @pl.when(i == 0)
        def _():
            wo_stage[...] = wo_ref[...].astype(bf16)
            start()
            xl_scr[1] = jnp.zeros((tm, LW), f32)
            halo[...] = jnp.zeros_like(halo)
            hcar[...] = jnp.zeros_like(hcar)

        rows_per = tm // 8
        xp_ref = xl_scr.at[(i + 1) % 2]
        sp = _softplus(-lam_ref[...])
        br, bi = br_ref[...], bi_ref[...]

        def lru_conv():
            xp = xp_ref[...]
            u = cb_ref[...] + sum(cw_ref[k:k + 1, :] * _later(xp, halo[...], CONVW - 1 - k) for k in range(CONVW))
            halo[...] = xp[tm - 8:tm, :]
            u_ref[...] = u
            ub_scr[...] = u.astype(bf16)

        def lru_gate_matmuls():
            for g in range(NGRP):
                gs = slice(256 * g, 256 * g + 256)
                pr_scr[:, gs] = _dot(ub_scr[:, gs], wr_ref[g])
                pi_scr[:, gs] = _dot(ub_scr[:, gs], wi_ref[g])

        def lru_terms(piece):
            rows = slice(rows_per * piece, rows_per * piece + rows_per)
            _, ig, a, mult, _ = _gate_terms(pr_scr[rows, :], pi_scr[rows, :], br, bi, sp)
            a_scr[rows, :] = a
            b_scr[rows, :] = mult * (ig * u_ref[rows, :])

        def lru_scan(piece, hc):
            for t in range(rows_per * piece, rows_per * piece + rows_per):
                hc = a_scr[t:t + 1, :] * hc + b_scr[t:t + 1, :]
                hl_ref[t:t + 1, :] = hc
            return hc

        def lru_piece(ci, hc):
            if ci == 0:
                lru_conv()
            elif ci == 1:
                lru_gate_matmuls()
            elif ci == 2:
                lru_terms(0)
                lru_terms(1)
            else:
                hc = lru_scan(ci - 3, hc)
                lru_terms(ci - 1)
            return hc

        xx = x_ref[...]
        rstd = lax.rsqrt(jnp.mean(xx * xx, axis=-1, keepdims=True) + EPS)
        h_ref[...] = (xx * rstd * g_ref[...]).astype(bf16)
        c, sa, sb = c_ref[...], sa_ref[...], sb_ref[...]
        hc = jnp.where(i >= 2, hcar[...], 0.0)

        def z_chunk(ci):
            return _dot_nt(h_ref[...], wt_ref[ci * nc:(ci + 1) * nc, :])

        for ci in range(2):
            z = z_chunk(ci)
            hc = lru_piece(ci, hc)
            for j in range(nc // 128):
                r = _rope(z[:, 128 * j:128 * j + 128], c, sa, sb) * (HD ** -0.5)
                q_ref[:, ci * nc + 128 * j:ci * nc + 128 * j + 128] = r.astype(bf16)
        z = z_chunk(2)
        hc = lru_piece(2, hc)
        for j in range(2):
            k_ref[:, 128 * j:128 * j + 128] = _rope(z[:, 128 * j:128 * j + 128], c, sa, sb).astype(bf16)
        v_ref[...] = z[:, 256:512].astype(bf16)
        for sec, ref in enumerate((ga_ref, xl_ref, gl_ref)):
            for j in range(2):
                z = z_chunk(3 + 2 * sec + j)
                hc = lru_piece(3 + 2 * sec + j, hc)
                ref[:, j * nc:(j + 1) * nc] = z
                if sec == 1:
                    xl_scr[i % 2, :, j * nc:(j + 1) * nc] = z
        hcar[...] = lru_scan(7, lru_scan(6, hc))

        @pl.when(i == nt)
        def _():
            finish()

    cur = lambda w: pl.BlockSpec((tm, w), lambda i: (jnp.minimum(i, nt - 1), 0))
    prev = pl.BlockSpec((tm, LW), lambda i: (jnp.maximum(i - 1, 0), 0))
    full = lambda a: pl.BlockSpec(a.shape, lambda i: (0,) * a.ndim)
    big = lambda w, dt: pltpu.HBM((s, w), dt)
    return pl.pallas_call(
        body, name="fwd_in_lru", grid=(nt + 1,),
        in_specs=[cur(D), full(ln_gain), full(wt), cur(128), cur(128), cur(128), full(wo_shard), full(conv_w),
                  full(conv_b), full(wr), full(wi), full(br), full(bi), full(lam)],
        out_specs=[cur(D), cur(D), cur(KVW), cur(KVW), cur(D), cur(D), cur(D), prev, prev, HBM],
        out_shape=[big(D, bf16), big(D, bf16), big(KVW, bf16), big(KVW, bf16), big(D, f32), big(D, f32), big(D, f32),
                   big(LW, f32), big(LW, f32), pltpu.HBM((2 * D, D), bf16)],
        scratch_shapes=[pltpu.VMEM((WO_ROWS, D), bf16), pltpu.VMEM((2, tm, LW), f32), pltpu.VMEM((8, LW), f32),
                        pltpu.VMEM((tm, LW), bf16)] + [pltpu.VMEM((tm, LW), f32)] * 4 + [pltpu.VMEM((1, LW), f32)]
        + _comm_sems(1),
        compiler_params=_params(("arbitrary",), 56),
    )(*_in_hbm(x), ln_gain, *_in_hbm(wt), *tabs, wo_shard, conv_w, conv_b, wr, wi, br, bi, lam)


def _fwd_pipeline(x, ln_gain, wt, tabs, wo_shard, conv_w, conv_b, wr, wi, br, bi, lam, tm):
    s = x.shape[0]
    nt = s // tm
    nc = 512
    pieces = 8
    rows_per = tm // pieces

    def body(x0_ref, xn_ref, g_ref, wt_ref, c_ref, sa_ref, sb_ref, wo_ref, cw_ref, cb_ref, wr_ref, wi_ref, br_ref,
             bi_ref, lam_ref, h_ref, q_ref, k_ref, v_ref, ga_ref, xl_ref, gl_ref, u_ref, hl_ref, wo_all,
             wo_stage, hb, xl_scr, halo, u_scr, ub_scr, pr_scr, pi_scr, a_scr, b_scr, hcar,
             send_sems, recv_sems, local_sems):
        i = pl.program_id(0)
        start, finish = _gather_ops([wo_stage], [wo_all], send_sems, recv_sems, local_sems)
        gain = g_ref[...]

        def normed(xx):
            rstd = lax.rsqrt(jnp.mean(xx * xx, axis=-1, keepdims=True) + EPS)
            return (xx * rstd * gain).astype(bf16)

        @pl.when(i == 0)
        def _():
            wo_stage[...] = wo_ref[...].astype(bf16)
            start()
            hb[0] = normed(x0_ref[...])
            xl_scr[1] = jnp.zeros((tm, LW), f32)
            u_scr[0] = jnp.zeros((tm, LW), f32)
            ub_scr[0] = jnp.zeros((tm, LW), bf16)
            halo[...] = jnp.zeros_like(halo)
            hcar[...] = jnp.zeros_like(hcar)

        cur, nxt = 0, 1

        sp = _softplus(-lam_ref[...])
        br, bi = br_ref[...], bi_ref[...]
        c, sa, sb = c_ref[...], sa_ref[...], sb_ref[...]
        piece_rows = lambda p: slice(rows_per * p, rows_per * p + rows_per)

        def project(ci):
            z = _dot_nt(hb[cur], wt_ref[ci * nc:(ci + 1) * nc, :])
            if ci < 2:
                for j in range(nc // 128):
                    r = _rope(z[:, 128 * j:128 * j + 128], c, sa, sb) * (HD ** -0.5)
                    q_ref[:, ci * nc + 128 * j:ci * nc + 128 * j + 128] = r.astype(bf16)
            elif ci == 2:
                for j in range(2):
                    k_ref[:, 128 * j:128 * j + 128] = _rope(z[:, 128 * j:128 * j + 128], c, sa, sb).astype(bf16)
                v_ref[...] = z[:, 256:512].astype(bf16)
            else:
                sec, j = divmod(ci - 3, 2)
                (ga_ref, xl_ref, gl_ref)[sec][:, j * nc:(j + 1) * nc] = z
                if sec == 1:
                    xl_scr[cur, :, j * nc:(j + 1) * nc] = z

        def gate_matmuls():
            for g in range(NGRP):
                gs = slice(256 * g, 256 * g + 256)
                pr_scr[:, gs] = _dot(ub_scr[cur, :, gs], wr_ref[g])
                pi_scr[:, gs] = _dot(ub_scr[cur, :, gs], wi_ref[g])

        def gate_terms(p):
            rows = piece_rows(p)
            _, ig, a, mult, _ = _gate_terms(pr_scr[rows, :], pi_scr[rows, :], br, bi, sp)
            a_scr[rows, :] = a
            b_scr[rows, :] = mult * (ig * u_scr[cur, rows, :])

        def scan(p, hc):
            for t in range(rows_per * p, rows_per * p + rows_per):
                hc = a_scr[t:t + 1, :] * hc + b_scr[t:t + 1, :]
                hl_ref[t:t + 1, :] = hc
            return hc

        def conv(p):
            rows = piece_rows(p)
            xp = xl_scr[nxt, rows, :]
            before = halo[...] if p == 0 else xl_scr[nxt, rows_per * p - 8:rows_per * p, :]
            u = cb_ref[...] + sum(cw_ref[k:k + 1, :] * _later(xp, before, CONVW - 1 - k) for k in range(CONVW))
            u_scr[nxt, rows, :] = u
            ub_scr[nxt, rows, :] = u.astype(bf16)

        def norm(p):
            hb[nxt, piece_rows(p), :] = normed(xn_ref[piece_rows(p), :])

        def run():
            h_ref[...] = hb[cur]
            gate_matmuls()
            hc = jnp.where(i >= 3, hcar[...], 0.0)
            for ci in range(NIN // nc):
                project(ci)
                if ci < pieces:
                    conv(ci)
                    norm(ci)
                if ci >= 1:
                    gate_terms(ci - 1)
                if ci >= 2:
                    hc = scan(ci - 2, hc)
            hcar[...] = scan(pieces - 1, hc)
            halo[...] = xl_scr[nxt, tm - 8:tm, :]

            @pl.when(i <= nt)
            def _():
                u_ref[...] = u_scr[nxt]

        for parity in range(2):
            cur, nxt = parity, 1 - parity
            pl.when(i % 2 == parity)(run)

        @pl.when(i == nt + 1)
        def _():
            finish()

    at = lambda w, off: pl.BlockSpec((tm, w), lambda i: (jnp.clip(i + off, 0, nt - 1), 0))
    full = lambda a: pl.BlockSpec(a.shape, lambda i: (0,) * a.ndim)
    big = lambda w, dt: pltpu.HBM((s, w), dt)
    ring = lambda dt: pltpu.VMEM((2, tm, LW), dt)
    tile = pltpu.VMEM((tm, LW), f32)
    return pl.pallas_call(
        body, name="fwd_pipeline", grid=(nt + 2,),
        in_specs=[pl.BlockSpec((tm, D), lambda i: (0, 0)), at(D, 1), full(ln_gain), full(wt), at(128, 0), at(128, 0),
                  at(128, 0), full(wo_shard), full(conv_w), full(conv_b), full(wr), full(wi), full(br), full(bi),
                  full(lam)],
        out_specs=[at(D, 0), at(D, 0), at(KVW, 0), at(KVW, 0), at(D, 0), at(D, 0), at(D, 0), at(LW, -1), at(LW, -2),
                   HBM],
        out_shape=[big(D, bf16), big(D, bf16), big(KVW, bf16), big(KVW, bf16), big(D, f32), big(D, f32), big(D, f32),
                   big(LW, f32), big(LW, f32), pltpu.HBM((2 * D, D), bf16)],
        scratch_shapes=[pltpu.VMEM((WO_ROWS, D), bf16), ring(bf16), ring(f32), pltpu.VMEM((8, LW), f32), ring(f32),
                        ring(bf16), tile, tile, tile, tile, pltpu.VMEM((1, LW), f32)] + _comm_sems(1),
        compiler_params=_params(("arbitrary",), 56),
    )(*_in_hbm(x, x), ln_gain, *_in_hbm(wt), *tabs, wo_shard, conv_w, conv_b, wr, wi, br, bi, lam)


def _fwd_fused(x, ln_gain, wt, tabs, wo_shard, conv_w, conv_b, wr, wi, br, bi, lam, tm):
    s = x.shape[0]
    nt = s // tm
    nc = 512
    pieces = 8
    rows_per = tm // pieces
    later_chunks = (0, 1, 2, 3, 4, 7, 8)

    def body(x0_ref, xn_ref, g_ref, wt_ref, c_ref, sa_ref, sb_ref, wo_ref, cw_ref, cb_ref, wr_ref, wi_ref, br_ref,
             bi_ref, lam_ref, h_ref, q_ref, k_ref, v_ref, ga_ref, xl_ref, gl_ref, u_ref, hl_ref, r_ref, ig_ref, a_ref,
             im_ref, wo_all, wo_stage, hb, halo, ub_scr, pr_scr, pi_scr, b_scr, hcar,
             send_sems, recv_sems, local_sems):
        i = pl.program_id(0)
        start, finish = _gather_ops([wo_stage], [wo_all], send_sems, recv_sems, local_sems)
        gain = g_ref[...]

        def normed(xx):
            rstd = lax.rsqrt(jnp.mean(xx * xx, axis=-1, keepdims=True) + EPS)
            return (xx * rstd * gain).astype(bf16)

        @pl.when(i == 0)
        def _():
            wo_stage[...] = wo_ref[...].astype(bf16)
            start()
            hb[0] = normed(x0_ref[...])
            halo[...] = jnp.zeros_like(halo)
            hcar[...] = jnp.zeros_like(hcar)

        cur, nxt = i % 2, (i + 1) % 2
        sp = _softplus(-lam_ref[...])
        br, bi = br_ref[...], bi_ref[...]
        c, sa, sb = c_ref[...], sa_ref[...], sb_ref[...]
        piece_rows = lambda p: slice(rows_per * p, rows_per * p + rows_per)

        def project(ci):
            z = _dot_nt(hb[cur], wt_ref[ci * nc:(ci + 1) * nc, :])
            if ci < 2:
                for j in range(nc // 128):
                    r = _rope(z[:, 128 * j:128 * j + 128], c, sa, sb) * (HD ** -0.5)
                    q_ref[ci * nc + 128 * j:ci * nc + 128 * j + 128, :] = r.astype(bf16).T
            elif ci == 2:
                for j in range(2):
                    js = slice(128 * j, 128 * j + 128)
                    k_ref[js, :] = _rope(z[:, js], c, sa, sb).astype(bf16).T
                    v_ref[js, :] = z[:, KVW + 128 * j:KVW + 128 * j + 128].astype(bf16).T
            else:
                sec, j = divmod(ci - 3, 2)
                (ga_ref, xl_ref, gl_ref)[sec][:, j * nc:(j + 1) * nc] = z

        def gate_terms(p):
            rows = piece_rows(p)
            r, ig, a, mult, inv_mult = _gate_terms(pr_scr[rows, :], pi_scr[rows, :], br, bi, sp)
            r_ref[rows, :] = r
            ig_ref[rows, :] = ig
            a_ref[rows, :] = a
            im_ref[rows, :] = inv_mult
            b_scr[rows, :] = mult * (ig * u_ref[rows, :])

        def scan(p, hc):
            for t in range(rows_per * p, rows_per * p + rows_per):
                hc = a_ref[t:t + 1, :] * hc + b_scr[t:t + 1, :]
                hl_ref[t:t + 1, :] = hc
            return hc

        def norm_next(p):
            hb[nxt, piece_rows(p), :] = normed(xn_ref[piece_rows(p), :])

        h_ref[...] = hb[cur]
        project(5)
        project(6)
        xl = xl_ref[...]
        u = cb_ref[...] + sum(cw_ref[k:k + 1, :] * _later(xl, halo[...], CONVW - 1 - k) for k in range(CONVW))
        halo[...] = xl[tm - 8:tm, :]
        u_ref[...] = u
        ub_scr[...] = u.astype(bf16)
        for g in range(NGRP):
            gs = slice(256 * g, 256 * g + 256)
            pr_scr[:, gs] = _dot(ub_scr[:, gs], wr_ref[g])
            pi_scr[:, gs] = _dot(ub_scr[:, gs], wi_ref[g])
        hc = hcar[...]
        gate_terms(0)
        for slot, ci in enumerate(later_chunks):
            project(ci)
            norm_next(slot)
            gate_terms(slot + 1)
            hc = scan(slot, hc)
        norm_next(pieces - 1)
        hcar[...] = scan(pieces - 1, hc)

        @pl.when(i == nt - 1)
        def _():
            finish()

    row = lambda w: pl.BlockSpec((tm, w), lambda i: (i, 0))
    col = lambda w: pl.BlockSpec((w, tm), lambda i: (0, i))
    full = lambda a: pl.BlockSpec(a.shape, lambda i: (0,) * a.ndim)
    big = lambda w, dt: pltpu.HBM((s, w), dt)
    tile = pltpu.VMEM((tm, LW), f32)
    return pl.pallas_call(
        body, name="fwd_fused", grid=(nt,),
        in_specs=[pl.BlockSpec((tm, D), lambda i: (0, 0)), pl.BlockSpec((tm, D), lambda i: (jnp.minimum(i + 1, nt - 1), 0)),
                  full(ln_gain), full(wt), row(128), row(128), row(128), full(wo_shard), full(conv_w), full(conv_b),
                  full(wr), full(wi), full(br), full(bi), full(lam)],
        out_specs=[row(D), col(D), col(KVW), col(KVW), row(D), row(D), row(D)] + [row(LW)] * 6 + [HBM],
        out_shape=[big(D, bf16), pltpu.HBM((D, s), bf16), pltpu.HBM((KVW, s), bf16), pltpu.HBM((KVW, s), bf16),
                   big(D, f32), big(D, f32), big(D, f32)] + [big(LW, f32)] * 6 + [pltpu.HBM((2 * D, D), bf16)],
        scratch_shapes=[pltpu.VMEM((WO_ROWS, D), bf16), pltpu.VMEM((2, tm, D), bf16), pltpu.VMEM((8, LW), f32),
                        pltpu.VMEM((tm, LW), bf16), tile, tile, tile, pltpu.VMEM((1, LW), f32)] + _comm_sems(1),
        compiler_params=_params(("arbitrary",), 56),
    )(*_in_hbm(x, x), ln_gain, *_in_hbm(wt), *tabs, wo_shard, conv_w, conv_b, wr, wi, br, bi, lam)


def _lru_bwd(u, hl, dhl, xl, r, ig, a, im, conv_w, wr, wi, lam, tm):
    s = u.shape[0]
    nt = s // tm
    pieces = 8
    rows_per = tm // pieces

    def body(u_ref, h_ref, hp_ref, dh_ref, x_ref, xp_ref, r_ref, ig_ref, a_ref, im_ref, cw_ref, wr_ref, wi_ref,
             lam_ref, dxl_ref, dwr_ref, dwi_ref, dbr_ref, dbi_ref, dlam_ref, dcb_ref, dcw_ref,
             l_scr, du_scr, dpr_scr, dpi_scr, lcar, dunext):
        t0 = pl.program_id(0)
        tile = nt - 1 - t0

        @pl.when(t0 == 0)
        def _():
            lcar[...] = jnp.zeros_like(lcar)
            dunext[...] = jnp.zeros_like(dunext)
            for ref in (dwr_ref, dwi_ref, dbr_ref, dbi_ref, dlam_ref, dcb_ref, dcw_ref):
                ref[...] = jnp.zeros_like(ref)

        lam = lam_ref[...]
        sp = _softplus(-lam)
        hp = jnp.where(tile > 0, hp_ref[...], 0.0)

        def scan(p, c):
            for t in range(rows_per * p + rows_per - 1, rows_per * p - 1, -1):
                lt = dh_ref[t:t + 1, :] + c
                l_scr[t:t + 1, :] = lt
                c = a_ref[t:t + 1, :] * lt
            return c

        def terms(p, sums):
            rows = slice(rows_per * p, rows_per * p + rows_per)
            lt, u, r, i, a, inv_mult = l_scr[rows, :], u_ref[rows, :], r_ref[rows, :], ig_ref[rows, :], \
                a_ref[rows, :], im_ref[rows, :]
            before = hp if p == 0 else h_ref[rows_per * p - 8:rows_per * p, :]
            hprev = _later(h_ref[rows, :], before, 1)
            x2 = -2.0 * LRU_C * r * sp
            mult = jnp.where(x2 > -0.02, -x2 * (1.0 + x2 * (0.5 + x2 * (1.0 / 6.0))), 1.0 - a * a) * inv_mult
            da = lt * hprev
            dmult = lt * (i * u)
            di = lt * mult * u
            du_scr[rows, :] = lt * mult * i
            dla = da * a - dmult * (a * a) * inv_mult
            dr = dla * (-LRU_C * sp)
            dpr = dr * r * (1.0 - r)
            dpi = di * i * (1.0 - i)
            dpr_scr[rows, :] = dpr.astype(bf16)
            dpi_scr[rows, :] = dpi.astype(bf16)
            col = lambda t: jnp.sum(t, axis=0, keepdims=True)
            return sums[0] + col(dla * (-LRU_C * r)), sums[1] + col(dpr), sums[2] + col(dpi)

        sums = (jnp.zeros((1, LW), f32),) * 3
        c = scan(pieces - 1, lcar[...])
        for p in range(pieces - 1, -1, -1):
            if p > 0:
                c = scan(p - 1, c)
            sums = terms(p, sums)
        lcar[...] = c
        dlam_ref[...] += sums[0]
        dbr_ref[...] += sums[1]
        dbi_ref[...] += sums[2]

        ub = u_ref[...].astype(bf16)
        dug = []
        for g in range(NGRP):
            gs = slice(256 * g, 256 * g + 256)
            dwr_ref[g] += _dot_tn(ub[:, gs], dpr_scr[:, gs])
            dwi_ref[g] += _dot_tn(ub[:, gs], dpi_scr[:, gs])
            dug.append(_dot_nt(dpr_scr[:, gs], wr_ref[g]) + _dot_nt(dpi_scr[:, gs], wi_ref[g]))
        du = du_scr[...] + jnp.concatenate(dug, axis=1)

        dcb_ref[...] += jnp.sum(du, axis=0, keepdims=True)
        x = x_ref[...]
        before = jnp.where(tile > 0, xp_ref[...], 0.0)
        for k in range(CONVW):
            dcw_ref[k:k + 1, :] += jnp.sum(du * _later(x, before, CONVW - 1 - k), axis=0, keepdims=True)
        after = dunext[...]
        dxl = sum(cw_ref[k:k + 1, :] * _earlier(du, after, CONVW - 1 - k) for k in range(CONVW))
        dxl_ref[...] = dxl.astype(bf16)
        dunext[...] = du[0:8, :]

        @pl.when(t0 == nt - 1)
        def _():
            dlam_ref[...] = dlam_ref[...] * (-_sigmoid(-lam))

    rev = lambda i: (nt - 1 - i, 0)
    row = pl.BlockSpec((tm, LW), rev)
    prev8 = pl.BlockSpec((8, LW), lambda i: (jnp.maximum((nt - 1 - i) * (tm // 8) - 1, 0), 0))
    full = lambda a: pl.BlockSpec(a.shape, lambda i: (0,) * a.ndim)
    vec = pl.BlockSpec((1, LW), lambda i: (0, 0))
    bd = pl.BlockSpec((NGRP, 256, 256), lambda i: (0, 0, 0))
    return pl.pallas_call(
        body, name="lru_bwd", grid=(nt,),
        in_specs=[row, row, prev8, row, row, prev8, row, row, row, row, full(conv_w), full(wr), full(wi), full(lam)],
        out_specs=[row, bd, bd, vec, vec, vec, vec, pl.BlockSpec((CONVW, LW), lambda i: (0, 0))],
        out_shape=[pltpu.HBM((s,LW), bf16),
                   jax.ShapeDtypeStruct((NGRP, 256, 256), f32), jax.ShapeDtypeStruct((NGRP, 256, 256), f32),
                   jax.ShapeDtypeStruct((1, LW), f32), jax.ShapeDtypeStruct((1, LW), f32),
                   jax.ShapeDtypeStruct((1, LW), f32), jax.ShapeDtypeStruct((1, LW), f32),
                   jax.ShapeDtypeStruct((CONVW, LW), f32)],
        scratch_shapes=[pltpu.VMEM((tm, LW), f32), pltpu.VMEM((tm, LW), f32), pltpu.VMEM((tm, LW), bf16),
                        pltpu.VMEM((tm, LW), bf16), pltpu.VMEM((1, LW), f32), pltpu.VMEM((8, LW), f32)],
        compiler_params=_params(("arbitrary",), 56),
    )(*_in_hbm(u, hl, hl, dhl, xl, xl, r, ig, a, im), conv_w, wr, wi, lam)


def _gated_norm(t, gate, gain):
    sg = _sigmoid(gate)
    silu = gate * sg
    p = t * silu
    rstd = lax.rsqrt(jnp.mean(p * p, axis=-1, keepdims=True) + EPS)
    ph = p * rstd
    return sg, silu, rstd, ph, ph * gain


def _gated_norm_bwd(dy, t, gate, gain, sg, silu, rstd, ph):
    w = dy * gain
    dp = rstd * (w - ph * jnp.mean(w * ph, axis=-1, keepdims=True))
    dgate = dp * t * (sg * (1.0 + gate * (1.0 - sg)))
    return jnp.sum(dy * ph, axis=0, keepdims=True), dp * silu, dgate


def _out_fwd_bwd(x, tgt, o, ga, hl, gl, again, lgain, fgain, wo, tm):
    s = x.shape[0]
    nt = s // tm

    def body(x_ref, t_ref, o_ref, ga_ref, hl_ref, gl_ref, ag_ref, lg_ref, fg_ref, wo_ref,
             dx2_ref, do_ref, dga_ref, dhl_ref, dgl_ref, dwo_ref, gfg_ref, gag_ref, glg_ref, loss_ref, acc):
        i = pl.program_id(0)

        @pl.when(i == 0)
        def _():
            acc[...] = jnp.zeros_like(acc)
            for ref in (gfg_ref, gag_ref, glg_ref, loss_ref):
                ref[...] = jnp.zeros_like(ref)

        oo = jnp.concatenate([o_ref[128 * j:128 * j + 128, :].T for j in range(D // 128)], axis=1)
        gga, hh, ggl = ga_ref[...], hl_ref[...], gl_ref[...]
        ag, lg, fg = ag_ref[...], lg_ref[...], fg_ref[...]
        sga, silua, ra, pah, ya = _gated_norm(oo, gga, ag)
        sgl, silul, rl, plh, yl = _gated_norm(hh, ggl, lg)
        yab, ylb = ya.astype(bf16), yl.astype(bf16)
        y = _dot(yab, wo_ref[0:D, :]) + _dot(ylb, wo_ref[D:2 * D, :])
        x2 = x_ref[...] + y
        r2 = lax.rsqrt(jnp.mean(x2 * x2, axis=-1, keepdims=True) + EPS)
        x2h = x2 * r2
        err = x2h * fg - t_ref[...]
        loss_ref[...] += 0.5 * jnp.sum(jnp.sum(err * err, axis=-1, keepdims=True) * (1.0 / D))
        dout = err * (1.0 / D)
        gfg_ref[...] += jnp.sum(dout * x2h, axis=0, keepdims=True)
        w = dout * fg
        dx2 = r2 * (w - x2h * jnp.mean(w * x2h, axis=-1, keepdims=True))
        dx2_ref[...] = dx2
        dyb = dx2.astype(bf16)
        acc[0:D, :] += _dot_tn(yab, dyb)
        acc[D:2 * D, :] += _dot_tn(ylb, dyb)
        dya = _dot_nt(dyb, wo_ref[0:D, :])
        dyl = _dot_nt(dyb, wo_ref[D:2 * D, :])
        gag, do, dga = _gated_norm_bwd(dya, oo, gga, ag, sga, silua, ra, pah)
        glg, dhl, dgl = _gated_norm_bwd(dyl, hh, ggl, lg, sgl, silul, rl, plh)
        gag_ref[...] += gag
        glg_ref[...] += glg
        dob = do.astype(bf16)
        for j in range(D // 128):
            do_ref[128 * j:128 * j + 128, :] = dob[:, 128 * j:128 * j + 128].T
        dga_ref[...] = dga.astype(bf16)
        dhl_ref[...] = dhl
        dgl_ref[...] = dgl.astype(bf16)

        @pl.when(i == nt - 1)
        def _():
            dwo_ref[...] = acc[...].astype(bf16)

    row = pl.BlockSpec((tm, D), lambda i: (i, 0))
    col = pl.BlockSpec((D, tm), lambda i: (0, i))
    vec = pl.BlockSpec((1, D), lambda i: (0, 0))
    mat = pl.BlockSpec((2 * D, D), lambda i: (0, 0))
    return pl.pallas_call(
        body, name="out_fwd_bwd", grid=(nt,),
        in_specs=[row, row, col, row, row, row] + [vec] * 3 + [mat],
        out_specs=[row, col, row, row, row] + [mat, vec, vec, vec, pl.BlockSpec((1, 128), lambda i: (0, 0))],
        out_shape=[pltpu.HBM((s,D), f32), pltpu.HBM((D, s), bf16),
                   pltpu.HBM((s,D), bf16), pltpu.HBM((s,D), f32),
                   pltpu.HBM((s,D), bf16), pltpu.HBM((2 * D, D), bf16),
                   jax.ShapeDtypeStruct((1, D), f32), jax.ShapeDtypeStruct((1, D), f32),
                   jax.ShapeDtypeStruct((1, D), f32), jax.ShapeDtypeStruct((1, 128), f32)],
        scratch_shapes=[pltpu.VMEM((2 * D, D), f32)],
        compiler_params=_params(("arbitrary",), 56),
    )(*_in_hbm(x, tgt, o, ga, hl, gl), again, lgain, fgain, *_in_hbm(wo))


def _bwd_in(x, dx2, dq, dk, dv, dga, dxl, dgl, ln_gain, wt, tabs, tm):
    s = x.shape[0]

    def body(x_ref, dx2_ref, dq_ref, dk_ref, dv_ref, dga_ref, dxl_ref, dgl_ref, g_ref, wt_ref,
             c_ref, sa_ref, sb_ref, gx_ref, gln_ref, dzt_ref):
        @pl.when(pl.program_id(0) == 0)
        def _():
            gln_ref[...] = jnp.zeros_like(gln_ref)

        c, sa, sb = c_ref[...], sa_ref[...], sb_ref[...]
        for j in range(D // 128):
            js = slice(128 * j, 128 * j + 128)
            dzt_ref[js, :] = (_unrope_t(dq_ref[js, :], c, sa, sb) * (HD ** -0.5)).astype(bf16)
        for j in range(KVW // 128):
            js = slice(128 * j, 128 * j + 128)
            dzt_ref[D + 128 * j:D + 128 * j + 128, :] = _unrope_t(dk_ref[js, :], c, sa, sb).astype(bf16)
        dzt_ref[D + KVW:D + 2 * KVW, :] = dv_ref[...].astype(bf16)
        first = D + 2 * KVW
        dh = _dot_tn(dzt_ref[0:512, :], wt_ref[0:512, :])
        for ci in range(1, first // 512):
            dh = dh + _dot_tn(dzt_ref[512 * ci:512 * ci + 512, :], wt_ref[512 * ci:512 * ci + 512, :])
        for sec, ref in enumerate((dga_ref, dxl_ref, dgl_ref)):
            for j in range(D // 512):
                rows = slice(first + D * sec + 512 * j, first + D * sec + 512 * j + 512)
                dh = dh + _dot(ref[:, 512 * j:512 * j + 512], wt_ref[rows, :])
            for j in range(D // 128):
                dzt_ref[first + D * sec + 128 * j:first + D * sec + 128 * j + 128, :] = ref[:, 128 * j:128 * j + 128].T
        xx = x_ref[...]
        rstd = lax.rsqrt(jnp.mean(xx * xx, axis=-1, keepdims=True) + EPS)
        xh = xx * rstd
        gln_ref[...] += jnp.sum(dh * xh, axis=0, keepdims=True)
        w = dh * g_ref[...]
        gx_ref[...] = dx2_ref[...] + rstd * (w - xh * jnp.mean(w * xh, axis=-1, keepdims=True))

    row = lambda w: pl.BlockSpec((tm, w), lambda i: (i, 0))
    col = lambda w: pl.BlockSpec((w, tm), lambda i: (0, i))
    full = lambda a: pl.BlockSpec(a.shape, lambda i: (0, 0))
    return pl.pallas_call(
        body, name="bwd_in", grid=(s // tm,),
        in_specs=[row(D), row(D), col(D), col(KVW), col(KVW), row(D), row(D), row(D), full(ln_gain), full(wt),
                  col(128), col(128), col(128)],
        out_specs=[row(D), pl.BlockSpec((1, D), lambda i: (0, 0)), col(NIN)],
        out_shape=[pltpu.HBM((s,D), f32), jax.ShapeDtypeStruct((1, D), f32),
                   pltpu.HBM((NIN, s), bf16)],
        compiler_params=_params(("arbitrary",), 56),
    )(*_in_hbm(x, dx2, dq, dk, dv, dga, dxl, dgl), ln_gain, *_in_hbm(wt), *tabs)


WT_TERMS = 5


def _dwt_scatter(dzt, h, small, tm):
    s = h.shape[0]
    nk = s // tm
    srows = small.shape[0] // NDEV
    last = NDEV - 1

    def body(order_ref, dz_ref, h_ref, sm_ref, lwt_ref, lsm_ref, acc, stage, given, send_sems, recv_sems, local_sem,
             sm_send, sm_recv, sm_local):
        j, k = pl.program_id(0), pl.program_id(1)
        x, y, c = _place()
        sibling = (x, y, 1 - c)
        chips = [(1 - x, 1 - y), (1 - x, y), (x, 1 - y)]
        sm_start, sm_finish = _scatter_ops([sm_ref], [lsm_ref], sm_send, sm_recv, sm_local)

        def send(step):
            if step == last - 1:
                dst, to = lwt_ref.at[1], sibling
            elif step % 2 == 0:
                dst, to = given.at[step // 2], sibling
            else:
                dst, to = lwt_ref.at[2 + step // 2], (*chips[step // 2], c)
            return pltpu.make_async_remote_copy(
                src_ref=stage.at[step % 2], dst_ref=dst, send_sem=send_sems.at[step], recv_sem=recv_sems.at[step],
                device_id=to, device_id_type=MESH)

        def keep():
            return pltpu.make_async_copy(stage.at[last % 2], lwt_ref.at[0], local_sem)

        @pl.when((j == 0) & (k == 0))
        def _():
            sm_start()

        @pl.when(k == 0)
        def _():
            acc[...] = jnp.zeros_like(acc)

        acc[...] += _dot(dz_ref[...], h_ref[...])

        for step in range(NDEV):
            @pl.when((k == nk - 1) & (j == step))
            def _(step=step):
                if step >= 2:
                    send(step - 2).wait_send()
                if step % 2 == 1 and step < last:
                    send(step - 1).wait_recv()
                    stage[step % 2] = (acc[...] + given[step // 2].astype(f32)).astype(bf16)
                else:
                    stage[step % 2] = acc[...].astype(bf16)
                if step < last:
                    send(step).start()
                else:
                    keep().start()
                    send(last - 1).wait_send()
                    for peer_step in (1, 3, 5, last - 1):
                        send(peer_step).wait_recv()
                    keep().wait()
                    sm_finish()

    x, y, c = _place()
    dest = lambda cx, cy, cc: 4 * cx + 2 * cy + cc
    order = jnp.stack([dest(1 - x, 1 - y, 1 - c), dest(1 - x, 1 - y, c), dest(1 - x, y, 1 - c), dest(1 - x, y, c),
                       dest(x, 1 - y, 1 - c), dest(x, 1 - y, c), dest(x, y, 1 - c), dest(x, y, c)])
    return pl.pallas_call(
        body, name="dwt_scatter",
        grid_spec=pltpu.PrefetchScalarGridSpec(
            num_scalar_prefetch=1, grid=(NDEV, nk),
            in_specs=[pl.BlockSpec((WT_ROWS, tm), lambda j, k, order: (order[j], k)),
                      pl.BlockSpec((tm, D), lambda j, k, order: (k, 0)), HBM],
            out_specs=[HBM, HBM],
            scratch_shapes=[pltpu.VMEM((WT_ROWS, D), f32), pltpu.VMEM((2, WT_ROWS, D), bf16),
                            pltpu.VMEM((3, WT_ROWS, D), bf16),
                            pltpu.SemaphoreType.DMA((last,)), pltpu.SemaphoreType.DMA((last,)),
                            pltpu.SemaphoreType.DMA(())] + _comm_sems(1)),
        out_shape=[pltpu.HBM((WT_TERMS, WT_ROWS, D), bf16), pltpu.HBM((NDEV, srows, D), f32)],
        compiler_params=_params(("arbitrary", "arbitrary"), 32),
    )(order, *_in_hbm(dzt, h, small))


def _diag_blocks(bd):
    eye = jnp.eye(4, dtype=bd.dtype)
    return jnp.einsum('gjckd,jk->gjcd', bd.reshape(NGRP, 4, HD, 4, HD), eye).reshape(NQ, HD, HD)


def _sequence_step(x, tgt, wt, wo_shard, conv_w, p):
    s = x.shape[0]
    tm = min(256, s)
    tabs, tabs_t = _rope_tables(s)
    wr, wi = _block_diag(p["w_rgate"]), _block_diag(p["w_igate"])
    sinks = p["sinks"].reshape(NQ)
    h, qt, kt, vt, ga, xl, gl, u, hl, r, ig, a, im, wo = _fwd_fused(
        x, p["ln_gain"], wt, tabs, wo_shard, conv_w, p["conv_b"], wr, wi, p["b_rgate"], p["b_igate"],
        p["lru_lambda"], tm)
    ot = _attn_fwd_t(qt, kt, vt, sinks)
    dx2, dot, dga, dhl, dgl, dwo, g_fg, g_ag, g_lg, loss = _out_fwd_bwd(
        x, tgt, ot, ga, hl, gl, p["attn_out_gain"], p["lru_out_gain"], p["final_gain"], wo, tm)
    dqt, dkt, dvt, dsink, land_wo = _attn_bwd_t(qt, kt, vt, dot, sinks, dwo)
    dxl, dwr, dwi, dbr, dbi, dlam, dcb, dcw = _lru_bwd(u, hl, dhl, xl, r, ig, a, im, conv_w, wr, wi, p["lru_lambda"], tm)
    gx, g_ln, dzt = _bwd_in(x, dx2, dqt, dkt, dvt, dga, dxl, dgl, p["ln_gain"], wt, tabs_t, tm)
    small = dict(ln_gain=g_ln, sinks=dsink.reshape(NQ, BLK).sum(axis=1)[None], conv_w=dcw, conv_b=dcb,
                 w_rgate=_diag_blocks(dwr), b_rgate=dbr, w_igate=_diag_blocks(dwi), b_igate=dbi, lru_lambda=dlam,
                 attn_out_gain=g_ag, lru_out_gain=g_lg, final_gain=g_fg)
    land_wt, land_sm = _dwt_scatter(dzt, h, _pack_small(small, loss), min(512, s))
    return gx, land_wt, land_wo, land_sm


def _all_gather(srcs, out_dtypes, name):
    n = len(srcs)
    cast = [a.dtype != dt for a, dt in zip(srcs, out_dtypes)]

    def body(*refs):
        src_refs, out_refs = refs[:n], refs[n:2 * n]
        stage_refs = list(refs[2 * n:2 * n + sum(cast)])
        mine_refs = []
        for a in range(n):
            if cast[a]:
                st = stage_refs.pop(0)
                st[...] = src_refs[a][...].astype(out_dtypes[a])
                mine_refs.append(st)
            else:
                mine_refs.append(src_refs[a])
        start, finish = _gather_ops(mine_refs, out_refs, *refs[-3:])
        start()
        finish()

    vmem = pl.BlockSpec(memory_space=pltpu.VMEM)
    return pl.pallas_call(
        body, name=name,
        in_specs=[vmem] * n, out_specs=[HBM] * n,
        out_shape=[pltpu.HBM((NDEV * a.shape[0], a.shape[1]), dt) for a, dt in zip(srcs, out_dtypes)],
        scratch_shapes=[pltpu.VMEM(a.shape, dt) for a, dt, cst in zip(srcs, out_dtypes, cast) if cst] + _comm_sems(n),
        compiler_params=pltpu.CompilerParams(vmem_limit_bytes=32 * MIB),
    )(*srcs)


def _sum_slots(land, tr, name):
    terms, rows, cols = land.shape

    def body(l_ref, o_ref):
        acc = l_ref[0].astype(f32)
        for d in range(1, terms):
            acc = acc + l_ref[d].astype(f32)
        o_ref[...] = acc

    return pl.pallas_call(
        body, name=name, grid=(rows // tr,),
        in_specs=[pl.BlockSpec((terms, tr, cols), lambda i: (0, i, 0))],
        out_specs=pl.BlockSpec((tr, cols), lambda i: (i, 0)),
        out_shape=jax.ShapeDtypeStruct((rows, cols), f32),
        compiler_params=_params(("arbitrary",), 32),
    )(*_in_hbm(land))


def _adam_math(w, g, m, v):
    m2 = ADAM_B1 * m + (1.0 - ADAM_B1) * g
    v2 = ADAM_B2 * v + (1.0 - ADAM_B2) * (g * g)
    m_hat = m2 / (1.0 - ADAM_B1 ** ADAM_STEP)
    v_hat = v2 / (1.0 - ADAM_B2 ** ADAM_STEP)
    delta = -ADAM_LR * (m_hat / (jnp.sqrt(v_hat) + ADAM_EPS) + ADAM_WD * w)
    return delta, m2, v2


def _adamw(w, g, m, v, tr, name):
    rows, cols = w.shape

    def body(w_ref, g_ref, m_ref, v_ref, d_ref, m2_ref, v2_ref):
        d_ref[...], m2_ref[...], v2_ref[...] = _adam_math(w_ref[...], g_ref[...], m_ref[...], v_ref[...])

    blk = pl.BlockSpec((tr, cols), lambda i: (i, 0))
    return pl.pallas_call(
        body, name=name, grid=(rows // tr,),
        in_specs=[blk] * 4, out_specs=[blk] * 3,
        out_shape=[jax.ShapeDtypeStruct((rows, cols), f32)] * 3,
        compiler_params=_params(("arbitrary",), 32),
    )(*_in_hbm(w, g, m, v))


VEC_NAMES = ("ln_gain", "conv_b", "b_rgate", "b_igate", "lru_lambda", "attn_out_gain", "lru_out_gain", "final_gain")
ROW_RGATE, ROW_IGATE, ROW_VEC, ROW_SINKS = 0, 64, 128, 136
LOSS_LANE = NQ


def _adamw_small(g_rep, g_conv, w, m, v):
    names = list(VEC_NAMES) + ["sinks", "conv_w", "w_rgate", "w_igate"]
    ins = [g_rep, g_conv] + [d[k] for k in names for d in (w, m, v)]

    def body(*refs):
        g_ref, gc_ref = refs[0], refs[1]
        in_refs = refs[2:2 + 3 * len(names)]
        out_refs = refs[2 + 3 * len(names):]

        def update(j, g, at=None):
            w_ref, m_ref, v_ref = in_refs[3 * j:3 * j + 3]
            outs = out_refs[4 * j:4 * j + 4]
            pick = (lambda r: r[...]) if at is None else (lambda r: r[at])
            res = (g,) + _adam_math(pick(w_ref), g, pick(m_ref), pick(v_ref))
            for o_ref, val in zip(outs, res):
                if at is None:
                    o_ref[...] = val
                else:
                    o_ref[at] = val

        for j in range(len(VEC_NAMES)):
            update(j, g_ref[ROW_VEC + j:ROW_VEC + j + 1, :])
        update(len(VEC_NAMES), g_ref[ROW_SINKS:ROW_SINKS + 1, 0:NQ])
        update(len(VEC_NAMES) + 1, gc_ref[...], at=0)
        for gi, row0 in ((len(VEC_NAMES) + 2, ROW_RGATE), (len(VEC_NAMES) + 3, ROW_IGATE)):
            for nb in range(NQ):
                update(gi, g_ref[row0:row0 + HD, HD * nb:HD * nb + HD], at=(0, nb))

    vmem = pl.BlockSpec(memory_space=pltpu.VMEM)
    out_shape = [jax.ShapeDtypeStruct(w[k].shape, f32) for k in names for _ in range(4)]
    outs = pl.pallas_call(
        body, name="adamw_small",
        in_specs=[vmem] * len(ins), out_specs=[vmem] * len(out_shape), out_shape=out_shape,
        compiler_params=pltpu.CompilerParams(vmem_limit_bytes=32 * MIB),
    )(*ins)
    return {k: tuple(outs[4 * j:4 * j + 4]) for j, k in enumerate(names)}


def _pack_small(small, loss):
    gate = lambda g: g.transpose(1, 0, 2).reshape(HD, NQ * HD)
    row_s = jnp.concatenate([small["sinks"], loss[:, LOSS_LANE:128], jnp.zeros((1, D - 128), f32)], axis=1)
    rep = jnp.concatenate([gate(small["w_rgate"]), gate(small["w_igate"])] + [small[k] for k in VEC_NAMES]
                          + [row_s, jnp.zeros((SMALL_ROWS - ROW_SINKS - 1, D), f32)], axis=0)
    conv = small["conv_w"].reshape(CONVW, NDEV, 128).transpose(1, 0, 2)
    conv = jnp.pad(conv, ((0, 0), (0, 8 - CONVW), (0, D - 128)))
    return jnp.concatenate([rep.reshape(NDEV, SMALL_PER, D), conv], axis=1).reshape(NDEV * (SMALL_PER + 8), D)


def kernel(x, ln_gain, w_in, sinks, conv_w, conv_b, w_rgate, b_rgate, w_igate, b_igate, lru_lambda, attn_out_gain, lru_out_gain, w_out, final_gain, loss_target, m_ln_gain, m_w_in, m_sinks, m_conv_w, m_conv_b, m_w_rgate, m_b_rgate, m_w_igate, m_b_igate, m_lru_lambda, m_attn_out_gain, m_lru_out_gain, m_w_out, m_final_gain, v_ln_gain, v_w_in, v_sinks, v_conv_w, v_conv_b, v_w_rgate, v_b_rgate, v_w_igate, v_b_igate, v_lru_lambda, v_attn_out_gain, v_lru_out_gain, v_w_out, v_final_gain):
    w = dict(ln_gain=ln_gain, sinks=sinks, conv_w=conv_w, conv_b=conv_b, w_rgate=w_rgate, b_rgate=b_rgate,
             w_igate=w_igate, b_igate=b_igate, lru_lambda=lru_lambda, attn_out_gain=attn_out_gain,
             lru_out_gain=lru_out_gain, final_gain=final_gain.reshape(1, D))
    m = dict(ln_gain=m_ln_gain, sinks=m_sinks, conv_w=m_conv_w, conv_b=m_conv_b, w_rgate=m_w_rgate,
             b_rgate=m_b_rgate, w_igate=m_w_igate, b_igate=m_b_igate, lru_lambda=m_lru_lambda,
             attn_out_gain=m_attn_out_gain, lru_out_gain=m_lru_out_gain, final_gain=m_final_gain.reshape(1, D))
    v = dict(ln_gain=v_ln_gain, sinks=v_sinks, conv_w=v_conv_w, conv_b=v_conv_b, w_rgate=v_w_rgate,
             b_rgate=v_b_rgate, w_igate=v_w_igate, b_igate=v_b_igate, lru_lambda=v_lru_lambda,
             attn_out_gain=v_attn_out_gain, lru_out_gain=v_lru_out_gain, final_gain=v_final_gain.reshape(1, D))

    conv_blk = jnp.pad(conv_w[0], ((0, 8 - CONVW), (0, 0)))
    wt, cw_all = _all_gather([w_in[0].T, conv_blk], [bf16, f32], "gather_weights")
    conv_full = cw_all.reshape(NDEV, 8, 128)[:, 0:CONVW].transpose(1, 0, 2).reshape(CONVW, LW)

    p = {k: (w[k][0] if k in ("w_rgate", "w_igate") else w[k]) for k in w if k != "conv_w"}
    gx, land_wt, land_wo, land_sm = _sequence_step(x[0], loss_target[0], wt, w_out[0], conv_full, p)

    g_wt = _sum_slots(land_wt, 192, "sum_wt")
    g_wo = _sum_slots(land_wo, 256, "sum_wo")
    g_sm = _sum_slots(land_sm, SMALL_PER + 8, "sum_small")
    (g_rep,) = _all_gather([g_sm[0:SMALL_PER]], [f32], "gather_small")
    g_conv = g_sm[SMALL_PER:SMALL_PER + CONVW, 0:128]

    d_win, m_win, v_win = _adamw(w_in[0].T, g_wt, m_w_in[0].T, v_w_in[0].T, 192, "adamw_w_in")
    g_win, d_win, m_win, v_win = (t.T for t in (g_wt, d_win, m_win, v_win))
    d_wo, m_wo, v_wo = _adamw(w_out[0], g_wo, m_w_out[0], v_w_out[0], 256, "adamw_w_out")
    res = _adamw_small(g_rep, g_conv, w, m, v)
    res["w_in"] = tuple(t[None] for t in (g_win, d_win, m_win, v_win))
    res["w_out"] = tuple(t[None] for t in (g_wo, d_wo, m_wo, v_wo))
    res["final_gain"] = tuple(t.reshape(D) for t in res["final_gain"])

    order = ("ln_gain", "w_in", "sinks", "conv_w", "conv_b", "w_rgate", "b_rgate", "w_igate", "b_igate",
             "lru_lambda", "attn_out_gain", "lru_out_gain", "w_out", "final_gain")
    total_loss = g_rep[ROW_SINKS, LOSS_LANE]
    return (total_loss, gx[None]) + tuple(res[k][i] for i in range(4) for k in order)
```

```python
import jax
import jax.numpy as jnp
from jax import lax
from jax.experimental import pallas as pl
from jax.experimental.pallas import tpu as pltpu

f32 = jnp.float32
bf16 = jnp.bfloat16

D = 1024
HD = 64
NQ = 16
NKV = 4
GROUP = NQ // NKV
KVW = NKV * HD
BLK = 128
ROT = 16
THETA = 500000.0
NEG = -1e30
LW = 1024
NGRP = 4
CONVW = 4
LRU_C = 8.0
NIN = 4608
EPS = 1e-6
NDEV = 8
WT_ROWS = NIN // NDEV
WO_ROWS = 2 * D // NDEV
SMALL_ROWS = 192
SMALL_PER = SMALL_ROWS // NDEV

ADAM_LR = 0.001
ADAM_B1 = 0.9
ADAM_B2 = 0.999
ADAM_EPS = 1e-08
ADAM_WD = 0.01
ADAM_STEP = 10

NT = (((1,), (1,)), ((), ()))
TN = (((0,), (0,)), ((), ()))
MESH = pl.DeviceIdType.MESH
MIB = 1024 * 1024


def _dot(a, b):
    return jnp.dot(a, b, preferred_element_type=f32)


def _dot_nt(a, b):
    return lax.dot_general(a, b, NT, preferred_element_type=f32)


def _dot_tn(a, b):
    return lax.dot_general(a, b, TN, preferred_element_type=f32)


def _params(sem, vmem_mib):
    return pltpu.CompilerParams(dimension_semantics=sem, vmem_limit_bytes=vmem_mib * MIB)


def _sigmoid(x):
    return 0.5 * jnp.tanh(0.5 * x) + 0.5


def _softplus(x):
    return jnp.maximum(x, 0.0) + jnp.log(1.0 + jnp.exp(-jnp.abs(x)))


def _rope_tables(s):
    pos = jnp.arange(s, dtype=f32)
    inv_freq = THETA ** (-jnp.arange(0, ROT, 2, dtype=f32) / ROT)
    ang = pos[:, None] * inv_freq[None, :]
    cs = jnp.concatenate([jnp.cos(ang) - 1.0, jnp.sin(ang)], axis=1)
    d = jnp.arange(128) % HD
    j = jnp.arange(ROT)[:, None]
    pick_c = ((d < ROT) & (j == d % (ROT // 2))).astype(f32)
    pick_sa = ((d >= ROT // 2) & (d < ROT) & (j == d)).astype(f32)
    pick_sb = -((d < ROT // 2) & (j == d + ROT // 2)).astype(f32)
    spread = lambda pick: jnp.dot(cs, pick, precision=lax.Precision.HIGHEST)
    spread_t = lambda pick: jnp.dot(pick.T, cs.T, precision=lax.Precision.HIGHEST)
    tabs = 1.0 + spread(pick_c), spread(pick_sa), spread(pick_sb)
    tabs_t = 1.0 + spread_t(pick_c), spread_t(pick_sa), spread_t(pick_sb)
    return tabs, tabs_t


def _rope(t, c, sa, sb):
    return t * c + pltpu.roll(t, 8, 1) * sa + pltpu.roll(t, 120, 1) * sb


def _unrope_t(dr, c, sa, sb):
    return dr * c + pltpu.roll(dr * sa, 120, 0) + pltpu.roll(dr * sb, 8, 0)


def _place():
    return lax.axis_index("x"), lax.axis_index("y"), lax.axis_index("c")


def _gather_ops(mine_refs, out_refs, send_sems, recv_sems, local_sems):
    n = len(mine_refs)
    x, y, c = _place()
    me, sibling = (x, y, c), (x, y, 1 - c)
    chips = [(1 - x, y), (x, 1 - y), (1 - x, 1 - y)]

    def rows(a, dev):
        m = mine_refs[a].shape[0]
        return out_refs[a].at[pl.ds((4 * dev[0] + 2 * dev[1] + dev[2]) * m, m), :]

    def copy(a, k, block, to, own=False):
        return pltpu.make_async_remote_copy(
            src_ref=mine_refs[a] if own else rows(a, block), dst_ref=rows(a, block),
            send_sem=send_sems.at[a, k], recv_sem=recv_sems.at[a, k], device_id=to, device_id_type=MESH)

    def local(a):
        return pltpu.make_async_copy(mine_refs[a], rows(a, me), local_sems.at[a])

    def first(a):
        return [copy(a, 0, me, sibling, own=True)] + [copy(a, 1 + j, me, (*chip, c), own=True)
                                                      for j, chip in enumerate(chips)]

    def start():
        for a in range(n):
            local(a).start()
            for cp in first(a):
                cp.start()

    def finish():
        for j, chip in enumerate(chips):
            for a in range(n):
                copy(a, 1 + j, (*chip, c), me).wait_recv()
                copy(a, 4 + j, (*chip, c), sibling).start()
        for a in range(n):
            copy(a, 0, sibling, me).wait_recv()
            for j, chip in enumerate(chips):
                copy(a, 4 + j, (*chip, 1 - c), me).wait_recv()
        for a in range(n):
            for cp in first(a) + [copy(a, 4 + j, (*chip, c), sibling) for j, chip in enumerate(chips)]:
                cp.wait_send()
            local(a).wait()

    return start, finish


def _scatter_ops(src_refs, land_refs, send_sems, recv_sems, local_sems):
    n = len(src_refs)
    x, y, c = _place()
    my = 4 * x + 2 * y + c

    def peer(k):
        return x ^ (k >> 2), y ^ ((k >> 1) & 1), c ^ (k & 1)

    def piece(a, dev):
        m = src_refs[a].shape[0] // NDEV
        return src_refs[a].at[pl.ds(dev * m, m), :]

    def local(a):
        return pltpu.make_async_copy(piece(a, my), land_refs[a].at[my], local_sems.at[a])

    def send(a, k):
        px, py, pc = peer(k)
        return pltpu.make_async_remote_copy(
            src_ref=piece(a, 4 * px + 2 * py + pc), dst_ref=land_refs[a].at[my],
            send_sem=send_sems.at[a, k - 1], recv_sem=recv_sems.at[a, k - 1],
            device_id=(px, py, pc), device_id_type=MESH)

    def arrival(a, k):
        px, py, pc = peer(k)
        return pltpu.make_async_remote_copy(
            src_ref=piece(a, my), dst_ref=land_refs[a].at[4 * px + 2 * py + pc],
            send_sem=send_sems.at[a, k - 1], recv_sem=recv_sems.at[a, k - 1],
            device_id=(px, py, pc), device_id_type=MESH)

    def start():
        for a in range(n):
            local(a).start()
        for k in range(1, NDEV):
            for a in range(n):
                send(a, k).start()

    def finish():
        for k in range(1, NDEV):
            for a in range(n):
                send(a, k).wait_send()
        for k in range(1, NDEV):
            for a in range(n):
                arrival(a, k).wait_recv()
        for a in range(n):
            local(a).wait()

    return start, finish


def _in_hbm(*arrays):
    return tuple(pltpu.with_memory_space_constraint(a, pltpu.HBM) for a in arrays)


def _comm_sems(n):
    return [pltpu.SemaphoreType.DMA((n, 7)), pltpu.SemaphoreType.DMA((n, 7)), pltpu.SemaphoreType.DMA((n,))]


HBM = pl.BlockSpec(memory_space=pltpu.HBM)


def _sink_rows(sinks):
    return jnp.repeat(sinks.reshape(NKV, GROUP), BLK, axis=1)


def _band_softmax(s2_ref, ls, prev_offset, sink_row):
    jj = lax.broadcasted_iota(jnp.int32, (BLK, BLK), 0)
    ii = lax.broadcasted_iota(jnp.int32, (BLK, BLK), 1)
    from_prev = jj > ii
    sc = jnp.where(from_prev, s2_ref[0:BLK, ls] + prev_offset, s2_ref[BLK:2 * BLK, ls])
    m = jnp.maximum(jnp.max(sc, axis=0, keepdims=True), sink_row)
    p = jnp.exp(sc - m)
    es = jnp.exp(sink_row - m)
    inv = 1.0 / (jnp.sum(p, axis=0, keepdims=True) + es)
    return from_prev, p * inv, es * inv


def _put_split(dst_ref, ls, t, from_prev):
    t = t.astype(bf16)
    zero = jnp.zeros_like(t)
    dst_ref[0:BLK, ls] = jnp.where(from_prev, t, zero)
    dst_ref[BLK:2 * BLK, ls] = jnp.where(from_prev, zero, t)


def _heads_side_by_side(ref, h):
    return jnp.concatenate([ref[HD * (GROUP * h + g):HD * (GROUP * h + g) + HD, :] for g in range(GROUP)], axis=1)


def _kv_specs_t():
    prev = pl.BlockSpec((KVW, BLK), lambda n: (0, jnp.maximum(n - 1, 0)))
    cur = pl.BlockSpec((KVW, BLK), lambda n: (0, n))
    return [prev, cur, prev, cur]


def _attn_fwd_t(qt, kt, vt, sinks):
    s = qt.shape[1]

    def body(sink_ref, q_ref, kp_ref, kc_ref, vp_ref, vc_ref, o_ref, s2_scr, pn2_scr):
        n = pl.program_id(0)
        off = jnp.where(n > 0, 0.0, NEG)

        def scores(h):
            hs = slice(HD * h, HD * h + HD)
            kh = jnp.concatenate([kp_ref[hs, :], kc_ref[hs, :]], axis=1)
            s2_scr[h % 2] = _dot_tn(kh, _heads_side_by_side(q_ref, h))

        def probs(h):
            for g in range(GROUP):
                ls = slice(BLK * g, BLK * g + BLK)
                from_prev, pn, _ = _band_softmax(s2_scr.at[h % 2], ls, off, sink_ref[h:h + 1, ls])
                _put_split(pn2_scr.at[h % 2], ls, pn, from_prev)

        def outputs(h):
            hs = slice(HD * h, HD * h + HD)
            vh = jnp.concatenate([vp_ref[hs, :], vc_ref[hs, :]], axis=1)
            og = _dot(vh, pn2_scr[h % 2])
            for g in range(GROUP):
                a = GROUP * h + g
                o_ref[HD * a:HD * a + HD, :] = og[:, BLK * g:BLK * g + BLK]

        scores(0)
        for h in range(NKV):
            if h + 1 < NKV:
                scores(h + 1)
            probs(h)
            outputs(h)

    return pl.pallas_call(
        body, name="attn_fwd", grid=(s // BLK,),
        in_specs=[pl.BlockSpec((NKV, GROUP * BLK), lambda n: (0, 0)), pl.BlockSpec((D, BLK), lambda n: (0, n))]
        + _kv_specs_t(),
        out_specs=pl.BlockSpec((D, BLK), lambda n: (0, n)),
        out_shape=pltpu.HBM((D, s), f32),
        scratch_shapes=[pltpu.VMEM((2, 2 * BLK, GROUP * BLK), f32), pltpu.VMEM((2, 2 * BLK, GROUP * BLK), bf16)],
        compiler_params=_params(("arbitrary",), 32),
    )(_sink_rows(sinks), *_in_hbm(qt, kt, kt, vt, vt))


def _attn_bwd_t(qt, kt, vt, dot, sinks, dwo):
    s = qt.shape[1]
    nb = s // BLK

    def body(sink_ref, q_ref, do_ref, kp_ref, kc_ref, vp_ref, vc_ref, dwo_ref, dq_ref, dk_ref, dv_ref, ds_ref,
             land_ref, dk_hold, dv_hold, s2_scr, dp2_scr, pn2_scr, ds2_scr, send_sems, recv_sems, local_sems):
        n = pl.program_id(0)
        start, finish = _scatter_ops([dwo_ref], [land_ref], send_sems, recv_sems, local_sems)

        @pl.when(n == 0)
        def _():
            start()
            dk_hold[...] = jnp.zeros_like(dk_hold)
            dv_hold[...] = jnp.zeros_like(dv_hold)
            ds_ref[...] = jnp.zeros_like(ds_ref)

        @pl.when(n < nb)
        def _():
            off = jnp.where(n > 0, 0.0, NEG)

            def scores(h):
                hs = slice(HD * h, HD * h + HD)
                kh = jnp.concatenate([kp_ref[hs, :], kc_ref[hs, :]], axis=1)
                vh = jnp.concatenate([vp_ref[hs, :], vc_ref[hs, :]], axis=1)
                s2_scr[h % 2] = _dot_tn(kh, _heads_side_by_side(q_ref, h))
                dp2_scr[h % 2] = _dot_tn(vh, _heads_side_by_side(do_ref, h))

            def softmax_bwd(h):
                for g in range(GROUP):
                    ls = slice(BLK * g, BLK * g + BLK)
                    from_prev, pn, ps = _band_softmax(s2_scr.at[h % 2], ls, off, sink_ref[h:h + 1, ls])
                    dp = jnp.where(from_prev, dp2_scr[h % 2, 0:BLK, ls], dp2_scr[h % 2, BLK:2 * BLK, ls])
                    dsum = jnp.sum(pn * dp, axis=0, keepdims=True)
                    ds_ref[h:h + 1, ls] += -ps * dsum
                    _put_split(pn2_scr.at[h % 2], ls, pn, from_prev)
                    _put_split(ds2_scr.at[h % 2], ls, pn * (dp - dsum), from_prev)

            def grads(h):
                hs = slice(HD * h, HD * h + HD)
                kh = jnp.concatenate([kp_ref[hs, :], kc_ref[hs, :]], axis=1)
                dqg = _dot(kh, ds2_scr[h % 2])
                for g in range(GROUP):
                    a = GROUP * h + g
                    dq_ref[HD * a:HD * a + HD, :] = dqg[:, BLK * g:BLK * g + BLK]
                dkh = _dot_nt(_heads_side_by_side(q_ref, h), ds2_scr[h % 2])
                dvh = _dot_nt(_heads_side_by_side(do_ref, h), pn2_scr[h % 2])
                dk_ref[hs, :] = dk_hold[hs, :] + dkh[:, 0:BLK]
                dv_ref[hs, :] = dv_hold[hs, :] + dvh[:, 0:BLK]
                dk_hold[hs, :] = dkh[:, BLK:2 * BLK]
                dv_hold[hs, :] = dvh[:, BLK:2 * BLK]

            scores(0)
            for h in range(NKV):
                if h + 1 < NKV:
                    scores(h + 1)
                softmax_bwd(h)
                grads(h)

        @pl.when(n == nb)
        def _():
            dk_ref[...] = dk_hold[...]
            dv_ref[...] = dv_hold[...]
            finish()

    blk = pl.BlockSpec((D, BLK), lambda n: (0, jnp.minimum(n, nb - 1)))
    late = pl.BlockSpec((KVW, BLK), lambda n: (0, jnp.maximum(n - 1, 0)))
    whole = pl.BlockSpec((NKV, GROUP * BLK), lambda n: (0, 0))
    kv = [pl.BlockSpec((KVW, BLK), lambda n: (0, jnp.clip(n - 1, 0, nb - 1))),
          pl.BlockSpec((KVW, BLK), lambda n: (0, jnp.minimum(n, nb - 1)))]
    return pl.pallas_call(
        body, name="attn_bwd", grid=(nb + 1,),
        in_specs=[whole, blk, blk] + kv + kv + [HBM],
        out_specs=[blk, late, late, whole, HBM],
        out_shape=[pltpu.HBM((D, s), f32), pltpu.HBM((KVW, s), f32), pltpu.HBM((KVW, s), f32),
                   jax.ShapeDtypeStruct((NKV, GROUP * BLK), f32), pltpu.HBM((NDEV, WO_ROWS, D), bf16)],
        scratch_shapes=[pltpu.VMEM((KVW, BLK), f32), pltpu.VMEM((KVW, BLK), f32)]
        + [pltpu.VMEM((2, 2 * BLK, GROUP * BLK), f32)] * 2 + [pltpu.VMEM((2, 2 * BLK, GROUP * BLK), bf16)] * 2
        + _comm_sems(1),
        compiler_params=_params(("arbitrary",), 48),
    )(_sink_rows(sinks), *_in_hbm(qt, dot, kt, kt, vt, vt, dwo))


def _block_diag(w):
    w4 = w.reshape(NGRP, 4, HD, HD)
    eye = jnp.eye(4, dtype=w.dtype)
    return jnp.einsum('gjcd,jk->gjckd', w4, eye).reshape(NGRP, 256, 256).astype(bf16)


def _gate_terms(pr, pi, br, bi, sp):
    r = _sigmoid(pr + br)
    i = _sigmoid(pi + bi)
    la = -LRU_C * r * sp
    a = jnp.exp(la)
    x2 = 2.0 * la
    y = jnp.where(x2 > -0.02, -x2 * (1.0 + x2 * (0.5 + x2 * (1.0 / 6.0))), 1.0 - a * a)
    inv_mult = lax.rsqrt(jnp.maximum(y, 1e-30))
    return r, i, a, y * inv_mult, inv_mult


def _later(x, before, k):
    if k == 0:
        return x
    row = lax.broadcasted_iota(jnp.int32, before.shape, 0)
    rolled = pltpu.roll(x, k, 0)
    first = jnp.where(row < k, pltpu.roll(before, k, 0), rolled[0:8])
    return jnp.concatenate([first, rolled[8:]], axis=0)


def _earlier(x, after, k):
    if k == 0:
        return x
    n = x.shape[0]
    row = lax.broadcasted_iota(jnp.int32, after.shape, 0)
    rolled = pltpu.roll(x, n - k, 0)
    last = jnp.where(row >= 8 - k, pltpu.roll(after, 8 - k, 0), rolled[n - 8:n])
    return jnp.concatenate([rolled[0:n - 8], last], axis=0)


def _fwd_fused(x, ln_gain, wt, tabs, wo_shard, conv_w, conv_b, wr, wi, br, bi, lam, tm):
    s = x.shape[0]
    nt = s // tm
    nc = 512
    pieces = 8
    rows_per = tm // pieces
    later_chunks = (0, 1, 2, 3, 4, 7, 8)

    def body(x0_ref, xn_ref, g_ref, wt_ref, c_ref, sa_ref, sb_ref, wo_ref, cw_ref, cb_ref, wr_ref, wi_ref, br_ref,
             bi_ref, lam_ref, h_ref, q_ref, k_ref, v_ref, ga_ref, xl_ref, gl_ref, u_ref, hl_ref, r_ref, ig_ref, a_ref,
             im_ref, wo_all, wo_stage, hb, halo, ub_scr, pr_scr, pi_scr, b_scr, hcar,
             send_sems, recv_sems, local_sems):
        i = pl.program_id(0)
        start, finish = _gather_ops([wo_stage], [wo_all], send_sems, recv_sems, local_sems)
        gain = g_ref[...]

        def normed(xx):
            rstd = lax.rsqrt(jnp.mean(xx * xx, axis=-1, keepdims=True) + EPS)
            return (xx * rstd * gain).astype(bf16)

        @pl.when(i == 0)
        def _():
            wo_stage[...] = wo_ref[...].astype(bf16)
            start()
            hb[0] = normed(x0_ref[...])
            halo[...] = jnp.zeros_like(halo)
            hcar[...] = jnp.zeros_like(hcar)

        cur, nxt = i % 2, (i + 1) % 2
        sp = _softplus(-lam_ref[...])
        br, bi = br_ref[...], bi_ref[...]
        c, sa, sb = c_ref[...], sa_ref[...], sb_ref[...]
        piece_rows = lambda p: slice(rows_per * p, rows_per * p + rows_per)

        def project(ci):
            z = _dot_nt(hb[cur], wt_ref[ci * nc:(ci + 1) * nc, :])
            if ci < 2:
                for j in range(nc // 128):
                    r = _rope(z[:, 128 * j:128 * j + 128], c, sa, sb) * (HD ** -0.5)
                    q_ref[ci * nc + 128 * j:ci * nc + 128 * j + 128, :] = r.astype(bf16).T
            elif ci == 2:
                for j in range(2):
                    js = slice(128 * j, 128 * j + 128)
                    k_ref[js, :] = _rope(z[:, js], c, sa, sb).astype(bf16).T
                    v_ref[js, :] = z[:, KVW + 128 * j:KVW + 128 * j + 128].astype(bf16).T
            else:
                sec, j = divmod(ci - 3, 2)
                (ga_ref, xl_ref, gl_ref)[sec][:, j * nc:(j + 1) * nc] = z

        def gate_terms(p):
            rows = piece_rows(p)
            r, ig, a, mult, inv_mult = _gate_terms(pr_scr[rows, :], pi_scr[rows, :], br, bi, sp)
            r_ref[rows, :] = r
            ig_ref[rows, :] = ig
            a_ref[rows, :] = a
            im_ref[rows, :] = inv_mult
            b_scr[rows, :] = mult * (ig * u_ref[rows, :])

        def scan(p, hc):
            for t in range(rows_per * p, rows_per * p + rows_per):
                hc = a_ref[t:t + 1, :] * hc + b_scr[t:t + 1, :]
                hl_ref[t:t + 1, :] = hc
            return hc

        def norm_next(p):
            hb[nxt, piece_rows(p), :] = normed(xn_ref[piece_rows(p), :])

        h_ref[...] = hb[cur]
        project(5)
        project(6)
        xl = xl_ref[...]
        u = cb_ref[...] + sum(cw_ref[k:k + 1, :] * _later(xl, halo[...], CONVW - 1 - k) for k in range(CONVW))
        halo[...] = xl[tm - 8:tm, :]
        u_ref[...] = u
        ub_scr[...] = u.astype(bf16)
        for g in range(NGRP):
            gs = slice(256 * g, 256 * g + 256)
            pr_scr[:, gs] = _dot(ub_scr[:, gs], wr_ref[g])
            pi_scr[:, gs] = _dot(ub_scr[:, gs], wi_ref[g])
        hc = hcar[...]
        gate_terms(0)
        for slot, ci in enumerate(later_chunks):
            project(ci)
            norm_next(slot)
            gate_terms(slot + 1)
            hc = scan(slot, hc)
        norm_next(pieces - 1)
        hcar[...] = scan(pieces - 1, hc)

        @pl.when(i == nt - 1)
        def _():
            finish()

    row = lambda w: pl.BlockSpec((tm, w), lambda i: (i, 0))
    col = lambda w: pl.BlockSpec((w, tm), lambda i: (0, i))
    full = lambda a: pl.BlockSpec(a.shape, lambda i: (0,) * a.ndim)
    big = lambda w, dt: pltpu.HBM((s, w), dt)
    tile = pltpu.VMEM((tm, LW), f32)
    return pl.pallas_call(
        body, name="fwd_fused", grid=(nt,),
        in_specs=[pl.BlockSpec((tm, D), lambda i: (0, 0)), pl.BlockSpec((tm, D), lambda i: (jnp.minimum(i + 1, nt - 1), 0)),
                  full(ln_gain), full(wt), row(128), row(128), row(128), full(wo_shard), full(conv_w), full(conv_b),
                  full(wr), full(wi), full(br), full(bi), full(lam)],
        out_specs=[row(D), col(D), col(KVW), col(KVW), row(D), row(D), row(D)] + [row(LW)] * 6 + [HBM],
        out_shape=[big(D, bf16), pltpu.HBM((D, s), bf16), pltpu.HBM((KVW, s), bf16), pltpu.HBM((KVW, s), bf16),
                   big(D, f32), big(D, f32), big(D, f32)] + [big(LW, f32)] * 6 + [pltpu.HBM((2 * D, D), bf16)],
        scratch_shapes=[pltpu.VMEM((WO_ROWS, D), bf16), pltpu.VMEM((2, tm, D), bf16), pltpu.VMEM((8, LW), f32),
                        pltpu.VMEM((tm, LW), bf16), tile, tile, tile, pltpu.VMEM((1, LW), f32)] + _comm_sems(1),
        compiler_params=_params(("arbitrary",), 56),
    )(*_in_hbm(x, x), ln_gain, *_in_hbm(wt), *tabs, wo_shard, conv_w, conv_b, wr, wi, br, bi, lam)


def _lru_bwd(u, hl, dhl, xl, r, ig, a, im, conv_w, wr, wi, lam, tm):
    s = u.shape[0]
    nt = s // tm
    pieces = 8
    rows_per = tm // pieces

    def body(u_ref, h_ref, hp_ref, dh_ref, x_ref, xp_ref, r_ref, ig_ref, a_ref, im_ref, cw_ref, wr_ref, wi_ref,
             lam_ref, dxl_ref, dwr_ref, dwi_ref, dbr_ref, dbi_ref, dlam_ref, dcb_ref, dcw_ref,
             l_scr, du_scr, dpr_scr, dpi_scr, lcar, dunext):
        t0 = pl.program_id(0)
        tile = nt - 1 - t0

        @pl.when(t0 == 0)
        def _():
            lcar[...] = jnp.zeros_like(lcar)
            dunext[...] = jnp.zeros_like(dunext)
            for ref in (dwr_ref, dwi_ref, dbr_ref, dbi_ref, dlam_ref, dcb_ref, dcw_ref):
                ref[...] = jnp.zeros_like(ref)

        lam = lam_ref[...]
        sp = _softplus(-lam)
        hp = jnp.where(tile > 0, hp_ref[...], 0.0)

        def scan(p, c):
            for t in range(rows_per * p + rows_per - 1, rows_per * p - 1, -1):
                lt = dh_ref[t:t + 1, :] + c
                l_scr[t:t + 1, :] = lt
                c = a_ref[t:t + 1, :] * lt
            return c

        def terms(p, sums):
            rows = slice(rows_per * p, rows_per * p + rows_per)
            lt, u, r, i, a, inv_mult = l_scr[rows, :], u_ref[rows, :], r_ref[rows, :], ig_ref[rows, :], \
                a_ref[rows, :], im_ref[rows, :]
            before = hp if p == 0 else h_ref[rows_per * p - 8:rows_per * p, :]
            hprev = _later(h_ref[rows, :], before, 1)
            x2 = -2.0 * LRU_C * r * sp
            mult = jnp.where(x2 > -0.02, -x2 * (1.0 + x2 * (0.5 + x2 * (1.0 / 6.0))), 1.0 - a * a) * inv_mult
            da = lt * hprev
            dmult = lt * (i * u)
            di = lt * mult * u
            du_scr[rows, :] = lt * mult * i
            dla = da * a - dmult * (a * a) * inv_mult
            dr = dla * (-LRU_C * sp)
            dpr = dr * r * (1.0 - r)
            dpi = di * i * (1.0 - i)
            dpr_scr[rows, :] = dpr.astype(bf16)
            dpi_scr[rows, :] = dpi.astype(bf16)
            col = lambda t: jnp.sum(t, axis=0, keepdims=True)
            return sums[0] + col(dla * (-LRU_C * r)), sums[1] + col(dpr), sums[2] + col(dpi)

        sums = (jnp.zeros((1, LW), f32),) * 3
        c = scan(pieces - 1, lcar[...])
        for p in range(pieces - 1, -1, -1):
            if p > 0:
                c = scan(p - 1, c)
            sums = terms(p, sums)
        lcar[...] = c
        dlam_ref[...] += sums[0]
        dbr_ref[...] += sums[1]
        dbi_ref[...] += sums[2]

        ub = u_ref[...].astype(bf16)
        dug = []
        for g in range(NGRP):
            gs = slice(256 * g, 256 * g + 256)
            dwr_ref[g] += _dot_tn(ub[:, gs], dpr_scr[:, gs])
            dwi_ref[g] += _dot_tn(ub[:, gs], dpi_scr[:, gs])
            dug.append(_dot_nt(dpr_scr[:, gs], wr_ref[g]) + _dot_nt(dpi_scr[:, gs], wi_ref[g]))
        du = du_scr[...] + jnp.concatenate(dug, axis=1)

        dcb_ref[...] += jnp.sum(du, axis=0, keepdims=True)
        x = x_ref[...]
        before = jnp.where(tile > 0, xp_ref[...], 0.0)
        for k in range(CONVW):
            dcw_ref[k:k + 1, :] += jnp.sum(du * _later(x, before, CONVW - 1 - k), axis=0, keepdims=True)
        after = dunext[...]
        dxl = sum(cw_ref[k:k + 1, :] * _earlier(du, after, CONVW - 1 - k) for k in range(CONVW))
        dxl_ref[...] = dxl.astype(bf16)
        dunext[...] = du[0:8, :]

        @pl.when(t0 == nt - 1)
        def _():
            dlam_ref[...] = dlam_ref[...] * (-_sigmoid(-lam))

    rev = lambda i: (nt - 1 - i, 0)
    row = pl.BlockSpec((tm, LW), rev)
    prev8 = pl.BlockSpec((8, LW), lambda i: (jnp.maximum((nt - 1 - i) * (tm // 8) - 1, 0), 0))
    full = lambda a: pl.BlockSpec(a.shape, lambda i: (0,) * a.ndim)
    vec = pl.BlockSpec((1, LW), lambda i: (0, 0))
    bd = pl.BlockSpec((NGRP, 256, 256), lambda i: (0, 0, 0))
    return pl.pallas_call(
        body, name="lru_bwd", grid=(nt,),
        in_specs=[row, row, prev8, row, row, prev8, row, row, row, row, full(conv_w), full(wr), full(wi), full(lam)],
        out_specs=[row, bd, bd, vec, vec, vec, vec, pl.BlockSpec((CONVW, LW), lambda i: (0, 0))],
        out_shape=[pltpu.HBM((s, LW), bf16),
                   jax.ShapeDtypeStruct((NGRP, 256, 256), f32), jax.ShapeDtypeStruct((NGRP, 256, 256), f32),
                   jax.ShapeDtypeStruct((1, LW), f32), jax.ShapeDtypeStruct((1, LW), f32),
                   jax.ShapeDtypeStruct((1, LW), f32), jax.ShapeDtypeStruct((1, LW), f32),
                   jax.ShapeDtypeStruct((CONVW, LW), f32)],
        scratch_shapes=[pltpu.VMEM((tm, LW), f32), pltpu.VMEM((tm, LW), f32), pltpu.VMEM((tm, LW), bf16),
                        pltpu.VMEM((tm, LW), bf16), pltpu.VMEM((1, LW), f32), pltpu.VMEM((8, LW), f32)],
        compiler_params=_params(("arbitrary",), 56),
    )(*_in_hbm(u, hl, hl, dhl, xl, xl, r, ig, a, im), conv_w, wr, wi, lam)


def _gated_norm(t, gate, gain):
    sg = _sigmoid(gate)
    silu = gate * sg
    p = t * silu
    rstd = lax.rsqrt(jnp.mean(p * p, axis=-1, keepdims=True) + EPS)
    ph = p * rstd
    return sg, silu, rstd, ph, ph * gain


def _gated_norm_bwd(dy, t, gate, gain, sg, silu, rstd, ph):
    w = dy * gain
    dp = rstd * (w - ph * jnp.mean(w * ph, axis=-1, keepdims=True))
    dgate = dp * t * (sg * (1.0 + gate * (1.0 - sg)))
    return jnp.sum(dy * ph, axis=0, keepdims=True), dp * silu, dgate


def _out_fwd_bwd(x, tgt, o, ga, hl, gl, again, lgain, fgain, wo, tm):
    s = x.shape[0]
    nt = s // tm

    def body(x_ref, t_ref, o_ref, ga_ref, hl_ref, gl_ref, ag_ref, lg_ref, fg_ref, wo_ref,
             dx2_ref, do_ref, dga_ref, dhl_ref, dgl_ref, dwo_ref, gfg_ref, gag_ref, glg_ref, loss_ref, acc):
        i = pl.program_id(0)

        @pl.when(i == 0)
        def _():
            acc[...] = jnp.zeros_like(acc)
            for ref in (gfg_ref, gag_ref, glg_ref, loss_ref):
                ref[...] = jnp.zeros_like(ref)

        oo = jnp.concatenate([o_ref[128 * j:128 * j + 128, :].T for j in range(D // 128)], axis=1)
        gga, hh, ggl = ga_ref[...], hl_ref[...], gl_ref[...]
        ag, lg, fg = ag_ref[...], lg_ref[...], fg_ref[...]
        sga, silua, ra, pah, ya = _gated_norm(oo, gga, ag)
        sgl, silul, rl, plh, yl = _gated_norm(hh, ggl, lg)
        yab, ylb = ya.astype(bf16), yl.astype(bf16)
        y = _dot(yab, wo_ref[0:D, :]) + _dot(ylb, wo_ref[D:2 * D, :])
        x2 = x_ref[...] + y
        r2 = lax.rsqrt(jnp.mean(x2 * x2, axis=-1, keepdims=True) + EPS)
        x2h = x2 * r2
        err = x2h * fg - t_ref[...]
        loss_ref[...] += 0.5 * jnp.sum(jnp.sum(err * err, axis=-1, keepdims=True) * (1.0 / D))
        dout = err * (1.0 / D)
        gfg_ref[...] += jnp.sum(dout * x2h, axis=0, keepdims=True)
        w = dout * fg
        dx2 = r2 * (w - x2h * jnp.mean(w * x2h, axis=-1, keepdims=True))
        dx2_ref[...] = dx2
        dyb = dx2.astype(bf16)
        acc[0:D, :] += _dot_tn(yab, dyb)
        acc[D:2 * D, :] += _dot_tn(ylb, dyb)
        dya = _dot_nt(dyb, wo_ref[0:D, :])
        dyl = _dot_nt(dyb, wo_ref[D:2 * D, :])
        gag, do, dga = _gated_norm_bwd(dya, oo, gga, ag, sga, silua, ra, pah)
        glg, dhl, dgl = _gated_norm_bwd(dyl, hh, ggl, lg, sgl, silul, rl, plh)
        gag_ref[...] += gag
        glg_ref[...] += glg
        dob = do.astype(bf16)
        for j in range(D // 128):
            do_ref[128 * j:128 * j + 128, :] = dob[:, 128 * j:128 * j + 128].T
        dga_ref[...] = dga.astype(bf16)
        dhl_ref[...] = dhl
        dgl_ref[...] = dgl.astype(bf16)

        @pl.when(i == nt - 1)
        def _():
            dwo_ref[...] = acc[...].astype(bf16)

    row = pl.BlockSpec((tm, D), lambda i: (i, 0))
    col = pl.BlockSpec((D, tm), lambda i: (0, i))
    vec = pl.BlockSpec((1, D), lambda i: (0, 0))
    mat = pl.BlockSpec((2 * D, D), lambda i: (0, 0))
    return pl.pallas_call(
        body, name="out_fwd_bwd", grid=(nt,),
        in_specs=[row, row, col, row, row, row] + [vec] * 3 + [mat],
        out_specs=[row, col, row, row, row] + [mat, vec, vec, vec, pl.BlockSpec((1, 128), lambda i: (0, 0))],
        out_shape=[pltpu.HBM((s, D), f32), pltpu.HBM((D, s), bf16),
                   pltpu.HBM((s, D), bf16), pltpu.HBM((s, D), f32),
                   pltpu.HBM((s, D), bf16), pltpu.HBM((2 * D, D), bf16),
                   jax.ShapeDtypeStruct((1, D), f32), jax.ShapeDtypeStruct((1, D), f32),
                   jax.ShapeDtypeStruct((1, D), f32), jax.ShapeDtypeStruct((1, 128), f32)],
        scratch_shapes=[pltpu.VMEM((2 * D, D), f32)],
        compiler_params=_params(("arbitrary",), 56),
    )(*_in_hbm(x, tgt, o, ga, hl, gl), again, lgain, fgain, *_in_hbm(wo))


def _bwd_in(x, dx2, dq, dk, dv, dga, dxl, dgl, ln_gain, wt, tabs, tm):
    s = x.shape[0]

    def body(x_ref, dx2_ref, dq_ref, dk_ref, dv_ref, dga_ref, dxl_ref, dgl_ref, g_ref, wt_ref,
             c_ref, sa_ref, sb_ref, gx_ref, gln_ref, dzt_ref):
        @pl.when(pl.program_id(0) == 0)
        def _():
            gln_ref[...] = jnp.zeros_like(gln_ref)

        c, sa, sb = c_ref[...], sa_ref[...], sb_ref[...]
        for j in range(D // 128):
            js = slice(128 * j, 128 * j + 128)
            dzt_ref[js, :] = (_unrope_t(dq_ref[js, :], c, sa, sb) * (HD ** -0.5)).astype(bf16)
        for j in range(KVW // 128):
            js = slice(128 * j, 128 * j + 128)
            dzt_ref[D + 128 * j:D + 128 * j + 128, :] = _unrope_t(dk_ref[js, :], c, sa, sb).astype(bf16)
        dzt_ref[D + KVW:D + 2 * KVW, :] = dv_ref[...].astype(bf16)
        first = D + 2 * KVW
        dh = _dot_tn(dzt_ref[0:512, :], wt_ref[0:512, :])
        for ci in range(1, first // 512):
            dh = dh + _dot_tn(dzt_ref[512 * ci:512 * ci + 512, :], wt_ref[512 * ci:512 * ci + 512, :])
        for sec, ref in enumerate((dga_ref, dxl_ref, dgl_ref)):
            for j in range(D // 512):
                rows = slice(first + D * sec + 512 * j, first + D * sec + 512 * j + 512)
                dh = dh + _dot(ref[:, 512 * j:512 * j + 512], wt_ref[rows, :])
            for j in range(D // 128):
                dzt_ref[first + D * sec + 128 * j:first + D * sec + 128 * j + 128, :] = ref[:, 128 * j:128 * j + 128].T
        xx = x_ref[...]
        rstd = lax.rsqrt(jnp.mean(xx * xx, axis=-1, keepdims=True) + EPS)
        xh = xx * rstd
        gln_ref[...] += jnp.sum(dh * xh, axis=0, keepdims=True)
        w = dh * g_ref[...]
        gx_ref[...] = dx2_ref[...] + rstd * (w - xh * jnp.mean(w * xh, axis=-1, keepdims=True))

    row = lambda w: pl.BlockSpec((tm, w), lambda i: (i, 0))
    col = lambda w: pl.BlockSpec((w, tm), lambda i: (0, i))
    full = lambda a: pl.BlockSpec(a.shape, lambda i: (0, 0))
    return pl.pallas_call(
        body, name="bwd_in", grid=(s // tm,),
        in_specs=[row(D), row(D), col(D), col(KVW), col(KVW), row(D), row(D), row(D), full(ln_gain), full(wt),
                  col(128), col(128), col(128)],
        out_specs=[row(D), pl.BlockSpec((1, D), lambda i: (0, 0)), col(NIN)],
        out_shape=[pltpu.HBM((s, D), f32), jax.ShapeDtypeStruct((1, D), f32),
                   pltpu.HBM((NIN, s), bf16)],
        compiler_params=_params(("arbitrary",), 56),
    )(*_in_hbm(x, dx2, dq, dk, dv, dga, dxl, dgl), ln_gain, *_in_hbm(wt), *tabs)


WT_TERMS = 5


def _dwt_scatter(dzt, h, small, tm):
    s = h.shape[0]
    nk = s // tm
    srows = small.shape[0] // NDEV
    last = NDEV - 1

    def body(order_ref, dz_ref, h_ref, sm_ref, lwt_ref, lsm_ref, acc, stage, given, send_sems, recv_sems, local_sem,
             sm_send, sm_recv, sm_local):
        j, k = pl.program_id(0), pl.program_id(1)
        x, y, c = _place()
        sibling = (x, y, 1 - c)
        chips = [(1 - x, 1 - y), (1 - x, y), (x, 1 - y)]
        sm_start, sm_finish = _scatter_ops([sm_ref], [lsm_ref], sm_send, sm_recv, sm_local)

        def send(step):
            if step == last - 1:
                dst, to = lwt_ref.at[1], sibling
            elif step % 2 == 0:
                dst, to = given.at[step // 2], sibling
            else:
                dst, to = lwt_ref.at[2 + step // 2], (*chips[step // 2], c)
            return pltpu.make_async_remote_copy(
                src_ref=stage.at[step % 2], dst_ref=dst, send_sem=send_sems.at[step], recv_sem=recv_sems.at[step],
                device_id=to, device_id_type=MESH)

        def keep():
            return pltpu.make_async_copy(stage.at[last % 2], lwt_ref.at[0], local_sem)

        @pl.when((j == 0) & (k == 0))
        def _():
            sm_start()

        @pl.when(k == 0)
        def _():
            acc[...] = jnp.zeros_like(acc)

        acc[...] += _dot(dz_ref[...], h_ref[...])

        for step in range(NDEV):
            @pl.when((k == nk - 1) & (j == step))
            def _(step=step):
                if step >= 2:
                    send(step - 2).wait_send()
                if step % 2 == 1 and step < last:
                    send(step - 1).wait_recv()
                    stage[step % 2] = (acc[...] + given[step // 2].astype(f32)).astype(bf16)
                else:
                    stage[step % 2] = acc[...].astype(bf16)
                if step < last:
                    send(step).start()
                else:
                    keep().start()
                    send(last - 1).wait_send()
                    for peer_step in (1, 3, 5, last - 1):
                        send(peer_step).wait_recv()
                    keep().wait()
                    sm_finish()

    x, y, c = _place()
    dest = lambda cx, cy, cc: 4 * cx + 2 * cy + cc
    order = jnp.stack([dest(1 - x, 1 - y, 1 - c), dest(1 - x, 1 - y, c), dest(1 - x, y, 1 - c), dest(1 - x, y, c),
                       dest(x, 1 - y, 1 - c), dest(x, 1 - y, c), dest(x, y, 1 - c), dest(x, y, c)])
    return pl.pallas_call(
        body, name="dwt_scatter",
        grid_spec=pltpu.PrefetchScalarGridSpec(
            num_scalar_prefetch=1, grid=(NDEV, nk),
            in_specs=[pl.BlockSpec((WT_ROWS, tm), lambda j, k, order: (order[j], k)),
                      pl.BlockSpec((tm, D), lambda j, k, order: (k, 0)), HBM],
            out_specs=[HBM, HBM],
            scratch_shapes=[pltpu.VMEM((WT_ROWS, D), f32), pltpu.VMEM((2, WT_ROWS, D), bf16),
                            pltpu.VMEM((3, WT_ROWS, D), bf16),
                            pltpu.SemaphoreType.DMA((last,)), pltpu.SemaphoreType.DMA((last,)),
                            pltpu.SemaphoreType.DMA(())] + _comm_sems(1)),
        out_shape=[pltpu.HBM((WT_TERMS, WT_ROWS, D), bf16), pltpu.HBM((NDEV, srows, D), f32)],
        compiler_params=_params(("arbitrary", "arbitrary"), 32),
    )(order, *_in_hbm(dzt, h, small))


def _diag_blocks(bd):
    eye = jnp.eye(4, dtype=bd.dtype)
    return jnp.einsum('gjckd,jk->gjcd', bd.reshape(NGRP, 4, HD, 4, HD), eye).reshape(NQ, HD, HD)


def _sequence_step(x, tgt, wt, wo_shard, conv_w, p):
    s = x.shape[0]
    tm = min(256, s)
    tabs, tabs_t = _rope_tables(s)
    wr, wi = _block_diag(p["w_rgate"]), _block_diag(p["w_igate"])
    sinks = p["sinks"].reshape(NQ)
    h, qt, kt, vt, ga, xl, gl, u, hl, r, ig, a, im, wo = _fwd_fused(
        x, p["ln_gain"], wt, tabs, wo_shard, conv_w, p["conv_b"], wr, wi, p["b_rgate"], p["b_igate"],
        p["lru_lambda"], tm)
    ot = _attn_fwd_t(qt, kt, vt, sinks)
    dx2, dot, dga, dhl, dgl, dwo, g_fg, g_ag, g_lg, loss = _out_fwd_bwd(
        x, tgt, ot, ga, hl, gl, p["attn_out_gain"], p["lru_out_gain"], p["final_gain"], wo, tm)
    dqt, dkt, dvt, dsink, land_wo = _attn_bwd_t(qt, kt, vt, dot, sinks, dwo)
    dxl, dwr, dwi, dbr, dbi, dlam, dcb, dcw = _lru_bwd(u, hl, dhl, xl, r, ig, a, im, conv_w, wr, wi, p["lru_lambda"], tm)
    gx, g_ln, dzt = _bwd_in(x, dx2, dqt, dkt, dvt, dga, dxl, dgl, p["ln_gain"], wt, tabs_t, tm)
    small = dict(ln_gain=g_ln, sinks=dsink.reshape(NQ, BLK).sum(axis=1)[None], conv_w=dcw, conv_b=dcb,
                 w_rgate=_diag_blocks(dwr), b_rgate=dbr, w_igate=_diag_blocks(dwi), b_igate=dbi, lru_lambda=dlam,
                 attn_out_gain=g_ag, lru_out_gain=g_lg, final_gain=g_fg)
    land_wt, land_sm = _dwt_scatter(dzt, h, _pack_small(small, loss), min(1024, s))
    return gx, land_wt, land_wo, land_sm


def _all_gather(srcs, out_dtypes, name):
    n = len(srcs)
    cast = [a.dtype != dt for a, dt in zip(srcs, out_dtypes)]

    def body(*refs):
        src_refs, out_refs = refs[:n], refs[n:2 * n]
        stage_refs = list(refs[2 * n:2 * n + sum(cast)])
        mine_refs = []
        for a in range(n):
            if cast[a]:
                st = stage_refs.pop(0)
                st[...] = src_refs[a][...].astype(out_dtypes[a])
                mine_refs.append(st)
            else:
                mine_refs.append(src_refs[a])
        start, finish = _gather_ops(mine_refs, out_refs, *refs[-3:])
        start()
        finish()

    vmem = pl.BlockSpec(memory_space=pltpu.VMEM)
    return pl.pallas_call(
        body, name=name,
        in_specs=[vmem] * n, out_specs=[HBM] * n,
        out_shape=[pltpu.HBM((NDEV * a.shape[0], a.shape[1]), dt) for a, dt in zip(srcs, out_dtypes)],
        scratch_shapes=[pltpu.VMEM(a.shape, dt) for a, dt, cst in zip(srcs, out_dtypes, cast) if cst] + _comm_sems(n),
        compiler_params=pltpu.CompilerParams(vmem_limit_bytes=32 * MIB),
    )(*srcs)


def _sum_slots(land, tr, name):
    terms, rows, cols = land.shape

    def body(l_ref, o_ref):
        acc = l_ref[0].astype(f32)
        for d in range(1, terms):
            acc = acc + l_ref[d].astype(f32)
        o_ref[...] = acc

    return pl.pallas_call(
        body, name=name, grid=(rows // tr,),
        in_specs=[pl.BlockSpec((terms, tr, cols), lambda i: (0, i, 0))],
        out_specs=pl.BlockSpec((tr, cols), lambda i: (i, 0)),
        out_shape=jax.ShapeDtypeStruct((rows, cols), f32),
        compiler_params=_params(("arbitrary",), 32),
    )(*_in_hbm(land))


def _adam_math(w, g, m, v):
    m2 = ADAM_B1 * m + (1.0 - ADAM_B1) * g
    v2 = ADAM_B2 * v + (1.0 - ADAM_B2) * (g * g)
    m_hat = m2 / (1.0 - ADAM_B1 ** ADAM_STEP)
    v_hat = v2 / (1.0 - ADAM_B2 ** ADAM_STEP)
    delta = -ADAM_LR * (m_hat / (jnp.sqrt(v_hat) + ADAM_EPS) + ADAM_WD * w)
    return delta, m2, v2


def _adamw(w, g, m, v, tr, name):
    rows, cols = w.shape

    def body(w_ref, g_ref, m_ref, v_ref, d_ref, m2_ref, v2_ref):
        d_ref[...], m2_ref[...], v2_ref[...] = _adam_math(w_ref[...], g_ref[...], m_ref[...], v_ref[...])

    blk = pl.BlockSpec((tr, cols), lambda i: (i, 0))
    return pl.pallas_call(
        body, name=name, grid=(rows // tr,),
        in_specs=[blk] * 4, out_specs=[blk] * 3,
        out_shape=[jax.ShapeDtypeStruct((rows, cols), f32)] * 3,
        compiler_params=_params(("arbitrary",), 32),
    )(w, g, m, v)


VEC_NAMES = ("ln_gain", "conv_b", "b_rgate", "b_igate", "lru_lambda", "attn_out_gain", "lru_out_gain", "final_gain")
ROW_RGATE, ROW_IGATE, ROW_VEC, ROW_SINKS = 0, 64, 128, 136
LOSS_LANE = NQ


def _adamw_small(g_rep, g_conv, w, m, v):
    names = list(VEC_NAMES) + ["sinks", "conv_w", "w_rgate", "w_igate"]
    ins = [g_rep, g_conv] + [d[k] for k in names for d in (w, m, v)]

    def body(*refs):
        g_ref, gc_ref = refs[0], refs[1]
        in_refs = refs[2:2 + 3 * len(names)]
        out_refs = refs[2 + 3 * len(names):]

        def update(j, g, at=None):
            w_ref, m_ref, v_ref = in_refs[3 * j:3 * j + 3]
            outs = out_refs[4 * j:4 * j + 4]
            pick = (lambda r: r[...]) if at is None else (lambda r: r[at])
            res = (g,) + _adam_math(pick(w_ref), g, pick(m_ref), pick(v_ref))
            for o_ref, val in zip(outs, res):
                if at is None:
                    o_ref[...] = val
                else:
                    o_ref[at] = val

        for j in range(len(VEC_NAMES)):
            update(j, g_ref[ROW_VEC + j:ROW_VEC + j + 1, :])
        update(len(VEC_NAMES), g_ref[ROW_SINKS:ROW_SINKS + 1, 0:NQ])
        update(len(VEC_NAMES) + 1, gc_ref[...], at=0)
        for gi, row0 in ((len(VEC_NAMES) + 2, ROW_RGATE), (len(VEC_NAMES) + 3, ROW_IGATE)):
            for nb in range(NQ):
                update(gi, g_ref[row0:row0 + HD, HD * nb:HD * nb + HD], at=(0, nb))

    vmem = pl.BlockSpec(memory_space=pltpu.VMEM)
    out_shape = [jax.ShapeDtypeStruct(w[k].shape, f32) for k in names for _ in range(4)]
    outs = pl.pallas_call(
        body, name="adamw_small",
        in_specs=[vmem] * len(ins), out_specs=[vmem] * len(out_shape), out_shape=out_shape,
        compiler_params=pltpu.CompilerParams(vmem_limit_bytes=32 * MIB),
    )(*ins)
    return {k: tuple(outs[4 * j:4 * j + 4]) for j, k in enumerate(names)}


def _pack_small(small, loss):
    gate = lambda g: g.transpose(1, 0, 2).reshape(HD, NQ * HD)
    row_s = jnp.concatenate([small["sinks"], loss[:, LOSS_LANE:128], jnp.zeros((1, D - 128), f32)], axis=1)
    rep = jnp.concatenate([gate(small["w_rgate"]), gate(small["w_igate"])] + [small[k] for k in VEC_NAMES]
                          + [row_s, jnp.zeros((SMALL_ROWS - ROW_SINKS - 1, D), f32)], axis=0)
    conv = small["conv_w"].reshape(CONVW, NDEV, 128).transpose(1, 0, 2)
    conv = jnp.pad(conv, ((0, 0), (0, 8 - CONVW), (0, D - 128)))
    return jnp.concatenate([rep.reshape(NDEV, SMALL_PER, D), conv], axis=1).reshape(NDEV * (SMALL_PER + 8), D)


def kernel(x, ln_gain, w_in, sinks, conv_w, conv_b, w_rgate, b_rgate, w_igate, b_igate, lru_lambda, attn_out_gain, lru_out_gain, w_out, final_gain, loss_target, m_ln_gain, m_w_in, m_sinks, m_conv_w, m_conv_b, m_w_rgate, m_b_rgate, m_w_igate, m_b_igate, m_lru_lambda, m_attn_out_gain, m_lru_out_gain, m_w_out, m_final_gain, v_ln_gain, v_w_in, v_sinks, v_conv_w, v_conv_b, v_w_rgate, v_b_rgate, v_w_igate, v_b_igate, v_lru_lambda, v_attn_out_gain, v_lru_out_gain, v_w_out, v_final_gain):
    w = dict(ln_gain=ln_gain, sinks=sinks, conv_w=conv_w, conv_b=conv_b, w_rgate=w_rgate, b_rgate=b_rgate,
             w_igate=w_igate, b_igate=b_igate, lru_lambda=lru_lambda, attn_out_gain=attn_out_gain,
             lru_out_gain=lru_out_gain, final_gain=final_gain.reshape(1, D))
    m = dict(ln_gain=m_ln_gain, sinks=m_sinks, conv_w=m_conv_w, conv_b=m_conv_b, w_rgate=m_w_rgate,
             b_rgate=m_b_rgate, w_igate=m_w_igate, b_igate=m_b_igate, lru_lambda=m_lru_lambda,
             attn_out_gain=m_attn_out_gain, lru_out_gain=m_lru_out_gain, final_gain=m_final_gain.reshape(1, D))
    v = dict(ln_gain=v_ln_gain, sinks=v_sinks, conv_w=v_conv_w, conv_b=v_conv_b, w_rgate=v_w_rgate,
             b_rgate=v_b_rgate, w_igate=v_w_igate, b_igate=v_b_igate, lru_lambda=v_lru_lambda,
             attn_out_gain=v_attn_out_gain, lru_out_gain=v_lru_out_gain, final_gain=v_final_gain.reshape(1, D))

    conv_blk = jnp.pad(conv_w[0], ((0, 8 - CONVW), (0, 0)))
    wt, cw_all = _all_gather([w_in[0].T, conv_blk], [bf16, f32], "gather_weights")
    conv_full = cw_all.reshape(NDEV, 8, 128)[:, 0:CONVW].transpose(1, 0, 2).reshape(CONVW, LW)

    p = {k: (w[k][0] if k in ("w_rgate", "w_igate") else w[k]) for k in w if k != "conv_w"}
    gx, land_wt, land_wo, land_sm = _sequence_step(x[0], loss_target[0], wt, w_out[0], conv_full, p)

    g_wt = _sum_slots(land_wt, 192, "sum_wt")
    g_wo = _sum_slots(land_wo, 256, "sum_wo")
    g_sm = _sum_slots(land_sm, SMALL_PER + 8, "sum_small")
    (g_rep,) = _all_gather([g_sm[0:SMALL_PER]], [f32], "gather_small")
    g_conv = g_sm[SMALL_PER:SMALL_PER + CONVW, 0:128]

    d_win, m_win, v_win = _adamw(w_in[0].T, g_wt, m_w_in[0].T, v_w_in[0].T, 192, "adamw_w_in")
    g_win, d_win, m_win, v_win = (t.T for t in (g_wt, d_win, m_win, v_win))
    d_wo, m_wo, v_wo = _adamw(w_out[0], g_wo, m_w_out[0], v_w_out[0], 256, "adamw_w_out")
    res = _adamw_small(g_rep, g_conv, w, m, v)
    res["w_in"] = tuple(t[None] for t in (g_win, d_win, m_win, v_win))
    res["w_out"] = tuple(t[None] for t in (g_wo, d_wo, m_wo, v_wo))
    res["final_gain"] = tuple(t.reshape(D) for t in res["final_gain"])

    order = ("ln_gain", "w_in", "sinks", "conv_w", "conv_b", "w_rgate", "b_rgate", "w_igate", "b_igate",
             "lru_lambda", "attn_out_gain", "lru_out_gain", "w_out", "final_gain")
    total_loss = g_rep[ROW_SINKS, LOSS_LANE]
    return (total_loss, gx[None]) + tuple(res[k][i] for i in range(4) for k in order)
```

```python
import jax
import jax.numpy as jnp
from jax import lax
from jax.experimental import pallas as pl
from jax.experimental.pallas import tpu as pltpu

f32 = jnp.float32
bf16 = jnp.bfloat16

D = 1024
HD = 64
NQ = 16
NKV = 4
GROUP = NQ // NKV
KVW = NKV * HD
BLK = 128
ROT = 16
THETA = 500000.0
NEG = -1e30
LW = 1024
NGRP = 4
CONVW = 4
LRU_C = 8.0
NIN = 4608
EPS = 1e-6
NDEV = 8
WT_ROWS = NIN // NDEV
WO_ROWS = 2 * D // NDEV
SMALL_ROWS = 192
SMALL_PER = SMALL_ROWS // NDEV

ADAM_LR = 0.001
ADAM_B1 = 0.9
ADAM_B2 = 0.999
ADAM_EPS = 1e-08
ADAM_WD = 0.01
ADAM_STEP = 10

NT = (((1,), (1,)), ((), ()))
TN = (((0,), (0,)), ((), ()))
MESH = pl.DeviceIdType.MESH
MIB = 1024 * 1024


def _dot(a, b):
    return jnp.dot(a, b, preferred_element_type=f32)


def _dot_nt(a, b):
    return lax.dot_general(a, b, NT, preferred_element_type=f32)


def _dot_tn(a, b):
    return lax.dot_general(a, b, TN, preferred_element_type=f32)


def _params(sem, vmem_mib):
    return pltpu.CompilerParams(dimension_semantics=sem, vmem_limit_bytes=vmem_mib * MIB)


def _sigmoid(x):
    return 0.5 * jnp.tanh(0.5 * x) + 0.5


def _softplus(x):
    return jnp.maximum(x, 0.0) + jnp.log(1.0 + jnp.exp(-jnp.abs(x)))


def _rope_tables(s):
    pos = jnp.arange(s, dtype=f32)
    inv_freq = THETA ** (-jnp.arange(0, ROT, 2, dtype=f32) / ROT)
    ang = pos[:, None] * inv_freq[None, :]
    cs = jnp.concatenate([jnp.cos(ang) - 1.0, jnp.sin(ang)], axis=1)
    d = jnp.arange(128) % HD
    j = jnp.arange(ROT)[:, None]
    pick_c = ((d < ROT) & (j == d % (ROT // 2))).astype(f32)
    pick_sa = ((d >= ROT // 2) & (d < ROT) & (j == d)).astype(f32)
    pick_sb = -((d < ROT // 2) & (j == d + ROT // 2)).astype(f32)
    picks = jnp.concatenate([pick_c, pick_sa, pick_sb], axis=1)
    ones = jnp.concatenate([jnp.ones((1, 128), f32), jnp.zeros((1, 256), f32)], axis=1)
    return jnp.dot(cs, picks, precision=lax.Precision.HIGHEST) + ones


def _tables(tab_ref):
    return tab_ref[:, 0:128], tab_ref[:, 128:256], tab_ref[:, 256:384]


def _rope(t, c, sa, sb):
    return t * c + pltpu.roll(t, 8, 1) * sa + pltpu.roll(t, 120, 1) * sb


def _unrope_t(dr, c, sa, sb):
    return dr * c + pltpu.roll(dr * sa, 120, 0) + pltpu.roll(dr * sb, 8, 0)


def _place():
    return lax.axis_index("x"), lax.axis_index("y"), lax.axis_index("c")


def _gather_ops(mine_refs, out_refs, send_sems, recv_sems, local_sems):
    n = len(mine_refs)
    x, y, c = _place()
    me, sibling = (x, y, c), (x, y, 1 - c)
    chips = [(1 - x, y), (x, 1 - y), (1 - x, 1 - y)]

    def rows(a, dev):
        m = mine_refs[a].shape[0]
        return out_refs[a].at[pl.ds((4 * dev[0] + 2 * dev[1] + dev[2]) * m, m), :]

    def copy(a, k, block, to, own=False):
        return pltpu.make_async_remote_copy(
            src_ref=mine_refs[a] if own else rows(a, block), dst_ref=rows(a, block),
            send_sem=send_sems.at[a, k], recv_sem=recv_sems.at[a, k], device_id=to, device_id_type=MESH)

    def local(a):
        return pltpu.make_async_copy(mine_refs[a], rows(a, me), local_sems.at[a])

    def first(a):
        return [copy(a, 0, me, sibling, own=True)] + [copy(a, 1 + j, me, (*chip, c), own=True)
                                                      for j, chip in enumerate(chips)]

    def start():
        for a in range(n):
            local(a).start()
            for cp in first(a):
                cp.start()

    def finish():
        for j, chip in enumerate(chips):
            for a in range(n):
                copy(a, 1 + j, (*chip, c), me).wait_recv()
                copy(a, 4 + j, (*chip, c), sibling).start()
        for a in range(n):
            copy(a, 0, sibling, me).wait_recv()
            for j, chip in enumerate(chips):
                copy(a, 4 + j, (*chip, 1 - c), me).wait_recv()
        for a in range(n):
            for cp in first(a) + [copy(a, 4 + j, (*chip, c), sibling) for j, chip in enumerate(chips)]:
                cp.wait_send()
            local(a).wait()

    return start, finish


def _scatter_ops(src_refs, land_refs, send_sems, recv_sems, local_sems):
    n = len(src_refs)
    x, y, c = _place()
    my = 4 * x + 2 * y + c

    def peer(k):
        return x ^ (k >> 2), y ^ ((k >> 1) & 1), c ^ (k & 1)

    def piece(a, dev):
        m = src_refs[a].shape[0] // NDEV
        return src_refs[a].at[pl.ds(dev * m, m), :]

    def local(a):
        return pltpu.make_async_copy(piece(a, my), land_refs[a].at[my], local_sems.at[a])

    def send(a, k):
        px, py, pc = peer(k)
        return pltpu.make_async_remote_copy(
            src_ref=piece(a, 4 * px + 2 * py + pc), dst_ref=land_refs[a].at[my],
            send_sem=send_sems.at[a, k - 1], recv_sem=recv_sems.at[a, k - 1],
            device_id=(px, py, pc), device_id_type=MESH)

    def arrival(a, k):
        px, py, pc = peer(k)
        return pltpu.make_async_remote_copy(
            src_ref=piece(a, my), dst_ref=land_refs[a].at[4 * px + 2 * py + pc],
            send_sem=send_sems.at[a, k - 1], recv_sem=recv_sems.at[a, k - 1],
            device_id=(px, py, pc), device_id_type=MESH)

    def start():
        for a in range(n):
            local(a).start()
        for k in range(1, NDEV):
            for a in range(n):
                send(a, k).start()

    def finish():
        for k in range(1, NDEV):
            for a in range(n):
                send(a, k).wait_send()
        for k in range(1, NDEV):
            for a in range(n):
                arrival(a, k).wait_recv()
        for a in range(n):
            local(a).wait()

    return start, finish


def _in_hbm(*arrays):
    return tuple(pltpu.with_memory_space_constraint(a, pltpu.HBM) for a in arrays)


def _comm_sems(n):
    return [pltpu.SemaphoreType.DMA((n, 7)), pltpu.SemaphoreType.DMA((n, 7)), pltpu.SemaphoreType.DMA((n,))]


HBM = pl.BlockSpec(memory_space=pltpu.HBM)


def _sink_rows(sinks):
    return jnp.repeat(sinks.reshape(NKV, GROUP), BLK, axis=1)


def _band_softmax(s2_ref, ls, prev_offset, sink_row):
    jj = lax.broadcasted_iota(jnp.int32, (BLK, BLK), 0)
    ii = lax.broadcasted_iota(jnp.int32, (BLK, BLK), 1)
    from_prev = jj > ii
    sc = jnp.where(from_prev, s2_ref[0:BLK, ls] + prev_offset, s2_ref[BLK:2 * BLK, ls])
    m = jnp.maximum(jnp.max(sc, axis=0, keepdims=True), sink_row)
    p = jnp.exp(sc - m)
    es = jnp.exp(sink_row - m)
    inv = 1.0 / (jnp.sum(p, axis=0, keepdims=True) + es)
    return from_prev, p * inv, es * inv


def _put_split(dst_ref, ls, t, from_prev):
    t = t.astype(bf16)
    zero = jnp.zeros_like(t)
    dst_ref[0:BLK, ls] = jnp.where(from_prev, t, zero)
    dst_ref[BLK:2 * BLK, ls] = jnp.where(from_prev, zero, t)


def _heads_side_by_side(ref, h):
    return jnp.concatenate([ref[HD * (GROUP * h + g):HD * (GROUP * h + g) + HD, :] for g in range(GROUP)], axis=1)


def _kv_specs_t():
    prev = pl.BlockSpec((KVW, BLK), lambda n: (0, jnp.maximum(n - 1, 0)))
    cur = pl.BlockSpec((KVW, BLK), lambda n: (0, n))
    return [prev, cur, prev, cur]


def _attn_fwd_t(qt, kt, vt, sinks):
    s = qt.shape[1]

    def body(sink_ref, q_ref, kp_ref, kc_ref, vp_ref, vc_ref, o_ref, s2_scr, pn2_scr):
        n = pl.program_id(0)
        off = jnp.where(n > 0, 0.0, NEG)

        def scores(h):
            hs = slice(HD * h, HD * h + HD)
            kh = jnp.concatenate([kp_ref[hs, :], kc_ref[hs, :]], axis=1)
            s2_scr[h % 2] = _dot_tn(kh, _heads_side_by_side(q_ref, h))

        def probs(h):
            for g in range(GROUP):
                ls = slice(BLK * g, BLK * g + BLK)
                from_prev, pn, _ = _band_softmax(s2_scr.at[h % 2], ls, off, sink_ref[h:h + 1, ls])
                _put_split(pn2_scr.at[h % 2], ls, pn, from_prev)

        def outputs(h):
            hs = slice(HD * h, HD * h + HD)
            vh = jnp.concatenate([vp_ref[hs, :], vc_ref[hs, :]], axis=1)
            og = _dot(vh, pn2_scr[h % 2])
            for g in range(GROUP):
                a = GROUP * h + g
                o_ref[HD * a:HD * a + HD, :] = og[:, BLK * g:BLK * g + BLK]

        scores(0)
        for h in range(NKV):
            if h + 1 < NKV:
                scores(h + 1)
            probs(h)
            outputs(h)

    return pl.pallas_call(
        body, name="attn_fwd", grid=(s // BLK,),
        in_specs=[pl.BlockSpec((NKV, GROUP * BLK), lambda n: (0, 0)), pl.BlockSpec((D, BLK), lambda n: (0, n))]
        + _kv_specs_t(),
        out_specs=pl.BlockSpec((D, BLK), lambda n: (0, n)),
        out_shape=pltpu.HBM((D, s), f32),
        scratch_shapes=[pltpu.VMEM((2, 2 * BLK, GROUP * BLK), f32), pltpu.VMEM((2, 2 * BLK, GROUP * BLK), bf16)],
        compiler_params=_params(("arbitrary",), 32),
    )(_sink_rows(sinks), *_in_hbm(qt, kt, kt, vt, vt))


def _attn_bwd_t(qt, kt, vt, dot, sinks, dwo):
    s = qt.shape[1]
    nb = s // BLK

    def body(sink_ref, q_ref, do_ref, kp_ref, kc_ref, vp_ref, vc_ref, dwo_ref, dq_ref, dk_ref, dv_ref, ds_ref,
             land_ref, dk_hold, dv_hold, s2_scr, dp2_scr, pn2_scr, ds2_scr, send_sems, recv_sems, local_sems):
        n = pl.program_id(0)
        start, finish = _scatter_ops([dwo_ref], [land_ref], send_sems, recv_sems, local_sems)

        @pl.when(n == 0)
        def _():
            start()
            dk_hold[...] = jnp.zeros_like(dk_hold)
            dv_hold[...] = jnp.zeros_like(dv_hold)
            ds_ref[...] = jnp.zeros_like(ds_ref)

        @pl.when(n < nb)
        def _():
            off = jnp.where(n > 0, 0.0, NEG)

            def scores(h):
                hs = slice(HD * h, HD * h + HD)
                kh = jnp.concatenate([kp_ref[hs, :], kc_ref[hs, :]], axis=1)
                vh = jnp.concatenate([vp_ref[hs, :], vc_ref[hs, :]], axis=1)
                s2_scr[h % 2] = _dot_tn(kh, _heads_side_by_side(q_ref, h))
                dp2_scr[h % 2] = _dot_tn(vh, _heads_side_by_side(do_ref, h))

            def softmax_bwd(h):
                for g in range(GROUP):
                    ls = slice(BLK * g, BLK * g + BLK)
                    from_prev, pn, ps = _band_softmax(s2_scr.at[h % 2], ls, off, sink_ref[h:h + 1, ls])
                    dp = jnp.where(from_prev, dp2_scr[h % 2, 0:BLK, ls], dp2_scr[h % 2, BLK:2 * BLK, ls])
                    dsum = jnp.sum(pn * dp, axis=0, keepdims=True)
                    ds_ref[h:h + 1, ls] += -ps * dsum
                    _put_split(pn2_scr.at[h % 2], ls, pn, from_prev)
                    _put_split(ds2_scr.at[h % 2], ls, pn * (dp - dsum), from_prev)

            def grads(h):
                hs = slice(HD * h, HD * h + HD)
                kh = jnp.concatenate([kp_ref[hs, :], kc_ref[hs, :]], axis=1)
                dqg = _dot(kh, ds2_scr[h % 2])
                for g in range(GROUP):
                    a = GROUP * h + g
                    dq_ref[HD * a:HD * a + HD, :] = dqg[:, BLK * g:BLK * g + BLK]
                dkh = _dot_nt(_heads_side_by_side(q_ref, h), ds2_scr[h % 2])
                dvh = _dot_nt(_heads_side_by_side(do_ref, h), pn2_scr[h % 2])
                dk_ref[hs, :] = dk_hold[hs, :] + dkh[:, 0:BLK]
                dv_ref[hs, :] = dv_hold[hs, :] + dvh[:, 0:BLK]
                dk_hold[hs, :] = dkh[:, BLK:2 * BLK]
                dv_hold[hs, :] = dvh[:, BLK:2 * BLK]

            scores(0)
            for h in range(NKV):
                if h + 1 < NKV:
                    scores(h + 1)
                softmax_bwd(h)
                grads(h)

        @pl.when(n == nb)
        def _():
            dk_ref[...] = dk_hold[...]
            dv_ref[...] = dv_hold[...]
            finish()

    blk = pl.BlockSpec((D, BLK), lambda n: (0, jnp.minimum(n, nb - 1)))
    late = pl.BlockSpec((KVW, BLK), lambda n: (0, jnp.maximum(n - 1, 0)))
    whole = pl.BlockSpec((NKV, GROUP * BLK), lambda n: (0, 0))
    kv = [pl.BlockSpec((KVW, BLK), lambda n: (0, jnp.clip(n - 1, 0, nb - 1))),
          pl.BlockSpec((KVW, BLK), lambda n: (0, jnp.minimum(n, nb - 1)))]
    return pl.pallas_call(
        body, name="attn_bwd", grid=(nb + 1,),
        in_specs=[whole, blk, blk] + kv + kv + [HBM],
        out_specs=[blk, late, late, whole, HBM],
        out_shape=[pltpu.HBM((D, s), f32), pltpu.HBM((KVW, s), f32), pltpu.HBM((KVW, s), f32),
                   jax.ShapeDtypeStruct((NKV, GROUP * BLK), f32), pltpu.HBM((NDEV, WO_ROWS, D), bf16)],
        scratch_shapes=[pltpu.VMEM((KVW, BLK), f32), pltpu.VMEM((KVW, BLK), f32)]
        + [pltpu.VMEM((2, 2 * BLK, GROUP * BLK), f32)] * 2 + [pltpu.VMEM((2, 2 * BLK, GROUP * BLK), bf16)] * 2
        + _comm_sems(1),
        compiler_params=_params(("arbitrary",), 48),
    )(_sink_rows(sinks), *_in_hbm(qt, dot, kt, kt, vt, vt, dwo))


def _block_diag(w):
    w4 = w.reshape(NGRP, 4, HD, HD)
    eye = jnp.eye(4, dtype=w.dtype)
    return jnp.einsum('gjcd,jk->gjckd', w4, eye).reshape(NGRP, 256, 256).astype(bf16)


def _gate_terms(pr, pi, br, bi, sp):
    r = _sigmoid(pr + br)
    i = _sigmoid(pi + bi)
    la = -LRU_C * r * sp
    a = jnp.exp(la)
    x2 = 2.0 * la
    y = jnp.where(x2 > -0.02, -x2 * (1.0 + x2 * (0.5 + x2 * (1.0 / 6.0))), 1.0 - a * a)
    inv_mult = lax.rsqrt(jnp.maximum(y, 1e-30))
    return r, i, a, y * inv_mult, inv_mult


def _later(x, before, k):
    if k == 0:
        return x
    row = lax.broadcasted_iota(jnp.int32, before.shape, 0)
    rolled = pltpu.roll(x, k, 0)
    first = jnp.where(row < k, pltpu.roll(before, k, 0), rolled[0:8])
    return jnp.concatenate([first, rolled[8:]], axis=0)


def _earlier(x, after, k):
    if k == 0:
        return x
    n = x.shape[0]
    row = lax.broadcasted_iota(jnp.int32, after.shape, 0)
    rolled = pltpu.roll(x, n - k, 0)
    last = jnp.where(row >= 8 - k, pltpu.roll(after, 8 - k, 0), rolled[n - 8:n])
    return jnp.concatenate([rolled[0:n - 8], last], axis=0)


def _fwd_fused(x, ln_gain, wt, tabs, wo_shard, conv_w, conv_b, wr, wi, br, bi, lam, tm):
    s = x.shape[0]
    nt = s // tm
    nc = 512
    pieces = 8
    rows_per = tm // pieces
    later_chunks = (0, 1, 2, 3, 4, 7, 8)

    def body(x0_ref, xn_ref, g_ref, wt_ref, tab_ref, wo_ref, cw_ref, cb_ref, wr_ref, wi_ref, br_ref,
             bi_ref, lam_ref, h_ref, q_ref, k_ref, v_ref, ga_ref, xl_ref, gl_ref, u_ref, hl_ref, r_ref, ig_ref, a_ref,
             im_ref, wo_all, wo_stage, hb, halo, ub_scr, pr_scr, pi_scr, b_scr, hcar,
             send_sems, recv_sems, local_sems):
        i = pl.program_id(0)
        start, finish = _gather_ops([wo_stage], [wo_all], send_sems, recv_sems, local_sems)
        gain = g_ref[...]

        def normed(xx):
            rstd = lax.rsqrt(jnp.mean(xx * xx, axis=-1, keepdims=True) + EPS)
            return (xx * rstd * gain).astype(bf16)

        @pl.when(i == 0)
        def _():
            wo_stage[...] = wo_ref[...].astype(bf16)
            start()
            hb[0] = normed(x0_ref[...])
            halo[...] = jnp.zeros_like(halo)
            hcar[...] = jnp.zeros_like(hcar)

        cur, nxt = i % 2, (i + 1) % 2
        sp = _softplus(-lam_ref[...])
        br, bi = br_ref[...], bi_ref[...]
        c, sa, sb = _tables(tab_ref)
        piece_rows = lambda p: slice(rows_per * p, rows_per * p + rows_per)

        def project(ci):
            z = _dot_nt(hb[cur], wt_ref[ci * nc:(ci + 1) * nc, :])
            if ci < 2:
                for j in range(nc // 128):
                    r = _rope(z[:, 128 * j:128 * j + 128], c, sa, sb) * (HD ** -0.5)
                    q_ref[ci * nc + 128 * j:ci * nc + 128 * j + 128, :] = r.astype(bf16).T
            elif ci == 2:
                for j in range(2):
                    js = slice(128 * j, 128 * j + 128)
                    k_ref[js, :] = _rope(z[:, js], c, sa, sb).astype(bf16).T
                    v_ref[js, :] = z[:, KVW + 128 * j:KVW + 128 * j + 128].astype(bf16).T
            else:
                sec, j = divmod(ci - 3, 2)
                (ga_ref, xl_ref, gl_ref)[sec][:, j * nc:(j + 1) * nc] = z

        def gate_terms(p):
            rows = piece_rows(p)
            r, ig, a, mult, inv_mult = _gate_terms(pr_scr[rows, :], pi_scr[rows, :], br, bi, sp)
            r_ref[rows, :] = r
            ig_ref[rows, :] = ig
            a_ref[rows, :] = a
            im_ref[rows, :] = inv_mult
            b_scr[rows, :] = mult * (ig * u_ref[rows, :])

        def scan(p, hc):
            for t in range(rows_per * p, rows_per * p + rows_per):
                hc = a_ref[t:t + 1, :] * hc + b_scr[t:t + 1, :]
                hl_ref[t:t + 1, :] = hc
            return hc

        def norm_next(p):
            hb[nxt, piece_rows(p), :] = normed(xn_ref[piece_rows(p), :])

        h_ref[...] = hb[cur]
        project(5)
        project(6)
        xl = xl_ref[...]
        u = cb_ref[...] + sum(cw_ref[k:k + 1, :] * _later(xl, halo[...], CONVW - 1 - k) for k in range(CONVW))
        halo[...] = xl[tm - 8:tm, :]
        u_ref[...] = u
        ub_scr[...] = u.astype(bf16)
        for g in range(NGRP):
            gs = slice(256 * g, 256 * g + 256)
            pr_scr[:, gs] = _dot(ub_scr[:, gs], wr_ref[g])
            pi_scr[:, gs] = _dot(ub_scr[:, gs], wi_ref[g])
        hc = hcar[...]
        gate_terms(0)
        for slot, ci in enumerate(later_chunks):
            project(ci)
            norm_next(slot)
            gate_terms(slot + 1)
            hc = scan(slot, hc)
        norm_next(pieces - 1)
        hcar[...] = scan(pieces - 1, hc)

        @pl.when(i == nt - 1)
        def _():
            finish()

    row = lambda w: pl.BlockSpec((tm, w), lambda i: (i, 0))
    col = lambda w: pl.BlockSpec((w, tm), lambda i: (0, i))
    full = lambda a: pl.BlockSpec(a.shape, lambda i: (0,) * a.ndim)
    big = lambda w, dt: pltpu.HBM((s, w), dt)
    tile = pltpu.VMEM((tm, LW), f32)
    return pl.pallas_call(
        body, name="fwd_fused", grid=(nt,),
        in_specs=[pl.BlockSpec((tm, D), lambda i: (0, 0)), pl.BlockSpec((tm, D), lambda i: (jnp.minimum(i + 1, nt - 1), 0)),
                  full(ln_gain), full(wt), row(384), full(wo_shard), full(conv_w), full(conv_b),
                  full(wr), full(wi), full(br), full(bi), full(lam)],
        out_specs=[row(D), col(D), col(KVW), col(KVW), row(D), row(D), row(D)] + [row(LW)] * 6 + [HBM],
        out_shape=[big(D, bf16), pltpu.HBM((D, s), bf16), pltpu.HBM((KVW, s), bf16), pltpu.HBM((KVW, s), bf16),
                   big(D, f32), big(D, f32), big(D, f32)] + [big(LW, f32)] * 6 + [pltpu.HBM((2 * D, D), bf16)],
        scratch_shapes=[pltpu.VMEM((WO_ROWS, D), bf16), pltpu.VMEM((2, tm, D), bf16), pltpu.VMEM((8, LW), f32),
                        pltpu.VMEM((tm, LW), bf16), tile, tile, tile, pltpu.VMEM((1, LW), f32)] + _comm_sems(1),
        compiler_params=_params(("arbitrary",), 56),
    )(*_in_hbm(x, x), ln_gain, *_in_hbm(wt), tabs, wo_shard, conv_w, conv_b, wr, wi, br, bi, lam)


def _lru_bwd(u, hl, dhl, xl, r, ig, a, im, conv_w, wr, wi, lam, tm):
    s = u.shape[0]
    nt = s // tm
    pieces = 8
    rows_per = tm // pieces

    def body(u_ref, h_ref, hp_ref, dh_ref, x_ref, xp_ref, r_ref, ig_ref, a_ref, im_ref, cw_ref, wr_ref, wi_ref,
             lam_ref, dxl_ref, dwr_ref, dwi_ref, dbr_ref, dbi_ref, dlam_ref, dcb_ref, dcw_ref,
             l_scr, du_scr, dpr_scr, dpi_scr, lcar, dunext):
        t0 = pl.program_id(0)
        tile = nt - 1 - t0

        @pl.when(t0 == 0)
        def _():
            lcar[...] = jnp.zeros_like(lcar)
            dunext[...] = jnp.zeros_like(dunext)
            for ref in (dwr_ref, dwi_ref, dbr_ref, dbi_ref, dlam_ref, dcb_ref, dcw_ref):
                ref[...] = jnp.zeros_like(ref)

        lam = lam_ref[...]
        sp = _softplus(-lam)
        hp = jnp.where(tile > 0, hp_ref[...], 0.0)

        def scan(p, c):
            for t in range(rows_per * p + rows_per - 1, rows_per * p - 1, -1):
                lt = dh_ref[t:t + 1, :] + c
                l_scr[t:t + 1, :] = lt
                c = a_ref[t:t + 1, :] * lt
            return c

        def terms(p, sums):
            rows = slice(rows_per * p, rows_per * p + rows_per)
            lt, u, r, i, a, inv_mult = l_scr[rows, :], u_ref[rows, :], r_ref[rows, :], ig_ref[rows, :], \
                a_ref[rows, :], im_ref[rows, :]
            before = hp if p == 0 else h_ref[rows_per * p - 8:rows_per * p, :]
            hprev = _later(h_ref[rows, :], before, 1)
            x2 = -2.0 * LRU_C * r * sp
            mult = jnp.where(x2 > -0.02, -x2 * (1.0 + x2 * (0.5 + x2 * (1.0 / 6.0))), 1.0 - a * a) * inv_mult
            da = lt * hprev
            dmult = lt * (i * u)
            di = lt * mult * u
            du_scr[rows, :] = lt * mult * i
            dla = da * a - dmult * (a * a) * inv_mult
            dr = dla * (-LRU_C * sp)
            dpr = dr * r * (1.0 - r)
            dpi = di * i * (1.0 - i)
            dpr_scr[rows, :] = dpr.astype(bf16)
            dpi_scr[rows, :] = dpi.astype(bf16)
            col = lambda t: jnp.sum(t, axis=0, keepdims=True)
            return sums[0] + col(dla * (-LRU_C * r)), sums[1] + col(dpr), sums[2] + col(dpi)

        sums = (jnp.zeros((1, LW), f32),) * 3
        c = scan(pieces - 1, lcar[...])
        for p in range(pieces - 1, -1, -1):
            if p > 0:
                c = scan(p - 1, c)
            sums = terms(p, sums)
        lcar[...] = c
        dlam_ref[...] += sums[0]
        dbr_ref[...] += sums[1]
        dbi_ref[...] += sums[2]

        ub = u_ref[...].astype(bf16)
        dug = []
        for g in range(NGRP):
            gs = slice(256 * g, 256 * g + 256)
            dwr_ref[g] += _dot_tn(ub[:, gs], dpr_scr[:, gs])
            dwi_ref[g] += _dot_tn(ub[:, gs], dpi_scr[:, gs])
            dug.append(_dot_nt(dpr_scr[:, gs], wr_ref[g]) + _dot_nt(dpi_scr[:, gs], wi_ref[g]))
        du = du_scr[...] + jnp.concatenate(dug, axis=1)

        dcb_ref[...] += jnp.sum(du, axis=0, keepdims=True)
        x = x_ref[...]
        before = jnp.where(tile > 0, xp_ref[...], 0.0)
        for k in range(CONVW):
            dcw_ref[k:k + 1, :] += jnp.sum(du * _later(x, before, CONVW - 1 - k), axis=0, keepdims=True)
        after = dunext[...]
        dxl = sum(cw_ref[k:k + 1, :] * _earlier(du, after, CONVW - 1 - k) for k in range(CONVW))
        dxl_ref[...] = dxl.astype(bf16)
        dunext[...] = du[0:8, :]

        @pl.when(t0 == nt - 1)
        def _():
            dlam_ref[...] = dlam_ref[...] * (-_sigmoid(-lam))

    rev = lambda i: (nt - 1 - i, 0)
    row = pl.BlockSpec((tm, LW), rev)
    prev8 = pl.BlockSpec((8, LW), lambda i: (jnp.maximum((nt - 1 - i) * (tm // 8) - 1, 0), 0))
    full = lambda a: pl.BlockSpec(a.shape, lambda i: (0,) * a.ndim)
    vec = pl.BlockSpec((1, LW), lambda i: (0, 0))
    bd = pl.BlockSpec((NGRP, 256, 256), lambda i: (0, 0, 0))
    return pl.pallas_call(
        body, name="lru_bwd", grid=(nt,),
        in_specs=[row, row, prev8, row, row, prev8, row, row, row, row, full(conv_w), full(wr), full(wi), full(lam)],
        out_specs=[row, bd, bd, vec, vec, vec, vec, pl.BlockSpec((CONVW, LW), lambda i: (0, 0))],
        out_shape=[pltpu.HBM((s, LW), bf16),
                   jax.ShapeDtypeStruct((NGRP, 256, 256), f32), jax.ShapeDtypeStruct((NGRP, 256, 256), f32),
                   jax.ShapeDtypeStruct((1, LW), f32), jax.ShapeDtypeStruct((1, LW), f32),
                   jax.ShapeDtypeStruct((1, LW), f32), jax.ShapeDtypeStruct((1, LW), f32),
                   jax.ShapeDtypeStruct((CONVW, LW), f32)],
        scratch_shapes=[pltpu.VMEM((tm, LW), f32), pltpu.VMEM((tm, LW), f32), pltpu.VMEM((tm, LW), bf16),
                        pltpu.VMEM((tm, LW), bf16), pltpu.VMEM((1, LW), f32), pltpu.VMEM((8, LW), f32)],
        compiler_params=_params(("arbitrary",), 56),
    )(*_in_hbm(u, hl, hl, dhl, xl, xl, r, ig, a, im), conv_w, wr, wi, lam)


def _gated_norm(t, gate, gain):
    sg = _sigmoid(gate)
    silu = gate * sg
    p = t * silu
    rstd = lax.rsqrt(jnp.mean(p * p, axis=-1, keepdims=True) + EPS)
    ph = p * rstd
    return sg, silu, rstd, ph, ph * gain


def _gated_norm_bwd(dy, t, gate, gain, sg, silu, rstd, ph):
    w = dy * gain
    dp = rstd * (w - ph * jnp.mean(w * ph, axis=-1, keepdims=True))
    dgate = dp * t * (sg * (1.0 + gate * (1.0 - sg)))
    return jnp.sum(dy * ph, axis=0, keepdims=True), dp * silu, dgate


def _out_fwd_bwd(x, tgt, o, ga, hl, gl, again, lgain, fgain, wo, tm):
    s = x.shape[0]
    nt = s // tm

    def body(x_ref, t_ref, o_ref, ga_ref, hl_ref, gl_ref, ag_ref, lg_ref, fg_ref, wo_ref,
             dx2_ref, do_ref, dga_ref, dhl_ref, dgl_ref, dwo_ref, gfg_ref, gag_ref, glg_ref, loss_ref, acc):
        i = pl.program_id(0)

        @pl.when(i == 0)
        def _():
            acc[...] = jnp.zeros_like(acc)
            for ref in (gfg_ref, gag_ref, glg_ref, loss_ref):
                ref[...] = jnp.zeros_like(ref)

        oo = jnp.concatenate([o_ref[128 * j:128 * j + 128, :].T for j in range(D // 128)], axis=1)
        gga, hh, ggl = ga_ref[...], hl_ref[...], gl_ref[...]
        ag, lg, fg = ag_ref[...], lg_ref[...], fg_ref[...]
        sga, silua, ra, pah, ya = _gated_norm(oo, gga, ag)
        sgl, silul, rl, plh, yl = _gated_norm(hh, ggl, lg)
        yab, ylb = ya.astype(bf16), yl.astype(bf16)
        y = _dot(yab, wo_ref[0:D, :]) + _dot(ylb, wo_ref[D:2 * D, :])
        x2 = x_ref[...] + y
        r2 = lax.rsqrt(jnp.mean(x2 * x2, axis=-1, keepdims=True) + EPS)
        x2h = x2 * r2
        err = x2h * fg - t_ref[...]
        loss_ref[...] += 0.5 * jnp.sum(jnp.sum(err * err, axis=-1, keepdims=True) * (1.0 / D))
        dout = err * (1.0 / D)
        gfg_ref[...] += jnp.sum(dout * x2h, axis=0, keepdims=True)
        w = dout * fg
        dx2 = r2 * (w - x2h * jnp.mean(w * x2h, axis=-1, keepdims=True))
        dx2_ref[...] = dx2
        dyb = dx2.astype(bf16)
        acc[0:D, :] += _dot_tn(yab, dyb)
        acc[D:2 * D, :] += _dot_tn(ylb, dyb)
        dya = _dot_nt(dyb, wo_ref[0:D, :])
        dyl = _dot_nt(dyb, wo_ref[D:2 * D, :])
        gag, do, dga = _gated_norm_bwd(dya, oo, gga, ag, sga, silua, ra, pah)
        glg, dhl, dgl = _gated_norm_bwd(dyl, hh, ggl, lg, sgl, silul, rl, plh)
        gag_ref[...] += gag
        glg_ref[...] += glg
        dob = do.astype(bf16)
        for j in range(D // 128):
            do_ref[128 * j:128 * j + 128, :] = dob[:, 128 * j:128 * j + 128].T
        dga_ref[...] = dga.astype(bf16)
        dhl_ref[...] = dhl
        dgl_ref[...] = dgl.astype(bf16)

        @pl.when(i == nt - 1)
        def _():
            dwo_ref[...] = acc[...].astype(bf16)

    row = pl.BlockSpec((tm, D), lambda i: (i, 0))
    col = pl.BlockSpec((D, tm), lambda i: (0, i))
    vec = pl.BlockSpec((1, D), lambda i: (0, 0))
    mat = pl.BlockSpec((2 * D, D), lambda i: (0, 0))
    return pl.pallas_call(
        body, name="out_fwd_bwd", grid=(nt,),
        in_specs=[row, row, col, row, row, row] + [vec] * 3 + [mat],
        out_specs=[row, col, row, row, row] + [mat, vec, vec, vec, pl.BlockSpec((1, 128), lambda i: (0, 0))],
        out_shape=[pltpu.HBM((s, D), f32), pltpu.HBM((D, s), bf16),
                   pltpu.HBM((s, D), bf16), pltpu.HBM((s, D), f32),
                   pltpu.HBM((s, D), bf16), pltpu.HBM((2 * D, D), bf16),
                   jax.ShapeDtypeStruct((1, D), f32), jax.ShapeDtypeStruct((1, D), f32),
                   jax.ShapeDtypeStruct((1, D), f32), jax.ShapeDtypeStruct((1, 128), f32)],
        scratch_shapes=[pltpu.VMEM((2 * D, D), f32)],
        compiler_params=_params(("arbitrary",), 56),
    )(*_in_hbm(x, tgt, o, ga, hl, gl), again, lgain, fgain, *_in_hbm(wo))


def _bwd_in(x, dx2, dq, dk, dv, dga, dxl, dgl, ln_gain, wt, tabs, tm):
    s = x.shape[0]

    def body(x_ref, dx2_ref, dq_ref, dk_ref, dv_ref, dga_ref, dxl_ref, dgl_ref, g_ref, wt_ref,
             tab_ref, gx_ref, gln_ref, dzt_ref):
        @pl.when(pl.program_id(0) == 0)
        def _():
            gln_ref[...] = jnp.zeros_like(gln_ref)

        c, sa, sb = (t.T for t in _tables(tab_ref))
        for j in range(D // 128):
            js = slice(128 * j, 128 * j + 128)
            dzt_ref[js, :] = (_unrope_t(dq_ref[js, :], c, sa, sb) * (HD ** -0.5)).astype(bf16)
        for j in range(KVW // 128):
            js = slice(128 * j, 128 * j + 128)
            dzt_ref[D + 128 * j:D + 128 * j + 128, :] = _unrope_t(dk_ref[js, :], c, sa, sb).astype(bf16)
        dzt_ref[D + KVW:D + 2 * KVW, :] = dv_ref[...].astype(bf16)
        first = D + 2 * KVW
        dh = _dot_tn(dzt_ref[0:512, :], wt_ref[0:512, :])
        for ci in range(1, first // 512):
            dh = dh + _dot_tn(dzt_ref[512 * ci:512 * ci + 512, :], wt_ref[512 * ci:512 * ci + 512, :])
        for sec, ref in enumerate((dga_ref, dxl_ref, dgl_ref)):
            for j in range(D // 512):
                rows = slice(first + D * sec + 512 * j, first + D * sec + 512 * j + 512)
                dh = dh + _dot(ref[:, 512 * j:512 * j + 512], wt_ref[rows, :])
            for j in range(D // 128):
                dzt_ref[first + D * sec + 128 * j:first + D * sec + 128 * j + 128, :] = ref[:, 128 * j:128 * j + 128].T
        xx = x_ref[...]
        rstd = lax.rsqrt(jnp.mean(xx * xx, axis=-1, keepdims=True) + EPS)
        xh = xx * rstd
        gln_ref[...] += jnp.sum(dh * xh, axis=0, keepdims=True)
        w = dh * g_ref[...]
        gx_ref[...] = dx2_ref[...] + rstd * (w - xh * jnp.mean(w * xh, axis=-1, keepdims=True))

    row = lambda w: pl.BlockSpec((tm, w), lambda i: (i, 0))
    col = lambda w: pl.BlockSpec((w, tm), lambda i: (0, i))
    full = lambda a: pl.BlockSpec(a.shape, lambda i: (0, 0))
    return pl.pallas_call(
        body, name="bwd_in", grid=(s // tm,),
        in_specs=[row(D), row(D), col(D), col(KVW), col(KVW), row(D), row(D), row(D), full(ln_gain), full(wt),
                  row(384)],
        out_specs=[row(D), pl.BlockSpec((1, D), lambda i: (0, 0)), col(NIN)],
        out_shape=[pltpu.HBM((s, D), f32), jax.ShapeDtypeStruct((1, D), f32),
                   pltpu.HBM((NIN, s), bf16)],
        compiler_params=_params(("arbitrary",), 56),
    )(*_in_hbm(x, dx2, dq, dk, dv, dga, dxl, dgl), ln_gain, *_in_hbm(wt), tabs)


WT_TERMS = 5


def _dwt_scatter(dzt, h, small, tm):
    s = h.shape[0]
    nk = s // tm
    srows = small.shape[0] // NDEV
    last = NDEV - 1

    def body(order_ref, dz_ref, h_ref, sm_ref, lwt_ref, lsm_ref, acc, stage, given, send_sems, recv_sems, local_sem,
             sm_send, sm_recv, sm_local):
        j, k = pl.program_id(0), pl.program_id(1)
        x, y, c = _place()
        sibling = (x, y, 1 - c)
        chips = [(1 - x, 1 - y), (1 - x, y), (x, 1 - y)]
        sm_start, sm_finish = _scatter_ops([sm_ref], [lsm_ref], sm_send, sm_recv, sm_local)

        def send(step):
            if step == last - 1:
                dst, to = lwt_ref.at[1], sibling
            elif step % 2 == 0:
                dst, to = given.at[step // 2], sibling
            else:
                dst, to = lwt_ref.at[2 + step // 2], (*chips[step // 2], c)
            return pltpu.make_async_remote_copy(
                src_ref=stage.at[step % 2], dst_ref=dst, send_sem=send_sems.at[step], recv_sem=recv_sems.at[step],
                device_id=to, device_id_type=MESH)

        def keep():
            return pltpu.make_async_copy(stage.at[last % 2], lwt_ref.at[0], local_sem)

        @pl.when((j == 0) & (k == 0))
        def _():
            sm_start()

        @pl.when(k == 0)
        def _():
            acc[...] = jnp.zeros_like(acc)

        acc[...] += _dot(dz_ref[...], h_ref[...])

        for step in range(NDEV):
            @pl.when((k == nk - 1) & (j == step))
            def _(step=step):
                if step >= 2:
                    send(step - 2).wait_send()
                if step % 2 == 1 and step < last:
                    send(step - 1).wait_recv()
                    stage[step % 2] = (acc[...] + given[step // 2].astype(f32)).astype(bf16)
                else:
                    stage[step % 2] = acc[...].astype(bf16)
                if step < last:
                    send(step).start()
                else:
                    keep().start()
                    send(last - 1).wait_send()
                    for peer_step in (1, 3, 5, last - 1):
                        send(peer_step).wait_recv()
                    keep().wait()
                    sm_finish()

    x, y, c = _place()
    dest = lambda cx, cy, cc: 4 * cx + 2 * cy + cc
    order = jnp.stack([dest(1 - x, 1 - y, 1 - c), dest(1 - x, 1 - y, c), dest(1 - x, y, 1 - c), dest(1 - x, y, c),
                       dest(x, 1 - y, 1 - c), dest(x, 1 - y, c), dest(x, y, 1 - c), dest(x, y, c)])
    return pl.pallas_call(
        body, name="dwt_scatter",
        grid_spec=pltpu.PrefetchScalarGridSpec(
            num_scalar_prefetch=1, grid=(NDEV, nk),
            in_specs=[pl.BlockSpec((WT_ROWS, tm), lambda j, k, order: (order[j], k)),
                      pl.BlockSpec((tm, D), lambda j, k, order: (k, 0)), HBM],
            out_specs=[HBM, HBM],
            scratch_shapes=[pltpu.VMEM((WT_ROWS, D), f32), pltpu.VMEM((2, WT_ROWS, D), bf16),
                            pltpu.VMEM((3, WT_ROWS, D), bf16),
                            pltpu.SemaphoreType.DMA((last,)), pltpu.SemaphoreType.DMA((last,)),
                            pltpu.SemaphoreType.DMA(())] + _comm_sems(1)),
        out_shape=[pltpu.HBM((WT_TERMS, WT_ROWS, D), bf16), pltpu.HBM((NDEV, srows, D), f32)],
        compiler_params=_params(("arbitrary", "arbitrary"), 32),
    )(order, *_in_hbm(dzt, h, small))


def _diag_blocks(bd):
    eye = jnp.eye(4, dtype=bd.dtype)
    return jnp.einsum('gjckd,jk->gjcd', bd.reshape(NGRP, 4, HD, 4, HD), eye).reshape(NQ, HD, HD)


def _sequence_step(x, tgt, wt, wo_shard, conv_w, p):
    s = x.shape[0]
    tm = min(256, s)
    tabs = _rope_tables(s)
    wr, wi = _block_diag(p["w_rgate"]), _block_diag(p["w_igate"])
    sinks = p["sinks"].reshape(NQ)
    h, qt, kt, vt, ga, xl, gl, u, hl, r, ig, a, im, wo = _fwd_fused(
        x, p["ln_gain"], wt, tabs, wo_shard, conv_w, p["conv_b"], wr, wi, p["b_rgate"], p["b_igate"],
        p["lru_lambda"], tm)
    ot = _attn_fwd_t(qt, kt, vt, sinks)
    dx2, dot, dga, dhl, dgl, dwo, g_fg, g_ag, g_lg, loss = _out_fwd_bwd(
        x, tgt, ot, ga, hl, gl, p["attn_out_gain"], p["lru_out_gain"], p["final_gain"], wo, tm)
    dqt, dkt, dvt, dsink, land_wo = _attn_bwd_t(qt, kt, vt, dot, sinks, dwo)
    dxl, dwr, dwi, dbr, dbi, dlam, dcb, dcw = _lru_bwd(u, hl, dhl, xl, r, ig, a, im, conv_w, wr, wi, p["lru_lambda"], tm)
    gx, g_ln, dzt = _bwd_in(x, dx2, dqt, dkt, dvt, dga, dxl, dgl, p["ln_gain"], wt, tabs, tm)
    small = dict(ln_gain=g_ln, sinks=dsink.reshape(NQ, BLK).sum(axis=1)[None], conv_w=dcw, conv_b=dcb,
                 w_rgate=_diag_blocks(dwr), b_rgate=dbr, w_igate=_diag_blocks(dwi), b_igate=dbi, lru_lambda=dlam,
                 attn_out_gain=g_ag, lru_out_gain=g_lg, final_gain=g_fg)
    land_wt, land_sm = _dwt_scatter(dzt, h, _pack_small(small, loss), min(1024, s))
    return gx, land_wt, land_wo, land_sm


def _all_gather(srcs, out_dtypes, name):
    n = len(srcs)
    cast = [a.dtype != dt for a, dt in zip(srcs, out_dtypes)]

    def body(*refs):
        src_refs, out_refs = refs[:n], refs[n:2 * n]
        stage_refs = list(refs[2 * n:2 * n + sum(cast)])
        mine_refs = []
        for a in range(n):
            if cast[a]:
                st = stage_refs.pop(0)
                st[...] = src_refs[a][...].astype(out_dtypes[a])
                mine_refs.append(st)
            else:
                mine_refs.append(src_refs[a])
        start, finish = _gather_ops(mine_refs, out_refs, *refs[-3:])
        start()
        finish()

    vmem = pl.BlockSpec(memory_space=pltpu.VMEM)
    return pl.pallas_call(
        body, name=name,
        in_specs=[vmem] * n, out_specs=[HBM] * n,
        out_shape=[pltpu.HBM((NDEV * a.shape[0], a.shape[1]), dt) for a, dt in zip(srcs, out_dtypes)],
        scratch_shapes=[pltpu.VMEM(a.shape, dt) for a, dt, cst in zip(srcs, out_dtypes, cast) if cst] + _comm_sems(n),
        compiler_params=pltpu.CompilerParams(vmem_limit_bytes=32 * MIB),
    )(*srcs)


def _sum_slots(land, tr, name):
    terms, rows, cols = land.shape

    def body(l_ref, o_ref):
        acc = l_ref[0].astype(f32)
        for d in range(1, terms):
            acc = acc + l_ref[d].astype(f32)
        o_ref[...] = acc

    return pl.pallas_call(
        body, name=name, grid=(rows // tr,),
        in_specs=[pl.BlockSpec((terms, tr, cols), lambda i: (0, i, 0))],
        out_specs=pl.BlockSpec((tr, cols), lambda i: (i, 0)),
        out_shape=jax.ShapeDtypeStruct((rows, cols), f32),
        compiler_params=_params(("arbitrary",), 32),
    )(*_in_hbm(land))


def _adam_math(w, g, m, v):
    m2 = ADAM_B1 * m + (1.0 - ADAM_B1) * g
    v2 = ADAM_B2 * v + (1.0 - ADAM_B2) * (g * g)
    m_hat = m2 / (1.0 - ADAM_B1 ** ADAM_STEP)
    v_hat = v2 / (1.0 - ADAM_B2 ** ADAM_STEP)
    delta = -ADAM_LR * (m_hat / (jnp.sqrt(v_hat) + ADAM_EPS) + ADAM_WD * w)
    return delta, m2, v2


def _reduce_adamw(land, w, m, v, tr, name):
    terms, rows, cols = land.shape

    def body(l_ref, w_ref, m_ref, v_ref, g_ref, d_ref, m2_ref, v2_ref):
        g = l_ref[0].astype(f32)
        for t in range(1, terms):
            g = g + l_ref[t].astype(f32)
        g_ref[...] = g
        d_ref[...], m2_ref[...], v2_ref[...] = _adam_math(w_ref[...], g, m_ref[...], v_ref[...])

    blk = pl.BlockSpec((tr, cols), lambda i: (i, 0))
    return pl.pallas_call(
        body, name=name, grid=(rows // tr,),
        in_specs=[pl.BlockSpec((terms, tr, cols), lambda i: (0, i, 0))] + [blk] * 3, out_specs=[blk] * 4,
        out_shape=[jax.ShapeDtypeStruct((rows, cols), f32)] * 4,
        compiler_params=_params(("arbitrary",), 32),
    )(*_in_hbm(land), w, m, v)


VEC_NAMES = ("ln_gain", "conv_b", "b_rgate", "b_igate", "lru_lambda", "attn_out_gain", "lru_out_gain", "final_gain")
ROW_RGATE, ROW_IGATE, ROW_VEC, ROW_SINKS = 0, 64, 128, 136
LOSS_LANE = NQ


def _adamw_small(g_rep, g_conv, w, m, v):
    names = list(VEC_NAMES) + ["sinks", "conv_w", "w_rgate", "w_igate"]
    ins = [g_rep, g_conv] + [d[k] for k in names for d in (w, m, v)]

    def body(*refs):
        g_ref, gc_ref = refs[0], refs[1]
        in_refs = refs[2:2 + 3 * len(names)]
        out_refs = refs[2 + 3 * len(names):]

        def update(j, g, at=None):
            w_ref, m_ref, v_ref = in_refs[3 * j:3 * j + 3]
            outs = out_refs[4 * j:4 * j + 4]
            pick = (lambda r: r[...]) if at is None else (lambda r: r[at])
            res = (g,) + _adam_math(pick(w_ref), g, pick(m_ref), pick(v_ref))
            for o_ref, val in zip(outs, res):
                if at is None:
                    o_ref[...] = val
                else:
                    o_ref[at] = val

        for j in range(len(VEC_NAMES)):
            update(j, g_ref[ROW_VEC + j:ROW_VEC + j + 1, :])
        update(len(VEC_NAMES), g_ref[ROW_SINKS:ROW_SINKS + 1, 0:NQ])
        update(len(VEC_NAMES) + 1, gc_ref[...], at=0)
        for gi, row0 in ((len(VEC_NAMES) + 2, ROW_RGATE), (len(VEC_NAMES) + 3, ROW_IGATE)):
            for nb in range(NQ):
                update(gi, g_ref[row0:row0 + HD, HD * nb:HD * nb + HD], at=(0, nb))

    vmem = pl.BlockSpec(memory_space=pltpu.VMEM)
    out_shape = [jax.ShapeDtypeStruct(w[k].shape, f32) for k in names for _ in range(4)]
    outs = pl.pallas_call(
        body, name="adamw_small",
        in_specs=[vmem] * len(ins), out_specs=[vmem] * len(out_shape), out_shape=out_shape,
        compiler_params=pltpu.CompilerParams(vmem_limit_bytes=32 * MIB),
    )(*ins)
    return {k: tuple(outs[4 * j:4 * j + 4]) for j, k in enumerate(names)}


def _pack_small(small, loss):
    gate = lambda g: g.transpose(1, 0, 2).reshape(HD, NQ * HD)
    row_s = jnp.concatenate([small["sinks"], loss[:, LOSS_LANE:128], jnp.zeros((1, D - 128), f32)], axis=1)
    rep = jnp.concatenate([gate(small["w_rgate"]), gate(small["w_igate"])] + [small[k] for k in VEC_NAMES]
                          + [row_s, jnp.zeros((SMALL_ROWS - ROW_SINKS - 1, D), f32)], axis=0)
    conv = small["conv_w"].reshape(CONVW, NDEV, 128).transpose(1, 0, 2)
    conv = jnp.pad(conv, ((0, 0), (0, 8 - CONVW), (0, D - 128)))
    return jnp.concatenate([rep.reshape(NDEV, SMALL_PER, D), conv], axis=1).reshape(NDEV * (SMALL_PER + 8), D)


def kernel(x, ln_gain, w_in, sinks, conv_w, conv_b, w_rgate, b_rgate, w_igate, b_igate, lru_lambda, attn_out_gain, lru_out_gain, w_out, final_gain, loss_target, m_ln_gain, m_w_in, m_sinks, m_conv_w, m_conv_b, m_w_rgate, m_b_rgate, m_w_igate, m_b_igate, m_lru_lambda, m_attn_out_gain, m_lru_out_gain, m_w_out, m_final_gain, v_ln_gain, v_w_in, v_sinks, v_conv_w, v_conv_b, v_w_rgate, v_b_rgate, v_w_igate, v_b_igate, v_lru_lambda, v_attn_out_gain, v_lru_out_gain, v_w_out, v_final_gain):
    w = dict(ln_gain=ln_gain, sinks=sinks, conv_w=conv_w, conv_b=conv_b, w_rgate=w_rgate, b_rgate=b_rgate,
             w_igate=w_igate, b_igate=b_igate, lru_lambda=lru_lambda, attn_out_gain=attn_out_gain,
             lru_out_gain=lru_out_gain, final_gain=final_gain.reshape(1, D))
    m = dict(ln_gain=m_ln_gain, sinks=m_sinks, conv_w=m_conv_w, conv_b=m_conv_b, w_rgate=m_w_rgate,
             b_rgate=m_b_rgate, w_igate=m_w_igate, b_igate=m_b_igate, lru_lambda=m_lru_lambda,
             attn_out_gain=m_attn_out_gain, lru_out_gain=m_lru_out_gain, final_gain=m_final_gain.reshape(1, D))
    v = dict(ln_gain=v_ln_gain, sinks=v_sinks, conv_w=v_conv_w, conv_b=v_conv_b, w_rgate=v_w_rgate,
             b_rgate=v_b_rgate, w_igate=v_w_igate, b_igate=v_b_igate, lru_lambda=v_lru_lambda,
             attn_out_gain=v_attn_out_gain, lru_out_gain=v_lru_out_gain, final_gain=v_final_gain.reshape(1, D))

    conv_blk = jnp.pad(conv_w[0], ((0, 8 - CONVW), (0, 0)))
    wt, cw_all = _all_gather([w_in[0].T, conv_blk], [bf16, f32], "gather_weights")
    conv_full = cw_all.reshape(NDEV, 8, 128)[:, 0:CONVW].transpose(1, 0, 2).reshape(CONVW, LW)

    p = {k: (w[k][0] if k in ("w_rgate", "w_igate") else w[k]) for k in w if k != "conv_w"}
    gx, land_wt, land_wo, land_sm = _sequence_step(x[0], loss_target[0], wt, w_out[0], conv_full, p)

    g_sm = _sum_slots(land_sm, SMALL_PER + 8, "sum_small")
    (g_rep,) = _all_gather([g_sm[0:SMALL_PER]], [f32], "gather_small")
    g_conv = g_sm[SMALL_PER:SMALL_PER + CONVW, 0:128]

    wins = _reduce_adamw(land_wt, w_in[0].T, m_w_in[0].T, v_w_in[0].T, 192, "adamw_w_in")
    g_win, d_win, m_win, v_win = (t.T for t in wins)
    g_wo, d_wo, m_wo, v_wo = _reduce_adamw(land_wo, w_out[0], m_w_out[0], v_w_out[0], 256, "adamw_w_out")
    res = _adamw_small(g_rep, g_conv, w, m, v)
    res["w_in"] = tuple(t[None] for t in (g_win, d_win, m_win, v_win))
    res["w_out"] = tuple(t[None] for t in (g_wo, d_wo, m_wo, v_wo))
    res["final_gain"] = tuple(t.reshape(D) for t in res["final_gain"])

    order = ("ln_gain", "w_in", "sinks", "conv_w", "conv_b", "w_rgate", "b_rgate", "w_igate", "b_igate",
             "lru_lambda", "attn_out_gain", "lru_out_gain", "w_out", "final_gain")
    total_loss = g_rep[ROW_SINKS, LOSS_LANE]
    return (total_loss, gx[None]) + tuple(res[k][i] for i in range(4) for k in order)
```

```python
import jax
import jax.numpy as jnp
from jax import lax
from jax.experimental import pallas as pl
from jax.experimental.pallas import tpu as pltpu

f32 = jnp.float32
bf16 = jnp.bfloat16

D = 1024
HD = 64
NQ = 16
NKV = 4
GROUP = NQ // NKV
KVW = NKV * HD
BLK = 128
ROT = 16
THETA = 500000.0
NEG = -1e30
LW = 1024
NGRP = 4
CONVW = 4
LRU_C = 8.0
NIN = 4608
EPS = 1e-6
NDEV = 8
WT_ROWS = NIN // NDEV
WO_ROWS = 2 * D // NDEV
SMALL_ROWS = 192
SMALL_PER = SMALL_ROWS // NDEV

ADAM_LR = 0.001
ADAM_B1 = 0.9
ADAM_B2 = 0.999
ADAM_EPS = 1e-08
ADAM_WD = 0.01
ADAM_STEP = 10

NT = (((1,), (1,)), ((), ()))
TN = (((0,), (0,)), ((), ()))
MESH = pl.DeviceIdType.MESH
MIB = 1024 * 1024


def _dot(a, b):
    return jnp.dot(a, b, preferred_element_type=f32)


def _dot_nt(a, b):
    return lax.dot_general(a, b, NT, preferred_element_type=f32)


def _dot_tn(a, b):
    return lax.dot_general(a, b, TN, preferred_element_type=f32)


def _params(sem, vmem_mib):
    return pltpu.CompilerParams(dimension_semantics=sem, vmem_limit_bytes=vmem_mib * MIB)


def _sigmoid(x):
    return 0.5 * jnp.tanh(0.5 * x) + 0.5


def _softplus(x):
    return jnp.maximum(x, 0.0) + jnp.log(1.0 + jnp.exp(-jnp.abs(x)))


def _rope_tables(s):
    pos = jnp.arange(s, dtype=f32)
    inv_freq = THETA ** (-jnp.arange(0, ROT, 2, dtype=f32) / ROT)
    ang = pos[:, None] * inv_freq[None, :]
    cs = jnp.concatenate([jnp.cos(ang) - 1.0, jnp.sin(ang)], axis=1)
    d = jnp.arange(128) % HD
    j = jnp.arange(ROT)[:, None]
    pick_c = ((d < ROT) & (j == d % (ROT // 2))).astype(f32)
    pick_sa = ((d >= ROT // 2) & (d < ROT) & (j == d)).astype(f32)
    pick_sb = -((d < ROT // 2) & (j == d + ROT // 2)).astype(f32)
    picks = jnp.concatenate([pick_c, pick_sa, pick_sb], axis=1)
    ones = jnp.concatenate([jnp.ones((1, 128), f32), jnp.zeros((1, 256), f32)], axis=1)
    return jnp.dot(cs, picks, precision=lax.Precision.HIGHEST) + ones


def _tables(tab_ref):
    return tab_ref[:, 0:128], tab_ref[:, 128:256], tab_ref[:, 256:384]


def _rope(t, c, sa, sb):
    return t * c + pltpu.roll(t, 8, 1) * sa + pltpu.roll(t, 120, 1) * sb


def _unrope_t(dr, c, sa, sb):
    return dr * c + pltpu.roll(dr * sa, 120, 0) + pltpu.roll(dr * sb, 8, 0)


def _place():
    return lax.axis_index("x"), lax.axis_index("y"), lax.axis_index("c")


def _gather_ops(mine_refs, out_refs, send_sems, recv_sems, local_sems):
    n = len(mine_refs)
    x, y, c = _place()
    me, sibling = (x, y, c), (x, y, 1 - c)
    chips = [(1 - x, y), (x, 1 - y), (1 - x, 1 - y)]

    def rows(a, dev):
        m = mine_refs[a].shape[0]
        return out_refs[a].at[pl.ds((4 * dev[0] + 2 * dev[1] + dev[2]) * m, m), :]

    def copy(a, k, block, to, own=False):
        return pltpu.make_async_remote_copy(
            src_ref=mine_refs[a] if own else rows(a, block), dst_ref=rows(a, block),
            send_sem=send_sems.at[a, k], recv_sem=recv_sems.at[a, k], device_id=to, device_id_type=MESH)

    def local(a):
        return pltpu.make_async_copy(mine_refs[a], rows(a, me), local_sems.at[a])

    def first(a):
        return [copy(a, 0, me, sibling, own=True)] + [copy(a, 1 + j, me, (*chip, c), own=True)
                                                      for j, chip in enumerate(chips)]

    def start():
        for a in range(n):
            local(a).start()
            for cp in first(a):
                cp.start()

    def finish():
        for j, chip in enumerate(chips):
            for a in range(n):
                copy(a, 1 + j, (*chip, c), me).wait_recv()
                copy(a, 4 + j, (*chip, c), sibling).start()
        for a in range(n):
            copy(a, 0, sibling, me).wait_recv()
            for j, chip in enumerate(chips):
                copy(a, 4 + j, (*chip, 1 - c), me).wait_recv()
        for a in range(n):
            for cp in first(a) + [copy(a, 4 + j, (*chip, c), sibling) for j, chip in enumerate(chips)]:
                cp.wait_send()
            local(a).wait()

    return start, finish


def _relay_gather_ops(mine_refs, out_refs, send_sems, recv_sems, local_sems):
    n = len(mine_refs)
    x, y, c = _place()
    me, sibling = (x, y, c), (x, y, 1 - c)
    near = (x ^ (1 - c), y ^ c)
    far = (x ^ c, y ^ (1 - c))
    diag = (1 - x, 1 - y)

    def rows(a, dev):
        m = mine_refs[a].shape[0]
        return out_refs[a].at[pl.ds((4 * dev[0] + 2 * dev[1] + dev[2]) * m, m), :]

    def copy(a, k, block, to, own=False):
        return pltpu.make_async_remote_copy(
            src_ref=mine_refs[a] if own else rows(a, block), dst_ref=rows(a, block),
            send_sem=send_sems.at[a, k], recv_sem=recv_sems.at[a, k], device_id=to, device_id_type=MESH)

    def local(a):
        return pltpu.make_async_copy(mine_refs[a], rows(a, me), local_sems.at[a])

    def sends(a):
        return [copy(a, 0, me, sibling, own=True), copy(a, 1, me, (*near, c), own=True),
                copy(a, 2, me, (*far, c), own=True), copy(a, 3, (*near, c), (*far, c)),
                copy(a, 4, (*near, c), sibling), copy(a, 5, (*far, c), sibling), copy(a, 6, (*diag, c), sibling)]

    def arrivals(a):
        return [copy(a, 0, sibling, me), copy(a, 1, (*near, c), me), copy(a, 2, (*far, c), me),
                copy(a, 3, (*diag, c), me), copy(a, 4, (*far, 1 - c), me), copy(a, 5, (*near, 1 - c), me),
                copy(a, 6, (*diag, 1 - c), me)]

    def start():
        for a in range(n):
            local(a).start()
            for cp in sends(a)[0:3]:
                cp.start()

    def finish():
        for first, then in ((1, (3, 4)), (2, (5,)), (3, (6,))):
            for a in range(n):
                arrivals(a)[first].wait_recv()
                for k in then:
                    sends(a)[k].start()
        for a in range(n):
            for k in (0, 4, 5, 6):
                arrivals(a)[k].wait_recv()
        for a in range(n):
            for cp in sends(a):
                cp.wait_send()
            local(a).wait()

    return start, finish


def _scatter_ops(src_refs, land_refs, send_sems, recv_sems, local_sems):
    n = len(src_refs)
    x, y, c = _place()
    my = 4 * x + 2 * y + c

    def peer(k):
        return x ^ (k >> 2), y ^ ((k >> 1) & 1), c ^ (k & 1)

    def piece(a, dev):
        m = src_refs[a].shape[0] // NDEV
        return src_refs[a].at[pl.ds(dev * m, m), :]

    def local(a):
        return pltpu.make_async_copy(piece(a, my), land_refs[a].at[my], local_sems.at[a])

    def send(a, k):
        px, py, pc = peer(k)
        return pltpu.make_async_remote_copy(
            src_ref=piece(a, 4 * px + 2 * py + pc), dst_ref=land_refs[a].at[my],
            send_sem=send_sems.at[a, k - 1], recv_sem=recv_sems.at[a, k - 1],
            device_id=(px, py, pc), device_id_type=MESH)

    def arrival(a, k):
        px, py, pc = peer(k)
        return pltpu.make_async_remote_copy(
            src_ref=piece(a, my), dst_ref=land_refs[a].at[4 * px + 2 * py + pc],
            send_sem=send_sems.at[a, k - 1], recv_sem=recv_sems.at[a, k - 1],
            device_id=(px, py, pc), device_id_type=MESH)

    def start():
        for a in range(n):
            local(a).start()
        for k in range(1, NDEV):
            for a in range(n):
                send(a, k).start()

    def finish():
        for k in range(1, NDEV):
            for a in range(n):
                send(a, k).wait_send()
        for k in range(1, NDEV):
            for a in range(n):
                arrival(a, k).wait_recv()
        for a in range(n):
            local(a).wait()

    return start, finish


def _in_hbm(*arrays):
    return tuple(pltpu.with_memory_space_constraint(a, pltpu.HBM) for a in arrays)


def _comm_sems(n):
    return [pltpu.SemaphoreType.DMA((n, 7)), pltpu.SemaphoreType.DMA((n, 7)), pltpu.SemaphoreType.DMA((n,))]


HBM = pl.BlockSpec(memory_space=pltpu.HBM)


def _sink_rows(sinks):
    return jnp.repeat(sinks.reshape(NKV, GROUP), BLK, axis=1)


def _band_softmax(s2_ref, ls, prev_offset, sink_row):
    jj = lax.broadcasted_iota(jnp.int32, (BLK, BLK), 0)
    ii = lax.broadcasted_iota(jnp.int32, (BLK, BLK), 1)
    from_prev = jj > ii
    sc = jnp.where(from_prev, s2_ref[0:BLK, ls] + prev_offset, s2_ref[BLK:2 * BLK, ls])
    m = jnp.maximum(jnp.max(sc, axis=0, keepdims=True), sink_row)
    p = jnp.exp(sc - m)
    es = jnp.exp(sink_row - m)
    inv = 1.0 / (jnp.sum(p, axis=0, keepdims=True) + es)
    return from_prev, p * inv, es * inv


def _put_split(dst_ref, ls, t, from_prev):
    t = t.astype(bf16)
    zero = jnp.zeros_like(t)
    dst_ref[0:BLK, ls] = jnp.where(from_prev, t, zero)
    dst_ref[BLK:2 * BLK, ls] = jnp.where(from_prev, zero, t)


def _heads_side_by_side(ref, h):
    return jnp.concatenate([ref[HD * (GROUP * h + g):HD * (GROUP * h + g) + HD, :] for g in range(GROUP)], axis=1)


def _kv_specs_t():
    prev = pl.BlockSpec((KVW, BLK), lambda n: (0, jnp.maximum(n - 1, 0)))
    cur = pl.BlockSpec((KVW, BLK), lambda n: (0, n))
    return [prev, cur, prev, cur]


def _attn_fwd_t(qt, kt, vt, sinks):
    s = qt.shape[1]

    def body(sink_ref, q_ref, kp_ref, kc_ref, vp_ref, vc_ref, o_ref, s2_scr, pn2_scr):
        n = pl.program_id(0)
        off = jnp.where(n > 0, 0.0, NEG)

        def scores(h):
            hs = slice(HD * h, HD * h + HD)
            kh = jnp.concatenate([kp_ref[hs, :], kc_ref[hs, :]], axis=1)
            s2_scr[h % 2] = _dot_tn(kh, _heads_side_by_side(q_ref, h))

        def probs(h):
            for g in range(GROUP):
                ls = slice(BLK * g, BLK * g + BLK)
                from_prev, pn, _ = _band_softmax(s2_scr.at[h % 2], ls, off, sink_ref[h:h + 1, ls])
                _put_split(pn2_scr.at[h % 2], ls, pn, from_prev)

        def outputs(h):
            hs = slice(HD * h, HD * h + HD)
            vh = jnp.concatenate([vp_ref[hs, :], vc_ref[hs, :]], axis=1)
            og = _dot(vh, pn2_scr[h % 2])
            for g in range(GROUP):
                a = GROUP * h + g
                o_ref[HD * a:HD * a + HD, :] = og[:, BLK * g:BLK * g + BLK]

        scores(0)
        for h in range(NKV):
            if h + 1 < NKV:
                scores(h + 1)
            probs(h)
            outputs(h)

    return pl.pallas_call(
        body, name="attn_fwd", grid=(s // BLK,),
        in_specs=[pl.BlockSpec((NKV, GROUP * BLK), lambda n: (0, 0)), pl.BlockSpec((D, BLK), lambda n: (0, n))]
        + _kv_specs_t(),
        out_specs=pl.BlockSpec((D, BLK), lambda n: (0, n)),
        out_shape=pltpu.HBM((D, s), f32),
        scratch_shapes=[pltpu.VMEM((2, 2 * BLK, GROUP * BLK), f32), pltpu.VMEM((2, 2 * BLK, GROUP * BLK), bf16)],
        compiler_params=_params(("arbitrary",), 32),
    )(_sink_rows(sinks), *_in_hbm(qt, kt, kt, vt, vt))


def _attn_bwd_t(qt, kt, vt, dot, sinks, dwo):
    s = qt.shape[1]
    nb = s // BLK

    def body(sink_ref, q_ref, do_ref, kp_ref, kc_ref, vp_ref, vc_ref, dwo_ref, dq_ref, dk_ref, dv_ref, ds_ref,
             land_ref, dk_hold, dv_hold, s2_scr, dp2_scr, pn2_scr, ds2_scr, send_sems, recv_sems, local_sems):
        n = pl.program_id(0)
        start, finish = _scatter_ops([dwo_ref], [land_ref], send_sems, recv_sems, local_sems)

        @pl.when(n == 0)
        def _():
            start()
            dk_hold[...] = jnp.zeros_like(dk_hold)
            dv_hold[...] = jnp.zeros_like(dv_hold)
            ds_ref[...] = jnp.zeros_like(ds_ref)

        @pl.when(n < nb)
        def _():
            off = jnp.where(n > 0, 0.0, NEG)

            def scores(h):
                hs = slice(HD * h, HD * h + HD)
                kh = jnp.concatenate([kp_ref[hs, :], kc_ref[hs, :]], axis=1)
                vh = jnp.concatenate([vp_ref[hs, :], vc_ref[hs, :]], axis=1)
                s2_scr[h % 2] = _dot_tn(kh, _heads_side_by_side(q_ref, h))
                dp2_scr[h % 2] = _dot_tn(vh, _heads_side_by_side(do_ref, h))

            def softmax_bwd(h):
                for g in range(GROUP):
                    ls = slice(BLK * g, BLK * g + BLK)
                    from_prev, pn, ps = _band_softmax(s2_scr.at[h % 2], ls, off, sink_ref[h:h + 1, ls])
                    dp = jnp.where(from_prev, dp2_scr[h % 2, 0:BLK, ls], dp2_scr[h % 2, BLK:2 * BLK, ls])
                    dsum = jnp.sum(pn * dp, axis=0, keepdims=True)
                    ds_ref[h:h + 1, ls] += -ps * dsum
                    _put_split(pn2_scr.at[h % 2], ls, pn, from_prev)
                    _put_split(ds2_scr.at[h % 2], ls, pn * (dp - dsum), from_prev)

            def grads(h):
                hs = slice(HD * h, HD * h + HD)
                kh = jnp.concatenate([kp_ref[hs, :], kc_ref[hs, :]], axis=1)
                dqg = _dot(kh, ds2_scr[h % 2])
                for g in range(GROUP):
                    a = GROUP * h + g
                    dq_ref[HD * a:HD * a + HD, :] = dqg[:, BLK * g:BLK * g + BLK]
                dkh = _dot_nt(_heads_side_by_side(q_ref, h), ds2_scr[h % 2])
                dvh = _dot_nt(_heads_side_by_side(do_ref, h), pn2_scr[h % 2])
                dk_ref[hs, :] = dk_hold[hs, :] + dkh[:, 0:BLK]
                dv_ref[hs, :] = dv_hold[hs, :] + dvh[:, 0:BLK]
                dk_hold[hs, :] = dkh[:, BLK:2 * BLK]
                dv_hold[hs, :] = dvh[:, BLK:2 * BLK]

            scores(0)
            for h in range(NKV):
                if h + 1 < NKV:
                    scores(h + 1)
                softmax_bwd(h)
                grads(h)

        @pl.when(n == nb)
        def _():
            dk_ref[...] = dk_hold[...]
            dv_ref[...] = dv_hold[...]
            finish()

    blk = pl.BlockSpec((D, BLK), lambda n: (0, jnp.minimum(n, nb - 1)))
    late = pl.BlockSpec((KVW, BLK), lambda n: (0, jnp.maximum(n - 1, 0)))
    whole = pl.BlockSpec((NKV, GROUP * BLK), lambda n: (0, 0))
    kv = [pl.BlockSpec((KVW, BLK), lambda n: (0, jnp.clip(n - 1, 0, nb - 1))),
          pl.BlockSpec((KVW, BLK), lambda n: (0, jnp.minimum(n, nb - 1)))]
    return pl.pallas_call(
        body, name="attn_bwd", grid=(nb + 1,),
        in_specs=[whole, blk, blk] + kv + kv + [HBM],
        out_specs=[blk, late, late, whole, HBM],
        out_shape=[pltpu.HBM((D, s), f32), pltpu.HBM((KVW, s), f32), pltpu.HBM((KVW, s), f32),
                   jax.ShapeDtypeStruct((NKV, GROUP * BLK), f32), pltpu.HBM((NDEV, WO_ROWS, D), bf16)],
        scratch_shapes=[pltpu.VMEM((KVW, BLK), f32), pltpu.VMEM((KVW, BLK), f32)]
        + [pltpu.VMEM((2, 2 * BLK, GROUP * BLK), f32)] * 2 + [pltpu.VMEM((2, 2 * BLK, GROUP * BLK), bf16)] * 2
        + _comm_sems(1),
        compiler_params=_params(("arbitrary",), 48),
    )(_sink_rows(sinks), *_in_hbm(qt, dot, kt, kt, vt, vt, dwo))


def _block_diag(w):
    w4 = w.reshape(NGRP, 4, HD, HD)
    eye = jnp.eye(4, dtype=w.dtype)
    return jnp.einsum('gjcd,jk->gjckd', w4, eye).reshape(NGRP, 256, 256).astype(bf16)


def _gate_terms(pr, pi, br, bi, sp):
    r = _sigmoid(pr + br)
    i = _sigmoid(pi + bi)
    la = -LRU_C * r * sp
    a = jnp.exp(la)
    x2 = 2.0 * la
    y = jnp.where(x2 > -0.02, -x2 * (1.0 + x2 * (0.5 + x2 * (1.0 / 6.0))), 1.0 - a * a)
    inv_mult = lax.rsqrt(jnp.maximum(y, 1e-30))
    return r, i, a, y * inv_mult, inv_mult


def _later(x, before, k):
    if k == 0:
        return x
    row = lax.broadcasted_iota(jnp.int32, before.shape, 0)
    rolled = pltpu.roll(x, k, 0)
    first = jnp.where(row < k, pltpu.roll(before, k, 0), rolled[0:8])
    return jnp.concatenate([first, rolled[8:]], axis=0)


def _earlier(x, after, k):
    if k == 0:
        return x
    n = x.shape[0]
    row = lax.broadcasted_iota(jnp.int32, after.shape, 0)
    rolled = pltpu.roll(x, n - k, 0)
    last = jnp.where(row >= 8 - k, pltpu.roll(after, 8 - k, 0), rolled[n - 8:n])
    return jnp.concatenate([rolled[0:n - 8], last], axis=0)


def _fwd_fused(x, ln_gain, wt, tabs, wo_shard, conv_w, conv_b, wr, wi, br, bi, lam, tm):
    s = x.shape[0]
    nt = s // tm
    nc = 512
    pieces = 8
    rows_per = tm // pieces
    later_chunks = (0, 1, 2, 3, 4, 7, 8)

    def body(x0_ref, xn_ref, g_ref, wt_ref, tab_ref, wo_ref, cw_ref, cb_ref, wr_ref, wi_ref, br_ref,
             bi_ref, lam_ref, h_ref, q_ref, k_ref, v_ref, ga_ref, xl_ref, gl_ref, u_ref, hl_ref, r_ref, ig_ref, a_ref,
             im_ref, wo_all, wo_stage, hb, halo, ub_scr, pr_scr, pi_scr, b_scr, hcar,
             send_sems, recv_sems, local_sems):
        i = pl.program_id(0)
        start, finish = _gather_ops([wo_stage], [wo_all], send_sems, recv_sems, local_sems)
        gain = g_ref[...]

        def normed(xx):
            rstd = lax.rsqrt(jnp.mean(xx * xx, axis=-1, keepdims=True) + EPS)
            return (xx * rstd * gain).astype(bf16)

        @pl.when(i == 0)
        def _():
            wo_stage[...] = wo_ref[...].astype(bf16)
            start()
            hb[0] = normed(x0_ref[...])
            halo[...] = jnp.zeros_like(halo)
            hcar[...] = jnp.zeros_like(hcar)

        cur, nxt = i % 2, (i + 1) % 2
        sp = _softplus(-lam_ref[...])
        br, bi = br_ref[...], bi_ref[...]
        c, sa, sb = _tables(tab_ref)
        piece_rows = lambda p: slice(rows_per * p, rows_per * p + rows_per)

        def project(ci):
            z = _dot_nt(hb[cur], wt_ref[ci * nc:(ci + 1) * nc, :])
            if ci < 2:
                for j in range(nc // 128):
                    r = _rope(z[:, 128 * j:128 * j + 128], c, sa, sb) * (HD ** -0.5)
                    q_ref[ci * nc + 128 * j:ci * nc + 128 * j + 128, :] = r.astype(bf16).T
            elif ci == 2:
                for j in range(2):
                    js = slice(128 * j, 128 * j + 128)
                    k_ref[js, :] = _rope(z[:, js], c, sa, sb).astype(bf16).T
                    v_ref[js, :] = z[:, KVW + 128 * j:KVW + 128 * j + 128].astype(bf16).T
            else:
                sec, j = divmod(ci - 3, 2)
                (ga_ref, xl_ref, gl_ref)[sec][:, j * nc:(j + 1) * nc] = z

        def gate_terms(p):
            rows = piece_rows(p)
            r, ig, a, mult, inv_mult = _gate_terms(pr_scr[rows, :], pi_scr[rows, :], br, bi, sp)
            r_ref[rows, :] = r
            ig_ref[rows, :] = ig
            a_ref[rows, :] = a
            im_ref[rows, :] = inv_mult
            b_scr[rows, :] = mult * (ig * u_ref[rows, :])

        def scan(p, hc):
            for t in range(rows_per * p, rows_per * p + rows_per):
                hc = a_ref[t:t + 1, :] * hc + b_scr[t:t + 1, :]
                hl_ref[t:t + 1, :] = hc
            return hc

        def norm_next(p):
            hb[nxt, piece_rows(p), :] = normed(xn_ref[piece_rows(p), :])

        h_ref[...] = hb[cur]
        project(5)
        project(6)
        xl = xl_ref[...]
        u = cb_ref[...] + sum(cw_ref[k:k + 1, :] * _later(xl, halo[...], CONVW - 1 - k) for k in range(CONVW))
        halo[...] = xl[tm - 8:tm, :]
        u_ref[...] = u
        ub_scr[...] = u.astype(bf16)
        for g in range(NGRP):
            gs = slice(256 * g, 256 * g + 256)
            pr_scr[:, gs] = _dot(ub_scr[:, gs], wr_ref[g])
            pi_scr[:, gs] = _dot(ub_scr[:, gs], wi_ref[g])
        hc = hcar[...]
        gate_terms(0)
        for slot, ci in enumerate(later_chunks):
            project(ci)
            norm_next(slot)
            gate_terms(slot + 1)
            hc = scan(slot, hc)
        norm_next(pieces - 1)
        hcar[...] = scan(pieces - 1, hc)

        @pl.when(i == nt - 1)
        def _():
            finish()

    row = lambda w: pl.BlockSpec((tm, w), lambda i: (i, 0))
    col = lambda w: pl.BlockSpec((w, tm), lambda i: (0, i))
    full = lambda a: pl.BlockSpec(a.shape, lambda i: (0,) * a.ndim)
    big = lambda w, dt: pltpu.HBM((s, w), dt)
    tile = pltpu.VMEM((tm, LW), f32)
    return pl.pallas_call(
        body, name="fwd_fused", grid=(nt,),
        in_specs=[pl.BlockSpec((tm, D), lambda i: (0, 0)), pl.BlockSpec((tm, D), lambda i: (jnp.minimum(i + 1, nt - 1), 0)),
                  full(ln_gain), full(wt), row(384), full(wo_shard), full(conv_w), full(conv_b),
                  full(wr), full(wi), full(br), full(bi), full(lam)],
        out_specs=[row(D), col(D), col(KVW), col(KVW), row(D), row(D), row(D)] + [row(LW)] * 6 + [HBM],
        out_shape=[big(D, bf16), pltpu.HBM((D, s), bf16), pltpu.HBM((KVW, s), bf16), pltpu.HBM((KVW, s), bf16),
                   big(D, f32), big(D, f32), big(D, f32)] + [big(LW, f32)] * 6 + [pltpu.HBM((2 * D, D), bf16)],
        scratch_shapes=[pltpu.VMEM((WO_ROWS, D), bf16), pltpu.VMEM((2, tm, D), bf16), pltpu.VMEM((8, LW), f32),
                        pltpu.VMEM((tm, LW), bf16), tile, tile, tile, pltpu.VMEM((1, LW), f32)] + _comm_sems(1),
        compiler_params=_params(("arbitrary",), 56),
    )(*_in_hbm(x, x), ln_gain, *_in_hbm(wt), tabs, wo_shard, conv_w, conv_b, wr, wi, br, bi, lam)


def _lru_bwd(u, hl, dhl, xl, r, ig, a, im, conv_w, wr, wi, lam, tm):
    s = u.shape[0]
    nt = s // tm
    pieces = 8
    rows_per = tm // pieces

    def body(u_ref, h_ref, hp_ref, dh_ref, x_ref, xp_ref, r_ref, ig_ref, a_ref, im_ref, cw_ref, wr_ref, wi_ref,
             lam_ref, dxl_ref, dwr_ref, dwi_ref, dbr_ref, dbi_ref, dlam_ref, dcb_ref, dcw_ref,
             l_scr, du_scr, dpr_scr, dpi_scr, lcar, dunext):
        t0 = pl.program_id(0)
        tile = nt - 1 - t0

        @pl.when(t0 == 0)
        def _():
            lcar[...] = jnp.zeros_like(lcar)
            dunext[...] = jnp.zeros_like(dunext)
            for ref in (dwr_ref, dwi_ref, dbr_ref, dbi_ref, dlam_ref, dcb_ref, dcw_ref):
                ref[...] = jnp.zeros_like(ref)

        lam = lam_ref[...]
        sp = _softplus(-lam)
        hp = jnp.where(tile > 0, hp_ref[...], 0.0)

        def scan(p, c):
            for t in range(rows_per * p + rows_per - 1, rows_per * p - 1, -1):
                lt = dh_ref[t:t + 1, :] + c
                l_scr[t:t + 1, :] = lt
                c = a_ref[t:t + 1, :] * lt
            return c

        def terms(p, sums):
            rows = slice(rows_per * p, rows_per * p + rows_per)
            lt, u, r, i, a, inv_mult = l_scr[rows, :], u_ref[rows, :], r_ref[rows, :], ig_ref[rows, :], \
                a_ref[rows, :], im_ref[rows, :]
            before = hp if p == 0 else h_ref[rows_per * p - 8:rows_per * p, :]
            hprev = _later(h_ref[rows, :], before, 1)
            x2 = -2.0 * LRU_C * r * sp
            mult = jnp.where(x2 > -0.02, -x2 * (1.0 + x2 * (0.5 + x2 * (1.0 / 6.0))), 1.0 - a * a) * inv_mult
            da = lt * hprev
            dmult = lt * (i * u)
            di = lt * mult * u
            du_scr[rows, :] = lt * mult * i
            dla = da * a - dmult * (a * a) * inv_mult
            dr = dla * (-LRU_C * sp)
            dpr = dr * r * (1.0 - r)
            dpi = di * i * (1.0 - i)
            dpr_scr[rows, :] = dpr.astype(bf16)
            dpi_scr[rows, :] = dpi.astype(bf16)
            col = lambda t: jnp.sum(t, axis=0, keepdims=True)
            return sums[0] + col(dla * (-LRU_C * r)), sums[1] + col(dpr), sums[2] + col(dpi)

        sums = (jnp.zeros((1, LW), f32),) * 3
        c = scan(pieces - 1, lcar[...])
        for p in range(pieces - 1, -1, -1):
            if p > 0:
                c = scan(p - 1, c)
            sums = terms(p, sums)
        lcar[...] = c
        dlam_ref[...] += sums[0]
        dbr_ref[...] += sums[1]
        dbi_ref[...] += sums[2]

        ub = u_ref[...].astype(bf16)
        dug = []
        for g in range(NGRP):
            gs = slice(256 * g, 256 * g + 256)
            dwr_ref[g] += _dot_tn(ub[:, gs], dpr_scr[:, gs])
            dwi_ref[g] += _dot_tn(ub[:, gs], dpi_scr[:, gs])
            dug.append(_dot_nt(dpr_scr[:, gs], wr_ref[g]) + _dot_nt(dpi_scr[:, gs], wi_ref[g]))
        du = du_scr[...] + jnp.concatenate(dug, axis=1)

        dcb_ref[...] += jnp.sum(du, axis=0, keepdims=True)
        x = x_ref[...]
        before = jnp.where(tile > 0, xp_ref[...], 0.0)
        for k in range(CONVW):
            dcw_ref[k:k + 1, :] += jnp.sum(du * _later(x, before, CONVW - 1 - k), axis=0, keepdims=True)
        after = dunext[...]
        dxl = sum(cw_ref[k:k + 1, :] * _earlier(du, after, CONVW - 1 - k) for k in range(CONVW))
        dxl_ref[...] = dxl.astype(bf16)
        dunext[...] = du[0:8, :]

        @pl.when(t0 == nt - 1)
        def _():
            dlam_ref[...] = dlam_ref[...] * (-_sigmoid(-lam))

    rev = lambda i: (nt - 1 - i, 0)
    row = pl.BlockSpec((tm, LW), rev)
    prev8 = pl.BlockSpec((8, LW), lambda i: (jnp.maximum((nt - 1 - i) * (tm // 8) - 1, 0), 0))
    full = lambda a: pl.BlockSpec(a.shape, lambda i: (0,) * a.ndim)
    vec = pl.BlockSpec((1, LW), lambda i: (0, 0))
    bd = pl.BlockSpec((NGRP, 256, 256), lambda i: (0, 0, 0))
    return pl.pallas_call(
        body, name="lru_bwd", grid=(nt,),
        in_specs=[row, row, prev8, row, row, prev8, row, row, row, row, full(conv_w), full(wr), full(wi), full(lam)],
        out_specs=[row, bd, bd, vec, vec, vec, vec, pl.BlockSpec((CONVW, LW), lambda i: (0, 0))],
        out_shape=[pltpu.HBM((s, LW), bf16),
                   jax.ShapeDtypeStruct((NGRP, 256, 256), f32), jax.ShapeDtypeStruct((NGRP, 256, 256), f32),
                   jax.ShapeDtypeStruct((1, LW), f32), jax.ShapeDtypeStruct((1, LW), f32),
                   jax.ShapeDtypeStruct((1, LW), f32), jax.ShapeDtypeStruct((1, LW), f32),
                   jax.ShapeDtypeStruct((CONVW, LW), f32)],
        scratch_shapes=[pltpu.VMEM((tm, LW), f32), pltpu.VMEM((tm, LW), f32), pltpu.VMEM((tm, LW), bf16),
                        pltpu.VMEM((tm, LW), bf16), pltpu.VMEM((1, LW), f32), pltpu.VMEM((8, LW), f32)],
        compiler_params=_params(("arbitrary",), 56),
    )(*_in_hbm(u, hl, hl, dhl, xl, xl, r, ig, a, im), conv_w, wr, wi, lam)


def _gated_norm(t, gate, gain):
    sg = _sigmoid(gate)
    silu = gate * sg
    p = t * silu
    rstd = lax.rsqrt(jnp.mean(p * p, axis=-1, keepdims=True) + EPS)
    ph = p * rstd
    return sg, silu, rstd, ph, ph * gain


def _gated_norm_bwd(dy, t, gate, gain, sg, silu, rstd, ph):
    w = dy * gain
    dp = rstd * (w - ph * jnp.mean(w * ph, axis=-1, keepdims=True))
    dgate = dp * t * (sg * (1.0 + gate * (1.0 - sg)))
    return jnp.sum(dy * ph, axis=0, keepdims=True), dp * silu, dgate


def _out_fwd_bwd(x, tgt, o, ga, hl, gl, again, lgain, fgain, wo, tm):
    s = x.shape[0]
    nt = s // tm

    def body(x_ref, t_ref, o_ref, ga_ref, hl_ref, gl_ref, ag_ref, lg_ref, fg_ref, wo_ref,
             dx2_ref, do_ref, dga_ref, dhl_ref, dgl_ref, dwo_ref, gfg_ref, gag_ref, glg_ref, loss_ref, acc):
        i = pl.program_id(0)

        @pl.when(i == 0)
        def _():
            acc[...] = jnp.zeros_like(acc)
            for ref in (gfg_ref, gag_ref, glg_ref, loss_ref):
                ref[...] = jnp.zeros_like(ref)

        oo = jnp.concatenate([o_ref[128 * j:128 * j + 128, :].T for j in range(D // 128)], axis=1)
        gga, hh, ggl = ga_ref[...], hl_ref[...], gl_ref[...]
        ag, lg, fg = ag_ref[...], lg_ref[...], fg_ref[...]
        sga, silua, ra, pah, ya = _gated_norm(oo, gga, ag)
        sgl, silul, rl, plh, yl = _gated_norm(hh, ggl, lg)
        yab, ylb = ya.astype(bf16), yl.astype(bf16)
        y = _dot(yab, wo_ref[0:D, :]) + _dot(ylb, wo_ref[D:2 * D, :])
        x2 = x_ref[...] + y
        r2 = lax.rsqrt(jnp.mean(x2 * x2, axis=-1, keepdims=True) + EPS)
        x2h = x2 * r2
        err = x2h * fg - t_ref[...]
        loss_ref[...] += 0.5 * jnp.sum(jnp.sum(err * err, axis=-1, keepdims=True) * (1.0 / D))
        dout = err * (1.0 / D)
        gfg_ref[...] += jnp.sum(dout * x2h, axis=0, keepdims=True)
        w = dout * fg
        dx2 = r2 * (w - x2h * jnp.mean(w * x2h, axis=-1, keepdims=True))
        dx2_ref[...] = dx2
        dyb = dx2.astype(bf16)
        acc[0:D, :] += _dot_tn(yab, dyb)
        acc[D:2 * D, :] += _dot_tn(ylb, dyb)
        dya = _dot_nt(dyb, wo_ref[0:D, :])
        dyl = _dot_nt(dyb, wo_ref[D:2 * D, :])
        gag, do, dga = _gated_norm_bwd(dya, oo, gga, ag, sga, silua, ra, pah)
        glg, dhl, dgl = _gated_norm_bwd(dyl, hh, ggl, lg, sgl, silul, rl, plh)
        gag_ref[...] += gag
        glg_ref[...] += glg
        dob = do.astype(bf16)
        for j in range(D // 128):
            do_ref[128 * j:128 * j + 128, :] = dob[:, 128 * j:128 * j + 128].T
        dga_ref[...] = dga.astype(bf16)
        dhl_ref[...] = dhl
        dgl_ref[...] = dgl.astype(bf16)

        @pl.when(i == nt - 1)
        def _():
            dwo_ref[...] = acc[...].astype(bf16)

    row = pl.BlockSpec((tm, D), lambda i: (i, 0))
    col = pl.BlockSpec((D, tm), lambda i: (0, i))
    vec = pl.BlockSpec((1, D), lambda i: (0, 0))
    mat = pl.BlockSpec((2 * D, D), lambda i: (0, 0))
    return pl.pallas_call(
        body, name="out_fwd_bwd", grid=(nt,),
        in_specs=[row, row, col, row, row, row] + [vec] * 3 + [mat],
        out_specs=[row, col, row, row, row] + [mat, vec, vec, vec, pl.BlockSpec((1, 128), lambda i: (0, 0))],
        out_shape=[pltpu.HBM((s, D), f32), pltpu.HBM((D, s), bf16),
                   pltpu.HBM((s, D), bf16), pltpu.HBM((s, D), f32),
                   pltpu.HBM((s, D), bf16), pltpu.HBM((2 * D, D), bf16),
                   jax.ShapeDtypeStruct((1, D), f32), jax.ShapeDtypeStruct((1, D), f32),
                   jax.ShapeDtypeStruct((1, D), f32), jax.ShapeDtypeStruct((1, 128), f32)],
        scratch_shapes=[pltpu.VMEM((2 * D, D), f32)],
        compiler_params=_params(("arbitrary",), 56),
    )(*_in_hbm(x, tgt, o, ga, hl, gl), again, lgain, fgain, *_in_hbm(wo))


def _bwd_in(x, dx2, dq, dk, dv, dga, dxl, dgl, ln_gain, wt, tabs, tm):
    s = x.shape[0]

    def body(x_ref, dx2_ref, dq_ref, dk_ref, dv_ref, dga_ref, dxl_ref, dgl_ref, g_ref, wt_ref,
             tab_ref, gx_ref, gln_ref, dzt_ref):
        @pl.when(pl.program_id(0) == 0)
        def _():
            gln_ref[...] = jnp.zeros_like(gln_ref)

        c, sa, sb = (t.T for t in _tables(tab_ref))
        for j in range(D // 128):
            js = slice(128 * j, 128 * j + 128)
            dzt_ref[js, :] = (_unrope_t(dq_ref[js, :], c, sa, sb) * (HD ** -0.5)).astype(bf16)
        for j in range(KVW // 128):
            js = slice(128 * j, 128 * j + 128)
            dzt_ref[D + 128 * j:D + 128 * j + 128, :] = _unrope_t(dk_ref[js, :], c, sa, sb).astype(bf16)
        dzt_ref[D + KVW:D + 2 * KVW, :] = dv_ref[...].astype(bf16)
        first = D + 2 * KVW
        dh = _dot_tn(dzt_ref[0:512, :], wt_ref[0:512, :])
        for ci in range(1, first // 512):
            dh = dh + _dot_tn(dzt_ref[512 * ci:512 * ci + 512, :], wt_ref[512 * ci:512 * ci + 512, :])
        for sec, ref in enumerate((dga_ref, dxl_ref, dgl_ref)):
            for j in range(D // 512):
                rows = slice(first + D * sec + 512 * j, first + D * sec + 512 * j + 512)
                dh = dh + _dot(ref[:, 512 * j:512 * j + 512], wt_ref[rows, :])
            for j in range(D // 128):
                dzt_ref[first + D * sec + 128 * j:first + D * sec + 128 * j + 128, :] = ref[:, 128 * j:128 * j + 128].T
        xx = x_ref[...]
        rstd = lax.rsqrt(jnp.mean(xx * xx, axis=-1, keepdims=True) + EPS)
        xh = xx * rstd
        gln_ref[...] += jnp.sum(dh * xh, axis=0, keepdims=True)
        w = dh * g_ref[...]
        gx_ref[...] = dx2_ref[...] + rstd * (w - xh * jnp.mean(w * xh, axis=-1, keepdims=True))

    row = lambda w: pl.BlockSpec((tm, w), lambda i: (i, 0))
    col = lambda w: pl.BlockSpec((w, tm), lambda i: (0, i))
    full = lambda a: pl.BlockSpec(a.shape, lambda i: (0, 0))
    return pl.pallas_call(
        body, name="bwd_in", grid=(s // tm,),
        in_specs=[row(D), row(D), col(D), col(KVW), col(KVW), row(D), row(D), row(D), full(ln_gain), full(wt),
                  row(384)],
        out_specs=[row(D), pl.BlockSpec((1, D), lambda i: (0, 0)), col(NIN)],
        out_shape=[pltpu.HBM((s, D), f32), jax.ShapeDtypeStruct((1, D), f32),
                   pltpu.HBM((NIN, s), bf16)],
        compiler_params=_params(("arbitrary",), 56),
    )(*_in_hbm(x, dx2, dq, dk, dv, dga, dxl, dgl), ln_gain, *_in_hbm(wt), tabs)


WT_TERMS = 5


def _dwt_scatter(dzt, h, small, tm):
    s = h.shape[0]
    nk = s // tm
    srows = small.shape[0] // NDEV
    last = NDEV - 1

    def body(order_ref, dz_ref, h_ref, sm_ref, lwt_ref, lsm_ref, acc, stage, given, send_sems, recv_sems, local_sem,
             sm_send, sm_recv, sm_local):
        j, k = pl.program_id(0), pl.program_id(1)
        x, y, c = _place()
        sibling = (x, y, 1 - c)
        chips = [(1 - x, 1 - y), (1 - x, y), (x, 1 - y)]
        sm_start, sm_finish = _scatter_ops([sm_ref], [lsm_ref], sm_send, sm_recv, sm_local)

        def send(step):
            if step == last - 1:
                dst, to = lwt_ref.at[1], sibling
            elif step % 2 == 0:
                dst, to = given.at[step // 2], sibling
            else:
                dst, to = lwt_ref.at[2 + step // 2], (*chips[step // 2], c)
            return pltpu.make_async_remote_copy(
                src_ref=stage.at[step % 2], dst_ref=dst, send_sem=send_sems.at[step], recv_sem=recv_sems.at[step],
                device_id=to, device_id_type=MESH)

        def keep():
            return pltpu.make_async_copy(stage.at[last % 2], lwt_ref.at[0], local_sem)

        @pl.when((j == 0) & (k == 0))
        def _():
            sm_start()

        @pl.when(k == 0)
        def _():
            acc[...] = jnp.zeros_like(acc)

        acc[...] += _dot(dz_ref[...], h_ref[...])

        for step in range(NDEV):
            @pl.when((k == nk - 1) & (j == step))
            def _(step=step):
                if step >= 2:
                    send(step - 2).wait_send()
                if step % 2 == 1 and step < last:
                    send(step - 1).wait_recv()
                    stage[step % 2] = (acc[...] + given[step // 2].astype(f32)).astype(bf16)
                else:
                    stage[step % 2] = acc[...].astype(bf16)
                if step < last:
                    send(step).start()
                else:
                    keep().start()
                    send(last - 1).wait_send()
                    for peer_step in (1, 3, 5, last - 1):
                        send(peer_step).wait_recv()
                    keep().wait()
                    sm_finish()

    x, y, c = _place()
    dest = lambda cx, cy, cc: 4 * cx + 2 * cy + cc
    order = jnp.stack([dest(1 - x, 1 - y, 1 - c), dest(1 - x, 1 - y, c), dest(1 - x, y, 1 - c), dest(1 - x, y, c),
                       dest(x, 1 - y, 1 - c), dest(x, 1 - y, c), dest(x, y, 1 - c), dest(x, y, c)])
    return pl.pallas_call(
        body, name="dwt_scatter",
        grid_spec=pltpu.PrefetchScalarGridSpec(
            num_scalar_prefetch=1, grid=(NDEV, nk),
            in_specs=[pl.BlockSpec((WT_ROWS, tm), lambda j, k, order: (order[j], k)),
                      pl.BlockSpec((tm, D), lambda j, k, order: (k, 0)), HBM],
            out_specs=[HBM, HBM],
            scratch_shapes=[pltpu.VMEM((WT_ROWS, D), f32), pltpu.VMEM((2, WT_ROWS, D), bf16),
                            pltpu.VMEM((3, WT_ROWS, D), bf16),
                            pltpu.SemaphoreType.DMA((last,)), pltpu.SemaphoreType.DMA((last,)),
                            pltpu.SemaphoreType.DMA(())] + _comm_sems(1)),
        out_shape=[pltpu.HBM((WT_TERMS, WT_ROWS, D), bf16), pltpu.HBM((NDEV, srows, D), f32)],
        compiler_params=_params(("arbitrary", "arbitrary"), 32),
    )(order, *_in_hbm(dzt, h, small))


def _diag_blocks(bd):
    eye = jnp.eye(4, dtype=bd.dtype)
    return jnp.einsum('gjckd,jk->gjcd', bd.reshape(NGRP, 4, HD, 4, HD), eye).reshape(NQ, HD, HD)


def _sequence_step(x, tgt, wt, wo_shard, conv_w, p):
    s = x.shape[0]
    tm = min(256, s)
    tabs = _rope_tables(s)
    wr, wi = _block_diag(p["w_rgate"]), _block_diag(p["w_igate"])
    sinks = p["sinks"].reshape(NQ)
    h, qt, kt, vt, ga, xl, gl, u, hl, r, ig, a, im, wo = _fwd_fused(
        x, p["ln_gain"], wt, tabs, wo_shard, conv_w, p["conv_b"], wr, wi, p["b_rgate"], p["b_igate"],
        p["lru_lambda"], tm)
    ot = _attn_fwd_t(qt, kt, vt, sinks)
    dx2, dot, dga, dhl, dgl, dwo, g_fg, g_ag, g_lg, loss = _out_fwd_bwd(
        x, tgt, ot, ga, hl, gl, p["attn_out_gain"], p["lru_out_gain"], p["final_gain"], wo, tm)
    dqt, dkt, dvt, dsink, land_wo = _attn_bwd_t(qt, kt, vt, dot, sinks, dwo)
    dxl, dwr, dwi, dbr, dbi, dlam, dcb, dcw = _lru_bwd(u, hl, dhl, xl, r, ig, a, im, conv_w, wr, wi, p["lru_lambda"], tm)
    gx, g_ln, dzt = _bwd_in(x, dx2, dqt, dkt, dvt, dga, dxl, dgl, p["ln_gain"], wt, tabs, tm)
    small = dict(ln_gain=g_ln, sinks=dsink.reshape(NQ, BLK).sum(axis=1)[None], conv_w=dcw, conv_b=dcb,
                 w_rgate=_diag_blocks(dwr), b_rgate=dbr, w_igate=_diag_blocks(dwi), b_igate=dbi, lru_lambda=dlam,
                 attn_out_gain=g_ag, lru_out_gain=g_lg, final_gain=g_fg)
    land_wt, land_sm = _dwt_scatter(dzt, h, _pack_small(small, loss), min(1024, s))
    return gx, land_wt, land_wo, land_sm


def _all_gather(srcs, out_dtypes, name):
    n = len(srcs)
    cast = [a.dtype != dt for a, dt in zip(srcs, out_dtypes)]

    def body(*refs):
        src_refs, out_refs = refs[:n], refs[n:2 * n]
        stage_refs = list(refs[2 * n:2 * n + sum(cast)])
        mine_refs = []
        for a in range(n):
            if cast[a]:
                st = stage_refs.pop(0)
                st[...] = src_refs[a][...].astype(out_dtypes[a])
                mine_refs.append(st)
            else:
                mine_refs.append(src_refs[a])
        start, finish = _relay_gather_ops(mine_refs, out_refs, *refs[-3:])
        start()
        finish()

    vmem = pl.BlockSpec(memory_space=pltpu.VMEM)
    return pl.pallas_call(
        body, name=name,
        in_specs=[vmem] * n, out_specs=[HBM] * n,
        out_shape=[pltpu.HBM((NDEV * a.shape[0], a.shape[1]), dt) for a, dt in zip(srcs, out_dtypes)],
        scratch_shapes=[pltpu.VMEM(a.shape, dt) for a, dt, cst in zip(srcs, out_dtypes, cast) if cst] + _comm_sems(n),
        compiler_params=pltpu.CompilerParams(vmem_limit_bytes=32 * MIB),
    )(*srcs)


def _sum_slots(land, tr, name):
    terms, rows, cols = land.shape

    def body(l_ref, o_ref):
        acc = l_ref[0].astype(f32)
        for d in range(1, terms):
            acc = acc + l_ref[d].astype(f32)
        o_ref[...] = acc

    return pl.pallas_call(
        body, name=name, grid=(rows // tr,),
        in_specs=[pl.BlockSpec((terms, tr, cols), lambda i: (0, i, 0))],
        out_specs=pl.BlockSpec((tr, cols), lambda i: (i, 0)),
        out_shape=jax.ShapeDtypeStruct((rows, cols), f32),
        compiler_params=_params(("arbitrary",), 32),
    )(*_in_hbm(land))


def _adam_math(w, g, m, v):
    m2 = ADAM_B1 * m + (1.0 - ADAM_B1) * g
    v2 = ADAM_B2 * v + (1.0 - ADAM_B2) * (g * g)
    m_hat = m2 / (1.0 - ADAM_B1 ** ADAM_STEP)
    v_hat = v2 / (1.0 - ADAM_B2 ** ADAM_STEP)
    delta = -ADAM_LR * (m_hat / (jnp.sqrt(v_hat) + ADAM_EPS) + ADAM_WD * w)
    return delta, m2, v2


def _reduce_adamw(land, w, m, v, tr, name):
    terms, rows, cols = land.shape

    def body(l_ref, w_ref, m_ref, v_ref, g_ref, d_ref, m2_ref, v2_ref):
        g = l_ref[0].astype(f32)
        for t in range(1, terms):
            g = g + l_ref[t].astype(f32)
        g_ref[...] = g
        d_ref[...], m2_ref[...], v2_ref[...] = _adam_math(w_ref[...], g, m_ref[...], v_ref[...])

    blk = pl.BlockSpec((tr, cols), lambda i: (i, 0))
    return pl.pallas_call(
        body, name=name, grid=(rows // tr,),
        in_specs=[pl.BlockSpec((terms, tr, cols), lambda i: (0, i, 0))] + [blk] * 3, out_specs=[blk] * 4,
        out_shape=[jax.ShapeDtypeStruct((rows, cols), f32)] * 4,
        compiler_params=_params(("arbitrary",), 32),
    )(*_in_hbm(land), w, m, v)


VEC_NAMES = ("ln_gain", "conv_b", "b_rgate", "b_igate", "lru_lambda", "attn_out_gain", "lru_out_gain", "final_gain")
ROW_RGATE, ROW_IGATE, ROW_VEC, ROW_SINKS = 0, 64, 128, 136
LOSS_LANE = NQ


def _adamw_small(g_rep, g_conv, w, m, v):
    names = list(VEC_NAMES) + ["sinks", "conv_w", "w_rgate", "w_igate"]
    ins = [g_rep, g_conv] + [d[k] for k in names for d in (w, m, v)]

    def body(*refs):
        g_ref, gc_ref = refs[0], refs[1]
        in_refs = refs[2:2 + 3 * len(names)]
        out_refs = refs[2 + 3 * len(names):]

        def update(j, g, at=None):
            w_ref, m_ref, v_ref = in_refs[3 * j:3 * j + 3]
            outs = out_refs[4 * j:4 * j + 4]
            pick = (lambda r: r[...]) if at is None else (lambda r: r[at])
            res = (g,) + _adam_math(pick(w_ref), g, pick(m_ref), pick(v_ref))
            for o_ref, val in zip(outs, res):
                if at is None:
                    o_ref[...] = val
                else:
                    o_ref[at] = val

        for j in range(len(VEC_NAMES)):
            update(j, g_ref[ROW_VEC + j:ROW_VEC + j + 1, :])
        update(len(VEC_NAMES), g_ref[ROW_SINKS:ROW_SINKS + 1, 0:NQ])
        update(len(VEC_NAMES) + 1, gc_ref[...], at=0)
        for gi, row0 in ((len(VEC_NAMES) + 2, ROW_RGATE), (len(VEC_NAMES) + 3, ROW_IGATE)):
            for nb in range(NQ):
                update(gi, g_ref[row0:row0 + HD, HD * nb:HD * nb + HD], at=(0, nb))

    vmem = pl.BlockSpec(memory_space=pltpu.VMEM)
    out_shape = [jax.ShapeDtypeStruct(w[k].shape, f32) for k in names for _ in range(4)]
    outs = pl.pallas_call(
        body, name="adamw_small",
        in_specs=[vmem] * len(ins), out_specs=[vmem] * len(out_shape), out_shape=out_shape,
        compiler_params=pltpu.CompilerParams(vmem_limit_bytes=32 * MIB),
    )(*ins)
    return {k: tuple(outs[4 * j:4 * j + 4]) for j, k in enumerate(names)}


def _pack_small(small, loss):
    gate = lambda g: g.transpose(1, 0, 2).reshape(HD, NQ * HD)
    row_s = jnp.concatenate([small["sinks"], loss[:, LOSS_LANE:128], jnp.zeros((1, D - 128), f32)], axis=1)
    rep = jnp.concatenate([gate(small["w_rgate"]), gate(small["w_igate"])] + [small[k] for k in VEC_NAMES]
                          + [row_s, jnp.zeros((SMALL_ROWS - ROW_SINKS - 1, D), f32)], axis=0)
    conv = small["conv_w"].reshape(CONVW, NDEV, 128).transpose(1, 0, 2)
    conv = jnp.pad(conv, ((0, 0), (0, 8 - CONVW), (0, D - 128)))
    return jnp.concatenate([rep.reshape(NDEV, SMALL_PER, D), conv], axis=1).reshape(NDEV * (SMALL_PER + 8), D)


def kernel(x, ln_gain, w_in, sinks, conv_w, conv_b, w_rgate, b_rgate, w_igate, b_igate, lru_lambda, attn_out_gain, lru_out_gain, w_out, final_gain, loss_target, m_ln_gain, m_w_in, m_sinks, m_conv_w, m_conv_b, m_w_rgate, m_b_rgate, m_w_igate, m_b_igate, m_lru_lambda, m_attn_out_gain, m_lru_out_gain, m_w_out, m_final_gain, v_ln_gain, v_w_in, v_sinks, v_conv_w, v_conv_b, v_w_rgate, v_b_rgate, v_w_igate, v_b_igate, v_lru_lambda, v_attn_out_gain, v_lru_out_gain, v_w_out, v_final_gain):
    w = dict(ln_gain=ln_gain, sinks=sinks, conv_w=conv_w, conv_b=conv_b, w_rgate=w_rgate, b_rgate=b_rgate,
             w_igate=w_igate, b_igate=b_igate, lru_lambda=lru_lambda, attn_out_gain=attn_out_gain,
             lru_out_gain=lru_out_gain, final_gain=final_gain.reshape(1, D))
    m = dict(ln_gain=m_ln_gain, sinks=m_sinks, conv_w=m_conv_w, conv_b=m_conv_b, w_rgate=m_w_rgate,
             b_rgate=m_b_rgate, w_igate=m_w_igate, b_igate=m_b_igate, lru_lambda=m_lru_lambda,
             attn_out_gain=m_attn_out_gain, lru_out_gain=m_lru_out_gain, final_gain=m_final_gain.reshape(1, D))
    v = dict(ln_gain=v_ln_gain, sinks=v_sinks, conv_w=v_conv_w, conv_b=v_conv_b, w_rgate=v_w_rgate,
             b_rgate=v_b_rgate, w_igate=v_w_igate, b_igate=v_b_igate, lru_lambda=v_lru_lambda,
             attn_out_gain=v_attn_out_gain, lru_out_gain=v_lru_out_gain, final_gain=v_final_gain.reshape(1, D))

    conv_blk = jnp.pad(conv_w[0], ((0, 8 - CONVW), (0, 0)))
    wt, cw_all = _all_gather([w_in[0].T, conv_blk], [bf16, f32], "gather_weights")
    conv_full = cw_all.reshape(NDEV, 8, 128)[:, 0:CONVW].transpose(1, 0, 2).reshape(CONVW, LW)

    p = {k: (w[k][0] if k in ("w_rgate", "w_igate") else w[k]) for k in w if k != "conv_w"}
    gx, land_wt, land_wo, land_sm = _sequence_step(x[0], loss_target[0], wt, w_out[0], conv_full, p)

    g_sm = _sum_slots(land_sm, SMALL_PER + 8, "sum_small")
    (g_rep,) = _all_gather([g_sm[0:SMALL_PER]], [f32], "gather_small")
    g_conv = g_sm[SMALL_PER:SMALL_PER + CONVW, 0:128]

    wins = _reduce_adamw(land_wt, w_in[0].T, m_w_in[0].T, v_w_in[0].T, 192, "adamw_w_in")
    g_win, d_win, m_win, v_win = (t.T for t in wins)
    g_wo, d_wo, m_wo, v_wo = _reduce_adamw(land_wo, w_out[0], m_w_out[0], v_w_out[0], 256, "adamw_w_out")
    res = _adamw_small(g_rep, g_conv, w, m, v)
    res["w_in"] = tuple(t[None] for t in (g_win, d_win, m_win, v_win))
    res["w_out"] = tuple(t[None] for t in (g_wo, d_wo, m_wo, v_wo))
    res["final_gain"] = tuple(t.reshape(D) for t in res["final_gain"])

    order = ("ln_gain", "w_in", "sinks", "conv_w", "conv_b", "w_rgate", "b_rgate", "w_igate", "b_igate",
             "lru_lambda", "attn_out_gain", "lru_out_gain", "w_out", "final_gain")
    total_loss = g_rep[ROW_SINKS, LOSS_LANE]
    return (total_loss, gx[None]) + tuple(res[k][i] for i in range(4) for k in order)
```

```python
import jax
import jax.numpy as jnp
from jax import lax
from jax.experimental import pallas as pl
from jax.experimental.pallas import tpu as pltpu

f32 = jnp.float32
bf16 = jnp.bfloat16

D = 1024
HD = 64
NQ = 16
NKV = 4
GROUP = NQ // NKV
KVW = NKV * HD
BLK = 128
ROT = 16
THETA = 500000.0
NEG = -1e30
LW = 1024
NGRP = 4
CONVW = 4
LRU_C = 8.0
NIN = 4608
EPS = 1e-6
NDEV = 8
WT_ROWS = NIN // NDEV
WO_ROWS = 2 * D // NDEV
SMALL_ROWS = 192
SMALL_PER = SMALL_ROWS // NDEV

ADAM_LR = 0.001
ADAM_B1 = 0.9
ADAM_B2 = 0.999
ADAM_EPS = 1e-08
ADAM_WD = 0.01
ADAM_STEP = 10

NT = (((1,), (1,)), ((), ()))
TN = (((0,), (0,)), ((), ()))
MESH = pl.DeviceIdType.MESH
MIB = 1024 * 1024


def _dot(a, b):
    return jnp.dot(a, b, preferred_element_type=f32)


def _dot_nt(a, b):
    return lax.dot_general(a, b, NT, preferred_element_type=f32)


def _dot_tn(a, b):
    return lax.dot_general(a, b, TN, preferred_element_type=f32)


def _params(sem, vmem_mib):
    return pltpu.CompilerParams(dimension_semantics=sem, vmem_limit_bytes=vmem_mib * MIB)


def _sigmoid(x):
    return 0.5 * jnp.tanh(0.5 * x) + 0.5


def _softplus(x):
    return jnp.maximum(x, 0.0) + jnp.log(1.0 + jnp.exp(-jnp.abs(x)))


def _rope_tables(s):
    pos = jnp.arange(s, dtype=f32)
    inv_freq = THETA ** (-jnp.arange(0, ROT, 2, dtype=f32) / ROT)
    ang = pos[:, None] * inv_freq[None, :]
    cs = jnp.concatenate([jnp.cos(ang) - 1.0, jnp.sin(ang)], axis=1)
    d = jnp.arange(128) % HD
    j = jnp.arange(ROT)[:, None]
    pick_c = ((d < ROT) & (j == d % (ROT // 2))).astype(f32)
    pick_sa = ((d >= ROT // 2) & (d < ROT) & (j == d)).astype(f32)
    pick_sb = -((d < ROT // 2) & (j == d + ROT // 2)).astype(f32)
    picks = jnp.concatenate([pick_c, pick_sa, pick_sb], axis=1)
    ones = jnp.concatenate([jnp.ones((1, 128), f32), jnp.zeros((1, 256), f32)], axis=1)
    return jnp.dot(cs, picks, precision=lax.Precision.HIGHEST) + ones


def _tables(tab_ref):
    return tab_ref[:, 0:128], tab_ref[:, 128:256], tab_ref[:, 256:384]


def _rope(t, c, sa, sb):
    return t * c + pltpu.roll(t, 8, 1) * sa + pltpu.roll(t, 120, 1) * sb


def _unrope_t(dr, c, sa, sb):
    return dr * c + pltpu.roll(dr * sa, 120, 0) + pltpu.roll(dr * sb, 8, 0)


def _place():
    return lax.axis_index("x"), lax.axis_index("y"), lax.axis_index("c")


def _gather_ops(mine_refs, out_refs, send_sems, recv_sems, local_sems):
    n = len(mine_refs)
    x, y, c = _place()
    me, sibling = (x, y, c), (x, y, 1 - c)
    chips = [(1 - x, y), (x, 1 - y), (1 - x, 1 - y)]

    def rows(a, dev):
        m = mine_refs[a].shape[0]
        return out_refs[a].at[pl.ds((4 * dev[0] + 2 * dev[1] + dev[2]) * m, m), :]

    def copy(a, k, block, to, own=False):
        return pltpu.make_async_remote_copy(
            src_ref=mine_refs[a] if own else rows(a, block), dst_ref=rows(a, block),
            send_sem=send_sems.at[a, k], recv_sem=recv_sems.at[a, k], device_id=to, device_id_type=MESH)

    def local(a):
        return pltpu.make_async_copy(mine_refs[a], rows(a, me), local_sems.at[a])

    def first(a):
        return [copy(a, 0, me, sibling, own=True)] + [copy(a, 1 + j, me, (*chip, c), own=True)
                                                      for j, chip in enumerate(chips)]

    def start():
        for a in range(n):
            local(a).start()
            for cp in first(a):
                cp.start()

    def finish():
        for j, chip in enumerate(chips):
            for a in range(n):
                copy(a, 1 + j, (*chip, c), me).wait_recv()
                copy(a, 4 + j, (*chip, c), sibling).start()
        for a in range(n):
            copy(a, 0, sibling, me).wait_recv()
            for j, chip in enumerate(chips):
                copy(a, 4 + j, (*chip, 1 - c), me).wait_recv()
        for a in range(n):
            for cp in first(a) + [copy(a, 4 + j, (*chip, c), sibling) for j, chip in enumerate(chips)]:
                cp.wait_send()
            local(a).wait()

    return start, finish


def _relay_gather_ops(mine_refs, out_refs, send_sems, recv_sems, local_sems):
    n = len(mine_refs)
    x, y, c = _place()
    me, sibling = (x, y, c), (x, y, 1 - c)
    near = (x ^ (1 - c), y ^ c)
    far = (x ^ c, y ^ (1 - c))
    diag = (1 - x, 1 - y)

    def rows(a, dev):
        m = mine_refs[a].shape[0]
        return out_refs[a].at[pl.ds((4 * dev[0] + 2 * dev[1] + dev[2]) * m, m), :]

    def copy(a, k, block, to, own=False):
        return pltpu.make_async_remote_copy(
            src_ref=mine_refs[a] if own else rows(a, block), dst_ref=rows(a, block),
            send_sem=send_sems.at[a, k], recv_sem=recv_sems.at[a, k], device_id=to, device_id_type=MESH)

    def local(a):
        return pltpu.make_async_copy(mine_refs[a], rows(a, me), local_sems.at[a])

    def sends(a):
        return [copy(a, 0, me, sibling, own=True), copy(a, 1, me, (*near, c), own=True),
                copy(a, 2, me, (*far, c), own=True), copy(a, 3, (*near, c), (*far, c)),
                copy(a, 4, (*near, c), sibling), copy(a, 5, (*far, c), sibling), copy(a, 6, (*diag, c), sibling)]

    def arrivals(a):
        return [copy(a, 0, sibling, me), copy(a, 1, (*near, c), me), copy(a, 2, (*far, c), me),
                copy(a, 3, (*diag, c), me), copy(a, 4, (*far, 1 - c), me), copy(a, 5, (*near, 1 - c), me),
                copy(a, 6, (*diag, 1 - c), me)]

    def start():
        for a in range(n):
            local(a).start()
            for cp in sends(a)[0:3]:
                cp.start()

    def finish():
        for first, then in ((1, (3, 4)), (2, (5,)), (3, (6,))):
            for a in range(n):
                arrivals(a)[first].wait_recv()
                for k in then:
                    sends(a)[k].start()
        for a in range(n):
            for k in (0, 4, 5, 6):
                arrivals(a)[k].wait_recv()
        for a in range(n):
            for cp in sends(a):
                cp.wait_send()
            local(a).wait()

    return start, finish


def _scatter_ops(src_refs, land_refs, send_sems, recv_sems, local_sems):
    n = len(src_refs)
    x, y, c = _place()
    my = 4 * x + 2 * y + c

    def peer(k):
        return x ^ (k >> 2), y ^ ((k >> 1) & 1), c ^ (k & 1)

    def piece(a, dev):
        m = src_refs[a].shape[0] // NDEV
        return src_refs[a].at[pl.ds(dev * m, m), :]

    def local(a):
        return pltpu.make_async_copy(piece(a, my), land_refs[a].at[my], local_sems.at[a])

    def send(a, k):
        px, py, pc = peer(k)
        return pltpu.make_async_remote_copy(
            src_ref=piece(a, 4 * px + 2 * py + pc), dst_ref=land_refs[a].at[my],
            send_sem=send_sems.at[a, k - 1], recv_sem=recv_sems.at[a, k - 1],
            device_id=(px, py, pc), device_id_type=MESH)

    def arrival(a, k):
        px, py, pc = peer(k)
        return pltpu.make_async_remote_copy(
            src_ref=piece(a, my), dst_ref=land_refs[a].at[4 * px + 2 * py + pc],
            send_sem=send_sems.at[a, k - 1], recv_sem=recv_sems.at[a, k - 1],
            device_id=(px, py, pc), device_id_type=MESH)

    def start():
        for a in range(n):
            local(a).start()
        for k in range(1, NDEV):
            for a in range(n):
                send(a, k).start()

    def finish():
        for k in range(1, NDEV):
            for a in range(n):
                send(a, k).wait_send()
        for k in range(1, NDEV):
            for a in range(n):
                arrival(a, k).wait_recv()
        for a in range(n):
            local(a).wait()

    return start, finish


def _in_hbm(*arrays):
    return tuple(pltpu.with_memory_space_constraint(a, pltpu.HBM) for a in arrays)


def _comm_sems(n):
    return [pltpu.SemaphoreType.DMA((n, 7)), pltpu.SemaphoreType.DMA((n, 7)), pltpu.SemaphoreType.DMA((n,))]


HBM = pl.BlockSpec(memory_space=pltpu.HBM)


def _sink_rows(sinks):
    return jnp.repeat(sinks.reshape(NKV, GROUP), BLK, axis=1)


def _band_softmax(s2_ref, ls, prev_offset, sink_row):
    jj = lax.broadcasted_iota(jnp.int32, (BLK, BLK), 0)
    ii = lax.broadcasted_iota(jnp.int32, (BLK, BLK), 1)
    from_prev = jj > ii
    sc = jnp.where(from_prev, s2_ref[0:BLK, ls] + prev_offset, s2_ref[BLK:2 * BLK, ls])
    m = jnp.maximum(jnp.max(sc, axis=0, keepdims=True), sink_row)
    p = jnp.exp(sc - m)
    es = jnp.exp(sink_row - m)
    inv = 1.0 / (jnp.sum(p, axis=0, keepdims=True) + es)
    return from_prev, p * inv, es * inv


def _put_split(dst_ref, ls, t, from_prev):
    t = t.astype(bf16)
    zero = jnp.zeros_like(t)
    dst_ref[0:BLK, ls] = jnp.where(from_prev, t, zero)
    dst_ref[BLK:2 * BLK, ls] = jnp.where(from_prev, zero, t)


def _heads_side_by_side(ref, h):
    return jnp.concatenate([ref[HD * (GROUP * h + g):HD * (GROUP * h + g) + HD, :] for g in range(GROUP)], axis=1)


def _kv_specs_t():
    prev = pl.BlockSpec((KVW, BLK), lambda n: (0, jnp.maximum(n - 1, 0)))
    cur = pl.BlockSpec((KVW, BLK), lambda n: (0, n))
    return [prev, cur, prev, cur]


def _attn_fwd_t(qt, kt, vt, sinks):
    s = qt.shape[1]

    def body(sink_ref, q_ref, kp_ref, kc_ref, vp_ref, vc_ref, o_ref, s2_scr, pn2_scr):
        n = pl.program_id(0)
        off = jnp.where(n > 0, 0.0, NEG)

        def scores(h):
            hs = slice(HD * h, HD * h + HD)
            kh = jnp.concatenate([kp_ref[hs, :], kc_ref[hs, :]], axis=1)
            s2_scr[h % 2] = _dot_tn(kh, _heads_side_by_side(q_ref, h))

        def probs(h):
            for g in range(GROUP):
                ls = slice(BLK * g, BLK * g + BLK)
                from_prev, pn, _ = _band_softmax(s2_scr.at[h % 2], ls, off, sink_ref[h:h + 1, ls])
                _put_split(pn2_scr.at[h % 2], ls, pn, from_prev)

        def outputs(h):
            hs = slice(HD * h, HD * h + HD)
            vh = jnp.concatenate([vp_ref[hs, :], vc_ref[hs, :]], axis=1)
            og = _dot(vh, pn2_scr[h % 2])
            for g in range(GROUP):
                a = GROUP * h + g
                o_ref[HD * a:HD * a + HD, :] = og[:, BLK * g:BLK * g + BLK]

        scores(0)
        for h in range(NKV):
            if h + 1 < NKV:
                scores(h + 1)
            probs(h)
            outputs(h)

    return pl.pallas_call(
        body, name="attn_fwd", grid=(s // BLK,),
        in_specs=[pl.BlockSpec((NKV, GROUP * BLK), lambda n: (0, 0)), pl.BlockSpec((D, BLK), lambda n: (0, n))]
        + _kv_specs_t(),
        out_specs=pl.BlockSpec((D, BLK), lambda n: (0, n)),
        out_shape=pltpu.HBM((D, s), f32),
        scratch_shapes=[pltpu.VMEM((2, 2 * BLK, GROUP * BLK), f32), pltpu.VMEM((2, 2 * BLK, GROUP * BLK), bf16)],
        compiler_params=_params(("arbitrary",), 32),
    )(_sink_rows(sinks), *_in_hbm(qt, kt, kt, vt, vt))


def _attn_bwd_t(qt, kt, vt, dot, sinks, dwo):
    s = qt.shape[1]
    nb = s // BLK

    def body(sink_ref, q_ref, do_ref, kp_ref, kc_ref, vp_ref, vc_ref, dwo_ref, dq_ref, dk_ref, dv_ref, ds_ref,
             land_ref, dk_hold, dv_hold, s2_scr, dp2_scr, pn2_scr, ds2_scr, send_sems, recv_sems, local_sems):
        n = pl.program_id(0)
        start, finish = _scatter_ops([dwo_ref], [land_ref], send_sems, recv_sems, local_sems)

        @pl.when(n == 0)
        def _():
            start()
            dk_hold[...] = jnp.zeros_like(dk_hold)
            dv_hold[...] = jnp.zeros_like(dv_hold)
            ds_ref[...] = jnp.zeros_like(ds_ref)

        @pl.when(n < nb)
        def _():
            off = jnp.where(n > 0, 0.0, NEG)

            def scores(h):
                hs = slice(HD * h, HD * h + HD)
                kh = jnp.concatenate([kp_ref[hs, :], kc_ref[hs, :]], axis=1)
                vh = jnp.concatenate([vp_ref[hs, :], vc_ref[hs, :]], axis=1)
                s2_scr[h % 2] = _dot_tn(kh, _heads_side_by_side(q_ref, h))
                dp2_scr[h % 2] = _dot_tn(vh, _heads_side_by_side(do_ref, h))

            def softmax_bwd(h):
                for g in range(GROUP):
                    ls = slice(BLK * g, BLK * g + BLK)
                    from_prev, pn, ps = _band_softmax(s2_scr.at[h % 2], ls, off, sink_ref[h:h + 1, ls])
                    dp = jnp.where(from_prev, dp2_scr[h % 2, 0:BLK, ls], dp2_scr[h % 2, BLK:2 * BLK, ls])
                    dsum = jnp.sum(pn * dp, axis=0, keepdims=True)
                    ds_ref[h:h + 1, ls] += -ps * dsum
                    _put_split(pn2_scr.at[h % 2], ls, pn, from_prev)
                    _put_split(ds2_scr.at[h % 2], ls, pn * (dp - dsum), from_prev)

            def grads(h):
                hs = slice(HD * h, HD * h + HD)
                kh = jnp.concatenate([kp_ref[hs, :], kc_ref[hs, :]], axis=1)
                dqg = _dot(kh, ds2_scr[h % 2])
                for g in range(GROUP):
                    a = GROUP * h + g
                    dq_ref[HD * a:HD * a + HD, :] = dqg[:, BLK * g:BLK * g + BLK]
                dkh = _dot_nt(_heads_side_by_side(q_ref, h), ds2_scr[h % 2])
                dvh = _dot_nt(_heads_side_by_side(do_ref, h), pn2_scr[h % 2])
                dk_ref[hs, :] = dk_hold[hs, :] + dkh[:, 0:BLK]
                dv_ref[hs, :] = dv_hold[hs, :] + dvh[:, 0:BLK]
                dk_hold[hs, :] = dkh[:, BLK:2 * BLK]
                dv_hold[hs, :] = dvh[:, BLK:2 * BLK]

            scores(0)
            for h in range(NKV):
                if h + 1 < NKV:
                    scores(h + 1)
                softmax_bwd(h)
                grads(h)

        @pl.when(n == nb)
        def _():
            dk_ref[...] = dk_hold[...]
            dv_ref[...] = dv_hold[...]
            finish()

    blk = pl.BlockSpec((D, BLK), lambda n: (0, jnp.minimum(n, nb - 1)))
    late = pl.BlockSpec((KVW, BLK), lambda n: (0, jnp.maximum(n - 1, 0)))
    whole = pl.BlockSpec((NKV, GROUP * BLK), lambda n: (0, 0))
    kv = [pl.BlockSpec((KVW, BLK), lambda n: (0, jnp.clip(n - 1, 0, nb - 1))),
          pl.BlockSpec((KVW, BLK), lambda n: (0, jnp.minimum(n, nb - 1)))]
    return pl.pallas_call(
        body, name="attn_bwd", grid=(nb + 1,),
        in_specs=[whole, blk, blk] + kv + kv + [HBM],
        out_specs=[blk, late, late, whole, HBM],
        out_shape=[pltpu.HBM((D, s), f32), pltpu.HBM((KVW, s), f32), pltpu.HBM((KVW, s), f32),
                   jax.ShapeDtypeStruct((NKV, GROUP * BLK), f32), pltpu.HBM((NDEV, WO_ROWS, D), bf16)],
        scratch_shapes=[pltpu.VMEM((KVW, BLK), f32), pltpu.VMEM((KVW, BLK), f32)]
        + [pltpu.VMEM((2, 2 * BLK, GROUP * BLK), f32)] * 2 + [pltpu.VMEM((2, 2 * BLK, GROUP * BLK), bf16)] * 2
        + _comm_sems(1),
        compiler_params=_params(("arbitrary",), 48),
    )(_sink_rows(sinks), *_in_hbm(qt, dot, kt, kt, vt, vt, dwo))


def _block_diag(w):
    w4 = w.reshape(NGRP, 4, HD, HD)
    eye = jnp.eye(4, dtype=w.dtype)
    return jnp.einsum('gjcd,jk->gjckd', w4, eye).reshape(NGRP, 256, 256).astype(bf16)


def _gate_terms(pr, pi, br, bi, sp):
    r = _sigmoid(pr + br)
    i = _sigmoid(pi + bi)
    la = -LRU_C * r * sp
    a = jnp.exp(la)
    x2 = 2.0 * la
    y = jnp.where(x2 > -0.02, -x2 * (1.0 + x2 * (0.5 + x2 * (1.0 / 6.0))), 1.0 - a * a)
    inv_mult = lax.rsqrt(jnp.maximum(y, 1e-30))
    return r, i, a, y * inv_mult, inv_mult


def _later(x, before, k):
    if k == 0:
        return x
    row = lax.broadcasted_iota(jnp.int32, before.shape, 0)
    rolled = pltpu.roll(x, k, 0)
    first = jnp.where(row < k, pltpu.roll(before, k, 0), rolled[0:8])
    return jnp.concatenate([first, rolled[8:]], axis=0)


def _earlier(x, after, k):
    if k == 0:
        return x
    n = x.shape[0]
    row = lax.broadcasted_iota(jnp.int32, after.shape, 0)
    rolled = pltpu.roll(x, n - k, 0)
    last = jnp.where(row >= 8 - k, pltpu.roll(after, 8 - k, 0), rolled[n - 8:n])
    return jnp.concatenate([rolled[0:n - 8], last], axis=0)


def _fwd_fused(x, ln_gain, wt, tabs, wo_shard, conv_w, conv_b, wr, wi, br, bi, lam, tm):
    s = x.shape[0]
    nt = s // tm
    nc = 512
    pieces = 8
    rows_per = tm // pieces
    later_chunks = (0, 1, 2, 3, 4, 7, 8)

    def body(x0_ref, xn_ref, g_ref, wt_ref, tab_ref, wo_ref, cw_ref, cb_ref, wr_ref, wi_ref, br_ref,
             bi_ref, lam_ref, h_ref, q_ref, k_ref, v_ref, ga_ref, xl_ref, gl_ref, u_ref, hl_ref, r_ref, ig_ref, a_ref,
             im_ref, wo_all, wo_stage, hb, halo, ub_scr, pr_scr, pi_scr, b_scr, hcar,
             send_sems, recv_sems, local_sems):
        i = pl.program_id(0)
        start, finish = _gather_ops([wo_stage], [wo_all], send_sems, recv_sems, local_sems)
        gain = g_ref[...]

        def normed(xx):
            rstd = lax.rsqrt(jnp.mean(xx * xx, axis=-1, keepdims=True) + EPS)
            return (xx * rstd * gain).astype(bf16)

        @pl.when(i == 0)
        def _():
            wo_stage[...] = wo_ref[...].astype(bf16)
            start()
            hb[0] = normed(x0_ref[...])
            halo[...] = jnp.zeros_like(halo)
            hcar[...] = jnp.zeros_like(hcar)

        cur, nxt = i % 2, (i + 1) % 2
        sp = _softplus(-lam_ref[...])
        br, bi = br_ref[...], bi_ref[...]
        c, sa, sb = _tables(tab_ref)
        piece_rows = lambda p: slice(rows_per * p, rows_per * p + rows_per)

        def project(ci):
            z = _dot_nt(hb[cur], wt_ref[ci * nc:(ci + 1) * nc, :])
            if ci < 2:
                for j in range(nc // 128):
                    r = _rope(z[:, 128 * j:128 * j + 128], c, sa, sb) * (HD ** -0.5)
                    q_ref[ci * nc + 128 * j:ci * nc + 128 * j + 128, :] = r.astype(bf16).T
            elif ci == 2:
                for j in range(2):
                    js = slice(128 * j, 128 * j + 128)
                    k_ref[js, :] = _rope(z[:, js], c, sa, sb).astype(bf16).T
                    v_ref[js, :] = z[:, KVW + 128 * j:KVW + 128 * j + 128].astype(bf16).T
            else:
                sec, j = divmod(ci - 3, 2)
                (ga_ref, xl_ref, gl_ref)[sec][:, j * nc:(j + 1) * nc] = z

        def gate_terms(p):
            rows = piece_rows(p)
            r, ig, a, mult, inv_mult = _gate_terms(pr_scr[rows, :], pi_scr[rows, :], br, bi, sp)
            r_ref[rows, :] = r
            ig_ref[rows, :] = ig
            a_ref[rows, :] = a
            im_ref[rows, :] = inv_mult
            b_scr[rows, :] = mult * (ig * u_ref[rows, :])

        def scan(p, hc):
            for t in range(rows_per * p, rows_per * p + rows_per):
                hc = a_ref[t:t + 1, :] * hc + b_scr[t:t + 1, :]
                hl_ref[t:t + 1, :] = hc
            return hc

        def norm_next(p):
            hb[nxt, piece_rows(p), :] = normed(xn_ref[piece_rows(p), :])

        h_ref[...] = hb[cur]
        project(5)
        project(6)
        xl = xl_ref[...]
        u = cb_ref[...] + sum(cw_ref[k:k + 1, :] * _later(xl, halo[...], CONVW - 1 - k) for k in range(CONVW))
        halo[...] = xl[tm - 8:tm, :]
        u_ref[...] = u
        ub_scr[...] = u.astype(bf16)
        for g in range(NGRP):
            gs = slice(256 * g, 256 * g + 256)
            pr_scr[:, gs] = _dot(ub_scr[:, gs], wr_ref[g])
            pi_scr[:, gs] = _dot(ub_scr[:, gs], wi_ref[g])
        hc = hcar[...]
        gate_terms(0)
        for slot, ci in enumerate(later_chunks):
            project(ci)
            norm_next(slot)
            gate_terms(slot + 1)
            hc = scan(slot, hc)
        norm_next(pieces - 1)
        hcar[...] = scan(pieces - 1, hc)

        @pl.when(i == nt - 1)
        def _():
            finish()

    row = lambda w: pl.BlockSpec((tm, w), lambda i: (i, 0))
    col = lambda w: pl.BlockSpec((w, tm), lambda i: (0, i))
    full = lambda a: pl.BlockSpec(a.shape, lambda i: (0,) * a.ndim)
    big = lambda w, dt: pltpu.HBM((s, w), dt)
    tile = pltpu.VMEM((tm, LW), f32)
    return pl.pallas_call(
        body, name="fwd_fused", grid=(nt,),
        in_specs=[pl.BlockSpec((tm, D), lambda i: (0, 0)), pl.BlockSpec((tm, D), lambda i: (jnp.minimum(i + 1, nt - 1), 0)),
                  full(ln_gain), full(wt), row(384), full(wo_shard), full(conv_w), full(conv_b),
                  full(wr), full(wi), full(br), full(bi), full(lam)],
        out_specs=[row(D), col(D), col(KVW), col(KVW), row(D), row(D), row(D)] + [row(LW)] * 6 + [HBM],
        out_shape=[big(D, bf16), pltpu.HBM((D, s), bf16), pltpu.HBM((KVW, s), bf16), pltpu.HBM((KVW, s), bf16),
                   big(D, f32), big(D, f32), big(D, f32)] + [big(LW, f32)] * 6 + [pltpu.HBM((2 * D, D), bf16)],
        scratch_shapes=[pltpu.VMEM((WO_ROWS, D), bf16), pltpu.VMEM((2, tm, D), bf16), pltpu.VMEM((8, LW), f32),
                        pltpu.VMEM((tm, LW), bf16), tile, tile, tile, pltpu.VMEM((1, LW), f32)] + _comm_sems(1),
        compiler_params=_params(("arbitrary",), 56),
    )(*_in_hbm(x, x), ln_gain, *_in_hbm(wt), tabs, wo_shard, conv_w, conv_b, wr, wi, br, bi, lam)


def _lru_bwd(u, hl, dhl, xl, r, ig, a, im, conv_w, wr, wi, lam, tm):
    s = u.shape[0]
    nt = s // tm
    pieces = 8
    rows_per = tm // pieces

    def body(u_ref, h_ref, hp_ref, dh_ref, x_ref, xp_ref, r_ref, ig_ref, a_ref, im_ref, cw_ref, wr_ref, wi_ref,
             lam_ref, dxl_ref, dwr_ref, dwi_ref, dbr_ref, dbi_ref, dlam_ref, dcb_ref, dcw_ref,
             l_scr, du_scr, dpr_scr, dpi_scr, lcar, dunext):
        t0 = pl.program_id(0)
        tile = nt - 1 - t0

        @pl.when(t0 == 0)
        def _():
            lcar[...] = jnp.zeros_like(lcar)
            dunext[...] = jnp.zeros_like(dunext)
            for ref in (dwr_ref, dwi_ref, dbr_ref, dbi_ref, dlam_ref, dcb_ref, dcw_ref):
                ref[...] = jnp.zeros_like(ref)

        lam = lam_ref[...]
        sp = _softplus(-lam)
        hp = jnp.where(tile > 0, hp_ref[...], 0.0)

        def scan(p, c):
            for t in range(rows_per * p + rows_per - 1, rows_per * p - 1, -1):
                lt = dh_ref[t:t + 1, :] + c
                l_scr[t:t + 1, :] = lt
                c = a_ref[t:t + 1, :] * lt
            return c

        def terms(p, sums):
            rows = slice(rows_per * p, rows_per * p + rows_per)
            lt, u, r, i, a, inv_mult = l_scr[rows, :], u_ref[rows, :], r_ref[rows, :], ig_ref[rows, :], \
                a_ref[rows, :], im_ref[rows, :]
            before = hp if p == 0 else h_ref[rows_per * p - 8:rows_per * p, :]
            hprev = _later(h_ref[rows, :], before, 1)
            x2 = -2.0 * LRU_C * r * sp
            mult = jnp.where(x2 > -0.02, -x2 * (1.0 + x2 * (0.5 + x2 * (1.0 / 6.0))), 1.0 - a * a) * inv_mult
            da = lt * hprev
            dmult = lt * (i * u)
            di = lt * mult * u
            du_scr[rows, :] = lt * mult * i
            dla = da * a - dmult * (a * a) * inv_mult
            dr = dla * (-LRU_C * sp)
            dpr = dr * r * (1.0 - r)
            dpi = di * i * (1.0 - i)
            dpr_scr[rows, :] = dpr.astype(bf16)
            dpi_scr[rows, :] = dpi.astype(bf16)
            col = lambda t: jnp.sum(t, axis=0, keepdims=True)
            return sums[0] + col(dla * (-LRU_C * r)), sums[1] + col(dpr), sums[2] + col(dpi)

        sums = (jnp.zeros((1, LW), f32),) * 3
        c = scan(pieces - 1, lcar[...])
        for p in range(pieces - 1, -1, -1):
            if p > 0:
                c = scan(p - 1, c)
            sums = terms(p, sums)
        lcar[...] = c
        dlam_ref[...] += sums[0]
        dbr_ref[...] += sums[1]
        dbi_ref[...] += sums[2]

        ub = u_ref[...].astype(bf16)
        dug = []
        for g in range(NGRP):
            gs = slice(256 * g, 256 * g + 256)
            dwr_ref[g] += _dot_tn(ub[:, gs], dpr_scr[:, gs])
            dwi_ref[g] += _dot_tn(ub[:, gs], dpi_scr[:, gs])
            dug.append(_dot_nt(dpr_scr[:, gs], wr_ref[g]) + _dot_nt(dpi_scr[:, gs], wi_ref[g]))
        du = du_scr[...] + jnp.concatenate(dug, axis=1)

        dcb_ref[...] += jnp.sum(du, axis=0, keepdims=True)
        x = x_ref[...]
        before = jnp.where(tile > 0, xp_ref[...], 0.0)
        for k in range(CONVW):
            dcw_ref[k:k + 1, :] += jnp.sum(du * _later(x, before, CONVW - 1 - k), axis=0, keepdims=True)
        after = dunext[...]
        dxl = sum(cw_ref[k:k + 1, :] * _earlier(du, after, CONVW - 1 - k) for k in range(CONVW))
        dxl_ref[...] = dxl.astype(bf16)
        dunext[...] = du[0:8, :]

        @pl.when(t0 == nt - 1)
        def _():
            dlam_ref[...] = dlam_ref[...] * (-_sigmoid(-lam))

    rev = lambda i: (nt - 1 - i, 0)
    row = pl.BlockSpec((tm, LW), rev)
    prev8 = pl.BlockSpec((8, LW), lambda i: (jnp.maximum((nt - 1 - i) * (tm // 8) - 1, 0), 0))
    full = lambda a: pl.BlockSpec(a.shape, lambda i: (0,) * a.ndim)
    vec = pl.BlockSpec((1, LW), lambda i: (0, 0))
    bd = pl.BlockSpec((NGRP, 256, 256), lambda i: (0, 0, 0))
    return pl.pallas_call(
        body, name="lru_bwd", grid=(nt,),
        in_specs=[row, row, prev8, row, row, prev8, row, row, row, row, full(conv_w), full(wr), full(wi), full(lam)],
        out_specs=[row, bd, bd, vec, vec, vec, vec, pl.BlockSpec((CONVW, LW), lambda i: (0, 0))],
        out_shape=[pltpu.HBM((s, LW), bf16),
                   jax.ShapeDtypeStruct((NGRP, 256, 256), f32), jax.ShapeDtypeStruct((NGRP, 256, 256), f32),
                   jax.ShapeDtypeStruct((1, LW), f32), jax.ShapeDtypeStruct((1, LW), f32),
                   jax.ShapeDtypeStruct((1, LW), f32), jax.ShapeDtypeStruct((1, LW), f32),
                   jax.ShapeDtypeStruct((CONVW, LW), f32)],
        scratch_shapes=[pltpu.VMEM((tm, LW), f32), pltpu.VMEM((tm, LW), f32), pltpu.VMEM((tm, LW), bf16),
                        pltpu.VMEM((tm, LW), bf16), pltpu.VMEM((1, LW), f32), pltpu.VMEM((8, LW), f32)],
        compiler_params=_params(("arbitrary",), 56),
    )(*_in_hbm(u, hl, hl, dhl, xl, xl, r, ig, a, im), conv_w, wr, wi, lam)


def _gated_norm(t, gate, gain):
    sg = _sigmoid(gate)
    silu = gate * sg
    p = t * silu
    rstd = lax.rsqrt(jnp.mean(p * p, axis=-1, keepdims=True) + EPS)
    ph = p * rstd
    return sg, silu, rstd, ph, ph * gain


def _gated_norm_bwd(dy, t, gate, gain, sg, silu, rstd, ph):
    w = dy * gain
    dp = rstd * (w - ph * jnp.mean(w * ph, axis=-1, keepdims=True))
    dgate = dp * t * (sg * (1.0 + gate * (1.0 - sg)))
    return jnp.sum(dy * ph, axis=0, keepdims=True), dp * silu, dgate


def _out_fwd_bwd(x, tgt, o, ga, hl, gl, again, lgain, fgain, wo, tm):
    s = x.shape[0]
    nt = s // tm

    def body(x_ref, t_ref, o_ref, ga_ref, hl_ref, gl_ref, ag_ref, lg_ref, fg_ref, wo_ref,
             dx2_ref, do_ref, dga_ref, dhl_ref, dgl_ref, dwo_ref, gfg_ref, gag_ref, glg_ref, loss_ref, acc):
        i = pl.program_id(0)

        @pl.when(i == 0)
        def _():
            acc[...] = jnp.zeros_like(acc)
            for ref in (gfg_ref, gag_ref, glg_ref, loss_ref):
                ref[...] = jnp.zeros_like(ref)

        oo = jnp.concatenate([o_ref[128 * j:128 * j + 128, :].T for j in range(D // 128)], axis=1)
        gga, hh, ggl = ga_ref[...], hl_ref[...], gl_ref[...]
        ag, lg, fg = ag_ref[...], lg_ref[...], fg_ref[...]
        sga, silua, ra, pah, ya = _gated_norm(oo, gga, ag)
        sgl, silul, rl, plh, yl = _gated_norm(hh, ggl, lg)
        yab, ylb = ya.astype(bf16), yl.astype(bf16)
        y = _dot(yab, wo_ref[0:D, :]) + _dot(ylb, wo_ref[D:2 * D, :])
        x2 = x_ref[...] + y
        r2 = lax.rsqrt(jnp.mean(x2 * x2, axis=-1, keepdims=True) + EPS)
        x2h = x2 * r2
        err = x2h * fg - t_ref[...]
        loss_ref[...] += 0.5 * jnp.sum(jnp.sum(err * err, axis=-1, keepdims=True) * (1.0 / D))
        dout = err * (1.0 / D)
        gfg_ref[...] += jnp.sum(dout * x2h, axis=0, keepdims=True)
        w = dout * fg
        dx2 = r2 * (w - x2h * jnp.mean(w * x2h, axis=-1, keepdims=True))
        dx2_ref[...] = dx2
        dyb = dx2.astype(bf16)
        acc[0:D, :] += _dot_tn(yab, dyb)
        acc[D:2 * D, :] += _dot_tn(ylb, dyb)
        dya = _dot_nt(dyb, wo_ref[0:D, :])
        dyl = _dot_nt(dyb, wo_ref[D:2 * D, :])
        gag, do, dga = _gated_norm_bwd(dya, oo, gga, ag, sga, silua, ra, pah)
        glg, dhl, dgl = _gated_norm_bwd(dyl, hh, ggl, lg, sgl, silul, rl, plh)
        gag_ref[...] += gag
        glg_ref[...] += glg
        dob = do.astype(bf16)
        for j in range(D // 128):
            do_ref[128 * j:128 * j + 128, :] = dob[:, 128 * j:128 * j + 128].T
        dga_ref[...] = dga.astype(bf16)
        dhl_ref[...] = dhl
        dgl_ref[...] = dgl.astype(bf16)

        @pl.when(i == nt - 1)
        def _():
            dwo_ref[...] = acc[...].astype(bf16)

    row = pl.BlockSpec((tm, D), lambda i: (i, 0))
    col = pl.BlockSpec((D, tm), lambda i: (0, i))
    vec = pl.BlockSpec((1, D), lambda i: (0, 0))
    mat = pl.BlockSpec((2 * D, D), lambda i: (0, 0))
    return pl.pallas_call(
        body, name="out_fwd_bwd", grid=(nt,),
        in_specs=[row, row, col, row, row, row] + [vec] * 3 + [mat],
        out_specs=[row, col, row, row, row] + [mat, vec, vec, vec, pl.BlockSpec((1, 128), lambda i: (0, 0))],
        out_shape=[pltpu.HBM((s, D), f32), pltpu.HBM((D, s), bf16),
                   pltpu.HBM((s, D), bf16), pltpu.HBM((s, D), f32),
                   pltpu.HBM((s, D), bf16), pltpu.HBM((2 * D, D), bf16),
                   jax.ShapeDtypeStruct((1, D), f32), jax.ShapeDtypeStruct((1, D), f32),
                   jax.ShapeDtypeStruct((1, D), f32), jax.ShapeDtypeStruct((1, 128), f32)],
        scratch_shapes=[pltpu.VMEM((2 * D, D), f32)],
        compiler_params=_params(("arbitrary",), 56),
    )(*_in_hbm(x, tgt, o, ga, hl, gl), again, lgain, fgain, *_in_hbm(wo))


def _bwd_in(x, dx2, dq, dk, dv, dga, dxl, dgl, ln_gain, wt, tabs, tm):
    s = x.shape[0]

    def body(x_ref, dx2_ref, dq_ref, dk_ref, dv_ref, dga_ref, dxl_ref, dgl_ref, g_ref, wt_ref,
             tab_ref, gx_ref, gln_ref, dzt_ref):
        @pl.when(pl.program_id(0) == 0)
        def _():
            gln_ref[...] = jnp.zeros_like(gln_ref)

        c, sa, sb = (t.T for t in _tables(tab_ref))
        for j in range(D // 128):
            js = slice(128 * j, 128 * j + 128)
            dzt_ref[js, :] = (_unrope_t(dq_ref[js, :], c, sa, sb) * (HD ** -0.5)).astype(bf16)
        for j in range(KVW // 128):
            js = slice(128 * j, 128 * j + 128)
            dzt_ref[D + 128 * j:D + 128 * j + 128, :] = _unrope_t(dk_ref[js, :], c, sa, sb).astype(bf16)
        dzt_ref[D + KVW:D + 2 * KVW, :] = dv_ref[...].astype(bf16)
        first = D + 2 * KVW
        dh = _dot_tn(dzt_ref[0:512, :], wt_ref[0:512, :])
        for ci in range(1, first // 512):
            dh = dh + _dot_tn(dzt_ref[512 * ci:512 * ci + 512, :], wt_ref[512 * ci:512 * ci + 512, :])
        for sec, ref in enumerate((dga_ref, dxl_ref, dgl_ref)):
            for j in range(D // 512):
                rows = slice(first + D * sec + 512 * j, first + D * sec + 512 * j + 512)
                dh = dh + _dot(ref[:, 512 * j:512 * j + 512], wt_ref[rows, :])
            for j in range(D // 128):
                dzt_ref[first + D * sec + 128 * j:first + D * sec + 128 * j + 128, :] = ref[:, 128 * j:128 * j + 128].T
        xx = x_ref[...]
        rstd = lax.rsqrt(jnp.mean(xx * xx, axis=-1, keepdims=True) + EPS)
        xh = xx * rstd
        gln_ref[...] += jnp.sum(dh * xh, axis=0, keepdims=True)
        w = dh * g_ref[...]
        gx_ref[...] = dx2_ref[...] + rstd * (w - xh * jnp.mean(w * xh, axis=-1, keepdims=True))

    row = lambda w: pl.BlockSpec((tm, w), lambda i: (i, 0))
    col = lambda w: pl.BlockSpec((w, tm), lambda i: (0, i))
    full = lambda a: pl.BlockSpec(a.shape, lambda i: (0, 0))
    return pl.pallas_call(
        body, name="bwd_in", grid=(s // tm,),
        in_specs=[row(D), row(D), col(D), col(KVW), col(KVW), row(D), row(D), row(D), full(ln_gain), full(wt),
                  row(384)],
        out_specs=[row(D), pl.BlockSpec((1, D), lambda i: (0, 0)), col(NIN)],
        out_shape=[pltpu.HBM((s, D), f32), jax.ShapeDtypeStruct((1, D), f32),
                   pltpu.HBM((NIN, s), bf16)],
        compiler_params=_params(("arbitrary",), 56),
    )(*_in_hbm(x, dx2, dq, dk, dv, dga, dxl, dgl), ln_gain, *_in_hbm(wt), tabs)


WT_TERMS = 4


def _dwt_scatter(dzt, h, small, tm):
    s = h.shape[0]
    nk = s // tm
    srows = small.shape[0] // NDEV
    last = NDEV - 1

    def body(order_ref, dz_ref, h_ref, sm_ref, lwt_ref, lsm_ref, acc, stage, given, relayed, send_sems, recv_sems,
             local_sem, sm_send, sm_recv, sm_local):
        j, k = pl.program_id(0), pl.program_id(1)
        x, y, c = _place()
        sibling = (x, y, 1 - c)
        near = (x ^ (1 - c), y ^ c)
        far = (x ^ c, y ^ (1 - c))
        sm_start, sm_finish = _scatter_ops([sm_ref], [lsm_ref], sm_send, sm_recv, sm_local)

        def send(step):
            if step == last - 1:
                dst, to = lwt_ref.at[1], sibling
            elif step % 2 == 0:
                dst, to = given.at[step // 2], sibling
            elif step == 1:
                dst, to = relayed, (*near, c)
            else:
                dst, to = lwt_ref.at[1 + step // 2], (*(near if step == 3 else far), c)
            return pltpu.make_async_remote_copy(
                src_ref=stage.at[step % 2], dst_ref=dst, send_sem=send_sems.at[step], recv_sem=recv_sems.at[step],
                device_id=to, device_id_type=MESH)

        def keep():
            return pltpu.make_async_copy(stage.at[last % 2], lwt_ref.at[0], local_sem)

        @pl.when((j == 0) & (k == 0))
        def _():
            sm_start()

        @pl.when(k == 0)
        def _():
            acc[...] = jnp.zeros_like(acc)

        acc[...] += _dot(dz_ref[...], h_ref[...])

        for step in range(NDEV):
            @pl.when((k == nk - 1) & (j == step))
            def _(step=step):
                if step >= 2:
                    send(step - 2).wait_send()
                if step % 2 == 1 and step < last:
                    send(step - 1).wait_recv()
                    total = acc[...] + given[step // 2].astype(f32)
                    if step == 5:
                        send(1).wait_recv()
                        total = total + relayed[...].astype(f32)
                    stage[step % 2] = total.astype(bf16)
                else:
                    stage[step % 2] = acc[...].astype(bf16)
                if step < last:
                    send(step).start()
                else:
                    keep().start()
                    send(last - 1).wait_send()
                    for peer_step in (3, 5, last - 1):
                        send(peer_step).wait_recv()
                    keep().wait()
                    sm_finish()

    x, y, c = _place()
    dest = lambda chip, cc: 4 * chip[0] + 2 * chip[1] + cc
    near, far, diag = (x ^ (1 - c), y ^ c), (x ^ c, y ^ (1 - c)), (1 - x, 1 - y)
    order = jnp.stack([dest(diag, 1 - c), dest(diag, c), dest(far, 1 - c), dest(near, c),
                       dest(near, 1 - c), dest(far, c), dest((x, y), 1 - c), dest((x, y), c)])
    return pl.pallas_call(
        body, name="dwt_scatter",
        grid_spec=pltpu.PrefetchScalarGridSpec(
            num_scalar_prefetch=1, grid=(NDEV, nk),
            in_specs=[pl.BlockSpec((WT_ROWS, tm), lambda j, k, order: (order[j], k)),
                      pl.BlockSpec((tm, D), lambda j, k, order: (k, 0)), HBM],
            out_specs=[HBM, HBM],
            scratch_shapes=[pltpu.VMEM((WT_ROWS, D), f32), pltpu.VMEM((2, WT_ROWS, D), bf16),
                            pltpu.VMEM((3, WT_ROWS, D), bf16), pltpu.VMEM((WT_ROWS, D), bf16),
                            pltpu.SemaphoreType.DMA((last,)), pltpu.SemaphoreType.DMA((last,)),
                            pltpu.SemaphoreType.DMA(())] + _comm_sems(1)),
        out_shape=[pltpu.HBM((WT_TERMS, WT_ROWS, D), bf16), pltpu.HBM((NDEV, srows, D), f32)],
        compiler_params=_params(("arbitrary", "arbitrary"), 32),
    )(order, *_in_hbm(dzt, h, small))


def _diag_blocks(bd):
    eye = jnp.eye(4, dtype=bd.dtype)
    return jnp.einsum('gjckd,jk->gjcd', bd.reshape(NGRP, 4, HD, 4, HD), eye).reshape(NQ, HD, HD)


def _sequence_step(x, tgt, wt, wo_shard, conv_w, p):
    s = x.shape[0]
    tm = min(256, s)
    tabs = _rope_tables(s)
    wr, wi = _block_diag(p["w_rgate"]), _block_diag(p["w_igate"])
    sinks = p["sinks"].reshape(NQ)
    h, qt, kt, vt, ga, xl, gl, u, hl, r, ig, a, im, wo = _fwd_fused(
        x, p["ln_gain"], wt, tabs, wo_shard, conv_w, p["conv_b"], wr, wi, p["b_rgate"], p["b_igate"],
        p["lru_lambda"], tm)
    ot = _attn_fwd_t(qt, kt, vt, sinks)
    dx2, dot, dga, dhl, dgl, dwo, g_fg, g_ag, g_lg, loss = _out_fwd_bwd(
        x, tgt, ot, ga, hl, gl, p["attn_out_gain"], p["lru_out_gain"], p["final_gain"], wo, tm)
    dqt, dkt, dvt, dsink, land_wo = _attn_bwd_t(qt, kt, vt, dot, sinks, dwo)
    dxl, dwr, dwi, dbr, dbi, dlam, dcb, dcw = _lru_bwd(u, hl, dhl, xl, r, ig, a, im, conv_w, wr, wi, p["lru_lambda"], tm)
    gx, g_ln, dzt = _bwd_in(x, dx2, dqt, dkt, dvt, dga, dxl, dgl, p["ln_gain"], wt, tabs, tm)
    small = dict(ln_gain=g_ln, sinks=dsink.reshape(NQ, BLK).sum(axis=1)[None], conv_w=dcw, conv_b=dcb,
                 w_rgate=_diag_blocks(dwr), b_rgate=dbr, w_igate=_diag_blocks(dwi), b_igate=dbi, lru_lambda=dlam,
                 attn_out_gain=g_ag, lru_out_gain=g_lg, final_gain=g_fg)
    land_wt, land_sm = _dwt_scatter(dzt, h, _pack_small(small, loss), min(1024, s))
    return gx, land_wt, land_wo, land_sm


def _all_gather(srcs, out_dtypes, name):
    n = len(srcs)
    cast = [a.dtype != dt for a, dt in zip(srcs, out_dtypes)]

    def body(*refs):
        src_refs, out_refs = refs[:n], refs[n:2 * n]
        stage_refs = list(refs[2 * n:2 * n + sum(cast)])
        mine_refs = []
        for a in range(n):
            if cast[a]:
                st = stage_refs.pop(0)
                st[...] = src_refs[a][...].astype(out_dtypes[a])
                mine_refs.append(st)
            else:
                mine_refs.append(src_refs[a])
        start, finish = _relay_gather_ops(mine_refs, out_refs, *refs[-3:])
        start()
        finish()

    vmem = pl.BlockSpec(memory_space=pltpu.VMEM)
    return pl.pallas_call(
        body, name=name,
        in_specs=[vmem] * n, out_specs=[HBM] * n,
        out_shape=[pltpu.HBM((NDEV * a.shape[0], a.shape[1]), dt) for a, dt in zip(srcs, out_dtypes)],
        scratch_shapes=[pltpu.VMEM(a.shape, dt) for a, dt, cst in zip(srcs, out_dtypes, cast) if cst] + _comm_sems(n),
        compiler_params=pltpu.CompilerParams(vmem_limit_bytes=32 * MIB),
    )(*srcs)


def _sum_slots(land, tr, name):
    terms, rows, cols = land.shape

    def body(l_ref, o_ref):
        acc = l_ref[0].astype(f32)
        for d in range(1, terms):
            acc = acc + l_ref[d].astype(f32)
        o_ref[...] = acc

    return pl.pallas_call(
        body, name=name, grid=(rows // tr,),
        in_specs=[pl.BlockSpec((terms, tr, cols), lambda i: (0, i, 0))],
        out_specs=pl.BlockSpec((tr, cols), lambda i: (i, 0)),
        out_shape=jax.ShapeDtypeStruct((rows, cols), f32),
        compiler_params=_params(("arbitrary",), 32),
    )(*_in_hbm(land))


def _adam_math(w, g, m, v):
    m2 = ADAM_B1 * m + (1.0 - ADAM_B1) * g
    v2 = ADAM_B2 * v + (1.0 - ADAM_B2) * (g * g)
    m_hat = m2 / (1.0 - ADAM_B1 ** ADAM_STEP)
    v_hat = v2 / (1.0 - ADAM_B2 ** ADAM_STEP)
    delta = -ADAM_LR * (m_hat / (jnp.sqrt(v_hat) + ADAM_EPS) + ADAM_WD * w)
    return delta, m2, v2


def _reduce_adamw(land, w, m, v, tr, name):
    terms, rows, cols = land.shape

    def body(l_ref, w_ref, m_ref, v_ref, g_ref, d_ref, m2_ref, v2_ref):
        g = l_ref[0].astype(f32)
        for t in range(1, terms):
            g = g + l_ref[t].astype(f32)
        g_ref[...] = g
        d_ref[...], m2_ref[...], v2_ref[...] = _adam_math(w_ref[...], g, m_ref[...], v_ref[...])

    blk = pl.BlockSpec((tr, cols), lambda i: (i, 0))
    return pl.pallas_call(
        body, name=name, grid=(rows // tr,),
        in_specs=[pl.BlockSpec((terms, tr, cols), lambda i: (0, i, 0))] + [blk] * 3, out_specs=[blk] * 4,
        out_shape=[jax.ShapeDtypeStruct((rows, cols), f32)] * 4,
        compiler_params=_params(("arbitrary",), 32),
    )(*_in_hbm(land), w, m, v)


VEC_NAMES = ("ln_gain", "conv_b", "b_rgate", "b_igate", "lru_lambda", "attn_out_gain", "lru_out_gain", "final_gain")
ROW_RGATE, ROW_IGATE, ROW_VEC, ROW_SINKS = 0, 64, 128, 136
LOSS_LANE = NQ


def _adamw_small(g_rep, g_conv, w, m, v):
    names = list(VEC_NAMES) + ["sinks", "conv_w", "w_rgate", "w_igate"]
    ins = [g_rep, g_conv] + [d[k] for k in names for d in (w, m, v)]

    def body(*refs):
        g_ref, gc_ref = refs[0], refs[1]
        in_refs = refs[2:2 + 3 * len(names)]
        out_refs = refs[2 + 3 * len(names):]

        def update(j, g, at=None):
            w_ref, m_ref, v_ref = in_refs[3 * j:3 * j + 3]
            outs = out_refs[4 * j:4 * j + 4]
            pick = (lambda r: r[...]) if at is None else (lambda r: r[at])
            res = (g,) + _adam_math(pick(w_ref), g, pick(m_ref), pick(v_ref))
            for o_ref, val in zip(outs, res):
                if at is None:
                    o_ref[...] = val
                else:
                    o_ref[at] = val

        for j in range(len(VEC_NAMES)):
            update(j, g_ref[ROW_VEC + j:ROW_VEC + j + 1, :])
        update(len(VEC_NAMES), g_ref[ROW_SINKS:ROW_SINKS + 1, 0:NQ])
        update(len(VEC_NAMES) + 1, gc_ref[...], at=0)
        for gi, row0 in ((len(VEC_NAMES) + 2, ROW_RGATE), (len(VEC_NAMES) + 3, ROW_IGATE)):
            for nb in range(NQ):
                update(gi, g_ref[row0:row0 + HD, HD * nb:HD * nb + HD], at=(0, nb))

    vmem = pl.BlockSpec(memory_space=pltpu.VMEM)
    out_shape = [jax.ShapeDtypeStruct(w[k].shape, f32) for k in names for _ in range(4)]
    outs = pl.pallas_call(
        body, name="adamw_small",
        in_specs=[vmem] * len(ins), out_specs=[vmem] * len(out_shape), out_shape=out_shape,
        compiler_params=pltpu.CompilerParams(vmem_limit_bytes=32 * MIB),
    )(*ins)
    return {k: tuple(outs[4 * j:4 * j + 4]) for j, k in enumerate(names)}


def _pack_small(small, loss):
    gate = lambda g: g.transpose(1, 0, 2).reshape(HD, NQ * HD)
    row_s = jnp.concatenate([small["sinks"], loss[:, LOSS_LANE:128], jnp.zeros((1, D - 128), f32)], axis=1)
    rep = jnp.concatenate([gate(small["w_rgate"]), gate(small["w_igate"])] + [small[k] for k in VEC_NAMES]
                          + [row_s, jnp.zeros((SMALL_ROWS - ROW_SINKS - 1, D), f32)], axis=0)
    conv = small["conv_w"].reshape(CONVW, NDEV, 128).transpose(1, 0, 2)
    conv = jnp.pad(conv, ((0, 0), (0, 8 - CONVW), (0, D - 128)))
    return jnp.concatenate([rep.reshape(NDEV, SMALL_PER, D), conv], axis=1).reshape(NDEV * (SMALL_PER + 8), D)


def kernel(x, ln_gain, w_in, sinks, conv_w, conv_b, w_rgate, b_rgate, w_igate, b_igate, lru_lambda, attn_out_gain, lru_out_gain, w_out, final_gain, loss_target, m_ln_gain, m_w_in, m_sinks, m_conv_w, m_conv_b, m_w_rgate, m_b_rgate, m_w_igate, m_b_igate, m_lru_lambda, m_attn_out_gain, m_lru_out_gain, m_w_out, m_final_gain, v_ln_gain, v_w_in, v_sinks, v_conv_w, v_conv_b, v_w_rgate, v_b_rgate, v_w_igate, v_b_igate, v_lru_lambda, v_attn_out_gain, v_lru_out_gain, v_w_out, v_final_gain):
    w = dict(ln_gain=ln_gain, sinks=sinks, conv_w=conv_w, conv_b=conv_b, w_rgate=w_rgate, b_rgate=b_rgate,
             w_igate=w_igate, b_igate=b_igate, lru_lambda=lru_lambda, attn_out_gain=attn_out_gain,
             lru_out_gain=lru_out_gain, final_gain=final_gain.reshape(1, D))
    m = dict(ln_gain=m_ln_gain, sinks=m_sinks, conv_w=m_conv_w, conv_b=m_conv_b, w_rgate=m_w_rgate,
             b_rgate=m_b_rgate, w_igate=m_w_igate, b_igate=m_b_igate, lru_lambda=m_lru_lambda,
             attn_out_gain=m_attn_out_gain, lru_out_gain=m_lru_out_gain, final_gain=m_final_gain.reshape(1, D))
    v = dict(ln_gain=v_ln_gain, sinks=v_sinks, conv_w=v_conv_w, conv_b=v_conv_b, w_rgate=v_w_rgate,
             b_rgate=v_b_rgate, w_igate=v_w_igate, b_igate=v_b_igate, lru_lambda=v_lru_lambda,
             attn_out_gain=v_attn_out_gain, lru_out_gain=v_lru_out_gain, final_gain=v_final_gain.reshape(1, D))

    conv_blk = jnp.pad(conv_w[0], ((0, 8 - CONVW), (0, 0)))
    wt, cw_all = _all_gather([w_in[0].T, conv_blk], [bf16, f32], "gather_weights")
    conv_full = cw_all.reshape(NDEV, 8, 128)[:, 0:CONVW].transpose(1, 0, 2).reshape(CONVW, LW)

    p = {k: (w[k][0] if k in ("w_rgate", "w_igate") else w[k]) for k in w if k != "conv_w"}
    gx, land_wt, land_wo, land_sm = _sequence_step(x[0], loss_target[0], wt, w_out[0], conv_full, p)

    g_sm = _sum_slots(land_sm, SMALL_PER + 8, "sum_small")
    (g_rep,) = _all_gather([g_sm[0:SMALL_PER]], [f32], "gather_small")
    g_conv = g_sm[SMALL_PER:SMALL_PER + CONVW, 0:128]

    wins = _reduce_adamw(land_wt, w_in[0].T, m_w_in[0].T, v_w_in[0].T, 192, "adamw_w_in")
    g_win, d_win, m_win, v_win = (t.T for t in wins)
    g_wo, d_wo, m_wo, v_wo = _reduce_adamw(land_wo, w_out[0], m_w_out[0], v_w_out[0], 256, "adamw_w_out")
    res = _adamw_small(g_rep, g_conv, w, m, v)
    res["w_in"] = tuple(t[None] for t in (g_win, d_win, m_win, v_win))
    res["w_out"] = tuple(t[None] for t in (g_wo, d_wo, m_wo, v_wo))
    res["final_gain"] = tuple(t.reshape(D) for t in res["final_gain"])

    order = ("ln_gain", "w_in", "sinks", "conv_w", "conv_b", "w_rgate", "b_rgate", "w_igate", "b_igate",
             "lru_lambda", "attn_out_gain", "lru_out_gain", "w_out", "final_gain")
    total_loss = g_rep[ROW_SINKS, LOSS_LANE]
    return (total_loss, gx[None]) + tuple(res[k][i] for i in range(4) for k in order)
```

```python
import jax
import jax.numpy as jnp
from jax import lax
from jax.experimental import pallas as pl
from jax.experimental.pallas import tpu as pltpu

f32 = jnp.float32
bf16 = jnp.bfloat16

D = 1024
HD = 64
NQ = 16
NKV = 4
GROUP = NQ // NKV
KVW = NKV * HD
BLK = 128
ROT = 16
THETA = 500000.0
NEG = -1e30
LW = 1024
NGRP = 4
CONVW = 4
LRU_C = 8.0
NIN = 4608
EPS = 1e-6
NDEV = 8
WT_ROWS = NIN // NDEV
WO_ROWS = 2 * D // NDEV
SMALL_ROWS = 192
SMALL_PER = SMALL_ROWS // NDEV

ADAM_LR = 0.001
ADAM_B1 = 0.9
ADAM_B2 = 0.999
ADAM_EPS = 1e-08
ADAM_WD = 0.01
ADAM_STEP = 10

NT = (((1,), (1,)), ((), ()))
TN = (((0,), (0,)), ((), ()))
MESH = pl.DeviceIdType.MESH
MIB = 1024 * 1024


def _dot(a, b):
    return jnp.dot(a, b, preferred_element_type=f32)


def _dot_nt(a, b):
    return lax.dot_general(a, b, NT, preferred_element_type=f32)


def _dot_tn(a, b):
    return lax.dot_general(a, b, TN, preferred_element_type=f32)


def _params(sem, vmem_mib):
    return pltpu.CompilerParams(dimension_semantics=sem, vmem_limit_bytes=vmem_mib * MIB)


def _sigmoid(x):
    return 0.5 * jnp.tanh(0.5 * x) + 0.5


def _softplus(x):
    return jnp.maximum(x, 0.0) + jnp.log(1.0 + jnp.exp(-jnp.abs(x)))


def _rope_tables(s):
    pos = jnp.arange(s, dtype=f32)
    inv_freq = THETA ** (-jnp.arange(0, ROT, 2, dtype=f32) / ROT)
    ang = pos[:, None] * inv_freq[None, :]
    cs = jnp.concatenate([jnp.cos(ang) - 1.0, jnp.sin(ang)], axis=1)
    d = jnp.arange(128) % HD
    j = jnp.arange(ROT)[:, None]
    pick_c = ((d < ROT) & (j == d % (ROT // 2))).astype(f32)
    pick_sa = ((d >= ROT // 2) & (d < ROT) & (j == d)).astype(f32)
    pick_sb = -((d < ROT // 2) & (j == d + ROT // 2)).astype(f32)
    picks = jnp.concatenate([pick_c, pick_sa, pick_sb], axis=1)
    ones = jnp.concatenate([jnp.ones((1, 128), f32), jnp.zeros((1, 256), f32)], axis=1)
    return jnp.dot(cs, picks, precision=lax.Precision.HIGHEST) + ones


def _tables(tab_ref):
    return tab_ref[:, 0:128], tab_ref[:, 128:256], tab_ref[:, 256:384]


def _rope(t, c, sa, sb):
    return t * c + pltpu.roll(t, 8, 1) * sa + pltpu.roll(t, 120, 1) * sb


def _unrope_t(dr, c, sa, sb):
    return dr * c + pltpu.roll(dr * sa, 120, 0) + pltpu.roll(dr * sb, 8, 0)


def _place():
    return lax.axis_index("x"), lax.axis_index("y"), lax.axis_index("c")


def _gather_ops(mine_refs, out_refs, send_sems, recv_sems, local_sems):
    n = len(mine_refs)
    x, y, c = _place()
    me, sibling = (x, y, c), (x, y, 1 - c)
    chips = [(1 - x, y), (x, 1 - y), (1 - x, 1 - y)]

    def rows(a, dev):
        m = mine_refs[a].shape[0]
        return out_refs[a].at[pl.ds((4 * dev[0] + 2 * dev[1] + dev[2]) * m, m), :]

    def copy(a, k, block, to, own=False):
        return pltpu.make_async_remote_copy(
            src_ref=mine_refs[a] if own else rows(a, block), dst_ref=rows(a, block),
            send_sem=send_sems.at[a, k], recv_sem=recv_sems.at[a, k], device_id=to, device_id_type=MESH)

    def local(a):
        return pltpu.make_async_copy(mine_refs[a], rows(a, me), local_sems.at[a])

    def first(a):
        return [copy(a, 0, me, sibling, own=True)] + [copy(a, 1 + j, me, (*chip, c), own=True)
                                                      for j, chip in enumerate(chips)]

    def start():
        for a in range(n):
            local(a).start()
            for cp in first(a):
                cp.start()

    def finish():
        for j, chip in enumerate(chips):
            for a in range(n):
                copy(a, 1 + j, (*chip, c), me).wait_recv()
                copy(a, 4 + j, (*chip, c), sibling).start()
        for a in range(n):
            copy(a, 0, sibling, me).wait_recv()
            for j, chip in enumerate(chips):
                copy(a, 4 + j, (*chip, 1 - c), me).wait_recv()
        for a in range(n):
            for cp in first(a) + [copy(a, 4 + j, (*chip, c), sibling) for j, chip in enumerate(chips)]:
                cp.wait_send()
            local(a).wait()

    return start, finish


def _relay_gather_ops(mine_refs, out_refs, send_sems, recv_sems, local_sems):
    n = len(mine_refs)
    x, y, c = _place()
    me, sibling = (x, y, c), (x, y, 1 - c)
    near = (x ^ (1 - c), y ^ c)
    far = (x ^ c, y ^ (1 - c))
    diag = (1 - x, 1 - y)

    def rows(a, dev):
        m = mine_refs[a].shape[0]
        return out_refs[a].at[pl.ds((4 * dev[0] + 2 * dev[1] + dev[2]) * m, m), :]

    def copy(a, k, block, to, own=False):
        return pltpu.make_async_remote_copy(
            src_ref=mine_refs[a] if own else rows(a, block), dst_ref=rows(a, block),
            send_sem=send_sems.at[a, k], recv_sem=recv_sems.at[a, k], device_id=to, device_id_type=MESH)

    def local(a):
        return pltpu.make_async_copy(mine_refs[a], rows(a, me), local_sems.at[a])

    def sends(a):
        return [copy(a, 0, me, sibling, own=True), copy(a, 1, me, (*near, c), own=True),
                copy(a, 2, me, (*far, c), own=True), copy(a, 3, (*near, c), (*far, c)),
                copy(a, 4, (*near, c), sibling), copy(a, 5, (*far, c), sibling), copy(a, 6, (*diag, c), sibling)]

    def arrivals(a):
        return [copy(a, 0, sibling, me), copy(a, 1, (*near, c), me), copy(a, 2, (*far, c), me),
                copy(a, 3, (*diag, c), me), copy(a, 4, (*far, 1 - c), me), copy(a, 5, (*near, 1 - c), me),
                copy(a, 6, (*diag, 1 - c), me)]

    def start():
        for a in range(n):
            local(a).start()
            for cp in sends(a)[0:3]:
                cp.start()

    def finish():
        for first, then in ((1, (3, 4)), (2, (5,)), (3, (6,))):
            for a in range(n):
                arrivals(a)[first].wait_recv()
                for k in then:
                    sends(a)[k].start()
        for a in range(n):
            for k in (0, 4, 5, 6):
                arrivals(a)[k].wait_recv()
        for a in range(n):
            for cp in sends(a):
                cp.wait_send()
            local(a).wait()

    return start, finish


def _scatter_ops(src_refs, land_refs, send_sems, recv_sems, local_sems):
    n = len(src_refs)
    x, y, c = _place()
    my = 4 * x + 2 * y + c

    def peer(k):
        return x ^ (k >> 2), y ^ ((k >> 1) & 1), c ^ (k & 1)

    def piece(a, dev):
        m = src_refs[a].shape[0] // NDEV
        return src_refs[a].at[pl.ds(dev * m, m), :]

    def local(a):
        return pltpu.make_async_copy(piece(a, my), land_refs[a].at[my], local_sems.at[a])

    def send(a, k):
        px, py, pc = peer(k)
        return pltpu.make_async_remote_copy(
            src_ref=piece(a, 4 * px + 2 * py + pc), dst_ref=land_refs[a].at[my],
            send_sem=send_sems.at[a, k - 1], recv_sem=recv_sems.at[a, k - 1],
            device_id=(px, py, pc), device_id_type=MESH)

    def arrival(a, k):
        px, py, pc = peer(k)
        return pltpu.make_async_remote_copy(
            src_ref=piece(a, my), dst_ref=land_refs[a].at[4 * px + 2 * py + pc],
            send_sem=send_sems.at[a, k - 1], recv_sem=recv_sems.at[a, k - 1],
            device_id=(px, py, pc), device_id_type=MESH)

    def start():
        for a in range(n):
            local(a).start()
        for k in range(1, NDEV):
            for a in range(n):
                send(a, k).start()

    def finish():
        for k in range(1, NDEV):
            for a in range(n):
                send(a, k).wait_send()
        for k in range(1, NDEV):
            for a in range(n):
                arrival(a, k).wait_recv()
        for a in range(n):
            local(a).wait()

    return start, finish


def _in_hbm(*arrays):
    return tuple(pltpu.with_memory_space_constraint(a, pltpu.HBM) for a in arrays)


def _comm_sems(n):
    return [pltpu.SemaphoreType.DMA((n, 7)), pltpu.SemaphoreType.DMA((n, 7)), pltpu.SemaphoreType.DMA((n,))]


HBM = pl.BlockSpec(memory_space=pltpu.HBM)


def _sink_rows(sinks):
    return jnp.repeat(sinks.reshape(NKV, GROUP), BLK, axis=1)


def _band_softmax(s2_ref, ls, prev_offset, sink_row):
    jj = lax.broadcasted_iota(jnp.int32, (BLK, BLK), 0)
    ii = lax.broadcasted_iota(jnp.int32, (BLK, BLK), 1)
    from_prev = jj > ii
    sc = jnp.where(from_prev, s2_ref[0:BLK, ls] + prev_offset, s2_ref[BLK:2 * BLK, ls])
    m = jnp.maximum(jnp.max(sc, axis=0, keepdims=True), sink_row)
    p = jnp.exp(sc - m)
    es = jnp.exp(sink_row - m)
    inv = 1.0 / (jnp.sum(p, axis=0, keepdims=True) + es)
    return from_prev, p * inv, es * inv


def _put_split(dst_ref, ls, t, from_prev):
    t = t.astype(bf16)
    zero = jnp.zeros_like(t)
    dst_ref[0:BLK, ls] = jnp.where(from_prev, t, zero)
    dst_ref[BLK:2 * BLK, ls] = jnp.where(from_prev, zero, t)


def _heads_side_by_side(ref, h):
    return jnp.concatenate([ref[HD * (GROUP * h + g):HD * (GROUP * h + g) + HD, :] for g in range(GROUP)], axis=1)


def _kv_specs_t():
    prev = pl.BlockSpec((KVW, BLK), lambda n: (0, jnp.maximum(n - 1, 0)))
    cur = pl.BlockSpec((KVW, BLK), lambda n: (0, n))
    return [prev, cur, prev, cur]


def _attn_fwd_t(qt, kt, vt, sinks):
    s = qt.shape[1]

    def body(sink_ref, q_ref, kp_ref, kc_ref, vp_ref, vc_ref, o_ref, s2_scr, pn2_scr):
        n = pl.program_id(0)
        off = jnp.where(n > 0, 0.0, NEG)

        def scores(h):
            hs = slice(HD * h, HD * h + HD)
            kh = jnp.concatenate([kp_ref[hs, :], kc_ref[hs, :]], axis=1)
            s2_scr[h % 2] = _dot_tn(kh, _heads_side_by_side(q_ref, h))

        def probs(h):
            for g in range(GROUP):
                ls = slice(BLK * g, BLK * g + BLK)
                from_prev, pn, _ = _band_softmax(s2_scr.at[h % 2], ls, off, sink_ref[h:h + 1, ls])
                _put_split(pn2_scr.at[h % 2], ls, pn, from_prev)

        def outputs(h):
            hs = slice(HD * h, HD * h + HD)
            vh = jnp.concatenate([vp_ref[hs, :], vc_ref[hs, :]], axis=1)
            og = _dot(vh, pn2_scr[h % 2])
            for g in range(GROUP):
                a = GROUP * h + g
                o_ref[HD * a:HD * a + HD, :] = og[:, BLK * g:BLK * g + BLK]

        scores(0)
        for h in range(NKV):
            if h + 1 < NKV:
                scores(h + 1)
            probs(h)
            outputs(h)

    return pl.pallas_call(
        body, name="attn_fwd", grid=(s // BLK,),
        in_specs=[pl.BlockSpec((NKV, GROUP * BLK), lambda n: (0, 0)), pl.BlockSpec((D, BLK), lambda n: (0, n))]
        + _kv_specs_t(),
        out_specs=pl.BlockSpec((D, BLK), lambda n: (0, n)),
        out_shape=pltpu.HBM((D, s), f32),
        scratch_shapes=[pltpu.VMEM((2, 2 * BLK, GROUP * BLK), f32), pltpu.VMEM((2, 2 * BLK, GROUP * BLK), bf16)],
        compiler_params=_params(("arbitrary",), 32),
    )(_sink_rows(sinks), *_in_hbm(qt, kt, kt, vt, vt))


def _attn_bwd_t(qt, kt, vt, dot, sinks, dwo):
    s = qt.shape[1]
    nb = s // BLK

    def body(sink_ref, q_ref, do_ref, kp_ref, kc_ref, vp_ref, vc_ref, dwo_ref, dq_ref, dk_ref, dv_ref, ds_ref,
             land_ref, dk_hold, dv_hold, s2_scr, dp2_scr, pn2_scr, ds2_scr, send_sems, recv_sems, local_sems):
        n = pl.program_id(0)
        start, finish = _scatter_ops([dwo_ref], [land_ref], send_sems, recv_sems, local_sems)

        @pl.when(n == 0)
        def _():
            start()
            dk_hold[...] = jnp.zeros_like(dk_hold)
            dv_hold[...] = jnp.zeros_like(dv_hold)
            ds_ref[...] = jnp.zeros_like(ds_ref)

        @pl.when(n < nb)
        def _():
            off = jnp.where(n > 0, 0.0, NEG)

            def scores(h):
                hs = slice(HD * h, HD * h + HD)
                kh = jnp.concatenate([kp_ref[hs, :], kc_ref[hs, :]], axis=1)
                vh = jnp.concatenate([vp_ref[hs, :], vc_ref[hs, :]], axis=1)
                s2_scr[h % 2] = _dot_tn(kh, _heads_side_by_side(q_ref, h))
                dp2_scr[h % 2] = _dot_tn(vh, _heads_side_by_side(do_ref, h))

            def softmax_bwd(h):
                for g in range(GROUP):
                    ls = slice(BLK * g, BLK * g + BLK)
                    from_prev, pn, ps = _band_softmax(s2_scr.at[h % 2], ls, off, sink_ref[h:h + 1, ls])
                    dp = jnp.where(from_prev, dp2_scr[h % 2, 0:BLK, ls], dp2_scr[h % 2, BLK:2 * BLK, ls])
                    dsum = jnp.sum(pn * dp, axis=0, keepdims=True)
                    ds_ref[h:h + 1, ls] += -ps * dsum
                    _put_split(pn2_scr.at[h % 2], ls, pn, from_prev)
                    _put_split(ds2_scr.at[h % 2], ls, pn * (dp - dsum), from_prev)

            def grads(h):
                hs = slice(HD * h, HD * h + HD)
                kh = jnp.concatenate([kp_ref[hs, :], kc_ref[hs, :]], axis=1)
                dqg = _dot(kh, ds2_scr[h % 2])
                for g in range(GROUP):
                    a = GROUP * h + g
                    dq_ref[HD * a:HD * a + HD, :] = dqg[:, BLK * g:BLK * g + BLK]
                dkh = _dot_nt(_heads_side_by_side(q_ref, h), ds2_scr[h % 2])
                dvh = _dot_nt(_heads_side_by_side(do_ref, h), pn2_scr[h % 2])
                dk_ref[hs, :] = dk_hold[hs, :] + dkh[:, 0:BLK]
                dv_ref[hs, :] = dv_hold[hs, :] + dvh[:, 0:BLK]
                dk_hold[hs, :] = dkh[:, BLK:2 * BLK]
                dv_hold[hs, :] = dvh[:, BLK:2 * BLK]

            scores(0)
            for h in range(NKV):
                if h + 1 < NKV:
                    scores(h + 1)
                softmax_bwd(h)
                grads(h)

        @pl.when(n == nb)
        def _():
            dk_ref[...] = dk_hold[...]
            dv_ref[...] = dv_hold[...]
            finish()

    blk = pl.BlockSpec((D, BLK), lambda n: (0, jnp.minimum(n, nb - 1)))
    late = pl.BlockSpec((KVW, BLK), lambda n: (0, jnp.maximum(n - 1, 0)))
    whole = pl.BlockSpec((NKV, GROUP * BLK), lambda n: (0, 0))
    kv = [pl.BlockSpec((KVW, BLK), lambda n: (0, jnp.clip(n - 1, 0, nb - 1))),
          pl.BlockSpec((KVW, BLK), lambda n: (0, jnp.minimum(n, nb - 1)))]
    return pl.pallas_call(
        body, name="attn_bwd", grid=(nb + 1,),
        in_specs=[whole, blk, blk] + kv + kv + [HBM],
        out_specs=[blk, late, late, whole, HBM],
        out_shape=[pltpu.HBM((D, s), f32), pltpu.HBM((KVW, s), f32), pltpu.HBM((KVW, s), f32),
                   jax.ShapeDtypeStruct((NKV, GROUP * BLK), f32), pltpu.HBM((NDEV, WO_ROWS, D), bf16)],
        scratch_shapes=[pltpu.VMEM((KVW, BLK), f32), pltpu.VMEM((KVW, BLK), f32)]
        + [pltpu.VMEM((2, 2 * BLK, GROUP * BLK), f32)] * 2 + [pltpu.VMEM((2, 2 * BLK, GROUP * BLK), bf16)] * 2
        + _comm_sems(1),
        compiler_params=_params(("arbitrary",), 48),
    )(_sink_rows(sinks), *_in_hbm(qt, dot, kt, kt, vt, vt, dwo))


def _block_diag(w):
    w4 = w.reshape(NGRP, 4, HD, HD)
    eye = jnp.eye(4, dtype=w.dtype)
    return jnp.einsum('gjcd,jk->gjckd', w4, eye).reshape(NGRP, 256, 256).astype(bf16)


def _gate_terms(pr, pi, br, bi, sp):
    r = _sigmoid(pr + br)
    i = _sigmoid(pi + bi)
    la = -LRU_C * r * sp
    a = jnp.exp(la)
    x2 = 2.0 * la
    y = jnp.where(x2 > -0.02, -x2 * (1.0 + x2 * (0.5 + x2 * (1.0 / 6.0))), 1.0 - a * a)
    inv_mult = lax.rsqrt(jnp.maximum(y, 1e-30))
    return r, i, a, y * inv_mult, inv_mult


def _later(x, before, k):
    if k == 0:
        return x
    row = lax.broadcasted_iota(jnp.int32, before.shape, 0)
    rolled = pltpu.roll(x, k, 0)
    first = jnp.where(row < k, pltpu.roll(before, k, 0), rolled[0:8])
    return jnp.concatenate([first, rolled[8:]], axis=0)


def _earlier(x, after, k):
    if k == 0:
        return x
    n = x.shape[0]
    row = lax.broadcasted_iota(jnp.int32, after.shape, 0)
    rolled = pltpu.roll(x, n - k, 0)
    last = jnp.where(row >= 8 - k, pltpu.roll(after, 8 - k, 0), rolled[n - 8:n])
    return jnp.concatenate([rolled[0:n - 8], last], axis=0)


def _fwd_fused(x, ln_gain, wt, tabs, wo_shard, conv_w, conv_b, wr, wi, br, bi, lam, tm):
    s = x.shape[0]
    nt = s // tm
    nc = 512
    pieces = 8
    rows_per = tm // pieces
    later_chunks = (0, 1, 2, 3, 4, 7, 8)

    def body(x0_ref, xn_ref, g_ref, wt_ref, tab_ref, wo_ref, cw_ref, cb_ref, wr_ref, wi_ref, br_ref,
             bi_ref, lam_ref, h_ref, q_ref, k_ref, v_ref, ga_ref, xl_ref, gl_ref, u_ref, hl_ref, r_ref, ig_ref, a_ref,
             im_ref, wo_all, wo_stage, hb, halo, ub_scr, pr_scr, pi_scr, b_scr, hcar,
             send_sems, recv_sems, local_sems):
        i = pl.program_id(0)
        start, finish = _gather_ops([wo_stage], [wo_all], send_sems, recv_sems, local_sems)
        gain = g_ref[...]

        def normed(xx):
            rstd = lax.rsqrt(jnp.mean(xx * xx, axis=-1, keepdims=True) + EPS)
            return (xx * rstd * gain).astype(bf16)

        @pl.when(i == 0)
        def _():
            wo_stage[...] = wo_ref[...].astype(bf16)
            start()
            hb[0] = normed(x0_ref[...])
            halo[...] = jnp.zeros_like(halo)
            hcar[...] = jnp.zeros_like(hcar)

        cur, nxt = i % 2, (i + 1) % 2
        sp = _softplus(-lam_ref[...])
        br, bi = br_ref[...], bi_ref[...]
        c, sa, sb = _tables(tab_ref)
        piece_rows = lambda p: slice(rows_per * p, rows_per * p + rows_per)

        def project(ci):
            z = _dot_nt(hb[cur], wt_ref[ci * nc:(ci + 1) * nc, :])
            if ci < 2:
                for j in range(nc // 128):
                    r = _rope(z[:, 128 * j:128 * j + 128], c, sa, sb) * (HD ** -0.5)
                    q_ref[ci * nc + 128 * j:ci * nc + 128 * j + 128, :] = r.astype(bf16).T
            elif ci == 2:
                for j in range(2):
                    js = slice(128 * j, 128 * j + 128)
                    k_ref[js, :] = _rope(z[:, js], c, sa, sb).astype(bf16).T
                    v_ref[js, :] = z[:, KVW + 128 * j:KVW + 128 * j + 128].astype(bf16).T
            else:
                sec, j = divmod(ci - 3, 2)
                (ga_ref, xl_ref, gl_ref)[sec][:, j * nc:(j + 1) * nc] = z

        def gate_terms(p):
            rows = piece_rows(p)
            r, ig, a, mult, inv_mult = _gate_terms(pr_scr[rows, :], pi_scr[rows, :], br, bi, sp)
            r_ref[rows, :] = r
            ig_ref[rows, :] = ig
            a_ref[rows, :] = a
            im_ref[rows, :] = inv_mult
            b_scr[rows, :] = mult * (ig * u_ref[rows, :])

        def scan(p, hc):
            for t in range(rows_per * p, rows_per * p + rows_per):
                hc = a_ref[t:t + 1, :] * hc + b_scr[t:t + 1, :]
                hl_ref[t:t + 1, :] = hc
            return hc

        def norm_next(p):
            hb[nxt, piece_rows(p), :] = normed(xn_ref[piece_rows(p), :])

        h_ref[...] = hb[cur]
        project(5)
        project(6)
        xl = xl_ref[...]
        u = cb_ref[...] + sum(cw_ref[k:k + 1, :] * _later(xl, halo[...], CONVW - 1 - k) for k in range(CONVW))
        halo[...] = xl[tm - 8:tm, :]
        u_ref[...] = u
        ub_scr[...] = u.astype(bf16)
        for g in range(NGRP):
            gs = slice(256 * g, 256 * g + 256)
            pr_scr[:, gs] = _dot(ub_scr[:, gs], wr_ref[g])
            pi_scr[:, gs] = _dot(ub_scr[:, gs], wi_ref[g])
        hc = hcar[...]
        gate_terms(0)
        for slot, ci in enumerate(later_chunks):
            project(ci)
            norm_next(slot)
            gate_terms(slot + 1)
            hc = scan(slot, hc)
        norm_next(pieces - 1)
        hcar[...] = scan(pieces - 1, hc)

        @pl.when(i == nt - 1)
        def _():
            finish()

    row = lambda w: pl.BlockSpec((tm, w), lambda i: (i, 0))
    col = lambda w: pl.BlockSpec((w, tm), lambda i: (0, i))
    full = lambda a: pl.BlockSpec(a.shape, lambda i: (0,) * a.ndim)
    big = lambda w, dt: pltpu.HBM((s, w), dt)
    tile = pltpu.VMEM((tm, LW), f32)
    return pl.pallas_call(
        body, name="fwd_fused", grid=(nt,),
        in_specs=[pl.BlockSpec((tm, D), lambda i: (0, 0)), pl.BlockSpec((tm, D), lambda i: (jnp.minimum(i + 1, nt - 1), 0)),
                  full(ln_gain), full(wt), row(384), full(wo_shard), full(conv_w), full(conv_b),
                  full(wr), full(wi), full(br), full(bi), full(lam)],
        out_specs=[row(D), col(D), col(KVW), col(KVW), row(D), row(D), row(D)] + [row(LW)] * 6 + [HBM],
        out_shape=[big(D, bf16), pltpu.HBM((D, s), bf16), pltpu.HBM((KVW, s), bf16), pltpu.HBM((KVW, s), bf16),
                   big(D, f32), big(D, f32), big(D, f32)] + [big(LW, f32)] * 6 + [pltpu.HBM((2 * D, D), bf16)],
        scratch_shapes=[pltpu.VMEM((WO_ROWS, D), bf16), pltpu.VMEM((2, tm, D), bf16), pltpu.VMEM((8, LW), f32),
                        pltpu.VMEM((tm, LW), bf16), tile, tile, tile, pltpu.VMEM((1, LW), f32)] + _comm_sems(1),
        compiler_params=_params(("arbitrary",), 56),
    )(*_in_hbm(x, x), ln_gain, *_in_hbm(wt), tabs, wo_shard, conv_w, conv_b, wr, wi, br, bi, lam)


def _lru_bwd(u, hl, dhl, xl, r, ig, a, im, conv_w, wr, wi, lam, tm):
    s = u.shape[0]
    nt = s // tm
    pieces = 8
    rows_per = tm // pieces

    def body(u_ref, h_ref, hp_ref, dh_ref, x_ref, r_ref, ig_ref, a_ref, im_ref, cw_ref, wr_ref, wi_ref,
             lam_ref, dxl_ref, dwr_ref, dwi_ref, dbr_ref, dbi_ref, dlam_ref, dcb_ref, dcw_ref,
             l_scr, du_scr, dpr_scr, dpi_scr, lcar, dunext):
        t0 = pl.program_id(0)
        tile = nt - 1 - t0

        @pl.when(t0 == 0)
        def _():
            lcar[...] = jnp.zeros_like(lcar)
            dunext[...] = jnp.zeros_like(dunext)
            for ref in (dwr_ref, dwi_ref, dbr_ref, dbi_ref, dlam_ref, dcb_ref, dcw_ref):
                ref[...] = jnp.zeros_like(ref)

        lam = lam_ref[...]
        sp = _softplus(-lam)
        hp = jnp.where(tile > 0, hp_ref[...], 0.0)

        def scan(p, c):
            for t in range(rows_per * p + rows_per - 1, rows_per * p - 1, -1):
                lt = dh_ref[t:t + 1, :] + c
                l_scr[t:t + 1, :] = lt
                c = a_ref[t:t + 1, :] * lt
            return c

        def terms(p, sums):
            rows = slice(rows_per * p, rows_per * p + rows_per)
            lt, u, r, i, a, inv_mult = l_scr[rows, :], u_ref[rows, :], r_ref[rows, :], ig_ref[rows, :], \
                a_ref[rows, :], im_ref[rows, :]
            before = hp if p == 0 else h_ref[rows_per * p - 8:rows_per * p, :]
            hprev = _later(h_ref[rows, :], before, 1)
            x2 = -2.0 * LRU_C * r * sp
            mult = jnp.where(x2 > -0.02, -x2 * (1.0 + x2 * (0.5 + x2 * (1.0 / 6.0))), 1.0 - a * a) * inv_mult
            da = lt * hprev
            dmult = lt * (i * u)
            di = lt * mult * u
            du_scr[rows, :] = lt * mult * i
            dla = da * a - dmult * (a * a) * inv_mult
            dr = dla * (-LRU_C * sp)
            dpr = dr * r * (1.0 - r)
            dpi = di * i * (1.0 - i)
            dpr_scr[rows, :] = dpr.astype(bf16)
            dpi_scr[rows, :] = dpi.astype(bf16)
            col = lambda t: jnp.sum(t, axis=0, keepdims=True)
            return sums[0] + col(dla * (-LRU_C * r)), sums[1] + col(dpr), sums[2] + col(dpi)

        sums = (jnp.zeros((1, LW), f32),) * 3
        c = scan(pieces - 1, lcar[...])
        for p in range(pieces - 1, -1, -1):
            if p > 0:
                c = scan(p - 1, c)
            sums = terms(p, sums)
        lcar[...] = c
        dlam_ref[...] += sums[0]
        dbr_ref[...] += sums[1]
        dbi_ref[...] += sums[2]

        ub = u_ref[...].astype(bf16)
        dug = []
        for g in range(NGRP):
            gs = slice(256 * g, 256 * g + 256)
            dwr_ref[g] += _dot_tn(ub[:, gs], dpr_scr[:, gs])
            dwi_ref[g] += _dot_tn(ub[:, gs], dpi_scr[:, gs])
            dug.append(_dot_nt(dpr_scr[:, gs], wr_ref[g]) + _dot_nt(dpi_scr[:, gs], wi_ref[g]))
        du = du_scr[...] + jnp.concatenate(dug, axis=1)

        dcb_ref[...] += jnp.sum(du, axis=0, keepdims=True)
        x = x_ref[...]
        after = dunext[...]
        dxl = jnp.zeros_like(du)
        for k in range(CONVW):
            e = _earlier(du, after, CONVW - 1 - k)
            dxl = dxl + cw_ref[k:k + 1, :] * e
            dcw_ref[k:k + 1, :] += jnp.sum(e * x, axis=0, keepdims=True)
        dxl_ref[...] = dxl.astype(bf16)
        dunext[...] = du[0:8, :]

        @pl.when(t0 == nt - 1)
        def _():
            dlam_ref[...] = dlam_ref[...] * (-_sigmoid(-lam))

    rev = lambda i: (nt - 1 - i, 0)
    row = pl.BlockSpec((tm, LW), rev)
    prev8 = pl.BlockSpec((8, LW), lambda i: (jnp.maximum((nt - 1 - i) * (tm // 8) - 1, 0), 0))
    full = lambda a: pl.BlockSpec(a.shape, lambda i: (0,) * a.ndim)
    vec = pl.BlockSpec((1, LW), lambda i: (0, 0))
    bd = pl.BlockSpec((NGRP, 256, 256), lambda i: (0, 0, 0))
    return pl.pallas_call(
        body, name="lru_bwd", grid=(nt,),
        in_specs=[row, row, prev8, row, row, row, row, row, row, full(conv_w), full(wr), full(wi), full(lam)],
        out_specs=[row, bd, bd, vec, vec, vec, vec, pl.BlockSpec((CONVW, LW), lambda i: (0, 0))],
        out_shape=[pltpu.HBM((s, LW), bf16),
                   jax.ShapeDtypeStruct((NGRP, 256, 256), f32), jax.ShapeDtypeStruct((NGRP, 256, 256), f32),
                   jax.ShapeDtypeStruct((1, LW), f32), jax.ShapeDtypeStruct((1, LW), f32),
                   jax.ShapeDtypeStruct((1, LW), f32), jax.ShapeDtypeStruct((1, LW), f32),
                   jax.ShapeDtypeStruct((CONVW, LW), f32)],
        scratch_shapes=[pltpu.VMEM((tm, LW), f32), pltpu.VMEM((tm, LW), f32), pltpu.VMEM((tm, LW), bf16),
                        pltpu.VMEM((tm, LW), bf16), pltpu.VMEM((1, LW), f32), pltpu.VMEM((8, LW), f32)],
        compiler_params=_params(("arbitrary",), 56),
    )(*_in_hbm(u, hl, hl, dhl, xl, r, ig, a, im), conv_w, wr, wi, lam)


def _gated_norm(t, gate, gain):
    sg = _sigmoid(gate)
    silu = gate * sg
    p = t * silu
    rstd = lax.rsqrt(jnp.mean(p * p, axis=-1, keepdims=True) + EPS)
    ph = p * rstd
    return sg, silu, rstd, ph, ph * gain


def _gated_norm_bwd(dy, t, gate, gain, sg, silu, rstd, ph):
    w = dy * gain
    dp = rstd * (w - ph * jnp.mean(w * ph, axis=-1, keepdims=True))
    dgate = dp * t * (sg * (1.0 + gate * (1.0 - sg)))
    return jnp.sum(dy * ph, axis=0, keepdims=True), dp * silu, dgate


def _out_fwd_bwd(x, tgt, o, ga, hl, gl, again, lgain, fgain, wo, tm):
    s = x.shape[0]
    nt = s // tm

    def body(x_ref, t_ref, o_ref, ga_ref, hl_ref, gl_ref, ag_ref, lg_ref, fg_ref, wo_ref,
             dx2_ref, do_ref, dga_ref, dhl_ref, dgl_ref, dwo_ref, gfg_ref, gag_ref, glg_ref, loss_ref, acc):
        i = pl.program_id(0)

        @pl.when(i == 0)
        def _():
            acc[...] = jnp.zeros_like(acc)
            for ref in (gfg_ref, gag_ref, glg_ref, loss_ref):
                ref[...] = jnp.zeros_like(ref)

        oo = jnp.concatenate([o_ref[128 * j:128 * j + 128, :].T for j in range(D // 128)], axis=1)
        gga, hh, ggl = ga_ref[...], hl_ref[...], gl_ref[...]
        ag, lg, fg = ag_ref[...], lg_ref[...], fg_ref[...]
        sga, silua, ra, pah, ya = _gated_norm(oo, gga, ag)
        sgl, silul, rl, plh, yl = _gated_norm(hh, ggl, lg)
        yab, ylb = ya.astype(bf16), yl.astype(bf16)
        y = _dot(yab, wo_ref[0:D, :]) + _dot(ylb, wo_ref[D:2 * D, :])
        x2 = x_ref[...] + y
        r2 = lax.rsqrt(jnp.mean(x2 * x2, axis=-1, keepdims=True) + EPS)
        x2h = x2 * r2
        err = x2h * fg - t_ref[...]
        loss_ref[...] += 0.5 * jnp.sum(jnp.sum(err * err, axis=-1, keepdims=True) * (1.0 / D))
        dout = err * (1.0 / D)
        gfg_ref[...] += jnp.sum(dout * x2h, axis=0, keepdims=True)
        w = dout * fg
        dx2 = r2 * (w - x2h * jnp.mean(w * x2h, axis=-1, keepdims=True))
        dx2_ref[...] = dx2
        dyb = dx2.astype(bf16)
        acc[0:D, :] += _dot_tn(yab, dyb)
        acc[D:2 * D, :] += _dot_tn(ylb, dyb)
        dya = _dot_nt(dyb, wo_ref[0:D, :])
        dyl = _dot_nt(dyb, wo_ref[D:2 * D, :])
        gag, do, dga = _gated_norm_bwd(dya, oo, gga, ag, sga, silua, ra, pah)
        glg, dhl, dgl = _gated_norm_bwd(dyl, hh, ggl, lg, sgl, silul, rl, plh)
        gag_ref[...] += gag
        glg_ref[...] += glg
        dob = do.astype(bf16)
        for j in range(D // 128):
            do_ref[128 * j:128 * j + 128, :] = dob[:, 128 * j:128 * j + 128].T
        dga_ref[...] = dga.astype(bf16)
        dhl_ref[...] = dhl
        dgl_ref[...] = dgl.astype(bf16)

        @pl.when(i == nt - 1)
        def _():
            dwo_ref[...] = acc[...].astype(bf16)

    row = pl.BlockSpec((tm, D), lambda i: (i, 0))
    col = pl.BlockSpec((D, tm), lambda i: (0, i))
    vec = pl.BlockSpec((1, D), lambda i: (0, 0))
    mat = pl.BlockSpec((2 * D, D), lambda i: (0, 0))
    return pl.pallas_call(
        body, name="out_fwd_bwd", grid=(nt,),
        in_specs=[row, row, col, row, row, row] + [vec] * 3 + [mat],
        out_specs=[row, col, row, row, row] + [mat, vec, vec, vec, pl.BlockSpec((1, 128), lambda i: (0, 0))],
        out_shape=[pltpu.HBM((s, D), f32), pltpu.HBM((D, s), bf16),
                   pltpu.HBM((s, D), bf16), pltpu.HBM((s, D), f32),
                   pltpu.HBM((s, D), bf16), pltpu.HBM((2 * D, D), bf16),
                   jax.ShapeDtypeStruct((1, D), f32), jax.ShapeDtypeStruct((1, D), f32),
                   jax.ShapeDtypeStruct((1, D), f32), jax.ShapeDtypeStruct((1, 128), f32)],
        scratch_shapes=[pltpu.VMEM((2 * D, D), f32)],
        compiler_params=_params(("arbitrary",), 56),
    )(*_in_hbm(x, tgt, o, ga, hl, gl), again, lgain, fgain, *_in_hbm(wo))


def _bwd_in(x, dx2, dq, dk, dv, dga, dxl, dgl, ln_gain, wt, tabs, tm):
    s = x.shape[0]

    def body(x_ref, dx2_ref, dq_ref, dk_ref, dv_ref, dga_ref, dxl_ref, dgl_ref, g_ref, wt_ref,
             tab_ref, gx_ref, gln_ref, dzt_ref):
        @pl.when(pl.program_id(0) == 0)
        def _():
            gln_ref[...] = jnp.zeros_like(gln_ref)

        c, sa, sb = (t.T for t in _tables(tab_ref))
        for j in range(D // 128):
            js = slice(128 * j, 128 * j + 128)
            dzt_ref[js, :] = (_unrope_t(dq_ref[js, :], c, sa, sb) * (HD ** -0.5)).astype(bf16)
        for j in range(KVW // 128):
            js = slice(128 * j, 128 * j + 128)
            dzt_ref[D + 128 * j:D + 128 * j + 128, :] = _unrope_t(dk_ref[js, :], c, sa, sb).astype(bf16)
        dzt_ref[D + KVW:D + 2 * KVW, :] = dv_ref[...].astype(bf16)
        first = D + 2 * KVW
        dh = _dot_tn(dzt_ref[0:512, :], wt_ref[0:512, :])
        for ci in range(1, first // 512):
            dh = dh + _dot_tn(dzt_ref[512 * ci:512 * ci + 512, :], wt_ref[512 * ci:512 * ci + 512, :])
        for sec, ref in enumerate((dga_ref, dxl_ref, dgl_ref)):
            for j in range(D // 512):
                rows = slice(first + D * sec + 512 * j, first + D * sec + 512 * j + 512)
                dh = dh + _dot(ref[:, 512 * j:512 * j + 512], wt_ref[rows, :])
            for j in range(D // 128):
                dzt_ref[first + D * sec + 128 * j:first + D * sec + 128 * j + 128, :] = ref[:, 128 * j:128 * j + 128].T
        xx = x_ref[...]
        rstd = lax.rsqrt(jnp.mean(xx * xx, axis=-1, keepdims=True) + EPS)
        xh = xx * rstd
        gln_ref[...] += jnp.sum(dh * xh, axis=0, keepdims=True)
        w = dh * g_ref[...]
        gx_ref[...] = dx2_ref[...] + rstd * (w - xh * jnp.mean(w * xh, axis=-1, keepdims=True))

    row = lambda w: pl.BlockSpec((tm, w), lambda i: (i, 0))
    col = lambda w: pl.BlockSpec((w, tm), lambda i: (0, i))
    full = lambda a: pl.BlockSpec(a.shape, lambda i: (0, 0))
    return pl.pallas_call(
        body, name="bwd_in", grid=(s // tm,),
        in_specs=[row(D), row(D), col(D), col(KVW), col(KVW), row(D), row(D), row(D), full(ln_gain), full(wt),
                  row(384)],
        out_specs=[row(D), pl.BlockSpec((1, D), lambda i: (0, 0)), col(NIN)],
        out_shape=[pltpu.HBM((s, D), f32), jax.ShapeDtypeStruct((1, D), f32),
                   pltpu.HBM((NIN, s), bf16)],
        compiler_params=_params(("arbitrary",), 56),
    )(*_in_hbm(x, dx2, dq, dk, dv, dga, dxl, dgl), ln_gain, *_in_hbm(wt), tabs)


WT_TERMS = 4


def _dwt_scatter(dzt, h, small, tm):
    s = h.shape[0]
    nk = s // tm
    srows = small.shape[0] // NDEV
    last = NDEV - 1

    def body(order_ref, dz_ref, h_ref, sm_ref, lwt_ref, lsm_ref, acc, stage, given, relayed, send_sems, recv_sems,
             local_sem, sm_send, sm_recv, sm_local):
        j, k = pl.program_id(0), pl.program_id(1)
        x, y, c = _place()
        sibling = (x, y, 1 - c)
        near = (x ^ (1 - c), y ^ c)
        far = (x ^ c, y ^ (1 - c))
        sm_start, sm_finish = _scatter_ops([sm_ref], [lsm_ref], sm_send, sm_recv, sm_local)

        def send(step):
            if step == last - 1:
                dst, to = lwt_ref.at[1], sibling
            elif step % 2 == 0:
                dst, to = given.at[step // 2], sibling
            elif step == 1:
                dst, to = relayed, (*near, c)
            else:
                dst, to = lwt_ref.at[1 + step // 2], (*(near if step == 3 else far), c)
            return pltpu.make_async_remote_copy(
                src_ref=stage.at[step % 2], dst_ref=dst, send_sem=send_sems.at[step], recv_sem=recv_sems.at[step],
                device_id=to, device_id_type=MESH)

        def keep():
            return pltpu.make_async_copy(stage.at[last % 2], lwt_ref.at[0], local_sem)

        @pl.when((j == 0) & (k == 0))
        def _():
            sm_start()

        @pl.when(k == 0)
        def _():
            acc[...] = jnp.zeros_like(acc)

        acc[...] += _dot(dz_ref[...], h_ref[...])

        for step in range(NDEV):
            @pl.when((k == nk - 1) & (j == step))
            def _(step=step):
                if step >= 2:
                    send(step - 2).wait_send()
                if step % 2 == 1 and step < last:
                    send(step - 1).wait_recv()
                    total = acc[...] + given[step // 2].astype(f32)
                    if step == 5:
                        send(1).wait_recv()
                        total = total + relayed[...].astype(f32)
                    stage[step % 2] = total.astype(bf16)
                else:
                    stage[step % 2] = acc[...].astype(bf16)
                if step < last:
                    send(step).start()
                else:
                    keep().start()
                    send(last - 1).wait_send()
                    for peer_step in (3, 5, last - 1):
                        send(peer_step).wait_recv()
                    keep().wait()
                    sm_finish()

    x, y, c = _place()
    dest = lambda chip, cc: 4 * chip[0] + 2 * chip[1] + cc
    near, far, diag = (x ^ (1 - c), y ^ c), (x ^ c, y ^ (1 - c)), (1 - x, 1 - y)
    order = jnp.stack([dest(diag, 1 - c), dest(diag, c), dest(far, 1 - c), dest(near, c),
                       dest(near, 1 - c), dest(far, c), dest((x, y), 1 - c), dest((x, y), c)])
    return pl.pallas_call(
        body, name="dwt_scatter",
        grid_spec=pltpu.PrefetchScalarGridSpec(
            num_scalar_prefetch=1, grid=(NDEV, nk),
            in_specs=[pl.BlockSpec((WT_ROWS, tm), lambda j, k, order: (order[j], k)),
                      pl.BlockSpec((tm, D), lambda j, k, order: (k, 0)), HBM],
            out_specs=[HBM, HBM],
            scratch_shapes=[pltpu.VMEM((WT_ROWS, D), f32), pltpu.VMEM((2, WT_ROWS, D), bf16),
                            pltpu.VMEM((3, WT_ROWS, D), bf16), pltpu.VMEM((WT_ROWS, D), bf16),
                            pltpu.SemaphoreType.DMA((last,)), pltpu.SemaphoreType.DMA((last,)),
                            pltpu.SemaphoreType.DMA(())] + _comm_sems(1)),
        out_shape=[pltpu.HBM((WT_TERMS, WT_ROWS, D), bf16), pltpu.HBM((NDEV, srows, D), f32)],
        compiler_params=_params(("arbitrary", "arbitrary"), 32),
    )(order, *_in_hbm(dzt, h, small))


def _diag_blocks(bd):
    eye = jnp.eye(4, dtype=bd.dtype)
    return jnp.einsum('gjckd,jk->gjcd', bd.reshape(NGRP, 4, HD, 4, HD), eye).reshape(NQ, HD, HD)


def _sequence_step(x, tgt, wt, wo_shard, conv_w, p):
    s = x.shape[0]
    tm = min(256, s)
    tabs = _rope_tables(s)
    wr, wi = _block_diag(p["w_rgate"]), _block_diag(p["w_igate"])
    sinks = p["sinks"].reshape(NQ)
    h, qt, kt, vt, ga, xl, gl, u, hl, r, ig, a, im, wo = _fwd_fused(
        x, p["ln_gain"], wt, tabs, wo_shard, conv_w, p["conv_b"], wr, wi, p["b_rgate"], p["b_igate"],
        p["lru_lambda"], tm)
    ot = _attn_fwd_t(qt, kt, vt, sinks)
    dx2, dot, dga, dhl, dgl, dwo, g_fg, g_ag, g_lg, loss = _out_fwd_bwd(
        x, tgt, ot, ga, hl, gl, p["attn_out_gain"], p["lru_out_gain"], p["final_gain"], wo, tm)
    dqt, dkt, dvt, dsink, land_wo = _attn_bwd_t(qt, kt, vt, dot, sinks, dwo)
    dxl, dwr, dwi, dbr, dbi, dlam, dcb, dcw = _lru_bwd(u, hl, dhl, xl, r, ig, a, im, conv_w, wr, wi, p["lru_lambda"], tm)
    gx, g_ln, dzt = _bwd_in(x, dx2, dqt, dkt, dvt, dga, dxl, dgl, p["ln_gain"], wt, tabs, tm)
    small = dict(ln_gain=g_ln, sinks=dsink.reshape(NQ, BLK).sum(axis=1)[None], conv_w=dcw, conv_b=dcb,
                 w_rgate=_diag_blocks(dwr), b_rgate=dbr, w_igate=_diag_blocks(dwi), b_igate=dbi, lru_lambda=dlam,
                 attn_out_gain=g_ag, lru_out_gain=g_lg, final_gain=g_fg)
    land_wt, land_sm = _dwt_scatter(dzt, h, _pack_small(small, loss), min(1024, s))
    return gx, land_wt, land_wo, land_sm


def _all_gather(srcs, out_dtypes, name, relay):
    n = len(srcs)
    cast = [a.dtype != dt for a, dt in zip(srcs, out_dtypes)]

    def body(*refs):
        src_refs, out_refs = refs[:n], refs[n:2 * n]
        stage_refs = list(refs[2 * n:2 * n + sum(cast)])
        mine_refs = []
        for a in range(n):
            if cast[a]:
                st = stage_refs.pop(0)
                st[...] = src_refs[a][...].astype(out_dtypes[a])
                mine_refs.append(st)
            else:
                mine_refs.append(src_refs[a])
        start, finish = (_relay_gather_ops if relay else _gather_ops)(mine_refs, out_refs, *refs[-3:])
        start()
        finish()

    vmem = pl.BlockSpec(memory_space=pltpu.VMEM)
    return pl.pallas_call(
        body, name=name,
        in_specs=[vmem] * n, out_specs=[HBM] * n,
        out_shape=[pltpu.HBM((NDEV * a.shape[0], a.shape[1]), dt) for a, dt in zip(srcs, out_dtypes)],
        scratch_shapes=[pltpu.VMEM(a.shape, dt) for a, dt, cst in zip(srcs, out_dtypes, cast) if cst] + _comm_sems(n),
        compiler_params=pltpu.CompilerParams(vmem_limit_bytes=32 * MIB),
    )(*srcs)


def _sum_slots(land, tr, name):
    terms, rows, cols = land.shape

    def body(l_ref, o_ref):
        acc = l_ref[0].astype(f32)
        for d in range(1, terms):
            acc = acc + l_ref[d].astype(f32)
        o_ref[...] = acc

    return pl.pallas_call(
        body, name=name, grid=(rows // tr,),
        in_specs=[pl.BlockSpec((terms, tr, cols), lambda i: (0, i, 0))],
        out_specs=pl.BlockSpec((tr, cols), lambda i: (i, 0)),
        out_shape=jax.ShapeDtypeStruct((rows, cols), f32),
        compiler_params=_params(("arbitrary",), 32),
    )(*_in_hbm(land))


def _adam_math(w, g, m, v):
    m2 = ADAM_B1 * m + (1.0 - ADAM_B1) * g
    v2 = ADAM_B2 * v + (1.0 - ADAM_B2) * (g * g)
    m_hat = m2 / (1.0 - ADAM_B1 ** ADAM_STEP)
    v_hat = v2 / (1.0 - ADAM_B2 ** ADAM_STEP)
    delta = -ADAM_LR * (m_hat / (jnp.sqrt(v_hat) + ADAM_EPS) + ADAM_WD * w)
    return delta, m2, v2


def _reduce_adamw(land, w, m, v, tr, name):
    terms, rows, cols = land.shape

    def body(l_ref, w_ref, m_ref, v_ref, g_ref, d_ref, m2_ref, v2_ref):
        g = l_ref[0].astype(f32)
        for t in range(1, terms):
            g = g + l_ref[t].astype(f32)
        g_ref[...] = g
        d_ref[...], m2_ref[...], v2_ref[...] = _adam_math(w_ref[...], g, m_ref[...], v_ref[...])

    blk = pl.BlockSpec((tr, cols), lambda i: (i, 0))
    return pl.pallas_call(
        body, name=name, grid=(rows // tr,),
        in_specs=[pl.BlockSpec((terms, tr, cols), lambda i: (0, i, 0))] + [blk] * 3, out_specs=[blk] * 4,
        out_shape=[jax.ShapeDtypeStruct((rows, cols), f32)] * 4,
        compiler_params=_params(("arbitrary",), 32),
    )(*_in_hbm(land), w, m, v)


VEC_NAMES = ("ln_gain", "conv_b", "b_rgate", "b_igate", "lru_lambda", "attn_out_gain", "lru_out_gain", "final_gain")
ROW_RGATE, ROW_IGATE, ROW_VEC, ROW_SINKS = 0, 64, 128, 136
LOSS_LANE = NQ


def _adamw_small(g_rep, g_conv, w, m, v):
    names = list(VEC_NAMES) + ["sinks", "conv_w", "w_rgate", "w_igate"]
    ins = [g_rep, g_conv] + [d[k] for k in names for d in (w, m, v)]

    def body(*refs):
        g_ref, gc_ref = refs[0], refs[1]
        in_refs = refs[2:2 + 3 * len(names)]
        out_refs = refs[2 + 3 * len(names):]

        def update(j, g, at=None):
            w_ref, m_ref, v_ref = in_refs[3 * j:3 * j + 3]
            outs = out_refs[4 * j:4 * j + 4]
            pick = (lambda r: r[...]) if at is None else (lambda r: r[at])
            res = (g,) + _adam_math(pick(w_ref), g, pick(m_ref), pick(v_ref))
            for o_ref, val in zip(outs, res):
                if at is None:
                    o_ref[...] = val
                else:
                    o_ref[at] = val

        for j in range(len(VEC_NAMES)):
            update(j, g_ref[ROW_VEC + j:ROW_VEC + j + 1, :])
        update(len(VEC_NAMES), g_ref[ROW_SINKS:ROW_SINKS + 1, 0:NQ])
        update(len(VEC_NAMES) + 1, gc_ref[...], at=0)
        for gi, row0 in ((len(VEC_NAMES) + 2, ROW_RGATE), (len(VEC_NAMES) + 3, ROW_IGATE)):
            for nb in range(NQ):
                update(gi, g_ref[row0:row0 + HD, HD * nb:HD * nb + HD], at=(0, nb))

    vmem = pl.BlockSpec(memory_space=pltpu.VMEM)
    out_shape = [jax.ShapeDtypeStruct(w[k].shape, f32) for k in names for _ in range(4)]
    outs = pl.pallas_call(
        body, name="adamw_small",
        in_specs=[vmem] * len(ins), out_specs=[vmem] * len(out_shape), out_shape=out_shape,
        compiler_params=pltpu.CompilerParams(vmem_limit_bytes=32 * MIB),
    )(*ins)
    return {k: tuple(outs[4 * j:4 * j + 4]) for j, k in enumerate(names)}


def _pack_small(small, loss):
    gate = lambda g: g.transpose(1, 0, 2).reshape(HD, NQ * HD)
    row_s = jnp.concatenate([small["sinks"], loss[:, LOSS_LANE:128], jnp.zeros((1, D - 128), f32)], axis=1)
    rep = jnp.concatenate([gate(small["w_rgate"]), gate(small["w_igate"])] + [small[k] for k in VEC_NAMES]
                          + [row_s, jnp.zeros((SMALL_ROWS - ROW_SINKS - 1, D), f32)], axis=0)
    conv = small["conv_w"].reshape(CONVW, NDEV, 128).transpose(1, 0, 2)
    conv = jnp.pad(conv, ((0, 0), (0, 8 - CONVW), (0, D - 128)))
    return jnp.concatenate([rep.reshape(NDEV, SMALL_PER, D), conv], axis=1).reshape(NDEV * (SMALL_PER + 8), D)


def kernel(x, ln_gain, w_in, sinks, conv_w, conv_b, w_rgate, b_rgate, w_igate, b_igate, lru_lambda, attn_out_gain, lru_out_gain, w_out, final_gain, loss_target, m_ln_gain, m_w_in, m_sinks, m_conv_w, m_conv_b, m_w_rgate, m_b_rgate, m_w_igate, m_b_igate, m_lru_lambda, m_attn_out_gain, m_lru_out_gain, m_w_out, m_final_gain, v_ln_gain, v_w_in, v_sinks, v_conv_w, v_conv_b, v_w_rgate, v_b_rgate, v_w_igate, v_b_igate, v_lru_lambda, v_attn_out_gain, v_lru_out_gain, v_w_out, v_final_gain):
    w = dict(ln_gain=ln_gain, sinks=sinks, conv_w=conv_w, conv_b=conv_b, w_rgate=w_rgate, b_rgate=b_rgate,
             w_igate=w_igate, b_igate=b_igate, lru_lambda=lru_lambda, attn_out_gain=attn_out_gain,
             lru_out_gain=lru_out_gain, final_gain=final_gain.reshape(1, D))
    m = dict(ln_gain=m_ln_gain, sinks=m_sinks, conv_w=m_conv_w, conv_b=m_conv_b, w_rgate=m_w_rgate,
             b_rgate=m_b_rgate, w_igate=m_w_igate, b_igate=m_b_igate, lru_lambda=m_lru_lambda,
             attn_out_gain=m_attn_out_gain, lru_out_gain=m_lru_out_gain, final_gain=m_final_gain.reshape(1, D))
    v = dict(ln_gain=v_ln_gain, sinks=v_sinks, conv_w=v_conv_w, conv_b=v_conv_b, w_rgate=v_w_rgate,
             b_rgate=v_b_rgate, w_igate=v_w_igate, b_igate=v_b_igate, lru_lambda=v_lru_lambda,
             attn_out_gain=v_attn_out_gain, lru_out_gain=v_lru_out_gain, final_gain=v_final_gain.reshape(1, D))

    conv_blk = jnp.pad(conv_w[0], ((0, 8 - CONVW), (0, 0)))
    wt, cw_all = _all_gather([w_in[0].T, conv_blk], [bf16, f32], "gather_weights", relay=True)
    conv_full = cw_all.reshape(NDEV, 8, 128)[:, 0:CONVW].transpose(1, 0, 2).reshape(CONVW, LW)

    p = {k: (w[k][0] if k in ("w_rgate", "w_igate") else w[k]) for k in w if k != "conv_w"}
    gx, land_wt, land_wo, land_sm = _sequence_step(x[0], loss_target[0], wt, w_out[0], conv_full, p)

    g_sm = _sum_slots(land_sm, SMALL_PER + 8, "sum_small")
    (g_rep,) = _all_gather([g_sm[0:SMALL_PER]], [f32], "gather_small", relay=False)
    g_conv = g_sm[SMALL_PER:SMALL_PER + CONVW, 0:128]

    wins = _reduce_adamw(land_wt, w_in[0].T, m_w_in[0].T, v_w_in[0].T, 192, "adamw_w_in")
    g_win, d_win, m_win, v_win = (t.T for t in wins)
    g_wo, d_wo, m_wo, v_wo = _reduce_adamw(land_wo, w_out[0], m_w_out[0], v_w_out[0], 256, "adamw_w_out")
    res = _adamw_small(g_rep, g_conv, w, m, v)
    res["w_in"] = tuple(t[None] for t in (g_win, d_win, m_win, v_win))
    res["w_out"] = tuple(t[None] for t in (g_wo, d_wo, m_wo, v_wo))
    res["final_gain"] = tuple(t.reshape(D) for t in res["final_gain"])

    order = ("ln_gain", "w_in", "sinks", "conv_w", "conv_b", "w_rgate", "b_rgate", "w_igate", "b_igate",
             "lru_lambda", "attn_out_gain", "lru_out_gain", "w_out", "final_gain")
    total_loss = g_rep[ROW_SINKS, LOSS_LANE]
    return (total_loss, gx[None]) + tuple(res[k][i] for i in range(4) for k in order)
```

```python
import jax
import jax.numpy as jnp
from jax import lax
from jax.experimental import pallas as pl
from jax.experimental.pallas import tpu as pltpu

f32 = jnp.float32
bf16 = jnp.bfloat16

D = 1024
HD = 64
NQ = 16
NKV = 4
GROUP = NQ // NKV
KVW = NKV * HD
BLK = 128
ROT = 16
THETA = 500000.0
NEG = -1e30
LW = 1024
NGRP = 4
CONVW = 4
LRU_C = 8.0
NIN = 4608
EPS = 1e-6
NDEV = 8
WT_ROWS = NIN // NDEV
WO_ROWS = 2 * D // NDEV
SMALL_ROWS = 192
SMALL_PER = SMALL_ROWS // NDEV

ADAM_LR = 0.001
ADAM_B1 = 0.9
ADAM_B2 = 0.999
ADAM_EPS = 1e-08
ADAM_WD = 0.01
ADAM_STEP = 10

NT = (((1,), (1,)), ((), ()))
TN = (((0,), (0,)), ((), ()))
MESH = pl.DeviceIdType.MESH
MIB = 1024 * 1024


def _dot(a, b):
    return jnp.dot(a, b, preferred_element_type=f32)


def _dot_nt(a, b):
    return lax.dot_general(a, b, NT, preferred_element_type=f32)


def _dot_tn(a, b):
    return lax.dot_general(a, b, TN, preferred_element_type=f32)


def _params(sem, vmem_mib):
    return pltpu.CompilerParams(dimension_semantics=sem, vmem_limit_bytes=vmem_mib * MIB)


def _sigmoid(x):
    return 0.5 * jnp.tanh(0.5 * x) + 0.5


def _softplus(x):
    return jnp.maximum(x, 0.0) + jnp.log(1.0 + jnp.exp(-jnp.abs(x)))


def _rope_tables(s):
    pos = jnp.arange(s, dtype=f32)
    inv_freq = THETA ** (-jnp.arange(0, ROT, 2, dtype=f32) / ROT)
    ang = pos[:, None] * inv_freq[None, :]
    cs = jnp.concatenate([jnp.cos(ang) - 1.0, jnp.sin(ang)], axis=1)
    d = jnp.arange(128) % HD
    j = jnp.arange(ROT)[:, None]
    pick_c = ((d < ROT) & (j == d % (ROT // 2))).astype(f32)
    pick_sa = ((d >= ROT // 2) & (d < ROT) & (j == d)).astype(f32)
    pick_sb = -((d < ROT // 2) & (j == d + ROT // 2)).astype(f32)
    picks = jnp.concatenate([pick_c, pick_sa, pick_sb], axis=1)
    ones = jnp.concatenate([jnp.ones((1, 128), f32), jnp.zeros((1, 256), f32)], axis=1)
    return jnp.dot(cs, picks, precision=lax.Precision.HIGHEST) + ones


def _tables(tab_ref):
    return tab_ref[:, 0:128], tab_ref[:, 128:256], tab_ref[:, 256:384]


def _rope(t, c, sa, sb):
    return t * c + pltpu.roll(t, 8, 1) * sa + pltpu.roll(t, 120, 1) * sb


def _unrope_t(dr, c, sa, sb):
    return dr * c + pltpu.roll(dr * sa, 120, 0) + pltpu.roll(dr * sb, 8, 0)


def _place():
    return lax.axis_index("x"), lax.axis_index("y"), lax.axis_index("c")


def _gather_ops(mine_refs, out_refs, send_sems, recv_sems, local_sems):
    n = len(mine_refs)
    x, y, c = _place()
    me, sibling = (x, y, c), (x, y, 1 - c)
    chips = [(1 - x, y), (x, 1 - y), (1 - x, 1 - y)]

    def rows(a, dev):
        m = mine_refs[a].shape[0]
        return out_refs[a].at[pl.ds((4 * dev[0] + 2 * dev[1] + dev[2]) * m, m), :]

    def copy(a, k, block, to, own=False):
        return pltpu.make_async_remote_copy(
            src_ref=mine_refs[a] if own else rows(a, block), dst_ref=rows(a, block),
            send_sem=send_sems.at[a, k], recv_sem=recv_sems.at[a, k], device_id=to, device_id_type=MESH)

    def local(a):
        return pltpu.make_async_copy(mine_refs[a], rows(a, me), local_sems.at[a])

    def first(a):
        return [copy(a, 0, me, sibling, own=True)] + [copy(a, 1 + j, me, (*chip, c), own=True)
                                                      for j, chip in enumerate(chips)]

    def start():
        for a in range(n):
            local(a).start()
            for cp in first(a):
                cp.start()

    def finish():
        for j, chip in enumerate(chips):
            for a in range(n):
                copy(a, 1 + j, (*chip, c), me).wait_recv()
                copy(a, 4 + j, (*chip, c), sibling).start()
        for a in range(n):
            copy(a, 0, sibling, me).wait_recv()
            for j, chip in enumerate(chips):
                copy(a, 4 + j, (*chip, 1 - c), me).wait_recv()
        for a in range(n):
            for cp in first(a) + [copy(a, 4 + j, (*chip, c), sibling) for j, chip in enumerate(chips)]:
                cp.wait_send()
            local(a).wait()

    return start, finish


def _relay_gather_ops(mine_refs, out_refs, send_sems, recv_sems, local_sems):
    n = len(mine_refs)
    x, y, c = _place()
    me, sibling = (x, y, c), (x, y, 1 - c)
    near = (x ^ (1 - c), y ^ c)
    far = (x ^ c, y ^ (1 - c))
    diag = (1 - x, 1 - y)

    def rows(a, dev):
        m = mine_refs[a].shape[0]
        return out_refs[a].at[pl.ds((4 * dev[0] + 2 * dev[1] + dev[2]) * m, m), :]

    def copy(a, k, block, to, own=False):
        return pltpu.make_async_remote_copy(
            src_ref=mine_refs[a] if own else rows(a, block), dst_ref=rows(a, block),
            send_sem=send_sems.at[a, k], recv_sem=recv_sems.at[a, k], device_id=to, device_id_type=MESH)

    def local(a):
        return pltpu.make_async_copy(mine_refs[a], rows(a, me), local_sems.at[a])

    def sends(a):
        return [copy(a, 0, me, sibling, own=True), copy(a, 1, me, (*near, c), own=True),
                copy(a, 2, me, (*far, c), own=True), copy(a, 3, (*near, c), (*far, c)),
                copy(a, 4, (*near, c), sibling), copy(a, 5, (*far, c), sibling), copy(a, 6, (*diag, c), sibling)]

    def arrivals(a):
        return [copy(a, 0, sibling, me), copy(a, 1, (*near, c), me), copy(a, 2, (*far, c), me),
                copy(a, 3, (*diag, c), me), copy(a, 4, (*far, 1 - c), me), copy(a, 5, (*near, 1 - c), me),
                copy(a, 6, (*diag, 1 - c), me)]

    def start():
        for a in range(n):
            local(a).start()
            for cp in sends(a)[0:3]:
                cp.start()

    def finish():
        for first, then in ((1, (3, 4)), (2, (5,)), (3, (6,))):
            for a in range(n):
                arrivals(a)[first].wait_recv()
                for k in then:
                    sends(a)[k].start()
        for a in range(n):
            for k in (0, 4, 5, 6):
                arrivals(a)[k].wait_recv()
        for a in range(n):
            for cp in sends(a):
                cp.wait_send()
            local(a).wait()

    return start, finish


def _scatter_ops(src_refs, land_refs, send_sems, recv_sems, local_sems):
    n = len(src_refs)
    x, y, c = _place()
    my = 4 * x + 2 * y + c

    def peer(k):
        return x ^ (k >> 2), y ^ ((k >> 1) & 1), c ^ (k & 1)

    def piece(a, dev):
        m = src_refs[a].shape[0] // NDEV
        return src_refs[a].at[pl.ds(dev * m, m), :]

    def local(a):
        return pltpu.make_async_copy(piece(a, my), land_refs[a].at[my], local_sems.at[a])

    def send(a, k):
        px, py, pc = peer(k)
        return pltpu.make_async_remote_copy(
            src_ref=piece(a, 4 * px + 2 * py + pc), dst_ref=land_refs[a].at[my],
            send_sem=send_sems.at[a, k - 1], recv_sem=recv_sems.at[a, k - 1],
            device_id=(px, py, pc), device_id_type=MESH)

    def arrival(a, k):
        px, py, pc = peer(k)
        return pltpu.make_async_remote_copy(
            src_ref=piece(a, my), dst_ref=land_refs[a].at[4 * px + 2 * py + pc],
            send_sem=send_sems.at[a, k - 1], recv_sem=recv_sems.at[a, k - 1],
            device_id=(px, py, pc), device_id_type=MESH)

    def start():
        for a in range(n):
            local(a).start()
        for k in range(1, NDEV):
            for a in range(n):
                send(a, k).start()

    def finish():
        for k in range(1, NDEV):
            for a in range(n):
                send(a, k).wait_send()
        for k in range(1, NDEV):
            for a in range(n):
                arrival(a, k).wait_recv()
        for a in range(n):
            local(a).wait()

    return start, finish


def _in_hbm(*arrays):
    return tuple(pltpu.with_memory_space_constraint(a, pltpu.HBM) for a in arrays)


def _comm_sems(n):
    return [pltpu.SemaphoreType.DMA((n, 7)), pltpu.SemaphoreType.DMA((n, 7)), pltpu.SemaphoreType.DMA((n,))]


HBM = pl.BlockSpec(memory_space=pltpu.HBM)


def _sink_rows(sinks):
    return jnp.repeat(sinks.reshape(NKV, GROUP), BLK, axis=1)


def _band_softmax(s2_ref, ls, prev_offset, sink_row):
    jj = lax.broadcasted_iota(jnp.int32, (BLK, BLK), 0)
    ii = lax.broadcasted_iota(jnp.int32, (BLK, BLK), 1)
    from_prev = jj > ii
    sc = jnp.where(from_prev, s2_ref[0:BLK, ls] + prev_offset, s2_ref[BLK:2 * BLK, ls])
    m = jnp.maximum(jnp.max(sc, axis=0, keepdims=True), sink_row)
    p = jnp.exp(sc - m)
    es = jnp.exp(sink_row - m)
    inv = 1.0 / (jnp.sum(p, axis=0, keepdims=True) + es)
    return from_prev, p * inv, es * inv


def _put_split(dst_ref, ls, t, from_prev):
    t = t.astype(bf16)
    zero = jnp.zeros_like(t)
    dst_ref[0:BLK, ls] = jnp.where(from_prev, t, zero)
    dst_ref[BLK:2 * BLK, ls] = jnp.where(from_prev, zero, t)


def _heads_side_by_side(ref, h):
    return jnp.concatenate([ref[HD * (GROUP * h + g):HD * (GROUP * h + g) + HD, :] for g in range(GROUP)], axis=1)


def _kv_specs_t():
    prev = pl.BlockSpec((KVW, BLK), lambda n: (0, jnp.maximum(n - 1, 0)))
    cur = pl.BlockSpec((KVW, BLK), lambda n: (0, n))
    return [prev, cur, prev, cur]


def _attn_fwd_t(qt, kt, vt, sinks):
    s = qt.shape[1]

    def body(sink_ref, q_ref, kp_ref, kc_ref, vp_ref, vc_ref, o_ref, s2_scr, pn2_scr):
        n = pl.program_id(0)
        off = jnp.where(n > 0, 0.0, NEG)

        def scores(h):
            hs = slice(HD * h, HD * h + HD)
            kh = jnp.concatenate([kp_ref[hs, :], kc_ref[hs, :]], axis=1)
            s2_scr[h % 2] = _dot_tn(kh, _heads_side_by_side(q_ref, h))

        def probs(h):
            for g in range(GROUP):
                ls = slice(BLK * g, BLK * g + BLK)
                from_prev, pn, _ = _band_softmax(s2_scr.at[h % 2], ls, off, sink_ref[h:h + 1, ls])
                _put_split(pn2_scr.at[h % 2], ls, pn, from_prev)

        def outputs(h):
            hs = slice(HD * h, HD * h + HD)
            vh = jnp.concatenate([vp_ref[hs, :], vc_ref[hs, :]], axis=1)
            og = _dot(vh, pn2_scr[h % 2])
            for g in range(GROUP):
                a = GROUP * h + g
                o_ref[HD * a:HD * a + HD, :] = og[:, BLK * g:BLK * g + BLK]

        scores(0)
        for h in range(NKV):
            if h + 1 < NKV:
                scores(h + 1)
            probs(h)
            outputs(h)

    return pl.pallas_call(
        body, name="attn_fwd", grid=(s // BLK,),
        in_specs=[pl.BlockSpec((NKV, GROUP * BLK), lambda n: (0, 0)), pl.BlockSpec((D, BLK), lambda n: (0, n))]
        + _kv_specs_t(),
        out_specs=pl.BlockSpec((D, BLK), lambda n: (0, n)),
        out_shape=pltpu.HBM((D, s), f32),
        scratch_shapes=[pltpu.VMEM((2, 2 * BLK, GROUP * BLK), f32), pltpu.VMEM((2, 2 * BLK, GROUP * BLK), bf16)],
        compiler_params=_params(("arbitrary",), 32),
    )(_sink_rows(sinks), *_in_hbm(qt, kt, kt, vt, vt))


def _attn_bwd_t(qt, kt, vt, dot, sinks, dwo):
    s = qt.shape[1]
    nb = s // BLK

    def body(sink_ref, q_ref, do_ref, kp_ref, kc_ref, vp_ref, vc_ref, dwo_ref, dq_ref, dk_ref, dv_ref, ds_ref,
             land_ref, dk_hold, dv_hold, s2_scr, dp2_scr, pn2_scr, ds2_scr, send_sems, recv_sems, local_sems):
        n = pl.program_id(0)
        start, finish = _scatter_ops([dwo_ref], [land_ref], send_sems, recv_sems, local_sems)

        @pl.when(n == 0)
        def _():
            start()
            dk_hold[...] = jnp.zeros_like(dk_hold)
            dv_hold[...] = jnp.zeros_like(dv_hold)
            ds_ref[...] = jnp.zeros_like(ds_ref)

        @pl.when(n < nb)
        def _():
            off = jnp.where(n > 0, 0.0, NEG)

            def scores(h):
                hs = slice(HD * h, HD * h + HD)
                kh = jnp.concatenate([kp_ref[hs, :], kc_ref[hs, :]], axis=1)
                vh = jnp.concatenate([vp_ref[hs, :], vc_ref[hs, :]], axis=1)
                s2_scr[h % 2] = _dot_tn(kh, _heads_side_by_side(q_ref, h))
                dp2_scr[h % 2] = _dot_tn(vh, _heads_side_by_side(do_ref, h))

            def softmax_bwd(h):
                for g in range(GROUP):
                    ls = slice(BLK * g, BLK * g + BLK)
                    from_prev, pn, ps = _band_softmax(s2_scr.at[h % 2], ls, off, sink_ref[h:h + 1, ls])
                    dp = jnp.where(from_prev, dp2_scr[h % 2, 0:BLK, ls], dp2_scr[h % 2, BLK:2 * BLK, ls])
                    dsum = jnp.sum(pn * dp, axis=0, keepdims=True)
                    ds_ref[h:h + 1, ls] += -ps * dsum
                    _put_split(pn2_scr.at[h % 2], ls, pn, from_prev)
                    _put_split(ds2_scr.at[h % 2], ls, pn * (dp - dsum), from_prev)

            def grads(h):
                hs = slice(HD * h, HD * h + HD)
                kh = jnp.concatenate([kp_ref[hs, :], kc_ref[hs, :]], axis=1)
                dqg = _dot(kh, ds2_scr[h % 2])
                for g in range(GROUP):
                    a = GROUP * h + g
                    dq_ref[HD * a:HD * a + HD, :] = dqg[:, BLK * g:BLK * g + BLK]
                dkh = _dot_nt(_heads_side_by_side(q_ref, h), ds2_scr[h % 2])
                dvh = _dot_nt(_heads_side_by_side(do_ref, h), pn2_scr[h % 2])
                dk_ref[hs, :] = dk_hold[hs, :] + dkh[:, 0:BLK]
                dv_ref[hs, :] = dv_hold[hs, :] + dvh[:, 0:BLK]
                dk_hold[hs, :] = dkh[:, BLK:2 * BLK]
                dv_hold[hs, :] = dvh[:, BLK:2 * BLK]

            scores(0)
            for h in range(NKV):
                if h + 1 < NKV:
                    scores(h + 1)
                softmax_bwd(h)
                grads(h)

        @pl.when(n == nb)
        def _():
            dk_ref[...] = dk_hold[...]
            dv_ref[...] = dv_hold[...]
            finish()

    blk = pl.BlockSpec((D, BLK), lambda n: (0, jnp.minimum(n, nb - 1)))
    late = pl.BlockSpec((KVW, BLK), lambda n: (0, jnp.maximum(n - 1, 0)))
    whole = pl.BlockSpec((NKV, GROUP * BLK), lambda n: (0, 0))
    kv = [pl.BlockSpec((KVW, BLK), lambda n: (0, jnp.clip(n - 1, 0, nb - 1))),
          pl.BlockSpec((KVW, BLK), lambda n: (0, jnp.minimum(n, nb - 1)))]
    return pl.pallas_call(
        body, name="attn_bwd", grid=(nb + 1,),
        in_specs=[whole, blk, blk] + kv + kv + [HBM],
        out_specs=[blk, late, late, whole, HBM],
        out_shape=[pltpu.HBM((D, s), f32), pltpu.HBM((KVW, s), f32), pltpu.HBM((KVW, s), f32),
                   jax.ShapeDtypeStruct((NKV, GROUP * BLK), f32), pltpu.HBM((NDEV, WO_ROWS, D), bf16)],
        scratch_shapes=[pltpu.VMEM((KVW, BLK), f32), pltpu.VMEM((KVW, BLK), f32)]
        + [pltpu.VMEM((2, 2 * BLK, GROUP * BLK), f32)] * 2 + [pltpu.VMEM((2, 2 * BLK, GROUP * BLK), bf16)] * 2
        + _comm_sems(1),
        compiler_params=_params(("arbitrary",), 48),
    )(_sink_rows(sinks), *_in_hbm(qt, dot, kt, kt, vt, vt, dwo))


def _block_diag(w):
    w4 = w.reshape(NGRP, 4, HD, HD)
    eye = jnp.eye(4, dtype=w.dtype)
    return jnp.einsum('gjcd,jk->gjckd', w4, eye).reshape(NGRP, 256, 256).astype(bf16)


def _gate_terms(pr, pi, br, bi, sp):
    r = _sigmoid(pr + br)
    i = _sigmoid(pi + bi)
    la = -LRU_C * r * sp
    a = jnp.exp(la)
    x2 = 2.0 * la
    y = jnp.where(x2 > -0.02, -x2 * (1.0 + x2 * (0.5 + x2 * (1.0 / 6.0))), 1.0 - a * a)
    inv_mult = lax.rsqrt(jnp.maximum(y, 1e-30))
    return r, i, a, y * inv_mult, inv_mult


def _later(x, before, k):
    if k == 0:
        return x
    row = lax.broadcasted_iota(jnp.int32, before.shape, 0)
    rolled = pltpu.roll(x, k, 0)
    first = jnp.where(row < k, pltpu.roll(before, k, 0), rolled[0:8])
    return jnp.concatenate([first, rolled[8:]], axis=0)


def _earlier(x, after, k):
    if k == 0:
        return x
    n = x.shape[0]
    row = lax.broadcasted_iota(jnp.int32, after.shape, 0)
    rolled = pltpu.roll(x, n - k, 0)
    last = jnp.where(row >= 8 - k, pltpu.roll(after, 8 - k, 0), rolled[n - 8:n])
    return jnp.concatenate([rolled[0:n - 8], last], axis=0)


def _fwd_fused(h, wt, tabs, wo_shard, conv_w, conv_b, wr, wi, br, bi, lam, tm):
    s = h.shape[0]
    nt = s // tm
    nc = 512
    pieces = 8
    rows_per = tm // pieces
    later_chunks = (0, 1, 2, 3, 4, 7, 8)

    def body(h_ref, wt_ref, tab_ref, wo_ref, cw_ref, cb_ref, wr_ref, wi_ref, br_ref,
             bi_ref, lam_ref, q_ref, k_ref, v_ref, ga_ref, xl_ref, gl_ref, u_ref, hl_ref, r_ref, ig_ref, a_ref,
             im_ref, wo_all, wo_stage, halo, ub_scr, pr_scr, pi_scr, b_scr, hcar,
             send_sems, recv_sems, local_sems):
        i = pl.program_id(0)
        start, finish = _gather_ops([wo_stage], [wo_all], send_sems, recv_sems, local_sems)

        @pl.when(i == 0)
        def _():
            wo_stage[...] = wo_ref[...].astype(bf16)
            start()
            halo[...] = jnp.zeros_like(halo)
            hcar[...] = jnp.zeros_like(hcar)

        sp = _softplus(-lam_ref[...])
        br, bi = br_ref[...], bi_ref[...]
        c, sa, sb = _tables(tab_ref)
        piece_rows = lambda p: slice(rows_per * p, rows_per * p + rows_per)

        def project(ci):
            z = _dot_nt(h_ref[...], wt_ref[ci * nc:(ci + 1) * nc, :])
            if ci < 2:
                for j in range(nc // 128):
                    r = _rope(z[:, 128 * j:128 * j + 128], c, sa, sb) * (HD ** -0.5)
                    q_ref[ci * nc + 128 * j:ci * nc + 128 * j + 128, :] = r.astype(bf16).T
            elif ci == 2:
                for j in range(2):
                    js = slice(128 * j, 128 * j + 128)
                    k_ref[js, :] = _rope(z[:, js], c, sa, sb).astype(bf16).T
                    v_ref[js, :] = z[:, KVW + 128 * j:KVW + 128 * j + 128].astype(bf16).T
            else:
                sec, j = divmod(ci - 3, 2)
                (ga_ref, xl_ref, gl_ref)[sec][:, j * nc:(j + 1) * nc] = z

        def gate_terms(p):
            rows = piece_rows(p)
            r, ig, a, mult, inv_mult = _gate_terms(pr_scr[rows, :], pi_scr[rows, :], br, bi, sp)
            r_ref[rows, :] = r
            ig_ref[rows, :] = ig
            a_ref[rows, :] = a
            im_ref[rows, :] = inv_mult
            b_scr[rows, :] = mult * (ig * u_ref[rows, :])

        def scan(p, hc):
            for t in range(rows_per * p, rows_per * p + rows_per):
                hc = a_ref[t:t + 1, :] * hc + b_scr[t:t + 1, :]
                hl_ref[t:t + 1, :] = hc
            return hc

        project(5)
        project(6)
        xl = xl_ref[...]
        u = cb_ref[...] + sum(cw_ref[k:k + 1, :] * _later(xl, halo[...], CONVW - 1 - k) for k in range(CONVW))
        halo[...] = xl[tm - 8:tm, :]
        u_ref[...] = u
        ub_scr[...] = u.astype(bf16)
        for g in range(NGRP):
            gs = slice(256 * g, 256 * g + 256)
            pr_scr[:, gs] = _dot(ub_scr[:, gs], wr_ref[g])
            pi_scr[:, gs] = _dot(ub_scr[:, gs], wi_ref[g])
        hc = hcar[...]
        gate_terms(0)
        for slot, ci in enumerate(later_chunks):
            project(ci)
            gate_terms(slot + 1)
            hc = scan(slot, hc)
        hcar[...] = scan(pieces - 1, hc)

        @pl.when(i == nt - 1)
        def _():
            finish()

    row = lambda w: pl.BlockSpec((tm, w), lambda i: (i, 0))
    col = lambda w: pl.BlockSpec((w, tm), lambda i: (0, i))
    full = lambda a: pl.BlockSpec(a.shape, lambda i: (0,) * a.ndim)
    big = lambda w, dt: pltpu.HBM((s, w), dt)
    tile = pltpu.VMEM((tm, LW), f32)
    return pl.pallas_call(
        body, name="fwd_fused", grid=(nt,),
        in_specs=[row(D), full(wt), row(384), full(wo_shard), full(conv_w), full(conv_b),
                  full(wr), full(wi), full(br), full(bi), full(lam)],
        out_specs=[col(D), col(KVW), col(KVW), row(D), row(D), row(D)] + [row(LW)] * 6 + [HBM],
        out_shape=[pltpu.HBM((D, s), bf16), pltpu.HBM((KVW, s), bf16), pltpu.HBM((KVW, s), bf16),
                   big(D, f32), big(D, f32), big(D, f32)] + [big(LW, f32)] * 6 + [pltpu.HBM((2 * D, D), bf16)],
        scratch_shapes=[pltpu.VMEM((WO_ROWS, D), bf16), pltpu.VMEM((8, LW), f32),
                        pltpu.VMEM((tm, LW), bf16), tile, tile, tile, pltpu.VMEM((1, LW), f32)] + _comm_sems(1),
        compiler_params=_params(("arbitrary",), 56),
    )(*_in_hbm(h, wt), tabs, wo_shard, conv_w, conv_b, wr, wi, br, bi, lam)


def _lru_bwd(u, hl, dhl, xl, r, ig, a, im, conv_w, wr, wi, lam, tm):
    s = u.shape[0]
    nt = s // tm
    pieces = 8
    rows_per = tm // pieces

    def body(u_ref, h_ref, hp_ref, dh_ref, x_ref, r_ref, ig_ref, a_ref, im_ref, cw_ref, wr_ref, wi_ref,
             lam_ref, dxl_ref, dwr_ref, dwi_ref, dbr_ref, dbi_ref, dlam_ref, dcb_ref, dcw_ref,
             l_scr, du_scr, dpr_scr, dpi_scr, lcar, dunext):
        t0 = pl.program_id(0)
        tile = nt - 1 - t0

        @pl.when(t0 == 0)
        def _():
            lcar[...] = jnp.zeros_like(lcar)
            dunext[...] = jnp.zeros_like(dunext)
            for ref in (dwr_ref, dwi_ref, dbr_ref, dbi_ref, dlam_ref, dcb_ref, dcw_ref):
                ref[...] = jnp.zeros_like(ref)

        lam = lam_ref[...]
        sp = _softplus(-lam)
        hp = jnp.where(tile > 0, hp_ref[...], 0.0)

        def scan(p, c):
            for t in range(rows_per * p + rows_per - 1, rows_per * p - 1, -1):
                lt = dh_ref[t:t + 1, :] + c
                l_scr[t:t + 1, :] = lt
                c = a_ref[t:t + 1, :] * lt
            return c

        def terms(p, sums):
            rows = slice(rows_per * p, rows_per * p + rows_per)
            lt, u, r, i, a, inv_mult = l_scr[rows, :], u_ref[rows, :], r_ref[rows, :], ig_ref[rows, :], \
                a_ref[rows, :], im_ref[rows, :]
            before = hp if p == 0 else h_ref[rows_per * p - 8:rows_per * p, :]
            hprev = _later(h_ref[rows, :], before, 1)
            x2 = -2.0 * LRU_C * r * sp
            mult = jnp.where(x2 > -0.02, -x2 * (1.0 + x2 * (0.5 + x2 * (1.0 / 6.0))), 1.0 - a * a) * inv_mult
            da = lt * hprev
            dmult = lt * (i * u)
            di = lt * mult * u
            du_scr[rows, :] = lt * mult * i
            dla = da * a - dmult * (a * a) * inv_mult
            dr = dla * (-LRU_C * sp)
            dpr = dr * r * (1.0 - r)
            dpi = di * i * (1.0 - i)
            dpr_scr[rows, :] = dpr.astype(bf16)
            dpi_scr[rows, :] = dpi.astype(bf16)
            col = lambda t: jnp.sum(t, axis=0, keepdims=True)
            return sums[0] + col(dla * (-LRU_C * r)), sums[1] + col(dpr), sums[2] + col(dpi)

        sums = (jnp.zeros((1, LW), f32),) * 3
        c = scan(pieces - 1, lcar[...])
        for p in range(pieces - 1, -1, -1):
            if p > 0:
                c = scan(p - 1, c)
            sums = terms(p, sums)
        lcar[...] = c
        dlam_ref[...] += sums[0]
        dbr_ref[...] += sums[1]
        dbi_ref[...] += sums[2]

        ub = u_ref[...].astype(bf16)
        dug = []
        for g in range(NGRP):
            gs = slice(256 * g, 256 * g + 256)
            dwr_ref[g] += _dot_tn(ub[:, gs], dpr_scr[:, gs])
            dwi_ref[g] += _dot_tn(ub[:, gs], dpi_scr[:, gs])
            dug.append(_dot_nt(dpr_scr[:, gs], wr_ref[g]) + _dot_nt(dpi_scr[:, gs], wi_ref[g]))
        du = du_scr[...] + jnp.concatenate(dug, axis=1)

        dcb_ref[...] += jnp.sum(du, axis=0, keepdims=True)
        x = x_ref[...]
        after = dunext[...]
        dxl = jnp.zeros_like(du)
        for k in range(CONVW):
            e = _earlier(du, after, CONVW - 1 - k)
            dxl = dxl + cw_ref[k:k + 1, :] * e
            dcw_ref[k:k + 1, :] += jnp.sum(e * x, axis=0, keepdims=True)
        dxl_ref[...] = dxl.astype(bf16)
        dunext[...] = du[0:8, :]

        @pl.when(t0 == nt - 1)
        def _():
            dlam_ref[...] = dlam_ref[...] * (-_sigmoid(-lam))

    rev = lambda i: (nt - 1 - i, 0)
    row = pl.BlockSpec((tm, LW), rev)
    prev8 = pl.BlockSpec((8, LW), lambda i: (jnp.maximum((nt - 1 - i) * (tm // 8) - 1, 0), 0))
    full = lambda a: pl.BlockSpec(a.shape, lambda i: (0,) * a.ndim)
    vec = pl.BlockSpec((1, LW), lambda i: (0, 0))
    bd = pl.BlockSpec((NGRP, 256, 256), lambda i: (0, 0, 0))
    return pl.pallas_call(
        body, name="lru_bwd", grid=(nt,),
        in_specs=[row, row, prev8, row, row, row, row, row, row, full(conv_w), full(wr), full(wi), full(lam)],
        out_specs=[row, bd, bd, vec, vec, vec, vec, pl.BlockSpec((CONVW, LW), lambda i: (0, 0))],
        out_shape=[pltpu.HBM((s, LW), bf16),
                   jax.ShapeDtypeStruct((NGRP, 256, 256), f32), jax.ShapeDtypeStruct((NGRP, 256, 256), f32),
                   jax.ShapeDtypeStruct((1, LW), f32), jax.ShapeDtypeStruct((1, LW), f32),
                   jax.ShapeDtypeStruct((1, LW), f32), jax.ShapeDtypeStruct((1, LW), f32),
                   jax.ShapeDtypeStruct((CONVW, LW), f32)],
        scratch_shapes=[pltpu.VMEM((tm, LW), f32), pltpu.VMEM((tm, LW), f32), pltpu.VMEM((tm, LW), bf16),
                        pltpu.VMEM((tm, LW), bf16), pltpu.VMEM((1, LW), f32), pltpu.VMEM((8, LW), f32)],
        compiler_params=_params(("arbitrary",), 56),
    )(*_in_hbm(u, hl, hl, dhl, xl, r, ig, a, im), conv_w, wr, wi, lam)


def _gated_norm(t, gate, gain):
    sg = _sigmoid(gate)
    silu = gate * sg
    p = t * silu
    rstd = lax.rsqrt(jnp.mean(p * p, axis=-1, keepdims=True) + EPS)
    ph = p * rstd
    return sg, silu, rstd, ph, ph * gain


def _gated_norm_bwd(dy, t, gate, gain, sg, silu, rstd, ph):
    w = dy * gain
    dp = rstd * (w - ph * jnp.mean(w * ph, axis=-1, keepdims=True))
    dgate = dp * t * (sg * (1.0 + gate * (1.0 - sg)))
    return jnp.sum(dy * ph, axis=0, keepdims=True), dp * silu, dgate


def _out_fwd_bwd(x, tgt, o, ga, hl, gl, again, lgain, fgain, wo, tm):
    s = x.shape[0]
    nt = s // tm

    def body(x_ref, t_ref, o_ref, ga_ref, hl_ref, gl_ref, ag_ref, lg_ref, fg_ref, wo_ref,
             dx2_ref, do_ref, dga_ref, dhl_ref, dgl_ref, dwo_ref, gfg_ref, gag_ref, glg_ref, loss_ref, acc):
        i = pl.program_id(0)

        @pl.when(i == 0)
        def _():
            acc[...] = jnp.zeros_like(acc)
            for ref in (gfg_ref, gag_ref, glg_ref, loss_ref):
                ref[...] = jnp.zeros_like(ref)

        oo = jnp.concatenate([o_ref[128 * j:128 * j + 128, :].T for j in range(D // 128)], axis=1)
        gga, hh, ggl = ga_ref[...], hl_ref[...], gl_ref[...]
        ag, lg, fg = ag_ref[...], lg_ref[...], fg_ref[...]
        sga, silua, ra, pah, ya = _gated_norm(oo, gga, ag)
        sgl, silul, rl, plh, yl = _gated_norm(hh, ggl, lg)
        yab, ylb = ya.astype(bf16), yl.astype(bf16)
        y = _dot(yab, wo_ref[0:D, :]) + _dot(ylb, wo_ref[D:2 * D, :])
        x2 = x_ref[...] + y
        r2 = lax.rsqrt(jnp.mean(x2 * x2, axis=-1, keepdims=True) + EPS)
        x2h = x2 * r2
        err = x2h * fg - t_ref[...]
        loss_ref[...] += 0.5 * jnp.sum(jnp.sum(err * err, axis=-1, keepdims=True) * (1.0 / D))
        dout = err * (1.0 / D)
        gfg_ref[...] += jnp.sum(dout * x2h, axis=0, keepdims=True)
        w = dout * fg
        dx2 = r2 * (w - x2h * jnp.mean(w * x2h, axis=-1, keepdims=True))
        dx2_ref[...] = dx2
        dyb = dx2.astype(bf16)
        acc[0:D, :] += _dot_tn(yab, dyb)
        acc[D:2 * D, :] += _dot_tn(ylb, dyb)
        dya = _dot_nt(dyb, wo_ref[0:D, :])
        dyl = _dot_nt(dyb, wo_ref[D:2 * D, :])
        gag, do, dga = _gated_norm_bwd(dya, oo, gga, ag, sga, silua, ra, pah)
        glg, dhl, dgl = _gated_norm_bwd(dyl, hh, ggl, lg, sgl, silul, rl, plh)
        gag_ref[...] += gag
        glg_ref[...] += glg
        dob = do.astype(bf16)
        for j in range(D // 128):
            do_ref[128 * j:128 * j + 128, :] = dob[:, 128 * j:128 * j + 128].T
        dga_ref[...] = dga.astype(bf16)
        dhl_ref[...] = dhl
        dgl_ref[...] = dgl.astype(bf16)

        @pl.when(i == nt - 1)
        def _():
            dwo_ref[...] = acc[...].astype(bf16)

    row = pl.BlockSpec((tm, D), lambda i: (i, 0))
    col = pl.BlockSpec((D, tm), lambda i: (0, i))
    vec = pl.BlockSpec((1, D), lambda i: (0, 0))
    mat = pl.BlockSpec((2 * D, D), lambda i: (0, 0))
    return pl.pallas_call(
        body, name="out_fwd_bwd", grid=(nt,),
        in_specs=[row, row, col, row, row, row] + [vec] * 3 + [mat],
        out_specs=[row, col, row, row, row] + [mat, vec, vec, vec, pl.BlockSpec((1, 128), lambda i: (0, 0))],
        out_shape=[pltpu.HBM((s, D), f32), pltpu.HBM((D, s), bf16),
                   pltpu.HBM((s, D), bf16), pltpu.HBM((s, D), f32),
                   pltpu.HBM((s, D), bf16), pltpu.HBM((2 * D, D), bf16),
                   jax.ShapeDtypeStruct((1, D), f32), jax.ShapeDtypeStruct((1, D), f32),
                   jax.ShapeDtypeStruct((1, D), f32), jax.ShapeDtypeStruct((1, 128), f32)],
        scratch_shapes=[pltpu.VMEM((2 * D, D), f32)],
        compiler_params=_params(("arbitrary",), 56),
    )(*_in_hbm(x, tgt, o, ga, hl, gl), again, lgain, fgain, *_in_hbm(wo))


def _bwd_in(x, dx2, dq, dk, dv, dga, dxl, dgl, ln_gain, wt, tabs, tm):
    s = x.shape[0]

    def body(x_ref, dx2_ref, dq_ref, dk_ref, dv_ref, dga_ref, dxl_ref, dgl_ref, g_ref, wt_ref,
             tab_ref, gx_ref, gln_ref, dzt_ref):
        @pl.when(pl.program_id(0) == 0)
        def _():
            gln_ref[...] = jnp.zeros_like(gln_ref)

        c, sa, sb = (t.T for t in _tables(tab_ref))
        for j in range(D // 128):
            js = slice(128 * j, 128 * j + 128)
            dzt_ref[js, :] = (_unrope_t(dq_ref[js, :], c, sa, sb) * (HD ** -0.5)).astype(bf16)
        for j in range(KVW // 128):
            js = slice(128 * j, 128 * j + 128)
            dzt_ref[D + 128 * j:D + 128 * j + 128, :] = _unrope_t(dk_ref[js, :], c, sa, sb).astype(bf16)
        dzt_ref[D + KVW:D + 2 * KVW, :] = dv_ref[...].astype(bf16)
        first = D + 2 * KVW
        dh = _dot_tn(dzt_ref[0:512, :], wt_ref[0:512, :])
        for ci in range(1, first // 512):
            dh = dh + _dot_tn(dzt_ref[512 * ci:512 * ci + 512, :], wt_ref[512 * ci:512 * ci + 512, :])
        for sec, ref in enumerate((dga_ref, dxl_ref, dgl_ref)):
            for j in range(D // 512):
                rows = slice(first + D * sec + 512 * j, first + D * sec + 512 * j + 512)
                dh = dh + _dot(ref[:, 512 * j:512 * j + 512], wt_ref[rows, :])
            for j in range(D // 128):
                dzt_ref[first + D * sec + 128 * j:first + D * sec + 128 * j + 128, :] = ref[:, 128 * j:128 * j + 128].T
        xx = x_ref[...]
        rstd = lax.rsqrt(jnp.mean(xx * xx, axis=-1, keepdims=True) + EPS)
        xh = xx * rstd
        gln_ref[...] += jnp.sum(dh * xh, axis=0, keepdims=True)
        w = dh * g_ref[...]
        gx_ref[...] = dx2_ref[...] + rstd * (w - xh * jnp.mean(w * xh, axis=-1, keepdims=True))

    row = lambda w: pl.BlockSpec((tm, w), lambda i: (i, 0))
    col = lambda w: pl.BlockSpec((w, tm), lambda i: (0, i))
    full = lambda a: pl.BlockSpec(a.shape, lambda i: (0, 0))
    return pl.pallas_call(
        body, name="bwd_in", grid=(s // tm,),
        in_specs=[row(D), row(D), col(D), col(KVW), col(KVW), row(D), row(D), row(D), full(ln_gain), full(wt),
                  row(384)],
        out_specs=[row(D), pl.BlockSpec((1, D), lambda i: (0, 0)), col(NIN)],
        out_shape=[pltpu.HBM((s, D), f32), jax.ShapeDtypeStruct((1, D), f32),
                   pltpu.HBM((NIN, s), bf16)],
        compiler_params=_params(("arbitrary",), 56),
    )(*_in_hbm(x, dx2, dq, dk, dv, dga, dxl, dgl), ln_gain, *_in_hbm(wt), tabs)


WT_TERMS = 4


def _dwt_scatter(dzt, h, small, tm):
    s = h.shape[0]
    nk = s // tm
    srows = small.shape[0] // NDEV
    last = NDEV - 1

    def body(order_ref, dz_ref, h_ref, sm_ref, lwt_ref, lsm_ref, acc, stage, given, relayed, send_sems, recv_sems,
             local_sem, sm_send, sm_recv, sm_local):
        j, k = pl.program_id(0), pl.program_id(1)
        x, y, c = _place()
        sibling = (x, y, 1 - c)
        near = (x ^ (1 - c), y ^ c)
        far = (x ^ c, y ^ (1 - c))
        sm_start, sm_finish = _scatter_ops([sm_ref], [lsm_ref], sm_send, sm_recv, sm_local)

        def send(step):
            if step == last - 1:
                dst, to = lwt_ref.at[1], sibling
            elif step % 2 == 0:
                dst, to = given.at[step // 2], sibling
            elif step == 1:
                dst, to = relayed, (*near, c)
            else:
                dst, to = lwt_ref.at[1 + step // 2], (*(near if step == 3 else far), c)
            return pltpu.make_async_remote_copy(
                src_ref=stage.at[step % 2], dst_ref=dst, send_sem=send_sems.at[step], recv_sem=recv_sems.at[step],
                device_id=to, device_id_type=MESH)

        def keep():
            return pltpu.make_async_copy(stage.at[last % 2], lwt_ref.at[0], local_sem)

        @pl.when((j == 0) & (k == 0))
        def _():
            sm_start()

        @pl.when(k == 0)
        def _():
            acc[...] = jnp.zeros_like(acc)

        acc[...] += _dot(dz_ref[...], h_ref[...])

        for step in range(NDEV):
            @pl.when((k == nk - 1) & (j == step))
            def _(step=step):
                if step >= 2:
                    send(step - 2).wait_send()
                if step % 2 == 1 and step < last:
                    send(step - 1).wait_recv()
                    total = acc[...] + given[step // 2].astype(f32)
                    if step == 5:
                        send(1).wait_recv()
                        total = total + relayed[...].astype(f32)
                    stage[step % 2] = total.astype(bf16)
                else:
                    stage[step % 2] = acc[...].astype(bf16)
                if step < last:
                    send(step).start()
                else:
                    keep().start()
                    send(last - 1).wait_send()
                    for peer_step in (3, 5, last - 1):
                        send(peer_step).wait_recv()
                    keep().wait()
                    sm_finish()

    x, y, c = _place()
    dest = lambda chip, cc: 4 * chip[0] + 2 * chip[1] + cc
    near, far, diag = (x ^ (1 - c), y ^ c), (x ^ c, y ^ (1 - c)), (1 - x, 1 - y)
    order = jnp.stack([dest(diag, 1 - c), dest(diag, c), dest(far, 1 - c), dest(near, c),
                       dest(near, 1 - c), dest(far, c), dest((x, y), 1 - c), dest((x, y), c)])
    return pl.pallas_call(
        body, name="dwt_scatter",
        grid_spec=pltpu.PrefetchScalarGridSpec(
            num_scalar_prefetch=1, grid=(NDEV, nk),
            in_specs=[pl.BlockSpec((WT_ROWS, tm), lambda j, k, order: (order[j], k)),
                      pl.BlockSpec((tm, D), lambda j, k, order: (k, 0)), HBM],
            out_specs=[HBM, HBM],
            scratch_shapes=[pltpu.VMEM((WT_ROWS, D), f32), pltpu.VMEM((2, WT_ROWS, D), bf16),
                            pltpu.VMEM((3, WT_ROWS, D), bf16), pltpu.VMEM((WT_ROWS, D), bf16),
                            pltpu.SemaphoreType.DMA((last,)), pltpu.SemaphoreType.DMA((last,)),
                            pltpu.SemaphoreType.DMA(())] + _comm_sems(1)),
        out_shape=[pltpu.HBM((WT_TERMS, WT_ROWS, D), bf16), pltpu.HBM((NDEV, srows, D), f32)],
        compiler_params=_params(("arbitrary", "arbitrary"), 48),
    )(order, *_in_hbm(dzt, h, small))


def _diag_blocks(bd):
    eye = jnp.eye(4, dtype=bd.dtype)
    return jnp.einsum('gjckd,jk->gjcd', bd.reshape(NGRP, 4, HD, 4, HD), eye).reshape(NQ, HD, HD)


def _sequence_step(x, h, tgt, wt, wo_shard, conv_w, p):
    s = x.shape[0]
    tm = min(256, s)
    tabs = _rope_tables(s)
    wr, wi = _block_diag(p["w_rgate"]), _block_diag(p["w_igate"])
    sinks = p["sinks"].reshape(NQ)
    qt, kt, vt, ga, xl, gl, u, hl, r, ig, a, im, wo = _fwd_fused(
        h, wt, tabs, wo_shard, conv_w, p["conv_b"], wr, wi, p["b_rgate"], p["b_igate"], p["lru_lambda"], tm)
    ot = _attn_fwd_t(qt, kt, vt, sinks)
    dx2, dot, dga, dhl, dgl, dwo, g_fg, g_ag, g_lg, loss = _out_fwd_bwd(
        x, tgt, ot, ga, hl, gl, p["attn_out_gain"], p["lru_out_gain"], p["final_gain"], wo, tm)
    dqt, dkt, dvt, dsink, land_wo = _attn_bwd_t(qt, kt, vt, dot, sinks, dwo)
    dxl, dwr, dwi, dbr, dbi, dlam, dcb, dcw = _lru_bwd(u, hl, dhl, xl, r, ig, a, im, conv_w, wr, wi, p["lru_lambda"], tm)
    gx, g_ln, dzt = _bwd_in(x, dx2, dqt, dkt, dvt, dga, dxl, dgl, p["ln_gain"], wt, tabs, tm)
    small = dict(ln_gain=g_ln, sinks=dsink.reshape(NQ, BLK).sum(axis=1)[None], conv_w=dcw, conv_b=dcb,
                 w_rgate=_diag_blocks(dwr), b_rgate=dbr, w_igate=_diag_blocks(dwi), b_igate=dbi, lru_lambda=dlam,
                 attn_out_gain=g_ag, lru_out_gain=g_lg, final_gain=g_fg)
    land_wt, land_sm = _dwt_scatter(dzt, h, _pack_small(small, loss), min(2048, s))
    return gx, land_wt, land_wo, land_sm


def _all_gather(srcs, out_dtypes, name):
    n = len(srcs)
    cast = [a.dtype != dt for a, dt in zip(srcs, out_dtypes)]

    def body(*refs):
        src_refs, out_refs = refs[:n], refs[n:2 * n]
        stage_refs = list(refs[2 * n:2 * n + sum(cast)])
        mine_refs = []
        for a in range(n):
            if cast[a]:
                st = stage_refs.pop(0)
                st[...] = src_refs[a][...].astype(out_dtypes[a])
                mine_refs.append(st)
            else:
                mine_refs.append(src_refs[a])
        start, finish = _gather_ops(mine_refs, out_refs, *refs[-3:])
        start()
        finish()

    vmem = pl.BlockSpec(memory_space=pltpu.VMEM)
    return pl.pallas_call(
        body, name=name,
        in_specs=[vmem] * n, out_specs=[HBM] * n,
        out_shape=[pltpu.HBM((NDEV * a.shape[0], a.shape[1]), dt) for a, dt in zip(srcs, out_dtypes)],
        scratch_shapes=[pltpu.VMEM(a.shape, dt) for a, dt, cst in zip(srcs, out_dtypes, cast) if cst] + _comm_sems(n),
        compiler_params=pltpu.CompilerParams(vmem_limit_bytes=32 * MIB),
    )(*srcs)


def _gather_weights(wt_shard, conv_blk, x, ln_gain, tm):
    s = x.shape[0]

    def body(wt_ref, cw_ref, g_ref, x_ref, wt_all, cw_all, h_ref, stage, xbuf, hbuf, send_sems, recv_sems, local_sems):
        stage[...] = wt_ref[...].astype(bf16)
        start, finish = _relay_gather_ops([stage, cw_ref], [wt_all, cw_all], send_sems, recv_sems, local_sems)
        start()
        gain = g_ref[...]
        for i in range(s // tm):
            rows = pl.ds(i * tm, tm)
            pltpu.sync_copy(x_ref.at[rows, :], xbuf)
            xx = xbuf[...]
            rstd = lax.rsqrt(jnp.mean(xx * xx, axis=-1, keepdims=True) + EPS)
            hbuf[...] = (xx * rstd * gain).astype(bf16)
            pltpu.sync_copy(hbuf, h_ref.at[rows, :])
        finish()

    vmem = pl.BlockSpec(memory_space=pltpu.VMEM)
    return pl.pallas_call(
        body, name="gather_weights",
        in_specs=[vmem, vmem, vmem, HBM], out_specs=[HBM, HBM, HBM],
        out_shape=[pltpu.HBM((NIN, D), bf16), pltpu.HBM((NDEV * 8, 128), f32), pltpu.HBM((s, D), bf16)],
        scratch_shapes=[pltpu.VMEM((WT_ROWS, D), bf16), pltpu.VMEM((tm, D), f32), pltpu.VMEM((tm, D), bf16)]
        + _comm_sems(2),
        compiler_params=pltpu.CompilerParams(vmem_limit_bytes=32 * MIB),
    )(wt_shard, conv_blk, ln_gain, *_in_hbm(x))


def _sum_slots(land, tr, name):
    terms, rows, cols = land.shape

    def body(l_ref, o_ref):
        acc = l_ref[0].astype(f32)
        for d in range(1, terms):
            acc = acc + l_ref[d].astype(f32)
        o_ref[...] = acc

    return pl.pallas_call(
        body, name=name, grid=(rows // tr,),
        in_specs=[pl.BlockSpec((terms, tr, cols), lambda i: (0, i, 0))],
        out_specs=pl.BlockSpec((tr, cols), lambda i: (i, 0)),
        out_shape=jax.ShapeDtypeStruct((rows, cols), f32),
        compiler_params=_params(("arbitrary",), 32),
    )(*_in_hbm(land))


def _adam_math(w, g, m, v):
    m2 = ADAM_B1 * m + (1.0 - ADAM_B1) * g
    v2 = ADAM_B2 * v + (1.0 - ADAM_B2) * (g * g)
    m_hat = m2 / (1.0 - ADAM_B1 ** ADAM_STEP)
    v_hat = v2 / (1.0 - ADAM_B2 ** ADAM_STEP)
    delta = -ADAM_LR * (m_hat / (jnp.sqrt(v_hat) + ADAM_EPS) + ADAM_WD * w)
    return delta, m2, v2


def _reduce_adamw(land, w, m, v, tr, name):
    terms, rows, cols = land.shape

    def body(l_ref, w_ref, m_ref, v_ref, g_ref, d_ref, m2_ref, v2_ref):
        g = l_ref[0].astype(f32)
        for t in range(1, terms):
            g = g + l_ref[t].astype(f32)
        g_ref[...] = g
        d_ref[...], m2_ref[...], v2_ref[...] = _adam_math(w_ref[...], g, m_ref[...], v_ref[...])

    blk = pl.BlockSpec((tr, cols), lambda i: (i, 0))
    return pl.pallas_call(
        body, name=name, grid=(rows // tr,),
        in_specs=[pl.BlockSpec((terms, tr, cols), lambda i: (0, i, 0))] + [blk] * 3, out_specs=[blk] * 4,
        out_shape=[jax.ShapeDtypeStruct((rows, cols), f32)] * 4,
        compiler_params=_params(("arbitrary",), 32),
    )(*_in_hbm(land), w, m, v)


VEC_NAMES = ("ln_gain", "conv_b", "b_rgate", "b_igate", "lru_lambda", "attn_out_gain", "lru_out_gain", "final_gain")
ROW_RGATE, ROW_IGATE, ROW_VEC, ROW_SINKS = 0, 64, 128, 136
LOSS_LANE = NQ


def _adamw_small(g_rep, g_conv, w, m, v):
    names = list(VEC_NAMES) + ["sinks", "conv_w", "w_rgate", "w_igate"]
    ins = [g_rep, g_conv] + [d[k] for k in names for d in (w, m, v)]

    def body(*refs):
        g_ref, gc_ref = refs[0], refs[1]
        in_refs = refs[2:2 + 3 * len(names)]
        out_refs = refs[2 + 3 * len(names):]

        def update(j, g, at=None):
            w_ref, m_ref, v_ref = in_refs[3 * j:3 * j + 3]
            outs = out_refs[4 * j:4 * j + 4]
            pick = (lambda r: r[...]) if at is None else (lambda r: r[at])
            res = (g,) + _adam_math(pick(w_ref), g, pick(m_ref), pick(v_ref))
            for o_ref, val in zip(outs, res):
                if at is None:
                    o_ref[...] = val
                else:
                    o_ref[at] = val

        for j in range(len(VEC_NAMES)):
            update(j, g_ref[ROW_VEC + j:ROW_VEC + j + 1, :])
        update(len(VEC_NAMES), g_ref[ROW_SINKS:ROW_SINKS + 1, 0:NQ])
        update(len(VEC_NAMES) + 1, gc_ref[...], at=0)
        for gi, row0 in ((len(VEC_NAMES) + 2, ROW_RGATE), (len(VEC_NAMES) + 3, ROW_IGATE)):
            for nb in range(NQ):
                update(gi, g_ref[row0:row0 + HD, HD * nb:HD * nb + HD], at=(0, nb))

    vmem = pl.BlockSpec(memory_space=pltpu.VMEM)
    out_shape = [jax.ShapeDtypeStruct(w[k].shape, f32) for k in names for _ in range(4)]
    outs = pl.pallas_call(
        body, name="adamw_small",
        in_specs=[vmem] * len(ins), out_specs=[vmem] * len(out_shape), out_shape=out_shape,
        compiler_params=pltpu.CompilerParams(vmem_limit_bytes=32 * MIB),
    )(*ins)
    return {k: tuple(outs[4 * j:4 * j + 4]) for j, k in enumerate(names)}


def _pack_small(small, loss):
    gate = lambda g: g.transpose(1, 0, 2).reshape(HD, NQ * HD)
    row_s = jnp.concatenate([small["sinks"], loss[:, LOSS_LANE:128], jnp.zeros((1, D - 128), f32)], axis=1)
    rep = jnp.concatenate([gate(small["w_rgate"]), gate(small["w_igate"])] + [small[k] for k in VEC_NAMES]
                          + [row_s, jnp.zeros((SMALL_ROWS - ROW_SINKS - 1, D), f32)], axis=0)
    conv = small["conv_w"].reshape(CONVW, NDEV, 128).transpose(1, 0, 2)
    conv = jnp.pad(conv, ((0, 0), (0, 8 - CONVW), (0, D - 128)))
    return jnp.concatenate([rep.reshape(NDEV, SMALL_PER, D), conv], axis=1).reshape(NDEV * (SMALL_PER + 8), D)


def kernel(x, ln_gain, w_in, sinks, conv_w, conv_b, w_rgate, b_rgate, w_igate, b_igate, lru_lambda, attn_out_gain, lru_out_gain, w_out, final_gain, loss_target, m_ln_gain, m_w_in, m_sinks, m_conv_w, m_conv_b, m_w_rgate, m_b_rgate, m_w_igate, m_b_igate, m_lru_lambda, m_attn_out_gain, m_lru_out_gain, m_w_out, m_final_gain, v_ln_gain, v_w_in, v_sinks, v_conv_w, v_conv_b, v_w_rgate, v_b_rgate, v_w_igate, v_b_igate, v_lru_lambda, v_attn_out_gain, v_lru_out_gain, v_w_out, v_final_gain):
    w = dict(ln_gain=ln_gain, sinks=sinks, conv_w=conv_w, conv_b=conv_b, w_rgate=w_rgate, b_rgate=b_rgate,
             w_igate=w_igate, b_igate=b_igate, lru_lambda=lru_lambda, attn_out_gain=attn_out_gain,
             lru_out_gain=lru_out_gain, final_gain=final_gain.reshape(1, D))
    m = dict(ln_gain=m_ln_gain, sinks=m_sinks, conv_w=m_conv_w, conv_b=m_conv_b, w_rgate=m_w_rgate,
             b_rgate=m_b_rgate, w_igate=m_w_igate, b_igate=m_b_igate, lru_lambda=m_lru_lambda,
             attn_out_gain=m_attn_out_gain, lru_out_gain=m_lru_out_gain, final_gain=m_final_gain.reshape(1, D))
    v = dict(ln_gain=v_ln_gain, sinks=v_sinks, conv_w=v_conv_w, conv_b=v_conv_b, w_rgate=v_w_rgate,
             b_rgate=v_b_rgate, w_igate=v_w_igate, b_igate=v_b_igate, lru_lambda=v_lru_lambda,
             attn_out_gain=v_attn_out_gain, lru_out_gain=v_lru_out_gain, final_gain=v_final_gain.reshape(1, D))

    conv_blk = jnp.pad(conv_w[0], ((0, 8 - CONVW), (0, 0)))
    wt, cw_all, h = _gather_weights(w_in[0].T, conv_blk, x[0], ln_gain, min(512, x.shape[1]))
    conv_full = cw_all.reshape(NDEV, 8, 128)[:, 0:CONVW].transpose(1, 0, 2).reshape(CONVW, LW)

    p = {k: (w[k][0] if k in ("w_rgate", "w_igate") else w[k]) for k in w if k != "conv_w"}
    gx, land_wt, land_wo, land_sm = _sequence_step(x[0], h, loss_target[0], wt, w_out[0], conv_full, p)

    g_sm = _sum_slots(land_sm, SMALL_PER + 8, "sum_small")
    (g_rep,) = _all_gather([g_sm[0:SMALL_PER]], [f32], "gather_small")
    g_conv = g_sm[SMALL_PER:SMALL_PER + CONVW, 0:128]

    wins = _reduce_adamw(land_wt, w_in[0].T, m_w_in[0].T, v_w_in[0].T, 192, "adamw_w_in")
    g_win, d_win, m_win, v_win = (t.T for t in wins)
    g_wo, d_wo, m_wo, v_wo = _reduce_adamw(land_wo, w_out[0], m_w_out[0], v_w_out[0], 256, "adamw_w_out")
    res = _adamw_small(g_rep, g_conv, w, m, v)
    res["w_in"] = tuple(t[None] for t in (g_win, d_win, m_win, v_win))
    res["w_out"] = tuple(t[None] for t in (g_wo, d_wo, m_wo, v_wo))
    res["final_gain"] = tuple(t.reshape(D) for t in res["final_gain"])

    order = ("ln_gain", "w_in", "sinks", "conv_w", "conv_b", "w_rgate", "b_rgate", "w_igate", "b_igate",
             "lru_lambda", "attn_out_gain", "lru_out_gain", "w_out", "final_gain")
    total_loss = g_rep[ROW_SINKS, LOSS_LANE]
    return (total_loss, gx[None]) + tuple(res[k][i] for i in range(4) for k in order)
```

```python
import jax
import jax.numpy as jnp
from jax import lax
from jax.experimental import pallas as pl
from jax.experimental.pallas import tpu as pltpu

f32 = jnp.float32
bf16 = jnp.bfloat16

D = 1024
HD = 64
NQ = 16
NKV = 4
GROUP = NQ // NKV
KVW = NKV * HD
BLK = 128
ROT = 16
THETA = 500000.0
NEG = -1e30
LW = 1024
NGRP = 4
CONVW = 4
LRU_C = 8.0
NIN = 4608
EPS = 1e-6
NDEV = 8
WT_ROWS = NIN // NDEV
WO_ROWS = 2 * D // NDEV
SMALL_ROWS = 192
SMALL_PER = SMALL_ROWS // NDEV

ADAM_LR = 0.001
ADAM_B1 = 0.9
ADAM_B2 = 0.999
ADAM_EPS = 1e-08
ADAM_WD = 0.01
ADAM_STEP = 10

NT = (((1,), (1,)), ((), ()))
TN = (((0,), (0,)), ((), ()))
MESH = pl.DeviceIdType.MESH
MIB = 1024 * 1024


def _dot(a, b):
    return jnp.dot(a, b, preferred_element_type=f32)


def _dot_nt(a, b):
    return lax.dot_general(a, b, NT, preferred_element_type=f32)


def _dot_tn(a, b):
    return lax.dot_general(a, b, TN, preferred_element_type=f32)


def _params(sem, vmem_mib):
    return pltpu.CompilerParams(dimension_semantics=sem, vmem_limit_bytes=vmem_mib * MIB)


def _sigmoid(x):
    return 0.5 * jnp.tanh(0.5 * x) + 0.5


def _softplus(x):
    return jnp.maximum(x, 0.0) + jnp.log(1.0 + jnp.exp(-jnp.abs(x)))


def _tables(tab_ref):
    return tab_ref[:, 0:128], tab_ref[:, 128:256], tab_ref[:, 256:384]


def _rope(t, c, sa, sb):
    return t * c + pltpu.roll(t, 8, 1) * sa + pltpu.roll(t, 120, 1) * sb


def _unrope_t(dr, c, sa, sb):
    return dr * c + pltpu.roll(dr * sa, 120, 0) + pltpu.roll(dr * sb, 8, 0)


def _place():
    return lax.axis_index("x"), lax.axis_index("y"), lax.axis_index("c")


def _gather_ops(mine_refs, out_refs, send_sems, recv_sems, local_sems):
    n = len(mine_refs)
    x, y, c = _place()
    me, sibling = (x, y, c), (x, y, 1 - c)
    chips = [(1 - x, y), (x, 1 - y), (1 - x, 1 - y)]

    def rows(a, dev):
        m = mine_refs[a].shape[0]
        return out_refs[a].at[pl.ds((4 * dev[0] + 2 * dev[1] + dev[2]) * m, m), :]

    def copy(a, k, block, to, own=False):
        return pltpu.make_async_remote_copy(
            src_ref=mine_refs[a] if own else rows(a, block), dst_ref=rows(a, block),
            send_sem=send_sems.at[a, k], recv_sem=recv_sems.at[a, k], device_id=to, device_id_type=MESH)

    def local(a):
        return pltpu.make_async_copy(mine_refs[a], rows(a, me), local_sems.at[a])

    def first(a):
        return [copy(a, 0, me, sibling, own=True)] + [copy(a, 1 + j, me, (*chip, c), own=True)
                                                      for j, chip in enumerate(chips)]

    def start():
        for a in range(n):
            local(a).start()
            for cp in first(a):
                cp.start()

    def finish():
        for j, chip in enumerate(chips):
            for a in range(n):
                copy(a, 1 + j, (*chip, c), me).wait_recv()
                copy(a, 4 + j, (*chip, c), sibling).start()
        for a in range(n):
            copy(a, 0, sibling, me).wait_recv()
            for j, chip in enumerate(chips):
                copy(a, 4 + j, (*chip, 1 - c), me).wait_recv()
        for a in range(n):
            for cp in first(a) + [copy(a, 4 + j, (*chip, c), sibling) for j, chip in enumerate(chips)]:
                cp.wait_send()
            local(a).wait()

    return start, finish


def _relay_gather_ops(mine_refs, out_refs, send_sems, recv_sems, local_sems):
    n = len(mine_refs)
    x, y, c = _place()
    me, sibling = (x, y, c), (x, y, 1 - c)
    near = (x ^ (1 - c), y ^ c)
    far = (x ^ c, y ^ (1 - c))
    diag = (1 - x, 1 - y)

    def rows(a, dev):
        m = mine_refs[a].shape[0]
        return out_refs[a].at[pl.ds((4 * dev[0] + 2 * dev[1] + dev[2]) * m, m), :]

    def copy(a, k, block, to, own=False):
        return pltpu.make_async_remote_copy(
            src_ref=mine_refs[a] if own else rows(a, block), dst_ref=rows(a, block),
            send_sem=send_sems.at[a, k], recv_sem=recv_sems.at[a, k], device_id=to, device_id_type=MESH)

    def local(a):
        return pltpu.make_async_copy(mine_refs[a], rows(a, me), local_sems.at[a])

    def sends(a):
        return [copy(a, 0, me, sibling, own=True), copy(a, 1, me, (*near, c), own=True),
                copy(a, 2, me, (*far, c), own=True), copy(a, 3, (*near, c), (*far, c)),
                copy(a, 4, (*near, c), sibling), copy(a, 5, (*far, c), sibling), copy(a, 6, (*diag, c), sibling)]

    def arrivals(a):
        return [copy(a, 0, sibling, me), copy(a, 1, (*near, c), me), copy(a, 2, (*far, c), me),
                copy(a, 3, (*diag, c), me), copy(a, 4, (*far, 1 - c), me), copy(a, 5, (*near, 1 - c), me),
                copy(a, 6, (*diag, 1 - c), me)]

    def start():
        for a in range(n):
            local(a).start()
            for cp in sends(a)[0:3]:
                cp.start()

    def finish():
        for first, then in ((1, (3, 4)), (2, (5,)), (3, (6,))):
            for a in range(n):
                arrivals(a)[first].wait_recv()
                for k in then:
                    sends(a)[k].start()
        for a in range(n):
            for k in (0, 4, 5, 6):
                arrivals(a)[k].wait_recv()
        for a in range(n):
            for cp in sends(a):
                cp.wait_send()
            local(a).wait()

    return start, finish


def _scatter_ops(src_refs, land_refs, send_sems, recv_sems, local_sems):
    n = len(src_refs)
    x, y, c = _place()
    my = 4 * x + 2 * y + c

    def peer(k):
        return x ^ (k >> 2), y ^ ((k >> 1) & 1), c ^ (k & 1)

    def piece(a, dev):
        m = src_refs[a].shape[0] // NDEV
        return src_refs[a].at[pl.ds(dev * m, m), :]

    def local(a):
        return pltpu.make_async_copy(piece(a, my), land_refs[a].at[my], local_sems.at[a])

    def send(a, k):
        px, py, pc = peer(k)
        return pltpu.make_async_remote_copy(
            src_ref=piece(a, 4 * px + 2 * py + pc), dst_ref=land_refs[a].at[my],
            send_sem=send_sems.at[a, k - 1], recv_sem=recv_sems.at[a, k - 1],
            device_id=(px, py, pc), device_id_type=MESH)

    def arrival(a, k):
        px, py, pc = peer(k)
        return pltpu.make_async_remote_copy(
            src_ref=piece(a, my), dst_ref=land_refs[a].at[4 * px + 2 * py + pc],
            send_sem=send_sems.at[a, k - 1], recv_sem=recv_sems.at[a, k - 1],
            device_id=(px, py, pc), device_id_type=MESH)

    def start():
        for a in range(n):
            local(a).start()
        for k in range(1, NDEV):
            for a in range(n):
                send(a, k).start()

    def finish():
        for k in range(1, NDEV):
            for a in range(n):
                send(a, k).wait_send()
        for k in range(1, NDEV):
            for a in range(n):
                arrival(a, k).wait_recv()
        for a in range(n):
            local(a).wait()

    return start, finish


def _in_hbm(*arrays):
    return tuple(pltpu.with_memory_space_constraint(a, pltpu.HBM) for a in arrays)


def _comm_sems(n):
    return [pltpu.SemaphoreType.DMA((n, 7)), pltpu.SemaphoreType.DMA((n, 7)), pltpu.SemaphoreType.DMA((n,))]


HBM = pl.BlockSpec(memory_space=pltpu.HBM)


def _sink_rows(sinks):
    return jnp.repeat(sinks.reshape(NKV, GROUP), BLK, axis=1)


def _band_softmax(s2_ref, ls, prev_offset, sink_row):
    jj = lax.broadcasted_iota(jnp.int32, (BLK, BLK), 0)
    ii = lax.broadcasted_iota(jnp.int32, (BLK, BLK), 1)
    from_prev = jj > ii
    sc = jnp.where(from_prev, s2_ref[0:BLK, ls] + prev_offset, s2_ref[BLK:2 * BLK, ls])
    m = jnp.maximum(jnp.max(sc, axis=0, keepdims=True), sink_row)
    p = jnp.exp(sc - m)
    es = jnp.exp(sink_row - m)
    inv = 1.0 / (jnp.sum(p, axis=0, keepdims=True) + es)
    return from_prev, p * inv, es * inv


def _put_split(dst_ref, ls, t, from_prev):
    t = t.astype(bf16)
    zero = jnp.zeros_like(t)
    dst_ref[0:BLK, ls] = jnp.where(from_prev, t, zero)
    dst_ref[BLK:2 * BLK, ls] = jnp.where(from_prev, zero, t)


def _heads_side_by_side(ref, h):
    return jnp.concatenate([ref[HD * (GROUP * h + g):HD * (GROUP * h + g) + HD, :] for g in range(GROUP)], axis=1)


def _kv_specs_t():
    prev = pl.BlockSpec((KVW, BLK), lambda n: (0, jnp.maximum(n - 1, 0)))
    cur = pl.BlockSpec((KVW, BLK), lambda n: (0, n))
    return [prev, cur, prev, cur]


def _attn_fwd_t(qt, kt, vt, sinks):
    s = qt.shape[1]

    def body(sink_ref, q_ref, kp_ref, kc_ref, vp_ref, vc_ref, o_ref, s2_scr, pn2_scr):
        n = pl.program_id(0)
        off = jnp.where(n > 0, 0.0, NEG)

        def scores(h):
            hs = slice(HD * h, HD * h + HD)
            kh = jnp.concatenate([kp_ref[hs, :], kc_ref[hs, :]], axis=1)
            s2_scr[h % 2] = _dot_tn(kh, _heads_side_by_side(q_ref, h))

        def probs(h):
            for g in range(GROUP):
                ls = slice(BLK * g, BLK * g + BLK)
                from_prev, pn, _ = _band_softmax(s2_scr.at[h % 2], ls, off, sink_ref[h:h + 1, ls])
                _put_split(pn2_scr.at[h % 2], ls, pn, from_prev)

        def outputs(h):
            hs = slice(HD * h, HD * h + HD)
            vh = jnp.concatenate([vp_ref[hs, :], vc_ref[hs, :]], axis=1)
            og = _dot(vh, pn2_scr[h % 2])
            for g in range(GROUP):
                a = GROUP * h + g
                o_ref[HD * a:HD * a + HD, :] = og[:, BLK * g:BLK * g + BLK]

        scores(0)
        for h in range(NKV):
            if h + 1 < NKV:
                scores(h + 1)
            probs(h)
            outputs(h)

    return pl.pallas_call(
        body, name="attn_fwd", grid=(s // BLK,),
        in_specs=[pl.BlockSpec((NKV, GROUP * BLK), lambda n: (0, 0)), pl.BlockSpec((D, BLK), lambda n: (0, n))]
        + _kv_specs_t(),
        out_specs=pl.BlockSpec((D, BLK), lambda n: (0, n)),
        out_shape=pltpu.HBM((D, s), f32),
        scratch_shapes=[pltpu.VMEM((2, 2 * BLK, GROUP * BLK), f32), pltpu.VMEM((2, 2 * BLK, GROUP * BLK), bf16)],
        compiler_params=_params(("arbitrary",), 32),
    )(_sink_rows(sinks), *_in_hbm(qt, kt, kt, vt, vt))


def _attn_bwd_t(qt, kt, vt, dot, sinks, dwo):
    s = qt.shape[1]
    nb = s // BLK

    def body(sink_ref, q_ref, do_ref, kp_ref, kc_ref, vp_ref, vc_ref, dwo_ref, dq_ref, dk_ref, dv_ref, ds_ref,
             land_ref, dk_hold, dv_hold, s2_scr, dp2_scr, pn2_scr, ds2_scr, send_sems, recv_sems, local_sems):
        n = pl.program_id(0)
        start, finish = _scatter_ops([dwo_ref], [land_ref], send_sems, recv_sems, local_sems)

        @pl.when(n == 0)
        def _():
            start()
            dk_hold[...] = jnp.zeros_like(dk_hold)
            dv_hold[...] = jnp.zeros_like(dv_hold)
            ds_ref[...] = jnp.zeros_like(ds_ref)

        @pl.when(n < nb)
        def _():
            off = jnp.where(n > 0, 0.0, NEG)

            def scores(h):
                hs = slice(HD * h, HD * h + HD)
                kh = jnp.concatenate([kp_ref[hs, :], kc_ref[hs, :]], axis=1)
                vh = jnp.concatenate([vp_ref[hs, :], vc_ref[hs, :]], axis=1)
                s2_scr[h % 2] = _dot_tn(kh, _heads_side_by_side(q_ref, h))
                dp2_scr[h % 2] = _dot_tn(vh, _heads_side_by_side(do_ref, h))

            def softmax_bwd(h):
                for g in range(GROUP):
                    ls = slice(BLK * g, BLK * g + BLK)
                    from_prev, pn, ps = _band_softmax(s2_scr.at[h % 2], ls, off, sink_ref[h:h + 1, ls])
                    dp = jnp.where(from_prev, dp2_scr[h % 2, 0:BLK, ls], dp2_scr[h % 2, BLK:2 * BLK, ls])
                    dsum = jnp.sum(pn * dp, axis=0, keepdims=True)
                    ds_ref[h:h + 1, ls] += -ps * dsum
                    _put_split(pn2_scr.at[h % 2], ls, pn, from_prev)
                    _put_split(ds2_scr.at[h % 2], ls, pn * (dp - dsum), from_prev)

            def grads(h):
                hs = slice(HD * h, HD * h + HD)
                kh = jnp.concatenate([kp_ref[hs, :], kc_ref[hs, :]], axis=1)
                dqg = _dot(kh, ds2_scr[h % 2])
                for g in range(GROUP):
                    a = GROUP * h + g
                    dq_ref[HD * a:HD * a + HD, :] = dqg[:, BLK * g:BLK * g + BLK]
                dkh = _dot_nt(_heads_side_by_side(q_ref, h), ds2_scr[h % 2])
                dvh = _dot_nt(_heads_side_by_side(do_ref, h), pn2_scr[h % 2])
                dk_ref[hs, :] = dk_hold[hs, :] + dkh[:, 0:BLK]
                dv_ref[hs, :] = dv_hold[hs, :] + dvh[:, 0:BLK]
                dk_hold[hs, :] = dkh[:, BLK:2 * BLK]
                dv_hold[hs, :] = dvh[:, BLK:2 * BLK]

            scores(0)
            for h in range(NKV):
                if h + 1 < NKV:
                    scores(h + 1)
                softmax_bwd(h)
                grads(h)

        @pl.when(n == nb)
        def _():
            dk_ref[...] = dk_hold[...]
            dv_ref[...] = dv_hold[...]
            finish()

    blk = pl.BlockSpec((D, BLK), lambda n: (0, jnp.minimum(n, nb - 1)))
    late = pl.BlockSpec((KVW, BLK), lambda n: (0, jnp.maximum(n - 1, 0)))
    whole = pl.BlockSpec((NKV, GROUP * BLK), lambda n: (0, 0))
    kv = [pl.BlockSpec((KVW, BLK), lambda n: (0, jnp.clip(n - 1, 0, nb - 1))),
          pl.BlockSpec((KVW, BLK), lambda n: (0, jnp.minimum(n, nb - 1)))]
    return pl.pallas_call(
        body, name="attn_bwd", grid=(nb + 1,),
        in_specs=[whole, blk, blk] + kv + kv + [HBM],
        out_specs=[blk, late, late, whole, HBM],
        out_shape=[pltpu.HBM((D, s), f32), pltpu.HBM((KVW, s), f32), pltpu.HBM((KVW, s), f32),
                   jax.ShapeDtypeStruct((NKV, GROUP * BLK), f32), pltpu.HBM((NDEV, WO_ROWS, D), bf16)],
        scratch_shapes=[pltpu.VMEM((KVW, BLK), f32), pltpu.VMEM((KVW, BLK), f32)]
        + [pltpu.VMEM((2, 2 * BLK, GROUP * BLK), f32)] * 2 + [pltpu.VMEM((2, 2 * BLK, GROUP * BLK), bf16)] * 2
        + _comm_sems(1),
        compiler_params=_params(("arbitrary",), 48),
    )(_sink_rows(sinks), *_in_hbm(qt, dot, kt, kt, vt, vt, dwo))


def _block_diag(w):
    w4 = w.reshape(NGRP, 4, HD, HD)
    eye = jnp.eye(4, dtype=w.dtype)
    return jnp.einsum('gjcd,jk->gjckd', w4, eye).reshape(NGRP, 256, 256).astype(bf16)


def _gate_terms(pr, pi, br, bi, sp):
    r = _sigmoid(pr + br)
    i = _sigmoid(pi + bi)
    la = -LRU_C * r * sp
    a = jnp.exp(la)
    x2 = 2.0 * la
    y = jnp.where(x2 > -0.02, -x2 * (1.0 + x2 * (0.5 + x2 * (1.0 / 6.0))), 1.0 - a * a)
    inv_mult = lax.rsqrt(jnp.maximum(y, 1e-30))
    return r, i, a, y * inv_mult, inv_mult


def _later(x, before, k):
    if k == 0:
        return x
    row = lax.broadcasted_iota(jnp.int32, before.shape, 0)
    rolled = pltpu.roll(x, k, 0)
    first = jnp.where(row < k, pltpu.roll(before, k, 0), rolled[0:8])
    return jnp.concatenate([first, rolled[8:]], axis=0)


def _earlier(x, after, k):
    if k == 0:
        return x
    n = x.shape[0]
    row = lax.broadcasted_iota(jnp.int32, after.shape, 0)
    rolled = pltpu.roll(x, n - k, 0)
    last = jnp.where(row >= 8 - k, pltpu.roll(after, 8 - k, 0), rolled[n - 8:n])
    return jnp.concatenate([rolled[0:n - 8], last], axis=0)


def _fwd_fused(h, wt, tabs, wo_shard, conv_w, conv_b, wr, wi, br, bi, lam, tm):
    s = h.shape[0]
    nt = s // tm
    nc = 512
    pieces = 8
    rows_per = tm // pieces
    later_chunks = (0, 1, 2, 3, 4, 7, 8)

    def body(h_ref, wt_ref, tab_ref, wo_ref, cw_ref, cb_ref, wr_ref, wi_ref, br_ref,
             bi_ref, lam_ref, q_ref, k_ref, v_ref, ga_ref, xl_ref, gl_ref, u_ref, hl_ref, r_ref, ig_ref, a_ref,
             im_ref, wo_all, wo_stage, halo, ub_scr, pr_scr, pi_scr, b_scr, hcar,
             send_sems, recv_sems, local_sems):
        i = pl.program_id(0)
        start, finish = _gather_ops([wo_stage], [wo_all], send_sems, recv_sems, local_sems)

        @pl.when(i == 0)
        def _():
            wo_stage[...] = wo_ref[...].astype(bf16)
            start()
            halo[...] = jnp.zeros_like(halo)
            hcar[...] = jnp.zeros_like(hcar)

        sp = _softplus(-lam_ref[...])
        br, bi = br_ref[...], bi_ref[...]
        c, sa, sb = _tables(tab_ref)
        piece_rows = lambda p: slice(rows_per * p, rows_per * p + rows_per)

        def project(ci):
            z = _dot_nt(h_ref[...], wt_ref[ci * nc:(ci + 1) * nc, :])
            if ci < 2:
                for j in range(nc // 128):
                    r = _rope(z[:, 128 * j:128 * j + 128], c, sa, sb) * (HD ** -0.5)
                    q_ref[ci * nc + 128 * j:ci * nc + 128 * j + 128, :] = r.astype(bf16).T
            elif ci == 2:
                for j in range(2):
                    js = slice(128 * j, 128 * j + 128)
                    k_ref[js, :] = _rope(z[:, js], c, sa, sb).astype(bf16).T
                    v_ref[js, :] = z[:, KVW + 128 * j:KVW + 128 * j + 128].astype(bf16).T
            else:
                sec, j = divmod(ci - 3, 2)
                (ga_ref, xl_ref, gl_ref)[sec][:, j * nc:(j + 1) * nc] = z

        def gate_terms(p):
            rows = piece_rows(p)
            r, ig, a, mult, inv_mult = _gate_terms(pr_scr[rows, :], pi_scr[rows, :], br, bi, sp)
            r_ref[rows, :] = r
            ig_ref[rows, :] = ig
            a_ref[rows, :] = a
            im_ref[rows, :] = inv_mult
            b_scr[rows, :] = mult * (ig * u_ref[rows, :])

        def scan(p, hc):
            for t in range(rows_per * p, rows_per * p + rows_per):
                hc = a_ref[t:t + 1, :] * hc + b_scr[t:t + 1, :]
                hl_ref[t:t + 1, :] = hc
            return hc

        project(5)
        project(6)
        xl = xl_ref[...]
        u = cb_ref[...] + sum(cw_ref[k:k + 1, :] * _later(xl, halo[...], CONVW - 1 - k) for k in range(CONVW))
        halo[...] = xl[tm - 8:tm, :]
        u_ref[...] = u
        ub_scr[...] = u.astype(bf16)
        for g in range(NGRP):
            gs = slice(256 * g, 256 * g + 256)
            pr_scr[:, gs] = _dot(ub_scr[:, gs], wr_ref[g])
            pi_scr[:, gs] = _dot(ub_scr[:, gs], wi_ref[g])
        hc = hcar[...]
        gate_terms(0)
        for slot, ci in enumerate(later_chunks):
            project(ci)
            gate_terms(slot + 1)
            hc = scan(slot, hc)
        hcar[...] = scan(pieces - 1, hc)

        @pl.when(i == nt - 1)
        def _():
            finish()

    row = lambda w: pl.BlockSpec((tm, w), lambda i: (i, 0))
    col = lambda w: pl.BlockSpec((w, tm), lambda i: (0, i))
    full = lambda a: pl.BlockSpec(a.shape, lambda i: (0,) * a.ndim)
    big = lambda w, dt: pltpu.HBM((s, w), dt)
    tile = pltpu.VMEM((tm, LW), f32)
    return pl.pallas_call(
        body, name="fwd_fused", grid=(nt,),
        in_specs=[row(D), full(wt), row(384), full(wo_shard), full(conv_w), full(conv_b),
                  full(wr), full(wi), full(br), full(bi), full(lam)],
        out_specs=[col(D), col(KVW), col(KVW), row(D), row(D), row(D)] + [row(LW)] * 6 + [HBM],
        out_shape=[pltpu.HBM((D, s), bf16), pltpu.HBM((KVW, s), bf16), pltpu.HBM((KVW, s), bf16),
                   big(D, f32), big(D, f32), big(D, f32)] + [big(LW, f32)] * 6 + [pltpu.HBM((2 * D, D), bf16)],
        scratch_shapes=[pltpu.VMEM((WO_ROWS, D), bf16), pltpu.VMEM((8, LW), f32),
                        pltpu.VMEM((tm, LW), bf16), tile, tile, tile, pltpu.VMEM((1, LW), f32)] + _comm_sems(1),
        compiler_params=_params(("arbitrary",), 56),
    )(*_in_hbm(h, wt), tabs, wo_shard, conv_w, conv_b, wr, wi, br, bi, lam)


def _lru_bwd(u, hl, dhl, xl, r, ig, a, im, conv_w, wr, wi, lam, tm):
    s = u.shape[0]
    nt = s // tm
    pieces = 8
    rows_per = tm // pieces

    def body(u_ref, h_ref, hp_ref, dh_ref, x_ref, r_ref, ig_ref, a_ref, im_ref, cw_ref, wr_ref, wi_ref,
             lam_ref, dxl_ref, dwr_ref, dwi_ref, dbr_ref, dbi_ref, dlam_ref, dcb_ref, dcw_ref,
             l_scr, du_scr, dpr_scr, dpi_scr, lcar, dunext):
        t0 = pl.program_id(0)
        tile = nt - 1 - t0

        @pl.when(t0 == 0)
        def _():
            lcar[...] = jnp.zeros_like(lcar)
            dunext[...] = jnp.zeros_like(dunext)
            for ref in (dwr_ref, dwi_ref, dbr_ref, dbi_ref, dlam_ref, dcb_ref, dcw_ref):
                ref[...] = jnp.zeros_like(ref)

        lam = lam_ref[...]
        sp = _softplus(-lam)
        hp = jnp.where(tile > 0, hp_ref[...], 0.0)

        def scan(p, c):
            for t in range(rows_per * p + rows_per - 1, rows_per * p - 1, -1):
                lt = dh_ref[t:t + 1, :] + c
                l_scr[t:t + 1, :] = lt
                c = a_ref[t:t + 1, :] * lt
            return c

        def terms(p, sums):
            rows = slice(rows_per * p, rows_per * p + rows_per)
            lt, u, r, i, a, inv_mult = l_scr[rows, :], u_ref[rows, :], r_ref[rows, :], ig_ref[rows, :], \
                a_ref[rows, :], im_ref[rows, :]
            before = hp if p == 0 else h_ref[rows_per * p - 8:rows_per * p, :]
            hprev = _later(h_ref[rows, :], before, 1)
            x2 = -2.0 * LRU_C * r * sp
            mult = jnp.where(x2 > -0.02, -x2 * (1.0 + x2 * (0.5 + x2 * (1.0 / 6.0))), 1.0 - a * a) * inv_mult
            da = lt * hprev
            dmult = lt * (i * u)
            di = lt * mult * u
            du_scr[rows, :] = lt * mult * i
            dla = da * a - dmult * (a * a) * inv_mult
            dr = dla * (-LRU_C * sp)
            dpr = dr * r * (1.0 - r)
            dpi = di * i * (1.0 - i)
            dpr_scr[rows, :] = dpr.astype(bf16)
            dpi_scr[rows, :] = dpi.astype(bf16)
            col = lambda t: jnp.sum(t, axis=0, keepdims=True)
            return sums[0] + col(dla * (-LRU_C * r)), sums[1] + col(dpr), sums[2] + col(dpi)

        sums = (jnp.zeros((1, LW), f32),) * 3
        c = scan(pieces - 1, lcar[...])
        for p in range(pieces - 1, -1, -1):
            if p > 0:
                c = scan(p - 1, c)
            sums = terms(p, sums)
        lcar[...] = c
        dlam_ref[...] += sums[0]
        dbr_ref[...] += sums[1]
        dbi_ref[...] += sums[2]

        ub = u_ref[...].astype(bf16)
        dug = []
        for g in range(NGRP):
            gs = slice(256 * g, 256 * g + 256)
            dwr_ref[g] += _dot_tn(ub[:, gs], dpr_scr[:, gs])
            dwi_ref[g] += _dot_tn(ub[:, gs], dpi_scr[:, gs])
            dug.append(_dot_nt(dpr_scr[:, gs], wr_ref[g]) + _dot_nt(dpi_scr[:, gs], wi_ref[g]))
        du = du_scr[...] + jnp.concatenate(dug, axis=1)

        dcb_ref[...] += jnp.sum(du, axis=0, keepdims=True)
        x = x_ref[...]
        after = dunext[...]
        dxl = jnp.zeros_like(du)
        for k in range(CONVW):
            e = _earlier(du, after, CONVW - 1 - k)
            dxl = dxl + cw_ref[k:k + 1, :] * e
            dcw_ref[k:k + 1, :] += jnp.sum(e * x, axis=0, keepdims=True)
        dxl_ref[...] = dxl.astype(bf16)
        dunext[...] = du[0:8, :]

        @pl.when(t0 == nt - 1)
        def _():
            dlam_ref[...] = dlam_ref[...] * (-_sigmoid(-lam))

    rev = lambda i: (nt - 1 - i, 0)
    row = pl.BlockSpec((tm, LW), rev)
    prev8 = pl.BlockSpec((8, LW), lambda i: (jnp.maximum((nt - 1 - i) * (tm // 8) - 1, 0), 0))
    full = lambda a: pl.BlockSpec(a.shape, lambda i: (0,) * a.ndim)
    vec = pl.BlockSpec((1, LW), lambda i: (0, 0))
    bd = pl.BlockSpec((NGRP, 256, 256), lambda i: (0, 0, 0))
    return pl.pallas_call(
        body, name="lru_bwd", grid=(nt,),
        in_specs=[row, row, prev8, row, row, row, row, row, row, full(conv_w), full(wr), full(wi), full(lam)],
        out_specs=[row, bd, bd, vec, vec, vec, vec, pl.BlockSpec((CONVW, LW), lambda i: (0, 0))],
        out_shape=[pltpu.HBM((s, LW), bf16),
                   jax.ShapeDtypeStruct((NGRP, 256, 256), f32), jax.ShapeDtypeStruct((NGRP, 256, 256), f32),
                   jax.ShapeDtypeStruct((1, LW), f32), jax.ShapeDtypeStruct((1, LW), f32),
                   jax.ShapeDtypeStruct((1, LW), f32), jax.ShapeDtypeStruct((1, LW), f32),
                   jax.ShapeDtypeStruct((CONVW, LW), f32)],
        scratch_shapes=[pltpu.VMEM((tm, LW), f32), pltpu.VMEM((tm, LW), f32), pltpu.VMEM((tm, LW), bf16),
                        pltpu.VMEM((tm, LW), bf16), pltpu.VMEM((1, LW), f32), pltpu.VMEM((8, LW), f32)],
        compiler_params=_params(("arbitrary",), 56),
    )(*_in_hbm(u, hl, hl, dhl, xl, r, ig, a, im), conv_w, wr, wi, lam)


def _gated_norm(t, gate, gain):
    sg = _sigmoid(gate)
    silu = gate * sg
    p = t * silu
    rstd = lax.rsqrt(jnp.mean(p * p, axis=-1, keepdims=True) + EPS)
    ph = p * rstd
    return sg, silu, rstd, ph, ph * gain


def _gated_norm_bwd(dy, t, gate, gain, sg, silu, rstd, ph):
    w = dy * gain
    dp = rstd * (w - ph * jnp.mean(w * ph, axis=-1, keepdims=True))
    dgate = dp * t * (sg * (1.0 + gate * (1.0 - sg)))
    return jnp.sum(dy * ph, axis=0, keepdims=True), dp * silu, dgate


def _out_fwd_bwd(x, tgt, o, ga, hl, gl, again, lgain, fgain, wo, tm):
    s = x.shape[0]
    nt = s // tm

    def body(x_ref, t_ref, o_ref, ga_ref, hl_ref, gl_ref, ag_ref, lg_ref, fg_ref, wo_ref,
             dx2_ref, do_ref, dga_ref, dhl_ref, dgl_ref, dwo_ref, gfg_ref, gag_ref, glg_ref, loss_ref, acc):
        i = pl.program_id(0)

        @pl.when(i == 0)
        def _():
            acc[...] = jnp.zeros_like(acc)
            for ref in (gfg_ref, gag_ref, glg_ref, loss_ref):
                ref[...] = jnp.zeros_like(ref)

        oo = jnp.concatenate([o_ref[128 * j:128 * j + 128, :].T for j in range(D // 128)], axis=1)
        gga, hh, ggl = ga_ref[...], hl_ref[...], gl_ref[...]
        ag, lg, fg = ag_ref[...], lg_ref[...], fg_ref[...]
        sga, silua, ra, pah, ya = _gated_norm(oo, gga, ag)
        sgl, silul, rl, plh, yl = _gated_norm(hh, ggl, lg)
        yab, ylb = ya.astype(bf16), yl.astype(bf16)
        y = _dot(yab, wo_ref[0:D, :]) + _dot(ylb, wo_ref[D:2 * D, :])
        x2 = x_ref[...] + y
        r2 = lax.rsqrt(jnp.mean(x2 * x2, axis=-1, keepdims=True) + EPS)
        x2h = x2 * r2
        err = x2h * fg - t_ref[...]
        loss_ref[...] += 0.5 * jnp.sum(jnp.sum(err * err, axis=-1, keepdims=True) * (1.0 / D))
        dout = err * (1.0 / D)
        gfg_ref[...] += jnp.sum(dout * x2h, axis=0, keepdims=True)
        w = dout * fg
        dx2 = r2 * (w - x2h * jnp.mean(w * x2h, axis=-1, keepdims=True))
        dx2_ref[...] = dx2
        dyb = dx2.astype(bf16)
        acc[0:D, :] += _dot_tn(yab, dyb)
        acc[D:2 * D, :] += _dot_tn(ylb, dyb)
        dya = _dot_nt(dyb, wo_ref[0:D, :])
        dyl = _dot_nt(dyb, wo_ref[D:2 * D, :])
        gag, do, dga = _gated_norm_bwd(dya, oo, gga, ag, sga, silua, ra, pah)
        glg, dhl, dgl = _gated_norm_bwd(dyl, hh, ggl, lg, sgl, silul, rl, plh)
        gag_ref[...] += gag
        glg_ref[...] += glg
        dob = do.astype(bf16)
        for j in range(D // 128):
            do_ref[128 * j:128 * j + 128, :] = dob[:, 128 * j:128 * j + 128].T
        dga_ref[...] = dga.astype(bf16)
        dhl_ref[...] = dhl
        dgl_ref[...] = dgl.astype(bf16)

        @pl.when(i == nt - 1)
        def _():
            dwo_ref[...] = acc[...].astype(bf16)

    row = pl.BlockSpec((tm, D), lambda i: (i, 0))
    col = pl.BlockSpec((D, tm), lambda i: (0, i))
    vec = pl.BlockSpec((1, D), lambda i: (0, 0))
    mat = pl.BlockSpec((2 * D, D), lambda i: (0, 0))
    return pl.pallas_call(
        body, name="out_fwd_bwd", grid=(nt,),
        in_specs=[row, row, col, row, row, row] + [vec] * 3 + [mat],
        out_specs=[row, col, row, row, row] + [mat, vec, vec, vec, pl.BlockSpec((1, 128), lambda i: (0, 0))],
        out_shape=[pltpu.HBM((s, D), f32), pltpu.HBM((D, s), bf16),
                   pltpu.HBM((s, D), bf16), pltpu.HBM((s, D), f32),
                   pltpu.HBM((s, D), bf16), pltpu.HBM((2 * D, D), bf16),
                   jax.ShapeDtypeStruct((1, D), f32), jax.ShapeDtypeStruct((1, D), f32),
                   jax.ShapeDtypeStruct((1, D), f32), jax.ShapeDtypeStruct((1, 128), f32)],
        scratch_shapes=[pltpu.VMEM((2 * D, D), f32)],
        compiler_params=_params(("arbitrary",), 56),
    )(*_in_hbm(x, tgt, o, ga, hl, gl), again, lgain, fgain, *_in_hbm(wo))


def _bwd_in(x, dx2, dq, dk, dv, dga, dxl, dgl, ln_gain, wt, tabs, tm):
    s = x.shape[0]

    def body(x_ref, dx2_ref, dq_ref, dk_ref, dv_ref, dga_ref, dxl_ref, dgl_ref, g_ref, wt_ref,
             tab_ref, gx_ref, gln_ref, dzt_ref):
        @pl.when(pl.program_id(0) == 0)
        def _():
            gln_ref[...] = jnp.zeros_like(gln_ref)

        c, sa, sb = (t.T for t in _tables(tab_ref))
        for j in range(D // 128):
            js = slice(128 * j, 128 * j + 128)
            dzt_ref[js, :] = (_unrope_t(dq_ref[js, :], c, sa, sb) * (HD ** -0.5)).astype(bf16)
        for j in range(KVW // 128):
            js = slice(128 * j, 128 * j + 128)
            dzt_ref[D + 128 * j:D + 128 * j + 128, :] = _unrope_t(dk_ref[js, :], c, sa, sb).astype(bf16)
        dzt_ref[D + KVW:D + 2 * KVW, :] = dv_ref[...].astype(bf16)
        first = D + 2 * KVW
        dh = _dot_tn(dzt_ref[0:512, :], wt_ref[0:512, :])
        for ci in range(1, first // 512):
            dh = dh + _dot_tn(dzt_ref[512 * ci:512 * ci + 512, :], wt_ref[512 * ci:512 * ci + 512, :])
        for sec, ref in enumerate((dga_ref, dxl_ref, dgl_ref)):
            for j in range(D // 512):
                rows = slice(first + D * sec + 512 * j, first + D * sec + 512 * j + 512)
                dh = dh + _dot(ref[:, 512 * j:512 * j + 512], wt_ref[rows, :])
            for j in range(D // 128):
                dzt_ref[first + D * sec + 128 * j:first + D * sec + 128 * j + 128, :] = ref[:, 128 * j:128 * j + 128].T
        xx = x_ref[...]
        rstd = lax.rsqrt(jnp.mean(xx * xx, axis=-1, keepdims=True) + EPS)
        xh = xx * rstd
        gln_ref[...] += jnp.sum(dh * xh, axis=0, keepdims=True)
        w = dh * g_ref[...]
        gx_ref[...] = dx2_ref[...] + rstd * (w - xh * jnp.mean(w * xh, axis=-1, keepdims=True))

    row = lambda w: pl.BlockSpec((tm, w), lambda i: (i, 0))
    col = lambda w: pl.BlockSpec((w, tm), lambda i: (0, i))
    full = lambda a: pl.BlockSpec(a.shape, lambda i: (0, 0))
    return pl.pallas_call(
        body, name="bwd_in", grid=(s // tm,),
        in_specs=[row(D), row(D), col(D), col(KVW), col(KVW), row(D), row(D), row(D), full(ln_gain), full(wt),
                  row(384)],
        out_specs=[row(D), pl.BlockSpec((1, D), lambda i: (0, 0)), col(NIN)],
        out_shape=[pltpu.HBM((s, D), f32), jax.ShapeDtypeStruct((1, D), f32),
                   pltpu.HBM((NIN, s), bf16)],
        compiler_params=_params(("arbitrary",), 56),
    )(*_in_hbm(x, dx2, dq, dk, dv, dga, dxl, dgl), ln_gain, *_in_hbm(wt), tabs)


WT_TERMS = 4


def _dwt_scatter(dzt, h, small, tm):
    s = h.shape[0]
    nk = s // tm
    srows = small.shape[0] // NDEV
    last = NDEV - 1

    def body(order_ref, dz_ref, h_ref, sm_ref, lwt_ref, lsm_ref, acc, stage, given, relayed, send_sems, recv_sems,
             local_sem, sm_send, sm_recv, sm_local):
        j, k = pl.program_id(0), pl.program_id(1)
        x, y, c = _place()
        sibling = (x, y, 1 - c)
        near = (x ^ (1 - c), y ^ c)
        far = (x ^ c, y ^ (1 - c))
        sm_start, sm_finish = _scatter_ops([sm_ref], [lsm_ref], sm_send, sm_recv, sm_local)

        def send(step):
            if step == last - 1:
                dst, to = lwt_ref.at[1], sibling
            elif step % 2 == 0:
                dst, to = given.at[step // 2], sibling
            elif step == 1:
                dst, to = relayed, (*near, c)
            else:
                dst, to = lwt_ref.at[1 + step // 2], (*(near if step == 3 else far), c)
            return pltpu.make_async_remote_copy(
                src_ref=stage.at[step % 2], dst_ref=dst, send_sem=send_sems.at[step], recv_sem=recv_sems.at[step],
                device_id=to, device_id_type=MESH)

        def keep():
            return pltpu.make_async_copy(stage.at[last % 2], lwt_ref.at[0], local_sem)

        @pl.when((j == 0) & (k == 0))
        def _():
            sm_start()

        @pl.when(k == 0)
        def _():
            acc[...] = jnp.zeros_like(acc)

        acc[...] += _dot(dz_ref[...], h_ref[...])

        for step in range(NDEV):
            @pl.when((k == nk - 1) & (j == step))
            def _(step=step):
                if step >= 2:
                    send(step - 2).wait_send()
                if step % 2 == 1 and step < last:
                    send(step - 1).wait_recv()
                    total = acc[...] + given[step // 2].astype(f32)
                    if step == 5:
                        send(1).wait_recv()
                        total = total + relayed[...].astype(f32)
                    stage[step % 2] = total.astype(bf16)
                else:
                    stage[step % 2] = acc[...].astype(bf16)
                if step < last:
                    send(step).start()
                else:
                    keep().start()
                    send(last - 1).wait_send()
                    for peer_step in (3, 5, last - 1):
                        send(peer_step).wait_recv()
                    keep().wait()
                    sm_finish()

    x, y, c = _place()
    dest = lambda chip, cc: 4 * chip[0] + 2 * chip[1] + cc
    near, far, diag = (x ^ (1 - c), y ^ c), (x ^ c, y ^ (1 - c)), (1 - x, 1 - y)
    order = jnp.stack([dest(diag, 1 - c), dest(diag, c), dest(far, 1 - c), dest(near, c),
                       dest(near, 1 - c), dest(far, c), dest((x, y), 1 - c), dest((x, y), c)])
    return pl.pallas_call(
        body, name="dwt_scatter",
        grid_spec=pltpu.PrefetchScalarGridSpec(
            num_scalar_prefetch=1, grid=(NDEV, nk),
            in_specs=[pl.BlockSpec((WT_ROWS, tm), lambda j, k, order: (order[j], k)),
                      pl.BlockSpec((tm, D), lambda j, k, order: (k, 0)), HBM],
            out_specs=[HBM, HBM],
            scratch_shapes=[pltpu.VMEM((WT_ROWS, D), f32), pltpu.VMEM((2, WT_ROWS, D), bf16),
                            pltpu.VMEM((3, WT_ROWS, D), bf16), pltpu.VMEM((WT_ROWS, D), bf16),
                            pltpu.SemaphoreType.DMA((last,)), pltpu.SemaphoreType.DMA((last,)),
                            pltpu.SemaphoreType.DMA(())] + _comm_sems(1)),
        out_shape=[pltpu.HBM((WT_TERMS, WT_ROWS, D), bf16), pltpu.HBM((NDEV, srows, D), f32)],
        compiler_params=_params(("arbitrary", "arbitrary"), 48),
    )(order, *_in_hbm(dzt, h, small))


def _diag_blocks(bd):
    eye = jnp.eye(4, dtype=bd.dtype)
    return jnp.einsum('gjckd,jk->gjcd', bd.reshape(NGRP, 4, HD, 4, HD), eye).reshape(NQ, HD, HD)


def _sequence_step(x, h, tabs, tgt, wt, wo_shard, conv_w, p):
    s = x.shape[0]
    tm = min(256, s)
    wr, wi = _block_diag(p["w_rgate"]), _block_diag(p["w_igate"])
    sinks = p["sinks"].reshape(NQ)
    qt, kt, vt, ga, xl, gl, u, hl, r, ig, a, im, wo = _fwd_fused(
        h, wt, tabs, wo_shard, conv_w, p["conv_b"], wr, wi, p["b_rgate"], p["b_igate"], p["lru_lambda"], tm)
    ot = _attn_fwd_t(qt, kt, vt, sinks)
    dx2, dot, dga, dhl, dgl, dwo, g_fg, g_ag, g_lg, loss = _out_fwd_bwd(
        x, tgt, ot, ga, hl, gl, p["attn_out_gain"], p["lru_out_gain"], p["final_gain"], wo, tm)
    dqt, dkt, dvt, dsink, land_wo = _attn_bwd_t(qt, kt, vt, dot, sinks, dwo)
    dxl, dwr, dwi, dbr, dbi, dlam, dcb, dcw = _lru_bwd(u, hl, dhl, xl, r, ig, a, im, conv_w, wr, wi, p["lru_lambda"], tm)
    gx, g_ln, dzt = _bwd_in(x, dx2, dqt, dkt, dvt, dga, dxl, dgl, p["ln_gain"], wt, tabs, tm)
    small = dict(ln_gain=g_ln, sinks=dsink.reshape(NQ, BLK).sum(axis=1)[None], conv_w=dcw, conv_b=dcb,
                 w_rgate=_diag_blocks(dwr), b_rgate=dbr, w_igate=_diag_blocks(dwi), b_igate=dbi, lru_lambda=dlam,
                 attn_out_gain=g_ag, lru_out_gain=g_lg, final_gain=g_fg)
    land_wt, land_sm = _dwt_scatter(dzt, h, _pack_small(small, loss), min(2048, s))
    return gx, land_wt, land_wo, land_sm


def _all_gather(srcs, out_dtypes, name):
    n = len(srcs)
    cast = [a.dtype != dt for a, dt in zip(srcs, out_dtypes)]

    def body(*refs):
        src_refs, out_refs = refs[:n], refs[n:2 * n]
        stage_refs = list(refs[2 * n:2 * n + sum(cast)])
        mine_refs = []
        for a in range(n):
            if cast[a]:
                st = stage_refs.pop(0)
                st[...] = src_refs[a][...].astype(out_dtypes[a])
                mine_refs.append(st)
            else:
                mine_refs.append(src_refs[a])
        start, finish = _gather_ops(mine_refs, out_refs, *refs[-3:])
        start()
        finish()

    vmem = pl.BlockSpec(memory_space=pltpu.VMEM)
    return pl.pallas_call(
        body, name=name,
        in_specs=[vmem] * n, out_specs=[HBM] * n,
        out_shape=[pltpu.HBM((NDEV * a.shape[0], a.shape[1]), dt) for a, dt in zip(srcs, out_dtypes)],
        scratch_shapes=[pltpu.VMEM(a.shape, dt) for a, dt, cst in zip(srcs, out_dtypes, cast) if cst] + _comm_sems(n),
        compiler_params=pltpu.CompilerParams(vmem_limit_bytes=32 * MIB),
    )(*srcs)


def _gather_weights(wt_shard, conv_blk, x, ln_gain, tm):
    s = x.shape[0]
    d = jnp.arange(128) % HD
    inv_freq = THETA ** (-jnp.arange(0, ROT, 2, dtype=f32) / ROT)
    lane_freq = jnp.where(d < ROT, inv_freq[d % (ROT // 2)], 0.0)[None, :]

    def body(wt_ref, cw_ref, g_ref, f_ref, x_ref, wt_all, cw_all, h_ref, tab_ref, stage, xbuf, hbuf, tbuf,
             send_sems, recv_sems, local_sems):
        stage[...] = wt_ref[...].astype(bf16)
        start, finish = _relay_gather_ops([stage, cw_ref], [wt_all, cw_all], send_sems, recv_sems, local_sems)
        start()
        gain = g_ref[...]
        dd = lax.broadcasted_iota(jnp.int32, (tm, 128), 1) % HD
        for i in range(s // tm):
            rows = pl.ds(i * tm, tm)
            pltpu.sync_copy(x_ref.at[rows, :], xbuf)
            xx = xbuf[...]
            rstd = lax.rsqrt(jnp.mean(xx * xx, axis=-1, keepdims=True) + EPS)
            hbuf[...] = (xx * rstd * gain).astype(bf16)
            pltpu.sync_copy(hbuf, h_ref.at[rows, :])
            pos = (lax.broadcasted_iota(jnp.int32, (tm, 128), 0) + i * tm).astype(f32)
            ang = pos * f_ref[...]
            cos, sin = jnp.cos(ang), jnp.sin(ang)
            tbuf[:, 0:128] = jnp.where(dd < ROT, cos, 1.0)
            tbuf[:, 128:256] = jnp.where((dd >= ROT // 2) & (dd < ROT), sin, 0.0)
            tbuf[:, 256:384] = jnp.where(dd < ROT // 2, -sin, 0.0)
            pltpu.sync_copy(tbuf, tab_ref.at[rows, :])
        finish()

    vmem = pl.BlockSpec(memory_space=pltpu.VMEM)
    return pl.pallas_call(
        body, name="gather_weights",
        in_specs=[vmem, vmem, vmem, vmem, HBM], out_specs=[HBM, HBM, HBM, HBM],
        out_shape=[pltpu.HBM((NIN, D), bf16), pltpu.HBM((NDEV * 8, 128), f32), pltpu.HBM((s, D), bf16),
                   pltpu.HBM((s, 384), f32)],
        scratch_shapes=[pltpu.VMEM((WT_ROWS, D), bf16), pltpu.VMEM((tm, D), f32), pltpu.VMEM((tm, D), bf16),
                        pltpu.VMEM((tm, 384), f32)] + _comm_sems(2),
        compiler_params=pltpu.CompilerParams(vmem_limit_bytes=32 * MIB),
    )(wt_shard, conv_blk, ln_gain, lane_freq, *_in_hbm(x))


def _sum_slots(land, tr, name):
    terms, rows, cols = land.shape

    def body(l_ref, o_ref):
        acc = l_ref[0].astype(f32)
        for d in range(1, terms):
            acc = acc + l_ref[d].astype(f32)
        o_ref[...] = acc

    return pl.pallas_call(
        body, name=name, grid=(rows // tr,),
        in_specs=[pl.BlockSpec((terms, tr, cols), lambda i: (0, i, 0))],
        out_specs=pl.BlockSpec((tr, cols), lambda i: (i, 0)),
        out_shape=jax.ShapeDtypeStruct((rows, cols), f32),
        compiler_params=_params(("arbitrary",), 32),
    )(*_in_hbm(land))


def _adam_math(w, g, m, v):
    m2 = ADAM_B1 * m + (1.0 - ADAM_B1) * g
    v2 = ADAM_B2 * v + (1.0 - ADAM_B2) * (g * g)
    m_hat = m2 / (1.0 - ADAM_B1 ** ADAM_STEP)
    v_hat = v2 / (1.0 - ADAM_B2 ** ADAM_STEP)
    delta = -ADAM_LR * (m_hat / (jnp.sqrt(v_hat) + ADAM_EPS) + ADAM_WD * w)
    return delta, m2, v2


def _reduce_adamw(land, w, m, v, tr, name):
    terms, rows, cols = land.shape

    def body(l_ref, w_ref, m_ref, v_ref, g_ref, d_ref, m2_ref, v2_ref):
        g = l_ref[0].astype(f32)
        for t in range(1, terms):
            g = g + l_ref[t].astype(f32)
        g_ref[...] = g
        d_ref[...], m2_ref[...], v2_ref[...] = _adam_math(w_ref[...], g, m_ref[...], v_ref[...])

    blk = pl.BlockSpec((tr, cols), lambda i: (i, 0))
    return pl.pallas_call(
        body, name=name, grid=(rows // tr,),
        in_specs=[pl.BlockSpec((terms, tr, cols), lambda i: (0, i, 0))] + [blk] * 3, out_specs=[blk] * 4,
        out_shape=[jax.ShapeDtypeStruct((rows, cols), f32)] * 4,
        compiler_params=_params(("arbitrary",), 32),
    )(*_in_hbm(land), w, m, v)


VEC_NAMES = ("ln_gain", "conv_b", "b_rgate", "b_igate", "lru_lambda", "attn_out_gain", "lru_out_gain", "final_gain")
ROW_RGATE, ROW_IGATE, ROW_VEC, ROW_SINKS = 0, 64, 128, 136
LOSS_LANE = NQ


def _adamw_small(g_rep, g_conv, w, m, v):
    names = list(VEC_NAMES) + ["sinks", "conv_w", "w_rgate", "w_igate"]
    ins = [g_rep, g_conv] + [d[k] for k in names for d in (w, m, v)]

    def body(*refs):
        g_ref, gc_ref = refs[0], refs[1]
        in_refs = refs[2:2 + 3 * len(names)]
        out_refs = refs[2 + 3 * len(names):]

        def update(j, g, at=None):
            w_ref, m_ref, v_ref = in_refs[3 * j:3 * j + 3]
            outs = out_refs[4 * j:4 * j + 4]
            pick = (lambda r: r[...]) if at is None else (lambda r: r[at])
            res = (g,) + _adam_math(pick(w_ref), g, pick(m_ref), pick(v_ref))
            for o_ref, val in zip(outs, res):
                if at is None:
                    o_ref[...] = val
                else:
                    o_ref[at] = val

        for j in range(len(VEC_NAMES)):
            update(j, g_ref[ROW_VEC + j:ROW_VEC + j + 1, :])
        update(len(VEC_NAMES), g_ref[ROW_SINKS:ROW_SINKS + 1, 0:NQ])
        update(len(VEC_NAMES) + 1, gc_ref[...], at=0)
        for gi, row0 in ((len(VEC_NAMES) + 2, ROW_RGATE), (len(VEC_NAMES) + 3, ROW_IGATE)):
            for nb in range(NQ):
                update(gi, g_ref[row0:row0 + HD, HD * nb:HD * nb + HD], at=(0, nb))

    vmem = pl.BlockSpec(memory_space=pltpu.VMEM)
    out_shape = [jax.ShapeDtypeStruct(w[k].shape, f32) for k in names for _ in range(4)]
    outs = pl.pallas_call(
        body, name="adamw_small",
        in_specs=[vmem] * len(ins), out_specs=[vmem] * len(out_shape), out_shape=out_shape,
        compiler_params=pltpu.CompilerParams(vmem_limit_bytes=32 * MIB),
    )(*ins)
    return {k: tuple(outs[4 * j:4 * j + 4]) for j, k in enumerate(names)}


def _pack_small(small, loss):
    gate = lambda g: g.transpose(1, 0, 2).reshape(HD, NQ * HD)
    row_s = jnp.concatenate([small["sinks"], loss[:, LOSS_LANE:128], jnp.zeros((1, D - 128), f32)], axis=1)
    rep = jnp.concatenate([gate(small["w_rgate"]), gate(small["w_igate"])] + [small[k] for k in VEC_NAMES]
                          + [row_s, jnp.zeros((SMALL_ROWS - ROW_SINKS - 1, D), f32)], axis=0)
    conv = small["conv_w"].reshape(CONVW, NDEV, 128).transpose(1, 0, 2)
    conv = jnp.pad(conv, ((0, 0), (0, 8 - CONVW), (0, D - 128)))
    return jnp.concatenate([rep.reshape(NDEV, SMALL_PER, D), conv], axis=1).reshape(NDEV * (SMALL_PER + 8), D)


def kernel(x, ln_gain, w_in, sinks, conv_w, conv_b, w_rgate, b_rgate, w_igate, b_igate, lru_lambda, attn_out_gain, lru_out_gain, w_out, final_gain, loss_target, m_ln_gain, m_w_in, m_sinks, m_conv_w, m_conv_b, m_w_rgate, m_b_rgate, m_w_igate, m_b_igate, m_lru_lambda, m_attn_out_gain, m_lru_out_gain, m_w_out, m_final_gain, v_ln_gain, v_w_in, v_sinks, v_conv_w, v_conv_b, v_w_rgate, v_b_rgate, v_w_igate, v_b_igate, v_lru_lambda, v_attn_out_gain, v_lru_out_gain, v_w_out, v_final_gain):
    w = dict(ln_gain=ln_gain, sinks=sinks, conv_w=conv_w, conv_b=conv_b, w_rgate=w_rgate, b_rgate=b_rgate,
             w_igate=w_igate, b_igate=b_igate, lru_lambda=lru_lambda, attn_out_gain=attn_out_gain,
             lru_out_gain=lru_out_gain, final_gain=final_gain.reshape(1, D))
    m = dict(ln_gain=m_ln_gain, sinks=m_sinks, conv_w=m_conv_w, conv_b=m_conv_b, w_rgate=m_w_rgate,
             b_rgate=m_b_rgate, w_igate=m_w_igate, b_igate=m_b_igate, lru_lambda=m_lru_lambda,
             attn_out_gain=m_attn_out_gain, lru_out_gain=m_lru_out_gain, final_gain=m_final_gain.reshape(1, D))
    v = dict(ln_gain=v_ln_gain, sinks=v_sinks, conv_w=v_conv_w, conv_b=v_conv_b, w_rgate=v_w_rgate,
             b_rgate=v_b_rgate, w_igate=v_w_igate, b_igate=v_b_igate, lru_lambda=v_lru_lambda,
             attn_out_gain=v_attn_out_gain, lru_out_gain=v_lru_out_gain, final_gain=v_final_gain.reshape(1, D))

    conv_blk = jnp.pad(conv_w[0], ((0, 8 - CONVW), (0, 0)))
    wt, cw_all, h, tabs = _gather_weights(w_in[0].T, conv_blk, x[0], ln_gain, min(512, x.shape[1]))
    conv_full = cw_all.reshape(NDEV, 8, 128)[:, 0:CONVW].transpose(1, 0, 2).reshape(CONVW, LW)

    p = {k: (w[k][0] if k in ("w_rgate", "w_igate") else w[k]) for k in w if k != "conv_w"}
    gx, land_wt, land_wo, land_sm = _sequence_step(x[0], h, tabs, loss_target[0], wt, w_out[0], conv_full, p)

    g_sm = _sum_slots(land_sm, SMALL_PER + 8, "sum_small")
    (g_rep,) = _all_gather([g_sm[0:SMALL_PER]], [f32], "gather_small")
    g_conv = g_sm[SMALL_PER:SMALL_PER + CONVW, 0:128]

    wins = _reduce_adamw(land_wt, w_in[0].T, m_w_in[0].T, v_w_in[0].T, 192, "adamw_w_in")
    g_win, d_win, m_win, v_win = (t.T for t in wins)
    g_wo, d_wo, m_wo, v_wo = _reduce_adamw(land_wo, w_out[0], m_w_out[0], v_w_out[0], 256, "adamw_w_out")
    res = _adamw_small(g_rep, g_conv, w, m, v)
    res["w_in"] = tuple(t[None] for t in (g_win, d_win, m_win, v_win))
    res["w_out"] = tuple(t[None] for t in (g_wo, d_wo, m_wo, v_wo))
    res["final_gain"] = tuple(t.reshape(D) for t in res["final_gain"])

    order = ("ln_gain", "w_in", "sinks", "conv_w", "conv_b", "w_rgate", "b_rgate", "w_igate", "b_igate",
             "lru_lambda", "attn_out_gain", "lru_out_gain", "w_out", "final_gain")
    total_loss = g_rep[ROW_SINKS, LOSS_LANE]
    return (total_loss, gx[None]) + tuple(res[k][i] for i in range(4) for k in order)
```

```python
import jax
import jax.numpy as jnp
from jax import lax
from jax.experimental import pallas as pl
from jax.experimental.pallas import tpu as pltpu

f32 = jnp.float32
bf16 = jnp.bfloat16

D = 1024
HD = 64
NQ = 16
NKV = 4
GROUP = NQ // NKV
KVW = NKV * HD
BLK = 128
ROT = 16
THETA = 500000.0
NEG = -1e30
LW = 1024
NGRP = 4
CONVW = 4
LRU_C = 8.0
NIN = 4608
EPS = 1e-6
NDEV = 8
WT_ROWS = NIN // NDEV
WO_ROWS = 2 * D // NDEV
SMALL_ROWS = 192
SMALL_PER = SMALL_ROWS // NDEV

ADAM_LR = 0.001
ADAM_B1 = 0.9
ADAM_B2 = 0.999
ADAM_EPS = 1e-08
ADAM_WD = 0.01
ADAM_STEP = 10

NT = (((1,), (1,)), ((), ()))
TN = (((0,), (0,)), ((), ()))
MESH = pl.DeviceIdType.MESH
MIB = 1024 * 1024


def _dot(a, b):
    return jnp.dot(a, b, preferred_element_type=f32)


def _dot_nt(a, b):
    return lax.dot_general(a, b, NT, preferred_element_type=f32)


def _dot_tn(a, b):
    return lax.dot_general(a, b, TN, preferred_element_type=f32)


def _params(sem, vmem_mib):
    return pltpu.CompilerParams(dimension_semantics=sem, vmem_limit_bytes=vmem_mib * MIB)


def _sigmoid(x):
    return 0.5 * jnp.tanh(0.5 * x) + 0.5


def _softplus(x):
    return jnp.maximum(x, 0.0) + jnp.log(1.0 + jnp.exp(-jnp.abs(x)))


def _tables(tab_ref):
    return tab_ref[:, 0:128], tab_ref[:, 128:256], tab_ref[:, 256:384]


def _rope(t, c, sa, sb):
    return t * c + pltpu.roll(t, 8, 1) * sa + pltpu.roll(t, 120, 1) * sb


def _unrope_t(dr, c, sa, sb):
    return dr * c + pltpu.roll(dr * sa, 120, 0) + pltpu.roll(dr * sb, 8, 0)


def _place():
    return lax.axis_index("x"), lax.axis_index("y"), lax.axis_index("c")


def _gather_ops(mine_refs, out_refs, send_sems, recv_sems, local_sems):
    n = len(mine_refs)
    x, y, c = _place()
    me, sibling = (x, y, c), (x, y, 1 - c)
    chips = [(1 - x, y), (x, 1 - y), (1 - x, 1 - y)]

    def rows(a, dev):
        m = mine_refs[a].shape[0]
        return out_refs[a].at[pl.ds((4 * dev[0] + 2 * dev[1] + dev[2]) * m, m), :]

    def copy(a, k, block, to, own=False):
        return pltpu.make_async_remote_copy(
            src_ref=mine_refs[a] if own else rows(a, block), dst_ref=rows(a, block),
            send_sem=send_sems.at[a, k], recv_sem=recv_sems.at[a, k], device_id=to, device_id_type=MESH)

    def local(a):
        return pltpu.make_async_copy(mine_refs[a], rows(a, me), local_sems.at[a])

    def first(a):
        return [copy(a, 0, me, sibling, own=True)] + [copy(a, 1 + j, me, (*chip, c), own=True)
                                                      for j, chip in enumerate(chips)]

    def start():
        for a in range(n):
            local(a).start()
            for cp in first(a):
                cp.start()

    def finish():
        for j, chip in enumerate(chips):
            for a in range(n):
                copy(a, 1 + j, (*chip, c), me).wait_recv()
                copy(a, 4 + j, (*chip, c), sibling).start()
        for a in range(n):
            copy(a, 0, sibling, me).wait_recv()
            for j, chip in enumerate(chips):
                copy(a, 4 + j, (*chip, 1 - c), me).wait_recv()
        for a in range(n):
            for cp in first(a) + [copy(a, 4 + j, (*chip, c), sibling) for j, chip in enumerate(chips)]:
                cp.wait_send()
            local(a).wait()

    return start, finish


def _relay_gather_ops(mine_refs, out_refs, send_sems, recv_sems, local_sems):
    n = len(mine_refs)
    x, y, c = _place()
    me, sibling = (x, y, c), (x, y, 1 - c)
    near = (x ^ (1 - c), y ^ c)
    far = (x ^ c, y ^ (1 - c))
    diag = (1 - x, 1 - y)

    def rows(a, dev):
        m = mine_refs[a].shape[0]
        return out_refs[a].at[pl.ds((4 * dev[0] + 2 * dev[1] + dev[2]) * m, m), :]

    def copy(a, k, block, to, own=False):
        return pltpu.make_async_remote_copy(
            src_ref=mine_refs[a] if own else rows(a, block), dst_ref=rows(a, block),
            send_sem=send_sems.at[a, k], recv_sem=recv_sems.at[a, k], device_id=to, device_id_type=MESH)

    def local(a):
        return pltpu.make_async_copy(mine_refs[a], rows(a, me), local_sems.at[a])

    def sends(a):
        return [copy(a, 0, me, sibling, own=True), copy(a, 1, me, (*near, c), own=True),
                copy(a, 2, me, (*far, c), own=True), copy(a, 3, (*near, c), (*far, c)),
                copy(a, 4, (*near, c), sibling), copy(a, 5, (*far, c), sibling), copy(a, 6, (*diag, c), sibling)]

    def arrivals(a):
        return [copy(a, 0, sibling, me), copy(a, 1, (*near, c), me), copy(a, 2, (*far, c), me),
                copy(a, 3, (*diag, c), me), copy(a, 4, (*far, 1 - c), me), copy(a, 5, (*near, 1 - c), me),
                copy(a, 6, (*diag, 1 - c), me)]

    def start():
        for a in range(n):
            local(a).start()
            for cp in sends(a)[0:3]:
                cp.start()

    def relay():
        for first, then in ((1, (3, 4)), (2, (5,)), (3, (6,))):
            for a in range(n):
                arrivals(a)[first].wait_recv()
                for k in then:
                    sends(a)[k].start()

    def finish():
        for a in range(n):
            for k in (0, 4, 5, 6):
                arrivals(a)[k].wait_recv()
        for a in range(n):
            for cp in sends(a):
                cp.wait_send()
            local(a).wait()

    return start, relay, finish


def _scatter_ops(src_refs, land_refs, send_sems, recv_sems, local_sems):
    n = len(src_refs)
    x, y, c = _place()
    my = 4 * x + 2 * y + c

    def peer(k):
        return x ^ (k >> 2), y ^ ((k >> 1) & 1), c ^ (k & 1)

    def piece(a, dev):
        m = src_refs[a].shape[0] // NDEV
        return src_refs[a].at[pl.ds(dev * m, m), :]

    def local(a):
        return pltpu.make_async_copy(piece(a, my), land_refs[a].at[my], local_sems.at[a])

    def send(a, k):
        px, py, pc = peer(k)
        return pltpu.make_async_remote_copy(
            src_ref=piece(a, 4 * px + 2 * py + pc), dst_ref=land_refs[a].at[my],
            send_sem=send_sems.at[a, k - 1], recv_sem=recv_sems.at[a, k - 1],
            device_id=(px, py, pc), device_id_type=MESH)

    def arrival(a, k):
        px, py, pc = peer(k)
        return pltpu.make_async_remote_copy(
            src_ref=piece(a, my), dst_ref=land_refs[a].at[4 * px + 2 * py + pc],
            send_sem=send_sems.at[a, k - 1], recv_sem=recv_sems.at[a, k - 1],
            device_id=(px, py, pc), device_id_type=MESH)

    def start():
        for a in range(n):
            local(a).start()
        for k in range(1, NDEV):
            for a in range(n):
                send(a, k).start()

    def finish():
        for k in range(1, NDEV):
            for a in range(n):
                send(a, k).wait_send()
        for k in range(1, NDEV):
            for a in range(n):
                arrival(a, k).wait_recv()
        for a in range(n):
            local(a).wait()

    return start, finish


def _in_hbm(*arrays):
    return tuple(pltpu.with_memory_space_constraint(a, pltpu.HBM) for a in arrays)


def _comm_sems(n):
    return [pltpu.SemaphoreType.DMA((n, 7)), pltpu.SemaphoreType.DMA((n, 7)), pltpu.SemaphoreType.DMA((n,))]


HBM = pl.BlockSpec(memory_space=pltpu.HBM)


def _sink_rows(sinks):
    return jnp.repeat(sinks.reshape(NKV, GROUP), BLK, axis=1)


def _band_softmax(s2_ref, ls, prev_offset, sink_row):
    jj = lax.broadcasted_iota(jnp.int32, (BLK, BLK), 0)
    ii = lax.broadcasted_iota(jnp.int32, (BLK, BLK), 1)
    from_prev = jj > ii
    sc = jnp.where(from_prev, s2_ref[0:BLK, ls] + prev_offset, s2_ref[BLK:2 * BLK, ls])
    m = jnp.maximum(jnp.max(sc, axis=0, keepdims=True), sink_row)
    p = jnp.exp(sc - m)
    es = jnp.exp(sink_row - m)
    inv = 1.0 / (jnp.sum(p, axis=0, keepdims=True) + es)
    return from_prev, p * inv, es * inv


def _put_split(dst_ref, ls, t, from_prev):
    t = t.astype(bf16)
    zero = jnp.zeros_like(t)
    dst_ref[0:BLK, ls] = jnp.where(from_prev, t, zero)
    dst_ref[BLK:2 * BLK, ls] = jnp.where(from_prev, zero, t)


def _heads_side_by_side(ref, h):
    return jnp.concatenate([ref[HD * (GROUP * h + g):HD * (GROUP * h + g) + HD, :] for g in range(GROUP)], axis=1)


def _kv_specs_t():
    prev = pl.BlockSpec((KVW, BLK), lambda n: (0, jnp.maximum(n - 1, 0)))
    cur = pl.BlockSpec((KVW, BLK), lambda n: (0, n))
    return [prev, cur, prev, cur]


def _attn_fwd_t(qt, kt, vt, sinks):
    s = qt.shape[1]

    def body(sink_ref, q_ref, kp_ref, kc_ref, vp_ref, vc_ref, o_ref, s2_scr, pn2_scr):
        n = pl.program_id(0)
        off = jnp.where(n > 0, 0.0, NEG)

        def scores(h):
            hs = slice(HD * h, HD * h + HD)
            kh = jnp.concatenate([kp_ref[hs, :], kc_ref[hs, :]], axis=1)
            s2_scr[h % 2] = _dot_tn(kh, _heads_side_by_side(q_ref, h))

        def probs(h):
            for g in range(GROUP):
                ls = slice(BLK * g, BLK * g + BLK)
                from_prev, pn, _ = _band_softmax(s2_scr.at[h % 2], ls, off, sink_ref[h:h + 1, ls])
                _put_split(pn2_scr.at[h % 2], ls, pn, from_prev)

        def outputs(h):
            hs = slice(HD * h, HD * h + HD)
            vh = jnp.concatenate([vp_ref[hs, :], vc_ref[hs, :]], axis=1)
            og = _dot(vh, pn2_scr[h % 2])
            for g in range(GROUP):
                a = GROUP * h + g
                o_ref[HD * a:HD * a + HD, :] = og[:, BLK * g:BLK * g + BLK]

        scores(0)
        for h in range(NKV):
            if h + 1 < NKV:
                scores(h + 1)
            probs(h)
            outputs(h)

    return pl.pallas_call(
        body, name="attn_fwd", grid=(s // BLK,),
        in_specs=[pl.BlockSpec((NKV, GROUP * BLK), lambda n: (0, 0)), pl.BlockSpec((D, BLK), lambda n: (0, n))]
        + _kv_specs_t(),
        out_specs=pl.BlockSpec((D, BLK), lambda n: (0, n)),
        out_shape=pltpu.HBM((D, s), f32),
        scratch_shapes=[pltpu.VMEM((2, 2 * BLK, GROUP * BLK), f32), pltpu.VMEM((2, 2 * BLK, GROUP * BLK), bf16)],
        compiler_params=_params(("arbitrary",), 32),
    )(_sink_rows(sinks), *_in_hbm(qt, kt, kt, vt, vt))


def _attn_bwd_t(qt, kt, vt, dot, sinks, dwo):
    s = qt.shape[1]
    nb = s // BLK

    def body(sink_ref, q_ref, do_ref, kp_ref, kc_ref, vp_ref, vc_ref, dwo_ref, dq_ref, dk_ref, dv_ref, ds_ref,
             land_ref, dk_hold, dv_hold, s2_scr, dp2_scr, pn2_scr, ds2_scr, send_sems, recv_sems, local_sems):
        n = pl.program_id(0)
        start, finish = _scatter_ops([dwo_ref], [land_ref], send_sems, recv_sems, local_sems)

        @pl.when(n == 0)
        def _():
            start()
            dk_hold[...] = jnp.zeros_like(dk_hold)
            dv_hold[...] = jnp.zeros_like(dv_hold)
            ds_ref[...] = jnp.zeros_like(ds_ref)

        @pl.when(n < nb)
        def _():
            off = jnp.where(n > 0, 0.0, NEG)

            def scores(h):
                hs = slice(HD * h, HD * h + HD)
                kh = jnp.concatenate([kp_ref[hs, :], kc_ref[hs, :]], axis=1)
                vh = jnp.concatenate([vp_ref[hs, :], vc_ref[hs, :]], axis=1)
                s2_scr[h % 2] = _dot_tn(kh, _heads_side_by_side(q_ref, h))
                dp2_scr[h % 2] = _dot_tn(vh, _heads_side_by_side(do_ref, h))

            def softmax_bwd(h):
                for g in range(GROUP):
                    ls = slice(BLK * g, BLK * g + BLK)
                    from_prev, pn, ps = _band_softmax(s2_scr.at[h % 2], ls, off, sink_ref[h:h + 1, ls])
                    dp = jnp.where(from_prev, dp2_scr[h % 2, 0:BLK, ls], dp2_scr[h % 2, BLK:2 * BLK, ls])
                    dsum = jnp.sum(pn * dp, axis=0, keepdims=True)
                    ds_ref[h:h + 1, ls] += -ps * dsum
                    _put_split(pn2_scr.at[h % 2], ls, pn, from_prev)
                    _put_split(ds2_scr.at[h % 2], ls, pn * (dp - dsum), from_prev)

            def grads(h):
                hs = slice(HD * h, HD * h + HD)
                kh = jnp.concatenate([kp_ref[hs, :], kc_ref[hs, :]], axis=1)
                dqg = _dot(kh, ds2_scr[h % 2])
                for g in range(GROUP):
                    a = GROUP * h + g
                    dq_ref[HD * a:HD * a + HD, :] = dqg[:, BLK * g:BLK * g + BLK]
                dkh = _dot_nt(_heads_side_by_side(q_ref, h), ds2_scr[h % 2])
                dvh = _dot_nt(_heads_side_by_side(do_ref, h), pn2_scr[h % 2])
                dk_ref[hs, :] = dk_hold[hs, :] + dkh[:, 0:BLK]
                dv_ref[hs, :] = dv_hold[hs, :] + dvh[:, 0:BLK]
                dk_hold[hs, :] = dkh[:, BLK:2 * BLK]
                dv_hold[hs, :] = dvh[:, BLK:2 * BLK]

            scores(0)
            for h in range(NKV):
                if h + 1 < NKV:
                    scores(h + 1)
                softmax_bwd(h)
                grads(h)

        @pl.when(n == nb)
        def _():
            dk_ref[...] = dk_hold[...]
            dv_ref[...] = dv_hold[...]
            finish()

    blk = pl.BlockSpec((D, BLK), lambda n: (0, jnp.minimum(n, nb - 1)))
    late = pl.BlockSpec((KVW, BLK), lambda n: (0, jnp.maximum(n - 1, 0)))
    whole = pl.BlockSpec((NKV, GROUP * BLK), lambda n: (0, 0))
    kv = [pl.BlockSpec((KVW, BLK), lambda n: (0, jnp.clip(n - 1, 0, nb - 1))),
          pl.BlockSpec((KVW, BLK), lambda n: (0, jnp.minimum(n, nb - 1)))]
    return pl.pallas_call(
        body, name="attn_bwd", grid=(nb + 1,),
        in_specs=[whole, blk, blk] + kv + kv + [HBM],
        out_specs=[blk, late, late, whole, HBM],
        out_shape=[pltpu.HBM((D, s), f32), pltpu.HBM((KVW, s), f32), pltpu.HBM((KVW, s), f32),
                   jax.ShapeDtypeStruct((NKV, GROUP * BLK), f32), pltpu.HBM((NDEV, WO_ROWS, D), bf16)],
        scratch_shapes=[pltpu.VMEM((KVW, BLK), f32), pltpu.VMEM((KVW, BLK), f32)]
        + [pltpu.VMEM((2, 2 * BLK, GROUP * BLK), f32)] * 2 + [pltpu.VMEM((2, 2 * BLK, GROUP * BLK), bf16)] * 2
        + _comm_sems(1),
        compiler_params=_params(("arbitrary",), 48),
    )(_sink_rows(sinks), *_in_hbm(qt, dot, kt, kt, vt, vt, dwo))


def _block_diag(w):
    w4 = w.reshape(NGRP, 4, HD, HD)
    eye = jnp.eye(4, dtype=w.dtype)
    return jnp.einsum('gjcd,jk->gjckd', w4, eye).reshape(NGRP, 256, 256).astype(bf16)


def _gate_terms(pr, pi, br, bi, sp):
    r = _sigmoid(pr + br)
    i = _sigmoid(pi + bi)
    la = -LRU_C * r * sp
    a = jnp.exp(la)
    x2 = 2.0 * la
    y = jnp.where(x2 > -0.02, -x2 * (1.0 + x2 * (0.5 + x2 * (1.0 / 6.0))), 1.0 - a * a)
    inv_mult = lax.rsqrt(jnp.maximum(y, 1e-30))
    return r, i, a, y * inv_mult, inv_mult


def _later(x, before, k):
    if k == 0:
        return x
    row = lax.broadcasted_iota(jnp.int32, before.shape, 0)
    rolled = pltpu.roll(x, k, 0)
    first = jnp.where(row < k, pltpu.roll(before, k, 0), rolled[0:8])
    return jnp.concatenate([first, rolled[8:]], axis=0)


def _earlier(x, after, k):
    if k == 0:
        return x
    n = x.shape[0]
    row = lax.broadcasted_iota(jnp.int32, after.shape, 0)
    rolled = pltpu.roll(x, n - k, 0)
    last = jnp.where(row >= 8 - k, pltpu.roll(after, 8 - k, 0), rolled[n - 8:n])
    return jnp.concatenate([rolled[0:n - 8], last], axis=0)


def _fwd_fused(h, wt, tabs, wo_shard, conv_w, conv_b, wr, wi, br, bi, lam, tm):
    s = h.shape[0]
    nt = s // tm
    nc = 512
    pieces = 8
    rows_per = tm // pieces
    later_chunks = (0, 1, 2, 3, 4, 7, 8)

    def body(h_ref, wt_ref, tab_ref, wo_ref, cw_ref, cb_ref, wr_ref, wi_ref, br_ref,
             bi_ref, lam_ref, q_ref, k_ref, v_ref, ga_ref, xl_ref, gl_ref, u_ref, hl_ref, r_ref, ig_ref, a_ref,
             im_ref, wo_all, wo_stage, halo, ub_scr, pr_scr, pi_scr, b_scr, hcar,
             send_sems, recv_sems, local_sems):
        i = pl.program_id(0)
        start, finish = _gather_ops([wo_stage], [wo_all], send_sems, recv_sems, local_sems)

        @pl.when(i == 0)
        def _():
            wo_stage[...] = wo_ref[...].astype(bf16)
            start()
            halo[...] = jnp.zeros_like(halo)
            hcar[...] = jnp.zeros_like(hcar)

        sp = _softplus(-lam_ref[...])
        br, bi = br_ref[...], bi_ref[...]
        c, sa, sb = _tables(tab_ref)
        piece_rows = lambda p: slice(rows_per * p, rows_per * p + rows_per)

        def project(ci):
            z = _dot_nt(h_ref[...], wt_ref[ci * nc:(ci + 1) * nc, :])
            if ci < 2:
                for j in range(nc // 128):
                    r = _rope(z[:, 128 * j:128 * j + 128], c, sa, sb) * (HD ** -0.5)
                    q_ref[ci * nc + 128 * j:ci * nc + 128 * j + 128, :] = r.astype(bf16).T
            elif ci == 2:
                for j in range(2):
                    js = slice(128 * j, 128 * j + 128)
                    k_ref[js, :] = _rope(z[:, js], c, sa, sb).astype(bf16).T
                    v_ref[js, :] = z[:, KVW + 128 * j:KVW + 128 * j + 128].astype(bf16).T
            else:
                sec, j = divmod(ci - 3, 2)
                (ga_ref, xl_ref, gl_ref)[sec][:, j * nc:(j + 1) * nc] = z

        def gate_terms(p):
            rows = piece_rows(p)
            r, ig, a, mult, inv_mult = _gate_terms(pr_scr[rows, :], pi_scr[rows, :], br, bi, sp)
            r_ref[rows, :] = r
            ig_ref[rows, :] = ig
            a_ref[rows, :] = a
            im_ref[rows, :] = inv_mult
            b_scr[rows, :] = mult * (ig * u_ref[rows, :])

        def scan(p, hc):
            for t in range(rows_per * p, rows_per * p + rows_per):
                hc = a_ref[t:t + 1, :] * hc + b_scr[t:t + 1, :]
                hl_ref[t:t + 1, :] = hc
            return hc

        project(5)
        project(6)
        xl = xl_ref[...]
        u = cb_ref[...] + sum(cw_ref[k:k + 1, :] * _later(xl, halo[...], CONVW - 1 - k) for k in range(CONVW))
        halo[...] = xl[tm - 8:tm, :]
        u_ref[...] = u
        ub_scr[...] = u.astype(bf16)
        for g in range(NGRP):
            gs = slice(256 * g, 256 * g + 256)
            pr_scr[:, gs] = _dot(ub_scr[:, gs], wr_ref[g])
            pi_scr[:, gs] = _dot(ub_scr[:, gs], wi_ref[g])
        hc = hcar[...]
        gate_terms(0)
        for slot, ci in enumerate(later_chunks):
            project(ci)
            gate_terms(slot + 1)
            hc = scan(slot, hc)
        hcar[...] = scan(pieces - 1, hc)

        @pl.when(i == nt - 1)
        def _():
            finish()

    row = lambda w: pl.BlockSpec((tm, w), lambda i: (i, 0))
    col = lambda w: pl.BlockSpec((w, tm), lambda i: (0, i))
    full = lambda a: pl.BlockSpec(a.shape, lambda i: (0,) * a.ndim)
    big = lambda w, dt: pltpu.HBM((s, w), dt)
    tile = pltpu.VMEM((tm, LW), f32)
    return pl.pallas_call(
        body, name="fwd_fused", grid=(nt,),
        in_specs=[row(D), full(wt), row(384), full(wo_shard), full(conv_w), full(conv_b),
                  full(wr), full(wi), full(br), full(bi), full(lam)],
        out_specs=[col(D), col(KVW), col(KVW), row(D), row(D), row(D)] + [row(LW)] * 6 + [HBM],
        out_shape=[pltpu.HBM((D, s), bf16), pltpu.HBM((KVW, s), bf16), pltpu.HBM((KVW, s), bf16),
                   big(D, f32), big(D, f32), big(D, f32)] + [big(LW, f32)] * 6 + [pltpu.HBM((2 * D, D), bf16)],
        scratch_shapes=[pltpu.VMEM((WO_ROWS, D), bf16), pltpu.VMEM((8, LW), f32),
                        pltpu.VMEM((tm, LW), bf16), tile, tile, tile, pltpu.VMEM((1, LW), f32)] + _comm_sems(1),
        compiler_params=_params(("arbitrary",), 56),
    )(*_in_hbm(h, wt), tabs, wo_shard, conv_w, conv_b, wr, wi, br, bi, lam)


def _lru_bwd(u, hl, dhl, xl, r, ig, a, im, conv_w, wr, wi, lam, tm):
    s = u.shape[0]
    nt = s // tm
    pieces = 8
    rows_per = tm // pieces

    def body(u_ref, h_ref, hp_ref, dh_ref, x_ref, r_ref, ig_ref, a_ref, im_ref, cw_ref, wr_ref, wi_ref,
             lam_ref, dxl_ref, dwr_ref, dwi_ref, dbr_ref, dbi_ref, dlam_ref, dcb_ref, dcw_ref,
             l_scr, du_scr, dpr_scr, dpi_scr, lcar, dunext):
        t0 = pl.program_id(0)
        tile = nt - 1 - t0

        @pl.when(t0 == 0)
        def _():
            lcar[...] = jnp.zeros_like(lcar)
            dunext[...] = jnp.zeros_like(dunext)
            for ref in (dwr_ref, dwi_ref, dbr_ref, dbi_ref, dlam_ref, dcb_ref, dcw_ref):
                ref[...] = jnp.zeros_like(ref)

        lam = lam_ref[...]
        sp = _softplus(-lam)
        hp = jnp.where(tile > 0, hp_ref[...], 0.0)

        def scan(p, c):
            for t in range(rows_per * p + rows_per - 1, rows_per * p - 1, -1):
                lt = dh_ref[t:t + 1, :] + c
                l_scr[t:t + 1, :] = lt
                c = a_ref[t:t + 1, :] * lt
            return c

        def terms(p, sums):
            rows = slice(rows_per * p, rows_per * p + rows_per)
            lt, u, r, i, a, inv_mult = l_scr[rows, :], u_ref[rows, :], r_ref[rows, :], ig_ref[rows, :], \
                a_ref[rows, :], im_ref[rows, :]
            before = hp if p == 0 else h_ref[rows_per * p - 8:rows_per * p, :]
            hprev = _later(h_ref[rows, :], before, 1)
            x2 = -2.0 * LRU_C * r * sp
            mult = jnp.where(x2 > -0.02, -x2 * (1.0 + x2 * (0.5 + x2 * (1.0 / 6.0))), 1.0 - a * a) * inv_mult
            da = lt * hprev
            dmult = lt * (i * u)
            di = lt * mult * u
            du_scr[rows, :] = lt * mult * i
            dla = da * a - dmult * (a * a) * inv_mult
            dr = dla * (-LRU_C * sp)
            dpr = dr * r * (1.0 - r)
            dpi = di * i * (1.0 - i)
            dpr_scr[rows, :] = dpr.astype(bf16)
            dpi_scr[rows, :] = dpi.astype(bf16)
            col = lambda t: jnp.sum(t, axis=0, keepdims=True)
            return sums[0] + col(dla * (-LRU_C * r)), sums[1] + col(dpr), sums[2] + col(dpi)

        sums = (jnp.zeros((1, LW), f32),) * 3
        c = scan(pieces - 1, lcar[...])
        for p in range(pieces - 1, -1, -1):
            if p > 0:
                c = scan(p - 1, c)
            sums = terms(p, sums)
        lcar[...] = c
        dlam_ref[...] += sums[0]
        dbr_ref[...] += sums[1]
        dbi_ref[...] += sums[2]

        ub = u_ref[...].astype(bf16)
        dug = []
        for g in range(NGRP):
            gs = slice(256 * g, 256 * g + 256)
            dwr_ref[g] += _dot_tn(ub[:, gs], dpr_scr[:, gs])
            dwi_ref[g] += _dot_tn(ub[:, gs], dpi_scr[:, gs])
            dug.append(_dot_nt(dpr_scr[:, gs], wr_ref[g]) + _dot_nt(dpi_scr[:, gs], wi_ref[g]))
        du = du_scr[...] + jnp.concatenate(dug, axis=1)

        dcb_ref[...] += jnp.sum(du, axis=0, keepdims=True)
        x = x_ref[...]
        after = dunext[...]
        dxl = jnp.zeros_like(du)
        for k in range(CONVW):
            e = _earlier(du, after, CONVW - 1 - k)
            dxl = dxl + cw_ref[k:k + 1, :] * e
            dcw_ref[k:k + 1, :] += jnp.sum(e * x, axis=0, keepdims=True)
        dxl_ref[...] = dxl.astype(bf16)
        dunext[...] = du[0:8, :]

        @pl.when(t0 == nt - 1)
        def _():
            dlam_ref[...] = dlam_ref[...] * (-_sigmoid(-lam))

    rev = lambda i: (nt - 1 - i, 0)
    row = pl.BlockSpec((tm, LW), rev)
    prev8 = pl.BlockSpec((8, LW), lambda i: (jnp.maximum((nt - 1 - i) * (tm // 8) - 1, 0), 0))
    full = lambda a: pl.BlockSpec(a.shape, lambda i: (0,) * a.ndim)
    vec = pl.BlockSpec((1, LW), lambda i: (0, 0))
    bd = pl.BlockSpec((NGRP, 256, 256), lambda i: (0, 0, 0))
    return pl.pallas_call(
        body, name="lru_bwd", grid=(nt,),
        in_specs=[row, row, prev8, row, row, row, row, row, row, full(conv_w), full(wr), full(wi), full(lam)],
        out_specs=[row, bd, bd, vec, vec, vec, vec, pl.BlockSpec((CONVW, LW), lambda i: (0, 0))],
        out_shape=[pltpu.HBM((s, LW), bf16),
                   jax.ShapeDtypeStruct((NGRP, 256, 256), f32), jax.ShapeDtypeStruct((NGRP, 256, 256), f32),
                   jax.ShapeDtypeStruct((1, LW), f32), jax.ShapeDtypeStruct((1, LW), f32),
                   jax.ShapeDtypeStruct((1, LW), f32), jax.ShapeDtypeStruct((1, LW), f32),
                   jax.ShapeDtypeStruct((CONVW, LW), f32)],
        scratch_shapes=[pltpu.VMEM((tm, LW), f32), pltpu.VMEM((tm, LW), f32), pltpu.VMEM((tm, LW), bf16),
                        pltpu.VMEM((tm, LW), bf16), pltpu.VMEM((1, LW), f32), pltpu.VMEM((8, LW), f32)],
        compiler_params=_params(("arbitrary",), 56),
    )(*_in_hbm(u, hl, hl, dhl, xl, r, ig, a, im), conv_w, wr, wi, lam)


def _gated_norm(t, gate, gain):
    sg = _sigmoid(gate)
    silu = gate * sg
    p = t * silu
    rstd = lax.rsqrt(jnp.mean(p * p, axis=-1, keepdims=True) + EPS)
    ph = p * rstd
    return sg, silu, rstd, ph, ph * gain


def _gated_norm_bwd(dy, t, gate, gain, sg, silu, rstd, ph):
    w = dy * gain
    dp = rstd * (w - ph * jnp.mean(w * ph, axis=-1, keepdims=True))
    dgate = dp * t * (sg * (1.0 + gate * (1.0 - sg)))
    return jnp.sum(dy * ph, axis=0, keepdims=True), dp * silu, dgate


def _out_fwd_bwd(x, tgt, o, ga, hl, gl, again, lgain, fgain, wo, tm):
    s = x.shape[0]
    nt = s // tm

    def body(x_ref, t_ref, o_ref, ga_ref, hl_ref, gl_ref, ag_ref, lg_ref, fg_ref, wo_ref,
             dx2_ref, do_ref, dga_ref, dhl_ref, dgl_ref, dwo_ref, gfg_ref, gag_ref, glg_ref, loss_ref, acc):
        i = pl.program_id(0)

        @pl.when(i == 0)
        def _():
            acc[...] = jnp.zeros_like(acc)
            for ref in (gfg_ref, gag_ref, glg_ref, loss_ref):
                ref[...] = jnp.zeros_like(ref)

        oo = jnp.concatenate([o_ref[128 * j:128 * j + 128, :].T for j in range(D // 128)], axis=1)
        gga, hh, ggl = ga_ref[...], hl_ref[...], gl_ref[...]
        ag, lg, fg = ag_ref[...], lg_ref[...], fg_ref[...]
        sga, silua, ra, pah, ya = _gated_norm(oo, gga, ag)
        sgl, silul, rl, plh, yl = _gated_norm(hh, ggl, lg)
        yab, ylb = ya.astype(bf16), yl.astype(bf16)
        y = _dot(yab, wo_ref[0:D, :]) + _dot(ylb, wo_ref[D:2 * D, :])
        x2 = x_ref[...] + y
        r2 = lax.rsqrt(jnp.mean(x2 * x2, axis=-1, keepdims=True) + EPS)
        x2h = x2 * r2
        err = x2h * fg - t_ref[...]
        loss_ref[...] += 0.5 * jnp.sum(jnp.sum(err * err, axis=-1, keepdims=True) * (1.0 / D))
        dout = err * (1.0 / D)
        gfg_ref[...] += jnp.sum(dout * x2h, axis=0, keepdims=True)
        w = dout * fg
        dx2 = r2 * (w - x2h * jnp.mean(w * x2h, axis=-1, keepdims=True))
        dx2_ref[...] = dx2
        dyb = dx2.astype(bf16)
        acc[0:D, :] += _dot_tn(yab, dyb)
        acc[D:2 * D, :] += _dot_tn(ylb, dyb)
        dya = _dot_nt(dyb, wo_ref[0:D, :])
        dyl = _dot_nt(dyb, wo_ref[D:2 * D, :])
        gag, do, dga = _gated_norm_bwd(dya, oo, gga, ag, sga, silua, ra, pah)
        glg, dhl, dgl = _gated_norm_bwd(dyl, hh, ggl, lg, sgl, silul, rl, plh)
        gag_ref[...] += gag
        glg_ref[...] += glg
        dob = do.astype(bf16)
        for j in range(D // 128):
            do_ref[128 * j:128 * j + 128, :] = dob[:, 128 * j:128 * j + 128].T
        dga_ref[...] = dga.astype(bf16)
        dhl_ref[...] = dhl
        dgl_ref[...] = dgl.astype(bf16)

        @pl.when(i == nt - 1)
        def _():
            dwo_ref[...] = acc[...].astype(bf16)

    row = pl.BlockSpec((tm, D), lambda i: (i, 0))
    col = pl.BlockSpec((D, tm), lambda i: (0, i))
    vec = pl.BlockSpec((1, D), lambda i: (0, 0))
    mat = pl.BlockSpec((2 * D, D), lambda i: (0, 0))
    return pl.pallas_call(
        body, name="out_fwd_bwd", grid=(nt,),
        in_specs=[row, row, col, row, row, row] + [vec] * 3 + [mat],
        out_specs=[row, col, row, row, row] + [mat, vec, vec, vec, pl.BlockSpec((1, 128), lambda i: (0, 0))],
        out_shape=[pltpu.HBM((s, D), f32), pltpu.HBM((D, s), bf16),
                   pltpu.HBM((s, D), bf16), pltpu.HBM((s, D), f32),
                   pltpu.HBM((s, D), bf16), pltpu.HBM((2 * D, D), bf16),
                   jax.ShapeDtypeStruct((1, D), f32), jax.ShapeDtypeStruct((1, D), f32),
                   jax.ShapeDtypeStruct((1, D), f32), jax.ShapeDtypeStruct((1, 128), f32)],
        scratch_shapes=[pltpu.VMEM((2 * D, D), f32)],
        compiler_params=_params(("arbitrary",), 56),
    )(*_in_hbm(x, tgt, o, ga, hl, gl), again, lgain, fgain, *_in_hbm(wo))


def _bwd_in(x, dx2, dq, dk, dv, dga, dxl, dgl, ln_gain, wt, tabs, tm):
    s = x.shape[0]

    def body(x_ref, dx2_ref, dq_ref, dk_ref, dv_ref, dga_ref, dxl_ref, dgl_ref, g_ref, wt_ref,
             tab_ref, gx_ref, gln_ref, dzt_ref):
        @pl.when(pl.program_id(0) == 0)
        def _():
            gln_ref[...] = jnp.zeros_like(gln_ref)

        c, sa, sb = (t.T for t in _tables(tab_ref))
        for j in range(D // 128):
            js = slice(128 * j, 128 * j + 128)
            dzt_ref[js, :] = (_unrope_t(dq_ref[js, :], c, sa, sb) * (HD ** -0.5)).astype(bf16)
        for j in range(KVW // 128):
            js = slice(128 * j, 128 * j + 128)
            dzt_ref[D + 128 * j:D + 128 * j + 128, :] = _unrope_t(dk_ref[js, :], c, sa, sb).astype(bf16)
        dzt_ref[D + KVW:D + 2 * KVW, :] = dv_ref[...].astype(bf16)
        first = D + 2 * KVW
        dh = _dot_tn(dzt_ref[0:512, :], wt_ref[0:512, :])
        for ci in range(1, first // 512):
            dh = dh + _dot_tn(dzt_ref[512 * ci:512 * ci + 512, :], wt_ref[512 * ci:512 * ci + 512, :])
        for sec, ref in enumerate((dga_ref, dxl_ref, dgl_ref)):
            for j in range(D // 512):
                rows = slice(first + D * sec + 512 * j, first + D * sec + 512 * j + 512)
                dh = dh + _dot(ref[:, 512 * j:512 * j + 512], wt_ref[rows, :])
            for j in range(D // 128):
                dzt_ref[first + D * sec + 128 * j:first + D * sec + 128 * j + 128, :] = ref[:, 128 * j:128 * j + 128].T
        xx = x_ref[...]
        rstd = lax.rsqrt(jnp.mean(xx * xx, axis=-1, keepdims=True) + EPS)
        xh = xx * rstd
        gln_ref[...] += jnp.sum(dh * xh, axis=0, keepdims=True)
        w = dh * g_ref[...]
        gx_ref[...] = dx2_ref[...] + rstd * (w - xh * jnp.mean(w * xh, axis=-1, keepdims=True))

    row = lambda w: pl.BlockSpec((tm, w), lambda i: (i, 0))
    col = lambda w: pl.BlockSpec((w, tm), lambda i: (0, i))
    full = lambda a: pl.BlockSpec(a.shape, lambda i: (0, 0))
    return pl.pallas_call(
        body, name="bwd_in", grid=(s // tm,),
        in_specs=[row(D), row(D), col(D), col(KVW), col(KVW), row(D), row(D), row(D), full(ln_gain), full(wt),
                  row(384)],
        out_specs=[row(D), pl.BlockSpec((1, D), lambda i: (0, 0)), col(NIN)],
        out_shape=[pltpu.HBM((s, D), f32), jax.ShapeDtypeStruct((1, D), f32),
                   pltpu.HBM((NIN, s), bf16)],
        compiler_params=_params(("arbitrary",), 56),
    )(*_in_hbm(x, dx2, dq, dk, dv, dga, dxl, dgl), ln_gain, *_in_hbm(wt), tabs)


WT_TERMS = 4


def _dwt_scatter(dzt, h, small, tm):
    s = h.shape[0]
    nk = s // tm
    srows = small.shape[0] // NDEV
    last = NDEV - 1

    def body(order_ref, dz_ref, h_ref, sm_ref, lwt_ref, lsm_ref, acc, stage, given, relayed, send_sems, recv_sems,
             local_sem, sm_send, sm_recv, sm_local):
        j, k = pl.program_id(0), pl.program_id(1)
        x, y, c = _place()
        sibling = (x, y, 1 - c)
        near = (x ^ (1 - c), y ^ c)
        far = (x ^ c, y ^ (1 - c))
        sm_start, sm_finish = _scatter_ops([sm_ref], [lsm_ref], sm_send, sm_recv, sm_local)

        def send(step):
            if step == last - 1:
                dst, to = lwt_ref.at[1], sibling
            elif step % 2 == 0:
                dst, to = given.at[step // 2], sibling
            elif step == 1:
                dst, to = relayed, (*near, c)
            else:
                dst, to = lwt_ref.at[1 + step // 2], (*(near if step == 3 else far), c)
            return pltpu.make_async_remote_copy(
                src_ref=stage.at[step % 2], dst_ref=dst, send_sem=send_sems.at[step], recv_sem=recv_sems.at[step],
                device_id=to, device_id_type=MESH)

        def keep():
            return pltpu.make_async_copy(stage.at[last % 2], lwt_ref.at[0], local_sem)

        @pl.when((j == 0) & (k == 0))
        def _():
            sm_start()

        @pl.when(k == 0)
        def _():
            acc[...] = jnp.zeros_like(acc)

        acc[...] += _dot(dz_ref[...], h_ref[...])

        for step in range(NDEV):
            @pl.when((k == nk - 1) & (j == step))
            def _(step=step):
                if step >= 2:
                    send(step - 2).wait_send()
                if step % 2 == 1 and step < last:
                    send(step - 1).wait_recv()
                    total = acc[...] + given[step // 2].astype(f32)
                    if step == 5:
                        send(1).wait_recv()
                        total = total + relayed[...].astype(f32)
                    stage[step % 2] = total.astype(bf16)
                else:
                    stage[step % 2] = acc[...].astype(bf16)
                if step < last:
                    send(step).start()
                else:
                    keep().start()
                    send(last - 1).wait_send()
                    for peer_step in (3, 5, last - 1):
                        send(peer_step).wait_recv()
                    keep().wait()
                    sm_finish()

    x, y, c = _place()
    dest = lambda chip, cc: 4 * chip[0] + 2 * chip[1] + cc
    near, far, diag = (x ^ (1 - c), y ^ c), (x ^ c, y ^ (1 - c)), (1 - x, 1 - y)
    order = jnp.stack([dest(diag, 1 - c), dest(diag, c), dest(far, 1 - c), dest(near, c),
                       dest(near, 1 - c), dest(far, c), dest((x, y), 1 - c), dest((x, y), c)])
    return pl.pallas_call(
        body, name="dwt_scatter",
        grid_spec=pltpu.PrefetchScalarGridSpec(
            num_scalar_prefetch=1, grid=(NDEV, nk),
            in_specs=[pl.BlockSpec((WT_ROWS, tm), lambda j, k, order: (order[j], k)),
                      pl.BlockSpec((tm, D), lambda j, k, order: (k, 0)), HBM],
            out_specs=[HBM, HBM],
            scratch_shapes=[pltpu.VMEM((WT_ROWS, D), f32), pltpu.VMEM((2, WT_ROWS, D), bf16),
                            pltpu.VMEM((3, WT_ROWS, D), bf16), pltpu.VMEM((WT_ROWS, D), bf16),
                            pltpu.SemaphoreType.DMA((last,)), pltpu.SemaphoreType.DMA((last,)),
                            pltpu.SemaphoreType.DMA(())] + _comm_sems(1)),
        out_shape=[pltpu.HBM((WT_TERMS, WT_ROWS, D), bf16), pltpu.HBM((NDEV, srows, D), f32)],
        compiler_params=_params(("arbitrary", "arbitrary"), 48),
    )(order, *_in_hbm(dzt, h, small))


def _diag_blocks(bd):
    eye = jnp.eye(4, dtype=bd.dtype)
    return jnp.einsum('gjckd,jk->gjcd', bd.reshape(NGRP, 4, HD, 4, HD), eye).reshape(NQ, HD, HD)


def _sequence_step(x, h, tabs, tgt, wt, wo_shard, conv_w, p):
    s = x.shape[0]
    tm = min(256, s)
    wr, wi = _block_diag(p["w_rgate"]), _block_diag(p["w_igate"])
    sinks = p["sinks"].reshape(NQ)
    qt, kt, vt, ga, xl, gl, u, hl, r, ig, a, im, wo = _fwd_fused(
        h, wt, tabs, wo_shard, conv_w, p["conv_b"], wr, wi, p["b_rgate"], p["b_igate"], p["lru_lambda"], tm)
    ot = _attn_fwd_t(qt, kt, vt, sinks)
    dx2, dot, dga, dhl, dgl, dwo, g_fg, g_ag, g_lg, loss = _out_fwd_bwd(
        x, tgt, ot, ga, hl, gl, p["attn_out_gain"], p["lru_out_gain"], p["final_gain"], wo, tm)
    dqt, dkt, dvt, dsink, land_wo = _attn_bwd_t(qt, kt, vt, dot, sinks, dwo)
    dxl, dwr, dwi, dbr, dbi, dlam, dcb, dcw = _lru_bwd(u, hl, dhl, xl, r, ig, a, im, conv_w, wr, wi, p["lru_lambda"], tm)
    gx, g_ln, dzt = _bwd_in(x, dx2, dqt, dkt, dvt, dga, dxl, dgl, p["ln_gain"], wt, tabs, tm)
    small = dict(ln_gain=g_ln, sinks=dsink.reshape(NQ, BLK).sum(axis=1)[None], conv_w=dcw, conv_b=dcb,
                 w_rgate=_diag_blocks(dwr), b_rgate=dbr, w_igate=_diag_blocks(dwi), b_igate=dbi, lru_lambda=dlam,
                 attn_out_gain=g_ag, lru_out_gain=g_lg, final_gain=g_fg)
    land_wt, land_sm = _dwt_scatter(dzt, h, _pack_small(small, loss), min(2048, s))
    return gx, land_wt, land_wo, land_sm


def _all_gather(srcs, out_dtypes, name):
    n = len(srcs)
    cast = [a.dtype != dt for a, dt in zip(srcs, out_dtypes)]

    def body(*refs):
        src_refs, out_refs = refs[:n], refs[n:2 * n]
        stage_refs = list(refs[2 * n:2 * n + sum(cast)])
        mine_refs = []
        for a in range(n):
            if cast[a]:
                st = stage_refs.pop(0)
                st[...] = src_refs[a][...].astype(out_dtypes[a])
                mine_refs.append(st)
            else:
                mine_refs.append(src_refs[a])
        start, finish = _gather_ops(mine_refs, out_refs, *refs[-3:])
        start()
        finish()

    vmem = pl.BlockSpec(memory_space=pltpu.VMEM)
    return pl.pallas_call(
        body, name=name,
        in_specs=[vmem] * n, out_specs=[HBM] * n,
        out_shape=[pltpu.HBM((NDEV * a.shape[0], a.shape[1]), dt) for a, dt in zip(srcs, out_dtypes)],
        scratch_shapes=[pltpu.VMEM(a.shape, dt) for a, dt, cst in zip(srcs, out_dtypes, cast) if cst] + _comm_sems(n),
        compiler_params=pltpu.CompilerParams(vmem_limit_bytes=32 * MIB),
    )(*srcs)


def _gather_weights(wt_shard, conv_blk, x, ln_gain, tm):
    s = x.shape[0]

    def body(wt_ref, cw_ref, g_ref, x_ref, wt_all, cw_all, h_ref, tab_ref, stage, xbuf, hbuf, tbuf,
             send_sems, recv_sems, local_sems):
        stage[...] = wt_ref[...].astype(bf16)
        start, relay, finish = _relay_gather_ops([stage, cw_ref], [wt_all, cw_all], send_sems, recv_sems, local_sems)
        start()
        gain = g_ref[...]
        for i in range(s // tm):
            rows = pl.ds(i * tm, tm)
            pltpu.sync_copy(x_ref.at[rows, :], xbuf)
            xx = xbuf[...]
            rstd = lax.rsqrt(jnp.mean(xx * xx, axis=-1, keepdims=True) + EPS)
            hbuf[...] = (xx * rstd * gain).astype(bf16)
            pltpu.sync_copy(hbuf, h_ref.at[rows, :])
        relay()
        dd = lax.broadcasted_iota(jnp.int32, (tm, 128), 1) % HD
        freq = jnp.zeros((tm, 128), f32)
        for j in range(ROT // 2):
            freq = jnp.where((dd < ROT) & (dd % (ROT // 2) == j), THETA ** (-2.0 * j / ROT), freq)
        for i in range(s // tm):
            pos = (lax.broadcasted_iota(jnp.int32, (tm, 128), 0) + i * tm).astype(f32)
            ang = pos * freq
            cos, sin = jnp.cos(ang), jnp.sin(ang)
            tbuf[:, 0:128] = jnp.where(dd < ROT, cos, 1.0)
            tbuf[:, 128:256] = jnp.where((dd >= ROT // 2) & (dd < ROT), sin, 0.0)
            tbuf[:, 256:384] = jnp.where(dd < ROT // 2, -sin, 0.0)
            pltpu.sync_copy(tbuf, tab_ref.at[pl.ds(i * tm, tm), :])
        finish()

    vmem = pl.BlockSpec(memory_space=pltpu.VMEM)
    return pl.pallas_call(
        body, name="gather_weights",
        in_specs=[vmem, vmem, vmem, HBM], out_specs=[HBM, HBM, HBM, HBM],
        out_shape=[pltpu.HBM((NIN, D), bf16), pltpu.HBM((NDEV * 8, 128), f32), pltpu.HBM((s, D), bf16),
                   pltpu.HBM((s, 384), f32)],
        scratch_shapes=[pltpu.VMEM((WT_ROWS, D), bf16), pltpu.VMEM((tm, D), f32), pltpu.VMEM((tm, D), bf16),
                        pltpu.VMEM((tm, 384), f32)] + _comm_sems(2),
        compiler_params=pltpu.CompilerParams(vmem_limit_bytes=32 * MIB),
    )(wt_shard, conv_blk, ln_gain, *_in_hbm(x))


def _sum_slots(land, tr, name):
    terms, rows, cols = land.shape

    def body(l_ref, o_ref):
        acc = l_ref[0].astype(f32)
        for d in range(1, terms):
            acc = acc + l_ref[d].astype(f32)
        o_ref[...] = acc

    return pl.pallas_call(
        body, name=name, grid=(rows // tr,),
        in_specs=[pl.BlockSpec((terms, tr, cols), lambda i: (0, i, 0))],
        out_specs=pl.BlockSpec((tr, cols), lambda i: (i, 0)),
        out_shape=jax.ShapeDtypeStruct((rows, cols), f32),
        compiler_params=_params(("arbitrary",), 32),
    )(*_in_hbm(land))


def _adam_math(w, g, m, v):
    m2 = ADAM_B1 * m + (1.0 - ADAM_B1) * g
    v2 = ADAM_B2 * v + (1.0 - ADAM_B2) * (g * g)
    m_hat = m2 / (1.0 - ADAM_B1 ** ADAM_STEP)
    v_hat = v2 / (1.0 - ADAM_B2 ** ADAM_STEP)
    delta = -ADAM_LR * (m_hat / (jnp.sqrt(v_hat) + ADAM_EPS) + ADAM_WD * w)
    return delta, m2, v2


def _reduce_adamw(land, w, m, v, tr, name):
    terms, rows, cols = land.shape

    def body(l_ref, w_ref, m_ref, v_ref, g_ref, d_ref, m2_ref, v2_ref):
        g = l_ref[0].astype(f32)
        for t in range(1, terms):
            g = g + l_ref[t].astype(f32)
        g_ref[...] = g
        d_ref[...], m2_ref[...], v2_ref[...] = _adam_math(w_ref[...], g, m_ref[...], v_ref[...])

    blk = pl.BlockSpec((tr, cols), lambda i: (i, 0))
    return pl.pallas_call(
        body, name=name, grid=(rows // tr,),
        in_specs=[pl.BlockSpec((terms, tr, cols), lambda i: (0, i, 0))] + [blk] * 3, out_specs=[blk] * 4,
        out_shape=[jax.ShapeDtypeStruct((rows, cols), f32)] * 4,
        compiler_params=_params(("arbitrary",), 32),
    )(*_in_hbm(land), w, m, v)


VEC_NAMES = ("ln_gain", "conv_b", "b_rgate", "b_igate", "lru_lambda", "attn_out_gain", "lru_out_gain", "final_gain")
ROW_RGATE, ROW_IGATE, ROW_VEC, ROW_SINKS = 0, 64, 128, 136
LOSS_LANE = NQ


def _adamw_small(g_rep, g_conv, w, m, v):
    names = list(VEC_NAMES) + ["sinks", "conv_w", "w_rgate", "w_igate"]
    ins = [g_rep, g_conv] + [d[k] for k in names for d in (w, m, v)]

    def body(*refs):
        g_ref, gc_ref = refs[0], refs[1]
        in_refs = refs[2:2 + 3 * len(names)]
        out_refs = refs[2 + 3 * len(names):]

        def update(j, g, at=None):
            w_ref, m_ref, v_ref = in_refs[3 * j:3 * j + 3]
            outs = out_refs[4 * j:4 * j + 4]
            pick = (lambda r: r[...]) if at is None else (lambda r: r[at])
            res = (g,) + _adam_math(pick(w_ref), g, pick(m_ref), pick(v_ref))
            for o_ref, val in zip(outs, res):
                if at is None:
                    o_ref[...] = val
                else:
                    o_ref[at] = val

        for j in range(len(VEC_NAMES)):
            update(j, g_ref[ROW_VEC + j:ROW_VEC + j + 1, :])
        update(len(VEC_NAMES), g_ref[ROW_SINKS:ROW_SINKS + 1, 0:NQ])
        update(len(VEC_NAMES) + 1, gc_ref[...], at=0)
        for gi, row0 in ((len(VEC_NAMES) + 2, ROW_RGATE), (len(VEC_NAMES) + 3, ROW_IGATE)):
            for nb in range(NQ):
                update(gi, g_ref[row0:row0 + HD, HD * nb:HD * nb + HD], at=(0, nb))

    vmem = pl.BlockSpec(memory_space=pltpu.VMEM)
    out_shape = [jax.ShapeDtypeStruct(w[k].shape, f32) for k in names for _ in range(4)]
    outs = pl.pallas_call(
        body, name="adamw_small",
        in_specs=[vmem] * len(ins), out_specs=[vmem] * len(out_shape), out_shape=out_shape,
        compiler_params=pltpu.CompilerParams(vmem_limit_bytes=32 * MIB),
    )(*ins)
    return {k: tuple(outs[4 * j:4 * j + 4]) for j, k in enumerate(names)}


def _pack_small(small, loss):
    gate = lambda g: g.transpose(1, 0, 2).reshape(HD, NQ * HD)
    row_s = jnp.concatenate([small["sinks"], loss[:, LOSS_LANE:128], jnp.zeros((1, D - 128), f32)], axis=1)
    rep = jnp.concatenate([gate(small["w_rgate"]), gate(small["w_igate"])] + [small[k] for k in VEC_NAMES]
                          + [row_s, jnp.zeros((SMALL_ROWS - ROW_SINKS - 1, D), f32)], axis=0)
    conv = small["conv_w"].reshape(CONVW, NDEV, 128).transpose(1, 0, 2)
    conv = jnp.pad(conv, ((0, 0), (0, 8 - CONVW), (0, D - 128)))
    return jnp.concatenate([rep.reshape(NDEV, SMALL_PER, D), conv], axis=1).reshape(NDEV * (SMALL_PER + 8), D)


def kernel(x, ln_gain, w_in, sinks, conv_w, conv_b, w_rgate, b_rgate, w_igate, b_igate, lru_lambda, attn_out_gain, lru_out_gain, w_out, final_gain, loss_target, m_ln_gain, m_w_in, m_sinks, m_conv_w, m_conv_b, m_w_rgate, m_b_rgate, m_w_igate, m_b_igate, m_lru_lambda, m_attn_out_gain, m_lru_out_gain, m_w_out, m_final_gain, v_ln_gain, v_w_in, v_sinks, v_conv_w, v_conv_b, v_w_rgate, v_b_rgate, v_w_igate, v_b_igate, v_lru_lambda, v_attn_out_gain, v_lru_out_gain, v_w_out, v_final_gain):
    w = dict(ln_gain=ln_gain, sinks=sinks, conv_w=conv_w, conv_b=conv_b, w_rgate=w_rgate, b_rgate=b_rgate,
             w_igate=w_igate, b_igate=b_igate, lru_lambda=lru_lambda, attn_out_gain=attn_out_gain,
             lru_out_gain=lru_out_gain, final_gain=final_gain.reshape(1, D))
    m = dict(ln_gain=m_ln_gain, sinks=m_sinks, conv_w=m_conv_w, conv_b=m_conv_b, w_rgate=m_w_rgate,
             b_rgate=m_b_rgate, w_igate=m_w_igate, b_igate=m_b_igate, lru_lambda=m_lru_lambda,
             attn_out_gain=m_attn_out_gain, lru_out_gain=m_lru_out_gain, final_gain=m_final_gain.reshape(1, D))
    v = dict(ln_gain=v_ln_gain, sinks=v_sinks, conv_w=v_conv_w, conv_b=v_conv_b, w_rgate=v_w_rgate,
             b_rgate=v_b_rgate, w_igate=v_w_igate, b_igate=v_b_igate, lru_lambda=v_lru_lambda,
             attn_out_gain=v_attn_out_gain, lru_out_gain=v_lru_out_gain, final_gain=v_final_gain.reshape(1, D))

    conv_blk = jnp.pad(conv_w[0], ((0, 8 - CONVW), (0, 0)))
    wt, cw_all, h, tabs = _gather_weights(w_in[0].T, conv_blk, x[0], ln_gain, min(512, x.shape[1]))
    conv_full = cw_all.reshape(NDEV, 8, 128)[:, 0:CONVW].transpose(1, 0, 2).reshape(CONVW, LW)

    p = {k: (w[k][0] if k in ("w_rgate", "w_igate") else w[k]) for k in w if k != "conv_w"}
    gx, land_wt, land_wo, land_sm = _sequence_step(x[0], h, tabs, loss_target[0], wt, w_out[0], conv_full, p)

    g_sm = _sum_slots(land_sm, SMALL_PER + 8, "sum_small")
    (g_rep,) = _all_gather([g_sm[0:SMALL_PER]], [f32], "gather_small")
    g_conv = g_sm[SMALL_PER:SMALL_PER + CONVW, 0:128]

    wins = _reduce_adamw(land_wt, w_in[0].T, m_w_in[0].T, v_w_in[0].T, 192, "adamw_w_in")
    g_win, d_win, m_win, v_win = (t.T for t in wins)
    g_wo, d_wo, m_wo, v_wo = _reduce_adamw(land_wo, w_out[0], m_w_out[0], v_w_out[0], 256, "adamw_w_out")
    res = _adamw_small(g_rep, g_conv, w, m, v)
    res["w_in"] = tuple(t[None] for t in (g_win, d_win, m_win, v_win))
    res["w_out"] = tuple(t[None] for t in (g_wo, d_wo, m_wo, v_wo))
    res["final_gain"] = tuple(t.reshape(D) for t in res["final_gain"])

    order = ("ln_gain", "w_in", "sinks", "conv_w", "conv_b", "w_rgate", "b_rgate", "w_igate", "b_igate",
             "lru_lambda", "attn_out_gain", "lru_out_gain", "w_out", "final_gain")
    total_loss = g_rep[ROW_SINKS, LOSS_LANE]
    return (total_loss, gx[None]) + tuple(res[k][i] for i in range(4) for k in order)
```

```python
import jax
import jax.numpy as jnp
from jax import lax
from jax.experimental import pallas as pl
from jax.experimental.pallas import tpu as pltpu

f32 = jnp.float32
bf16 = jnp.bfloat16

D = 1024
HD = 64
NQ = 16
NKV = 4
GROUP = NQ // NKV
KVW = NKV * HD
BLK = 128
ROT = 16
THETA = 500000.0
NEG = -1e30
LW = 1024
NGRP = 4
CONVW = 4
LRU_C = 8.0
NIN = 4608
EPS = 1e-6
NDEV = 8
WT_ROWS = NIN // NDEV
WO_ROWS = 2 * D // NDEV
SMALL_ROWS = 192
SMALL_PER = SMALL_ROWS // NDEV

ADAM_LR = 0.001
ADAM_B1 = 0.9
ADAM_B2 = 0.999
ADAM_EPS = 1e-08
ADAM_WD = 0.01
ADAM_STEP = 10

NT = (((1,), (1,)), ((), ()))
TN = (((0,), (0,)), ((), ()))
MESH = pl.DeviceIdType.MESH
MIB = 1024 * 1024


def _dot(a, b):
    return jnp.dot(a, b, preferred_element_type=f32)


def _dot_nt(a, b):
    return lax.dot_general(a, b, NT, preferred_element_type=f32)


def _dot_tn(a, b):
    return lax.dot_general(a, b, TN, preferred_element_type=f32)


def _params(sem, vmem_mib):
    return pltpu.CompilerParams(dimension_semantics=sem, vmem_limit_bytes=vmem_mib * MIB)


def _sigmoid(x):
    return 0.5 * jnp.tanh(0.5 * x) + 0.5


def _softplus(x):
    return jnp.maximum(x, 0.0) + jnp.log(1.0 + jnp.exp(-jnp.abs(x)))


def _rope_tables(s):
    pos = jnp.arange(s, dtype=f32)
    inv_freq = THETA ** (-jnp.arange(0, ROT, 2, dtype=f32) / ROT)
    ang = pos[:, None] * inv_freq[None, :]
    cs = jnp.concatenate([jnp.cos(ang) - 1.0, jnp.sin(ang)], axis=1)
    d = jnp.arange(128) % HD
    j = jnp.arange(ROT)[:, None]
    pick_c = ((d < ROT) & (j == d % (ROT // 2))).astype(f32)
    pick_sa = ((d >= ROT // 2) & (d < ROT) & (j == d)).astype(f32)
    pick_sb = -((d < ROT // 2) & (j == d + ROT // 2)).astype(f32)
    picks = jnp.concatenate([pick_c, pick_sa, pick_sb], axis=1)
    ones = jnp.concatenate([jnp.ones((1, 128), f32), jnp.zeros((1, 256), f32)], axis=1)
    return jnp.dot(cs, picks, precision=lax.Precision.HIGHEST) + ones


def _tables(tab_ref):
    return tab_ref[:, 0:128], tab_ref[:, 128:256], tab_ref[:, 256:384]


def _rope(t, c, sa, sb):
    return t * c + pltpu.roll(t, 8, 1) * sa + pltpu.roll(t, 120, 1) * sb


def _unrope_t(dr, c, sa, sb):
    return dr * c + pltpu.roll(dr * sa, 120, 0) + pltpu.roll(dr * sb, 8, 0)


def _place():
    return lax.axis_index("x"), lax.axis_index("y"), lax.axis_index("c")


def _gather_ops(mine_refs, out_refs, send_sems, recv_sems, local_sems):
    n = len(mine_refs)
    x, y, c = _place()
    me, sibling = (x, y, c), (x, y, 1 - c)
    chips = [(1 - x, y), (x, 1 - y), (1 - x, 1 - y)]

    def rows(a, dev):
        m = mine_refs[a].shape[0]
        return out_refs[a].at[pl.ds((4 * dev[0] + 2 * dev[1] + dev[2]) * m, m), :]

    def copy(a, k, block, to, own=False):
        return pltpu.make_async_remote_copy(
            src_ref=mine_refs[a] if own else rows(a, block), dst_ref=rows(a, block),
            send_sem=send_sems.at[a, k], recv_sem=recv_sems.at[a, k], device_id=to, device_id_type=MESH)

    def local(a):
        return pltpu.make_async_copy(mine_refs[a], rows(a, me), local_sems.at[a])

    def first(a):
        return [copy(a, 0, me, sibling, own=True)] + [copy(a, 1 + j, me, (*chip, c), own=True)
                                                      for j, chip in enumerate(chips)]

    def start():
        for a in range(n):
            local(a).start()
            for cp in first(a):
                cp.start()

    def finish():
        for j, chip in enumerate(chips):
            for a in range(n):
                copy(a, 1 + j, (*chip, c), me).wait_recv()
                copy(a, 4 + j, (*chip, c), sibling).start()
        for a in range(n):
            copy(a, 0, sibling, me).wait_recv()
            for j, chip in enumerate(chips):
                copy(a, 4 + j, (*chip, 1 - c), me).wait_recv()
        for a in range(n):
            for cp in first(a) + [copy(a, 4 + j, (*chip, c), sibling) for j, chip in enumerate(chips)]:
                cp.wait_send()
            local(a).wait()

    return start, finish


def _relay_gather_ops(mine_refs, out_refs, send_sems, recv_sems, local_sems):
    n = len(mine_refs)
    x, y, c = _place()
    me, sibling = (x, y, c), (x, y, 1 - c)
    near = (x ^ (1 - c), y ^ c)
    far = (x ^ c, y ^ (1 - c))
    diag = (1 - x, 1 - y)

    def rows(a, dev):
        m = mine_refs[a].shape[0]
        return out_refs[a].at[pl.ds((4 * dev[0] + 2 * dev[1] + dev[2]) * m, m), :]

    def copy(a, k, block, to, own=False):
        return pltpu.make_async_remote_copy(
            src_ref=mine_refs[a] if own else rows(a, block), dst_ref=rows(a, block),
            send_sem=send_sems.at[a, k], recv_sem=recv_sems.at[a, k], device_id=to, device_id_type=MESH)

    def local(a):
        return pltpu.make_async_copy(mine_refs[a], rows(a, me), local_sems.at[a])

    def sends(a):
        return [copy(a, 0, me, sibling, own=True), copy(a, 1, me, (*near, c), own=True),
                copy(a, 2, me, (*far, c), own=True), copy(a, 3, (*near, c), (*far, c)),
                copy(a, 4, (*near, c), sibling), copy(a, 5, (*far, c), sibling), copy(a, 6, (*diag, c), sibling)]

    def arrivals(a):
        return [copy(a, 0, sibling, me), copy(a, 1, (*near, c), me), copy(a, 2, (*far, c), me),
                copy(a, 3, (*diag, c), me), copy(a, 4, (*far, 1 - c), me), copy(a, 5, (*near, 1 - c), me),
                copy(a, 6, (*diag, 1 - c), me)]

    def start():
        for a in range(n):
            local(a).start()
            for cp in sends(a)[0:3]:
                cp.start()

    def finish():
        for first, then in ((1, (3, 4)), (2, (5,)), (3, (6,))):
            for a in range(n):
                arrivals(a)[first].wait_recv()
                for k in then:
                    sends(a)[k].start()
        for a in range(n):
            for k in (0, 4, 5, 6):
                arrivals(a)[k].wait_recv()
        for a in range(n):
            for cp in sends(a):
                cp.wait_send()
            local(a).wait()

    return start, finish


def _scatter_ops(src_refs, land_refs, send_sems, recv_sems, local_sems):
    n = len(src_refs)
    x, y, c = _place()
    my = 4 * x + 2 * y + c

    def peer(k):
        return x ^ (k >> 2), y ^ ((k >> 1) & 1), c ^ (k & 1)

    def piece(a, dev):
        m = src_refs[a].shape[0] // NDEV
        return src_refs[a].at[pl.ds(dev * m, m), :]

    def local(a):
        return pltpu.make_async_copy(piece(a, my), land_refs[a].at[my], local_sems.at[a])

    def send(a, k):
        px, py, pc = peer(k)
        return pltpu.make_async_remote_copy(
            src_ref=piece(a, 4 * px + 2 * py + pc), dst_ref=land_refs[a].at[my],
            send_sem=send_sems.at[a, k - 1], recv_sem=recv_sems.at[a, k - 1],
            device_id=(px, py, pc), device_id_type=MESH)

    def arrival(a, k):
        px, py, pc = peer(k)
        return pltpu.make_async_remote_copy(
            src_ref=piece(a, my), dst_ref=land_refs[a].at[4 * px + 2 * py + pc],
            send_sem=send_sems.at[a, k - 1], recv_sem=recv_sems.at[a, k - 1],
            device_id=(px, py, pc), device_id_type=MESH)

    def start():
        for a in range(n):
            local(a).start()
        for k in range(1, NDEV):
            for a in range(n):
                send(a, k).start()

    def finish():
        for k in range(1, NDEV):
            for a in range(n):
                send(a, k).wait_send()
        for k in range(1, NDEV):
            for a in range(n):
                arrival(a, k).wait_recv()
        for a in range(n):
            local(a).wait()

    return start, finish


def _in_hbm(*arrays):
    return tuple(pltpu.with_memory_space_constraint(a, pltpu.HBM) for a in arrays)


def _comm_sems(n):
    return [pltpu.SemaphoreType.DMA((n, 7)), pltpu.SemaphoreType.DMA((n, 7)), pltpu.SemaphoreType.DMA((n,))]


HBM = pl.BlockSpec(memory_space=pltpu.HBM)


def _sink_rows(sinks):
    return jnp.repeat(sinks.reshape(NKV, GROUP), BLK, axis=1)


def _band_softmax(s2_ref, ls, prev_offset, sink_row):
    jj = lax.broadcasted_iota(jnp.int32, (BLK, BLK), 0)
    ii = lax.broadcasted_iota(jnp.int32, (BLK, BLK), 1)
    from_prev = jj > ii
    sc = jnp.where(from_prev, s2_ref[0:BLK, ls] + prev_offset, s2_ref[BLK:2 * BLK, ls])
    m = jnp.maximum(jnp.max(sc, axis=0, keepdims=True), sink_row)
    p = jnp.exp(sc - m)
    es = jnp.exp(sink_row - m)
    inv = 1.0 / (jnp.sum(p, axis=0, keepdims=True) + es)
    return from_prev, p * inv, es * inv


def _put_split(dst_ref, ls, t, from_prev):
    t = t.astype(bf16)
    zero = jnp.zeros_like(t)
    dst_ref[0:BLK, ls] = jnp.where(from_prev, t, zero)
    dst_ref[BLK:2 * BLK, ls] = jnp.where(from_prev, zero, t)


def _heads_side_by_side(ref, h):
    return jnp.concatenate([ref[HD * (GROUP * h + g):HD * (GROUP * h + g) + HD, :] for g in range(GROUP)], axis=1)


def _kv_specs_t():
    prev = pl.BlockSpec((KVW, BLK), lambda n: (0, jnp.maximum(n - 1, 0)))
    cur = pl.BlockSpec((KVW, BLK), lambda n: (0, n))
    return [prev, cur, prev, cur]


def _attn_fwd_t(qt, kt, vt, sinks):
    s = qt.shape[1]

    def body(sink_ref, q_ref, kp_ref, kc_ref, vp_ref, vc_ref, o_ref, s2_scr, pn2_scr):
        n = pl.program_id(0)
        off = jnp.where(n > 0, 0.0, NEG)

        def scores(h):
            hs = slice(HD * h, HD * h + HD)
            kh = jnp.concatenate([kp_ref[hs, :], kc_ref[hs, :]], axis=1)
            s2_scr[h % 2] = _dot_tn(kh, _heads_side_by_side(q_ref, h))

        def probs(h):
            for g in range(GROUP):
                ls = slice(BLK * g, BLK * g + BLK)
                from_prev, pn, _ = _band_softmax(s2_scr.at[h % 2], ls, off, sink_ref[h:h + 1, ls])
                _put_split(pn2_scr.at[h % 2], ls, pn, from_prev)

        def outputs(h):
            hs = slice(HD * h, HD * h + HD)
            vh = jnp.concatenate([vp_ref[hs, :], vc_ref[hs, :]], axis=1)
            og = _dot(vh, pn2_scr[h % 2])
            for g in range(GROUP):
                a = GROUP * h + g
                o_ref[HD * a:HD * a + HD, :] = og[:, BLK * g:BLK * g + BLK]

        scores(0)
        for h in range(NKV):
            if h + 1 < NKV:
                scores(h + 1)
            probs(h)
            outputs(h)

    return pl.pallas_call(
        body, name="attn_fwd", grid=(s // BLK,),
        in_specs=[pl.BlockSpec((NKV, GROUP * BLK), lambda n: (0, 0)), pl.BlockSpec((D, BLK), lambda n: (0, n))]
        + _kv_specs_t(),
        out_specs=pl.BlockSpec((D, BLK), lambda n: (0, n)),
        out_shape=pltpu.HBM((D, s), f32),
        scratch_shapes=[pltpu.VMEM((2, 2 * BLK, GROUP * BLK), f32), pltpu.VMEM((2, 2 * BLK, GROUP * BLK), bf16)],
        compiler_params=_params(("arbitrary",), 32),
    )(_sink_rows(sinks), *_in_hbm(qt, kt, kt, vt, vt))


def _attn_bwd_t(qt, kt, vt, dot, sinks, dwo):
    s = qt.shape[1]
    nb = s // BLK

    def body(sink_ref, q_ref, do_ref, kp_ref, kc_ref, vp_ref, vc_ref, dwo_ref, dq_ref, dk_ref, dv_ref, ds_ref,
             land_ref, dk_hold, dv_hold, s2_scr, dp2_scr, pn2_scr, ds2_scr, send_sems, recv_sems, local_sems):
        n = pl.program_id(0)
        start, finish = _scatter_ops([dwo_ref], [land_ref], send_sems, recv_sems, local_sems)

        @pl.when(n == 0)
        def _():
            start()
            dk_hold[...] = jnp.zeros_like(dk_hold)
            dv_hold[...] = jnp.zeros_like(dv_hold)
            ds_ref[...] = jnp.zeros_like(ds_ref)

        @pl.when(n < nb)
        def _():
            off = jnp.where(n > 0, 0.0, NEG)

            def scores(h):
                hs = slice(HD * h, HD * h + HD)
                kh = jnp.concatenate([kp_ref[hs, :], kc_ref[hs, :]], axis=1)
                vh = jnp.concatenate([vp_ref[hs, :], vc_ref[hs, :]], axis=1)
                s2_scr[h % 2] = _dot_tn(kh, _heads_side_by_side(q_ref, h))
                dp2_scr[h % 2] = _dot_tn(vh, _heads_side_by_side(do_ref, h))

            def softmax_bwd(h):
                for g in range(GROUP):
                    ls = slice(BLK * g, BLK * g + BLK)
                    from_prev, pn, ps = _band_softmax(s2_scr.at[h % 2], ls, off, sink_ref[h:h + 1, ls])
                    dp = jnp.where(from_prev, dp2_scr[h % 2, 0:BLK, ls], dp2_scr[h % 2, BLK:2 * BLK, ls])
                    dsum = jnp.sum(pn * dp, axis=0, keepdims=True)
                    ds_ref[h:h + 1, ls] += -ps * dsum
                    _put_split(pn2_scr.at[h % 2], ls, pn, from_prev)
                    _put_split(ds2_scr.at[h % 2], ls, pn * (dp - dsum), from_prev)

            def grads(h):
                hs = slice(HD * h, HD * h + HD)
                kh = jnp.concatenate([kp_ref[hs, :], kc_ref[hs, :]], axis=1)
                dqg = _dot(kh, ds2_scr[h % 2])
                for g in range(GROUP):
                    a = GROUP * h + g
                    dq_ref[HD * a:HD * a + HD, :] = dqg[:, BLK * g:BLK * g + BLK]
                dkh = _dot_nt(_heads_side_by_side(q_ref, h), ds2_scr[h % 2])
                dvh = _dot_nt(_heads_side_by_side(do_ref, h), pn2_scr[h % 2])
                dk_ref[hs, :] = dk_hold[hs, :] + dkh[:, 0:BLK]
                dv_ref[hs, :] = dv_hold[hs, :] + dvh[:, 0:BLK]
                dk_hold[hs, :] = dkh[:, BLK:2 * BLK]
                dv_hold[hs, :] = dvh[:, BLK:2 * BLK]

            scores(0)
            for h in range(NKV):
                if h + 1 < NKV:
                    scores(h + 1)
                softmax_bwd(h)
                grads(h)

        @pl.when(n == nb)
        def _():
            dk_ref[...] = dk_hold[...]
            dv_ref[...] = dv_hold[...]
            finish()

    blk = pl.BlockSpec((D, BLK), lambda n: (0, jnp.minimum(n, nb - 1)))
    late = pl.BlockSpec((KVW, BLK), lambda n: (0, jnp.maximum(n - 1, 0)))
    whole = pl.BlockSpec((NKV, GROUP * BLK), lambda n: (0, 0))
    kv = [pl.BlockSpec((KVW, BLK), lambda n: (0, jnp.clip(n - 1, 0, nb - 1))),
          pl.BlockSpec((KVW, BLK), lambda n: (0, jnp.minimum(n, nb - 1)))]
    return pl.pallas_call(
        body, name="attn_bwd", grid=(nb + 1,),
        in_specs=[whole, blk, blk] + kv + kv + [HBM],
        out_specs=[blk, late, late, whole, HBM],
        out_shape=[pltpu.HBM((D, s), f32), pltpu.HBM((KVW, s), f32), pltpu.HBM((KVW, s), f32),
                   jax.ShapeDtypeStruct((NKV, GROUP * BLK), f32), pltpu.HBM((NDEV, WO_ROWS, D), bf16)],
        scratch_shapes=[pltpu.VMEM((KVW, BLK), f32), pltpu.VMEM((KVW, BLK), f32)]
        + [pltpu.VMEM((2, 2 * BLK, GROUP * BLK), f32)] * 2 + [pltpu.VMEM((2, 2 * BLK, GROUP * BLK), bf16)] * 2
        + _comm_sems(1),
        compiler_params=_params(("arbitrary",), 48),
    )(_sink_rows(sinks), *_in_hbm(qt, dot, kt, kt, vt, vt, dwo))


def _block_diag(w):
    w4 = w.reshape(NGRP, 4, HD, HD)
    eye = jnp.eye(4, dtype=w.dtype)
    return jnp.einsum('gjcd,jk->gjckd', w4, eye).reshape(NGRP, 256, 256).astype(bf16)


def _gate_terms(pr, pi, br, bi, sp):
    r = _sigmoid(pr + br)
    i = _sigmoid(pi + bi)
    la = -LRU_C * r * sp
    a = jnp.exp(la)
    x2 = 2.0 * la
    y = jnp.where(x2 > -0.02, -x2 * (1.0 + x2 * (0.5 + x2 * (1.0 / 6.0))), 1.0 - a * a)
    inv_mult = lax.rsqrt(jnp.maximum(y, 1e-30))
    return r, i, a, y * inv_mult, inv_mult


def _later(x, before, k):
    if k == 0:
        return x
    row = lax.broadcasted_iota(jnp.int32, before.shape, 0)
    rolled = pltpu.roll(x, k, 0)
    first = jnp.where(row < k, pltpu.roll(before, k, 0), rolled[0:8])
    return jnp.concatenate([first, rolled[8:]], axis=0)


def _earlier(x, after, k):
    if k == 0:
        return x
    n = x.shape[0]
    row = lax.broadcasted_iota(jnp.int32, after.shape, 0)
    rolled = pltpu.roll(x, n - k, 0)
    last = jnp.where(row >= 8 - k, pltpu.roll(after, 8 - k, 0), rolled[n - 8:n])
    return jnp.concatenate([rolled[0:n - 8], last], axis=0)


def _fwd_fused(h, wt, tabs, wo_shard, conv_w, conv_b, wr, wi, br, bi, lam, tm):
    s = h.shape[0]
    nt = s // tm
    nc = 512
    pieces = 8
    rows_per = tm // pieces
    later_chunks = (0, 1, 2, 3, 4, 7, 8)

    def body(h_ref, wt_ref, tab_ref, wo_ref, cw_ref, cb_ref, wr_ref, wi_ref, br_ref,
             bi_ref, lam_ref, q_ref, k_ref, v_ref, ga_ref, xl_ref, gl_ref, u_ref, hl_ref, r_ref, ig_ref, a_ref,
             im_ref, wo_all, wo_stage, halo, ub_scr, pr_scr, pi_scr, b_scr, hcar,
             send_sems, recv_sems, local_sems):
        i = pl.program_id(0)
        start, finish = _gather_ops([wo_stage], [wo_all], send_sems, recv_sems, local_sems)

        @pl.when(i == 0)
        def _():
            wo_stage[...] = wo_ref[...].astype(bf16)
            start()
            halo[...] = jnp.zeros_like(halo)
            hcar[...] = jnp.zeros_like(hcar)

        sp = _softplus(-lam_ref[...])
        br, bi = br_ref[...], bi_ref[...]
        c, sa, sb = _tables(tab_ref)
        piece_rows = lambda p: slice(rows_per * p, rows_per * p + rows_per)

        def project(ci):
            z = _dot_nt(h_ref[...], wt_ref[ci * nc:(ci + 1) * nc, :])
            if ci < 2:
                for j in range(nc // 128):
                    r = _rope(z[:, 128 * j:128 * j + 128], c, sa, sb) * (HD ** -0.5)
                    q_ref[ci * nc + 128 * j:ci * nc + 128 * j + 128, :] = r.astype(bf16).T
            elif ci == 2:
                for j in range(2):
                    js = slice(128 * j, 128 * j + 128)
                    k_ref[js, :] = _rope(z[:, js], c, sa, sb).astype(bf16).T
                    v_ref[js, :] = z[:, KVW + 128 * j:KVW + 128 * j + 128].astype(bf16).T
            else:
                sec, j = divmod(ci - 3, 2)
                (ga_ref, xl_ref, gl_ref)[sec][:, j * nc:(j + 1) * nc] = z

        def gate_terms(p):
            rows = piece_rows(p)
            r, ig, a, mult, inv_mult = _gate_terms(pr_scr[rows, :], pi_scr[rows, :], br, bi, sp)
            r_ref[rows, :] = r
            ig_ref[rows, :] = ig
            a_ref[rows, :] = a
            im_ref[rows, :] = inv_mult
            b_scr[rows, :] = mult * (ig * u_ref[rows, :])

        def scan(p, hc):
            for t in range(rows_per * p, rows_per * p + rows_per):
                hc = a_ref[t:t + 1, :] * hc + b_scr[t:t + 1, :]
                hl_ref[t:t + 1, :] = hc
            return hc

        project(5)
        project(6)
        xl = xl_ref[...]
        u = cb_ref[...] + sum(cw_ref[k:k + 1, :] * _later(xl, halo[...], CONVW - 1 - k) for k in range(CONVW))
        halo[...] = xl[tm - 8:tm, :]
        u_ref[...] = u
        ub_scr[...] = u.astype(bf16)
        for g in range(NGRP):
            gs = slice(256 * g, 256 * g + 256)
            pr_scr[:, gs] = _dot(ub_scr[:, gs], wr_ref[g])
            pi_scr[:, gs] = _dot(ub_scr[:, gs], wi_ref[g])
        hc = hcar[...]
        gate_terms(0)
        for slot, ci in enumerate(later_chunks):
            project(ci)
            gate_terms(slot + 1)
            hc = scan(slot, hc)
        hcar[...] = scan(pieces - 1, hc)

        @pl.when(i == nt - 1)
        def _():
            finish()

    row = lambda w: pl.BlockSpec((tm, w), lambda i: (i, 0))
    col = lambda w: pl.BlockSpec((w, tm), lambda i: (0, i))
    full = lambda a: pl.BlockSpec(a.shape, lambda i: (0,) * a.ndim)
    big = lambda w, dt: pltpu.HBM((s, w), dt)
    tile = pltpu.VMEM((tm, LW), f32)
    return pl.pallas_call(
        body, name="fwd_fused", grid=(nt,),
        in_specs=[row(D), full(wt), row(384), full(wo_shard), full(conv_w), full(conv_b),
                  full(wr), full(wi), full(br), full(bi), full(lam)],
        out_specs=[col(D), col(KVW), col(KVW), row(D), row(D), row(D)] + [row(LW)] * 6 + [HBM],
        out_shape=[pltpu.HBM((D, s), bf16), pltpu.HBM((KVW, s), bf16), pltpu.HBM((KVW, s), bf16),
                   big(D, f32), big(D, f32), big(D, f32)] + [big(LW, f32)] * 6 + [pltpu.HBM((2 * D, D), bf16)],
        scratch_shapes=[pltpu.VMEM((WO_ROWS, D), bf16), pltpu.VMEM((8, LW), f32),
                        pltpu.VMEM((tm, LW), bf16), tile, tile, tile, pltpu.VMEM((1, LW), f32)] + _comm_sems(1),
        compiler_params=_params(("arbitrary",), 56),
    )(*_in_hbm(h, wt), tabs, wo_shard, conv_w, conv_b, wr, wi, br, bi, lam)


def _lru_bwd(u, hl, dhl, xl, r, ig, a, im, conv_w, wr, wi, lam, tm):
    s = u.shape[0]
    nt = s // tm
    pieces = 8
    rows_per = tm // pieces

    def body(u_ref, h_ref, hp_ref, dh_ref, x_ref, r_ref, ig_ref, a_ref, im_ref, cw_ref, wr_ref, wi_ref,
             lam_ref, dxl_ref, dwr_ref, dwi_ref, dbr_ref, dbi_ref, dlam_ref, dcb_ref, dcw_ref,
             l_scr, du_scr, dpr_scr, dpi_scr, lcar, dunext):
        t0 = pl.program_id(0)
        tile = nt - 1 - t0

        @pl.when(t0 == 0)
        def _():
            lcar[...] = jnp.zeros_like(lcar)
            dunext[...] = jnp.zeros_like(dunext)
            for ref in (dwr_ref, dwi_ref, dbr_ref, dbi_ref, dlam_ref, dcb_ref, dcw_ref):
                ref[...] = jnp.zeros_like(ref)

        lam = lam_ref[...]
        sp = _softplus(-lam)
        hp = jnp.where(tile > 0, hp_ref[...], 0.0)

        def scan(p, c):
            for t in range(rows_per * p + rows_per - 1, rows_per * p - 1, -1):
                lt = dh_ref[t:t + 1, :] + c
                l_scr[t:t + 1, :] = lt
                c = a_ref[t:t + 1, :] * lt
            return c

        def terms(p, sums):
            rows = slice(rows_per * p, rows_per * p + rows_per)
            lt, u, r, i, a, inv_mult = l_scr[rows, :], u_ref[rows, :], r_ref[rows, :], ig_ref[rows, :], \
                a_ref[rows, :], im_ref[rows, :]
            before = hp if p == 0 else h_ref[rows_per * p - 8:rows_per * p, :]
            hprev = _later(h_ref[rows, :], before, 1)
            x2 = -2.0 * LRU_C * r * sp
            mult = jnp.where(x2 > -0.02, -x2 * (1.0 + x2 * (0.5 + x2 * (1.0 / 6.0))), 1.0 - a * a) * inv_mult
            da = lt * hprev
            dmult = lt * (i * u)
            di = lt * mult * u
            du_scr[rows, :] = lt * mult * i
            dla = da * a - dmult * (a * a) * inv_mult
            dr = dla * (-LRU_C * sp)
            dpr = dr * r * (1.0 - r)
            dpi = di * i * (1.0 - i)
            dpr_scr[rows, :] = dpr.astype(bf16)
            dpi_scr[rows, :] = dpi.astype(bf16)
            col = lambda t: jnp.sum(t, axis=0, keepdims=True)
            return sums[0] + col(dla * (-LRU_C * r)), sums[1] + col(dpr), sums[2] + col(dpi)

        sums = (jnp.zeros((1, LW), f32),) * 3
        c = scan(pieces - 1, lcar[...])
        for p in range(pieces - 1, -1, -1):
            if p > 0:
                c = scan(p - 1, c)
            sums = terms(p, sums)
        lcar[...] = c
        dlam_ref[...] += sums[0]
        dbr_ref[...] += sums[1]
        dbi_ref[...] += sums[2]

        ub = u_ref[...].astype(bf16)
        dug = []
        for g in range(NGRP):
            gs = slice(256 * g, 256 * g + 256)
            dwr_ref[g] += _dot_tn(ub[:, gs], dpr_scr[:, gs])
            dwi_ref[g] += _dot_tn(ub[:, gs], dpi_scr[:, gs])
            dug.append(_dot_nt(dpr_scr[:, gs], wr_ref[g]) + _dot_nt(dpi_scr[:, gs], wi_ref[g]))
        du = du_scr[...] + jnp.concatenate(dug, axis=1)

        dcb_ref[...] += jnp.sum(du, axis=0, keepdims=True)
        x = x_ref[...]
        after = dunext[...]
        dxl = jnp.zeros_like(du)
        for k in range(CONVW):
            e = _earlier(du, after, CONVW - 1 - k)
            dxl = dxl + cw_ref[k:k + 1, :] * e
            dcw_ref[k:k + 1, :] += jnp.sum(e * x, axis=0, keepdims=True)
        dxl_ref[...] = dxl.astype(bf16)
        dunext[...] = du[0:8, :]

        @pl.when(t0 == nt - 1)
        def _():
            dlam_ref[...] = dlam_ref[...] * (-_sigmoid(-lam))

    rev = lambda i: (nt - 1 - i, 0)
    row = pl.BlockSpec((tm, LW), rev)
    prev8 = pl.BlockSpec((8, LW), lambda i: (jnp.maximum((nt - 1 - i) * (tm // 8) - 1, 0), 0))
    full = lambda a: pl.BlockSpec(a.shape, lambda i: (0,) * a.ndim)
    vec = pl.BlockSpec((1, LW), lambda i: (0, 0))
    bd = pl.BlockSpec((NGRP, 256, 256), lambda i: (0, 0, 0))
    return pl.pallas_call(
        body, name="lru_bwd", grid=(nt,),
        in_specs=[row, row, prev8, row, row, row, row, row, row, full(conv_w), full(wr), full(wi), full(lam)],
        out_specs=[row, bd, bd, vec, vec, vec, vec, pl.BlockSpec((CONVW, LW), lambda i: (0, 0))],
        out_shape=[pltpu.HBM((s, LW), bf16),
                   jax.ShapeDtypeStruct((NGRP, 256, 256), f32), jax.ShapeDtypeStruct((NGRP, 256, 256), f32),
                   jax.ShapeDtypeStruct((1, LW), f32), jax.ShapeDtypeStruct((1, LW), f32),
                   jax.ShapeDtypeStruct((1, LW), f32), jax.ShapeDtypeStruct((1, LW), f32),
                   jax.ShapeDtypeStruct((CONVW, LW), f32)],
        scratch_shapes=[pltpu.VMEM((tm, LW), f32), pltpu.VMEM((tm, LW), f32), pltpu.VMEM((tm, LW), bf16),
                        pltpu.VMEM((tm, LW), bf16), pltpu.VMEM((1, LW), f32), pltpu.VMEM((8, LW), f32)],
        compiler_params=_params(("arbitrary",), 56),
    )(*_in_hbm(u, hl, hl, dhl, xl, r, ig, a, im), conv_w, wr, wi, lam)


def _gated_norm(t, gate, gain):
    sg = _sigmoid(gate)
    silu = gate * sg
    p = t * silu
    rstd = lax.rsqrt(jnp.mean(p * p, axis=-1, keepdims=True) + EPS)
    ph = p * rstd
    return sg, silu, rstd, ph, ph * gain


def _gated_norm_bwd(dy, t, gate, gain, sg, silu, rstd, ph):
    w = dy * gain
    dp = rstd * (w - ph * jnp.mean(w * ph, axis=-1, keepdims=True))
    dgate = dp * t * (sg * (1.0 + gate * (1.0 - sg)))
    return jnp.sum(dy * ph, axis=0, keepdims=True), dp * silu, dgate


def _out_fwd_bwd(x, tgt, o, ga, hl, gl, again, lgain, fgain, wo, tm):
    s = x.shape[0]
    nt = s // tm

    def body(x_ref, t_ref, o_ref, ga_ref, hl_ref, gl_ref, ag_ref, lg_ref, fg_ref, wo_ref,
             dx2_ref, do_ref, dga_ref, dhl_ref, dgl_ref, dwo_ref, gfg_ref, gag_ref, glg_ref, loss_ref, acc):
        i = pl.program_id(0)

        @pl.when(i == 0)
        def _():
            acc[...] = jnp.zeros_like(acc)
            for ref in (gfg_ref, gag_ref, glg_ref, loss_ref):
                ref[...] = jnp.zeros_like(ref)

        oo = jnp.concatenate([o_ref[128 * j:128 * j + 128, :].T for j in range(D // 128)], axis=1)
        gga, hh, ggl = ga_ref[...], hl_ref[...], gl_ref[...]
        ag, lg, fg = ag_ref[...], lg_ref[...], fg_ref[...]
        sga, silua, ra, pah, ya = _gated_norm(oo, gga, ag)
        sgl, silul, rl, plh, yl = _gated_norm(hh, ggl, lg)
        yab, ylb = ya.astype(bf16), yl.astype(bf16)
        y = _dot(yab, wo_ref[0:D, :]) + _dot(ylb, wo_ref[D:2 * D, :])
        x2 = x_ref[...] + y
        r2 = lax.rsqrt(jnp.mean(x2 * x2, axis=-1, keepdims=True) + EPS)
        x2h = x2 * r2
        err = x2h * fg - t_ref[...]
        loss_ref[...] += 0.5 * jnp.sum(jnp.sum(err * err, axis=-1, keepdims=True) * (1.0 / D))
        dout = err * (1.0 / D)
        gfg_ref[...] += jnp.sum(dout * x2h, axis=0, keepdims=True)
        w = dout * fg
        dx2 = r2 * (w - x2h * jnp.mean(w * x2h, axis=-1, keepdims=True))
        dx2_ref[...] = dx2
        dyb = dx2.astype(bf16)
        acc[0:D, :] += _dot_tn(yab, dyb)
        acc[D:2 * D, :] += _dot_tn(ylb, dyb)
        dya = _dot_nt(dyb, wo_ref[0:D, :])
        dyl = _dot_nt(dyb, wo_ref[D:2 * D, :])
        gag, do, dga = _gated_norm_bwd(dya, oo, gga, ag, sga, silua, ra, pah)
        glg, dhl, dgl = _gated_norm_bwd(dyl, hh, ggl, lg, sgl, silul, rl, plh)
        gag_ref[...] += gag
        glg_ref[...] += glg
        dob = do.astype(bf16)
        for j in range(D // 128):
            do_ref[128 * j:128 * j + 128, :] = dob[:, 128 * j:128 * j + 128].T
        dga_ref[...] = dga.astype(bf16)
        dhl_ref[...] = dhl
        dgl_ref[...] = dgl.astype(bf16)

        @pl.when(i == nt - 1)
        def _():
            dwo_ref[...] = acc[...].astype(bf16)

    row = pl.BlockSpec((tm, D), lambda i: (i, 0))
    col = pl.BlockSpec((D, tm), lambda i: (0, i))
    vec = pl.BlockSpec((1, D), lambda i: (0, 0))
    mat = pl.BlockSpec((2 * D, D), lambda i: (0, 0))
    return pl.pallas_call(
        body, name="out_fwd_bwd", grid=(nt,),
        in_specs=[row, row, col, row, row, row] + [vec] * 3 + [mat],
        out_specs=[row, col, row, row, row] + [mat, vec, vec, vec, pl.BlockSpec((1, 128), lambda i: (0, 0))],
        out_shape=[pltpu.HBM((s, D), f32), pltpu.HBM((D, s), bf16),
                   pltpu.HBM((s, D), bf16), pltpu.HBM((s, D), f32),
                   pltpu.HBM((s, D), bf16), pltpu.HBM((2 * D, D), bf16),
                   jax.ShapeDtypeStruct((1, D), f32), jax.ShapeDtypeStruct((1, D), f32),
                   jax.ShapeDtypeStruct((1, D), f32), jax.ShapeDtypeStruct((1, 128), f32)],
        scratch_shapes=[pltpu.VMEM((2 * D, D), f32)],
        compiler_params=_params(("arbitrary",), 56),
    )(*_in_hbm(x, tgt, o, ga, hl, gl), again, lgain, fgain, *_in_hbm(wo))


def _bwd_in(x, dx2, dq, dk, dv, dga, dxl, dgl, ln_gain, wt, tabs, tm):
    s = x.shape[0]

    def body(x_ref, dx2_ref, dq_ref, dk_ref, dv_ref, dga_ref, dxl_ref, dgl_ref, g_ref, wt_ref,
             tab_ref, gx_ref, gln_ref, dzt_ref):
        @pl.when(pl.program_id(0) == 0)
        def _():
            gln_ref[...] = jnp.zeros_like(gln_ref)

        c, sa, sb = (t.T for t in _tables(tab_ref))
        for j in range(D // 128):
            js = slice(128 * j, 128 * j + 128)
            dzt_ref[js, :] = (_unrope_t(dq_ref[js, :], c, sa, sb) * (HD ** -0.5)).astype(bf16)
        for j in range(KVW // 128):
            js = slice(128 * j, 128 * j + 128)
            dzt_ref[D + 128 * j:D + 128 * j + 128, :] = _unrope_t(dk_ref[js, :], c, sa, sb).astype(bf16)
        dzt_ref[D + KVW:D + 2 * KVW, :] = dv_ref[...].astype(bf16)
        first = D + 2 * KVW
        dh = _dot_tn(dzt_ref[0:512, :], wt_ref[0:512, :])
        for ci in range(1, first // 512):
            dh = dh + _dot_tn(dzt_ref[512 * ci:512 * ci + 512, :], wt_ref[512 * ci:512 * ci + 512, :])
        for sec, ref in enumerate((dga_ref, dxl_ref, dgl_ref)):
            for j in range(D // 512):
                rows = slice(first + D * sec + 512 * j, first + D * sec + 512 * j + 512)
                dh = dh + _dot(ref[:, 512 * j:512 * j + 512], wt_ref[rows, :])
            for j in range(D // 128):
                dzt_ref[first + D * sec + 128 * j:first + D * sec + 128 * j + 128, :] = ref[:, 128 * j:128 * j + 128].T
        xx = x_ref[...]
        rstd = lax.rsqrt(jnp.mean(xx * xx, axis=-1, keepdims=True) + EPS)
        xh = xx * rstd
        gln_ref[...] += jnp.sum(dh * xh, axis=0, keepdims=True)
        w = dh * g_ref[...]
        gx_ref[...] = dx2_ref[...] + rstd * (w - xh * jnp.mean(w * xh, axis=-1, keepdims=True))

    row = lambda w: pl.BlockSpec((tm, w), lambda i: (i, 0))
    col = lambda w: pl.BlockSpec((w, tm), lambda i: (0, i))
    full = lambda a: pl.BlockSpec(a.shape, lambda i: (0, 0))
    return pl.pallas_call(
        body, name="bwd_in", grid=(s // tm,),
        in_specs=[row(D), row(D), col(D), col(KVW), col(KVW), row(D), row(D), row(D), full(ln_gain), full(wt),
                  row(384)],
        out_specs=[row(D), pl.BlockSpec((1, D), lambda i: (0, 0)), col(NIN)],
        out_shape=[pltpu.HBM((s, D), f32), jax.ShapeDtypeStruct((1, D), f32),
                   pltpu.HBM((NIN, s), bf16)],
        compiler_params=_params(("arbitrary",), 56),
    )(*_in_hbm(x, dx2, dq, dk, dv, dga, dxl, dgl), ln_gain, *_in_hbm(wt), tabs)


WT_TERMS = 4


def _dwt_scatter(dzt, h, small, tm):
    s = h.shape[0]
    nk = s // tm
    srows = small.shape[0] // NDEV
    last = NDEV - 1

    def body(order_ref, dz_ref, h_ref, sm_ref, lwt_ref, lsm_ref, acc, stage, given, relayed, send_sems, recv_sems,
             local_sem, sm_send, sm_recv, sm_local):
        j, k = pl.program_id(0), pl.program_id(1)
        x, y, c = _place()
        sibling = (x, y, 1 - c)
        near = (x ^ (1 - c), y ^ c)
        far = (x ^ c, y ^ (1 - c))
        sm_start, sm_finish = _scatter_ops([sm_ref], [lsm_ref], sm_send, sm_recv, sm_local)

        def send(step):
            if step == last - 1:
                dst, to = lwt_ref.at[1], sibling
            elif step % 2 == 0:
                dst, to = given.at[step // 2], sibling
            elif step == 1:
                dst, to = relayed, (*near, c)
            else:
                dst, to = lwt_ref.at[1 + step // 2], (*(near if step == 3 else far), c)
            return pltpu.make_async_remote_copy(
                src_ref=stage.at[step % 2], dst_ref=dst, send_sem=send_sems.at[step], recv_sem=recv_sems.at[step],
                device_id=to, device_id_type=MESH)

        def keep():
            return pltpu.make_async_copy(stage.at[last % 2], lwt_ref.at[0], local_sem)

        @pl.when((j == 0) & (k == 0))
        def _():
            sm_start()

        @pl.when(k == 0)
        def _():
            acc[...] = jnp.zeros_like(acc)

        acc[...] += _dot(dz_ref[...], h_ref[...])

        for step in range(NDEV):
            @pl.when((k == nk - 1) & (j == step))
            def _(step=step):
                if step >= 2:
                    send(step - 2).wait_send()
                if step % 2 == 1 and step < last:
                    send(step - 1).wait_recv()
                    total = acc[...] + given[step // 2].astype(f32)
                    if step == 5:
                        send(1).wait_recv()
                        total = total + relayed[...].astype(f32)
                    stage[step % 2] = total.astype(bf16)
                else:
                    stage[step % 2] = acc[...].astype(bf16)
                if step < last:
                    send(step).start()
                else:
                    keep().start()
                    send(last - 1).wait_send()
                    for peer_step in (3, 5, last - 1):
                        send(peer_step).wait_recv()
                    keep().wait()
                    sm_finish()

    x, y, c = _place()
    dest = lambda chip, cc: 4 * chip[0] + 2 * chip[1] + cc
    near, far, diag = (x ^ (1 - c), y ^ c), (x ^ c, y ^ (1 - c)), (1 - x, 1 - y)
    order = jnp.stack([dest(diag, 1 - c), dest(diag, c), dest(far, 1 - c), dest(near, c),
                       dest(near, 1 - c), dest(far, c), dest((x, y), 1 - c), dest((x, y), c)])
    return pl.pallas_call(
        body, name="dwt_scatter",
        grid_spec=pltpu.PrefetchScalarGridSpec(
            num_scalar_prefetch=1, grid=(NDEV, nk),
            in_specs=[pl.BlockSpec((WT_ROWS, tm), lambda j, k, order: (order[j], k)),
                      pl.BlockSpec((tm, D), lambda j, k, order: (k, 0)), HBM],
            out_specs=[HBM, HBM],
            scratch_shapes=[pltpu.VMEM((WT_ROWS, D), f32), pltpu.VMEM((2, WT_ROWS, D), bf16),
                            pltpu.VMEM((3, WT_ROWS, D), bf16), pltpu.VMEM((WT_ROWS, D), bf16),
                            pltpu.SemaphoreType.DMA((last,)), pltpu.SemaphoreType.DMA((last,)),
                            pltpu.SemaphoreType.DMA(())] + _comm_sems(1)),
        out_shape=[pltpu.HBM((WT_TERMS, WT_ROWS, D), bf16), pltpu.HBM((NDEV, srows, D), f32)],
        compiler_params=_params(("arbitrary", "arbitrary"), 48),
    )(order, *_in_hbm(dzt, h, small))


def _diag_blocks(bd):
    eye = jnp.eye(4, dtype=bd.dtype)
    return jnp.einsum('gjckd,jk->gjcd', bd.reshape(NGRP, 4, HD, 4, HD), eye).reshape(NQ, HD, HD)


def _sequence_step(x, h, tgt, wt, wo_shard, conv_w, p):
    s = x.shape[0]
    tm = min(256, s)
    tabs = _rope_tables(s)
    wr, wi = _block_diag(p["w_rgate"]), _block_diag(p["w_igate"])
    sinks = p["sinks"].reshape(NQ)
    qt, kt, vt, ga, xl, gl, u, hl, r, ig, a, im, wo = _fwd_fused(
        h, wt, tabs, wo_shard, conv_w, p["conv_b"], wr, wi, p["b_rgate"], p["b_igate"], p["lru_lambda"], tm)
    ot = _attn_fwd_t(qt, kt, vt, sinks)
    dx2, dot, dga, dhl, dgl, dwo, g_fg, g_ag, g_lg, loss = _out_fwd_bwd(
        x, tgt, ot, ga, hl, gl, p["attn_out_gain"], p["lru_out_gain"], p["final_gain"], wo, tm)
    dqt, dkt, dvt, dsink, land_wo = _attn_bwd_t(qt, kt, vt, dot, sinks, dwo)
    dxl, dwr, dwi, dbr, dbi, dlam, dcb, dcw = _lru_bwd(u, hl, dhl, xl, r, ig, a, im, conv_w, wr, wi, p["lru_lambda"], tm)
    gx, g_ln, dzt = _bwd_in(x, dx2, dqt, dkt, dvt, dga, dxl, dgl, p["ln_gain"], wt, tabs, tm)
    small = dict(ln_gain=g_ln, sinks=dsink.reshape(NQ, BLK).sum(axis=1)[None], conv_w=dcw, conv_b=dcb,
                 w_rgate=_diag_blocks(dwr), b_rgate=dbr, w_igate=_diag_blocks(dwi), b_igate=dbi, lru_lambda=dlam,
                 attn_out_gain=g_ag, lru_out_gain=g_lg, final_gain=g_fg)
    land_wt, land_sm = _dwt_scatter(dzt, h, _pack_small(small, loss), min(2048, s))
    return gx, land_wt, land_wo, land_sm


def _gather_weights(wt_shard, conv_blk, x, ln_gain, tm):
    s = x.shape[0]

    def body(wt_ref, cw_ref, g_ref, x_ref, wt_all, cw_all, h_ref, stage, xbuf, hbuf, send_sems, recv_sems, local_sems):
        stage[...] = wt_ref[...].astype(bf16)
        start, finish = _relay_gather_ops([stage, cw_ref], [wt_all, cw_all], send_sems, recv_sems, local_sems)
        start()
        gain = g_ref[...]
        for i in range(s // tm):
            rows = pl.ds(i * tm, tm)
            pltpu.sync_copy(x_ref.at[rows, :], xbuf)
            xx = xbuf[...]
            rstd = lax.rsqrt(jnp.mean(xx * xx, axis=-1, keepdims=True) + EPS)
            hbuf[...] = (xx * rstd * gain).astype(bf16)
            pltpu.sync_copy(hbuf, h_ref.at[rows, :])
        finish()

    vmem = pl.BlockSpec(memory_space=pltpu.VMEM)
    return pl.pallas_call(
        body, name="gather_weights",
        in_specs=[vmem, vmem, vmem, HBM], out_specs=[HBM, HBM, HBM],
        out_shape=[pltpu.HBM((NIN, D), bf16), pltpu.HBM((NDEV * 8, 128), f32), pltpu.HBM((s, D), bf16)],
        scratch_shapes=[pltpu.VMEM((WT_ROWS, D), bf16), pltpu.VMEM((tm, D), f32), pltpu.VMEM((tm, D), bf16)]
        + _comm_sems(2),
        compiler_params=pltpu.CompilerParams(vmem_limit_bytes=32 * MIB),
    )(wt_shard, conv_blk, ln_gain, *_in_hbm(x))


def _adam_math(w, g, m, v):
    m2 = ADAM_B1 * m + (1.0 - ADAM_B1) * g
    v2 = ADAM_B2 * v + (1.0 - ADAM_B2) * (g * g)
    m_hat = m2 / (1.0 - ADAM_B1 ** ADAM_STEP)
    v_hat = v2 / (1.0 - ADAM_B2 ** ADAM_STEP)
    delta = -ADAM_LR * (m_hat / (jnp.sqrt(v_hat) + ADAM_EPS) + ADAM_WD * w)
    return delta, m2, v2


def _reduce_adamw(land, w, m, v, tr, name, land_small=None):
    terms, rows, cols = land.shape
    nsteps = rows // tr
    hosts = land_small is not None

    def body(l_ref, w_ref, m_ref, v_ref, *rest):
        if hosts:
            sm_ref, g_ref, d_ref, m2_ref, v2_ref, rep_all, tail_ref, stage, send_sems, recv_sems, local_sems = rest
            start, finish = _gather_ops([stage], [rep_all], send_sems, recv_sems, local_sems)

            @pl.when(pl.program_id(0) == 0)
            def _():
                acc = sm_ref[0]
                for dev in range(1, NDEV):
                    acc = acc + sm_ref[dev]
                stage[...] = acc[0:SMALL_PER]
                tail_ref[...] = acc[SMALL_PER:]
                start()
        else:
            g_ref, d_ref, m2_ref, v2_ref = rest
        g = l_ref[0].astype(f32)
        for t in range(1, terms):
            g = g + l_ref[t].astype(f32)
        g_ref[...] = g
        d_ref[...], m2_ref[...], v2_ref[...] = _adam_math(w_ref[...], g, m_ref[...], v_ref[...])
        if hosts:
            @pl.when(pl.program_id(0) == nsteps - 1)
            def _():
                finish()

    blk = pl.BlockSpec((tr, cols), lambda i: (i, 0))
    in_specs = [pl.BlockSpec((terms, tr, cols), lambda i: (0, i, 0))] + [blk] * 3
    out_specs, out_shape, scratch, extra = [blk] * 4, [jax.ShapeDtypeStruct((rows, cols), f32)] * 4, [], ()
    if hosts:
        in_specs += [pl.BlockSpec(land_small.shape, lambda i: (0, 0, 0))]
        out_specs += [HBM, pl.BlockSpec((8, D), lambda i: (0, 0))]
        out_shape += [pltpu.HBM((SMALL_ROWS, D), f32), jax.ShapeDtypeStruct((8, D), f32)]
        scratch = [pltpu.VMEM((SMALL_PER, D), f32)] + _comm_sems(1)
        extra = (land_small,)
    return pl.pallas_call(
        body, name=name, grid=(nsteps,),
        in_specs=in_specs, out_specs=out_specs, out_shape=out_shape, scratch_shapes=scratch,
        compiler_params=_params(("arbitrary",), 32),
    )(*_in_hbm(land), w, m, v, *extra)


VEC_NAMES = ("ln_gain", "conv_b", "b_rgate", "b_igate", "lru_lambda", "attn_out_gain", "lru_out_gain", "final_gain")
ROW_RGATE, ROW_IGATE, ROW_VEC, ROW_SINKS = 0, 64, 128, 136
LOSS_LANE = NQ


def _adamw_small(g_rep, g_conv, w, m, v):
    names = list(VEC_NAMES) + ["sinks", "conv_w", "w_rgate", "w_igate"]
    ins = [g_rep, g_conv] + [d[k] for k in names for d in (w, m, v)]

    def body(*refs):
        g_ref, gc_ref = refs[0], refs[1]
        in_refs = refs[2:2 + 3 * len(names)]
        out_refs = refs[2 + 3 * len(names):]

        def update(j, g, at=None):
            w_ref, m_ref, v_ref = in_refs[3 * j:3 * j + 3]
            outs = out_refs[4 * j:4 * j + 4]
            pick = (lambda r: r[...]) if at is None else (lambda r: r[at])
            res = (g,) + _adam_math(pick(w_ref), g, pick(m_ref), pick(v_ref))
            for o_ref, val in zip(outs, res):
                if at is None:
                    o_ref[...] = val
                else:
                    o_ref[at] = val

        for j in range(len(VEC_NAMES)):
            update(j, g_ref[ROW_VEC + j:ROW_VEC + j + 1, :])
        update(len(VEC_NAMES), g_ref[ROW_SINKS:ROW_SINKS + 1, 0:NQ])
        update(len(VEC_NAMES) + 1, gc_ref[...], at=0)
        for gi, row0 in ((len(VEC_NAMES) + 2, ROW_RGATE), (len(VEC_NAMES) + 3, ROW_IGATE)):
            for nb in range(NQ):
                update(gi, g_ref[row0:row0 + HD, HD * nb:HD * nb + HD], at=(0, nb))

    vmem = pl.BlockSpec(memory_space=pltpu.VMEM)
    out_shape = [jax.ShapeDtypeStruct(w[k].shape, f32) for k in names for _ in range(4)]
    outs = pl.pallas_call(
        body, name="adamw_small",
        in_specs=[vmem] * len(ins), out_specs=[vmem] * len(out_shape), out_shape=out_shape,
        compiler_params=pltpu.CompilerParams(vmem_limit_bytes=32 * MIB),
    )(*ins)
    return {k: tuple(outs[4 * j:4 * j + 4]) for j, k in enumerate(names)}


def _pack_small(small, loss):
    gate = lambda g: g.transpose(1, 0, 2).reshape(HD, NQ * HD)
    row_s = jnp.concatenate([small["sinks"], loss[:, LOSS_LANE:128], jnp.zeros((1, D - 128), f32)], axis=1)
    rep = jnp.concatenate([gate(small["w_rgate"]), gate(small["w_igate"])] + [small[k] for k in VEC_NAMES]
                          + [row_s, jnp.zeros((SMALL_ROWS - ROW_SINKS - 1, D), f32)], axis=0)
    conv = small["conv_w"].reshape(CONVW, NDEV, 128).transpose(1, 0, 2)
    conv = jnp.pad(conv, ((0, 0), (0, 8 - CONVW), (0, D - 128)))
    return jnp.concatenate([rep.reshape(NDEV, SMALL_PER, D), conv], axis=1).reshape(NDEV * (SMALL_PER + 8), D)


def kernel(x, ln_gain, w_in, sinks, conv_w, conv_b, w_rgate, b_rgate, w_igate, b_igate, lru_lambda, attn_out_gain, lru_out_gain, w_out, final_gain, loss_target, m_ln_gain, m_w_in, m_sinks, m_conv_w, m_conv_b, m_w_rgate, m_b_rgate, m_w_igate, m_b_igate, m_lru_lambda, m_attn_out_gain, m_lru_out_gain, m_w_out, m_final_gain, v_ln_gain, v_w_in, v_sinks, v_conv_w, v_conv_b, v_w_rgate, v_b_rgate, v_w_igate, v_b_igate, v_lru_lambda, v_attn_out_gain, v_lru_out_gain, v_w_out, v_final_gain):
    w = dict(ln_gain=ln_gain, sinks=sinks, conv_w=conv_w, conv_b=conv_b, w_rgate=w_rgate, b_rgate=b_rgate,
             w_igate=w_igate, b_igate=b_igate, lru_lambda=lru_lambda, attn_out_gain=attn_out_gain,
             lru_out_gain=lru_out_gain, final_gain=final_gain.reshape(1, D))
    m = dict(ln_gain=m_ln_gain, sinks=m_sinks, conv_w=m_conv_w, conv_b=m_conv_b, w_rgate=m_w_rgate,
             b_rgate=m_b_rgate, w_igate=m_w_igate, b_igate=m_b_igate, lru_lambda=m_lru_lambda,
             attn_out_gain=m_attn_out_gain, lru_out_gain=m_lru_out_gain, final_gain=m_final_gain.reshape(1, D))
    v = dict(ln_gain=v_ln_gain, sinks=v_sinks, conv_w=v_conv_w, conv_b=v_conv_b, w_rgate=v_w_rgate,
             b_rgate=v_b_rgate, w_igate=v_w_igate, b_igate=v_b_igate, lru_lambda=v_lru_lambda,
             attn_out_gain=v_attn_out_gain, lru_out_gain=v_lru_out_gain, final_gain=v_final_gain.reshape(1, D))

    conv_blk = jnp.pad(conv_w[0], ((0, 8 - CONVW), (0, 0)))
    wt, cw_all, h = _gather_weights(w_in[0].T, conv_blk, x[0], ln_gain, min(512, x.shape[1]))
    conv_full = cw_all.reshape(NDEV, 8, 128)[:, 0:CONVW].transpose(1, 0, 2).reshape(CONVW, LW)

    p = {k: (w[k][0] if k in ("w_rgate", "w_igate") else w[k]) for k in w if k != "conv_w"}
    gx, land_wt, land_wo, land_sm = _sequence_step(x[0], h, loss_target[0], wt, w_out[0], conv_full, p)

    *wins, g_rep, g_tail = _reduce_adamw(land_wt, w_in[0].T, m_w_in[0].T, v_w_in[0].T, 192, "adamw_w_in", land_sm)
    g_conv = g_tail[0:CONVW, 0:128]
    g_win, d_win, m_win, v_win = (t.T for t in wins)
    g_wo, d_wo, m_wo, v_wo = _reduce_adamw(land_wo, w_out[0], m_w_out[0], v_w_out[0], 256, "adamw_w_out")
    res = _adamw_small(g_rep, g_conv, w, m, v)
    res["w_in"] = tuple(t[None] for t in (g_win, d_win, m_win, v_win))
    res["w_out"] = tuple(t[None] for t in (g_wo, d_wo, m_wo, v_wo))
    res["final_gain"] = tuple(t.reshape(D) for t in res["final_gain"])

    order = ("ln_gain", "w_in", "sinks", "conv_w", "conv_b", "w_rgate", "b_rgate", "w_igate", "b_igate",
             "lru_lambda", "attn_out_gain", "lru_out_gain", "w_out", "final_gain")
    total_loss = g_rep[ROW_SINKS, LOSS_LANE]
    return (total_loss, gx[None]) + tuple(res[k][i] for i in range(4) for k in order)
```

```python
import jax
import jax.numpy as jnp
from jax import lax
from jax.experimental import pallas as pl
from jax.experimental.pallas import tpu as pltpu

f32 = jnp.float32
bf16 = jnp.bfloat16

D = 1024
HD = 64
NQ = 16
NKV = 4
GROUP = NQ // NKV
KVW = NKV * HD
BLK = 128
ROT = 16
THETA = 500000.0
NEG = -1e30
LW = 1024
NGRP = 4
CONVW = 4
LRU_C = 8.0
NIN = 4608
EPS = 1e-6
NDEV = 8
WT_ROWS = NIN // NDEV
WO_ROWS = 2 * D // NDEV
SMALL_ROWS = 192
SMALL_PER = SMALL_ROWS // NDEV

ADAM_LR = 0.001
ADAM_B1 = 0.9
ADAM_B2 = 0.999
ADAM_EPS = 1e-08
ADAM_WD = 0.01
ADAM_STEP = 10

NT = (((1,), (1,)), ((), ()))
TN = (((0,), (0,)), ((), ()))
MESH = pl.DeviceIdType.MESH
MIB = 1024 * 1024


def _dot(a, b):
    return jnp.dot(a, b, preferred_element_type=f32)


def _dot_nt(a, b):
    return lax.dot_general(a, b, NT, preferred_element_type=f32)


def _dot_tn(a, b):
    return lax.dot_general(a, b, TN, preferred_element_type=f32)


def _params(sem, vmem_mib):
    return pltpu.CompilerParams(dimension_semantics=sem, vmem_limit_bytes=vmem_mib * MIB)


def _sigmoid(x):
    return 0.5 * jnp.tanh(0.5 * x) + 0.5


def _softplus(x):
    return jnp.maximum(x, 0.0) + jnp.log(1.0 + jnp.exp(-jnp.abs(x)))


def _rope_tables(s):
    pos = jnp.arange(s, dtype=f32)
    inv_freq = THETA ** (-jnp.arange(0, ROT, 2, dtype=f32) / ROT)
    ang = pos[:, None] * inv_freq[None, :]
    cs = jnp.concatenate([jnp.cos(ang) - 1.0, jnp.sin(ang)], axis=1)
    d = jnp.arange(128) % HD
    j = jnp.arange(ROT)[:, None]
    pick_c = ((d < ROT) & (j == d % (ROT // 2))).astype(f32)
    pick_sa = ((d >= ROT // 2) & (d < ROT) & (j == d)).astype(f32)
    pick_sb = -((d < ROT // 2) & (j == d + ROT // 2)).astype(f32)
    picks = jnp.concatenate([pick_c, pick_sa, pick_sb], axis=1)
    ones = jnp.concatenate([jnp.ones((1, 128), f32), jnp.zeros((1, 256), f32)], axis=1)
    return jnp.dot(cs, picks, precision=lax.Precision.HIGHEST) + ones


def _tables(tab_ref):
    return tab_ref[:, 0:128], tab_ref[:, 128:256], tab_ref[:, 256:384]


def _rope(t, c, sa, sb):
    return t * c + pltpu.roll(t, 8, 1) * sa + pltpu.roll(t, 120, 1) * sb


def _unrope_t(dr, c, sa, sb):
    return dr * c + pltpu.roll(dr * sa, 120, 0) + pltpu.roll(dr * sb, 8, 0)


def _place():
    return lax.axis_index("x"), lax.axis_index("y"), lax.axis_index("c")


def _gather_ops(mine_refs, out_refs, send_sems, recv_sems, local_sems):
    n = len(mine_refs)
    x, y, c = _place()
    me, sibling = (x, y, c), (x, y, 1 - c)
    chips = [(1 - x, y), (x, 1 - y), (1 - x, 1 - y)]

    def rows(a, dev):
        m = mine_refs[a].shape[0]
        return out_refs[a].at[pl.ds((4 * dev[0] + 2 * dev[1] + dev[2]) * m, m), :]

    def copy(a, k, block, to, own=False):
        return pltpu.make_async_remote_copy(
            src_ref=mine_refs[a] if own else rows(a, block), dst_ref=rows(a, block),
            send_sem=send_sems.at[a, k], recv_sem=recv_sems.at[a, k], device_id=to, device_id_type=MESH)

    def local(a):
        return pltpu.make_async_copy(mine_refs[a], rows(a, me), local_sems.at[a])

    def first(a):
        return [copy(a, 0, me, sibling, own=True)] + [copy(a, 1 + j, me, (*chip, c), own=True)
                                                      for j, chip in enumerate(chips)]

    def start():
        for a in range(n):
            local(a).start()
            for cp in first(a):
                cp.start()

    def finish():
        for j, chip in enumerate(chips):
            for a in range(n):
                copy(a, 1 + j, (*chip, c), me).wait_recv()
                copy(a, 4 + j, (*chip, c), sibling).start()
        for a in range(n):
            copy(a, 0, sibling, me).wait_recv()
            for j, chip in enumerate(chips):
                copy(a, 4 + j, (*chip, 1 - c), me).wait_recv()
        for a in range(n):
            for cp in first(a) + [copy(a, 4 + j, (*chip, c), sibling) for j, chip in enumerate(chips)]:
                cp.wait_send()
            local(a).wait()

    return start, finish


def _relay_gather_ops(mine_refs, out_refs, send_sems, recv_sems, local_sems):
    n = len(mine_refs)
    x, y, c = _place()
    me, sibling = (x, y, c), (x, y, 1 - c)
    near = (x ^ (1 - c), y ^ c)
    far = (x ^ c, y ^ (1 - c))
    diag = (1 - x, 1 - y)

    def rows(a, dev):
        m = mine_refs[a].shape[0]
        return out_refs[a].at[pl.ds((4 * dev[0] + 2 * dev[1] + dev[2]) * m, m), :]

    def copy(a, k, block, to, own=False):
        return pltpu.make_async_remote_copy(
            src_ref=mine_refs[a] if own else rows(a, block), dst_ref=rows(a, block),
            send_sem=send_sems.at[a, k], recv_sem=recv_sems.at[a, k], device_id=to, device_id_type=MESH)

    def local(a):
        return pltpu.make_async_copy(mine_refs[a], rows(a, me), local_sems.at[a])

    def sends(a):
        return [copy(a, 0, me, sibling, own=True), copy(a, 1, me, (*near, c), own=True),
                copy(a, 2, me, (*far, c), own=True), copy(a, 3, (*near, c), (*far, c)),
                copy(a, 4, (*near, c), sibling), copy(a, 5, (*far, c), sibling), copy(a, 6, (*diag, c), sibling)]

    def arrivals(a):
        return [copy(a, 0, sibling, me), copy(a, 1, (*near, c), me), copy(a, 2, (*far, c), me),
                copy(a, 3, (*diag, c), me), copy(a, 4, (*far, 1 - c), me), copy(a, 5, (*near, 1 - c), me),
                copy(a, 6, (*diag, 1 - c), me)]

    def start():
        for a in range(n):
            local(a).start()
            for cp in sends(a)[0:3]:
                cp.start()

    def finish():
        for first, then in ((1, (3, 4)), (2, (5,)), (3, (6,))):
            for a in range(n):
                arrivals(a)[first].wait_recv()
                for k in then:
                    sends(a)[k].start()
        for a in range(n):
            for k in (0, 4, 5, 6):
                arrivals(a)[k].wait_recv()
        for a in range(n):
            for cp in sends(a):
                cp.wait_send()
            local(a).wait()

    return start, finish


def _scatter_ops(src_refs, land_refs, send_sems, recv_sems, local_sems):
    n = len(src_refs)
    x, y, c = _place()
    my = 4 * x + 2 * y + c

    def peer(k):
        return x ^ (k >> 2), y ^ ((k >> 1) & 1), c ^ (k & 1)

    def piece(a, dev):
        m = src_refs[a].shape[0] // NDEV
        return src_refs[a].at[pl.ds(dev * m, m), :]

    def local(a):
        return pltpu.make_async_copy(piece(a, my), land_refs[a].at[my], local_sems.at[a])

    def send(a, k):
        px, py, pc = peer(k)
        return pltpu.make_async_remote_copy(
            src_ref=piece(a, 4 * px + 2 * py + pc), dst_ref=land_refs[a].at[my],
            send_sem=send_sems.at[a, k - 1], recv_sem=recv_sems.at[a, k - 1],
            device_id=(px, py, pc), device_id_type=MESH)

    def arrival(a, k):
        px, py, pc = peer(k)
        return pltpu.make_async_remote_copy(
            src_ref=piece(a, my), dst_ref=land_refs[a].at[4 * px + 2 * py + pc],
            send_sem=send_sems.at[a, k - 1], recv_sem=recv_sems.at[a, k - 1],
            device_id=(px, py, pc), device_id_type=MESH)

    def start():
        for a in range(n):
            local(a).start()
        for k in range(1, NDEV):
            for a in range(n):
                send(a, k).start()

    def finish():
        for k in range(1, NDEV):
            for a in range(n):
                send(a, k).wait_send()
        for k in range(1, NDEV):
            for a in range(n):
                arrival(a, k).wait_recv()
        for a in range(n):
            local(a).wait()

    return start, finish


def _in_hbm(*arrays):
    return tuple(pltpu.with_memory_space_constraint(a, pltpu.HBM) for a in arrays)


def _comm_sems(n):
    return [pltpu.SemaphoreType.DMA((n, 7)), pltpu.SemaphoreType.DMA((n, 7)), pltpu.SemaphoreType.DMA((n,))]


HBM = pl.BlockSpec(memory_space=pltpu.HBM)


def _sink_rows(sinks):
    return jnp.repeat(sinks.reshape(NKV, GROUP), BLK, axis=1)


def _band_softmax(s2_ref, ls, prev_offset, sink_row):
    jj = lax.broadcasted_iota(jnp.int32, (BLK, BLK), 0)
    ii = lax.broadcasted_iota(jnp.int32, (BLK, BLK), 1)
    from_prev = jj > ii
    sc = jnp.where(from_prev, s2_ref[0:BLK, ls] + prev_offset, s2_ref[BLK:2 * BLK, ls])
    m = jnp.maximum(jnp.max(sc, axis=0, keepdims=True), sink_row)
    p = jnp.exp(sc - m)
    es = jnp.exp(sink_row - m)
    inv = 1.0 / (jnp.sum(p, axis=0, keepdims=True) + es)
    return from_prev, p * inv, es * inv


def _put_split(dst_ref, ls, t, from_prev):
    t = t.astype(bf16)
    zero = jnp.zeros_like(t)
    dst_ref[0:BLK, ls] = jnp.where(from_prev, t, zero)
    dst_ref[BLK:2 * BLK, ls] = jnp.where(from_prev, zero, t)


def _heads_side_by_side(ref, h):
    return jnp.concatenate([ref[HD * (GROUP * h + g):HD * (GROUP * h + g) + HD, :] for g in range(GROUP)], axis=1)


def _kv_specs_t():
    prev = pl.BlockSpec((KVW, BLK), lambda n: (0, jnp.maximum(n - 1, 0)))
    cur = pl.BlockSpec((KVW, BLK), lambda n: (0, n))
    return [prev, cur, prev, cur]


def _attn_fwd_t(qt, kt, vt, sinks):
    s = qt.shape[1]

    def body(sink_ref, q_ref, kp_ref, kc_ref, vp_ref, vc_ref, o_ref, s2_scr, pn2_scr):
        n = pl.program_id(0)
        off = jnp.where(n > 0, 0.0, NEG)

        def scores(h):
            hs = slice(HD * h, HD * h + HD)
            kh = jnp.concatenate([kp_ref[hs, :], kc_ref[hs, :]], axis=1)
            s2_scr[h % 2] = _dot_tn(kh, _heads_side_by_side(q_ref, h))

        def probs(h):
            for g in range(GROUP):
                ls = slice(BLK * g, BLK * g + BLK)
                from_prev, pn, _ = _band_softmax(s2_scr.at[h % 2], ls, off, sink_ref[h:h + 1, ls])
                _put_split(pn2_scr.at[h % 2], ls, pn, from_prev)

        def outputs(h):
            hs = slice(HD * h, HD * h + HD)
            vh = jnp.concatenate([vp_ref[hs, :], vc_ref[hs, :]], axis=1)
            og = _dot(vh, pn2_scr[h % 2])
            for g in range(GROUP):
                a = GROUP * h + g
                o_ref[HD * a:HD * a + HD, :] = og[:, BLK * g:BLK * g + BLK]

        scores(0)
        for h in range(NKV):
            if h + 1 < NKV:
                scores(h + 1)
            probs(h)
            outputs(h)

    return pl.pallas_call(
        body, name="attn_fwd", grid=(s // BLK,),
        in_specs=[pl.BlockSpec((NKV, GROUP * BLK), lambda n: (0, 0)), pl.BlockSpec((D, BLK), lambda n: (0, n))]
        + _kv_specs_t(),
        out_specs=pl.BlockSpec((D, BLK), lambda n: (0, n)),
        out_shape=pltpu.HBM((D, s), f32),
        scratch_shapes=[pltpu.VMEM((2, 2 * BLK, GROUP * BLK), f32), pltpu.VMEM((2, 2 * BLK, GROUP * BLK), bf16)],
        compiler_params=_params(("arbitrary",), 32),
    )(_sink_rows(sinks), *_in_hbm(qt, kt, kt, vt, vt))


def _attn_bwd_t(qt, kt, vt, dot, sinks, dwo):
    s = qt.shape[1]
    nb = s // BLK

    def body(sink_ref, q_ref, do_ref, kp_ref, kc_ref, vp_ref, vc_ref, dwo_ref, dq_ref, dk_ref, dv_ref, ds_ref,
             land_ref, dk_hold, dv_hold, s2_scr, dp2_scr, pn2_scr, ds2_scr, send_sems, recv_sems, local_sems):
        n = pl.program_id(0)
        start, finish = _scatter_ops([dwo_ref], [land_ref], send_sems, recv_sems, local_sems)

        @pl.when(n == 0)
        def _():
            start()
            dk_hold[...] = jnp.zeros_like(dk_hold)
            dv_hold[...] = jnp.zeros_like(dv_hold)
            ds_ref[...] = jnp.zeros_like(ds_ref)

        @pl.when(n < nb)
        def _():
            off = jnp.where(n > 0, 0.0, NEG)

            def scores(h):
                hs = slice(HD * h, HD * h + HD)
                kh = jnp.concatenate([kp_ref[hs, :], kc_ref[hs, :]], axis=1)
                vh = jnp.concatenate([vp_ref[hs, :], vc_ref[hs, :]], axis=1)
                s2_scr[h % 2] = _dot_tn(kh, _heads_side_by_side(q_ref, h))
                dp2_scr[h % 2] = _dot_tn(vh, _heads_side_by_side(do_ref, h))

            def softmax_bwd(h):
                for g in range(GROUP):
                    ls = slice(BLK * g, BLK * g + BLK)
                    from_prev, pn, ps = _band_softmax(s2_scr.at[h % 2], ls, off, sink_ref[h:h + 1, ls])
                    dp = jnp.where(from_prev, dp2_scr[h % 2, 0:BLK, ls], dp2_scr[h % 2, BLK:2 * BLK, ls])
                    dsum = jnp.sum(pn * dp, axis=0, keepdims=True)
                    ds_ref[h:h + 1, ls] += -ps * dsum
                    _put_split(pn2_scr.at[h % 2], ls, pn, from_prev)
                    _put_split(ds2_scr.at[h % 2], ls, pn * (dp - dsum), from_prev)

            def grads(h):
                hs = slice(HD * h, HD * h + HD)
                kh = jnp.concatenate([kp_ref[hs, :], kc_ref[hs, :]], axis=1)
                dqg = _dot(kh, ds2_scr[h % 2])
                for g in range(GROUP):
                    a = GROUP * h + g
                    dq_ref[HD * a:HD * a + HD, :] = dqg[:, BLK * g:BLK * g + BLK]
                dkh = _dot_nt(_heads_side_by_side(q_ref, h), ds2_scr[h % 2])
                dvh = _dot_nt(_heads_side_by_side(do_ref, h), pn2_scr[h % 2])
                dk_ref[hs, :] = dk_hold[hs, :] + dkh[:, 0:BLK]
                dv_ref[hs, :] = dv_hold[hs, :] + dvh[:, 0:BLK]
                dk_hold[hs, :] = dkh[:, BLK:2 * BLK]
                dv_hold[hs, :] = dvh[:, BLK:2 * BLK]

            scores(0)
            for h in range(NKV):
                if h + 1 < NKV:
                    scores(h + 1)
                softmax_bwd(h)
                grads(h)

        @pl.when(n == nb)
        def _():
            dk_ref[...] = dk_hold[...]
            dv_ref[...] = dv_hold[...]
            finish()

    blk = pl.BlockSpec((D, BLK), lambda n: (0, jnp.minimum(n, nb - 1)))
    late = pl.BlockSpec((KVW, BLK), lambda n: (0, jnp.maximum(n - 1, 0)))
    whole = pl.BlockSpec((NKV, GROUP * BLK), lambda n: (0, 0))
    kv = [pl.BlockSpec((KVW, BLK), lambda n: (0, jnp.clip(n - 1, 0, nb - 1))),
          pl.BlockSpec((KVW, BLK), lambda n: (0, jnp.minimum(n, nb - 1)))]
    return pl.pallas_call(
        body, name="attn_bwd", grid=(nb + 1,),
        in_specs=[whole, blk, blk] + kv + kv + [HBM],
        out_specs=[blk, late, late, whole, HBM],
        out_shape=[pltpu.HBM((D, s), f32), pltpu.HBM((KVW, s), f32), pltpu.HBM((KVW, s), f32),
                   jax.ShapeDtypeStruct((NKV, GROUP * BLK), f32), pltpu.HBM((NDEV, WO_ROWS, D), bf16)],
        scratch_shapes=[pltpu.VMEM((KVW, BLK), f32), pltpu.VMEM((KVW, BLK), f32)]
        + [pltpu.VMEM((2, 2 * BLK, GROUP * BLK), f32)] * 2 + [pltpu.VMEM((2, 2 * BLK, GROUP * BLK), bf16)] * 2
        + _comm_sems(1),
        compiler_params=_params(("arbitrary",), 48),
    )(_sink_rows(sinks), *_in_hbm(qt, dot, kt, kt, vt, vt, dwo))


def _block_diag(w):
    w4 = w.reshape(NGRP, 4, HD, HD)
    eye = jnp.eye(4, dtype=w.dtype)
    return jnp.einsum('gjcd,jk->gjckd', w4, eye).reshape(NGRP, 256, 256).astype(bf16)


def _gate_terms(pr, pi, br, bi, sp):
    r = _sigmoid(pr + br)
    i = _sigmoid(pi + bi)
    la = -LRU_C * r * sp
    a = jnp.exp(la)
    x2 = 2.0 * la
    y = jnp.where(x2 > -0.02, -x2 * (1.0 + x2 * (0.5 + x2 * (1.0 / 6.0))), 1.0 - a * a)
    inv_mult = lax.rsqrt(jnp.maximum(y, 1e-30))
    return r, i, a, y * inv_mult, inv_mult


def _later(x, before, k):
    if k == 0:
        return x
    row = lax.broadcasted_iota(jnp.int32, before.shape, 0)
    rolled = pltpu.roll(x, k, 0)
    first = jnp.where(row < k, pltpu.roll(before, k, 0), rolled[0:8])
    return jnp.concatenate([first, rolled[8:]], axis=0)


def _earlier(x, after, k):
    if k == 0:
        return x
    n = x.shape[0]
    row = lax.broadcasted_iota(jnp.int32, after.shape, 0)
    rolled = pltpu.roll(x, n - k, 0)
    last = jnp.where(row >= 8 - k, pltpu.roll(after, 8 - k, 0), rolled[n - 8:n])
    return jnp.concatenate([rolled[0:n - 8], last], axis=0)


def _fwd_fused(h, wt, tabs, wo_shard, conv_w, conv_b, wr, wi, br, bi, lam, tm):
    s = h.shape[0]
    nt = s // tm
    nc = 512
    pieces = 8
    rows_per = tm // pieces
    later_chunks = (0, 1, 2, 3, 4, 7, 8)

    def body(h_ref, wt_ref, tab_ref, wo_ref, cw_ref, cb_ref, wr_ref, wi_ref, br_ref,
             bi_ref, lam_ref, q_ref, k_ref, v_ref, ga_ref, xl_ref, gl_ref, u_ref, hl_ref, r_ref, ig_ref, a_ref,
             im_ref, wo_all, wo_stage, halo, ub_scr, pr_scr, pi_scr, b_scr, hcar,
             send_sems, recv_sems, local_sems):
        i = pl.program_id(0)
        start, finish = _gather_ops([wo_stage], [wo_all], send_sems, recv_sems, local_sems)

        @pl.when(i == 0)
        def _():
            wo_stage[...] = wo_ref[...].astype(bf16)
            start()
            halo[...] = jnp.zeros_like(halo)
            hcar[...] = jnp.zeros_like(hcar)

        sp = _softplus(-lam_ref[...])
        br, bi = br_ref[...], bi_ref[...]
        c, sa, sb = _tables(tab_ref)
        piece_rows = lambda p: slice(rows_per * p, rows_per * p + rows_per)

        def project(ci):
            z = _dot_nt(h_ref[...], wt_ref[ci * nc:(ci + 1) * nc, :])
            if ci < 2:
                for j in range(nc // 128):
                    r = _rope(z[:, 128 * j:128 * j + 128], c, sa, sb) * (HD ** -0.5)
                    q_ref[ci * nc + 128 * j:ci * nc + 128 * j + 128, :] = r.astype(bf16).T
            elif ci == 2:
                for j in range(2):
                    js = slice(128 * j, 128 * j + 128)
                    k_ref[js, :] = _rope(z[:, js], c, sa, sb).astype(bf16).T
                    v_ref[js, :] = z[:, KVW + 128 * j:KVW + 128 * j + 128].astype(bf16).T
            else:
                sec, j = divmod(ci - 3, 2)
                (ga_ref, xl_ref, gl_ref)[sec][:, j * nc:(j + 1) * nc] = z

        def gate_terms(p):
            rows = piece_rows(p)
            r, ig, a, mult, inv_mult = _gate_terms(pr_scr[rows, :], pi_scr[rows, :], br, bi, sp)
            r_ref[rows, :] = r
            ig_ref[rows, :] = ig
            a_ref[rows, :] = a
            im_ref[rows, :] = inv_mult
            b_scr[rows, :] = mult * (ig * u_ref[rows, :])

        def scan(p, hc):
            for t in range(rows_per * p, rows_per * p + rows_per):
                hc = a_ref[t:t + 1, :] * hc + b_scr[t:t + 1, :]
                hl_ref[t:t + 1, :] = hc
            return hc

        project(5)
        project(6)
        xl = xl_ref[...]
        u = cb_ref[...] + sum(cw_ref[k:k + 1, :] * _later(xl, halo[...], CONVW - 1 - k) for k in range(CONVW))
        halo[...] = xl[tm - 8:tm, :]
        u_ref[...] = u
        ub_scr[...] = u.astype(bf16)
        for g in range(NGRP):
            gs = slice(256 * g, 256 * g + 256)
            pr_scr[:, gs] = _dot(ub_scr[:, gs], wr_ref[g])
            pi_scr[:, gs] = _dot(ub_scr[:, gs], wi_ref[g])
        hc = hcar[...]
        gate_terms(0)
        for slot, ci in enumerate(later_chunks):
            project(ci)
            gate_terms(slot + 1)
            hc = scan(slot, hc)
        hcar[...] = scan(pieces - 1, hc)

        @pl.when(i == nt - 1)
        def _():
            finish()

    row = lambda w: pl.BlockSpec((tm, w), lambda i: (i, 0))
    col = lambda w: pl.BlockSpec((w, tm), lambda i: (0, i))
    full = lambda a: pl.BlockSpec(a.shape, lambda i: (0,) * a.ndim)
    big = lambda w, dt: pltpu.HBM((s, w), dt)
    tile = pltpu.VMEM((tm, LW), f32)
    return pl.pallas_call(
        body, name="fwd_fused", grid=(nt,),
        in_specs=[row(D), full(wt), row(384), full(wo_shard), full(conv_w), full(conv_b),
                  full(wr), full(wi), full(br), full(bi), full(lam)],
        out_specs=[col(D), col(KVW), col(KVW), row(D), row(D), row(D)] + [row(LW)] * 6 + [HBM],
        out_shape=[pltpu.HBM((D, s), bf16), pltpu.HBM((KVW, s), bf16), pltpu.HBM((KVW, s), bf16),
                   big(D, f32), big(D, f32), big(D, f32)] + [big(LW, f32)] * 6 + [pltpu.HBM((2 * D, D), bf16)],
        scratch_shapes=[pltpu.VMEM((WO_ROWS, D), bf16), pltpu.VMEM((8, LW), f32),
                        pltpu.VMEM((tm, LW), bf16), tile, tile, tile, pltpu.VMEM((1, LW), f32)] + _comm_sems(1),
        compiler_params=_params(("arbitrary",), 56),
    )(*_in_hbm(h, wt), tabs, wo_shard, conv_w, conv_b, wr, wi, br, bi, lam)


def _lru_bwd(u, hl, dhl, xl, r, ig, a, im, conv_w, wr, wi, lam, tm):
    s = u.shape[0]
    nt = s // tm
    pieces = 8
    rows_per = tm // pieces

    def body(u_ref, h_ref, hp_ref, dh_ref, x_ref, r_ref, ig_ref, a_ref, im_ref, cw_ref, wr_ref, wi_ref,
             lam_ref, dxl_ref, dwr_ref, dwi_ref, dbr_ref, dbi_ref, dlam_ref, dcb_ref, dcw_ref,
             l_scr, du_scr, dpr_scr, dpi_scr, lcar, dunext):
        t0 = pl.program_id(0)
        tile = nt - 1 - t0

        @pl.when(t0 == 0)
        def _():
            lcar[...] = jnp.zeros_like(lcar)
            dunext[...] = jnp.zeros_like(dunext)
            for ref in (dwr_ref, dwi_ref, dbr_ref, dbi_ref, dlam_ref, dcb_ref, dcw_ref):
                ref[...] = jnp.zeros_like(ref)

        lam = lam_ref[...]
        sp = _softplus(-lam)
        hp = jnp.where(tile > 0, hp_ref[...], 0.0)

        def scan(p, c):
            for t in range(rows_per * p + rows_per - 1, rows_per * p - 1, -1):
                lt = dh_ref[t:t + 1, :] + c
                l_scr[t:t + 1, :] = lt
                c = a_ref[t:t + 1, :] * lt
            return c

        def terms(p, sums):
            rows = slice(rows_per * p, rows_per * p + rows_per)
            lt, u, r, i, a, inv_mult = l_scr[rows, :], u_ref[rows, :], r_ref[rows, :], ig_ref[rows, :], \
                a_ref[rows, :], im_ref[rows, :]
            before = hp if p == 0 else h_ref[rows_per * p - 8:rows_per * p, :]
            hprev = _later(h_ref[rows, :], before, 1)
            x2 = -2.0 * LRU_C * r * sp
            mult = jnp.where(x2 > -0.02, -x2 * (1.0 + x2 * (0.5 + x2 * (1.0 / 6.0))), 1.0 - a * a) * inv_mult
            da = lt * hprev
            dmult = lt * (i * u)
            di = lt * mult * u
            du_scr[rows, :] = lt * mult * i
            dla = da * a - dmult * (a * a) * inv_mult
            dr = dla * (-LRU_C * sp)
            dpr = dr * r * (1.0 - r)
            dpi = di * i * (1.0 - i)
            dpr_scr[rows, :] = dpr.astype(bf16)
            dpi_scr[rows, :] = dpi.astype(bf16)
            col = lambda t: jnp.sum(t, axis=0, keepdims=True)
            return sums[0] + col(dla * (-LRU_C * r)), sums[1] + col(dpr), sums[2] + col(dpi)

        sums = (jnp.zeros((1, LW), f32),) * 3
        c = scan(pieces - 1, lcar[...])
        for p in range(pieces - 1, -1, -1):
            if p > 0:
                c = scan(p - 1, c)
            sums = terms(p, sums)
        lcar[...] = c
        dlam_ref[...] += sums[0]
        dbr_ref[...] += sums[1]
        dbi_ref[...] += sums[2]

        ub = u_ref[...].astype(bf16)
        dug = []
        for g in range(NGRP):
            gs = slice(256 * g, 256 * g + 256)
            dwr_ref[g] += _dot_tn(ub[:, gs], dpr_scr[:, gs])
            dwi_ref[g] += _dot_tn(ub[:, gs], dpi_scr[:, gs])
            dug.append(_dot_nt(dpr_scr[:, gs], wr_ref[g]) + _dot_nt(dpi_scr[:, gs], wi_ref[g]))
        du = du_scr[...] + jnp.concatenate(dug, axis=1)

        dcb_ref[...] += jnp.sum(du, axis=0, keepdims=True)
        x = x_ref[...]
        after = dunext[...]
        dxl = jnp.zeros_like(du)
        for k in range(CONVW):
            e = _earlier(du, after, CONVW - 1 - k)
            dxl = dxl + cw_ref[k:k + 1, :] * e
            dcw_ref[k:k + 1, :] += jnp.sum(e * x, axis=0, keepdims=True)
        dxl_ref[...] = dxl.astype(bf16)
        dunext[...] = du[0:8, :]

        @pl.when(t0 == nt - 1)
        def _():
            dlam_ref[...] = dlam_ref[...] * (-_sigmoid(-lam))

    rev = lambda i: (nt - 1 - i, 0)
    row = pl.BlockSpec((tm, LW), rev)
    prev8 = pl.BlockSpec((8, LW), lambda i: (jnp.maximum((nt - 1 - i) * (tm // 8) - 1, 0), 0))
    full = lambda a: pl.BlockSpec(a.shape, lambda i: (0,) * a.ndim)
    vec = pl.BlockSpec((1, LW), lambda i: (0, 0))
    bd = pl.BlockSpec((NGRP, 256, 256), lambda i: (0, 0, 0))
    return pl.pallas_call(
        body, name="lru_bwd", grid=(nt,),
        in_specs=[row, row, prev8, row, row, row, row, row, row, full(conv_w), full(wr), full(wi), full(lam)],
        out_specs=[row, bd, bd, vec, vec, vec, vec, pl.BlockSpec((CONVW, LW), lambda i: (0, 0))],
        out_shape=[pltpu.HBM((s, LW), bf16),
                   jax.ShapeDtypeStruct((NGRP, 256, 256), f32), jax.ShapeDtypeStruct((NGRP, 256, 256), f32),
                   jax.ShapeDtypeStruct((1, LW), f32), jax.ShapeDtypeStruct((1, LW), f32),
                   jax.ShapeDtypeStruct((1, LW), f32), jax.ShapeDtypeStruct((1, LW), f32),
                   jax.ShapeDtypeStruct((CONVW, LW), f32)],
        scratch_shapes=[pltpu.VMEM((tm, LW), f32), pltpu.VMEM((tm, LW), f32), pltpu.VMEM((tm, LW), bf16),
                        pltpu.VMEM((tm, LW), bf16), pltpu.VMEM((1, LW), f32), pltpu.VMEM((8, LW), f32)],
        compiler_params=_params(("arbitrary",), 56),
    )(*_in_hbm(u, hl, hl, dhl, xl, r, ig, a, im), conv_w, wr, wi, lam)


def _gated_norm(t, gate, gain):
    sg = _sigmoid(gate)
    silu = gate * sg
    p = t * silu
    rstd = lax.rsqrt(jnp.mean(p * p, axis=-1, keepdims=True) + EPS)
    ph = p * rstd
    return sg, silu, rstd, ph, ph * gain


def _gated_norm_bwd(dy, t, gate, gain, sg, silu, rstd, ph):
    w = dy * gain
    dp = rstd * (w - ph * jnp.mean(w * ph, axis=-1, keepdims=True))
    dgate = dp * t * (sg * (1.0 + gate * (1.0 - sg)))
    return jnp.sum(dy * ph, axis=0, keepdims=True), dp * silu, dgate


def _out_fwd_bwd(x, tgt, o, ga, hl, gl, again, lgain, fgain, wo, tm):
    s = x.shape[0]
    nt = s // tm

    def body(x_ref, t_ref, o_ref, ga_ref, hl_ref, gl_ref, ag_ref, lg_ref, fg_ref, wo_ref,
             dx2_ref, do_ref, dga_ref, dhl_ref, dgl_ref, dwo_ref, gfg_ref, gag_ref, glg_ref, loss_ref, acc):
        i = pl.program_id(0)

        @pl.when(i == 0)
        def _():
            acc[...] = jnp.zeros_like(acc)
            for ref in (gfg_ref, gag_ref, glg_ref, loss_ref):
                ref[...] = jnp.zeros_like(ref)

        oo = jnp.concatenate([o_ref[128 * j:128 * j + 128, :].T for j in range(D // 128)], axis=1)
        gga, hh, ggl = ga_ref[...], hl_ref[...], gl_ref[...]
        ag, lg, fg = ag_ref[...], lg_ref[...], fg_ref[...]
        sga, silua, ra, pah, ya = _gated_norm(oo, gga, ag)
        sgl, silul, rl, plh, yl = _gated_norm(hh, ggl, lg)
        yab, ylb = ya.astype(bf16), yl.astype(bf16)
        y = _dot(yab, wo_ref[0:D, :]) + _dot(ylb, wo_ref[D:2 * D, :])
        x2 = x_ref[...] + y
        r2 = lax.rsqrt(jnp.mean(x2 * x2, axis=-1, keepdims=True) + EPS)
        x2h = x2 * r2
        err = x2h * fg - t_ref[...]
        loss_ref[...] += 0.5 * jnp.sum(jnp.sum(err * err, axis=-1, keepdims=True) * (1.0 / D))
        dout = err * (1.0 / D)
        gfg_ref[...] += jnp.sum(dout * x2h, axis=0, keepdims=True)
        w = dout * fg
        dx2 = r2 * (w - x2h * jnp.mean(w * x2h, axis=-1, keepdims=True))
        dx2_ref[...] = dx2
        dyb = dx2.astype(bf16)
        acc[0:D, :] += _dot_tn(yab, dyb)
        acc[D:2 * D, :] += _dot_tn(ylb, dyb)
        dya = _dot_nt(dyb, wo_ref[0:D, :])
        dyl = _dot_nt(dyb, wo_ref[D:2 * D, :])
        gag, do, dga = _gated_norm_bwd(dya, oo, gga, ag, sga, silua, ra, pah)
        glg, dhl, dgl = _gated_norm_bwd(dyl, hh, ggl, lg, sgl, silul, rl, plh)
        gag_ref[...] += gag
        glg_ref[...] += glg
        dob = do.astype(bf16)
        for j in range(D // 128):
            do_ref[128 * j:128 * j + 128, :] = dob[:, 128 * j:128 * j + 128].T
        dga_ref[...] = dga.astype(bf16)
        dhl_ref[...] = dhl
        dgl_ref[...] = dgl.astype(bf16)

        @pl.when(i == nt - 1)
        def _():
            dwo_ref[...] = acc[...].astype(bf16)

    row = pl.BlockSpec((tm, D), lambda i: (i, 0))
    col = pl.BlockSpec((D, tm), lambda i: (0, i))
    vec = pl.BlockSpec((1, D), lambda i: (0, 0))
    mat = pl.BlockSpec((2 * D, D), lambda i: (0, 0))
    return pl.pallas_call(
        body, name="out_fwd_bwd", grid=(nt,),
        in_specs=[row, row, col, row, row, row] + [vec] * 3 + [mat],
        out_specs=[row, col, row, row, row] + [mat, vec, vec, vec, pl.BlockSpec((1, 128), lambda i: (0, 0))],
        out_shape=[pltpu.HBM((s, D), f32), pltpu.HBM((D, s), bf16),
                   pltpu.HBM((s, D), bf16), pltpu.HBM((s, D), f32),
                   pltpu.HBM((s, D), bf16), pltpu.HBM((2 * D, D), bf16),
                   jax.ShapeDtypeStruct((1, D), f32), jax.ShapeDtypeStruct((1, D), f32),
                   jax.ShapeDtypeStruct((1, D), f32), jax.ShapeDtypeStruct((1, 128), f32)],
        scratch_shapes=[pltpu.VMEM((2 * D, D), f32)],
        compiler_params=_params(("arbitrary",), 56),
    )(*_in_hbm(x, tgt, o, ga, hl, gl), again, lgain, fgain, *_in_hbm(wo))


def _bwd_in(x, dx2, dq, dk, dv, dga, dxl, dgl, ln_gain, wt, tabs, tm):
    s = x.shape[0]

    def body(x_ref, dx2_ref, dq_ref, dk_ref, dv_ref, dga_ref, dxl_ref, dgl_ref, g_ref, wt_ref,
             tab_ref, gx_ref, gln_ref, dzt_ref):
        @pl.when(pl.program_id(0) == 0)
        def _():
            gln_ref[...] = jnp.zeros_like(gln_ref)

        c, sa, sb = (t.T for t in _tables(tab_ref))
        for j in range(D // 128):
            js = slice(128 * j, 128 * j + 128)
            dzt_ref[js, :] = (_unrope_t(dq_ref[js, :], c, sa, sb) * (HD ** -0.5)).astype(bf16)
        for j in range(KVW // 128):
            js = slice(128 * j, 128 * j + 128)
            dzt_ref[D + 128 * j:D + 128 * j + 128, :] = _unrope_t(dk_ref[js, :], c, sa, sb).astype(bf16)
        dzt_ref[D + KVW:D + 2 * KVW, :] = dv_ref[...].astype(bf16)
        first = D + 2 * KVW
        dh = _dot_tn(dzt_ref[0:512, :], wt_ref[0:512, :])
        for ci in range(1, first // 512):
            dh = dh + _dot_tn(dzt_ref[512 * ci:512 * ci + 512, :], wt_ref[512 * ci:512 * ci + 512, :])
        for sec, ref in enumerate((dga_ref, dxl_ref, dgl_ref)):
            for j in range(D // 512):
                rows = slice(first + D * sec + 512 * j, first + D * sec + 512 * j + 512)
                dh = dh + _dot(ref[:, 512 * j:512 * j + 512], wt_ref[rows, :])
            for j in range(D // 128):
                dzt_ref[first + D * sec + 128 * j:first + D * sec + 128 * j + 128, :] = ref[:, 128 * j:128 * j + 128].T
        xx = x_ref[...]
        rstd = lax.rsqrt(jnp.mean(xx * xx, axis=-1, keepdims=True) + EPS)
        xh = xx * rstd
        gln_ref[...] += jnp.sum(dh * xh, axis=0, keepdims=True)
        w = dh * g_ref[...]
        gx_ref[...] = dx2_ref[...] + rstd * (w - xh * jnp.mean(w * xh, axis=-1, keepdims=True))

    row = lambda w: pl.BlockSpec((tm, w), lambda i: (i, 0))
    col = lambda w: pl.BlockSpec((w, tm), lambda i: (0, i))
    full = lambda a: pl.BlockSpec(a.shape, lambda i: (0, 0))
    return pl.pallas_call(
        body, name="bwd_in", grid=(s // tm,),
        in_specs=[row(D), row(D), col(D), col(KVW), col(KVW), row(D), row(D), row(D), full(ln_gain), full(wt),
                  row(384)],
        out_specs=[row(D), pl.BlockSpec((1, D), lambda i: (0, 0)), col(NIN)],
        out_shape=[pltpu.HBM((s, D), f32), jax.ShapeDtypeStruct((1, D), f32),
                   pltpu.HBM((NIN, s), bf16)],
        compiler_params=_params(("arbitrary",), 56),
    )(*_in_hbm(x, dx2, dq, dk, dv, dga, dxl, dgl), ln_gain, *_in_hbm(wt), tabs)


WT_TERMS = 4


def _dwt_scatter(dzt, h, small, tm):
    s = h.shape[0]
    nk = s // tm
    srows = small.shape[0] // NDEV
    last = NDEV - 1
    sm_turn = 4

    def body(order_ref, dz_ref, h_ref, sm_ref, lwt_ref, rep_all, tail_ref, acc, stage, given, relayed, lsm, rep_stage,
             send_sems, recv_sems, local_sem, sm_send, sm_recv, sm_local, rep_send, rep_recv, rep_local):
        j, k = pl.program_id(0), pl.program_id(1)
        x, y, c = _place()
        sibling = (x, y, 1 - c)
        near = (x ^ (1 - c), y ^ c)
        far = (x ^ c, y ^ (1 - c))
        sm_start, sm_finish = _scatter_ops([sm_ref], [lsm], sm_send, sm_recv, sm_local)
        rep_start, rep_finish = _gather_ops([rep_stage], [rep_all], rep_send, rep_recv, rep_local)

        def send(step):
            if step == last - 1:
                dst, to = lwt_ref.at[1], sibling
            elif step % 2 == 0:
                dst, to = given.at[step // 2], sibling
            elif step == 1:
                dst, to = relayed, (*near, c)
            else:
                dst, to = lwt_ref.at[1 + step // 2], (*(near if step == 3 else far), c)
            return pltpu.make_async_remote_copy(
                src_ref=stage.at[step % 2], dst_ref=dst, send_sem=send_sems.at[step], recv_sem=recv_sems.at[step],
                device_id=to, device_id_type=MESH)

        def keep():
            return pltpu.make_async_copy(stage.at[last % 2], lwt_ref.at[0], local_sem)

        @pl.when((j == 0) & (k == 0))
        def _():
            sm_start()

        @pl.when(k == 0)
        def _():
            acc[...] = jnp.zeros_like(acc)

        acc[...] += _dot(dz_ref[...], h_ref[...])

        for step in range(NDEV):
            @pl.when((k == nk - 1) & (j == step))
            def _(step=step):
                if step >= 2:
                    send(step - 2).wait_send()
                if step % 2 == 1 and step < last:
                    send(step - 1).wait_recv()
                    total = acc[...] + given[step // 2].astype(f32)
                    if step == 5:
                        send(1).wait_recv()
                        total = total + relayed[...].astype(f32)
                    stage[step % 2] = total.astype(bf16)
                else:
                    stage[step % 2] = acc[...].astype(bf16)
                if step < last:
                    send(step).start()
                else:
                    keep().start()
                    send(last - 1).wait_send()
                    for peer_step in (3, 5, last - 1):
                        send(peer_step).wait_recv()
                    keep().wait()
                    rep_finish()
                if step == sm_turn:
                    sm_finish()
                    total_sm = lsm[0]
                    for dev in range(1, NDEV):
                        total_sm = total_sm + lsm[dev]
                    rep_stage[...] = total_sm[0:SMALL_PER]
                    tail_ref[...] = total_sm[SMALL_PER:]
                    rep_start()

    x, y, c = _place()
    dest = lambda chip, cc: 4 * chip[0] + 2 * chip[1] + cc
    near, far, diag = (x ^ (1 - c), y ^ c), (x ^ c, y ^ (1 - c)), (1 - x, 1 - y)
    order = jnp.stack([dest(diag, 1 - c), dest(diag, c), dest(far, 1 - c), dest(near, c),
                       dest(near, 1 - c), dest(far, c), dest((x, y), 1 - c), dest((x, y), c)])
    return pl.pallas_call(
        body, name="dwt_scatter",
        grid_spec=pltpu.PrefetchScalarGridSpec(
            num_scalar_prefetch=1, grid=(NDEV, nk),
            in_specs=[pl.BlockSpec((WT_ROWS, tm), lambda j, k, order: (order[j], k)),
                      pl.BlockSpec((tm, D), lambda j, k, order: (k, 0)), HBM],
            out_specs=[HBM, HBM, pl.BlockSpec((srows - SMALL_PER, D), lambda j, k, order: (0, 0))],
            scratch_shapes=[pltpu.VMEM((WT_ROWS, D), f32), pltpu.VMEM((2, WT_ROWS, D), bf16),
                            pltpu.VMEM((3, WT_ROWS, D), bf16), pltpu.VMEM((WT_ROWS, D), bf16),
                            pltpu.VMEM((NDEV, srows, D), f32), pltpu.VMEM((SMALL_PER, D), f32),
                            pltpu.SemaphoreType.DMA((last,)), pltpu.SemaphoreType.DMA((last,)),
                            pltpu.SemaphoreType.DMA(())] + _comm_sems(1) + _comm_sems(1)),
        out_shape=[pltpu.HBM((WT_TERMS, WT_ROWS, D), bf16), pltpu.HBM((SMALL_ROWS, D), f32),
                   jax.ShapeDtypeStruct((srows - SMALL_PER, D), f32)],
        compiler_params=_params(("arbitrary", "arbitrary"), 48),
    )(order, *_in_hbm(dzt, h, small))


def _diag_blocks(bd):
    eye = jnp.eye(4, dtype=bd.dtype)
    return jnp.einsum('gjckd,jk->gjcd', bd.reshape(NGRP, 4, HD, 4, HD), eye).reshape(NQ, HD, HD)


def _sequence_step(x, h, tgt, wt, wo_shard, conv_w, p):
    s = x.shape[0]
    tm = min(256, s)
    tabs = _rope_tables(s)
    wr, wi = _block_diag(p["w_rgate"]), _block_diag(p["w_igate"])
    sinks = p["sinks"].reshape(NQ)
    qt, kt, vt, ga, xl, gl, u, hl, r, ig, a, im, wo = _fwd_fused(
        h, wt, tabs, wo_shard, conv_w, p["conv_b"], wr, wi, p["b_rgate"], p["b_igate"], p["lru_lambda"], tm)
    ot = _attn_fwd_t(qt, kt, vt, sinks)
    dx2, dot, dga, dhl, dgl, dwo, g_fg, g_ag, g_lg, loss = _out_fwd_bwd(
        x, tgt, ot, ga, hl, gl, p["attn_out_gain"], p["lru_out_gain"], p["final_gain"], wo, tm)
    dqt, dkt, dvt, dsink, land_wo = _attn_bwd_t(qt, kt, vt, dot, sinks, dwo)
    dxl, dwr, dwi, dbr, dbi, dlam, dcb, dcw = _lru_bwd(u, hl, dhl, xl, r, ig, a, im, conv_w, wr, wi, p["lru_lambda"], tm)
    gx, g_ln, dzt = _bwd_in(x, dx2, dqt, dkt, dvt, dga, dxl, dgl, p["ln_gain"], wt, tabs, tm)
    small = dict(ln_gain=g_ln, sinks=dsink.reshape(NQ, BLK).sum(axis=1)[None], conv_w=dcw, conv_b=dcb,
                 w_rgate=_diag_blocks(dwr), b_rgate=dbr, w_igate=_diag_blocks(dwi), b_igate=dbi, lru_lambda=dlam,
                 attn_out_gain=g_ag, lru_out_gain=g_lg, final_gain=g_fg)
    land_wt, g_rep, g_tail = _dwt_scatter(dzt, h, _pack_small(small, loss), min(2048, s))
    return gx, land_wt, land_wo, g_rep, g_tail


def _gather_weights(wt_shard, conv_blk, x, ln_gain, tm):
    s = x.shape[0]

    def body(wt_ref, cw_ref, g_ref, x_ref, wt_all, cw_all, h_ref, stage, xbuf, hbuf, send_sems, recv_sems, local_sems):
        stage[...] = wt_ref[...].astype(bf16)
        start, finish = _relay_gather_ops([stage, cw_ref], [wt_all, cw_all], send_sems, recv_sems, local_sems)
        start()
        gain = g_ref[...]
        for i in range(s // tm):
            rows = pl.ds(i * tm, tm)
            pltpu.sync_copy(x_ref.at[rows, :], xbuf)
            xx = xbuf[...]
            rstd = lax.rsqrt(jnp.mean(xx * xx, axis=-1, keepdims=True) + EPS)
            hbuf[...] = (xx * rstd * gain).astype(bf16)
            pltpu.sync_copy(hbuf, h_ref.at[rows, :])
        finish()

    vmem = pl.BlockSpec(memory_space=pltpu.VMEM)
    return pl.pallas_call(
        body, name="gather_weights",
        in_specs=[vmem, vmem, vmem, HBM], out_specs=[HBM, HBM, HBM],
        out_shape=[pltpu.HBM((NIN, D), bf16), pltpu.HBM((NDEV * 8, 128), f32), pltpu.HBM((s, D), bf16)],
        scratch_shapes=[pltpu.VMEM((WT_ROWS, D), bf16), pltpu.VMEM((tm, D), f32), pltpu.VMEM((tm, D), bf16)]
        + _comm_sems(2),
        compiler_params=pltpu.CompilerParams(vmem_limit_bytes=32 * MIB),
    )(wt_shard, conv_blk, ln_gain, *_in_hbm(x))


def _adam_math(w, g, m, v):
    m2 = ADAM_B1 * m + (1.0 - ADAM_B1) * g
    v2 = ADAM_B2 * v + (1.0 - ADAM_B2) * (g * g)
    m_hat = m2 / (1.0 - ADAM_B1 ** ADAM_STEP)
    v_hat = v2 / (1.0 - ADAM_B2 ** ADAM_STEP)
    delta = -ADAM_LR * (m_hat / (jnp.sqrt(v_hat) + ADAM_EPS) + ADAM_WD * w)
    return delta, m2, v2


def _reduce_adamw(land, w, m, v, tr, name):
    terms, rows, cols = land.shape

    def body(l_ref, w_ref, m_ref, v_ref, g_ref, d_ref, m2_ref, v2_ref):
        g = l_ref[0].astype(f32)
        for t in range(1, terms):
            g = g + l_ref[t].astype(f32)
        g_ref[...] = g
        d_ref[...], m2_ref[...], v2_ref[...] = _adam_math(w_ref[...], g, m_ref[...], v_ref[...])

    blk = pl.BlockSpec((tr, cols), lambda i: (i, 0))
    return pl.pallas_call(
        body, name=name, grid=(rows // tr,),
        in_specs=[pl.BlockSpec((terms, tr, cols), lambda i: (0, i, 0))] + [blk] * 3, out_specs=[blk] * 4,
        out_shape=[jax.ShapeDtypeStruct((rows, cols), f32)] * 4,
        compiler_params=_params(("arbitrary",), 32),
    )(*_in_hbm(land), w, m, v)


VEC_NAMES = ("ln_gain", "conv_b", "b_rgate", "b_igate", "lru_lambda", "attn_out_gain", "lru_out_gain", "final_gain")
ROW_RGATE, ROW_IGATE, ROW_VEC, ROW_SINKS = 0, 64, 128, 136
LOSS_LANE = NQ


def _adamw_small(g_rep, g_conv, w, m, v):
    names = list(VEC_NAMES) + ["sinks", "conv_w", "w_rgate", "w_igate"]
    ins = [g_rep, g_conv] + [d[k] for k in names for d in (w, m, v)]

    def body(*refs):
        g_ref, gc_ref = refs[0], refs[1]
        in_refs = refs[2:2 + 3 * len(names)]
        out_refs = refs[2 + 3 * len(names):]

        def update(j, g, at=None):
            w_ref, m_ref, v_ref = in_refs[3 * j:3 * j + 3]
            outs = out_refs[4 * j:4 * j + 4]
            pick = (lambda r: r[...]) if at is None else (lambda r: r[at])
            res = (g,) + _adam_math(pick(w_ref), g, pick(m_ref), pick(v_ref))
            for o_ref, val in zip(outs, res):
                if at is None:
                    o_ref[...] = val
                else:
                    o_ref[at] = val

        for j in range(len(VEC_NAMES)):
            update(j, g_ref[ROW_VEC + j:ROW_VEC + j + 1, :])
        update(len(VEC_NAMES), g_ref[ROW_SINKS:ROW_SINKS + 1, 0:NQ])
        update(len(VEC_NAMES) + 1, gc_ref[...], at=0)
        for gi, row0 in ((len(VEC_NAMES) + 2, ROW_RGATE), (len(VEC_NAMES) + 3, ROW_IGATE)):
            for nb in range(NQ):
                update(gi, g_ref[row0:row0 + HD, HD * nb:HD * nb + HD], at=(0, nb))

    vmem = pl.BlockSpec(memory_space=pltpu.VMEM)
    out_shape = [jax.ShapeDtypeStruct(w[k].shape, f32) for k in names for _ in range(4)]
    outs = pl.pallas_call(
        body, name="adamw_small",
        in_specs=[vmem] * len(ins), out_specs=[vmem] * len(out_shape), out_shape=out_shape,
        compiler_params=pltpu.CompilerParams(vmem_limit_bytes=32 * MIB),
    )(*ins)
    return {k: tuple(outs[4 * j:4 * j + 4]) for j, k in enumerate(names)}


def _pack_small(small, loss):
    gate = lambda g: g.transpose(1, 0, 2).reshape(HD, NQ * HD)
    row_s = jnp.concatenate([small["sinks"], loss[:, LOSS_LANE:128], jnp.zeros((1, D - 128), f32)], axis=1)
    rep = jnp.concatenate([gate(small["w_rgate"]), gate(small["w_igate"])] + [small[k] for k in VEC_NAMES]
                          + [row_s, jnp.zeros((SMALL_ROWS - ROW_SINKS - 1, D), f32)], axis=0)
    conv = small["conv_w"].reshape(CONVW, NDEV, 128).transpose(1, 0, 2)
    conv = jnp.pad(conv, ((0, 0), (0, 8 - CONVW), (0, D - 128)))
    return jnp.concatenate([rep.reshape(NDEV, SMALL_PER, D), conv], axis=1).reshape(NDEV * (SMALL_PER + 8), D)


def kernel(x, ln_gain, w_in, sinks, conv_w, conv_b, w_rgate, b_rgate, w_igate, b_igate, lru_lambda, attn_out_gain, lru_out_gain, w_out, final_gain, loss_target, m_ln_gain, m_w_in, m_sinks, m_conv_w, m_conv_b, m_w_rgate, m_b_rgate, m_w_igate, m_b_igate, m_lru_lambda, m_attn_out_gain, m_lru_out_gain, m_w_out, m_final_gain, v_ln_gain, v_w_in, v_sinks, v_conv_w, v_conv_b, v_w_rgate, v_b_rgate, v_w_igate, v_b_igate, v_lru_lambda, v_attn_out_gain, v_lru_out_gain, v_w_out, v_final_gain):
    w = dict(ln_gain=ln_gain, sinks=sinks, conv_w=conv_w, conv_b=conv_b, w_rgate=w_rgate, b_rgate=b_rgate,
             w_igate=w_igate, b_igate=b_igate, lru_lambda=lru_lambda, attn_out_gain=attn_out_gain,
             lru_out_gain=lru_out_gain, final_gain=final_gain.reshape(1, D))
    m = dict(ln_gain=m_ln_gain, sinks=m_sinks, conv_w=m_conv_w, conv_b=m_conv_b, w_rgate=m_w_rgate,
             b_rgate=m_b_rgate, w_igate=m_w_igate, b_igate=m_b_igate, lru_lambda=m_lru_lambda,
             attn_out_gain=m_attn_out_gain, lru_out_gain=m_lru_out_gain, final_gain=m_final_gain.reshape(1, D))
    v = dict(ln_gain=v_ln_gain, sinks=v_sinks, conv_w=v_conv_w, conv_b=v_conv_b, w_rgate=v_w_rgate,
             b_rgate=v_b_rgate, w_igate=v_w_igate, b_igate=v_b_igate, lru_lambda=v_lru_lambda,
             attn_out_gain=v_attn_out_gain, lru_out_gain=v_lru_out_gain, final_gain=v_final_gain.reshape(1, D))

    conv_blk = jnp.pad(conv_w[0], ((0, 8 - CONVW), (0, 0)))
    wt, cw_all, h = _gather_weights(w_in[0].T, conv_blk, x[0], ln_gain, min(512, x.shape[1]))
    conv_full = cw_all.reshape(NDEV, 8, 128)[:, 0:CONVW].transpose(1, 0, 2).reshape(CONVW, LW)

    p = {k: (w[k][0] if k in ("w_rgate", "w_igate") else w[k]) for k in w if k != "conv_w"}
    gx, land_wt, land_wo, g_rep, g_tail = _sequence_step(x[0], h, loss_target[0], wt, w_out[0], conv_full, p)
    g_conv = g_tail[0:CONVW, 0:128]

    wins = _reduce_adamw(land_wt, w_in[0].T, m_w_in[0].T, v_w_in[0].T, 192, "adamw_w_in")
    g_win, d_win, m_win, v_win = (t.T for t in wins)
    g_wo, d_wo, m_wo, v_wo = _reduce_adamw(land_wo, w_out[0], m_w_out[0], v_w_out[0], 256, "adamw_w_out")
    res = _adamw_small(g_rep, g_conv, w, m, v)
    res["w_in"] = tuple(t[None] for t in (g_win, d_win, m_win, v_win))
    res["w_out"] = tuple(t[None] for t in (g_wo, d_wo, m_wo, v_wo))
    res["final_gain"] = tuple(t.reshape(D) for t in res["final_gain"])

    order = ("ln_gain", "w_in", "sinks", "conv_w", "conv_b", "w_rgate", "b_rgate", "w_igate", "b_igate",
             "lru_lambda", "attn_out_gain", "lru_out_gain", "w_out", "final_gain")
    total_loss = g_rep[ROW_SINKS, LOSS_LANE]
    return (total_loss, gx[None]) + tuple(res[k][i] for i in range(4) for k in order)
```

```python
import jax
import jax.numpy as jnp
from jax import lax
from jax.experimental import pallas as pl
from jax.experimental.pallas import tpu as pltpu

f32 = jnp.float32
bf16 = jnp.bfloat16

D = 1024
HD = 64
NQ = 16
NKV = 4
GROUP = NQ // NKV
KVW = NKV * HD
BLK = 128
ROT = 16
THETA = 500000.0
NEG = -1e30
LW = 1024
NGRP = 4
CONVW = 4
LRU_C = 8.0
NIN = 4608
EPS = 1e-6
NDEV = 8
WT_ROWS = NIN // NDEV
WO_ROWS = 2 * D // NDEV
SMALL_ROWS = 192
SMALL_PER = SMALL_ROWS // NDEV

ADAM_LR = 0.001
ADAM_B1 = 0.9
ADAM_B2 = 0.999
ADAM_EPS = 1e-08
ADAM_WD = 0.01
ADAM_STEP = 10

NT = (((1,), (1,)), ((), ()))
TN = (((0,), (0,)), ((), ()))
MESH = pl.DeviceIdType.MESH
MIB = 1024 * 1024


def _dot(a, b):
    return jnp.dot(a, b, preferred_element_type=f32)


def _dot_nt(a, b):
    return lax.dot_general(a, b, NT, preferred_element_type=f32)


def _dot_tn(a, b):
    return lax.dot_general(a, b, TN, preferred_element_type=f32)


def _params(sem, vmem_mib):
    return pltpu.CompilerParams(dimension_semantics=sem, vmem_limit_bytes=vmem_mib * MIB)


def _sigmoid(x):
    return 0.5 * jnp.tanh(0.5 * x) + 0.5


def _softplus(x):
    return jnp.maximum(x, 0.0) + jnp.log(1.0 + jnp.exp(-jnp.abs(x)))


def _rope_tables(s):
    pos = jnp.arange(s, dtype=f32)
    inv_freq = THETA ** (-jnp.arange(0, ROT, 2, dtype=f32) / ROT)
    ang = pos[:, None] * inv_freq[None, :]
    cs = jnp.concatenate([jnp.cos(ang) - 1.0, jnp.sin(ang)], axis=1)
    d = jnp.arange(128) % HD
    j = jnp.arange(ROT)[:, None]
    pick_c = ((d < ROT) & (j == d % (ROT // 2))).astype(f32)
    pick_sa = ((d >= ROT // 2) & (d < ROT) & (j == d)).astype(f32)
    pick_sb = -((d < ROT // 2) & (j == d + ROT // 2)).astype(f32)
    picks = jnp.concatenate([pick_c, pick_sa, pick_sb], axis=1)
    ones = jnp.concatenate([jnp.ones((1, 128), f32), jnp.zeros((1, 256), f32)], axis=1)
    return jnp.dot(cs, picks, precision=lax.Precision.HIGHEST) + ones


def _tables(tab_ref):
    return tab_ref[:, 0:128], tab_ref[:, 128:256], tab_ref[:, 256:384]


def _rope(t, c, sa, sb):
    return t * c + pltpu.roll(t, 8, 1) * sa + pltpu.roll(t, 120, 1) * sb


def _unrope_t(dr, c, sa, sb):
    return dr * c + pltpu.roll(dr * sa, 120, 0) + pltpu.roll(dr * sb, 8, 0)


def _place():
    return lax.axis_index("x"), lax.axis_index("y"), lax.axis_index("c")


def _gather_ops(mine_refs, out_refs, send_sems, recv_sems, local_sems):
    n = len(mine_refs)
    x, y, c = _place()
    me, sibling = (x, y, c), (x, y, 1 - c)
    chips = [(1 - x, y), (x, 1 - y), (1 - x, 1 - y)]

    def rows(a, dev):
        m = mine_refs[a].shape[0]
        return out_refs[a].at[pl.ds((4 * dev[0] + 2 * dev[1] + dev[2]) * m, m), :]

    def copy(a, k, block, to, own=False):
        return pltpu.make_async_remote_copy(
            src_ref=mine_refs[a] if own else rows(a, block), dst_ref=rows(a, block),
            send_sem=send_sems.at[a, k], recv_sem=recv_sems.at[a, k], device_id=to, device_id_type=MESH)

    def local(a):
        return pltpu.make_async_copy(mine_refs[a], rows(a, me), local_sems.at[a])

    def first(a):
        return [copy(a, 0, me, sibling, own=True)] + [copy(a, 1 + j, me, (*chip, c), own=True)
                                                      for j, chip in enumerate(chips)]

    def start():
        for a in range(n):
            local(a).start()
            for cp in first(a):
                cp.start()

    def pass_on():
        for j, chip in enumerate(chips):
            for a in range(n):
                copy(a, 1 + j, (*chip, c), me).wait_recv()
                copy(a, 4 + j, (*chip, c), sibling).start()

    def finish():
        for a in range(n):
            copy(a, 0, sibling, me).wait_recv()
            for j, chip in enumerate(chips):
                copy(a, 4 + j, (*chip, 1 - c), me).wait_recv()
        for a in range(n):
            for cp in first(a) + [copy(a, 4 + j, (*chip, c), sibling) for j, chip in enumerate(chips)]:
                cp.wait_send()
            local(a).wait()

    return start, pass_on, finish


def _relay_gather_ops(mine_refs, out_refs, send_sems, recv_sems, local_sems):
    n = len(mine_refs)
    x, y, c = _place()
    me, sibling = (x, y, c), (x, y, 1 - c)
    near = (x ^ (1 - c), y ^ c)
    far = (x ^ c, y ^ (1 - c))
    diag = (1 - x, 1 - y)

    def rows(a, dev):
        m = mine_refs[a].shape[0]
        return out_refs[a].at[pl.ds((4 * dev[0] + 2 * dev[1] + dev[2]) * m, m), :]

    def copy(a, k, block, to, own=False):
        return pltpu.make_async_remote_copy(
            src_ref=mine_refs[a] if own else rows(a, block), dst_ref=rows(a, block),
            send_sem=send_sems.at[a, k], recv_sem=recv_sems.at[a, k], device_id=to, device_id_type=MESH)

    def local(a):
        return pltpu.make_async_copy(mine_refs[a], rows(a, me), local_sems.at[a])

    def sends(a):
        return [copy(a, 0, me, sibling, own=True), copy(a, 1, me, (*near, c), own=True),
                copy(a, 2, me, (*far, c), own=True), copy(a, 3, (*near, c), (*far, c)),
                copy(a, 4, (*near, c), sibling), copy(a, 5, (*far, c), sibling), copy(a, 6, (*diag, c), sibling)]

    def arrivals(a):
        return [copy(a, 0, sibling, me), copy(a, 1, (*near, c), me), copy(a, 2, (*far, c), me),
                copy(a, 3, (*diag, c), me), copy(a, 4, (*far, 1 - c), me), copy(a, 5, (*near, 1 - c), me),
                copy(a, 6, (*diag, 1 - c), me)]

    def start():
        for a in range(n):
            local(a).start()
            for cp in sends(a)[0:3]:
                cp.start()

    def finish():
        for first, then in ((1, (3, 4)), (2, (5,)), (3, (6,))):
            for a in range(n):
                arrivals(a)[first].wait_recv()
                for k in then:
                    sends(a)[k].start()
        for a in range(n):
            for k in (0, 4, 5, 6):
                arrivals(a)[k].wait_recv()
        for a in range(n):
            for cp in sends(a):
                cp.wait_send()
            local(a).wait()

    return start, finish


def _scatter_ops(src_refs, land_refs, send_sems, recv_sems, local_sems):
    n = len(src_refs)
    x, y, c = _place()
    my = 4 * x + 2 * y + c

    def peer(k):
        return x ^ (k >> 2), y ^ ((k >> 1) & 1), c ^ (k & 1)

    def piece(a, dev):
        m = src_refs[a].shape[0] // NDEV
        return src_refs[a].at[pl.ds(dev * m, m), :]

    def local(a):
        return pltpu.make_async_copy(piece(a, my), land_refs[a].at[my], local_sems.at[a])

    def send(a, k):
        px, py, pc = peer(k)
        return pltpu.make_async_remote_copy(
            src_ref=piece(a, 4 * px + 2 * py + pc), dst_ref=land_refs[a].at[my],
            send_sem=send_sems.at[a, k - 1], recv_sem=recv_sems.at[a, k - 1],
            device_id=(px, py, pc), device_id_type=MESH)

    def arrival(a, k):
        px, py, pc = peer(k)
        return pltpu.make_async_remote_copy(
            src_ref=piece(a, my), dst_ref=land_refs[a].at[4 * px + 2 * py + pc],
            send_sem=send_sems.at[a, k - 1], recv_sem=recv_sems.at[a, k - 1],
            device_id=(px, py, pc), device_id_type=MESH)

    def start():
        for a in range(n):
            local(a).start()
        for k in range(1, NDEV):
            for a in range(n):
                send(a, k).start()

    def finish():
        for k in range(1, NDEV):
            for a in range(n):
                send(a, k).wait_send()
        for k in range(1, NDEV):
            for a in range(n):
                arrival(a, k).wait_recv()
        for a in range(n):
            local(a).wait()

    return start, finish


def _in_hbm(*arrays):
    return tuple(pltpu.with_memory_space_constraint(a, pltpu.HBM) for a in arrays)


def _comm_sems(n):
    return [pltpu.SemaphoreType.DMA((n, 7)), pltpu.SemaphoreType.DMA((n, 7)), pltpu.SemaphoreType.DMA((n,))]


HBM = pl.BlockSpec(memory_space=pltpu.HBM)


def _sink_rows(sinks):
    return jnp.repeat(sinks.reshape(NKV, GROUP), BLK, axis=1)


def _band_softmax(s2_ref, ls, prev_offset, sink_row):
    jj = lax.broadcasted_iota(jnp.int32, (BLK, BLK), 0)
    ii = lax.broadcasted_iota(jnp.int32, (BLK, BLK), 1)
    from_prev = jj > ii
    sc = jnp.where(from_prev, s2_ref[0:BLK, ls] + prev_offset, s2_ref[BLK:2 * BLK, ls])
    m = jnp.maximum(jnp.max(sc, axis=0, keepdims=True), sink_row)
    p = jnp.exp(sc - m)
    es = jnp.exp(sink_row - m)
    inv = 1.0 / (jnp.sum(p, axis=0, keepdims=True) + es)
    return from_prev, p * inv, es * inv


def _put_split(dst_ref, ls, t, from_prev):
    t = t.astype(bf16)
    zero = jnp.zeros_like(t)
    dst_ref[0:BLK, ls] = jnp.where(from_prev, t, zero)
    dst_ref[BLK:2 * BLK, ls] = jnp.where(from_prev, zero, t)


def _heads_side_by_side(ref, h):
    return jnp.concatenate([ref[HD * (GROUP * h + g):HD * (GROUP * h + g) + HD, :] for g in range(GROUP)], axis=1)


def _kv_specs_t():
    prev = pl.BlockSpec((KVW, BLK), lambda n: (0, jnp.maximum(n - 1, 0)))
    cur = pl.BlockSpec((KVW, BLK), lambda n: (0, n))
    return [prev, cur, prev, cur]


def _attn_fwd_t(qt, kt, vt, sinks):
    s = qt.shape[1]

    def body(sink_ref, q_ref, kp_ref, kc_ref, vp_ref, vc_ref, o_ref, s2_scr, pn2_scr):
        n = pl.program_id(0)
        off = jnp.where(n > 0, 0.0, NEG)

        def scores(h):
            hs = slice(HD * h, HD * h + HD)
            kh = jnp.concatenate([kp_ref[hs, :], kc_ref[hs, :]], axis=1)
            s2_scr[h % 2] = _dot_tn(kh, _heads_side_by_side(q_ref, h))

        def probs(h):
            for g in range(GROUP):
                ls = slice(BLK * g, BLK * g + BLK)
                from_prev, pn, _ = _band_softmax(s2_scr.at[h % 2], ls, off, sink_ref[h:h + 1, ls])
                _put_split(pn2_scr.at[h % 2], ls, pn, from_prev)

        def outputs(h):
            hs = slice(HD * h, HD * h + HD)
            vh = jnp.concatenate([vp_ref[hs, :], vc_ref[hs, :]], axis=1)
            og = _dot(vh, pn2_scr[h % 2])
            for g in range(GROUP):
                a = GROUP * h + g
                o_ref[HD * a:HD * a + HD, :] = og[:, BLK * g:BLK * g + BLK]

        scores(0)
        for h in range(NKV):
            if h + 1 < NKV:
                scores(h + 1)
            probs(h)
            outputs(h)

    return pl.pallas_call(
        body, name="attn_fwd", grid=(s // BLK,),
        in_specs=[pl.BlockSpec((NKV, GROUP * BLK), lambda n: (0, 0)), pl.BlockSpec((D, BLK), lambda n: (0, n))]
        + _kv_specs_t(),
        out_specs=pl.BlockSpec((D, BLK), lambda n: (0, n)),
        out_shape=pltpu.HBM((D, s), f32),
        scratch_shapes=[pltpu.VMEM((2, 2 * BLK, GROUP * BLK), f32), pltpu.VMEM((2, 2 * BLK, GROUP * BLK), bf16)],
        compiler_params=_params(("arbitrary",), 32),
    )(_sink_rows(sinks), *_in_hbm(qt, kt, kt, vt, vt))


def _attn_bwd_t(qt, kt, vt, dot, sinks, dwo):
    s = qt.shape[1]
    nb = s // BLK

    def body(sink_ref, q_ref, do_ref, kp_ref, kc_ref, vp_ref, vc_ref, dwo_ref, dq_ref, dk_ref, dv_ref, ds_ref,
             land_ref, dk_hold, dv_hold, s2_scr, dp2_scr, pn2_scr, ds2_scr, send_sems, recv_sems, local_sems):
        n = pl.program_id(0)
        start, finish = _scatter_ops([dwo_ref], [land_ref], send_sems, recv_sems, local_sems)

        @pl.when(n == 0)
        def _():
            start()
            dk_hold[...] = jnp.zeros_like(dk_hold)
            dv_hold[...] = jnp.zeros_like(dv_hold)
            ds_ref[...] = jnp.zeros_like(ds_ref)

        @pl.when(n < nb)
        def _():
            off = jnp.where(n > 0, 0.0, NEG)

            def scores(h):
                hs = slice(HD * h, HD * h + HD)
                kh = jnp.concatenate([kp_ref[hs, :], kc_ref[hs, :]], axis=1)
                vh = jnp.concatenate([vp_ref[hs, :], vc_ref[hs, :]], axis=1)
                s2_scr[h % 2] = _dot_tn(kh, _heads_side_by_side(q_ref, h))
                dp2_scr[h % 2] = _dot_tn(vh, _heads_side_by_side(do_ref, h))

            def softmax_bwd(h):
                for g in range(GROUP):
                    ls = slice(BLK * g, BLK * g + BLK)
                    from_prev, pn, ps = _band_softmax(s2_scr.at[h % 2], ls, off, sink_ref[h:h + 1, ls])
                    dp = jnp.where(from_prev, dp2_scr[h % 2, 0:BLK, ls], dp2_scr[h % 2, BLK:2 * BLK, ls])
                    dsum = jnp.sum(pn * dp, axis=0, keepdims=True)
                    ds_ref[h:h + 1, ls] += -ps * dsum
                    _put_split(pn2_scr.at[h % 2], ls, pn, from_prev)
                    _put_split(ds2_scr.at[h % 2], ls, pn * (dp - dsum), from_prev)

            def grads(h):
                hs = slice(HD * h, HD * h + HD)
                kh = jnp.concatenate([kp_ref[hs, :], kc_ref[hs, :]], axis=1)
                dqg = _dot(kh, ds2_scr[h % 2])
                for g in range(GROUP):
                    a = GROUP * h + g
                    dq_ref[HD * a:HD * a + HD, :] = dqg[:, BLK * g:BLK * g + BLK]
                dkh = _dot_nt(_heads_side_by_side(q_ref, h), ds2_scr[h % 2])
                dvh = _dot_nt(_heads_side_by_side(do_ref, h), pn2_scr[h % 2])
                dk_ref[hs, :] = dk_hold[hs, :] + dkh[:, 0:BLK]
                dv_ref[hs, :] = dv_hold[hs, :] + dvh[:, 0:BLK]
                dk_hold[hs, :] = dkh[:, BLK:2 * BLK]
                dv_hold[hs, :] = dvh[:, BLK:2 * BLK]

            scores(0)
            for h in range(NKV):
                if h + 1 < NKV:
                    scores(h + 1)
                softmax_bwd(h)
                grads(h)

        @pl.when(n == nb)
        def _():
            dk_ref[...] = dk_hold[...]
            dv_ref[...] = dv_hold[...]
            finish()

    blk = pl.BlockSpec((D, BLK), lambda n: (0, jnp.minimum(n, nb - 1)))
    late = pl.BlockSpec((KVW, BLK), lambda n: (0, jnp.maximum(n - 1, 0)))
    whole = pl.BlockSpec((NKV, GROUP * BLK), lambda n: (0, 0))
    kv = [pl.BlockSpec((KVW, BLK), lambda n: (0, jnp.clip(n - 1, 0, nb - 1))),
          pl.BlockSpec((KVW, BLK), lambda n: (0, jnp.minimum(n, nb - 1)))]
    return pl.pallas_call(
        body, name="attn_bwd", grid=(nb + 1,),
        in_specs=[whole, blk, blk] + kv + kv + [HBM],
        out_specs=[blk, late, late, whole, HBM],
        out_shape=[pltpu.HBM((D, s), f32), pltpu.HBM((KVW, s), f32), pltpu.HBM((KVW, s), f32),
                   jax.ShapeDtypeStruct((NKV, GROUP * BLK), f32), pltpu.HBM((NDEV, WO_ROWS, D), bf16)],
        scratch_shapes=[pltpu.VMEM((KVW, BLK), f32), pltpu.VMEM((KVW, BLK), f32)]
        + [pltpu.VMEM((2, 2 * BLK, GROUP * BLK), f32)] * 2 + [pltpu.VMEM((2, 2 * BLK, GROUP * BLK), bf16)] * 2
        + _comm_sems(1),
        compiler_params=_params(("arbitrary",), 48),
    )(_sink_rows(sinks), *_in_hbm(qt, dot, kt, kt, vt, vt, dwo))


def _block_diag(w):
    w4 = w.reshape(NGRP, 4, HD, HD)
    eye = jnp.eye(4, dtype=w.dtype)
    return jnp.einsum('gjcd,jk->gjckd', w4, eye).reshape(NGRP, 256, 256).astype(bf16)


def _gate_terms(pr, pi, br, bi, sp):
    r = _sigmoid(pr + br)
    i = _sigmoid(pi + bi)
    la = -LRU_C * r * sp
    a = jnp.exp(la)
    x2 = 2.0 * la
    y = jnp.where(x2 > -0.02, -x2 * (1.0 + x2 * (0.5 + x2 * (1.0 / 6.0))), 1.0 - a * a)
    inv_mult = lax.rsqrt(jnp.maximum(y, 1e-30))
    return r, i, a, y * inv_mult, inv_mult


def _later(x, before, k):
    if k == 0:
        return x
    row = lax.broadcasted_iota(jnp.int32, before.shape, 0)
    rolled = pltpu.roll(x, k, 0)
    first = jnp.where(row < k, pltpu.roll(before, k, 0), rolled[0:8])
    return jnp.concatenate([first, rolled[8:]], axis=0)


def _earlier(x, after, k):
    if k == 0:
        return x
    n = x.shape[0]
    row = lax.broadcasted_iota(jnp.int32, after.shape, 0)
    rolled = pltpu.roll(x, n - k, 0)
    last = jnp.where(row >= 8 - k, pltpu.roll(after, 8 - k, 0), rolled[n - 8:n])
    return jnp.concatenate([rolled[0:n - 8], last], axis=0)


def _fwd_fused(h, wt, tabs, wo_shard, conv_w, conv_b, wr, wi, br, bi, lam, tm):
    s = h.shape[0]
    nt = s // tm
    nc = 512
    pieces = 8
    rows_per = tm // pieces
    later_chunks = (0, 1, 2, 3, 4, 7, 8)

    def body(h_ref, wt_ref, tab_ref, wo_ref, cw_ref, cb_ref, wr_ref, wi_ref, br_ref,
             bi_ref, lam_ref, q_ref, k_ref, v_ref, ga_ref, xl_ref, gl_ref, u_ref, hl_ref, r_ref, ig_ref, a_ref,
             im_ref, wo_all, wo_stage, halo, ub_scr, pr_scr, pi_scr, b_scr, hcar,
             send_sems, recv_sems, local_sems):
        i = pl.program_id(0)
        start, pass_on, finish = _gather_ops([wo_stage], [wo_all], send_sems, recv_sems, local_sems)

        @pl.when(i == 0)
        def _():
            wo_stage[...] = wo_ref[...].astype(bf16)
            start()
            halo[...] = jnp.zeros_like(halo)
            hcar[...] = jnp.zeros_like(hcar)

        sp = _softplus(-lam_ref[...])
        br, bi = br_ref[...], bi_ref[...]
        c, sa, sb = _tables(tab_ref)
        piece_rows = lambda p: slice(rows_per * p, rows_per * p + rows_per)

        def project(ci):
            z = _dot_nt(h_ref[...], wt_ref[ci * nc:(ci + 1) * nc, :])
            if ci < 2:
                for j in range(nc // 128):
                    r = _rope(z[:, 128 * j:128 * j + 128], c, sa, sb) * (HD ** -0.5)
                    q_ref[ci * nc + 128 * j:ci * nc + 128 * j + 128, :] = r.astype(bf16).T
            elif ci == 2:
                for j in range(2):
                    js = slice(128 * j, 128 * j + 128)
                    k_ref[js, :] = _rope(z[:, js], c, sa, sb).astype(bf16).T
                    v_ref[js, :] = z[:, KVW + 128 * j:KVW + 128 * j + 128].astype(bf16).T
            else:
                sec, j = divmod(ci - 3, 2)
                (ga_ref, xl_ref, gl_ref)[sec][:, j * nc:(j + 1) * nc] = z

        def gate_terms(p):
            rows = piece_rows(p)
            r, ig, a, mult, inv_mult = _gate_terms(pr_scr[rows, :], pi_scr[rows, :], br, bi, sp)
            r_ref[rows, :] = r
            ig_ref[rows, :] = ig
            a_ref[rows, :] = a
            im_ref[rows, :] = inv_mult
            b_scr[rows, :] = mult * (ig * u_ref[rows, :])

        def scan(p, hc):
            for t in range(rows_per * p, rows_per * p + rows_per):
                hc = a_ref[t:t + 1, :] * hc + b_scr[t:t + 1, :]
                hl_ref[t:t + 1, :] = hc
            return hc

        project(5)
        project(6)
        xl = xl_ref[...]
        u = cb_ref[...] + sum(cw_ref[k:k + 1, :] * _later(xl, halo[...], CONVW - 1 - k) for k in range(CONVW))
        halo[...] = xl[tm - 8:tm, :]
        u_ref[...] = u
        ub_scr[...] = u.astype(bf16)
        for g in range(NGRP):
            gs = slice(256 * g, 256 * g + 256)
            pr_scr[:, gs] = _dot(ub_scr[:, gs], wr_ref[g])
            pi_scr[:, gs] = _dot(ub_scr[:, gs], wi_ref[g])
        hc = hcar[...]
        gate_terms(0)
        for slot, ci in enumerate(later_chunks):
            project(ci)
            gate_terms(slot + 1)
            hc = scan(slot, hc)
        hcar[...] = scan(pieces - 1, hc)

        @pl.when(i == max(nt - 2, 0))
        def _():
            pass_on()

        @pl.when(i == nt - 1)
        def _():
            finish()

    row = lambda w: pl.BlockSpec((tm, w), lambda i: (i, 0))
    col = lambda w: pl.BlockSpec((w, tm), lambda i: (0, i))
    full = lambda a: pl.BlockSpec(a.shape, lambda i: (0,) * a.ndim)
    big = lambda w, dt: pltpu.HBM((s, w), dt)
    tile = pltpu.VMEM((tm, LW), f32)
    return pl.pallas_call(
        body, name="fwd_fused", grid=(nt,),
        in_specs=[row(D), full(wt), row(384), full(wo_shard), full(conv_w), full(conv_b),
                  full(wr), full(wi), full(br), full(bi), full(lam)],
        out_specs=[col(D), col(KVW), col(KVW), row(D), row(D), row(D)] + [row(LW)] * 6 + [HBM],
        out_shape=[pltpu.HBM((D, s), bf16), pltpu.HBM((KVW, s), bf16), pltpu.HBM((KVW, s), bf16),
                   big(D, f32), big(D, f32), big(D, f32)] + [big(LW, f32)] * 6 + [pltpu.HBM((2 * D, D), bf16)],
        scratch_shapes=[pltpu.VMEM((WO_ROWS, D), bf16), pltpu.VMEM((8, LW), f32),
                        pltpu.VMEM((tm, LW), bf16), tile, tile, tile, pltpu.VMEM((1, LW), f32)] + _comm_sems(1),
        compiler_params=_params(("arbitrary",), 56),
    )(*_in_hbm(h, wt), tabs, wo_shard, conv_w, conv_b, wr, wi, br, bi, lam)


def _lru_bwd(u, hl, dhl, xl, r, ig, a, im, conv_w, wr, wi, lam, tm):
    s = u.shape[0]
    nt = s // tm
    pieces = 8
    rows_per = tm // pieces

    def body(u_ref, h_ref, hp_ref, dh_ref, x_ref, r_ref, ig_ref, a_ref, im_ref, cw_ref, wr_ref, wi_ref,
             lam_ref, dxl_ref, dwr_ref, dwi_ref, dbr_ref, dbi_ref, dlam_ref, dcb_ref, dcw_ref,
             l_scr, du_scr, dpr_scr, dpi_scr, lcar, dunext):
        t0 = pl.program_id(0)
        tile = nt - 1 - t0

        @pl.when(t0 == 0)
        def _():
            lcar[...] = jnp.zeros_like(lcar)
            dunext[...] = jnp.zeros_like(dunext)
            for ref in (dwr_ref, dwi_ref, dbr_ref, dbi_ref, dlam_ref, dcb_ref, dcw_ref):
                ref[...] = jnp.zeros_like(ref)

        lam = lam_ref[...]
        sp = _softplus(-lam)
        hp = jnp.where(tile > 0, hp_ref[...], 0.0)

        def scan(p, c):
            for t in range(rows_per * p + rows_per - 1, rows_per * p - 1, -1):
                lt = dh_ref[t:t + 1, :] + c
                l_scr[t:t + 1, :] = lt
                c = a_ref[t:t + 1, :] * lt
            return c

        def terms(p, sums):
            rows = slice(rows_per * p, rows_per * p + rows_per)
            lt, u, r, i, a, inv_mult = l_scr[rows, :], u_ref[rows, :], r_ref[rows, :], ig_ref[rows, :], \
                a_ref[rows, :], im_ref[rows, :]
            before = hp if p == 0 else h_ref[rows_per * p - 8:rows_per * p, :]
            hprev = _later(h_ref[rows, :], before, 1)
            x2 = -2.0 * LRU_C * r * sp
            mult = jnp.where(x2 > -0.02, -x2 * (1.0 + x2 * (0.5 + x2 * (1.0 / 6.0))), 1.0 - a * a) * inv_mult
            da = lt * hprev
            dmult = lt * (i * u)
            di = lt * mult * u
            du_scr[rows, :] = lt * mult * i
            dla = da * a - dmult * (a * a) * inv_mult
            dr = dla * (-LRU_C * sp)
            dpr = dr * r * (1.0 - r)
            dpi = di * i * (1.0 - i)
            dpr_scr[rows, :] = dpr.astype(bf16)
            dpi_scr[rows, :] = dpi.astype(bf16)
            col = lambda t: jnp.sum(t, axis=0, keepdims=True)
            return sums[0] + col(dla * (-LRU_C * r)), sums[1] + col(dpr), sums[2] + col(dpi)

        sums = (jnp.zeros((1, LW), f32),) * 3
        c = scan(pieces - 1, lcar[...])
        for p in range(pieces - 1, -1, -1):
            if p > 0:
                c = scan(p - 1, c)
            sums = terms(p, sums)
        lcar[...] = c
        dlam_ref[...] += sums[0]
        dbr_ref[...] += sums[1]
        dbi_ref[...] += sums[2]

        ub = u_ref[...].astype(bf16)
        dug = []
        for g in range(NGRP):
            gs = slice(256 * g, 256 * g + 256)
            dwr_ref[g] += _dot_tn(ub[:, gs], dpr_scr[:, gs])
            dwi_ref[g] += _dot_tn(ub[:, gs], dpi_scr[:, gs])
            dug.append(_dot_nt(dpr_scr[:, gs], wr_ref[g]) + _dot_nt(dpi_scr[:, gs], wi_ref[g]))
        du = du_scr[...] + jnp.concatenate(dug, axis=1)

        dcb_ref[...] += jnp.sum(du, axis=0, keepdims=True)
        x = x_ref[...]
        after = dunext[...]
        dxl = jnp.zeros_like(du)
        for k in range(CONVW):
            e = _earlier(du, after, CONVW - 1 - k)
            dxl = dxl + cw_ref[k:k + 1, :] * e
            dcw_ref[k:k + 1, :] += jnp.sum(e * x, axis=0, keepdims=True)
        dxl_ref[...] = dxl.astype(bf16)
        dunext[...] = du[0:8, :]

        @pl.when(t0 == nt - 1)
        def _():
            dlam_ref[...] = dlam_ref[...] * (-_sigmoid(-lam))

    rev = lambda i: (nt - 1 - i, 0)
    row = pl.BlockSpec((tm, LW), rev)
    prev8 = pl.BlockSpec((8, LW), lambda i: (jnp.maximum((nt - 1 - i) * (tm // 8) - 1, 0), 0))
    full = lambda a: pl.BlockSpec(a.shape, lambda i: (0,) * a.ndim)
    vec = pl.BlockSpec((1, LW), lambda i: (0, 0))
    bd = pl.BlockSpec((NGRP, 256, 256), lambda i: (0, 0, 0))
    return pl.pallas_call(
        body, name="lru_bwd", grid=(nt,),
        in_specs=[row, row, prev8, row, row, row, row, row, row, full(conv_w), full(wr), full(wi), full(lam)],
        out_specs=[row, bd, bd, vec, vec, vec, vec, pl.BlockSpec((CONVW, LW), lambda i: (0, 0))],
        out_shape=[pltpu.HBM((s, LW), bf16),
                   jax.ShapeDtypeStruct((NGRP, 256, 256), f32), jax.ShapeDtypeStruct((NGRP, 256, 256), f32),
                   jax.ShapeDtypeStruct((1, LW), f32), jax.ShapeDtypeStruct((1, LW), f32),
                   jax.ShapeDtypeStruct((1, LW), f32), jax.ShapeDtypeStruct((1, LW), f32),
                   jax.ShapeDtypeStruct((CONVW, LW), f32)],
        scratch_shapes=[pltpu.VMEM((tm, LW), f32), pltpu.VMEM((tm, LW), f32), pltpu.VMEM((tm, LW), bf16),
                        pltpu.VMEM((tm, LW), bf16), pltpu.VMEM((1, LW), f32), pltpu.VMEM((8, LW), f32)],
        compiler_params=_params(("arbitrary",), 56),
    )(*_in_hbm(u, hl, hl, dhl, xl, r, ig, a, im), conv_w, wr, wi, lam)


def _gated_norm(t, gate, gain):
    sg = _sigmoid(gate)
    silu = gate * sg
    p = t * silu
    rstd = lax.rsqrt(jnp.mean(p * p, axis=-1, keepdims=True) + EPS)
    ph = p * rstd
    return sg, silu, rstd, ph, ph * gain


def _gated_norm_bwd(dy, t, gate, gain, sg, silu, rstd, ph):
    w = dy * gain
    dp = rstd * (w - ph * jnp.mean(w * ph, axis=-1, keepdims=True))
    dgate = dp * t * (sg * (1.0 + gate * (1.0 - sg)))
    return jnp.sum(dy * ph, axis=0, keepdims=True), dp * silu, dgate


def _out_fwd_bwd(x, tgt, o, ga, hl, gl, again, lgain, fgain, wo, tm):
    s = x.shape[0]
    nt = s // tm

    def body(x_ref, t_ref, o_ref, ga_ref, hl_ref, gl_ref, ag_ref, lg_ref, fg_ref, wo_ref,
             dx2_ref, do_ref, dga_ref, dhl_ref, dgl_ref, dwo_ref, gfg_ref, gag_ref, glg_ref, loss_ref, acc):
        i = pl.program_id(0)

        @pl.when(i == 0)
        def _():
            acc[...] = jnp.zeros_like(acc)
            for ref in (gfg_ref, gag_ref, glg_ref, loss_ref):
                ref[...] = jnp.zeros_like(ref)

        oo = jnp.concatenate([o_ref[128 * j:128 * j + 128, :].T for j in range(D // 128)], axis=1)
        gga, hh, ggl = ga_ref[...], hl_ref[...], gl_ref[...]
        ag, lg, fg = ag_ref[...], lg_ref[...], fg_ref[...]
        sga, silua, ra, pah, ya = _gated_norm(oo, gga, ag)
        sgl, silul, rl, plh, yl = _gated_norm(hh, ggl, lg)
        yab, ylb = ya.astype(bf16), yl.astype(bf16)
        y = _dot(yab, wo_ref[0:D, :]) + _dot(ylb, wo_ref[D:2 * D, :])
        x2 = x_ref[...] + y
        r2 = lax.rsqrt(jnp.mean(x2 * x2, axis=-1, keepdims=True) + EPS)
        x2h = x2 * r2
        err = x2h * fg - t_ref[...]
        loss_ref[...] += 0.5 * jnp.sum(jnp.sum(err * err, axis=-1, keepdims=True) * (1.0 / D))
        dout = err * (1.0 / D)
        gfg_ref[...] += jnp.sum(dout * x2h, axis=0, keepdims=True)
        w = dout * fg
        dx2 = r2 * (w - x2h * jnp.mean(w * x2h, axis=-1, keepdims=True))
        dx2_ref[...] = dx2
        dyb = dx2.astype(bf16)
        acc[0:D, :] += _dot_tn(yab, dyb)
        acc[D:2 * D, :] += _dot_tn(ylb, dyb)
        dya = _dot_nt(dyb, wo_ref[0:D, :])
        dyl = _dot_nt(dyb, wo_ref[D:2 * D, :])
        gag, do, dga = _gated_norm_bwd(dya, oo, gga, ag, sga, silua, ra, pah)
        glg, dhl, dgl = _gated_norm_bwd(dyl, hh, ggl, lg, sgl, silul, rl, plh)
        gag_ref[...] += gag
        glg_ref[...] += glg
        dob = do.astype(bf16)
        for j in range(D // 128):
            do_ref[128 * j:128 * j + 128, :] = dob[:, 128 * j:128 * j + 128].T
        dga_ref[...] = dga.astype(bf16)
        dhl_ref[...] = dhl
        dgl_ref[...] = dgl.astype(bf16)

        @pl.when(i == nt - 1)
        def _():
            dwo_ref[...] = acc[...].astype(bf16)

    row = pl.BlockSpec((tm, D), lambda i: (i, 0))
    col = pl.BlockSpec((D, tm), lambda i: (0, i))
    vec = pl.BlockSpec((1, D), lambda i: (0, 0))
    mat = pl.BlockSpec((2 * D, D), lambda i: (0, 0))
    return pl.pallas_call(
        body, name="out_fwd_bwd", grid=(nt,),
        in_specs=[row, row, col, row, row, row] + [vec] * 3 + [mat],
        out_specs=[row, col, row, row, row] + [mat, vec, vec, vec, pl.BlockSpec((1, 128), lambda i: (0, 0))],
        out_shape=[pltpu.HBM((s, D), f32), pltpu.HBM((D, s), bf16),
                   pltpu.HBM((s, D), bf16), pltpu.HBM((s, D), f32),
                   pltpu.HBM((s, D), bf16), pltpu.HBM((2 * D, D), bf16),
                   jax.ShapeDtypeStruct((1, D), f32), jax.ShapeDtypeStruct((1, D), f32),
                   jax.ShapeDtypeStruct((1, D), f32), jax.ShapeDtypeStruct((1, 128), f32)],
        scratch_shapes=[pltpu.VMEM((2 * D, D), f32)],
        compiler_params=_params(("arbitrary",), 56),
    )(*_in_hbm(x, tgt, o, ga, hl, gl), again, lgain, fgain, *_in_hbm(wo))


def _bwd_in(x, dx2, dq, dk, dv, dga, dxl, dgl, ln_gain, wt, tabs, tm):
    s = x.shape[0]

    def body(x_ref, dx2_ref, dq_ref, dk_ref, dv_ref, dga_ref, dxl_ref, dgl_ref, g_ref, wt_ref,
             tab_ref, gx_ref, gln_ref, dzt_ref):
        @pl.when(pl.program_id(0) == 0)
        def _():
            gln_ref[...] = jnp.zeros_like(gln_ref)

        c, sa, sb = (t.T for t in _tables(tab_ref))
        for j in range(D // 128):
            js = slice(128 * j, 128 * j + 128)
            dzt_ref[js, :] = (_unrope_t(dq_ref[js, :], c, sa, sb) * (HD ** -0.5)).astype(bf16)
        for j in range(KVW // 128):
            js = slice(128 * j, 128 * j + 128)
            dzt_ref[D + 128 * j:D + 128 * j + 128, :] = _unrope_t(dk_ref[js, :], c, sa, sb).astype(bf16)
        dzt_ref[D + KVW:D + 2 * KVW, :] = dv_ref[...].astype(bf16)
        first = D + 2 * KVW
        dh = _dot_tn(dzt_ref[0:512, :], wt_ref[0:512, :])
        for ci in range(1, first // 512):
            dh = dh + _dot_tn(dzt_ref[512 * ci:512 * ci + 512, :], wt_ref[512 * ci:512 * ci + 512, :])
        for sec, ref in enumerate((dga_ref, dxl_ref, dgl_ref)):
            for j in range(D // 512):
                rows = slice(first + D * sec + 512 * j, first + D * sec + 512 * j + 512)
                dh = dh + _dot(ref[:, 512 * j:512 * j + 512], wt_ref[rows, :])
            for j in range(D // 128):
                dzt_ref[first + D * sec + 128 * j:first + D * sec + 128 * j + 128, :] = ref[:, 128 * j:128 * j + 128].T
        xx = x_ref[...]
        rstd = lax.rsqrt(jnp.mean(xx * xx, axis=-1, keepdims=True) + EPS)
        xh = xx * rstd
        gln_ref[...] += jnp.sum(dh * xh, axis=0, keepdims=True)
        w = dh * g_ref[...]
        gx_ref[...] = dx2_ref[...] + rstd * (w - xh * jnp.mean(w * xh, axis=-1, keepdims=True))

    row = lambda w: pl.BlockSpec((tm, w), lambda i: (i, 0))
    col = lambda w: pl.BlockSpec((w, tm), lambda i: (0, i))
    full = lambda a: pl.BlockSpec(a.shape, lambda i: (0, 0))
    return pl.pallas_call(
        body, name="bwd_in", grid=(s // tm,),
        in_specs=[row(D), row(D), col(D), col(KVW), col(KVW), row(D), row(D), row(D), full(ln_gain), full(wt),
                  row(384)],
        out_specs=[row(D), pl.BlockSpec((1, D), lambda i: (0, 0)), col(NIN)],
        out_shape=[pltpu.HBM((s, D), f32), jax.ShapeDtypeStruct((1, D), f32),
                   pltpu.HBM((NIN, s), bf16)],
        compiler_params=_params(("arbitrary",), 56),
    )(*_in_hbm(x, dx2, dq, dk, dv, dga, dxl, dgl), ln_gain, *_in_hbm(wt), tabs)


WT_TERMS = 4


def _dwt_scatter(dzt, h, small, tm):
    s = h.shape[0]
    nk = s // tm
    srows = small.shape[0] // NDEV
    last = NDEV - 1
    sm_turn = 2

    def body(order_ref, dz_ref, h_ref, sm_ref, lwt_ref, rep_all, tail_ref, acc, stage, given, relayed, lsm, rep_stage,
             send_sems, recv_sems, local_sem, sm_send, sm_recv, sm_local, rep_send, rep_recv, rep_local):
        j, k = pl.program_id(0), pl.program_id(1)
        x, y, c = _place()
        sibling = (x, y, 1 - c)
        near = (x ^ (1 - c), y ^ c)
        far = (x ^ c, y ^ (1 - c))
        sm_start, sm_finish = _scatter_ops([sm_ref], [lsm], sm_send, sm_recv, sm_local)
        rep_start, rep_pass_on, rep_finish = _gather_ops([rep_stage], [rep_all], rep_send, rep_recv, rep_local)

        def send(step):
            if step == last - 1:
                dst, to = lwt_ref.at[1], sibling
            elif step % 2 == 0:
                dst, to = given.at[step // 2], sibling
            elif step == 1:
                dst, to = relayed, (*near, c)
            else:
                dst, to = lwt_ref.at[1 + step // 2], (*(near if step == 3 else far), c)
            return pltpu.make_async_remote_copy(
                src_ref=stage.at[step % 2], dst_ref=dst, send_sem=send_sems.at[step], recv_sem=recv_sems.at[step],
                device_id=to, device_id_type=MESH)

        def keep():
            return pltpu.make_async_copy(stage.at[last % 2], lwt_ref.at[0], local_sem)

        @pl.when((j == 0) & (k == 0))
        def _():
            sm_start()

        @pl.when(k == 0)
        def _():
            acc[...] = jnp.zeros_like(acc)

        acc[...] += _dot(dz_ref[...], h_ref[...])

        for step in range(NDEV):
            @pl.when((k == nk - 1) & (j == step))
            def _(step=step):
                if step >= 2:
                    send(step - 2).wait_send()
                if step % 2 == 1 and step < last:
                    send(step - 1).wait_recv()
                    total = acc[...] + given[step // 2].astype(f32)
                    if step == 5:
                        send(1).wait_recv()
                        total = total + relayed[...].astype(f32)
                    stage[step % 2] = total.astype(bf16)
                else:
                    stage[step % 2] = acc[...].astype(bf16)
                if step < last:
                    send(step).start()
                else:
                    keep().start()
                    send(last - 1).wait_send()
                    for peer_step in (3, 5, last - 1):
                        send(peer_step).wait_recv()
                    keep().wait()
                    rep_finish()
                if step == sm_turn:
                    sm_finish()
                    total_sm = lsm[0]
                    for dev in range(1, NDEV):
                        total_sm = total_sm + lsm[dev]
                    rep_stage[...] = total_sm[0:SMALL_PER]
                    tail_ref[...] = total_sm[SMALL_PER:]
                    rep_start()
                if step == last - 1:
                    rep_pass_on()

    x, y, c = _place()
    dest = lambda chip, cc: 4 * chip[0] + 2 * chip[1] + cc
    near, far, diag = (x ^ (1 - c), y ^ c), (x ^ c, y ^ (1 - c)), (1 - x, 1 - y)
    order = jnp.stack([dest(diag, 1 - c), dest(diag, c), dest(far, 1 - c), dest(near, c),
                       dest(near, 1 - c), dest(far, c), dest((x, y), 1 - c), dest((x, y), c)])
    return pl.pallas_call(
        body, name="dwt_scatter",
        grid_spec=pltpu.PrefetchScalarGridSpec(
            num_scalar_prefetch=1, grid=(NDEV, nk),
            in_specs=[pl.BlockSpec((WT_ROWS, tm), lambda j, k, order: (order[j], k)),
                      pl.BlockSpec((tm, D), lambda j, k, order: (k, 0)), HBM],
            out_specs=[HBM, HBM, pl.BlockSpec((srows - SMALL_PER, D), lambda j, k, order: (0, 0))],
            scratch_shapes=[pltpu.VMEM((WT_ROWS, D), f32), pltpu.VMEM((2, WT_ROWS, D), bf16),
                            pltpu.VMEM((3, WT_ROWS, D), bf16), pltpu.VMEM((WT_ROWS, D), bf16),
                            pltpu.VMEM((NDEV, srows, D), f32), pltpu.VMEM((SMALL_PER, D), f32),
                            pltpu.SemaphoreType.DMA((last,)), pltpu.SemaphoreType.DMA((last,)),
                            pltpu.SemaphoreType.DMA(())] + _comm_sems(1) + _comm_sems(1)),
        out_shape=[pltpu.HBM((WT_TERMS, WT_ROWS, D), bf16), pltpu.HBM((SMALL_ROWS, D), f32),
                   jax.ShapeDtypeStruct((srows - SMALL_PER, D), f32)],
        compiler_params=_params(("arbitrary", "arbitrary"), 48),
    )(order, *_in_hbm(dzt, h, small))


def _diag_blocks(bd):
    eye = jnp.eye(4, dtype=bd.dtype)
    return jnp.einsum('gjckd,jk->gjcd', bd.reshape(NGRP, 4, HD, 4, HD), eye).reshape(NQ, HD, HD)


def _sequence_step(x, h, tgt, wt, wo_shard, conv_w, p):
    s = x.shape[0]
    tm = min(256, s)
    tabs = _rope_tables(s)
    wr, wi = _block_diag(p["w_rgate"]), _block_diag(p["w_igate"])
    sinks = p["sinks"].reshape(NQ)
    qt, kt, vt, ga, xl, gl, u, hl, r, ig, a, im, wo = _fwd_fused(
        h, wt, tabs, wo_shard, conv_w, p["conv_b"], wr, wi, p["b_rgate"], p["b_igate"], p["lru_lambda"], tm)
    ot = _attn_fwd_t(qt, kt, vt, sinks)
    dx2, dot, dga, dhl, dgl, dwo, g_fg, g_ag, g_lg, loss = _out_fwd_bwd(
        x, tgt, ot, ga, hl, gl, p["attn_out_gain"], p["lru_out_gain"], p["final_gain"], wo, tm)
    dqt, dkt, dvt, dsink, land_wo = _attn_bwd_t(qt, kt, vt, dot, sinks, dwo)
    dxl, dwr, dwi, dbr, dbi, dlam, dcb, dcw = _lru_bwd(u, hl, dhl, xl, r, ig, a, im, conv_w, wr, wi, p["lru_lambda"], tm)
    gx, g_ln, dzt = _bwd_in(x, dx2, dqt, dkt, dvt, dga, dxl, dgl, p["ln_gain"], wt, tabs, tm)
    small = dict(ln_gain=g_ln, sinks=dsink.reshape(NQ, BLK).sum(axis=1)[None], conv_w=dcw, conv_b=dcb,
                 w_rgate=_diag_blocks(dwr), b_rgate=dbr, w_igate=_diag_blocks(dwi), b_igate=dbi, lru_lambda=dlam,
                 attn_out_gain=g_ag, lru_out_gain=g_lg, final_gain=g_fg)
    land_wt, g_rep, g_tail = _dwt_scatter(dzt, h, _pack_small(small, loss), min(2048, s))
    return gx, land_wt, land_wo, g_rep, g_tail


def _gather_weights(wt_shard, conv_blk, x, ln_gain, tm):
    s = x.shape[0]

    def body(wt_ref, cw_ref, g_ref, x_ref, wt_all, cw_all, h_ref, stage, xbuf, hbuf, send_sems, recv_sems, local_sems):
        stage[...] = wt_ref[...].astype(bf16)
        start, finish = _relay_gather_ops([stage, cw_ref], [wt_all, cw_all], send_sems, recv_sems, local_sems)
        start()
        gain = g_ref[...]
        for i in range(s // tm):
            rows = pl.ds(i * tm, tm)
            pltpu.sync_copy(x_ref.at[rows, :], xbuf)
            xx = xbuf[...]
            rstd = lax.rsqrt(jnp.mean(xx * xx, axis=-1, keepdims=True) + EPS)
            hbuf[...] = (xx * rstd * gain).astype(bf16)
            pltpu.sync_copy(hbuf, h_ref.at[rows, :])
        finish()

    vmem = pl.BlockSpec(memory_space=pltpu.VMEM)
    return pl.pallas_call(
        body, name="gather_weights",
        in_specs=[vmem, vmem, vmem, HBM], out_specs=[HBM, HBM, HBM],
        out_shape=[pltpu.HBM((NIN, D), bf16), pltpu.HBM((NDEV * 8, 128), f32), pltpu.HBM((s, D), bf16)],
        scratch_shapes=[pltpu.VMEM((WT_ROWS, D), bf16), pltpu.VMEM((tm, D), f32), pltpu.VMEM((tm, D), bf16)]
        + _comm_sems(2),
        compiler_params=pltpu.CompilerParams(vmem_limit_bytes=32 * MIB),
    )(wt_shard, conv_blk, ln_gain, *_in_hbm(x))


def _adam_math(w, g, m, v):
    m2 = ADAM_B1 * m + (1.0 - ADAM_B1) * g
    v2 = ADAM_B2 * v + (1.0 - ADAM_B2) * (g * g)
    m_hat = m2 / (1.0 - ADAM_B1 ** ADAM_STEP)
    v_hat = v2 / (1.0 - ADAM_B2 ** ADAM_STEP)
    delta = -ADAM_LR * (m_hat / (jnp.sqrt(v_hat) + ADAM_EPS) + ADAM_WD * w)
    return delta, m2, v2


def _reduce_adamw(land, w, m, v, tr, name):
    terms, rows, cols = land.shape

    def body(l_ref, w_ref, m_ref, v_ref, g_ref, d_ref, m2_ref, v2_ref):
        g = l_ref[0].astype(f32)
        for t in range(1, terms):
            g = g + l_ref[t].astype(f32)
        g_ref[...] = g
        d_ref[...], m2_ref[...], v2_ref[...] = _adam_math(w_ref[...], g, m_ref[...], v_ref[...])

    blk = pl.BlockSpec((tr, cols), lambda i: (i, 0))
    return pl.pallas_call(
        body, name=name, grid=(rows // tr,),
        in_specs=[pl.BlockSpec((terms, tr, cols), lambda i: (0, i, 0))] + [blk] * 3, out_specs=[blk] * 4,
        out_shape=[jax.ShapeDtypeStruct((rows, cols), f32)] * 4,
        compiler_params=_params(("arbitrary",), 32),
    )(*_in_hbm(land), w, m, v)


VEC_NAMES = ("ln_gain", "conv_b", "b_rgate", "b_igate", "lru_lambda", "attn_out_gain", "lru_out_gain", "final_gain")
ROW_RGATE, ROW_IGATE, ROW_VEC, ROW_SINKS = 0, 64, 128, 136
LOSS_LANE = NQ


def _adamw_small(g_rep, g_conv, w, m, v):
    names = list(VEC_NAMES) + ["sinks", "conv_w", "w_rgate", "w_igate"]
    ins = [g_rep, g_conv] + [d[k] for k in names for d in (w, m, v)]

    def body(*refs):
        g_ref, gc_ref = refs[0], refs[1]
        in_refs = refs[2:2 + 3 * len(names)]
        out_refs = refs[2 + 3 * len(names):]

        def update(j, g, at=None):
            w_ref, m_ref, v_ref = in_refs[3 * j:3 * j + 3]
            outs = out_refs[4 * j:4 * j + 4]
            pick = (lambda r: r[...]) if at is None else (lambda r: r[at])
            res = (g,) + _adam_math(pick(w_ref), g, pick(m_ref), pick(v_ref))
            for o_ref, val in zip(outs, res):
                if at is None:
                    o_ref[...] = val
                else:
                    o_ref[at] = val

        for j in range(len(VEC_NAMES)):
            update(j, g_ref[ROW_VEC + j:ROW_VEC + j + 1, :])
        update(len(VEC_NAMES), g_ref[ROW_SINKS:ROW_SINKS + 1, 0:NQ])
        update(len(VEC_NAMES) + 1, gc_ref[...], at=0)
        for gi, row0 in ((len(VEC_NAMES) + 2, ROW_RGATE), (len(VEC_NAMES) + 3, ROW_IGATE)):
            for nb in range(NQ):
                update(gi, g_ref[row0:row0 + HD, HD * nb:HD * nb + HD], at=(0, nb))

    vmem = pl.BlockSpec(memory_space=pltpu.VMEM)
    out_shape = [jax.ShapeDtypeStruct(w[k].shape, f32) for k in names for _ in range(4)]
    outs = pl.pallas_call(
        body, name="adamw_small",
        in_specs=[vmem] * len(ins), out_specs=[vmem] * len(out_shape), out_shape=out_shape,
        compiler_params=pltpu.CompilerParams(vmem_limit_bytes=32 * MIB),
    )(*ins)
    return {k: tuple(outs[4 * j:4 * j + 4]) for j, k in enumerate(names)}


def _pack_small(small, loss):
    gate = lambda g: g.transpose(1, 0, 2).reshape(HD, NQ * HD)
    row_s = jnp.concatenate([small["sinks"], loss[:, LOSS_LANE:128], jnp.zeros((1, D - 128), f32)], axis=1)
    rep = jnp.concatenate([gate(small["w_rgate"]), gate(small["w_igate"])] + [small[k] for k in VEC_NAMES]
                          + [row_s, jnp.zeros((SMALL_ROWS - ROW_SINKS - 1, D), f32)], axis=0)
    conv = small["conv_w"].reshape(CONVW, NDEV, 128).transpose(1, 0, 2)
    conv = jnp.pad(conv, ((0, 0), (0, 8 - CONVW), (0, D - 128)))
    return jnp.concatenate([rep.reshape(NDEV, SMALL_PER, D), conv], axis=1).reshape(NDEV * (SMALL_PER + 8), D)


def kernel(x, ln_gain, w_in, sinks, conv_w, conv_b, w_rgate, b_rgate, w_igate, b_igate, lru_lambda, attn_out_gain, lru_out_gain, w_out, final_gain, loss_target, m_ln_gain, m_w_in, m_sinks, m_conv_w, m_conv_b, m_w_rgate, m_b_rgate, m_w_igate, m_b_igate, m_lru_lambda, m_attn_out_gain, m_lru_out_gain, m_w_out, m_final_gain, v_ln_gain, v_w_in, v_sinks, v_conv_w, v_conv_b, v_w_rgate, v_b_rgate, v_w_igate, v_b_igate, v_lru_lambda, v_attn_out_gain, v_lru_out_gain, v_w_out, v_final_gain):
    w = dict(ln_gain=ln_gain, sinks=sinks, conv_w=conv_w, conv_b=conv_b, w_rgate=w_rgate, b_rgate=b_rgate,
             w_igate=w_igate, b_igate=b_igate, lru_lambda=lru_lambda, attn_out_gain=attn_out_gain,
             lru_out_gain=lru_out_gain, final_gain=final_gain.reshape(1, D))
    m = dict(ln_gain=m_ln_gain, sinks=m_sinks, conv_w=m_conv_w, conv_b=m_conv_b, w_rgate=m_w_rgate,
             b_rgate=m_b_rgate, w_igate=m_w_igate, b_igate=m_b_igate, lru_lambda=m_lru_lambda,
             attn_out_gain=m_attn_out_gain, lru_out_gain=m_lru_out_gain, final_gain=m_final_gain.reshape(1, D))
    v = dict(ln_gain=v_ln_gain, sinks=v_sinks, conv_w=v_conv_w, conv_b=v_conv_b, w_rgate=v_w_rgate,
             b_rgate=v_b_rgate, w_igate=v_w_igate, b_igate=v_b_igate, lru_lambda=v_lru_lambda,
             attn_out_gain=v_attn_out_gain, lru_out_gain=v_lru_out_gain, final_gain=v_final_gain.reshape(1, D))

    conv_blk = jnp.pad(conv_w[0], ((0, 8 - CONVW), (0, 0)))
    wt, cw_all, h = _gather_weights(w_in[0].T, conv_blk, x[0], ln_gain, min(512, x.shape[1]))
    conv_full = cw_all.reshape(NDEV, 8, 128)[:, 0:CONVW].transpose(1, 0, 2).reshape(CONVW, LW)

    p = {k: (w[k][0] if k in ("w_rgate", "w_igate") else w[k]) for k in w if k != "conv_w"}
    gx, land_wt, land_wo, g_rep, g_tail = _sequence_step(x[0], h, loss_target[0], wt, w_out[0], conv_full, p)
    g_conv = g_tail[0:CONVW, 0:128]

    wins = _reduce_adamw(land_wt, w_in[0].T, m_w_in[0].T, v_w_in[0].T, 192, "adamw_w_in")
    g_win, d_win, m_win, v_win = (t.T for t in wins)
    g_wo, d_wo, m_wo, v_wo = _reduce_adamw(land_wo, w_out[0], m_w_out[0], v_w_out[0], 256, "adamw_w_out")
    res = _adamw_small(g_rep, g_conv, w, m, v)
    res["w_in"] = tuple(t[None] for t in (g_win, d_win, m_win, v_win))
    res["w_out"] = tuple(t[None] for t in (g_wo, d_wo, m_wo, v_wo))
    res["final_gain"] = tuple(t.reshape(D) for t in res["final_gain"])

    order = ("ln_gain", "w_in", "sinks", "conv_w", "conv_b", "w_rgate", "b_rgate", "w_igate", "b_igate",
             "lru_lambda", "attn_out_gain", "lru_out_gain", "w_out", "final_gain")
    total_loss = g_rep[ROW_SINKS, LOSS_LANE]
    return (total_loss, gx[None]) + tuple(res[k][i] for i in range(4) for k in order)
```

```python
import jax
import jax.numpy as jnp
from jax import lax
from jax.experimental import pallas as pl
from jax.experimental.pallas import tpu as pltpu

f32 = jnp.float32
bf16 = jnp.bfloat16

D = 1024
HD = 64
NQ = 16
NKV = 4
GROUP = NQ // NKV
KVW = NKV * HD
BLK = 128
ROT = 16
THETA = 500000.0
NEG = -1e30
LW = 1024
NGRP = 4
CONVW = 4
LRU_C = 8.0
NIN = 4608
EPS = 1e-6
NDEV = 8
WT_ROWS = NIN // NDEV
WO_ROWS = 2 * D // NDEV
SMALL_ROWS = 192
SMALL_PER = SMALL_ROWS // NDEV

ADAM_LR = 0.001
ADAM_B1 = 0.9
ADAM_B2 = 0.999
ADAM_EPS = 1e-08
ADAM_WD = 0.01
ADAM_STEP = 10

NT = (((1,), (1,)), ((), ()))
TN = (((0,), (0,)), ((), ()))
MESH = pl.DeviceIdType.MESH
MIB = 1024 * 1024


def _dot(a, b):
    return jnp.dot(a, b, preferred_element_type=f32)


def _dot_nt(a, b):
    return lax.dot_general(a, b, NT, preferred_element_type=f32)


def _dot_tn(a, b):
    return lax.dot_general(a, b, TN, preferred_element_type=f32)


def _params(sem, vmem_mib):
    return pltpu.CompilerParams(dimension_semantics=sem, vmem_limit_bytes=vmem_mib * MIB)


def _sigmoid(x):
    return 0.5 * jnp.tanh(0.5 * x) + 0.5


def _softplus(x):
    return jnp.maximum(x, 0.0) + jnp.log(1.0 + jnp.exp(-jnp.abs(x)))


def _rope_tables(s):
    pos = jnp.arange(s, dtype=f32)
    inv_freq = THETA ** (-jnp.arange(0, ROT, 2, dtype=f32) / ROT)
    ang = pos[:, None] * inv_freq[None, :]
    cs = jnp.concatenate([jnp.cos(ang) - 1.0, jnp.sin(ang)], axis=1)
    d = jnp.arange(128) % HD
    j = jnp.arange(ROT)[:, None]
    pick_c = ((d < ROT) & (j == d % (ROT // 2))).astype(f32)
    pick_sa = ((d >= ROT // 2) & (d < ROT) & (j == d)).astype(f32)
    pick_sb = -((d < ROT // 2) & (j == d + ROT // 2)).astype(f32)
    picks = jnp.concatenate([pick_c, pick_sa, pick_sb], axis=1)
    ones = jnp.concatenate([jnp.ones((1, 128), f32), jnp.zeros((1, 256), f32)], axis=1)
    return jnp.dot(cs, picks, precision=lax.Precision.HIGHEST) + ones


def _tables(tab_ref):
    return tab_ref[:, 0:128], tab_ref[:, 128:256], tab_ref[:, 256:384]


def _rope(t, c, sa, sb):
    return t * c + pltpu.roll(t, 8, 1) * sa + pltpu.roll(t, 120, 1) * sb


def _unrope_t(dr, c, sa, sb):
    return dr * c + pltpu.roll(dr * sa, 120, 0) + pltpu.roll(dr * sb, 8, 0)


def _place():
    return lax.axis_index("x"), lax.axis_index("y"), lax.axis_index("c")


def _gather_ops(mine_refs, out_refs, send_sems, recv_sems, local_sems):
    n = len(mine_refs)
    x, y, c = _place()
    me, sibling = (x, y, c), (x, y, 1 - c)
    chips = [(1 - x, y), (x, 1 - y), (1 - x, 1 - y)]

    def rows(a, dev):
        m = mine_refs[a].shape[0]
        return out_refs[a].at[pl.ds((4 * dev[0] + 2 * dev[1] + dev[2]) * m, m), :]

    def copy(a, k, block, to, own=False):
        return pltpu.make_async_remote_copy(
            src_ref=mine_refs[a] if own else rows(a, block), dst_ref=rows(a, block),
            send_sem=send_sems.at[a, k], recv_sem=recv_sems.at[a, k], device_id=to, device_id_type=MESH)

    def local(a):
        return pltpu.make_async_copy(mine_refs[a], rows(a, me), local_sems.at[a])

    def first(a):
        return [copy(a, 0, me, sibling, own=True)] + [copy(a, 1 + j, me, (*chip, c), own=True)
                                                      for j, chip in enumerate(chips)]

    def start():
        for a in range(n):
            local(a).start()
            for cp in first(a):
                cp.start()

    def pass_on():
        for j, chip in enumerate(chips):
            for a in range(n):
                copy(a, 1 + j, (*chip, c), me).wait_recv()
                copy(a, 4 + j, (*chip, c), sibling).start()

    def finish():
        for a in range(n):
            copy(a, 0, sibling, me).wait_recv()
            for j, chip in enumerate(chips):
                copy(a, 4 + j, (*chip, 1 - c), me).wait_recv()
        for a in range(n):
            for cp in first(a) + [copy(a, 4 + j, (*chip, c), sibling) for j, chip in enumerate(chips)]:
                cp.wait_send()
            local(a).wait()

    return start, pass_on, finish


def _relay_gather_ops(mine_refs, out_refs, send_sems, recv_sems, local_sems):
    n = len(mine_refs)
    x, y, c = _place()
    me, sibling = (x, y, c), (x, y, 1 - c)
    near = (x ^ (1 - c), y ^ c)
    far = (x ^ c, y ^ (1 - c))
    diag = (1 - x, 1 - y)

    def rows(a, dev):
        m = mine_refs[a].shape[0]
        return out_refs[a].at[pl.ds((4 * dev[0] + 2 * dev[1] + dev[2]) * m, m), :]

    def copy(a, k, block, to, own=False):
        return pltpu.make_async_remote_copy(
            src_ref=mine_refs[a] if own else rows(a, block), dst_ref=rows(a, block),
            send_sem=send_sems.at[a, k], recv_sem=recv_sems.at[a, k], device_id=to, device_id_type=MESH)

    def local(a):
        return pltpu.make_async_copy(mine_refs[a], rows(a, me), local_sems.at[a])

    def sends(a):
        return [copy(a, 0, me, sibling, own=True), copy(a, 1, me, (*near, c), own=True),
                copy(a, 2, me, (*far, c), own=True), copy(a, 3, (*near, c), (*far, c)),
                copy(a, 4, (*near, c), sibling), copy(a, 5, (*far, c), sibling), copy(a, 6, (*diag, c), sibling)]

    def arrivals(a):
        return [copy(a, 0, sibling, me), copy(a, 1, (*near, c), me), copy(a, 2, (*far, c), me),
                copy(a, 3, (*diag, c), me), copy(a, 4, (*far, 1 - c), me), copy(a, 5, (*near, 1 - c), me),
                copy(a, 6, (*diag, 1 - c), me)]

    def start():
        for a in range(n):
            local(a).start()
            for cp in sends(a)[0:3]:
                cp.start()

    def finish():
        for first, then in ((1, (3, 4)), (2, (5,)), (3, (6,))):
            for a in range(n):
                arrivals(a)[first].wait_recv()
                for k in then:
                    sends(a)[k].start()
        for a in range(n):
            for k in (0, 4, 5, 6):
                arrivals(a)[k].wait_recv()
        for a in range(n):
            for cp in sends(a):
                cp.wait_send()
            local(a).wait()

    return start, finish


def _scatter_ops(src_refs, land_refs, send_sems, recv_sems, local_sems):
    n = len(src_refs)
    x, y, c = _place()
    my = 4 * x + 2 * y + c

    def peer(k):
        return x ^ (k >> 2), y ^ ((k >> 1) & 1), c ^ (k & 1)

    def piece(a, dev):
        m = src_refs[a].shape[0] // NDEV
        return src_refs[a].at[pl.ds(dev * m, m), :]

    def local(a):
        return pltpu.make_async_copy(piece(a, my), land_refs[a].at[my], local_sems.at[a])

    def send(a, k):
        px, py, pc = peer(k)
        return pltpu.make_async_remote_copy(
            src_ref=piece(a, 4 * px + 2 * py + pc), dst_ref=land_refs[a].at[my],
            send_sem=send_sems.at[a, k - 1], recv_sem=recv_sems.at[a, k - 1],
            device_id=(px, py, pc), device_id_type=MESH)

    def arrival(a, k):
        px, py, pc = peer(k)
        return pltpu.make_async_remote_copy(
            src_ref=piece(a, my), dst_ref=land_refs[a].at[4 * px + 2 * py + pc],
            send_sem=send_sems.at[a, k - 1], recv_sem=recv_sems.at[a, k - 1],
            device_id=(px, py, pc), device_id_type=MESH)

    def start():
        for a in range(n):
            local(a).start()
        for k in range(1, NDEV):
            for a in range(n):
                send(a, k).start()

    def finish():
        for k in range(1, NDEV):
            for a in range(n):
                send(a, k).wait_send()
        for k in range(1, NDEV):
            for a in range(n):
                arrival(a, k).wait_recv()
        for a in range(n):
            local(a).wait()

    return start, finish


def _in_hbm(*arrays):
    return tuple(pltpu.with_memory_space_constraint(a, pltpu.HBM) for a in arrays)


def _comm_sems(n):
    return [pltpu.SemaphoreType.DMA((n, 7)), pltpu.SemaphoreType.DMA((n, 7)), pltpu.SemaphoreType.DMA((n,))]


HBM = pl.BlockSpec(memory_space=pltpu.HBM)


def _sink_rows(sinks):
    return jnp.repeat(sinks.reshape(NKV, GROUP), BLK, axis=1)


def _band_softmax(s2_ref, ls, prev_offset, sink_row):
    jj = lax.broadcasted_iota(jnp.int32, (BLK, BLK), 0)
    ii = lax.broadcasted_iota(jnp.int32, (BLK, BLK), 1)
    from_prev = jj > ii
    sc = jnp.where(from_prev, s2_ref[0:BLK, ls] + prev_offset, s2_ref[BLK:2 * BLK, ls])
    m = jnp.maximum(jnp.max(sc, axis=0, keepdims=True), sink_row)
    p = jnp.exp(sc - m)
    es = jnp.exp(sink_row - m)
    inv = 1.0 / (jnp.sum(p, axis=0, keepdims=True) + es)
    return from_prev, p * inv, es * inv


def _put_split(dst_ref, ls, t, from_prev):
    t = t.astype(bf16)
    zero = jnp.zeros_like(t)
    dst_ref[0:BLK, ls] = jnp.where(from_prev, t, zero)
    dst_ref[BLK:2 * BLK, ls] = jnp.where(from_prev, zero, t)


def _heads_side_by_side(ref, h):
    return jnp.concatenate([ref[HD * (GROUP * h + g):HD * (GROUP * h + g) + HD, :] for g in range(GROUP)], axis=1)


def _kv_specs_t():
    prev = pl.BlockSpec((KVW, BLK), lambda n: (0, jnp.maximum(n - 1, 0)))
    cur = pl.BlockSpec((KVW, BLK), lambda n: (0, n))
    return [prev, cur, prev, cur]


def _attn_fwd_t(qt, kt, vt, sinks):
    s = qt.shape[1]

    def body(sink_ref, q_ref, kp_ref, kc_ref, vp_ref, vc_ref, o_ref, s2_scr, pn2_scr):
        n = pl.program_id(0)
        off = jnp.where(n > 0, 0.0, NEG)

        def scores(h):
            hs = slice(HD * h, HD * h + HD)
            kh = jnp.concatenate([kp_ref[hs, :], kc_ref[hs, :]], axis=1)
            s2_scr[h % 2] = _dot_tn(kh, _heads_side_by_side(q_ref, h))

        def probs(h):
            for g in range(GROUP):
                ls = slice(BLK * g, BLK * g + BLK)
                from_prev, pn, _ = _band_softmax(s2_scr.at[h % 2], ls, off, sink_ref[h:h + 1, ls])
                _put_split(pn2_scr.at[h % 2], ls, pn, from_prev)

        def outputs(h):
            hs = slice(HD * h, HD * h + HD)
            vh = jnp.concatenate([vp_ref[hs, :], vc_ref[hs, :]], axis=1)
            og = _dot(vh, pn2_scr[h % 2])
            for g in range(GROUP):
                a = GROUP * h + g
                o_ref[HD * a:HD * a + HD, :] = og[:, BLK * g:BLK * g + BLK]

        scores(0)
        for h in range(NKV):
            if h + 1 < NKV:
                scores(h + 1)
            probs(h)
            outputs(h)

    return pl.pallas_call(
        body, name="attn_fwd", grid=(s // BLK,),
        in_specs=[pl.BlockSpec((NKV, GROUP * BLK), lambda n: (0, 0)), pl.BlockSpec((D, BLK), lambda n: (0, n))]
        + _kv_specs_t(),
        out_specs=pl.BlockSpec((D, BLK), lambda n: (0, n)),
        out_shape=pltpu.HBM((D, s), f32),
        scratch_shapes=[pltpu.VMEM((2, 2 * BLK, GROUP * BLK), f32), pltpu.VMEM((2, 2 * BLK, GROUP * BLK), bf16)],
        compiler_params=_params(("arbitrary",), 32),
    )(_sink_rows(sinks), *_in_hbm(qt, kt, kt, vt, vt))


def _attn_bwd_t(qt, kt, vt, dot, sinks, dwo):
    s = qt.shape[1]
    nb = s // BLK

    def body(sink_ref, q_ref, do_ref, kp_ref, kc_ref, vp_ref, vc_ref, dwo_ref, dq_ref, dk_ref, dv_ref, ds_ref,
             land_ref, dk_hold, dv_hold, s2_scr, dp2_scr, pn2_scr, ds2_scr, send_sems, recv_sems, local_sems):
        n = pl.program_id(0)
        start, finish = _scatter_ops([dwo_ref], [land_ref], send_sems, recv_sems, local_sems)

        @pl.when(n == 0)
        def _():
            start()
            dk_hold[...] = jnp.zeros_like(dk_hold)
            dv_hold[...] = jnp.zeros_like(dv_hold)
            ds_ref[...] = jnp.zeros_like(ds_ref)

        @pl.when(n < nb)
        def _():
            off = jnp.where(n > 0, 0.0, NEG)

            def scores(h):
                hs = slice(HD * h, HD * h + HD)
                kh = jnp.concatenate([kp_ref[hs, :], kc_ref[hs, :]], axis=1)
                vh = jnp.concatenate([vp_ref[hs, :], vc_ref[hs, :]], axis=1)
                s2_scr[h % 2] = _dot_tn(kh, _heads_side_by_side(q_ref, h))
                dp2_scr[h % 2] = _dot_tn(vh, _heads_side_by_side(do_ref, h))

            def softmax_bwd(h):
                for g in range(GROUP):
                    ls = slice(BLK * g, BLK * g + BLK)
                    from_prev, pn, ps = _band_softmax(s2_scr.at[h % 2], ls, off, sink_ref[h:h + 1, ls])
                    dp = jnp.where(from_prev, dp2_scr[h % 2, 0:BLK, ls], dp2_scr[h % 2, BLK:2 * BLK, ls])
                    dsum = jnp.sum(pn * dp, axis=0, keepdims=True)
                    ds_ref[h:h + 1, ls] += -ps * dsum
                    _put_split(pn2_scr.at[h % 2], ls, pn, from_prev)
                    _put_split(ds2_scr.at[h % 2], ls, pn * (dp - dsum), from_prev)

            def grads(h):
                hs = slice(HD * h, HD * h + HD)
                kh = jnp.concatenate([kp_ref[hs, :], kc_ref[hs, :]], axis=1)
                dqg = _dot(kh, ds2_scr[h % 2])
                for g in range(GROUP):
                    a = GROUP * h + g
                    dq_ref[HD * a:HD * a + HD, :] = dqg[:, BLK * g:BLK * g + BLK]
                dkh = _dot_nt(_heads_side_by_side(q_ref, h), ds2_scr[h % 2])
                dvh = _dot_nt(_heads_side_by_side(do_ref, h), pn2_scr[h % 2])
                dk_ref[hs, :] = dk_hold[hs, :] + dkh[:, 0:BLK]
                dv_ref[hs, :] = dv_hold[hs, :] + dvh[:, 0:BLK]
                dk_hold[hs, :] = dkh[:, BLK:2 * BLK]
                dv_hold[hs, :] = dvh[:, BLK:2 * BLK]

            scores(0)
            for h in range(NKV):
                if h + 1 < NKV:
                    scores(h + 1)
                softmax_bwd(h)
                grads(h)

        @pl.when(n == nb)
        def _():
            dk_ref[...] = dk_hold[...]
            dv_ref[...] = dv_hold[...]
            finish()

    blk = pl.BlockSpec((D, BLK), lambda n: (0, jnp.minimum(n, nb - 1)))
    late = pl.BlockSpec((KVW, BLK), lambda n: (0, jnp.maximum(n - 1, 0)))
    whole = pl.BlockSpec((NKV, GROUP * BLK), lambda n: (0, 0))
    kv = [pl.BlockSpec((KVW, BLK), lambda n: (0, jnp.clip(n - 1, 0, nb - 1))),
          pl.BlockSpec((KVW, BLK), lambda n: (0, jnp.minimum(n, nb - 1)))]
    return pl.pallas_call(
        body, name="attn_bwd", grid=(nb + 1,),
        in_specs=[whole, blk, blk] + kv + kv + [HBM],
        out_specs=[blk, late, late, whole, HBM],
        out_shape=[pltpu.HBM((D, s), f32), pltpu.HBM((KVW, s), f32), pltpu.HBM((KVW, s), f32),
                   jax.ShapeDtypeStruct((NKV, GROUP * BLK), f32), pltpu.HBM((NDEV, WO_ROWS, D), bf16)],
        scratch_shapes=[pltpu.VMEM((KVW, BLK), f32), pltpu.VMEM((KVW, BLK), f32)]
        + [pltpu.VMEM((2, 2 * BLK, GROUP * BLK), f32)] * 2 + [pltpu.VMEM((2, 2 * BLK, GROUP * BLK), bf16)] * 2
        + _comm_sems(1),
        compiler_params=_params(("arbitrary",), 48),
    )(_sink_rows(sinks), *_in_hbm(qt, dot, kt, kt, vt, vt, dwo))


def _gate_terms(pr, pi, br, bi, sp):
    r = _sigmoid(pr + br)
    i = _sigmoid(pi + bi)
    la = -LRU_C * r * sp
    a = jnp.exp(la)
    x2 = 2.0 * la
    y = jnp.where(x2 > -0.02, -x2 * (1.0 + x2 * (0.5 + x2 * (1.0 / 6.0))), 1.0 - a * a)
    inv_mult = lax.rsqrt(jnp.maximum(y, 1e-30))
    return r, i, a, y * inv_mult, inv_mult


def _later(x, before, k):
    if k == 0:
        return x
    row = lax.broadcasted_iota(jnp.int32, before.shape, 0)
    rolled = pltpu.roll(x, k, 0)
    first = jnp.where(row < k, pltpu.roll(before, k, 0), rolled[0:8])
    return jnp.concatenate([first, rolled[8:]], axis=0)


def _earlier(x, after, k):
    if k == 0:
        return x
    n = x.shape[0]
    row = lax.broadcasted_iota(jnp.int32, after.shape, 0)
    rolled = pltpu.roll(x, n - k, 0)
    last = jnp.where(row >= 8 - k, pltpu.roll(after, 8 - k, 0), rolled[n - 8:n])
    return jnp.concatenate([rolled[0:n - 8], last], axis=0)


def _fwd_fused(h, wt, tabs, wo_shard, conv_w, conv_b, wr, wi, br, bi, lam, tm):
    s = h.shape[0]
    nt = s // tm
    nc = 512
    pieces = 8
    rows_per = tm // pieces
    later_chunks = (0, 1, 2, 3, 4, 7, 8)

    def body(h_ref, wt_ref, tab_ref, wo_ref, cw_ref, cb_ref, wr_ref, wi_ref, br_ref,
             bi_ref, lam_ref, q_ref, k_ref, v_ref, ga_ref, xl_ref, gl_ref, u_ref, hl_ref, r_ref, ig_ref, a_ref,
             im_ref, wo_all, wo_stage, halo, ub_scr, pr_scr, pi_scr, b_scr, hcar,
             send_sems, recv_sems, local_sems):
        i = pl.program_id(0)
        start, pass_on, finish = _gather_ops([wo_stage], [wo_all], send_sems, recv_sems, local_sems)

        @pl.when(i == 0)
        def _():
            wo_stage[...] = wo_ref[...].astype(bf16)
            start()
            halo[...] = jnp.zeros_like(halo)
            hcar[...] = jnp.zeros_like(hcar)

        sp = _softplus(-lam_ref[...])
        br, bi = br_ref[...], bi_ref[...]
        c, sa, sb = _tables(tab_ref)
        piece_rows = lambda p: slice(rows_per * p, rows_per * p + rows_per)

        def project(ci):
            z = _dot_nt(h_ref[...], wt_ref[ci * nc:(ci + 1) * nc, :])
            if ci < 2:
                for j in range(nc // 128):
                    r = _rope(z[:, 128 * j:128 * j + 128], c, sa, sb) * (HD ** -0.5)
                    q_ref[ci * nc + 128 * j:ci * nc + 128 * j + 128, :] = r.astype(bf16).T
            elif ci == 2:
                for j in range(2):
                    js = slice(128 * j, 128 * j + 128)
                    k_ref[js, :] = _rope(z[:, js], c, sa, sb).astype(bf16).T
                    v_ref[js, :] = z[:, KVW + 128 * j:KVW + 128 * j + 128].astype(bf16).T
            else:
                sec, j = divmod(ci - 3, 2)
                (ga_ref, xl_ref, gl_ref)[sec][:, j * nc:(j + 1) * nc] = z

        def gate_terms(p):
            rows = piece_rows(p)
            r, ig, a, mult, inv_mult = _gate_terms(pr_scr[rows, :], pi_scr[rows, :], br, bi, sp)
            r_ref[rows, :] = r
            ig_ref[rows, :] = ig
            a_ref[rows, :] = a
            im_ref[rows, :] = inv_mult
            b_scr[rows, :] = mult * (ig * u_ref[rows, :])

        def scan(p, hc):
            for t in range(rows_per * p, rows_per * p + rows_per):
                hc = a_ref[t:t + 1, :] * hc + b_scr[t:t + 1, :]
                hl_ref[t:t + 1, :] = hc
            return hc

        project(5)
        project(6)
        xl = xl_ref[...]
        u = cb_ref[...] + sum(cw_ref[k:k + 1, :] * _later(xl, halo[...], CONVW - 1 - k) for k in range(CONVW))
        halo[...] = xl[tm - 8:tm, :]
        u_ref[...] = u
        ub_scr[...] = u.astype(bf16)
        for g in range(NGRP):
            gs = slice(256 * g, 256 * g + 256)
            pr_scr[:, gs] = _dot(ub_scr[:, gs], wr_ref[g])
            pi_scr[:, gs] = _dot(ub_scr[:, gs], wi_ref[g])
        hc = hcar[...]
        gate_terms(0)
        for slot, ci in enumerate(later_chunks):
            project(ci)
            gate_terms(slot + 1)
            hc = scan(slot, hc)
        hcar[...] = scan(pieces - 1, hc)

        @pl.when(i == max(nt - 2, 0))
        def _():
            pass_on()

        @pl.when(i == nt - 1)
        def _():
            finish()

    row = lambda w: pl.BlockSpec((tm, w), lambda i: (i, 0))
    col = lambda w: pl.BlockSpec((w, tm), lambda i: (0, i))
    full = lambda a: pl.BlockSpec(a.shape, lambda i: (0,) * a.ndim)
    big = lambda w, dt: pltpu.HBM((s, w), dt)
    tile = pltpu.VMEM((tm, LW), f32)
    return pl.pallas_call(
        body, name="fwd_fused", grid=(nt,),
        in_specs=[row(D), full(wt), row(384), full(wo_shard), full(conv_w), full(conv_b),
                  full(wr), full(wi), full(br), full(bi), full(lam)],
        out_specs=[col(D), col(KVW), col(KVW), row(D), row(D), row(D)] + [row(LW)] * 6 + [HBM],
        out_shape=[pltpu.HBM((D, s), bf16), pltpu.HBM((KVW, s), bf16), pltpu.HBM((KVW, s), bf16),
                   big(D, f32), big(D, f32), big(D, f32)] + [big(LW, f32)] * 6 + [pltpu.HBM((2 * D, D), bf16)],
        scratch_shapes=[pltpu.VMEM((WO_ROWS, D), bf16), pltpu.VMEM((8, LW), f32),
                        pltpu.VMEM((tm, LW), bf16), tile, tile, tile, pltpu.VMEM((1, LW), f32)] + _comm_sems(1),
        compiler_params=_params(("arbitrary",), 56),
    )(*_in_hbm(h, wt), tabs, wo_shard, conv_w, conv_b, wr, wi, br, bi, lam)


def _lru_bwd(u, hl, dhl, xl, r, ig, a, im, conv_w, wr, wi, lam, tm):
    s = u.shape[0]
    nt = s // tm
    pieces = 8
    rows_per = tm // pieces

    def body(u_ref, h_ref, hp_ref, dh_ref, x_ref, r_ref, ig_ref, a_ref, im_ref, cw_ref, wr_ref, wi_ref,
             lam_ref, dxl_ref, dwr_ref, dwi_ref, dbr_ref, dbi_ref, dlam_ref, dcb_ref, dcw_ref,
             l_scr, du_scr, dpr_scr, dpi_scr, lcar, dunext):
        t0 = pl.program_id(0)
        tile = nt - 1 - t0

        @pl.when(t0 == 0)
        def _():
            lcar[...] = jnp.zeros_like(lcar)
            dunext[...] = jnp.zeros_like(dunext)
            for ref in (dwr_ref, dwi_ref, dbr_ref, dbi_ref, dlam_ref, dcb_ref, dcw_ref):
                ref[...] = jnp.zeros_like(ref)

        lam = lam_ref[...]
        sp = _softplus(-lam)
        hp = jnp.where(tile > 0, hp_ref[...], 0.0)

        def scan(p, c):
            for t in range(rows_per * p + rows_per - 1, rows_per * p - 1, -1):
                lt = dh_ref[t:t + 1, :] + c
                l_scr[t:t + 1, :] = lt
                c = a_ref[t:t + 1, :] * lt
            return c

        def terms(p, sums):
            rows = slice(rows_per * p, rows_per * p + rows_per)
            lt, u, r, i, a, inv_mult = l_scr[rows, :], u_ref[rows, :], r_ref[rows, :], ig_ref[rows, :], \
                a_ref[rows, :], im_ref[rows, :]
            before = hp if p == 0 else h_ref[rows_per * p - 8:rows_per * p, :]
            hprev = _later(h_ref[rows, :], before, 1)
            x2 = -2.0 * LRU_C * r * sp
            mult = jnp.where(x2 > -0.02, -x2 * (1.0 + x2 * (0.5 + x2 * (1.0 / 6.0))), 1.0 - a * a) * inv_mult
            da = lt * hprev
            dmult = lt * (i * u)
            di = lt * mult * u
            du_scr[rows, :] = lt * mult * i
            dla = da * a - dmult * (a * a) * inv_mult
            dr = dla * (-LRU_C * sp)
            dpr = dr * r * (1.0 - r)
            dpi = di * i * (1.0 - i)
            dpr_scr[rows, :] = dpr.astype(bf16)
            dpi_scr[rows, :] = dpi.astype(bf16)
            col = lambda t: jnp.sum(t, axis=0, keepdims=True)
            return sums[0] + col(dla * (-LRU_C * r)), sums[1] + col(dpr), sums[2] + col(dpi)

        sums = (jnp.zeros((1, LW), f32),) * 3
        c = scan(pieces - 1, lcar[...])
        for p in range(pieces - 1, -1, -1):
            if p > 0:
                c = scan(p - 1, c)
            sums = terms(p, sums)
        lcar[...] = c
        dlam_ref[...] += sums[0]
        dbr_ref[...] += sums[1]
        dbi_ref[...] += sums[2]

        ub = u_ref[...].astype(bf16)
        dug = []
        for g in range(NGRP):
            gs = slice(256 * g, 256 * g + 256)
            dwr_ref[g] += _dot_tn(ub[:, gs], dpr_scr[:, gs])
            dwi_ref[g] += _dot_tn(ub[:, gs], dpi_scr[:, gs])
            dug.append(_dot_nt(dpr_scr[:, gs], wr_ref[g]) + _dot_nt(dpi_scr[:, gs], wi_ref[g]))
        du = du_scr[...] + jnp.concatenate(dug, axis=1)

        dcb_ref[...] += jnp.sum(du, axis=0, keepdims=True)
        x = x_ref[...]
        after = dunext[...]
        dxl = jnp.zeros_like(du)
        for k in range(CONVW):
            e = _earlier(du, after, CONVW - 1 - k)
            dxl = dxl + cw_ref[k:k + 1, :] * e
            dcw_ref[k:k + 1, :] += jnp.sum(e * x, axis=0, keepdims=True)
        dxl_ref[...] = dxl.astype(bf16)
        dunext[...] = du[0:8, :]

        @pl.when(t0 == nt - 1)
        def _():
            dlam_ref[...] = dlam_ref[...] * (-_sigmoid(-lam))

    rev = lambda i: (nt - 1 - i, 0)
    row = pl.BlockSpec((tm, LW), rev)
    prev8 = pl.BlockSpec((8, LW), lambda i: (jnp.maximum((nt - 1 - i) * (tm // 8) - 1, 0), 0))
    full = lambda a: pl.BlockSpec(a.shape, lambda i: (0,) * a.ndim)
    vec = pl.BlockSpec((1, LW), lambda i: (0, 0))
    bd = pl.BlockSpec((NGRP, 256, 256), lambda i: (0, 0, 0))
    return pl.pallas_call(
        body, name="lru_bwd", grid=(nt,),
        in_specs=[row, row, prev8, row, row, row, row, row, row, full(conv_w), full(wr), full(wi), full(lam)],
        out_specs=[row, bd, bd, vec, vec, vec, vec, pl.BlockSpec((CONVW, LW), lambda i: (0, 0))],
        out_shape=[pltpu.HBM((s, LW), bf16),
                   jax.ShapeDtypeStruct((NGRP, 256, 256), f32), jax.ShapeDtypeStruct((NGRP, 256, 256), f32),
                   jax.ShapeDtypeStruct((1, LW), f32), jax.ShapeDtypeStruct((1, LW), f32),
                   jax.ShapeDtypeStruct((1, LW), f32), jax.ShapeDtypeStruct((1, LW), f32),
                   jax.ShapeDtypeStruct((CONVW, LW), f32)],
        scratch_shapes=[pltpu.VMEM((tm, LW), f32), pltpu.VMEM((tm, LW), f32), pltpu.VMEM((tm, LW), bf16),
                        pltpu.VMEM((tm, LW), bf16), pltpu.VMEM((1, LW), f32), pltpu.VMEM((8, LW), f32)],
        compiler_params=_params(("arbitrary",), 56),
    )(*_in_hbm(u, hl, hl, dhl, xl, r, ig, a, im), conv_w, wr, wi, lam)


def _gated_norm(t, gate, gain):
    sg = _sigmoid(gate)
    silu = gate * sg
    p = t * silu
    rstd = lax.rsqrt(jnp.mean(p * p, axis=-1, keepdims=True) + EPS)
    ph = p * rstd
    return sg, silu, rstd, ph, ph * gain


def _gated_norm_bwd(dy, t, gate, gain, sg, silu, rstd, ph):
    w = dy * gain
    dp = rstd * (w - ph * jnp.mean(w * ph, axis=-1, keepdims=True))
    dgate = (dp * t) * (sg + silu * (1.0 - sg))
    return jnp.sum(dy * ph, axis=0, keepdims=True), dp * silu, dgate


def _out_fwd_bwd(x, tgt, o, ga, hl, gl, again, lgain, fgain, wo, tm):
    s = x.shape[0]
    nt = s // tm

    def body(x_ref, t_ref, o_ref, ga_ref, hl_ref, gl_ref, ag_ref, lg_ref, fg_ref, wo_ref,
             dx2_ref, do_ref, dga_ref, dhl_ref, dgl_ref, dwo_ref, gfg_ref, gag_ref, glg_ref, loss_ref, acc):
        i = pl.program_id(0)

        @pl.when(i == 0)
        def _():
            acc[...] = jnp.zeros_like(acc)
            for ref in (gfg_ref, gag_ref, glg_ref, loss_ref):
                ref[...] = jnp.zeros_like(ref)

        oo = jnp.concatenate([o_ref[128 * j:128 * j + 128, :].T for j in range(D // 128)], axis=1)
        gga, hh, ggl = ga_ref[...], hl_ref[...], gl_ref[...]
        ag, lg, fg = ag_ref[...], lg_ref[...], fg_ref[...]
        sga, silua, ra, pah, ya = _gated_norm(oo, gga, ag)
        sgl, silul, rl, plh, yl = _gated_norm(hh, ggl, lg)
        yab, ylb = ya.astype(bf16), yl.astype(bf16)
        y = _dot(yab, wo_ref[0:D, :]) + _dot(ylb, wo_ref[D:2 * D, :])
        x2 = x_ref[...] + y
        r2 = lax.rsqrt(jnp.mean(x2 * x2, axis=-1, keepdims=True) + EPS)
        x2h = x2 * r2
        err = x2h * fg - t_ref[...]
        loss_ref[...] += 0.5 * jnp.sum(jnp.sum(err * err, axis=-1, keepdims=True) * (1.0 / D))
        gfg_ref[...] += jnp.sum(err * x2h, axis=0, keepdims=True) * (1.0 / D)
        w = err * (fg * (1.0 / D))
        dx2 = r2 * (w - x2h * jnp.mean(w * x2h, axis=-1, keepdims=True))
        dx2_ref[...] = dx2
        dyb = dx2.astype(bf16)
        acc[0:D, :] += _dot_tn(yab, dyb)
        acc[D:2 * D, :] += _dot_tn(ylb, dyb)
        dya = _dot_nt(dyb, wo_ref[0:D, :])
        dyl = _dot_nt(dyb, wo_ref[D:2 * D, :])
        gag, do, dga = _gated_norm_bwd(dya, oo, gga, ag, sga, silua, ra, pah)
        glg, dhl, dgl = _gated_norm_bwd(dyl, hh, ggl, lg, sgl, silul, rl, plh)
        gag_ref[...] += gag
        glg_ref[...] += glg
        dob = do.astype(bf16)
        for j in range(D // 128):
            do_ref[128 * j:128 * j + 128, :] = dob[:, 128 * j:128 * j + 128].T
        dga_ref[...] = dga.astype(bf16)
        dhl_ref[...] = dhl
        dgl_ref[...] = dgl.astype(bf16)

        @pl.when(i == nt - 1)
        def _():
            dwo_ref[...] = acc[...].astype(bf16)

    row = pl.BlockSpec((tm, D), lambda i: (i, 0))
    col = pl.BlockSpec((D, tm), lambda i: (0, i))
    vec = pl.BlockSpec((1, D), lambda i: (0, 0))
    mat = pl.BlockSpec((2 * D, D), lambda i: (0, 0))
    return pl.pallas_call(
        body, name="out_fwd_bwd", grid=(nt,),
        in_specs=[row, row, col, row, row, row] + [vec] * 3 + [mat],
        out_specs=[row, col, row, row, row] + [mat, vec, vec, vec, pl.BlockSpec((1, 128), lambda i: (0, 0))],
        out_shape=[pltpu.HBM((s, D), f32), pltpu.HBM((D, s), bf16),
                   pltpu.HBM((s, D), bf16), pltpu.HBM((s, D), f32),
                   pltpu.HBM((s, D), bf16), pltpu.HBM((2 * D, D), bf16),
                   jax.ShapeDtypeStruct((1, D), f32), jax.ShapeDtypeStruct((1, D), f32),
                   jax.ShapeDtypeStruct((1, D), f32), jax.ShapeDtypeStruct((1, 128), f32)],
        scratch_shapes=[pltpu.VMEM((2 * D, D), f32)],
        compiler_params=_params(("arbitrary",), 56),
    )(*_in_hbm(x, tgt, o, ga, hl, gl), again, lgain, fgain, *_in_hbm(wo))


def _bwd_in(x, dx2, dq, dk, dv, dga, dxl, dgl, ln_gain, wt, tabs, tm):
    s = x.shape[0]

    def body(x_ref, dx2_ref, dq_ref, dk_ref, dv_ref, dga_ref, dxl_ref, dgl_ref, g_ref, wt_ref,
             tab_ref, gx_ref, gln_ref, dzt_ref):
        @pl.when(pl.program_id(0) == 0)
        def _():
            gln_ref[...] = jnp.zeros_like(gln_ref)

        c, sa, sb = (t.T for t in _tables(tab_ref))
        for j in range(D // 128):
            js = slice(128 * j, 128 * j + 128)
            dzt_ref[js, :] = (_unrope_t(dq_ref[js, :], c, sa, sb) * (HD ** -0.5)).astype(bf16)
        for j in range(KVW // 128):
            js = slice(128 * j, 128 * j + 128)
            dzt_ref[D + 128 * j:D + 128 * j + 128, :] = _unrope_t(dk_ref[js, :], c, sa, sb).astype(bf16)
        dzt_ref[D + KVW:D + 2 * KVW, :] = dv_ref[...].astype(bf16)
        first = D + 2 * KVW
        dh = _dot_tn(dzt_ref[0:512, :], wt_ref[0:512, :])
        for ci in range(1, first // 512):
            dh = dh + _dot_tn(dzt_ref[512 * ci:512 * ci + 512, :], wt_ref[512 * ci:512 * ci + 512, :])
        for sec, ref in enumerate((dga_ref, dxl_ref, dgl_ref)):
            for j in range(D // 512):
                rows = slice(first + D * sec + 512 * j, first + D * sec + 512 * j + 512)
                dh = dh + _dot(ref[:, 512 * j:512 * j + 512], wt_ref[rows, :])
            for j in range(D // 128):
                dzt_ref[first + D * sec + 128 * j:first + D * sec + 128 * j + 128, :] = ref[:, 128 * j:128 * j + 128].T
        xx = x_ref[...]
        rstd = lax.rsqrt(jnp.mean(xx * xx, axis=-1, keepdims=True) + EPS)
        xh = xx * rstd
        gln_ref[...] += jnp.sum(dh * xh, axis=0, keepdims=True)
        w = dh * g_ref[...]
        gx_ref[...] = dx2_ref[...] + rstd * (w - xh * jnp.mean(w * xh, axis=-1, keepdims=True))

    row = lambda w: pl.BlockSpec((tm, w), lambda i: (i, 0))
    col = lambda w: pl.BlockSpec((w, tm), lambda i: (0, i))
    full = lambda a: pl.BlockSpec(a.shape, lambda i: (0, 0))
    return pl.pallas_call(
        body, name="bwd_in", grid=(s // tm,),
        in_specs=[row(D), row(D), col(D), col(KVW), col(KVW), row(D), row(D), row(D), full(ln_gain), full(wt),
                  row(384)],
        out_specs=[row(D), pl.BlockSpec((1, D), lambda i: (0, 0)), col(NIN)],
        out_shape=[pltpu.HBM((s, D), f32), jax.ShapeDtypeStruct((1, D), f32),
                   pltpu.HBM((NIN, s), bf16)],
        compiler_params=_params(("arbitrary",), 56),
    )(*_in_hbm(x, dx2, dq, dk, dv, dga, dxl, dgl), ln_gain, *_in_hbm(wt), tabs)


WT_TERMS = 4


def _dwt_scatter(dzt, h, small, tm):
    s = h.shape[0]
    nk = s // tm
    srows = small.shape[0] // NDEV
    last = NDEV - 1
    sm_turn = 2

    def body(order_ref, dz_ref, h_ref, sm_ref, lwt_ref, rep_all, tail_ref, acc, stage, given, relayed, lsm, rep_stage,
             send_sems, recv_sems, local_sem, sm_send, sm_recv, sm_local, rep_send, rep_recv, rep_local):
        j, k = pl.program_id(0), pl.program_id(1)
        x, y, c = _place()
        sibling = (x, y, 1 - c)
        near = (x ^ (1 - c), y ^ c)
        far = (x ^ c, y ^ (1 - c))
        sm_start, sm_finish = _scatter_ops([sm_ref], [lsm], sm_send, sm_recv, sm_local)
        rep_start, rep_pass_on, rep_finish = _gather_ops([rep_stage], [rep_all], rep_send, rep_recv, rep_local)

        def send(step):
            if step == last - 1:
                dst, to = lwt_ref.at[1], sibling
            elif step % 2 == 0:
                dst, to = given.at[step // 2], sibling
            elif step == 1:
                dst, to = relayed, (*near, c)
            else:
                dst, to = lwt_ref.at[1 + step // 2], (*(near if step == 3 else far), c)
            return pltpu.make_async_remote_copy(
                src_ref=stage.at[step % 2], dst_ref=dst, send_sem=send_sems.at[step], recv_sem=recv_sems.at[step],
                device_id=to, device_id_type=MESH)

        def keep():
            return pltpu.make_async_copy(stage.at[last % 2], lwt_ref.at[0], local_sem)

        @pl.when((j == 0) & (k == 0))
        def _():
            sm_start()

        @pl.when(k == 0)
        def _():
            acc[...] = jnp.zeros_like(acc)

        acc[...] += _dot(dz_ref[...], h_ref[...])

        for step in range(NDEV):
            @pl.when((k == nk - 1) & (j == step))
            def _(step=step):
                if step >= 2:
                    send(step - 2).wait_send()
                if step % 2 == 1 and step < last:
                    send(step - 1).wait_recv()
                    total = acc[...] + given[step // 2].astype(f32)
                    if step == 5:
                        send(1).wait_recv()
                        total = total + relayed[...].astype(f32)
                    stage[step % 2] = total.astype(bf16)
                else:
                    stage[step % 2] = acc[...].astype(bf16)
                if step < last:
                    send(step).start()
                else:
                    keep().start()
                    send(last - 1).wait_send()
                    for peer_step in (3, 5, last - 1):
                        send(peer_step).wait_recv()
                    keep().wait()
                    rep_finish()
                if step == sm_turn:
                    sm_finish()
                    total_sm = lsm[0]
                    for dev in range(1, NDEV):
                        total_sm = total_sm + lsm[dev]
                    rep_stage[...] = total_sm[0:SMALL_PER]
                    tail_ref[...] = total_sm[SMALL_PER:]
                    rep_start()
                if step == last - 1:
                    rep_pass_on()

    x, y, c = _place()
    dest = lambda chip, cc: 4 * chip[0] + 2 * chip[1] + cc
    near, far, diag = (x ^ (1 - c), y ^ c), (x ^ c, y ^ (1 - c)), (1 - x, 1 - y)
    order = jnp.stack([dest(diag, 1 - c), dest(diag, c), dest(far, 1 - c), dest(near, c),
                       dest(near, 1 - c), dest(far, c), dest((x, y), 1 - c), dest((x, y), c)])
    return pl.pallas_call(
        body, name="dwt_scatter",
        grid_spec=pltpu.PrefetchScalarGridSpec(
            num_scalar_prefetch=1, grid=(NDEV, nk),
            in_specs=[pl.BlockSpec((WT_ROWS, tm), lambda j, k, order: (order[j], k)),
                      pl.BlockSpec((tm, D), lambda j, k, order: (k, 0)), HBM],
            out_specs=[HBM, HBM, pl.BlockSpec((srows - SMALL_PER, D), lambda j, k, order: (0, 0))],
            scratch_shapes=[pltpu.VMEM((WT_ROWS, D), f32), pltpu.VMEM((2, WT_ROWS, D), bf16),
                            pltpu.VMEM((3, WT_ROWS, D), bf16), pltpu.VMEM((WT_ROWS, D), bf16),
                            pltpu.VMEM((NDEV, srows, D), f32), pltpu.VMEM((SMALL_PER, D), f32),
                            pltpu.SemaphoreType.DMA((last,)), pltpu.SemaphoreType.DMA((last,)),
                            pltpu.SemaphoreType.DMA(())] + _comm_sems(1) + _comm_sems(1)),
        out_shape=[pltpu.HBM((WT_TERMS, WT_ROWS, D), bf16), pltpu.HBM((SMALL_ROWS, D), f32),
                   jax.ShapeDtypeStruct((srows - SMALL_PER, D), f32)],
        compiler_params=_params(("arbitrary", "arbitrary"), 48),
    )(order, *_in_hbm(dzt, h, small))


def _diag_blocks(bd):
    eye = jnp.eye(4, dtype=bd.dtype)
    return jnp.einsum('gjckd,jk->gjcd', bd.reshape(NGRP, 4, HD, 4, HD), eye).reshape(NQ, HD, HD)


def _sequence_step(x, h, tgt, wt, wo_shard, conv_w, wr, wi, p):
    s = x.shape[0]
    tm = min(256, s)
    tabs = _rope_tables(s)
    sinks = p["sinks"].reshape(NQ)
    qt, kt, vt, ga, xl, gl, u, hl, r, ig, a, im, wo = _fwd_fused(
        h, wt, tabs, wo_shard, conv_w, p["conv_b"], wr, wi, p["b_rgate"], p["b_igate"], p["lru_lambda"], tm)
    ot = _attn_fwd_t(qt, kt, vt, sinks)
    dx2, dot, dga, dhl, dgl, dwo, g_fg, g_ag, g_lg, loss = _out_fwd_bwd(
        x, tgt, ot, ga, hl, gl, p["attn_out_gain"], p["lru_out_gain"], p["final_gain"], wo, tm)
    dqt, dkt, dvt, dsink, land_wo = _attn_bwd_t(qt, kt, vt, dot, sinks, dwo)
    dxl, dwr, dwi, dbr, dbi, dlam, dcb, dcw = _lru_bwd(u, hl, dhl, xl, r, ig, a, im, conv_w, wr, wi, p["lru_lambda"], tm)
    gx, g_ln, dzt = _bwd_in(x, dx2, dqt, dkt, dvt, dga, dxl, dgl, p["ln_gain"], wt, tabs, tm)
    small = dict(ln_gain=g_ln, sinks=dsink.reshape(NQ, BLK).sum(axis=1)[None], conv_w=dcw, conv_b=dcb,
                 w_rgate=_diag_blocks(dwr), b_rgate=dbr, w_igate=_diag_blocks(dwi), b_igate=dbi, lru_lambda=dlam,
                 attn_out_gain=g_ag, lru_out_gain=g_lg, final_gain=g_fg)
    land_wt, g_rep, g_tail = _dwt_scatter(dzt, h, _pack_small(small, loss), min(2048, s))
    return gx, land_wt, land_wo, g_rep, g_tail


def _gather_weights(wt_shard, conv_blk, x, ln_gain, w_rgate, w_igate, tm):
    s = x.shape[0]

    def body(wt_ref, cw_ref, g_ref, wrg_ref, wig_ref, x_ref, wt_all, cw_all, h_ref, wr_ref, wi_ref,
             stage, xbuf, hbuf, send_sems, recv_sems, local_sems):
        stage[...] = wt_ref[...].astype(bf16)
        start, finish = _relay_gather_ops([stage, cw_ref], [wt_all, cw_all], send_sems, recv_sems, local_sems)
        start()
        for src, dst in ((wrg_ref, wr_ref), (wig_ref, wi_ref)):
            dst[...] = jnp.zeros_like(dst)
            for nb in range(NQ):
                g, j = divmod(nb, 4)
                dst[g, HD * j:HD * j + HD, HD * j:HD * j + HD] = src[nb].astype(bf16)
        gain = g_ref[...]
        for i in range(s // tm):
            rows = pl.ds(i * tm, tm)
            pltpu.sync_copy(x_ref.at[rows, :], xbuf)
            xx = xbuf[...]
            rstd = lax.rsqrt(jnp.mean(xx * xx, axis=-1, keepdims=True) + EPS)
            hbuf[...] = (xx * rstd * gain).astype(bf16)
            pltpu.sync_copy(hbuf, h_ref.at[rows, :])
        finish()

    vmem = pl.BlockSpec(memory_space=pltpu.VMEM)
    return pl.pallas_call(
        body, name="gather_weights",
        in_specs=[vmem] * 5 + [HBM], out_specs=[HBM, HBM, HBM, vmem, vmem],
        out_shape=[pltpu.HBM((NIN, D), bf16), pltpu.HBM((NDEV * 8, 128), f32), pltpu.HBM((s, D), bf16)]
        + [jax.ShapeDtypeStruct((NGRP, 256, 256), bf16)] * 2,
        scratch_shapes=[pltpu.VMEM((WT_ROWS, D), bf16), pltpu.VMEM((tm, D), f32), pltpu.VMEM((tm, D), bf16)]
        + _comm_sems(2),
        compiler_params=pltpu.CompilerParams(vmem_limit_bytes=32 * MIB),
    )(wt_shard, conv_blk, ln_gain, w_rgate, w_igate, *_in_hbm(x))


def _adam_math(w, g, m, v):
    m2 = ADAM_B1 * m + (1.0 - ADAM_B1) * g
    v2 = ADAM_B2 * v + (1.0 - ADAM_B2) * (g * g)
    m_hat = m2 / (1.0 - ADAM_B1 ** ADAM_STEP)
    v_hat = v2 / (1.0 - ADAM_B2 ** ADAM_STEP)
    delta = -ADAM_LR * (m_hat / (jnp.sqrt(v_hat) + ADAM_EPS) + ADAM_WD * w)
    return delta, m2, v2


def _reduce_adamw(land, w, m, v, tr, name):
    terms, rows, cols = land.shape

    def body(l_ref, w_ref, m_ref, v_ref, g_ref, d_ref, m2_ref, v2_ref):
        g = l_ref[0].astype(f32)
        for t in range(1, terms):
            g = g + l_ref[t].astype(f32)
        g_ref[...] = g
        d_ref[...], m2_ref[...], v2_ref[...] = _adam_math(w_ref[...], g, m_ref[...], v_ref[...])

    blk = pl.BlockSpec((tr, cols), lambda i: (i, 0))
    return pl.pallas_call(
        body, name=name, grid=(rows // tr,),
        in_specs=[pl.BlockSpec((terms, tr, cols), lambda i: (0, i, 0))] + [blk] * 3, out_specs=[blk] * 4,
        out_shape=[jax.ShapeDtypeStruct((rows, cols), f32)] * 4,
        compiler_params=_params(("arbitrary",), 32),
    )(*_in_hbm(land), w, m, v)


VEC_NAMES = ("ln_gain", "conv_b", "b_rgate", "b_igate", "lru_lambda", "attn_out_gain", "lru_out_gain", "final_gain")
ROW_RGATE, ROW_IGATE, ROW_VEC, ROW_SINKS = 0, 64, 128, 136
LOSS_LANE = NQ


def _adamw_small(g_rep, g_conv, w, m, v):
    names = list(VEC_NAMES) + ["sinks", "conv_w", "w_rgate", "w_igate"]
    ins = [g_rep, g_conv] + [d[k] for k in names for d in (w, m, v)]

    def body(*refs):
        g_ref, gc_ref = refs[0], refs[1]
        in_refs = refs[2:2 + 3 * len(names)]
        out_refs = refs[2 + 3 * len(names):]

        def update(j, g, at=None):
            w_ref, m_ref, v_ref = in_refs[3 * j:3 * j + 3]
            outs = out_refs[4 * j:4 * j + 4]
            pick = (lambda r: r[...]) if at is None else (lambda r: r[at])
            res = (g,) + _adam_math(pick(w_ref), g, pick(m_ref), pick(v_ref))
            for o_ref, val in zip(outs, res):
                if at is None:
                    o_ref[...] = val
                else:
                    o_ref[at] = val

        for j in range(len(VEC_NAMES)):
            update(j, g_ref[ROW_VEC + j:ROW_VEC + j + 1, :])
        update(len(VEC_NAMES), g_ref[ROW_SINKS:ROW_SINKS + 1, 0:NQ])
        update(len(VEC_NAMES) + 1, gc_ref[...], at=0)
        for gi, row0 in ((len(VEC_NAMES) + 2, ROW_RGATE), (len(VEC_NAMES) + 3, ROW_IGATE)):
            for nb in range(NQ):
                update(gi, g_ref[row0:row0 + HD, HD * nb:HD * nb + HD], at=(0, nb))

    vmem = pl.BlockSpec(memory_space=pltpu.VMEM)
    out_shape = [jax.ShapeDtypeStruct(w[k].shape, f32) for k in names for _ in range(4)]
    outs = pl.pallas_call(
        body, name="adamw_small",
        in_specs=[vmem] * len(ins), out_specs=[vmem] * len(out_shape), out_shape=out_shape,
        compiler_params=pltpu.CompilerParams(vmem_limit_bytes=32 * MIB),
    )(*ins)
    return {k: tuple(outs[4 * j:4 * j + 4]) for j, k in enumerate(names)}


def _pack_small(small, loss):
    gate = lambda g: g.transpose(1, 0, 2).reshape(HD, NQ * HD)
    row_s = jnp.concatenate([small["sinks"], loss[:, LOSS_LANE:128], jnp.zeros((1, D - 128), f32)], axis=1)
    rep = jnp.concatenate([gate(small["w_rgate"]), gate(small["w_igate"])] + [small[k] for k in VEC_NAMES]
                          + [row_s, jnp.zeros((SMALL_ROWS - ROW_SINKS - 1, D), f32)], axis=0)
    conv = small["conv_w"].reshape(CONVW, NDEV, 128).transpose(1, 0, 2)
    conv = jnp.pad(conv, ((0, 0), (0, 8 - CONVW), (0, D - 128)))
    return jnp.concatenate([rep.reshape(NDEV, SMALL_PER, D), conv], axis=1).reshape(NDEV * (SMALL_PER + 8), D)


def kernel(x, ln_gain, w_in, sinks, conv_w, conv_b, w_rgate, b_rgate, w_igate, b_igate, lru_lambda, attn_out_gain, lru_out_gain, w_out, final_gain, loss_target, m_ln_gain, m_w_in, m_sinks, m_conv_w, m_conv_b, m_w_rgate, m_b_rgate, m_w_igate, m_b_igate, m_lru_lambda, m_attn_out_gain, m_lru_out_gain, m_w_out, m_final_gain, v_ln_gain, v_w_in, v_sinks, v_conv_w, v_conv_b, v_w_rgate, v_b_rgate, v_w_igate, v_b_igate, v_lru_lambda, v_attn_out_gain, v_lru_out_gain, v_w_out, v_final_gain):
    w = dict(ln_gain=ln_gain, sinks=sinks, conv_w=conv_w, conv_b=conv_b, w_rgate=w_rgate, b_rgate=b_rgate,
             w_igate=w_igate, b_igate=b_igate, lru_lambda=lru_lambda, attn_out_gain=attn_out_gain,
             lru_out_gain=lru_out_gain, final_gain=final_gain.reshape(1, D))
    m = dict(ln_gain=m_ln_gain, sinks=m_sinks, conv_w=m_conv_w, conv_b=m_conv_b, w_rgate=m_w_rgate,
             b_rgate=m_b_rgate, w_igate=m_w_igate, b_igate=m_b_igate, lru_lambda=m_lru_lambda,
             attn_out_gain=m_attn_out_gain, lru_out_gain=m_lru_out_gain, final_gain=m_final_gain.reshape(1, D))
    v = dict(ln_gain=v_ln_gain, sinks=v_sinks, conv_w=v_conv_w, conv_b=v_conv_b, w_rgate=v_w_rgate,
             b_rgate=v_b_rgate, w_igate=v_w_igate, b_igate=v_b_igate, lru_lambda=v_lru_lambda,
             attn_out_gain=v_attn_out_gain, lru_out_gain=v_lru_out_gain, final_gain=v_final_gain.reshape(1, D))

    conv_blk = jnp.pad(conv_w[0], ((0, 8 - CONVW), (0, 0)))
    wt, cw_all, h, wr, wi = _gather_weights(w_in[0].T, conv_blk, x[0], ln_gain, w_rgate[0], w_igate[0],
                                            min(512, x.shape[1]))
    conv_full = cw_all.reshape(NDEV, 8, 128)[:, 0:CONVW].transpose(1, 0, 2).reshape(CONVW, LW)

    p = {k: w[k] for k in w if k not in ("conv_w", "w_rgate", "w_igate")}
    gx, land_wt, land_wo, g_rep, g_tail = _sequence_step(
        x[0], h, loss_target[0], wt, w_out[0], conv_full, wr, wi, p)
    g_conv = g_tail[0:CONVW, 0:128]

    wins = _reduce_adamw(land_wt, w_in[0].T, m_w_in[0].T, v_w_in[0].T, 192, "adamw_w_in")
    g_win, d_win, m_win, v_win = (t.T for t in wins)
    g_wo, d_wo, m_wo, v_wo = _reduce_adamw(land_wo, w_out[0], m_w_out[0], v_w_out[0], 256, "adamw_w_out")
    res = _adamw_small(g_rep, g_conv, w, m, v)
    res["w_in"] = tuple(t[None] for t in (g_win, d_win, m_win, v_win))
    res["w_out"] = tuple(t[None] for t in (g_wo, d_wo, m_wo, v_wo))
    res["final_gain"] = tuple(t.reshape(D) for t in res["final_gain"])

    order = ("ln_gain", "w_in", "sinks", "conv_w", "conv_b", "w_rgate", "b_rgate", "w_igate", "b_igate",
             "lru_lambda", "attn_out_gain", "lru_out_gain", "w_out", "final_gain")
    total_loss = g_rep[ROW_SINKS, LOSS_LANE]
    return (total_loss, gx[None]) + tuple(res[k][i] for i in range(4) for k in order)
```

```python
import jax
import jax.numpy as jnp
from jax import lax
from jax.experimental import pallas as pl
from jax.experimental.pallas import tpu as pltpu

f32 = jnp.float32
bf16 = jnp.bfloat16

D = 1024
HD = 64
NQ = 16
NKV = 4
GROUP = NQ // NKV
KVW = NKV * HD
BLK = 128
ROT = 16
THETA = 500000.0
NEG = -1e30
LW = 1024
NGRP = 4
CONVW = 4
LRU_C = 8.0
NIN = 4608
EPS = 1e-6
NDEV = 8
WT_ROWS = NIN // NDEV
WO_ROWS = 2 * D // NDEV
SMALL_ROWS = 192
SMALL_PER = SMALL_ROWS // NDEV

ADAM_LR = 0.001
ADAM_B1 = 0.9
ADAM_B2 = 0.999
ADAM_EPS = 1e-08
ADAM_WD = 0.01
ADAM_STEP = 10

NT = (((1,), (1,)), ((), ()))
TN = (((0,), (0,)), ((), ()))
MESH = pl.DeviceIdType.MESH
MIB = 1024 * 1024


def _dot(a, b):
    return jnp.dot(a, b, preferred_element_type=f32)


def _dot_nt(a, b):
    return lax.dot_general(a, b, NT, preferred_element_type=f32)


def _dot_tn(a, b):
    return lax.dot_general(a, b, TN, preferred_element_type=f32)


def _params(sem, vmem_mib):
    return pltpu.CompilerParams(dimension_semantics=sem, vmem_limit_bytes=vmem_mib * MIB)


def _sigmoid(x):
    return 0.5 * jnp.tanh(0.5 * x) + 0.5


def _softplus(x):
    return jnp.maximum(x, 0.0) + jnp.log(1.0 + jnp.exp(-jnp.abs(x)))


def _rope_tables(s):
    pos = jnp.arange(s, dtype=f32)
    inv_freq = THETA ** (-jnp.arange(0, ROT, 2, dtype=f32) / ROT)
    ang = pos[:, None] * inv_freq[None, :]
    cs = jnp.concatenate([jnp.cos(ang) - 1.0, jnp.sin(ang)], axis=1)
    d = jnp.arange(128) % HD
    j = jnp.arange(ROT)[:, None]
    pick_c = ((d < ROT) & (j == d % (ROT // 2))).astype(f32)
    pick_sa = ((d >= ROT // 2) & (d < ROT) & (j == d)).astype(f32)
    pick_sb = -((d < ROT // 2) & (j == d + ROT // 2)).astype(f32)
    picks = jnp.concatenate([pick_c, pick_sa, pick_sb], axis=1)
    ones = jnp.concatenate([jnp.ones((1, 128), f32), jnp.zeros((1, 256), f32)], axis=1)
    return jnp.dot(cs, picks, precision=lax.Precision.HIGHEST) + ones


def _tables(tab_ref):
    return tab_ref[:, 0:128], tab_ref[:, 128:256], tab_ref[:, 256:384]


def _rope(t, c, sa, sb):
    return t * c + pltpu.roll(t, 8, 1) * sa + pltpu.roll(t, 120, 1) * sb


def _unrope_t(dr, c, sa, sb):
    return dr * c + pltpu.roll(dr * sa, 120, 0) + pltpu.roll(dr * sb, 8, 0)


def _place():
    return lax.axis_index("x"), lax.axis_index("y"), lax.axis_index("c")


def _gather_ops(mine_refs, out_refs, send_sems, recv_sems, local_sems):
    n = len(mine_refs)
    x, y, c = _place()
    me, sibling = (x, y, c), (x, y, 1 - c)
    chips = [(1 - x, y), (x, 1 - y), (1 - x, 1 - y)]

    def rows(a, dev):
        m = mine_refs[a].shape[0]
        return out_refs[a].at[pl.ds((4 * dev[0] + 2 * dev[1] + dev[2]) * m, m), :]

    def copy(a, k, block, to, own=False):
        return pltpu.make_async_remote_copy(
            src_ref=mine_refs[a] if own else rows(a, block), dst_ref=rows(a, block),
            send_sem=send_sems.at[a, k], recv_sem=recv_sems.at[a, k], device_id=to, device_id_type=MESH)

    def local(a):
        return pltpu.make_async_copy(mine_refs[a], rows(a, me), local_sems.at[a])

    def first(a):
        return [copy(a, 0, me, sibling, own=True)] + [copy(a, 1 + j, me, (*chip, c), own=True)
                                                      for j, chip in enumerate(chips)]

    def start():
        for a in range(n):
            local(a).start()
            for cp in first(a):
                cp.start()

    def pass_on():
        for j, chip in enumerate(chips):
            for a in range(n):
                copy(a, 1 + j, (*chip, c), me).wait_recv()
                copy(a, 4 + j, (*chip, c), sibling).start()

    def finish():
        for a in range(n):
            copy(a, 0, sibling, me).wait_recv()
            for j, chip in enumerate(chips):
                copy(a, 4 + j, (*chip, 1 - c), me).wait_recv()
        for a in range(n):
            for cp in first(a) + [copy(a, 4 + j, (*chip, c), sibling) for j, chip in enumerate(chips)]:
                cp.wait_send()
            local(a).wait()

    return start, pass_on, finish


def _relay_gather_ops(mine_refs, out_refs, send_sems, recv_sems, local_sems):
    n = len(mine_refs)
    x, y, c = _place()
    me, sibling = (x, y, c), (x, y, 1 - c)
    near = (x ^ (1 - c), y ^ c)
    far = (x ^ c, y ^ (1 - c))
    diag = (1 - x, 1 - y)

    def rows(a, dev):
        m = mine_refs[a].shape[0]
        return out_refs[a].at[pl.ds((4 * dev[0] + 2 * dev[1] + dev[2]) * m, m), :]

    def copy(a, k, block, to, own=False):
        return pltpu.make_async_remote_copy(
            src_ref=mine_refs[a] if own else rows(a, block), dst_ref=rows(a, block),
            send_sem=send_sems.at[a, k], recv_sem=recv_sems.at[a, k], device_id=to, device_id_type=MESH)

    def local(a):
        return pltpu.make_async_copy(mine_refs[a], rows(a, me), local_sems.at[a])

    def sends(a):
        return [copy(a, 0, me, sibling, own=True), copy(a, 1, me, (*near, c), own=True),
                copy(a, 2, me, (*far, c), own=True), copy(a, 3, (*near, c), (*far, c)),
                copy(a, 4, (*near, c), sibling), copy(a, 5, (*far, c), sibling), copy(a, 6, (*diag, c), sibling)]

    def arrivals(a):
        return [copy(a, 0, sibling, me), copy(a, 1, (*near, c), me), copy(a, 2, (*far, c), me),
                copy(a, 3, (*diag, c), me), copy(a, 4, (*far, 1 - c), me), copy(a, 5, (*near, 1 - c), me),
                copy(a, 6, (*diag, 1 - c), me)]

    def start():
        for a in range(n):
            local(a).start()
            for cp in sends(a)[0:3]:
                cp.start()

    def finish():
        for first, then in ((1, (3, 4)), (2, (5,)), (3, (6,))):
            for a in range(n):
                arrivals(a)[first].wait_recv()
                for k in then:
                    sends(a)[k].start()
        for a in range(n):
            for k in (0, 4, 5, 6):
                arrivals(a)[k].wait_recv()
        for a in range(n):
            for cp in sends(a):
                cp.wait_send()
            local(a).wait()

    return start, finish


def _scatter_ops(src_refs, land_refs, send_sems, recv_sems, local_sems):
    n = len(src_refs)
    x, y, c = _place()
    my = 4 * x + 2 * y + c

    def peer(k):
        return x ^ (k >> 2), y ^ ((k >> 1) & 1), c ^ (k & 1)

    def piece(a, dev):
        m = src_refs[a].shape[0] // NDEV
        return src_refs[a].at[pl.ds(dev * m, m), :]

    def local(a):
        return pltpu.make_async_copy(piece(a, my), land_refs[a].at[my], local_sems.at[a])

    def send(a, k):
        px, py, pc = peer(k)
        return pltpu.make_async_remote_copy(
            src_ref=piece(a, 4 * px + 2 * py + pc), dst_ref=land_refs[a].at[my],
            send_sem=send_sems.at[a, k - 1], recv_sem=recv_sems.at[a, k - 1],
            device_id=(px, py, pc), device_id_type=MESH)

    def arrival(a, k):
        px, py, pc = peer(k)
        return pltpu.make_async_remote_copy(
            src_ref=piece(a, my), dst_ref=land_refs[a].at[4 * px + 2 * py + pc],
            send_sem=send_sems.at[a, k - 1], recv_sem=recv_sems.at[a, k - 1],
            device_id=(px, py, pc), device_id_type=MESH)

    def start():
        for a in range(n):
            local(a).start()
        for k in range(1, NDEV):
            for a in range(n):
                send(a, k).start()

    def finish():
        for k in range(1, NDEV):
            for a in range(n):
                send(a, k).wait_send()
        for k in range(1, NDEV):
            for a in range(n):
                arrival(a, k).wait_recv()
        for a in range(n):
            local(a).wait()

    return start, finish


def _in_hbm(*arrays):
    return tuple(pltpu.with_memory_space_constraint(a, pltpu.HBM) for a in arrays)


def _comm_sems(n):
    return [pltpu.SemaphoreType.DMA((n, 7)), pltpu.SemaphoreType.DMA((n, 7)), pltpu.SemaphoreType.DMA((n,))]


HBM = pl.BlockSpec(memory_space=pltpu.HBM)


def _sink_rows(sinks):
    return jnp.repeat(sinks.reshape(NKV, GROUP), BLK, axis=1)


def _band_softmax(s2_ref, ls, prev_offset, sink_row):
    jj = lax.broadcasted_iota(jnp.int32, (BLK, BLK), 0)
    ii = lax.broadcasted_iota(jnp.int32, (BLK, BLK), 1)
    from_prev = jj > ii
    sc = jnp.where(from_prev, s2_ref[0:BLK, ls] + prev_offset, s2_ref[BLK:2 * BLK, ls])
    m = jnp.maximum(jnp.max(sc, axis=0, keepdims=True), sink_row)
    p = jnp.exp(sc - m)
    es = jnp.exp(sink_row - m)
    inv = 1.0 / (jnp.sum(p, axis=0, keepdims=True) + es)
    return from_prev, p * inv, es * inv


def _put_split(dst_ref, ls, t, from_prev):
    t = t.astype(bf16)
    zero = jnp.zeros_like(t)
    dst_ref[0:BLK, ls] = jnp.where(from_prev, t, zero)
    dst_ref[BLK:2 * BLK, ls] = jnp.where(from_prev, zero, t)


def _heads_side_by_side(ref, h):
    return jnp.concatenate([ref[HD * (GROUP * h + g):HD * (GROUP * h + g) + HD, :] for g in range(GROUP)], axis=1)


def _kv_specs_t():
    prev = pl.BlockSpec((KVW, BLK), lambda n: (0, jnp.maximum(n - 1, 0)))
    cur = pl.BlockSpec((KVW, BLK), lambda n: (0, n))
    return [prev, cur, prev, cur]


def _attn_fwd_t(qt, kt, vt, sinks):
    s = qt.shape[1]

    def body(sink_ref, q_ref, kp_ref, kc_ref, vp_ref, vc_ref, o_ref, s2_scr, pn2_scr):
        n = pl.program_id(0)
        off = jnp.where(n > 0, 0.0, NEG)

        def scores(h):
            hs = slice(HD * h, HD * h + HD)
            kh = jnp.concatenate([kp_ref[hs, :], kc_ref[hs, :]], axis=1)
            s2_scr[h % 2] = _dot_tn(kh, _heads_side_by_side(q_ref, h))

        def probs(h):
            for g in range(GROUP):
                ls = slice(BLK * g, BLK * g + BLK)
                from_prev, pn, _ = _band_softmax(s2_scr.at[h % 2], ls, off, sink_ref[h:h + 1, ls])
                _put_split(pn2_scr.at[h % 2], ls, pn, from_prev)

        def outputs(h):
            hs = slice(HD * h, HD * h + HD)
            vh = jnp.concatenate([vp_ref[hs, :], vc_ref[hs, :]], axis=1)
            og = _dot(vh, pn2_scr[h % 2])
            for g in range(GROUP):
                a = GROUP * h + g
                o_ref[HD * a:HD * a + HD, :] = og[:, BLK * g:BLK * g + BLK]

        scores(0)
        for h in range(NKV):
            if h + 1 < NKV:
                scores(h + 1)
            probs(h)
            outputs(h)

    return pl.pallas_call(
        body, name="attn_fwd", grid=(s // BLK,),
        in_specs=[pl.BlockSpec((NKV, GROUP * BLK), lambda n: (0, 0)), pl.BlockSpec((D, BLK), lambda n: (0, n))]
        + _kv_specs_t(),
        out_specs=pl.BlockSpec((D, BLK), lambda n: (0, n)),
        out_shape=pltpu.HBM((D, s), f32),
        scratch_shapes=[pltpu.VMEM((2, 2 * BLK, GROUP * BLK), f32), pltpu.VMEM((2, 2 * BLK, GROUP * BLK), bf16)],
        compiler_params=_params(("arbitrary",), 32),
    )(_sink_rows(sinks), *_in_hbm(qt, kt, kt, vt, vt))


def _attn_bwd_t(qt, kt, vt, dot, sinks, dwo):
    s = qt.shape[1]
    nb = s // BLK

    def body(sink_ref, q_ref, do_ref, kp_ref, kc_ref, vp_ref, vc_ref, dwo_ref, dq_ref, dk_ref, dv_ref, ds_ref,
             land_ref, dk_hold, dv_hold, s2_scr, dp2_scr, pn2_scr, ds2_scr, send_sems, recv_sems, local_sems):
        n = pl.program_id(0)
        start, finish = _scatter_ops([dwo_ref], [land_ref], send_sems, recv_sems, local_sems)

        @pl.when(n == 0)
        def _():
            start()
            dk_hold[...] = jnp.zeros_like(dk_hold)
            dv_hold[...] = jnp.zeros_like(dv_hold)
            ds_ref[...] = jnp.zeros_like(ds_ref)

        @pl.when(n < nb)
        def _():
            off = jnp.where(n > 0, 0.0, NEG)

            def scores(h):
                hs = slice(HD * h, HD * h + HD)
                kh = jnp.concatenate([kp_ref[hs, :], kc_ref[hs, :]], axis=1)
                vh = jnp.concatenate([vp_ref[hs, :], vc_ref[hs, :]], axis=1)
                s2_scr[h % 2] = _dot_tn(kh, _heads_side_by_side(q_ref, h))
                dp2_scr[h % 2] = _dot_tn(vh, _heads_side_by_side(do_ref, h))

            def softmax_bwd(h):
                for g in range(GROUP):
                    ls = slice(BLK * g, BLK * g + BLK)
                    from_prev, pn, ps = _band_softmax(s2_scr.at[h % 2], ls, off, sink_ref[h:h + 1, ls])
                    dp = jnp.where(from_prev, dp2_scr[h % 2, 0:BLK, ls], dp2_scr[h % 2, BLK:2 * BLK, ls])
                    dsum = jnp.sum(pn * dp, axis=0, keepdims=True)
                    ds_ref[h:h + 1, ls] += -ps * dsum
                    _put_split(pn2_scr.at[h % 2], ls, pn, from_prev)
                    _put_split(ds2_scr.at[h % 2], ls, pn * (dp - dsum), from_prev)

            def grads(h):
                hs = slice(HD * h, HD * h + HD)
                kh = jnp.concatenate([kp_ref[hs, :], kc_ref[hs, :]], axis=1)
                dqg = _dot(kh, ds2_scr[h % 2])
                for g in range(GROUP):
                    a = GROUP * h + g
                    dq_ref[HD * a:HD * a + HD, :] = dqg[:, BLK * g:BLK * g + BLK]
                dkh = _dot_nt(_heads_side_by_side(q_ref, h), ds2_scr[h % 2])
                dvh = _dot_nt(_heads_side_by_side(do_ref, h), pn2_scr[h % 2])
                dk_ref[hs, :] = dk_hold[hs, :] + dkh[:, 0:BLK]
                dv_ref[hs, :] = dv_hold[hs, :] + dvh[:, 0:BLK]
                dk_hold[hs, :] = dkh[:, BLK:2 * BLK]
                dv_hold[hs, :] = dvh[:, BLK:2 * BLK]

            scores(0)
            for h in range(NKV):
                if h + 1 < NKV:
                    scores(h + 1)
                softmax_bwd(h)
                grads(h)

        @pl.when(n == nb)
        def _():
            dk_ref[...] = dk_hold[...]
            dv_ref[...] = dv_hold[...]
            finish()

    blk = pl.BlockSpec((D, BLK), lambda n: (0, jnp.minimum(n, nb - 1)))
    late = pl.BlockSpec((KVW, BLK), lambda n: (0, jnp.maximum(n - 1, 0)))
    whole = pl.BlockSpec((NKV, GROUP * BLK), lambda n: (0, 0))
    kv = [pl.BlockSpec((KVW, BLK), lambda n: (0, jnp.clip(n - 1, 0, nb - 1))),
          pl.BlockSpec((KVW, BLK), lambda n: (0, jnp.minimum(n, nb - 1)))]
    return pl.pallas_call(
        body, name="attn_bwd", grid=(nb + 1,),
        in_specs=[whole, blk, blk] + kv + kv + [HBM],
        out_specs=[blk, late, late, whole, HBM],
        out_shape=[pltpu.HBM((D, s), f32), pltpu.HBM((KVW, s), f32), pltpu.HBM((KVW, s), f32),
                   jax.ShapeDtypeStruct((NKV, GROUP * BLK), f32), pltpu.HBM((NDEV, WO_ROWS, D), bf16)],
        scratch_shapes=[pltpu.VMEM((KVW, BLK), f32), pltpu.VMEM((KVW, BLK), f32)]
        + [pltpu.VMEM((2, 2 * BLK, GROUP * BLK), f32)] * 2 + [pltpu.VMEM((2, 2 * BLK, GROUP * BLK), bf16)] * 2
        + _comm_sems(1),
        compiler_params=_params(("arbitrary",), 48),
    )(_sink_rows(sinks), *_in_hbm(qt, dot, kt, kt, vt, vt, dwo))


def _gate_terms(pr, pi, br, bi, sp):
    r = _sigmoid(pr + br)
    i = _sigmoid(pi + bi)
    a = jnp.exp(r * (-LRU_C * sp))
    n = r * (2.0 * LRU_C * sp)
    y = jnp.where(n < 0.02, n * (1.0 - n * (0.5 - n * (1.0 / 6.0))), 1.0 - a * a)
    inv_mult = lax.rsqrt(jnp.maximum(y, 1e-30))
    return r, i, a, y * inv_mult, inv_mult


def _later(x, before, k):
    if k == 0:
        return x
    row = lax.broadcasted_iota(jnp.int32, before.shape, 0)
    rolled = pltpu.roll(x, k, 0)
    first = jnp.where(row < k, pltpu.roll(before, k, 0), rolled[0:8])
    return jnp.concatenate([first, rolled[8:]], axis=0)


def _earlier(x, after, k):
    if k == 0:
        return x
    n = x.shape[0]
    row = lax.broadcasted_iota(jnp.int32, after.shape, 0)
    rolled = pltpu.roll(x, n - k, 0)
    last = jnp.where(row >= 8 - k, pltpu.roll(after, 8 - k, 0), rolled[n - 8:n])
    return jnp.concatenate([rolled[0:n - 8], last], axis=0)


def _fwd_fused(h, wt, tabs, wo_shard, conv_w, conv_b, wr, wi, br, bi, lam, tm):
    s = h.shape[0]
    nt = s // tm
    nc = 512
    pieces = 8
    rows_per = tm // pieces
    later_chunks = (0, 1, 2, 3, 4, 7, 8)

    def body(h_ref, wt_ref, tab_ref, wo_ref, cw_ref, cb_ref, wr_ref, wi_ref, br_ref,
             bi_ref, lam_ref, q_ref, k_ref, v_ref, ga_ref, xl_ref, gl_ref, u_ref, hl_ref, r_ref, ig_ref, a_ref,
             mu_ref, im_ref, wo_all, wo_stage, halo, ub_scr, pr_scr, pi_scr, b_scr, hcar,
             send_sems, recv_sems, local_sems):
        i = pl.program_id(0)
        start, pass_on, finish = _gather_ops([wo_stage], [wo_all], send_sems, recv_sems, local_sems)

        @pl.when(i == 0)
        def _():
            wo_stage[...] = wo_ref[...].astype(bf16)
            start()
            halo[...] = jnp.zeros_like(halo)
            hcar[...] = jnp.zeros_like(hcar)

        sp = _softplus(-lam_ref[...])
        br, bi = br_ref[...], bi_ref[...]
        c, sa, sb = _tables(tab_ref)
        piece_rows = lambda p: slice(rows_per * p, rows_per * p + rows_per)

        def project(ci):
            z = _dot_nt(h_ref[...], wt_ref[ci * nc:(ci + 1) * nc, :])
            if ci < 2:
                for j in range(nc // 128):
                    r = _rope(z[:, 128 * j:128 * j + 128], c, sa, sb) * (HD ** -0.5)
                    q_ref[ci * nc + 128 * j:ci * nc + 128 * j + 128, :] = r.astype(bf16).T
            elif ci == 2:
                for j in range(2):
                    js = slice(128 * j, 128 * j + 128)
                    k_ref[js, :] = _rope(z[:, js], c, sa, sb).astype(bf16).T
                    v_ref[js, :] = z[:, KVW + 128 * j:KVW + 128 * j + 128].astype(bf16).T
            else:
                sec, j = divmod(ci - 3, 2)
                (ga_ref, xl_ref, gl_ref)[sec][:, j * nc:(j + 1) * nc] = z

        def gate_terms(p):
            rows = piece_rows(p)
            r, ig, a, mult, inv_mult = _gate_terms(pr_scr[rows, :], pi_scr[rows, :], br, bi, sp)
            r_ref[rows, :] = r
            ig_ref[rows, :] = ig
            a_ref[rows, :] = a
            mu_ref[rows, :] = mult
            im_ref[rows, :] = inv_mult
            b_scr[rows, :] = mult * (ig * u_ref[rows, :])

        def scan(p, hc):
            for t in range(rows_per * p, rows_per * p + rows_per):
                hc = a_ref[t:t + 1, :] * hc + b_scr[t:t + 1, :]
                hl_ref[t:t + 1, :] = hc
            return hc

        project(5)
        project(6)
        xl = xl_ref[...]
        u = cb_ref[...] + sum(cw_ref[k:k + 1, :] * _later(xl, halo[...], CONVW - 1 - k) for k in range(CONVW))
        halo[...] = xl[tm - 8:tm, :]
        u_ref[...] = u
        ub_scr[...] = u.astype(bf16)
        for g in range(NGRP):
            gs = slice(256 * g, 256 * g + 256)
            pr_scr[:, gs] = _dot(ub_scr[:, gs], wr_ref[g])
            pi_scr[:, gs] = _dot(ub_scr[:, gs], wi_ref[g])
        hc = hcar[...]
        gate_terms(0)
        for slot, ci in enumerate(later_chunks):
            project(ci)
            gate_terms(slot + 1)
            hc = scan(slot, hc)
        hcar[...] = scan(pieces - 1, hc)

        @pl.when(i == max(nt - 2, 0))
        def _():
            pass_on()

        @pl.when(i == nt - 1)
        def _():
            finish()

    row = lambda w: pl.BlockSpec((tm, w), lambda i: (i, 0))
    col = lambda w: pl.BlockSpec((w, tm), lambda i: (0, i))
    full = lambda a: pl.BlockSpec(a.shape, lambda i: (0,) * a.ndim)
    big = lambda w, dt: pltpu.HBM((s, w), dt)
    tile = pltpu.VMEM((tm, LW), f32)
    return pl.pallas_call(
        body, name="fwd_fused", grid=(nt,),
        in_specs=[row(D), full(wt), row(384), full(wo_shard), full(conv_w), full(conv_b),
                  full(wr), full(wi), full(br), full(bi), full(lam)],
        out_specs=[col(D), col(KVW), col(KVW), row(D), row(D), row(D)] + [row(LW)] * 7 + [HBM],
        out_shape=[pltpu.HBM((D, s), bf16), pltpu.HBM((KVW, s), bf16), pltpu.HBM((KVW, s), bf16),
                   big(D, f32), big(D, f32), big(D, f32)] + [big(LW, f32)] * 7 + [pltpu.HBM((2 * D, D), bf16)],
        scratch_shapes=[pltpu.VMEM((WO_ROWS, D), bf16), pltpu.VMEM((8, LW), f32),
                        pltpu.VMEM((tm, LW), bf16), tile, tile, tile, pltpu.VMEM((1, LW), f32)] + _comm_sems(1),
        compiler_params=_params(("arbitrary",), 56),
    )(*_in_hbm(h, wt), tabs, wo_shard, conv_w, conv_b, wr, wi, br, bi, lam)


def _lru_bwd(u, hl, dhl, xl, r, ig, a, mu, im, conv_w, wr, wi, lam, tm):
    s = u.shape[0]
    nt = s // tm
    pieces = 8
    rows_per = tm // pieces

    def body(u_ref, h_ref, hp_ref, dh_ref, x_ref, r_ref, ig_ref, a_ref, mu_ref, im_ref, cw_ref, wr_ref, wi_ref,
             lam_ref, dxl_ref, dwr_ref, dwi_ref, dbr_ref, dbi_ref, dlam_ref, dcb_ref, dcw_ref,
             l_scr, du_scr, dpr_scr, dpi_scr, lcar, dunext):
        t0 = pl.program_id(0)
        tile = nt - 1 - t0

        @pl.when(t0 == 0)
        def _():
            lcar[...] = jnp.zeros_like(lcar)
            dunext[...] = jnp.zeros_like(dunext)
            for ref in (dwr_ref, dwi_ref, dbr_ref, dbi_ref, dlam_ref, dcb_ref, dcw_ref):
                ref[...] = jnp.zeros_like(ref)

        lam = lam_ref[...]
        sp = _softplus(-lam)
        hp = jnp.where(tile > 0, hp_ref[...], 0.0)

        def scan(p, c):
            for t in range(rows_per * p + rows_per - 1, rows_per * p - 1, -1):
                lt = dh_ref[t:t + 1, :] + c
                l_scr[t:t + 1, :] = lt
                c = a_ref[t:t + 1, :] * lt
            return c

        def terms(p, sums):
            rows = slice(rows_per * p, rows_per * p + rows_per)
            lt, u, r, i, a, mult, inv_mult = l_scr[rows, :], u_ref[rows, :], r_ref[rows, :], ig_ref[rows, :], \
                a_ref[rows, :], mu_ref[rows, :], im_ref[rows, :]
            before = hp if p == 0 else h_ref[rows_per * p - 8:rows_per * p, :]
            hprev = _later(h_ref[rows, :], before, 1)
            iu = i * u
            lm = lt * mult
            du_scr[rows, :] = lm * i
            dla = (lt * hprev) * a - ((lt * iu) * (a * a)) * inv_mult
            dlar = dla * r
            dpr = (dlar * (1.0 - r)) * (-LRU_C * sp)
            dpi = (lm * iu) * (1.0 - i)
            dpr_scr[rows, :] = dpr.astype(bf16)
            dpi_scr[rows, :] = dpi.astype(bf16)
            col = lambda t: jnp.sum(t, axis=0, keepdims=True)
            return sums[0] + col(dlar), sums[1] + col(dpr), sums[2] + col(dpi)

        sums = (jnp.zeros((1, LW), f32),) * 3
        c = scan(pieces - 1, lcar[...])
        for p in range(pieces - 1, -1, -1):
            if p > 0:
                c = scan(p - 1, c)
            sums = terms(p, sums)
        lcar[...] = c
        dlam_ref[...] += sums[0] * (-LRU_C)
        dbr_ref[...] += sums[1]
        dbi_ref[...] += sums[2]

        ub = u_ref[...].astype(bf16)
        dug = []
        for g in range(NGRP):
            gs = slice(256 * g, 256 * g + 256)
            dwr_ref[g] += _dot_tn(ub[:, gs], dpr_scr[:, gs])
            dwi_ref[g] += _dot_tn(ub[:, gs], dpi_scr[:, gs])
            dug.append(_dot_nt(dpr_scr[:, gs], wr_ref[g]) + _dot_nt(dpi_scr[:, gs], wi_ref[g]))
        du = du_scr[...] + jnp.concatenate(dug, axis=1)

        dcb_ref[...] += jnp.sum(du, axis=0, keepdims=True)
        x = x_ref[...]
        after = dunext[...]
        dxl = jnp.zeros_like(du)
        for k in range(CONVW):
            e = _earlier(du, after, CONVW - 1 - k)
            dxl = dxl + cw_ref[k:k + 1, :] * e
            dcw_ref[k:k + 1, :] += jnp.sum(e * x, axis=0, keepdims=True)
        dxl_ref[...] = dxl.astype(bf16)
        dunext[...] = du[0:8, :]

        @pl.when(t0 == nt - 1)
        def _():
            dlam_ref[...] = dlam_ref[...] * (-_sigmoid(-lam))

    rev = lambda i: (nt - 1 - i, 0)
    row = pl.BlockSpec((tm, LW), rev)
    prev8 = pl.BlockSpec((8, LW), lambda i: (jnp.maximum((nt - 1 - i) * (tm // 8) - 1, 0), 0))
    full = lambda a: pl.BlockSpec(a.shape, lambda i: (0,) * a.ndim)
    vec = pl.BlockSpec((1, LW), lambda i: (0, 0))
    bd = pl.BlockSpec((NGRP, 256, 256), lambda i: (0, 0, 0))
    return pl.pallas_call(
        body, name="lru_bwd", grid=(nt,),
        in_specs=[row, row, prev8] + [row] * 7 + [full(conv_w), full(wr), full(wi), full(lam)],
        out_specs=[row, bd, bd, vec, vec, vec, vec, pl.BlockSpec((CONVW, LW), lambda i: (0, 0))],
        out_shape=[pltpu.HBM((s, LW), bf16),
                   jax.ShapeDtypeStruct((NGRP, 256, 256), f32), jax.ShapeDtypeStruct((NGRP, 256, 256), f32),
                   jax.ShapeDtypeStruct((1, LW), f32), jax.ShapeDtypeStruct((1, LW), f32),
                   jax.ShapeDtypeStruct((1, LW), f32), jax.ShapeDtypeStruct((1, LW), f32),
                   jax.ShapeDtypeStruct((CONVW, LW), f32)],
        scratch_shapes=[pltpu.VMEM((tm, LW), f32), pltpu.VMEM((tm, LW), f32), pltpu.VMEM((tm, LW), bf16),
                        pltpu.VMEM((tm, LW), bf16), pltpu.VMEM((1, LW), f32), pltpu.VMEM((8, LW), f32)],
        compiler_params=_params(("arbitrary",), 56),
    )(*_in_hbm(u, hl, hl, dhl, xl, r, ig, a, mu, im), conv_w, wr, wi, lam)


def _gated_norm(t, gate, gain):
    sg = _sigmoid(gate)
    silu = gate * sg
    p = t * silu
    rstd = lax.rsqrt(jnp.mean(p * p, axis=-1, keepdims=True) + EPS)
    ph = p * rstd
    return sg, silu, rstd, ph, ph * gain


def _gated_norm_bwd(dy, t, gate, gain, sg, silu, rstd, ph):
    w = dy * gain
    dp = rstd * (w - ph * jnp.mean(w * ph, axis=-1, keepdims=True))
    dgate = (dp * t) * (sg + silu * (1.0 - sg))
    return jnp.sum(dy * ph, axis=0, keepdims=True), dp * silu, dgate


def _out_fwd_bwd(x, tgt, o, ga, hl, gl, again, lgain, fgain, wo, tm):
    s = x.shape[0]
    nt = s // tm

    def body(x_ref, t_ref, o_ref, ga_ref, hl_ref, gl_ref, ag_ref, lg_ref, fg_ref, wo_ref,
             dx2_ref, do_ref, dga_ref, dhl_ref, dgl_ref, dwo_ref, gfg_ref, gag_ref, glg_ref, loss_ref, acc):
        i = pl.program_id(0)

        @pl.when(i == 0)
        def _():
            acc[...] = jnp.zeros_like(acc)
            for ref in (gfg_ref, gag_ref, glg_ref, loss_ref):
                ref[...] = jnp.zeros_like(ref)

        oo = jnp.concatenate([o_ref[128 * j:128 * j + 128, :].T for j in range(D // 128)], axis=1)
        gga, hh, ggl = ga_ref[...], hl_ref[...], gl_ref[...]
        ag, lg, fg = ag_ref[...], lg_ref[...], fg_ref[...]
        sga, silua, ra, pah, ya = _gated_norm(oo, gga, ag)
        sgl, silul, rl, plh, yl = _gated_norm(hh, ggl, lg)
        yab, ylb = ya.astype(bf16), yl.astype(bf16)
        y = _dot(yab, wo_ref[0:D, :]) + _dot(ylb, wo_ref[D:2 * D, :])
        x2 = x_ref[...] + y
        r2 = lax.rsqrt(jnp.mean(x2 * x2, axis=-1, keepdims=True) + EPS)
        x2h = x2 * r2
        err = x2h * fg - t_ref[...]
        loss_ref[...] += 0.5 * jnp.sum(jnp.sum(err * err, axis=-1, keepdims=True) * (1.0 / D))
        gfg_ref[...] += jnp.sum(err * x2h, axis=0, keepdims=True) * (1.0 / D)
        w = err * (fg * (1.0 / D))
        dx2 = r2 * (w - x2h * jnp.mean(w * x2h, axis=-1, keepdims=True))
        dx2_ref[...] = dx2
        dyb = dx2.astype(bf16)
        acc[0:D, :] += _dot_tn(yab, dyb)
        acc[D:2 * D, :] += _dot_tn(ylb, dyb)
        dya = _dot_nt(dyb, wo_ref[0:D, :])
        dyl = _dot_nt(dyb, wo_ref[D:2 * D, :])
        gag, do, dga = _gated_norm_bwd(dya, oo, gga, ag, sga, silua, ra, pah)
        glg, dhl, dgl = _gated_norm_bwd(dyl, hh, ggl, lg, sgl, silul, rl, plh)
        gag_ref[...] += gag
        glg_ref[...] += glg
        dob = do.astype(bf16)
        for j in range(D // 128):
            do_ref[128 * j:128 * j + 128, :] = dob[:, 128 * j:128 * j + 128].T
        dga_ref[...] = dga.astype(bf16)
        dhl_ref[...] = dhl
        dgl_ref[...] = dgl.astype(bf16)

        @pl.when(i == nt - 1)
        def _():
            dwo_ref[...] = acc[...].astype(bf16)

    row = pl.BlockSpec((tm, D), lambda i: (i, 0))
    col = pl.BlockSpec((D, tm), lambda i: (0, i))
    vec = pl.BlockSpec((1, D), lambda i: (0, 0))
    mat = pl.BlockSpec((2 * D, D), lambda i: (0, 0))
    return pl.pallas_call(
        body, name="out_fwd_bwd", grid=(nt,),
        in_specs=[row, row, col, row, row, row] + [vec] * 3 + [mat],
        out_specs=[row, col, row, row, row] + [mat, vec, vec, vec, pl.BlockSpec((1, 128), lambda i: (0, 0))],
        out_shape=[pltpu.HBM((s, D), f32), pltpu.HBM((D, s), bf16),
                   pltpu.HBM((s, D), bf16), pltpu.HBM((s, D), f32),
                   pltpu.HBM((s, D), bf16), pltpu.HBM((2 * D, D), bf16),
                   jax.ShapeDtypeStruct((1, D), f32), jax.ShapeDtypeStruct((1, D), f32),
                   jax.ShapeDtypeStruct((1, D), f32), jax.ShapeDtypeStruct((1, 128), f32)],
        scratch_shapes=[pltpu.VMEM((2 * D, D), f32)],
        compiler_params=_params(("arbitrary",), 56),
    )(*_in_hbm(x, tgt, o, ga, hl, gl), again, lgain, fgain, *_in_hbm(wo))


def _bwd_in(x, dx2, dq, dk, dv, dga, dxl, dgl, ln_gain, wt, tabs, tm):
    s = x.shape[0]

    def body(x_ref, dx2_ref, dq_ref, dk_ref, dv_ref, dga_ref, dxl_ref, dgl_ref, g_ref, wt_ref,
             tab_ref, gx_ref, gln_ref, dzt_ref):
        @pl.when(pl.program_id(0) == 0)
        def _():
            gln_ref[...] = jnp.zeros_like(gln_ref)

        c, sa, sb = (t.T for t in _tables(tab_ref))
        for j in range(D // 128):
            js = slice(128 * j, 128 * j + 128)
            dzt_ref[js, :] = (_unrope_t(dq_ref[js, :], c, sa, sb) * (HD ** -0.5)).astype(bf16)
        for j in range(KVW // 128):
            js = slice(128 * j, 128 * j + 128)
            dzt_ref[D + 128 * j:D + 128 * j + 128, :] = _unrope_t(dk_ref[js, :], c, sa, sb).astype(bf16)
        dzt_ref[D + KVW:D + 2 * KVW, :] = dv_ref[...].astype(bf16)
        first = D + 2 * KVW
        dh = _dot_tn(dzt_ref[0:512, :], wt_ref[0:512, :])
        for ci in range(1, first // 512):
            dh = dh + _dot_tn(dzt_ref[512 * ci:512 * ci + 512, :], wt_ref[512 * ci:512 * ci + 512, :])
        for sec, ref in enumerate((dga_ref, dxl_ref, dgl_ref)):
            for j in range(D // 512):
                rows = slice(first + D * sec + 512 * j, first + D * sec + 512 * j + 512)
                dh = dh + _dot(ref[:, 512 * j:512 * j + 512], wt_ref[rows, :])
            for j in range(D // 128):
                dzt_ref[first + D * sec + 128 * j:first + D * sec + 128 * j + 128, :] = ref[:, 128 * j:128 * j + 128].T
        xx = x_ref[...]
        rstd = lax.rsqrt(jnp.mean(xx * xx, axis=-1, keepdims=True) + EPS)
        xh = xx * rstd
        gln_ref[...] += jnp.sum(dh * xh, axis=0, keepdims=True)
        w = dh * g_ref[...]
        gx_ref[...] = dx2_ref[...] + rstd * (w - xh * jnp.mean(w * xh, axis=-1, keepdims=True))

    row = lambda w: pl.BlockSpec((tm, w), lambda i: (i, 0))
    col = lambda w: pl.BlockSpec((w, tm), lambda i: (0, i))
    full = lambda a: pl.BlockSpec(a.shape, lambda i: (0, 0))
    return pl.pallas_call(
        body, name="bwd_in", grid=(s // tm,),
        in_specs=[row(D), row(D), col(D), col(KVW), col(KVW), row(D), row(D), row(D), full(ln_gain), full(wt),
                  row(384)],
        out_specs=[row(D), pl.BlockSpec((1, D), lambda i: (0, 0)), col(NIN)],
        out_shape=[pltpu.HBM((s, D), f32), jax.ShapeDtypeStruct((1, D), f32),
                   pltpu.HBM((NIN, s), bf16)],
        compiler_params=_params(("arbitrary",), 56),
    )(*_in_hbm(x, dx2, dq, dk, dv, dga, dxl, dgl), ln_gain, *_in_hbm(wt), tabs)


WT_TERMS = 4


def _dwt_scatter(dzt, h, small, tm):
    s = h.shape[0]
    nk = s // tm
    srows = small.shape[0] // NDEV
    last = NDEV - 1
    sm_turn = 2

    def body(order_ref, dz_ref, h_ref, sm_ref, lwt_ref, rep_all, tail_ref, acc, stage, given, relayed, lsm, rep_stage,
             send_sems, recv_sems, local_sem, sm_send, sm_recv, sm_local, rep_send, rep_recv, rep_local):
        j, k = pl.program_id(0), pl.program_id(1)
        x, y, c = _place()
        sibling = (x, y, 1 - c)
        near = (x ^ (1 - c), y ^ c)
        far = (x ^ c, y ^ (1 - c))
        sm_start, sm_finish = _scatter_ops([sm_ref], [lsm], sm_send, sm_recv, sm_local)
        rep_start, rep_pass_on, rep_finish = _gather_ops([rep_stage], [rep_all], rep_send, rep_recv, rep_local)

        def send(step):
            if step == last - 1:
                dst, to = lwt_ref.at[1], sibling
            elif step % 2 == 0:
                dst, to = given.at[step // 2], sibling
            elif step == 1:
                dst, to = relayed, (*near, c)
            else:
                dst, to = lwt_ref.at[1 + step // 2], (*(near if step == 3 else far), c)
            return pltpu.make_async_remote_copy(
                src_ref=stage.at[step % 2], dst_ref=dst, send_sem=send_sems.at[step], recv_sem=recv_sems.at[step],
                device_id=to, device_id_type=MESH)

        def keep():
            return pltpu.make_async_copy(stage.at[last % 2], lwt_ref.at[0], local_sem)

        @pl.when((j == 0) & (k == 0))
        def _():
            sm_start()

        @pl.when(k == 0)
        def _():
            acc[...] = jnp.zeros_like(acc)

        acc[...] += _dot(dz_ref[...], h_ref[...])

        for step in range(NDEV):
            @pl.when((k == nk - 1) & (j == step))
            def _(step=step):
                if step >= 2:
                    send(step - 2).wait_send()
                if step % 2 == 1 and step < last:
                    send(step - 1).wait_recv()
                    total = acc[...] + given[step // 2].astype(f32)
                    if step == 5:
                        send(1).wait_recv()
                        total = total + relayed[...].astype(f32)
                    stage[step % 2] = total.astype(bf16)
                else:
                    stage[step % 2] = acc[...].astype(bf16)
                if step < last:
                    send(step).start()
                else:
                    keep().start()
                    send(last - 1).wait_send()
                    for peer_step in (3, 5, last - 1):
                        send(peer_step).wait_recv()
                    keep().wait()
                    rep_finish()
                if step == sm_turn:
                    sm_finish()
                    total_sm = lsm[0]
                    for dev in range(1, NDEV):
                        total_sm = total_sm + lsm[dev]
                    rep_stage[...] = total_sm[0:SMALL_PER]
                    tail_ref[...] = total_sm[SMALL_PER:]
                    rep_start()
                if step == last - 1:
                    rep_pass_on()

    x, y, c = _place()
    dest = lambda chip, cc: 4 * chip[0] + 2 * chip[1] + cc
    near, far, diag = (x ^ (1 - c), y ^ c), (x ^ c, y ^ (1 - c)), (1 - x, 1 - y)
    order = jnp.stack([dest(diag, 1 - c), dest(diag, c), dest(far, 1 - c), dest(near, c),
                       dest(near, 1 - c), dest(far, c), dest((x, y), 1 - c), dest((x, y), c)])
    return pl.pallas_call(
        body, name="dwt_scatter",
        grid_spec=pltpu.PrefetchScalarGridSpec(
            num_scalar_prefetch=1, grid=(NDEV, nk),
            in_specs=[pl.BlockSpec((WT_ROWS, tm), lambda j, k, order: (order[j], k)),
                      pl.BlockSpec((tm, D), lambda j, k, order: (k, 0)), HBM],
            out_specs=[HBM, HBM, pl.BlockSpec((srows - SMALL_PER, D), lambda j, k, order: (0, 0))],
            scratch_shapes=[pltpu.VMEM((WT_ROWS, D), f32), pltpu.VMEM((2, WT_ROWS, D), bf16),
                            pltpu.VMEM((3, WT_ROWS, D), bf16), pltpu.VMEM((WT_ROWS, D), bf16),
                            pltpu.VMEM((NDEV, srows, D), f32), pltpu.VMEM((SMALL_PER, D), f32),
                            pltpu.SemaphoreType.DMA((last,)), pltpu.SemaphoreType.DMA((last,)),
                            pltpu.SemaphoreType.DMA(())] + _comm_sems(1) + _comm_sems(1)),
        out_shape=[pltpu.HBM((WT_TERMS, WT_ROWS, D), bf16), pltpu.HBM((SMALL_ROWS, D), f32),
                   jax.ShapeDtypeStruct((srows - SMALL_PER, D), f32)],
        compiler_params=_params(("arbitrary", "arbitrary"), 48),
    )(order, *_in_hbm(dzt, h, small))


def _diag_blocks(bd):
    eye = jnp.eye(4, dtype=bd.dtype)
    return jnp.einsum('gjckd,jk->gjcd', bd.reshape(NGRP, 4, HD, 4, HD), eye).reshape(NQ, HD, HD)


def _sequence_step(x, h, tgt, wt, wo_shard, conv_w, wr, wi, p):
    s = x.shape[0]
    tm = min(256, s)
    tabs = _rope_tables(s)
    sinks = p["sinks"].reshape(NQ)
    qt, kt, vt, ga, xl, gl, u, hl, r, ig, a, mu, im, wo = _fwd_fused(
        h, wt, tabs, wo_shard, conv_w, p["conv_b"], wr, wi, p["b_rgate"], p["b_igate"], p["lru_lambda"], tm)
    ot = _attn_fwd_t(qt, kt, vt, sinks)
    dx2, dot, dga, dhl, dgl, dwo, g_fg, g_ag, g_lg, loss = _out_fwd_bwd(
        x, tgt, ot, ga, hl, gl, p["attn_out_gain"], p["lru_out_gain"], p["final_gain"], wo, tm)
    dqt, dkt, dvt, dsink, land_wo = _attn_bwd_t(qt, kt, vt, dot, sinks, dwo)
    dxl, dwr, dwi, dbr, dbi, dlam, dcb, dcw = _lru_bwd(
        u, hl, dhl, xl, r, ig, a, mu, im, conv_w, wr, wi, p["lru_lambda"], tm)
    gx, g_ln, dzt = _bwd_in(x, dx2, dqt, dkt, dvt, dga, dxl, dgl, p["ln_gain"], wt, tabs, tm)
    small = dict(ln_gain=g_ln, sinks=dsink.reshape(NQ, BLK).sum(axis=1)[None], conv_w=dcw, conv_b=dcb,
                 w_rgate=_diag_blocks(dwr), b_rgate=dbr, w_igate=_diag_blocks(dwi), b_igate=dbi, lru_lambda=dlam,
                 attn_out_gain=g_ag, lru_out_gain=g_lg, final_gain=g_fg)
    land_wt, g_rep, g_tail = _dwt_scatter(dzt, h, _pack_small(small, loss), min(2048, s))
    return gx, land_wt, land_wo, g_rep, g_tail


def _gather_weights(wt_shard, conv_blk, x, ln_gain, w_rgate, w_igate, tm):
    s = x.shape[0]

    def body(wt_ref, cw_ref, g_ref, wrg_ref, wig_ref, x_ref, wt_all, cw_all, h_ref, wr_ref, wi_ref,
             stage, xbuf, hbuf, send_sems, recv_sems, local_sems):
        stage[...] = wt_ref[...].astype(bf16)
        start, finish = _relay_gather_ops([stage, cw_ref], [wt_all, cw_all], send_sems, recv_sems, local_sems)
        start()
        for src, dst in ((wrg_ref, wr_ref), (wig_ref, wi_ref)):
            dst[...] = jnp.zeros_like(dst)
            for nb in range(NQ):
                g, j = divmod(nb, 4)
                dst[g, HD * j:HD * j + HD, HD * j:HD * j + HD] = src[nb].astype(bf16)
        gain = g_ref[...]
        for i in range(s // tm):
            rows = pl.ds(i * tm, tm)
            pltpu.sync_copy(x_ref.at[rows, :], xbuf)
            xx = xbuf[...]
            rstd = lax.rsqrt(jnp.mean(xx * xx, axis=-1, keepdims=True) + EPS)
            hbuf[...] = (xx * rstd * gain).astype(bf16)
            pltpu.sync_copy(hbuf, h_ref.at[rows, :])
        finish()

    vmem = pl.BlockSpec(memory_space=pltpu.VMEM)
    return pl.pallas_call(
        body, name="gather_weights",
        in_specs=[vmem] * 5 + [HBM], out_specs=[HBM, HBM, HBM, vmem, vmem],
        out_shape=[pltpu.HBM((NIN, D), bf16), pltpu.HBM((NDEV * 8, 128), f32), pltpu.HBM((s, D), bf16)]
        + [jax.ShapeDtypeStruct((NGRP, 256, 256), bf16)] * 2,
        scratch_shapes=[pltpu.VMEM((WT_ROWS, D), bf16), pltpu.VMEM((tm, D), f32), pltpu.VMEM((tm, D), bf16)]
        + _comm_sems(2),
        compiler_params=pltpu.CompilerParams(vmem_limit_bytes=32 * MIB),
    )(wt_shard, conv_blk, ln_gain, w_rgate, w_igate, *_in_hbm(x))


def _adam_math(w, g, m, v):
    m2 = ADAM_B1 * m + (1.0 - ADAM_B1) * g
    v2 = ADAM_B2 * v + (1.0 - ADAM_B2) * (g * g)
    m_hat = m2 / (1.0 - ADAM_B1 ** ADAM_STEP)
    v_hat = v2 / (1.0 - ADAM_B2 ** ADAM_STEP)
    delta = -ADAM_LR * (m_hat / (jnp.sqrt(v_hat) + ADAM_EPS) + ADAM_WD * w)
    return delta, m2, v2


def _reduce_adamw(land, w, m, v, tr, name):
    terms, rows, cols = land.shape

    def body(l_ref, w_ref, m_ref, v_ref, g_ref, d_ref, m2_ref, v2_ref):
        g = l_ref[0].astype(f32)
        for t in range(1, terms):
            g = g + l_ref[t].astype(f32)
        g_ref[...] = g
        d_ref[...], m2_ref[...], v2_ref[...] = _adam_math(w_ref[...], g, m_ref[...], v_ref[...])

    blk = pl.BlockSpec((tr, cols), lambda i: (i, 0))
    return pl.pallas_call(
        body, name=name, grid=(rows // tr,),
        in_specs=[pl.BlockSpec((terms, tr, cols), lambda i: (0, i, 0))] + [blk] * 3, out_specs=[blk] * 4,
        out_shape=[jax.ShapeDtypeStruct((rows, cols), f32)] * 4,
        compiler_params=_params(("arbitrary",), 32),
    )(*_in_hbm(land), w, m, v)


VEC_NAMES = ("ln_gain", "conv_b", "b_rgate", "b_igate", "lru_lambda", "attn_out_gain", "lru_out_gain", "final_gain")
ROW_RGATE, ROW_IGATE, ROW_VEC, ROW_SINKS = 0, 64, 128, 136
LOSS_LANE = NQ


def _adamw_small(g_rep, g_conv, w, m, v):
    names = list(VEC_NAMES) + ["sinks", "conv_w", "w_rgate", "w_igate"]
    ins = [g_rep, g_conv] + [d[k] for k in names for d in (w, m, v)]

    def body(*refs):
        g_ref, gc_ref = refs[0], refs[1]
        in_refs = refs[2:2 + 3 * len(names)]
        out_refs = refs[2 + 3 * len(names):]

        def update(j, g, at=None):
            w_ref, m_ref, v_ref = in_refs[3 * j:3 * j + 3]
            outs = out_refs[4 * j:4 * j + 4]
            pick = (lambda r: r[...]) if at is None else (lambda r: r[at])
            res = (g,) + _adam_math(pick(w_ref), g, pick(m_ref), pick(v_ref))
            for o_ref, val in zip(outs, res):
                if at is None:
                    o_ref[...] = val
                else:
                    o_ref[at] = val

        for j in range(len(VEC_NAMES)):
            update(j, g_ref[ROW_VEC + j:ROW_VEC + j + 1, :])
        update(len(VEC_NAMES), g_ref[ROW_SINKS:ROW_SINKS + 1, 0:NQ])
        update(len(VEC_NAMES) + 1, gc_ref[...], at=0)
        for gi, row0 in ((len(VEC_NAMES) + 2, ROW_RGATE), (len(VEC_NAMES) + 3, ROW_IGATE)):
            for nb in range(NQ):
                update(gi, g_ref[row0:row0 + HD, HD * nb:HD * nb + HD], at=(0, nb))

    vmem = pl.BlockSpec(memory_space=pltpu.VMEM)
    out_shape = [jax.ShapeDtypeStruct(w[k].shape, f32) for k in names for _ in range(4)]
    outs = pl.pallas_call(
        body, name="adamw_small",
        in_specs=[vmem] * len(ins), out_specs=[vmem] * len(out_shape), out_shape=out_shape,
        compiler_params=pltpu.CompilerParams(vmem_limit_bytes=32 * MIB),
    )(*ins)
    return {k: tuple(outs[4 * j:4 * j + 4]) for j, k in enumerate(names)}


def _pack_small(small, loss):
    gate = lambda g: g.transpose(1, 0, 2).reshape(HD, NQ * HD)
    row_s = jnp.concatenate([small["sinks"], loss[:, LOSS_LANE:128], jnp.zeros((1, D - 128), f32)], axis=1)
    rep = jnp.concatenate([gate(small["w_rgate"]), gate(small["w_igate"])] + [small[k] for k in VEC_NAMES]
                          + [row_s, jnp.zeros((SMALL_ROWS - ROW_SINKS - 1, D), f32)], axis=0)
    conv = small["conv_w"].reshape(CONVW, NDEV, 128).transpose(1, 0, 2)
    conv = jnp.pad(conv, ((0, 0), (0, 8 - CONVW), (0, D - 128)))
    return jnp.concatenate([rep.reshape(NDEV, SMALL_PER, D), conv], axis=1).reshape(NDEV * (SMALL_PER + 8), D)


def kernel(x, ln_gain, w_in, sinks, conv_w, conv_b, w_rgate, b_rgate, w_igate, b_igate, lru_lambda, attn_out_gain, lru_out_gain, w_out, final_gain, loss_target, m_ln_gain, m_w_in, m_sinks, m_conv_w, m_conv_b, m_w_rgate, m_b_rgate, m_w_igate, m_b_igate, m_lru_lambda, m_attn_out_gain, m_lru_out_gain, m_w_out, m_final_gain, v_ln_gain, v_w_in, v_sinks, v_conv_w, v_conv_b, v_w_rgate, v_b_rgate, v_w_igate, v_b_igate, v_lru_lambda, v_attn_out_gain, v_lru_out_gain, v_w_out, v_final_gain):
    w = dict(ln_gain=ln_gain, sinks=sinks, conv_w=conv_w, conv_b=conv_b, w_rgate=w_rgate, b_rgate=b_rgate,
             w_igate=w_igate, b_igate=b_igate, lru_lambda=lru_lambda, attn_out_gain=attn_out_gain,
             lru_out_gain=lru_out_gain, final_gain=final_gain.reshape(1, D))
    m = dict(ln_gain=m_ln_gain, sinks=m_sinks, conv_w=m_conv_w, conv_b=m_conv_b, w_rgate=m_w_rgate,
             b_rgate=m_b_rgate, w_igate=m_w_igate, b_igate=m_b_igate, lru_lambda=m_lru_lambda,
             attn_out_gain=m_attn_out_gain, lru_out_gain=m_lru_out_gain, final_gain=m_final_gain.reshape(1, D))
    v = dict(ln_gain=v_ln_gain, sinks=v_sinks, conv_w=v_conv_w, conv_b=v_conv_b, w_rgate=v_w_rgate,
             b_rgate=v_b_rgate, w_igate=v_w_igate, b_igate=v_b_igate, lru_lambda=v_lru_lambda,
             attn_out_gain=v_attn_out_gain, lru_out_gain=v_lru_out_gain, final_gain=v_final_gain.reshape(1, D))

    conv_blk = jnp.pad(conv_w[0], ((0, 8 - CONVW), (0, 0)))
    wt, cw_all, h, wr, wi = _gather_weights(w_in[0].T, conv_blk, x[0], ln_gain, w_rgate[0], w_igate[0],
                                            min(512, x.shape[1]))
    conv_full = cw_all.reshape(NDEV, 8, 128)[:, 0:CONVW].transpose(1, 0, 2).reshape(CONVW, LW)

    p = {k: w[k] for k in w if k not in ("conv_w", "w_rgate", "w_igate")}
    gx, land_wt, land_wo, g_rep, g_tail = _sequence_step(
        x[0], h, loss_target[0], wt, w_out[0], conv_full, wr, wi, p)
    g_conv = g_tail[0:CONVW, 0:128]

    wins = _reduce_adamw(land_wt, w_in[0].T, m_w_in[0].T, v_w_in[0].T, 192, "adamw_w_in")
    g_win, d_win, m_win, v_win = (t.T for t in wins)
    g_wo, d_wo, m_wo, v_wo = _reduce_adamw(land_wo, w_out[0], m_w_out[0], v_w_out[0], 256, "adamw_w_out")
    res = _adamw_small(g_rep, g_conv, w, m, v)
    res["w_in"] = tuple(t[None] for t in (g_win, d_win, m_win, v_win))
    res["w_out"] = tuple(t[None] for t in (g_wo, d_wo, m_wo, v_wo))
    res["final_gain"] = tuple(t.reshape(D) for t in res["final_gain"])

    order = ("ln_gain", "w_in", "sinks", "conv_w", "conv_b", "w_rgate", "b_rgate", "w_igate", "b_igate",
             "lru_lambda", "attn_out_gain", "lru_out_gain", "w_out", "final_gain")
    total_loss = g_rep[ROW_SINKS, LOSS_LANE]
    return (total_loss, gx[None]) + tuple(res[k][i] for i in range(4) for k in order)
```

```python
import jax
import jax.numpy as jnp
from jax import lax
from jax.experimental import pallas as pl
from jax.experimental.pallas import tpu as pltpu

f32 = jnp.float32
bf16 = jnp.bfloat16

D = 1024
HD = 64
NQ = 16
NKV = 4
GROUP = NQ // NKV
KVW = NKV * HD
BLK = 128
ROT = 16
THETA = 500000.0
NEG = -1e30
LW = 1024
NGRP = 4
CONVW = 4
LRU_C = 8.0
NIN = 4608
EPS = 1e-6
NDEV = 8
WT_ROWS = NIN // NDEV
WO_ROWS = 2 * D // NDEV
SMALL_ROWS = 192
SMALL_PER = SMALL_ROWS // NDEV

ADAM_LR = 0.001
ADAM_B1 = 0.9
ADAM_B2 = 0.999
ADAM_EPS = 1e-08
ADAM_WD = 0.01
ADAM_STEP = 10

NT = (((1,), (1,)), ((), ()))
TN = (((0,), (0,)), ((), ()))
MESH = pl.DeviceIdType.MESH
MIB = 1024 * 1024


def _dot(a, b):
    return jnp.dot(a, b, preferred_element_type=f32)


def _dot_nt(a, b):
    return lax.dot_general(a, b, NT, preferred_element_type=f32)


def _dot_tn(a, b):
    return lax.dot_general(a, b, TN, preferred_element_type=f32)


def _params(sem, vmem_mib):
    return pltpu.CompilerParams(dimension_semantics=sem, vmem_limit_bytes=vmem_mib * MIB)


def _sigmoid(x):
    return 0.5 * jnp.tanh(0.5 * x) + 0.5


def _softplus(x):
    return jnp.maximum(x, 0.0) + jnp.log(1.0 + jnp.exp(-jnp.abs(x)))


def _rope_tables(s):
    pos = jnp.arange(s, dtype=f32)
    inv_freq = THETA ** (-jnp.arange(0, ROT, 2, dtype=f32) / ROT)
    ang = pos[:, None] * inv_freq[None, :]
    cs = jnp.concatenate([jnp.cos(ang) - 1.0, jnp.sin(ang)], axis=1)
    d = jnp.arange(128) % HD
    j = jnp.arange(ROT)[:, None]
    pick_c = ((d < ROT) & (j == d % (ROT // 2))).astype(f32)
    pick_sa = ((d >= ROT // 2) & (d < ROT) & (j == d)).astype(f32)
    pick_sb = -((d < ROT // 2) & (j == d + ROT // 2)).astype(f32)
    picks = jnp.concatenate([pick_c, pick_sa, pick_sb], axis=1)
    ones = jnp.concatenate([jnp.ones((1, 128), f32), jnp.zeros((1, 256), f32)], axis=1)
    return jnp.dot(cs, picks, precision=lax.Precision.HIGHEST) + ones


def _tables(tab_ref):
    return tab_ref[:, 0:128], tab_ref[:, 128:256], tab_ref[:, 256:384]


def _rope(t, c, sa, sb):
    return t * c + pltpu.roll(t, 8, 1) * sa + pltpu.roll(t, 120, 1) * sb


def _unrope_t(dr, c, sa, sb):
    return dr * c + pltpu.roll(dr * sa, 120, 0) + pltpu.roll(dr * sb, 8, 0)


def _place():
    return lax.axis_index("x"), lax.axis_index("y"), lax.axis_index("c")


def _gather_ops(mine_refs, out_refs, send_sems, recv_sems, local_sems):
    n = len(mine_refs)
    x, y, c = _place()
    me, sibling = (x, y, c), (x, y, 1 - c)
    chips = [(1 - x, y), (x, 1 - y), (1 - x, 1 - y)]

    def rows(a, dev):
        m = mine_refs[a].shape[0]
        return out_refs[a].at[pl.ds((4 * dev[0] + 2 * dev[1] + dev[2]) * m, m), :]

    def copy(a, k, block, to, own=False):
        return pltpu.make_async_remote_copy(
            src_ref=mine_refs[a] if own else rows(a, block), dst_ref=rows(a, block),
            send_sem=send_sems.at[a, k], recv_sem=recv_sems.at[a, k], device_id=to, device_id_type=MESH)

    def local(a):
        return pltpu.make_async_copy(mine_refs[a], rows(a, me), local_sems.at[a])

    def first(a):
        return [copy(a, 0, me, sibling, own=True)] + [copy(a, 1 + j, me, (*chip, c), own=True)
                                                      for j, chip in enumerate(chips)]

    def start():
        for a in range(n):
            local(a).start()
            for cp in first(a):
                cp.start()

    def pass_on():
        for j, chip in enumerate(chips):
            for a in range(n):
                copy(a, 1 + j, (*chip, c), me).wait_recv()
                copy(a, 4 + j, (*chip, c), sibling).start()

    def finish():
        for a in range(n):
            copy(a, 0, sibling, me).wait_recv()
            for j, chip in enumerate(chips):
                copy(a, 4 + j, (*chip, 1 - c), me).wait_recv()
        for a in range(n):
            for cp in first(a) + [copy(a, 4 + j, (*chip, c), sibling) for j, chip in enumerate(chips)]:
                cp.wait_send()
            local(a).wait()

    return start, pass_on, finish


def _relay_gather_ops(mine_refs, out_refs, send_sems, recv_sems, local_sems):
    n = len(mine_refs)
    x, y, c = _place()
    me, sibling = (x, y, c), (x, y, 1 - c)
    near = (x ^ (1 - c), y ^ c)
    far = (x ^ c, y ^ (1 - c))
    diag = (1 - x, 1 - y)

    def rows(a, dev):
        m = mine_refs[a].shape[0]
        return out_refs[a].at[pl.ds((4 * dev[0] + 2 * dev[1] + dev[2]) * m, m), :]

    def copy(a, k, block, to, own=False):
        return pltpu.make_async_remote_copy(
            src_ref=mine_refs[a] if own else rows(a, block), dst_ref=rows(a, block),
            send_sem=send_sems.at[a, k], recv_sem=recv_sems.at[a, k], device_id=to, device_id_type=MESH)

    def local(a):
        return pltpu.make_async_copy(mine_refs[a], rows(a, me), local_sems.at[a])

    def sends(a):
        return [copy(a, 0, me, sibling, own=True), copy(a, 1, me, (*near, c), own=True),
                copy(a, 2, me, (*far, c), own=True), copy(a, 3, (*near, c), (*far, c)),
                copy(a, 4, (*near, c), sibling), copy(a, 5, (*far, c), sibling), copy(a, 6, (*diag, c), sibling)]

    def arrivals(a):
        return [copy(a, 0, sibling, me), copy(a, 1, (*near, c), me), copy(a, 2, (*far, c), me),
                copy(a, 3, (*diag, c), me), copy(a, 4, (*far, 1 - c), me), copy(a, 5, (*near, 1 - c), me),
                copy(a, 6, (*diag, 1 - c), me)]

    def start():
        for a in range(n):
            local(a).start()
            for cp in sends(a)[0:3]:
                cp.start()

    def finish():
        for first, then in ((1, (3, 4)), (2, (5,)), (3, (6,))):
            for a in range(n):
                arrivals(a)[first].wait_recv()
                for k in then:
                    sends(a)[k].start()
        for a in range(n):
            for k in (0, 4, 5, 6):
                arrivals(a)[k].wait_recv()
        for a in range(n):
            for cp in sends(a):
                cp.wait_send()
            local(a).wait()

    return start, finish


def _scatter_ops(src_refs, land_refs, send_sems, recv_sems, local_sems):
    n = len(src_refs)
    x, y, c = _place()
    my = 4 * x + 2 * y + c

    def peer(k):
        return x ^ (k >> 2), y ^ ((k >> 1) & 1), c ^ (k & 1)

    def piece(a, dev):
        m = src_refs[a].shape[0] // NDEV
        return src_refs[a].at[pl.ds(dev * m, m), :]

    def local(a):
        return pltpu.make_async_copy(piece(a, my), land_refs[a].at[my], local_sems.at[a])

    def send(a, k):
        px, py, pc = peer(k)
        return pltpu.make_async_remote_copy(
            src_ref=piece(a, 4 * px + 2 * py + pc), dst_ref=land_refs[a].at[my],
            send_sem=send_sems.at[a, k - 1], recv_sem=recv_sems.at[a, k - 1],
            device_id=(px, py, pc), device_id_type=MESH)

    def arrival(a, k):
        px, py, pc = peer(k)
        return pltpu.make_async_remote_copy(
            src_ref=piece(a, my), dst_ref=land_refs[a].at[4 * px + 2 * py + pc],
            send_sem=send_sems.at[a, k - 1], recv_sem=recv_sems.at[a, k - 1],
            device_id=(px, py, pc), device_id_type=MESH)

    def start():
        for a in range(n):
            local(a).start()
        for k in range(1, NDEV):
            for a in range(n):
                send(a, k).start()

    def finish():
        for k in range(1, NDEV):
            for a in range(n):
                send(a, k).wait_send()
        for k in range(1, NDEV):
            for a in range(n):
                arrival(a, k).wait_recv()
        for a in range(n):
            local(a).wait()

    return start, finish


def _in_hbm(*arrays):
    return tuple(pltpu.with_memory_space_constraint(a, pltpu.HBM) for a in arrays)


def _comm_sems(n):
    return [pltpu.SemaphoreType.DMA((n, 7)), pltpu.SemaphoreType.DMA((n, 7)), pltpu.SemaphoreType.DMA((n,))]


HBM = pl.BlockSpec(memory_space=pltpu.HBM)


def _sink_rows(sinks):
    return jnp.repeat(sinks.reshape(NKV, GROUP), BLK, axis=1)


def _band_softmax(s2_ref, ls, prev_offset, sink_row):
    jj = lax.broadcasted_iota(jnp.int32, (BLK, BLK), 0)
    ii = lax.broadcasted_iota(jnp.int32, (BLK, BLK), 1)
    from_prev = jj > ii
    sc = jnp.where(from_prev, s2_ref[0:BLK, ls] + prev_offset, s2_ref[BLK:2 * BLK, ls])
    m = jnp.maximum(jnp.max(sc, axis=0, keepdims=True), sink_row)
    p = jnp.exp(sc - m)
    es = jnp.exp(sink_row - m)
    inv = 1.0 / (jnp.sum(p, axis=0, keepdims=True) + es)
    return from_prev, p * inv, es * inv


def _put_split(dst_ref, ls, t, from_prev):
    t = t.astype(bf16)
    zero = jnp.zeros_like(t)
    dst_ref[0:BLK, ls] = jnp.where(from_prev, t, zero)
    dst_ref[BLK:2 * BLK, ls] = jnp.where(from_prev, zero, t)


def _heads_side_by_side(ref, h):
    return jnp.concatenate([ref[HD * (GROUP * h + g):HD * (GROUP * h + g) + HD, :] for g in range(GROUP)], axis=1)


def _kv_specs_t():
    prev = pl.BlockSpec((KVW, BLK), lambda n: (0, jnp.maximum(n - 1, 0)))
    cur = pl.BlockSpec((KVW, BLK), lambda n: (0, n))
    return [prev, cur, prev, cur]


def _attn_fwd_t(qt, kt, vt, sinks):
    s = qt.shape[1]

    def body(sink_ref, q_ref, kp_ref, kc_ref, vp_ref, vc_ref, o_ref, s2_scr, pn2_scr):
        n = pl.program_id(0)
        off = jnp.where(n > 0, 0.0, NEG)

        def scores(h):
            hs = slice(HD * h, HD * h + HD)
            kh = jnp.concatenate([kp_ref[hs, :], kc_ref[hs, :]], axis=1)
            s2_scr[h % 2] = _dot_tn(kh, _heads_side_by_side(q_ref, h))

        def probs(h):
            for g in range(GROUP):
                ls = slice(BLK * g, BLK * g + BLK)
                from_prev, pn, _ = _band_softmax(s2_scr.at[h % 2], ls, off, sink_ref[h:h + 1, ls])
                _put_split(pn2_scr.at[h % 2], ls, pn, from_prev)

        def outputs(h):
            hs = slice(HD * h, HD * h + HD)
            vh = jnp.concatenate([vp_ref[hs, :], vc_ref[hs, :]], axis=1)
            og = _dot(vh, pn2_scr[h % 2])
            for g in range(GROUP):
                a = GROUP * h + g
                o_ref[HD * a:HD * a + HD, :] = og[:, BLK * g:BLK * g + BLK]

        scores(0)
        for h in range(NKV):
            if h + 1 < NKV:
                scores(h + 1)
            probs(h)
            outputs(h)

    return pl.pallas_call(
        body, name="attn_fwd", grid=(s // BLK,),
        in_specs=[pl.BlockSpec((NKV, GROUP * BLK), lambda n: (0, 0)), pl.BlockSpec((D, BLK), lambda n: (0, n))]
        + _kv_specs_t(),
        out_specs=pl.BlockSpec((D, BLK), lambda n: (0, n)),
        out_shape=pltpu.HBM((D, s), f32),
        scratch_shapes=[pltpu.VMEM((2, 2 * BLK, GROUP * BLK), f32), pltpu.VMEM((2, 2 * BLK, GROUP * BLK), bf16)],
        compiler_params=_params(("arbitrary",), 32),
    )(_sink_rows(sinks), *_in_hbm(qt, kt, kt, vt, vt))


def _attn_bwd_t(qt, kt, vt, dot, sinks, dwo):
    s = qt.shape[1]
    nb = s // BLK

    def body(sink_ref, q_ref, do_ref, kp_ref, kc_ref, vp_ref, vc_ref, dwo_ref, dq_ref, dk_ref, dv_ref, ds_ref,
             land_ref, dk_hold, dv_hold, s2_scr, dp2_scr, pn2_scr, ds2_scr, send_sems, recv_sems, local_sems):
        n = pl.program_id(0)
        start, finish = _scatter_ops([dwo_ref], [land_ref], send_sems, recv_sems, local_sems)

        @pl.when(n == 0)
        def _():
            start()
            dk_hold[...] = jnp.zeros_like(dk_hold)
            dv_hold[...] = jnp.zeros_like(dv_hold)
            ds_ref[...] = jnp.zeros_like(ds_ref)

        @pl.when(n < nb)
        def _():
            off = jnp.where(n > 0, 0.0, NEG)

            def scores(h):
                hs = slice(HD * h, HD * h + HD)
                kh = jnp.concatenate([kp_ref[hs, :], kc_ref[hs, :]], axis=1)
                vh = jnp.concatenate([vp_ref[hs, :], vc_ref[hs, :]], axis=1)
                s2_scr[h % 2] = _dot_tn(kh, _heads_side_by_side(q_ref, h))
                dp2_scr[h % 2] = _dot_tn(vh, _heads_side_by_side(do_ref, h))

            def softmax_bwd(h):
                for g in range(GROUP):
                    ls = slice(BLK * g, BLK * g + BLK)
                    from_prev, pn, ps = _band_softmax(s2_scr.at[h % 2], ls, off, sink_ref[h:h + 1, ls])
                    dp = jnp.where(from_prev, dp2_scr[h % 2, 0:BLK, ls], dp2_scr[h % 2, BLK:2 * BLK, ls])
                    dsum = jnp.sum(pn * dp, axis=0, keepdims=True)
                    ds_ref[h:h + 1, ls] += -ps * dsum
                    _put_split(pn2_scr.at[h % 2], ls, pn, from_prev)
                    _put_split(ds2_scr.at[h % 2], ls, pn * (dp - dsum), from_prev)

            def grads(h):
                hs = slice(HD * h, HD * h + HD)
                kh = jnp.concatenate([kp_ref[hs, :], kc_ref[hs, :]], axis=1)
                dqg = _dot(kh, ds2_scr[h % 2])
                for g in range(GROUP):
                    a = GROUP * h + g
                    dq_ref[HD * a:HD * a + HD, :] = dqg[:, BLK * g:BLK * g + BLK]
                dkh = _dot_nt(_heads_side_by_side(q_ref, h), ds2_scr[h % 2])
                dvh = _dot_nt(_heads_side_by_side(do_ref, h), pn2_scr[h % 2])
                dk_ref[hs, :] = dk_hold[hs, :] + dkh[:, 0:BLK]
                dv_ref[hs, :] = dv_hold[hs, :] + dvh[:, 0:BLK]
                dk_hold[hs, :] = dkh[:, BLK:2 * BLK]
                dv_hold[hs, :] = dvh[:, BLK:2 * BLK]

            scores(0)
            for h in range(NKV):
                if h + 1 < NKV:
                    scores(h + 1)
                softmax_bwd(h)
                grads(h)

        @pl.when(n == nb)
        def _():
            dk_ref[...] = dk_hold[...]
            dv_ref[...] = dv_hold[...]
            finish()

    blk = pl.BlockSpec((D, BLK), lambda n: (0, jnp.minimum(n, nb - 1)))
    late = pl.BlockSpec((KVW, BLK), lambda n: (0, jnp.maximum(n - 1, 0)))
    whole = pl.BlockSpec((NKV, GROUP * BLK), lambda n: (0, 0))
    kv = [pl.BlockSpec((KVW, BLK), lambda n: (0, jnp.clip(n - 1, 0, nb - 1))),
          pl.BlockSpec((KVW, BLK), lambda n: (0, jnp.minimum(n, nb - 1)))]
    return pl.pallas_call(
        body, name="attn_bwd", grid=(nb + 1,),
        in_specs=[whole, blk, blk] + kv + kv + [HBM],
        out_specs=[blk, late, late, whole, HBM],
        out_shape=[pltpu.HBM((D, s), f32), pltpu.HBM((KVW, s), f32), pltpu.HBM((KVW, s), f32),
                   jax.ShapeDtypeStruct((NKV, GROUP * BLK), f32), pltpu.HBM((NDEV, WO_ROWS, D), bf16)],
        scratch_shapes=[pltpu.VMEM((KVW, BLK), f32), pltpu.VMEM((KVW, BLK), f32)]
        + [pltpu.VMEM((2, 2 * BLK, GROUP * BLK), f32)] * 2 + [pltpu.VMEM((2, 2 * BLK, GROUP * BLK), bf16)] * 2
        + _comm_sems(1),
        compiler_params=_params(("arbitrary",), 48),
    )(_sink_rows(sinks), *_in_hbm(qt, dot, kt, kt, vt, vt, dwo))


def _decay_terms(r, sp):
    a = jnp.exp(r * (-LRU_C * sp))
    n = r * (2.0 * LRU_C * sp)
    y = jnp.where(n < 0.02, n * (1.0 - n * (0.5 - n * (1.0 / 6.0))), 1.0 - a * a)
    inv_mult = lax.rsqrt(jnp.maximum(y, 1e-30))
    return a, y * inv_mult, inv_mult


def _later(x, before, k):
    if k == 0:
        return x
    row = lax.broadcasted_iota(jnp.int32, before.shape, 0)
    rolled = pltpu.roll(x, k, 0)
    first = jnp.where(row < k, pltpu.roll(before, k, 0), rolled[0:8])
    return jnp.concatenate([first, rolled[8:]], axis=0)


def _earlier(x, after, k):
    if k == 0:
        return x
    n = x.shape[0]
    row = lax.broadcasted_iota(jnp.int32, after.shape, 0)
    rolled = pltpu.roll(x, n - k, 0)
    last = jnp.where(row >= 8 - k, pltpu.roll(after, 8 - k, 0), rolled[n - 8:n])
    return jnp.concatenate([rolled[0:n - 8], last], axis=0)


def _fwd_fused(h, wt, tabs, wo_shard, conv_w, conv_b, wr, wi, br, bi, lam, tm):
    s = h.shape[0]
    nt = s // tm
    nc = 512
    pieces = 8
    rows_per = tm // pieces
    later_chunks = (0, 1, 2, 3, 4, 7, 8)

    def body(h_ref, wt_ref, tab_ref, wo_ref, cw_ref, cb_ref, wr_ref, wi_ref, br_ref,
             bi_ref, lam_ref, q_ref, k_ref, v_ref, ga_ref, xl_ref, gl_ref, u_ref, hl_ref, r_ref, ig_ref,
             wo_all, wo_stage, halo, ub_scr, pr_scr, pi_scr, b_scr, a_scr, hcar,
             send_sems, recv_sems, local_sems):
        i = pl.program_id(0)
        start, pass_on, finish = _gather_ops([wo_stage], [wo_all], send_sems, recv_sems, local_sems)

        @pl.when(i == 0)
        def _():
            wo_stage[...] = wo_ref[...].astype(bf16)
            start()
            halo[...] = jnp.zeros_like(halo)
            hcar[...] = jnp.zeros_like(hcar)

        sp = _softplus(-lam_ref[...])
        br, bi = br_ref[...], bi_ref[...]
        c, sa, sb = _tables(tab_ref)
        piece_rows = lambda p: slice(rows_per * p, rows_per * p + rows_per)

        def project(ci):
            z = _dot_nt(h_ref[...], wt_ref[ci * nc:(ci + 1) * nc, :])
            if ci < 2:
                for j in range(nc // 128):
                    r = _rope(z[:, 128 * j:128 * j + 128], c, sa, sb) * (HD ** -0.5)
                    q_ref[ci * nc + 128 * j:ci * nc + 128 * j + 128, :] = r.astype(bf16).T
            elif ci == 2:
                for j in range(2):
                    js = slice(128 * j, 128 * j + 128)
                    k_ref[js, :] = _rope(z[:, js], c, sa, sb).astype(bf16).T
                    v_ref[js, :] = z[:, KVW + 128 * j:KVW + 128 * j + 128].astype(bf16).T
            else:
                sec, j = divmod(ci - 3, 2)
                (ga_ref, xl_ref, gl_ref)[sec][:, j * nc:(j + 1) * nc] = z

        def gate_terms(p):
            rows = piece_rows(p)
            r = _sigmoid(pr_scr[rows, :] + br)
            ig = _sigmoid(pi_scr[rows, :] + bi)
            a, mult, _ = _decay_terms(r, sp)
            r_ref[rows, :] = r
            ig_ref[rows, :] = ig
            a_scr[rows, :] = a
            b_scr[rows, :] = mult * (ig * u_ref[rows, :])

        def scan(p, hc):
            for t in range(rows_per * p, rows_per * p + rows_per):
                hc = a_scr[t:t + 1, :] * hc + b_scr[t:t + 1, :]
                hl_ref[t:t + 1, :] = hc
            return hc

        project(5)
        project(6)
        xl = xl_ref[...]
        u = cb_ref[...] + sum(cw_ref[k:k + 1, :] * _later(xl, halo[...], CONVW - 1 - k) for k in range(CONVW))
        halo[...] = xl[tm - 8:tm, :]
        u_ref[...] = u
        ub_scr[...] = u.astype(bf16)
        for g in range(NGRP):
            gs = slice(256 * g, 256 * g + 256)
            pr_scr[:, gs] = _dot(ub_scr[:, gs], wr_ref[g])
            pi_scr[:, gs] = _dot(ub_scr[:, gs], wi_ref[g])
        hc = hcar[...]
        gate_terms(0)
        for slot, ci in enumerate(later_chunks):
            project(ci)
            gate_terms(slot + 1)
            hc = scan(slot, hc)
        hcar[...] = scan(pieces - 1, hc)

        @pl.when(i == max(nt - 2, 0))
        def _():
            pass_on()

        @pl.when(i == nt - 1)
        def _():
            finish()

    row = lambda w: pl.BlockSpec((tm, w), lambda i: (i, 0))
    col = lambda w: pl.BlockSpec((w, tm), lambda i: (0, i))
    full = lambda a: pl.BlockSpec(a.shape, lambda i: (0,) * a.ndim)
    big = lambda w, dt: pltpu.HBM((s, w), dt)
    tile = pltpu.VMEM((tm, LW), f32)
    return pl.pallas_call(
        body, name="fwd_fused", grid=(nt,),
        in_specs=[row(D), full(wt), row(384), full(wo_shard), full(conv_w), full(conv_b),
                  full(wr), full(wi), full(br), full(bi), full(lam)],
        out_specs=[col(D), col(KVW), col(KVW), row(D), row(D), row(D)] + [row(LW)] * 4 + [HBM],
        out_shape=[pltpu.HBM((D, s), bf16), pltpu.HBM((KVW, s), bf16), pltpu.HBM((KVW, s), bf16),
                   big(D, f32), big(D, f32), big(D, f32)] + [big(LW, f32)] * 4 + [pltpu.HBM((2 * D, D), bf16)],
        scratch_shapes=[pltpu.VMEM((WO_ROWS, D), bf16), pltpu.VMEM((8, LW), f32), pltpu.VMEM((tm, LW), bf16)]
        + [tile] * 4 + [pltpu.VMEM((1, LW), f32)] + _comm_sems(1),
        compiler_params=_params(("arbitrary",), 56),
    )(*_in_hbm(h, wt), tabs, wo_shard, conv_w, conv_b, wr, wi, br, bi, lam)


def _lru_bwd(u, hl, dhl, xl, r, ig, conv_w, wr, wi, lam, tm):
    s = u.shape[0]
    nt = s // tm
    pieces = 8
    rows_per = tm // pieces

    def body(u_ref, h_ref, hp_ref, dh_ref, x_ref, r_ref, ig_ref, cw_ref, wr_ref, wi_ref,
             lam_ref, dxl_ref, dwr_ref, dwi_ref, dbr_ref, dbi_ref, dlam_ref, dcb_ref, dcw_ref,
             l_scr, du_scr, a_scr, mu_scr, im_scr, dpr_scr, dpi_scr, lcar, dunext):
        t0 = pl.program_id(0)
        tile = nt - 1 - t0

        @pl.when(t0 == 0)
        def _():
            lcar[...] = jnp.zeros_like(lcar)
            dunext[...] = jnp.zeros_like(dunext)
            for ref in (dwr_ref, dwi_ref, dbr_ref, dbi_ref, dlam_ref, dcb_ref, dcw_ref):
                ref[...] = jnp.zeros_like(ref)

        lam = lam_ref[...]
        sp = _softplus(-lam)
        hp = jnp.where(tile > 0, hp_ref[...], 0.0)

        def decay(p):
            rows = slice(rows_per * p, rows_per * p + rows_per)
            a_scr[rows, :], mu_scr[rows, :], im_scr[rows, :] = _decay_terms(r_ref[rows, :], sp)

        def scan(p, c):
            for t in range(rows_per * p + rows_per - 1, rows_per * p - 1, -1):
                lt = dh_ref[t:t + 1, :] + c
                l_scr[t:t + 1, :] = lt
                c = a_scr[t:t + 1, :] * lt
            return c

        def terms(p, sums):
            rows = slice(rows_per * p, rows_per * p + rows_per)
            lt, u, r, i, a, mult, inv_mult = l_scr[rows, :], u_ref[rows, :], r_ref[rows, :], ig_ref[rows, :], \
                a_scr[rows, :], mu_scr[rows, :], im_scr[rows, :]
            before = hp if p == 0 else h_ref[rows_per * p - 8:rows_per * p, :]
            hprev = _later(h_ref[rows, :], before, 1)
            iu = i * u
            lm = lt * mult
            du_scr[rows, :] = lm * i
            dla = (lt * hprev) * a - ((lt * iu) * (a * a)) * inv_mult
            dlar = dla * r
            dpr = (dlar * (1.0 - r)) * (-LRU_C * sp)
            dpi = (lm * iu) * (1.0 - i)
            dpr_scr[rows, :] = dpr.astype(bf16)
            dpi_scr[rows, :] = dpi.astype(bf16)
            col = lambda t: jnp.sum(t, axis=0, keepdims=True)
            return sums[0] + col(dlar), sums[1] + col(dpr), sums[2] + col(dpi)

        sums = (jnp.zeros((1, LW), f32),) * 3
        decay(pieces - 1)
        c = scan(pieces - 1, lcar[...])
        for p in range(pieces - 1, -1, -1):
            if p > 0:
                decay(p - 1)
                c = scan(p - 1, c)
            sums = terms(p, sums)
        lcar[...] = c
        dlam_ref[...] += sums[0] * (-LRU_C)
        dbr_ref[...] += sums[1]
        dbi_ref[...] += sums[2]

        ub = u_ref[...].astype(bf16)
        dug = []
        for g in range(NGRP):
            gs = slice(256 * g, 256 * g + 256)
            dwr_ref[g] += _dot_tn(ub[:, gs], dpr_scr[:, gs])
            dwi_ref[g] += _dot_tn(ub[:, gs], dpi_scr[:, gs])
            dug.append(_dot_nt(dpr_scr[:, gs], wr_ref[g]) + _dot_nt(dpi_scr[:, gs], wi_ref[g]))
        du = du_scr[...] + jnp.concatenate(dug, axis=1)

        dcb_ref[...] += jnp.sum(du, axis=0, keepdims=True)
        x = x_ref[...]
        after = dunext[...]
        dxl = jnp.zeros_like(du)
        for k in range(CONVW):
            e = _earlier(du, after, CONVW - 1 - k)
            dxl = dxl + cw_ref[k:k + 1, :] * e
            dcw_ref[k:k + 1, :] += jnp.sum(e * x, axis=0, keepdims=True)
        dxl_ref[...] = dxl.astype(bf16)
        dunext[...] = du[0:8, :]

        @pl.when(t0 == nt - 1)
        def _():
            dlam_ref[...] = dlam_ref[...] * (-_sigmoid(-lam))

    rev = lambda i: (nt - 1 - i, 0)
    row = pl.BlockSpec((tm, LW), rev)
    prev8 = pl.BlockSpec((8, LW), lambda i: (jnp.maximum((nt - 1 - i) * (tm // 8) - 1, 0), 0))
    full = lambda a: pl.BlockSpec(a.shape, lambda i: (0,) * a.ndim)
    vec = pl.BlockSpec((1, LW), lambda i: (0, 0))
    bd = pl.BlockSpec((NGRP, 256, 256), lambda i: (0, 0, 0))
    return pl.pallas_call(
        body, name="lru_bwd", grid=(nt,),
        in_specs=[row, row, prev8] + [row] * 4 + [full(conv_w), full(wr), full(wi), full(lam)],
        out_specs=[row, bd, bd, vec, vec, vec, vec, pl.BlockSpec((CONVW, LW), lambda i: (0, 0))],
        out_shape=[pltpu.HBM((s, LW), bf16),
                   jax.ShapeDtypeStruct((NGRP, 256, 256), f32), jax.ShapeDtypeStruct((NGRP, 256, 256), f32),
                   jax.ShapeDtypeStruct((1, LW), f32), jax.ShapeDtypeStruct((1, LW), f32),
                   jax.ShapeDtypeStruct((1, LW), f32), jax.ShapeDtypeStruct((1, LW), f32),
                   jax.ShapeDtypeStruct((CONVW, LW), f32)],
        scratch_shapes=[pltpu.VMEM((tm, LW), f32)] * 5 + [pltpu.VMEM((tm, LW), bf16)] * 2
        + [pltpu.VMEM((1, LW), f32), pltpu.VMEM((8, LW), f32)],
        compiler_params=_params(("arbitrary",), 56),
    )(*_in_hbm(u, hl, hl, dhl, xl, r, ig), conv_w, wr, wi, lam)


def _gated_norm(t, gate, gain):
    sg = _sigmoid(gate)
    silu = gate * sg
    p = t * silu
    rstd = lax.rsqrt(jnp.mean(p * p, axis=-1, keepdims=True) + EPS)
    ph = p * rstd
    return sg, silu, rstd, ph, ph * gain


def _gated_norm_bwd(dy, t, gate, gain, sg, silu, rstd, ph):
    w = dy * gain
    dp = rstd * (w - ph * jnp.mean(w * ph, axis=-1, keepdims=True))
    dgate = (dp * t) * (sg + silu * (1.0 - sg))
    return jnp.sum(dy * ph, axis=0, keepdims=True), dp * silu, dgate


def _out_fwd_bwd(x, tgt, o, ga, hl, gl, again, lgain, fgain, wo, tm):
    s = x.shape[0]
    nt = s // tm

    def body(x_ref, t_ref, o_ref, ga_ref, hl_ref, gl_ref, ag_ref, lg_ref, fg_ref, wo_ref,
             dx2_ref, do_ref, dga_ref, dhl_ref, dgl_ref, dwo_ref, gfg_ref, gag_ref, glg_ref, loss_ref, acc):
        i = pl.program_id(0)

        @pl.when(i == 0)
        def _():
            acc[...] = jnp.zeros_like(acc)
            for ref in (gfg_ref, gag_ref, glg_ref, loss_ref):
                ref[...] = jnp.zeros_like(ref)

        oo = jnp.concatenate([o_ref[128 * j:128 * j + 128, :].T for j in range(D // 128)], axis=1)
        gga, hh, ggl = ga_ref[...], hl_ref[...], gl_ref[...]
        ag, lg, fg = ag_ref[...], lg_ref[...], fg_ref[...]
        sga, silua, ra, pah, ya = _gated_norm(oo, gga, ag)
        sgl, silul, rl, plh, yl = _gated_norm(hh, ggl, lg)
        yab, ylb = ya.astype(bf16), yl.astype(bf16)
        y = _dot(yab, wo_ref[0:D, :]) + _dot(ylb, wo_ref[D:2 * D, :])
        x2 = x_ref[...] + y
        r2 = lax.rsqrt(jnp.mean(x2 * x2, axis=-1, keepdims=True) + EPS)
        x2h = x2 * r2
        err = x2h * fg - t_ref[...]
        loss_ref[...] += 0.5 * jnp.sum(jnp.sum(err * err, axis=-1, keepdims=True) * (1.0 / D))
        gfg_ref[...] += jnp.sum(err * x2h, axis=0, keepdims=True) * (1.0 / D)
        w = err * (fg * (1.0 / D))
        dx2 = r2 * (w - x2h * jnp.mean(w * x2h, axis=-1, keepdims=True))
        dx2_ref[...] = dx2
        dyb = dx2.astype(bf16)
        acc[0:D, :] += _dot_tn(yab, dyb)
        acc[D:2 * D, :] += _dot_tn(ylb, dyb)
        dya = _dot_nt(dyb, wo_ref[0:D, :])
        dyl = _dot_nt(dyb, wo_ref[D:2 * D, :])
        gag, do, dga = _gated_norm_bwd(dya, oo, gga, ag, sga, silua, ra, pah)
        glg, dhl, dgl = _gated_norm_bwd(dyl, hh, ggl, lg, sgl, silul, rl, plh)
        gag_ref[...] += gag
        glg_ref[...] += glg
        dob = do.astype(bf16)
        for j in range(D // 128):
            do_ref[128 * j:128 * j + 128, :] = dob[:, 128 * j:128 * j + 128].T
        dga_ref[...] = dga.astype(bf16)
        dhl_ref[...] = dhl
        dgl_ref[...] = dgl.astype(bf16)

        @pl.when(i == nt - 1)
        def _():
            dwo_ref[...] = acc[...].astype(bf16)

    row = pl.BlockSpec((tm, D), lambda i: (i, 0))
    col = pl.BlockSpec((D, tm), lambda i: (0, i))
    vec = pl.BlockSpec((1, D), lambda i: (0, 0))
    mat = pl.BlockSpec((2 * D, D), lambda i: (0, 0))
    return pl.pallas_call(
        body, name="out_fwd_bwd", grid=(nt,),
        in_specs=[row, row, col, row, row, row] + [vec] * 3 + [mat],
        out_specs=[row, col, row, row, row] + [mat, vec, vec, vec, pl.BlockSpec((1, 128), lambda i: (0, 0))],
        out_shape=[pltpu.HBM((s, D), f32), pltpu.HBM((D, s), bf16),
                   pltpu.HBM((s, D), bf16), pltpu.HBM((s, D), f32),
                   pltpu.HBM((s, D), bf16), pltpu.HBM((2 * D, D), bf16),
                   jax.ShapeDtypeStruct((1, D), f32), jax.ShapeDtypeStruct((1, D), f32),
                   jax.ShapeDtypeStruct((1, D), f32), jax.ShapeDtypeStruct((1, 128), f32)],
        scratch_shapes=[pltpu.VMEM((2 * D, D), f32)],
        compiler_params=_params(("arbitrary",), 56),
    )(*_in_hbm(x, tgt, o, ga, hl, gl), again, lgain, fgain, *_in_hbm(wo))


def _bwd_in(x, dx2, dq, dk, dv, dga, dxl, dgl, ln_gain, wt, tabs, tm):
    s = x.shape[0]

    def body(x_ref, dx2_ref, dq_ref, dk_ref, dv_ref, dga_ref, dxl_ref, dgl_ref, g_ref, wt_ref,
             tab_ref, gx_ref, gln_ref, dzt_ref):
        @pl.when(pl.program_id(0) == 0)
        def _():
            gln_ref[...] = jnp.zeros_like(gln_ref)

        c, sa, sb = (t.T for t in _tables(tab_ref))
        for j in range(D // 128):
            js = slice(128 * j, 128 * j + 128)
            dzt_ref[js, :] = (_unrope_t(dq_ref[js, :], c, sa, sb) * (HD ** -0.5)).astype(bf16)
        for j in range(KVW // 128):
            js = slice(128 * j, 128 * j + 128)
            dzt_ref[D + 128 * j:D + 128 * j + 128, :] = _unrope_t(dk_ref[js, :], c, sa, sb).astype(bf16)
        dzt_ref[D + KVW:D + 2 * KVW, :] = dv_ref[...].astype(bf16)
        first = D + 2 * KVW
        dh = _dot_tn(dzt_ref[0:512, :], wt_ref[0:512, :])
        for ci in range(1, first // 512):
            dh = dh + _dot_tn(dzt_ref[512 * ci:512 * ci + 512, :], wt_ref[512 * ci:512 * ci + 512, :])
        for sec, ref in enumerate((dga_ref, dxl_ref, dgl_ref)):
            for j in range(D // 512):
                rows = slice(first + D * sec + 512 * j, first + D * sec + 512 * j + 512)
                dh = dh + _dot(ref[:, 512 * j:512 * j + 512], wt_ref[rows, :])
            for j in range(D // 128):
                dzt_ref[first + D * sec + 128 * j:first + D * sec + 128 * j + 128, :] = ref[:, 128 * j:128 * j + 128].T
        xx = x_ref[...]
        rstd = lax.rsqrt(jnp.mean(xx * xx, axis=-1, keepdims=True) + EPS)
        xh = xx * rstd
        gln_ref[...] += jnp.sum(dh * xh, axis=0, keepdims=True)
        w = dh * g_ref[...]
        gx_ref[...] = dx2_ref[...] + rstd * (w - xh * jnp.mean(w * xh, axis=-1, keepdims=True))

    row = lambda w: pl.BlockSpec((tm, w), lambda i: (i, 0))
    col = lambda w: pl.BlockSpec((w, tm), lambda i: (0, i))
    full = lambda a: pl.BlockSpec(a.shape, lambda i: (0, 0))
    return pl.pallas_call(
        body, name="bwd_in", grid=(s // tm,),
        in_specs=[row(D), row(D), col(D), col(KVW), col(KVW), row(D), row(D), row(D), full(ln_gain), full(wt),
                  row(384)],
        out_specs=[row(D), pl.BlockSpec((1, D), lambda i: (0, 0)), col(NIN)],
        out_shape=[pltpu.HBM((s, D), f32), jax.ShapeDtypeStruct((1, D), f32),
                   pltpu.HBM((NIN, s), bf16)],
        compiler_params=_params(("arbitrary",), 56),
    )(*_in_hbm(x, dx2, dq, dk, dv, dga, dxl, dgl), ln_gain, *_in_hbm(wt), tabs)


WT_TERMS = 4


def _dwt_scatter(dzt, h, small, tm):
    s = h.shape[0]
    nk = s // tm
    srows = small.shape[0] // NDEV
    last = NDEV - 1
    sm_turn = 2

    def body(order_ref, dz_ref, h_ref, sm_ref, lwt_ref, rep_all, tail_ref, acc, stage, given, relayed, lsm, rep_stage,
             send_sems, recv_sems, local_sem, sm_send, sm_recv, sm_local, rep_send, rep_recv, rep_local):
        j, k = pl.program_id(0), pl.program_id(1)
        x, y, c = _place()
        sibling = (x, y, 1 - c)
        near = (x ^ (1 - c), y ^ c)
        far = (x ^ c, y ^ (1 - c))
        sm_start, sm_finish = _scatter_ops([sm_ref], [lsm], sm_send, sm_recv, sm_local)
        rep_start, rep_pass_on, rep_finish = _gather_ops([rep_stage], [rep_all], rep_send, rep_recv, rep_local)

        def send(step):
            if step == last - 1:
                dst, to = lwt_ref.at[1], sibling
            elif step % 2 == 0:
                dst, to = given.at[step // 2], sibling
            elif step == 1:
                dst, to = relayed, (*near, c)
            else:
                dst, to = lwt_ref.at[1 + step // 2], (*(near if step == 3 else far), c)
            return pltpu.make_async_remote_copy(
                src_ref=stage.at[step % 2], dst_ref=dst, send_sem=send_sems.at[step], recv_sem=recv_sems.at[step],
                device_id=to, device_id_type=MESH)

        def keep():
            return pltpu.make_async_copy(stage.at[last % 2], lwt_ref.at[0], local_sem)

        @pl.when((j == 0) & (k == 0))
        def _():
            sm_start()

        @pl.when(k == 0)
        def _():
            acc[...] = jnp.zeros_like(acc)

        acc[...] += _dot(dz_ref[...], h_ref[...])

        for step in range(NDEV):
            @pl.when((k == nk - 1) & (j == step))
            def _(step=step):
                if step >= 2:
                    send(step - 2).wait_send()
                if step % 2 == 1 and step < last:
                    send(step - 1).wait_recv()
                    total = acc[...] + given[step // 2].astype(f32)
                    if step == 5:
                        send(1).wait_recv()
                        total = total + relayed[...].astype(f32)
                    stage[step % 2] = total.astype(bf16)
                else:
                    stage[step % 2] = acc[...].astype(bf16)
                if step < last:
                    send(step).start()
                else:
                    keep().start()
                    send(last - 1).wait_send()
                    for peer_step in (3, 5, last - 1):
                        send(peer_step).wait_recv()
                    keep().wait()
                    rep_finish()
                if step == sm_turn:
                    sm_finish()
                    total_sm = lsm[0]
                    for dev in range(1, NDEV):
                        total_sm = total_sm + lsm[dev]
                    rep_stage[...] = total_sm[0:SMALL_PER]
                    tail_ref[...] = total_sm[SMALL_PER:]
                    rep_start()
                if step == last - 1:
                    rep_pass_on()

    x, y, c = _place()
    dest = lambda chip, cc: 4 * chip[0] + 2 * chip[1] + cc
    near, far, diag = (x ^ (1 - c), y ^ c), (x ^ c, y ^ (1 - c)), (1 - x, 1 - y)
    order = jnp.stack([dest(diag, 1 - c), dest(diag, c), dest(far, 1 - c), dest(near, c),
                       dest(near, 1 - c), dest(far, c), dest((x, y), 1 - c), dest((x, y), c)])
    return pl.pallas_call(
        body, name="dwt_scatter",
        grid_spec=pltpu.PrefetchScalarGridSpec(
            num_scalar_prefetch=1, grid=(NDEV, nk),
            in_specs=[pl.BlockSpec((WT_ROWS, tm), lambda j, k, order: (order[j], k)),
                      pl.BlockSpec((tm, D), lambda j, k, order: (k, 0)), HBM],
            out_specs=[HBM, HBM, pl.BlockSpec((srows - SMALL_PER, D), lambda j, k, order: (0, 0))],
            scratch_shapes=[pltpu.VMEM((WT_ROWS, D), f32), pltpu.VMEM((2, WT_ROWS, D), bf16),
                            pltpu.VMEM((3, WT_ROWS, D), bf16), pltpu.VMEM((WT_ROWS, D), bf16),
                            pltpu.VMEM((NDEV, srows, D), f32), pltpu.VMEM((SMALL_PER, D), f32),
                            pltpu.SemaphoreType.DMA((last,)), pltpu.SemaphoreType.DMA((last,)),
                            pltpu.SemaphoreType.DMA(())] + _comm_sems(1) + _comm_sems(1)),
        out_shape=[pltpu.HBM((WT_TERMS, WT_ROWS, D), bf16), pltpu.HBM((SMALL_ROWS, D), f32),
                   jax.ShapeDtypeStruct((srows - SMALL_PER, D), f32)],
        compiler_params=_params(("arbitrary", "arbitrary"), 48),
    )(order, *_in_hbm(dzt, h, small))


def _diag_blocks(bd):
    eye = jnp.eye(4, dtype=bd.dtype)
    return jnp.einsum('gjckd,jk->gjcd', bd.reshape(NGRP, 4, HD, 4, HD), eye).reshape(NQ, HD, HD)


def _sequence_step(x, h, tgt, wt, wo_shard, conv_w, wr, wi, p):
    s = x.shape[0]
    tm = min(256, s)
    tabs = _rope_tables(s)
    sinks = p["sinks"].reshape(NQ)
    qt, kt, vt, ga, xl, gl, u, hl, r, ig, wo = _fwd_fused(
        h, wt, tabs, wo_shard, conv_w, p["conv_b"], wr, wi, p["b_rgate"], p["b_igate"], p["lru_lambda"], tm)
    ot = _attn_fwd_t(qt, kt, vt, sinks)
    dx2, dot, dga, dhl, dgl, dwo, g_fg, g_ag, g_lg, loss = _out_fwd_bwd(
        x, tgt, ot, ga, hl, gl, p["attn_out_gain"], p["lru_out_gain"], p["final_gain"], wo, tm)
    dqt, dkt, dvt, dsink, land_wo = _attn_bwd_t(qt, kt, vt, dot, sinks, dwo)
    dxl, dwr, dwi, dbr, dbi, dlam, dcb, dcw = _lru_bwd(u, hl, dhl, xl, r, ig, conv_w, wr, wi, p["lru_lambda"], tm)
    gx, g_ln, dzt = _bwd_in(x, dx2, dqt, dkt, dvt, dga, dxl, dgl, p["ln_gain"], wt, tabs, tm)
    small = dict(ln_gain=g_ln, sinks=dsink.reshape(NQ, BLK).sum(axis=1)[None], conv_w=dcw, conv_b=dcb,
                 w_rgate=_diag_blocks(dwr), b_rgate=dbr, w_igate=_diag_blocks(dwi), b_igate=dbi, lru_lambda=dlam,
                 attn_out_gain=g_ag, lru_out_gain=g_lg, final_gain=g_fg)
    land_wt, g_rep, g_tail = _dwt_scatter(dzt, h, _pack_small(small, loss), min(2048, s))
    return gx, land_wt, land_wo, g_rep, g_tail


def _gather_weights(wt_shard, conv_blk, x, ln_gain, w_rgate, w_igate, tm):
    s = x.shape[0]

    def body(wt_ref, cw_ref, g_ref, wrg_ref, wig_ref, x_ref, wt_all, cw_all, h_ref, wr_ref, wi_ref,
             stage, xbuf, hbuf, send_sems, recv_sems, local_sems):
        stage[...] = wt_ref[...].astype(bf16)
        start, finish = _relay_gather_ops([stage, cw_ref], [wt_all, cw_all], send_sems, recv_sems, local_sems)
        start()
        for src, dst in ((wrg_ref, wr_ref), (wig_ref, wi_ref)):
            dst[...] = jnp.zeros_like(dst)
            for nb in range(NQ):
                g, j = divmod(nb, 4)
                dst[g, HD * j:HD * j + HD, HD * j:HD * j + HD] = src[nb].astype(bf16)
        gain = g_ref[...]
        for i in range(s // tm):
            rows = pl.ds(i * tm, tm)
            pltpu.sync_copy(x_ref.at[rows, :], xbuf)
            xx = xbuf[...]
            rstd = lax.rsqrt(jnp.mean(xx * xx, axis=-1, keepdims=True) + EPS)
            hbuf[...] = (xx * rstd * gain).astype(bf16)
            pltpu.sync_copy(hbuf, h_ref.at[rows, :])
        finish()

    vmem = pl.BlockSpec(memory_space=pltpu.VMEM)
    return pl.pallas_call(
        body, name="gather_weights",
        in_specs=[vmem] * 5 + [HBM], out_specs=[HBM, HBM, HBM, vmem, vmem],
        out_shape=[pltpu.HBM((NIN, D), bf16), pltpu.HBM((NDEV * 8, 128), f32), pltpu.HBM((s, D), bf16)]
        + [jax.ShapeDtypeStruct((NGRP, 256, 256), bf16)] * 2,
        scratch_shapes=[pltpu.VMEM((WT_ROWS, D), bf16), pltpu.VMEM((tm, D), f32), pltpu.VMEM((tm, D), bf16)]
        + _comm_sems(2),
        compiler_params=pltpu.CompilerParams(vmem_limit_bytes=32 * MIB),
    )(wt_shard, conv_blk, ln_gain, w_rgate, w_igate, *_in_hbm(x))


def _adam_math(w, g, m, v):
    m2 = ADAM_B1 * m + (1.0 - ADAM_B1) * g
    v2 = ADAM_B2 * v + (1.0 - ADAM_B2) * (g * g)
    m_hat = m2 / (1.0 - ADAM_B1 ** ADAM_STEP)
    v_hat = v2 / (1.0 - ADAM_B2 ** ADAM_STEP)
    delta = -ADAM_LR * (m_hat / (jnp.sqrt(v_hat) + ADAM_EPS) + ADAM_WD * w)
    return delta, m2, v2


def _reduce_adamw(land, w, m, v, tr, name):
    terms, rows, cols = land.shape

    def body(l_ref, w_ref, m_ref, v_ref, g_ref, d_ref, m2_ref, v2_ref):
        g = l_ref[0].astype(f32)
        for t in range(1, terms):
            g = g + l_ref[t].astype(f32)
        g_ref[...] = g
        d_ref[...], m2_ref[...], v2_ref[...] = _adam_math(w_ref[...], g, m_ref[...], v_ref[...])

    blk = pl.BlockSpec((tr, cols), lambda i: (i, 0))
    return pl.pallas_call(
        body, name=name, grid=(rows // tr,),
        in_specs=[pl.BlockSpec((terms, tr, cols), lambda i: (0, i, 0))] + [blk] * 3, out_specs=[blk] * 4,
        out_shape=[jax.ShapeDtypeStruct((rows, cols), f32)] * 4,
        compiler_params=_params(("arbitrary",), 32),
    )(*_in_hbm(land), w, m, v)


VEC_NAMES = ("ln_gain", "conv_b", "b_rgate", "b_igate", "lru_lambda", "attn_out_gain", "lru_out_gain", "final_gain")
ROW_RGATE, ROW_IGATE, ROW_VEC, ROW_SINKS = 0, 64, 128, 136
LOSS_LANE = NQ


def _adamw_small(g_rep, g_conv, w, m, v):
    names = list(VEC_NAMES) + ["sinks", "conv_w", "w_rgate", "w_igate"]
    ins = [g_rep, g_conv] + [d[k] for k in names for d in (w, m, v)]

    def body(*refs):
        g_ref, gc_ref = refs[0], refs[1]
        in_refs = refs[2:2 + 3 * len(names)]
        out_refs = refs[2 + 3 * len(names):]

        def update(j, g, at=None):
            w_ref, m_ref, v_ref = in_refs[3 * j:3 * j + 3]
            outs = out_refs[4 * j:4 * j + 4]
            pick = (lambda r: r[...]) if at is None else (lambda r: r[at])
            res = (g,) + _adam_math(pick(w_ref), g, pick(m_ref), pick(v_ref))
            for o_ref, val in zip(outs, res):
                if at is None:
                    o_ref[...] = val
                else:
                    o_ref[at] = val

        for j in range(len(VEC_NAMES)):
            update(j, g_ref[ROW_VEC + j:ROW_VEC + j + 1, :])
        update(len(VEC_NAMES), g_ref[ROW_SINKS:ROW_SINKS + 1, 0:NQ])
        update(len(VEC_NAMES) + 1, gc_ref[...], at=0)
        for gi, row0 in ((len(VEC_NAMES) + 2, ROW_RGATE), (len(VEC_NAMES) + 3, ROW_IGATE)):
            for nb in range(NQ):
                update(gi, g_ref[row0:row0 + HD, HD * nb:HD * nb + HD], at=(0, nb))

    vmem = pl.BlockSpec(memory_space=pltpu.VMEM)
    out_shape = [jax.ShapeDtypeStruct(w[k].shape, f32) for k in names for _ in range(4)]
    outs = pl.pallas_call(
        body, name="adamw_small",
        in_specs=[vmem] * len(ins), out_specs=[vmem] * len(out_shape), out_shape=out_shape,
        compiler_params=pltpu.CompilerParams(vmem_limit_bytes=32 * MIB),
    )(*ins)
    return {k: tuple(outs[4 * j:4 * j + 4]) for j, k in enumerate(names)}


def _pack_small(small, loss):
    gate = lambda g: g.transpose(1, 0, 2).reshape(HD, NQ * HD)
    row_s = jnp.concatenate([small["sinks"], loss[:, LOSS_LANE:128], jnp.zeros((1, D - 128), f32)], axis=1)
    rep = jnp.concatenate([gate(small["w_rgate"]), gate(small["w_igate"])] + [small[k] for k in VEC_NAMES]
                          + [row_s, jnp.zeros((SMALL_ROWS - ROW_SINKS - 1, D), f32)], axis=0)
    conv = small["conv_w"].reshape(CONVW, NDEV, 128).transpose(1, 0, 2)
    conv = jnp.pad(conv, ((0, 0), (0, 8 - CONVW), (0, D - 128)))
    return jnp.concatenate([rep.reshape(NDEV, SMALL_PER, D), conv], axis=1).reshape(NDEV * (SMALL_PER + 8), D)


def kernel(x, ln_gain, w_in, sinks, conv_w, conv_b, w_rgate, b_rgate, w_igate, b_igate, lru_lambda, attn_out_gain, lru_out_gain, w_out, final_gain, loss_target, m_ln_gain, m_w_in, m_sinks, m_conv_w, m_conv_b, m_w_rgate, m_b_rgate, m_w_igate, m_b_igate, m_lru_lambda, m_attn_out_gain, m_lru_out_gain, m_w_out, m_final_gain, v_ln_gain, v_w_in, v_sinks, v_conv_w, v_conv_b, v_w_rgate, v_b_rgate, v_w_igate, v_b_igate, v_lru_lambda, v_attn_out_gain, v_lru_out_gain, v_w_out, v_final_gain):
    w = dict(ln_gain=ln_gain, sinks=sinks, conv_w=conv_w, conv_b=conv_b, w_rgate=w_rgate, b_rgate=b_rgate,
             w_igate=w_igate, b_igate=b_igate, lru_lambda=lru_lambda, attn_out_gain=attn_out_gain,
             lru_out_gain=lru_out_gain, final_gain=final_gain.reshape(1, D))
    m = dict(ln_gain=m_ln_gain, sinks=m_sinks, conv_w=m_conv_w, conv_b=m_conv_b, w_rgate=m_w_rgate,
             b_rgate=m_b_rgate, w_igate=m_w_igate, b_igate=m_b_igate, lru_lambda=m_lru_lambda,
             attn_out_gain=m_attn_out_gain, lru_out_gain=m_lru_out_gain, final_gain=m_final_gain.reshape(1, D))
    v = dict(ln_gain=v_ln_gain, sinks=v_sinks, conv_w=v_conv_w, conv_b=v_conv_b, w_rgate=v_w_rgate,
             b_rgate=v_b_rgate, w_igate=v_w_igate, b_igate=v_b_igate, lru_lambda=v_lru_lambda,
             attn_out_gain=v_attn_out_gain, lru_out_gain=v_lru_out_gain, final_gain=v_final_gain.reshape(1, D))

    conv_blk = jnp.pad(conv_w[0], ((0, 8 - CONVW), (0, 0)))
    wt, cw_all, h, wr, wi = _gather_weights(w_in[0].T, conv_blk, x[0], ln_gain, w_rgate[0], w_igate[0],
                                            min(512, x.shape[1]))
    conv_full = cw_all.reshape(NDEV, 8, 128)[:, 0:CONVW].transpose(1, 0, 2).reshape(CONVW, LW)

    p = {k: w[k] for k in w if k not in ("conv_w", "w_rgate", "w_igate")}
    gx, land_wt, land_wo, g_rep, g_tail = _sequence_step(
        x[0], h, loss_target[0], wt, w_out[0], conv_full, wr, wi, p)
    g_conv = g_tail[0:CONVW, 0:128]

    wins = _reduce_adamw(land_wt, w_in[0].T, m_w_in[0].T, v_w_in[0].T, 192, "adamw_w_in")
    g_win, d_win, m_win, v_win = (t.T for t in wins)
    g_wo, d_wo, m_wo, v_wo = _reduce_adamw(land_wo, w_out[0], m_w_out[0], v_w_out[0], 256, "adamw_w_out")
    res = _adamw_small(g_rep, g_conv, w, m, v)
    res["w_in"] = tuple(t[None] for t in (g_win, d_win, m_win, v_win))
    res["w_out"] = tuple(t[None] for t in (g_wo, d_wo, m_wo, v_wo))
    res["final_gain"] = tuple(t.reshape(D) for t in res["final_gain"])

    order = ("ln_gain", "w_in", "sinks", "conv_w", "conv_b", "w_rgate", "b_rgate", "w_igate", "b_igate",
             "lru_lambda", "attn_out_gain", "lru_out_gain", "w_out", "final_gain")
    total_loss = g_rep[ROW_SINKS, LOSS_LANE]
    return (total_loss, gx[None]) + tuple(res[k][i] for i in range(4) for k in order)
```

```python
import jax
import jax.numpy as jnp
from jax import lax
from jax.experimental import pallas as pl
from jax.experimental.pallas import tpu as pltpu

f32 = jnp.float32
bf16 = jnp.bfloat16

D = 1024
HD = 64
NQ = 16
NKV = 4
GROUP = NQ // NKV
KVW = NKV * HD
BLK = 128
ROT = 16
THETA = 500000.0
NEG = -1e30
LW = 1024
NGRP = 4
CONVW = 4
LRU_C = 8.0
NIN = 4608
EPS = 1e-6
NDEV = 8
WT_ROWS = NIN // NDEV
WO_ROWS = 2 * D // NDEV
SMALL_ROWS = 192
SMALL_PER = SMALL_ROWS // NDEV

ADAM_LR = 0.001
ADAM_B1 = 0.9
ADAM_B2 = 0.999
ADAM_EPS = 1e-08
ADAM_WD = 0.01
ADAM_STEP = 10

NT = (((1,), (1,)), ((), ()))
TN = (((0,), (0,)), ((), ()))
MESH = pl.DeviceIdType.MESH
MIB = 1024 * 1024


def _dot(a, b):
    return jnp.dot(a, b, preferred_element_type=f32)


def _dot_nt(a, b):
    return lax.dot_general(a, b, NT, preferred_element_type=f32)


def _dot_tn(a, b):
    return lax.dot_general(a, b, TN, preferred_element_type=f32)


def _params(sem, vmem_mib):
    return pltpu.CompilerParams(dimension_semantics=sem, vmem_limit_bytes=vmem_mib * MIB)


def _sigmoid(x):
    return 0.5 * jnp.tanh(0.5 * x) + 0.5


def _softplus(x):
    return jnp.maximum(x, 0.0) + jnp.log(1.0 + jnp.exp(-jnp.abs(x)))


def _rope_tables(s):
    pos = jnp.arange(s, dtype=f32)
    inv_freq = THETA ** (-jnp.arange(0, ROT, 2, dtype=f32) / ROT)
    ang = pos[:, None] * inv_freq[None, :]
    cs = jnp.concatenate([jnp.cos(ang) - 1.0, jnp.sin(ang)], axis=1)
    d = jnp.arange(128) % HD
    j = jnp.arange(ROT)[:, None]
    pick_c = ((d < ROT) & (j == d % (ROT // 2))).astype(f32)
    pick_sa = ((d >= ROT // 2) & (d < ROT) & (j == d)).astype(f32)
    pick_sb = -((d < ROT // 2) & (j == d + ROT // 2)).astype(f32)
    picks = jnp.concatenate([pick_c, pick_sa, pick_sb], axis=1)
    ones = jnp.concatenate([jnp.ones((1, 128), f32), jnp.zeros((1, 256), f32)], axis=1)
    return jnp.dot(cs, picks, precision=lax.Precision.HIGHEST) + ones


def _tables(tab_ref):
    return tab_ref[:, 0:128], tab_ref[:, 128:256], tab_ref[:, 256:384]


def _rope(t, c, sa, sb):
    return t * c + pltpu.roll(t, 8, 1) * sa + pltpu.roll(t, 120, 1) * sb


def _unrope_t(dr, c, sa, sb):
    return dr * c + pltpu.roll(dr * sa, 120, 0) + pltpu.roll(dr * sb, 8, 0)


def _place():
    return lax.axis_index("x"), lax.axis_index("y"), lax.axis_index("c")


def _gather_ops(mine_refs, out_refs, send_sems, recv_sems, local_sems):
    n = len(mine_refs)
    x, y, c = _place()
    me, sibling = (x, y, c), (x, y, 1 - c)
    chips = [(1 - x, y), (x, 1 - y), (1 - x, 1 - y)]

    def rows(a, dev):
        m = mine_refs[a].shape[0]
        return out_refs[a].at[pl.ds((4 * dev[0] + 2 * dev[1] + dev[2]) * m, m), :]

    def copy(a, k, block, to, own=False):
        return pltpu.make_async_remote_copy(
            src_ref=mine_refs[a] if own else rows(a, block), dst_ref=rows(a, block),
            send_sem=send_sems.at[a, k], recv_sem=recv_sems.at[a, k], device_id=to, device_id_type=MESH)

    def local(a):
        return pltpu.make_async_copy(mine_refs[a], rows(a, me), local_sems.at[a])

    def first(a):
        return [copy(a, 0, me, sibling, own=True)] + [copy(a, 1 + j, me, (*chip, c), own=True)
                                                      for j, chip in enumerate(chips)]

    def start():
        for a in range(n):
            local(a).start()
            for cp in first(a):
                cp.start()

    def pass_on():
        for j, chip in enumerate(chips):
            for a in range(n):
                copy(a, 1 + j, (*chip, c), me).wait_recv()
                copy(a, 4 + j, (*chip, c), sibling).start()

    def finish():
        for a in range(n):
            copy(a, 0, sibling, me).wait_recv()
            for j, chip in enumerate(chips):
                copy(a, 4 + j, (*chip, 1 - c), me).wait_recv()
        for a in range(n):
            for cp in first(a) + [copy(a, 4 + j, (*chip, c), sibling) for j, chip in enumerate(chips)]:
                cp.wait_send()
            local(a).wait()

    return start, pass_on, finish


def _relay_gather_ops(mine_refs, out_refs, send_sems, recv_sems, local_sems):
    n = len(mine_refs)
    x, y, c = _place()
    me, sibling = (x, y, c), (x, y, 1 - c)
    near = (x ^ (1 - c), y ^ c)
    far = (x ^ c, y ^ (1 - c))
    diag = (1 - x, 1 - y)

    def rows(a, dev):
        m = mine_refs[a].shape[0]
        return out_refs[a].at[pl.ds((4 * dev[0] + 2 * dev[1] + dev[2]) * m, m), :]

    def copy(a, k, block, to, own=False):
        return pltpu.make_async_remote_copy(
            src_ref=mine_refs[a] if own else rows(a, block), dst_ref=rows(a, block),
            send_sem=send_sems.at[a, k], recv_sem=recv_sems.at[a, k], device_id=to, device_id_type=MESH)

    def local(a):
        return pltpu.make_async_copy(mine_refs[a], rows(a, me), local_sems.at[a])

    def sends(a):
        return [copy(a, 0, me, sibling, own=True), copy(a, 1, me, (*near, c), own=True),
                copy(a, 2, me, (*far, c), own=True), copy(a, 3, (*near, c), (*far, c)),
                copy(a, 4, (*near, c), sibling), copy(a, 5, (*far, c), sibling), copy(a, 6, (*diag, c), sibling)]

    def arrivals(a):
        return [copy(a, 0, sibling, me), copy(a, 1, (*near, c), me), copy(a, 2, (*far, c), me),
                copy(a, 3, (*diag, c), me), copy(a, 4, (*far, 1 - c), me), copy(a, 5, (*near, 1 - c), me),
                copy(a, 6, (*diag, 1 - c), me)]

    def start():
        for a in range(n):
            local(a).start()
            for cp in sends(a)[0:3]:
                cp.start()

    def finish():
        for first, then in ((1, (3, 4)), (2, (5,)), (3, (6,))):
            for a in range(n):
                arrivals(a)[first].wait_recv()
                for k in then:
                    sends(a)[k].start()
        for a in range(n):
            for k in (0, 4, 5, 6):
                arrivals(a)[k].wait_recv()
        for a in range(n):
            for cp in sends(a):
                cp.wait_send()
            local(a).wait()

    return start, finish


def _scatter_ops(src_refs, land_refs, send_sems, recv_sems, local_sems):
    n = len(src_refs)
    x, y, c = _place()
    my = 4 * x + 2 * y + c

    def peer(k):
        return x ^ (k >> 2), y ^ ((k >> 1) & 1), c ^ (k & 1)

    def piece(a, dev):
        m = src_refs[a].shape[0] // NDEV
        return src_refs[a].at[pl.ds(dev * m, m), :]

    def local(a):
        return pltpu.make_async_copy(piece(a, my), land_refs[a].at[my], local_sems.at[a])

    def send(a, k):
        px, py, pc = peer(k)
        return pltpu.make_async_remote_copy(
            src_ref=piece(a, 4 * px + 2 * py + pc), dst_ref=land_refs[a].at[my],
            send_sem=send_sems.at[a, k - 1], recv_sem=recv_sems.at[a, k - 1],
            device_id=(px, py, pc), device_id_type=MESH)

    def arrival(a, k):
        px, py, pc = peer(k)
        return pltpu.make_async_remote_copy(
            src_ref=piece(a, my), dst_ref=land_refs[a].at[4 * px + 2 * py + pc],
            send_sem=send_sems.at[a, k - 1], recv_sem=recv_sems.at[a, k - 1],
            device_id=(px, py, pc), device_id_type=MESH)

    def start():
        for a in range(n):
            local(a).start()
        for k in range(1, NDEV):
            for a in range(n):
                send(a, k).start()

    def finish():
        for k in range(1, NDEV):
            for a in range(n):
                send(a, k).wait_send()
        for k in range(1, NDEV):
            for a in range(n):
                arrival(a, k).wait_recv()
        for a in range(n):
            local(a).wait()

    return start, finish


def _in_hbm(*arrays):
    return tuple(pltpu.with_memory_space_constraint(a, pltpu.HBM) for a in arrays)


def _comm_sems(n):
    return [pltpu.SemaphoreType.DMA((n, 7)), pltpu.SemaphoreType.DMA((n, 7)), pltpu.SemaphoreType.DMA((n,))]


HBM = pl.BlockSpec(memory_space=pltpu.HBM)


def _sink_rows(sinks):
    return jnp.repeat(sinks.reshape(NKV, GROUP), BLK, axis=1)


def _band_softmax(s2_ref, ls, prev_offset, sink_row):
    jj = lax.broadcasted_iota(jnp.int32, (BLK, BLK), 0)
    ii = lax.broadcasted_iota(jnp.int32, (BLK, BLK), 1)
    from_prev = jj > ii
    sc = jnp.where(from_prev, s2_ref[0:BLK, ls] + prev_offset, s2_ref[BLK:2 * BLK, ls])
    m = jnp.maximum(jnp.max(sc, axis=0, keepdims=True), sink_row)
    p = jnp.exp(sc - m)
    es = jnp.exp(sink_row - m)
    inv = 1.0 / (jnp.sum(p, axis=0, keepdims=True) + es)
    return from_prev, p * inv, es * inv


def _put_split(dst_ref, ls, t, from_prev):
    t = t.astype(bf16)
    zero = jnp.zeros_like(t)
    dst_ref[0:BLK, ls] = jnp.where(from_prev, t, zero)
    dst_ref[BLK:2 * BLK, ls] = jnp.where(from_prev, zero, t)


def _heads_side_by_side(ref, h):
    return jnp.concatenate([ref[HD * (GROUP * h + g):HD * (GROUP * h + g) + HD, :] for g in range(GROUP)], axis=1)


def _kv_specs_t():
    prev = pl.BlockSpec((KVW, BLK), lambda n: (0, jnp.maximum(n - 1, 0)))
    cur = pl.BlockSpec((KVW, BLK), lambda n: (0, n))
    return [prev, cur, prev, cur]


def _attn_fwd_t(qt, kt, vt, sinks):
    s = qt.shape[1]

    def body(sink_ref, q_ref, kp_ref, kc_ref, vp_ref, vc_ref, o_ref, s2_scr, pn2_scr):
        n = pl.program_id(0)
        off = jnp.where(n > 0, 0.0, NEG)

        def scores(h):
            hs = slice(HD * h, HD * h + HD)
            kh = jnp.concatenate([kp_ref[hs, :], kc_ref[hs, :]], axis=1)
            s2_scr[h % 2] = _dot_tn(kh, _heads_side_by_side(q_ref, h))

        def probs(h):
            for g in range(GROUP):
                ls = slice(BLK * g, BLK * g + BLK)
                from_prev, pn, _ = _band_softmax(s2_scr.at[h % 2], ls, off, sink_ref[h:h + 1, ls])
                _put_split(pn2_scr.at[h % 2], ls, pn, from_prev)

        def outputs(h):
            hs = slice(HD * h, HD * h + HD)
            vh = jnp.concatenate([vp_ref[hs, :], vc_ref[hs, :]], axis=1)
            og = _dot(vh, pn2_scr[h % 2])
            for g in range(GROUP):
                a = GROUP * h + g
                o_ref[HD * a:HD * a + HD, :] = og[:, BLK * g:BLK * g + BLK]

        scores(0)
        for h in range(NKV):
            if h + 1 < NKV:
                scores(h + 1)
            probs(h)
            outputs(h)

    return pl.pallas_call(
        body, name="attn_fwd", grid=(s // BLK,),
        in_specs=[pl.BlockSpec((NKV, GROUP * BLK), lambda n: (0, 0)), pl.BlockSpec((D, BLK), lambda n: (0, n))]
        + _kv_specs_t(),
        out_specs=pl.BlockSpec((D, BLK), lambda n: (0, n)),
        out_shape=pltpu.HBM((D, s), f32),
        scratch_shapes=[pltpu.VMEM((2, 2 * BLK, GROUP * BLK), f32), pltpu.VMEM((2, 2 * BLK, GROUP * BLK), bf16)],
        compiler_params=_params(("arbitrary",), 32),
    )(_sink_rows(sinks), *_in_hbm(qt, kt, kt, vt, vt))


def _attn_bwd_t(qt, kt, vt, dot, sinks, dwo):
    s = qt.shape[1]
    nb = s // BLK

    def body(sink_ref, q_ref, do_ref, kp_ref, kc_ref, vp_ref, vc_ref, dwo_ref, dq_ref, dk_ref, dv_ref, ds_ref,
             land_ref, dk_hold, dv_hold, s2_scr, dp2_scr, pn2_scr, ds2_scr, send_sems, recv_sems, local_sems):
        n = pl.program_id(0)
        start, finish = _scatter_ops([dwo_ref], [land_ref], send_sems, recv_sems, local_sems)

        @pl.when(n == 0)
        def _():
            start()
            dk_hold[...] = jnp.zeros_like(dk_hold)
            dv_hold[...] = jnp.zeros_like(dv_hold)
            ds_ref[...] = jnp.zeros_like(ds_ref)

        @pl.when(n < nb)
        def _():
            off = jnp.where(n > 0, 0.0, NEG)

            def scores(h):
                hs = slice(HD * h, HD * h + HD)
                kh = jnp.concatenate([kp_ref[hs, :], kc_ref[hs, :]], axis=1)
                vh = jnp.concatenate([vp_ref[hs, :], vc_ref[hs, :]], axis=1)
                s2_scr[h % 2] = _dot_tn(kh, _heads_side_by_side(q_ref, h))
                dp2_scr[h % 2] = _dot_tn(vh, _heads_side_by_side(do_ref, h))

            def softmax_bwd(h):
                for g in range(GROUP):
                    ls = slice(BLK * g, BLK * g + BLK)
                    from_prev, pn, ps = _band_softmax(s2_scr.at[h % 2], ls, off, sink_ref[h:h + 1, ls])
                    dp = jnp.where(from_prev, dp2_scr[h % 2, 0:BLK, ls], dp2_scr[h % 2, BLK:2 * BLK, ls])
                    dsum = jnp.sum(pn * dp, axis=0, keepdims=True)
                    ds_ref[h:h + 1, ls] += -ps * dsum
                    _put_split(pn2_scr.at[h % 2], ls, pn, from_prev)
                    _put_split(ds2_scr.at[h % 2], ls, pn * (dp - dsum), from_prev)

            def grads(h):
                hs = slice(HD * h, HD * h + HD)
                kh = jnp.concatenate([kp_ref[hs, :], kc_ref[hs, :]], axis=1)
                dqg = _dot(kh, ds2_scr[h % 2])
                for g in range(GROUP):
                    a = GROUP * h + g
                    dq_ref[HD * a:HD * a + HD, :] = dqg[:, BLK * g:BLK * g + BLK]
                dkh = _dot_nt(_heads_side_by_side(q_ref, h), ds2_scr[h % 2])
                dvh = _dot_nt(_heads_side_by_side(do_ref, h), pn2_scr[h % 2])
                dk_ref[hs, :] = dk_hold[hs, :] + dkh[:, 0:BLK]
                dv_ref[hs, :] = dv_hold[hs, :] + dvh[:, 0:BLK]
                dk_hold[hs, :] = dkh[:, BLK:2 * BLK]
                dv_hold[hs, :] = dvh[:, BLK:2 * BLK]

            scores(0)
            for h in range(NKV):
                if h + 1 < NKV:
                    scores(h + 1)
                softmax_bwd(h)
                grads(h)

        @pl.when(n == nb)
        def _():
            dk_ref[...] = dk_hold[...]
            dv_ref[...] = dv_hold[...]
            finish()

    blk = pl.BlockSpec((D, BLK), lambda n: (0, jnp.minimum(n, nb - 1)))
    late = pl.BlockSpec((KVW, BLK), lambda n: (0, jnp.maximum(n - 1, 0)))
    whole = pl.BlockSpec((NKV, GROUP * BLK), lambda n: (0, 0))
    kv = [pl.BlockSpec((KVW, BLK), lambda n: (0, jnp.clip(n - 1, 0, nb - 1))),
          pl.BlockSpec((KVW, BLK), lambda n: (0, jnp.minimum(n, nb - 1)))]
    return pl.pallas_call(
        body, name="attn_bwd", grid=(nb + 1,),
        in_specs=[whole, blk, blk] + kv + kv + [HBM],
        out_specs=[blk, late, late, whole, HBM],
        out_shape=[pltpu.HBM((D, s), f32), pltpu.HBM((KVW, s), f32), pltpu.HBM((KVW, s), f32),
                   jax.ShapeDtypeStruct((NKV, GROUP * BLK), f32), pltpu.HBM((NDEV, WO_ROWS, D), bf16)],
        scratch_shapes=[pltpu.VMEM((KVW, BLK), f32), pltpu.VMEM((KVW, BLK), f32)]
        + [pltpu.VMEM((2, 2 * BLK, GROUP * BLK), f32)] * 2 + [pltpu.VMEM((2, 2 * BLK, GROUP * BLK), bf16)] * 2
        + _comm_sems(1),
        compiler_params=_params(("arbitrary",), 48),
    )(_sink_rows(sinks), *_in_hbm(qt, dot, kt, kt, vt, vt, dwo))


def _decay_terms(r, sp):
    a = jnp.exp(r * (-LRU_C * sp))
    n = r * (2.0 * LRU_C * sp)
    y = jnp.where(n < 0.02, n * (1.0 - n * (0.5 - n * (1.0 / 6.0))), 1.0 - a * a)
    inv_mult = lax.rsqrt(jnp.maximum(y, 1e-30))
    return a, y * inv_mult, inv_mult


def _later(x, before, k):
    if k == 0:
        return x
    row = lax.broadcasted_iota(jnp.int32, before.shape, 0)
    rolled = pltpu.roll(x, k, 0)
    first = jnp.where(row < k, pltpu.roll(before, k, 0), rolled[0:8])
    return jnp.concatenate([first, rolled[8:]], axis=0)


def _earlier(x, after, k):
    if k == 0:
        return x
    n = x.shape[0]
    row = lax.broadcasted_iota(jnp.int32, after.shape, 0)
    rolled = pltpu.roll(x, n - k, 0)
    last = jnp.where(row >= 8 - k, pltpu.roll(after, 8 - k, 0), rolled[n - 8:n])
    return jnp.concatenate([rolled[0:n - 8], last], axis=0)


def _fwd_fused(h, wt, tabs, wo_shard, conv_w, conv_b, wr, wi, br, bi, lam, tm):
    s = h.shape[0]
    nt = s // tm
    nc = 512
    pieces = 8
    rows_per = tm // pieces
    later_chunks = (0, 1, 2, 3, 4, 7, 8)

    def body(h_ref, wt_ref, tab_ref, wo_ref, cw_ref, cb_ref, wr_ref, wi_ref, br_ref,
             bi_ref, lam_ref, q_ref, k_ref, v_ref, ga_ref, xl_ref, gl_ref, u_ref, hl_ref, r_ref, ig_ref,
             wo_all, wo_stage, halo, ub_scr, pr_scr, pi_scr, b_scr, a_scr, hcar,
             send_sems, recv_sems, local_sems):
        i = pl.program_id(0)
        start, pass_on, finish = _gather_ops([wo_stage], [wo_all], send_sems, recv_sems, local_sems)

        @pl.when(i == 0)
        def _():
            wo_stage[...] = wo_ref[...].astype(bf16)
            start()
            halo[...] = jnp.zeros_like(halo)
            hcar[...] = jnp.zeros_like(hcar)

        sp = _softplus(-lam_ref[...])
        br, bi = br_ref[...], bi_ref[...]
        c, sa, sb = _tables(tab_ref)
        piece_rows = lambda p: slice(rows_per * p, rows_per * p + rows_per)

        def project(ci):
            z = _dot_nt(h_ref[...], wt_ref[ci * nc:(ci + 1) * nc, :])
            if ci < 2:
                for j in range(nc // 128):
                    r = _rope(z[:, 128 * j:128 * j + 128], c, sa, sb) * (HD ** -0.5)
                    q_ref[ci * nc + 128 * j:ci * nc + 128 * j + 128, :] = r.astype(bf16).T
            elif ci == 2:
                for j in range(2):
                    js = slice(128 * j, 128 * j + 128)
                    k_ref[js, :] = _rope(z[:, js], c, sa, sb).astype(bf16).T
                    v_ref[js, :] = z[:, KVW + 128 * j:KVW + 128 * j + 128].astype(bf16).T
            else:
                sec, j = divmod(ci - 3, 2)
                (ga_ref, xl_ref, gl_ref)[sec][:, j * nc:(j + 1) * nc] = z

        def gate_terms(p):
            rows = piece_rows(p)
            r = _sigmoid(pr_scr[rows, :] + br)
            ig = _sigmoid(pi_scr[rows, :] + bi)
            a, mult, _ = _decay_terms(r, sp)
            r_ref[rows, :] = r
            ig_ref[rows, :] = ig
            a_scr[rows, :] = a
            b_scr[rows, :] = mult * (ig * u_ref[rows, :])

        def scan(p, hc):
            for t in range(rows_per * p, rows_per * p + rows_per):
                hc = a_scr[t:t + 1, :] * hc + b_scr[t:t + 1, :]
                hl_ref[t:t + 1, :] = hc
            return hc

        project(5)
        project(6)
        xl = xl_ref[...]
        u = cb_ref[...] + sum(cw_ref[k:k + 1, :] * _later(xl, halo[...], CONVW - 1 - k) for k in range(CONVW))
        halo[...] = xl[tm - 8:tm, :]
        u_ref[...] = u
        ub_scr[...] = u.astype(bf16)
        for g in range(NGRP):
            gs = slice(256 * g, 256 * g + 256)
            pr_scr[:, gs] = _dot(ub_scr[:, gs], wr_ref[g])
            pi_scr[:, gs] = _dot(ub_scr[:, gs], wi_ref[g])
        hc = hcar[...]
        gate_terms(0)
        for slot, ci in enumerate(later_chunks):
            project(ci)
            gate_terms(slot + 1)
            hc = scan(slot, hc)
        hcar[...] = scan(pieces - 1, hc)

        @pl.when(i == max(nt - 2, 0))
        def _():
            pass_on()

        @pl.when(i == nt - 1)
        def _():
            finish()

    row = lambda w: pl.BlockSpec((tm, w), lambda i: (i, 0))
    col = lambda w: pl.BlockSpec((w, tm), lambda i: (0, i))
    full = lambda a: pl.BlockSpec(a.shape, lambda i: (0,) * a.ndim)
    big = lambda w, dt: pltpu.HBM((s, w), dt)
    tile = pltpu.VMEM((tm, LW), f32)
    return pl.pallas_call(
        body, name="fwd_fused", grid=(nt,),
        in_specs=[row(D), full(wt), row(384), full(wo_shard), full(conv_w), full(conv_b),
                  full(wr), full(wi), full(br), full(bi), full(lam)],
        out_specs=[col(D), col(KVW), col(KVW), row(D), row(D), row(D)] + [row(LW)] * 4 + [HBM],
        out_shape=[pltpu.HBM((D, s), bf16), pltpu.HBM((KVW, s), bf16), pltpu.HBM((KVW, s), bf16),
                   big(D, f32), big(D, f32), big(D, f32)] + [big(LW, f32)] * 4 + [pltpu.HBM((2 * D, D), bf16)],
        scratch_shapes=[pltpu.VMEM((WO_ROWS, D), bf16), pltpu.VMEM((8, LW), f32), pltpu.VMEM((tm, LW), bf16)]
        + [tile] * 4 + [pltpu.VMEM((1, LW), f32)] + _comm_sems(1),
        compiler_params=_params(("arbitrary",), 56),
    )(*_in_hbm(h, wt), tabs, wo_shard, conv_w, conv_b, wr, wi, br, bi, lam)


def _lru_bwd(u, hl, dhl, xl, r, ig, conv_w, wr, wi, lam, tm):
    s = u.shape[0]
    nt = s // tm
    pieces = 8
    rows_per = tm // pieces

    def body(u_ref, h_ref, hp_ref, dh_ref, x_ref, r_ref, ig_ref, cw_ref, wr_ref, wi_ref,
             lam_ref, dxl_ref, dwr_ref, dwi_ref, dbr_ref, dbi_ref, dlam_ref, dcb_ref, dcw_ref,
             l_scr, du_scr, a_scr, mu_scr, im_scr, dpr_scr, dpi_scr, lcar, dunext):
        t0 = pl.program_id(0)
        tile = nt - 1 - t0

        @pl.when(t0 == 0)
        def _():
            lcar[...] = jnp.zeros_like(lcar)
            dunext[...] = jnp.zeros_like(dunext)
            for ref in (dwr_ref, dwi_ref, dbr_ref, dbi_ref, dlam_ref, dcb_ref, dcw_ref):
                ref[...] = jnp.zeros_like(ref)

        lam = lam_ref[...]
        sp = _softplus(-lam)
        hp = jnp.where(tile > 0, hp_ref[...], 0.0)

        def decay(p):
            rows = slice(rows_per * p, rows_per * p + rows_per)
            a_scr[rows, :], mu_scr[rows, :], im_scr[rows, :] = _decay_terms(r_ref[rows, :], sp)

        def scan(p, c):
            for t in range(rows_per * p + rows_per - 1, rows_per * p - 1, -1):
                lt = dh_ref[t:t + 1, :] + c
                l_scr[t:t + 1, :] = lt
                c = a_scr[t:t + 1, :] * lt
            return c

        def terms(p, sums):
            rows = slice(rows_per * p, rows_per * p + rows_per)
            lt, u, r, i, a, mult, inv_mult = l_scr[rows, :], u_ref[rows, :], r_ref[rows, :], ig_ref[rows, :], \
                a_scr[rows, :], mu_scr[rows, :], im_scr[rows, :]
            before = hp if p == 0 else h_ref[rows_per * p - 8:rows_per * p, :]
            hprev = _later(h_ref[rows, :], before, 1)
            iu = i * u
            lm = lt * mult
            du_scr[rows, :] = lm * i
            dla = (lt * hprev) * a - ((lt * iu) * (a * a)) * inv_mult
            dlar = dla * r
            dpr = (dlar * (1.0 - r)) * (-LRU_C * sp)
            dpi = (lm * iu) * (1.0 - i)
            dpr_scr[rows, :] = dpr.astype(bf16)
            dpi_scr[rows, :] = dpi.astype(bf16)
            col = lambda t: jnp.sum(t, axis=0, keepdims=True)
            return sums[0] + col(dlar), sums[1] + col(dpr), sums[2] + col(dpi)

        sums = (jnp.zeros((1, LW), f32),) * 3
        decay(pieces - 1)
        c = scan(pieces - 1, lcar[...])
        for p in range(pieces - 1, -1, -1):
            if p > 0:
                decay(p - 1)
                c = scan(p - 1, c)
            sums = terms(p, sums)
        lcar[...] = c
        dlam_ref[...] += sums[0] * (-LRU_C)
        dbr_ref[...] += sums[1]
        dbi_ref[...] += sums[2]

        ub = u_ref[...].astype(bf16)
        dug = []
        for g in range(NGRP):
            gs = slice(256 * g, 256 * g + 256)
            dwr_ref[g] += _dot_tn(ub[:, gs], dpr_scr[:, gs])
            dwi_ref[g] += _dot_tn(ub[:, gs], dpi_scr[:, gs])
            dug.append(_dot_nt(dpr_scr[:, gs], wr_ref[g]) + _dot_nt(dpi_scr[:, gs], wi_ref[g]))
        du = du_scr[...] + jnp.concatenate(dug, axis=1)

        dcb_ref[...] += jnp.sum(du, axis=0, keepdims=True)
        x = x_ref[...]
        after = dunext[...]
        dxl = jnp.zeros_like(du)
        for k in range(CONVW):
            e = _earlier(du, after, CONVW - 1 - k)
            dxl = dxl + cw_ref[k:k + 1, :] * e
            dcw_ref[k:k + 1, :] += jnp.sum(e * x, axis=0, keepdims=True)
        dxl_ref[...] = dxl.astype(bf16)
        dunext[...] = du[0:8, :]

        @pl.when(t0 == nt - 1)
        def _():
            dlam_ref[...] = dlam_ref[...] * (-_sigmoid(-lam))

    rev = lambda i: (nt - 1 - i, 0)
    row = pl.BlockSpec((tm, LW), rev)
    prev8 = pl.BlockSpec((8, LW), lambda i: (jnp.maximum((nt - 1 - i) * (tm // 8) - 1, 0), 0))
    full = lambda a: pl.BlockSpec(a.shape, lambda i: (0,) * a.ndim)
    vec = pl.BlockSpec((1, LW), lambda i: (0, 0))
    bd = pl.BlockSpec((NGRP, 256, 256), lambda i: (0, 0, 0))
    return pl.pallas_call(
        body, name="lru_bwd", grid=(nt,),
        in_specs=[row, row, prev8] + [row] * 4 + [full(conv_w), full(wr), full(wi), full(lam)],
        out_specs=[row, bd, bd, vec, vec, vec, vec, pl.BlockSpec((CONVW, LW), lambda i: (0, 0))],
        out_shape=[pltpu.HBM((s, LW), bf16),
                   jax.ShapeDtypeStruct((NGRP, 256, 256), f32), jax.ShapeDtypeStruct((NGRP, 256, 256), f32),
                   jax.ShapeDtypeStruct((1, LW), f32), jax.ShapeDtypeStruct((1, LW), f32),
                   jax.ShapeDtypeStruct((1, LW), f32), jax.ShapeDtypeStruct((1, LW), f32),
                   jax.ShapeDtypeStruct((CONVW, LW), f32)],
        scratch_shapes=[pltpu.VMEM((tm, LW), f32)] * 5 + [pltpu.VMEM((tm, LW), bf16)] * 2
        + [pltpu.VMEM((1, LW), f32), pltpu.VMEM((8, LW), f32)],
        compiler_params=_params(("arbitrary",), 56),
    )(*_in_hbm(u, hl, hl, dhl, xl, r, ig), conv_w, wr, wi, lam)


def _gated_norm(t, gate, gain):
    sg = _sigmoid(gate)
    silu = gate * sg
    p = t * silu
    rstd = lax.rsqrt(jnp.mean(p * p, axis=-1, keepdims=True) + EPS)
    ph = p * rstd
    return sg, silu, rstd, ph, ph * gain


def _gated_norm_bwd(dy, t, gate, gain, sg, silu, rstd, ph):
    w = dy * gain
    dp = rstd * (w - ph * jnp.mean(w * ph, axis=-1, keepdims=True))
    dgate = (dp * t) * (sg + silu * (1.0 - sg))
    return jnp.sum(dy * ph, axis=0, keepdims=True), dp * silu, dgate


def _out_fwd_bwd(x, tgt, o, ga, hl, gl, again, lgain, fgain, wo, tm):
    s = x.shape[0]
    nt = s // tm

    def body(x_ref, t_ref, o_ref, ga_ref, hl_ref, gl_ref, ag_ref, lg_ref, fg_ref, wo_ref,
             dx2_ref, do_ref, dga_ref, dhl_ref, dgl_ref, dwo_ref, gfg_ref, gag_ref, glg_ref, loss_ref, acc):
        i = pl.program_id(0)

        @pl.when(i == 0)
        def _():
            acc[...] = jnp.zeros_like(acc)
            for ref in (gfg_ref, gag_ref, glg_ref, loss_ref):
                ref[...] = jnp.zeros_like(ref)

        oo = jnp.concatenate([o_ref[128 * j:128 * j + 128, :].T for j in range(D // 128)], axis=1)
        gga, hh, ggl = ga_ref[...], hl_ref[...], gl_ref[...]
        ag, lg, fg = ag_ref[...], lg_ref[...], fg_ref[...]
        sga, silua, ra, pah, ya = _gated_norm(oo, gga, ag)
        sgl, silul, rl, plh, yl = _gated_norm(hh, ggl, lg)
        yab, ylb = ya.astype(bf16), yl.astype(bf16)
        y = _dot(yab, wo_ref[0:D, :]) + _dot(ylb, wo_ref[D:2 * D, :])
        x2 = x_ref[...] + y
        r2 = lax.rsqrt(jnp.mean(x2 * x2, axis=-1, keepdims=True) + EPS)
        x2h = x2 * r2
        err = x2h * fg - t_ref[...]
        loss_ref[...] += 0.5 * jnp.sum(jnp.sum(err * err, axis=-1, keepdims=True) * (1.0 / D))
        gfg_ref[...] += jnp.sum(err * x2h, axis=0, keepdims=True) * (1.0 / D)
        w = err * (fg * (1.0 / D))
        dx2 = r2 * (w - x2h * jnp.mean(w * x2h, axis=-1, keepdims=True))
        dx2_ref[...] = dx2
        dyb = dx2.astype(bf16)
        acc[0:D, :] += _dot_tn(yab, dyb)
        acc[D:2 * D, :] += _dot_tn(ylb, dyb)
        dya = _dot_nt(dyb, wo_ref[0:D, :])
        dyl = _dot_nt(dyb, wo_ref[D:2 * D, :])
        gag, do, dga = _gated_norm_bwd(dya, oo, gga, ag, sga, silua, ra, pah)
        glg, dhl, dgl = _gated_norm_bwd(dyl, hh, ggl, lg, sgl, silul, rl, plh)
        gag_ref[...] += gag
        glg_ref[...] += glg
        dob = do.astype(bf16)
        for j in range(D // 128):
            do_ref[128 * j:128 * j + 128, :] = dob[:, 128 * j:128 * j + 128].T
        dga_ref[...] = dga.astype(bf16)
        dhl_ref[...] = dhl
        dgl_ref[...] = dgl.astype(bf16)

        @pl.when(i == nt - 1)
        def _():
            dwo_ref[...] = acc[...].astype(bf16)

    row = pl.BlockSpec((tm, D), lambda i: (i, 0))
    col = pl.BlockSpec((D, tm), lambda i: (0, i))
    vec = pl.BlockSpec((1, D), lambda i: (0, 0))
    mat = pl.BlockSpec((2 * D, D), lambda i: (0, 0))
    return pl.pallas_call(
        body, name="out_fwd_bwd", grid=(nt,),
        in_specs=[row, row, col, row, row, row] + [vec] * 3 + [mat],
        out_specs=[row, col, row, row, row] + [mat, vec, vec, vec, pl.BlockSpec((1, 128), lambda i: (0, 0))],
        out_shape=[pltpu.HBM((s, D), f32), pltpu.HBM((D, s), bf16),
                   pltpu.HBM((s, D), bf16), pltpu.HBM((s, D), f32),
                   pltpu.HBM((s, D), bf16), pltpu.HBM((2 * D, D), bf16),
                   jax.ShapeDtypeStruct((1, D), f32), jax.ShapeDtypeStruct((1, D), f32),
                   jax.ShapeDtypeStruct((1, D), f32), jax.ShapeDtypeStruct((1, 128), f32)],
        scratch_shapes=[pltpu.VMEM((2 * D, D), f32)],
        compiler_params=_params(("arbitrary",), 56),
    )(*_in_hbm(x, tgt, o, ga, hl, gl), again, lgain, fgain, *_in_hbm(wo))


def _bwd_in(x, dx2, dq, dk, dv, dga, dxl, dgl, ln_gain, wt, tabs, tm):
    s = x.shape[0]

    def body(x_ref, dx2_ref, dq_ref, dk_ref, dv_ref, dga_ref, dxl_ref, dgl_ref, g_ref, wt_ref,
             tab_ref, gx_ref, gln_ref, dzt_ref):
        @pl.when(pl.program_id(0) == 0)
        def _():
            gln_ref[...] = jnp.zeros_like(gln_ref)

        c, sa, sb = (t.T for t in _tables(tab_ref))
        for j in range(D // 128):
            js = slice(128 * j, 128 * j + 128)
            dzt_ref[js, :] = (_unrope_t(dq_ref[js, :], c, sa, sb) * (HD ** -0.5)).astype(bf16)
        for j in range(KVW // 128):
            js = slice(128 * j, 128 * j + 128)
            dzt_ref[D + 128 * j:D + 128 * j + 128, :] = _unrope_t(dk_ref[js, :], c, sa, sb).astype(bf16)
        dzt_ref[D + KVW:D + 2 * KVW, :] = dv_ref[...].astype(bf16)
        first = D + 2 * KVW
        dh = _dot_tn(dzt_ref[0:first, :], wt_ref[0:first, :])
        for sec, ref in enumerate((dga_ref, dxl_ref, dgl_ref)):
            dh = dh + _dot(ref[...], wt_ref[first + D * sec:first + D * sec + D, :])
            for j in range(D // 128):
                dzt_ref[first + D * sec + 128 * j:first + D * sec + 128 * j + 128, :] = ref[:, 128 * j:128 * j + 128].T
        xx = x_ref[...]
        rstd = lax.rsqrt(jnp.mean(xx * xx, axis=-1, keepdims=True) + EPS)
        xh = xx * rstd
        gln_ref[...] += jnp.sum(dh * xh, axis=0, keepdims=True)
        w = dh * g_ref[...]
        gx_ref[...] = dx2_ref[...] + rstd * (w - xh * jnp.mean(w * xh, axis=-1, keepdims=True))

    row = lambda w: pl.BlockSpec((tm, w), lambda i: (i, 0))
    col = lambda w: pl.BlockSpec((w, tm), lambda i: (0, i))
    full = lambda a: pl.BlockSpec(a.shape, lambda i: (0, 0))
    return pl.pallas_call(
        body, name="bwd_in", grid=(s // tm,),
        in_specs=[row(D), row(D), col(D), col(KVW), col(KVW), row(D), row(D), row(D), full(ln_gain), full(wt),
                  row(384)],
        out_specs=[row(D), pl.BlockSpec((1, D), lambda i: (0, 0)), col(NIN)],
        out_shape=[pltpu.HBM((s, D), f32), jax.ShapeDtypeStruct((1, D), f32),
                   pltpu.HBM((NIN, s), bf16)],
        compiler_params=_params(("arbitrary",), 56),
    )(*_in_hbm(x, dx2, dq, dk, dv, dga, dxl, dgl), ln_gain, *_in_hbm(wt), tabs)


WT_TERMS = 4


def _dwt_scatter(dzt, h, small, tm):
    s = h.shape[0]
    nk = s // tm
    srows = small.shape[0] // NDEV
    last = NDEV - 1
    sm_turn = 2

    def body(order_ref, dz_ref, h_ref, sm_ref, lwt_ref, rep_all, tail_ref, acc, stage, given, relayed, lsm, rep_stage,
             send_sems, recv_sems, local_sem, sm_send, sm_recv, sm_local, rep_send, rep_recv, rep_local):
        j, k = pl.program_id(0), pl.program_id(1)
        x, y, c = _place()
        sibling = (x, y, 1 - c)
        near = (x ^ (1 - c), y ^ c)
        far = (x ^ c, y ^ (1 - c))
        sm_start, sm_finish = _scatter_ops([sm_ref], [lsm], sm_send, sm_recv, sm_local)
        rep_start, rep_pass_on, rep_finish = _gather_ops([rep_stage], [rep_all], rep_send, rep_recv, rep_local)

        def send(step):
            if step == last - 1:
                dst, to = lwt_ref.at[1], sibling
            elif step % 2 == 0:
                dst, to = given.at[step // 2], sibling
            elif step == 1:
                dst, to = relayed, (*near, c)
            else:
                dst, to = lwt_ref.at[1 + step // 2], (*(near if step == 3 else far), c)
            return pltpu.make_async_remote_copy(
                src_ref=stage.at[step % 2], dst_ref=dst, send_sem=send_sems.at[step], recv_sem=recv_sems.at[step],
                device_id=to, device_id_type=MESH)

        def keep():
            return pltpu.make_async_copy(stage.at[last % 2], lwt_ref.at[0], local_sem)

        @pl.when((j == 0) & (k == 0))
        def _():
            sm_start()

        @pl.when(k == 0)
        def _():
            acc[...] = jnp.zeros_like(acc)

        acc[...] += _dot(dz_ref[...], h_ref[...])

        for step in range(NDEV):
            @pl.when((k == nk - 1) & (j == step))
            def _(step=step):
                if step >= 2:
                    send(step - 2).wait_send()
                if step % 2 == 1 and step < last:
                    send(step - 1).wait_recv()
                    total = acc[...] + given[step // 2].astype(f32)
                    if step == 5:
                        send(1).wait_recv()
                        total = total + relayed[...].astype(f32)
                    stage[step % 2] = total.astype(bf16)
                else:
                    stage[step % 2] = acc[...].astype(bf16)
                if step < last:
                    send(step).start()
                else:
                    keep().start()
                    send(last - 1).wait_send()
                    for peer_step in (3, 5, last - 1):
                        send(peer_step).wait_recv()
                    keep().wait()
                    rep_finish()
                if step == sm_turn:
                    sm_finish()
                    total_sm = lsm[0]
                    for dev in range(1, NDEV):
                        total_sm = total_sm + lsm[dev]
                    rep_stage[...] = total_sm[0:SMALL_PER]
                    tail_ref[...] = total_sm[SMALL_PER:]
                    rep_start()
                if step == last - 1:
                    rep_pass_on()

    x, y, c = _place()
    dest = lambda chip, cc: 4 * chip[0] + 2 * chip[1] + cc
    near, far, diag = (x ^ (1 - c), y ^ c), (x ^ c, y ^ (1 - c)), (1 - x, 1 - y)
    order = jnp.stack([dest(diag, 1 - c), dest(diag, c), dest(far, 1 - c), dest(near, c),
                       dest(near, 1 - c), dest(far, c), dest((x, y), 1 - c), dest((x, y), c)])
    return pl.pallas_call(
        body, name="dwt_scatter",
        grid_spec=pltpu.PrefetchScalarGridSpec(
            num_scalar_prefetch=1, grid=(NDEV, nk),
            in_specs=[pl.BlockSpec((WT_ROWS, tm), lambda j, k, order: (order[j], k)),
                      pl.BlockSpec((tm, D), lambda j, k, order: (k, 0)), HBM],
            out_specs=[HBM, HBM, pl.BlockSpec((srows - SMALL_PER, D), lambda j, k, order: (0, 0))],
            scratch_shapes=[pltpu.VMEM((WT_ROWS, D), f32), pltpu.VMEM((2, WT_ROWS, D), bf16),
                            pltpu.VMEM((3, WT_ROWS, D), bf16), pltpu.VMEM((WT_ROWS, D), bf16),
                            pltpu.VMEM((NDEV, srows, D), f32), pltpu.VMEM((SMALL_PER, D), f32),
                            pltpu.SemaphoreType.DMA((last,)), pltpu.SemaphoreType.DMA((last,)),
                            pltpu.SemaphoreType.DMA(())] + _comm_sems(1) + _comm_sems(1)),
        out_shape=[pltpu.HBM((WT_TERMS, WT_ROWS, D), bf16), pltpu.HBM((SMALL_ROWS, D), f32),
                   jax.ShapeDtypeStruct((srows - SMALL_PER, D), f32)],
        compiler_params=_params(("arbitrary", "arbitrary"), 48),
    )(order, *_in_hbm(dzt, h, small))


def _diag_blocks(bd):
    eye = jnp.eye(4, dtype=bd.dtype)
    return jnp.einsum('gjckd,jk->gjcd', bd.reshape(NGRP, 4, HD, 4, HD), eye).reshape(NQ, HD, HD)


def _sequence_step(x, h, tgt, wt, wo_shard, conv_w, wr, wi, p):
    s = x.shape[0]
    tm = min(256, s)
    tabs = _rope_tables(s)
    sinks = p["sinks"].reshape(NQ)
    qt, kt, vt, ga, xl, gl, u, hl, r, ig, wo = _fwd_fused(
        h, wt, tabs, wo_shard, conv_w, p["conv_b"], wr, wi, p["b_rgate"], p["b_igate"], p["lru_lambda"], tm)
    ot = _attn_fwd_t(qt, kt, vt, sinks)
    dx2, dot, dga, dhl, dgl, dwo, g_fg, g_ag, g_lg, loss = _out_fwd_bwd(
        x, tgt, ot, ga, hl, gl, p["attn_out_gain"], p["lru_out_gain"], p["final_gain"], wo, tm)
    dqt, dkt, dvt, dsink, land_wo = _attn_bwd_t(qt, kt, vt, dot, sinks, dwo)
    dxl, dwr, dwi, dbr, dbi, dlam, dcb, dcw = _lru_bwd(u, hl, dhl, xl, r, ig, conv_w, wr, wi, p["lru_lambda"], tm)
    gx, g_ln, dzt = _bwd_in(x, dx2, dqt, dkt, dvt, dga, dxl, dgl, p["ln_gain"], wt, tabs, tm)
    small = dict(ln_gain=g_ln, sinks=dsink.reshape(NQ, BLK).sum(axis=1)[None], conv_w=dcw, conv_b=dcb,
                 w_rgate=_diag_blocks(dwr), b_rgate=dbr, w_igate=_diag_blocks(dwi), b_igate=dbi, lru_lambda=dlam,
                 attn_out_gain=g_ag, lru_out_gain=g_lg, final_gain=g_fg)
    land_wt, g_rep, g_tail = _dwt_scatter(dzt, h, _pack_small(small, loss), min(2048, s))
    return gx, land_wt, land_wo, g_rep, g_tail


def _gather_weights(wt_shard, conv_blk, x, ln_gain, w_rgate, w_igate, tm):
    s = x.shape[0]

    def body(wt_ref, cw_ref, g_ref, wrg_ref, wig_ref, x_ref, wt_all, cw_all, h_ref, wr_ref, wi_ref,
             stage, xbuf, hbuf, send_sems, recv_sems, local_sems):
        stage[...] = wt_ref[...].astype(bf16)
        start, finish = _relay_gather_ops([stage, cw_ref], [wt_all, cw_all], send_sems, recv_sems, local_sems)
        start()
        for src, dst in ((wrg_ref, wr_ref), (wig_ref, wi_ref)):
            dst[...] = jnp.zeros_like(dst)
            for nb in range(NQ):
                g, j = divmod(nb, 4)
                dst[g, HD * j:HD * j + HD, HD * j:HD * j + HD] = src[nb].astype(bf16)
        gain = g_ref[...]
        for i in range(s // tm):
            rows = pl.ds(i * tm, tm)
            pltpu.sync_copy(x_ref.at[rows, :], xbuf)
            xx = xbuf[...]
            rstd = lax.rsqrt(jnp.mean(xx * xx, axis=-1, keepdims=True) + EPS)
            hbuf[...] = (xx * rstd * gain).astype(bf16)
            pltpu.sync_copy(hbuf, h_ref.at[rows, :])
        finish()

    vmem = pl.BlockSpec(memory_space=pltpu.VMEM)
    return pl.pallas_call(
        body, name="gather_weights",
        in_specs=[vmem] * 5 + [HBM], out_specs=[HBM, HBM, HBM, vmem, vmem],
        out_shape=[pltpu.HBM((NIN, D), bf16), pltpu.HBM((NDEV * 8, 128), f32), pltpu.HBM((s, D), bf16)]
        + [jax.ShapeDtypeStruct((NGRP, 256, 256), bf16)] * 2,
        scratch_shapes=[pltpu.VMEM((WT_ROWS, D), bf16), pltpu.VMEM((tm, D), f32), pltpu.VMEM((tm, D), bf16)]
        + _comm_sems(2),
        compiler_params=pltpu.CompilerParams(vmem_limit_bytes=32 * MIB),
    )(wt_shard, conv_blk, ln_gain, w_rgate, w_igate, *_in_hbm(x))


def _adam_math(w, g, m, v):
    m2 = ADAM_B1 * m + (1.0 - ADAM_B1) * g
    v2 = ADAM_B2 * v + (1.0 - ADAM_B2) * (g * g)
    m_hat = m2 / (1.0 - ADAM_B1 ** ADAM_STEP)
    v_hat = v2 / (1.0 - ADAM_B2 ** ADAM_STEP)
    delta = -ADAM_LR * (m_hat / (jnp.sqrt(v_hat) + ADAM_EPS) + ADAM_WD * w)
    return delta, m2, v2


def _reduce_adamw(land, w, m, v, tr, name):
    terms, rows, cols = land.shape

    def body(l_ref, w_ref, m_ref, v_ref, g_ref, d_ref, m2_ref, v2_ref):
        g = l_ref[0].astype(f32)
        for t in range(1, terms):
            g = g + l_ref[t].astype(f32)
        g_ref[...] = g
        d_ref[...], m2_ref[...], v2_ref[...] = _adam_math(w_ref[...], g, m_ref[...], v_ref[...])

    blk = pl.BlockSpec((tr, cols), lambda i: (i, 0))
    return pl.pallas_call(
        body, name=name, grid=(rows // tr,),
        in_specs=[pl.BlockSpec((terms, tr, cols), lambda i: (0, i, 0))] + [blk] * 3, out_specs=[blk] * 4,
        out_shape=[jax.ShapeDtypeStruct((rows, cols), f32)] * 4,
        compiler_params=_params(("arbitrary",), 32),
    )(*_in_hbm(land), w, m, v)


VEC_NAMES = ("ln_gain", "conv_b", "b_rgate", "b_igate", "lru_lambda", "attn_out_gain", "lru_out_gain", "final_gain")
ROW_RGATE, ROW_IGATE, ROW_VEC, ROW_SINKS = 0, 64, 128, 136
LOSS_LANE = NQ


def _adamw_small(g_rep, g_conv, w, m, v):
    names = list(VEC_NAMES) + ["sinks", "conv_w", "w_rgate", "w_igate"]
    ins = [g_rep, g_conv] + [d[k] for k in names for d in (w, m, v)]

    def body(*refs):
        g_ref, gc_ref = refs[0], refs[1]
        in_refs = refs[2:2 + 3 * len(names)]
        out_refs = refs[2 + 3 * len(names):]

        def update(j, g, at=None):
            w_ref, m_ref, v_ref = in_refs[3 * j:3 * j + 3]
            outs = out_refs[4 * j:4 * j + 4]
            pick = (lambda r: r[...]) if at is None else (lambda r: r[at])
            res = (g,) + _adam_math(pick(w_ref), g, pick(m_ref), pick(v_ref))
            for o_ref, val in zip(outs, res):
                if at is None:
                    o_ref[...] = val
                else:
                    o_ref[at] = val

        for j in range(len(VEC_NAMES)):
            update(j, g_ref[ROW_VEC + j:ROW_VEC + j + 1, :])
        update(len(VEC_NAMES), g_ref[ROW_SINKS:ROW_SINKS + 1, 0:NQ])
        update(len(VEC_NAMES) + 1, gc_ref[...], at=0)
        for gi, row0 in ((len(VEC_NAMES) + 2, ROW_RGATE), (len(VEC_NAMES) + 3, ROW_IGATE)):
            for nb in range(NQ):
                update(gi, g_ref[row0:row0 + HD, HD * nb:HD * nb + HD], at=(0, nb))

    vmem = pl.BlockSpec(memory_space=pltpu.VMEM)
    out_shape = [jax.ShapeDtypeStruct(w[k].shape, f32) for k in names for _ in range(4)]
    outs = pl.pallas_call(
        body, name="adamw_small",
        in_specs=[vmem] * len(ins), out_specs=[vmem] * len(out_shape), out_shape=out_shape,
        compiler_params=pltpu.CompilerParams(vmem_limit_bytes=32 * MIB),
    )(*ins)
    return {k: tuple(outs[4 * j:4 * j + 4]) for j, k in enumerate(names)}


def _pack_small(small, loss):
    gate = lambda g: g.transpose(1, 0, 2).reshape(HD, NQ * HD)
    row_s = jnp.concatenate([small["sinks"], loss[:, LOSS_LANE:128], jnp.zeros((1, D - 128), f32)], axis=1)
    rep = jnp.concatenate([gate(small["w_rgate"]), gate(small["w_igate"])] + [small[k] for k in VEC_NAMES]
                          + [row_s, jnp.zeros((SMALL_ROWS - ROW_SINKS - 1, D), f32)], axis=0)
    conv = small["conv_w"].reshape(CONVW, NDEV, 128).transpose(1, 0, 2)
    conv = jnp.pad(conv, ((0, 0), (0, 8 - CONVW), (0, D - 128)))
    return jnp.concatenate([rep.reshape(NDEV, SMALL_PER, D), conv], axis=1).reshape(NDEV * (SMALL_PER + 8), D)


def kernel(x, ln_gain, w_in, sinks, conv_w, conv_b, w_rgate, b_rgate, w_igate, b_igate, lru_lambda, attn_out_gain, lru_out_gain, w_out, final_gain, loss_target, m_ln_gain, m_w_in, m_sinks, m_conv_w, m_conv_b, m_w_rgate, m_b_rgate, m_w_igate, m_b_igate, m_lru_lambda, m_attn_out_gain, m_lru_out_gain, m_w_out, m_final_gain, v_ln_gain, v_w_in, v_sinks, v_conv_w, v_conv_b, v_w_rgate, v_b_rgate, v_w_igate, v_b_igate, v_lru_lambda, v_attn_out_gain, v_lru_out_gain, v_w_out, v_final_gain):
    w = dict(ln_gain=ln_gain, sinks=sinks, conv_w=conv_w, conv_b=conv_b, w_rgate=w_rgate, b_rgate=b_rgate,
             w_igate=w_igate, b_igate=b_igate, lru_lambda=lru_lambda, attn_out_gain=attn_out_gain,
             lru_out_gain=lru_out_gain, final_gain=final_gain.reshape(1, D))
    m = dict(ln_gain=m_ln_gain, sinks=m_sinks, conv_w=m_conv_w, conv_b=m_conv_b, w_rgate=m_w_rgate,
             b_rgate=m_b_rgate, w_igate=m_w_igate, b_igate=m_b_igate, lru_lambda=m_lru_lambda,
             attn_out_gain=m_attn_out_gain, lru_out_gain=m_lru_out_gain, final_gain=m_final_gain.reshape(1, D))
    v = dict(ln_gain=v_ln_gain, sinks=v_sinks, conv_w=v_conv_w, conv_b=v_conv_b, w_rgate=v_w_rgate,
             b_rgate=v_b_rgate, w_igate=v_w_igate, b_igate=v_b_igate, lru_lambda=v_lru_lambda,
             attn_out_gain=v_attn_out_gain, lru_out_gain=v_lru_out_gain, final_gain=v_final_gain.reshape(1, D))

    conv_blk = jnp.pad(conv_w[0], ((0, 8 - CONVW), (0, 0)))
    wt, cw_all, h, wr, wi = _gather_weights(w_in[0].T, conv_blk, x[0], ln_gain, w_rgate[0], w_igate[0],
                                            min(512, x.shape[1]))
    conv_full = cw_all.reshape(NDEV, 8, 128)[:, 0:CONVW].transpose(1, 0, 2).reshape(CONVW, LW)

    p = {k: w[k] for k in w if k not in ("conv_w", "w_rgate", "w_igate")}
    gx, land_wt, land_wo, g_rep, g_tail = _sequence_step(
        x[0], h, loss_target[0], wt, w_out[0], conv_full, wr, wi, p)
    g_conv = g_tail[0:CONVW, 0:128]

    wins = _reduce_adamw(land_wt, w_in[0].T, m_w_in[0].T, v_w_in[0].T, 192, "adamw_w_in")
    g_win, d_win, m_win, v_win = (t.T for t in wins)
    g_wo, d_wo, m_wo, v_wo = _reduce_adamw(land_wo, w_out[0], m_w_out[0], v_w_out[0], 256, "adamw_w_out")
    res = _adamw_small(g_rep, g_conv, w, m, v)
    res["w_in"] = tuple(t[None] for t in (g_win, d_win, m_win, v_win))
    res["w_out"] = tuple(t[None] for t in (g_wo, d_wo, m_wo, v_wo))
    res["final_gain"] = tuple(t.reshape(D) for t in res["final_gain"])

    order = ("ln_gain", "w_in", "sinks", "conv_w", "conv_b", "w_rgate", "b_rgate", "w_igate", "b_igate",
             "lru_lambda", "attn_out_gain", "lru_out_gain", "w_out", "final_gain")
    total_loss = g_rep[ROW_SINKS, LOSS_LANE]
    return (total_loss, gx[None]) + tuple(res[k][i] for i in range(4) for k in order)
```

```python
import jax
import jax.numpy as jnp
from jax import lax
from jax.experimental import pallas as pl
from jax.experimental.pallas import tpu as pltpu

f32 = jnp.float32
bf16 = jnp.bfloat16

D = 1024
HD = 64
NQ = 16
NKV = 4
GROUP = NQ // NKV
KVW = NKV * HD
BLK = 128
ROT = 16
THETA = 500000.0
NEG = -1e30
LW = 1024
NGRP = 4
CONVW = 4
LRU_C = 8.0
NIN = 4608
EPS = 1e-6
NDEV = 8
WT_ROWS = NIN // NDEV
WO_ROWS = 2 * D // NDEV
SMALL_ROWS = 192
SMALL_PER = SMALL_ROWS // NDEV

ADAM_LR = 0.001
ADAM_B1 = 0.9
ADAM_B2 = 0.999
ADAM_EPS = 1e-08
ADAM_WD = 0.01
ADAM_STEP = 10

NT = (((1,), (1,)), ((), ()))
TN = (((0,), (0,)), ((), ()))
MESH = pl.DeviceIdType.MESH
MIB = 1024 * 1024


def _dot(a, b):
    return jnp.dot(a, b, preferred_element_type=f32)


def _dot_nt(a, b):
    return lax.dot_general(a, b, NT, preferred_element_type=f32)


def _dot_tn(a, b):
    return lax.dot_general(a, b, TN, preferred_element_type=f32)


def _params(sem, vmem_mib):
    return pltpu.CompilerParams(dimension_semantics=sem, vmem_limit_bytes=vmem_mib * MIB)


def _sigmoid(x):
    return 0.5 * jnp.tanh(0.5 * x) + 0.5


def _softplus(x):
    return jnp.maximum(x, 0.0) + jnp.log(1.0 + jnp.exp(-jnp.abs(x)))


def _rope_tables(s):
    pos = jnp.arange(s, dtype=f32)
    inv_freq = THETA ** (-jnp.arange(0, ROT, 2, dtype=f32) / ROT)
    ang = pos[:, None] * inv_freq[None, :]
    cs = jnp.concatenate([jnp.cos(ang) - 1.0, jnp.sin(ang)], axis=1)
    d = jnp.arange(128) % HD
    j = jnp.arange(ROT)[:, None]
    pick_c = ((d < ROT) & (j == d % (ROT // 2))).astype(f32)
    pick_sa = ((d >= ROT // 2) & (d < ROT) & (j == d)).astype(f32)
    pick_sb = -((d < ROT // 2) & (j == d + ROT // 2)).astype(f32)
    picks = jnp.concatenate([pick_c, pick_sa, pick_sb], axis=1)
    ones = jnp.concatenate([jnp.ones((1, 128), f32), jnp.zeros((1, 256), f32)], axis=1)
    return jnp.dot(cs, picks, precision=lax.Precision.HIGHEST) + ones


def _tables(tab_ref):
    return tab_ref[:, 0:128], tab_ref[:, 128:256], tab_ref[:, 256:384]


def _rope(t, c, sa, sb):
    return t * c + pltpu.roll(t, 8, 1) * sa + pltpu.roll(t, 120, 1) * sb


def _unrope_t(dr, c, sa, sb):
    return dr * c + pltpu.roll(dr * sa, 120, 0) + pltpu.roll(dr * sb, 8, 0)


def _place():
    return lax.axis_index("x"), lax.axis_index("y"), lax.axis_index("c")


def _gather_ops(mine_refs, out_refs, send_sems, recv_sems, local_sems):
    n = len(mine_refs)
    x, y, c = _place()
    me, sibling = (x, y, c), (x, y, 1 - c)
    chips = [(1 - x, y), (x, 1 - y), (1 - x, 1 - y)]

    def rows(a, dev):
        m = mine_refs[a].shape[0]
        return out_refs[a].at[pl.ds((4 * dev[0] + 2 * dev[1] + dev[2]) * m, m), :]

    def copy(a, k, block, to, own=False):
        return pltpu.make_async_remote_copy(
            src_ref=mine_refs[a] if own else rows(a, block), dst_ref=rows(a, block),
            send_sem=send_sems.at[a, k], recv_sem=recv_sems.at[a, k], device_id=to, device_id_type=MESH)

    def local(a):
        return pltpu.make_async_copy(mine_refs[a], rows(a, me), local_sems.at[a])

    def first(a):
        return [copy(a, 0, me, sibling, own=True)] + [copy(a, 1 + j, me, (*chip, c), own=True)
                                                      for j, chip in enumerate(chips)]

    def start():
        for a in range(n):
            local(a).start()
            for cp in first(a):
                cp.start()

    def pass_on():
        for j, chip in enumerate(chips):
            for a in range(n):
                copy(a, 1 + j, (*chip, c), me).wait_recv()
                copy(a, 4 + j, (*chip, c), sibling).start()

    def finish():
        for a in range(n):
            copy(a, 0, sibling, me).wait_recv()
            for j, chip in enumerate(chips):
                copy(a, 4 + j, (*chip, 1 - c), me).wait_recv()
        for a in range(n):
            for cp in first(a) + [copy(a, 4 + j, (*chip, c), sibling) for j, chip in enumerate(chips)]:
                cp.wait_send()
            local(a).wait()

    return start, pass_on, finish


def _relay_gather_ops(mine_refs, out_refs, send_sems, recv_sems, local_sems):
    n = len(mine_refs)
    x, y, c = _place()
    me, sibling = (x, y, c), (x, y, 1 - c)
    near = (x ^ (1 - c), y ^ c)
    far = (x ^ c, y ^ (1 - c))
    diag = (1 - x, 1 - y)

    def rows(a, dev):
        m = mine_refs[a].shape[0]
        return out_refs[a].at[pl.ds((4 * dev[0] + 2 * dev[1] + dev[2]) * m, m), :]

    def copy(a, k, block, to, own=False):
        return pltpu.make_async_remote_copy(
            src_ref=mine_refs[a] if own else rows(a, block), dst_ref=rows(a, block),
            send_sem=send_sems.at[a, k], recv_sem=recv_sems.at[a, k], device_id=to, device_id_type=MESH)

    def local(a):
        return pltpu.make_async_copy(mine_refs[a], rows(a, me), local_sems.at[a])

    def sends(a):
        return [copy(a, 0, me, sibling, own=True), copy(a, 1, me, (*near, c), own=True),
                copy(a, 2, me, (*far, c), own=True), copy(a, 3, (*near, c), (*far, c)),
                copy(a, 4, (*near, c), sibling), copy(a, 5, (*far, c), sibling), copy(a, 6, (*diag, c), sibling)]

    def arrivals(a):
        return [copy(a, 0, sibling, me), copy(a, 1, (*near, c), me), copy(a, 2, (*far, c), me),
                copy(a, 3, (*diag, c), me), copy(a, 4, (*far, 1 - c), me), copy(a, 5, (*near, 1 - c), me),
                copy(a, 6, (*diag, 1 - c), me)]

    def start():
        for a in range(n):
            local(a).start()
            for cp in sends(a)[0:3]:
                cp.start()

    def finish():
        for first, then in ((1, (3, 4)), (2, (5,)), (3, (6,))):
            for a in range(n):
                arrivals(a)[first].wait_recv()
                for k in then:
                    sends(a)[k].start()
        for a in range(n):
            for k in (0, 4, 5, 6):
                arrivals(a)[k].wait_recv()
        for a in range(n):
            for cp in sends(a):
                cp.wait_send()
            local(a).wait()

    return start, finish


def _scatter_ops(src_refs, land_refs, send_sems, recv_sems, local_sems):
    n = len(src_refs)
    x, y, c = _place()
    my = 4 * x + 2 * y + c

    def peer(k):
        return x ^ (k >> 2), y ^ ((k >> 1) & 1), c ^ (k & 1)

    def piece(a, dev):
        m = src_refs[a].shape[0] // NDEV
        return src_refs[a].at[pl.ds(dev * m, m), :]

    def local(a):
        return pltpu.make_async_copy(piece(a, my), land_refs[a].at[my], local_sems.at[a])

    def send(a, k):
        px, py, pc = peer(k)
        return pltpu.make_async_remote_copy(
            src_ref=piece(a, 4 * px + 2 * py + pc), dst_ref=land_refs[a].at[my],
            send_sem=send_sems.at[a, k - 1], recv_sem=recv_sems.at[a, k - 1],
            device_id=(px, py, pc), device_id_type=MESH)

    def arrival(a, k):
        px, py, pc = peer(k)
        return pltpu.make_async_remote_copy(
            src_ref=piece(a, my), dst_ref=land_refs[a].at[4 * px + 2 * py + pc],
            send_sem=send_sems.at[a, k - 1], recv_sem=recv_sems.at[a, k - 1],
            device_id=(px, py, pc), device_id_type=MESH)

    def start():
        for a in range(n):
            local(a).start()
        for k in range(1, NDEV):
            for a in range(n):
                send(a, k).start()

    def finish():
        for k in range(1, NDEV):
            for a in range(n):
                send(a, k).wait_send()
        for k in range(1, NDEV):
            for a in range(n):
                arrival(a, k).wait_recv()
        for a in range(n):
            local(a).wait()

    return start, finish


def _in_hbm(*arrays):
    return tuple(pltpu.with_memory_space_constraint(a, pltpu.HBM) for a in arrays)


def _comm_sems(n):
    return [pltpu.SemaphoreType.DMA((n, 7)), pltpu.SemaphoreType.DMA((n, 7)), pltpu.SemaphoreType.DMA((n,))]


HBM = pl.BlockSpec(memory_space=pltpu.HBM)


def _sink_rows(sinks):
    return jnp.repeat(sinks.reshape(NKV, GROUP), BLK, axis=1)


def _band_softmax(s2_ref, ls, prev_offset, sink_row):
    jj = lax.broadcasted_iota(jnp.int32, (BLK, BLK), 0)
    ii = lax.broadcasted_iota(jnp.int32, (BLK, BLK), 1)
    from_prev = jj > ii
    sc = jnp.where(from_prev, s2_ref[0:BLK, ls] + prev_offset, s2_ref[BLK:2 * BLK, ls])
    m = jnp.maximum(jnp.max(sc, axis=0, keepdims=True), sink_row)
    p = jnp.exp(sc - m)
    es = jnp.exp(sink_row - m)
    inv = 1.0 / (jnp.sum(p, axis=0, keepdims=True) + es)
    return from_prev, p * inv, es * inv


def _put_split(dst_ref, ls, t, from_prev):
    t = t.astype(bf16)
    zero = jnp.zeros_like(t)
    dst_ref[0:BLK, ls] = jnp.where(from_prev, t, zero)
    dst_ref[BLK:2 * BLK, ls] = jnp.where(from_prev, zero, t)


def _heads_side_by_side(ref, h):
    return jnp.concatenate([ref[HD * (GROUP * h + g):HD * (GROUP * h + g) + HD, :] for g in range(GROUP)], axis=1)


def _kv_specs_t():
    prev = pl.BlockSpec((KVW, BLK), lambda n: (0, jnp.maximum(n - 1, 0)))
    cur = pl.BlockSpec((KVW, BLK), lambda n: (0, n))
    return [prev, cur, prev, cur]


def _attn_fwd_t(qt, kt, vt, sinks):
    s = qt.shape[1]

    def body(sink_ref, q_ref, kp_ref, kc_ref, vp_ref, vc_ref, o_ref, s2_scr, pn2_scr):
        n = pl.program_id(0)
        off = jnp.where(n > 0, 0.0, NEG)

        def scores(h):
            hs = slice(HD * h, HD * h + HD)
            kh = jnp.concatenate([kp_ref[hs, :], kc_ref[hs, :]], axis=1)
            s2_scr[h % 2] = _dot_tn(kh, _heads_side_by_side(q_ref, h))

        def probs(h):
            for g in range(GROUP):
                ls = slice(BLK * g, BLK * g + BLK)
                from_prev, pn, _ = _band_softmax(s2_scr.at[h % 2], ls, off, sink_ref[h:h + 1, ls])
                _put_split(pn2_scr.at[h % 2], ls, pn, from_prev)

        def outputs(h):
            hs = slice(HD * h, HD * h + HD)
            vh = jnp.concatenate([vp_ref[hs, :], vc_ref[hs, :]], axis=1)
            og = _dot(vh, pn2_scr[h % 2])
            for g in range(GROUP):
                a = GROUP * h + g
                o_ref[HD * a:HD * a + HD, :] = og[:, BLK * g:BLK * g + BLK]

        scores(0)
        for h in range(NKV):
            if h + 1 < NKV:
                scores(h + 1)
            probs(h)
            outputs(h)

    return pl.pallas_call(
        body, name="attn_fwd", grid=(s // BLK,),
        in_specs=[pl.BlockSpec((NKV, GROUP * BLK), lambda n: (0, 0)), pl.BlockSpec((D, BLK), lambda n: (0, n))]
        + _kv_specs_t(),
        out_specs=pl.BlockSpec((D, BLK), lambda n: (0, n)),
        out_shape=pltpu.HBM((D, s), f32),
        scratch_shapes=[pltpu.VMEM((2, 2 * BLK, GROUP * BLK), f32), pltpu.VMEM((2, 2 * BLK, GROUP * BLK), bf16)],
        compiler_params=_params(("arbitrary",), 32),
    )(_sink_rows(sinks), *_in_hbm(qt, kt, kt, vt, vt))


def _attn_bwd_t(qt, kt, vt, dot, sinks, dwo):
    s = qt.shape[1]
    nb = s // BLK

    def body(sink_ref, q_ref, do_ref, kp_ref, kc_ref, vp_ref, vc_ref, dwo_ref, dq_ref, dk_ref, dv_ref, ds_ref,
             land_ref, dk_hold, dv_hold, s2_scr, dp2_scr, pn2_scr, ds2_scr, send_sems, recv_sems, local_sems):
        n = pl.program_id(0)
        start, finish = _scatter_ops([dwo_ref], [land_ref], send_sems, recv_sems, local_sems)

        @pl.when(n == 0)
        def _():
            start()
            dk_hold[...] = jnp.zeros_like(dk_hold)
            dv_hold[...] = jnp.zeros_like(dv_hold)
            ds_ref[...] = jnp.zeros_like(ds_ref)

        @pl.when(n < nb)
        def _():
            off = jnp.where(n > 0, 0.0, NEG)

            def scores(h):
                hs = slice(HD * h, HD * h + HD)
                kh = jnp.concatenate([kp_ref[hs, :], kc_ref[hs, :]], axis=1)
                vh = jnp.concatenate([vp_ref[hs, :], vc_ref[hs, :]], axis=1)
                s2_scr[h % 2] = _dot_tn(kh, _heads_side_by_side(q_ref, h))
                dp2_scr[h % 2] = _dot_tn(vh, _heads_side_by_side(do_ref, h))

            def softmax_bwd(h):
                for g in range(GROUP):
                    ls = slice(BLK * g, BLK * g + BLK)
                    from_prev, pn, ps = _band_softmax(s2_scr.at[h % 2], ls, off, sink_ref[h:h + 1, ls])
                    dp = jnp.where(from_prev, dp2_scr[h % 2, 0:BLK, ls], dp2_scr[h % 2, BLK:2 * BLK, ls])
                    dsum = jnp.sum(pn * dp, axis=0, keepdims=True)
                    ds_ref[h:h + 1, ls] += -ps * dsum
                    _put_split(pn2_scr.at[h % 2], ls, pn, from_prev)
                    _put_split(ds2_scr.at[h % 2], ls, pn * (dp - dsum), from_prev)

            def grads(h):
                hs = slice(HD * h, HD * h + HD)
                kh = jnp.concatenate([kp_ref[hs, :], kc_ref[hs, :]], axis=1)
                dqg = _dot(kh, ds2_scr[h % 2])
                for g in range(GROUP):
                    a = GROUP * h + g
                    dq_ref[HD * a:HD * a + HD, :] = dqg[:, BLK * g:BLK * g + BLK]
                dkh = _dot_nt(_heads_side_by_side(q_ref, h), ds2_scr[h % 2])
                dvh = _dot_nt(_heads_side_by_side(do_ref, h), pn2_scr[h % 2])
                dk_ref[hs, :] = dk_hold[hs, :] + dkh[:, 0:BLK]
                dv_ref[hs, :] = dv_hold[hs, :] + dvh[:, 0:BLK]
                dk_hold[hs, :] = dkh[:, BLK:2 * BLK]
                dv_hold[hs, :] = dvh[:, BLK:2 * BLK]

            scores(0)
            for h in range(NKV):
                if h + 1 < NKV:
                    scores(h + 1)
                softmax_bwd(h)
                grads(h)

        @pl.when(n == nb)
        def _():
            dk_ref[...] = dk_hold[...]
            dv_ref[...] = dv_hold[...]
            finish()

    blk = pl.BlockSpec((D, BLK), lambda n: (0, jnp.minimum(n, nb - 1)))
    late = pl.BlockSpec((KVW, BLK), lambda n: (0, jnp.maximum(n - 1, 0)))
    whole = pl.BlockSpec((NKV, GROUP * BLK), lambda n: (0, 0))
    kv = [pl.BlockSpec((KVW, BLK), lambda n: (0, jnp.clip(n - 1, 0, nb - 1))),
          pl.BlockSpec((KVW, BLK), lambda n: (0, jnp.minimum(n, nb - 1)))]
    return pl.pallas_call(
        body, name="attn_bwd", grid=(nb + 1,),
        in_specs=[whole, blk, blk] + kv + kv + [HBM],
        out_specs=[blk, late, late, whole, HBM],
        out_shape=[pltpu.HBM((D, s), f32), pltpu.HBM((KVW, s), f32), pltpu.HBM((KVW, s), f32),
                   jax.ShapeDtypeStruct((NKV, GROUP * BLK), f32), pltpu.HBM((NDEV, WO_ROWS, D), bf16)],
        scratch_shapes=[pltpu.VMEM((KVW, BLK), f32), pltpu.VMEM((KVW, BLK), f32)]
        + [pltpu.VMEM((2, 2 * BLK, GROUP * BLK), f32)] * 2 + [pltpu.VMEM((2, 2 * BLK, GROUP * BLK), bf16)] * 2
        + _comm_sems(1),
        compiler_params=_params(("arbitrary",), 48),
    )(_sink_rows(sinks), *_in_hbm(qt, dot, kt, kt, vt, vt, dwo))


def _decay_terms(r, sp):
    a = jnp.exp(r * (-LRU_C * sp))
    n = r * (2.0 * LRU_C * sp)
    y = jnp.where(n < 0.02, n * (1.0 - n * (0.5 - n * (1.0 / 6.0))), 1.0 - a * a)
    inv_mult = lax.rsqrt(jnp.maximum(y, 1e-30))
    return a, y * inv_mult, inv_mult


def _later(x, before, k):
    if k == 0:
        return x
    row = lax.broadcasted_iota(jnp.int32, before.shape, 0)
    rolled = pltpu.roll(x, k, 0)
    first = jnp.where(row < k, pltpu.roll(before, k, 0), rolled[0:8])
    return jnp.concatenate([first, rolled[8:]], axis=0)


def _earlier(x, after, k):
    if k == 0:
        return x
    n = x.shape[0]
    row = lax.broadcasted_iota(jnp.int32, after.shape, 0)
    rolled = pltpu.roll(x, n - k, 0)
    last = jnp.where(row >= 8 - k, pltpu.roll(after, 8 - k, 0), rolled[n - 8:n])
    return jnp.concatenate([rolled[0:n - 8], last], axis=0)


def _fwd_fused(h, wt, tabs, wo_shard, conv_w, conv_b, wr, wi, br, bi, lam, tm):
    s = h.shape[0]
    nt = s // tm
    nc = 512
    pieces = 8
    rows_per = tm // pieces
    later_chunks = (0, 1, 2, 3, 4, 7, 8)

    def body(h_ref, wt_ref, tab_ref, wo_ref, cw_ref, cb_ref, wr_ref, wi_ref, br_ref,
             bi_ref, lam_ref, q_ref, k_ref, v_ref, ga_ref, xl_ref, gl_ref, u_ref, hl_ref, r_ref, ig_ref,
             wo_all, wo_stage, halo, ub_scr, pr_scr, pi_scr, b_scr, a_scr, hcar,
             send_sems, recv_sems, local_sems):
        i = pl.program_id(0)
        start, pass_on, finish = _gather_ops([wo_stage], [wo_all], send_sems, recv_sems, local_sems)

        @pl.when(i == 0)
        def _():
            wo_stage[...] = wo_ref[...].astype(bf16)
            start()
            halo[...] = jnp.zeros_like(halo)
            hcar[...] = jnp.zeros_like(hcar)

        sp = _softplus(-lam_ref[...])
        br, bi = br_ref[...], bi_ref[...]
        c, sa, sb = _tables(tab_ref)
        piece_rows = lambda p: slice(rows_per * p, rows_per * p + rows_per)

        def project(ci):
            z = _dot_nt(h_ref[...], wt_ref[ci * nc:(ci + 1) * nc, :])
            if ci < 2:
                for j in range(nc // 128):
                    r = _rope(z[:, 128 * j:128 * j + 128], c, sa, sb) * (HD ** -0.5)
                    q_ref[ci * nc + 128 * j:ci * nc + 128 * j + 128, :] = r.astype(bf16).T
            elif ci == 2:
                for j in range(2):
                    js = slice(128 * j, 128 * j + 128)
                    k_ref[js, :] = _rope(z[:, js], c, sa, sb).astype(bf16).T
                    v_ref[js, :] = z[:, KVW + 128 * j:KVW + 128 * j + 128].astype(bf16).T
            else:
                sec, j = divmod(ci - 3, 2)
                (ga_ref, xl_ref, gl_ref)[sec][:, j * nc:(j + 1) * nc] = z

        def gate_terms(p):
            rows = piece_rows(p)
            r = _sigmoid(pr_scr[rows, :] + br)
            ig = _sigmoid(pi_scr[rows, :] + bi)
            a, mult, _ = _decay_terms(r, sp)
            r_ref[rows, :] = r
            ig_ref[rows, :] = ig
            a_scr[rows, :] = a
            b_scr[rows, :] = mult * (ig * u_ref[rows, :])

        def scan(p, hc):
            for t in range(rows_per * p, rows_per * p + rows_per):
                hc = a_scr[t:t + 1, :] * hc + b_scr[t:t + 1, :]
                hl_ref[t:t + 1, :] = hc
            return hc

        project(5)
        project(6)
        xl = xl_ref[...]
        u = cb_ref[...] + sum(cw_ref[k:k + 1, :] * _later(xl, halo[...], CONVW - 1 - k) for k in range(CONVW))
        halo[...] = xl[tm - 8:tm, :]
        u_ref[...] = u
        ub_scr[...] = u.astype(bf16)
        for g in range(NGRP):
            gs = slice(256 * g, 256 * g + 256)
            pr_scr[:, gs] = _dot(ub_scr[:, gs], wr_ref[g])
            pi_scr[:, gs] = _dot(ub_scr[:, gs], wi_ref[g])
        hc = hcar[...]
        gate_terms(0)
        for slot, ci in enumerate(later_chunks):
            project(ci)
            gate_terms(slot + 1)
            hc = scan(slot, hc)
        hcar[...] = scan(pieces - 1, hc)

        @pl.when(i == max(nt - 2, 0))
        def _():
            pass_on()

        @pl.when(i == nt - 1)
        def _():
            finish()

    row = lambda w: pl.BlockSpec((tm, w), lambda i: (i, 0))
    col = lambda w: pl.BlockSpec((w, tm), lambda i: (0, i))
    full = lambda a: pl.BlockSpec(a.shape, lambda i: (0,) * a.ndim)
    big = lambda w, dt: pltpu.HBM((s, w), dt)
    tile = pltpu.VMEM((tm, LW), f32)
    return pl.pallas_call(
        body, name="fwd_fused", grid=(nt,),
        in_specs=[row(D), full(wt), row(384), full(wo_shard), full(conv_w), full(conv_b),
                  full(wr), full(wi), full(br), full(bi), full(lam)],
        out_specs=[col(D), col(KVW), col(KVW), row(D), row(D), row(D)] + [row(LW)] * 4 + [HBM],
        out_shape=[pltpu.HBM((D, s), bf16), pltpu.HBM((KVW, s), bf16), pltpu.HBM((KVW, s), bf16),
                   big(D, f32), big(D, f32), big(D, f32)] + [big(LW, f32)] * 4 + [pltpu.HBM((2 * D, D), bf16)],
        scratch_shapes=[pltpu.VMEM((WO_ROWS, D), bf16), pltpu.VMEM((8, LW), f32), pltpu.VMEM((tm, LW), bf16)]
        + [tile] * 4 + [pltpu.VMEM((1, LW), f32)] + _comm_sems(1),
        compiler_params=_params(("arbitrary",), 56),
    )(*_in_hbm(h, wt), tabs, wo_shard, conv_w, conv_b, wr, wi, br, bi, lam)


def _lru_bwd(u, hl, dhl, xl, r, ig, conv_w, wr, wi, lam, tm):
    s = u.shape[0]
    nt = s // tm
    pieces = 8
    rows_per = tm // pieces

    def body(u_ref, h_ref, hp_ref, dh_ref, x_ref, r_ref, ig_ref, cw_ref, wr_ref, wi_ref,
             lam_ref, dxl_ref, dwr_ref, dwi_ref, dbr_ref, dbi_ref, dlam_ref, dcb_ref, dcw_ref,
             l_scr, du_scr, a_scr, mu_scr, im_scr, dpr_scr, dpi_scr, lcar, dunext):
        t0 = pl.program_id(0)
        tile = nt - 1 - t0

        @pl.when(t0 == 0)
        def _():
            lcar[...] = jnp.zeros_like(lcar)
            dunext[...] = jnp.zeros_like(dunext)
            for ref in (dwr_ref, dwi_ref, dbr_ref, dbi_ref, dlam_ref, dcb_ref, dcw_ref):
                ref[...] = jnp.zeros_like(ref)

        lam = lam_ref[...]
        sp = _softplus(-lam)
        hp = jnp.where(tile > 0, hp_ref[...], 0.0)

        def decay(p):
            rows = slice(rows_per * p, rows_per * p + rows_per)
            a_scr[rows, :], mu_scr[rows, :], im_scr[rows, :] = _decay_terms(r_ref[rows, :], sp)

        def scan(p, c):
            for t in range(rows_per * p + rows_per - 1, rows_per * p - 1, -1):
                lt = dh_ref[t:t + 1, :] + c
                l_scr[t:t + 1, :] = lt
                c = a_scr[t:t + 1, :] * lt
            return c

        def terms(p, sums):
            rows = slice(rows_per * p, rows_per * p + rows_per)
            lt, u, r, i, a, mult, inv_mult = l_scr[rows, :], u_ref[rows, :], r_ref[rows, :], ig_ref[rows, :], \
                a_scr[rows, :], mu_scr[rows, :], im_scr[rows, :]
            before = hp if p == 0 else h_ref[rows_per * p - 8:rows_per * p, :]
            hprev = _later(h_ref[rows, :], before, 1)
            iu = i * u
            lm = lt * mult
            du_scr[rows, :] = lm * i
            dla = (lt * hprev) * a - ((lt * iu) * (a * a)) * inv_mult
            dlar = dla * r
            dpr = (dlar * (1.0 - r)) * (-LRU_C * sp)
            dpi = (lm * iu) * (1.0 - i)
            dpr_scr[rows, :] = dpr.astype(bf16)
            dpi_scr[rows, :] = dpi.astype(bf16)
            col = lambda t: jnp.sum(t, axis=0, keepdims=True)
            return sums[0] + col(dlar), sums[1] + col(dpr), sums[2] + col(dpi)

        sums = (jnp.zeros((1, LW), f32),) * 3
        decay(pieces - 1)
        c = scan(pieces - 1, lcar[...])
        for p in range(pieces - 1, -1, -1):
            if p > 0:
                decay(p - 1)
                c = scan(p - 1, c)
            sums = terms(p, sums)
        lcar[...] = c
        dlam_ref[...] += sums[0] * (-LRU_C)
        dbr_ref[...] += sums[1]
        dbi_ref[...] += sums[2]

        ub = u_ref[...].astype(bf16)
        dug = []
        for g in range(NGRP):
            gs = slice(256 * g, 256 * g + 256)
            dwr_ref[g] += _dot_tn(ub[:, gs], dpr_scr[:, gs])
            dwi_ref[g] += _dot_tn(ub[:, gs], dpi_scr[:, gs])
            dug.append(_dot_nt(dpr_scr[:, gs], wr_ref[g]) + _dot_nt(dpi_scr[:, gs], wi_ref[g]))
        du = du_scr[...] + jnp.concatenate(dug, axis=1)

        dcb_ref[...] += jnp.sum(du, axis=0, keepdims=True)
        x = x_ref[...]
        after = dunext[...]
        dxl = jnp.zeros_like(du)
        for k in range(CONVW):
            e = _earlier(du, after, CONVW - 1 - k)
            dxl = dxl + cw_ref[k:k + 1, :] * e
            dcw_ref[k:k + 1, :] += jnp.sum(e * x, axis=0, keepdims=True)
        dxl_ref[...] = dxl.astype(bf16)
        dunext[...] = du[0:8, :]

        @pl.when(t0 == nt - 1)
        def _():
            dlam_ref[...] = dlam_ref[...] * (-_sigmoid(-lam))

    rev = lambda i: (nt - 1 - i, 0)
    row = pl.BlockSpec((tm, LW), rev)
    prev8 = pl.BlockSpec((8, LW), lambda i: (jnp.maximum((nt - 1 - i) * (tm // 8) - 1, 0), 0))
    full = lambda a: pl.BlockSpec(a.shape, lambda i: (0,) * a.ndim)
    vec = pl.BlockSpec((1, LW), lambda i: (0, 0))
    bd = pl.BlockSpec((NGRP, 256, 256), lambda i: (0, 0, 0))
    return pl.pallas_call(
        body, name="lru_bwd", grid=(nt,),
        in_specs=[row, row, prev8] + [row] * 4 + [full(conv_w), full(wr), full(wi), full(lam)],
        out_specs=[row, bd, bd, vec, vec, vec, vec, pl.BlockSpec((CONVW, LW), lambda i: (0, 0))],
        out_shape=[pltpu.HBM((s, LW), bf16),
                   jax.ShapeDtypeStruct((NGRP, 256, 256), f32), jax.ShapeDtypeStruct((NGRP, 256, 256), f32),
                   jax.ShapeDtypeStruct((1, LW), f32), jax.ShapeDtypeStruct((1, LW), f32),
                   jax.ShapeDtypeStruct((1, LW), f32), jax.ShapeDtypeStruct((1, LW), f32),
                   jax.ShapeDtypeStruct((CONVW, LW), f32)],
        scratch_shapes=[pltpu.VMEM((tm, LW), f32)] * 5 + [pltpu.VMEM((tm, LW), bf16)] * 2
        + [pltpu.VMEM((1, LW), f32), pltpu.VMEM((8, LW), f32)],
        compiler_params=_params(("arbitrary",), 56),
    )(*_in_hbm(u, hl, hl, dhl, xl, r, ig), conv_w, wr, wi, lam)


def _gated_norm(t, gate, gain):
    sg = _sigmoid(gate)
    silu = gate * sg
    p = t * silu
    rstd = lax.rsqrt(jnp.mean(p * p, axis=-1, keepdims=True) + EPS)
    ph = p * rstd
    return sg, silu, rstd, ph, ph * gain


def _gated_norm_bwd(dy, t, gate, gain, sg, silu, rstd, ph):
    w = dy * gain
    dp = rstd * (w - ph * jnp.mean(w * ph, axis=-1, keepdims=True))
    dgate = (dp * t) * (sg + silu * (1.0 - sg))
    return jnp.sum(dy * ph, axis=0, keepdims=True), dp * silu, dgate


def _out_fwd_bwd(x, tgt, o, ga, hl, gl, again, lgain, fgain, wo, tm):
    s = x.shape[0]
    nt = s // tm

    def body(x_ref, t_ref, o_ref, ga_ref, hl_ref, gl_ref, ag_ref, lg_ref, fg_ref, wo_ref,
             dx2_ref, do_ref, dga_ref, dhl_ref, dgl_ref, dwo_ref, gfg_ref, gag_ref, glg_ref, loss_ref, acc):
        i = pl.program_id(0)

        @pl.when(i == 0)
        def _():
            acc[...] = jnp.zeros_like(acc)
            for ref in (gfg_ref, gag_ref, glg_ref, loss_ref):
                ref[...] = jnp.zeros_like(ref)

        oo = jnp.concatenate([o_ref[128 * j:128 * j + 128, :].T for j in range(D // 128)], axis=1)
        gga, hh, ggl = ga_ref[...], hl_ref[...], gl_ref[...]
        ag, lg, fg = ag_ref[...], lg_ref[...], fg_ref[...]
        sga, silua, ra, pah, ya = _gated_norm(oo, gga, ag)
        sgl, silul, rl, plh, yl = _gated_norm(hh, ggl, lg)
        yab, ylb = ya.astype(bf16), yl.astype(bf16)
        y = _dot(yab, wo_ref[0:D, :]) + _dot(ylb, wo_ref[D:2 * D, :])
        x2 = x_ref[...] + y
        r2 = lax.rsqrt(jnp.mean(x2 * x2, axis=-1, keepdims=True) + EPS)
        x2h = x2 * r2
        err = x2h * fg - t_ref[...]
        loss_ref[...] += 0.5 * jnp.sum(jnp.sum(err * err, axis=-1, keepdims=True) * (1.0 / D))
        gfg_ref[...] += jnp.sum(err * x2h, axis=0, keepdims=True) * (1.0 / D)
        w = err * (fg * (1.0 / D))
        dx2 = r2 * (w - x2h * jnp.mean(w * x2h, axis=-1, keepdims=True))
        dx2_ref[...] = dx2
        dyb = dx2.astype(bf16)
        acc[0:D, :] += _dot_tn(yab, dyb)
        acc[D:2 * D, :] += _dot_tn(ylb, dyb)
        dya = _dot_nt(dyb, wo_ref[0:D, :])
        dyl = _dot_nt(dyb, wo_ref[D:2 * D, :])
        gag, do, dga = _gated_norm_bwd(dya, oo, gga, ag, sga, silua, ra, pah)
        glg, dhl, dgl = _gated_norm_bwd(dyl, hh, ggl, lg, sgl, silul, rl, plh)
        gag_ref[...] += gag
        glg_ref[...] += glg
        dob = do.astype(bf16)
        for j in range(D // 128):
            do_ref[128 * j:128 * j + 128, :] = dob[:, 128 * j:128 * j + 128].T
        dga_ref[...] = dga.astype(bf16)
        dhl_ref[...] = dhl
        dgl_ref[...] = dgl.astype(bf16)

        @pl.when(i == nt - 1)
        def _():
            dwo_ref[...] = acc[...].astype(bf16)

    row = pl.BlockSpec((tm, D), lambda i: (i, 0))
    col = pl.BlockSpec((D, tm), lambda i: (0, i))
    vec = pl.BlockSpec((1, D), lambda i: (0, 0))
    mat = pl.BlockSpec((2 * D, D), lambda i: (0, 0))
    return pl.pallas_call(
        body, name="out_fwd_bwd", grid=(nt,),
        in_specs=[row, row, col, row, row, row] + [vec] * 3 + [mat],
        out_specs=[row, col, row, row, row] + [mat, vec, vec, vec, pl.BlockSpec((1, 128), lambda i: (0, 0))],
        out_shape=[pltpu.HBM((s, D), f32), pltpu.HBM((D, s), bf16),
                   pltpu.HBM((s, D), bf16), pltpu.HBM((s, D), f32),
                   pltpu.HBM((s, D), bf16), pltpu.HBM((2 * D, D), bf16),
                   jax.ShapeDtypeStruct((1, D), f32), jax.ShapeDtypeStruct((1, D), f32),
                   jax.ShapeDtypeStruct((1, D), f32), jax.ShapeDtypeStruct((1, 128), f32)],
        scratch_shapes=[pltpu.VMEM((2 * D, D), f32)],
        compiler_params=_params(("arbitrary",), 56),
    )(*_in_hbm(x, tgt, o, ga, hl, gl), again, lgain, fgain, *_in_hbm(wo))


def _bwd_in(x, dx2, dq, dk, dv, dga, dxl, dgl, ln_gain, wt, tabs, tm):
    s = x.shape[0]

    def body(x_ref, dx2_ref, dq_ref, dk_ref, dv_ref, dga_ref, dxl_ref, dgl_ref, g_ref, wt_ref,
             tab_ref, gx_ref, gln_ref, slab_ref):
        @pl.when(pl.program_id(0) == 0)
        def _():
            gln_ref[...] = jnp.zeros_like(gln_ref)

        dzt_ref = slab_ref.at[0]

        c, sa, sb = (t.T for t in _tables(tab_ref))
        for j in range(D // 128):
            js = slice(128 * j, 128 * j + 128)
            dzt_ref[js, :] = (_unrope_t(dq_ref[js, :], c, sa, sb) * (HD ** -0.5)).astype(bf16)
        for j in range(KVW // 128):
            js = slice(128 * j, 128 * j + 128)
            dzt_ref[D + 128 * j:D + 128 * j + 128, :] = _unrope_t(dk_ref[js, :], c, sa, sb).astype(bf16)
        dzt_ref[D + KVW:D + 2 * KVW, :] = dv_ref[...].astype(bf16)
        first = D + 2 * KVW
        dh = _dot_tn(dzt_ref[0:first, :], wt_ref[0:first, :])
        for sec, ref in enumerate((dga_ref, dxl_ref, dgl_ref)):
            dh = dh + _dot(ref[...], wt_ref[first + D * sec:first + D * sec + D, :])
            for j in range(D // 128):
                dzt_ref[first + D * sec + 128 * j:first + D * sec + 128 * j + 128, :] = ref[:, 128 * j:128 * j + 128].T
        xx = x_ref[...]
        rstd = lax.rsqrt(jnp.mean(xx * xx, axis=-1, keepdims=True) + EPS)
        xh = xx * rstd
        gln_ref[...] += jnp.sum(dh * xh, axis=0, keepdims=True)
        w = dh * g_ref[...]
        gx_ref[...] = dx2_ref[...] + rstd * (w - xh * jnp.mean(w * xh, axis=-1, keepdims=True))

    row = lambda w: pl.BlockSpec((tm, w), lambda i: (i, 0))
    col = lambda w: pl.BlockSpec((w, tm), lambda i: (0, i))
    full = lambda a: pl.BlockSpec(a.shape, lambda i: (0, 0))
    return pl.pallas_call(
        body, name="bwd_in", grid=(s // tm,),
        in_specs=[row(D), row(D), col(D), col(KVW), col(KVW), row(D), row(D), row(D), full(ln_gain), full(wt),
                  row(384)],
        out_specs=[row(D), pl.BlockSpec((1, D), lambda i: (0, 0)), pl.BlockSpec((1, NIN, tm), lambda i: (i, 0, 0))],
        out_shape=[pltpu.HBM((s, D), f32), jax.ShapeDtypeStruct((1, D), f32),
                   pltpu.HBM((s // tm, NIN, tm), bf16)],
        compiler_params=_params(("arbitrary",), 56),
    )(*_in_hbm(x, dx2, dq, dk, dv, dga, dxl, dgl), ln_gain, *_in_hbm(wt), tabs)


WT_TERMS = 4


def _dwt_scatter(dzt, h, small, tm):
    s = h.shape[0]
    nk = s // tm
    slab = dzt.shape[2]
    per = tm // slab
    srows = small.shape[0] // NDEV
    last = NDEV - 1
    sm_turn = 2

    def body(order_ref, dz_ref, h_ref, sm_ref, lwt_ref, rep_all, tail_ref, acc, stage, given, relayed, lsm, rep_stage,
             send_sems, recv_sems, local_sem, sm_send, sm_recv, sm_local, rep_send, rep_recv, rep_local):
        j, k = pl.program_id(0), pl.program_id(1)
        x, y, c = _place()
        sibling = (x, y, 1 - c)
        near = (x ^ (1 - c), y ^ c)
        far = (x ^ c, y ^ (1 - c))
        sm_start, sm_finish = _scatter_ops([sm_ref], [lsm], sm_send, sm_recv, sm_local)
        rep_start, rep_pass_on, rep_finish = _gather_ops([rep_stage], [rep_all], rep_send, rep_recv, rep_local)

        def send(step):
            if step == last - 1:
                dst, to = lwt_ref.at[1], sibling
            elif step % 2 == 0:
                dst, to = given.at[step // 2], sibling
            elif step == 1:
                dst, to = relayed, (*near, c)
            else:
                dst, to = lwt_ref.at[1 + step // 2], (*(near if step == 3 else far), c)
            return pltpu.make_async_remote_copy(
                src_ref=stage.at[step % 2], dst_ref=dst, send_sem=send_sems.at[step], recv_sem=recv_sems.at[step],
                device_id=to, device_id_type=MESH)

        def keep():
            return pltpu.make_async_copy(stage.at[last % 2], lwt_ref.at[0], local_sem)

        @pl.when((j == 0) & (k == 0))
        def _():
            sm_start()

        @pl.when(k == 0)
        def _():
            acc[...] = jnp.zeros_like(acc)

        acc[...] += _dot(jnp.concatenate([dz_ref[t] for t in range(per)], axis=1), h_ref[...])

        for step in range(NDEV):
            @pl.when((k == nk - 1) & (j == step))
            def _(step=step):
                if step >= 2:
                    send(step - 2).wait_send()
                if step % 2 == 1 and step < last:
                    send(step - 1).wait_recv()
                    total = acc[...] + given[step // 2].astype(f32)
                    if step == 5:
                        send(1).wait_recv()
                        total = total + relayed[...].astype(f32)
                    stage[step % 2] = total.astype(bf16)
                else:
                    stage[step % 2] = acc[...].astype(bf16)
                if step < last:
                    send(step).start()
                else:
                    keep().start()
                    send(last - 1).wait_send()
                    for peer_step in (3, 5, last - 1):
                        send(peer_step).wait_recv()
                    keep().wait()
                    rep_finish()
                if step == sm_turn:
                    sm_finish()
                    total_sm = lsm[0]
                    for dev in range(1, NDEV):
                        total_sm = total_sm + lsm[dev]
                    rep_stage[...] = total_sm[0:SMALL_PER]
                    tail_ref[...] = total_sm[SMALL_PER:]
                    rep_start()
                if step == last - 1:
                    rep_pass_on()

    x, y, c = _place()
    dest = lambda chip, cc: 4 * chip[0] + 2 * chip[1] + cc
    near, far, diag = (x ^ (1 - c), y ^ c), (x ^ c, y ^ (1 - c)), (1 - x, 1 - y)
    order = jnp.stack([dest(diag, 1 - c), dest(diag, c), dest(far, 1 - c), dest(near, c),
                       dest(near, 1 - c), dest(far, c), dest((x, y), 1 - c), dest((x, y), c)])
    return pl.pallas_call(
        body, name="dwt_scatter",
        grid_spec=pltpu.PrefetchScalarGridSpec(
            num_scalar_prefetch=1, grid=(NDEV, nk),
            in_specs=[pl.BlockSpec((per, WT_ROWS, slab), lambda j, k, order: (k, order[j], 0)),
                      pl.BlockSpec((tm, D), lambda j, k, order: (k, 0)), HBM],
            out_specs=[HBM, HBM, pl.BlockSpec((srows - SMALL_PER, D), lambda j, k, order: (0, 0))],
            scratch_shapes=[pltpu.VMEM((WT_ROWS, D), f32), pltpu.VMEM((2, WT_ROWS, D), bf16),
                            pltpu.VMEM((3, WT_ROWS, D), bf16), pltpu.VMEM((WT_ROWS, D), bf16),
                            pltpu.VMEM((NDEV, srows, D), f32), pltpu.VMEM((SMALL_PER, D), f32),
                            pltpu.SemaphoreType.DMA((last,)), pltpu.SemaphoreType.DMA((last,)),
                            pltpu.SemaphoreType.DMA(())] + _comm_sems(1) + _comm_sems(1)),
        out_shape=[pltpu.HBM((WT_TERMS, WT_ROWS, D), bf16), pltpu.HBM((SMALL_ROWS, D), f32),
                   jax.ShapeDtypeStruct((srows - SMALL_PER, D), f32)],
        compiler_params=_params(("arbitrary", "arbitrary"), 48),
    )(order, *_in_hbm(dzt, h, small))


def _diag_blocks(bd):
    eye = jnp.eye(4, dtype=bd.dtype)
    return jnp.einsum('gjckd,jk->gjcd', bd.reshape(NGRP, 4, HD, 4, HD), eye).reshape(NQ, HD, HD)


def _sequence_step(x, h, tgt, wt, wo_shard, conv_w, wr, wi, p):
    s = x.shape[0]
    tm = min(256, s)
    tabs = _rope_tables(s)
    sinks = p["sinks"].reshape(NQ)
    qt, kt, vt, ga, xl, gl, u, hl, r, ig, wo = _fwd_fused(
        h, wt, tabs, wo_shard, conv_w, p["conv_b"], wr, wi, p["b_rgate"], p["b_igate"], p["lru_lambda"], tm)
    ot = _attn_fwd_t(qt, kt, vt, sinks)
    dx2, dot, dga, dhl, dgl, dwo, g_fg, g_ag, g_lg, loss = _out_fwd_bwd(
        x, tgt, ot, ga, hl, gl, p["attn_out_gain"], p["lru_out_gain"], p["final_gain"], wo, tm)
    dqt, dkt, dvt, dsink, land_wo = _attn_bwd_t(qt, kt, vt, dot, sinks, dwo)
    dxl, dwr, dwi, dbr, dbi, dlam, dcb, dcw = _lru_bwd(u, hl, dhl, xl, r, ig, conv_w, wr, wi, p["lru_lambda"], tm)
    gx, g_ln, dzt = _bwd_in(x, dx2, dqt, dkt, dvt, dga, dxl, dgl, p["ln_gain"], wt, tabs, tm)
    small = dict(ln_gain=g_ln, sinks=dsink.reshape(NQ, BLK).sum(axis=1)[None], conv_w=dcw, conv_b=dcb,
                 w_rgate=_diag_blocks(dwr), b_rgate=dbr, w_igate=_diag_blocks(dwi), b_igate=dbi, lru_lambda=dlam,
                 attn_out_gain=g_ag, lru_out_gain=g_lg, final_gain=g_fg)
    land_wt, g_rep, g_tail = _dwt_scatter(dzt, h, _pack_small(small, loss), min(2048, s))
    return gx, land_wt, land_wo, g_rep, g_tail


def _gather_weights(wt_shard, conv_blk, x, ln_gain, w_rgate, w_igate, tm):
    s = x.shape[0]

    def body(wt_ref, cw_ref, g_ref, wrg_ref, wig_ref, x_ref, wt_all, cw_all, h_ref, wr_ref, wi_ref,
             stage, xbuf, hbuf, send_sems, recv_sems, local_sems):
        stage[...] = wt_ref[...].astype(bf16)
        start, finish = _relay_gather_ops([stage, cw_ref], [wt_all, cw_all], send_sems, recv_sems, local_sems)
        start()
        for src, dst in ((wrg_ref, wr_ref), (wig_ref, wi_ref)):
            dst[...] = jnp.zeros_like(dst)
            for nb in range(NQ):
                g, j = divmod(nb, 4)
                dst[g, HD * j:HD * j + HD, HD * j:HD * j + HD] = src[nb].astype(bf16)
        gain = g_ref[...]
        for i in range(s // tm):
            rows = pl.ds(i * tm, tm)
            pltpu.sync_copy(x_ref.at[rows, :], xbuf)
            xx = xbuf[...]
            rstd = lax.rsqrt(jnp.mean(xx * xx, axis=-1, keepdims=True) + EPS)
            hbuf[...] = (xx * rstd * gain).astype(bf16)
            pltpu.sync_copy(hbuf, h_ref.at[rows, :])
        finish()

    vmem = pl.BlockSpec(memory_space=pltpu.VMEM)
    return pl.pallas_call(
        body, name="gather_weights",
        in_specs=[vmem] * 5 + [HBM], out_specs=[HBM, HBM, HBM, vmem, vmem],
        out_shape=[pltpu.HBM((NIN, D), bf16), pltpu.HBM((NDEV * 8, 128), f32), pltpu.HBM((s, D), bf16)]
        + [jax.ShapeDtypeStruct((NGRP, 256, 256), bf16)] * 2,
        scratch_shapes=[pltpu.VMEM((WT_ROWS, D), bf16), pltpu.VMEM((tm, D), f32), pltpu.VMEM((tm, D), bf16)]
        + _comm_sems(2),
        compiler_params=pltpu.CompilerParams(vmem_limit_bytes=32 * MIB),
    )(wt_shard, conv_blk, ln_gain, w_rgate, w_igate, *_in_hbm(x))


def _adam_math(w, g, m, v):
    m2 = ADAM_B1 * m + (1.0 - ADAM_B1) * g
    v2 = ADAM_B2 * v + (1.0 - ADAM_B2) * (g * g)
    m_hat = m2 / (1.0 - ADAM_B1 ** ADAM_STEP)
    v_hat = v2 / (1.0 - ADAM_B2 ** ADAM_STEP)
    delta = -ADAM_LR * (m_hat / (jnp.sqrt(v_hat) + ADAM_EPS) + ADAM_WD * w)
    return delta, m2, v2


def _reduce_adamw(land, w, m, v, tr, name):
    terms, rows, cols = land.shape

    def body(l_ref, w_ref, m_ref, v_ref, g_ref, d_ref, m2_ref, v2_ref):
        g = l_ref[0].astype(f32)
        for t in range(1, terms):
            g = g + l_ref[t].astype(f32)
        g_ref[...] = g
        d_ref[...], m2_ref[...], v2_ref[...] = _adam_math(w_ref[...], g, m_ref[...], v_ref[...])

    blk = pl.BlockSpec((tr, cols), lambda i: (i, 0))
    return pl.pallas_call(
        body, name=name, grid=(rows // tr,),
        in_specs=[pl.BlockSpec((terms, tr, cols), lambda i: (0, i, 0))] + [blk] * 3, out_specs=[blk] * 4,
        out_shape=[jax.ShapeDtypeStruct((rows, cols), f32)] * 4,
        compiler_params=_params(("arbitrary",), 32),
    )(*_in_hbm(land), w, m, v)


VEC_NAMES = ("ln_gain", "conv_b", "b_rgate", "b_igate", "lru_lambda", "attn_out_gain", "lru_out_gain", "final_gain")
ROW_RGATE, ROW_IGATE, ROW_VEC, ROW_SINKS = 0, 64, 128, 136
LOSS_LANE = NQ


def _adamw_small(g_rep, g_conv, w, m, v):
    names = list(VEC_NAMES) + ["sinks", "conv_w", "w_rgate", "w_igate"]
    ins = [g_rep, g_conv] + [d[k] for k in names for d in (w, m, v)]

    def body(*refs):
        g_ref, gc_ref = refs[0], refs[1]
        in_refs = refs[2:2 + 3 * len(names)]
        out_refs = refs[2 + 3 * len(names):]

        def update(j, g, at=None):
            w_ref, m_ref, v_ref = in_refs[3 * j:3 * j + 3]
            outs = out_refs[4 * j:4 * j + 4]
            pick = (lambda r: r[...]) if at is None else (lambda r: r[at])
            res = (g,) + _adam_math(pick(w_ref), g, pick(m_ref), pick(v_ref))
            for o_ref, val in zip(outs, res):
                if at is None:
                    o_ref[...] = val
                else:
                    o_ref[at] = val

        for j in range(len(VEC_NAMES)):
            update(j, g_ref[ROW_VEC + j:ROW_VEC + j + 1, :])
        update(len(VEC_NAMES), g_ref[ROW_SINKS:ROW_SINKS + 1, 0:NQ])
        update(len(VEC_NAMES) + 1, gc_ref[...], at=0)
        for gi, row0 in ((len(VEC_NAMES) + 2, ROW_RGATE), (len(VEC_NAMES) + 3, ROW_IGATE)):
            for nb in range(NQ):
                update(gi, g_ref[row0:row0 + HD, HD * nb:HD * nb + HD], at=(0, nb))

    vmem = pl.BlockSpec(memory_space=pltpu.VMEM)
    out_shape = [jax.ShapeDtypeStruct(w[k].shape, f32) for k in names for _ in range(4)]
    outs = pl.pallas_call(
        body, name="adamw_small",
        in_specs=[vmem] * len(ins), out_specs=[vmem] * len(out_shape), out_shape=out_shape,
        compiler_params=pltpu.CompilerParams(vmem_limit_bytes=32 * MIB),
    )(*ins)
    return {k: tuple(outs[4 * j:4 * j + 4]) for j, k in enumerate(names)}


def _pack_small(small, loss):
    gate = lambda g: g.transpose(1, 0, 2).reshape(HD, NQ * HD)
    row_s = jnp.concatenate([small["sinks"], loss[:, LOSS_LANE:128], jnp.zeros((1, D - 128), f32)], axis=1)
    rep = jnp.concatenate([gate(small["w_rgate"]), gate(small["w_igate"])] + [small[k] for k in VEC_NAMES]
                          + [row_s, jnp.zeros((SMALL_ROWS - ROW_SINKS - 1, D), f32)], axis=0)
    conv = small["conv_w"].reshape(CONVW, NDEV, 128).transpose(1, 0, 2)
    conv = jnp.pad(conv, ((0, 0), (0, 8 - CONVW), (0, D - 128)))
    return jnp.concatenate([rep.reshape(NDEV, SMALL_PER, D), conv], axis=1).reshape(NDEV * (SMALL_PER + 8), D)


def kernel(x, ln_gain, w_in, sinks, conv_w, conv_b, w_rgate, b_rgate, w_igate, b_igate, lru_lambda, attn_out_gain, lru_out_gain, w_out, final_gain, loss_target, m_ln_gain, m_w_in, m_sinks, m_conv_w, m_conv_b, m_w_rgate, m_b_rgate, m_w_igate, m_b_igate, m_lru_lambda, m_attn_out_gain, m_lru_out_gain, m_w_out, m_final_gain, v_ln_gain, v_w_in, v_sinks, v_conv_w, v_conv_b, v_w_rgate, v_b_rgate, v_w_igate, v_b_igate, v_lru_lambda, v_attn_out_gain, v_lru_out_gain, v_w_out, v_final_gain):
    w = dict(ln_gain=ln_gain, sinks=sinks, conv_w=conv_w, conv_b=conv_b, w_rgate=w_rgate, b_rgate=b_rgate,
             w_igate=w_igate, b_igate=b_igate, lru_lambda=lru_lambda, attn_out_gain=attn_out_gain,
             lru_out_gain=lru_out_gain, final_gain=final_gain.reshape(1, D))
    m = dict(ln_gain=m_ln_gain, sinks=m_sinks, conv_w=m_conv_w, conv_b=m_conv_b, w_rgate=m_w_rgate,
             b_rgate=m_b_rgate, w_igate=m_w_igate, b_igate=m_b_igate, lru_lambda=m_lru_lambda,
             attn_out_gain=m_attn_out_gain, lru_out_gain=m_lru_out_gain, final_gain=m_final_gain.reshape(1, D))
    v = dict(ln_gain=v_ln_gain, sinks=v_sinks, conv_w=v_conv_w, conv_b=v_conv_b, w_rgate=v_w_rgate,
             b_rgate=v_b_rgate, w_igate=v_w_igate, b_igate=v_b_igate, lru_lambda=v_lru_lambda,
             attn_out_gain=v_attn_out_gain, lru_out_gain=v_lru_out_gain, final_gain=v_final_gain.reshape(1, D))

    conv_blk = jnp.pad(conv_w[0], ((0, 8 - CONVW), (0, 0)))
    wt, cw_all, h, wr, wi = _gather_weights(w_in[0].T, conv_blk, x[0], ln_gain, w_rgate[0], w_igate[0],
                                            min(512, x.shape[1]))
    conv_full = cw_all.reshape(NDEV, 8, 128)[:, 0:CONVW].transpose(1, 0, 2).reshape(CONVW, LW)

    p = {k: w[k] for k in w if k not in ("conv_w", "w_rgate", "w_igate")}
    gx, land_wt, land_wo, g_rep, g_tail = _sequence_step(
        x[0], h, loss_target[0], wt, w_out[0], conv_full, wr, wi, p)
    g_conv = g_tail[0:CONVW, 0:128]

    wins = _reduce_adamw(land_wt, w_in[0].T, m_w_in[0].T, v_w_in[0].T, 192, "adamw_w_in")
    g_win, d_win, m_win, v_win = (t.T for t in wins)
    g_wo, d_wo, m_wo, v_wo = _reduce_adamw(land_wo, w_out[0], m_w_out[0], v_w_out[0], 256, "adamw_w_out")
    res = _adamw_small(g_rep, g_conv, w, m, v)
    res["w_in"] = tuple(t[None] for t in (g_win, d_win, m_win, v_win))
    res["w_out"] = tuple(t[None] for t in (g_wo, d_wo, m_wo, v_wo))
    res["final_gain"] = tuple(t.reshape(D) for t in res["final_gain"])

    order = ("ln_gain", "w_in", "sinks", "conv_w", "conv_b", "w_rgate", "b_rgate", "w_igate", "b_igate",
             "lru_lambda", "attn_out_gain", "lru_out_gain", "w_out", "final_gain")
    total_loss = g_rep[ROW_SINKS, LOSS_LANE]
    return (total_loss, gx[None]) + tuple(res[k][i] for i in range(4) for k in order)
```

```python
import jax
import jax.numpy as jnp
from jax import lax
from jax.experimental import pallas as pl
from jax.experimental.pallas import tpu as pltpu

f32 = jnp.float32
bf16 = jnp.bfloat16

D = 1024
HD = 64
NQ = 16
NKV = 4
GROUP = NQ // NKV
KVW = NKV * HD
BLK = 128
ROT = 16
THETA = 500000.0
NEG = -1e30
LW = 1024
NGRP = 4
CONVW = 4
LRU_C = 8.0
NIN = 4608
EPS = 1e-6
NDEV = 8
WT_ROWS = NIN // NDEV
WO_ROWS = 2 * D // NDEV
SMALL_ROWS = 192
SMALL_PER = SMALL_ROWS // NDEV

ADAM_LR = 0.001
ADAM_B1 = 0.9
ADAM_B2 = 0.999
ADAM_EPS = 1e-08
ADAM_WD = 0.01
ADAM_STEP = 10

NT = (((1,), (1,)), ((), ()))
TN = (((0,), (0,)), ((), ()))
MESH = pl.DeviceIdType.MESH
MIB = 1024 * 1024


def _dot(a, b):
    return jnp.dot(a, b, preferred_element_type=f32)


def _dot_nt(a, b):
    return lax.dot_general(a, b, NT, preferred_element_type=f32)


def _dot_tn(a, b):
    return lax.dot_general(a, b, TN, preferred_element_type=f32)


def _params(sem, vmem_mib):
    return pltpu.CompilerParams(dimension_semantics=sem, vmem_limit_bytes=vmem_mib * MIB)


def _sigmoid(x):
    return 0.5 * jnp.tanh(0.5 * x) + 0.5


def _softplus(x):
    return jnp.maximum(x, 0.0) + jnp.log(1.0 + jnp.exp(-jnp.abs(x)))


def _rope_tables(s):
    pos = jnp.arange(s, dtype=f32)
    inv_freq = THETA ** (-jnp.arange(0, ROT, 2, dtype=f32) / ROT)
    ang = pos[:, None] * inv_freq[None, :]
    cs = jnp.concatenate([jnp.cos(ang) - 1.0, jnp.sin(ang)], axis=1)
    d = jnp.arange(128) % HD
    j = jnp.arange(ROT)[:, None]
    pick_c = ((d < ROT) & (j == d % (ROT // 2))).astype(f32)
    pick_sa = ((d >= ROT // 2) & (d < ROT) & (j == d)).astype(f32)
    pick_sb = -((d < ROT // 2) & (j == d + ROT // 2)).astype(f32)
    picks = jnp.concatenate([pick_c, pick_sa, pick_sb], axis=1)
    ones = jnp.concatenate([jnp.ones((1, 128), f32), jnp.zeros((1, 256), f32)], axis=1)
    return jnp.dot(cs, picks, precision=lax.Precision.HIGHEST) + ones


def _tables(tab_ref):
    return tab_ref[:, 0:128], tab_ref[:, 128:256], tab_ref[:, 256:384]


def _rope(t, c, sa, sb):
    return t * c + pltpu.roll(t, 8, 1) * sa + pltpu.roll(t, 120, 1) * sb


def _unrope_t(dr, c, sa, sb):
    return dr * c + pltpu.roll(dr * sa, 120, 0) + pltpu.roll(dr * sb, 8, 0)


def _place():
    return lax.axis_index("x"), lax.axis_index("y"), lax.axis_index("c")


def _gather_ops(mine_refs, out_refs, send_sems, recv_sems, local_sems):
    n = len(mine_refs)
    x, y, c = _place()
    me, sibling = (x, y, c), (x, y, 1 - c)
    chips = [(1 - x, y), (x, 1 - y), (1 - x, 1 - y)]

    def rows(a, dev):
        m = mine_refs[a].shape[0]
        return out_refs[a].at[pl.ds((4 * dev[0] + 2 * dev[1] + dev[2]) * m, m), :]

    def copy(a, k, block, to, own=False):
        return pltpu.make_async_remote_copy(
            src_ref=mine_refs[a] if own else rows(a, block), dst_ref=rows(a, block),
            send_sem=send_sems.at[a, k], recv_sem=recv_sems.at[a, k], device_id=to, device_id_type=MESH)

    def local(a):
        return pltpu.make_async_copy(mine_refs[a], rows(a, me), local_sems.at[a])

    def first(a):
        return [copy(a, 0, me, sibling, own=True)] + [copy(a, 1 + j, me, (*chip, c), own=True)
                                                      for j, chip in enumerate(chips)]

    def start():
        for a in range(n):
            local(a).start()
            for cp in first(a):
                cp.start()

    def pass_on():
        for j, chip in enumerate(chips):
            for a in range(n):
                copy(a, 1 + j, (*chip, c), me).wait_recv()
                copy(a, 4 + j, (*chip, c), sibling).start()

    def finish():
        for a in range(n):
            copy(a, 0, sibling, me).wait_recv()
            for j, chip in enumerate(chips):
                copy(a, 4 + j, (*chip, 1 - c), me).wait_recv()
        for a in range(n):
            for cp in first(a) + [copy(a, 4 + j, (*chip, c), sibling) for j, chip in enumerate(chips)]:
                cp.wait_send()
            local(a).wait()

    return start, pass_on, finish


def _relay_gather_ops(mine_refs, out_refs, send_sems, recv_sems, local_sems):
    n = len(mine_refs)
    x, y, c = _place()
    me, sibling = (x, y, c), (x, y, 1 - c)
    near = (x ^ (1 - c), y ^ c)
    far = (x ^ c, y ^ (1 - c))
    diag = (1 - x, 1 - y)

    def rows(a, dev):
        m = mine_refs[a].shape[0]
        return out_refs[a].at[pl.ds((4 * dev[0] + 2 * dev[1] + dev[2]) * m, m), :]

    def copy(a, k, block, to, own=False):
        return pltpu.make_async_remote_copy(
            src_ref=mine_refs[a] if own else rows(a, block), dst_ref=rows(a, block),
            send_sem=send_sems.at[a, k], recv_sem=recv_sems.at[a, k], device_id=to, device_id_type=MESH)

    def local(a):
        return pltpu.make_async_copy(mine_refs[a], rows(a, me), local_sems.at[a])

    def sends(a):
        return [copy(a, 0, me, sibling, own=True), copy(a, 1, me, (*near, c), own=True),
                copy(a, 2, me, (*far, c), own=True), copy(a, 3, (*near, c), (*far, c)),
                copy(a, 4, (*near, c), sibling), copy(a, 5, (*far, c), sibling), copy(a, 6, (*diag, c), sibling)]

    def arrivals(a):
        return [copy(a, 0, sibling, me), copy(a, 1, (*near, c), me), copy(a, 2, (*far, c), me),
                copy(a, 3, (*diag, c), me), copy(a, 4, (*far, 1 - c), me), copy(a, 5, (*near, 1 - c), me),
                copy(a, 6, (*diag, 1 - c), me)]

    def start():
        for a in range(n):
            local(a).start()
            for cp in sends(a)[0:3]:
                cp.start()

    def finish():
        for first, then in ((1, (3, 4)), (2, (5,)), (3, (6,))):
            for a in range(n):
                arrivals(a)[first].wait_recv()
                for k in then:
                    sends(a)[k].start()
        for a in range(n):
            for k in (0, 4, 5, 6):
                arrivals(a)[k].wait_recv()
        for a in range(n):
            for cp in sends(a):
                cp.wait_send()
            local(a).wait()

    return start, finish


def _scatter_ops(src_refs, land_refs, send_sems, recv_sems, local_sems):
    n = len(src_refs)
    x, y, c = _place()
    my = 4 * x + 2 * y + c

    def peer(k):
        return x ^ (k >> 2), y ^ ((k >> 1) & 1), c ^ (k & 1)

    def piece(a, dev):
        m = src_refs[a].shape[0] // NDEV
        return src_refs[a].at[pl.ds(dev * m, m), :]

    def local(a):
        return pltpu.make_async_copy(piece(a, my), land_refs[a].at[my], local_sems.at[a])

    def send(a, k):
        px, py, pc = peer(k)
        return pltpu.make_async_remote_copy(
            src_ref=piece(a, 4 * px + 2 * py + pc), dst_ref=land_refs[a].at[my],
            send_sem=send_sems.at[a, k - 1], recv_sem=recv_sems.at[a, k - 1],
            device_id=(px, py, pc), device_id_type=MESH)

    def arrival(a, k):
        px, py, pc = peer(k)
        return pltpu.make_async_remote_copy(
            src_ref=piece(a, my), dst_ref=land_refs[a].at[4 * px + 2 * py + pc],
            send_sem=send_sems.at[a, k - 1], recv_sem=recv_sems.at[a, k - 1],
            device_id=(px, py, pc), device_id_type=MESH)

    def start():
        for a in range(n):
            local(a).start()
        for k in range(1, NDEV):
            for a in range(n):
                send(a, k).start()

    def finish():
        for k in range(1, NDEV):
            for a in range(n):
                send(a, k).wait_send()
        for k in range(1, NDEV):
            for a in range(n):
                arrival(a, k).wait_recv()
        for a in range(n):
            local(a).wait()

    return start, finish


def _in_hbm(*arrays):
    return tuple(pltpu.with_memory_space_constraint(a, pltpu.HBM) for a in arrays)


def _comm_sems(n):
    return [pltpu.SemaphoreType.DMA((n, 7)), pltpu.SemaphoreType.DMA((n, 7)), pltpu.SemaphoreType.DMA((n,))]


HBM = pl.BlockSpec(memory_space=pltpu.HBM)


def _sink_rows(sinks):
    return jnp.repeat(sinks.reshape(NKV, GROUP), BLK, axis=1)


def _band_softmax(s2_ref, ls, prev_offset, sink_row):
    jj = lax.broadcasted_iota(jnp.int32, (BLK, BLK), 0)
    ii = lax.broadcasted_iota(jnp.int32, (BLK, BLK), 1)
    from_prev = jj > ii
    sc = jnp.where(from_prev, s2_ref[0:BLK, ls] + prev_offset, s2_ref[BLK:2 * BLK, ls])
    m = jnp.maximum(jnp.max(sc, axis=0, keepdims=True), sink_row)
    p = jnp.exp(sc - m)
    es = jnp.exp(sink_row - m)
    inv = 1.0 / (jnp.sum(p, axis=0, keepdims=True) + es)
    return from_prev, p * inv, es * inv


def _put_split(dst_ref, ls, t, from_prev):
    t = t.astype(bf16)
    zero = jnp.zeros_like(t)
    dst_ref[0:BLK, ls] = jnp.where(from_prev, t, zero)
    dst_ref[BLK:2 * BLK, ls] = jnp.where(from_prev, zero, t)


def _heads_side_by_side(ref, h):
    return jnp.concatenate([ref[HD * (GROUP * h + g):HD * (GROUP * h + g) + HD, :] for g in range(GROUP)], axis=1)


def _kv_specs_t():
    prev = pl.BlockSpec((KVW, BLK), lambda n: (0, jnp.maximum(n - 1, 0)))
    cur = pl.BlockSpec((KVW, BLK), lambda n: (0, n))
    return [prev, cur, prev, cur]


def _attn_fwd_t(qt, kt, vt, sinks):
    s = qt.shape[1]

    def body(sink_ref, q_ref, kp_ref, kc_ref, vp_ref, vc_ref, o_ref, s2_scr, pn2_scr):
        n = pl.program_id(0)
        off = jnp.where(n > 0, 0.0, NEG)

        def scores(h):
            hs = slice(HD * h, HD * h + HD)
            kh = jnp.concatenate([kp_ref[hs, :], kc_ref[hs, :]], axis=1)
            s2_scr[h % 2] = _dot_tn(kh, _heads_side_by_side(q_ref, h))

        def probs(h):
            for g in range(GROUP):
                ls = slice(BLK * g, BLK * g + BLK)
                from_prev, pn, _ = _band_softmax(s2_scr.at[h % 2], ls, off, sink_ref[h:h + 1, ls])
                _put_split(pn2_scr.at[h % 2], ls, pn, from_prev)

        def outputs(h):
            hs = slice(HD * h, HD * h + HD)
            vh = jnp.concatenate([vp_ref[hs, :], vc_ref[hs, :]], axis=1)
            og = _dot(vh, pn2_scr[h % 2])
            for g in range(GROUP):
                a = GROUP * h + g
                o_ref[HD * a:HD * a + HD, :] = og[:, BLK * g:BLK * g + BLK]

        scores(0)
        for h in range(NKV):
            if h + 1 < NKV:
                scores(h + 1)
            probs(h)
            outputs(h)

    return pl.pallas_call(
        body, name="attn_fwd", grid=(s // BLK,),
        in_specs=[pl.BlockSpec((NKV, GROUP * BLK), lambda n: (0, 0)), pl.BlockSpec((D, BLK), lambda n: (0, n))]
        + _kv_specs_t(),
        out_specs=pl.BlockSpec((D, BLK), lambda n: (0, n)),
        out_shape=pltpu.HBM((D, s), f32),
        scratch_shapes=[pltpu.VMEM((2, 2 * BLK, GROUP * BLK), f32), pltpu.VMEM((2, 2 * BLK, GROUP * BLK), bf16)],
        compiler_params=_params(("arbitrary",), 32),
    )(_sink_rows(sinks), *_in_hbm(qt, kt, kt, vt, vt))


def _attn_bwd_t(qt, kt, vt, dot, sinks, dwo):
    s = qt.shape[1]
    nb = s // BLK

    def body(sink_ref, q_ref, do_ref, kp_ref, kc_ref, vp_ref, vc_ref, dwo_ref, dq_ref, dk_ref, dv_ref, ds_ref,
             land_ref, dk_hold, dv_hold, s2_scr, dp2_scr, pn2_scr, ds2_scr, send_sems, recv_sems, local_sems):
        n = pl.program_id(0)
        start, finish = _scatter_ops([dwo_ref], [land_ref], send_sems, recv_sems, local_sems)

        @pl.when(n == 0)
        def _():
            start()
            dk_hold[...] = jnp.zeros_like(dk_hold)
            dv_hold[...] = jnp.zeros_like(dv_hold)
            ds_ref[...] = jnp.zeros_like(ds_ref)

        @pl.when(n < nb)
        def _():
            off = jnp.where(n > 0, 0.0, NEG)

            def scores(h):
                hs = slice(HD * h, HD * h + HD)
                kh = jnp.concatenate([kp_ref[hs, :], kc_ref[hs, :]], axis=1)
                vh = jnp.concatenate([vp_ref[hs, :], vc_ref[hs, :]], axis=1)
                s2_scr[h % 2] = _dot_tn(kh, _heads_side_by_side(q_ref, h))
                dp2_scr[h % 2] = _dot_tn(vh, _heads_side_by_side(do_ref, h))

            def softmax_bwd(h):
                for g in range(GROUP):
                    ls = slice(BLK * g, BLK * g + BLK)
                    from_prev, pn, ps = _band_softmax(s2_scr.at[h % 2], ls, off, sink_ref[h:h + 1, ls])
                    dp = jnp.where(from_prev, dp2_scr[h % 2, 0:BLK, ls], dp2_scr[h % 2, BLK:2 * BLK, ls])
                    dsum = jnp.sum(pn * dp, axis=0, keepdims=True)
                    ds_ref[h:h + 1, ls] += -ps * dsum
                    _put_split(pn2_scr.at[h % 2], ls, pn, from_prev)
                    _put_split(ds2_scr.at[h % 2], ls, pn * (dp - dsum), from_prev)

            def grads(h):
                hs = slice(HD * h, HD * h + HD)
                kh = jnp.concatenate([kp_ref[hs, :], kc_ref[hs, :]], axis=1)
                dqg = _dot(kh, ds2_scr[h % 2])
                for g in range(GROUP):
                    a = GROUP * h + g
                    dq_ref[HD * a:HD * a + HD, :] = dqg[:, BLK * g:BLK * g + BLK]
                dkh = _dot_nt(_heads_side_by_side(q_ref, h), ds2_scr[h % 2])
                dvh = _dot_nt(_heads_side_by_side(do_ref, h), pn2_scr[h % 2])
                dk_ref[hs, :] = dk_hold[hs, :] + dkh[:, 0:BLK]
                dv_ref[hs, :] = dv_hold[hs, :] + dvh[:, 0:BLK]
                dk_hold[hs, :] = dkh[:, BLK:2 * BLK]
                dv_hold[hs, :] = dvh[:, BLK:2 * BLK]

            scores(0)
            for h in range(NKV):
                if h + 1 < NKV:
                    scores(h + 1)
                softmax_bwd(h)
                grads(h)

        @pl.when(n == nb)
        def _():
            dk_ref[...] = dk_hold[...]
            dv_ref[...] = dv_hold[...]
            finish()

    blk = pl.BlockSpec((D, BLK), lambda n: (0, jnp.minimum(n, nb - 1)))
    late = pl.BlockSpec((KVW, BLK), lambda n: (0, jnp.maximum(n - 1, 0)))
    whole = pl.BlockSpec((NKV, GROUP * BLK), lambda n: (0, 0))
    kv = [pl.BlockSpec((KVW, BLK), lambda n: (0, jnp.clip(n - 1, 0, nb - 1))),
          pl.BlockSpec((KVW, BLK), lambda n: (0, jnp.minimum(n, nb - 1)))]
    return pl.pallas_call(
        body, name="attn_bwd", grid=(nb + 1,),
        in_specs=[whole, blk, blk] + kv + kv + [HBM],
        out_specs=[blk, late, late, whole, HBM],
        out_shape=[pltpu.HBM((D, s), f32), pltpu.HBM((KVW, s), f32), pltpu.HBM((KVW, s), f32),
                   jax.ShapeDtypeStruct((NKV, GROUP * BLK), f32), pltpu.HBM((NDEV, WO_ROWS, D), bf16)],
        scratch_shapes=[pltpu.VMEM((KVW, BLK), f32), pltpu.VMEM((KVW, BLK), f32)]
        + [pltpu.VMEM((2, 2 * BLK, GROUP * BLK), f32)] * 2 + [pltpu.VMEM((2, 2 * BLK, GROUP * BLK), bf16)] * 2
        + _comm_sems(1),
        compiler_params=_params(("arbitrary",), 48),
    )(_sink_rows(sinks), *_in_hbm(qt, dot, kt, kt, vt, vt, dwo))


def _decay_terms(r, sp):
    a = jnp.exp(r * (-LRU_C * sp))
    n = r * (2.0 * LRU_C * sp)
    y = jnp.where(n < 0.02, n * (1.0 - n * (0.5 - n * (1.0 / 6.0))), 1.0 - a * a)
    inv_mult = lax.rsqrt(jnp.maximum(y, 1e-30))
    return a, y * inv_mult, inv_mult


def _later(x, before, k):
    if k == 0:
        return x
    row = lax.broadcasted_iota(jnp.int32, before.shape, 0)
    rolled = pltpu.roll(x, k, 0)
    first = jnp.where(row < k, pltpu.roll(before, k, 0), rolled[0:8])
    return jnp.concatenate([first, rolled[8:]], axis=0)


def _earlier(x, after, k):
    if k == 0:
        return x
    n = x.shape[0]
    row = lax.broadcasted_iota(jnp.int32, after.shape, 0)
    rolled = pltpu.roll(x, n - k, 0)
    last = jnp.where(row >= 8 - k, pltpu.roll(after, 8 - k, 0), rolled[n - 8:n])
    return jnp.concatenate([rolled[0:n - 8], last], axis=0)


def _fwd_fused(h, wt, tabs, wo_shard, conv_w, conv_b, wr, wi, br, bi, lam, tm):
    s = h.shape[0]
    nt = s // tm
    nc = 512
    pieces = 8
    rows_per = tm // pieces
    later_chunks = (0, 1, 2, 3, 4, 7, 8)

    def body(h_ref, wt_ref, tab_ref, wo_ref, cw_ref, cb_ref, wr_ref, wi_ref, br_ref,
             bi_ref, lam_ref, q_ref, k_ref, v_ref, ga_ref, xl_ref, gl_ref, u_ref, hl_ref, r_ref, ig_ref,
             wo_all, wo_stage, halo, ub_scr, pr_scr, pi_scr, b_scr, a_scr, hcar,
             send_sems, recv_sems, local_sems):
        i = pl.program_id(0)
        start, pass_on, finish = _gather_ops([wo_stage], [wo_all], send_sems, recv_sems, local_sems)

        @pl.when(i == 0)
        def _():
            wo_stage[...] = wo_ref[...].astype(bf16)
            start()
            halo[...] = jnp.zeros_like(halo)
            hcar[...] = jnp.zeros_like(hcar)

        sp = _softplus(-lam_ref[...])
        br, bi = br_ref[...], bi_ref[...]
        c, sa, sb = _tables(tab_ref)
        piece_rows = lambda p: slice(rows_per * p, rows_per * p + rows_per)

        def project(ci):
            z = _dot_nt(h_ref[...], wt_ref[ci * nc:(ci + 1) * nc, :])
            if ci < 2:
                for j in range(nc // 128):
                    r = _rope(z[:, 128 * j:128 * j + 128], c, sa, sb) * (HD ** -0.5)
                    q_ref[ci * nc + 128 * j:ci * nc + 128 * j + 128, :] = r.astype(bf16).T
            elif ci == 2:
                for j in range(2):
                    js = slice(128 * j, 128 * j + 128)
                    k_ref[js, :] = _rope(z[:, js], c, sa, sb).astype(bf16).T
                    v_ref[js, :] = z[:, KVW + 128 * j:KVW + 128 * j + 128].astype(bf16).T
            else:
                sec, j = divmod(ci - 3, 2)
                (ga_ref, xl_ref, gl_ref)[sec][:, j * nc:(j + 1) * nc] = z

        def gate_terms(p):
            rows = piece_rows(p)
            r = _sigmoid(pr_scr[rows, :] + br)
            ig = _sigmoid(pi_scr[rows, :] + bi)
            a, mult, _ = _decay_terms(r, sp)
            r_ref[rows, :] = r
            ig_ref[rows, :] = ig
            a_scr[rows, :] = a
            b_scr[rows, :] = mult * (ig * u_ref[rows, :])

        def scan(p, hc):
            for t in range(rows_per * p, rows_per * p + rows_per):
                hc = a_scr[t:t + 1, :] * hc + b_scr[t:t + 1, :]
                hl_ref[t:t + 1, :] = hc
            return hc

        project(5)
        project(6)
        xl = xl_ref[...]
        u = cb_ref[...] + sum(cw_ref[k:k + 1, :] * _later(xl, halo[...], CONVW - 1 - k) for k in range(CONVW))
        halo[...] = xl[tm - 8:tm, :]
        u_ref[...] = u
        ub_scr[...] = u.astype(bf16)
        for g in range(NGRP):
            gs = slice(256 * g, 256 * g + 256)
            pr_scr[:, gs] = _dot(ub_scr[:, gs], wr_ref[g])
            pi_scr[:, gs] = _dot(ub_scr[:, gs], wi_ref[g])
        hc = hcar[...]
        gate_terms(0)
        for slot, ci in enumerate(later_chunks):
            project(ci)
            gate_terms(slot + 1)
            hc = scan(slot, hc)
        hcar[...] = scan(pieces - 1, hc)

        @pl.when(i == max(nt - 2, 0))
        def _():
            pass_on()

        @pl.when(i == nt - 1)
        def _():
            finish()

    row = lambda w: pl.BlockSpec((tm, w), lambda i: (i, 0))
    col = lambda w: pl.BlockSpec((w, tm), lambda i: (0, i))
    full = lambda a: pl.BlockSpec(a.shape, lambda i: (0,) * a.ndim)
    big = lambda w, dt: pltpu.HBM((s, w), dt)
    tile = pltpu.VMEM((tm, LW), f32)
    return pl.pallas_call(
        body, name="fwd_fused", grid=(nt,),
        in_specs=[row(D), full(wt), row(384), full(wo_shard), full(conv_w), full(conv_b),
                  full(wr), full(wi), full(br), full(bi), full(lam)],
        out_specs=[col(D), col(KVW), col(KVW), row(D), row(D), row(D)] + [row(LW)] * 4 + [HBM],
        out_shape=[pltpu.HBM((D, s), bf16), pltpu.HBM((KVW, s), bf16), pltpu.HBM((KVW, s), bf16),
                   big(D, f32), big(D, f32), big(D, f32)] + [big(LW, f32)] * 4 + [pltpu.HBM((2 * D, D), bf16)],
        scratch_shapes=[pltpu.VMEM((WO_ROWS, D), bf16), pltpu.VMEM((8, LW), f32), pltpu.VMEM((tm, LW), bf16)]
        + [tile] * 4 + [pltpu.VMEM((1, LW), f32)] + _comm_sems(1),
        compiler_params=_params(("arbitrary",), 56),
    )(*_in_hbm(h, wt), tabs, wo_shard, conv_w, conv_b, wr, wi, br, bi, lam)


def _lru_bwd(u, hl, dhl, xl, r, ig, conv_w, wr, wi, lam, tm):
    s = u.shape[0]
    nt = s // tm
    pieces = 8
    rows_per = tm // pieces

    def body(u_ref, h_ref, hp_ref, dh_ref, x_ref, r_ref, ig_ref, cw_ref, wr_ref, wi_ref,
             lam_ref, dxl_ref, dwr_ref, dwi_ref, dbr_ref, dbi_ref, dlam_ref, dcb_ref, dcw_ref,
             l_scr, du_scr, a_scr, mu_scr, im_scr, dpr_scr, dpi_scr, lcar, dunext):
        t0 = pl.program_id(0)
        tile = nt - 1 - t0

        @pl.when(t0 == 0)
        def _():
            lcar[...] = jnp.zeros_like(lcar)
            dunext[...] = jnp.zeros_like(dunext)
            for ref in (dwr_ref, dwi_ref, dbr_ref, dbi_ref, dlam_ref, dcb_ref, dcw_ref):
                ref[...] = jnp.zeros_like(ref)

        lam = lam_ref[...]
        sp = _softplus(-lam)
        hp = jnp.where(tile > 0, hp_ref[...], 0.0)

        def decay(p):
            rows = slice(rows_per * p, rows_per * p + rows_per)
            a_scr[rows, :], mu_scr[rows, :], im_scr[rows, :] = _decay_terms(r_ref[rows, :], sp)

        def scan(p, c):
            for t in range(rows_per * p + rows_per - 1, rows_per * p - 1, -1):
                lt = dh_ref[t:t + 1, :] + c
                l_scr[t:t + 1, :] = lt
                c = a_scr[t:t + 1, :] * lt
            return c

        def terms(p, sums):
            rows = slice(rows_per * p, rows_per * p + rows_per)
            lt, u, r, i, a, mult, inv_mult = l_scr[rows, :], u_ref[rows, :], r_ref[rows, :], ig_ref[rows, :], \
                a_scr[rows, :], mu_scr[rows, :], im_scr[rows, :]
            before = hp if p == 0 else h_ref[rows_per * p - 8:rows_per * p, :]
            hprev = _later(h_ref[rows, :], before, 1)
            iu = i * u
            lm = lt * mult
            du_scr[rows, :] = lm * i
            dla = (lt * hprev) * a - ((lt * iu) * (a * a)) * inv_mult
            dlar = dla * r
            dpr = (dlar * (1.0 - r)) * (-LRU_C * sp)
            dpi = (lm * iu) * (1.0 - i)
            dpr_scr[rows, :] = dpr.astype(bf16)
            dpi_scr[rows, :] = dpi.astype(bf16)
            col = lambda t: jnp.sum(t, axis=0, keepdims=True)
            return sums[0] + col(dlar), sums[1] + col(dpr), sums[2] + col(dpi)

        sums = (jnp.zeros((1, LW), f32),) * 3
        decay(pieces - 1)
        c = scan(pieces - 1, lcar[...])
        for p in range(pieces - 1, -1, -1):
            if p > 0:
                decay(p - 1)
                c = scan(p - 1, c)
            sums = terms(p, sums)
        lcar[...] = c
        dlam_ref[...] += sums[0] * (-LRU_C)
        dbr_ref[...] += sums[1]
        dbi_ref[...] += sums[2]

        ub = u_ref[...].astype(bf16)
        dug = []
        for g in range(NGRP):
            gs = slice(256 * g, 256 * g + 256)
            dwr_ref[g] += _dot_tn(ub[:, gs], dpr_scr[:, gs])
            dwi_ref[g] += _dot_tn(ub[:, gs], dpi_scr[:, gs])
            dug.append(_dot_nt(dpr_scr[:, gs], wr_ref[g]) + _dot_nt(dpi_scr[:, gs], wi_ref[g]))
        du = du_scr[...] + jnp.concatenate(dug, axis=1)

        dcb_ref[...] += jnp.sum(du, axis=0, keepdims=True)
        x = x_ref[...]
        after = dunext[...]
        dxl = jnp.zeros_like(du)
        for k in range(CONVW):
            e = _earlier(du, after, CONVW - 1 - k)
            dxl = dxl + cw_ref[k:k + 1, :] * e
            dcw_ref[k:k + 1, :] += jnp.sum(e * x, axis=0, keepdims=True)
        dxl_ref[...] = dxl.astype(bf16)
        dunext[...] = du[0:8, :]

        @pl.when(t0 == nt - 1)
        def _():
            dlam_ref[...] = dlam_ref[...] * (-_sigmoid(-lam))

    rev = lambda i: (nt - 1 - i, 0)
    row = pl.BlockSpec((tm, LW), rev)
    prev8 = pl.BlockSpec((8, LW), lambda i: (jnp.maximum((nt - 1 - i) * (tm // 8) - 1, 0), 0))
    full = lambda a: pl.BlockSpec(a.shape, lambda i: (0,) * a.ndim)
    vec = pl.BlockSpec((1, LW), lambda i: (0, 0))
    bd = pl.BlockSpec((NGRP, 256, 256), lambda i: (0, 0, 0))
    return pl.pallas_call(
        body, name="lru_bwd", grid=(nt,),
        in_specs=[row, row, prev8] + [row] * 4 + [full(conv_w), full(wr), full(wi), full(lam)],
        out_specs=[row, bd, bd, vec, vec, vec, vec, pl.BlockSpec((CONVW, LW), lambda i: (0, 0))],
        out_shape=[pltpu.HBM((s, LW), bf16),
                   jax.ShapeDtypeStruct((NGRP, 256, 256), f32), jax.ShapeDtypeStruct((NGRP, 256, 256), f32),
                   jax.ShapeDtypeStruct((1, LW), f32), jax.ShapeDtypeStruct((1, LW), f32),
                   jax.ShapeDtypeStruct((1, LW), f32), jax.ShapeDtypeStruct((1, LW), f32),
                   jax.ShapeDtypeStruct((CONVW, LW), f32)],
        scratch_shapes=[pltpu.VMEM((tm, LW), f32)] * 5 + [pltpu.VMEM((tm, LW), bf16)] * 2
        + [pltpu.VMEM((1, LW), f32), pltpu.VMEM((8, LW), f32)],
        compiler_params=_params(("arbitrary",), 56),
    )(*_in_hbm(u, hl, hl, dhl, xl, r, ig), conv_w, wr, wi, lam)


def _gated_norm(t, gate, gain):
    sg = _sigmoid(gate)
    silu = gate * sg
    p = t * silu
    rstd = lax.rsqrt(jnp.mean(p * p, axis=-1, keepdims=True) + EPS)
    ph = p * rstd
    return sg, silu, rstd, ph, ph * gain


def _gated_norm_bwd(dy, t, gate, gain, sg, silu, rstd, ph):
    w = dy * gain
    dp = rstd * (w - ph * jnp.mean(w * ph, axis=-1, keepdims=True))
    dgate = (dp * t) * (sg + silu * (1.0 - sg))
    return jnp.sum(dy * ph, axis=0, keepdims=True), dp * silu, dgate


def _out_fwd_bwd(x, tgt, o, ga, hl, gl, again, lgain, fgain, wo, tm):
    s = x.shape[0]
    nt = s // tm

    def body(x_ref, t_ref, o_ref, ga_ref, hl_ref, gl_ref, ag_ref, lg_ref, fg_ref, wo_ref,
             dx2_ref, do_ref, dga_ref, dhl_ref, dgl_ref, dwo_ref, gfg_ref, gag_ref, glg_ref, loss_ref, acc):
        i = pl.program_id(0)

        @pl.when(i == 0)
        def _():
            acc[...] = jnp.zeros_like(acc)
            for ref in (gfg_ref, gag_ref, glg_ref, loss_ref):
                ref[...] = jnp.zeros_like(ref)

        oo = jnp.concatenate([o_ref[128 * j:128 * j + 128, :].T for j in range(D // 128)], axis=1)
        gga, hh, ggl = ga_ref[...], hl_ref[...], gl_ref[...]
        ag, lg, fg = ag_ref[...], lg_ref[...], fg_ref[...]
        sga, silua, ra, pah, ya = _gated_norm(oo, gga, ag)
        sgl, silul, rl, plh, yl = _gated_norm(hh, ggl, lg)
        yab, ylb = ya.astype(bf16), yl.astype(bf16)
        y = _dot(yab, wo_ref[0:D, :]) + _dot(ylb, wo_ref[D:2 * D, :])
        x2 = x_ref[...] + y
        r2 = lax.rsqrt(jnp.mean(x2 * x2, axis=-1, keepdims=True) + EPS)
        x2h = x2 * r2
        err = x2h * fg - t_ref[...]
        loss_ref[...] += 0.5 * jnp.sum(jnp.sum(err * err, axis=-1, keepdims=True) * (1.0 / D))
        gfg_ref[...] += jnp.sum(err * x2h, axis=0, keepdims=True) * (1.0 / D)
        w = err * (fg * (1.0 / D))
        dx2 = r2 * (w - x2h * jnp.mean(w * x2h, axis=-1, keepdims=True))
        dx2_ref[...] = dx2
        dyb = dx2.astype(bf16)
        acc[0:D, :] += _dot_tn(yab, dyb)
        acc[D:2 * D, :] += _dot_tn(ylb, dyb)
        dya = _dot_nt(dyb, wo_ref[0:D, :])
        dyl = _dot_nt(dyb, wo_ref[D:2 * D, :])
        gag, do, dga = _gated_norm_bwd(dya, oo, gga, ag, sga, silua, ra, pah)
        glg, dhl, dgl = _gated_norm_bwd(dyl, hh, ggl, lg, sgl, silul, rl, plh)
        gag_ref[...] += gag
        glg_ref[...] += glg
        dob = do.astype(bf16)
        for j in range(D // 128):
            do_ref[128 * j:128 * j + 128, :] = dob[:, 128 * j:128 * j + 128].T
        dga_ref[...] = dga.astype(bf16)
        dhl_ref[...] = dhl
        dgl_ref[...] = dgl.astype(bf16)

        @pl.when(i == nt - 1)
        def _():
            dwo_ref[...] = acc[...].astype(bf16)

    row = pl.BlockSpec((tm, D), lambda i: (i, 0))
    col = pl.BlockSpec((D, tm), lambda i: (0, i))
    vec = pl.BlockSpec((1, D), lambda i: (0, 0))
    mat = pl.BlockSpec((2 * D, D), lambda i: (0, 0))
    return pl.pallas_call(
        body, name="out_fwd_bwd", grid=(nt,),
        in_specs=[row, row, col, row, row, row] + [vec] * 3 + [mat],
        out_specs=[row, col, row, row, row] + [mat, vec, vec, vec, pl.BlockSpec((1, 128), lambda i: (0, 0))],
        out_shape=[pltpu.HBM((s, D), f32), pltpu.HBM((D, s), bf16),
                   pltpu.HBM((s, D), bf16), pltpu.HBM((s, D), f32),
                   pltpu.HBM((s, D), bf16), pltpu.HBM((2 * D, D), bf16),
                   jax.ShapeDtypeStruct((1, D), f32), jax.ShapeDtypeStruct((1, D), f32),
                   jax.ShapeDtypeStruct((1, D), f32), jax.ShapeDtypeStruct((1, 128), f32)],
        scratch_shapes=[pltpu.VMEM((2 * D, D), f32)],
        compiler_params=_params(("arbitrary",), 56),
    )(*_in_hbm(x, tgt, o, ga, hl, gl), again, lgain, fgain, *_in_hbm(wo))


def _bwd_in(x, dx2, dq, dk, dv, dga, dxl, dgl, ln_gain, wt, tabs, tm):
    s = x.shape[0]

    def body(x_ref, dx2_ref, dq_ref, dk_ref, dv_ref, dga_ref, dxl_ref, dgl_ref, g_ref, wt_ref,
             tab_ref, gx_ref, gln_ref, dzt_ref):
        @pl.when(pl.program_id(0) == 0)
        def _():
            gln_ref[...] = jnp.zeros_like(gln_ref)

        c, sa, sb = (t.T for t in _tables(tab_ref))
        for j in range(D // 128):
            js = slice(128 * j, 128 * j + 128)
            dzt_ref[js, :] = (_unrope_t(dq_ref[js, :], c, sa, sb) * (HD ** -0.5)).astype(bf16)
        for j in range(KVW // 128):
            js = slice(128 * j, 128 * j + 128)
            dzt_ref[D + 128 * j:D + 128 * j + 128, :] = _unrope_t(dk_ref[js, :], c, sa, sb).astype(bf16)
        dzt_ref[D + KVW:D + 2 * KVW, :] = dv_ref[...].astype(bf16)
        first = D + 2 * KVW
        dh = _dot_tn(dzt_ref[0:first, :], wt_ref[0:first, :])
        for sec, ref in enumerate((dga_ref, dxl_ref, dgl_ref)):
            dh = dh + _dot(ref[...], wt_ref[first + D * sec:first + D * sec + D, :])
            for j in range(D // 128):
                dzt_ref[first + D * sec + 128 * j:first + D * sec + 128 * j + 128, :] = ref[:, 128 * j:128 * j + 128].T
        xx = x_ref[...]
        rstd = lax.rsqrt(jnp.mean(xx * xx, axis=-1, keepdims=True) + EPS)
        xh = xx * rstd
        gln_ref[...] += jnp.sum(dh * xh, axis=0, keepdims=True)
        w = dh * g_ref[...]
        gx_ref[...] = dx2_ref[...] + rstd * (w - xh * jnp.mean(w * xh, axis=-1, keepdims=True))

    row = lambda w: pl.BlockSpec((tm, w), lambda i: (i, 0))
    col = lambda w: pl.BlockSpec((w, tm), lambda i: (0, i))
    full = lambda a: pl.BlockSpec(a.shape, lambda i: (0, 0))
    return pl.pallas_call(
        body, name="bwd_in", grid=(s // tm,),
        in_specs=[row(D), row(D), col(D), col(KVW), col(KVW), row(D), row(D), row(D), full(ln_gain), full(wt),
                  row(384)],
        out_specs=[row(D), pl.BlockSpec((1, D), lambda i: (0, 0)), col(NIN)],
        out_shape=[pltpu.HBM((s, D), f32), jax.ShapeDtypeStruct((1, D), f32),
                   pltpu.HBM((NIN, s), bf16)],
        compiler_params=_params(("arbitrary",), 56),
    )(*_in_hbm(x, dx2, dq, dk, dv, dga, dxl, dgl), ln_gain, *_in_hbm(wt), tabs)


WT_TERMS = 4


def _dwt_scatter(dzt, h, small, tm):
    s = h.shape[0]
    nk = s // tm
    srows = small.shape[0] // NDEV
    last = NDEV - 1
    sm_turn = 2

    def body(order_ref, dz_ref, h_ref, sm_ref, lwt_ref, rep_all, tail_ref, acc, stage, given, relayed, lsm, rep_stage,
             send_sems, recv_sems, local_sem, sm_send, sm_recv, sm_local, rep_send, rep_recv, rep_local):
        j, k = pl.program_id(0), pl.program_id(1)
        x, y, c = _place()
        sibling = (x, y, 1 - c)
        near = (x ^ (1 - c), y ^ c)
        far = (x ^ c, y ^ (1 - c))
        sm_start, sm_finish = _scatter_ops([sm_ref], [lsm], sm_send, sm_recv, sm_local)
        rep_start, rep_pass_on, rep_finish = _gather_ops([rep_stage], [rep_all], rep_send, rep_recv, rep_local)

        def send(step):
            if step == last - 1:
                dst, to = lwt_ref.at[1], sibling
            elif step % 2 == 0:
                dst, to = given.at[step // 2], sibling
            elif step == 1:
                dst, to = relayed, (*near, c)
            else:
                dst, to = lwt_ref.at[1 + step // 2], (*(near if step == 3 else far), c)
            return pltpu.make_async_remote_copy(
                src_ref=stage.at[step % 2], dst_ref=dst, send_sem=send_sems.at[step], recv_sem=recv_sems.at[step],
                device_id=to, device_id_type=MESH)

        def keep():
            return pltpu.make_async_copy(stage.at[last % 2], lwt_ref.at[0], local_sem)

        @pl.when((j == 0) & (k == 0))
        def _():
            sm_start()

        @pl.when(k == 0)
        def _():
            acc[...] = jnp.zeros_like(acc)

        acc[...] += _dot(dz_ref[...], h_ref[...])

        for step in range(NDEV):
            @pl.when((k == nk - 1) & (j == step))
            def _(step=step):
                if step >= 2:
                    send(step - 2).wait_send()
                if step % 2 == 1 and step < last:
                    send(step - 1).wait_recv()
                    total = acc[...] + given[step // 2].astype(f32)
                    if step == 5:
                        send(1).wait_recv()
                        total = total + relayed[...].astype(f32)
                    stage[step % 2] = total.astype(bf16)
                else:
                    stage[step % 2] = acc[...].astype(bf16)
                if step < last:
                    send(step).start()
                else:
                    keep().start()
                    send(last - 1).wait_send()
                    for peer_step in (3, 5, last - 1):
                        send(peer_step).wait_recv()
                    keep().wait()
                    rep_finish()
                if step == sm_turn:
                    sm_finish()
                    total_sm = lsm[0]
                    for dev in range(1, NDEV):
                        total_sm = total_sm + lsm[dev]
                    rep_stage[...] = total_sm[0:SMALL_PER]
                    tail_ref[...] = total_sm[SMALL_PER:]
                    rep_start()
                if step == last - 1:
                    rep_pass_on()

    x, y, c = _place()
    dest = lambda chip, cc: 4 * chip[0] + 2 * chip[1] + cc
    near, far, diag = (x ^ (1 - c), y ^ c), (x ^ c, y ^ (1 - c)), (1 - x, 1 - y)
    order = jnp.stack([dest(diag, 1 - c), dest(diag, c), dest(far, 1 - c), dest(near, c),
                       dest(near, 1 - c), dest(far, c), dest((x, y), 1 - c), dest((x, y), c)])
    return pl.pallas_call(
        body, name="dwt_scatter",
        grid_spec=pltpu.PrefetchScalarGridSpec(
            num_scalar_prefetch=1, grid=(NDEV, nk),
            in_specs=[pl.BlockSpec((WT_ROWS, tm), lambda j, k, order: (order[j], k)),
                      pl.BlockSpec((tm, D), lambda j, k, order: (k, 0)), HBM],
            out_specs=[HBM, HBM, pl.BlockSpec((srows - SMALL_PER, D), lambda j, k, order: (0, 0))],
            scratch_shapes=[pltpu.VMEM((WT_ROWS, D), f32), pltpu.VMEM((2, WT_ROWS, D), bf16),
                            pltpu.VMEM((3, WT_ROWS, D), bf16), pltpu.VMEM((WT_ROWS, D), bf16),
                            pltpu.VMEM((NDEV, srows, D), f32), pltpu.VMEM((SMALL_PER, D), f32),
                            pltpu.SemaphoreType.DMA((last,)), pltpu.SemaphoreType.DMA((last,)),
                            pltpu.SemaphoreType.DMA(())] + _comm_sems(1) + _comm_sems(1)),
        out_shape=[pltpu.HBM((WT_TERMS, WT_ROWS, D), bf16), pltpu.HBM((SMALL_ROWS, D), f32),
                   jax.ShapeDtypeStruct((srows - SMALL_PER, D), f32)],
        compiler_params=_params(("arbitrary", "arbitrary"), 48),
    )(order, *_in_hbm(dzt, h, small))


def _diag_blocks(bd):
    eye = jnp.eye(4, dtype=bd.dtype)
    return jnp.einsum('gjckd,jk->gjcd', bd.reshape(NGRP, 4, HD, 4, HD), eye).reshape(NQ, HD, HD)


def _sequence_step(x, h, tgt, wt, wo_shard, conv_w, wr, wi, p):
    s = x.shape[0]
    tm = min(256, s)
    tabs = _rope_tables(s)
    sinks = p["sinks"].reshape(NQ)
    qt, kt, vt, ga, xl, gl, u, hl, r, ig, wo = _fwd_fused(
        h, wt, tabs, wo_shard, conv_w, p["conv_b"], wr, wi, p["b_rgate"], p["b_igate"], p["lru_lambda"], tm)
    ot = _attn_fwd_t(qt, kt, vt, sinks)
    dx2, dot, dga, dhl, dgl, dwo, g_fg, g_ag, g_lg, loss = _out_fwd_bwd(
        x, tgt, ot, ga, hl, gl, p["attn_out_gain"], p["lru_out_gain"], p["final_gain"], wo, tm)
    dqt, dkt, dvt, dsink, land_wo = _attn_bwd_t(qt, kt, vt, dot, sinks, dwo)
    dxl, dwr, dwi, dbr, dbi, dlam, dcb, dcw = _lru_bwd(
        u, hl, dhl, xl, r, ig, conv_w, wr, wi, p["lru_lambda"], min(512, s))
    gx, g_ln, dzt = _bwd_in(x, dx2, dqt, dkt, dvt, dga, dxl, dgl, p["ln_gain"], wt, tabs, tm)
    small = dict(ln_gain=g_ln, sinks=dsink.reshape(NQ, BLK).sum(axis=1)[None], conv_w=dcw, conv_b=dcb,
                 w_rgate=_diag_blocks(dwr), b_rgate=dbr, w_igate=_diag_blocks(dwi), b_igate=dbi, lru_lambda=dlam,
                 attn_out_gain=g_ag, lru_out_gain=g_lg, final_gain=g_fg)
    land_wt, g_rep, g_tail = _dwt_scatter(dzt, h, _pack_small(small, loss), min(2048, s))
    return gx, land_wt, land_wo, g_rep, g_tail


def _gather_weights(wt_shard, conv_blk, x, ln_gain, w_rgate, w_igate, tm):
    s = x.shape[0]

    def body(wt_ref, cw_ref, g_ref, wrg_ref, wig_ref, x_ref, wt_all, cw_all, h_ref, wr_ref, wi_ref,
             stage, xbuf, hbuf, send_sems, recv_sems, local_sems):
        stage[...] = wt_ref[...].astype(bf16)
        start, finish = _relay_gather_ops([stage, cw_ref], [wt_all, cw_all], send_sems, recv_sems, local_sems)
        start()
        for src, dst in ((wrg_ref, wr_ref), (wig_ref, wi_ref)):
            dst[...] = jnp.zeros_like(dst)
            for nb in range(NQ):
                g, j = divmod(nb, 4)
                dst[g, HD * j:HD * j + HD, HD * j:HD * j + HD] = src[nb].astype(bf16)
        gain = g_ref[...]
        for i in range(s // tm):
            rows = pl.ds(i * tm, tm)
            pltpu.sync_copy(x_ref.at[rows, :], xbuf)
            xx = xbuf[...]
            rstd = lax.rsqrt(jnp.mean(xx * xx, axis=-1, keepdims=True) + EPS)
            hbuf[...] = (xx * rstd * gain).astype(bf16)
            pltpu.sync_copy(hbuf, h_ref.at[rows, :])
        finish()

    vmem = pl.BlockSpec(memory_space=pltpu.VMEM)
    return pl.pallas_call(
        body, name="gather_weights",
        in_specs=[vmem] * 5 + [HBM], out_specs=[HBM, HBM, HBM, vmem, vmem],
        out_shape=[pltpu.HBM((NIN, D), bf16), pltpu.HBM((NDEV * 8, 128), f32), pltpu.HBM((s, D), bf16)]
        + [jax.ShapeDtypeStruct((NGRP, 256, 256), bf16)] * 2,
        scratch_shapes=[pltpu.VMEM((WT_ROWS, D), bf16), pltpu.VMEM((tm, D), f32), pltpu.VMEM((tm, D), bf16)]
        + _comm_sems(2),
        compiler_params=pltpu.CompilerParams(vmem_limit_bytes=32 * MIB),
    )(wt_shard, conv_blk, ln_gain, w_rgate, w_igate, *_in_hbm(x))


def _adam_math(w, g, m, v):
    m2 = ADAM_B1 * m + (1.0 - ADAM_B1) * g
    v2 = ADAM_B2 * v + (1.0 - ADAM_B2) * (g * g)
    m_hat = m2 / (1.0 - ADAM_B1 ** ADAM_STEP)
    v_hat = v2 / (1.0 - ADAM_B2 ** ADAM_STEP)
    delta = -ADAM_LR * (m_hat / (jnp.sqrt(v_hat) + ADAM_EPS) + ADAM_WD * w)
    return delta, m2, v2


def _reduce_adamw(land, w, m, v, tr, name):
    terms, rows, cols = land.shape

    def body(l_ref, w_ref, m_ref, v_ref, g_ref, d_ref, m2_ref, v2_ref):
        g = l_ref[0].astype(f32)
        for t in range(1, terms):
            g = g + l_ref[t].astype(f32)
        g_ref[...] = g
        d_ref[...], m2_ref[...], v2_ref[...] = _adam_math(w_ref[...], g, m_ref[...], v_ref[...])

    blk = pl.BlockSpec((tr, cols), lambda i: (i, 0))
    return pl.pallas_call(
        body, name=name, grid=(rows // tr,),
        in_specs=[pl.BlockSpec((terms, tr, cols), lambda i: (0, i, 0))] + [blk] * 3, out_specs=[blk] * 4,
        out_shape=[jax.ShapeDtypeStruct((rows, cols), f32)] * 4,
        compiler_params=_params(("arbitrary",), 32),
    )(*_in_hbm(land), w, m, v)


VEC_NAMES = ("ln_gain", "conv_b", "b_rgate", "b_igate", "lru_lambda", "attn_out_gain", "lru_out_gain", "final_gain")
ROW_RGATE, ROW_IGATE, ROW_VEC, ROW_SINKS = 0, 64, 128, 136
LOSS_LANE = NQ


def _adamw_small(g_rep, g_conv, w, m, v):
    names = list(VEC_NAMES) + ["sinks", "conv_w", "w_rgate", "w_igate"]
    ins = [g_rep, g_conv] + [d[k] for k in names for d in (w, m, v)]

    def body(*refs):
        g_ref, gc_ref = refs[0], refs[1]
        in_refs = refs[2:2 + 3 * len(names)]
        out_refs = refs[2 + 3 * len(names):]

        def update(j, g, at=None):
            w_ref, m_ref, v_ref = in_refs[3 * j:3 * j + 3]
            outs = out_refs[4 * j:4 * j + 4]
            pick = (lambda r: r[...]) if at is None else (lambda r: r[at])
            res = (g,) + _adam_math(pick(w_ref), g, pick(m_ref), pick(v_ref))
            for o_ref, val in zip(outs, res):
                if at is None:
                    o_ref[...] = val
                else:
                    o_ref[at] = val

        for j in range(len(VEC_NAMES)):
            update(j, g_ref[ROW_VEC + j:ROW_VEC + j + 1, :])
        update(len(VEC_NAMES), g_ref[ROW_SINKS:ROW_SINKS + 1, 0:NQ])
        update(len(VEC_NAMES) + 1, gc_ref[...], at=0)
        for gi, row0 in ((len(VEC_NAMES) + 2, ROW_RGATE), (len(VEC_NAMES) + 3, ROW_IGATE)):
            for nb in range(NQ):
                update(gi, g_ref[row0:row0 + HD, HD * nb:HD * nb + HD], at=(0, nb))

    vmem = pl.BlockSpec(memory_space=pltpu.VMEM)
    out_shape = [jax.ShapeDtypeStruct(w[k].shape, f32) for k in names for _ in range(4)]
    outs = pl.pallas_call(
        body, name="adamw_small",
        in_specs=[vmem] * len(ins), out_specs=[vmem] * len(out_shape), out_shape=out_shape,
        compiler_params=pltpu.CompilerParams(vmem_limit_bytes=32 * MIB),
    )(*ins)
    return {k: tuple(outs[4 * j:4 * j + 4]) for j, k in enumerate(names)}


def _pack_small(small, loss):
    gate = lambda g: g.transpose(1, 0, 2).reshape(HD, NQ * HD)
    row_s = jnp.concatenate([small["sinks"], loss[:, LOSS_LANE:128], jnp.zeros((1, D - 128), f32)], axis=1)
    rep = jnp.concatenate([gate(small["w_rgate"]), gate(small["w_igate"])] + [small[k] for k in VEC_NAMES]
                          + [row_s, jnp.zeros((SMALL_ROWS - ROW_SINKS - 1, D), f32)], axis=0)
    conv = small["conv_w"].reshape(CONVW, NDEV, 128).transpose(1, 0, 2)
    conv = jnp.pad(conv, ((0, 0), (0, 8 - CONVW), (0, D - 128)))
    return jnp.concatenate([rep.reshape(NDEV, SMALL_PER, D), conv], axis=1).reshape(NDEV * (SMALL_PER + 8), D)


def kernel(x, ln_gain, w_in, sinks, conv_w, conv_b, w_rgate, b_rgate, w_igate, b_igate, lru_lambda, attn_out_gain, lru_out_gain, w_out, final_gain, loss_target, m_ln_gain, m_w_in, m_sinks, m_conv_w, m_conv_b, m_w_rgate, m_b_rgate, m_w_igate, m_b_igate, m_lru_lambda, m_attn_out_gain, m_lru_out_gain, m_w_out, m_final_gain, v_ln_gain, v_w_in, v_sinks, v_conv_w, v_conv_b, v_w_rgate, v_b_rgate, v_w_igate, v_b_igate, v_lru_lambda, v_attn_out_gain, v_lru_out_gain, v_w_out, v_final_gain):
    w = dict(ln_gain=ln_gain, sinks=sinks, conv_w=conv_w, conv_b=conv_b, w_rgate=w_rgate, b_rgate=b_rgate,
             w_igate=w_igate, b_igate=b_igate, lru_lambda=lru_lambda, attn_out_gain=attn_out_gain,
             lru_out_gain=lru_out_gain, final_gain=final_gain.reshape(1, D))
    m = dict(ln_gain=m_ln_gain, sinks=m_sinks, conv_w=m_conv_w, conv_b=m_conv_b, w_rgate=m_w_rgate,
             b_rgate=m_b_rgate, w_igate=m_w_igate, b_igate=m_b_igate, lru_lambda=m_lru_lambda,
             attn_out_gain=m_attn_out_gain, lru_out_gain=m_lru_out_gain, final_gain=m_final_gain.reshape(1, D))
    v = dict(ln_gain=v_ln_gain, sinks=v_sinks, conv_w=v_conv_w, conv_b=v_conv_b, w_rgate=v_w_rgate,
             b_rgate=v_b_rgate, w_igate=v_w_igate, b_igate=v_b_igate, lru_lambda=v_lru_lambda,
             attn_out_gain=v_attn_out_gain, lru_out_gain=v_lru_out_gain, final_gain=v_final_gain.reshape(1, D))

    conv_blk = jnp.pad(conv_w[0], ((0, 8 - CONVW), (0, 0)))
    wt, cw_all, h, wr, wi = _gather_weights(w_in[0].T, conv_blk, x[0], ln_gain, w_rgate[0], w_igate[0],
                                            min(512, x.shape[1]))
    conv_full = cw_all.reshape(NDEV, 8, 128)[:, 0:CONVW].transpose(1, 0, 2).reshape(CONVW, LW)

    p = {k: w[k] for k in w if k not in ("conv_w", "w_rgate", "w_igate")}
    gx, land_wt, land_wo, g_rep, g_tail = _sequence_step(
        x[0], h, loss_target[0], wt, w_out[0], conv_full, wr, wi, p)
    g_conv = g_tail[0:CONVW, 0:128]

    wins = _reduce_adamw(land_wt, w_in[0].T, m_w_in[0].T, v_w_in[0].T, 192, "adamw_w_in")
    g_win, d_win, m_win, v_win = (t.T for t in wins)
    g_wo, d_wo, m_wo, v_wo = _reduce_adamw(land_wo, w_out[0], m_w_out[0], v_w_out[0], 256, "adamw_w_out")
    res = _adamw_small(g_rep, g_conv, w, m, v)
    res["w_in"] = tuple(t[None] for t in (g_win, d_win, m_win, v_win))
    res["w_out"] = tuple(t[None] for t in (g_wo, d_wo, m_wo, v_wo))
    res["final_gain"] = tuple(t.reshape(D) for t in res["final_gain"])

    order = ("ln_gain", "w_in", "sinks", "conv_w", "conv_b", "w_rgate", "b_rgate", "w_igate", "b_igate",
             "lru_lambda", "attn_out_gain", "lru_out_gain", "w_out", "final_gain")
    total_loss = g_rep[ROW_SINKS, LOSS_LANE]
    return (total_loss, gx[None]) + tuple(res[k][i] for i in range(4) for k in order)
```

```python
import jax
import jax.numpy as jnp
from jax import lax
from jax.experimental import pallas as pl
from jax.experimental.pallas import tpu as pltpu

f32 = jnp.float32
bf16 = jnp.bfloat16

D = 1024
HD = 64
NQ = 16
NKV = 4
GROUP = NQ // NKV
KVW = NKV * HD
BLK = 128
ROT = 16
THETA = 500000.0
NEG = -1e30
LW = 1024
NGRP = 4
CONVW = 4
LRU_C = 8.0
NIN = 4608
EPS = 1e-6
NDEV = 8
WT_ROWS = NIN // NDEV
WO_ROWS = 2 * D // NDEV
SMALL_ROWS = 192
SMALL_PER = SMALL_ROWS // NDEV

ADAM_LR = 0.001
ADAM_B1 = 0.9
ADAM_B2 = 0.999
ADAM_EPS = 1e-08
ADAM_WD = 0.01
ADAM_STEP = 10

NT = (((1,), (1,)), ((), ()))
TN = (((0,), (0,)), ((), ()))
MESH = pl.DeviceIdType.MESH
MIB = 1024 * 1024


def _dot(a, b):
    return jnp.dot(a, b, preferred_element_type=f32)


def _dot_nt(a, b):
    return lax.dot_general(a, b, NT, preferred_element_type=f32)


def _dot_tn(a, b):
    return lax.dot_general(a, b, TN, preferred_element_type=f32)


def _params(sem, vmem_mib):
    return pltpu.CompilerParams(dimension_semantics=sem, vmem_limit_bytes=vmem_mib * MIB)


def _sigmoid(x):
    return 0.5 * jnp.tanh(0.5 * x) + 0.5


def _softplus(x):
    return jnp.maximum(x, 0.0) + jnp.log(1.0 + jnp.exp(-jnp.abs(x)))


def _rope_tables(s):
    pos = jnp.arange(s, dtype=f32)
    inv_freq = THETA ** (-jnp.arange(0, ROT, 2, dtype=f32) / ROT)
    ang = pos[:, None] * inv_freq[None, :]
    cs = jnp.concatenate([jnp.cos(ang) - 1.0, jnp.sin(ang)], axis=1)
    d = jnp.arange(128) % HD
    j = jnp.arange(ROT)[:, None]
    pick_c = ((d < ROT) & (j == d % (ROT // 2))).astype(f32)
    pick_sa = ((d >= ROT // 2) & (d < ROT) & (j == d)).astype(f32)
    pick_sb = -((d < ROT // 2) & (j == d + ROT // 2)).astype(f32)
    picks = jnp.concatenate([pick_c, pick_sa, pick_sb], axis=1)
    ones = jnp.concatenate([jnp.ones((1, 128), f32), jnp.zeros((1, 256), f32)], axis=1)
    return jnp.dot(cs, picks, precision=lax.Precision.HIGHEST) + ones


def _tables(tab_ref):
    return tab_ref[:, 0:128], tab_ref[:, 128:256], tab_ref[:, 256:384]


def _rope(t, c, sa, sb):
    return t * c + pltpu.roll(t, 8, 1) * sa + pltpu.roll(t, 120, 1) * sb


def _unrope_t(dr, c, sa, sb):
    return dr * c + pltpu.roll(dr * sa, 120, 0) + pltpu.roll(dr * sb, 8, 0)


def _place():
    return lax.axis_index("x"), lax.axis_index("y"), lax.axis_index("c")


def _gather_ops(mine_refs, out_refs, send_sems, recv_sems, local_sems):
    n = len(mine_refs)
    x, y, c = _place()
    me, sibling = (x, y, c), (x, y, 1 - c)
    chips = [(1 - x, y), (x, 1 - y), (1 - x, 1 - y)]

    def rows(a, dev):
        m = mine_refs[a].shape[0]
        return out_refs[a].at[pl.ds((4 * dev[0] + 2 * dev[1] + dev[2]) * m, m), :]

    def copy(a, k, block, to, own=False):
        return pltpu.make_async_remote_copy(
            src_ref=mine_refs[a] if own else rows(a, block), dst_ref=rows(a, block),
            send_sem=send_sems.at[a, k], recv_sem=recv_sems.at[a, k], device_id=to, device_id_type=MESH)

    def local(a):
        return pltpu.make_async_copy(mine_refs[a], rows(a, me), local_sems.at[a])

    def first(a):
        return [copy(a, 0, me, sibling, own=True)] + [copy(a, 1 + j, me, (*chip, c), own=True)
                                                      for j, chip in enumerate(chips)]

    def start():
        for a in range(n):
            local(a).start()
            for cp in first(a):
                cp.start()

    def pass_on():
        for j, chip in enumerate(chips):
            for a in range(n):
                copy(a, 1 + j, (*chip, c), me).wait_recv()
                copy(a, 4 + j, (*chip, c), sibling).start()

    def finish():
        for a in range(n):
            copy(a, 0, sibling, me).wait_recv()
            for j, chip in enumerate(chips):
                copy(a, 4 + j, (*chip, 1 - c), me).wait_recv()
        for a in range(n):
            for cp in first(a) + [copy(a, 4 + j, (*chip, c), sibling) for j, chip in enumerate(chips)]:
                cp.wait_send()
            local(a).wait()

    return start, pass_on, finish


def _relay_gather_ops(mine_refs, out_refs, send_sems, recv_sems, local_sems):
    n = len(mine_refs)
    x, y, c = _place()
    me, sibling = (x, y, c), (x, y, 1 - c)
    near = (x ^ (1 - c), y ^ c)
    far = (x ^ c, y ^ (1 - c))
    diag = (1 - x, 1 - y)

    def rows(a, dev):
        m = mine_refs[a].shape[0]
        return out_refs[a].at[pl.ds((4 * dev[0] + 2 * dev[1] + dev[2]) * m, m), :]

    def copy(a, k, block, to, own=False):
        return pltpu.make_async_remote_copy(
            src_ref=mine_refs[a] if own else rows(a, block), dst_ref=rows(a, block),
            send_sem=send_sems.at[a, k], recv_sem=recv_sems.at[a, k], device_id=to, device_id_type=MESH)

    def local(a):
        return pltpu.make_async_copy(mine_refs[a], rows(a, me), local_sems.at[a])

    def sends(a):
        return [copy(a, 0, me, sibling, own=True), copy(a, 1, me, (*near, c), own=True),
                copy(a, 2, me, (*far, c), own=True), copy(a, 3, (*near, c), (*far, c)),
                copy(a, 4, (*near, c), sibling), copy(a, 5, (*far, c), sibling), copy(a, 6, (*diag, c), sibling)]

    def arrivals(a):
        return [copy(a, 0, sibling, me), copy(a, 1, (*near, c), me), copy(a, 2, (*far, c), me),
                copy(a, 3, (*diag, c), me), copy(a, 4, (*far, 1 - c), me), copy(a, 5, (*near, 1 - c), me),
                copy(a, 6, (*diag, 1 - c), me)]

    def start():
        for a in range(n):
            local(a).start()
            for cp in sends(a)[0:3]:
                cp.start()

    def finish():
        for first, then in ((1, (3, 4)), (2, (5,)), (3, (6,))):
            for a in range(n):
                arrivals(a)[first].wait_recv()
                for k in then:
                    sends(a)[k].start()
        for a in range(n):
            for k in (0, 4, 5, 6):
                arrivals(a)[k].wait_recv()
        for a in range(n):
            for cp in sends(a):
                cp.wait_send()
            local(a).wait()

    return start, finish


def _scatter_ops(src_refs, land_refs, send_sems, recv_sems, local_sems):
    n = len(src_refs)
    x, y, c = _place()
    my = 4 * x + 2 * y + c

    def peer(k):
        return x ^ (k >> 2), y ^ ((k >> 1) & 1), c ^ (k & 1)

    def piece(a, dev):
        m = src_refs[a].shape[0] // NDEV
        return src_refs[a].at[pl.ds(dev * m, m), :]

    def local(a):
        return pltpu.make_async_copy(piece(a, my), land_refs[a].at[my], local_sems.at[a])

    def send(a, k):
        px, py, pc = peer(k)
        return pltpu.make_async_remote_copy(
            src_ref=piece(a, 4 * px + 2 * py + pc), dst_ref=land_refs[a].at[my],
            send_sem=send_sems.at[a, k - 1], recv_sem=recv_sems.at[a, k - 1],
            device_id=(px, py, pc), device_id_type=MESH)

    def arrival(a, k):
        px, py, pc = peer(k)
        return pltpu.make_async_remote_copy(
            src_ref=piece(a, my), dst_ref=land_refs[a].at[4 * px + 2 * py + pc],
            send_sem=send_sems.at[a, k - 1], recv_sem=recv_sems.at[a, k - 1],
            device_id=(px, py, pc), device_id_type=MESH)

    def start():
        for a in range(n):
            local(a).start()
        for k in range(1, NDEV):
            for a in range(n):
                send(a, k).start()

    def finish():
        for k in range(1, NDEV):
            for a in range(n):
                send(a, k).wait_send()
        for k in range(1, NDEV):
            for a in range(n):
                arrival(a, k).wait_recv()
        for a in range(n):
            local(a).wait()

    return start, finish


def _in_hbm(*arrays):
    return tuple(pltpu.with_memory_space_constraint(a, pltpu.HBM) for a in arrays)


def _comm_sems(n):
    return [pltpu.SemaphoreType.DMA((n, 7)), pltpu.SemaphoreType.DMA((n, 7)), pltpu.SemaphoreType.DMA((n,))]


HBM = pl.BlockSpec(memory_space=pltpu.HBM)


def _sink_rows(sinks):
    return jnp.repeat(sinks.reshape(NKV, GROUP), BLK, axis=1)


def _band_softmax(s2_ref, ls, prev_offset, sink_row):
    jj = lax.broadcasted_iota(jnp.int32, (BLK, BLK), 0)
    ii = lax.broadcasted_iota(jnp.int32, (BLK, BLK), 1)
    from_prev = jj > ii
    sc = jnp.where(from_prev, s2_ref[0:BLK, ls] + prev_offset, s2_ref[BLK:2 * BLK, ls])
    m = jnp.maximum(jnp.max(sc, axis=0, keepdims=True), sink_row)
    p = jnp.exp(sc - m)
    es = jnp.exp(sink_row - m)
    inv = 1.0 / (jnp.sum(p, axis=0, keepdims=True) + es)
    return from_prev, p * inv, es * inv


def _put_split(dst_ref, ls, t, from_prev):
    t = t.astype(bf16)
    zero = jnp.zeros_like(t)
    dst_ref[0:BLK, ls] = jnp.where(from_prev, t, zero)
    dst_ref[BLK:2 * BLK, ls] = jnp.where(from_prev, zero, t)


def _heads_side_by_side(ref, h):
    return jnp.concatenate([ref[HD * (GROUP * h + g):HD * (GROUP * h + g) + HD, :] for g in range(GROUP)], axis=1)


def _kv_specs_t():
    prev = pl.BlockSpec((KVW, BLK), lambda n: (0, jnp.maximum(n - 1, 0)))
    cur = pl.BlockSpec((KVW, BLK), lambda n: (0, n))
    return [prev, cur, prev, cur]


def _attn_fwd_t(qt, kt, vt, sinks):
    s = qt.shape[1]

    def body(sink_ref, q_ref, kp_ref, kc_ref, vp_ref, vc_ref, o_ref, s2_scr, pn2_scr):
        n = pl.program_id(0)
        off = jnp.where(n > 0, 0.0, NEG)

        def scores(h):
            hs = slice(HD * h, HD * h + HD)
            kh = jnp.concatenate([kp_ref[hs, :], kc_ref[hs, :]], axis=1)
            s2_scr[h % 2] = _dot_tn(kh, _heads_side_by_side(q_ref, h))

        def probs(h):
            for g in range(GROUP):
                ls = slice(BLK * g, BLK * g + BLK)
                from_prev, pn, _ = _band_softmax(s2_scr.at[h % 2], ls, off, sink_ref[h:h + 1, ls])
                _put_split(pn2_scr.at[h % 2], ls, pn, from_prev)

        def outputs(h):
            hs = slice(HD * h, HD * h + HD)
            vh = jnp.concatenate([vp_ref[hs, :], vc_ref[hs, :]], axis=1)
            og = _dot(vh, pn2_scr[h % 2])
            for g in range(GROUP):
                a = GROUP * h + g
                o_ref[HD * a:HD * a + HD, :] = og[:, BLK * g:BLK * g + BLK]

        scores(0)
        for h in range(NKV):
            if h + 1 < NKV:
                scores(h + 1)
            probs(h)
            outputs(h)

    return pl.pallas_call(
        body, name="attn_fwd", grid=(s // BLK,),
        in_specs=[pl.BlockSpec((NKV, GROUP * BLK), lambda n: (0, 0)), pl.BlockSpec((D, BLK), lambda n: (0, n))]
        + _kv_specs_t(),
        out_specs=pl.BlockSpec((D, BLK), lambda n: (0, n)),
        out_shape=pltpu.HBM((D, s), f32),
        scratch_shapes=[pltpu.VMEM((2, 2 * BLK, GROUP * BLK), f32), pltpu.VMEM((2, 2 * BLK, GROUP * BLK), bf16)],
        compiler_params=_params(("arbitrary",), 32),
    )(_sink_rows(sinks), *_in_hbm(qt, kt, kt, vt, vt))


def _attn_bwd_t(qt, kt, vt, dot, sinks, dwo):
    s = qt.shape[1]
    nb = s // BLK

    def body(sink_ref, q_ref, do_ref, kp_ref, kc_ref, vp_ref, vc_ref, dwo_ref, dq_ref, dk_ref, dv_ref, ds_ref,
             land_ref, dk_hold, dv_hold, s2_scr, dp2_scr, pn2_scr, ds2_scr, send_sems, recv_sems, local_sems):
        n = pl.program_id(0)
        start, finish = _scatter_ops([dwo_ref], [land_ref], send_sems, recv_sems, local_sems)

        @pl.when(n == 0)
        def _():
            start()
            dk_hold[...] = jnp.zeros_like(dk_hold)
            dv_hold[...] = jnp.zeros_like(dv_hold)
            ds_ref[...] = jnp.zeros_like(ds_ref)

        @pl.when(n < nb)
        def _():
            off = jnp.where(n > 0, 0.0, NEG)

            def scores(h):
                hs = slice(HD * h, HD * h + HD)
                kh = jnp.concatenate([kp_ref[hs, :], kc_ref[hs, :]], axis=1)
                vh = jnp.concatenate([vp_ref[hs, :], vc_ref[hs, :]], axis=1)
                s2_scr[h % 2] = _dot_tn(kh, _heads_side_by_side(q_ref, h))
                dp2_scr[h % 2] = _dot_tn(vh, _heads_side_by_side(do_ref, h))

            def softmax_bwd(h):
                for g in range(GROUP):
                    ls = slice(BLK * g, BLK * g + BLK)
                    from_prev, pn, ps = _band_softmax(s2_scr.at[h % 2], ls, off, sink_ref[h:h + 1, ls])
                    dp = jnp.where(from_prev, dp2_scr[h % 2, 0:BLK, ls], dp2_scr[h % 2, BLK:2 * BLK, ls])
                    dsum = jnp.sum(pn * dp, axis=0, keepdims=True)
                    ds_ref[h:h + 1, ls] += -ps * dsum
                    _put_split(pn2_scr.at[h % 2], ls, pn, from_prev)
                    _put_split(ds2_scr.at[h % 2], ls, pn * (dp - dsum), from_prev)

            def grads(h):
                hs = slice(HD * h, HD * h + HD)
                kh = jnp.concatenate([kp_ref[hs, :], kc_ref[hs, :]], axis=1)
                dqg = _dot(kh, ds2_scr[h % 2])
                for g in range(GROUP):
                    a = GROUP * h + g
                    dq_ref[HD * a:HD * a + HD, :] = dqg[:, BLK * g:BLK * g + BLK]
                dkh = _dot_nt(_heads_side_by_side(q_ref, h), ds2_scr[h % 2])
                dvh = _dot_nt(_heads_side_by_side(do_ref, h), pn2_scr[h % 2])
                dk_ref[hs, :] = dk_hold[hs, :] + dkh[:, 0:BLK]
                dv_ref[hs, :] = dv_hold[hs, :] + dvh[:, 0:BLK]
                dk_hold[hs, :] = dkh[:, BLK:2 * BLK]
                dv_hold[hs, :] = dvh[:, BLK:2 * BLK]

            scores(0)
            for h in range(NKV):
                if h + 1 < NKV:
                    scores(h + 1)
                softmax_bwd(h)
                grads(h)

        @pl.when(n == nb)
        def _():
            dk_ref[...] = dk_hold[...]
            dv_ref[...] = dv_hold[...]
            finish()

    blk = pl.BlockSpec((D, BLK), lambda n: (0, jnp.minimum(n, nb - 1)))
    late = pl.BlockSpec((KVW, BLK), lambda n: (0, jnp.maximum(n - 1, 0)))
    whole = pl.BlockSpec((NKV, GROUP * BLK), lambda n: (0, 0))
    kv = [pl.BlockSpec((KVW, BLK), lambda n: (0, jnp.clip(n - 1, 0, nb - 1))),
          pl.BlockSpec((KVW, BLK), lambda n: (0, jnp.minimum(n, nb - 1)))]
    return pl.pallas_call(
        body, name="attn_bwd", grid=(nb + 1,),
        in_specs=[whole, blk, blk] + kv + kv + [HBM],
        out_specs=[blk, late, late, whole, HBM],
        out_shape=[pltpu.HBM((D, s), f32), pltpu.HBM((KVW, s), f32), pltpu.HBM((KVW, s), f32),
                   jax.ShapeDtypeStruct((NKV, GROUP * BLK), f32), pltpu.HBM((NDEV, WO_ROWS, D), bf16)],
        scratch_shapes=[pltpu.VMEM((KVW, BLK), f32), pltpu.VMEM((KVW, BLK), f32)]
        + [pltpu.VMEM((2, 2 * BLK, GROUP * BLK), f32)] * 2 + [pltpu.VMEM((2, 2 * BLK, GROUP * BLK), bf16)] * 2
        + _comm_sems(1),
        compiler_params=_params(("arbitrary",), 48),
    )(_sink_rows(sinks), *_in_hbm(qt, dot, kt, kt, vt, vt, dwo))


def _decay_terms(r, sp):
    a = jnp.exp(r * (-LRU_C * sp))
    n = r * (2.0 * LRU_C * sp)
    y = jnp.where(n < 0.02, n * (1.0 - n * (0.5 - n * (1.0 / 6.0))), 1.0 - a * a)
    inv_mult = lax.rsqrt(jnp.maximum(y, 1e-30))
    return a, y * inv_mult, inv_mult


def _later(x, before, k):
    if k == 0:
        return x
    row = lax.broadcasted_iota(jnp.int32, before.shape, 0)
    rolled = pltpu.roll(x, k, 0)
    first = jnp.where(row < k, pltpu.roll(before, k, 0), rolled[0:8])
    return jnp.concatenate([first, rolled[8:]], axis=0)


def _earlier(x, after, k):
    if k == 0:
        return x
    n = x.shape[0]
    row = lax.broadcasted_iota(jnp.int32, after.shape, 0)
    rolled = pltpu.roll(x, n - k, 0)
    last = jnp.where(row >= 8 - k, pltpu.roll(after, 8 - k, 0), rolled[n - 8:n])
    return jnp.concatenate([rolled[0:n - 8], last], axis=0)


def _fwd_fused(h, wt, tabs, wo_shard, conv_w, conv_b, wr, wi, br, bi, lam, tm):
    s = h.shape[0]
    nt = s // tm
    nc = 512
    pieces = 8
    rows_per = tm // pieces
    later_chunks = (0, 1, 2, 3, 4, 7, 8)

    def body(h_ref, wt_ref, tab_ref, wo_ref, cw_ref, cb_ref, wr_ref, wi_ref, br_ref,
             bi_ref, lam_ref, q_ref, k_ref, v_ref, ga_ref, xl_ref, gl_ref, u_ref, hl_ref, r_ref, ig_ref,
             wo_all, wo_stage, halo, ub_scr, pr_scr, pi_scr, b_scr, a_scr, hcar,
             send_sems, recv_sems, local_sems):
        i = pl.program_id(0)
        start, pass_on, finish = _gather_ops([wo_stage], [wo_all], send_sems, recv_sems, local_sems)

        @pl.when(i == 0)
        def _():
            wo_stage[...] = wo_ref[...].astype(bf16)
            start()
            halo[...] = jnp.zeros_like(halo)
            hcar[...] = jnp.zeros_like(hcar)

        sp = _softplus(-lam_ref[...])
        br, bi = br_ref[...], bi_ref[...]
        c, sa, sb = _tables(tab_ref)
        piece_rows = lambda p: slice(rows_per * p, rows_per * p + rows_per)

        def project(ci):
            z = _dot_nt(h_ref[...], wt_ref[ci * nc:(ci + 1) * nc, :])
            if ci < 2:
                for j in range(nc // 128):
                    r = _rope(z[:, 128 * j:128 * j + 128], c, sa, sb) * (HD ** -0.5)
                    q_ref[ci * nc + 128 * j:ci * nc + 128 * j + 128, :] = r.astype(bf16).T
            elif ci == 2:
                for j in range(2):
                    js = slice(128 * j, 128 * j + 128)
                    k_ref[js, :] = _rope(z[:, js], c, sa, sb).astype(bf16).T
                    v_ref[js, :] = z[:, KVW + 128 * j:KVW + 128 * j + 128].astype(bf16).T
            else:
                sec, j = divmod(ci - 3, 2)
                (ga_ref, xl_ref, gl_ref)[sec][:, j * nc:(j + 1) * nc] = z

        def gate_terms(p):
            rows = piece_rows(p)
            r = _sigmoid(pr_scr[rows, :] + br)
            ig = _sigmoid(pi_scr[rows, :] + bi)
            a, mult, _ = _decay_terms(r, sp)
            r_ref[rows, :] = r
            ig_ref[rows, :] = ig
            a_scr[rows, :] = a
            b_scr[rows, :] = mult * (ig * u_ref[rows, :])

        def scan(p, hc):
            for t in range(rows_per * p, rows_per * p + rows_per):
                hc = a_scr[t:t + 1, :] * hc + b_scr[t:t + 1, :]
                hl_ref[t:t + 1, :] = hc
            return hc

        project(5)
        project(6)
        xl = xl_ref[...]
        u = cb_ref[...] + sum(cw_ref[k:k + 1, :] * _later(xl, halo[...], CONVW - 1 - k) for k in range(CONVW))
        halo[...] = xl[tm - 8:tm, :]
        u_ref[...] = u
        ub_scr[...] = u.astype(bf16)
        for g in range(NGRP):
            gs = slice(256 * g, 256 * g + 256)
            pr_scr[:, gs] = _dot(ub_scr[:, gs], wr_ref[g])
            pi_scr[:, gs] = _dot(ub_scr[:, gs], wi_ref[g])
        hc = hcar[...]
        gate_terms(0)
        for slot, ci in enumerate(later_chunks):
            project(ci)
            gate_terms(slot + 1)
            hc = scan(slot, hc)
        hcar[...] = scan(pieces - 1, hc)

        @pl.when(i == max(nt - 2, 0))
        def _():
            pass_on()

        @pl.when(i == nt - 1)
        def _():
            finish()

    row = lambda w: pl.BlockSpec((tm, w), lambda i: (i, 0))
    col = lambda w: pl.BlockSpec((w, tm), lambda i: (0, i))
    full = lambda a: pl.BlockSpec(a.shape, lambda i: (0,) * a.ndim)
    big = lambda w, dt: pltpu.HBM((s, w), dt)
    tile = pltpu.VMEM((tm, LW), f32)
    return pl.pallas_call(
        body, name="fwd_fused", grid=(nt,),
        in_specs=[row(D), full(wt), row(384), full(wo_shard), full(conv_w), full(conv_b),
                  full(wr), full(wi), full(br), full(bi), full(lam)],
        out_specs=[col(D), col(KVW), col(KVW), row(D), row(D), row(D)] + [row(LW)] * 4 + [HBM],
        out_shape=[pltpu.HBM((D, s), bf16), pltpu.HBM((KVW, s), bf16), pltpu.HBM((KVW, s), bf16),
                   big(D, f32), big(D, f32), big(D, f32)] + [big(LW, f32)] * 4 + [pltpu.HBM((2 * D, D), bf16)],
        scratch_shapes=[pltpu.VMEM((WO_ROWS, D), bf16), pltpu.VMEM((8, LW), f32), pltpu.VMEM((tm, LW), bf16)]
        + [tile] * 4 + [pltpu.VMEM((1, LW), f32)] + _comm_sems(1),
        compiler_params=_params(("arbitrary",), 56),
    )(*_in_hbm(h, wt), tabs, wo_shard, conv_w, conv_b, wr, wi, br, bi, lam)


def _lru_bwd(u, hl, dhl, xl, r, ig, conv_w, wr, wi, lam, tm):
    s = u.shape[0]
    nt = s // tm
    pieces = 8
    rows_per = tm // pieces

    def body(u_ref, h_ref, hp_ref, dh_ref, x_ref, r_ref, ig_ref, cw_ref, wr_ref, wi_ref,
             lam_ref, dxl_ref, dwr_ref, dwi_ref, dbr_ref, dbi_ref, dlam_ref, dcb_ref, dcw_ref,
             l_scr, du_scr, a_scr, mu_scr, im_scr, dpr_scr, dpi_scr, lcar, dunext):
        t0 = pl.program_id(0)
        tile = nt - 1 - t0

        @pl.when(t0 == 0)
        def _():
            lcar[...] = jnp.zeros_like(lcar)
            dunext[...] = jnp.zeros_like(dunext)
            for ref in (dwr_ref, dwi_ref, dbr_ref, dbi_ref, dlam_ref, dcb_ref, dcw_ref):
                ref[...] = jnp.zeros_like(ref)

        lam = lam_ref[...]
        sp = _softplus(-lam)
        hp = jnp.where(tile > 0, hp_ref[...], 0.0)

        def decay(p):
            rows = slice(rows_per * p, rows_per * p + rows_per)
            a_scr[rows, :], mu_scr[rows, :], im_scr[rows, :] = _decay_terms(r_ref[rows, :], sp)

        def scan(p, c):
            row = lax.broadcasted_iota(jnp.int32, (8, LW), 0)
            for t0 in range(rows_per * p + rows_per - 8, rows_per * p - 8, -8):
                x, a = dh_ref[t0:t0 + 8, :], a_scr[t0:t0 + 8, :]
                m = jnp.where(row < 7, pltpu.roll(a, 7, 0), 1.0)
                for d in (1, 2, 4):
                    x = x + m * jnp.where(row < 8 - d, pltpu.roll(x, 8 - d, 0), 0.0)
                    m = m * jnp.where(row < 8 - d, pltpu.roll(m, 8 - d, 0), 1.0)
                lt = x + m * c
                l_scr[t0:t0 + 8, :] = lt
                c = a[0:1, :] * lt[0:1, :]
            return c

        def terms(p, sums):
            rows = slice(rows_per * p, rows_per * p + rows_per)
            lt, u, r, i, a, mult, inv_mult = l_scr[rows, :], u_ref[rows, :], r_ref[rows, :], ig_ref[rows, :], \
                a_scr[rows, :], mu_scr[rows, :], im_scr[rows, :]
            before = hp if p == 0 else h_ref[rows_per * p - 8:rows_per * p, :]
            hprev = _later(h_ref[rows, :], before, 1)
            iu = i * u
            lm = lt * mult
            du_scr[rows, :] = lm * i
            dla = (lt * hprev) * a - ((lt * iu) * (a * a)) * inv_mult
            dlar = dla * r
            dpr = (dlar * (1.0 - r)) * (-LRU_C * sp)
            dpi = (lm * iu) * (1.0 - i)
            dpr_scr[rows, :] = dpr.astype(bf16)
            dpi_scr[rows, :] = dpi.astype(bf16)
            col = lambda t: jnp.sum(t, axis=0, keepdims=True)
            return sums[0] + col(dlar), sums[1] + col(dpr), sums[2] + col(dpi)

        sums = (jnp.zeros((1, LW), f32),) * 3
        decay(pieces - 1)
        c = scan(pieces - 1, lcar[...])
        for p in range(pieces - 1, -1, -1):
            if p > 0:
                decay(p - 1)
                c = scan(p - 1, c)
            sums = terms(p, sums)
        lcar[...] = c
        dlam_ref[...] += sums[0] * (-LRU_C)
        dbr_ref[...] += sums[1]
        dbi_ref[...] += sums[2]

        ub = u_ref[...].astype(bf16)
        dug = []
        for g in range(NGRP):
            gs = slice(256 * g, 256 * g + 256)
            dwr_ref[g] += _dot_tn(ub[:, gs], dpr_scr[:, gs])
            dwi_ref[g] += _dot_tn(ub[:, gs], dpi_scr[:, gs])
            dug.append(_dot_nt(dpr_scr[:, gs], wr_ref[g]) + _dot_nt(dpi_scr[:, gs], wi_ref[g]))
        du = du_scr[...] + jnp.concatenate(dug, axis=1)

        dcb_ref[...] += jnp.sum(du, axis=0, keepdims=True)
        x = x_ref[...]
        after = dunext[...]
        dxl = jnp.zeros_like(du)
        for k in range(CONVW):
            e = _earlier(du, after, CONVW - 1 - k)
            dxl = dxl + cw_ref[k:k + 1, :] * e
            dcw_ref[k:k + 1, :] += jnp.sum(e * x, axis=0, keepdims=True)
        dxl_ref[...] = dxl.astype(bf16)
        dunext[...] = du[0:8, :]

        @pl.when(t0 == nt - 1)
        def _():
            dlam_ref[...] = dlam_ref[...] * (-_sigmoid(-lam))

    rev = lambda i: (nt - 1 - i, 0)
    row = pl.BlockSpec((tm, LW), rev)
    prev8 = pl.BlockSpec((8, LW), lambda i: (jnp.maximum((nt - 1 - i) * (tm // 8) - 1, 0), 0))
    full = lambda a: pl.BlockSpec(a.shape, lambda i: (0,) * a.ndim)
    vec = pl.BlockSpec((1, LW), lambda i: (0, 0))
    bd = pl.BlockSpec((NGRP, 256, 256), lambda i: (0, 0, 0))
    return pl.pallas_call(
        body, name="lru_bwd", grid=(nt,),
        in_specs=[row, row, prev8] + [row] * 4 + [full(conv_w), full(wr), full(wi), full(lam)],
        out_specs=[row, bd, bd, vec, vec, vec, vec, pl.BlockSpec((CONVW, LW), lambda i: (0, 0))],
        out_shape=[pltpu.HBM((s, LW), bf16),
                   jax.ShapeDtypeStruct((NGRP, 256, 256), f32), jax.ShapeDtypeStruct((NGRP, 256, 256), f32),
                   jax.ShapeDtypeStruct((1, LW), f32), jax.ShapeDtypeStruct((1, LW), f32),
                   jax.ShapeDtypeStruct((1, LW), f32), jax.ShapeDtypeStruct((1, LW), f32),
                   jax.ShapeDtypeStruct((CONVW, LW), f32)],
        scratch_shapes=[pltpu.VMEM((tm, LW), f32)] * 5 + [pltpu.VMEM((tm, LW), bf16)] * 2
        + [pltpu.VMEM((1, LW), f32), pltpu.VMEM((8, LW), f32)],
        compiler_params=_params(("arbitrary",), 56),
    )(*_in_hbm(u, hl, hl, dhl, xl, r, ig), conv_w, wr, wi, lam)


def _gated_norm(t, gate, gain):
    sg = _sigmoid(gate)
    silu = gate * sg
    p = t * silu
    rstd = lax.rsqrt(jnp.mean(p * p, axis=-1, keepdims=True) + EPS)
    ph = p * rstd
    return sg, silu, rstd, ph, ph * gain


def _gated_norm_bwd(dy, t, gate, gain, sg, silu, rstd, ph):
    w = dy * gain
    dp = rstd * (w - ph * jnp.mean(w * ph, axis=-1, keepdims=True))
    dgate = (dp * t) * (sg + silu * (1.0 - sg))
    return jnp.sum(dy * ph, axis=0, keepdims=True), dp * silu, dgate


def _out_fwd_bwd(x, tgt, o, ga, hl, gl, again, lgain, fgain, wo, tm):
    s = x.shape[0]
    nt = s // tm

    def body(x_ref, t_ref, o_ref, ga_ref, hl_ref, gl_ref, ag_ref, lg_ref, fg_ref, wo_ref,
             dx2_ref, do_ref, dga_ref, dhl_ref, dgl_ref, dwo_ref, gfg_ref, gag_ref, glg_ref, loss_ref, acc):
        i = pl.program_id(0)

        @pl.when(i == 0)
        def _():
            acc[...] = jnp.zeros_like(acc)
            for ref in (gfg_ref, gag_ref, glg_ref, loss_ref):
                ref[...] = jnp.zeros_like(ref)

        oo = jnp.concatenate([o_ref[128 * j:128 * j + 128, :].T for j in range(D // 128)], axis=1)
        gga, hh, ggl = ga_ref[...], hl_ref[...], gl_ref[...]
        ag, lg, fg = ag_ref[...], lg_ref[...], fg_ref[...]
        sga, silua, ra, pah, ya = _gated_norm(oo, gga, ag)
        sgl, silul, rl, plh, yl = _gated_norm(hh, ggl, lg)
        yab, ylb = ya.astype(bf16), yl.astype(bf16)
        y = _dot(yab, wo_ref[0:D, :]) + _dot(ylb, wo_ref[D:2 * D, :])
        x2 = x_ref[...] + y
        r2 = lax.rsqrt(jnp.mean(x2 * x2, axis=-1, keepdims=True) + EPS)
        x2h = x2 * r2
        err = x2h * fg - t_ref[...]
        loss_ref[...] += 0.5 * jnp.sum(jnp.sum(err * err, axis=-1, keepdims=True) * (1.0 / D))
        gfg_ref[...] += jnp.sum(err * x2h, axis=0, keepdims=True) * (1.0 / D)
        w = err * (fg * (1.0 / D))
        dx2 = r2 * (w - x2h * jnp.mean(w * x2h, axis=-1, keepdims=True))
        dx2_ref[...] = dx2
        dyb = dx2.astype(bf16)
        acc[0:D, :] += _dot_tn(yab, dyb)
        acc[D:2 * D, :] += _dot_tn(ylb, dyb)
        dya = _dot_nt(dyb, wo_ref[0:D, :])
        dyl = _dot_nt(dyb, wo_ref[D:2 * D, :])
        gag, do, dga = _gated_norm_bwd(dya, oo, gga, ag, sga, silua, ra, pah)
        glg, dhl, dgl = _gated_norm_bwd(dyl, hh, ggl, lg, sgl, silul, rl, plh)
        gag_ref[...] += gag
        glg_ref[...] += glg
        dob = do.astype(bf16)
        for j in range(D // 128):
            do_ref[128 * j:128 * j + 128, :] = dob[:, 128 * j:128 * j + 128].T
        dga_ref[...] = dga.astype(bf16)
        dhl_ref[...] = dhl
        dgl_ref[...] = dgl.astype(bf16)

        @pl.when(i == nt - 1)
        def _():
            dwo_ref[...] = acc[...].astype(bf16)

    row = pl.BlockSpec((tm, D), lambda i: (i, 0))
    col = pl.BlockSpec((D, tm), lambda i: (0, i))
    vec = pl.BlockSpec((1, D), lambda i: (0, 0))
    mat = pl.BlockSpec((2 * D, D), lambda i: (0, 0))
    return pl.pallas_call(
        body, name="out_fwd_bwd", grid=(nt,),
        in_specs=[row, row, col, row, row, row] + [vec] * 3 + [mat],
        out_specs=[row, col, row, row, row] + [mat, vec, vec, vec, pl.BlockSpec((1, 128), lambda i: (0, 0))],
        out_shape=[pltpu.HBM((s, D), f32), pltpu.HBM((D, s), bf16),
                   pltpu.HBM((s, D), bf16), pltpu.HBM((s, D), f32),
                   pltpu.HBM((s, D), bf16), pltpu.HBM((2 * D, D), bf16),
                   jax.ShapeDtypeStruct((1, D), f32), jax.ShapeDtypeStruct((1, D), f32),
                   jax.ShapeDtypeStruct((1, D), f32), jax.ShapeDtypeStruct((1, 128), f32)],
        scratch_shapes=[pltpu.VMEM((2 * D, D), f32)],
        compiler_params=_params(("arbitrary",), 56),
    )(*_in_hbm(x, tgt, o, ga, hl, gl), again, lgain, fgain, *_in_hbm(wo))


def _bwd_in(x, dx2, dq, dk, dv, dga, dxl, dgl, ln_gain, wt, tabs, tm):
    s = x.shape[0]

    def body(x_ref, dx2_ref, dq_ref, dk_ref, dv_ref, dga_ref, dxl_ref, dgl_ref, g_ref, wt_ref,
             tab_ref, gx_ref, gln_ref, dzt_ref):
        @pl.when(pl.program_id(0) == 0)
        def _():
            gln_ref[...] = jnp.zeros_like(gln_ref)

        c, sa, sb = (t.T for t in _tables(tab_ref))
        for j in range(D // 128):
            js = slice(128 * j, 128 * j + 128)
            dzt_ref[js, :] = (_unrope_t(dq_ref[js, :], c, sa, sb) * (HD ** -0.5)).astype(bf16)
        for j in range(KVW // 128):
            js = slice(128 * j, 128 * j + 128)
            dzt_ref[D + 128 * j:D + 128 * j + 128, :] = _unrope_t(dk_ref[js, :], c, sa, sb).astype(bf16)
        dzt_ref[D + KVW:D + 2 * KVW, :] = dv_ref[...].astype(bf16)
        first = D + 2 * KVW
        dh = _dot_tn(dzt_ref[0:first, :], wt_ref[0:first, :])
        for sec, ref in enumerate((dga_ref, dxl_ref, dgl_ref)):
            dh = dh + _dot(ref[...], wt_ref[first + D * sec:first + D * sec + D, :])
            for j in range(D // 128):
                dzt_ref[first + D * sec + 128 * j:first + D * sec + 128 * j + 128, :] = ref[:, 128 * j:128 * j + 128].T
        xx = x_ref[...]
        rstd = lax.rsqrt(jnp.mean(xx * xx, axis=-1, keepdims=True) + EPS)
        xh = xx * rstd
        gln_ref[...] += jnp.sum(dh * xh, axis=0, keepdims=True)
        w = dh * g_ref[...]
        gx_ref[...] = dx2_ref[...] + rstd * (w - xh * jnp.mean(w * xh, axis=-1, keepdims=True))

    row = lambda w: pl.BlockSpec((tm, w), lambda i: (i, 0))
    col = lambda w: pl.BlockSpec((w, tm), lambda i: (0, i))
    full = lambda a: pl.BlockSpec(a.shape, lambda i: (0, 0))
    return pl.pallas_call(
        body, name="bwd_in", grid=(s // tm,),
        in_specs=[row(D), row(D), col(D), col(KVW), col(KVW), row(D), row(D), row(D), full(ln_gain), full(wt),
                  row(384)],
        out_specs=[row(D), pl.BlockSpec((1, D), lambda i: (0, 0)), col(NIN)],
        out_shape=[pltpu.HBM((s, D), f32), jax.ShapeDtypeStruct((1, D), f32),
                   pltpu.HBM((NIN, s), bf16)],
        compiler_params=_params(("arbitrary",), 56),
    )(*_in_hbm(x, dx2, dq, dk, dv, dga, dxl, dgl), ln_gain, *_in_hbm(wt), tabs)


WT_TERMS = 4


def _dwt_scatter(dzt, h, small, tm):
    s = h.shape[0]
    nk = s // tm
    srows = small.shape[0] // NDEV
    last = NDEV - 1
    sm_turn = 2

    def body(order_ref, dz_ref, h_ref, sm_ref, lwt_ref, rep_all, tail_ref, acc, stage, given, relayed, lsm, rep_stage,
             send_sems, recv_sems, local_sem, sm_send, sm_recv, sm_local, rep_send, rep_recv, rep_local):
        j, k = pl.program_id(0), pl.program_id(1)
        x, y, c = _place()
        sibling = (x, y, 1 - c)
        near = (x ^ (1 - c), y ^ c)
        far = (x ^ c, y ^ (1 - c))
        sm_start, sm_finish = _scatter_ops([sm_ref], [lsm], sm_send, sm_recv, sm_local)
        rep_start, rep_pass_on, rep_finish = _gather_ops([rep_stage], [rep_all], rep_send, rep_recv, rep_local)

        def send(step):
            if step == last - 1:
                dst, to = lwt_ref.at[1], sibling
            elif step % 2 == 0:
                dst, to = given.at[step // 2], sibling
            elif step == 1:
                dst, to = relayed, (*near, c)
            else:
                dst, to = lwt_ref.at[1 + step // 2], (*(near if step == 3 else far), c)
            return pltpu.make_async_remote_copy(
                src_ref=stage.at[step % 2], dst_ref=dst, send_sem=send_sems.at[step], recv_sem=recv_sems.at[step],
                device_id=to, device_id_type=MESH)

        def keep():
            return pltpu.make_async_copy(stage.at[last % 2], lwt_ref.at[0], local_sem)

        @pl.when((j == 0) & (k == 0))
        def _():
            sm_start()

        @pl.when(k == 0)
        def _():
            acc[...] = jnp.zeros_like(acc)

        acc[...] += _dot(dz_ref[...], h_ref[...])

        for step in range(NDEV):
            @pl.when((k == nk - 1) & (j == step))
            def _(step=step):
                if step >= 2:
                    send(step - 2).wait_send()
                if step % 2 == 1 and step < last:
                    send(step - 1).wait_recv()
                    total = acc[...] + given[step // 2].astype(f32)
                    if step == 5:
                        send(1).wait_recv()
                        total = total + relayed[...].astype(f32)
                    stage[step % 2] = total.astype(bf16)
                else:
                    stage[step % 2] = acc[...].astype(bf16)
                if step < last:
                    send(step).start()
                else:
                    keep().start()
                    send(last - 1).wait_send()
                    for peer_step in (3, 5, last - 1):
                        send(peer_step).wait_recv()
                    keep().wait()
                    rep_finish()
                if step == sm_turn:
                    sm_finish()
                    total_sm = lsm[0]
                    for dev in range(1, NDEV):
                        total_sm = total_sm + lsm[dev]
                    rep_stage[...] = total_sm[0:SMALL_PER]
                    tail_ref[...] = total_sm[SMALL_PER:]
                    rep_start()
                if step == last - 1:
                    rep_pass_on()

    x, y, c = _place()
    dest = lambda chip, cc: 4 * chip[0] + 2 * chip[1] + cc
    near, far, diag = (x ^ (1 - c), y ^ c), (x ^ c, y ^ (1 - c)), (1 - x, 1 - y)
    order = jnp.stack([dest(diag, 1 - c), dest(diag, c), dest(far, 1 - c), dest(near, c),
                       dest(near, 1 - c), dest(far, c), dest((x, y), 1 - c), dest((x, y), c)])
    return pl.pallas_call(
        body, name="dwt_scatter",
        grid_spec=pltpu.PrefetchScalarGridSpec(
            num_scalar_prefetch=1, grid=(NDEV, nk),
            in_specs=[pl.BlockSpec((WT_ROWS, tm), lambda j, k, order: (order[j], k)),
                      pl.BlockSpec((tm, D), lambda j, k, order: (k, 0)), HBM],
            out_specs=[HBM, HBM, pl.BlockSpec((srows - SMALL_PER, D), lambda j, k, order: (0, 0))],
            scratch_shapes=[pltpu.VMEM((WT_ROWS, D), f32), pltpu.VMEM((2, WT_ROWS, D), bf16),
                            pltpu.VMEM((3, WT_ROWS, D), bf16), pltpu.VMEM((WT_ROWS, D), bf16),
                            pltpu.VMEM((NDEV, srows, D), f32), pltpu.VMEM((SMALL_PER, D), f32),
                            pltpu.SemaphoreType.DMA((last,)), pltpu.SemaphoreType.DMA((last,)),
                            pltpu.SemaphoreType.DMA(())] + _comm_sems(1) + _comm_sems(1)),
        out_shape=[pltpu.HBM((WT_TERMS, WT_ROWS, D), bf16), pltpu.HBM((SMALL_ROWS, D), f32),
                   jax.ShapeDtypeStruct((srows - SMALL_PER, D), f32)],
        compiler_params=_params(("arbitrary", "arbitrary"), 48),
    )(order, *_in_hbm(dzt, h, small))


def _diag_blocks(bd):
    eye = jnp.eye(4, dtype=bd.dtype)
    return jnp.einsum('gjckd,jk->gjcd', bd.reshape(NGRP, 4, HD, 4, HD), eye).reshape(NQ, HD, HD)


def _sequence_step(x, h, tgt, wt, wo_shard, conv_w, wr, wi, p):
    s = x.shape[0]
    tm = min(256, s)
    tabs = _rope_tables(s)
    sinks = p["sinks"].reshape(NQ)
    qt, kt, vt, ga, xl, gl, u, hl, r, ig, wo = _fwd_fused(
        h, wt, tabs, wo_shard, conv_w, p["conv_b"], wr, wi, p["b_rgate"], p["b_igate"], p["lru_lambda"], tm)
    ot = _attn_fwd_t(qt, kt, vt, sinks)
    dx2, dot, dga, dhl, dgl, dwo, g_fg, g_ag, g_lg, loss = _out_fwd_bwd(
        x, tgt, ot, ga, hl, gl, p["attn_out_gain"], p["lru_out_gain"], p["final_gain"], wo, tm)
    dqt, dkt, dvt, dsink, land_wo = _attn_bwd_t(qt, kt, vt, dot, sinks, dwo)
    dxl, dwr, dwi, dbr, dbi, dlam, dcb, dcw = _lru_bwd(u, hl, dhl, xl, r, ig, conv_w, wr, wi, p["lru_lambda"], tm)
    gx, g_ln, dzt = _bwd_in(x, dx2, dqt, dkt, dvt, dga, dxl, dgl, p["ln_gain"], wt, tabs, tm)
    small = dict(ln_gain=g_ln, sinks=dsink.reshape(NQ, BLK).sum(axis=1)[None], conv_w=dcw, conv_b=dcb,
                 w_rgate=_diag_blocks(dwr), b_rgate=dbr, w_igate=_diag_blocks(dwi), b_igate=dbi, lru_lambda=dlam,
                 attn_out_gain=g_ag, lru_out_gain=g_lg, final_gain=g_fg)
    land_wt, g_rep, g_tail = _dwt_scatter(dzt, h, _pack_small(small, loss), min(2048, s))
    return gx, land_wt, land_wo, g_rep, g_tail


def _gather_weights(wt_shard, conv_blk, x, ln_gain, w_rgate, w_igate, tm):
    s = x.shape[0]

    def body(wt_ref, cw_ref, g_ref, wrg_ref, wig_ref, x_ref, wt_all, cw_all, h_ref, wr_ref, wi_ref,
             stage, xbuf, hbuf, send_sems, recv_sems, local_sems):
        stage[...] = wt_ref[...].astype(bf16)
        start, finish = _relay_gather_ops([stage, cw_ref], [wt_all, cw_all], send_sems, recv_sems, local_sems)
        start()
        for src, dst in ((wrg_ref, wr_ref), (wig_ref, wi_ref)):
            dst[...] = jnp.zeros_like(dst)
            for nb in range(NQ):
                g, j = divmod(nb, 4)
                dst[g, HD * j:HD * j + HD, HD * j:HD * j + HD] = src[nb].astype(bf16)
        gain = g_ref[...]
        for i in range(s // tm):
            rows = pl.ds(i * tm, tm)
            pltpu.sync_copy(x_ref.at[rows, :], xbuf)
            xx = xbuf[...]
            rstd = lax.rsqrt(jnp.mean(xx * xx, axis=-1, keepdims=True) + EPS)
            hbuf[...] = (xx * rstd * gain).astype(bf16)
            pltpu.sync_copy(hbuf, h_ref.at[rows, :])
        finish()

    vmem = pl.BlockSpec(memory_space=pltpu.VMEM)
    return pl.pallas_call(
        body, name="gather_weights",
        in_specs=[vmem] * 5 + [HBM], out_specs=[HBM, HBM, HBM, vmem, vmem],
        out_shape=[pltpu.HBM((NIN, D), bf16), pltpu.HBM((NDEV * 8, 128), f32), pltpu.HBM((s, D), bf16)]
        + [jax.ShapeDtypeStruct((NGRP, 256, 256), bf16)] * 2,
        scratch_shapes=[pltpu.VMEM((WT_ROWS, D), bf16), pltpu.VMEM((tm, D), f32), pltpu.VMEM((tm, D), bf16)]
        + _comm_sems(2),
        compiler_params=pltpu.CompilerParams(vmem_limit_bytes=32 * MIB),
    )(wt_shard, conv_blk, ln_gain, w_rgate, w_igate, *_in_hbm(x))


def _adam_math(w, g, m, v):
    m2 = ADAM_B1 * m + (1.0 - ADAM_B1) * g
    v2 = ADAM_B2 * v + (1.0 - ADAM_B2) * (g * g)
    m_hat = m2 / (1.0 - ADAM_B1 ** ADAM_STEP)
    v_hat = v2 / (1.0 - ADAM_B2 ** ADAM_STEP)
    delta = -ADAM_LR * (m_hat / (jnp.sqrt(v_hat) + ADAM_EPS) + ADAM_WD * w)
    return delta, m2, v2


def _reduce_adamw(land, w, m, v, tr, name):
    terms, rows, cols = land.shape

    def body(l_ref, w_ref, m_ref, v_ref, g_ref, d_ref, m2_ref, v2_ref):
        g = l_ref[0].astype(f32)
        for t in range(1, terms):
            g = g + l_ref[t].astype(f32)
        g_ref[...] = g
        d_ref[...], m2_ref[...], v2_ref[...] = _adam_math(w_ref[...], g, m_ref[...], v_ref[...])

    blk = pl.BlockSpec((tr, cols), lambda i: (i, 0))
    return pl.pallas_call(
        body, name=name, grid=(rows // tr,),
        in_specs=[pl.BlockSpec((terms, tr, cols), lambda i: (0, i, 0))] + [blk] * 3, out_specs=[blk] * 4,
        out_shape=[jax.ShapeDtypeStruct((rows, cols), f32)] * 4,
        compiler_params=_params(("arbitrary",), 32),
    )(*_in_hbm(land), w, m, v)


VEC_NAMES = ("ln_gain", "conv_b", "b_rgate", "b_igate", "lru_lambda", "attn_out_gain", "lru_out_gain", "final_gain")
ROW_RGATE, ROW_IGATE, ROW_VEC, ROW_SINKS = 0, 64, 128, 136
LOSS_LANE = NQ


def _adamw_small(g_rep, g_conv, w, m, v):
    names = list(VEC_NAMES) + ["sinks", "conv_w", "w_rgate", "w_igate"]
    ins = [g_rep, g_conv] + [d[k] for k in names for d in (w, m, v)]

    def body(*refs):
        g_ref, gc_ref = refs[0], refs[1]
        in_refs = refs[2:2 + 3 * len(names)]
        out_refs = refs[2 + 3 * len(names):]

        def update(j, g, at=None):
            w_ref, m_ref, v_ref = in_refs[3 * j:3 * j + 3]
            outs = out_refs[4 * j:4 * j + 4]
            pick = (lambda r: r[...]) if at is None else (lambda r: r[at])
            res = (g,) + _adam_math(pick(w_ref), g, pick(m_ref), pick(v_ref))
            for o_ref, val in zip(outs, res):
                if at is None:
                    o_ref[...] = val
                else:
                    o_ref[at] = val

        for j in range(len(VEC_NAMES)):
            update(j, g_ref[ROW_VEC + j:ROW_VEC + j + 1, :])
        update(len(VEC_NAMES), g_ref[ROW_SINKS:ROW_SINKS + 1, 0:NQ])
        update(len(VEC_NAMES) + 1, gc_ref[...], at=0)
        for gi, row0 in ((len(VEC_NAMES) + 2, ROW_RGATE), (len(VEC_NAMES) + 3, ROW_IGATE)):
            for nb in range(NQ):
                update(gi, g_ref[row0:row0 + HD, HD * nb:HD * nb + HD], at=(0, nb))

    vmem = pl.BlockSpec(memory_space=pltpu.VMEM)
    out_shape = [jax.ShapeDtypeStruct(w[k].shape, f32) for k in names for _ in range(4)]
    outs = pl.pallas_call(
        body, name="adamw_small",
        in_specs=[vmem] * len(ins), out_specs=[vmem] * len(out_shape), out_shape=out_shape,
        compiler_params=pltpu.CompilerParams(vmem_limit_bytes=32 * MIB),
    )(*ins)
    return {k: tuple(outs[4 * j:4 * j + 4]) for j, k in enumerate(names)}


def _pack_small(small, loss):
    gate = lambda g: g.transpose(1, 0, 2).reshape(HD, NQ * HD)
    row_s = jnp.concatenate([small["sinks"], loss[:, LOSS_LANE:128], jnp.zeros((1, D - 128), f32)], axis=1)
    rep = jnp.concatenate([gate(small["w_rgate"]), gate(small["w_igate"])] + [small[k] for k in VEC_NAMES]
                          + [row_s, jnp.zeros((SMALL_ROWS - ROW_SINKS - 1, D), f32)], axis=0)
    conv = small["conv_w"].reshape(CONVW, NDEV, 128).transpose(1, 0, 2)
    conv = jnp.pad(conv, ((0, 0), (0, 8 - CONVW), (0, D - 128)))
    return jnp.concatenate([rep.reshape(NDEV, SMALL_PER, D), conv], axis=1).reshape(NDEV * (SMALL_PER + 8), D)


def kernel(x, ln_gain, w_in, sinks, conv_w, conv_b, w_rgate, b_rgate, w_igate, b_igate, lru_lambda, attn_out_gain, lru_out_gain, w_out, final_gain, loss_target, m_ln_gain, m_w_in, m_sinks, m_conv_w, m_conv_b, m_w_rgate, m_b_rgate, m_w_igate, m_b_igate, m_lru_lambda, m_attn_out_gain, m_lru_out_gain, m_w_out, m_final_gain, v_ln_gain, v_w_in, v_sinks, v_conv_w, v_conv_b, v_w_rgate, v_b_rgate, v_w_igate, v_b_igate, v_lru_lambda, v_attn_out_gain, v_lru_out_gain, v_w_out, v_final_gain):
    w = dict(ln_gain=ln_gain, sinks=sinks, conv_w=conv_w, conv_b=conv_b, w_rgate=w_rgate, b_rgate=b_rgate,
             w_igate=w_igate, b_igate=b_igate, lru_lambda=lru_lambda, attn_out_gain=attn_out_gain,
             lru_out_gain=lru_out_gain, final_gain=final_gain.reshape(1, D))
    m = dict(ln_gain=m_ln_gain, sinks=m_sinks, conv_w=m_conv_w, conv_b=m_conv_b, w_rgate=m_w_rgate,
             b_rgate=m_b_rgate, w_igate=m_w_igate, b_igate=m_b_igate, lru_lambda=m_lru_lambda,
             attn_out_gain=m_attn_out_gain, lru_out_gain=m_lru_out_gain, final_gain=m_final_gain.reshape(1, D))
    v = dict(ln_gain=v_ln_gain, sinks=v_sinks, conv_w=v_conv_w, conv_b=v_conv_b, w_rgate=v_w_rgate,
             b_rgate=v_b_rgate, w_igate=v_w_igate, b_igate=v_b_igate, lru_lambda=v_lru_lambda,
             attn_out_gain=v_attn_out_gain, lru_out_gain=v_lru_out_gain, final_gain=v_final_gain.reshape(1, D))

    conv_blk = jnp.pad(conv_w[0], ((0, 8 - CONVW), (0, 0)))
    wt, cw_all, h, wr, wi = _gather_weights(w_in[0].T, conv_blk, x[0], ln_gain, w_rgate[0], w_igate[0],
                                            min(512, x.shape[1]))
    conv_full = cw_all.reshape(NDEV, 8, 128)[:, 0:CONVW].transpose(1, 0, 2).reshape(CONVW, LW)

    p = {k: w[k] for k in w if k not in ("conv_w", "w_rgate", "w_igate")}
    gx, land_wt, land_wo, g_rep, g_tail = _sequence_step(
        x[0], h, loss_target[0], wt, w_out[0], conv_full, wr, wi, p)
    g_conv = g_tail[0:CONVW, 0:128]

    wins = _reduce_adamw(land_wt, w_in[0].T, m_w_in[0].T, v_w_in[0].T, 192, "adamw_w_in")
    g_win, d_win, m_win, v_win = (t.T for t in wins)
    g_wo, d_wo, m_wo, v_wo = _reduce_adamw(land_wo, w_out[0], m_w_out[0], v_w_out[0], 256, "adamw_w_out")
    res = _adamw_small(g_rep, g_conv, w, m, v)
    res["w_in"] = tuple(t[None] for t in (g_win, d_win, m_win, v_win))
    res["w_out"] = tuple(t[None] for t in (g_wo, d_wo, m_wo, v_wo))
    res["final_gain"] = tuple(t.reshape(D) for t in res["final_gain"])

    order = ("ln_gain", "w_in", "sinks", "conv_w", "conv_b", "w_rgate", "b_rgate", "w_igate", "b_igate",
             "lru_lambda", "attn_out_gain", "lru_out_gain", "w_out", "final_gain")
    total_loss = g_rep[ROW_SINKS, LOSS_LANE]
    return (total_loss, gx[None]) + tuple(res[k][i] for i in range(4) for k in order)
```

```python
import jax
import jax.numpy as jnp
from jax import lax
from jax.experimental import pallas as pl
from jax.experimental.pallas import tpu as pltpu

f32 = jnp.float32
bf16 = jnp.bfloat16

D = 1024
HD = 64
NQ = 16
NKV = 4
GROUP = NQ // NKV
KVW = NKV * HD
BLK = 128
ROT = 16
THETA = 500000.0
NEG = -1e30
LW = 1024
NGRP = 4
CONVW = 4
LRU_C = 8.0
NIN = 4608
EPS = 1e-6
NDEV = 8
WT_ROWS = NIN // NDEV
WO_ROWS = 2 * D // NDEV
SMALL_ROWS = 192
SMALL_PER = SMALL_ROWS // NDEV

ADAM_LR = 0.001
ADAM_B1 = 0.9
ADAM_B2 = 0.999
ADAM_EPS = 1e-08
ADAM_WD = 0.01
ADAM_STEP = 10

NT = (((1,), (1,)), ((), ()))
TN = (((0,), (0,)), ((), ()))
MESH = pl.DeviceIdType.MESH
MIB = 1024 * 1024


def _dot(a, b):
    return jnp.dot(a, b, preferred_element_type=f32)


def _dot_nt(a, b):
    return lax.dot_general(a, b, NT, preferred_element_type=f32)


def _dot_tn(a, b):
    return lax.dot_general(a, b, TN, preferred_element_type=f32)


def _params(sem, vmem_mib):
    return pltpu.CompilerParams(dimension_semantics=sem, vmem_limit_bytes=vmem_mib * MIB)


def _sigmoid(x):
    return 0.5 * jnp.tanh(0.5 * x) + 0.5


def _softplus(x):
    return jnp.maximum(x, 0.0) + jnp.log(1.0 + jnp.exp(-jnp.abs(x)))


def _rope_tables(s):
    pos = jnp.arange(s, dtype=f32)
    inv_freq = THETA ** (-jnp.arange(0, ROT, 2, dtype=f32) / ROT)
    ang = pos[:, None] * inv_freq[None, :]
    cs = jnp.concatenate([jnp.cos(ang) - 1.0, jnp.sin(ang)], axis=1)
    d = jnp.arange(128) % HD
    j = jnp.arange(ROT)[:, None]
    pick_c = ((d < ROT) & (j == d % (ROT // 2))).astype(f32)
    pick_sa = ((d >= ROT // 2) & (d < ROT) & (j == d)).astype(f32)
    pick_sb = -((d < ROT // 2) & (j == d + ROT // 2)).astype(f32)
    picks = jnp.concatenate([pick_c, pick_sa, pick_sb], axis=1)
    ones = jnp.concatenate([jnp.ones((1, 128), f32), jnp.zeros((1, 256), f32)], axis=1)
    return jnp.dot(cs, picks, precision=lax.Precision.HIGHEST) + ones


def _tables(tab_ref):
    return tab_ref[:, 0:128], tab_ref[:, 128:256], tab_ref[:, 256:384]


def _rope(t, c, sa, sb):
    return t * c + pltpu.roll(t, 8, 1) * sa + pltpu.roll(t, 120, 1) * sb


def _unrope_t(dr, c, sa, sb):
    return dr * c + pltpu.roll(dr * sa, 120, 0) + pltpu.roll(dr * sb, 8, 0)


def _place():
    return lax.axis_index("x"), lax.axis_index("y"), lax.axis_index("c")


def _gather_ops(mine_refs, out_refs, send_sems, recv_sems, local_sems):
    n = len(mine_refs)
    x, y, c = _place()
    me, sibling = (x, y, c), (x, y, 1 - c)
    chips = [(1 - x, y), (x, 1 - y), (1 - x, 1 - y)]

    def rows(a, dev):
        m = mine_refs[a].shape[0]
        return out_refs[a].at[pl.ds((4 * dev[0] + 2 * dev[1] + dev[2]) * m, m), :]

    def copy(a, k, block, to, own=False):
        return pltpu.make_async_remote_copy(
            src_ref=mine_refs[a] if own else rows(a, block), dst_ref=rows(a, block),
            send_sem=send_sems.at[a, k], recv_sem=recv_sems.at[a, k], device_id=to, device_id_type=MESH)

    def local(a):
        return pltpu.make_async_copy(mine_refs[a], rows(a, me), local_sems.at[a])

    def first(a):
        return [copy(a, 0, me, sibling, own=True)] + [copy(a, 1 + j, me, (*chip, c), own=True)
                                                      for j, chip in enumerate(chips)]

    def start():
        for a in range(n):
            local(a).start()
            for cp in first(a):
                cp.start()

    def pass_on():
        for j, chip in enumerate(chips):
            for a in range(n):
                copy(a, 1 + j, (*chip, c), me).wait_recv()
                copy(a, 4 + j, (*chip, c), sibling).start()

    def finish():
        for a in range(n):
            copy(a, 0, sibling, me).wait_recv()
            for j, chip in enumerate(chips):
                copy(a, 4 + j, (*chip, 1 - c), me).wait_recv()
        for a in range(n):
            for cp in first(a) + [copy(a, 4 + j, (*chip, c), sibling) for j, chip in enumerate(chips)]:
                cp.wait_send()
            local(a).wait()

    return start, pass_on, finish


def _relay_gather_ops(mine_refs, out_refs, send_sems, recv_sems, local_sems):
    n = len(mine_refs)
    x, y, c = _place()
    me, sibling = (x, y, c), (x, y, 1 - c)
    near = (x ^ (1 - c), y ^ c)
    far = (x ^ c, y ^ (1 - c))
    diag = (1 - x, 1 - y)

    def rows(a, dev):
        m = mine_refs[a].shape[0]
        return out_refs[a].at[pl.ds((4 * dev[0] + 2 * dev[1] + dev[2]) * m, m), :]

    def copy(a, k, block, to, own=False):
        return pltpu.make_async_remote_copy(
            src_ref=mine_refs[a] if own else rows(a, block), dst_ref=rows(a, block),
            send_sem=send_sems.at[a, k], recv_sem=recv_sems.at[a, k], device_id=to, device_id_type=MESH)

    def local(a):
        return pltpu.make_async_copy(mine_refs[a], rows(a, me), local_sems.at[a])

    def sends(a):
        return [copy(a, 0, me, sibling, own=True), copy(a, 1, me, (*near, c), own=True),
                copy(a, 2, me, (*far, c), own=True), copy(a, 3, (*near, c), (*far, c)),
                copy(a, 4, (*near, c), sibling), copy(a, 5, (*far, c), sibling), copy(a, 6, (*diag, c), sibling)]

    def arrivals(a):
        return [copy(a, 0, sibling, me), copy(a, 1, (*near, c), me), copy(a, 2, (*far, c), me),
                copy(a, 3, (*diag, c), me), copy(a, 4, (*far, 1 - c), me), copy(a, 5, (*near, 1 - c), me),
                copy(a, 6, (*diag, 1 - c), me)]

    def start():
        for a in range(n):
            local(a).start()
            for cp in sends(a)[0:3]:
                cp.start()

    def finish():
        for first, then in ((1, (3, 4)), (2, (5,)), (3, (6,))):
            for a in range(n):
                arrivals(a)[first].wait_recv()
                for k in then:
                    sends(a)[k].start()
        for a in range(n):
            for k in (0, 4, 5, 6):
                arrivals(a)[k].wait_recv()
        for a in range(n):
            for cp in sends(a):
                cp.wait_send()
            local(a).wait()

    return start, finish


def _scatter_ops(src_refs, land_refs, send_sems, recv_sems, local_sems):
    n = len(src_refs)
    x, y, c = _place()
    my = 4 * x + 2 * y + c

    def peer(k):
        return x ^ (k >> 2), y ^ ((k >> 1) & 1), c ^ (k & 1)

    def piece(a, dev):
        m = src_refs[a].shape[0] // NDEV
        return src_refs[a].at[pl.ds(dev * m, m), :]

    def local(a):
        return pltpu.make_async_copy(piece(a, my), land_refs[a].at[my], local_sems.at[a])

    def send(a, k):
        px, py, pc = peer(k)
        return pltpu.make_async_remote_copy(
            src_ref=piece(a, 4 * px + 2 * py + pc), dst_ref=land_refs[a].at[my],
            send_sem=send_sems.at[a, k - 1], recv_sem=recv_sems.at[a, k - 1],
            device_id=(px, py, pc), device_id_type=MESH)

    def arrival(a, k):
        px, py, pc = peer(k)
        return pltpu.make_async_remote_copy(
            src_ref=piece(a, my), dst_ref=land_refs[a].at[4 * px + 2 * py + pc],
            send_sem=send_sems.at[a, k - 1], recv_sem=recv_sems.at[a, k - 1],
            device_id=(px, py, pc), device_id_type=MESH)

    def start():
        for a in range(n):
            local(a).start()
        for k in range(1, NDEV):
            for a in range(n):
                send(a, k).start()

    def finish():
        for k in range(1, NDEV):
            for a in range(n):
                send(a, k).wait_send()
        for k in range(1, NDEV):
            for a in range(n):
                arrival(a, k).wait_recv()
        for a in range(n):
            local(a).wait()

    return start, finish


def _in_hbm(*arrays):
    return tuple(pltpu.with_memory_space_constraint(a, pltpu.HBM) for a in arrays)


def _comm_sems(n):
    return [pltpu.SemaphoreType.DMA((n, 7)), pltpu.SemaphoreType.DMA((n, 7)), pltpu.SemaphoreType.DMA((n,))]


HBM = pl.BlockSpec(memory_space=pltpu.HBM)


def _sink_rows(sinks):
    return jnp.repeat(sinks.reshape(NKV, GROUP), BLK, axis=1)


def _band_softmax(s2_ref, ls, prev_offset, sink_row):
    jj = lax.broadcasted_iota(jnp.int32, (BLK, BLK), 0)
    ii = lax.broadcasted_iota(jnp.int32, (BLK, BLK), 1)
    from_prev = jj > ii
    sc = jnp.where(from_prev, s2_ref[0:BLK, ls] + prev_offset, s2_ref[BLK:2 * BLK, ls])
    m = jnp.maximum(jnp.max(sc, axis=0, keepdims=True), sink_row)
    p = jnp.exp(sc - m)
    es = jnp.exp(sink_row - m)
    inv = 1.0 / (jnp.sum(p, axis=0, keepdims=True) + es)
    return from_prev, p * inv, es * inv


def _put_split(dst_ref, ls, t, from_prev):
    t = t.astype(bf16)
    zero = jnp.zeros_like(t)
    dst_ref[0:BLK, ls] = jnp.where(from_prev, t, zero)
    dst_ref[BLK:2 * BLK, ls] = jnp.where(from_prev, zero, t)


def _heads_side_by_side(ref, h):
    return jnp.concatenate([ref[HD * (GROUP * h + g):HD * (GROUP * h + g) + HD, :] for g in range(GROUP)], axis=1)


def _kv_specs_t():
    prev = pl.BlockSpec((KVW, BLK), lambda n: (0, jnp.maximum(n - 1, 0)))
    cur = pl.BlockSpec((KVW, BLK), lambda n: (0, n))
    return [prev, cur, prev, cur]


def _attn_fwd_t(qt, kt, vt, sinks):
    s = qt.shape[1]

    def body(sink_ref, q_ref, kp_ref, kc_ref, vp_ref, vc_ref, o_ref, s2_scr, pn2_scr):
        n = pl.program_id(0)
        off = jnp.where(n > 0, 0.0, NEG)

        def scores(h):
            hs = slice(HD * h, HD * h + HD)
            kh = jnp.concatenate([kp_ref[hs, :], kc_ref[hs, :]], axis=1)
            s2_scr[h % 2] = _dot_tn(kh, _heads_side_by_side(q_ref, h))

        def probs(h):
            for g in range(GROUP):
                ls = slice(BLK * g, BLK * g + BLK)
                from_prev, pn, _ = _band_softmax(s2_scr.at[h % 2], ls, off, sink_ref[h:h + 1, ls])
                _put_split(pn2_scr.at[h % 2], ls, pn, from_prev)

        def outputs(h):
            hs = slice(HD * h, HD * h + HD)
            vh = jnp.concatenate([vp_ref[hs, :], vc_ref[hs, :]], axis=1)
            og = _dot(vh, pn2_scr[h % 2])
            for g in range(GROUP):
                a = GROUP * h + g
                o_ref[HD * a:HD * a + HD, :] = og[:, BLK * g:BLK * g + BLK]

        scores(0)
        for h in range(NKV):
            if h + 1 < NKV:
                scores(h + 1)
            probs(h)
            outputs(h)

    return pl.pallas_call(
        body, name="attn_fwd", grid=(s // BLK,),
        in_specs=[pl.BlockSpec((NKV, GROUP * BLK), lambda n: (0, 0)), pl.BlockSpec((D, BLK), lambda n: (0, n))]
        + _kv_specs_t(),
        out_specs=pl.BlockSpec((D, BLK), lambda n: (0, n)),
        out_shape=pltpu.HBM((D, s), f32),
        scratch_shapes=[pltpu.VMEM((2, 2 * BLK, GROUP * BLK), f32), pltpu.VMEM((2, 2 * BLK, GROUP * BLK), bf16)],
        compiler_params=_params(("arbitrary",), 32),
    )(_sink_rows(sinks), *_in_hbm(qt, kt, kt, vt, vt))


def _attn_bwd_t(qt, kt, vt, dot, sinks, dwo):
    s = qt.shape[1]
    nb = s // BLK

    def body(sink_ref, q_ref, do_ref, kp_ref, kc_ref, vp_ref, vc_ref, dwo_ref, dq_ref, dk_ref, dv_ref, ds_ref,
             land_ref, dk_hold, dv_hold, s2_scr, dp2_scr, pn2_scr, ds2_scr, send_sems, recv_sems, local_sems):
        n = pl.program_id(0)
        start, finish = _scatter_ops([dwo_ref], [land_ref], send_sems, recv_sems, local_sems)

        @pl.when(n == 0)
        def _():
            start()
            dk_hold[...] = jnp.zeros_like(dk_hold)
            dv_hold[...] = jnp.zeros_like(dv_hold)
            ds_ref[...] = jnp.zeros_like(ds_ref)

        @pl.when(n < nb)
        def _():
            off = jnp.where(n > 0, 0.0, NEG)

            def scores(h):
                hs = slice(HD * h, HD * h + HD)
                kh = jnp.concatenate([kp_ref[hs, :], kc_ref[hs, :]], axis=1)
                vh = jnp.concatenate([vp_ref[hs, :], vc_ref[hs, :]], axis=1)
                s2_scr[h % 2] = _dot_tn(kh, _heads_side_by_side(q_ref, h))
                dp2_scr[h % 2] = _dot_tn(vh, _heads_side_by_side(do_ref, h))

            def softmax_bwd(h):
                for g in range(GROUP):
                    ls = slice(BLK * g, BLK * g + BLK)
                    from_prev, pn, ps = _band_softmax(s2_scr.at[h % 2], ls, off, sink_ref[h:h + 1, ls])
                    dp = jnp.where(from_prev, dp2_scr[h % 2, 0:BLK, ls], dp2_scr[h % 2, BLK:2 * BLK, ls])
                    dsum = jnp.sum(pn * dp, axis=0, keepdims=True)
                    ds_ref[h:h + 1, ls] += -ps * dsum
                    _put_split(pn2_scr.at[h % 2], ls, pn, from_prev)
                    _put_split(ds2_scr.at[h % 2], ls, pn * (dp - dsum), from_prev)

            def grads(h):
                hs = slice(HD * h, HD * h + HD)
                kh = jnp.concatenate([kp_ref[hs, :], kc_ref[hs, :]], axis=1)
                dqg = _dot(kh, ds2_scr[h % 2])
                for g in range(GROUP):
                    a = GROUP * h + g
                    dq_ref[HD * a:HD * a + HD, :] = dqg[:, BLK * g:BLK * g + BLK]
                dkh = _dot_nt(_heads_side_by_side(q_ref, h), ds2_scr[h % 2])
                dvh = _dot_nt(_heads_side_by_side(do_ref, h), pn2_scr[h % 2])
                dk_ref[hs, :] = dk_hold[hs, :] + dkh[:, 0:BLK]
                dv_ref[hs, :] = dv_hold[hs, :] + dvh[:, 0:BLK]
                dk_hold[hs, :] = dkh[:, BLK:2 * BLK]
                dv_hold[hs, :] = dvh[:, BLK:2 * BLK]

            scores(0)
            for h in range(NKV):
                if h + 1 < NKV:
                    scores(h + 1)
                softmax_bwd(h)
                grads(h)

        @pl.when(n == nb)
        def _():
            dk_ref[...] = dk_hold[...]
            dv_ref[...] = dv_hold[...]
            finish()

    blk = pl.BlockSpec((D, BLK), lambda n: (0, jnp.minimum(n, nb - 1)))
    late = pl.BlockSpec((KVW, BLK), lambda n: (0, jnp.maximum(n - 1, 0)))
    whole = pl.BlockSpec((NKV, GROUP * BLK), lambda n: (0, 0))
    kv = [pl.BlockSpec((KVW, BLK), lambda n: (0, jnp.clip(n - 1, 0, nb - 1))),
          pl.BlockSpec((KVW, BLK), lambda n: (0, jnp.minimum(n, nb - 1)))]
    return pl.pallas_call(
        body, name="attn_bwd", grid=(nb + 1,),
        in_specs=[whole, blk, blk] + kv + kv + [HBM],
        out_specs=[blk, late, late, whole, HBM],
        out_shape=[pltpu.HBM((D, s), f32), pltpu.HBM((KVW, s), f32), pltpu.HBM((KVW, s), f32),
                   jax.ShapeDtypeStruct((NKV, GROUP * BLK), f32), pltpu.HBM((NDEV, WO_ROWS, D), bf16)],
        scratch_shapes=[pltpu.VMEM((KVW, BLK), f32), pltpu.VMEM((KVW, BLK), f32)]
        + [pltpu.VMEM((2, 2 * BLK, GROUP * BLK), f32)] * 2 + [pltpu.VMEM((2, 2 * BLK, GROUP * BLK), bf16)] * 2
        + _comm_sems(1),
        compiler_params=_params(("arbitrary",), 48),
    )(_sink_rows(sinks), *_in_hbm(qt, dot, kt, kt, vt, vt, dwo))


def _decay_terms(r, sp):
    a = jnp.exp(r * (-LRU_C * sp))
    n = r * (2.0 * LRU_C * sp)
    y = jnp.where(n < 0.02, n * (1.0 - n * (0.5 - n * (1.0 / 6.0))), 1.0 - a * a)
    inv_mult = lax.rsqrt(jnp.maximum(y, 1e-30))
    return a, y * inv_mult, inv_mult


def _later(x, before, k):
    if k == 0:
        return x
    row = lax.broadcasted_iota(jnp.int32, before.shape, 0)
    rolled = pltpu.roll(x, k, 0)
    first = jnp.where(row < k, pltpu.roll(before, k, 0), rolled[0:8])
    return jnp.concatenate([first, rolled[8:]], axis=0)


def _earlier(x, after, k):
    if k == 0:
        return x
    n = x.shape[0]
    row = lax.broadcasted_iota(jnp.int32, after.shape, 0)
    rolled = pltpu.roll(x, n - k, 0)
    last = jnp.where(row >= 8 - k, pltpu.roll(after, 8 - k, 0), rolled[n - 8:n])
    return jnp.concatenate([rolled[0:n - 8], last], axis=0)


def _fwd_fused(h, wt, tabs, wo_shard, conv_w, conv_b, wr, wi, br, bi, lam, tm):
    s = h.shape[0]
    nt = s // tm
    nc = 512
    pieces = 8
    rows_per = tm // pieces
    later_chunks = (0, 1, 2, 3, 4, 7, 8)

    def body(h_ref, wt_ref, tab_ref, wo_ref, cw_ref, cb_ref, wr_ref, wi_ref, br_ref,
             bi_ref, lam_ref, q_ref, k_ref, v_ref, ga_ref, xl_ref, gl_ref, u_ref, hl_ref, r_ref, ig_ref,
             wo_all, wo_stage, halo, ub_scr, pr_scr, pi_scr, b_scr, a_scr, hcar,
             send_sems, recv_sems, local_sems):
        i = pl.program_id(0)
        start, pass_on, finish = _gather_ops([wo_stage], [wo_all], send_sems, recv_sems, local_sems)

        @pl.when(i == 0)
        def _():
            wo_stage[...] = wo_ref[...].astype(bf16)
            start()
            halo[...] = jnp.zeros_like(halo)
            hcar[...] = jnp.zeros_like(hcar)

        sp = _softplus(-lam_ref[...])
        br, bi = br_ref[...], bi_ref[...]
        c, sa, sb = _tables(tab_ref)
        piece_rows = lambda p: slice(rows_per * p, rows_per * p + rows_per)

        def project(ci):
            z = _dot_nt(h_ref[...], wt_ref[ci * nc:(ci + 1) * nc, :])
            if ci < 2:
                for j in range(nc // 128):
                    r = _rope(z[:, 128 * j:128 * j + 128], c, sa, sb) * (HD ** -0.5)
                    q_ref[ci * nc + 128 * j:ci * nc + 128 * j + 128, :] = r.astype(bf16).T
            elif ci == 2:
                for j in range(2):
                    js = slice(128 * j, 128 * j + 128)
                    k_ref[js, :] = _rope(z[:, js], c, sa, sb).astype(bf16).T
                    v_ref[js, :] = z[:, KVW + 128 * j:KVW + 128 * j + 128].astype(bf16).T
            else:
                sec, j = divmod(ci - 3, 2)
                (ga_ref, xl_ref, gl_ref)[sec][:, j * nc:(j + 1) * nc] = z

        def gate_terms(p):
            rows = piece_rows(p)
            r = _sigmoid(pr_scr[rows, :] + br)
            ig = _sigmoid(pi_scr[rows, :] + bi)
            a, mult, _ = _decay_terms(r, sp)
            r_ref[rows, :] = r
            ig_ref[rows, :] = ig
            a_scr[rows, :] = a
            b_scr[rows, :] = mult * (ig * u_ref[rows, :])

        def scan(p, hc):
            for t in range(rows_per * p, rows_per * p + rows_per):
                hc = a_scr[t:t + 1, :] * hc + b_scr[t:t + 1, :]
                hl_ref[t:t + 1, :] = hc
            return hc

        project(5)
        project(6)
        xl = xl_ref[...]
        u = cb_ref[...] + sum(cw_ref[k:k + 1, :] * _later(xl, halo[...], CONVW - 1 - k) for k in range(CONVW))
        halo[...] = xl[tm - 8:tm, :]
        u_ref[...] = u
        ub_scr[...] = u.astype(bf16)
        for g in range(NGRP):
            gs = slice(256 * g, 256 * g + 256)
            pr_scr[:, gs] = _dot(ub_scr[:, gs], wr_ref[g])
            pi_scr[:, gs] = _dot(ub_scr[:, gs], wi_ref[g])
        hc = hcar[...]
        gate_terms(0)
        for slot, ci in enumerate(later_chunks):
            project(ci)
            gate_terms(slot + 1)
            hc = scan(slot, hc)
        hcar[...] = scan(pieces - 1, hc)

        @pl.when(i == max(nt - 2, 0))
        def _():
            pass_on()

        @pl.when(i == nt - 1)
        def _():
            finish()

    row = lambda w: pl.BlockSpec((tm, w), lambda i: (i, 0))
    col = lambda w: pl.BlockSpec((w, tm), lambda i: (0, i))
    full = lambda a: pl.BlockSpec(a.shape, lambda i: (0,) * a.ndim)
    big = lambda w, dt: pltpu.HBM((s, w), dt)
    tile = pltpu.VMEM((tm, LW), f32)
    return pl.pallas_call(
        body, name="fwd_fused", grid=(nt,),
        in_specs=[row(D), full(wt), row(384), full(wo_shard), full(conv_w), full(conv_b),
                  full(wr), full(wi), full(br), full(bi), full(lam)],
        out_specs=[col(D), col(KVW), col(KVW), row(D), row(D), row(D)] + [row(LW)] * 4 + [HBM],
        out_shape=[pltpu.HBM((D, s), bf16), pltpu.HBM((KVW, s), bf16), pltpu.HBM((KVW, s), bf16),
                   big(D, f32), big(D, f32), big(D, f32)] + [big(LW, f32)] * 4 + [pltpu.HBM((2 * D, D), bf16)],
        scratch_shapes=[pltpu.VMEM((WO_ROWS, D), bf16), pltpu.VMEM((8, LW), f32), pltpu.VMEM((tm, LW), bf16)]
        + [tile] * 4 + [pltpu.VMEM((1, LW), f32)] + _comm_sems(1),
        compiler_params=_params(("arbitrary",), 56),
    )(*_in_hbm(h, wt), tabs, wo_shard, conv_w, conv_b, wr, wi, br, bi, lam)


def _lru_bwd(u, hl, dhl, xl, r, ig, conv_w, wr, wi, lam, tm):
    s = u.shape[0]
    nt = s // tm
    pieces = 8
    rows_per = tm // pieces

    def body(u_hbm, h_hbm, hp_ref, dh_hbm, x_hbm, r_hbm, ig_hbm, cw_ref, wr_ref, wi_ref,
             lam_ref, dxl_ref, dwr_ref, dwi_ref, dbr_ref, dbi_ref, dlam_ref, dcb_ref, dcw_ref,
             l_scr, du_scr, a_scr, mu_scr, im_scr, dpr_scr, dpi_scr, lcar, dunext, ring, ring_sems):
        t0 = pl.program_id(0)
        tile = nt - 1 - t0
        streams = (u_hbm, h_hbm, dh_hbm, x_hbm, r_hbm, ig_hbm)

        def fetch(step, k):
            slot = step % 3
            return pltpu.make_async_copy(streams[k].at[pl.ds((nt - 1 - step) * tm, tm), :], ring.at[slot, k],
                                         ring_sems.at[slot, k])

        @pl.when(t0 == 0)
        def _():
            for step in range(min(2, nt)):
                for k in range(len(streams)):
                    fetch(step, k).start()
            lcar[...] = jnp.zeros_like(lcar)
            dunext[...] = jnp.zeros_like(dunext)
            for ref in (dwr_ref, dwi_ref, dbr_ref, dbi_ref, dlam_ref, dcb_ref, dcw_ref):
                ref[...] = jnp.zeros_like(ref)

        @pl.when(t0 + 2 < nt)
        def _():
            for k in range(len(streams)):
                fetch(t0 + 2, k).start()

        for k in range(len(streams)):
            fetch(t0, k).wait()
        u_ref, h_ref, dh_ref, x_ref, r_ref, ig_ref = (ring.at[t0 % 3, k] for k in range(len(streams)))

        lam = lam_ref[...]
        sp = _softplus(-lam)
        hp = jnp.where(tile > 0, hp_ref[...], 0.0)

        def decay(p):
            rows = slice(rows_per * p, rows_per * p + rows_per)
            a_scr[rows, :], mu_scr[rows, :], im_scr[rows, :] = _decay_terms(r_ref[rows, :], sp)

        def scan(p, c):
            for t in range(rows_per * p + rows_per - 1, rows_per * p - 1, -1):
                lt = dh_ref[t:t + 1, :] + c
                l_scr[t:t + 1, :] = lt
                c = a_scr[t:t + 1, :] * lt
            return c

        def terms(p, sums):
            rows = slice(rows_per * p, rows_per * p + rows_per)
            lt, u, r, i, a, mult, inv_mult = l_scr[rows, :], u_ref[rows, :], r_ref[rows, :], ig_ref[rows, :], \
                a_scr[rows, :], mu_scr[rows, :], im_scr[rows, :]
            before = hp if p == 0 else h_ref[rows_per * p - 8:rows_per * p, :]
            hprev = _later(h_ref[rows, :], before, 1)
            iu = i * u
            lm = lt * mult
            du_scr[rows, :] = lm * i
            dla = (lt * hprev) * a - ((lt * iu) * (a * a)) * inv_mult
            dlar = dla * r
            dpr = (dlar * (1.0 - r)) * (-LRU_C * sp)
            dpi = (lm * iu) * (1.0 - i)
            dpr_scr[rows, :] = dpr.astype(bf16)
            dpi_scr[rows, :] = dpi.astype(bf16)
            col = lambda t: jnp.sum(t, axis=0, keepdims=True)
            return sums[0] + col(dlar), sums[1] + col(dpr), sums[2] + col(dpi)

        sums = (jnp.zeros((1, LW), f32),) * 3
        decay(pieces - 1)
        c = scan(pieces - 1, lcar[...])
        for p in range(pieces - 1, -1, -1):
            if p > 0:
                decay(p - 1)
                c = scan(p - 1, c)
            sums = terms(p, sums)
        lcar[...] = c
        dlam_ref[...] += sums[0] * (-LRU_C)
        dbr_ref[...] += sums[1]
        dbi_ref[...] += sums[2]

        ub = u_ref[...].astype(bf16)
        dug = []
        for g in range(NGRP):
            gs = slice(256 * g, 256 * g + 256)
            dwr_ref[g] += _dot_tn(ub[:, gs], dpr_scr[:, gs])
            dwi_ref[g] += _dot_tn(ub[:, gs], dpi_scr[:, gs])
            dug.append(_dot_nt(dpr_scr[:, gs], wr_ref[g]) + _dot_nt(dpi_scr[:, gs], wi_ref[g]))
        du = du_scr[...] + jnp.concatenate(dug, axis=1)

        dcb_ref[...] += jnp.sum(du, axis=0, keepdims=True)
        x = x_ref[...]
        after = dunext[...]
        dxl = jnp.zeros_like(du)
        for k in range(CONVW):
            e = _earlier(du, after, CONVW - 1 - k)
            dxl = dxl + cw_ref[k:k + 1, :] * e
            dcw_ref[k:k + 1, :] += jnp.sum(e * x, axis=0, keepdims=True)
        dxl_ref[...] = dxl.astype(bf16)
        dunext[...] = du[0:8, :]

        @pl.when(t0 == nt - 1)
        def _():
            dlam_ref[...] = dlam_ref[...] * (-_sigmoid(-lam))

    rev = lambda i: (nt - 1 - i, 0)
    row = pl.BlockSpec((tm, LW), rev)
    prev8 = pl.BlockSpec((8, LW), lambda i: (jnp.maximum((nt - 1 - i) * (tm // 8) - 1, 0), 0))
    full = lambda a: pl.BlockSpec(a.shape, lambda i: (0,) * a.ndim)
    vec = pl.BlockSpec((1, LW), lambda i: (0, 0))
    bd = pl.BlockSpec((NGRP, 256, 256), lambda i: (0, 0, 0))
    return pl.pallas_call(
        body, name="lru_bwd", grid=(nt,),
        in_specs=[HBM, HBM, prev8] + [HBM] * 4 + [full(conv_w), full(wr), full(wi), full(lam)],
        out_specs=[row, bd, bd, vec, vec, vec, vec, pl.BlockSpec((CONVW, LW), lambda i: (0, 0))],
        out_shape=[pltpu.HBM((s, LW), bf16),
                   jax.ShapeDtypeStruct((NGRP, 256, 256), f32), jax.ShapeDtypeStruct((NGRP, 256, 256), f32),
                   jax.ShapeDtypeStruct((1, LW), f32), jax.ShapeDtypeStruct((1, LW), f32),
                   jax.ShapeDtypeStruct((1, LW), f32), jax.ShapeDtypeStruct((1, LW), f32),
                   jax.ShapeDtypeStruct((CONVW, LW), f32)],
        scratch_shapes=[pltpu.VMEM((tm, LW), f32)] * 5 + [pltpu.VMEM((tm, LW), bf16)] * 2
        + [pltpu.VMEM((1, LW), f32), pltpu.VMEM((8, LW), f32), pltpu.VMEM((3, 6, tm, LW), f32),
           pltpu.SemaphoreType.DMA((3, 6))],
        compiler_params=_params(("arbitrary",), 56),
    )(*_in_hbm(u, hl, hl, dhl, xl, r, ig), conv_w, wr, wi, lam)


def _gated_norm(t, gate, gain):
    sg = _sigmoid(gate)
    silu = gate * sg
    p = t * silu
    rstd = lax.rsqrt(jnp.mean(p * p, axis=-1, keepdims=True) + EPS)
    ph = p * rstd
    return sg, silu, rstd, ph, ph * gain


def _gated_norm_bwd(dy, t, gate, gain, sg, silu, rstd, ph):
    w = dy * gain
    dp = rstd * (w - ph * jnp.mean(w * ph, axis=-1, keepdims=True))
    dgate = (dp * t) * (sg + silu * (1.0 - sg))
    return jnp.sum(dy * ph, axis=0, keepdims=True), dp * silu, dgate


def _out_fwd_bwd(x, tgt, o, ga, hl, gl, again, lgain, fgain, wo, tm):
    s = x.shape[0]
    nt = s // tm

    def body(x_ref, t_ref, o_ref, ga_ref, hl_ref, gl_ref, ag_ref, lg_ref, fg_ref, wo_ref,
             dx2_ref, do_ref, dga_ref, dhl_ref, dgl_ref, dwo_ref, gfg_ref, gag_ref, glg_ref, loss_ref, acc):
        i = pl.program_id(0)

        @pl.when(i == 0)
        def _():
            acc[...] = jnp.zeros_like(acc)
            for ref in (gfg_ref, gag_ref, glg_ref, loss_ref):
                ref[...] = jnp.zeros_like(ref)

        oo = jnp.concatenate([o_ref[128 * j:128 * j + 128, :].T for j in range(D // 128)], axis=1)
        gga, hh, ggl = ga_ref[...], hl_ref[...], gl_ref[...]
        ag, lg, fg = ag_ref[...], lg_ref[...], fg_ref[...]
        sga, silua, ra, pah, ya = _gated_norm(oo, gga, ag)
        sgl, silul, rl, plh, yl = _gated_norm(hh, ggl, lg)
        yab, ylb = ya.astype(bf16), yl.astype(bf16)
        y = _dot(yab, wo_ref[0:D, :]) + _dot(ylb, wo_ref[D:2 * D, :])
        x2 = x_ref[...] + y
        r2 = lax.rsqrt(jnp.mean(x2 * x2, axis=-1, keepdims=True) + EPS)
        x2h = x2 * r2
        err = x2h * fg - t_ref[...]
        loss_ref[...] += 0.5 * jnp.sum(jnp.sum(err * err, axis=-1, keepdims=True) * (1.0 / D))
        gfg_ref[...] += jnp.sum(err * x2h, axis=0, keepdims=True) * (1.0 / D)
        w = err * (fg * (1.0 / D))
        dx2 = r2 * (w - x2h * jnp.mean(w * x2h, axis=-1, keepdims=True))
        dx2_ref[...] = dx2
        dyb = dx2.astype(bf16)
        acc[0:D, :] += _dot_tn(yab, dyb)
        acc[D:2 * D, :] += _dot_tn(ylb, dyb)
        dya = _dot_nt(dyb, wo_ref[0:D, :])
        dyl = _dot_nt(dyb, wo_ref[D:2 * D, :])
        gag, do, dga = _gated_norm_bwd(dya, oo, gga, ag, sga, silua, ra, pah)
        glg, dhl, dgl = _gated_norm_bwd(dyl, hh, ggl, lg, sgl, silul, rl, plh)
        gag_ref[...] += gag
        glg_ref[...] += glg
        dob = do.astype(bf16)
        for j in range(D // 128):
            do_ref[128 * j:128 * j + 128, :] = dob[:, 128 * j:128 * j + 128].T
        dga_ref[...] = dga.astype(bf16)
        dhl_ref[...] = dhl
        dgl_ref[...] = dgl.astype(bf16)

        @pl.when(i == nt - 1)
        def _():
            dwo_ref[...] = acc[...].astype(bf16)

    row = pl.BlockSpec((tm, D), lambda i: (i, 0))
    col = pl.BlockSpec((D, tm), lambda i: (0, i))
    vec = pl.BlockSpec((1, D), lambda i: (0, 0))
    mat = pl.BlockSpec((2 * D, D), lambda i: (0, 0))
    return pl.pallas_call(
        body, name="out_fwd_bwd", grid=(nt,),
        in_specs=[row, row, col, row, row, row] + [vec] * 3 + [mat],
        out_specs=[row, col, row, row, row] + [mat, vec, vec, vec, pl.BlockSpec((1, 128), lambda i: (0, 0))],
        out_shape=[pltpu.HBM((s, D), f32), pltpu.HBM((D, s), bf16),
                   pltpu.HBM((s, D), bf16), pltpu.HBM((s, D), f32),
                   pltpu.HBM((s, D), bf16), pltpu.HBM((2 * D, D), bf16),
                   jax.ShapeDtypeStruct((1, D), f32), jax.ShapeDtypeStruct((1, D), f32),
                   jax.ShapeDtypeStruct((1, D), f32), jax.ShapeDtypeStruct((1, 128), f32)],
        scratch_shapes=[pltpu.VMEM((2 * D, D), f32)],
        compiler_params=_params(("arbitrary",), 56),
    )(*_in_hbm(x, tgt, o, ga, hl, gl), again, lgain, fgain, *_in_hbm(wo))


def _bwd_in(x, dx2, dq, dk, dv, dga, dxl, dgl, ln_gain, wt, tabs, tm):
    s = x.shape[0]

    def body(x_ref, dx2_ref, dq_ref, dk_ref, dv_ref, dga_ref, dxl_ref, dgl_ref, g_ref, wt_ref,
             tab_ref, gx_ref, gln_ref, dzt_ref):
        @pl.when(pl.program_id(0) == 0)
        def _():
            gln_ref[...] = jnp.zeros_like(gln_ref)

        c, sa, sb = (t.T for t in _tables(tab_ref))
        for j in range(D // 128):
            js = slice(128 * j, 128 * j + 128)
            dzt_ref[js, :] = (_unrope_t(dq_ref[js, :], c, sa, sb) * (HD ** -0.5)).astype(bf16)
        for j in range(KVW // 128):
            js = slice(128 * j, 128 * j + 128)
            dzt_ref[D + 128 * j:D + 128 * j + 128, :] = _unrope_t(dk_ref[js, :], c, sa, sb).astype(bf16)
        dzt_ref[D + KVW:D + 2 * KVW, :] = dv_ref[...].astype(bf16)
        first = D + 2 * KVW
        dh = _dot_tn(dzt_ref[0:first, :], wt_ref[0:first, :])
        for sec, ref in enumerate((dga_ref, dxl_ref, dgl_ref)):
            dh = dh + _dot(ref[...], wt_ref[first + D * sec:first + D * sec + D, :])
            for j in range(D // 128):
                dzt_ref[first + D * sec + 128 * j:first + D * sec + 128 * j + 128, :] = ref[:, 128 * j:128 * j + 128].T
        xx = x_ref[...]
        rstd = lax.rsqrt(jnp.mean(xx * xx, axis=-1, keepdims=True) + EPS)
        xh = xx * rstd
        gln_ref[...] += jnp.sum(dh * xh, axis=0, keepdims=True)
        w = dh * g_ref[...]
        gx_ref[...] = dx2_ref[...] + rstd * (w - xh * jnp.mean(w * xh, axis=-1, keepdims=True))

    row = lambda w: pl.BlockSpec((tm, w), lambda i: (i, 0))
    col = lambda w: pl.BlockSpec((w, tm), lambda i: (0, i))
    full = lambda a: pl.BlockSpec(a.shape, lambda i: (0, 0))
    return pl.pallas_call(
        body, name="bwd_in", grid=(s // tm,),
        in_specs=[row(D), row(D), col(D), col(KVW), col(KVW), row(D), row(D), row(D), full(ln_gain), full(wt),
                  row(384)],
        out_specs=[row(D), pl.BlockSpec((1, D), lambda i: (0, 0)), col(NIN)],
        out_shape=[pltpu.HBM((s, D), f32), jax.ShapeDtypeStruct((1, D), f32),
                   pltpu.HBM((NIN, s), bf16)],
        compiler_params=_params(("arbitrary",), 56),
    )(*_in_hbm(x, dx2, dq, dk, dv, dga, dxl, dgl), ln_gain, *_in_hbm(wt), tabs)


WT_TERMS = 4


def _dwt_scatter(dzt, h, small, tm):
    s = h.shape[0]
    nk = s // tm
    srows = small.shape[0] // NDEV
    last = NDEV - 1
    sm_turn = 2

    def body(order_ref, dz_ref, h_ref, sm_ref, lwt_ref, rep_all, tail_ref, acc, stage, given, relayed, lsm, rep_stage,
             send_sems, recv_sems, local_sem, sm_send, sm_recv, sm_local, rep_send, rep_recv, rep_local):
        j, k = pl.program_id(0), pl.program_id(1)
        x, y, c = _place()
        sibling = (x, y, 1 - c)
        near = (x ^ (1 - c), y ^ c)
        far = (x ^ c, y ^ (1 - c))
        sm_start, sm_finish = _scatter_ops([sm_ref], [lsm], sm_send, sm_recv, sm_local)
        rep_start, rep_pass_on, rep_finish = _gather_ops([rep_stage], [rep_all], rep_send, rep_recv, rep_local)

        def send(step):
            if step == last - 1:
                dst, to = lwt_ref.at[1], sibling
            elif step % 2 == 0:
                dst, to = given.at[step // 2], sibling
            elif step == 1:
                dst, to = relayed, (*near, c)
            else:
                dst, to = lwt_ref.at[1 + step // 2], (*(near if step == 3 else far), c)
            return pltpu.make_async_remote_copy(
                src_ref=stage.at[step % 2], dst_ref=dst, send_sem=send_sems.at[step], recv_sem=recv_sems.at[step],
                device_id=to, device_id_type=MESH)

        def keep():
            return pltpu.make_async_copy(stage.at[last % 2], lwt_ref.at[0], local_sem)

        @pl.when((j == 0) & (k == 0))
        def _():
            sm_start()

        @pl.when(k == 0)
        def _():
            acc[...] = jnp.zeros_like(acc)

        acc[...] += _dot(dz_ref[...], h_ref[...])

        for step in range(NDEV):
            @pl.when((k == nk - 1) & (j == step))
            def _(step=step):
                if step >= 2:
                    send(step - 2).wait_send()
                if step % 2 == 1 and step < last:
                    send(step - 1).wait_recv()
                    total = acc[...] + given[step // 2].astype(f32)
                    if step == 5:
                        send(1).wait_recv()
                        total = total + relayed[...].astype(f32)
                    stage[step % 2] = total.astype(bf16)
                else:
                    stage[step % 2] = acc[...].astype(bf16)
                if step < last:
                    send(step).start()
                else:
                    keep().start()
                    send(last - 1).wait_send()
                    for peer_step in (3, 5, last - 1):
                        send(peer_step).wait_recv()
                    keep().wait()
                    rep_finish()
                if step == sm_turn:
                    sm_finish()
                    total_sm = lsm[0]
                    for dev in range(1, NDEV):
                        total_sm = total_sm + lsm[dev]
                    rep_stage[...] = total_sm[0:SMALL_PER]
                    tail_ref[...] = total_sm[SMALL_PER:]
                    rep_start()
                if step == last - 1:
                    rep_pass_on()

    x, y, c = _place()
    dest = lambda chip, cc: 4 * chip[0] + 2 * chip[1] + cc
    near, far, diag = (x ^ (1 - c), y ^ c), (x ^ c, y ^ (1 - c)), (1 - x, 1 - y)
    order = jnp.stack([dest(diag, 1 - c), dest(diag, c), dest(far, 1 - c), dest(near, c),
                       dest(near, 1 - c), dest(far, c), dest((x, y), 1 - c), dest((x, y), c)])
    return pl.pallas_call(
        body, name="dwt_scatter",
        grid_spec=pltpu.PrefetchScalarGridSpec(
            num_scalar_prefetch=1, grid=(NDEV, nk),
            in_specs=[pl.BlockSpec((WT_ROWS, tm), lambda j, k, order: (order[j], k)),
                      pl.BlockSpec((tm, D), lambda j, k, order: (k, 0)), HBM],
            out_specs=[HBM, HBM, pl.BlockSpec((srows - SMALL_PER, D), lambda j, k, order: (0, 0))],
            scratch_shapes=[pltpu.VMEM((WT_ROWS, D), f32), pltpu.VMEM((2, WT_ROWS, D), bf16),
                            pltpu.VMEM((3, WT_ROWS, D), bf16), pltpu.VMEM((WT_ROWS, D), bf16),
                            pltpu.VMEM((NDEV, srows, D), f32), pltpu.VMEM((SMALL_PER, D), f32),
                            pltpu.SemaphoreType.DMA((last,)), pltpu.SemaphoreType.DMA((last,)),
                            pltpu.SemaphoreType.DMA(())] + _comm_sems(1) + _comm_sems(1)),
        out_shape=[pltpu.HBM((WT_TERMS, WT_ROWS, D), bf16), pltpu.HBM((SMALL_ROWS, D), f32),
                   jax.ShapeDtypeStruct((srows - SMALL_PER, D), f32)],
        compiler_params=_params(("arbitrary", "arbitrary"), 48),
    )(order, *_in_hbm(dzt, h, small))


def _diag_blocks(bd):
    eye = jnp.eye(4, dtype=bd.dtype)
    return jnp.einsum('gjckd,jk->gjcd', bd.reshape(NGRP, 4, HD, 4, HD), eye).reshape(NQ, HD, HD)


def _sequence_step(x, h, tgt, wt, wo_shard, conv_w, wr, wi, p):
    s = x.shape[0]
    tm = min(256, s)
    tabs = _rope_tables(s)
    sinks = p["sinks"].reshape(NQ)
    qt, kt, vt, ga, xl, gl, u, hl, r, ig, wo = _fwd_fused(
        h, wt, tabs, wo_shard, conv_w, p["conv_b"], wr, wi, p["b_rgate"], p["b_igate"], p["lru_lambda"], tm)
    ot = _attn_fwd_t(qt, kt, vt, sinks)
    dx2, dot, dga, dhl, dgl, dwo, g_fg, g_ag, g_lg, loss = _out_fwd_bwd(
        x, tgt, ot, ga, hl, gl, p["attn_out_gain"], p["lru_out_gain"], p["final_gain"], wo, tm)
    dqt, dkt, dvt, dsink, land_wo = _attn_bwd_t(qt, kt, vt, dot, sinks, dwo)
    dxl, dwr, dwi, dbr, dbi, dlam, dcb, dcw = _lru_bwd(u, hl, dhl, xl, r, ig, conv_w, wr, wi, p["lru_lambda"], tm)
    gx, g_ln, dzt = _bwd_in(x, dx2, dqt, dkt, dvt, dga, dxl, dgl, p["ln_gain"], wt, tabs, tm)
    small = dict(ln_gain=g_ln, sinks=dsink.reshape(NQ, BLK).sum(axis=1)[None], conv_w=dcw, conv_b=dcb,
                 w_rgate=_diag_blocks(dwr), b_rgate=dbr, w_igate=_diag_blocks(dwi), b_igate=dbi, lru_lambda=dlam,
                 attn_out_gain=g_ag, lru_out_gain=g_lg, final_gain=g_fg)
    land_wt, g_rep, g_tail = _dwt_scatter(dzt, h, _pack_small(small, loss), min(2048, s))
    return gx, land_wt, land_wo, g_rep, g_tail


def _gather_weights(wt_shard, conv_blk, x, ln_gain, w_rgate, w_igate, tm):
    s = x.shape[0]

    def body(wt_ref, cw_ref, g_ref, wrg_ref, wig_ref, x_ref, wt_all, cw_all, h_ref, wr_ref, wi_ref,
             stage, xbuf, hbuf, send_sems, recv_sems, local_sems):
        stage[...] = wt_ref[...].astype(bf16)
        start, finish = _relay_gather_ops([stage, cw_ref], [wt_all, cw_all], send_sems, recv_sems, local_sems)
        start()
        for src, dst in ((wrg_ref, wr_ref), (wig_ref, wi_ref)):
            dst[...] = jnp.zeros_like(dst)
            for nb in range(NQ):
                g, j = divmod(nb, 4)
                dst[g, HD * j:HD * j + HD, HD * j:HD * j + HD] = src[nb].astype(bf16)
        gain = g_ref[...]
        for i in range(s // tm):
            rows = pl.ds(i * tm, tm)
            pltpu.sync_copy(x_ref.at[rows, :], xbuf)
            xx = xbuf[...]
            rstd = lax.rsqrt(jnp.mean(xx * xx, axis=-1, keepdims=True) + EPS)
            hbuf[...] = (xx * rstd * gain).astype(bf16)
            pltpu.sync_copy(hbuf, h_ref.at[rows, :])
        finish()

    vmem = pl.BlockSpec(memory_space=pltpu.VMEM)
    return pl.pallas_call(
        body, name="gather_weights",
        in_specs=[vmem] * 5 + [HBM], out_specs=[HBM, HBM, HBM, vmem, vmem],
        out_shape=[pltpu.HBM((NIN, D), bf16), pltpu.HBM((NDEV * 8, 128), f32), pltpu.HBM((s, D), bf16)]
        + [jax.ShapeDtypeStruct((NGRP, 256, 256), bf16)] * 2,
        scratch_shapes=[pltpu.VMEM((WT_ROWS, D), bf16), pltpu.VMEM((tm, D), f32), pltpu.VMEM((tm, D), bf16)]
        + _comm_sems(2),
        compiler_params=pltpu.CompilerParams(vmem_limit_bytes=32 * MIB),
    )(wt_shard, conv_blk, ln_gain, w_rgate, w_igate, *_in_hbm(x))


def _adam_math(w, g, m, v):
    m2 = ADAM_B1 * m + (1.0 - ADAM_B1) * g
    v2 = ADAM_B2 * v + (1.0 - ADAM_B2) * (g * g)
    m_hat = m2 / (1.0 - ADAM_B1 ** ADAM_STEP)
    v_hat = v2 / (1.0 - ADAM_B2 ** ADAM_STEP)
    delta = -ADAM_LR * (m_hat / (jnp.sqrt(v_hat) + ADAM_EPS) + ADAM_WD * w)
    return delta, m2, v2


def _reduce_adamw(land, w, m, v, tr, name):
    terms, rows, cols = land.shape

    def body(l_ref, w_ref, m_ref, v_ref, g_ref, d_ref, m2_ref, v2_ref):
        g = l_ref[0].astype(f32)
        for t in range(1, terms):
            g = g + l_ref[t].astype(f32)
        g_ref[...] = g
        d_ref[...], m2_ref[...], v2_ref[...] = _adam_math(w_ref[...], g, m_ref[...], v_ref[...])

    blk = pl.BlockSpec((tr, cols), lambda i: (i, 0))
    return pl.pallas_call(
        body, name=name, grid=(rows // tr,),
        in_specs=[pl.BlockSpec((terms, tr, cols), lambda i: (0, i, 0))] + [blk] * 3, out_specs=[blk] * 4,
        out_shape=[jax.ShapeDtypeStruct((rows, cols), f32)] * 4,
        compiler_params=_params(("arbitrary",), 32),
    )(*_in_hbm(land), w, m, v)


VEC_NAMES = ("ln_gain", "conv_b", "b_rgate", "b_igate", "lru_lambda", "attn_out_gain", "lru_out_gain", "final_gain")
ROW_RGATE, ROW_IGATE, ROW_VEC, ROW_SINKS = 0, 64, 128, 136
LOSS_LANE = NQ


def _adamw_small(g_rep, g_conv, w, m, v):
    names = list(VEC_NAMES) + ["sinks", "conv_w", "w_rgate", "w_igate"]
    ins = [g_rep, g_conv] + [d[k] for k in names for d in (w, m, v)]

    def body(*refs):
        g_ref, gc_ref = refs[0], refs[1]
        in_refs = refs[2:2 + 3 * len(names)]
        out_refs = refs[2 + 3 * len(names):]

        def update(j, g, at=None):
            w_ref, m_ref, v_ref = in_refs[3 * j:3 * j + 3]
            outs = out_refs[4 * j:4 * j + 4]
            pick = (lambda r: r[...]) if at is None else (lambda r: r[at])
            res = (g,) + _adam_math(pick(w_ref), g, pick(m_ref), pick(v_ref))
            for o_ref, val in zip(outs, res):
                if at is None:
                    o_ref[...] = val
                else:
                    o_ref[at] = val

        for j in range(len(VEC_NAMES)):
            update(j, g_ref[ROW_VEC + j:ROW_VEC + j + 1, :])
        update(len(VEC_NAMES), g_ref[ROW_SINKS:ROW_SINKS + 1, 0:NQ])
        update(len(VEC_NAMES) + 1, gc_ref[...], at=0)
        for gi, row0 in ((len(VEC_NAMES) + 2, ROW_RGATE), (len(VEC_NAMES) + 3, ROW_IGATE)):
            for nb in range(NQ):
                update(gi, g_ref[row0:row0 + HD, HD * nb:HD * nb + HD], at=(0, nb))

    vmem = pl.BlockSpec(memory_space=pltpu.VMEM)
    out_shape = [jax.ShapeDtypeStruct(w[k].shape, f32) for k in names for _ in range(4)]
    outs = pl.pallas_call(
        body, name="adamw_small",
        in_specs=[vmem] * len(ins), out_specs=[vmem] * len(out_shape), out_shape=out_shape,
        compiler_params=pltpu.CompilerParams(vmem_limit_bytes=32 * MIB),
    )(*ins)
    return {k: tuple(outs[4 * j:4 * j + 4]) for j, k in enumerate(names)}


def _pack_small(small, loss):
    gate = lambda g: g.transpose(1, 0, 2).reshape(HD, NQ * HD)
    row_s = jnp.concatenate([small["sinks"], loss[:, LOSS_LANE:128], jnp.zeros((1, D - 128), f32)], axis=1)
    rep = jnp.concatenate([gate(small["w_rgate"]), gate(small["w_igate"])] + [small[k] for k in VEC_NAMES]
                          + [row_s, jnp.zeros((SMALL_ROWS - ROW_SINKS - 1, D), f32)], axis=0)
    conv = small["conv_w"].reshape(CONVW, NDEV, 128).transpose(1, 0, 2)
    conv = jnp.pad(conv, ((0, 0), (0, 8 - CONVW), (0, D - 128)))
    return jnp.concatenate([rep.reshape(NDEV, SMALL_PER, D), conv], axis=1).reshape(NDEV * (SMALL_PER + 8), D)


def kernel(x, ln_gain, w_in, sinks, conv_w, conv_b, w_rgate, b_rgate, w_igate, b_igate, lru_lambda, attn_out_gain, lru_out_gain, w_out, final_gain, loss_target, m_ln_gain, m_w_in, m_sinks, m_conv_w, m_conv_b, m_w_rgate, m_b_rgate, m_w_igate, m_b_igate, m_lru_lambda, m_attn_out_gain, m_lru_out_gain, m_w_out, m_final_gain, v_ln_gain, v_w_in, v_sinks, v_conv_w, v_conv_b, v_w_rgate, v_b_rgate, v_w_igate, v_b_igate, v_lru_lambda, v_attn_out_gain, v_lru_out_gain, v_w_out, v_final_gain):
    w = dict(ln_gain=ln_gain, sinks=sinks, conv_w=conv_w, conv_b=conv_b, w_rgate=w_rgate, b_rgate=b_rgate,
             w_igate=w_igate, b_igate=b_igate, lru_lambda=lru_lambda, attn_out_gain=attn_out_gain,
             lru_out_gain=lru_out_gain, final_gain=final_gain.reshape(1, D))
    m = dict(ln_gain=m_ln_gain, sinks=m_sinks, conv_w=m_conv_w, conv_b=m_conv_b, w_rgate=m_w_rgate,
             b_rgate=m_b_rgate, w_igate=m_w_igate, b_igate=m_b_igate, lru_lambda=m_lru_lambda,
             attn_out_gain=m_attn_out_gain, lru_out_gain=m_lru_out_gain, final_gain=m_final_gain.reshape(1, D))
    v = dict(ln_gain=v_ln_gain, sinks=v_sinks, conv_w=v_conv_w, conv_b=v_conv_b, w_rgate=v_w_rgate,
             b_rgate=v_b_rgate, w_igate=v_w_igate, b_igate=v_b_igate, lru_lambda=v_lru_lambda,
             attn_out_gain=v_attn_out_gain, lru_out_gain=v_lru_out_gain, final_gain=v_final_gain.reshape(1, D))

    conv_blk = jnp.pad(conv_w[0], ((0, 8 - CONVW), (0, 0)))
    wt, cw_all, h, wr, wi = _gather_weights(w_in[0].T, conv_blk, x[0], ln_gain, w_rgate[0], w_igate[0],
                                            min(512, x.shape[1]))
    conv_full = cw_all.reshape(NDEV, 8, 128)[:, 0:CONVW].transpose(1, 0, 2).reshape(CONVW, LW)

    p = {k: w[k] for k in w if k not in ("conv_w", "w_rgate", "w_igate")}
    gx, land_wt, land_wo, g_rep, g_tail = _sequence_step(
        x[0], h, loss_target[0], wt, w_out[0], conv_full, wr, wi, p)
    g_conv = g_tail[0:CONVW, 0:128]

    wins = _reduce_adamw(land_wt, w_in[0].T, m_w_in[0].T, v_w_in[0].T, 192, "adamw_w_in")
    g_win, d_win, m_win, v_win = (t.T for t in wins)
    g_wo, d_wo, m_wo, v_wo = _reduce_adamw(land_wo, w_out[0], m_w_out[0], v_w_out[0], 256, "adamw_w_out")
    res = _adamw_small(g_rep, g_conv, w, m, v)
    res["w_in"] = tuple(t[None] for t in (g_win, d_win, m_win, v_win))
    res["w_out"] = tuple(t[None] for t in (g_wo, d_wo, m_wo, v_wo))
    res["final_gain"] = tuple(t.reshape(D) for t in res["final_gain"])

    order = ("ln_gain", "w_in", "sinks", "conv_w", "conv_b", "w_rgate", "b_rgate", "w_igate", "b_igate",
             "lru_lambda", "attn_out_gain", "lru_out_gain", "w_out", "final_gain")
    total_loss = g_rep[ROW_SINKS, LOSS_LANE]
    return (total_loss, gx[None]) + tuple(res[k][i] for i in range(4) for k in order)
```

```python
import jax
import jax.numpy as jnp
from jax import lax
from jax.experimental import pallas as pl
from jax.experimental.pallas import tpu as pltpu

f32 = jnp.float32
bf16 = jnp.bfloat16

D = 1024
HD = 64
NQ = 16
NKV = 4
GROUP = NQ // NKV
KVW = NKV * HD
BLK = 128
ROT = 16
THETA = 500000.0
NEG = -1e30
LW = 1024
NGRP = 4
CONVW = 4
LRU_C = 8.0
NIN = 4608
EPS = 1e-6
NDEV = 8
WT_ROWS = NIN // NDEV
WO_ROWS = 2 * D // NDEV
SMALL_ROWS = 192
SMALL_PER = SMALL_ROWS // NDEV

ADAM_LR = 0.001
ADAM_B1 = 0.9
ADAM_B2 = 0.999
ADAM_EPS = 1e-08
ADAM_WD = 0.01
ADAM_STEP = 10

NT = (((1,), (1,)), ((), ()))
TN = (((0,), (0,)), ((), ()))
MESH = pl.DeviceIdType.MESH
MIB = 1024 * 1024


def _dot(a, b):
    return jnp.dot(a, b, preferred_element_type=f32)


def _dot_nt(a, b):
    return lax.dot_general(a, b, NT, preferred_element_type=f32)


def _dot_tn(a, b):
    return lax.dot_general(a, b, TN, preferred_element_type=f32)


def _params(sem, vmem_mib):
    return pltpu.CompilerParams(dimension_semantics=sem, vmem_limit_bytes=vmem_mib * MIB)


def _sigmoid(x):
    return 0.5 * jnp.tanh(0.5 * x) + 0.5


def _softplus(x):
    return jnp.maximum(x, 0.0) + jnp.log(1.0 + jnp.exp(-jnp.abs(x)))


def _rope_tables(s):
    pos = jnp.arange(s, dtype=f32)
    inv_freq = THETA ** (-jnp.arange(0, ROT, 2, dtype=f32) / ROT)
    ang = pos[:, None] * inv_freq[None, :]
    cs = jnp.concatenate([jnp.cos(ang) - 1.0, jnp.sin(ang)], axis=1)
    d = jnp.arange(128) % HD
    j = jnp.arange(ROT)[:, None]
    pick_c = ((d < ROT) & (j == d % (ROT // 2))).astype(f32)
    pick_sa = ((d >= ROT // 2) & (d < ROT) & (j == d)).astype(f32)
    pick_sb = -((d < ROT // 2) & (j == d + ROT // 2)).astype(f32)
    picks = jnp.concatenate([pick_c, pick_sa, pick_sb], axis=1)
    ones = jnp.concatenate([jnp.ones((1, 128), f32), jnp.zeros((1, 256), f32)], axis=1)
    return jnp.dot(cs, picks, precision=lax.Precision.HIGHEST) + ones


def _tables(tab_ref):
    return tab_ref[:, 0:128], tab_ref[:, 128:256], tab_ref[:, 256:384]


def _rope(t, c, sa, sb):
    return t * c + pltpu.roll(t, 8, 1) * sa + pltpu.roll(t, 120, 1) * sb


def _unrope_t(dr, c, sa, sb):
    return dr * c + pltpu.roll(dr * sa, 120, 0) + pltpu.roll(dr * sb, 8, 0)


def _place():
    return lax.axis_index("x"), lax.axis_index("y"), lax.axis_index("c")


def _gather_ops(mine_refs, out_refs, send_sems, recv_sems, local_sems):
    n = len(mine_refs)
    x, y, c = _place()
    me, sibling = (x, y, c), (x, y, 1 - c)
    chips = [(1 - x, y), (x, 1 - y), (1 - x, 1 - y)]

    def rows(a, dev):
        m = mine_refs[a].shape[0]
        return out_refs[a].at[pl.ds((4 * dev[0] + 2 * dev[1] + dev[2]) * m, m), :]

    def copy(a, k, block, to, own=False):
        return pltpu.make_async_remote_copy(
            src_ref=mine_refs[a] if own else rows(a, block), dst_ref=rows(a, block),
            send_sem=send_sems.at[a, k], recv_sem=recv_sems.at[a, k], device_id=to, device_id_type=MESH)

    def local(a):
        return pltpu.make_async_copy(mine_refs[a], rows(a, me), local_sems.at[a])

    def first(a):
        return [copy(a, 0, me, sibling, own=True)] + [copy(a, 1 + j, me, (*chip, c), own=True)
                                                      for j, chip in enumerate(chips)]

    def start():
        for a in range(n):
            local(a).start()
            for cp in first(a):
                cp.start()

    def pass_on():
        for j, chip in enumerate(chips):
            for a in range(n):
                copy(a, 1 + j, (*chip, c), me).wait_recv()
                copy(a, 4 + j, (*chip, c), sibling).start()

    def finish():
        for a in range(n):
            copy(a, 0, sibling, me).wait_recv()
            for j, chip in enumerate(chips):
                copy(a, 4 + j, (*chip, 1 - c), me).wait_recv()
        for a in range(n):
            for cp in first(a) + [copy(a, 4 + j, (*chip, c), sibling) for j, chip in enumerate(chips)]:
                cp.wait_send()
            local(a).wait()

    return start, pass_on, finish


def _relay_gather_ops(mine_refs, out_refs, send_sems, recv_sems, local_sems):
    n = len(mine_refs)
    x, y, c = _place()
    me, sibling = (x, y, c), (x, y, 1 - c)
    near = (x ^ (1 - c), y ^ c)
    far = (x ^ c, y ^ (1 - c))
    diag = (1 - x, 1 - y)

    def rows(a, dev):
        m = mine_refs[a].shape[0]
        return out_refs[a].at[pl.ds((4 * dev[0] + 2 * dev[1] + dev[2]) * m, m), :]

    def copy(a, k, block, to, own=False):
        return pltpu.make_async_remote_copy(
            src_ref=mine_refs[a] if own else rows(a, block), dst_ref=rows(a, block),
            send_sem=send_sems.at[a, k], recv_sem=recv_sems.at[a, k], device_id=to, device_id_type=MESH)

    def local(a):
        return pltpu.make_async_copy(mine_refs[a], rows(a, me), local_sems.at[a])

    def sends(a):
        return [copy(a, 0, me, sibling, own=True), copy(a, 1, me, (*near, c), own=True),
                copy(a, 2, me, (*far, c), own=True), copy(a, 3, (*near, c), (*far, c)),
                copy(a, 4, (*near, c), sibling), copy(a, 5, (*far, c), sibling), copy(a, 6, (*diag, c), sibling)]

    def arrivals(a):
        return [copy(a, 0, sibling, me), copy(a, 1, (*near, c), me), copy(a, 2, (*far, c), me),
                copy(a, 3, (*diag, c), me), copy(a, 4, (*far, 1 - c), me), copy(a, 5, (*near, 1 - c), me),
                copy(a, 6, (*diag, 1 - c), me)]

    def start():
        for a in range(n):
            local(a).start()
            for cp in sends(a)[0:3]:
                cp.start()

    def finish():
        for first, then in ((1, (3, 4)), (2, (5,)), (3, (6,))):
            for a in range(n):
                arrivals(a)[first].wait_recv()
                for k in then:
                    sends(a)[k].start()
        for a in range(n):
            for k in (0, 4, 5, 6):
                arrivals(a)[k].wait_recv()
        for a in range(n):
            for cp in sends(a):
                cp.wait_send()
            local(a).wait()

    return start, finish


def _scatter_ops(src_refs, land_refs, send_sems, recv_sems, local_sems):
    n = len(src_refs)
    x, y, c = _place()
    my = 4 * x + 2 * y + c

    def peer(k):
        return x ^ (k >> 2), y ^ ((k >> 1) & 1), c ^ (k & 1)

    def piece(a, dev):
        m = src_refs[a].shape[0] // NDEV
        return src_refs[a].at[pl.ds(dev * m, m), :]

    def local(a):
        return pltpu.make_async_copy(piece(a, my), land_refs[a].at[my], local_sems.at[a])

    def send(a, k):
        px, py, pc = peer(k)
        return pltpu.make_async_remote_copy(
            src_ref=piece(a, 4 * px + 2 * py + pc), dst_ref=land_refs[a].at[my],
            send_sem=send_sems.at[a, k - 1], recv_sem=recv_sems.at[a, k - 1],
            device_id=(px, py, pc), device_id_type=MESH)

    def arrival(a, k):
        px, py, pc = peer(k)
        return pltpu.make_async_remote_copy(
            src_ref=piece(a, my), dst_ref=land_refs[a].at[4 * px + 2 * py + pc],
            send_sem=send_sems.at[a, k - 1], recv_sem=recv_sems.at[a, k - 1],
            device_id=(px, py, pc), device_id_type=MESH)

    def start():
        for a in range(n):
            local(a).start()
        for k in range(1, NDEV):
            for a in range(n):
                send(a, k).start()

    def finish():
        for k in range(1, NDEV):
            for a in range(n):
                send(a, k).wait_send()
        for k in range(1, NDEV):
            for a in range(n):
                arrival(a, k).wait_recv()
        for a in range(n):
            local(a).wait()

    return start, finish


def _in_hbm(*arrays):
    return tuple(pltpu.with_memory_space_constraint(a, pltpu.HBM) for a in arrays)


def _comm_sems(n):
    return [pltpu.SemaphoreType.DMA((n, 7)), pltpu.SemaphoreType.DMA((n, 7)), pltpu.SemaphoreType.DMA((n,))]


HBM = pl.BlockSpec(memory_space=pltpu.HBM)


def _sink_rows(sinks):
    return jnp.repeat(sinks.reshape(NKV, GROUP), BLK, axis=1)


def _band_softmax(s2_ref, ls, prev_offset, sink_row):
    jj = lax.broadcasted_iota(jnp.int32, (BLK, BLK), 0)
    ii = lax.broadcasted_iota(jnp.int32, (BLK, BLK), 1)
    from_prev = jj > ii
    sc = jnp.where(from_prev, s2_ref[0:BLK, ls] + prev_offset, s2_ref[BLK:2 * BLK, ls])
    m = jnp.maximum(jnp.max(sc, axis=0, keepdims=True), sink_row)
    p = jnp.exp(sc - m)
    es = jnp.exp(sink_row - m)
    inv = 1.0 / (jnp.sum(p, axis=0, keepdims=True) + es)
    return from_prev, p * inv, es * inv


def _put_split(dst_ref, ls, t, from_prev):
    t = t.astype(bf16)
    zero = jnp.zeros_like(t)
    dst_ref[0:BLK, ls] = jnp.where(from_prev, t, zero)
    dst_ref[BLK:2 * BLK, ls] = jnp.where(from_prev, zero, t)


def _heads_side_by_side(ref, h):
    return jnp.concatenate([ref[HD * (GROUP * h + g):HD * (GROUP * h + g) + HD, :] for g in range(GROUP)], axis=1)


def _kv_specs_t():
    prev = pl.BlockSpec((KVW, BLK), lambda n: (0, jnp.maximum(n - 1, 0)))
    cur = pl.BlockSpec((KVW, BLK), lambda n: (0, n))
    return [prev, cur, prev, cur]


def _attn_fwd_t(qt, kt, vt, sinks):
    s = qt.shape[1]

    def body(sink_ref, q_ref, kp_ref, kc_ref, vp_ref, vc_ref, o_ref, s2_scr, pn2_scr):
        n = pl.program_id(0)
        off = jnp.where(n > 0, 0.0, NEG)

        def scores(h):
            hs = slice(HD * h, HD * h + HD)
            kh = jnp.concatenate([kp_ref[hs, :], kc_ref[hs, :]], axis=1)
            s2_scr[h % 2] = _dot_tn(kh, _heads_side_by_side(q_ref, h))

        def probs(h):
            for g in range(GROUP):
                ls = slice(BLK * g, BLK * g + BLK)
                from_prev, pn, _ = _band_softmax(s2_scr.at[h % 2], ls, off, sink_ref[h:h + 1, ls])
                _put_split(pn2_scr.at[h % 2], ls, pn, from_prev)

        def outputs(h):
            hs = slice(HD * h, HD * h + HD)
            vh = jnp.concatenate([vp_ref[hs, :], vc_ref[hs, :]], axis=1)
            og = _dot(vh, pn2_scr[h % 2])
            for g in range(GROUP):
                a = GROUP * h + g
                o_ref[HD * a:HD * a + HD, :] = og[:, BLK * g:BLK * g + BLK]

        scores(0)
        for h in range(NKV):
            if h + 1 < NKV:
                scores(h + 1)
            probs(h)
            outputs(h)

    return pl.pallas_call(
        body, name="attn_fwd", grid=(s // BLK,),
        in_specs=[pl.BlockSpec((NKV, GROUP * BLK), lambda n: (0, 0)), pl.BlockSpec((D, BLK), lambda n: (0, n))]
        + _kv_specs_t(),
        out_specs=pl.BlockSpec((D, BLK), lambda n: (0, n)),
        out_shape=pltpu.HBM((D, s), f32),
        scratch_shapes=[pltpu.VMEM((2, 2 * BLK, GROUP * BLK), f32), pltpu.VMEM((2, 2 * BLK, GROUP * BLK), bf16)],
        compiler_params=_params(("arbitrary",), 32),
    )(_sink_rows(sinks), *_in_hbm(qt, kt, kt, vt, vt))


def _attn_bwd_t(qt, kt, vt, dot, sinks, dwo):
    s = qt.shape[1]
    nb = s // BLK

    def body(sink_ref, q_ref, do_ref, kp_ref, kc_ref, vp_ref, vc_ref, dwo_ref, dq_ref, dk_ref, dv_ref, ds_ref,
             land_ref, dk_hold, dv_hold, s2_scr, dp2_scr, pn2_scr, ds2_scr, send_sems, recv_sems, local_sems):
        n = pl.program_id(0)
        start, finish = _scatter_ops([dwo_ref], [land_ref], send_sems, recv_sems, local_sems)

        @pl.when(n == 0)
        def _():
            start()
            dk_hold[...] = jnp.zeros_like(dk_hold)
            dv_hold[...] = jnp.zeros_like(dv_hold)
            ds_ref[...] = jnp.zeros_like(ds_ref)

        @pl.when(n < nb)
        def _():
            off = jnp.where(n > 0, 0.0, NEG)

            def scores(h):
                hs = slice(HD * h, HD * h + HD)
                kh = jnp.concatenate([kp_ref[hs, :], kc_ref[hs, :]], axis=1)
                vh = jnp.concatenate([vp_ref[hs, :], vc_ref[hs, :]], axis=1)
                s2_scr[h % 2] = _dot_tn(kh, _heads_side_by_side(q_ref, h))
                dp2_scr[h % 2] = _dot_tn(vh, _heads_side_by_side(do_ref, h))

            def softmax_bwd(h):
                for g in range(GROUP):
                    ls = slice(BLK * g, BLK * g + BLK)
                    from_prev, pn, ps = _band_softmax(s2_scr.at[h % 2], ls, off, sink_ref[h:h + 1, ls])
                    dp = jnp.where(from_prev, dp2_scr[h % 2, 0:BLK, ls], dp2_scr[h % 2, BLK:2 * BLK, ls])
                    dsum = jnp.sum(pn * dp, axis=0, keepdims=True)
                    ds_ref[h:h + 1, ls] += -ps * dsum
                    _put_split(pn2_scr.at[h % 2], ls, pn, from_prev)
                    _put_split(ds2_scr.at[h % 2], ls, pn * (dp - dsum), from_prev)

            def grads(h):
                hs = slice(HD * h, HD * h + HD)
                kh = jnp.concatenate([kp_ref[hs, :], kc_ref[hs, :]], axis=1)
                dqg = _dot(kh, ds2_scr[h % 2])
                for g in range(GROUP):
                    a = GROUP * h + g
                    dq_ref[HD * a:HD * a + HD, :] = dqg[:, BLK * g:BLK * g + BLK]
                dkh = _dot_nt(_heads_side_by_side(q_ref, h), ds2_scr[h % 2])
                dvh = _dot_nt(_heads_side_by_side(do_ref, h), pn2_scr[h % 2])
                dk_ref[hs, :] = dk_hold[hs, :] + dkh[:, 0:BLK]
                dv_ref[hs, :] = dv_hold[hs, :] + dvh[:, 0:BLK]
                dk_hold[hs, :] = dkh[:, BLK:2 * BLK]
                dv_hold[hs, :] = dvh[:, BLK:2 * BLK]

            scores(0)
            for h in range(NKV):
                if h + 1 < NKV:
                    scores(h + 1)
                softmax_bwd(h)
                grads(h)

        @pl.when(n == nb)
        def _():
            dk_ref[...] = dk_hold[...]
            dv_ref[...] = dv_hold[...]
            finish()

    blk = pl.BlockSpec((D, BLK), lambda n: (0, jnp.minimum(n, nb - 1)))
    late = pl.BlockSpec((KVW, BLK), lambda n: (0, jnp.maximum(n - 1, 0)))
    whole = pl.BlockSpec((NKV, GROUP * BLK), lambda n: (0, 0))
    kv = [pl.BlockSpec((KVW, BLK), lambda n: (0, jnp.clip(n - 1, 0, nb - 1))),
          pl.BlockSpec((KVW, BLK), lambda n: (0, jnp.minimum(n, nb - 1)))]
    return pl.pallas_call(
        body, name="attn_bwd", grid=(nb + 1,),
        in_specs=[whole, blk, blk] + kv + kv + [HBM],
        out_specs=[blk, late, late, whole, HBM],
        out_shape=[pltpu.HBM((D, s), f32), pltpu.HBM((KVW, s), f32), pltpu.HBM((KVW, s), f32),
                   jax.ShapeDtypeStruct((NKV, GROUP * BLK), f32), pltpu.HBM((NDEV, WO_ROWS, D), bf16)],
        scratch_shapes=[pltpu.VMEM((KVW, BLK), f32), pltpu.VMEM((KVW, BLK), f32)]
        + [pltpu.VMEM((2, 2 * BLK, GROUP * BLK), f32)] * 2 + [pltpu.VMEM((2, 2 * BLK, GROUP * BLK), bf16)] * 2
        + _comm_sems(1),
        compiler_params=_params(("arbitrary",), 48),
    )(_sink_rows(sinks), *_in_hbm(qt, dot, kt, kt, vt, vt, dwo))


def _decay_terms(r, sp):
    a = jnp.exp(r * (-LRU_C * sp))
    n = r * (2.0 * LRU_C * sp)
    y = jnp.where(n < 0.02, n * (1.0 - n * (0.5 - n * (1.0 / 6.0))), 1.0 - a * a)
    inv_mult = lax.rsqrt(jnp.maximum(y, 1e-30))
    return a, y * inv_mult, inv_mult


def _later(x, before, k):
    if k == 0:
        return x
    row = lax.broadcasted_iota(jnp.int32, before.shape, 0)
    rolled = pltpu.roll(x, k, 0)
    first = jnp.where(row < k, pltpu.roll(before, k, 0), rolled[0:8])
    return jnp.concatenate([first, rolled[8:]], axis=0)


def _earlier(x, after, k):
    if k == 0:
        return x
    n = x.shape[0]
    row = lax.broadcasted_iota(jnp.int32, after.shape, 0)
    rolled = pltpu.roll(x, n - k, 0)
    last = jnp.where(row >= 8 - k, pltpu.roll(after, 8 - k, 0), rolled[n - 8:n])
    return jnp.concatenate([rolled[0:n - 8], last], axis=0)


def _fwd_fused(h, wt, tabs, wo_shard, conv_w, conv_b, wr, wi, br, bi, lam, tm):
    s = h.shape[0]
    nt = s // tm
    nc = 512
    pieces = 8
    rows_per = tm // pieces
    later_chunks = (0, 1, 2, 3, 4, 7, 8)

    def body(h_ref, wt_ref, tab_ref, wo_ref, cw_ref, cb_ref, wr_ref, wi_ref, br_ref,
             bi_ref, lam_ref, q_ref, k_ref, v_ref, ga_ref, xl_ref, gl_ref, u_ref, hl_ref, r_ref, ig_ref,
             wo_all, wo_stage, halo, ub_scr, pr_scr, pi_scr, b_scr, a_scr, hcar,
             send_sems, recv_sems, local_sems):
        i = pl.program_id(0)
        start, pass_on, finish = _gather_ops([wo_stage], [wo_all], send_sems, recv_sems, local_sems)

        @pl.when(i == 0)
        def _():
            wo_stage[...] = wo_ref[...].astype(bf16)
            start()
            halo[...] = jnp.zeros_like(halo)
            hcar[...] = jnp.zeros_like(hcar)

        sp = _softplus(-lam_ref[...])
        br, bi = br_ref[...], bi_ref[...]
        c, sa, sb = _tables(tab_ref)
        piece_rows = lambda p: slice(rows_per * p, rows_per * p + rows_per)

        def project(ci):
            z = _dot_nt(h_ref[...], wt_ref[ci * nc:(ci + 1) * nc, :])
            if ci < 2:
                for j in range(nc // 128):
                    r = _rope(z[:, 128 * j:128 * j + 128], c, sa, sb) * (HD ** -0.5)
                    q_ref[ci * nc + 128 * j:ci * nc + 128 * j + 128, :] = r.astype(bf16).T
            elif ci == 2:
                for j in range(2):
                    js = slice(128 * j, 128 * j + 128)
                    k_ref[js, :] = _rope(z[:, js], c, sa, sb).astype(bf16).T
                    v_ref[js, :] = z[:, KVW + 128 * j:KVW + 128 * j + 128].astype(bf16).T
            else:
                sec, j = divmod(ci - 3, 2)
                (ga_ref, xl_ref, gl_ref)[sec][:, j * nc:(j + 1) * nc] = z

        def gate_terms(p):
            rows = piece_rows(p)
            r = _sigmoid(pr_scr[rows, :] + br)
            ig = _sigmoid(pi_scr[rows, :] + bi)
            a, mult, _ = _decay_terms(r, sp)
            r_ref[rows, :] = r
            ig_ref[rows, :] = ig
            a_scr[rows, :] = a
            b_scr[rows, :] = mult * (ig * u_ref[rows, :])

        def scan(p, hc):
            for t in range(rows_per * p, rows_per * p + rows_per):
                hc = a_scr[t:t + 1, :] * hc + b_scr[t:t + 1, :]
                hl_ref[t:t + 1, :] = hc
            return hc

        project(5)
        project(6)
        xl = xl_ref[...]
        u = cb_ref[...] + sum(cw_ref[k:k + 1, :] * _later(xl, halo[...], CONVW - 1 - k) for k in range(CONVW))
        halo[...] = xl[tm - 8:tm, :]
        u_ref[...] = u
        ub_scr[...] = u.astype(bf16)
        for g in range(NGRP):
            gs = slice(256 * g, 256 * g + 256)
            pr_scr[:, gs] = _dot(ub_scr[:, gs], wr_ref[g])
            pi_scr[:, gs] = _dot(ub_scr[:, gs], wi_ref[g])
        hc = hcar[...]
        gate_terms(0)
        for slot, ci in enumerate(later_chunks):
            project(ci)
            gate_terms(slot + 1)
            hc = scan(slot, hc)
        hcar[...] = scan(pieces - 1, hc)

        @pl.when(i == max(nt - 2, 0))
        def _():
            pass_on()

        @pl.when(i == nt - 1)
        def _():
            finish()

    row = lambda w: pl.BlockSpec((tm, w), lambda i: (i, 0))
    col = lambda w: pl.BlockSpec((w, tm), lambda i: (0, i))
    full = lambda a: pl.BlockSpec(a.shape, lambda i: (0,) * a.ndim)
    big = lambda w, dt: pltpu.HBM((s, w), dt)
    tile = pltpu.VMEM((tm, LW), f32)
    return pl.pallas_call(
        body, name="fwd_fused", grid=(nt,),
        in_specs=[row(D), full(wt), row(384), full(wo_shard), full(conv_w), full(conv_b),
                  full(wr), full(wi), full(br), full(bi), full(lam)],
        out_specs=[col(D), col(KVW), col(KVW), row(D), row(D), row(D)] + [row(LW)] * 4 + [HBM],
        out_shape=[pltpu.HBM((D, s), bf16), pltpu.HBM((KVW, s), bf16), pltpu.HBM((KVW, s), bf16),
                   big(D, f32), big(D, f32), big(D, f32)] + [big(LW, f32)] * 4 + [pltpu.HBM((2 * D, D), bf16)],
        scratch_shapes=[pltpu.VMEM((WO_ROWS, D), bf16), pltpu.VMEM((8, LW), f32), pltpu.VMEM((tm, LW), bf16)]
        + [tile] * 4 + [pltpu.VMEM((1, LW), f32)] + _comm_sems(1),
        compiler_params=_params(("arbitrary",), 56),
    )(*_in_hbm(h, wt), tabs, wo_shard, conv_w, conv_b, wr, wi, br, bi, lam)


def _lru_bwd(u, hl, dhl, xl, r, ig, conv_w, wr, wi, lam, tm):
    s = u.shape[0]
    nt = s // tm
    pieces = 8
    rows_per = tm // pieces

    def body(u_ref, h_ref, hp_ref, dh_ref, x_ref, r_ref, ig_ref, cw_ref, wr_ref, wi_ref,
             lam_ref, dxl_ref, dwr_ref, dwi_ref, dbr_ref, dbi_ref, dlam_ref, dcb_ref, dcw_ref,
             l_scr, du_scr, a_scr, mu_scr, im_scr, dpr_scr, dpi_scr, lcar, dunext):
        t0 = pl.program_id(0)
        tile = nt - 1 - t0

        @pl.when(t0 == 0)
        def _():
            lcar[...] = jnp.zeros_like(lcar)
            dunext[...] = jnp.zeros_like(dunext)
            for ref in (dwr_ref, dwi_ref, dbr_ref, dbi_ref, dlam_ref, dcb_ref, dcw_ref):
                ref[...] = jnp.zeros_like(ref)

        lam = lam_ref[...]
        sp = _softplus(-lam)
        hp = jnp.where(tile > 0, hp_ref[...], 0.0)

        def decay(p):
            rows = slice(rows_per * p, rows_per * p + rows_per)
            a_scr[rows, :], mu_scr[rows, :], im_scr[rows, :] = _decay_terms(r_ref[rows, :], sp)

        def scan(p, c):
            for t in range(rows_per * p + rows_per - 1, rows_per * p - 1, -1):
                lt = dh_ref[t:t + 1, :] + c
                l_scr[t:t + 1, :] = lt
                c = a_scr[t:t + 1, :] * lt
            return c

        def terms(p, sums):
            rows = slice(rows_per * p, rows_per * p + rows_per)
            lt, u, r, i, a, mult, inv_mult = l_scr[rows, :], u_ref[rows, :], r_ref[rows, :], ig_ref[rows, :], \
                a_scr[rows, :], mu_scr[rows, :], im_scr[rows, :]
            before = hp if p == 0 else h_ref[rows_per * p - 8:rows_per * p, :]
            hprev = _later(h_ref[rows, :], before, 1)
            iu = i * u
            lm = lt * mult
            du_scr[rows, :] = lm * i
            dla = (lt * hprev) * a - ((lt * iu) * (a * a)) * inv_mult
            dlar = dla * r
            dpr = (dlar * (1.0 - r)) * (-LRU_C * sp)
            dpi = (lm * iu) * (1.0 - i)
            dpr_scr[rows, :] = dpr.astype(bf16)
            dpi_scr[rows, :] = dpi.astype(bf16)
            col = lambda t: jnp.sum(t, axis=0, keepdims=True)
            return sums[0] + col(dlar), sums[1] + col(dpr), sums[2] + col(dpi)

        sums = (jnp.zeros((1, LW), f32),) * 3
        decay(pieces - 1)
        c = scan(pieces - 1, lcar[...])
        for p in range(pieces - 1, -1, -1):
            if p > 0:
                decay(p - 1)
                c = scan(p - 1, c)
            sums = terms(p, sums)
        lcar[...] = c
        dlam_ref[...] += sums[0] * (-LRU_C)
        dbr_ref[...] += sums[1]
        dbi_ref[...] += sums[2]

        ub = u_ref[...].astype(bf16)
        dug = []
        for g in range(NGRP):
            gs = slice(256 * g, 256 * g + 256)
            dwr_ref[g] += _dot_tn(ub[:, gs], dpr_scr[:, gs])
            dwi_ref[g] += _dot_tn(ub[:, gs], dpi_scr[:, gs])
            dug.append(_dot_nt(dpr_scr[:, gs], wr_ref[g]) + _dot_nt(dpi_scr[:, gs], wi_ref[g]))
        du = du_scr[...] + jnp.concatenate(dug, axis=1)

        dcb_ref[...] += jnp.sum(du, axis=0, keepdims=True)
        x = x_ref[...]
        after = dunext[...]
        dxl = jnp.zeros_like(du)
        for k in range(CONVW):
            e = _earlier(du, after, CONVW - 1 - k)
            dxl = dxl + cw_ref[k:k + 1, :] * e
            dcw_ref[k:k + 1, :] += jnp.sum(e * x, axis=0, keepdims=True)
        dxl_ref[...] = dxl.astype(bf16)
        dunext[...] = du[0:8, :]

        @pl.when(t0 == nt - 1)
        def _():
            dlam_ref[...] = dlam_ref[...] * (-_sigmoid(-lam))

    rev = lambda i: (nt - 1 - i, 0)
    row = pl.BlockSpec((tm, LW), rev)
    prev8 = pl.BlockSpec((8, LW), lambda i: (jnp.maximum((nt - 1 - i) * (tm // 8) - 1, 0), 0))
    full = lambda a: pl.BlockSpec(a.shape, lambda i: (0,) * a.ndim)
    vec = pl.BlockSpec((1, LW), lambda i: (0, 0))
    bd = pl.BlockSpec((NGRP, 256, 256), lambda i: (0, 0, 0))
    return pl.pallas_call(
        body, name="lru_bwd", grid=(nt,),
        in_specs=[row, row, prev8] + [row] * 4 + [full(conv_w), full(wr), full(wi), full(lam)],
        out_specs=[row, bd, bd, vec, vec, vec, vec, pl.BlockSpec((CONVW, LW), lambda i: (0, 0))],
        out_shape=[pltpu.HBM((s, LW), bf16),
                   jax.ShapeDtypeStruct((NGRP, 256, 256), f32), jax.ShapeDtypeStruct((NGRP, 256, 256), f32),
                   jax.ShapeDtypeStruct((1, LW), f32), jax.ShapeDtypeStruct((1, LW), f32),
                   jax.ShapeDtypeStruct((1, LW), f32), jax.ShapeDtypeStruct((1, LW), f32),
                   jax.ShapeDtypeStruct((CONVW, LW), f32)],
        scratch_shapes=[pltpu.VMEM((tm, LW), f32)] * 5 + [pltpu.VMEM((tm, LW), bf16)] * 2
        + [pltpu.VMEM((1, LW), f32), pltpu.VMEM((8, LW), f32)],
        compiler_params=_params(("arbitrary",), 56),
    )(*_in_hbm(u, hl, hl, dhl, xl, r, ig), conv_w, wr, wi, lam)


def _gated_norm(t, gate, gain):
    sg = _sigmoid(gate)
    silu = gate * sg
    p = t * silu
    rstd = lax.rsqrt(jnp.mean(p * p, axis=-1, keepdims=True) + EPS)
    ph = p * rstd
    return sg, silu, rstd, ph, ph * gain


def _gated_norm_bwd(dy, t, gate, gain, sg, silu, rstd, ph):
    w = dy * gain
    dp = rstd * (w - ph * jnp.mean(w * ph, axis=-1, keepdims=True))
    dgate = (dp * t) * (sg + silu * (1.0 - sg))
    return jnp.sum(dy * ph, axis=0, keepdims=True), dp * silu, dgate


def _out_fwd_bwd(x, tgt, o, ga, hl, gl, again, lgain, fgain, wo, tm):
    s = x.shape[0]
    nt = s // tm

    def body(x_ref, t_ref, o_ref, ga_ref, hl_ref, gl_ref, ag_ref, lg_ref, fg_ref, wo_ref,
             dx2_ref, do_ref, dga_ref, dhl_ref, dgl_ref, dwo_ref, gfg_ref, gag_ref, glg_ref, loss_ref, acc,
             ya_scr, yl_scr, dy_scr):
        i = pl.program_id(0)

        @pl.when(i == 0)
        def _():
            acc[...] = jnp.zeros_like(acc)
            for ref in (gfg_ref, gag_ref, glg_ref, loss_ref):
                ref[...] = jnp.zeros_like(ref)

        ag, lg, fg = ag_ref[...], lg_ref[...], fg_ref[...]

        def half(rs):
            oo = jnp.concatenate([o_ref[128 * j:128 * j + 128, rs].T for j in range(D // 128)], axis=1)
            gga, hh, ggl = ga_ref[rs, :], hl_ref[rs, :], gl_ref[rs, :]
            sga, silua, ra, pah, ya = _gated_norm(oo, gga, ag)
            sgl, silul, rl, plh, yl = _gated_norm(hh, ggl, lg)
            yab, ylb = ya.astype(bf16), yl.astype(bf16)
            ya_scr[rs, :] = yab
            yl_scr[rs, :] = ylb
            y = _dot(yab, wo_ref[0:D, :]) + _dot(ylb, wo_ref[D:2 * D, :])
            x2 = x_ref[rs, :] + y
            r2 = lax.rsqrt(jnp.mean(x2 * x2, axis=-1, keepdims=True) + EPS)
            x2h = x2 * r2
            err = x2h * fg - t_ref[rs, :]
            loss_ref[...] += 0.5 * jnp.sum(jnp.sum(err * err, axis=-1, keepdims=True) * (1.0 / D))
            gfg_ref[...] += jnp.sum(err * x2h, axis=0, keepdims=True) * (1.0 / D)
            w = err * (fg * (1.0 / D))
            dx2 = r2 * (w - x2h * jnp.mean(w * x2h, axis=-1, keepdims=True))
            dx2_ref[rs, :] = dx2
            dyb = dx2.astype(bf16)
            dy_scr[rs, :] = dyb
            dya = _dot_nt(dyb, wo_ref[0:D, :])
            dyl = _dot_nt(dyb, wo_ref[D:2 * D, :])
            gag, do, dga = _gated_norm_bwd(dya, oo, gga, ag, sga, silua, ra, pah)
            glg, dhl, dgl = _gated_norm_bwd(dyl, hh, ggl, lg, sgl, silul, rl, plh)
            gag_ref[...] += gag
            glg_ref[...] += glg
            dob = do.astype(bf16)
            for j in range(D // 128):
                do_ref[128 * j:128 * j + 128, rs] = dob[:, 128 * j:128 * j + 128].T
            dga_ref[rs, :] = dga.astype(bf16)
            dhl_ref[rs, :] = dhl
            dgl_ref[rs, :] = dgl.astype(bf16)

        for hf in range(2):
            half(slice(hf * (tm // 2), (hf + 1) * (tm // 2)))
        acc[0:D, :] += _dot_tn(ya_scr[...], dy_scr[...])
        acc[D:2 * D, :] += _dot_tn(yl_scr[...], dy_scr[...])

        @pl.when(i == nt - 1)
        def _():
            dwo_ref[...] = acc[...].astype(bf16)

    row = pl.BlockSpec((tm, D), lambda i: (i, 0))
    col = pl.BlockSpec((D, tm), lambda i: (0, i))
    vec = pl.BlockSpec((1, D), lambda i: (0, 0))
    mat = pl.BlockSpec((2 * D, D), lambda i: (0, 0))
    return pl.pallas_call(
        body, name="out_fwd_bwd", grid=(nt,),
        in_specs=[row, row, col, row, row, row] + [vec] * 3 + [mat],
        out_specs=[row, col, row, row, row] + [mat, vec, vec, vec, pl.BlockSpec((1, 128), lambda i: (0, 0))],
        out_shape=[pltpu.HBM((s, D), f32), pltpu.HBM((D, s), bf16),
                   pltpu.HBM((s, D), bf16), pltpu.HBM((s, D), f32),
                   pltpu.HBM((s, D), bf16), pltpu.HBM((2 * D, D), bf16),
                   jax.ShapeDtypeStruct((1, D), f32), jax.ShapeDtypeStruct((1, D), f32),
                   jax.ShapeDtypeStruct((1, D), f32), jax.ShapeDtypeStruct((1, 128), f32)],
        scratch_shapes=[pltpu.VMEM((2 * D, D), f32)] + [pltpu.VMEM((tm, D), bf16)] * 3,
        compiler_params=_params(("arbitrary",), 56),
    )(*_in_hbm(x, tgt, o, ga, hl, gl), again, lgain, fgain, *_in_hbm(wo))


def _bwd_in(x, dx2, dq, dk, dv, dga, dxl, dgl, ln_gain, wt, tabs, tm):
    s = x.shape[0]

    def body(x_ref, dx2_ref, dq_ref, dk_ref, dv_ref, dga_ref, dxl_ref, dgl_ref, g_ref, wt_ref,
             tab_ref, gx_ref, gln_ref, dzt_ref):
        @pl.when(pl.program_id(0) == 0)
        def _():
            gln_ref[...] = jnp.zeros_like(gln_ref)

        c, sa, sb = (t.T for t in _tables(tab_ref))
        for j in range(D // 128):
            js = slice(128 * j, 128 * j + 128)
            dzt_ref[js, :] = (_unrope_t(dq_ref[js, :], c, sa, sb) * (HD ** -0.5)).astype(bf16)
        for j in range(KVW // 128):
            js = slice(128 * j, 128 * j + 128)
            dzt_ref[D + 128 * j:D + 128 * j + 128, :] = _unrope_t(dk_ref[js, :], c, sa, sb).astype(bf16)
        dzt_ref[D + KVW:D + 2 * KVW, :] = dv_ref[...].astype(bf16)
        first = D + 2 * KVW
        dh = _dot_tn(dzt_ref[0:first, :], wt_ref[0:first, :])
        for sec, ref in enumerate((dga_ref, dxl_ref, dgl_ref)):
            dh = dh + _dot(ref[...], wt_ref[first + D * sec:first + D * sec + D, :])
            for j in range(D // 128):
                dzt_ref[first + D * sec + 128 * j:first + D * sec + 128 * j + 128, :] = ref[:, 128 * j:128 * j + 128].T
        xx = x_ref[...]
        rstd = lax.rsqrt(jnp.mean(xx * xx, axis=-1, keepdims=True) + EPS)
        xh = xx * rstd
        gln_ref[...] += jnp.sum(dh * xh, axis=0, keepdims=True)
        w = dh * g_ref[...]
        gx_ref[...] = dx2_ref[...] + rstd * (w - xh * jnp.mean(w * xh, axis=-1, keepdims=True))

    row = lambda w: pl.BlockSpec((tm, w), lambda i: (i, 0))
    col = lambda w: pl.BlockSpec((w, tm), lambda i: (0, i))
    full = lambda a: pl.BlockSpec(a.shape, lambda i: (0, 0))
    return pl.pallas_call(
        body, name="bwd_in", grid=(s // tm,),
        in_specs=[row(D), row(D), col(D), col(KVW), col(KVW), row(D), row(D), row(D), full(ln_gain), full(wt),
                  row(384)],
        out_specs=[row(D), pl.BlockSpec((1, D), lambda i: (0, 0)), col(NIN)],
        out_shape=[pltpu.HBM((s, D), f32), jax.ShapeDtypeStruct((1, D), f32),
                   pltpu.HBM((NIN, s), bf16)],
        compiler_params=_params(("arbitrary",), 56),
    )(*_in_hbm(x, dx2, dq, dk, dv, dga, dxl, dgl), ln_gain, *_in_hbm(wt), tabs)


WT_TERMS = 4


def _dwt_scatter(dzt, h, small, tm):
    s = h.shape[0]
    nk = s // tm
    srows = small.shape[0] // NDEV
    last = NDEV - 1
    sm_turn = 2

    def body(order_ref, dz_ref, h_ref, sm_ref, lwt_ref, rep_all, tail_ref, acc, stage, given, relayed, lsm, rep_stage,
             send_sems, recv_sems, local_sem, sm_send, sm_recv, sm_local, rep_send, rep_recv, rep_local):
        j, k = pl.program_id(0), pl.program_id(1)
        x, y, c = _place()
        sibling = (x, y, 1 - c)
        near = (x ^ (1 - c), y ^ c)
        far = (x ^ c, y ^ (1 - c))
        sm_start, sm_finish = _scatter_ops([sm_ref], [lsm], sm_send, sm_recv, sm_local)
        rep_start, rep_pass_on, rep_finish = _gather_ops([rep_stage], [rep_all], rep_send, rep_recv, rep_local)

        def send(step):
            if step == last - 1:
                dst, to = lwt_ref.at[1], sibling
            elif step % 2 == 0:
                dst, to = given.at[step // 2], sibling
            elif step == 1:
                dst, to = relayed, (*near, c)
            else:
                dst, to = lwt_ref.at[1 + step // 2], (*(near if step == 3 else far), c)
            return pltpu.make_async_remote_copy(
                src_ref=stage.at[step % 2], dst_ref=dst, send_sem=send_sems.at[step], recv_sem=recv_sems.at[step],
                device_id=to, device_id_type=MESH)

        def keep():
            return pltpu.make_async_copy(stage.at[last % 2], lwt_ref.at[0], local_sem)

        @pl.when((j == 0) & (k == 0))
        def _():
            sm_start()

        @pl.when(k == 0)
        def _():
            acc[...] = jnp.zeros_like(acc)

        acc[...] += _dot(dz_ref[...], h_ref[...])

        for step in range(NDEV):
            @pl.when((k == nk - 1) & (j == step))
            def _(step=step):
                if step >= 2:
                    send(step - 2).wait_send()
                if step % 2 == 1 and step < last:
                    send(step - 1).wait_recv()
                    total = acc[...] + given[step // 2].astype(f32)
                    if step == 5:
                        send(1).wait_recv()
                        total = total + relayed[...].astype(f32)
                    stage[step % 2] = total.astype(bf16)
                else:
                    stage[step % 2] = acc[...].astype(bf16)
                if step < last:
                    send(step).start()
                else:
                    keep().start()
                    send(last - 1).wait_send()
                    for peer_step in (3, 5, last - 1):
                        send(peer_step).wait_recv()
                    keep().wait()
                    rep_finish()
                if step == sm_turn:
                    sm_finish()
                    total_sm = lsm[0]
                    for dev in range(1, NDEV):
                        total_sm = total_sm + lsm[dev]
                    rep_stage[...] = total_sm[0:SMALL_PER]
                    tail_ref[...] = total_sm[SMALL_PER:]
                    rep_start()
                if step == last - 1:
                    rep_pass_on()

    x, y, c = _place()
    dest = lambda chip, cc: 4 * chip[0] + 2 * chip[1] + cc
    near, far, diag = (x ^ (1 - c), y ^ c), (x ^ c, y ^ (1 - c)), (1 - x, 1 - y)
    order = jnp.stack([dest(diag, 1 - c), dest(diag, c), dest(far, 1 - c), dest(near, c),
                       dest(near, 1 - c), dest(far, c), dest((x, y), 1 - c), dest((x, y), c)])
    return pl.pallas_call(
        body, name="dwt_scatter",
        grid_spec=pltpu.PrefetchScalarGridSpec(
            num_scalar_prefetch=1, grid=(NDEV, nk),
            in_specs=[pl.BlockSpec((WT_ROWS, tm), lambda j, k, order: (order[j], k)),
                      pl.BlockSpec((tm, D), lambda j, k, order: (k, 0)), HBM],
            out_specs=[HBM, HBM, pl.BlockSpec((srows - SMALL_PER, D), lambda j, k, order: (0, 0))],
            scratch_shapes=[pltpu.VMEM((WT_ROWS, D), f32), pltpu.VMEM((2, WT_ROWS, D), bf16),
                            pltpu.VMEM((3, WT_ROWS, D), bf16), pltpu.VMEM((WT_ROWS, D), bf16),
                            pltpu.VMEM((NDEV, srows, D), f32), pltpu.VMEM((SMALL_PER, D), f32),
                            pltpu.SemaphoreType.DMA((last,)), pltpu.SemaphoreType.DMA((last,)),
                            pltpu.SemaphoreType.DMA(())] + _comm_sems(1) + _comm_sems(1)),
        out_shape=[pltpu.HBM((WT_TERMS, WT_ROWS, D), bf16), pltpu.HBM((SMALL_ROWS, D), f32),
                   jax.ShapeDtypeStruct((srows - SMALL_PER, D), f32)],
        compiler_params=_params(("arbitrary", "arbitrary"), 48),
    )(order, *_in_hbm(dzt, h, small))


def _diag_blocks(bd):
    eye = jnp.eye(4, dtype=bd.dtype)
    return jnp.einsum('gjckd,jk->gjcd', bd.reshape(NGRP, 4, HD, 4, HD), eye).reshape(NQ, HD, HD)


def _sequence_step(x, h, tgt, wt, wo_shard, conv_w, wr, wi, p):
    s = x.shape[0]
    tm = min(256, s)
    tabs = _rope_tables(s)
    sinks = p["sinks"].reshape(NQ)
    qt, kt, vt, ga, xl, gl, u, hl, r, ig, wo = _fwd_fused(
        h, wt, tabs, wo_shard, conv_w, p["conv_b"], wr, wi, p["b_rgate"], p["b_igate"], p["lru_lambda"], tm)
    ot = _attn_fwd_t(qt, kt, vt, sinks)
    dx2, dot, dga, dhl, dgl, dwo, g_fg, g_ag, g_lg, loss = _out_fwd_bwd(
        x, tgt, ot, ga, hl, gl, p["attn_out_gain"], p["lru_out_gain"], p["final_gain"], wo, tm)
    dqt, dkt, dvt, dsink, land_wo = _attn_bwd_t(qt, kt, vt, dot, sinks, dwo)
    dxl, dwr, dwi, dbr, dbi, dlam, dcb, dcw = _lru_bwd(u, hl, dhl, xl, r, ig, conv_w, wr, wi, p["lru_lambda"], tm)
    gx, g_ln, dzt = _bwd_in(x, dx2, dqt, dkt, dvt, dga, dxl, dgl, p["ln_gain"], wt, tabs, tm)
    small = dict(ln_gain=g_ln, sinks=dsink.reshape(NQ, BLK).sum(axis=1)[None], conv_w=dcw, conv_b=dcb,
                 w_rgate=_diag_blocks(dwr), b_rgate=dbr, w_igate=_diag_blocks(dwi), b_igate=dbi, lru_lambda=dlam,
                 attn_out_gain=g_ag, lru_out_gain=g_lg, final_gain=g_fg)
    land_wt, g_rep, g_tail = _dwt_scatter(dzt, h, _pack_small(small, loss), min(2048, s))
    return gx, land_wt, land_wo, g_rep, g_tail


def _gather_weights(wt_shard, conv_blk, x, ln_gain, w_rgate, w_igate, tm):
    s = x.shape[0]

    def body(wt_ref, cw_ref, g_ref, wrg_ref, wig_ref, x_ref, wt_all, cw_all, h_ref, wr_ref, wi_ref,
             stage, xbuf, hbuf, send_sems, recv_sems, local_sems):
        stage[...] = wt_ref[...].astype(bf16)
        start, finish = _relay_gather_ops([stage, cw_ref], [wt_all, cw_all], send_sems, recv_sems, local_sems)
        start()
        for src, dst in ((wrg_ref, wr_ref), (wig_ref, wi_ref)):
            dst[...] = jnp.zeros_like(dst)
            for nb in range(NQ):
                g, j = divmod(nb, 4)
                dst[g, HD * j:HD * j + HD, HD * j:HD * j + HD] = src[nb].astype(bf16)
        gain = g_ref[...]
        for i in range(s // tm):
            rows = pl.ds(i * tm, tm)
            pltpu.sync_copy(x_ref.at[rows, :], xbuf)
            xx = xbuf[...]
            rstd = lax.rsqrt(jnp.mean(xx * xx, axis=-1, keepdims=True) + EPS)
            hbuf[...] = (xx * rstd * gain).astype(bf16)
            pltpu.sync_copy(hbuf, h_ref.at[rows, :])
        finish()

    vmem = pl.BlockSpec(memory_space=pltpu.VMEM)
    return pl.pallas_call(
        body, name="gather_weights",
        in_specs=[vmem] * 5 + [HBM], out_specs=[HBM, HBM, HBM, vmem, vmem],
        out_shape=[pltpu.HBM((NIN, D), bf16), pltpu.HBM((NDEV * 8, 128), f32), pltpu.HBM((s, D), bf16)]
        + [jax.ShapeDtypeStruct((NGRP, 256, 256), bf16)] * 2,
        scratch_shapes=[pltpu.VMEM((WT_ROWS, D), bf16), pltpu.VMEM((tm, D), f32), pltpu.VMEM((tm, D), bf16)]
        + _comm_sems(2),
        compiler_params=pltpu.CompilerParams(vmem_limit_bytes=32 * MIB),
    )(wt_shard, conv_blk, ln_gain, w_rgate, w_igate, *_in_hbm(x))


def _adam_math(w, g, m, v):
    m2 = ADAM_B1 * m + (1.0 - ADAM_B1) * g
    v2 = ADAM_B2 * v + (1.0 - ADAM_B2) * (g * g)
    m_hat = m2 / (1.0 - ADAM_B1 ** ADAM_STEP)
    v_hat = v2 / (1.0 - ADAM_B2 ** ADAM_STEP)
    delta = -ADAM_LR * (m_hat / (jnp.sqrt(v_hat) + ADAM_EPS) + ADAM_WD * w)
    return delta, m2, v2


def _reduce_adamw(land, w, m, v, tr, name):
    terms, rows, cols = land.shape

    def body(l_ref, w_ref, m_ref, v_ref, g_ref, d_ref, m2_ref, v2_ref):
        g = l_ref[0].astype(f32)
        for t in range(1, terms):
            g = g + l_ref[t].astype(f32)
        g_ref[...] = g
        d_ref[...], m2_ref[...], v2_ref[...] = _adam_math(w_ref[...], g, m_ref[...], v_ref[...])

    blk = pl.BlockSpec((tr, cols), lambda i: (i, 0))
    return pl.pallas_call(
        body, name=name, grid=(rows // tr,),
        in_specs=[pl.BlockSpec((terms, tr, cols), lambda i: (0, i, 0))] + [blk] * 3, out_specs=[blk] * 4,
        out_shape=[jax.ShapeDtypeStruct((rows, cols), f32)] * 4,
        compiler_params=_params(("arbitrary",), 32),
    )(*_in_hbm(land), w, m, v)


VEC_NAMES = ("ln_gain", "conv_b", "b_rgate", "b_igate", "lru_lambda", "attn_out_gain", "lru_out_gain", "final_gain")
ROW_RGATE, ROW_IGATE, ROW_VEC, ROW_SINKS = 0, 64, 128, 136
LOSS_LANE = NQ


def _adamw_small(g_rep, g_conv, w, m, v):
    names = list(VEC_NAMES) + ["sinks", "conv_w", "w_rgate", "w_igate"]
    ins = [g_rep, g_conv] + [d[k] for k in names for d in (w, m, v)]

    def body(*refs):
        g_ref, gc_ref = refs[0], refs[1]
        in_refs = refs[2:2 + 3 * len(names)]
        out_refs = refs[2 + 3 * len(names):]

        def update(j, g, at=None):
            w_ref, m_ref, v_ref = in_refs[3 * j:3 * j + 3]
            outs = out_refs[4 * j:4 * j + 4]
            pick = (lambda r: r[...]) if at is None else (lambda r: r[at])
            res = (g,) + _adam_math(pick(w_ref), g, pick(m_ref), pick(v_ref))
            for o_ref, val in zip(outs, res):
                if at is None:
                    o_ref[...] = val
                else:
                    o_ref[at] = val

        for j in range(len(VEC_NAMES)):
            update(j, g_ref[ROW_VEC + j:ROW_VEC + j + 1, :])
        update(len(VEC_NAMES), g_ref[ROW_SINKS:ROW_SINKS + 1, 0:NQ])
        update(len(VEC_NAMES) + 1, gc_ref[...], at=0)
        for gi, row0 in ((len(VEC_NAMES) + 2, ROW_RGATE), (len(VEC_NAMES) + 3, ROW_IGATE)):
            for nb in range(NQ):
                update(gi, g_ref[row0:row0 + HD, HD * nb:HD * nb + HD], at=(0, nb))

    vmem = pl.BlockSpec(memory_space=pltpu.VMEM)
    out_shape = [jax.ShapeDtypeStruct(w[k].shape, f32) for k in names for _ in range(4)]
    outs = pl.pallas_call(
        body, name="adamw_small",
        in_specs=[vmem] * len(ins), out_specs=[vmem] * len(out_shape), out_shape=out_shape,
        compiler_params=pltpu.CompilerParams(vmem_limit_bytes=32 * MIB),
    )(*ins)
    return {k: tuple(outs[4 * j:4 * j + 4]) for j, k in enumerate(names)}


def _pack_small(small, loss):
    gate = lambda g: g.transpose(1, 0, 2).reshape(HD, NQ * HD)
    row_s = jnp.concatenate([small["sinks"], loss[:, LOSS_LANE:128], jnp.zeros((1, D - 128), f32)], axis=1)
    rep = jnp.concatenate([gate(small["w_rgate"]), gate(small["w_igate"])] + [small[k] for k in VEC_NAMES]
                          + [row_s, jnp.zeros((SMALL_ROWS - ROW_SINKS - 1, D), f32)], axis=0)
    conv = small["conv_w"].reshape(CONVW, NDEV, 128).transpose(1, 0, 2)
    conv = jnp.pad(conv, ((0, 0), (0, 8 - CONVW), (0, D - 128)))
    return jnp.concatenate([rep.reshape(NDEV, SMALL_PER, D), conv], axis=1).reshape(NDEV * (SMALL_PER + 8), D)


def kernel(x, ln_gain, w_in, sinks, conv_w, conv_b, w_rgate, b_rgate, w_igate, b_igate, lru_lambda, attn_out_gain, lru_out_gain, w_out, final_gain, loss_target, m_ln_gain, m_w_in, m_sinks, m_conv_w, m_conv_b, m_w_rgate, m_b_rgate, m_w_igate, m_b_igate, m_lru_lambda, m_attn_out_gain, m_lru_out_gain, m_w_out, m_final_gain, v_ln_gain, v_w_in, v_sinks, v_conv_w, v_conv_b, v_w_rgate, v_b_rgate, v_w_igate, v_b_igate, v_lru_lambda, v_attn_out_gain, v_lru_out_gain, v_w_out, v_final_gain):
    w = dict(ln_gain=ln_gain, sinks=sinks, conv_w=conv_w, conv_b=conv_b, w_rgate=w_rgate, b_rgate=b_rgate,
             w_igate=w_igate, b_igate=b_igate, lru_lambda=lru_lambda, attn_out_gain=attn_out_gain,
             lru_out_gain=lru_out_gain, final_gain=final_gain.reshape(1, D))
    m = dict(ln_gain=m_ln_gain, sinks=m_sinks, conv_w=m_conv_w, conv_b=m_conv_b, w_rgate=m_w_rgate,
             b_rgate=m_b_rgate, w_igate=m_w_igate, b_igate=m_b_igate, lru_lambda=m_lru_lambda,
             attn_out_gain=m_attn_out_gain, lru_out_gain=m_lru_out_gain, final_gain=m_final_gain.reshape(1, D))
    v = dict(ln_gain=v_ln_gain, sinks=v_sinks, conv_w=v_conv_w, conv_b=v_conv_b, w_rgate=v_w_rgate,
             b_rgate=v_b_rgate, w_igate=v_w_igate, b_igate=v_b_igate, lru_lambda=v_lru_lambda,
             attn_out_gain=v_attn_out_gain, lru_out_gain=v_lru_out_gain, final_gain=v_final_gain.reshape(1, D))

    conv_blk = jnp.pad(conv_w[0], ((0, 8 - CONVW), (0, 0)))
    wt, cw_all, h, wr, wi = _gather_weights(w_in[0].T, conv_blk, x[0], ln_gain, w_rgate[0], w_igate[0],
                                            min(512, x.shape[1]))
    conv_full = cw_all.reshape(NDEV, 8, 128)[:, 0:CONVW].transpose(1, 0, 2).reshape(CONVW, LW)

    p = {k: w[k] for k in w if k not in ("conv_w", "w_rgate", "w_igate")}
    gx, land_wt, land_wo, g_rep, g_tail = _sequence_step(
        x[0], h, loss_target[0], wt, w_out[0], conv_full, wr, wi, p)
    g_conv = g_tail[0:CONVW, 0:128]

    wins = _reduce_adamw(land_wt, w_in[0].T, m_w_in[0].T, v_w_in[0].T, 192, "adamw_w_in")
    g_win, d_win, m_win, v_win = (t.T for t in wins)
    g_wo, d_wo, m_wo, v_wo = _reduce_adamw(land_wo, w_out[0], m_w_out[0], v_w_out[0], 256, "adamw_w_out")
    res = _adamw_small(g_rep, g_conv, w, m, v)
    res["w_in"] = tuple(t[None] for t in (g_win, d_win, m_win, v_win))
    res["w_out"] = tuple(t[None] for t in (g_wo, d_wo, m_wo, v_wo))
    res["final_gain"] = tuple(t.reshape(D) for t in res["final_gain"])

    order = ("ln_gain", "w_in", "sinks", "conv_w", "conv_b", "w_rgate", "b_rgate", "w_igate", "b_igate",
             "lru_lambda", "attn_out_gain", "lru_out_gain", "w_out", "final_gain")
    total_loss = g_rep[ROW_SINKS, LOSS_LANE]
    return (total_loss, gx[None]) + tuple(res[k][i] for i in range(4) for k in order)
```
